```python
import math
import jax, jax.numpy as jnp
from jax import lax
import numpy as np

D_MODEL = 1024
BATCH = 8
SEQ = 2048
DEPTH = 1

N_META = 16
D_MIX = D_MODEL
D_CONV = D_MIX // 2
CONV_HEAD_DIM = 64
N_CONV_HEADS = D_CONV // CONV_HEAD_DIM
D_SSM = D_MIX - D_CONV
SSM_GROUP = 16
N_SSM_GROUPS = D_SSM // SSM_GROUP
SSM_STATE = 64
CONV_WIDTH = 3
D_FF = 2816
D_IN_PROJ = 3 * D_CONV + D_SSM
RMS_EPS = 1e-6
DT_MIN = 1e-3
DT_MAX = 1e-1

kernel_name = "hymba_conv_s5_hybrid_layer"


def rms_norm(x, g):
    xf = x.astype(jnp.float32)
    y = xf * lax.rsqrt(jnp.mean(xf * xf, axis=-1, keepdims=True) + RMS_EPS)
    return (y * g.astype(jnp.float32)).astype(x.dtype)


def causal_dwconv(x, w, b=None):
    c = x.shape[-1]
    y = lax.conv_general_dilated(
        x, w[:, None, :].astype(x.dtype), window_strides=(1,),
        padding=[(CONV_WIDTH - 1, 0)], dimension_numbers=("NWC", "WIO", "NWC"),
        feature_group_count=c)
    if b is not None:
        y = y + b.astype(x.dtype)
    return y


def s5_group_ssm(u, lam_re, lam_im, log_dt, b_re, b_im, c_re, c_im, d_skip, w_glu):
    bsz, seq_len, _ = u.shape
    f32 = jnp.float32
    uf = u.astype(f32).reshape(bsz, seq_len, N_SSM_GROUPS, SSM_GROUP)
    lr = lam_re.astype(f32)
    li = lam_im.astype(f32)
    dt = jnp.exp(log_dt.astype(f32))[:, None]
    mag = jnp.exp(lr * dt)
    ang = li * dt
    a_re = mag * jnp.cos(ang)
    a_im = mag * jnp.sin(ang)
    den = lr * lr + li * li
    nr = a_re - 1.0
    f_re = (nr * lr + a_im * li) / den
    f_im = (a_im * lr - nr * li) / den
    br = b_re.astype(f32)
    bi = b_im.astype(f32)
    bb_re = f_re[..., None] * br - f_im[..., None] * bi
    bb_im = f_re[..., None] * bi + f_im[..., None] * br
    bu_re = jnp.einsum("blgh,gph->blgp", uf, bb_re)
    bu_im = jnp.einsum("blgh,gph->blgp", uf, bb_im)
    a_re_t = jnp.broadcast_to(a_re[None, None], (1, seq_len, N_SSM_GROUPS, SSM_STATE))
    a_im_t = jnp.broadcast_to(a_im[None, None], (1, seq_len, N_SSM_GROUPS, SSM_STATE))

    def combine(e1, e2):
        ar1, ai1, sr1, si1 = e1
        ar2, ai2, sr2, si2 = e2
        return (ar1 * ar2 - ai1 * ai2,
                ar1 * ai2 + ai1 * ar2,
                ar2 * sr1 - ai2 * si1 + sr2,
                ar2 * si1 + ai2 * sr1 + si2)

    _, _, s_re, s_im = lax.associative_scan(combine, (a_re_t, a_im_t, bu_re, bu_im), axis=1)
    y = (jnp.einsum("blgp,ghp->blgh", s_re, c_re.astype(f32))
         - jnp.einsum("blgp,ghp->blgh", s_im, c_im.astype(f32))
         + d_skip.astype(f32) * uf)
    y = y.reshape(bsz, seq_len, D_SSM)
    g = jax.nn.gelu(y)
    out = g * jax.nn.sigmoid(g @ w_glu.astype(f32))
    return out.astype(u.dtype)


def _fwd_setup_inputs(seed: int = 0) -> dict:
    key = jax.random.key(seed)
    ks = jax.random.split(key, 24)
    f32 = jnp.float32
    nrm = lambda k, shape, s: jax.random.normal(k, shape, f32) * s
    x = jax.random.normal(ks[0], (BATCH, SEQ, D_MODEL), f32)
    meta_tokens = nrm(ks[1], (N_META, D_MODEL), 1.0)
    norm_mix_g = 1.0 + nrm(ks[2], (DEPTH, D_MODEL), 0.02)
    w_in = nrm(ks[3], (DEPTH, D_MODEL, D_IN_PROJ), D_MODEL ** -0.5)
    conv_w = nrm(ks[4], (DEPTH, CONV_WIDTH, D_CONV), CONV_WIDTH ** -0.5)
    n = jnp.arange(SSM_STATE, dtype=f32)
    ssm_lam_re = -0.5 + nrm(ks[5], (DEPTH, N_SSM_GROUPS, SSM_STATE), 1e-3)
    ssm_lam_im = math.pi * n + nrm(ks[6], (DEPTH, N_SSM_GROUPS, SSM_STATE), 1e-3)
    ssm_log_dt = jax.random.uniform(ks[7], (DEPTH, N_SSM_GROUPS), f32,
                                    math.log(DT_MIN), math.log(DT_MAX))
    b_scale = (2.0 * SSM_GROUP) ** -0.5
    ssm_b_re = nrm(ks[8], (DEPTH, N_SSM_GROUPS, SSM_STATE, SSM_GROUP), b_scale)
    ssm_b_im = nrm(ks[9], (DEPTH, N_SSM_GROUPS, SSM_STATE, SSM_GROUP), b_scale)
    c_scale = (2.0 * SSM_STATE) ** -0.5
    ssm_c_re = nrm(ks[10], (DEPTH, N_SSM_GROUPS, SSM_GROUP, SSM_STATE), c_scale)
    ssm_c_im = nrm(ks[11], (DEPTH, N_SSM_GROUPS, SSM_GROUP, SSM_STATE), c_scale)
    ssm_d = nrm(ks[12], (DEPTH, N_SSM_GROUPS, SSM_GROUP), 1.0)
    ssm_w_glu = nrm(ks[13], (DEPTH, D_SSM, D_SSM), D_SSM ** -0.5)
    gain_conv_out = 1.0 + nrm(ks[14], (DEPTH, D_CONV), 0.02)
    gain_ssm_out = 1.0 + nrm(ks[15], (DEPTH, D_SSM), 0.02)
    w_out = nrm(ks[16], (DEPTH, D_MIX, D_MODEL), D_MIX ** -0.5)
    norm_ffn_g = 1.0 + nrm(ks[17], (DEPTH, D_MODEL), 0.02)
    w_up = nrm(ks[18], (DEPTH, D_MODEL, 2 * D_FF), D_MODEL ** -0.5)
    ffn_conv_w = nrm(ks[19], (DEPTH, CONV_WIDTH, 2 * D_FF), CONV_WIDTH ** -0.5)
    ffn_conv_b = nrm(ks[20], (DEPTH, 2 * D_FF), 0.01)
    w_down = nrm(ks[21], (DEPTH, D_FF, D_MODEL), D_FF ** -0.5)
    norm_final_g = 1.0 + nrm(ks[22], (D_MODEL,), 0.02)
    return {"x": x, "meta_tokens": meta_tokens, "norm_mix_g": norm_mix_g, "w_in": w_in,
            "conv_w": conv_w, "ssm_lam_re": ssm_lam_re, "ssm_lam_im": ssm_lam_im,
            "ssm_log_dt": ssm_log_dt, "ssm_b_re": ssm_b_re, "ssm_b_im": ssm_b_im,
            "ssm_c_re": ssm_c_re, "ssm_c_im": ssm_c_im, "ssm_d": ssm_d,
            "ssm_w_glu": ssm_w_glu, "gain_conv_out": gain_conv_out,
            "gain_ssm_out": gain_ssm_out, "w_out": w_out, "norm_ffn_g": norm_ffn_g,
            "w_up": w_up, "ffn_conv_w": ffn_conv_w, "ffn_conv_b": ffn_conv_b,
            "w_down": w_down, "norm_final_g": norm_final_g}


def _fwd_reference(x, meta_tokens, norm_mix_g, w_in, conv_w, ssm_lam_re, ssm_lam_im, ssm_log_dt,
              ssm_b_re, ssm_b_im, ssm_c_re, ssm_c_im, ssm_d, ssm_w_glu, gain_conv_out,
              gain_ssm_out, w_out, norm_ffn_g, w_up, ffn_conv_w, ffn_conv_b, w_down,
              norm_final_g):
    bsz = x.shape[0]
    meta = jnp.broadcast_to(meta_tokens.astype(x.dtype)[None], (bsz, N_META, D_MODEL))
    h = jnp.concatenate([meta, x], axis=1)
    for i in range(DEPTH):
        hn = rms_norm(h, norm_mix_g[i])
        proj = hn @ w_in[i].astype(h.dtype)
        b_gate = proj[..., :D_CONV]
        c_gate = proj[..., D_CONV:2 * D_CONV]
        v = proj[..., 2 * D_CONV:3 * D_CONV]
        u = proj[..., 3 * D_CONV:]
        conv_out = b_gate * causal_dwconv(c_gate * v, conv_w[i])
        ssm_out = s5_group_ssm(u, ssm_lam_re[i], ssm_lam_im[i], ssm_log_dt[i],
                               ssm_b_re[i], ssm_b_im[i], ssm_c_re[i], ssm_c_im[i],
                               ssm_d[i], ssm_w_glu[i])
        mixed = jnp.concatenate([rms_norm(conv_out, gain_conv_out[i]),
                                 rms_norm(ssm_out, gain_ssm_out[i])], axis=-1)
        h = h + mixed @ w_out[i].astype(h.dtype)
        hn = rms_norm(h, norm_ffn_g[i])
        up = causal_dwconv(hn @ w_up[i].astype(h.dtype), ffn_conv_w[i], ffn_conv_b[i])
        a = up[..., :D_FF]
        val = up[..., D_FF:]
        h = h + (jax.nn.silu(a) * val) @ w_down[i].astype(h.dtype)
    y = rms_norm(h, norm_final_g)
    return y[:, N_META:]


import jax as _jax
import jax.numpy as _jnp

TWIN_FORMAT = 'train_step'
FWD_PARAMS = ['x', 'meta_tokens', 'norm_mix_g', 'w_in', 'conv_w', 'ssm_lam_re', 'ssm_lam_im', 'ssm_log_dt', 'ssm_b_re', 'ssm_b_im', 'ssm_c_re', 'ssm_c_im', 'ssm_d', 'ssm_w_glu', 'gain_conv_out', 'gain_ssm_out', 'w_out', 'norm_ffn_g', 'w_up', 'ffn_conv_w', 'ffn_conv_b', 'w_down', 'norm_final_g']
TWIN_WEIGHTS = ['meta_tokens', 'norm_mix_g', 'w_in', 'conv_w', 'ssm_lam_re', 'ssm_lam_im', 'ssm_log_dt', 'ssm_b_re', 'ssm_b_im', 'ssm_c_re', 'ssm_c_im', 'ssm_d', 'ssm_w_glu', 'gain_conv_out', 'gain_ssm_out', 'w_out', 'norm_ffn_g', 'w_up', 'ffn_conv_w', 'ffn_conv_b', 'w_down', 'norm_final_g']
TWIN_DIFF_INPUT = 'x'
TWIN_INPUTS = ['x', 'meta_tokens', 'norm_mix_g', 'w_in', 'conv_w', 'ssm_lam_re', 'ssm_lam_im', 'ssm_log_dt', 'ssm_b_re', 'ssm_b_im', 'ssm_c_re', 'ssm_c_im', 'ssm_d', 'ssm_w_glu', 'gain_conv_out', 'gain_ssm_out', 'w_out', 'norm_ffn_g', 'w_up', 'ffn_conv_w', 'ffn_conv_b', 'w_down', 'norm_final_g', 'loss_target', 'm_meta_tokens', 'm_norm_mix_g', 'm_w_in', 'm_conv_w', 'm_ssm_lam_re', 'm_ssm_lam_im', 'm_ssm_log_dt', 'm_ssm_b_re', 'm_ssm_b_im', 'm_ssm_c_re', 'm_ssm_c_im', 'm_ssm_d', 'm_ssm_w_glu', 'm_gain_conv_out', 'm_gain_ssm_out', 'm_w_out', 'm_norm_ffn_g', 'm_w_up', 'm_ffn_conv_w', 'm_ffn_conv_b', 'm_w_down', 'm_norm_final_g', 'v_meta_tokens', 'v_norm_mix_g', 'v_w_in', 'v_conv_w', 'v_ssm_lam_re', 'v_ssm_lam_im', 'v_ssm_log_dt', 'v_ssm_b_re', 'v_ssm_b_im', 'v_ssm_c_re', 'v_ssm_c_im', 'v_ssm_d', 'v_ssm_w_glu', 'v_gain_conv_out', 'v_gain_ssm_out', 'v_w_out', 'v_norm_ffn_g', 'v_w_up', 'v_ffn_conv_w', 'v_ffn_conv_b', 'v_w_down', 'v_norm_final_g']
TWIN_OUTPUTS = ['loss', 'grad_x', 'grad_meta_tokens', 'grad_norm_mix_g', 'grad_w_in', 'grad_conv_w', 'grad_ssm_lam_re', 'grad_ssm_lam_im', 'grad_ssm_log_dt', 'grad_ssm_b_re', 'grad_ssm_b_im', 'grad_ssm_c_re', 'grad_ssm_c_im', 'grad_ssm_d', 'grad_ssm_w_glu', 'grad_gain_conv_out', 'grad_gain_ssm_out', 'grad_w_out', 'grad_norm_ffn_g', 'grad_w_up', 'grad_ffn_conv_w', 'grad_ffn_conv_b', 'grad_w_down', 'grad_norm_final_g', 'delta_meta_tokens', 'delta_norm_mix_g', 'delta_w_in', 'delta_conv_w', 'delta_ssm_lam_re', 'delta_ssm_lam_im', 'delta_ssm_log_dt', 'delta_ssm_b_re', 'delta_ssm_b_im', 'delta_ssm_c_re', 'delta_ssm_c_im', 'delta_ssm_d', 'delta_ssm_w_glu', 'delta_gain_conv_out', 'delta_gain_ssm_out', 'delta_w_out', 'delta_norm_ffn_g', 'delta_w_up', 'delta_ffn_conv_w', 'delta_ffn_conv_b', 'delta_w_down', 'delta_norm_final_g', 'new_m_meta_tokens', 'new_m_norm_mix_g', 'new_m_w_in', 'new_m_conv_w', 'new_m_ssm_lam_re', 'new_m_ssm_lam_im', 'new_m_ssm_log_dt', 'new_m_ssm_b_re', 'new_m_ssm_b_im', 'new_m_ssm_c_re', 'new_m_ssm_c_im', 'new_m_ssm_d', 'new_m_ssm_w_glu', 'new_m_gain_conv_out', 'new_m_gain_ssm_out', 'new_m_w_out', 'new_m_norm_ffn_g', 'new_m_w_up', 'new_m_ffn_conv_w', 'new_m_ffn_conv_b', 'new_m_w_down', 'new_m_norm_final_g', 'new_v_meta_tokens', 'new_v_norm_mix_g', 'new_v_w_in', 'new_v_conv_w', 'new_v_ssm_lam_re', 'new_v_ssm_lam_im', 'new_v_ssm_log_dt', 'new_v_ssm_b_re', 'new_v_ssm_b_im', 'new_v_ssm_c_re', 'new_v_ssm_c_im', 'new_v_ssm_d', 'new_v_ssm_w_glu', 'new_v_gain_conv_out', 'new_v_gain_ssm_out', 'new_v_w_out', 'new_v_norm_ffn_g', 'new_v_w_up', 'new_v_ffn_conv_w', 'new_v_ffn_conv_b', 'new_v_w_down', 'new_v_norm_final_g']
TWIN_LEAF_KINDS = {'loss': 'loss', 'grad_x': 'grad_x', 'grad_meta_tokens': 'grad_w', 'grad_norm_mix_g': 'grad_w', 'grad_w_in': 'grad_w', 'grad_conv_w': 'grad_w', 'grad_ssm_lam_re': 'grad_w', 'grad_ssm_lam_im': 'grad_w', 'grad_ssm_log_dt': 'grad_w', 'grad_ssm_b_re': 'grad_w', 'grad_ssm_b_im': 'grad_w', 'grad_ssm_c_re': 'grad_w', 'grad_ssm_c_im': 'grad_w', 'grad_ssm_d': 'grad_w', 'grad_ssm_w_glu': 'grad_w', 'grad_gain_conv_out': 'grad_w', 'grad_gain_ssm_out': 'grad_w', 'grad_w_out': 'grad_w', 'grad_norm_ffn_g': 'grad_w', 'grad_w_up': 'grad_w', 'grad_ffn_conv_w': 'grad_w', 'grad_ffn_conv_b': 'grad_w', 'grad_w_down': 'grad_w', 'grad_norm_final_g': 'grad_w', 'delta_meta_tokens': 'delta_w', 'delta_norm_mix_g': 'delta_w', 'delta_w_in': 'delta_w', 'delta_conv_w': 'delta_w', 'delta_ssm_lam_re': 'delta_w', 'delta_ssm_lam_im': 'delta_w', 'delta_ssm_log_dt': 'delta_w', 'delta_ssm_b_re': 'delta_w', 'delta_ssm_b_im': 'delta_w', 'delta_ssm_c_re': 'delta_w', 'delta_ssm_c_im': 'delta_w', 'delta_ssm_d': 'delta_w', 'delta_ssm_w_glu': 'delta_w', 'delta_gain_conv_out': 'delta_w', 'delta_gain_ssm_out': 'delta_w', 'delta_w_out': 'delta_w', 'delta_norm_ffn_g': 'delta_w', 'delta_w_up': 'delta_w', 'delta_ffn_conv_w': 'delta_w', 'delta_ffn_conv_b': 'delta_w', 'delta_w_down': 'delta_w', 'delta_norm_final_g': 'delta_w', 'new_m_meta_tokens': 'new_m', 'new_m_norm_mix_g': 'new_m', 'new_m_w_in': 'new_m', 'new_m_conv_w': 'new_m', 'new_m_ssm_lam_re': 'new_m', 'new_m_ssm_lam_im': 'new_m', 'new_m_ssm_log_dt': 'new_m', 'new_m_ssm_b_re': 'new_m', 'new_m_ssm_b_im': 'new_m', 'new_m_ssm_c_re': 'new_m', 'new_m_ssm_c_im': 'new_m', 'new_m_ssm_d': 'new_m', 'new_m_ssm_w_glu': 'new_m', 'new_m_gain_conv_out': 'new_m', 'new_m_gain_ssm_out': 'new_m', 'new_m_w_out': 'new_m', 'new_m_norm_ffn_g': 'new_m', 'new_m_w_up': 'new_m', 'new_m_ffn_conv_w': 'new_m', 'new_m_ffn_conv_b': 'new_m', 'new_m_w_down': 'new_m', 'new_m_norm_final_g': 'new_m', 'new_v_meta_tokens': 'new_v', 'new_v_norm_mix_g': 'new_v', 'new_v_w_in': 'new_v', 'new_v_conv_w': 'new_v', 'new_v_ssm_lam_re': 'new_v', 'new_v_ssm_lam_im': 'new_v', 'new_v_ssm_log_dt': 'new_v', 'new_v_ssm_b_re': 'new_v', 'new_v_ssm_b_im': 'new_v', 'new_v_ssm_c_re': 'new_v', 'new_v_ssm_c_im': 'new_v', 'new_v_ssm_d': 'new_v', 'new_v_ssm_w_glu': 'new_v', 'new_v_gain_conv_out': 'new_v', 'new_v_gain_ssm_out': 'new_v', 'new_v_w_out': 'new_v', 'new_v_norm_ffn_g': 'new_v', 'new_v_w_up': 'new_v', 'new_v_ffn_conv_w': 'new_v', 'new_v_ffn_conv_b': 'new_v', 'new_v_w_down': 'new_v', 'new_v_norm_final_g': 'new_v'}


def _forward(args):
    return _fwd_reference(*[args[k] for k in FWD_PARAMS])


def _output_shape():
    out = _jax.eval_shape(lambda: _forward(_fwd_setup_inputs(0)))
    return out.shape, out.dtype

N_MICROBATCH = 1
ADAM_LR = 0.001
ADAM_B1 = 0.9
ADAM_B2 = 0.999
ADAM_EPS = 1e-08
ADAM_WD = 0.01
ADAM_STEP = 10
PER_EXAMPLE_BATCH_AXIS = {'x': 0, 'loss_target': 0}
SHARED_INPUTS = []
_WEIGHT_DTYPES = {'meta_tokens': _jnp.float32, 'norm_mix_g': _jnp.float32, 'w_in': _jnp.float32, 'conv_w': _jnp.float32, 'ssm_lam_re': _jnp.float32, 'ssm_lam_im': _jnp.float32, 'ssm_log_dt': _jnp.float32, 'ssm_b_re': _jnp.float32, 'ssm_b_im': _jnp.float32, 'ssm_c_re': _jnp.float32, 'ssm_c_im': _jnp.float32, 'ssm_d': _jnp.float32, 'ssm_w_glu': _jnp.float32, 'gain_conv_out': _jnp.float32, 'gain_ssm_out': _jnp.float32, 'w_out': _jnp.float32, 'norm_ffn_g': _jnp.float32, 'w_up': _jnp.float32, 'ffn_conv_w': _jnp.float32, 'ffn_conv_b': _jnp.float32, 'w_down': _jnp.float32, 'norm_final_g': _jnp.float32}
MOMENT_SCALE = {'meta_tokens': 2.005843e-03, 'norm_mix_g': 1.425768e-01, 'w_in': 9.837124e-02, 'conv_w': 9.808079e-02, 'ssm_lam_re': 4.830826e-03, 'ssm_lam_im': 4.697845e-03, 'ssm_log_dt': 2.564544e+00, 'ssm_b_re': 3.170420e-03, 'ssm_b_im': 3.168234e-03, 'ssm_c_re': 6.372958e-03, 'ssm_c_im': 6.427217e-03, 'ssm_d': 1.087164e-01, 'ssm_w_glu': 2.735097e-02, 'gain_conv_out': 9.851658e-02, 'gain_ssm_out': 1.115475e-01, 'w_out': 9.798890e-02, 'norm_ffn_g': 7.035075e-02, 'w_up': 3.046901e-02, 'ffn_conv_w': 3.043082e-02, 'ffn_conv_b': 2.971571e-02, 'w_down': 5.033962e-02, 'norm_final_g': 1.602704e+01}


def _to_microbatches(a, axis):
    t = _jnp.moveaxis(a, axis, 0)
    t = t.reshape((N_MICROBATCH, t.shape[0] // N_MICROBATCH) + t.shape[1:])
    return _jnp.moveaxis(t, 1, axis + 1)


def setup_inputs(seed: int = 0) -> dict:
    inp = _fwd_setup_inputs(seed)
    key = _jax.random.fold_in(_jax.random.key(seed), 7919)
    shape, _ = _output_shape()
    out = dict(inp)
    out["loss_target"] = _jax.random.normal(_jax.random.fold_in(key, 0), shape, _jnp.float32)
    for i, name in enumerate(TWIN_WEIGHTS):
        w = inp[name].astype(_jnp.float32)
        if MOMENT_SCALE is None:
            s = _jnp.sqrt(_jnp.mean(_jnp.square(w)) + 1e-30)
        else:
            s = MOMENT_SCALE[name]
        km, kv = _jax.random.split(_jax.random.fold_in(key, i + 1))
        out[name] = w
        out["m_" + name] = s * _jax.random.normal(km, w.shape, _jnp.float32)
        out["v_" + name] = (s * s) * _jax.random.uniform(kv, w.shape, _jnp.float32, 0.5, 1.5)
    if N_MICROBATCH > 1:
        for name, axis in PER_EXAMPLE_BATCH_AXIS.items():
            out[name] = _to_microbatches(out[name], axis)
    return {'x': out['x'], 'meta_tokens': out['meta_tokens'], 'norm_mix_g': out['norm_mix_g'], 'w_in': out['w_in'], 'conv_w': out['conv_w'], 'ssm_lam_re': out['ssm_lam_re'], 'ssm_lam_im': out['ssm_lam_im'], 'ssm_log_dt': out['ssm_log_dt'], 'ssm_b_re': out['ssm_b_re'], 'ssm_b_im': out['ssm_b_im'], 'ssm_c_re': out['ssm_c_re'], 'ssm_c_im': out['ssm_c_im'], 'ssm_d': out['ssm_d'], 'ssm_w_glu': out['ssm_w_glu'], 'gain_conv_out': out['gain_conv_out'], 'gain_ssm_out': out['gain_ssm_out'], 'w_out': out['w_out'], 'norm_ffn_g': out['norm_ffn_g'], 'w_up': out['w_up'], 'ffn_conv_w': out['ffn_conv_w'], 'ffn_conv_b': out['ffn_conv_b'], 'w_down': out['w_down'], 'norm_final_g': out['norm_final_g'], 'loss_target': out['loss_target'], 'm_meta_tokens': out['m_meta_tokens'], 'm_norm_mix_g': out['m_norm_mix_g'], 'm_w_in': out['m_w_in'], 'm_conv_w': out['m_conv_w'], 'm_ssm_lam_re': out['m_ssm_lam_re'], 'm_ssm_lam_im': out['m_ssm_lam_im'], 'm_ssm_log_dt': out['m_ssm_log_dt'], 'm_ssm_b_re': out['m_ssm_b_re'], 'm_ssm_b_im': out['m_ssm_b_im'], 'm_ssm_c_re': out['m_ssm_c_re'], 'm_ssm_c_im': out['m_ssm_c_im'], 'm_ssm_d': out['m_ssm_d'], 'm_ssm_w_glu': out['m_ssm_w_glu'], 'm_gain_conv_out': out['m_gain_conv_out'], 'm_gain_ssm_out': out['m_gain_ssm_out'], 'm_w_out': out['m_w_out'], 'm_norm_ffn_g': out['m_norm_ffn_g'], 'm_w_up': out['m_w_up'], 'm_ffn_conv_w': out['m_ffn_conv_w'], 'm_ffn_conv_b': out['m_ffn_conv_b'], 'm_w_down': out['m_w_down'], 'm_norm_final_g': out['m_norm_final_g'], 'v_meta_tokens': out['v_meta_tokens'], 'v_norm_mix_g': out['v_norm_mix_g'], 'v_w_in': out['v_w_in'], 'v_conv_w': out['v_conv_w'], 'v_ssm_lam_re': out['v_ssm_lam_re'], 'v_ssm_lam_im': out['v_ssm_lam_im'], 'v_ssm_log_dt': out['v_ssm_log_dt'], 'v_ssm_b_re': out['v_ssm_b_re'], 'v_ssm_b_im': out['v_ssm_b_im'], 'v_ssm_c_re': out['v_ssm_c_re'], 'v_ssm_c_im': out['v_ssm_c_im'], 'v_ssm_d': out['v_ssm_d'], 'v_ssm_w_glu': out['v_ssm_w_glu'], 'v_gain_conv_out': out['v_gain_conv_out'], 'v_gain_ssm_out': out['v_gain_ssm_out'], 'v_w_out': out['v_w_out'], 'v_norm_ffn_g': out['v_norm_ffn_g'], 'v_w_up': out['v_w_up'], 'v_ffn_conv_w': out['v_ffn_conv_w'], 'v_ffn_conv_b': out['v_ffn_conv_b'], 'v_w_down': out['v_w_down'], 'v_norm_final_g': out['v_norm_final_g']}


def _loss(weights, diff, rest, loss_target):
    with _jax.named_scope("forward"):
        args = {**rest, TWIN_DIFF_INPUT: diff, **{k: w.astype(_WEIGHT_DTYPES[k]) for k, w in weights.items()}}
        y = _forward(args)
    with _jax.named_scope("loss_head"):
        err = _jnp.square(y.astype(_jnp.float32) - loss_target)
        return 0.5 * _jnp.sum(_jnp.mean(err, axis=-1)) if err.ndim else 0.5 * err


def _adamw(w, g, m, v):
    m = ADAM_B1 * m + (1.0 - ADAM_B1) * g
    v = ADAM_B2 * v + (1.0 - ADAM_B2) * _jnp.square(g)
    m_hat = m / (1.0 - ADAM_B1 ** ADAM_STEP)
    v_hat = v / (1.0 - ADAM_B2 ** ADAM_STEP)
    delta = -ADAM_LR * (m_hat / (_jnp.sqrt(v_hat) + ADAM_EPS) + ADAM_WD * w)
    return delta, m, v


def reference(x, meta_tokens, norm_mix_g, w_in, conv_w, ssm_lam_re, ssm_lam_im, ssm_log_dt, ssm_b_re, ssm_b_im, ssm_c_re, ssm_c_im, ssm_d, ssm_w_glu, gain_conv_out, gain_ssm_out, w_out, norm_ffn_g, w_up, ffn_conv_w, ffn_conv_b, w_down, norm_final_g, loss_target, m_meta_tokens, m_norm_mix_g, m_w_in, m_conv_w, m_ssm_lam_re, m_ssm_lam_im, m_ssm_log_dt, m_ssm_b_re, m_ssm_b_im, m_ssm_c_re, m_ssm_c_im, m_ssm_d, m_ssm_w_glu, m_gain_conv_out, m_gain_ssm_out, m_w_out, m_norm_ffn_g, m_w_up, m_ffn_conv_w, m_ffn_conv_b, m_w_down, m_norm_final_g, v_meta_tokens, v_norm_mix_g, v_w_in, v_conv_w, v_ssm_lam_re, v_ssm_lam_im, v_ssm_log_dt, v_ssm_b_re, v_ssm_b_im, v_ssm_c_re, v_ssm_c_im, v_ssm_d, v_ssm_w_glu, v_gain_conv_out, v_gain_ssm_out, v_w_out, v_norm_ffn_g, v_w_up, v_ffn_conv_w, v_ffn_conv_b, v_w_down, v_norm_final_g):
    given = dict(x=x, meta_tokens=meta_tokens, norm_mix_g=norm_mix_g, w_in=w_in, conv_w=conv_w, ssm_lam_re=ssm_lam_re, ssm_lam_im=ssm_lam_im, ssm_log_dt=ssm_log_dt, ssm_b_re=ssm_b_re, ssm_b_im=ssm_b_im, ssm_c_re=ssm_c_re, ssm_c_im=ssm_c_im, ssm_d=ssm_d, ssm_w_glu=ssm_w_glu, gain_conv_out=gain_conv_out, gain_ssm_out=gain_ssm_out, w_out=w_out, norm_ffn_g=norm_ffn_g, w_up=w_up, ffn_conv_w=ffn_conv_w, ffn_conv_b=ffn_conv_b, w_down=w_down, norm_final_g=norm_final_g, loss_target=loss_target, m_meta_tokens=m_meta_tokens, m_norm_mix_g=m_norm_mix_g, m_w_in=m_w_in, m_conv_w=m_conv_w, m_ssm_lam_re=m_ssm_lam_re, m_ssm_lam_im=m_ssm_lam_im, m_ssm_log_dt=m_ssm_log_dt, m_ssm_b_re=m_ssm_b_re, m_ssm_b_im=m_ssm_b_im, m_ssm_c_re=m_ssm_c_re, m_ssm_c_im=m_ssm_c_im, m_ssm_d=m_ssm_d, m_ssm_w_glu=m_ssm_w_glu, m_gain_conv_out=m_gain_conv_out, m_gain_ssm_out=m_gain_ssm_out, m_w_out=m_w_out, m_norm_ffn_g=m_norm_ffn_g, m_w_up=m_w_up, m_ffn_conv_w=m_ffn_conv_w, m_ffn_conv_b=m_ffn_conv_b, m_w_down=m_w_down, m_norm_final_g=m_norm_final_g, v_meta_tokens=v_meta_tokens, v_norm_mix_g=v_norm_mix_g, v_w_in=v_w_in, v_conv_w=v_conv_w, v_ssm_lam_re=v_ssm_lam_re, v_ssm_lam_im=v_ssm_lam_im, v_ssm_log_dt=v_ssm_log_dt, v_ssm_b_re=v_ssm_b_re, v_ssm_b_im=v_ssm_b_im, v_ssm_c_re=v_ssm_c_re, v_ssm_c_im=v_ssm_c_im, v_ssm_d=v_ssm_d, v_ssm_w_glu=v_ssm_w_glu, v_gain_conv_out=v_gain_conv_out, v_gain_ssm_out=v_gain_ssm_out, v_w_out=v_w_out, v_norm_ffn_g=v_norm_ffn_g, v_w_up=v_w_up, v_ffn_conv_w=v_ffn_conv_w, v_ffn_conv_b=v_ffn_conv_b, v_w_down=v_w_down, v_norm_final_g=v_norm_final_g)
    weights = {n: given[n] for n in TWIN_WEIGHTS}
    shared = {n: given[n] for n in SHARED_INPUTS}
    per_example = {n: given[n] for n in ['x']}
    grad_fn = _jax.value_and_grad(_loss, argnums=(0, 1))

    def one_microbatch(ex, loss_target):
        ex = dict(ex)
        diff = ex.pop(TWIN_DIFF_INPUT)
        return grad_fn(weights, diff, {**shared, **ex}, loss_target)

    if N_MICROBATCH == 1:
        loss, (grad_w, grad_x) = one_microbatch(per_example, given["loss_target"])
    else:
        def body(carry, xs):
            loss_sum, grad_sum = carry
            l_k, (gw_k, gx_k) = one_microbatch(xs[0], xs[1])
            with _jax.named_scope("update"):
                return (loss_sum + l_k, _jax.tree.map(_jnp.add, grad_sum, gw_k)), gx_k

        init = (_jnp.zeros((), _jnp.float32), _jax.tree.map(_jnp.zeros_like, weights))
        (loss, grad_w), grad_x = _jax.lax.scan(body, init, (per_example, given["loss_target"]))
    with _jax.named_scope("update"):
        delta_w, new_m, new_v = {}, {}, {}
        for n in TWIN_WEIGHTS:
            delta_w[n], new_m[n], new_v[n] = _adamw(weights[n], grad_w[n], given["m_" + n], given["v_" + n])
    return (loss, grad_x, *[grad_w[n] for n in TWIN_WEIGHTS], *[delta_w[n] for n in TWIN_WEIGHTS],
            *[new_m[n] for n in TWIN_WEIGHTS], *[new_v[n] for n in TWIN_WEIGHTS])
```

```python
import functools
import math

import jax
import jax.numpy as jnp
from jax import lax
from jax.experimental import pallas as pl
from jax.experimental.pallas import tpu as pltpu

F32 = jnp.float32
BF16 = jnp.bfloat16
MESH = pl.DeviceIdType.MESH

N_DEV = 8
RMS_EPS = 1e-6
ADAM_LR = 0.001
ADAM_B1 = 0.9
ADAM_B2 = 0.999
ADAM_EPS = 1e-08
ADAM_WD = 0.01
ADAM_STEP = 10
ADAM_BC1 = 1.0 - ADAM_B1 ** ADAM_STEP
ADAM_BC2 = 1.0 - ADAM_B2 ** ADAM_STEP

SUBLANES = 8
LANES = 128
ROW_TILE = 128
N_SEG = 8
SSM_CHUNK = 128
VMEM_LIMIT = 48 * 1024 * 1024

NN = ((1,), (0,))
NT = ((1,), (1,))
TN = ((0,), (0,))


def _params(**kw):
    return pltpu.CompilerParams(vmem_limit_bytes=VMEM_LIMIT, **kw)


def _dot(a, b, dims):
    return lax.dot_general(a, b, (dims, ((), ())), preferred_element_type=F32)


def _mean_sq_rsqrt(x):
    return lax.rsqrt(jnp.mean(x * x, axis=-1, keepdims=True) + RMS_EPS)


def _rms_bwd(x, r, g, dy):
    xhat = x * r
    dxh = dy * g
    dx = r * (dxh - xhat * jnp.mean(dxh * xhat, axis=-1, keepdims=True))
    return dx, dy * xhat


def _gelu(y):
    c = math.sqrt(2.0 / math.pi)
    t = jnp.tanh(c * (y + 0.044715 * y * y * y))
    return 0.5 * y * (1.0 + t), t


def _gelu_grad(y, t):
    c = math.sqrt(2.0 / math.pi)
    return 0.5 * (1.0 + t) + 0.5 * y * (1.0 - t * t) * c * (1.0 + 3.0 * 0.044715 * y * y)


def _shift_down(ext, k, rows):
    return pltpu.roll(ext, k, 0)[SUBLANES:SUBLANES + rows]


def _shift_up(ext, k, rows):
    n = ext.shape[0]
    return pltpu.roll(ext, n - k, 0)[:rows]


def _dev_index(p):
    return 4 * p[0] + 2 * p[1] + p[2]


def _allgather(name, shards):
    n = len(shards)

    def body(*refs):
        ins, outs = refs[:n], refs[n:2 * n]
        send_sems, recv_sems, local_sems = refs[2 * n:]
        x, y, c = lax.axis_index("x"), lax.axis_index("y"), lax.axis_index("c")
        me, sibling = (x, y, c), (x, y, 1 - c)
        chips = [(1 - x, y), (x, 1 - y), (1 - x, 1 - y)]

        def copy(a, k, block, to, src=None):
            dst = outs[a].at[_dev_index(block)]
            return pltpu.make_async_remote_copy(
                src_ref=dst if src is None else src, dst_ref=dst,
                send_sem=send_sems.at[a, k], recv_sem=recv_sems.at[a, k],
                device_id=to, device_id_type=MESH)

        mine = [pltpu.make_async_copy(ins[a], outs[a].at[_dev_index(me)], local_sems.at[a])
                for a in range(n)]
        for cp in mine:
            cp.start()
        first = []
        for a in range(n):
            first.append(copy(a, 0, me, sibling, src=ins[a]))
            for j, chip in enumerate(chips):
                first.append(copy(a, 1 + j, me, (*chip, c), src=ins[a]))
        for cp in first:
            cp.start()
        passed = []
        for j, chip in enumerate(chips):
            for a in range(n):
                copy(a, 1 + j, (*chip, c), me).wait_recv()
                fwd = copy(a, 4 + j, (*chip, c), sibling)
                fwd.start()
                passed.append(fwd)
        for a in range(n):
            copy(a, 0, sibling, me).wait_recv()
            for j, chip in enumerate(chips):
                copy(a, 4 + j, (*chip, 1 - c), me).wait_recv()
        for cp in first + passed:
            cp.wait_send()
        for cp in mine:
            cp.wait()

    any_spec = pl.BlockSpec(memory_space=pl.ANY)
    return pl.pallas_call(
        body, name=name,
        out_shape=[jax.ShapeDtypeStruct((N_DEV,) + s.shape, s.dtype) for s in shards],
        in_specs=[any_spec] * n, out_specs=[any_spec] * n,
        scratch_shapes=[pltpu.SemaphoreType.DMA((n, 7)), pltpu.SemaphoreType.DMA((n, 7)),
                        pltpu.SemaphoreType.DMA((n,))],
    )(*shards)


def _exchange(name, blocks):
    n = len(blocks)

    def body(*refs):
        ins, outs = refs[:n], refs[n:2 * n]
        send_sems, recv_sems, local_sems = refs[2 * n:]
        x, y, c = lax.axis_index("x"), lax.axis_index("y"), lax.axis_index("c")
        me = (x, y, c)
        copies = []
        for a in range(n):
            cp = pltpu.make_async_copy(ins[a].at[_dev_index(me)], outs[a].at[_dev_index(me)],
                                       local_sems.at[a])
            cp.start()
            copies.append(cp)
        for a in range(n):
            for k in range(1, N_DEV):
                peer = tuple((1 - v) if (k >> s) & 1 else v for v, s in zip(me, (2, 1, 0)))
                cp = pltpu.make_async_remote_copy(
                    src_ref=ins[a].at[_dev_index(peer)], dst_ref=outs[a].at[_dev_index(me)],
                    send_sem=send_sems.at[a, k - 1], recv_sem=recv_sems.at[a, k - 1],
                    device_id=peer, device_id_type=MESH)
                cp.start()
                copies.append(cp)
        for cp in copies:
            cp.wait()

    any_spec = pl.BlockSpec(memory_space=pl.ANY)
    return pl.pallas_call(
        body, name=name,
        out_shape=[jax.ShapeDtypeStruct(b.shape, b.dtype) for b in blocks],
        in_specs=[any_spec] * n, out_specs=[any_spec] * n,
        scratch_shapes=[pltpu.SemaphoreType.DMA((n, 7)), pltpu.SemaphoreType.DMA((n, 7)),
                        pltpu.SemaphoreType.DMA((n,))],
    )(*blocks)


def _mm(name, a, b, *, dims, grid, a_spec, b_spec, o_spec, out_shape, acc_shape=None,
        res=None, res_spec=None):
    n_red = grid[-1] if acc_shape is not None else 1
    red_axis = len(grid) - 1

    def body(*refs):
        a_ref, b_ref = refs[0], refs[1]
        r_ref = refs[2] if res is not None else None
        o_ref = refs[3] if res is not None else refs[2]
        part = _dot(a_ref[...], b_ref[...], dims)
        if acc_shape is None:
            if r_ref is not None:
                part = part + r_ref[...]
            o_ref[...] = part.astype(o_ref.dtype)
            return
        acc_ref = refs[-1]
        k = pl.program_id(red_axis)

        @pl.when(k == 0)
        def _():
            acc_ref[...] = part

        @pl.when(k > 0)
        def _():
            acc_ref[...] += part

        @pl.when(k == n_red - 1)
        def _():
            total = acc_ref[...]
            if r_ref is not None:
                total = total + r_ref[...]
            o_ref[...] = total.astype(o_ref.dtype)

    ins, in_specs = [a, b], [a_spec, b_spec]
    if res is not None:
        ins.append(res)
        in_specs.append(res_spec)
    return pl.pallas_call(
        body, name=name, grid=grid, in_specs=in_specs, out_specs=o_spec, out_shape=out_shape,
        scratch_shapes=[pltpu.VMEM(acc_shape, F32)] if acc_shape is not None else [],
        compiler_params=_params(),
    )(*ins)


def _rms_fwd(name, h, g):
    t, d = h.shape

    def body(h_ref, g_ref, o_ref):
        x = h_ref[...]
        o_ref[...] = (x * _mean_sq_rsqrt(x) * g_ref[...]).astype(BF16)

    return pl.pallas_call(
        body, name=name, grid=(t // ROW_TILE,),
        in_specs=[pl.BlockSpec((ROW_TILE, d), lambda i: (i, 0)), pl.BlockSpec((1, d), lambda i: (0, 0))],
        out_specs=pl.BlockSpec((ROW_TILE, d), lambda i: (i, 0)),
        out_shape=jax.ShapeDtypeStruct((t, d), BF16), compiler_params=_params(),
    )(h, g)


def _rms_bwd_res(name, dres, dhn, h, g):
    t, d = h.shape

    def body(dres_ref, dhn_ref, h_ref, g_ref, dh_ref, dhb_ref, dg_ref):
        x = h_ref[...]
        dx, dgt = _rms_bwd(x, _mean_sq_rsqrt(x), g_ref[...], dhn_ref[...])
        dh = dres_ref[...] + dx
        dh_ref[...] = dh
        dhb_ref[...] = dh.astype(BF16)

        @pl.when(pl.program_id(0) == 0)
        def _():
            dg_ref[...] = jnp.zeros_like(dg_ref)

        dg_ref[...] += jnp.sum(dgt, axis=0, keepdims=True)

    row = pl.BlockSpec((ROW_TILE, d), lambda i: (i, 0))
    vec = pl.BlockSpec((1, d), lambda i: (0, 0))
    return pl.pallas_call(
        body, name=name, grid=(t // ROW_TILE,), in_specs=[row, row, row, vec],
        out_specs=[row, row, vec],
        out_shape=[jax.ShapeDtypeStruct((t, d), F32), jax.ShapeDtypeStruct((t, d), BF16),
                   jax.ShapeDtypeStruct((1, d), F32)],
        compiler_params=_params(),
    )(dres, dhn, h, g)


def _loss_bwd(h2, target, g, row_lo, row_hi):
    t, d = h2.shape

    def body(h_ref, tg_ref, g_ref, dh_ref, dhb_ref, loss_ref, dg_ref):
        i = pl.program_id(0)
        x = h_ref[...]
        r = _mean_sq_rsqrt(x)
        gv = g_ref[...]
        y = x * r * gv
        rows = i * ROW_TILE + lax.broadcasted_iota(jnp.int32, (ROW_TILE, 1), 0)
        valid = jnp.logical_and(rows >= row_lo, rows < row_hi)
        err = jnp.where(valid, y - tg_ref[...], 0.0)
        dy = err * (1.0 / d)
        dx, dgt = _rms_bwd(x, r, gv, dy)
        dh_ref[...] = dx
        dhb_ref[...] = dx.astype(BF16)

        @pl.when(i == 0)
        def _():
            loss_ref[...] = jnp.zeros_like(loss_ref)
            dg_ref[...] = jnp.zeros_like(dg_ref)

        row_loss = jnp.mean(err * err, axis=-1, keepdims=True)
        loss_ref[...] += 0.5 * jnp.sum(row_loss, axis=0, keepdims=True)
        dg_ref[...] += jnp.sum(dgt, axis=0, keepdims=True)

    row = pl.BlockSpec((ROW_TILE, d), lambda i: (i, 0))
    vec = pl.BlockSpec((1, d), lambda i: (0, 0))
    return pl.pallas_call(
        body, name="loss_bwd", grid=(t // ROW_TILE,), in_specs=[row, row, vec],
        out_specs=[row, row, pl.BlockSpec((1, 1), lambda i: (0, 0)), vec],
        out_shape=[jax.ShapeDtypeStruct((t, d), F32), jax.ShapeDtypeStruct((t, d), BF16),
                   jax.ShapeDtypeStruct((1, 1), F32), jax.ShapeDtypeStruct((1, d), F32)],
        compiler_params=_params(),
    )(h2, target, g)


def _prev_halo(i):
    return jnp.maximum(i * (ROW_TILE // SUBLANES) - 1, 0)


def _causal_taps(cur, halo, first):
    rows = cur.shape[0]
    halo = jnp.where(first, 0.0, halo)
    ext = jnp.concatenate([halo, cur], axis=0)
    return _shift_down(ext, 2, rows), _shift_down(ext, 1, rows)


def _anticausal_taps(cur, halo, last):
    rows = cur.shape[0]
    halo = jnp.where(last, 0.0, halo)
    ext = jnp.concatenate([cur, halo], axis=0)
    return _shift_up(ext, 1, rows), _shift_up(ext, 2, rows)


def _mix_fwd(proj, y, w_glu, conv_w, gain_c, gain_s):
    t = proj.shape[0]
    dc = conv_w.shape[1]
    ds = y.shape[1]

    def body(p_ref, halo_ref, y_ref, wg_ref, cw_ref, gc_ref, gs_ref, mixed_ref, z_ref):
        i = pl.program_id(0)
        p = p_ref[...]
        b, c, v = p[:, :dc], p[:, dc:2 * dc], p[:, 2 * dc:3 * dc]
        cv = c * v
        hp = halo_ref[...]
        x2, x1 = _causal_taps(cv, hp[:, dc:2 * dc] * hp[:, 2 * dc:3 * dc], i == 0)
        cw = cw_ref[...]
        conv = cw[0:1] * x2 + cw[1:2] * x1 + cw[2:3] * cv
        co = b * conv
        mixed_ref[:, :dc] = (co * _mean_sq_rsqrt(co) * gc_ref[...]).astype(BF16)
        g, _ = _gelu(y_ref[...])
        z = _dot(g.astype(BF16), wg_ref[...], NN)
        z_ref[...] = z
        so = g * jax.nn.sigmoid(z)
        mixed_ref[:, dc:] = (so * _mean_sq_rsqrt(so) * gs_ref[...]).astype(BF16)

    const = lambda i: (0, 0)
    return pl.pallas_call(
        body, name="mix_fwd", grid=(t // ROW_TILE,),
        in_specs=[pl.BlockSpec((ROW_TILE, 3 * dc), lambda i: (i, 0)),
                  pl.BlockSpec((SUBLANES, 3 * dc), lambda i: (_prev_halo(i), 0)),
                  pl.BlockSpec((ROW_TILE, ds), lambda i: (i, 0)),
                  pl.BlockSpec((ds, ds), const), pl.BlockSpec(conv_w.shape, const),
                  pl.BlockSpec((1, dc), const), pl.BlockSpec((1, ds), const)],
        out_specs=[pl.BlockSpec((ROW_TILE, dc + ds), lambda i: (i, 0)),
                   pl.BlockSpec((ROW_TILE, ds), lambda i: (i, 0))],
        out_shape=[jax.ShapeDtypeStruct((t, dc + ds), BF16), jax.ShapeDtypeStruct((t, ds), F32)],
        compiler_params=_params(),
    )(proj, proj, y, w_glu, conv_w, gain_c, gain_s)


def _mix_bwd1(proj, y, z, dmixed, w_glu, conv_w, gain_c, gain_s):
    t = proj.shape[0]
    dc = conv_w.shape[1]
    ds = y.shape[1]

    def body(p_ref, halo_ref, y_ref, z_ref, dm_ref, wg_ref, cw_ref, gc_ref, gs_ref,
             db_ref, dconv_ref, dy_ref, dwg_ref, dcw_ref, dgc_ref, dgs_ref):
        i = pl.program_id(0)

        @pl.when(i == 0)
        def _():
            dwg_ref[...] = jnp.zeros_like(dwg_ref)
            dcw_ref[...] = jnp.zeros_like(dcw_ref)
            dgc_ref[...] = jnp.zeros_like(dgc_ref)
            dgs_ref[...] = jnp.zeros_like(dgs_ref)

        p = p_ref[...]
        b, c, v = p[:, :dc], p[:, dc:2 * dc], p[:, 2 * dc:3 * dc]
        cv = c * v
        hp = halo_ref[...]
        x2, x1 = _causal_taps(cv, hp[:, dc:2 * dc] * hp[:, 2 * dc:3 * dc], i == 0)
        cw = cw_ref[...]
        conv = cw[0:1] * x2 + cw[1:2] * x1 + cw[2:3] * cv
        co = b * conv
        dm = dm_ref[...]
        dco, dgt = _rms_bwd(co, _mean_sq_rsqrt(co), gc_ref[...], dm[:, :dc])
        dgc_ref[...] += jnp.sum(dgt, axis=0, keepdims=True)
        db_ref[...] = (dco * conv).astype(BF16)
        dconv = dco * b
        dconv_ref[...] = dconv
        dcw_ref[0:1, :] += jnp.sum(dconv * x2, axis=0, keepdims=True)
        dcw_ref[1:2, :] += jnp.sum(dconv * x1, axis=0, keepdims=True)
        dcw_ref[2:3, :] += jnp.sum(dconv * cv, axis=0, keepdims=True)

        yv = y_ref[...]
        g, th = _gelu(yv)
        sg = jax.nn.sigmoid(z_ref[...])
        so = g * sg
        dso, dgt = _rms_bwd(so, _mean_sq_rsqrt(so), gs_ref[...], dm[:, dc:])
        dgs_ref[...] += jnp.sum(dgt, axis=0, keepdims=True)
        dz = (dso * g * sg * (1.0 - sg)).astype(BF16)
        dg = dso * sg + _dot(dz, wg_ref[...], NT)
        dwg_ref[...] += _dot(g.astype(BF16), dz, TN)
        dy_ref[...] = dg * _gelu_grad(yv, th)

    const = lambda i: (0, 0)
    row = lambda w: pl.BlockSpec((ROW_TILE, w), lambda i: (i, 0))
    return pl.pallas_call(
        body, name="mix_bwd1", grid=(t // ROW_TILE,),
        in_specs=[row(3 * dc), pl.BlockSpec((SUBLANES, 3 * dc), lambda i: (_prev_halo(i), 0)),
                  row(ds), row(ds), row(dc + ds), pl.BlockSpec((ds, ds), const),
                  pl.BlockSpec(conv_w.shape, const), pl.BlockSpec((1, dc), const),
                  pl.BlockSpec((1, ds), const)],
        out_specs=[row(dc), row(dc), row(ds), pl.BlockSpec((ds, ds), const),
                   pl.BlockSpec(conv_w.shape, const), pl.BlockSpec((1, dc), const),
                   pl.BlockSpec((1, ds), const)],
        out_shape=[jax.ShapeDtypeStruct((t, dc), BF16), jax.ShapeDtypeStruct((t, dc), F32),
                   jax.ShapeDtypeStruct((t, ds), F32), jax.ShapeDtypeStruct((ds, ds), F32),
                   jax.ShapeDtypeStruct(conv_w.shape, F32), jax.ShapeDtypeStruct((1, dc), F32),
                   jax.ShapeDtypeStruct((1, ds), F32)],
        compiler_params=_params(),
    )(proj, proj, y, z, dmixed, w_glu, conv_w, gain_c, gain_s)


def _mix_bwd2(proj, dconv, conv_w):
    t = proj.shape[0]
    dc = conv_w.shape[1]
    n_tiles = t // ROW_TILE
    per_tile = ROW_TILE // SUBLANES

    def body(c_ref, v_ref, d_ref, halo_ref, cw_ref, o_ref):
        i = pl.program_id(0)
        d = d_ref[...]
        u1, u2 = _anticausal_taps(d, halo_ref[...], i == n_tiles - 1)
        cw = cw_ref[...]
        dcv = cw[2:3] * d + cw[1:2] * u1 + cw[0:1] * u2
        o_ref[:, :dc] = (dcv * v_ref[...]).astype(BF16)
        o_ref[:, dc:] = (dcv * c_ref[...]).astype(BF16)

    return pl.pallas_call(
        body, name="mix_bwd2", grid=(n_tiles,),
        in_specs=[pl.BlockSpec((ROW_TILE, dc), lambda i: (i, 1)),
                  pl.BlockSpec((ROW_TILE, dc), lambda i: (i, 2)),
                  pl.BlockSpec((ROW_TILE, dc), lambda i: (i, 0)),
                  pl.BlockSpec((SUBLANES, dc),
                               lambda i: (jnp.minimum((i + 1) * per_tile, t // SUBLANES - 1), 0)),
                  pl.BlockSpec(conv_w.shape, lambda i: (0, 0))],
        out_specs=pl.BlockSpec((ROW_TILE, 2 * dc), lambda i: (i, 0)),
        out_shape=jax.ShapeDtypeStruct((t, 2 * dc), BF16), compiler_params=_params(),
    )(proj, proj, dconv, dconv, conv_w)


def _ffn_taps(up_ref, halo_ref, fw_ref, fb_ref, q, first):
    x0 = up_ref[q]
    x2, x1 = _causal_taps(x0, halo_ref[q], first)
    w = fw_ref[q]
    return w[0:1] * x2 + w[1:2] * x1 + w[2:3] * x0 + fb_ref[q], x2, x1, x0


def _ffn_fwd(up_pre, fw, fb):
    nb, t, w = up_pre.shape
    half = nb // 2

    def body(up_ref, halo_ref, fw_ref, fb_ref, act_ref):
        first = pl.program_id(0) == 0
        for q in range(half):
            a = _ffn_taps(up_ref, halo_ref, fw_ref, fb_ref, q, first)[0]
            val = _ffn_taps(up_ref, halo_ref, fw_ref, fb_ref, q + half, first)[0]
            act_ref[q] = (a * jax.nn.sigmoid(a) * val).astype(BF16)

    return pl.pallas_call(
        body, name="ffn_fwd", grid=(t // ROW_TILE,),
        in_specs=[pl.BlockSpec((nb, ROW_TILE, w), lambda i: (0, i, 0)),
                  pl.BlockSpec((nb, SUBLANES, w), lambda i: (0, _prev_halo(i), 0)),
                  pl.BlockSpec(fw.shape, lambda i: (0, 0, 0)),
                  pl.BlockSpec(fb.shape, lambda i: (0, 0, 0))],
        out_specs=pl.BlockSpec((half, ROW_TILE, w), lambda i: (0, i, 0)),
        out_shape=jax.ShapeDtypeStruct((half, t, w), BF16), compiler_params=_params(),
    )(up_pre, up_pre, fw, fb)


def _ffn_bwd1(up_pre, dact, fw, fb):
    nb, t, w = up_pre.shape
    half = nb // 2

    def body(up_ref, halo_ref, dact_ref, fw_ref, fb_ref, dup_ref, dfw_ref, dfb_ref):
        first = pl.program_id(0) == 0

        @pl.when(first)
        def _():
            dfw_ref[...] = jnp.zeros_like(dfw_ref)
            dfb_ref[...] = jnp.zeros_like(dfb_ref)

        for q in range(half):
            taps_a = _ffn_taps(up_ref, halo_ref, fw_ref, fb_ref, q, first)
            taps_v = _ffn_taps(up_ref, halo_ref, fw_ref, fb_ref, q + half, first)
            a, val = taps_a[0], taps_v[0]
            da_ct = dact_ref[q]
            sg = jax.nn.sigmoid(a)
            da = da_ct * val * sg * (1.0 + a * (1.0 - sg))
            dval = da_ct * a * sg
            for blk, dv, taps in ((q, da, taps_a), (q + half, dval, taps_v)):
                dup_ref[blk] = dv
                dfb_ref[blk] += jnp.sum(dv, axis=0, keepdims=True)
                for k in range(3):
                    dfw_ref[blk, k:k + 1, :] += jnp.sum(dv * taps[1 + k], axis=0, keepdims=True)

    return pl.pallas_call(
        body, name="ffn_bwd1", grid=(t // ROW_TILE,),
        in_specs=[pl.BlockSpec((nb, ROW_TILE, w), lambda i: (0, i, 0)),
                  pl.BlockSpec((nb, SUBLANES, w), lambda i: (0, _prev_halo(i), 0)),
                  pl.BlockSpec((half, ROW_TILE, w), lambda i: (0, i, 0)),
                  pl.BlockSpec(fw.shape, lambda i: (0, 0, 0)),
                  pl.BlockSpec(fb.shape, lambda i: (0, 0, 0))],
        out_specs=[pl.BlockSpec((nb, ROW_TILE, w), lambda i: (0, i, 0)),
                   pl.BlockSpec(fw.shape, lambda i: (0, 0, 0)),
                   pl.BlockSpec(fb.shape, lambda i: (0, 0, 0))],
        out_shape=[jax.ShapeDtypeStruct((nb, t, w), F32), jax.ShapeDtypeStruct(fw.shape, F32),
                   jax.ShapeDtypeStruct(fb.shape, F32)],
        compiler_params=_params(),
    )(up_pre, up_pre, dact, fw, fb)


def _ffn_bwd2(dup, fw):
    nb, t, w = dup.shape
    n_tiles = t // ROW_TILE
    per_tile = ROW_TILE // SUBLANES

    def body(d_ref, halo_ref, fw_ref, o_ref):
        last = pl.program_id(0) == n_tiles - 1
        for q in range(nb):
            d = d_ref[q]
            u1, u2 = _anticausal_taps(d, halo_ref[q], last)
            cw = fw_ref[q]
            o_ref[q] = (cw[2:3] * d + cw[1:2] * u1 + cw[0:1] * u2).astype(BF16)

    return pl.pallas_call(
        body, name="ffn_bwd2", grid=(n_tiles,),
        in_specs=[pl.BlockSpec((nb, ROW_TILE, w), lambda i: (0, i, 0)),
                  pl.BlockSpec((nb, SUBLANES, w),
                               lambda i: (0, jnp.minimum((i + 1) * per_tile, t // SUBLANES - 1), 0)),
                  pl.BlockSpec(fw.shape, lambda i: (0, 0, 0))],
        out_specs=pl.BlockSpec((nb, ROW_TILE, w), lambda i: (0, i, 0)),
        out_shape=jax.ShapeDtypeStruct((nb, t, w), BF16), compiler_params=_params(),
    )(dup, dup, fw)


def _to_segments(a):
    t, c = a.shape
    return a.reshape(N_SEG, t // N_SEG, c).transpose(1, 0, 2).reshape(t, c)


def _from_segments(a):
    t, c = a.shape
    return a.reshape(t // N_SEG, N_SEG, c).transpose(1, 0, 2).reshape(t, c)


def _cmul(ar, ai, br, bi):
    return ar * br - ai * bi, ar * bi + ai * br


def _segment_carries(pr, pi, fr, fi, forward):
    row = lax.broadcasted_iota(jnp.int32, fr.shape, 0)
    edge = row == (0 if forward else N_SEG - 1)
    shift = 1 if forward else N_SEG - 1
    sr, si = jnp.zeros_like(fr), jnp.zeros_like(fi)
    for _ in range(N_SEG - 1):
        tr, ti = _cmul(pr, pi, sr, si)
        sr = jnp.where(edge, 0.0, pltpu.roll(tr + fr, shift, 0))
        si = jnp.where(edge, 0.0, pltpu.roll(ti + fi, shift, 0))
    return sr, si


def _rows(i):
    return pl.ds(pl.multiple_of(i * SUBLANES, SUBLANES), SUBLANES)


def _s5_fwd(u, bb_re, bb_im, a_re, a_im, cc_re, cc_im, d_skip):
    t, ds = u.shape
    nch, _, cs = bb_re.shape
    steps = t // N_SEG

    def body(u_ref, bbr_ref, bbi_ref, ar_ref, ai_ref, ccr_ref, cci_ref, d_ref, sr_ref, si_ref, y_ref):
        ub = u_ref[...].astype(BF16)
        sr_ref[...] = _dot(ub, bbr_ref[...], NN)
        si_ref[...] = _dot(ub, bbi_ref[...], NN)
        ar = jnp.broadcast_to(ar_ref[...], (N_SEG, cs))
        ai = jnp.broadcast_to(ai_ref[...], (N_SEG, cs))
        zero = jnp.zeros((N_SEG, cs), F32)

        def totals(i, carry):
            sr, si, pr, pi = carry
            tr, ti = _cmul(ar, ai, sr, si)
            qr, qi = _cmul(ar, ai, pr, pi)
            return tr + sr_ref[_rows(i), :], ti + si_ref[_rows(i), :], qr, qi

        fr, fi, pr, pi = lax.fori_loop(0, steps, totals, (zero, zero, zero + 1.0, zero))
        s0r, s0i = _segment_carries(pr, pi, fr, fi, True)

        def scan(i, carry):
            tr, ti = _cmul(ar, ai, *carry)
            nr, ni = tr + sr_ref[_rows(i), :], ti + si_ref[_rows(i), :]
            sr_ref[_rows(i), :] = nr
            si_ref[_rows(i), :] = ni
            return nr, ni

        lax.fori_loop(0, steps, scan, (s0r, s0i))
        y_ref[...] = (_dot(sr_ref[...].astype(BF16), ccr_ref[...], NN)
                      - _dot(si_ref[...].astype(BF16), cci_ref[...], NN)
                      + d_ref[...] * u_ref[...])

    chunk3 = lambda r, c: pl.BlockSpec((None, r, c), lambda j: (j, 0, 0))
    return pl.pallas_call(
        body, name="s5_fwd", grid=(nch,),
        in_specs=[pl.BlockSpec((t, SSM_CHUNK), lambda j: (0, j)),
                  chunk3(SSM_CHUNK, cs), chunk3(SSM_CHUNK, cs), chunk3(1, cs), chunk3(1, cs),
                  chunk3(cs, SSM_CHUNK), chunk3(cs, SSM_CHUNK), chunk3(1, SSM_CHUNK)],
        out_specs=[pl.BlockSpec((t, cs), lambda j: (0, j)), pl.BlockSpec((t, cs), lambda j: (0, j)),
                   pl.BlockSpec((t, SSM_CHUNK), lambda j: (0, j))],
        out_shape=[jax.ShapeDtypeStruct((t, nch * cs), F32), jax.ShapeDtypeStruct((t, nch * cs), F32),
                   jax.ShapeDtypeStruct((t, ds), F32)],
        compiler_params=_params(),
    )(u, bb_re, bb_im, a_re, a_im, cc_re, cc_im, d_skip)


def _s5_bwd(dy, u, s_re, s_im, bb_re, bb_im, a_re, a_im, cc_re, cc_im, d_skip):
    t, ds = u.shape
    nch, _, cs = bb_re.shape
    steps = t // N_SEG

    def body(dy_ref, u_ref, sr_ref, si_ref, bbr_ref, bbi_ref, ar_ref, ai_ref, ccr_ref, cci_ref, d_ref,
             du_ref, dbbr_ref, dbbi_ref, dar_ref, dai_ref, dccr_ref, dcci_ref, dd_ref, gr_ref, gi_ref):
        dyv = dy_ref[...]
        dyb = dyv.astype(BF16)
        gr_ref[...] = _dot(dyb, ccr_ref[...], NT)
        gi_ref[...] = -_dot(dyb, cci_ref[...], NT)
        ar = jnp.broadcast_to(ar_ref[...], (N_SEG, cs))
        ai = -jnp.broadcast_to(ai_ref[...], (N_SEG, cs))
        zero = jnp.zeros((N_SEG, cs), F32)

        def totals(k, carry):
            i = steps - 1 - k
            gr, gi, pr, pi = carry
            tr, ti = _cmul(ar, ai, gr, gi)
            qr, qi = _cmul(ar, ai, pr, pi)
            return tr + gr_ref[_rows(i), :], ti + gi_ref[_rows(i), :], qr, qi

        fr, fi, pr, pi = lax.fori_loop(0, steps, totals, (zero, zero, zero + 1.0, zero))
        e0r, e0i = _segment_carries(pr, pi, fr, fi, False)

        def step(i, gr, gi, pr, pi, acc_r, acc_i):
            tr, ti = _cmul(ar, ai, gr, gi)
            nr, ni = tr + gr_ref[_rows(i), :], ti + gi_ref[_rows(i), :]
            gr_ref[_rows(i), :] = nr
            gi_ref[_rows(i), :] = ni
            return nr, ni, acc_r + nr * pr + ni * pi, acc_i + ni * pr - nr * pi

        def scan(k, carry):
            i = steps - 1 - k
            gr, gi, acc_r, acc_i = carry
            return step(i, gr, gi, sr_ref[_rows(i - 1), :], si_ref[_rows(i - 1), :], acc_r, acc_i)

        gr, gi, acc_r, acc_i = lax.fori_loop(0, steps - 1, scan, (e0r, e0i, zero, zero))
        row = lax.broadcasted_iota(jnp.int32, (N_SEG, cs), 0)
        last = _rows(steps - 1)
        pr = jnp.where(row == 0, 0.0, pltpu.roll(sr_ref[last, :], 1, 0))
        pi = jnp.where(row == 0, 0.0, pltpu.roll(si_ref[last, :], 1, 0))
        _, _, acc_r, acc_i = step(0, gr, gi, pr, pi, acc_r, acc_i)
        dar_ref[...] = jnp.sum(acc_r, axis=0, keepdims=True)
        dai_ref[...] = jnp.sum(acc_i, axis=0, keepdims=True)

        uv = u_ref[...]
        ub = uv.astype(BF16)
        grb = gr_ref[...].astype(BF16)
        gib = gi_ref[...].astype(BF16)
        du_ref[...] = d_ref[...] * dyv + _dot(grb, bbr_ref[...], NT) + _dot(gib, bbi_ref[...], NT)
        dbbr_ref[...] = _dot(ub, grb, TN)
        dbbi_ref[...] = _dot(ub, gib, TN)
        dccr_ref[...] = _dot(sr_ref[...].astype(BF16), dyb, TN)
        dcci_ref[...] = -_dot(si_ref[...].astype(BF16), dyb, TN)
        dd_ref[...] = jnp.sum(dyv * uv, axis=0, keepdims=True)

    chunk3 = lambda r, c: pl.BlockSpec((None, r, c), lambda j: (j, 0, 0))
    cols = lambda w: pl.BlockSpec((t, w), lambda j: (0, j))
    return pl.pallas_call(
        body, name="s5_bwd", grid=(nch,),
        in_specs=[cols(SSM_CHUNK), cols(SSM_CHUNK), cols(cs), cols(cs),
                  chunk3(SSM_CHUNK, cs), chunk3(SSM_CHUNK, cs), chunk3(1, cs), chunk3(1, cs),
                  chunk3(cs, SSM_CHUNK), chunk3(cs, SSM_CHUNK), chunk3(1, SSM_CHUNK)],
        out_specs=[cols(SSM_CHUNK), chunk3(SSM_CHUNK, cs), chunk3(SSM_CHUNK, cs), chunk3(1, cs),
                   chunk3(1, cs), chunk3(cs, SSM_CHUNK), chunk3(cs, SSM_CHUNK), chunk3(1, SSM_CHUNK)],
        out_shape=[jax.ShapeDtypeStruct((t, ds), F32),
                   jax.ShapeDtypeStruct((nch, SSM_CHUNK, cs), F32), jax.ShapeDtypeStruct((nch, SSM_CHUNK, cs), F32),
                   jax.ShapeDtypeStruct((nch, 1, cs), F32), jax.ShapeDtypeStruct((nch, 1, cs), F32),
                   jax.ShapeDtypeStruct((nch, cs, SSM_CHUNK), F32), jax.ShapeDtypeStruct((nch, cs, SSM_CHUNK), F32),
                   jax.ShapeDtypeStruct((nch, 1, SSM_CHUNK), F32)],
        scratch_shapes=[pltpu.VMEM((t, cs), F32), pltpu.VMEM((t, cs), F32)],
        compiler_params=_params(),
    )(dy, u, s_re, s_im, bb_re, bb_im, a_re, a_im, cc_re, cc_im, d_skip)


def _discretize(lr, li, log_dt, br, bi):
    dt = jnp.exp(log_dt)
    mag = jnp.exp(lr * dt)
    ang = li * dt
    a_re = mag * jnp.cos(ang)
    a_im = mag * jnp.sin(ang)
    den = lr * lr + li * li
    nr = a_re - 1.0
    f_re = (nr * lr + a_im * li) / den
    f_im = (a_im * lr - nr * li) / den
    return a_re, a_im, f_re * br - f_im * bi, f_re * bi + f_im * br


def _whole(shape):
    return pl.BlockSpec(shape, lambda: (0,) * len(shape))


def _disc_fwd(lr, li, log_dt, br, bi):
    def body(lr_ref, li_ref, dt_ref, br_ref, bi_ref, ar_ref, ai_ref, bbr_ref, bbi_ref):
        outs = _discretize(lr_ref[...], li_ref[...], dt_ref[...], br_ref[...], bi_ref[...])
        for ref, val in zip((ar_ref, ai_ref, bbr_ref, bbi_ref), outs):
            ref[...] = val

    args = (lr, li, log_dt, br, bi)
    outs = (lr, lr, br, br)
    return pl.pallas_call(
        body, name="disc_fwd", in_specs=[_whole(a.shape) for a in args],
        out_specs=[_whole(a.shape) for a in outs],
        out_shape=[jax.ShapeDtypeStruct(a.shape, F32) for a in outs],
    )(*args)


def _disc_bwd(lr, li, log_dt, br, bi, dar, dai, dbbr, dbbi):
    def body(lr_ref, li_ref, dt_ref, br_ref, bi_ref, dar_ref, dai_ref, dbbr_ref, dbbi_ref,
             dlr_ref, dli_ref, ddt_ref, dbr_ref, dbi_ref):
        _, vjp = jax.vjp(_discretize, lr_ref[...], li_ref[...], dt_ref[...], br_ref[...], bi_ref[...])
        grads = vjp((dar_ref[...], dai_ref[...], dbbr_ref[...], dbbi_ref[...]))
        for ref, val in zip((dlr_ref, dli_ref, ddt_ref, dbr_ref, dbi_ref), grads):
            ref[...] = val

    args = (lr, li, log_dt, br, bi, dar, dai, dbbr, dbbi)
    outs = (lr, li, log_dt, br, bi)
    return pl.pallas_call(
        body, name="disc_bwd", in_specs=[_whole(a.shape) for a in args],
        out_specs=[_whole(a.shape) for a in outs],
        out_shape=[jax.ShapeDtypeStruct(a.shape, F32) for a in outs],
    )(*args)


def _adamw(w, g, m, v):
    m = ADAM_B1 * m + (1.0 - ADAM_B1) * g
    v = ADAM_B2 * v + (1.0 - ADAM_B2) * (g * g)
    m_hat = m / ADAM_BC1
    v_hat = v / ADAM_BC2
    delta = -ADAM_LR * (m_hat / (jnp.sqrt(v_hat) + ADAM_EPS) + ADAM_WD * w)
    return delta, m, v


def _adamw_reduce(name, parts, w, m, v):
    _, r, c = parts.shape
    tr = r
    for cand in (256, 176, 128):
        if r % cand == 0:
            tr = cand
            break

    def body(p_ref, w_ref, m_ref, v_ref, g_ref, d_ref, nm_ref, nv_ref):
        g = p_ref[0].astype(F32)
        for k in range(1, N_DEV):
            g = g + p_ref[k].astype(F32)
        delta, nm, nv = _adamw(w_ref[...], g, m_ref[...], v_ref[...])
        g_ref[...] = g
        d_ref[...] = delta
        nm_ref[...] = nm
        nv_ref[...] = nv

    blk = pl.BlockSpec((tr, c), lambda i: (i, 0))
    return pl.pallas_call(
        body, name=name, grid=(r // tr,),
        in_specs=[pl.BlockSpec((N_DEV, tr, c), lambda i: (0, i, 0)), blk, blk, blk],
        out_specs=[blk] * 4, out_shape=[jax.ShapeDtypeStruct((r, c), F32)] * 4,
        compiler_params=_params(),
    )(parts, w, m, v)


def _pack(arrays, rows):
    flat = jnp.concatenate([a.reshape(-1) for a in arrays])
    return jnp.pad(flat, (0, rows * LANES - flat.shape[0])).reshape(rows, LANES)


def _unpack(packed, shapes):
    flat = packed.reshape(-1)
    out, off = [], 0
    for s in shapes:
        n = math.prod(s)
        out.append(flat[off:off + n].reshape(s))
        off += n
    return out


def _packed_rows(shapes):
    n = sum(math.prod(s) for s in shapes)
    return -(-n // (SUBLANES * LANES)) * SUBLANES


def _block_diag(x):
    j, g, r, c = x.shape
    eye = jnp.eye(g, dtype=x.dtype)
    return (x[:, :, :, None, :] * eye[None, :, None, :, None]).reshape(j, g * r, g * c)


def _diag_blocks(x, g):
    j, gr, gc = x.shape
    r, c = gr // g, gc // g
    eye = jnp.eye(g, dtype=x.dtype)
    return (x.reshape(j, g, r, g, c) * eye[None, :, None, :, None]).sum(axis=3)


def kernel(x, meta_tokens, norm_mix_g, w_in, conv_w, ssm_lam_re, ssm_lam_im, ssm_log_dt, ssm_b_re, ssm_b_im, ssm_c_re, ssm_c_im, ssm_d, ssm_w_glu, gain_conv_out, gain_ssm_out, w_out, norm_ffn_g, w_up, ffn_conv_w, ffn_conv_b, w_down, norm_final_g, loss_target, m_meta_tokens, m_norm_mix_g, m_w_in, m_conv_w, m_ssm_lam_re, m_ssm_lam_im, m_ssm_log_dt, m_ssm_b_re, m_ssm_b_im, m_ssm_c_re, m_ssm_c_im, m_ssm_d, m_ssm_w_glu, m_gain_conv_out, m_gain_ssm_out, m_w_out, m_norm_ffn_g, m_w_up, m_ffn_conv_w, m_ffn_conv_b, m_w_down, m_norm_final_g, v_meta_tokens, v_norm_mix_g, v_w_in, v_conv_w, v_ssm_lam_re, v_ssm_lam_im, v_ssm_log_dt, v_ssm_b_re, v_ssm_b_im, v_ssm_c_re, v_ssm_c_im, v_ssm_d, v_ssm_w_glu, v_gain_conv_out, v_gain_ssm_out, v_w_out, v_norm_ffn_g, v_w_up, v_ffn_conv_w, v_ffn_conv_b, v_w_down, v_norm_final_g):
    weights = dict(meta_tokens=meta_tokens, norm_mix_g=norm_mix_g, w_in=w_in, conv_w=conv_w, ssm_lam_re=ssm_lam_re, ssm_lam_im=ssm_lam_im, ssm_log_dt=ssm_log_dt, ssm_b_re=ssm_b_re, ssm_b_im=ssm_b_im, ssm_c_re=ssm_c_re, ssm_c_im=ssm_c_im, ssm_d=ssm_d, ssm_w_glu=ssm_w_glu, gain_conv_out=gain_conv_out, gain_ssm_out=gain_ssm_out, w_out=w_out, norm_ffn_g=norm_ffn_g, w_up=w_up, ffn_conv_w=ffn_conv_w, ffn_conv_b=ffn_conv_b, w_down=w_down, norm_final_g=norm_final_g)
    mom_m = dict(meta_tokens=m_meta_tokens, norm_mix_g=m_norm_mix_g, w_in=m_w_in, conv_w=m_conv_w, ssm_lam_re=m_ssm_lam_re, ssm_lam_im=m_ssm_lam_im, ssm_log_dt=m_ssm_log_dt, ssm_b_re=m_ssm_b_re, ssm_b_im=m_ssm_b_im, ssm_c_re=m_ssm_c_re, ssm_c_im=m_ssm_c_im, ssm_d=m_ssm_d, ssm_w_glu=m_ssm_w_glu, gain_conv_out=m_gain_conv_out, gain_ssm_out=m_gain_ssm_out, w_out=m_w_out, norm_ffn_g=m_norm_ffn_g, w_up=m_w_up, ffn_conv_w=m_ffn_conv_w, ffn_conv_b=m_ffn_conv_b, w_down=m_w_down, norm_final_g=m_norm_final_g)
    mom_v = dict(meta_tokens=v_meta_tokens, norm_mix_g=v_norm_mix_g, w_in=v_w_in, conv_w=v_conv_w, ssm_lam_re=v_ssm_lam_re, ssm_lam_im=v_ssm_lam_im, ssm_log_dt=v_ssm_log_dt, ssm_b_re=v_ssm_b_re, ssm_b_im=v_ssm_b_im, ssm_c_re=v_ssm_c_re, ssm_c_im=v_ssm_c_im, ssm_d=v_ssm_d, ssm_w_glu=v_ssm_w_glu, gain_conv_out=v_gain_conv_out, gain_ssm_out=v_gain_ssm_out, w_out=v_w_out, norm_ffn_g=v_norm_ffn_g, w_up=v_w_up, ffn_conv_w=v_ffn_conv_w, ffn_conv_b=v_ffn_conv_b, w_down=v_w_down, norm_final_g=v_norm_final_g)
    names = list(weights)

    n_meta, d_meta = meta_tokens.shape
    seq, d = x.shape[1], x.shape[2]
    rows_used = n_meta + seq
    t = -(-rows_used // ROW_TILE) * ROW_TILE
    d_in_s = w_in.shape[2]
    dc_s = conv_w.shape[2]
    dc = dc_s * N_DEV
    ds = ssm_w_glu.shape[2]
    n_groups, n_state, grp = ssm_b_re.shape[1:]
    ns = n_groups * n_state
    nch = ds // SSM_CHUNK
    gpc = n_groups // nch
    ff_s = w_up.shape[2]
    dn_s = w_down.shape[1]
    assert 3 * dc + ds == d_in_s * N_DEV and 2 * dn_s == ff_s and t % (N_SEG * SUBLANES) == 0

    small_shard = jnp.concatenate([meta_tokens.reshape(-1), conv_w.reshape(-1), ffn_conv_w.reshape(-1)])
    n_small = small_shard.shape[0]
    small_rows = -(-n_small // LANES)
    small_shard = jnp.pad(small_shard, (0, small_rows * LANES - n_small)).reshape(small_rows, LANES)
    g_in, g_up, g_out, g_down, g_glu, g_small = _allgather("allgather_weights", [
        w_in[0].astype(BF16), w_up[0].astype(BF16), w_out[0].astype(BF16), w_down[0].astype(BF16),
        ssm_w_glu[0].astype(BF16), small_shard])
    w_out_f = g_out.reshape(-1, d)
    w_down_f = g_down.reshape(N_DEV // 2, 2 * dn_s, d)
    w_glu_f = g_glu.reshape(ds, ds)
    g_small = g_small.reshape(N_DEV, -1)
    o1 = n_meta * d_meta
    o2 = o1 + 3 * dc_s
    meta_full = g_small[:, :o1].reshape(N_DEV, n_meta, d_meta).transpose(1, 0, 2).reshape(n_meta, d)
    conv_w_f = g_small[:, o1:o2].reshape(N_DEV, 3, dc_s).transpose(1, 0, 2).reshape(3, dc)
    fw = g_small[:, o2:o2 + 3 * ff_s].reshape(N_DEV, 3, ff_s)
    fb = ffn_conv_b.reshape(N_DEV, 1, ff_s)

    col = lambda a: a.reshape(ns, 1)
    lr, li = col(ssm_lam_re), col(ssm_lam_im)
    log_dt_e = jnp.broadcast_to(ssm_log_dt.reshape(n_groups, 1), (n_groups, n_state)).reshape(ns, 1)
    br, bi = ssm_b_re.reshape(ns, grp), ssm_b_im.reshape(ns, grp)
    a_re, a_im, bb_re, bb_im = _disc_fwd(lr, li, log_dt_e, br, bi)
    cs = gpc * n_state
    chunk_row = lambda a: a.reshape(nch, 1, cs)
    to_bb = lambda a: _block_diag(a.reshape(nch, gpc, n_state, grp).transpose(0, 1, 3, 2)).astype(BF16)
    to_cc = lambda a: _block_diag(a.reshape(nch, gpc, grp, n_state).transpose(0, 1, 3, 2)).astype(BF16)
    bbm_re, bbm_im = to_bb(bb_re), to_bb(bb_im)
    ccm_re, ccm_im = to_cc(ssm_c_re), to_cc(ssm_c_im)
    a_re_c, a_im_c = chunk_row(a_re), chunk_row(a_im)
    d_skip = ssm_d.reshape(nch, 1, SSM_CHUNK)

    h0 = jnp.concatenate([meta_full, x[0], jnp.zeros((t - rows_used, d), F32)], axis=0)
    target = jnp.pad(loss_target[0], ((n_meta, t - rows_used), (0, 0)))
    full_t = lambda w: pl.BlockSpec((t, w), lambda *_: (0, 0))

    hn1 = _rms_fwd("norm_mix", h0, norm_mix_g)
    proj = _mm("proj", hn1, g_in, dims=NN, grid=(N_DEV,), a_spec=full_t(d),
               b_spec=pl.BlockSpec((None, d, d_in_s), lambda j: (j, 0, 0)),
               o_spec=pl.BlockSpec((t, d_in_s), lambda j: (0, j)),
               out_shape=jax.ShapeDtypeStruct((t, N_DEV * d_in_s), F32))
    u_seg = _to_segments(proj[:, 3 * dc:])
    s_re, s_im, y_seg = _s5_fwd(u_seg, bbm_re, bbm_im, a_re_c, a_im_c, ccm_re, ccm_im, d_skip)
    y_ssm = _from_segments(y_seg)
    mixed, z_glu = _mix_fwd(proj, y_ssm, w_glu_f, conv_w_f, gain_conv_out, gain_ssm_out)
    tn_out = 256
    h1 = _mm("out_proj", mixed, w_out_f, dims=NN, grid=(d // tn_out,), a_spec=full_t(dc + ds),
             b_spec=pl.BlockSpec((dc + ds, tn_out), lambda j: (0, j)),
             o_spec=pl.BlockSpec((t, tn_out), lambda j: (0, j)),
             out_shape=jax.ShapeDtypeStruct((t, d), F32),
             res=h0, res_spec=pl.BlockSpec((t, tn_out), lambda j: (0, j)))
    hn2 = _rms_fwd("norm_ffn", h1, norm_ffn_g)
    up_pre = _mm("up_proj", hn2, g_up, dims=NN, grid=(N_DEV,), a_spec=full_t(d),
                 b_spec=pl.BlockSpec((None, d, ff_s), lambda j: (j, 0, 0)),
                 o_spec=pl.BlockSpec((None, t, ff_s), lambda j: (j, 0, 0)),
                 out_shape=jax.ShapeDtypeStruct((N_DEV, t, ff_s), F32))
    act = _ffn_fwd(up_pre, fw, fb)
    nhalf = N_DEV // 2
    h2 = _mm("down_proj", act, w_down_f, dims=NN, grid=(d // tn_out, nhalf),
             a_spec=pl.BlockSpec((None, t, ff_s), lambda j, k: (k, 0, 0)),
             b_spec=pl.BlockSpec((None, ff_s, tn_out), lambda j, k: (k, 0, j)),
             o_spec=pl.BlockSpec((t, tn_out), lambda j, k: (0, j)),
             out_shape=jax.ShapeDtypeStruct((t, d), F32), acc_shape=(t, tn_out),
             res=h1, res_spec=pl.BlockSpec((t, tn_out), lambda j, k: (0, j)))

    dh2, dh2_b, loss_part, d_norm_final = _loss_bwd(h2, target, norm_final_g.reshape(1, d), n_meta, rows_used)
    dact = _mm("down_dgrad", dh2_b, w_down_f, dims=NT, grid=(nhalf,), a_spec=full_t(d),
               b_spec=pl.BlockSpec((None, ff_s, d), lambda j: (j, 0, 0)),
               o_spec=pl.BlockSpec((None, t, ff_s), lambda j: (j, 0, 0)),
               out_shape=jax.ShapeDtypeStruct((nhalf, t, ff_s), F32))
    tn_w = 512
    dw_down = _mm("down_wgrad", act, dh2_b, dims=TN, grid=(nhalf, d // tn_w),
                  a_spec=pl.BlockSpec((None, t, ff_s), lambda j, n: (j, 0, 0)),
                  b_spec=pl.BlockSpec((t, tn_w), lambda j, n: (0, n)),
                  o_spec=pl.BlockSpec((None, ff_s, tn_w), lambda j, n: (j, 0, n)),
                  out_shape=jax.ShapeDtypeStruct((nhalf, ff_s, d), BF16))
    dup, d_fw, d_fb = _ffn_bwd1(up_pre, dact, fw, fb)
    dup_pre = _ffn_bwd2(dup, fw)
    dhn2 = _mm("up_dgrad", dup_pre, g_up, dims=NT, grid=(d // tn_w, N_DEV),
               a_spec=pl.BlockSpec((None, t, ff_s), lambda i, k: (k, 0, 0)),
               b_spec=pl.BlockSpec((None, tn_w, ff_s), lambda i, k: (k, i, 0)),
               o_spec=pl.BlockSpec((t, tn_w), lambda i, k: (0, i)),
               out_shape=jax.ShapeDtypeStruct((t, d), F32), acc_shape=(t, tn_w))
    dw_up = _mm("up_wgrad", hn2, dup_pre, dims=TN, grid=(N_DEV, d // tn_w),
                a_spec=pl.BlockSpec((t, tn_w), lambda j, i: (0, i)),
                b_spec=pl.BlockSpec((None, t, ff_s), lambda j, i: (j, 0, 0)),
                o_spec=pl.BlockSpec((None, tn_w, ff_s), lambda j, i: (j, i, 0)),
                out_shape=jax.ShapeDtypeStruct((N_DEV, d, ff_s), BF16))
    dh1, dh1_b, d_norm_ffn = _rms_bwd_res("norm_ffn_bwd", dh2, dhn2, h1, norm_ffn_g)
    dmixed = _mm("out_dgrad", dh1_b, w_out_f, dims=NT, grid=((dc + ds) // tn_out,), a_spec=full_t(d),
                 b_spec=pl.BlockSpec((tn_out, d), lambda i: (i, 0)),
                 o_spec=pl.BlockSpec((t, tn_out), lambda i: (0, i)),
                 out_shape=jax.ShapeDtypeStruct((t, dc + ds), F32))
    dw_out = _mm("out_wgrad", mixed, dh1_b, dims=TN, grid=((dc + ds) // tn_out,),
                 a_spec=pl.BlockSpec((t, tn_out), lambda i: (0, i)), b_spec=full_t(d),
                 o_spec=pl.BlockSpec((tn_out, d), lambda i: (i, 0)),
                 out_shape=jax.ShapeDtypeStruct((dc + ds, d), BF16))
    db_gate, dconv, dy_ssm, d_wglu, d_conv_w, d_gain_c, d_gain_s = _mix_bwd1(
        proj, y_ssm, z_glu, dmixed, w_glu_f, conv_w_f, gain_conv_out, gain_ssm_out)
    dcdv = _mix_bwd2(proj, dconv, conv_w_f)
    (du_seg, d_bbm_re, d_bbm_im, d_a_re, d_a_im, d_ccm_re, d_ccm_im, d_dskip) = _s5_bwd(
        _to_segments(dy_ssm), u_seg, s_re, s_im, bbm_re, bbm_im, a_re_c, a_im_c, ccm_re, ccm_im, d_skip)
    dproj = jnp.concatenate([db_gate, dcdv, _from_segments(du_seg).astype(BF16)], axis=1)
    dhn1 = _mm("proj_dgrad", dproj, g_in, dims=NT, grid=(d // tn_w, N_DEV),
               a_spec=pl.BlockSpec((t, d_in_s), lambda i, k: (0, k)),
               b_spec=pl.BlockSpec((None, tn_w, d_in_s), lambda i, k: (k, i, 0)),
               o_spec=pl.BlockSpec((t, tn_w), lambda i, k: (0, i)),
               out_shape=jax.ShapeDtypeStruct((t, d), F32), acc_shape=(t, tn_w))
    dw_in = _mm("proj_wgrad", hn1, dproj, dims=TN, grid=(N_DEV, d // tn_w),
                a_spec=pl.BlockSpec((t, tn_w), lambda j, i: (0, i)),
                b_spec=pl.BlockSpec((t, d_in_s), lambda j, i: (0, j)),
                o_spec=pl.BlockSpec((None, tn_w, d_in_s), lambda j, i: (j, i, 0)),
                out_shape=jax.ShapeDtypeStruct((N_DEV, d, d_in_s), BF16))
    dh0, _, d_norm_mix = _rms_bwd_res("norm_mix_bwd", dh1, dhn1, h0, norm_mix_g)
    grad_x = dh0[n_meta:rows_used][None]
    loss = lax.psum(loss_part[0, 0], ("x", "y", "c"))

    from_bb = lambda a: _diag_blocks(a, gpc).transpose(0, 1, 3, 2).reshape(ns, grp)
    from_cc = lambda a: _diag_blocks(a, gpc).transpose(0, 1, 3, 2).reshape(1, n_groups, grp, n_state)
    d_lr, d_li, d_dt_e, d_br, d_bi = _disc_bwd(
        lr, li, log_dt_e, br, bi, d_a_re.reshape(ns, 1), d_a_im.reshape(ns, 1),
        from_bb(d_bbm_re), from_bb(d_bbm_im))

    rep_grads = dict(
        norm_mix_g=d_norm_mix, ssm_lam_re=d_lr.reshape(ssm_lam_re.shape), ssm_lam_im=d_li.reshape(ssm_lam_im.shape),
        ssm_log_dt=d_dt_e.reshape(n_groups, n_state).sum(axis=1).reshape(ssm_log_dt.shape),
        ssm_b_re=d_br.reshape(ssm_b_re.shape), ssm_b_im=d_bi.reshape(ssm_b_im.shape),
        ssm_c_re=from_cc(d_ccm_re), ssm_c_im=from_cc(d_ccm_im), ssm_d=d_dskip.reshape(ssm_d.shape),
        gain_conv_out=d_gain_c, gain_ssm_out=d_gain_s, norm_ffn_g=d_norm_ffn,
        ffn_conv_b=d_fb.reshape(ffn_conv_b.shape), norm_final_g=d_norm_final.reshape(norm_final_g.shape))
    rep_names = list(rep_grads)
    rep_shapes = [weights[n].shape for n in rep_names]
    rep_rows = _packed_rows(rep_shapes)
    (rep_parts,) = _allgather("allgather_small_grads", [_pack([rep_grads[n] for n in rep_names], rep_rows)])
    rep_out = _adamw_reduce("adamw_replicated", rep_parts,
                            _pack([weights[n] for n in rep_names], rep_rows),
                            _pack([mom_m[n] for n in rep_names], rep_rows),
                            _pack([mom_v[n] for n in rep_names], rep_rows))
    rep_out = [_unpack(o, rep_shapes) for o in rep_out]

    d_meta = dh0[:n_meta].reshape(n_meta, N_DEV, d_meta).transpose(1, 0, 2)
    d_conv_w_b = d_conv_w.reshape(3, N_DEV, dc_s).transpose(1, 0, 2)
    small_blocks = jnp.concatenate([d_meta.reshape(N_DEV, -1), d_conv_w_b.reshape(N_DEV, -1),
                                    d_fw.reshape(N_DEV, -1)], axis=1)
    small_blocks = jnp.pad(small_blocks, ((0, 0), (0, small_rows * LANES - n_small)))
    small_blocks = small_blocks.reshape(N_DEV, small_rows, LANES)
    p_in, p_up, p_out, p_down, p_glu, p_small = _exchange("exchange_grads", [
        dw_in, dw_up, dw_out.reshape(N_DEV, -1, d), dw_down.reshape(N_DEV, dn_s, d),
        d_wglu.astype(BF16).reshape(N_DEV, -1, ds), small_blocks])
    shard_out = {}
    for n, parts in (("w_in", p_in), ("w_up", p_up), ("w_out", p_out), ("w_down", p_down),
                     ("ssm_w_glu", p_glu)):
        sh = weights[n].shape
        res = _adamw_reduce("adamw_" + n, parts, weights[n][0], mom_m[n][0], mom_v[n][0])
        shard_out[n] = [r.reshape(sh) for r in res]
    small_names = ["meta_tokens", "conv_w", "ffn_conv_w"]
    small_shapes = [weights[n].shape for n in small_names]
    pack_small = lambda tree: jnp.pad(jnp.concatenate([tree[n].reshape(-1) for n in small_names]),
                                      (0, small_rows * LANES - n_small)).reshape(small_rows, LANES)
    res = _adamw_reduce("adamw_small_shards", p_small, pack_small(weights), pack_small(mom_m), pack_small(mom_v))
    res = [_unpack(r, small_shapes) for r in res]
    for i, n in enumerate(small_names):
        shard_out[n] = [r[i] for r in res]
    for i, n in enumerate(rep_names):
        shard_out[n] = [r[i] for r in rep_out]

    grads = [shard_out[n][0] for n in names]
    deltas = [shard_out[n][1] for n in names]
    new_m = [shard_out[n][2] for n in names]
    new_v = [shard_out[n][3] for n in names]
    return (loss, grad_x, *grads, *deltas, *new_m, *new_v)
```

```python
import functools
import math

import jax
import jax.numpy as jnp
from jax import lax
from jax.experimental import pallas as pl
from jax.experimental.pallas import tpu as pltpu

F32 = jnp.float32
BF16 = jnp.bfloat16
MESH = pl.DeviceIdType.MESH

N_DEV = 8
RMS_EPS = 1e-6
ADAM_LR = 0.001
ADAM_B1 = 0.9
ADAM_B2 = 0.999
ADAM_EPS = 1e-08
ADAM_WD = 0.01
ADAM_STEP = 10
ADAM_BC1 = 1.0 - ADAM_B1 ** ADAM_STEP
ADAM_BC2 = 1.0 - ADAM_B2 ** ADAM_STEP

SUBLANES = 8
LANES = 128
ROW_TILE = 128
N_SEG = 8
SSM_CHUNK = 128
VMEM_LIMIT = 48 * 1024 * 1024

NN = ((1,), (0,))
NT = ((1,), (1,))
TN = ((0,), (0,))


def _params(**kw):
    return pltpu.CompilerParams(vmem_limit_bytes=VMEM_LIMIT, **kw)


def _dot(a, b, dims):
    return lax.dot_general(a, b, (dims, ((), ())), preferred_element_type=F32)


def _mean_sq_rsqrt(x):
    return lax.rsqrt(jnp.mean(x * x, axis=-1, keepdims=True) + RMS_EPS)


def _rms_bwd(x, r, g, dy):
    xhat = x * r
    dxh = dy * g
    dx = r * (dxh - xhat * jnp.mean(dxh * xhat, axis=-1, keepdims=True))
    return dx, dy * xhat


def _gelu(y):
    c = math.sqrt(2.0 / math.pi)
    t = jnp.tanh(c * (y + 0.044715 * y * y * y))
    return 0.5 * y * (1.0 + t), t


def _gelu_grad(y, t):
    c = math.sqrt(2.0 / math.pi)
    return 0.5 * (1.0 + t) + 0.5 * y * (1.0 - t * t) * c * (1.0 + 3.0 * 0.044715 * y * y)


def _shift_down(ext, k, rows):
    return pltpu.roll(ext, k, 0)[SUBLANES:SUBLANES + rows]


def _shift_up(ext, k, rows):
    n = ext.shape[0]
    return pltpu.roll(ext, n - k, 0)[:rows]


def _dev_index(p):
    return 4 * p[0] + 2 * p[1] + p[2]


def _allgather(name, shards):
    n = len(shards)

    def body(*refs):
        ins, outs = refs[:n], refs[n:2 * n]
        send_sems, recv_sems, local_sems = refs[2 * n:]
        x, y, c = lax.axis_index("x"), lax.axis_index("y"), lax.axis_index("c")
        me, sibling = (x, y, c), (x, y, 1 - c)
        chips = [(1 - x, y), (x, 1 - y), (1 - x, 1 - y)]

        def copy(a, k, block, to, src=None):
            dst = outs[a].at[_dev_index(block)]
            return pltpu.make_async_remote_copy(
                src_ref=dst if src is None else src, dst_ref=dst,
                send_sem=send_sems.at[a, k], recv_sem=recv_sems.at[a, k],
                device_id=to, device_id_type=MESH)

        mine = [pltpu.make_async_copy(ins[a], outs[a].at[_dev_index(me)], local_sems.at[a])
                for a in range(n)]
        for cp in mine:
            cp.start()
        first = []
        for a in range(n):
            first.append(copy(a, 0, me, sibling, src=ins[a]))
            for j, chip in enumerate(chips):
                first.append(copy(a, 1 + j, me, (*chip, c), src=ins[a]))
        for cp in first:
            cp.start()
        passed = []
        for j, chip in enumerate(chips):
            for a in range(n):
                copy(a, 1 + j, (*chip, c), me).wait_recv()
                fwd = copy(a, 4 + j, (*chip, c), sibling)
                fwd.start()
                passed.append(fwd)
        for a in range(n):
            copy(a, 0, sibling, me).wait_recv()
            for j, chip in enumerate(chips):
                copy(a, 4 + j, (*chip, 1 - c), me).wait_recv()
        for cp in first + passed:
            cp.wait_send()
        for cp in mine:
            cp.wait()

    any_spec = pl.BlockSpec(memory_space=pl.ANY)
    return pl.pallas_call(
        body, name=name,
        out_shape=[jax.ShapeDtypeStruct((N_DEV,) + s.shape, s.dtype) for s in shards],
        in_specs=[any_spec] * n, out_specs=[any_spec] * n,
        scratch_shapes=[pltpu.SemaphoreType.DMA((n, 7)), pltpu.SemaphoreType.DMA((n, 7)),
                        pltpu.SemaphoreType.DMA((n,))],
    )(*shards)


def _exchange(name, blocks):
    n = len(blocks)

    def body(*refs):
        ins, outs = refs[:n], refs[n:2 * n]
        send_sems, recv_sems, local_sems = refs[2 * n:]
        x, y, c = lax.axis_index("x"), lax.axis_index("y"), lax.axis_index("c")
        me = (x, y, c)
        copies = []
        for a in range(n):
            cp = pltpu.make_async_copy(ins[a].at[_dev_index(me)], outs[a].at[_dev_index(me)],
                                       local_sems.at[a])
            cp.start()
            copies.append(cp)
        for a in range(n):
            for k in range(1, N_DEV):
                peer = tuple((1 - v) if (k >> s) & 1 else v for v, s in zip(me, (2, 1, 0)))
                cp = pltpu.make_async_remote_copy(
                    src_ref=ins[a].at[_dev_index(peer)], dst_ref=outs[a].at[_dev_index(me)],
                    send_sem=send_sems.at[a, k - 1], recv_sem=recv_sems.at[a, k - 1],
                    device_id=peer, device_id_type=MESH)
                cp.start()
                copies.append(cp)
        for cp in copies:
            cp.wait()

    any_spec = pl.BlockSpec(memory_space=pl.ANY)
    return pl.pallas_call(
        body, name=name,
        out_shape=[jax.ShapeDtypeStruct(b.shape, b.dtype) for b in blocks],
        in_specs=[any_spec] * n, out_specs=[any_spec] * n,
        scratch_shapes=[pltpu.SemaphoreType.DMA((n, 7)), pltpu.SemaphoreType.DMA((n, 7)),
                        pltpu.SemaphoreType.DMA((n,))],
    )(*blocks)


HBM_SPEC = pl.BlockSpec(memory_space=pltpu.HBM)
SEM_SPEC = pl.BlockSpec(memory_space=pltpu.SEMAPHORE)
ANY_SPEC = pl.BlockSpec(memory_space=pl.ANY)
DATAFLOW = pltpu.SideEffectType.DATAFLOW_SIDE_EFFECTING


def _my_position():
    x, y, c = lax.axis_index("x"), lax.axis_index("y"), lax.axis_index("c")
    return (x, y, c)


def _peer(me, k):
    return tuple((1 - v) if (k >> s) & 1 else v for v, s in zip(me, (2, 1, 0)))


def _split_copies(src_refs, land_refs, send_sems, recv_sems, gather):
    me = _my_position()
    copies = []
    for a, (src, land) in enumerate(zip(src_refs, land_refs)):
        for k in range(1, N_DEV):
            peer = _peer(me, k)
            copies.append(pltpu.make_async_remote_copy(
                src_ref=src if gather else src.at[_dev_index(peer)], dst_ref=land.at[_dev_index(me)],
                send_sem=send_sems[a].at[k - 1], recv_sem=recv_sems[a].at[k - 1],
                device_id=peer, device_id_type=MESH))
    return copies


def _own_slot(block, like_shape):
    me = _dev_index(_my_position())
    return lax.dynamic_update_index_in_dim(lax.empty(like_shape, block.dtype), block, me, 0)


def _send_start(name, srcs, gather):
    n = len(srcs)
    me = _dev_index(_my_position())
    if gather:
        lands = [_own_slot(s, (N_DEV,) + s.shape) for s in srcs]
    else:
        lands = [_own_slot(lax.dynamic_index_in_dim(s, me, 0, keepdims=False), s.shape) for s in srcs]

    def body(*refs):
        src_refs, land_refs = refs[:n], refs[n:2 * n]
        send_sems, recv_sems = refs[2 * n:3 * n], refs[3 * n:4 * n]
        token = refs[-1]
        for cp in _split_copies(src_refs, land_refs, send_sems, recv_sems, gather):
            cp.start()
        token[...] = jnp.zeros_like(token)

    hbm = lambda a: pltpu.HBM(a.shape, a.dtype)
    sems = [pltpu.SemaphoreType.DMA((N_DEV - 1,))] * n
    outs = pl.pallas_call(
        body, name=name,
        out_shape=(*sems, *sems, *[hbm(s) for s in srcs], *[hbm(l) for l in lands],
                   jax.ShapeDtypeStruct((SUBLANES, LANES), F32)),
        in_specs=[HBM_SPEC] * (2 * n),
        out_specs=(*[SEM_SPEC] * (2 * n), *[HBM_SPEC] * (2 * n), pl.BlockSpec(memory_space=pltpu.VMEM)),
        input_output_aliases={i: 2 * n + i for i in range(2 * n)},
        compiler_params=pltpu.CompilerParams(has_side_effects=DATAFLOW),
    )(*[pltpu.with_memory_space_constraint(a, pltpu.HBM) for a in (*srcs, *lands)])
    state = dict(send=outs[:n], recv=outs[n:2 * n], srcs=outs[2 * n:3 * n], lands=outs[3 * n:4 * n],
                 gather=gather)
    return state, outs[-1]


def _send_wait(name, state, which, after):
    n = len(which)
    pick = lambda key: [state[key][i] for i in which]
    gather = state["gather"]

    def body(*refs):
        src_refs, land_refs = refs[:n], refs[n:2 * n]
        send_sems, recv_sems = refs[2 * n:3 * n], refs[3 * n:4 * n]
        for cp in _split_copies(src_refs, land_refs, send_sems, recv_sems, gather):
            cp.wait_send()
            cp.wait_recv()

    srcs, lands = pick("srcs"), pick("lands")
    hbm = lambda a: pltpu.HBM(a.shape, a.dtype)
    outs = pl.pallas_call(
        body, name=name,
        out_shape=(*[hbm(s) for s in srcs], *[hbm(l) for l in lands]),
        in_specs=[*[HBM_SPEC] * (2 * n), *[SEM_SPEC] * (2 * n), ANY_SPEC],
        out_specs=tuple([HBM_SPEC] * (2 * n)),
        input_output_aliases={i: i for i in range(2 * n)},
        compiler_params=pltpu.CompilerParams(has_side_effects=DATAFLOW),
    )(*srcs, *lands, *pick("send"), *pick("recv"), after)
    return outs[n:]


def _mm(name, a, b, *, dims, grid, a_spec, b_spec, o_spec, out_shape, acc_shape=None,
        res=None, res_spec=None):
    n_red = grid[-1] if acc_shape is not None else 1
    red_axis = len(grid) - 1

    def body(*refs):
        a_ref, b_ref = refs[0], refs[1]
        r_ref = refs[2] if res is not None else None
        o_ref = refs[3] if res is not None else refs[2]
        part = _dot(a_ref[...], b_ref[...], dims)
        if acc_shape is None:
            if r_ref is not None:
                part = part + r_ref[...]
            o_ref[...] = part.astype(o_ref.dtype)
            return
        acc_ref = refs[-1]
        k = pl.program_id(red_axis)

        @pl.when(k == 0)
        def _():
            acc_ref[...] = part

        @pl.when(k > 0)
        def _():
            acc_ref[...] += part

        @pl.when(k == n_red - 1)
        def _():
            total = acc_ref[...]
            if r_ref is not None:
                total = total + r_ref[...]
            o_ref[...] = total.astype(o_ref.dtype)

    ins, in_specs = [a, b], [a_spec, b_spec]
    if res is not None:
        ins.append(res)
        in_specs.append(res_spec)
    return pl.pallas_call(
        body, name=name, grid=grid, in_specs=in_specs, out_specs=o_spec, out_shape=out_shape,
        scratch_shapes=[pltpu.VMEM(acc_shape, F32)] if acc_shape is not None else [],
        compiler_params=_params(),
    )(*ins)


def _rms_fwd(name, h, g):
    t, d = h.shape

    def body(h_ref, g_ref, o_ref):
        x = h_ref[...]
        o_ref[...] = (x * _mean_sq_rsqrt(x) * g_ref[...]).astype(BF16)

    return pl.pallas_call(
        body, name=name, grid=(t // ROW_TILE,),
        in_specs=[pl.BlockSpec((ROW_TILE, d), lambda i: (i, 0)), pl.BlockSpec((1, d), lambda i: (0, 0))],
        out_specs=pl.BlockSpec((ROW_TILE, d), lambda i: (i, 0)),
        out_shape=jax.ShapeDtypeStruct((t, d), BF16), compiler_params=_params(),
    )(h, g)


def _rms_bwd_res(name, dres, dhn, h, g):
    t, d = h.shape

    def body(dres_ref, dhn_ref, h_ref, g_ref, dh_ref, dhb_ref, dg_ref):
        x = h_ref[...]
        dx, dgt = _rms_bwd(x, _mean_sq_rsqrt(x), g_ref[...], dhn_ref[...])
        dh = dres_ref[...] + dx
        dh_ref[...] = dh
        dhb_ref[...] = dh.astype(BF16)

        @pl.when(pl.program_id(0) == 0)
        def _():
            dg_ref[...] = jnp.zeros_like(dg_ref)

        dg_ref[...] += jnp.sum(dgt, axis=0, keepdims=True)

    row = pl.BlockSpec((ROW_TILE, d), lambda i: (i, 0))
    vec = pl.BlockSpec((1, d), lambda i: (0, 0))
    return pl.pallas_call(
        body, name=name, grid=(t // ROW_TILE,), in_specs=[row, row, row, vec],
        out_specs=[row, row, vec],
        out_shape=[jax.ShapeDtypeStruct((t, d), F32), jax.ShapeDtypeStruct((t, d), BF16),
                   jax.ShapeDtypeStruct((1, d), F32)],
        compiler_params=_params(),
    )(dres, dhn, h, g)


def _loss_bwd(h2, target, g, row_lo, row_hi):
    t, d = h2.shape

    def body(h_ref, tg_ref, g_ref, dh_ref, dhb_ref, loss_ref, dg_ref):
        i = pl.program_id(0)
        x = h_ref[...]
        r = _mean_sq_rsqrt(x)
        gv = g_ref[...]
        y = x * r * gv
        rows = i * ROW_TILE + lax.broadcasted_iota(jnp.int32, (ROW_TILE, 1), 0)
        valid = jnp.logical_and(rows >= row_lo, rows < row_hi)
        err = jnp.where(valid, y - tg_ref[...], 0.0)
        dy = err * (1.0 / d)
        dx, dgt = _rms_bwd(x, r, gv, dy)
        dh_ref[...] = dx
        dhb_ref[...] = dx.astype(BF16)

        @pl.when(i == 0)
        def _():
            loss_ref[...] = jnp.zeros_like(loss_ref)
            dg_ref[...] = jnp.zeros_like(dg_ref)

        row_loss = jnp.mean(err * err, axis=-1, keepdims=True)
        loss_ref[...] += 0.5 * jnp.sum(row_loss, axis=0, keepdims=True)
        dg_ref[...] += jnp.sum(dgt, axis=0, keepdims=True)

    row = pl.BlockSpec((ROW_TILE, d), lambda i: (i, 0))
    vec = pl.BlockSpec((1, d), lambda i: (0, 0))
    return pl.pallas_call(
        body, name="loss_bwd", grid=(t // ROW_TILE,), in_specs=[row, row, vec],
        out_specs=[row, row, pl.BlockSpec((1, 1), lambda i: (0, 0)), vec],
        out_shape=[jax.ShapeDtypeStruct((t, d), F32), jax.ShapeDtypeStruct((t, d), BF16),
                   jax.ShapeDtypeStruct((1, 1), F32), jax.ShapeDtypeStruct((1, d), F32)],
        compiler_params=_params(),
    )(h2, target, g)


def _prev_halo(i):
    return jnp.maximum(i * (ROW_TILE // SUBLANES) - 1, 0)


def _causal_taps(cur, halo, first):
    rows = cur.shape[0]
    halo = jnp.where(first, 0.0, halo)
    ext = jnp.concatenate([halo, cur], axis=0)
    return _shift_down(ext, 2, rows), _shift_down(ext, 1, rows)


def _anticausal_taps(cur, halo, last):
    rows = cur.shape[0]
    halo = jnp.where(last, 0.0, halo)
    ext = jnp.concatenate([cur, halo], axis=0)
    return _shift_up(ext, 1, rows), _shift_up(ext, 2, rows)


def _mix_fwd(proj, y, w_glu, conv_w, gain_c, gain_s):
    t = proj.shape[0]
    dc = conv_w.shape[1]
    ds = y.shape[1]

    def body(p_ref, halo_ref, y_ref, wg_ref, cw_ref, gc_ref, gs_ref, mixed_ref, z_ref):
        i = pl.program_id(0)
        p = p_ref[...]
        b, c, v = p[:, :dc], p[:, dc:2 * dc], p[:, 2 * dc:3 * dc]
        cv = c * v
        hp = halo_ref[...]
        x2, x1 = _causal_taps(cv, hp[:, dc:2 * dc] * hp[:, 2 * dc:3 * dc], i == 0)
        cw = cw_ref[...]
        conv = cw[0:1] * x2 + cw[1:2] * x1 + cw[2:3] * cv
        co = b * conv
        mixed_ref[:, :dc] = (co * _mean_sq_rsqrt(co) * gc_ref[...]).astype(BF16)
        g, _ = _gelu(y_ref[...])
        z = _dot(g.astype(BF16), wg_ref[...], NN)
        z_ref[...] = z
        so = g * jax.nn.sigmoid(z)
        mixed_ref[:, dc:] = (so * _mean_sq_rsqrt(so) * gs_ref[...]).astype(BF16)

    const = lambda i: (0, 0)
    return pl.pallas_call(
        body, name="mix_fwd", grid=(t // ROW_TILE,),
        in_specs=[pl.BlockSpec((ROW_TILE, 3 * dc), lambda i: (i, 0)),
                  pl.BlockSpec((SUBLANES, 3 * dc), lambda i: (_prev_halo(i), 0)),
                  pl.BlockSpec((ROW_TILE, ds), lambda i: (i, 0)),
                  pl.BlockSpec((ds, ds), const), pl.BlockSpec(conv_w.shape, const),
                  pl.BlockSpec((1, dc), const), pl.BlockSpec((1, ds), const)],
        out_specs=[pl.BlockSpec((ROW_TILE, dc + ds), lambda i: (i, 0)),
                   pl.BlockSpec((ROW_TILE, ds), lambda i: (i, 0))],
        out_shape=[jax.ShapeDtypeStruct((t, dc + ds), BF16), jax.ShapeDtypeStruct((t, ds), F32)],
        compiler_params=_params(),
    )(proj, proj, y, w_glu, conv_w, gain_c, gain_s)


def _mix_bwd1(proj, y, z, dmixed, w_glu, conv_w, gain_c, gain_s):
    t = proj.shape[0]
    dc = conv_w.shape[1]
    ds = y.shape[1]

    def body(p_ref, halo_ref, y_ref, z_ref, dm_ref, wg_ref, cw_ref, gc_ref, gs_ref,
             db_ref, dconv_ref, dy_ref, dwg_ref, dcw_ref, dgc_ref, dgs_ref):
        i = pl.program_id(0)

        @pl.when(i == 0)
        def _():
            dwg_ref[...] = jnp.zeros_like(dwg_ref)
            dcw_ref[...] = jnp.zeros_like(dcw_ref)
            dgc_ref[...] = jnp.zeros_like(dgc_ref)
            dgs_ref[...] = jnp.zeros_like(dgs_ref)

        p = p_ref[...]
        b, c, v = p[:, :dc], p[:, dc:2 * dc], p[:, 2 * dc:3 * dc]
        cv = c * v
        hp = halo_ref[...]
        x2, x1 = _causal_taps(cv, hp[:, dc:2 * dc] * hp[:, 2 * dc:3 * dc], i == 0)
        cw = cw_ref[...]
        conv = cw[0:1] * x2 + cw[1:2] * x1 + cw[2:3] * cv
        co = b * conv
        dm = dm_ref[...]
        dco, dgt = _rms_bwd(co, _mean_sq_rsqrt(co), gc_ref[...], dm[:, :dc])
        dgc_ref[...] += jnp.sum(dgt, axis=0, keepdims=True)
        db_ref[...] = (dco * conv).astype(BF16)
        dconv = dco * b
        dconv_ref[...] = dconv
        dcw_ref[0:1, :] += jnp.sum(dconv * x2, axis=0, keepdims=True)
        dcw_ref[1:2, :] += jnp.sum(dconv * x1, axis=0, keepdims=True)
        dcw_ref[2:3, :] += jnp.sum(dconv * cv, axis=0, keepdims=True)

        yv = y_ref[...]
        g, th = _gelu(yv)
        sg = jax.nn.sigmoid(z_ref[...])
        so = g * sg
        dso, dgt = _rms_bwd(so, _mean_sq_rsqrt(so), gs_ref[...], dm[:, dc:])
        dgs_ref[...] += jnp.sum(dgt, axis=0, keepdims=True)
        dz = (dso * g * sg * (1.0 - sg)).astype(BF16)
        dg = dso * sg + _dot(dz, wg_ref[...], NT)
        dwg_ref[...] += _dot(g.astype(BF16), dz, TN)
        dy_ref[...] = dg * _gelu_grad(yv, th)

    const = lambda i: (0, 0)
    row = lambda w: pl.BlockSpec((ROW_TILE, w), lambda i: (i, 0))
    return pl.pallas_call(
        body, name="mix_bwd1", grid=(t // ROW_TILE,),
        in_specs=[row(3 * dc), pl.BlockSpec((SUBLANES, 3 * dc), lambda i: (_prev_halo(i), 0)),
                  row(ds), row(ds), row(dc + ds), pl.BlockSpec((ds, ds), const),
                  pl.BlockSpec(conv_w.shape, const), pl.BlockSpec((1, dc), const),
                  pl.BlockSpec((1, ds), const)],
        out_specs=[row(dc), row(dc), row(ds), pl.BlockSpec((ds, ds), const),
                   pl.BlockSpec(conv_w.shape, const), pl.BlockSpec((1, dc), const),
                   pl.BlockSpec((1, ds), const)],
        out_shape=[jax.ShapeDtypeStruct((t, dc), BF16), jax.ShapeDtypeStruct((t, dc), F32),
                   jax.ShapeDtypeStruct((t, ds), F32), jax.ShapeDtypeStruct((ds, ds), F32),
                   jax.ShapeDtypeStruct(conv_w.shape, F32), jax.ShapeDtypeStruct((1, dc), F32),
                   jax.ShapeDtypeStruct((1, ds), F32)],
        compiler_params=_params(),
    )(proj, proj, y, z, dmixed, w_glu, conv_w, gain_c, gain_s)


def _mix_bwd2(proj, dconv, conv_w):
    t = proj.shape[0]
    dc = conv_w.shape[1]
    n_tiles = t // ROW_TILE
    per_tile = ROW_TILE // SUBLANES

    def body(c_ref, v_ref, d_ref, halo_ref, cw_ref, o_ref):
        i = pl.program_id(0)
        d = d_ref[...]
        u1, u2 = _anticausal_taps(d, halo_ref[...], i == n_tiles - 1)
        cw = cw_ref[...]
        dcv = cw[2:3] * d + cw[1:2] * u1 + cw[0:1] * u2
        o_ref[:, :dc] = (dcv * v_ref[...]).astype(BF16)
        o_ref[:, dc:] = (dcv * c_ref[...]).astype(BF16)

    return pl.pallas_call(
        body, name="mix_bwd2", grid=(n_tiles,),
        in_specs=[pl.BlockSpec((ROW_TILE, dc), lambda i: (i, 1)),
                  pl.BlockSpec((ROW_TILE, dc), lambda i: (i, 2)),
                  pl.BlockSpec((ROW_TILE, dc), lambda i: (i, 0)),
                  pl.BlockSpec((SUBLANES, dc),
                               lambda i: (jnp.minimum((i + 1) * per_tile, t // SUBLANES - 1), 0)),
                  pl.BlockSpec(conv_w.shape, lambda i: (0, 0))],
        out_specs=pl.BlockSpec((ROW_TILE, 2 * dc), lambda i: (i, 0)),
        out_shape=jax.ShapeDtypeStruct((t, 2 * dc), BF16), compiler_params=_params(),
    )(proj, proj, dconv, dconv, conv_w)


def _ffn_taps(up_ref, halo_ref, fw_ref, fb_ref, q, first):
    x0 = up_ref[q]
    x2, x1 = _causal_taps(x0, halo_ref[q], first)
    w = fw_ref[q]
    return w[0:1] * x2 + w[1:2] * x1 + w[2:3] * x0 + fb_ref[q], x2, x1, x0


def _ffn_fwd(up_pre, fw, fb):
    nb, t, w = up_pre.shape
    half = nb // 2

    def body(up_ref, halo_ref, fw_ref, fb_ref, act_ref):
        first = pl.program_id(0) == 0
        for q in range(half):
            a = _ffn_taps(up_ref, halo_ref, fw_ref, fb_ref, q, first)[0]
            val = _ffn_taps(up_ref, halo_ref, fw_ref, fb_ref, q + half, first)[0]
            act_ref[q] = (a * jax.nn.sigmoid(a) * val).astype(BF16)

    return pl.pallas_call(
        body, name="ffn_fwd", grid=(t // ROW_TILE,),
        in_specs=[pl.BlockSpec((nb, ROW_TILE, w), lambda i: (0, i, 0)),
                  pl.BlockSpec((nb, SUBLANES, w), lambda i: (0, _prev_halo(i), 0)),
                  pl.BlockSpec(fw.shape, lambda i: (0, 0, 0)),
                  pl.BlockSpec(fb.shape, lambda i: (0, 0, 0))],
        out_specs=pl.BlockSpec((half, ROW_TILE, w), lambda i: (0, i, 0)),
        out_shape=jax.ShapeDtypeStruct((half, t, w), BF16), compiler_params=_params(),
    )(up_pre, up_pre, fw, fb)


def _ffn_bwd1(up_pre, dact, fw, fb):
    nb, t, w = up_pre.shape
    half = nb // 2

    def body(up_ref, halo_ref, dact_ref, fw_ref, fb_ref, dup_ref, dfw_ref, dfb_ref):
        first = pl.program_id(0) == 0

        @pl.when(first)
        def _():
            dfw_ref[...] = jnp.zeros_like(dfw_ref)
            dfb_ref[...] = jnp.zeros_like(dfb_ref)

        for q in range(half):
            taps_a = _ffn_taps(up_ref, halo_ref, fw_ref, fb_ref, q, first)
            taps_v = _ffn_taps(up_ref, halo_ref, fw_ref, fb_ref, q + half, first)
            a, val = taps_a[0], taps_v[0]
            da_ct = dact_ref[q]
            sg = jax.nn.sigmoid(a)
            da = da_ct * val * sg * (1.0 + a * (1.0 - sg))
            dval = da_ct * a * sg
            for blk, dv, taps in ((q, da, taps_a), (q + half, dval, taps_v)):
                dup_ref[blk] = dv
                dfb_ref[blk] += jnp.sum(dv, axis=0, keepdims=True)
                for k in range(3):
                    dfw_ref[blk, k:k + 1, :] += jnp.sum(dv * taps[1 + k], axis=0, keepdims=True)

    return pl.pallas_call(
        body, name="ffn_bwd1", grid=(t // ROW_TILE,),
        in_specs=[pl.BlockSpec((nb, ROW_TILE, w), lambda i: (0, i, 0)),
                  pl.BlockSpec((nb, SUBLANES, w), lambda i: (0, _prev_halo(i), 0)),
                  pl.BlockSpec((half, ROW_TILE, w), lambda i: (0, i, 0)),
                  pl.BlockSpec(fw.shape, lambda i: (0, 0, 0)),
                  pl.BlockSpec(fb.shape, lambda i: (0, 0, 0))],
        out_specs=[pl.BlockSpec((nb, ROW_TILE, w), lambda i: (0, i, 0)),
                   pl.BlockSpec(fw.shape, lambda i: (0, 0, 0)),
                   pl.BlockSpec(fb.shape, lambda i: (0, 0, 0))],
        out_shape=[jax.ShapeDtypeStruct((nb, t, w), F32), jax.ShapeDtypeStruct(fw.shape, F32),
                   jax.ShapeDtypeStruct(fb.shape, F32)],
        compiler_params=_params(),
    )(up_pre, up_pre, dact, fw, fb)


def _ffn_bwd2(dup, fw):
    nb, t, w = dup.shape
    n_tiles = t // ROW_TILE
    per_tile = ROW_TILE // SUBLANES

    def body(d_ref, halo_ref, fw_ref, o_ref):
        last = pl.program_id(0) == n_tiles - 1
        for q in range(nb):
            d = d_ref[q]
            u1, u2 = _anticausal_taps(d, halo_ref[q], last)
            cw = fw_ref[q]
            o_ref[q] = (cw[2:3] * d + cw[1:2] * u1 + cw[0:1] * u2).astype(BF16)

    return pl.pallas_call(
        body, name="ffn_bwd2", grid=(n_tiles,),
        in_specs=[pl.BlockSpec((nb, ROW_TILE, w), lambda i: (0, i, 0)),
                  pl.BlockSpec((nb, SUBLANES, w),
                               lambda i: (0, jnp.minimum((i + 1) * per_tile, t // SUBLANES - 1), 0)),
                  pl.BlockSpec(fw.shape, lambda i: (0, 0, 0))],
        out_specs=pl.BlockSpec((nb, ROW_TILE, w), lambda i: (0, i, 0)),
        out_shape=jax.ShapeDtypeStruct((nb, t, w), BF16), compiler_params=_params(),
    )(dup, dup, fw)


def _to_segments(a):
    t, c = a.shape
    return a.reshape(N_SEG, t // N_SEG, c).transpose(1, 0, 2).reshape(t, c)


def _from_segments(a):
    t, c = a.shape
    return a.reshape(t // N_SEG, N_SEG, c).transpose(1, 0, 2).reshape(t, c)


def _cmul(ar, ai, br, bi):
    return ar * br - ai * bi, ar * bi + ai * br


def _segment_carries(pr, pi, fr, fi, forward):
    row = lax.broadcasted_iota(jnp.int32, fr.shape, 0)
    edge = row == (0 if forward else N_SEG - 1)
    shift = 1 if forward else N_SEG - 1
    sr, si = jnp.zeros_like(fr), jnp.zeros_like(fi)
    for _ in range(N_SEG - 1):
        tr, ti = _cmul(pr, pi, sr, si)
        sr = jnp.where(edge, 0.0, pltpu.roll(tr + fr, shift, 0))
        si = jnp.where(edge, 0.0, pltpu.roll(ti + fi, shift, 0))
    return sr, si


def _rows(i):
    return pl.ds(pl.multiple_of(i * SUBLANES, SUBLANES), SUBLANES)


def _s5_fwd(u, bb_re, bb_im, a_re, a_im, cc_re, cc_im, d_skip):
    t, ds = u.shape
    nch, _, cs = bb_re.shape
    steps = t // N_SEG

    def body(u_ref, bbr_ref, bbi_ref, ar_ref, ai_ref, ccr_ref, cci_ref, d_ref, sr_ref, si_ref, y_ref):
        ub = u_ref[...].astype(BF16)
        sr_ref[...] = _dot(ub, bbr_ref[...], NN)
        si_ref[...] = _dot(ub, bbi_ref[...], NN)
        ar = jnp.broadcast_to(ar_ref[...], (N_SEG, cs))
        ai = jnp.broadcast_to(ai_ref[...], (N_SEG, cs))
        zero = jnp.zeros((N_SEG, cs), F32)

        def totals(i, carry):
            sr, si, pr, pi = carry
            tr, ti = _cmul(ar, ai, sr, si)
            qr, qi = _cmul(ar, ai, pr, pi)
            return tr + sr_ref[_rows(i), :], ti + si_ref[_rows(i), :], qr, qi

        fr, fi, pr, pi = lax.fori_loop(0, steps, totals, (zero, zero, zero + 1.0, zero))
        s0r, s0i = _segment_carries(pr, pi, fr, fi, True)

        def scan(i, carry):
            tr, ti = _cmul(ar, ai, *carry)
            nr, ni = tr + sr_ref[_rows(i), :], ti + si_ref[_rows(i), :]
            sr_ref[_rows(i), :] = nr
            si_ref[_rows(i), :] = ni
            return nr, ni

        lax.fori_loop(0, steps, scan, (s0r, s0i))
        y_ref[...] = (_dot(sr_ref[...].astype(BF16), ccr_ref[...], NN)
                      - _dot(si_ref[...].astype(BF16), cci_ref[...], NN)
                      + d_ref[...] * u_ref[...])

    chunk3 = lambda r, c: pl.BlockSpec((None, r, c), lambda j: (j, 0, 0))
    return pl.pallas_call(
        body, name="s5_fwd", grid=(nch,),
        in_specs=[pl.BlockSpec((t, SSM_CHUNK), lambda j: (0, j)),
                  chunk3(SSM_CHUNK, cs), chunk3(SSM_CHUNK, cs), chunk3(1, cs), chunk3(1, cs),
                  chunk3(cs, SSM_CHUNK), chunk3(cs, SSM_CHUNK), chunk3(1, SSM_CHUNK)],
        out_specs=[pl.BlockSpec((t, cs), lambda j: (0, j)), pl.BlockSpec((t, cs), lambda j: (0, j)),
                   pl.BlockSpec((t, SSM_CHUNK), lambda j: (0, j))],
        out_shape=[jax.ShapeDtypeStruct((t, nch * cs), F32), jax.ShapeDtypeStruct((t, nch * cs), F32),
                   jax.ShapeDtypeStruct((t, ds), F32)],
        compiler_params=_params(),
    )(u, bb_re, bb_im, a_re, a_im, cc_re, cc_im, d_skip)


def _s5_bwd(dy, u, s_re, s_im, bb_re, bb_im, a_re, a_im, cc_re, cc_im, d_skip):
    t, ds = u.shape
    nch, _, cs = bb_re.shape
    steps = t // N_SEG

    def body(dy_ref, u_ref, sr_ref, si_ref, bbr_ref, bbi_ref, ar_ref, ai_ref, ccr_ref, cci_ref, d_ref,
             du_ref, dbbr_ref, dbbi_ref, dar_ref, dai_ref, dccr_ref, dcci_ref, dd_ref, gr_ref, gi_ref):
        dyv = dy_ref[...]
        dyb = dyv.astype(BF16)
        gr_ref[...] = _dot(dyb, ccr_ref[...], NT)
        gi_ref[...] = -_dot(dyb, cci_ref[...], NT)
        ar = jnp.broadcast_to(ar_ref[...], (N_SEG, cs))
        ai = -jnp.broadcast_to(ai_ref[...], (N_SEG, cs))
        zero = jnp.zeros((N_SEG, cs), F32)

        def totals(k, carry):
            i = steps - 1 - k
            gr, gi, pr, pi = carry
            tr, ti = _cmul(ar, ai, gr, gi)
            qr, qi = _cmul(ar, ai, pr, pi)
            return tr + gr_ref[_rows(i), :], ti + gi_ref[_rows(i), :], qr, qi

        fr, fi, pr, pi = lax.fori_loop(0, steps, totals, (zero, zero, zero + 1.0, zero))
        e0r, e0i = _segment_carries(pr, pi, fr, fi, False)

        def step(i, gr, gi, pr, pi, acc_r, acc_i):
            tr, ti = _cmul(ar, ai, gr, gi)
            nr, ni = tr + gr_ref[_rows(i), :], ti + gi_ref[_rows(i), :]
            gr_ref[_rows(i), :] = nr
            gi_ref[_rows(i), :] = ni
            return nr, ni, acc_r + nr * pr + ni * pi, acc_i + ni * pr - nr * pi

        def scan(k, carry):
            i = steps - 1 - k
            gr, gi, acc_r, acc_i = carry
            return step(i, gr, gi, sr_ref[_rows(i - 1), :], si_ref[_rows(i - 1), :], acc_r, acc_i)

        gr, gi, acc_r, acc_i = lax.fori_loop(0, steps - 1, scan, (e0r, e0i, zero, zero))
        row = lax.broadcasted_iota(jnp.int32, (N_SEG, cs), 0)
        last = _rows(steps - 1)
        pr = jnp.where(row == 0, 0.0, pltpu.roll(sr_ref[last, :], 1, 0))
        pi = jnp.where(row == 0, 0.0, pltpu.roll(si_ref[last, :], 1, 0))
        _, _, acc_r, acc_i = step(0, gr, gi, pr, pi, acc_r, acc_i)
        dar_ref[...] = jnp.sum(acc_r, axis=0, keepdims=True)
        dai_ref[...] = jnp.sum(acc_i, axis=0, keepdims=True)

        uv = u_ref[...]
        ub = uv.astype(BF16)
        grb = gr_ref[...].astype(BF16)
        gib = gi_ref[...].astype(BF16)
        du_ref[...] = d_ref[...] * dyv + _dot(grb, bbr_ref[...], NT) + _dot(gib, bbi_ref[...], NT)
        dbbr_ref[...] = _dot(ub, grb, TN)
        dbbi_ref[...] = _dot(ub, gib, TN)
        dccr_ref[...] = _dot(sr_ref[...].astype(BF16), dyb, TN)
        dcci_ref[...] = -_dot(si_ref[...].astype(BF16), dyb, TN)
        dd_ref[...] = jnp.sum(dyv * uv, axis=0, keepdims=True)

    chunk3 = lambda r, c: pl.BlockSpec((None, r, c), lambda j: (j, 0, 0))
    cols = lambda w: pl.BlockSpec((t, w), lambda j: (0, j))
    return pl.pallas_call(
        body, name="s5_bwd", grid=(nch,),
        in_specs=[cols(SSM_CHUNK), cols(SSM_CHUNK), cols(cs), cols(cs),
                  chunk3(SSM_CHUNK, cs), chunk3(SSM_CHUNK, cs), chunk3(1, cs), chunk3(1, cs),
                  chunk3(cs, SSM_CHUNK), chunk3(cs, SSM_CHUNK), chunk3(1, SSM_CHUNK)],
        out_specs=[cols(SSM_CHUNK), chunk3(SSM_CHUNK, cs), chunk3(SSM_CHUNK, cs), chunk3(1, cs),
                   chunk3(1, cs), chunk3(cs, SSM_CHUNK), chunk3(cs, SSM_CHUNK), chunk3(1, SSM_CHUNK)],
        out_shape=[jax.ShapeDtypeStruct((t, ds), F32),
                   jax.ShapeDtypeStruct((nch, SSM_CHUNK, cs), F32), jax.ShapeDtypeStruct((nch, SSM_CHUNK, cs), F32),
                   jax.ShapeDtypeStruct((nch, 1, cs), F32), jax.ShapeDtypeStruct((nch, 1, cs), F32),
                   jax.ShapeDtypeStruct((nch, cs, SSM_CHUNK), F32), jax.ShapeDtypeStruct((nch, cs, SSM_CHUNK), F32),
                   jax.ShapeDtypeStruct((nch, 1, SSM_CHUNK), F32)],
        scratch_shapes=[pltpu.VMEM((t, cs), F32), pltpu.VMEM((t, cs), F32)],
        compiler_params=_params(),
    )(dy, u, s_re, s_im, bb_re, bb_im, a_re, a_im, cc_re, cc_im, d_skip)


def _discretize(lr, li, log_dt, br, bi):
    dt = jnp.exp(log_dt)
    mag = jnp.exp(lr * dt)
    ang = li * dt
    a_re = mag * jnp.cos(ang)
    a_im = mag * jnp.sin(ang)
    den = lr * lr + li * li
    nr = a_re - 1.0
    f_re = (nr * lr + a_im * li) / den
    f_im = (a_im * lr - nr * li) / den
    return a_re, a_im, f_re * br - f_im * bi, f_re * bi + f_im * br


def _whole(shape):
    return pl.BlockSpec(shape, lambda: (0,) * len(shape))


def _disc_fwd(lr, li, log_dt, br, bi):
    def body(lr_ref, li_ref, dt_ref, br_ref, bi_ref, ar_ref, ai_ref, bbr_ref, bbi_ref):
        outs = _discretize(lr_ref[...], li_ref[...], dt_ref[...], br_ref[...], bi_ref[...])
        for ref, val in zip((ar_ref, ai_ref, bbr_ref, bbi_ref), outs):
            ref[...] = val

    args = (lr, li, log_dt, br, bi)
    outs = (lr, lr, br, br)
    return pl.pallas_call(
        body, name="disc_fwd", in_specs=[_whole(a.shape) for a in args],
        out_specs=[_whole(a.shape) for a in outs],
        out_shape=[jax.ShapeDtypeStruct(a.shape, F32) for a in outs],
    )(*args)


def _disc_bwd(lr, li, log_dt, br, bi, dar, dai, dbbr, dbbi):
    def body(lr_ref, li_ref, dt_ref, br_ref, bi_ref, dar_ref, dai_ref, dbbr_ref, dbbi_ref,
             dlr_ref, dli_ref, ddt_ref, dbr_ref, dbi_ref):
        _, vjp = jax.vjp(_discretize, lr_ref[...], li_ref[...], dt_ref[...], br_ref[...], bi_ref[...])
        grads = vjp((dar_ref[...], dai_ref[...], dbbr_ref[...], dbbi_ref[...]))
        for ref, val in zip((dlr_ref, dli_ref, ddt_ref, dbr_ref, dbi_ref), grads):
            ref[...] = val

    args = (lr, li, log_dt, br, bi, dar, dai, dbbr, dbbi)
    outs = (lr, li, log_dt, br, bi)
    return pl.pallas_call(
        body, name="disc_bwd", in_specs=[_whole(a.shape) for a in args],
        out_specs=[_whole(a.shape) for a in outs],
        out_shape=[jax.ShapeDtypeStruct(a.shape, F32) for a in outs],
    )(*args)


def _adamw(w, g, m, v):
    m = ADAM_B1 * m + (1.0 - ADAM_B1) * g
    v = ADAM_B2 * v + (1.0 - ADAM_B2) * (g * g)
    m_hat = m / ADAM_BC1
    v_hat = v / ADAM_BC2
    delta = -ADAM_LR * (m_hat / (jnp.sqrt(v_hat) + ADAM_EPS) + ADAM_WD * w)
    return delta, m, v


def _adamw_reduce(name, parts, w, m, v):
    _, r, c = parts.shape
    tr = r
    for cand in (256, 176, 128):
        if r % cand == 0:
            tr = cand
            break

    def body(p_ref, w_ref, m_ref, v_ref, g_ref, d_ref, nm_ref, nv_ref):
        g = p_ref[0].astype(F32)
        for k in range(1, N_DEV):
            g = g + p_ref[k].astype(F32)
        delta, nm, nv = _adamw(w_ref[...], g, m_ref[...], v_ref[...])
        g_ref[...] = g
        d_ref[...] = delta
        nm_ref[...] = nm
        nv_ref[...] = nv

    blk = pl.BlockSpec((tr, c), lambda i: (i, 0))
    return pl.pallas_call(
        body, name=name, grid=(r // tr,),
        in_specs=[pl.BlockSpec((N_DEV, tr, c), lambda i: (0, i, 0)), blk, blk, blk],
        out_specs=[blk] * 4, out_shape=[jax.ShapeDtypeStruct((r, c), F32)] * 4,
        compiler_params=_params(),
    )(parts, w, m, v)


def _pack(arrays, rows):
    flat = jnp.concatenate([a.reshape(-1) for a in arrays])
    return jnp.pad(flat, (0, rows * LANES - flat.shape[0])).reshape(rows, LANES)


def _unpack(packed, shapes):
    flat = packed.reshape(-1)
    out, off = [], 0
    for s in shapes:
        n = math.prod(s)
        out.append(flat[off:off + n].reshape(s))
        off += n
    return out


def _packed_rows(shapes):
    n = sum(math.prod(s) for s in shapes)
    return -(-n // (SUBLANES * LANES)) * SUBLANES


def _block_diag(x):
    j, g, r, c = x.shape
    eye = jnp.eye(g, dtype=x.dtype)
    return (x[:, :, :, None, :] * eye[None, :, None, :, None]).reshape(j, g * r, g * c)


def _diag_blocks(x, g):
    j, gr, gc = x.shape
    r, c = gr // g, gc // g
    eye = jnp.eye(g, dtype=x.dtype)
    return (x.reshape(j, g, r, g, c) * eye[None, :, None, :, None]).sum(axis=3)


def kernel(x, meta_tokens, norm_mix_g, w_in, conv_w, ssm_lam_re, ssm_lam_im, ssm_log_dt, ssm_b_re, ssm_b_im, ssm_c_re, ssm_c_im, ssm_d, ssm_w_glu, gain_conv_out, gain_ssm_out, w_out, norm_ffn_g, w_up, ffn_conv_w, ffn_conv_b, w_down, norm_final_g, loss_target, m_meta_tokens, m_norm_mix_g, m_w_in, m_conv_w, m_ssm_lam_re, m_ssm_lam_im, m_ssm_log_dt, m_ssm_b_re, m_ssm_b_im, m_ssm_c_re, m_ssm_c_im, m_ssm_d, m_ssm_w_glu, m_gain_conv_out, m_gain_ssm_out, m_w_out, m_norm_ffn_g, m_w_up, m_ffn_conv_w, m_ffn_conv_b, m_w_down, m_norm_final_g, v_meta_tokens, v_norm_mix_g, v_w_in, v_conv_w, v_ssm_lam_re, v_ssm_lam_im, v_ssm_log_dt, v_ssm_b_re, v_ssm_b_im, v_ssm_c_re, v_ssm_c_im, v_ssm_d, v_ssm_w_glu, v_gain_conv_out, v_gain_ssm_out, v_w_out, v_norm_ffn_g, v_w_up, v_ffn_conv_w, v_ffn_conv_b, v_w_down, v_norm_final_g):
    weights = dict(meta_tokens=meta_tokens, norm_mix_g=norm_mix_g, w_in=w_in, conv_w=conv_w, ssm_lam_re=ssm_lam_re, ssm_lam_im=ssm_lam_im, ssm_log_dt=ssm_log_dt, ssm_b_re=ssm_b_re, ssm_b_im=ssm_b_im, ssm_c_re=ssm_c_re, ssm_c_im=ssm_c_im, ssm_d=ssm_d, ssm_w_glu=ssm_w_glu, gain_conv_out=gain_conv_out, gain_ssm_out=gain_ssm_out, w_out=w_out, norm_ffn_g=norm_ffn_g, w_up=w_up, ffn_conv_w=ffn_conv_w, ffn_conv_b=ffn_conv_b, w_down=w_down, norm_final_g=norm_final_g)
    mom_m = dict(meta_tokens=m_meta_tokens, norm_mix_g=m_norm_mix_g, w_in=m_w_in, conv_w=m_conv_w, ssm_lam_re=m_ssm_lam_re, ssm_lam_im=m_ssm_lam_im, ssm_log_dt=m_ssm_log_dt, ssm_b_re=m_ssm_b_re, ssm_b_im=m_ssm_b_im, ssm_c_re=m_ssm_c_re, ssm_c_im=m_ssm_c_im, ssm_d=m_ssm_d, ssm_w_glu=m_ssm_w_glu, gain_conv_out=m_gain_conv_out, gain_ssm_out=m_gain_ssm_out, w_out=m_w_out, norm_ffn_g=m_norm_ffn_g, w_up=m_w_up, ffn_conv_w=m_ffn_conv_w, ffn_conv_b=m_ffn_conv_b, w_down=m_w_down, norm_final_g=m_norm_final_g)
    mom_v = dict(meta_tokens=v_meta_tokens, norm_mix_g=v_norm_mix_g, w_in=v_w_in, conv_w=v_conv_w, ssm_lam_re=v_ssm_lam_re, ssm_lam_im=v_ssm_lam_im, ssm_log_dt=v_ssm_log_dt, ssm_b_re=v_ssm_b_re, ssm_b_im=v_ssm_b_im, ssm_c_re=v_ssm_c_re, ssm_c_im=v_ssm_c_im, ssm_d=v_ssm_d, ssm_w_glu=v_ssm_w_glu, gain_conv_out=v_gain_conv_out, gain_ssm_out=v_gain_ssm_out, w_out=v_w_out, norm_ffn_g=v_norm_ffn_g, w_up=v_w_up, ffn_conv_w=v_ffn_conv_w, ffn_conv_b=v_ffn_conv_b, w_down=v_w_down, norm_final_g=v_norm_final_g)
    names = list(weights)

    n_meta, d_meta = meta_tokens.shape
    seq, d = x.shape[1], x.shape[2]
    rows_used = n_meta + seq
    t = -(-rows_used // ROW_TILE) * ROW_TILE
    d_in_s = w_in.shape[2]
    dc_s = conv_w.shape[2]
    dc = dc_s * N_DEV
    ds = ssm_w_glu.shape[2]
    n_groups, n_state, grp = ssm_b_re.shape[1:]
    ns = n_groups * n_state
    nch = ds // SSM_CHUNK
    gpc = n_groups // nch
    ff_s = w_up.shape[2]
    dn_s = w_down.shape[1]
    assert 3 * dc + ds == d_in_s * N_DEV and 2 * dn_s == ff_s and t % (N_SEG * SUBLANES) == 0

    small_shard = jnp.concatenate([meta_tokens.reshape(-1), conv_w.reshape(-1), ffn_conv_w.reshape(-1)])
    n_small = small_shard.shape[0]
    small_rows = -(-n_small // LANES)
    small_shard = jnp.pad(small_shard, (0, small_rows * LANES - n_small)).reshape(small_rows, LANES)
    tie = lambda a, token: lax.optimization_barrier((a, token))[0]
    ag, ag_token = _send_start("gather_weights_start", [
        small_shard, w_in[0].astype(BF16), ssm_w_glu[0].astype(BF16), w_out[0].astype(BF16),
        w_up[0].astype(BF16), w_down[0].astype(BF16)], gather=True)
    ssm_lam_re = tie(ssm_lam_re, ag_token)
    fb = ffn_conv_b.reshape(N_DEV, 1, ff_s)

    col = lambda a: a.reshape(ns, 1)
    lr, li = col(ssm_lam_re), col(ssm_lam_im)
    log_dt_e = jnp.broadcast_to(ssm_log_dt.reshape(n_groups, 1), (n_groups, n_state)).reshape(ns, 1)
    br, bi = ssm_b_re.reshape(ns, grp), ssm_b_im.reshape(ns, grp)
    a_re, a_im, bb_re, bb_im = _disc_fwd(lr, li, log_dt_e, br, bi)
    cs = gpc * n_state
    chunk_row = lambda a: a.reshape(nch, 1, cs)
    to_bb = lambda a: _block_diag(a.reshape(nch, gpc, n_state, grp).transpose(0, 1, 3, 2)).astype(BF16)
    to_cc = lambda a: _block_diag(a.reshape(nch, gpc, grp, n_state).transpose(0, 1, 3, 2)).astype(BF16)
    bbm_re, bbm_im = to_bb(bb_re), to_bb(bb_im)
    ccm_re, ccm_im = to_cc(ssm_c_re), to_cc(ssm_c_im)
    a_re_c, a_im_c = chunk_row(a_re), chunk_row(a_im)
    d_skip = ssm_d.reshape(nch, 1, SSM_CHUNK)

    g_small, g_in = _send_wait("gather_weights_wait_in", ag, (0, 1), bbm_im)
    g_small = g_small.reshape(N_DEV, -1)
    o1 = n_meta * d_meta
    o2 = o1 + 3 * dc_s
    meta_full = g_small[:, :o1].reshape(N_DEV, n_meta, d_meta).transpose(1, 0, 2).reshape(n_meta, d)
    conv_w_f = g_small[:, o1:o2].reshape(N_DEV, 3, dc_s).transpose(1, 0, 2).reshape(3, dc)
    fw = g_small[:, o2:o2 + 3 * ff_s].reshape(N_DEV, 3, ff_s)
    h0 = jnp.concatenate([meta_full, x[0], jnp.zeros((t - rows_used, d), F32)], axis=0)
    target = jnp.pad(loss_target[0], ((n_meta, t - rows_used), (0, 0)))
    full_t = lambda w: pl.BlockSpec((t, w), lambda *_: (0, 0))

    hn1 = _rms_fwd("norm_mix", h0, norm_mix_g)
    proj = _mm("proj", hn1, g_in, dims=NN, grid=(N_DEV,), a_spec=full_t(d),
               b_spec=pl.BlockSpec((None, d, d_in_s), lambda j: (j, 0, 0)),
               o_spec=pl.BlockSpec((t, d_in_s), lambda j: (0, j)),
               out_shape=jax.ShapeDtypeStruct((t, N_DEV * d_in_s), F32))
    u_seg = _to_segments(proj[:, 3 * dc:])
    s_re, s_im, y_seg = _s5_fwd(u_seg, bbm_re, bbm_im, a_re_c, a_im_c, ccm_re, ccm_im, d_skip)
    y_ssm = _from_segments(y_seg)
    g_glu, g_out = _send_wait("gather_weights_wait_mix", ag, (2, 3), y_ssm)
    w_out_f = g_out.reshape(-1, d)
    w_glu_f = g_glu.reshape(ds, ds)
    mixed, z_glu = _mix_fwd(proj, y_ssm, w_glu_f, conv_w_f, gain_conv_out, gain_ssm_out)
    tn_out = 256
    h1 = _mm("out_proj", mixed, w_out_f, dims=NN, grid=(d // tn_out,), a_spec=full_t(dc + ds),
             b_spec=pl.BlockSpec((dc + ds, tn_out), lambda j: (0, j)),
             o_spec=pl.BlockSpec((t, tn_out), lambda j: (0, j)),
             out_shape=jax.ShapeDtypeStruct((t, d), F32),
             res=h0, res_spec=pl.BlockSpec((t, tn_out), lambda j: (0, j)))
    hn2 = _rms_fwd("norm_ffn", h1, norm_ffn_g)
    (g_up,) = _send_wait("gather_weights_wait_up", ag, (4,), hn2)
    up_pre = _mm("up_proj", hn2, g_up, dims=NN, grid=(N_DEV,), a_spec=full_t(d),
                 b_spec=pl.BlockSpec((None, d, ff_s), lambda j: (j, 0, 0)),
                 o_spec=pl.BlockSpec((None, t, ff_s), lambda j: (j, 0, 0)),
                 out_shape=jax.ShapeDtypeStruct((N_DEV, t, ff_s), F32))
    act = _ffn_fwd(up_pre, fw, fb)
    nhalf = N_DEV // 2
    (g_down,) = _send_wait("gather_weights_wait_down", ag, (5,), act)
    w_down_f = g_down.reshape(nhalf, 2 * dn_s, d)
    h2 = _mm("down_proj", act, w_down_f, dims=NN, grid=(d // tn_out, nhalf),
             a_spec=pl.BlockSpec((None, t, ff_s), lambda j, k: (k, 0, 0)),
             b_spec=pl.BlockSpec((None, ff_s, tn_out), lambda j, k: (k, 0, j)),
             o_spec=pl.BlockSpec((t, tn_out), lambda j, k: (0, j)),
             out_shape=jax.ShapeDtypeStruct((t, d), F32), acc_shape=(t, tn_out),
             res=h1, res_spec=pl.BlockSpec((t, tn_out), lambda j, k: (0, j)))

    dh2, dh2_b, loss_part, d_norm_final = _loss_bwd(h2, target, norm_final_g.reshape(1, d), n_meta, rows_used)
    dact = _mm("down_dgrad", dh2_b, w_down_f, dims=NT, grid=(nhalf,), a_spec=full_t(d),
               b_spec=pl.BlockSpec((None, ff_s, d), lambda j: (j, 0, 0)),
               o_spec=pl.BlockSpec((None, t, ff_s), lambda j: (j, 0, 0)),
               out_shape=jax.ShapeDtypeStruct((nhalf, t, ff_s), F32))
    tn_w = 512
    dw_down = _mm("down_wgrad", act, dh2_b, dims=TN, grid=(nhalf, d // tn_w),
                  a_spec=pl.BlockSpec((None, t, ff_s), lambda j, n: (j, 0, 0)),
                  b_spec=pl.BlockSpec((t, tn_w), lambda j, n: (0, n)),
                  o_spec=pl.BlockSpec((None, ff_s, tn_w), lambda j, n: (j, 0, n)),
                  out_shape=jax.ShapeDtypeStruct((nhalf, ff_s, d), BF16))
    ex_down, token = _send_start("exchange_down_start", [dw_down.reshape(N_DEV, dn_s, d)], gather=False)
    dup, d_fw, d_fb = _ffn_bwd1(up_pre, tie(dact, token), fw, fb)
    dup_pre = _ffn_bwd2(dup, fw)
    dhn2 = _mm("up_dgrad", dup_pre, g_up, dims=NT, grid=(d // tn_w, N_DEV),
               a_spec=pl.BlockSpec((None, t, ff_s), lambda i, k: (k, 0, 0)),
               b_spec=pl.BlockSpec((None, tn_w, ff_s), lambda i, k: (k, i, 0)),
               o_spec=pl.BlockSpec((t, tn_w), lambda i, k: (0, i)),
               out_shape=jax.ShapeDtypeStruct((t, d), F32), acc_shape=(t, tn_w))
    dw_up = _mm("up_wgrad", hn2, dup_pre, dims=TN, grid=(N_DEV, d // tn_w),
                a_spec=pl.BlockSpec((t, tn_w), lambda j, i: (0, i)),
                b_spec=pl.BlockSpec((None, t, ff_s), lambda j, i: (j, 0, 0)),
                o_spec=pl.BlockSpec((None, tn_w, ff_s), lambda j, i: (j, i, 0)),
                out_shape=jax.ShapeDtypeStruct((N_DEV, d, ff_s), BF16))
    ex_up, token = _send_start("exchange_up_start", [dw_up], gather=False)
    dh1, dh1_b, d_norm_ffn = _rms_bwd_res("norm_ffn_bwd", dh2, tie(dhn2, token), h1, norm_ffn_g)
    dmixed = _mm("out_dgrad", dh1_b, w_out_f, dims=NT, grid=((dc + ds) // tn_out,), a_spec=full_t(d),
                 b_spec=pl.BlockSpec((tn_out, d), lambda i: (i, 0)),
                 o_spec=pl.BlockSpec((t, tn_out), lambda i: (0, i)),
                 out_shape=jax.ShapeDtypeStruct((t, dc + ds), F32))
    dw_out = _mm("out_wgrad", mixed, dh1_b, dims=TN, grid=((dc + ds) // tn_out,),
                 a_spec=pl.BlockSpec((t, tn_out), lambda i: (0, i)), b_spec=full_t(d),
                 o_spec=pl.BlockSpec((tn_out, d), lambda i: (i, 0)),
                 out_shape=jax.ShapeDtypeStruct((dc + ds, d), BF16))
    db_gate, dconv, dy_ssm, d_wglu, d_conv_w, d_gain_c, d_gain_s = _mix_bwd1(
        proj, y_ssm, z_glu, dmixed, w_glu_f, conv_w_f, gain_conv_out, gain_ssm_out)
    ex_mix, token = _send_start("exchange_mix_start", [
        dw_out.reshape(N_DEV, -1, d), d_wglu.astype(BF16).reshape(N_DEV, -1, ds)], gather=False)
    dcdv = _mix_bwd2(proj, tie(dconv, token), conv_w_f)
    (du_seg, d_bbm_re, d_bbm_im, d_a_re, d_a_im, d_ccm_re, d_ccm_im, d_dskip) = _s5_bwd(
        _to_segments(dy_ssm), u_seg, s_re, s_im, bbm_re, bbm_im, a_re_c, a_im_c, ccm_re, ccm_im, d_skip)
    dproj = jnp.concatenate([db_gate, dcdv, _from_segments(du_seg).astype(BF16)], axis=1)
    dhn1 = _mm("proj_dgrad", dproj, g_in, dims=NT, grid=(d // tn_w, N_DEV),
               a_spec=pl.BlockSpec((t, d_in_s), lambda i, k: (0, k)),
               b_spec=pl.BlockSpec((None, tn_w, d_in_s), lambda i, k: (k, i, 0)),
               o_spec=pl.BlockSpec((t, tn_w), lambda i, k: (0, i)),
               out_shape=jax.ShapeDtypeStruct((t, d), F32), acc_shape=(t, tn_w))
    dh0, _, d_norm_mix = _rms_bwd_res("norm_mix_bwd", dh1, dhn1, h0, norm_mix_g)
    dw_in = _mm("proj_wgrad", hn1, dproj, dims=TN, grid=(N_DEV, d // tn_w),
                a_spec=pl.BlockSpec((t, tn_w), lambda j, i: (0, i)),
                b_spec=pl.BlockSpec((t, d_in_s), lambda j, i: (0, j)),
                o_spec=pl.BlockSpec((None, tn_w, d_in_s), lambda j, i: (j, i, 0)),
                out_shape=jax.ShapeDtypeStruct((N_DEV, d, d_in_s), BF16))
    grad_x = dh0[n_meta:rows_used][None]
    loss = lax.psum(loss_part[0, 0], ("x", "y", "c"))
    d_meta_b = dh0[:n_meta].reshape(n_meta, N_DEV, d_meta).transpose(1, 0, 2)
    d_conv_w_b = d_conv_w.reshape(3, N_DEV, dc_s).transpose(1, 0, 2)
    small_blocks = jnp.concatenate([d_meta_b.reshape(N_DEV, -1), d_conv_w_b.reshape(N_DEV, -1),
                                    d_fw.reshape(N_DEV, -1)], axis=1)
    small_blocks = jnp.pad(small_blocks, ((0, 0), (0, small_rows * LANES - n_small)))
    small_blocks = small_blocks.reshape(N_DEV, small_rows, LANES)
    ex_in, token = _send_start("exchange_in_start", [dw_in, small_blocks], gather=False)
    d_norm_mix = tie(d_norm_mix, token)

    from_bb = lambda a: _diag_blocks(a, gpc).transpose(0, 1, 3, 2).reshape(ns, grp)
    from_cc = lambda a: _diag_blocks(a, gpc).transpose(0, 1, 3, 2).reshape(1, n_groups, grp, n_state)
    d_lr, d_li, d_dt_e, d_br, d_bi = _disc_bwd(
        lr, li, log_dt_e, br, bi, d_a_re.reshape(ns, 1), d_a_im.reshape(ns, 1),
        from_bb(d_bbm_re), from_bb(d_bbm_im))

    rep_grads = dict(
        norm_mix_g=d_norm_mix, ssm_lam_re=d_lr.reshape(ssm_lam_re.shape), ssm_lam_im=d_li.reshape(ssm_lam_im.shape),
        ssm_log_dt=d_dt_e.reshape(n_groups, n_state).sum(axis=1).reshape(ssm_log_dt.shape),
        ssm_b_re=d_br.reshape(ssm_b_re.shape), ssm_b_im=d_bi.reshape(ssm_b_im.shape),
        ssm_c_re=from_cc(d_ccm_re), ssm_c_im=from_cc(d_ccm_im), ssm_d=d_dskip.reshape(ssm_d.shape),
        gain_conv_out=d_gain_c, gain_ssm_out=d_gain_s, norm_ffn_g=d_norm_ffn,
        ffn_conv_b=d_fb.reshape(ffn_conv_b.shape), norm_final_g=d_norm_final.reshape(norm_final_g.shape))
    rep_names = list(rep_grads)
    rep_shapes = [weights[n].shape for n in rep_names]
    rep_rows = _packed_rows(rep_shapes)
    (rep_parts,) = _allgather("allgather_small_grads", [_pack([rep_grads[n] for n in rep_names], rep_rows)])
    rep_out = _adamw_reduce("adamw_replicated", rep_parts,
                            _pack([weights[n] for n in rep_names], rep_rows),
                            _pack([mom_m[n] for n in rep_names], rep_rows),
                            _pack([mom_v[n] for n in rep_names], rep_rows))
    rep_out = [_unpack(o, rep_shapes) for o in rep_out]

    shard_out = {}

    def update(n, parts):
        res = _adamw_reduce("adamw_" + n, parts, weights[n][0], mom_m[n][0], mom_v[n][0])
        shard_out[n] = [r.reshape(weights[n].shape) for r in res]
        return res[0]

    (p_down,) = _send_wait("exchange_down_wait", ex_down, (0,), rep_out[0][0])
    done = update("w_down", p_down)
    (p_up,) = _send_wait("exchange_up_wait", ex_up, (0,), done)
    done = update("w_up", p_up)
    p_out, p_glu = _send_wait("exchange_mix_wait", ex_mix, (0, 1), done)
    update("w_out", p_out)
    done = update("ssm_w_glu", p_glu)
    p_in, p_small = _send_wait("exchange_in_wait", ex_in, (0, 1), done)
    update("w_in", p_in)
    small_names = ["meta_tokens", "conv_w", "ffn_conv_w"]
    small_shapes = [weights[n].shape for n in small_names]
    pack_small = lambda tree: jnp.pad(jnp.concatenate([tree[n].reshape(-1) for n in small_names]),
                                      (0, small_rows * LANES - n_small)).reshape(small_rows, LANES)
    res = _adamw_reduce("adamw_small_shards", p_small, pack_small(weights), pack_small(mom_m), pack_small(mom_v))
    res = [_unpack(r, small_shapes) for r in res]
    for i, n in enumerate(small_names):
        shard_out[n] = [r[i] for r in res]
    for i, n in enumerate(rep_names):
        shard_out[n] = [r[i] for r in rep_out]

    grads = [shard_out[n][0] for n in names]
    deltas = [shard_out[n][1] for n in names]
    new_m = [shard_out[n][2] for n in names]
    new_v = [shard_out[n][3] for n in names]
    return (loss, grad_x, *grads, *deltas, *new_m, *new_v)
```

```python
import functools
import math

import jax
import jax.numpy as jnp
from jax import lax
from jax.experimental import pallas as pl
from jax.experimental.pallas import tpu as pltpu

F32 = jnp.float32
BF16 = jnp.bfloat16
MESH = pl.DeviceIdType.MESH

N_DEV = 8
RMS_EPS = 1e-6
ADAM_LR = 0.001
ADAM_B1 = 0.9
ADAM_B2 = 0.999
ADAM_EPS = 1e-08
ADAM_WD = 0.01
ADAM_STEP = 10
ADAM_BC1 = 1.0 - ADAM_B1 ** ADAM_STEP
ADAM_BC2 = 1.0 - ADAM_B2 ** ADAM_STEP

SUBLANES = 8
LANES = 128
ROW_TILE = 128
N_SEG = 8
SSM_CHUNK = 128
VMEM_LIMIT = 48 * 1024 * 1024

NN = ((1,), (0,))
NT = ((1,), (1,))
TN = ((0,), (0,))


def _params(**kw):
    return pltpu.CompilerParams(vmem_limit_bytes=VMEM_LIMIT, **kw)


def _dot(a, b, dims):
    return lax.dot_general(a, b, (dims, ((), ())), preferred_element_type=F32)


def _mean_sq_rsqrt(x):
    return lax.rsqrt(jnp.mean(x * x, axis=-1, keepdims=True) + RMS_EPS)


def _rms_bwd(x, r, g, dy):
    xhat = x * r
    dxh = dy * g
    dx = r * (dxh - xhat * jnp.mean(dxh * xhat, axis=-1, keepdims=True))
    return dx, dy * xhat


def _gelu(y):
    c = math.sqrt(2.0 / math.pi)
    t = jnp.tanh(c * (y + 0.044715 * y * y * y))
    return 0.5 * y * (1.0 + t), t


def _gelu_grad(y, t):
    c = math.sqrt(2.0 / math.pi)
    return 0.5 * (1.0 + t) + 0.5 * y * (1.0 - t * t) * c * (1.0 + 3.0 * 0.044715 * y * y)


def _shift_down(ext, k, rows):
    return pltpu.roll(ext, k, 0)[SUBLANES:SUBLANES + rows]


def _shift_up(ext, k, rows):
    n = ext.shape[0]
    return pltpu.roll(ext, n - k, 0)[:rows]


def _dev_index(p):
    return 4 * p[0] + 2 * p[1] + p[2]


def _allgather(name, shards, deps=()):
    n = len(shards)

    def body(*refs):
        ins, outs = refs[:n], refs[n:2 * n]
        send_sems, recv_sems, local_sems = refs[2 * n:]
        x, y, c = lax.axis_index("x"), lax.axis_index("y"), lax.axis_index("c")
        me, sibling = (x, y, c), (x, y, 1 - c)
        chips = [(1 - x, y), (x, 1 - y), (1 - x, 1 - y)]

        def copy(a, k, block, to, src=None):
            dst = outs[a].at[_dev_index(block)]
            return pltpu.make_async_remote_copy(
                src_ref=dst if src is None else src, dst_ref=dst,
                send_sem=send_sems.at[a, k], recv_sem=recv_sems.at[a, k],
                device_id=to, device_id_type=MESH)

        mine = [pltpu.make_async_copy(ins[a], outs[a].at[_dev_index(me)], local_sems.at[a])
                for a in range(n)]
        for cp in mine:
            cp.start()
        first = []
        for a in range(n):
            first.append(copy(a, 0, me, sibling, src=ins[a]))
            for j, chip in enumerate(chips):
                first.append(copy(a, 1 + j, me, (*chip, c), src=ins[a]))
        for cp in first:
            cp.start()
        passed = []
        for j, chip in enumerate(chips):
            for a in range(n):
                copy(a, 1 + j, (*chip, c), me).wait_recv()
                fwd = copy(a, 4 + j, (*chip, c), sibling)
                fwd.start()
                passed.append(fwd)
        for a in range(n):
            copy(a, 0, sibling, me).wait_recv()
            for j, chip in enumerate(chips):
                copy(a, 4 + j, (*chip, 1 - c), me).wait_recv()
        for cp in first + passed:
            cp.wait_send()
        for cp in mine:
            cp.wait()

    any_spec = pl.BlockSpec(memory_space=pl.ANY)
    return pl.pallas_call(
        _ignoring_deps(body, n, deps), name=name,
        out_shape=[jax.ShapeDtypeStruct((N_DEV,) + s.shape, s.dtype) for s in shards],
        in_specs=[any_spec] * (n + len(deps)), out_specs=[any_spec] * n,
        scratch_shapes=[pltpu.SemaphoreType.DMA((n, 7)), pltpu.SemaphoreType.DMA((n, 7)),
                        pltpu.SemaphoreType.DMA((n,))],
    )(*shards, *deps)


HBM_SPEC = pl.BlockSpec(memory_space=pltpu.HBM)
SEM_SPEC = pl.BlockSpec(memory_space=pltpu.SEMAPHORE)
ANY_SPEC = pl.BlockSpec(memory_space=pl.ANY)
DATAFLOW = pltpu.SideEffectType.DATAFLOW_SIDE_EFFECTING


def _ignoring_deps(body, n_in, deps):
    n_dep = len(deps)

    def wrapped(*refs):
        return body(*refs[:n_in], *refs[n_in + n_dep:])

    return wrapped


def _my_position():
    x, y, c = lax.axis_index("x"), lax.axis_index("y"), lax.axis_index("c")
    return (x, y, c)


def _peer(me, k):
    return tuple((1 - v) if (k >> s) & 1 else v for v, s in zip(me, (2, 1, 0)))


def _split_copies(src_refs, land_refs, send_sems, recv_sems, gather):
    me = _my_position()
    copies = []
    for a, (src, land) in enumerate(zip(src_refs, land_refs)):
        for k in range(1, N_DEV):
            peer = _peer(me, k)
            copies.append(pltpu.make_async_remote_copy(
                src_ref=src if gather else src.at[_dev_index(peer)], dst_ref=land.at[_dev_index(me)],
                send_sem=send_sems[a].at[k - 1], recv_sem=recv_sems[a].at[k - 1],
                device_id=peer, device_id_type=MESH))
    return copies


def _own_slot(block, like_shape):
    me = _dev_index(_my_position())
    return lax.dynamic_update_index_in_dim(lax.empty(like_shape, block.dtype), block, me, 0)


def _send_start(name, srcs, gather):
    n = len(srcs)
    me = _dev_index(_my_position())
    if gather:
        lands = [_own_slot(s, (N_DEV,) + s.shape) for s in srcs]
    else:
        lands = [_own_slot(lax.dynamic_index_in_dim(s, me, 0, keepdims=False), s.shape) for s in srcs]

    def body(*refs):
        src_refs, land_refs = refs[:n], refs[n:2 * n]
        send_sems, recv_sems = refs[2 * n:3 * n], refs[3 * n:4 * n]
        token = refs[-1]
        for cp in _split_copies(src_refs, land_refs, send_sems, recv_sems, gather):
            cp.start()
        token[...] = jnp.zeros_like(token)

    hbm = lambda a: pltpu.HBM(a.shape, a.dtype)
    sems = [pltpu.SemaphoreType.DMA((N_DEV - 1,))] * n
    outs = pl.pallas_call(
        body, name=name,
        out_shape=(*sems, *sems, *[hbm(s) for s in srcs], *[hbm(l) for l in lands],
                   jax.ShapeDtypeStruct((SUBLANES, LANES), F32)),
        in_specs=[HBM_SPEC] * (2 * n),
        out_specs=(*[SEM_SPEC] * (2 * n), *[HBM_SPEC] * (2 * n), pl.BlockSpec(memory_space=pltpu.VMEM)),
        input_output_aliases={i: 2 * n + i for i in range(2 * n)},
        compiler_params=pltpu.CompilerParams(has_side_effects=DATAFLOW),
    )(*[pltpu.with_memory_space_constraint(a, pltpu.HBM) for a in (*srcs, *lands)])
    state = dict(send=outs[:n], recv=outs[n:2 * n], srcs=outs[2 * n:3 * n], lands=outs[3 * n:4 * n],
                 gather=gather)
    return state, outs[-1]


def _send_wait(name, state, which, after):
    n = len(which)
    pick = lambda key: [state[key][i] for i in which]
    gather = state["gather"]

    def body(*refs):
        src_refs, land_refs = refs[:n], refs[n:2 * n]
        send_sems, recv_sems = refs[2 * n:3 * n], refs[3 * n:4 * n]
        for cp in _split_copies(src_refs, land_refs, send_sems, recv_sems, gather):
            cp.wait_send()
            cp.wait_recv()

    srcs, lands = pick("srcs"), pick("lands")
    hbm = lambda a: pltpu.HBM(a.shape, a.dtype)
    outs = pl.pallas_call(
        body, name=name,
        out_shape=(*[hbm(s) for s in srcs], *[hbm(l) for l in lands]),
        in_specs=[*[HBM_SPEC] * (2 * n), *[SEM_SPEC] * (2 * n), ANY_SPEC],
        out_specs=tuple([HBM_SPEC] * (2 * n)),
        input_output_aliases={i: i for i in range(2 * n)},
        compiler_params=pltpu.CompilerParams(has_side_effects=DATAFLOW),
    )(*srcs, *lands, *pick("send"), *pick("recv"), after)
    return outs[n:]


def _mm(name, a, b, *, dims, grid, a_spec, b_spec, o_spec, out_shape, acc_shape=None,
        res=None, res_spec=None):
    n_red = grid[-1] if acc_shape is not None else 1
    red_axis = len(grid) - 1

    def body(*refs):
        a_ref, b_ref = refs[0], refs[1]
        r_ref = refs[2] if res is not None else None
        o_ref = refs[3] if res is not None else refs[2]
        part = _dot(a_ref[...], b_ref[...], dims)
        if acc_shape is None:
            if r_ref is not None:
                part = part + r_ref[...]
            o_ref[...] = part.astype(o_ref.dtype)
            return
        acc_ref = refs[-1]
        k = pl.program_id(red_axis)

        @pl.when(k == 0)
        def _():
            acc_ref[...] = part

        @pl.when(k > 0)
        def _():
            acc_ref[...] += part

        @pl.when(k == n_red - 1)
        def _():
            total = acc_ref[...]
            if r_ref is not None:
                total = total + r_ref[...]
            o_ref[...] = total.astype(o_ref.dtype)

    ins, in_specs = [a, b], [a_spec, b_spec]
    if res is not None:
        ins.append(res)
        in_specs.append(res_spec)
    return pl.pallas_call(
        body, name=name, grid=grid, in_specs=in_specs, out_specs=o_spec, out_shape=out_shape,
        scratch_shapes=[pltpu.VMEM(acc_shape, F32)] if acc_shape is not None else [],
        compiler_params=_params(),
    )(*ins)


def _rms_fwd(name, h, g):
    t, d = h.shape

    def body(h_ref, g_ref, o_ref):
        x = h_ref[...]
        o_ref[...] = (x * _mean_sq_rsqrt(x) * g_ref[...]).astype(BF16)

    return pl.pallas_call(
        body, name=name, grid=(t // ROW_TILE,),
        in_specs=[pl.BlockSpec((ROW_TILE, d), lambda i: (i, 0)), pl.BlockSpec((1, d), lambda i: (0, 0))],
        out_specs=pl.BlockSpec((ROW_TILE, d), lambda i: (i, 0)),
        out_shape=jax.ShapeDtypeStruct((t, d), BF16), compiler_params=_params(),
    )(h, g)


def _rms_bwd_res(name, dres, dhn, h, g, deps=()):
    t, d = h.shape

    def body(dres_ref, dhn_ref, h_ref, g_ref, dh_ref, dhb_ref, dg_ref):
        x = h_ref[...]
        dx, dgt = _rms_bwd(x, _mean_sq_rsqrt(x), g_ref[...], dhn_ref[...])
        dh = dres_ref[...] + dx
        dh_ref[...] = dh
        dhb_ref[...] = dh.astype(BF16)

        @pl.when(pl.program_id(0) == 0)
        def _():
            dg_ref[...] = jnp.zeros_like(dg_ref)

        dg_ref[...] += jnp.sum(dgt, axis=0, keepdims=True)

    row = pl.BlockSpec((ROW_TILE, d), lambda i: (i, 0))
    vec = pl.BlockSpec((1, d), lambda i: (0, 0))
    return pl.pallas_call(
        _ignoring_deps(body, 4, deps), name=name, grid=(t // ROW_TILE,),
        in_specs=[row, row, row, vec] + [ANY_SPEC] * len(deps), out_specs=[row, row, vec],
        out_shape=[jax.ShapeDtypeStruct((t, d), F32), jax.ShapeDtypeStruct((t, d), BF16),
                   jax.ShapeDtypeStruct((1, d), F32)],
        compiler_params=_params(),
    )(dres, dhn, h, g, *deps)


def _loss_bwd(h2, target, g, row_lo, row_hi):
    t, d = h2.shape

    def body(h_ref, tg_ref, g_ref, dh_ref, dhb_ref, loss_ref, dg_ref):
        i = pl.program_id(0)
        x = h_ref[...]
        r = _mean_sq_rsqrt(x)
        gv = g_ref[...]
        y = x * r * gv
        rows = i * ROW_TILE + lax.broadcasted_iota(jnp.int32, (ROW_TILE, 1), 0)
        valid = jnp.logical_and(rows >= row_lo, rows < row_hi)
        err = jnp.where(valid, y - tg_ref[...], 0.0)
        dy = err * (1.0 / d)
        dx, dgt = _rms_bwd(x, r, gv, dy)
        dh_ref[...] = dx
        dhb_ref[...] = dx.astype(BF16)

        @pl.when(i == 0)
        def _():
            loss_ref[...] = jnp.zeros_like(loss_ref)
            dg_ref[...] = jnp.zeros_like(dg_ref)

        row_loss = jnp.mean(err * err, axis=-1, keepdims=True)
        loss_ref[...] += 0.5 * jnp.sum(row_loss, axis=0, keepdims=True)
        dg_ref[...] += jnp.sum(dgt, axis=0, keepdims=True)

    row = pl.BlockSpec((ROW_TILE, d), lambda i: (i, 0))
    vec = pl.BlockSpec((1, d), lambda i: (0, 0))
    return pl.pallas_call(
        body, name="loss_bwd", grid=(t // ROW_TILE,), in_specs=[row, row, vec],
        out_specs=[row, row, pl.BlockSpec((1, 1), lambda i: (0, 0)), vec],
        out_shape=[jax.ShapeDtypeStruct((t, d), F32), jax.ShapeDtypeStruct((t, d), BF16),
                   jax.ShapeDtypeStruct((1, 1), F32), jax.ShapeDtypeStruct((1, d), F32)],
        compiler_params=_params(),
    )(h2, target, g)


def _prev_halo(i):
    return jnp.maximum(i * (ROW_TILE // SUBLANES) - 1, 0)


def _causal_taps(cur, halo, first):
    rows = cur.shape[0]
    halo = jnp.where(first, 0.0, halo)
    ext = jnp.concatenate([halo, cur], axis=0)
    return _shift_down(ext, 2, rows), _shift_down(ext, 1, rows)


def _anticausal_taps(cur, halo, last):
    rows = cur.shape[0]
    halo = jnp.where(last, 0.0, halo)
    ext = jnp.concatenate([cur, halo], axis=0)
    return _shift_up(ext, 1, rows), _shift_up(ext, 2, rows)


def _mix_fwd(proj, y, w_glu, conv_w, gain_c, gain_s):
    t = proj.shape[0]
    dc = conv_w.shape[1]
    ds = y.shape[1]

    def body(p_ref, halo_ref, y_ref, wg_ref, cw_ref, gc_ref, gs_ref, mixed_ref, z_ref):
        i = pl.program_id(0)
        p = p_ref[...]
        b, c, v = p[:, :dc], p[:, dc:2 * dc], p[:, 2 * dc:3 * dc]
        cv = c * v
        hp = halo_ref[...]
        x2, x1 = _causal_taps(cv, hp[:, dc:2 * dc] * hp[:, 2 * dc:3 * dc], i == 0)
        cw = cw_ref[...]
        conv = cw[0:1] * x2 + cw[1:2] * x1 + cw[2:3] * cv
        co = b * conv
        mixed_ref[:, :dc] = (co * _mean_sq_rsqrt(co) * gc_ref[...]).astype(BF16)
        g, _ = _gelu(y_ref[...])
        z = _dot(g.astype(BF16), wg_ref[...], NN)
        z_ref[...] = z
        so = g * jax.nn.sigmoid(z)
        mixed_ref[:, dc:] = (so * _mean_sq_rsqrt(so) * gs_ref[...]).astype(BF16)

    const = lambda i: (0, 0)
    return pl.pallas_call(
        body, name="mix_fwd", grid=(t // ROW_TILE,),
        in_specs=[pl.BlockSpec((ROW_TILE, 3 * dc), lambda i: (i, 0)),
                  pl.BlockSpec((SUBLANES, 3 * dc), lambda i: (_prev_halo(i), 0)),
                  pl.BlockSpec((ROW_TILE, ds), lambda i: (i, 0)),
                  pl.BlockSpec((ds, ds), const), pl.BlockSpec(conv_w.shape, const),
                  pl.BlockSpec((1, dc), const), pl.BlockSpec((1, ds), const)],
        out_specs=[pl.BlockSpec((ROW_TILE, dc + ds), lambda i: (i, 0)),
                   pl.BlockSpec((ROW_TILE, ds), lambda i: (i, 0))],
        out_shape=[jax.ShapeDtypeStruct((t, dc + ds), BF16), jax.ShapeDtypeStruct((t, ds), F32)],
        compiler_params=_params(),
    )(proj, proj, y, w_glu, conv_w, gain_c, gain_s)


def _mix_bwd1(proj, y, z, dmixed, w_glu, conv_w, gain_c, gain_s):
    t = proj.shape[0]
    dc = conv_w.shape[1]
    ds = y.shape[1]

    def body(p_ref, halo_ref, y_ref, z_ref, dm_ref, wg_ref, cw_ref, gc_ref, gs_ref,
             db_ref, dconv_ref, dy_ref, dwg_ref, dcw_ref, dgc_ref, dgs_ref):
        i = pl.program_id(0)

        @pl.when(i == 0)
        def _():
            dwg_ref[...] = jnp.zeros_like(dwg_ref)
            dcw_ref[...] = jnp.zeros_like(dcw_ref)
            dgc_ref[...] = jnp.zeros_like(dgc_ref)
            dgs_ref[...] = jnp.zeros_like(dgs_ref)

        p = p_ref[...]
        b, c, v = p[:, :dc], p[:, dc:2 * dc], p[:, 2 * dc:3 * dc]
        cv = c * v
        hp = halo_ref[...]
        x2, x1 = _causal_taps(cv, hp[:, dc:2 * dc] * hp[:, 2 * dc:3 * dc], i == 0)
        cw = cw_ref[...]
        conv = cw[0:1] * x2 + cw[1:2] * x1 + cw[2:3] * cv
        co = b * conv
        dm = dm_ref[...]
        dco, dgt = _rms_bwd(co, _mean_sq_rsqrt(co), gc_ref[...], dm[:, :dc])
        dgc_ref[...] += jnp.sum(dgt, axis=0, keepdims=True)
        db_ref[...] = (dco * conv).astype(BF16)
        dconv = dco * b
        dconv_ref[...] = dconv
        dcw_ref[0:1, :] += jnp.sum(dconv * x2, axis=0, keepdims=True)
        dcw_ref[1:2, :] += jnp.sum(dconv * x1, axis=0, keepdims=True)
        dcw_ref[2:3, :] += jnp.sum(dconv * cv, axis=0, keepdims=True)

        yv = y_ref[...]
        g, th = _gelu(yv)
        sg = jax.nn.sigmoid(z_ref[...])
        so = g * sg
        dso, dgt = _rms_bwd(so, _mean_sq_rsqrt(so), gs_ref[...], dm[:, dc:])
        dgs_ref[...] += jnp.sum(dgt, axis=0, keepdims=True)
        dz = (dso * g * sg * (1.0 - sg)).astype(BF16)
        dg = dso * sg + _dot(dz, wg_ref[...], NT)
        dwg_ref[...] += _dot(g.astype(BF16), dz, TN)
        dy_ref[...] = dg * _gelu_grad(yv, th)

    const = lambda i: (0, 0)
    row = lambda w: pl.BlockSpec((ROW_TILE, w), lambda i: (i, 0))
    return pl.pallas_call(
        body, name="mix_bwd1", grid=(t // ROW_TILE,),
        in_specs=[row(3 * dc), pl.BlockSpec((SUBLANES, 3 * dc), lambda i: (_prev_halo(i), 0)),
                  row(ds), row(ds), row(dc + ds), pl.BlockSpec((ds, ds), const),
                  pl.BlockSpec(conv_w.shape, const), pl.BlockSpec((1, dc), const),
                  pl.BlockSpec((1, ds), const)],
        out_specs=[row(dc), row(dc), row(ds), pl.BlockSpec((ds, ds), const),
                   pl.BlockSpec(conv_w.shape, const), pl.BlockSpec((1, dc), const),
                   pl.BlockSpec((1, ds), const)],
        out_shape=[jax.ShapeDtypeStruct((t, dc), BF16), jax.ShapeDtypeStruct((t, dc), F32),
                   jax.ShapeDtypeStruct((t, ds), F32), jax.ShapeDtypeStruct((ds, ds), F32),
                   jax.ShapeDtypeStruct(conv_w.shape, F32), jax.ShapeDtypeStruct((1, dc), F32),
                   jax.ShapeDtypeStruct((1, ds), F32)],
        compiler_params=_params(),
    )(proj, proj, y, z, dmixed, w_glu, conv_w, gain_c, gain_s)


def _mix_bwd2(proj, dconv, conv_w, deps=()):
    t = proj.shape[0]
    dc = conv_w.shape[1]
    n_tiles = t // ROW_TILE
    per_tile = ROW_TILE // SUBLANES

    def body(c_ref, v_ref, d_ref, halo_ref, cw_ref, o_ref):
        i = pl.program_id(0)
        d = d_ref[...]
        u1, u2 = _anticausal_taps(d, halo_ref[...], i == n_tiles - 1)
        cw = cw_ref[...]
        dcv = cw[2:3] * d + cw[1:2] * u1 + cw[0:1] * u2
        o_ref[:, :dc] = (dcv * v_ref[...]).astype(BF16)
        o_ref[:, dc:] = (dcv * c_ref[...]).astype(BF16)

    return pl.pallas_call(
        _ignoring_deps(body, 5, deps), name="mix_bwd2", grid=(n_tiles,),
        in_specs=[pl.BlockSpec((ROW_TILE, dc), lambda i: (i, 1)),
                  pl.BlockSpec((ROW_TILE, dc), lambda i: (i, 2)),
                  pl.BlockSpec((ROW_TILE, dc), lambda i: (i, 0)),
                  pl.BlockSpec((SUBLANES, dc),
                               lambda i: (jnp.minimum((i + 1) * per_tile, t // SUBLANES - 1), 0)),
                  pl.BlockSpec(conv_w.shape, lambda i: (0, 0))] + [ANY_SPEC] * len(deps),
        out_specs=pl.BlockSpec((ROW_TILE, 2 * dc), lambda i: (i, 0)),
        out_shape=jax.ShapeDtypeStruct((t, 2 * dc), BF16), compiler_params=_params(),
    )(proj, proj, dconv, dconv, conv_w, *deps)


def _ffn_taps(up_ref, halo_ref, fw_ref, fb_ref, q, first):
    x0 = up_ref[q]
    x2, x1 = _causal_taps(x0, halo_ref[q], first)
    w = fw_ref[q]
    return w[0:1] * x2 + w[1:2] * x1 + w[2:3] * x0 + fb_ref[q], x2, x1, x0


def _ffn_fwd(up_pre, fw, fb):
    nb, t, w = up_pre.shape
    half = nb // 2

    def body(up_ref, halo_ref, fw_ref, fb_ref, act_ref):
        first = pl.program_id(0) == 0
        for q in range(half):
            a = _ffn_taps(up_ref, halo_ref, fw_ref, fb_ref, q, first)[0]
            val = _ffn_taps(up_ref, halo_ref, fw_ref, fb_ref, q + half, first)[0]
            act_ref[q] = (a * jax.nn.sigmoid(a) * val).astype(BF16)

    return pl.pallas_call(
        body, name="ffn_fwd", grid=(t // ROW_TILE,),
        in_specs=[pl.BlockSpec((nb, ROW_TILE, w), lambda i: (0, i, 0)),
                  pl.BlockSpec((nb, SUBLANES, w), lambda i: (0, _prev_halo(i), 0)),
                  pl.BlockSpec(fw.shape, lambda i: (0, 0, 0)),
                  pl.BlockSpec(fb.shape, lambda i: (0, 0, 0))],
        out_specs=pl.BlockSpec((half, ROW_TILE, w), lambda i: (0, i, 0)),
        out_shape=jax.ShapeDtypeStruct((half, t, w), BF16), compiler_params=_params(),
    )(up_pre, up_pre, fw, fb)


def _ffn_bwd1(up_pre, dact, fw, fb, deps=()):
    nb, t, w = up_pre.shape
    half = nb // 2

    def body(up_ref, halo_ref, dact_ref, fw_ref, fb_ref, dup_ref, dfw_ref, dfb_ref):
        first = pl.program_id(0) == 0

        @pl.when(first)
        def _():
            dfw_ref[...] = jnp.zeros_like(dfw_ref)
            dfb_ref[...] = jnp.zeros_like(dfb_ref)

        for q in range(half):
            taps_a = _ffn_taps(up_ref, halo_ref, fw_ref, fb_ref, q, first)
            taps_v = _ffn_taps(up_ref, halo_ref, fw_ref, fb_ref, q + half, first)
            a, val = taps_a[0], taps_v[0]
            da_ct = dact_ref[q]
            sg = jax.nn.sigmoid(a)
            da = da_ct * val * sg * (1.0 + a * (1.0 - sg))
            dval = da_ct * a * sg
            for blk, dv, taps in ((q, da, taps_a), (q + half, dval, taps_v)):
                dup_ref[blk] = dv
                dfb_ref[blk] += jnp.sum(dv, axis=0, keepdims=True)
                for k in range(3):
                    dfw_ref[blk, k:k + 1, :] += jnp.sum(dv * taps[1 + k], axis=0, keepdims=True)

    return pl.pallas_call(
        _ignoring_deps(body, 5, deps), name="ffn_bwd1", grid=(t // ROW_TILE,),
        in_specs=[pl.BlockSpec((nb, ROW_TILE, w), lambda i: (0, i, 0)),
                  pl.BlockSpec((nb, SUBLANES, w), lambda i: (0, _prev_halo(i), 0)),
                  pl.BlockSpec((half, ROW_TILE, w), lambda i: (0, i, 0)),
                  pl.BlockSpec(fw.shape, lambda i: (0, 0, 0)),
                  pl.BlockSpec(fb.shape, lambda i: (0, 0, 0))] + [ANY_SPEC] * len(deps),
        out_specs=[pl.BlockSpec((nb, ROW_TILE, w), lambda i: (0, i, 0)),
                   pl.BlockSpec(fw.shape, lambda i: (0, 0, 0)),
                   pl.BlockSpec(fb.shape, lambda i: (0, 0, 0))],
        out_shape=[jax.ShapeDtypeStruct((nb, t, w), F32), jax.ShapeDtypeStruct(fw.shape, F32),
                   jax.ShapeDtypeStruct(fb.shape, F32)],
        compiler_params=_params(),
    )(up_pre, up_pre, dact, fw, fb, *deps)


def _ffn_bwd2(dup, fw):
    nb, t, w = dup.shape
    n_tiles = t // ROW_TILE
    per_tile = ROW_TILE // SUBLANES

    def body(d_ref, halo_ref, fw_ref, o_ref):
        last = pl.program_id(0) == n_tiles - 1
        for q in range(nb):
            d = d_ref[q]
            u1, u2 = _anticausal_taps(d, halo_ref[q], last)
            cw = fw_ref[q]
            o_ref[q] = (cw[2:3] * d + cw[1:2] * u1 + cw[0:1] * u2).astype(BF16)

    return pl.pallas_call(
        body, name="ffn_bwd2", grid=(n_tiles,),
        in_specs=[pl.BlockSpec((nb, ROW_TILE, w), lambda i: (0, i, 0)),
                  pl.BlockSpec((nb, SUBLANES, w),
                               lambda i: (0, jnp.minimum((i + 1) * per_tile, t // SUBLANES - 1), 0)),
                  pl.BlockSpec(fw.shape, lambda i: (0, 0, 0))],
        out_specs=pl.BlockSpec((nb, ROW_TILE, w), lambda i: (0, i, 0)),
        out_shape=jax.ShapeDtypeStruct((nb, t, w), BF16), compiler_params=_params(),
    )(dup, dup, fw)


def _to_segments(a):
    t, c = a.shape
    return a.reshape(N_SEG, t // N_SEG, c).transpose(1, 0, 2).reshape(t, c)


def _from_segments(a):
    t, c = a.shape
    return a.reshape(t // N_SEG, N_SEG, c).transpose(1, 0, 2).reshape(t, c)


def _cmul(ar, ai, br, bi):
    return ar * br - ai * bi, ar * bi + ai * br


def _segment_carries(pr, pi, fr, fi, forward):
    row = lax.broadcasted_iota(jnp.int32, fr.shape, 0)
    edge = row == (0 if forward else N_SEG - 1)
    shift = 1 if forward else N_SEG - 1
    sr, si = jnp.zeros_like(fr), jnp.zeros_like(fi)
    for _ in range(N_SEG - 1):
        tr, ti = _cmul(pr, pi, sr, si)
        sr = jnp.where(edge, 0.0, pltpu.roll(tr + fr, shift, 0))
        si = jnp.where(edge, 0.0, pltpu.roll(ti + fi, shift, 0))
    return sr, si


def _rows(i):
    return pl.ds(pl.multiple_of(i * SUBLANES, SUBLANES), SUBLANES)


def _s5_fwd(u, bb_re, bb_im, a_re, a_im, cc_re, cc_im, d_skip):
    t, ds = u.shape
    nch, _, cs = bb_re.shape
    steps = t // N_SEG

    def body(u_ref, bbr_ref, bbi_ref, ar_ref, ai_ref, ccr_ref, cci_ref, d_ref, sr_ref, si_ref, y_ref):
        ub = u_ref[...].astype(BF16)
        sr_ref[...] = _dot(ub, bbr_ref[...], NN)
        si_ref[...] = _dot(ub, bbi_ref[...], NN)
        ar = jnp.broadcast_to(ar_ref[...], (N_SEG, cs))
        ai = jnp.broadcast_to(ai_ref[...], (N_SEG, cs))
        zero = jnp.zeros((N_SEG, cs), F32)

        def totals(i, carry):
            sr, si, pr, pi = carry
            tr, ti = _cmul(ar, ai, sr, si)
            qr, qi = _cmul(ar, ai, pr, pi)
            return tr + sr_ref[_rows(i), :], ti + si_ref[_rows(i), :], qr, qi

        fr, fi, pr, pi = lax.fori_loop(0, steps, totals, (zero, zero, zero + 1.0, zero))
        s0r, s0i = _segment_carries(pr, pi, fr, fi, True)

        def scan(i, carry):
            tr, ti = _cmul(ar, ai, *carry)
            nr, ni = tr + sr_ref[_rows(i), :], ti + si_ref[_rows(i), :]
            sr_ref[_rows(i), :] = nr
            si_ref[_rows(i), :] = ni
            return nr, ni

        lax.fori_loop(0, steps, scan, (s0r, s0i))
        y_ref[...] = (_dot(sr_ref[...].astype(BF16), ccr_ref[...], NN)
                      - _dot(si_ref[...].astype(BF16), cci_ref[...], NN)
                      + d_ref[...] * u_ref[...])

    chunk3 = lambda r, c: pl.BlockSpec((None, r, c), lambda j: (j, 0, 0))
    return pl.pallas_call(
        body, name="s5_fwd", grid=(nch,),
        in_specs=[pl.BlockSpec((t, SSM_CHUNK), lambda j: (0, j)),
                  chunk3(SSM_CHUNK, cs), chunk3(SSM_CHUNK, cs), chunk3(1, cs), chunk3(1, cs),
                  chunk3(cs, SSM_CHUNK), chunk3(cs, SSM_CHUNK), chunk3(1, SSM_CHUNK)],
        out_specs=[pl.BlockSpec((t, cs), lambda j: (0, j)), pl.BlockSpec((t, cs), lambda j: (0, j)),
                   pl.BlockSpec((t, SSM_CHUNK), lambda j: (0, j))],
        out_shape=[jax.ShapeDtypeStruct((t, nch * cs), F32), jax.ShapeDtypeStruct((t, nch * cs), F32),
                   jax.ShapeDtypeStruct((t, ds), F32)],
        compiler_params=_params(),
    )(u, bb_re, bb_im, a_re, a_im, cc_re, cc_im, d_skip)


def _s5_bwd(dy, u, s_re, s_im, bb_re, bb_im, a_re, a_im, cc_re, cc_im, d_skip):
    t, ds = u.shape
    nch, _, cs = bb_re.shape
    steps = t // N_SEG

    def body(dy_ref, u_ref, sr_ref, si_ref, bbr_ref, bbi_ref, ar_ref, ai_ref, ccr_ref, cci_ref, d_ref,
             du_ref, dbbr_ref, dbbi_ref, dar_ref, dai_ref, dccr_ref, dcci_ref, dd_ref, gr_ref, gi_ref):
        dyv = dy_ref[...]
        dyb = dyv.astype(BF16)
        gr_ref[...] = _dot(dyb, ccr_ref[...], NT)
        gi_ref[...] = -_dot(dyb, cci_ref[...], NT)
        ar = jnp.broadcast_to(ar_ref[...], (N_SEG, cs))
        ai = -jnp.broadcast_to(ai_ref[...], (N_SEG, cs))
        zero = jnp.zeros((N_SEG, cs), F32)

        def totals(k, carry):
            i = steps - 1 - k
            gr, gi, pr, pi = carry
            tr, ti = _cmul(ar, ai, gr, gi)
            qr, qi = _cmul(ar, ai, pr, pi)
            return tr + gr_ref[_rows(i), :], ti + gi_ref[_rows(i), :], qr, qi

        fr, fi, pr, pi = lax.fori_loop(0, steps, totals, (zero, zero, zero + 1.0, zero))
        e0r, e0i = _segment_carries(pr, pi, fr, fi, False)

        def step(i, gr, gi, pr, pi, acc_r, acc_i):
            tr, ti = _cmul(ar, ai, gr, gi)
            nr, ni = tr + gr_ref[_rows(i), :], ti + gi_ref[_rows(i), :]
            gr_ref[_rows(i), :] = nr
            gi_ref[_rows(i), :] = ni
            return nr, ni, acc_r + nr * pr + ni * pi, acc_i + ni * pr - nr * pi

        def scan(k, carry):
            i = steps - 1 - k
            gr, gi, acc_r, acc_i = carry
            return step(i, gr, gi, sr_ref[_rows(i - 1), :], si_ref[_rows(i - 1), :], acc_r, acc_i)

        gr, gi, acc_r, acc_i = lax.fori_loop(0, steps - 1, scan, (e0r, e0i, zero, zero))
        row = lax.broadcasted_iota(jnp.int32, (N_SEG, cs), 0)
        last = _rows(steps - 1)
        pr = jnp.where(row == 0, 0.0, pltpu.roll(sr_ref[last, :], 1, 0))
        pi = jnp.where(row == 0, 0.0, pltpu.roll(si_ref[last, :], 1, 0))
        _, _, acc_r, acc_i = step(0, gr, gi, pr, pi, acc_r, acc_i)
        dar_ref[...] = jnp.sum(acc_r, axis=0, keepdims=True)
        dai_ref[...] = jnp.sum(acc_i, axis=0, keepdims=True)

        uv = u_ref[...]
        ub = uv.astype(BF16)
        grb = gr_ref[...].astype(BF16)
        gib = gi_ref[...].astype(BF16)
        du_ref[...] = d_ref[...] * dyv + _dot(grb, bbr_ref[...], NT) + _dot(gib, bbi_ref[...], NT)
        dbbr_ref[...] = _dot(ub, grb, TN)
        dbbi_ref[...] = _dot(ub, gib, TN)
        dccr_ref[...] = _dot(sr_ref[...].astype(BF16), dyb, TN)
        dcci_ref[...] = -_dot(si_ref[...].astype(BF16), dyb, TN)
        dd_ref[...] = jnp.sum(dyv * uv, axis=0, keepdims=True)

    chunk3 = lambda r, c: pl.BlockSpec((None, r, c), lambda j: (j, 0, 0))
    cols = lambda w: pl.BlockSpec((t, w), lambda j: (0, j))
    return pl.pallas_call(
        body, name="s5_bwd", grid=(nch,),
        in_specs=[cols(SSM_CHUNK), cols(SSM_CHUNK), cols(cs), cols(cs),
                  chunk3(SSM_CHUNK, cs), chunk3(SSM_CHUNK, cs), chunk3(1, cs), chunk3(1, cs),
                  chunk3(cs, SSM_CHUNK), chunk3(cs, SSM_CHUNK), chunk3(1, SSM_CHUNK)],
        out_specs=[cols(SSM_CHUNK), chunk3(SSM_CHUNK, cs), chunk3(SSM_CHUNK, cs), chunk3(1, cs),
                   chunk3(1, cs), chunk3(cs, SSM_CHUNK), chunk3(cs, SSM_CHUNK), chunk3(1, SSM_CHUNK)],
        out_shape=[jax.ShapeDtypeStruct((t, ds), F32),
                   jax.ShapeDtypeStruct((nch, SSM_CHUNK, cs), F32), jax.ShapeDtypeStruct((nch, SSM_CHUNK, cs), F32),
                   jax.ShapeDtypeStruct((nch, 1, cs), F32), jax.ShapeDtypeStruct((nch, 1, cs), F32),
                   jax.ShapeDtypeStruct((nch, cs, SSM_CHUNK), F32), jax.ShapeDtypeStruct((nch, cs, SSM_CHUNK), F32),
                   jax.ShapeDtypeStruct((nch, 1, SSM_CHUNK), F32)],
        scratch_shapes=[pltpu.VMEM((t, cs), F32), pltpu.VMEM((t, cs), F32)],
        compiler_params=_params(),
    )(dy, u, s_re, s_im, bb_re, bb_im, a_re, a_im, cc_re, cc_im, d_skip)


def _discretize(lr, li, log_dt, br, bi):
    dt = jnp.exp(log_dt)
    mag = jnp.exp(lr * dt)
    ang = li * dt
    a_re = mag * jnp.cos(ang)
    a_im = mag * jnp.sin(ang)
    den = lr * lr + li * li
    nr = a_re - 1.0
    f_re = (nr * lr + a_im * li) / den
    f_im = (a_im * lr - nr * li) / den
    return a_re, a_im, f_re * br - f_im * bi, f_re * bi + f_im * br


def _whole(shape):
    return pl.BlockSpec(shape, lambda: (0,) * len(shape))


def _disc_fwd(lr, li, log_dt, br, bi, deps=()):
    def body(lr_ref, li_ref, dt_ref, br_ref, bi_ref, ar_ref, ai_ref, bbr_ref, bbi_ref):
        outs = _discretize(lr_ref[...], li_ref[...], dt_ref[...], br_ref[...], bi_ref[...])
        for ref, val in zip((ar_ref, ai_ref, bbr_ref, bbi_ref), outs):
            ref[...] = val

    args = (lr, li, log_dt, br, bi)
    outs = (lr, lr, br, br)
    return pl.pallas_call(
        _ignoring_deps(body, 5, deps), name="disc_fwd",
        in_specs=[_whole(a.shape) for a in args] + [ANY_SPEC] * len(deps),
        out_specs=[_whole(a.shape) for a in outs],
        out_shape=[jax.ShapeDtypeStruct(a.shape, F32) for a in outs],
    )(*args, *deps)


def _disc_bwd(lr, li, log_dt, br, bi, dar, dai, dbbr, dbbi):
    def body(lr_ref, li_ref, dt_ref, br_ref, bi_ref, dar_ref, dai_ref, dbbr_ref, dbbi_ref,
             dlr_ref, dli_ref, ddt_ref, dbr_ref, dbi_ref):
        _, vjp = jax.vjp(_discretize, lr_ref[...], li_ref[...], dt_ref[...], br_ref[...], bi_ref[...])
        grads = vjp((dar_ref[...], dai_ref[...], dbbr_ref[...], dbbi_ref[...]))
        for ref, val in zip((dlr_ref, dli_ref, ddt_ref, dbr_ref, dbi_ref), grads):
            ref[...] = val

    args = (lr, li, log_dt, br, bi, dar, dai, dbbr, dbbi)
    outs = (lr, li, log_dt, br, bi)
    return pl.pallas_call(
        body, name="disc_bwd", in_specs=[_whole(a.shape) for a in args],
        out_specs=[_whole(a.shape) for a in outs],
        out_shape=[jax.ShapeDtypeStruct(a.shape, F32) for a in outs],
    )(*args)


def _adamw(w, g, m, v):
    m = ADAM_B1 * m + (1.0 - ADAM_B1) * g
    v = ADAM_B2 * v + (1.0 - ADAM_B2) * (g * g)
    m_hat = m / ADAM_BC1
    v_hat = v / ADAM_BC2
    delta = -ADAM_LR * (m_hat / (jnp.sqrt(v_hat) + ADAM_EPS) + ADAM_WD * w)
    return delta, m, v


def _adamw_reduce(name, parts, w, m, v):
    _, r, c = parts.shape
    tr = r
    for cand in (256, 176, 128):
        if r % cand == 0:
            tr = cand
            break

    def body(p_ref, w_ref, m_ref, v_ref, g_ref, d_ref, nm_ref, nv_ref):
        g = p_ref[0].astype(F32)
        for k in range(1, N_DEV):
            g = g + p_ref[k].astype(F32)
        delta, nm, nv = _adamw(w_ref[...], g, m_ref[...], v_ref[...])
        g_ref[...] = g
        d_ref[...] = delta
        nm_ref[...] = nm
        nv_ref[...] = nv

    blk = pl.BlockSpec((tr, c), lambda i: (i, 0))
    return pl.pallas_call(
        body, name=name, grid=(r // tr,),
        in_specs=[pl.BlockSpec((N_DEV, tr, c), lambda i: (0, i, 0)), blk, blk, blk],
        out_specs=[blk] * 4, out_shape=[jax.ShapeDtypeStruct((r, c), F32)] * 4,
        compiler_params=_params(),
    )(parts, w, m, v)


def _pack(arrays, rows):
    flat = jnp.concatenate([a.reshape(-1) for a in arrays])
    return jnp.pad(flat, (0, rows * LANES - flat.shape[0])).reshape(rows, LANES)


def _unpack(packed, shapes):
    flat = packed.reshape(-1)
    out, off = [], 0
    for s in shapes:
        n = math.prod(s)
        out.append(flat[off:off + n].reshape(s))
        off += n
    return out


def _packed_rows(shapes):
    n = sum(math.prod(s) for s in shapes)
    return -(-n // (SUBLANES * LANES)) * SUBLANES


def _block_diag(x):
    j, g, r, c = x.shape
    eye = jnp.eye(g, dtype=x.dtype)
    return (x[:, :, :, None, :] * eye[None, :, None, :, None]).reshape(j, g * r, g * c)


def _diag_blocks(x, g):
    j, gr, gc = x.shape
    r, c = gr // g, gc // g
    eye = jnp.eye(g, dtype=x.dtype)
    return (x.reshape(j, g, r, g, c) * eye[None, :, None, :, None]).sum(axis=3)


def kernel(x, meta_tokens, norm_mix_g, w_in, conv_w, ssm_lam_re, ssm_lam_im, ssm_log_dt, ssm_b_re, ssm_b_im, ssm_c_re, ssm_c_im, ssm_d, ssm_w_glu, gain_conv_out, gain_ssm_out, w_out, norm_ffn_g, w_up, ffn_conv_w, ffn_conv_b, w_down, norm_final_g, loss_target, m_meta_tokens, m_norm_mix_g, m_w_in, m_conv_w, m_ssm_lam_re, m_ssm_lam_im, m_ssm_log_dt, m_ssm_b_re, m_ssm_b_im, m_ssm_c_re, m_ssm_c_im, m_ssm_d, m_ssm_w_glu, m_gain_conv_out, m_gain_ssm_out, m_w_out, m_norm_ffn_g, m_w_up, m_ffn_conv_w, m_ffn_conv_b, m_w_down, m_norm_final_g, v_meta_tokens, v_norm_mix_g, v_w_in, v_conv_w, v_ssm_lam_re, v_ssm_lam_im, v_ssm_log_dt, v_ssm_b_re, v_ssm_b_im, v_ssm_c_re, v_ssm_c_im, v_ssm_d, v_ssm_w_glu, v_gain_conv_out, v_gain_ssm_out, v_w_out, v_norm_ffn_g, v_w_up, v_ffn_conv_w, v_ffn_conv_b, v_w_down, v_norm_final_g):
    weights = dict(meta_tokens=meta_tokens, norm_mix_g=norm_mix_g, w_in=w_in, conv_w=conv_w, ssm_lam_re=ssm_lam_re, ssm_lam_im=ssm_lam_im, ssm_log_dt=ssm_log_dt, ssm_b_re=ssm_b_re, ssm_b_im=ssm_b_im, ssm_c_re=ssm_c_re, ssm_c_im=ssm_c_im, ssm_d=ssm_d, ssm_w_glu=ssm_w_glu, gain_conv_out=gain_conv_out, gain_ssm_out=gain_ssm_out, w_out=w_out, norm_ffn_g=norm_ffn_g, w_up=w_up, ffn_conv_w=ffn_conv_w, ffn_conv_b=ffn_conv_b, w_down=w_down, norm_final_g=norm_final_g)
    mom_m = dict(meta_tokens=m_meta_tokens, norm_mix_g=m_norm_mix_g, w_in=m_w_in, conv_w=m_conv_w, ssm_lam_re=m_ssm_lam_re, ssm_lam_im=m_ssm_lam_im, ssm_log_dt=m_ssm_log_dt, ssm_b_re=m_ssm_b_re, ssm_b_im=m_ssm_b_im, ssm_c_re=m_ssm_c_re, ssm_c_im=m_ssm_c_im, ssm_d=m_ssm_d, ssm_w_glu=m_ssm_w_glu, gain_conv_out=m_gain_conv_out, gain_ssm_out=m_gain_ssm_out, w_out=m_w_out, norm_ffn_g=m_norm_ffn_g, w_up=m_w_up, ffn_conv_w=m_ffn_conv_w, ffn_conv_b=m_ffn_conv_b, w_down=m_w_down, norm_final_g=m_norm_final_g)
    mom_v = dict(meta_tokens=v_meta_tokens, norm_mix_g=v_norm_mix_g, w_in=v_w_in, conv_w=v_conv_w, ssm_lam_re=v_ssm_lam_re, ssm_lam_im=v_ssm_lam_im, ssm_log_dt=v_ssm_log_dt, ssm_b_re=v_ssm_b_re, ssm_b_im=v_ssm_b_im, ssm_c_re=v_ssm_c_re, ssm_c_im=v_ssm_c_im, ssm_d=v_ssm_d, ssm_w_glu=v_ssm_w_glu, gain_conv_out=v_gain_conv_out, gain_ssm_out=v_gain_ssm_out, w_out=v_w_out, norm_ffn_g=v_norm_ffn_g, w_up=v_w_up, ffn_conv_w=v_ffn_conv_w, ffn_conv_b=v_ffn_conv_b, w_down=v_w_down, norm_final_g=v_norm_final_g)
    names = list(weights)

    n_meta, d_meta = meta_tokens.shape
    seq, d = x.shape[1], x.shape[2]
    rows_used = n_meta + seq
    t = -(-rows_used // ROW_TILE) * ROW_TILE
    d_in_s = w_in.shape[2]
    dc_s = conv_w.shape[2]
    dc = dc_s * N_DEV
    ds = ssm_w_glu.shape[2]
    n_groups, n_state, grp = ssm_b_re.shape[1:]
    ns = n_groups * n_state
    nch = ds // SSM_CHUNK
    gpc = n_groups // nch
    ff_s = w_up.shape[2]
    dn_s = w_down.shape[1]
    assert 3 * dc + ds == d_in_s * N_DEV and 2 * dn_s == ff_s and t % (N_SEG * SUBLANES) == 0

    small_shard = jnp.concatenate([meta_tokens.reshape(-1), conv_w.reshape(-1), ffn_conv_w.reshape(-1)])
    n_small = small_shard.shape[0]
    small_rows = -(-n_small // LANES)
    small_shard = jnp.pad(small_shard, (0, small_rows * LANES - n_small)).reshape(small_rows, LANES)
    ag, ag_token = _send_start("gather_weights_start", [
        small_shard, w_in[0].astype(BF16), ssm_w_glu[0].astype(BF16), w_out[0].astype(BF16),
        w_up[0].astype(BF16), w_down[0].astype(BF16)], gather=True)
    fb = ffn_conv_b.reshape(N_DEV, 1, ff_s)

    col = lambda a: a.reshape(ns, 1)
    lr, li = col(ssm_lam_re), col(ssm_lam_im)
    log_dt_e = jnp.broadcast_to(ssm_log_dt.reshape(n_groups, 1), (n_groups, n_state)).reshape(ns, 1)
    br, bi = ssm_b_re.reshape(ns, grp), ssm_b_im.reshape(ns, grp)
    a_re, a_im, bb_re, bb_im = _disc_fwd(lr, li, log_dt_e, br, bi, deps=(ag_token,))
    cs = gpc * n_state
    chunk_row = lambda a: a.reshape(nch, 1, cs)
    to_bb = lambda a: _block_diag(a.reshape(nch, gpc, n_state, grp).transpose(0, 1, 3, 2)).astype(BF16)
    to_cc = lambda a: _block_diag(a.reshape(nch, gpc, grp, n_state).transpose(0, 1, 3, 2)).astype(BF16)
    bbm_re, bbm_im = to_bb(bb_re), to_bb(bb_im)
    ccm_re, ccm_im = to_cc(ssm_c_re), to_cc(ssm_c_im)
    a_re_c, a_im_c = chunk_row(a_re), chunk_row(a_im)
    d_skip = ssm_d.reshape(nch, 1, SSM_CHUNK)

    g_small, g_in = _send_wait("gather_weights_wait_in", ag, (0, 1), bbm_im)
    g_small = g_small.reshape(N_DEV, -1)
    o1 = n_meta * d_meta
    o2 = o1 + 3 * dc_s
    meta_full = g_small[:, :o1].reshape(N_DEV, n_meta, d_meta).transpose(1, 0, 2).reshape(n_meta, d)
    conv_w_f = g_small[:, o1:o2].reshape(N_DEV, 3, dc_s).transpose(1, 0, 2).reshape(3, dc)
    fw = g_small[:, o2:o2 + 3 * ff_s].reshape(N_DEV, 3, ff_s)
    h0 = jnp.concatenate([meta_full, x[0], jnp.zeros((t - rows_used, d), F32)], axis=0)
    target = jnp.pad(loss_target[0], ((n_meta, t - rows_used), (0, 0)))
    full_t = lambda w: pl.BlockSpec((t, w), lambda *_: (0, 0))

    hn1 = _rms_fwd("norm_mix", h0, norm_mix_g)
    proj = _mm("proj", hn1, g_in, dims=NN, grid=(N_DEV,), a_spec=full_t(d),
               b_spec=pl.BlockSpec((None, d, d_in_s), lambda j: (j, 0, 0)),
               o_spec=pl.BlockSpec((t, d_in_s), lambda j: (0, j)),
               out_shape=jax.ShapeDtypeStruct((t, N_DEV * d_in_s), F32))
    u_seg = _to_segments(proj[:, 3 * dc:])
    s_re, s_im, y_seg = _s5_fwd(u_seg, bbm_re, bbm_im, a_re_c, a_im_c, ccm_re, ccm_im, d_skip)
    y_ssm = _from_segments(y_seg)
    g_glu, g_out = _send_wait("gather_weights_wait_mix", ag, (2, 3), y_ssm)
    w_out_f = g_out.reshape(-1, d)
    w_glu_f = g_glu.reshape(ds, ds)
    mixed, z_glu = _mix_fwd(proj, y_ssm, w_glu_f, conv_w_f, gain_conv_out, gain_ssm_out)
    tn_out = 256
    h1 = _mm("out_proj", mixed, w_out_f, dims=NN, grid=(d // tn_out,), a_spec=full_t(dc + ds),
             b_spec=pl.BlockSpec((dc + ds, tn_out), lambda j: (0, j)),
             o_spec=pl.BlockSpec((t, tn_out), lambda j: (0, j)),
             out_shape=jax.ShapeDtypeStruct((t, d), F32),
             res=h0, res_spec=pl.BlockSpec((t, tn_out), lambda j: (0, j)))
    hn2 = _rms_fwd("norm_ffn", h1, norm_ffn_g)
    (g_up,) = _send_wait("gather_weights_wait_up", ag, (4,), hn2)
    up_pre = _mm("up_proj", hn2, g_up, dims=NN, grid=(N_DEV,), a_spec=full_t(d),
                 b_spec=pl.BlockSpec((None, d, ff_s), lambda j: (j, 0, 0)),
                 o_spec=pl.BlockSpec((None, t, ff_s), lambda j: (j, 0, 0)),
                 out_shape=jax.ShapeDtypeStruct((N_DEV, t, ff_s), F32))
    act = _ffn_fwd(up_pre, fw, fb)
    nhalf = N_DEV // 2
    (g_down,) = _send_wait("gather_weights_wait_down", ag, (5,), act)
    w_down_f = g_down.reshape(nhalf, 2 * dn_s, d)
    h2 = _mm("down_proj", act, w_down_f, dims=NN, grid=(d // tn_out, nhalf),
             a_spec=pl.BlockSpec((None, t, ff_s), lambda j, k: (k, 0, 0)),
             b_spec=pl.BlockSpec((None, ff_s, tn_out), lambda j, k: (k, 0, j)),
             o_spec=pl.BlockSpec((t, tn_out), lambda j, k: (0, j)),
             out_shape=jax.ShapeDtypeStruct((t, d), F32), acc_shape=(t, tn_out),
             res=h1, res_spec=pl.BlockSpec((t, tn_out), lambda j, k: (0, j)))

    dh2, dh2_b, loss_part, d_norm_final = _loss_bwd(h2, target, norm_final_g.reshape(1, d), n_meta, rows_used)
    dact = _mm("down_dgrad", dh2_b, w_down_f, dims=NT, grid=(nhalf,), a_spec=full_t(d),
               b_spec=pl.BlockSpec((None, ff_s, d), lambda j: (j, 0, 0)),
               o_spec=pl.BlockSpec((None, t, ff_s), lambda j: (j, 0, 0)),
               out_shape=jax.ShapeDtypeStruct((nhalf, t, ff_s), F32))
    tn_w = 512
    dw_down = _mm("down_wgrad", act, dh2_b, dims=TN, grid=(nhalf, d // tn_w),
                  a_spec=pl.BlockSpec((None, t, ff_s), lambda j, n: (j, 0, 0)),
                  b_spec=pl.BlockSpec((t, tn_w), lambda j, n: (0, n)),
                  o_spec=pl.BlockSpec((None, ff_s, tn_w), lambda j, n: (j, 0, n)),
                  out_shape=jax.ShapeDtypeStruct((nhalf, ff_s, d), BF16))
    ex_down, token = _send_start("exchange_down_start", [dw_down.reshape(N_DEV, dn_s, d)], gather=False)
    dup, d_fw, d_fb = _ffn_bwd1(up_pre, dact, fw, fb, deps=(token,))
    dup_pre = _ffn_bwd2(dup, fw)
    dhn2 = _mm("up_dgrad", dup_pre, g_up, dims=NT, grid=(d // tn_w, N_DEV),
               a_spec=pl.BlockSpec((None, t, ff_s), lambda i, k: (k, 0, 0)),
               b_spec=pl.BlockSpec((None, tn_w, ff_s), lambda i, k: (k, i, 0)),
               o_spec=pl.BlockSpec((t, tn_w), lambda i, k: (0, i)),
               out_shape=jax.ShapeDtypeStruct((t, d), F32), acc_shape=(t, tn_w))
    dw_up = _mm("up_wgrad", hn2, dup_pre, dims=TN, grid=(N_DEV, d // tn_w),
                a_spec=pl.BlockSpec((t, tn_w), lambda j, i: (0, i)),
                b_spec=pl.BlockSpec((None, t, ff_s), lambda j, i: (j, 0, 0)),
                o_spec=pl.BlockSpec((None, tn_w, ff_s), lambda j, i: (j, i, 0)),
                out_shape=jax.ShapeDtypeStruct((N_DEV, d, ff_s), BF16))
    ex_up, token = _send_start("exchange_up_start", [dw_up], gather=False)
    dh1, dh1_b, d_norm_ffn = _rms_bwd_res("norm_ffn_bwd", dh2, dhn2, h1, norm_ffn_g, deps=(token,))
    dmixed = _mm("out_dgrad", dh1_b, w_out_f, dims=NT, grid=((dc + ds) // tn_out,), a_spec=full_t(d),
                 b_spec=pl.BlockSpec((tn_out, d), lambda i: (i, 0)),
                 o_spec=pl.BlockSpec((t, tn_out), lambda i: (0, i)),
                 out_shape=jax.ShapeDtypeStruct((t, dc + ds), F32))
    dw_out = _mm("out_wgrad", mixed, dh1_b, dims=TN, grid=((dc + ds) // tn_out,),
                 a_spec=pl.BlockSpec((t, tn_out), lambda i: (0, i)), b_spec=full_t(d),
                 o_spec=pl.BlockSpec((tn_out, d), lambda i: (i, 0)),
                 out_shape=jax.ShapeDtypeStruct((dc + ds, d), BF16))
    db_gate, dconv, dy_ssm, d_wglu, d_conv_w, d_gain_c, d_gain_s = _mix_bwd1(
        proj, y_ssm, z_glu, dmixed, w_glu_f, conv_w_f, gain_conv_out, gain_ssm_out)
    ex_mix, token = _send_start("exchange_mix_start", [
        dw_out.reshape(N_DEV, -1, d), d_wglu.astype(BF16).reshape(N_DEV, -1, ds)], gather=False)
    dcdv = _mix_bwd2(proj, dconv, conv_w_f, deps=(token,))
    (du_seg, d_bbm_re, d_bbm_im, d_a_re, d_a_im, d_ccm_re, d_ccm_im, d_dskip) = _s5_bwd(
        _to_segments(dy_ssm), u_seg, s_re, s_im, bbm_re, bbm_im, a_re_c, a_im_c, ccm_re, ccm_im, d_skip)
    dproj = jnp.concatenate([db_gate, dcdv, _from_segments(du_seg).astype(BF16)], axis=1)
    dhn1 = _mm("proj_dgrad", dproj, g_in, dims=NT, grid=(d // tn_w, N_DEV),
               a_spec=pl.BlockSpec((t, d_in_s), lambda i, k: (0, k)),
               b_spec=pl.BlockSpec((None, tn_w, d_in_s), lambda i, k: (k, i, 0)),
               o_spec=pl.BlockSpec((t, tn_w), lambda i, k: (0, i)),
               out_shape=jax.ShapeDtypeStruct((t, d), F32), acc_shape=(t, tn_w))
    dh0, _, d_norm_mix = _rms_bwd_res("norm_mix_bwd", dh1, dhn1, h0, norm_mix_g)
    dw_in = _mm("proj_wgrad", hn1, dproj, dims=TN, grid=(N_DEV, d // tn_w),
                a_spec=pl.BlockSpec((t, tn_w), lambda j, i: (0, i)),
                b_spec=pl.BlockSpec((t, d_in_s), lambda j, i: (0, j)),
                o_spec=pl.BlockSpec((None, tn_w, d_in_s), lambda j, i: (j, i, 0)),
                out_shape=jax.ShapeDtypeStruct((N_DEV, d, d_in_s), BF16))
    grad_x = dh0[n_meta:rows_used][None]
    loss = lax.psum(loss_part[0, 0], ("x", "y", "c"))
    d_meta_b = dh0[:n_meta].reshape(n_meta, N_DEV, d_meta).transpose(1, 0, 2)
    d_conv_w_b = d_conv_w.reshape(3, N_DEV, dc_s).transpose(1, 0, 2)
    small_blocks = jnp.concatenate([d_meta_b.reshape(N_DEV, -1), d_conv_w_b.reshape(N_DEV, -1),
                                    d_fw.reshape(N_DEV, -1)], axis=1)
    small_blocks = jnp.pad(small_blocks, ((0, 0), (0, small_rows * LANES - n_small)))
    small_blocks = small_blocks.reshape(N_DEV, small_rows, LANES)
    ex_in, ex_in_token = _send_start("exchange_in_start", [dw_in, small_blocks], gather=False)

    from_bb = lambda a: _diag_blocks(a, gpc).transpose(0, 1, 3, 2).reshape(ns, grp)
    from_cc = lambda a: _diag_blocks(a, gpc).transpose(0, 1, 3, 2).reshape(1, n_groups, grp, n_state)
    d_lr, d_li, d_dt_e, d_br, d_bi = _disc_bwd(
        lr, li, log_dt_e, br, bi, d_a_re.reshape(ns, 1), d_a_im.reshape(ns, 1),
        from_bb(d_bbm_re), from_bb(d_bbm_im))

    rep_grads = dict(
        norm_mix_g=d_norm_mix, ssm_lam_re=d_lr.reshape(ssm_lam_re.shape), ssm_lam_im=d_li.reshape(ssm_lam_im.shape),
        ssm_log_dt=d_dt_e.reshape(n_groups, n_state).sum(axis=1).reshape(ssm_log_dt.shape),
        ssm_b_re=d_br.reshape(ssm_b_re.shape), ssm_b_im=d_bi.reshape(ssm_b_im.shape),
        ssm_c_re=from_cc(d_ccm_re), ssm_c_im=from_cc(d_ccm_im), ssm_d=d_dskip.reshape(ssm_d.shape),
        gain_conv_out=d_gain_c, gain_ssm_out=d_gain_s, norm_ffn_g=d_norm_ffn,
        ffn_conv_b=d_fb.reshape(ffn_conv_b.shape), norm_final_g=d_norm_final.reshape(norm_final_g.shape))
    rep_names = list(rep_grads)
    rep_shapes = [weights[n].shape for n in rep_names]
    rep_rows = _packed_rows(rep_shapes)
    (rep_parts,) = _allgather("allgather_small_grads", [_pack([rep_grads[n] for n in rep_names], rep_rows)],
                              deps=(ex_in_token,))
    rep_out = _adamw_reduce("adamw_replicated", rep_parts,
                            _pack([weights[n] for n in rep_names], rep_rows),
                            _pack([mom_m[n] for n in rep_names], rep_rows),
                            _pack([mom_v[n] for n in rep_names], rep_rows))
    rep_out = [_unpack(o, rep_shapes) for o in rep_out]

    shard_out = {}

    def update(n, parts):
        res = _adamw_reduce("adamw_" + n, parts, weights[n][0], mom_m[n][0], mom_v[n][0])
        shard_out[n] = [r.reshape(weights[n].shape) for r in res]
        return res[0]

    (p_down,) = _send_wait("exchange_down_wait", ex_down, (0,), rep_out[0][0])
    done = update("w_down", p_down)
    (p_up,) = _send_wait("exchange_up_wait", ex_up, (0,), done)
    done = update("w_up", p_up)
    p_out, p_glu = _send_wait("exchange_mix_wait", ex_mix, (0, 1), done)
    update("w_out", p_out)
    done = update("ssm_w_glu", p_glu)
    p_in, p_small = _send_wait("exchange_in_wait", ex_in, (0, 1), done)
    update("w_in", p_in)
    small_names = ["meta_tokens", "conv_w", "ffn_conv_w"]
    small_shapes = [weights[n].shape for n in small_names]
    pack_small = lambda tree: jnp.pad(jnp.concatenate([tree[n].reshape(-1) for n in small_names]),
                                      (0, small_rows * LANES - n_small)).reshape(small_rows, LANES)
    res = _adamw_reduce("adamw_small_shards", p_small, pack_small(weights), pack_small(mom_m), pack_small(mom_v))
    res = [_unpack(r, small_shapes) for r in res]
    for i, n in enumerate(small_names):
        shard_out[n] = [r[i] for r in res]
    for i, n in enumerate(rep_names):
        shard_out[n] = [r[i] for r in rep_out]

    grads = [shard_out[n][0] for n in names]
    deltas = [shard_out[n][1] for n in names]
    new_m = [shard_out[n][2] for n in names]
    new_v = [shard_out[n][3] for n in names]
    return (loss, grad_x, *grads, *deltas, *new_m, *new_v)
```

```python
import functools
import math

import jax
import jax.numpy as jnp
from jax import lax
from jax.experimental import pallas as pl
from jax.experimental.pallas import tpu as pltpu

F32 = jnp.float32
BF16 = jnp.bfloat16
MESH = pl.DeviceIdType.MESH

N_DEV = 8
RMS_EPS = 1e-6
ADAM_LR = 0.001
ADAM_B1 = 0.9
ADAM_B2 = 0.999
ADAM_EPS = 1e-08
ADAM_WD = 0.01
ADAM_STEP = 10
ADAM_BC1 = 1.0 - ADAM_B1 ** ADAM_STEP
ADAM_BC2 = 1.0 - ADAM_B2 ** ADAM_STEP

SUBLANES = 8
LANES = 128
ROW_TILE = 128
N_SEG = 8
SSM_CHUNK = 128
VMEM_LIMIT = 48 * 1024 * 1024

NN = ((1,), (0,))
NT = ((1,), (1,))
TN = ((0,), (0,))


def _params(**kw):
    return pltpu.CompilerParams(vmem_limit_bytes=VMEM_LIMIT, **kw)


def _dot(a, b, dims):
    return lax.dot_general(a, b, (dims, ((), ())), preferred_element_type=F32)


def _mean_sq_rsqrt(x):
    return lax.rsqrt(jnp.mean(x * x, axis=-1, keepdims=True) + RMS_EPS)


def _rms_bwd(x, r, g, dy):
    xhat = x * r
    dxh = dy * g
    dx = r * (dxh - xhat * jnp.mean(dxh * xhat, axis=-1, keepdims=True))
    return dx, dy * xhat


def _gelu(y):
    c = math.sqrt(2.0 / math.pi)
    t = jnp.tanh(c * (y + 0.044715 * y * y * y))
    return 0.5 * y * (1.0 + t), t


def _gelu_grad(y, t):
    c = math.sqrt(2.0 / math.pi)
    return 0.5 * (1.0 + t) + 0.5 * y * (1.0 - t * t) * c * (1.0 + 3.0 * 0.044715 * y * y)


def _shift_down(ext, k, rows):
    return pltpu.roll(ext, k, 0)[SUBLANES:SUBLANES + rows]


def _shift_up(ext, k, rows):
    n = ext.shape[0]
    return pltpu.roll(ext, n - k, 0)[:rows]


def _dev_index(p):
    return 4 * p[0] + 2 * p[1] + p[2]


def _allgather(name, shards, deps=()):
    n = len(shards)

    def body(*refs):
        ins, outs = refs[:n], refs[n:2 * n]
        send_sems, recv_sems, local_sems = refs[2 * n:]
        x, y, c = lax.axis_index("x"), lax.axis_index("y"), lax.axis_index("c")
        me, sibling = (x, y, c), (x, y, 1 - c)
        chips = [(1 - x, y), (x, 1 - y), (1 - x, 1 - y)]

        def copy(a, k, block, to, src=None):
            dst = outs[a].at[_dev_index(block)]
            return pltpu.make_async_remote_copy(
                src_ref=dst if src is None else src, dst_ref=dst,
                send_sem=send_sems.at[a, k], recv_sem=recv_sems.at[a, k],
                device_id=to, device_id_type=MESH)

        mine = [pltpu.make_async_copy(ins[a], outs[a].at[_dev_index(me)], local_sems.at[a])
                for a in range(n)]
        for cp in mine:
            cp.start()
        first = []
        for a in range(n):
            first.append(copy(a, 0, me, sibling, src=ins[a]))
            for j, chip in enumerate(chips):
                first.append(copy(a, 1 + j, me, (*chip, c), src=ins[a]))
        for cp in first:
            cp.start()
        passed = []
        for j, chip in enumerate(chips):
            for a in range(n):
                copy(a, 1 + j, (*chip, c), me).wait_recv()
                fwd = copy(a, 4 + j, (*chip, c), sibling)
                fwd.start()
                passed.append(fwd)
        for a in range(n):
            copy(a, 0, sibling, me).wait_recv()
            for j, chip in enumerate(chips):
                copy(a, 4 + j, (*chip, 1 - c), me).wait_recv()
        for cp in first + passed:
            cp.wait_send()
        for cp in mine:
            cp.wait()

    any_spec = pl.BlockSpec(memory_space=pl.ANY)
    return pl.pallas_call(
        _ignoring_deps(body, n, deps), name=name,
        out_shape=[jax.ShapeDtypeStruct((N_DEV,) + s.shape, s.dtype) for s in shards],
        in_specs=[any_spec] * (n + len(deps)), out_specs=[any_spec] * n,
        scratch_shapes=[pltpu.SemaphoreType.DMA((n, 7)), pltpu.SemaphoreType.DMA((n, 7)),
                        pltpu.SemaphoreType.DMA((n,))],
    )(*shards, *deps)


HBM_SPEC = pl.BlockSpec(memory_space=pltpu.HBM)
SEM_SPEC = pl.BlockSpec(memory_space=pltpu.SEMAPHORE)
ANY_SPEC = pl.BlockSpec(memory_space=pl.ANY)
DATAFLOW = pltpu.SideEffectType.DATAFLOW_SIDE_EFFECTING


def _ignoring_deps(body, n_in, deps):
    n_dep = len(deps)

    def wrapped(*refs):
        return body(*refs[:n_in], *refs[n_in + n_dep:])

    return wrapped


def _my_position():
    x, y, c = lax.axis_index("x"), lax.axis_index("y"), lax.axis_index("c")
    return (x, y, c)


def _peer(me, k):
    return tuple((1 - v) if (k >> s) & 1 else v for v, s in zip(me, (2, 1, 0)))


def _split_copies(src_refs, land_refs, send_sems, recv_sems, gather):
    me = _my_position()
    copies = []
    for a, (src, land) in enumerate(zip(src_refs, land_refs)):
        for k in range(1, N_DEV):
            peer = _peer(me, k)
            copies.append(pltpu.make_async_remote_copy(
                src_ref=src if gather[a] else src.at[_dev_index(peer)], dst_ref=land.at[_dev_index(me)],
                send_sem=send_sems[a].at[k - 1], recv_sem=recv_sems[a].at[k - 1],
                device_id=peer, device_id_type=MESH))
    return copies


def _own_slot(block, like_shape):
    me = _dev_index(_my_position())
    return lax.dynamic_update_index_in_dim(lax.empty(like_shape, block.dtype), block, me, 0)


def _send_start(name, srcs, gather):
    n = len(srcs)
    me = _dev_index(_my_position())
    gather = [gather] * n if isinstance(gather, bool) else list(gather)
    lands = [_own_slot(s, (N_DEV,) + s.shape) if g else
             _own_slot(lax.dynamic_index_in_dim(s, me, 0, keepdims=False), s.shape)
             for s, g in zip(srcs, gather)]

    def body(*refs):
        src_refs, land_refs = refs[:n], refs[n:2 * n]
        send_sems, recv_sems = refs[2 * n:3 * n], refs[3 * n:4 * n]
        token = refs[-1]
        for cp in _split_copies(src_refs, land_refs, send_sems, recv_sems, gather):
            cp.start()
        token[...] = jnp.zeros_like(token)

    hbm = lambda a: pltpu.HBM(a.shape, a.dtype)
    sems = [pltpu.SemaphoreType.DMA((N_DEV - 1,))] * n
    outs = pl.pallas_call(
        body, name=name,
        out_shape=(*sems, *sems, *[hbm(s) for s in srcs], *[hbm(l) for l in lands],
                   jax.ShapeDtypeStruct((SUBLANES, LANES), F32)),
        in_specs=[HBM_SPEC] * (2 * n),
        out_specs=(*[SEM_SPEC] * (2 * n), *[HBM_SPEC] * (2 * n), pl.BlockSpec(memory_space=pltpu.VMEM)),
        input_output_aliases={i: 2 * n + i for i in range(2 * n)},
        compiler_params=pltpu.CompilerParams(has_side_effects=DATAFLOW),
    )(*[pltpu.with_memory_space_constraint(a, pltpu.HBM) for a in (*srcs, *lands)])
    state = dict(send=outs[:n], recv=outs[n:2 * n], srcs=outs[2 * n:3 * n], lands=outs[3 * n:4 * n],
                 gather=gather)
    return state, outs[-1]


def _send_wait(name, state, which, after):
    n = len(which)
    pick = lambda key: [state[key][i] for i in which]
    gather = pick("gather")

    def body(*refs):
        src_refs, land_refs = refs[:n], refs[n:2 * n]
        send_sems, recv_sems = refs[2 * n:3 * n], refs[3 * n:4 * n]
        for cp in _split_copies(src_refs, land_refs, send_sems, recv_sems, gather):
            cp.wait_send()
            cp.wait_recv()

    srcs, lands = pick("srcs"), pick("lands")
    hbm = lambda a: pltpu.HBM(a.shape, a.dtype)
    outs = pl.pallas_call(
        body, name=name,
        out_shape=(*[hbm(s) for s in srcs], *[hbm(l) for l in lands]),
        in_specs=[*[HBM_SPEC] * (2 * n), *[SEM_SPEC] * (2 * n), ANY_SPEC],
        out_specs=tuple([HBM_SPEC] * (2 * n)),
        input_output_aliases={i: i for i in range(2 * n)},
        compiler_params=pltpu.CompilerParams(has_side_effects=DATAFLOW),
    )(*srcs, *lands, *pick("send"), *pick("recv"), after)
    return outs[n:]


def _mm(name, a, b, *, dims, grid, a_spec, b_spec, o_spec, out_shape, acc_shape=None,
        res=None, res_spec=None):
    n_red = grid[-1] if acc_shape is not None else 1
    red_axis = len(grid) - 1

    def body(*refs):
        a_ref, b_ref = refs[0], refs[1]
        r_ref = refs[2] if res is not None else None
        o_ref = refs[3] if res is not None else refs[2]
        part = _dot(a_ref[...], b_ref[...], dims)
        if acc_shape is None:
            if r_ref is not None:
                part = part + r_ref[...]
            o_ref[...] = part.astype(o_ref.dtype)
            return
        acc_ref = refs[-1]
        k = pl.program_id(red_axis)

        @pl.when(k == 0)
        def _():
            acc_ref[...] = part

        @pl.when(k > 0)
        def _():
            acc_ref[...] += part

        @pl.when(k == n_red - 1)
        def _():
            total = acc_ref[...]
            if r_ref is not None:
                total = total + r_ref[...]
            o_ref[...] = total.astype(o_ref.dtype)

    ins, in_specs = [a, b], [a_spec, b_spec]
    if res is not None:
        ins.append(res)
        in_specs.append(res_spec)
    return pl.pallas_call(
        body, name=name, grid=grid, in_specs=in_specs, out_specs=o_spec, out_shape=out_shape,
        scratch_shapes=[pltpu.VMEM(acc_shape, F32)] if acc_shape is not None else [],
        compiler_params=_params(),
    )(*ins)


def _rms_fwd(name, h, g):
    t, d = h.shape

    def body(h_ref, g_ref, o_ref):
        x = h_ref[...]
        o_ref[...] = (x * _mean_sq_rsqrt(x) * g_ref[...]).astype(BF16)

    return pl.pallas_call(
        body, name=name, grid=(t // ROW_TILE,),
        in_specs=[pl.BlockSpec((ROW_TILE, d), lambda i: (i, 0)), pl.BlockSpec((1, d), lambda i: (0, 0))],
        out_specs=pl.BlockSpec((ROW_TILE, d), lambda i: (i, 0)),
        out_shape=jax.ShapeDtypeStruct((t, d), BF16), compiler_params=_params(),
    )(h, g)


def _rms_bwd_res(name, dres, dhn, h, g, deps=()):
    t, d = h.shape

    def body(dres_ref, dhn_ref, h_ref, g_ref, dh_ref, dhb_ref, dg_ref):
        x = h_ref[...]
        dx, dgt = _rms_bwd(x, _mean_sq_rsqrt(x), g_ref[...], dhn_ref[...])
        dh = dres_ref[...] + dx
        dh_ref[...] = dh
        dhb_ref[...] = dh.astype(BF16)

        @pl.when(pl.program_id(0) == 0)
        def _():
            dg_ref[...] = jnp.zeros_like(dg_ref)

        dg_ref[...] += jnp.sum(dgt, axis=0, keepdims=True)

    row = pl.BlockSpec((ROW_TILE, d), lambda i: (i, 0))
    vec = pl.BlockSpec((1, d), lambda i: (0, 0))
    return pl.pallas_call(
        _ignoring_deps(body, 4, deps), name=name, grid=(t // ROW_TILE,),
        in_specs=[row, row, row, vec] + [ANY_SPEC] * len(deps), out_specs=[row, row, vec],
        out_shape=[jax.ShapeDtypeStruct((t, d), F32), jax.ShapeDtypeStruct((t, d), BF16),
                   jax.ShapeDtypeStruct((1, d), F32)],
        compiler_params=_params(),
    )(dres, dhn, h, g, *deps)


def _loss_bwd(h2, target, g, row_lo, row_hi):
    t, d = h2.shape

    def body(h_ref, tg_ref, g_ref, dh_ref, dhb_ref, loss_ref, dg_ref):
        i = pl.program_id(0)
        x = h_ref[...]
        r = _mean_sq_rsqrt(x)
        gv = g_ref[...]
        y = x * r * gv
        rows = i * ROW_TILE + lax.broadcasted_iota(jnp.int32, (ROW_TILE, 1), 0)
        valid = jnp.logical_and(rows >= row_lo, rows < row_hi)
        err = jnp.where(valid, y - tg_ref[...], 0.0)
        dy = err * (1.0 / d)
        dx, dgt = _rms_bwd(x, r, gv, dy)
        dh_ref[...] = dx
        dhb_ref[...] = dx.astype(BF16)

        @pl.when(i == 0)
        def _():
            loss_ref[...] = jnp.zeros_like(loss_ref)
            dg_ref[...] = jnp.zeros_like(dg_ref)

        row_loss = jnp.mean(err * err, axis=-1, keepdims=True)
        loss_ref[...] += 0.5 * jnp.sum(row_loss, axis=0, keepdims=True)
        dg_ref[...] += jnp.sum(dgt, axis=0, keepdims=True)

    row = pl.BlockSpec((ROW_TILE, d), lambda i: (i, 0))
    vec = pl.BlockSpec((1, d), lambda i: (0, 0))
    return pl.pallas_call(
        body, name="loss_bwd", grid=(t // ROW_TILE,), in_specs=[row, row, vec],
        out_specs=[row, row, pl.BlockSpec((1, 1), lambda i: (0, 0)), vec],
        out_shape=[jax.ShapeDtypeStruct((t, d), F32), jax.ShapeDtypeStruct((t, d), BF16),
                   jax.ShapeDtypeStruct((1, 1), F32), jax.ShapeDtypeStruct((1, d), F32)],
        compiler_params=_params(),
    )(h2, target, g)


def _prev_halo(i):
    return jnp.maximum(i * (ROW_TILE // SUBLANES) - 1, 0)


def _causal_taps(cur, halo, first):
    rows = cur.shape[0]
    halo = jnp.where(first, 0.0, halo)
    ext = jnp.concatenate([halo, cur], axis=0)
    return _shift_down(ext, 2, rows), _shift_down(ext, 1, rows)


def _anticausal_taps(cur, halo, last):
    rows = cur.shape[0]
    halo = jnp.where(last, 0.0, halo)
    ext = jnp.concatenate([cur, halo], axis=0)
    return _shift_up(ext, 1, rows), _shift_up(ext, 2, rows)


def _mix_fwd(proj, y, w_glu, conv_w, gain_c, gain_s):
    t = proj.shape[0]
    dc = conv_w.shape[1]
    ds = y.shape[1]

    def body(p_ref, halo_ref, y_ref, wg_ref, cw_ref, gc_ref, gs_ref, mixed_ref, z_ref):
        i = pl.program_id(0)
        p = p_ref[...]
        b, c, v = p[:, :dc], p[:, dc:2 * dc], p[:, 2 * dc:3 * dc]
        cv = c * v
        hp = halo_ref[...]
        x2, x1 = _causal_taps(cv, hp[:, dc:2 * dc] * hp[:, 2 * dc:3 * dc], i == 0)
        cw = cw_ref[...]
        conv = cw[0:1] * x2 + cw[1:2] * x1 + cw[2:3] * cv
        co = b * conv
        mixed_ref[:, :dc] = (co * _mean_sq_rsqrt(co) * gc_ref[...]).astype(BF16)
        g, _ = _gelu(y_ref[...])
        z = _dot(g.astype(BF16), wg_ref[...], NN)
        z_ref[...] = z
        so = g * jax.nn.sigmoid(z)
        mixed_ref[:, dc:] = (so * _mean_sq_rsqrt(so) * gs_ref[...]).astype(BF16)

    const = lambda i: (0, 0)
    return pl.pallas_call(
        body, name="mix_fwd", grid=(t // ROW_TILE,),
        in_specs=[pl.BlockSpec((ROW_TILE, 3 * dc), lambda i: (i, 0)),
                  pl.BlockSpec((SUBLANES, 3 * dc), lambda i: (_prev_halo(i), 0)),
                  pl.BlockSpec((ROW_TILE, ds), lambda i: (i, 0)),
                  pl.BlockSpec((ds, ds), const), pl.BlockSpec(conv_w.shape, const),
                  pl.BlockSpec((1, dc), const), pl.BlockSpec((1, ds), const)],
        out_specs=[pl.BlockSpec((ROW_TILE, dc + ds), lambda i: (i, 0)),
                   pl.BlockSpec((ROW_TILE, ds), lambda i: (i, 0))],
        out_shape=[jax.ShapeDtypeStruct((t, dc + ds), BF16), jax.ShapeDtypeStruct((t, ds), F32)],
        compiler_params=_params(),
    )(proj, proj, y, w_glu, conv_w, gain_c, gain_s)


def _mix_bwd1(proj, y, z, dmixed, w_glu, conv_w, gain_c, gain_s):
    t = proj.shape[0]
    dc = conv_w.shape[1]
    ds = y.shape[1]

    def body(p_ref, halo_ref, y_ref, z_ref, dm_ref, wg_ref, cw_ref, gc_ref, gs_ref,
             db_ref, dconv_ref, dy_ref, dwg_ref, dcw_ref, dgc_ref, dgs_ref):
        i = pl.program_id(0)

        @pl.when(i == 0)
        def _():
            dwg_ref[...] = jnp.zeros_like(dwg_ref)
            dcw_ref[...] = jnp.zeros_like(dcw_ref)
            dgc_ref[...] = jnp.zeros_like(dgc_ref)
            dgs_ref[...] = jnp.zeros_like(dgs_ref)

        p = p_ref[...]
        b, c, v = p[:, :dc], p[:, dc:2 * dc], p[:, 2 * dc:3 * dc]
        cv = c * v
        hp = halo_ref[...]
        x2, x1 = _causal_taps(cv, hp[:, dc:2 * dc] * hp[:, 2 * dc:3 * dc], i == 0)
        cw = cw_ref[...]
        conv = cw[0:1] * x2 + cw[1:2] * x1 + cw[2:3] * cv
        co = b * conv
        dm = dm_ref[...]
        dco, dgt = _rms_bwd(co, _mean_sq_rsqrt(co), gc_ref[...], dm[:, :dc])
        dgc_ref[...] += jnp.sum(dgt, axis=0, keepdims=True)
        db_ref[...] = (dco * conv).astype(BF16)
        dconv = dco * b
        dconv_ref[...] = dconv
        dcw_ref[0:1, :] += jnp.sum(dconv * x2, axis=0, keepdims=True)
        dcw_ref[1:2, :] += jnp.sum(dconv * x1, axis=0, keepdims=True)
        dcw_ref[2:3, :] += jnp.sum(dconv * cv, axis=0, keepdims=True)

        yv = y_ref[...]
        g, th = _gelu(yv)
        sg = jax.nn.sigmoid(z_ref[...])
        so = g * sg
        dso, dgt = _rms_bwd(so, _mean_sq_rsqrt(so), gs_ref[...], dm[:, dc:])
        dgs_ref[...] += jnp.sum(dgt, axis=0, keepdims=True)
        dz = (dso * g * sg * (1.0 - sg)).astype(BF16)
        dg = dso * sg + _dot(dz, wg_ref[...], NT)
        dwg_ref[...] += _dot(g.astype(BF16), dz, TN)
        dy_ref[...] = dg * _gelu_grad(yv, th)

    const = lambda i: (0, 0)
    row = lambda w: pl.BlockSpec((ROW_TILE, w), lambda i: (i, 0))
    return pl.pallas_call(
        body, name="mix_bwd1", grid=(t // ROW_TILE,),
        in_specs=[row(3 * dc), pl.BlockSpec((SUBLANES, 3 * dc), lambda i: (_prev_halo(i), 0)),
                  row(ds), row(ds), row(dc + ds), pl.BlockSpec((ds, ds), const),
                  pl.BlockSpec(conv_w.shape, const), pl.BlockSpec((1, dc), const),
                  pl.BlockSpec((1, ds), const)],
        out_specs=[row(dc), row(dc), row(ds), pl.BlockSpec((ds, ds), const),
                   pl.BlockSpec(conv_w.shape, const), pl.BlockSpec((1, dc), const),
                   pl.BlockSpec((1, ds), const)],
        out_shape=[jax.ShapeDtypeStruct((t, dc), BF16), jax.ShapeDtypeStruct((t, dc), F32),
                   jax.ShapeDtypeStruct((t, ds), F32), jax.ShapeDtypeStruct((ds, ds), F32),
                   jax.ShapeDtypeStruct(conv_w.shape, F32), jax.ShapeDtypeStruct((1, dc), F32),
                   jax.ShapeDtypeStruct((1, ds), F32)],
        compiler_params=_params(),
    )(proj, proj, y, z, dmixed, w_glu, conv_w, gain_c, gain_s)


def _mix_bwd2(proj, dconv, conv_w, deps=()):
    t = proj.shape[0]
    dc = conv_w.shape[1]
    n_tiles = t // ROW_TILE
    per_tile = ROW_TILE // SUBLANES

    def body(c_ref, v_ref, d_ref, halo_ref, cw_ref, o_ref):
        i = pl.program_id(0)
        d = d_ref[...]
        u1, u2 = _anticausal_taps(d, halo_ref[...], i == n_tiles - 1)
        cw = cw_ref[...]
        dcv = cw[2:3] * d + cw[1:2] * u1 + cw[0:1] * u2
        o_ref[:, :dc] = (dcv * v_ref[...]).astype(BF16)
        o_ref[:, dc:] = (dcv * c_ref[...]).astype(BF16)

    return pl.pallas_call(
        _ignoring_deps(body, 5, deps), name="mix_bwd2", grid=(n_tiles,),
        in_specs=[pl.BlockSpec((ROW_TILE, dc), lambda i: (i, 1)),
                  pl.BlockSpec((ROW_TILE, dc), lambda i: (i, 2)),
                  pl.BlockSpec((ROW_TILE, dc), lambda i: (i, 0)),
                  pl.BlockSpec((SUBLANES, dc),
                               lambda i: (jnp.minimum((i + 1) * per_tile, t // SUBLANES - 1), 0)),
                  pl.BlockSpec(conv_w.shape, lambda i: (0, 0))] + [ANY_SPEC] * len(deps),
        out_specs=pl.BlockSpec((ROW_TILE, 2 * dc), lambda i: (i, 0)),
        out_shape=jax.ShapeDtypeStruct((t, 2 * dc), BF16), compiler_params=_params(),
    )(proj, proj, dconv, dconv, conv_w, *deps)


def _ffn_taps(up_ref, halo_ref, fw_ref, fb_ref, q, first):
    x0 = up_ref[q]
    x2, x1 = _causal_taps(x0, halo_ref[q], first)
    w = fw_ref[q]
    return w[0:1] * x2 + w[1:2] * x1 + w[2:3] * x0 + fb_ref[q], x2, x1, x0


def _ffn_fwd(up_pre, fw, fb):
    nb, t, w = up_pre.shape
    half = nb // 2

    def body(up_ref, halo_ref, fw_ref, fb_ref, act_ref):
        first = pl.program_id(0) == 0
        for q in range(half):
            a = _ffn_taps(up_ref, halo_ref, fw_ref, fb_ref, q, first)[0]
            val = _ffn_taps(up_ref, halo_ref, fw_ref, fb_ref, q + half, first)[0]
            act_ref[q] = (a * jax.nn.sigmoid(a) * val).astype(BF16)

    return pl.pallas_call(
        body, name="ffn_fwd", grid=(t // ROW_TILE,),
        in_specs=[pl.BlockSpec((nb, ROW_TILE, w), lambda i: (0, i, 0)),
                  pl.BlockSpec((nb, SUBLANES, w), lambda i: (0, _prev_halo(i), 0)),
                  pl.BlockSpec(fw.shape, lambda i: (0, 0, 0)),
                  pl.BlockSpec(fb.shape, lambda i: (0, 0, 0))],
        out_specs=pl.BlockSpec((half, ROW_TILE, w), lambda i: (0, i, 0)),
        out_shape=jax.ShapeDtypeStruct((half, t, w), BF16), compiler_params=_params(),
    )(up_pre, up_pre, fw, fb)


def _ffn_bwd1(up_pre, dact, fw, fb, deps=()):
    nb, t, w = up_pre.shape
    half = nb // 2

    def body(up_ref, halo_ref, dact_ref, fw_ref, fb_ref, dup_ref, dfw_ref, dfb_ref):
        first = pl.program_id(0) == 0

        @pl.when(first)
        def _():
            dfw_ref[...] = jnp.zeros_like(dfw_ref)
            dfb_ref[...] = jnp.zeros_like(dfb_ref)

        for q in range(half):
            taps_a = _ffn_taps(up_ref, halo_ref, fw_ref, fb_ref, q, first)
            taps_v = _ffn_taps(up_ref, halo_ref, fw_ref, fb_ref, q + half, first)
            a, val = taps_a[0], taps_v[0]
            da_ct = dact_ref[q]
            sg = jax.nn.sigmoid(a)
            da = da_ct * val * sg * (1.0 + a * (1.0 - sg))
            dval = da_ct * a * sg
            for blk, dv, taps in ((q, da, taps_a), (q + half, dval, taps_v)):
                dup_ref[blk] = dv
                dfb_ref[blk] += jnp.sum(dv, axis=0, keepdims=True)
                for k in range(3):
                    dfw_ref[blk, k:k + 1, :] += jnp.sum(dv * taps[1 + k], axis=0, keepdims=True)

    return pl.pallas_call(
        _ignoring_deps(body, 5, deps), name="ffn_bwd1", grid=(t // ROW_TILE,),
        in_specs=[pl.BlockSpec((nb, ROW_TILE, w), lambda i: (0, i, 0)),
                  pl.BlockSpec((nb, SUBLANES, w), lambda i: (0, _prev_halo(i), 0)),
                  pl.BlockSpec((half, ROW_TILE, w), lambda i: (0, i, 0)),
                  pl.BlockSpec(fw.shape, lambda i: (0, 0, 0)),
                  pl.BlockSpec(fb.shape, lambda i: (0, 0, 0))] + [ANY_SPEC] * len(deps),
        out_specs=[pl.BlockSpec((nb, ROW_TILE, w), lambda i: (0, i, 0)),
                   pl.BlockSpec(fw.shape, lambda i: (0, 0, 0)),
                   pl.BlockSpec(fb.shape, lambda i: (0, 0, 0))],
        out_shape=[jax.ShapeDtypeStruct((nb, t, w), F32), jax.ShapeDtypeStruct(fw.shape, F32),
                   jax.ShapeDtypeStruct(fb.shape, F32)],
        compiler_params=_params(),
    )(up_pre, up_pre, dact, fw, fb, *deps)


def _ffn_bwd2(dup, fw):
    nb, t, w = dup.shape
    n_tiles = t // ROW_TILE
    per_tile = ROW_TILE // SUBLANES

    def body(d_ref, halo_ref, fw_ref, o_ref):
        last = pl.program_id(0) == n_tiles - 1
        for q in range(nb):
            d = d_ref[q]
            u1, u2 = _anticausal_taps(d, halo_ref[q], last)
            cw = fw_ref[q]
            o_ref[q] = (cw[2:3] * d + cw[1:2] * u1 + cw[0:1] * u2).astype(BF16)

    return pl.pallas_call(
        body, name="ffn_bwd2", grid=(n_tiles,),
        in_specs=[pl.BlockSpec((nb, ROW_TILE, w), lambda i: (0, i, 0)),
                  pl.BlockSpec((nb, SUBLANES, w),
                               lambda i: (0, jnp.minimum((i + 1) * per_tile, t // SUBLANES - 1), 0)),
                  pl.BlockSpec(fw.shape, lambda i: (0, 0, 0))],
        out_specs=pl.BlockSpec((nb, ROW_TILE, w), lambda i: (0, i, 0)),
        out_shape=jax.ShapeDtypeStruct((nb, t, w), BF16), compiler_params=_params(),
    )(dup, dup, fw)


def _to_segments(a):
    t, c = a.shape
    return a.reshape(N_SEG, t // N_SEG, c).transpose(1, 0, 2).reshape(t, c)


def _from_segments(a):
    t, c = a.shape
    return a.reshape(t // N_SEG, N_SEG, c).transpose(1, 0, 2).reshape(t, c)


def _cmul(ar, ai, br, bi):
    return ar * br - ai * bi, ar * bi + ai * br


def _segment_carries(pr, pi, fr, fi, forward):
    row = lax.broadcasted_iota(jnp.int32, fr.shape, 0)
    edge = row == (0 if forward else N_SEG - 1)
    shift = 1 if forward else N_SEG - 1
    sr, si = jnp.zeros_like(fr), jnp.zeros_like(fi)
    for _ in range(N_SEG - 1):
        tr, ti = _cmul(pr, pi, sr, si)
        sr = jnp.where(edge, 0.0, pltpu.roll(tr + fr, shift, 0))
        si = jnp.where(edge, 0.0, pltpu.roll(ti + fi, shift, 0))
    return sr, si


def _rows(i):
    return pl.ds(pl.multiple_of(i * SUBLANES, SUBLANES), SUBLANES)


def _s5_fwd(u, bb_re, bb_im, a_re, a_im, cc_re, cc_im, d_skip):
    t, ds = u.shape
    nch, _, cs = bb_re.shape
    steps = t // N_SEG

    def body(u_ref, bbr_ref, bbi_ref, ar_ref, ai_ref, ccr_ref, cci_ref, d_ref, sr_ref, si_ref, y_ref):
        ub = u_ref[...].astype(BF16)
        sr_ref[...] = _dot(ub, bbr_ref[...], NN)
        si_ref[...] = _dot(ub, bbi_ref[...], NN)
        ar = jnp.broadcast_to(ar_ref[...], (N_SEG, cs))
        ai = jnp.broadcast_to(ai_ref[...], (N_SEG, cs))
        zero = jnp.zeros((N_SEG, cs), F32)

        def totals(i, carry):
            sr, si, pr, pi = carry
            tr, ti = _cmul(ar, ai, sr, si)
            qr, qi = _cmul(ar, ai, pr, pi)
            return tr + sr_ref[_rows(i), :], ti + si_ref[_rows(i), :], qr, qi

        fr, fi, pr, pi = lax.fori_loop(0, steps, totals, (zero, zero, zero + 1.0, zero))
        s0r, s0i = _segment_carries(pr, pi, fr, fi, True)

        def scan(i, carry):
            tr, ti = _cmul(ar, ai, *carry)
            nr, ni = tr + sr_ref[_rows(i), :], ti + si_ref[_rows(i), :]
            sr_ref[_rows(i), :] = nr
            si_ref[_rows(i), :] = ni
            return nr, ni

        lax.fori_loop(0, steps, scan, (s0r, s0i))
        y_ref[...] = (_dot(sr_ref[...].astype(BF16), ccr_ref[...], NN)
                      - _dot(si_ref[...].astype(BF16), cci_ref[...], NN)
                      + d_ref[...] * u_ref[...])

    chunk3 = lambda r, c: pl.BlockSpec((None, r, c), lambda j: (j, 0, 0))
    return pl.pallas_call(
        body, name="s5_fwd", grid=(nch,),
        in_specs=[pl.BlockSpec((t, SSM_CHUNK), lambda j: (0, j)),
                  chunk3(SSM_CHUNK, cs), chunk3(SSM_CHUNK, cs), chunk3(1, cs), chunk3(1, cs),
                  chunk3(cs, SSM_CHUNK), chunk3(cs, SSM_CHUNK), chunk3(1, SSM_CHUNK)],
        out_specs=[pl.BlockSpec((t, cs), lambda j: (0, j)), pl.BlockSpec((t, cs), lambda j: (0, j)),
                   pl.BlockSpec((t, SSM_CHUNK), lambda j: (0, j))],
        out_shape=[jax.ShapeDtypeStruct((t, nch * cs), F32), jax.ShapeDtypeStruct((t, nch * cs), F32),
                   jax.ShapeDtypeStruct((t, ds), F32)],
        compiler_params=_params(),
    )(u, bb_re, bb_im, a_re, a_im, cc_re, cc_im, d_skip)


def _s5_bwd(dy, u, s_re, s_im, bb_re, bb_im, a_re, a_im, cc_re, cc_im, d_skip):
    t, ds = u.shape
    nch, _, cs = bb_re.shape
    steps = t // N_SEG

    def body(dy_ref, u_ref, sr_ref, si_ref, bbr_ref, bbi_ref, ar_ref, ai_ref, ccr_ref, cci_ref, d_ref,
             du_ref, dbbr_ref, dbbi_ref, dar_ref, dai_ref, dccr_ref, dcci_ref, dd_ref, gr_ref, gi_ref):
        dyv = dy_ref[...]
        dyb = dyv.astype(BF16)
        gr_ref[...] = _dot(dyb, ccr_ref[...], NT)
        gi_ref[...] = -_dot(dyb, cci_ref[...], NT)
        ar = jnp.broadcast_to(ar_ref[...], (N_SEG, cs))
        ai = -jnp.broadcast_to(ai_ref[...], (N_SEG, cs))
        zero = jnp.zeros((N_SEG, cs), F32)

        def totals(k, carry):
            i = steps - 1 - k
            gr, gi, pr, pi = carry
            tr, ti = _cmul(ar, ai, gr, gi)
            qr, qi = _cmul(ar, ai, pr, pi)
            return tr + gr_ref[_rows(i), :], ti + gi_ref[_rows(i), :], qr, qi

        fr, fi, pr, pi = lax.fori_loop(0, steps, totals, (zero, zero, zero + 1.0, zero))
        e0r, e0i = _segment_carries(pr, pi, fr, fi, False)

        def step(i, gr, gi, pr, pi, acc_r, acc_i):
            tr, ti = _cmul(ar, ai, gr, gi)
            nr, ni = tr + gr_ref[_rows(i), :], ti + gi_ref[_rows(i), :]
            gr_ref[_rows(i), :] = nr
            gi_ref[_rows(i), :] = ni
            return nr, ni, acc_r + nr * pr + ni * pi, acc_i + ni * pr - nr * pi

        def scan(k, carry):
            i = steps - 1 - k
            gr, gi, acc_r, acc_i = carry
            return step(i, gr, gi, sr_ref[_rows(i - 1), :], si_ref[_rows(i - 1), :], acc_r, acc_i)

        gr, gi, acc_r, acc_i = lax.fori_loop(0, steps - 1, scan, (e0r, e0i, zero, zero))
        row = lax.broadcasted_iota(jnp.int32, (N_SEG, cs), 0)
        last = _rows(steps - 1)
        pr = jnp.where(row == 0, 0.0, pltpu.roll(sr_ref[last, :], 1, 0))
        pi = jnp.where(row == 0, 0.0, pltpu.roll(si_ref[last, :], 1, 0))
        _, _, acc_r, acc_i = step(0, gr, gi, pr, pi, acc_r, acc_i)
        dar_ref[...] = jnp.sum(acc_r, axis=0, keepdims=True)
        dai_ref[...] = jnp.sum(acc_i, axis=0, keepdims=True)

        uv = u_ref[...]
        ub = uv.astype(BF16)
        grb = gr_ref[...].astype(BF16)
        gib = gi_ref[...].astype(BF16)
        du_ref[...] = d_ref[...] * dyv + _dot(grb, bbr_ref[...], NT) + _dot(gib, bbi_ref[...], NT)
        dbbr_ref[...] = _dot(ub, grb, TN)
        dbbi_ref[...] = _dot(ub, gib, TN)
        dccr_ref[...] = _dot(sr_ref[...].astype(BF16), dyb, TN)
        dcci_ref[...] = -_dot(si_ref[...].astype(BF16), dyb, TN)
        dd_ref[...] = jnp.sum(dyv * uv, axis=0, keepdims=True)

    chunk3 = lambda r, c: pl.BlockSpec((None, r, c), lambda j: (j, 0, 0))
    cols = lambda w: pl.BlockSpec((t, w), lambda j: (0, j))
    return pl.pallas_call(
        body, name="s5_bwd", grid=(nch,),
        in_specs=[cols(SSM_CHUNK), cols(SSM_CHUNK), cols(cs), cols(cs),
                  chunk3(SSM_CHUNK, cs), chunk3(SSM_CHUNK, cs), chunk3(1, cs), chunk3(1, cs),
                  chunk3(cs, SSM_CHUNK), chunk3(cs, SSM_CHUNK), chunk3(1, SSM_CHUNK)],
        out_specs=[cols(SSM_CHUNK), chunk3(SSM_CHUNK, cs), chunk3(SSM_CHUNK, cs), chunk3(1, cs),
                   chunk3(1, cs), chunk3(cs, SSM_CHUNK), chunk3(cs, SSM_CHUNK), chunk3(1, SSM_CHUNK)],
        out_shape=[jax.ShapeDtypeStruct((t, ds), F32),
                   jax.ShapeDtypeStruct((nch, SSM_CHUNK, cs), F32), jax.ShapeDtypeStruct((nch, SSM_CHUNK, cs), F32),
                   jax.ShapeDtypeStruct((nch, 1, cs), F32), jax.ShapeDtypeStruct((nch, 1, cs), F32),
                   jax.ShapeDtypeStruct((nch, cs, SSM_CHUNK), F32), jax.ShapeDtypeStruct((nch, cs, SSM_CHUNK), F32),
                   jax.ShapeDtypeStruct((nch, 1, SSM_CHUNK), F32)],
        scratch_shapes=[pltpu.VMEM((t, cs), F32), pltpu.VMEM((t, cs), F32)],
        compiler_params=_params(),
    )(dy, u, s_re, s_im, bb_re, bb_im, a_re, a_im, cc_re, cc_im, d_skip)


def _discretize(lr, li, log_dt, br, bi):
    dt = jnp.exp(log_dt)
    mag = jnp.exp(lr * dt)
    ang = li * dt
    a_re = mag * jnp.cos(ang)
    a_im = mag * jnp.sin(ang)
    den = lr * lr + li * li
    nr = a_re - 1.0
    f_re = (nr * lr + a_im * li) / den
    f_im = (a_im * lr - nr * li) / den
    return a_re, a_im, f_re * br - f_im * bi, f_re * bi + f_im * br


def _whole(shape):
    return pl.BlockSpec(shape, lambda: (0,) * len(shape))


def _disc_fwd(lr, li, log_dt, br, bi, deps=()):
    def body(lr_ref, li_ref, dt_ref, br_ref, bi_ref, ar_ref, ai_ref, bbr_ref, bbi_ref):
        outs = _discretize(lr_ref[...], li_ref[...], dt_ref[...], br_ref[...], bi_ref[...])
        for ref, val in zip((ar_ref, ai_ref, bbr_ref, bbi_ref), outs):
            ref[...] = val

    args = (lr, li, log_dt, br, bi)
    outs = (lr, lr, br, br)
    return pl.pallas_call(
        _ignoring_deps(body, 5, deps), name="disc_fwd",
        in_specs=[_whole(a.shape) for a in args] + [ANY_SPEC] * len(deps),
        out_specs=[_whole(a.shape) for a in outs],
        out_shape=[jax.ShapeDtypeStruct(a.shape, F32) for a in outs],
    )(*args, *deps)


def _disc_bwd(lr, li, log_dt, br, bi, dar, dai, dbbr, dbbi):
    def body(lr_ref, li_ref, dt_ref, br_ref, bi_ref, dar_ref, dai_ref, dbbr_ref, dbbi_ref,
             dlr_ref, dli_ref, ddt_ref, dbr_ref, dbi_ref):
        _, vjp = jax.vjp(_discretize, lr_ref[...], li_ref[...], dt_ref[...], br_ref[...], bi_ref[...])
        grads = vjp((dar_ref[...], dai_ref[...], dbbr_ref[...], dbbi_ref[...]))
        for ref, val in zip((dlr_ref, dli_ref, ddt_ref, dbr_ref, dbi_ref), grads):
            ref[...] = val

    args = (lr, li, log_dt, br, bi, dar, dai, dbbr, dbbi)
    outs = (lr, li, log_dt, br, bi)
    return pl.pallas_call(
        body, name="disc_bwd", in_specs=[_whole(a.shape) for a in args],
        out_specs=[_whole(a.shape) for a in outs],
        out_shape=[jax.ShapeDtypeStruct(a.shape, F32) for a in outs],
    )(*args)


def _adamw(w, g, m, v):
    m = ADAM_B1 * m + (1.0 - ADAM_B1) * g
    v = ADAM_B2 * v + (1.0 - ADAM_B2) * (g * g)
    m_hat = m / ADAM_BC1
    v_hat = v / ADAM_BC2
    delta = -ADAM_LR * (m_hat / (jnp.sqrt(v_hat) + ADAM_EPS) + ADAM_WD * w)
    return delta, m, v


def _adamw_reduce(name, parts, w, m, v):
    _, r, c = parts.shape
    tr = r
    for cand in (256, 176, 128):
        if r % cand == 0:
            tr = cand
            break

    def body(p_ref, w_ref, m_ref, v_ref, g_ref, d_ref, nm_ref, nv_ref):
        g = p_ref[0].astype(F32)
        for k in range(1, N_DEV):
            g = g + p_ref[k].astype(F32)
        delta, nm, nv = _adamw(w_ref[...], g, m_ref[...], v_ref[...])
        g_ref[...] = g
        d_ref[...] = delta
        nm_ref[...] = nm
        nv_ref[...] = nv

    blk = pl.BlockSpec((tr, c), lambda i: (i, 0))
    return pl.pallas_call(
        body, name=name, grid=(r // tr,),
        in_specs=[pl.BlockSpec((N_DEV, tr, c), lambda i: (0, i, 0)), blk, blk, blk],
        out_specs=[blk] * 4, out_shape=[jax.ShapeDtypeStruct((r, c), F32)] * 4,
        compiler_params=_params(),
    )(parts, w, m, v)


def _sum_parts(name, parts):
    _, r, c = parts.shape

    def body(p_ref, o_ref):
        g = p_ref[0]
        for k in range(1, N_DEV):
            g = g + p_ref[k]
        o_ref[...] = g

    return pl.pallas_call(
        body, name=name, in_specs=[_whole(parts.shape)], out_specs=_whole((r, c)),
        out_shape=jax.ShapeDtypeStruct((r, c), F32), compiler_params=_params(),
    )(parts)


def _adamw_many(name, grads, ws, ms, vs):
    n = len(grads)

    def body(*refs):
        ins, outs = refs[:4 * n], refs[4 * n:]
        for i in range(n):
            g, w, m, v = (ins[j * n + i][...] for j in range(4))
            for ref, val in zip((outs[i], outs[n + i], outs[2 * n + i]), _adamw(w, g, m, v)):
                ref[...] = val

    args = (*grads, *ws, *ms, *vs)
    outs = pl.pallas_call(
        body, name=name, in_specs=[_whole(a.shape) for a in args],
        out_specs=[_whole(a.shape) for a in ws] * 3,
        out_shape=[jax.ShapeDtypeStruct(a.shape, F32) for a in ws] * 3, compiler_params=_params(),
    )(*args)
    return outs[:n], outs[n:2 * n], outs[2 * n:]


def _pack(arrays, rows):
    flat = jnp.concatenate([a.reshape(-1) for a in arrays])
    return jnp.pad(flat, (0, rows * LANES - flat.shape[0])).reshape(rows, LANES)


def _unpack(packed, shapes):
    flat = packed.reshape(-1)
    out, off = [], 0
    for s in shapes:
        n = math.prod(s)
        out.append(flat[off:off + n].reshape(s))
        off += n
    return out


def _packed_rows(shapes):
    n = sum(math.prod(s) for s in shapes)
    return -(-n // (SUBLANES * LANES)) * SUBLANES


def _block_diag(x):
    j, g, r, c = x.shape
    eye = jnp.eye(g, dtype=x.dtype)
    return (x[:, :, :, None, :] * eye[None, :, None, :, None]).reshape(j, g * r, g * c)


def _diag_blocks(x, g):
    j, gr, gc = x.shape
    r, c = gr // g, gc // g
    eye = jnp.eye(g, dtype=x.dtype)
    return (x.reshape(j, g, r, g, c) * eye[None, :, None, :, None]).sum(axis=3)


def kernel(x, meta_tokens, norm_mix_g, w_in, conv_w, ssm_lam_re, ssm_lam_im, ssm_log_dt, ssm_b_re, ssm_b_im, ssm_c_re, ssm_c_im, ssm_d, ssm_w_glu, gain_conv_out, gain_ssm_out, w_out, norm_ffn_g, w_up, ffn_conv_w, ffn_conv_b, w_down, norm_final_g, loss_target, m_meta_tokens, m_norm_mix_g, m_w_in, m_conv_w, m_ssm_lam_re, m_ssm_lam_im, m_ssm_log_dt, m_ssm_b_re, m_ssm_b_im, m_ssm_c_re, m_ssm_c_im, m_ssm_d, m_ssm_w_glu, m_gain_conv_out, m_gain_ssm_out, m_w_out, m_norm_ffn_g, m_w_up, m_ffn_conv_w, m_ffn_conv_b, m_w_down, m_norm_final_g, v_meta_tokens, v_norm_mix_g, v_w_in, v_conv_w, v_ssm_lam_re, v_ssm_lam_im, v_ssm_log_dt, v_ssm_b_re, v_ssm_b_im, v_ssm_c_re, v_ssm_c_im, v_ssm_d, v_ssm_w_glu, v_gain_conv_out, v_gain_ssm_out, v_w_out, v_norm_ffn_g, v_w_up, v_ffn_conv_w, v_ffn_conv_b, v_w_down, v_norm_final_g):
    weights = dict(meta_tokens=meta_tokens, norm_mix_g=norm_mix_g, w_in=w_in, conv_w=conv_w, ssm_lam_re=ssm_lam_re, ssm_lam_im=ssm_lam_im, ssm_log_dt=ssm_log_dt, ssm_b_re=ssm_b_re, ssm_b_im=ssm_b_im, ssm_c_re=ssm_c_re, ssm_c_im=ssm_c_im, ssm_d=ssm_d, ssm_w_glu=ssm_w_glu, gain_conv_out=gain_conv_out, gain_ssm_out=gain_ssm_out, w_out=w_out, norm_ffn_g=norm_ffn_g, w_up=w_up, ffn_conv_w=ffn_conv_w, ffn_conv_b=ffn_conv_b, w_down=w_down, norm_final_g=norm_final_g)
    mom_m = dict(meta_tokens=m_meta_tokens, norm_mix_g=m_norm_mix_g, w_in=m_w_in, conv_w=m_conv_w, ssm_lam_re=m_ssm_lam_re, ssm_lam_im=m_ssm_lam_im, ssm_log_dt=m_ssm_log_dt, ssm_b_re=m_ssm_b_re, ssm_b_im=m_ssm_b_im, ssm_c_re=m_ssm_c_re, ssm_c_im=m_ssm_c_im, ssm_d=m_ssm_d, ssm_w_glu=m_ssm_w_glu, gain_conv_out=m_gain_conv_out, gain_ssm_out=m_gain_ssm_out, w_out=m_w_out, norm_ffn_g=m_norm_ffn_g, w_up=m_w_up, ffn_conv_w=m_ffn_conv_w, ffn_conv_b=m_ffn_conv_b, w_down=m_w_down, norm_final_g=m_norm_final_g)
    mom_v = dict(meta_tokens=v_meta_tokens, norm_mix_g=v_norm_mix_g, w_in=v_w_in, conv_w=v_conv_w, ssm_lam_re=v_ssm_lam_re, ssm_lam_im=v_ssm_lam_im, ssm_log_dt=v_ssm_log_dt, ssm_b_re=v_ssm_b_re, ssm_b_im=v_ssm_b_im, ssm_c_re=v_ssm_c_re, ssm_c_im=v_ssm_c_im, ssm_d=v_ssm_d, ssm_w_glu=v_ssm_w_glu, gain_conv_out=v_gain_conv_out, gain_ssm_out=v_gain_ssm_out, w_out=v_w_out, norm_ffn_g=v_norm_ffn_g, w_up=v_w_up, ffn_conv_w=v_ffn_conv_w, ffn_conv_b=v_ffn_conv_b, w_down=v_w_down, norm_final_g=v_norm_final_g)
    names = list(weights)

    n_meta, d_meta = meta_tokens.shape
    seq, d = x.shape[1], x.shape[2]
    rows_used = n_meta + seq
    t = -(-rows_used // ROW_TILE) * ROW_TILE
    d_in_s = w_in.shape[2]
    dc_s = conv_w.shape[2]
    dc = dc_s * N_DEV
    ds = ssm_w_glu.shape[2]
    n_groups, n_state, grp = ssm_b_re.shape[1:]
    ns = n_groups * n_state
    nch = ds // SSM_CHUNK
    gpc = n_groups // nch
    ff_s = w_up.shape[2]
    dn_s = w_down.shape[1]
    assert 3 * dc + ds == d_in_s * N_DEV and 2 * dn_s == ff_s and t % (N_SEG * SUBLANES) == 0

    small_shard = jnp.concatenate([meta_tokens.reshape(-1), conv_w.reshape(-1), ffn_conv_w.reshape(-1)])
    n_small = small_shard.shape[0]
    small_rows = -(-n_small // LANES)
    small_shard = jnp.pad(small_shard, (0, small_rows * LANES - n_small)).reshape(small_rows, LANES)
    ag, ag_token = _send_start("gather_weights_start", [
        small_shard, w_in[0].astype(BF16), ssm_w_glu[0].astype(BF16), w_out[0].astype(BF16),
        w_up[0].astype(BF16), w_down[0].astype(BF16)], gather=True)
    fb = ffn_conv_b.reshape(N_DEV, 1, ff_s)

    col = lambda a: a.reshape(ns, 1)
    lr, li = col(ssm_lam_re), col(ssm_lam_im)
    log_dt_e = jnp.broadcast_to(ssm_log_dt.reshape(n_groups, 1), (n_groups, n_state)).reshape(ns, 1)
    br, bi = ssm_b_re.reshape(ns, grp), ssm_b_im.reshape(ns, grp)
    a_re, a_im, bb_re, bb_im = _disc_fwd(lr, li, log_dt_e, br, bi, deps=(ag_token,))
    cs = gpc * n_state
    chunk_row = lambda a: a.reshape(nch, 1, cs)
    to_bb = lambda a: _block_diag(a.reshape(nch, gpc, n_state, grp).transpose(0, 1, 3, 2)).astype(BF16)
    to_cc = lambda a: _block_diag(a.reshape(nch, gpc, grp, n_state).transpose(0, 1, 3, 2)).astype(BF16)
    bbm_re, bbm_im = to_bb(bb_re), to_bb(bb_im)
    ccm_re, ccm_im = to_cc(ssm_c_re), to_cc(ssm_c_im)
    a_re_c, a_im_c = chunk_row(a_re), chunk_row(a_im)
    d_skip = ssm_d.reshape(nch, 1, SSM_CHUNK)

    (g_small,) = _send_wait("gather_weights_wait_small", ag, (0,), bbm_im)
    g_small = g_small.reshape(N_DEV, -1)
    o1 = n_meta * d_meta
    o2 = o1 + 3 * dc_s
    meta_full = g_small[:, :o1].reshape(N_DEV, n_meta, d_meta).transpose(1, 0, 2).reshape(n_meta, d)
    conv_w_f = g_small[:, o1:o2].reshape(N_DEV, 3, dc_s).transpose(1, 0, 2).reshape(3, dc)
    fw = g_small[:, o2:o2 + 3 * ff_s].reshape(N_DEV, 3, ff_s)
    h0 = jnp.concatenate([meta_full, x[0], jnp.zeros((t - rows_used, d), F32)], axis=0)
    target = jnp.pad(loss_target[0], ((n_meta, t - rows_used), (0, 0)))
    full_t = lambda w: pl.BlockSpec((t, w), lambda *_: (0, 0))

    hn1 = _rms_fwd("norm_mix", h0, norm_mix_g)
    (g_in,) = _send_wait("gather_weights_wait_in", ag, (1,), hn1)
    proj = _mm("proj", hn1, g_in, dims=NN, grid=(N_DEV,), a_spec=full_t(d),
               b_spec=pl.BlockSpec((None, d, d_in_s), lambda j: (j, 0, 0)),
               o_spec=pl.BlockSpec((t, d_in_s), lambda j: (0, j)),
               out_shape=jax.ShapeDtypeStruct((t, N_DEV * d_in_s), F32))
    u_seg = _to_segments(proj[:, 3 * dc:])
    s_re, s_im, y_seg = _s5_fwd(u_seg, bbm_re, bbm_im, a_re_c, a_im_c, ccm_re, ccm_im, d_skip)
    y_ssm = _from_segments(y_seg)
    g_glu, g_out = _send_wait("gather_weights_wait_mix", ag, (2, 3), y_ssm)
    w_out_f = g_out.reshape(-1, d)
    w_glu_f = g_glu.reshape(ds, ds)
    mixed, z_glu = _mix_fwd(proj, y_ssm, w_glu_f, conv_w_f, gain_conv_out, gain_ssm_out)
    tn_out = 256
    h1 = _mm("out_proj", mixed, w_out_f, dims=NN, grid=(d // tn_out,), a_spec=full_t(dc + ds),
             b_spec=pl.BlockSpec((dc + ds, tn_out), lambda j: (0, j)),
             o_spec=pl.BlockSpec((t, tn_out), lambda j: (0, j)),
             out_shape=jax.ShapeDtypeStruct((t, d), F32),
             res=h0, res_spec=pl.BlockSpec((t, tn_out), lambda j: (0, j)))
    hn2 = _rms_fwd("norm_ffn", h1, norm_ffn_g)
    (g_up,) = _send_wait("gather_weights_wait_up", ag, (4,), hn2)
    up_pre = _mm("up_proj", hn2, g_up, dims=NN, grid=(N_DEV,), a_spec=full_t(d),
                 b_spec=pl.BlockSpec((None, d, ff_s), lambda j: (j, 0, 0)),
                 o_spec=pl.BlockSpec((None, t, ff_s), lambda j: (j, 0, 0)),
                 out_shape=jax.ShapeDtypeStruct((N_DEV, t, ff_s), F32))
    act = _ffn_fwd(up_pre, fw, fb)
    nhalf = N_DEV // 2
    (g_down,) = _send_wait("gather_weights_wait_down", ag, (5,), act)
    w_down_f = g_down.reshape(nhalf, 2 * dn_s, d)
    h2 = _mm("down_proj", act, w_down_f, dims=NN, grid=(d // tn_out, nhalf),
             a_spec=pl.BlockSpec((None, t, ff_s), lambda j, k: (k, 0, 0)),
             b_spec=pl.BlockSpec((None, ff_s, tn_out), lambda j, k: (k, 0, j)),
             o_spec=pl.BlockSpec((t, tn_out), lambda j, k: (0, j)),
             out_shape=jax.ShapeDtypeStruct((t, d), F32), acc_shape=(t, tn_out),
             res=h1, res_spec=pl.BlockSpec((t, tn_out), lambda j, k: (0, j)))

    dh2, dh2_b, loss_part, d_norm_final = _loss_bwd(h2, target, norm_final_g.reshape(1, d), n_meta, rows_used)
    dact = _mm("down_dgrad", dh2_b, w_down_f, dims=NT, grid=(nhalf,), a_spec=full_t(d),
               b_spec=pl.BlockSpec((None, ff_s, d), lambda j: (j, 0, 0)),
               o_spec=pl.BlockSpec((None, t, ff_s), lambda j: (j, 0, 0)),
               out_shape=jax.ShapeDtypeStruct((nhalf, t, ff_s), F32))
    tn_w = 512
    dw_down = _mm("down_wgrad", act, dh2_b, dims=TN, grid=(nhalf, d // tn_w),
                  a_spec=pl.BlockSpec((None, t, ff_s), lambda j, n: (j, 0, 0)),
                  b_spec=pl.BlockSpec((t, tn_w), lambda j, n: (0, n)),
                  o_spec=pl.BlockSpec((None, ff_s, tn_w), lambda j, n: (j, 0, n)),
                  out_shape=jax.ShapeDtypeStruct((nhalf, ff_s, d), BF16))
    ex_down, token = _send_start("exchange_down_start", [dw_down.reshape(N_DEV, dn_s, d)], gather=False)
    dup, d_fw, d_fb = _ffn_bwd1(up_pre, dact, fw, fb, deps=(token,))
    dup_pre = _ffn_bwd2(dup, fw)
    dhn2 = _mm("up_dgrad", dup_pre, g_up, dims=NT, grid=(d // tn_w, N_DEV),
               a_spec=pl.BlockSpec((None, t, ff_s), lambda i, k: (k, 0, 0)),
               b_spec=pl.BlockSpec((None, tn_w, ff_s), lambda i, k: (k, i, 0)),
               o_spec=pl.BlockSpec((t, tn_w), lambda i, k: (0, i)),
               out_shape=jax.ShapeDtypeStruct((t, d), F32), acc_shape=(t, tn_w))
    dw_up = _mm("up_wgrad", hn2, dup_pre, dims=TN, grid=(N_DEV, d // tn_w),
                a_spec=pl.BlockSpec((t, tn_w), lambda j, i: (0, i)),
                b_spec=pl.BlockSpec((None, t, ff_s), lambda j, i: (j, 0, 0)),
                o_spec=pl.BlockSpec((None, tn_w, ff_s), lambda j, i: (j, i, 0)),
                out_shape=jax.ShapeDtypeStruct((N_DEV, d, ff_s), BF16))
    ex_up, token = _send_start("exchange_up_start", [dw_up], gather=False)
    dh1, dh1_b, d_norm_ffn = _rms_bwd_res("norm_ffn_bwd", dh2, dhn2, h1, norm_ffn_g, deps=(token,))
    dmixed = _mm("out_dgrad", dh1_b, w_out_f, dims=NT, grid=((dc + ds) // tn_out,), a_spec=full_t(d),
                 b_spec=pl.BlockSpec((tn_out, d), lambda i: (i, 0)),
                 o_spec=pl.BlockSpec((t, tn_out), lambda i: (0, i)),
                 out_shape=jax.ShapeDtypeStruct((t, dc + ds), F32))
    dw_out = _mm("out_wgrad", mixed, dh1_b, dims=TN, grid=((dc + ds) // tn_out,),
                 a_spec=pl.BlockSpec((t, tn_out), lambda i: (0, i)), b_spec=full_t(d),
                 o_spec=pl.BlockSpec((tn_out, d), lambda i: (i, 0)),
                 out_shape=jax.ShapeDtypeStruct((dc + ds, d), BF16))
    db_gate, dconv, dy_ssm, d_wglu, d_conv_w, d_gain_c, d_gain_s = _mix_bwd1(
        proj, y_ssm, z_glu, dmixed, w_glu_f, conv_w_f, gain_conv_out, gain_ssm_out)
    ex_mix, token = _send_start("exchange_mix_start", [
        dw_out.reshape(N_DEV, -1, d), d_wglu.astype(BF16).reshape(N_DEV, -1, ds),
        d_conv_w.reshape(3, N_DEV, dc_s).transpose(1, 0, 2), d_fw], gather=False)
    dcdv = _mix_bwd2(proj, dconv, conv_w_f, deps=(token,))
    (du_seg, d_bbm_re, d_bbm_im, d_a_re, d_a_im, d_ccm_re, d_ccm_im, d_dskip) = _s5_bwd(
        _to_segments(dy_ssm), u_seg, s_re, s_im, bbm_re, bbm_im, a_re_c, a_im_c, ccm_re, ccm_im, d_skip)
    dproj = jnp.concatenate([db_gate, dcdv, _from_segments(du_seg).astype(BF16)], axis=1)
    dhn1 = _mm("proj_dgrad", dproj, g_in, dims=NT, grid=(d // tn_w, N_DEV),
               a_spec=pl.BlockSpec((t, d_in_s), lambda i, k: (0, k)),
               b_spec=pl.BlockSpec((None, tn_w, d_in_s), lambda i, k: (k, i, 0)),
               o_spec=pl.BlockSpec((t, tn_w), lambda i, k: (0, i)),
               out_shape=jax.ShapeDtypeStruct((t, d), F32), acc_shape=(t, tn_w))

    from_bb = lambda a: _diag_blocks(a, gpc).transpose(0, 1, 3, 2).reshape(ns, grp)
    from_cc = lambda a: _diag_blocks(a, gpc).transpose(0, 1, 3, 2).reshape(n_groups * grp, n_state)
    d_lr, d_li, d_dt_e, d_br, d_bi = _disc_bwd(
        lr, li, log_dt_e, br, bi, d_a_re.reshape(ns, 1), d_a_im.reshape(ns, 1),
        from_bb(d_bbm_re), from_bb(d_bbm_im))

    rep2d = dict(
        ssm_lam_re=(n_groups, n_state), ssm_lam_im=(n_groups, n_state), ssm_log_dt=(1, n_groups),
        ssm_b_re=(ns, grp), ssm_b_im=(ns, grp), ssm_c_re=(n_groups * grp, n_state),
        ssm_c_im=(n_groups * grp, n_state), ssm_d=(n_groups, grp), gain_conv_out=(1, dc),
        gain_ssm_out=(1, ds), norm_ffn_g=(1, d), ffn_conv_b=(1, N_DEV * ff_s), norm_final_g=(1, d))
    rep_names = list(rep2d)
    rep_grads = dict(
        ssm_lam_re=d_lr, ssm_lam_im=d_li, ssm_log_dt=d_dt_e.reshape(n_groups, n_state).sum(axis=1),
        ssm_b_re=d_br, ssm_b_im=d_bi, ssm_c_re=from_cc(d_ccm_re), ssm_c_im=from_cc(d_ccm_im),
        ssm_d=d_dskip, gain_conv_out=d_gain_c, gain_ssm_out=d_gain_s, norm_ffn_g=d_norm_ffn,
        ffn_conv_b=d_fb, norm_final_g=d_norm_final)
    rep_shapes = [rep2d[n] for n in rep_names] + [(1, 1)]
    rep_rows = _packed_rows(rep_shapes)
    rep_pack = _pack([rep_grads[n] for n in rep_names] + [loss_part], rep_rows)
    ex_rep, rep_token = _send_start("gather_small_grads_start", [rep_pack], gather=True)

    dh0, _, d_norm_mix = _rms_bwd_res("norm_mix_bwd", dh1, dhn1, h0, norm_mix_g, deps=(rep_token,))
    dw_in = _mm("proj_wgrad", hn1, dproj, dims=TN, grid=(N_DEV, d // tn_w),
                a_spec=pl.BlockSpec((t, tn_w), lambda j, i: (0, i)),
                b_spec=pl.BlockSpec((t, d_in_s), lambda j, i: (0, j)),
                o_spec=pl.BlockSpec((None, tn_w, d_in_s), lambda j, i: (j, i, 0)),
                out_shape=jax.ShapeDtypeStruct((N_DEV, d, d_in_s), BF16))
    grad_x = dh0[n_meta:rows_used][None]
    d_meta_b = dh0[:n_meta].reshape(n_meta, N_DEV, d_meta).transpose(1, 0, 2)
    ex_in, ex_in_token = _send_start("exchange_in_start", [dw_in, d_meta_b, d_norm_mix],
                                     gather=[False, False, True])

    shard_out = {}
    (rep_parts,) = _send_wait("gather_small_grads_wait", ex_rep, (0,), ex_in_token)
    rep_sum = _sum_parts("sum_small_grads", rep_parts)
    *rep_g, loss = _unpack(rep_sum, rep_shapes)
    loss = loss.reshape(())
    as2d = lambda tree: [tree[n].reshape(rep2d[n]) for n in rep_names]
    rep_res = _adamw_many("adamw_replicated", rep_g, as2d(weights), as2d(mom_m), as2d(mom_v))
    for i, n in enumerate(rep_names):
        shard_out[n] = [r.reshape(weights[n].shape) for r in (rep_g[i], *(res[i] for res in rep_res))]

    def update(n, parts):
        sh = weights[n].shape
        two_d = lambda a: a.reshape(parts.shape[1:])
        res = _adamw_reduce("adamw_" + n, parts, two_d(weights[n]), two_d(mom_m[n]), two_d(mom_v[n]))
        shard_out[n] = [r.reshape(sh) for r in res]
        return res[0]

    (p_down,) = _send_wait("exchange_down_wait", ex_down, (0,), rep_sum)
    done = update("w_down", p_down)
    (p_up,) = _send_wait("exchange_up_wait", ex_up, (0,), done)
    done = update("w_up", p_up)
    p_out, p_glu, p_cw, p_fw = _send_wait("exchange_mix_wait", ex_mix, (0, 1, 2, 3), done)
    update("w_out", p_out)
    update("ssm_w_glu", p_glu)
    update("conv_w", p_cw)
    done = update("ffn_conv_w", p_fw)
    p_in, p_meta, p_nm = _send_wait("exchange_in_wait", ex_in, (0, 1, 2), done)
    update("w_in", p_in)
    update("meta_tokens", p_meta)
    update("norm_mix_g", p_nm)

    grads = [shard_out[n][0] for n in names]
    deltas = [shard_out[n][1] for n in names]
    new_m = [shard_out[n][2] for n in names]
    new_v = [shard_out[n][3] for n in names]
    return (loss, grad_x, *grads, *deltas, *new_m, *new_v)
```

```python
import functools
import math

import jax
import jax.numpy as jnp
from jax import lax
from jax.experimental import pallas as pl
from jax.experimental.pallas import tpu as pltpu

F32 = jnp.float32
BF16 = jnp.bfloat16
MESH = pl.DeviceIdType.MESH

N_DEV = 8
RMS_EPS = 1e-6
ADAM_LR = 0.001
ADAM_B1 = 0.9
ADAM_B2 = 0.999
ADAM_EPS = 1e-08
ADAM_WD = 0.01
ADAM_STEP = 10
ADAM_BC1 = 1.0 - ADAM_B1 ** ADAM_STEP
ADAM_BC2 = 1.0 - ADAM_B2 ** ADAM_STEP

SUBLANES = 8
LANES = 128
ROW_TILE = 128
ROW_CHUNK = 32
N_SEG = 8
HALO_ROWS = 16
SSM_CHUNK = 128
VMEM_LIMIT = 48 * 1024 * 1024

NN = ((1,), (0,))
NT = ((1,), (1,))
TN = ((0,), (0,))


def _params(**kw):
    return pltpu.CompilerParams(vmem_limit_bytes=VMEM_LIMIT, **kw)


def _dot(a, b, dims):
    return lax.dot_general(a, b, (dims, ((), ())), preferred_element_type=F32)


def _mean_sq_rsqrt(x):
    return lax.rsqrt(jnp.mean(x * x, axis=-1, keepdims=True) + RMS_EPS)


def _rms_bwd(x, r, g, dy):
    xhat = x * r
    dxh = dy * g
    dx = r * (dxh - xhat * jnp.mean(dxh * xhat, axis=-1, keepdims=True))
    return dx, dy * xhat


def _gelu(y):
    c = math.sqrt(2.0 / math.pi)
    t = jnp.tanh(c * (y + 0.044715 * y * y * y))
    return 0.5 * y * (1.0 + t), t


def _gelu_grad(y, t):
    c = math.sqrt(2.0 / math.pi)
    return 0.5 * (1.0 + t) + 0.5 * y * (1.0 - t * t) * c * (1.0 + 3.0 * 0.044715 * y * y)


def _wrap_prev_halo(halo, first):
    seg = lax.broadcasted_iota(jnp.int32, halo.shape, 0) % N_SEG
    wrapped = jnp.where(seg == 0, 0.0, pltpu.roll(halo, 1, 0))
    return jnp.where(first, wrapped, halo)


def _wrap_next_halo(halo, last):
    seg = lax.broadcasted_iota(jnp.int32, halo.shape, 0) % N_SEG
    wrapped = jnp.where(seg == N_SEG - 1, 0.0, pltpu.roll(halo, halo.shape[0] - 1, 0))
    return jnp.where(last, wrapped, halo)


def _rows_before(get, halo, r, rows, back):
    lo = r - back
    if lo >= 0:
        return get(lo, rows)
    return jnp.concatenate([halo[HALO_ROWS + lo:], get(0, rows + lo)], axis=0)


def _rows_after(get, halo, r, rows, ahead, tile_rows):
    over = r + ahead + rows - tile_rows
    if over <= 0:
        return get(r + ahead, rows)
    return jnp.concatenate([get(r + ahead, rows - over), halo[:over]], axis=0)


def _dev_index(p):
    return 4 * p[0] + 2 * p[1] + p[2]


def _allgather(name, shards, deps=()):
    n = len(shards)

    def body(*refs):
        ins, outs = refs[:n], refs[n:2 * n]
        send_sems, recv_sems, local_sems = refs[2 * n:]
        x, y, c = lax.axis_index("x"), lax.axis_index("y"), lax.axis_index("c")
        me, sibling = (x, y, c), (x, y, 1 - c)
        chips = [(1 - x, y), (x, 1 - y), (1 - x, 1 - y)]

        def copy(a, k, block, to, src=None):
            dst = outs[a].at[_dev_index(block)]
            return pltpu.make_async_remote_copy(
                src_ref=dst if src is None else src, dst_ref=dst,
                send_sem=send_sems.at[a, k], recv_sem=recv_sems.at[a, k],
                device_id=to, device_id_type=MESH)

        mine = [pltpu.make_async_copy(ins[a], outs[a].at[_dev_index(me)], local_sems.at[a])
                for a in range(n)]
        for cp in mine:
            cp.start()
        first = []
        for a in range(n):
            first.append(copy(a, 0, me, sibling, src=ins[a]))
            for j, chip in enumerate(chips):
                first.append(copy(a, 1 + j, me, (*chip, c), src=ins[a]))
        for cp in first:
            cp.start()
        passed = []
        for j, chip in enumerate(chips):
            for a in range(n):
                copy(a, 1 + j, (*chip, c), me).wait_recv()
                fwd = copy(a, 4 + j, (*chip, c), sibling)
                fwd.start()
                passed.append(fwd)
        for a in range(n):
            copy(a, 0, sibling, me).wait_recv()
            for j, chip in enumerate(chips):
                copy(a, 4 + j, (*chip, 1 - c), me).wait_recv()
        for cp in first + passed:
            cp.wait_send()
        for cp in mine:
            cp.wait()

    any_spec = pl.BlockSpec(memory_space=pl.ANY)
    return pl.pallas_call(
        _ignoring_deps(body, n, deps), name=name,
        out_shape=[jax.ShapeDtypeStruct((N_DEV,) + s.shape, s.dtype) for s in shards],
        in_specs=[any_spec] * (n + len(deps)), out_specs=[any_spec] * n,
        scratch_shapes=[pltpu.SemaphoreType.DMA((n, 7)), pltpu.SemaphoreType.DMA((n, 7)),
                        pltpu.SemaphoreType.DMA((n,))],
    )(*shards, *deps)


HBM_SPEC = pl.BlockSpec(memory_space=pltpu.HBM)
SEM_SPEC = pl.BlockSpec(memory_space=pltpu.SEMAPHORE)
ANY_SPEC = pl.BlockSpec(memory_space=pl.ANY)
DATAFLOW = pltpu.SideEffectType.DATAFLOW_SIDE_EFFECTING


def _ignoring_deps(body, n_in, deps):
    n_dep = len(deps)

    def wrapped(*refs):
        return body(*refs[:n_in], *refs[n_in + n_dep:])

    return wrapped


def _my_position():
    x, y, c = lax.axis_index("x"), lax.axis_index("y"), lax.axis_index("c")
    return (x, y, c)


def _peer(me, k):
    return tuple((1 - v) if (k >> s) & 1 else v for v, s in zip(me, (2, 1, 0)))


def _split_copies(src_refs, land_refs, send_sems, recv_sems, gather):
    me = _my_position()
    copies = []
    for a, (src, land) in enumerate(zip(src_refs, land_refs)):
        for k in range(1, N_DEV):
            peer = _peer(me, k)
            copies.append(pltpu.make_async_remote_copy(
                src_ref=src if gather[a] else src.at[_dev_index(peer)], dst_ref=land.at[_dev_index(me)],
                send_sem=send_sems[a].at[k - 1], recv_sem=recv_sems[a].at[k - 1],
                device_id=peer, device_id_type=MESH))
    return copies


def _own_slot(block, like_shape):
    me = _dev_index(_my_position())
    return lax.dynamic_update_index_in_dim(lax.empty(like_shape, block.dtype), block, me, 0)


def _send_start(name, srcs, gather):
    n = len(srcs)
    me = _dev_index(_my_position())
    gather = [gather] * n if isinstance(gather, bool) else list(gather)
    lands = [_own_slot(s, (N_DEV,) + s.shape) if g else
             _own_slot(lax.dynamic_index_in_dim(s, me, 0, keepdims=False), s.shape)
             for s, g in zip(srcs, gather)]

    def body(*refs):
        src_refs, land_refs = refs[:n], refs[n:2 * n]
        send_sems, recv_sems = refs[2 * n:3 * n], refs[3 * n:4 * n]
        token = refs[-1]
        for cp in _split_copies(src_refs, land_refs, send_sems, recv_sems, gather):
            cp.start()
        token[...] = jnp.zeros_like(token)

    hbm = lambda a: pltpu.HBM(a.shape, a.dtype)
    sems = [pltpu.SemaphoreType.DMA((N_DEV - 1,))] * n
    outs = pl.pallas_call(
        body, name=name,
        out_shape=(*sems, *sems, *[hbm(s) for s in srcs], *[hbm(l) for l in lands],
                   jax.ShapeDtypeStruct((SUBLANES, LANES), F32)),
        in_specs=[HBM_SPEC] * (2 * n),
        out_specs=(*[SEM_SPEC] * (2 * n), *[HBM_SPEC] * (2 * n), pl.BlockSpec(memory_space=pltpu.VMEM)),
        input_output_aliases={i: 2 * n + i for i in range(2 * n)},
        compiler_params=pltpu.CompilerParams(has_side_effects=DATAFLOW),
    )(*[pltpu.with_memory_space_constraint(a, pltpu.HBM) for a in (*srcs, *lands)])
    state = dict(send=outs[:n], recv=outs[n:2 * n], srcs=outs[2 * n:3 * n], lands=outs[3 * n:4 * n],
                 gather=gather)
    return state, outs[-1]


def _send_wait(name, state, which, after):
    n = len(which)
    pick = lambda key: [state[key][i] for i in which]
    gather = pick("gather")

    def body(*refs):
        src_refs, land_refs = refs[:n], refs[n:2 * n]
        send_sems, recv_sems = refs[2 * n:3 * n], refs[3 * n:4 * n]
        for cp in _split_copies(src_refs, land_refs, send_sems, recv_sems, gather):
            cp.wait_send()
            cp.wait_recv()

    srcs, lands = pick("srcs"), pick("lands")
    hbm = lambda a: pltpu.HBM(a.shape, a.dtype)
    outs = pl.pallas_call(
        body, name=name,
        out_shape=(*[hbm(s) for s in srcs], *[hbm(l) for l in lands]),
        in_specs=[*[HBM_SPEC] * (2 * n), *[SEM_SPEC] * (2 * n), ANY_SPEC],
        out_specs=tuple([HBM_SPEC] * (2 * n)),
        input_output_aliases={i: i for i in range(2 * n)},
        compiler_params=pltpu.CompilerParams(has_side_effects=DATAFLOW),
    )(*srcs, *lands, *pick("send"), *pick("recv"), after)
    return outs[n:]


def _mm(name, a, b, *, dims, grid, a_spec, b_spec, o_spec, out_shape, acc_shape=None,
        res=None, res_spec=None):
    n_red = grid[-1] if acc_shape is not None else 1
    red_axis = len(grid) - 1

    def body(*refs):
        a_ref, b_ref = refs[0], refs[1]
        r_ref = refs[2] if res is not None else None
        o_ref = refs[3] if res is not None else refs[2]
        part = _dot(a_ref[...], b_ref[...], dims)
        if acc_shape is None:
            if r_ref is not None:
                part = part + r_ref[...]
            o_ref[...] = part.astype(o_ref.dtype)
            return
        acc_ref = refs[-1]
        k = pl.program_id(red_axis)

        @pl.when(k == 0)
        def _():
            acc_ref[...] = part

        @pl.when(k > 0)
        def _():
            acc_ref[...] += part

        @pl.when(k == n_red - 1)
        def _():
            total = acc_ref[...]
            if r_ref is not None:
                total = total + r_ref[...]
            o_ref[...] = total.astype(o_ref.dtype)

    ins, in_specs = [a, b], [a_spec, b_spec]
    if res is not None:
        ins.append(res)
        in_specs.append(res_spec)
    return pl.pallas_call(
        body, name=name, grid=grid, in_specs=in_specs, out_specs=o_spec, out_shape=out_shape,
        scratch_shapes=[pltpu.VMEM(acc_shape, F32)] if acc_shape is not None else [],
        compiler_params=_params(),
    )(*ins)


def _rms_fwd(name, h, g):
    t, d = h.shape

    def body(h_ref, g_ref, o_ref):
        x = h_ref[...]
        o_ref[...] = (x * _mean_sq_rsqrt(x) * g_ref[...]).astype(BF16)

    return pl.pallas_call(
        body, name=name, grid=(t // ROW_TILE,),
        in_specs=[pl.BlockSpec((ROW_TILE, d), lambda i: (i, 0)), pl.BlockSpec((1, d), lambda i: (0, 0))],
        out_specs=pl.BlockSpec((ROW_TILE, d), lambda i: (i, 0)),
        out_shape=jax.ShapeDtypeStruct((t, d), BF16), compiler_params=_params(),
    )(h, g)


def _rms_bwd_res(name, dres, dhn, h, g, deps=()):
    t, d = h.shape

    def body(dres_ref, dhn_ref, h_ref, g_ref, dh_ref, dhb_ref, dg_ref):
        x = h_ref[...]
        dx, dgt = _rms_bwd(x, _mean_sq_rsqrt(x), g_ref[...], dhn_ref[...])
        dh = dres_ref[...] + dx
        dh_ref[...] = dh
        dhb_ref[...] = dh.astype(BF16)

        @pl.when(pl.program_id(0) == 0)
        def _():
            dg_ref[...] = jnp.zeros_like(dg_ref)

        dg_ref[...] += jnp.sum(dgt, axis=0, keepdims=True)

    row = pl.BlockSpec((ROW_TILE, d), lambda i: (i, 0))
    vec = pl.BlockSpec((1, d), lambda i: (0, 0))
    return pl.pallas_call(
        _ignoring_deps(body, 4, deps), name=name, grid=(t // ROW_TILE,),
        in_specs=[row, row, row, vec] + [ANY_SPEC] * len(deps), out_specs=[row, row, vec],
        out_shape=[jax.ShapeDtypeStruct((t, d), F32), jax.ShapeDtypeStruct((t, d), BF16),
                   jax.ShapeDtypeStruct((1, d), F32)],
        compiler_params=_params(),
    )(dres, dhn, h, g, *deps)


def _loss_bwd(h2, target, g, row_lo, row_hi):
    t, d = h2.shape

    def body(h_ref, tg_ref, g_ref, dh_ref, dhb_ref, loss_ref, dg_ref):
        i = pl.program_id(0)
        x = h_ref[...]
        r = _mean_sq_rsqrt(x)
        gv = g_ref[...]
        y = x * r * gv
        rows = i * ROW_TILE + lax.broadcasted_iota(jnp.int32, (ROW_TILE, 1), 0)
        time = (rows % N_SEG) * (t // N_SEG) + rows // N_SEG
        valid = jnp.logical_and(time >= row_lo, time < row_hi)
        err = jnp.where(valid, y - tg_ref[...], 0.0)
        dy = err * (1.0 / d)
        dx, dgt = _rms_bwd(x, r, gv, dy)
        dh_ref[...] = dx
        dhb_ref[...] = dx.astype(BF16)

        @pl.when(i == 0)
        def _():
            loss_ref[...] = jnp.zeros_like(loss_ref)
            dg_ref[...] = jnp.zeros_like(dg_ref)

        row_loss = jnp.mean(err * err, axis=-1, keepdims=True)
        loss_ref[...] += 0.5 * jnp.sum(row_loss, axis=0, keepdims=True)
        dg_ref[...] += jnp.sum(dgt, axis=0, keepdims=True)

    row = pl.BlockSpec((ROW_TILE, d), lambda i: (i, 0))
    vec = pl.BlockSpec((1, d), lambda i: (0, 0))
    return pl.pallas_call(
        body, name="loss_bwd", grid=(t // ROW_TILE,), in_specs=[row, row, vec],
        out_specs=[row, row, pl.BlockSpec((1, 1), lambda i: (0, 0)), vec],
        out_shape=[jax.ShapeDtypeStruct((t, d), F32), jax.ShapeDtypeStruct((t, d), BF16),
                   jax.ShapeDtypeStruct((1, 1), F32), jax.ShapeDtypeStruct((1, d), F32)],
        compiler_params=_params(),
    )(h2, target, g)


def _prev_halo(i, t):
    return jnp.where(i == 0, t // HALO_ROWS - 1, i * (ROW_TILE // HALO_ROWS) - 1)


def _next_halo(i, t):
    return jnp.where(i == t // ROW_TILE - 1, 0, (i + 1) * (ROW_TILE // HALO_ROWS))


def _causal_taps(cur, halo, first):
    rows = cur.shape[0]
    ext = jnp.concatenate([_wrap_prev_halo(halo, first), cur], axis=0)
    return ext[:rows], ext[N_SEG:N_SEG + rows]


def _anticausal_taps(cur, halo, last):
    rows = cur.shape[0]
    ext = jnp.concatenate([cur, _wrap_next_halo(halo, last)], axis=0)
    return ext[N_SEG:N_SEG + rows], ext[2 * N_SEG:2 * N_SEG + rows]


def _mix_fwd(proj, y, w_glu, conv_w, gain_c, gain_s):
    t = proj.shape[0]
    dc = conv_w.shape[1]
    ds = y.shape[1]

    def body(p_ref, halo_ref, y_ref, wg_ref, cw_ref, gc_ref, gs_ref, mixed_ref, z_ref):
        i = pl.program_id(0)
        p = p_ref[...]
        b, c, v = p[:, :dc], p[:, dc:2 * dc], p[:, 2 * dc:3 * dc]
        cv = c * v
        hp = halo_ref[...]
        x2, x1 = _causal_taps(cv, hp[:, dc:2 * dc] * hp[:, 2 * dc:3 * dc], i == 0)
        cw = cw_ref[...]
        conv = cw[0:1] * x2 + cw[1:2] * x1 + cw[2:3] * cv
        co = b * conv
        mixed_ref[:, :dc] = (co * _mean_sq_rsqrt(co) * gc_ref[...]).astype(BF16)
        g, _ = _gelu(y_ref[...])
        z = _dot(g.astype(BF16), wg_ref[...], NN)
        z_ref[...] = z
        so = g * jax.nn.sigmoid(z)
        mixed_ref[:, dc:] = (so * _mean_sq_rsqrt(so) * gs_ref[...]).astype(BF16)

    const = lambda i: (0, 0)
    return pl.pallas_call(
        body, name="mix_fwd", grid=(t // ROW_TILE,),
        in_specs=[pl.BlockSpec((ROW_TILE, 3 * dc), lambda i: (i, 0)),
                  pl.BlockSpec((HALO_ROWS, 3 * dc), lambda i: (_prev_halo(i, t), 0)),
                  pl.BlockSpec((ROW_TILE, ds), lambda i: (i, 0)),
                  pl.BlockSpec((ds, ds), const), pl.BlockSpec(conv_w.shape, const),
                  pl.BlockSpec((1, dc), const), pl.BlockSpec((1, ds), const)],
        out_specs=[pl.BlockSpec((ROW_TILE, dc + ds), lambda i: (i, 0)),
                   pl.BlockSpec((ROW_TILE, ds), lambda i: (i, 0))],
        out_shape=[jax.ShapeDtypeStruct((t, dc + ds), BF16), jax.ShapeDtypeStruct((t, ds), F32)],
        compiler_params=_params(),
    )(proj, proj, y, w_glu, conv_w, gain_c, gain_s)


def _mix_bwd1(proj, y, z, dmixed, w_glu, conv_w, gain_c, gain_s):
    t = proj.shape[0]
    dc = conv_w.shape[1]
    ds = y.shape[1]

    def body(p_ref, halo_ref, y_ref, z_ref, dm_ref, wg_ref, cw_ref, gc_ref, gs_ref,
             db_ref, dconv_ref, dy_ref, dwg_ref, dcw_ref, dgc_ref, dgs_ref):
        i = pl.program_id(0)

        @pl.when(i == 0)
        def _():
            dwg_ref[...] = jnp.zeros_like(dwg_ref)
            dcw_ref[...] = jnp.zeros_like(dcw_ref)
            dgc_ref[...] = jnp.zeros_like(dgc_ref)
            dgs_ref[...] = jnp.zeros_like(dgs_ref)

        p = p_ref[...]
        b, c, v = p[:, :dc], p[:, dc:2 * dc], p[:, 2 * dc:3 * dc]
        cv = c * v
        hp = halo_ref[...]
        x2, x1 = _causal_taps(cv, hp[:, dc:2 * dc] * hp[:, 2 * dc:3 * dc], i == 0)
        cw = cw_ref[...]
        conv = cw[0:1] * x2 + cw[1:2] * x1 + cw[2:3] * cv
        co = b * conv
        dm = dm_ref[...]
        dco, dgt = _rms_bwd(co, _mean_sq_rsqrt(co), gc_ref[...], dm[:, :dc])
        dgc_ref[...] += jnp.sum(dgt, axis=0, keepdims=True)
        db_ref[...] = (dco * conv).astype(BF16)
        dconv = dco * b
        dconv_ref[...] = dconv
        dcw_ref[0:1, :] += jnp.sum(dconv * x2, axis=0, keepdims=True)
        dcw_ref[1:2, :] += jnp.sum(dconv * x1, axis=0, keepdims=True)
        dcw_ref[2:3, :] += jnp.sum(dconv * cv, axis=0, keepdims=True)

        yv = y_ref[...]
        g, th = _gelu(yv)
        sg = jax.nn.sigmoid(z_ref[...])
        so = g * sg
        dso, dgt = _rms_bwd(so, _mean_sq_rsqrt(so), gs_ref[...], dm[:, dc:])
        dgs_ref[...] += jnp.sum(dgt, axis=0, keepdims=True)
        dz = (dso * g * sg * (1.0 - sg)).astype(BF16)
        dg = dso * sg + _dot(dz, wg_ref[...], NT)
        dwg_ref[...] += _dot(g.astype(BF16), dz, TN)
        dy_ref[...] = dg * _gelu_grad(yv, th)

    const = lambda i: (0, 0)
    row = lambda w: pl.BlockSpec((ROW_TILE, w), lambda i: (i, 0))
    return pl.pallas_call(
        body, name="mix_bwd1", grid=(t // ROW_TILE,),
        in_specs=[row(3 * dc), pl.BlockSpec((HALO_ROWS, 3 * dc), lambda i: (_prev_halo(i, t), 0)),
                  row(ds), row(ds), row(dc + ds), pl.BlockSpec((ds, ds), const),
                  pl.BlockSpec(conv_w.shape, const), pl.BlockSpec((1, dc), const),
                  pl.BlockSpec((1, ds), const)],
        out_specs=[row(dc), row(dc), row(ds), pl.BlockSpec((ds, ds), const),
                   pl.BlockSpec(conv_w.shape, const), pl.BlockSpec((1, dc), const),
                   pl.BlockSpec((1, ds), const)],
        out_shape=[jax.ShapeDtypeStruct((t, dc), BF16), jax.ShapeDtypeStruct((t, dc), F32),
                   jax.ShapeDtypeStruct((t, ds), F32), jax.ShapeDtypeStruct((ds, ds), F32),
                   jax.ShapeDtypeStruct(conv_w.shape, F32), jax.ShapeDtypeStruct((1, dc), F32),
                   jax.ShapeDtypeStruct((1, ds), F32)],
        compiler_params=_params(),
    )(proj, proj, y, z, dmixed, w_glu, conv_w, gain_c, gain_s)


def _mix_bwd2(proj, dconv, conv_w, deps=()):
    t = proj.shape[0]
    dc = conv_w.shape[1]
    n_tiles = t // ROW_TILE

    def body(c_ref, v_ref, d_ref, halo_ref, cw_ref, o_ref):
        i = pl.program_id(0)
        d = d_ref[...]
        u1, u2 = _anticausal_taps(d, halo_ref[...], i == n_tiles - 1)
        cw = cw_ref[...]
        dcv = cw[2:3] * d + cw[1:2] * u1 + cw[0:1] * u2
        o_ref[:, :dc] = (dcv * v_ref[...]).astype(BF16)
        o_ref[:, dc:] = (dcv * c_ref[...]).astype(BF16)

    return pl.pallas_call(
        _ignoring_deps(body, 5, deps), name="mix_bwd2", grid=(n_tiles,),
        in_specs=[pl.BlockSpec((ROW_TILE, dc), lambda i: (i, 1)),
                  pl.BlockSpec((ROW_TILE, dc), lambda i: (i, 2)),
                  pl.BlockSpec((ROW_TILE, dc), lambda i: (i, 0)),
                  pl.BlockSpec((HALO_ROWS, dc), lambda i: (_next_halo(i, t), 0)),
                  pl.BlockSpec(conv_w.shape, lambda i: (0, 0))] + [ANY_SPEC] * len(deps),
        out_specs=pl.BlockSpec((ROW_TILE, 2 * dc), lambda i: (i, 0)),
        out_shape=jax.ShapeDtypeStruct((t, 2 * dc), BF16), compiler_params=_params(),
    )(proj, proj, dconv, dconv, conv_w, *deps)


def _ffn_taps(up_ref, halo, fw_ref, fb_ref, q, r):
    get = lambda s, n: up_ref[q, pl.ds(s, n), :]
    x0 = get(r, ROW_CHUNK)
    x1 = _rows_before(get, halo, r, ROW_CHUNK, N_SEG)
    x2 = _rows_before(get, halo, r, ROW_CHUNK, 2 * N_SEG)
    w = fw_ref[q]
    return w[0:1] * x2 + w[1:2] * x1 + w[2:3] * x0 + fb_ref[q], x2, x1, x0


def _ffn_fwd(up_pre, fw, fb):
    nb, t, w = up_pre.shape
    half = nb // 2

    def body(up_ref, halo_ref, fw_ref, fb_ref, act_ref):
        first = pl.program_id(0) == 0
        for q in range(half):
            halo_a = _wrap_prev_halo(halo_ref[q], first)
            halo_v = _wrap_prev_halo(halo_ref[q + half], first)
            for r in range(0, ROW_TILE, ROW_CHUNK):
                a = _ffn_taps(up_ref, halo_a, fw_ref, fb_ref, q, r)[0]
                val = _ffn_taps(up_ref, halo_v, fw_ref, fb_ref, q + half, r)[0]
                act_ref[q, pl.ds(r, ROW_CHUNK), :] = (a * jax.nn.sigmoid(a) * val).astype(BF16)

    return pl.pallas_call(
        body, name="ffn_fwd", grid=(t // ROW_TILE,),
        in_specs=[pl.BlockSpec((nb, ROW_TILE, w), lambda i: (0, i, 0)),
                  pl.BlockSpec((nb, HALO_ROWS, w), lambda i: (0, _prev_halo(i, t), 0)),
                  pl.BlockSpec(fw.shape, lambda i: (0, 0, 0)),
                  pl.BlockSpec(fb.shape, lambda i: (0, 0, 0))],
        out_specs=pl.BlockSpec((half, ROW_TILE, w), lambda i: (0, i, 0)),
        out_shape=jax.ShapeDtypeStruct((half, t, w), BF16), compiler_params=_params(),
    )(up_pre, up_pre, fw, fb)


def _ffn_bwd1(up_pre, dact, fw, fb, deps=()):
    nb, t, w = up_pre.shape
    half = nb // 2

    def body(up_ref, halo_ref, dact_ref, fw_ref, fb_ref, dup_ref, dfw_ref, dfb_ref):
        first = pl.program_id(0) == 0

        @pl.when(first)
        def _():
            dfw_ref[...] = jnp.zeros_like(dfw_ref)
            dfb_ref[...] = jnp.zeros_like(dfb_ref)

        for q in range(half):
            halos = {q: _wrap_prev_halo(halo_ref[q], first),
                     q + half: _wrap_prev_halo(halo_ref[q + half], first)}
            sums = {blk: [jnp.zeros((1, w), F32)] * 4 for blk in halos}
            for r in range(0, ROW_TILE, ROW_CHUNK):
                taps_a = _ffn_taps(up_ref, halos[q], fw_ref, fb_ref, q, r)
                taps_v = _ffn_taps(up_ref, halos[q + half], fw_ref, fb_ref, q + half, r)
                a, val = taps_a[0], taps_v[0]
                da_ct = dact_ref[q, pl.ds(r, ROW_CHUNK), :]
                sg = jax.nn.sigmoid(a)
                da = da_ct * val * sg * (1.0 + a * (1.0 - sg))
                dval = da_ct * a * sg
                for blk, dv, taps in ((q, da, taps_a), (q + half, dval, taps_v)):
                    dup_ref[blk, pl.ds(r, ROW_CHUNK), :] = dv
                    terms = (dv, dv * taps[1], dv * taps[2], dv * taps[3])
                    sums[blk] = [s + jnp.sum(v, axis=0, keepdims=True) for s, v in zip(sums[blk], terms)]
            for blk, (s_b, s_w0, s_w1, s_w2) in sums.items():
                dfb_ref[blk] += s_b
                for k, s_w in enumerate((s_w0, s_w1, s_w2)):
                    dfw_ref[blk, k:k + 1, :] += s_w

    return pl.pallas_call(
        _ignoring_deps(body, 5, deps), name="ffn_bwd1", grid=(t // ROW_TILE,),
        in_specs=[pl.BlockSpec((nb, ROW_TILE, w), lambda i: (0, i, 0)),
                  pl.BlockSpec((nb, HALO_ROWS, w), lambda i: (0, _prev_halo(i, t), 0)),
                  pl.BlockSpec((half, ROW_TILE, w), lambda i: (0, i, 0)),
                  pl.BlockSpec(fw.shape, lambda i: (0, 0, 0)),
                  pl.BlockSpec(fb.shape, lambda i: (0, 0, 0))] + [ANY_SPEC] * len(deps),
        out_specs=[pl.BlockSpec((nb, ROW_TILE, w), lambda i: (0, i, 0)),
                   pl.BlockSpec(fw.shape, lambda i: (0, 0, 0)),
                   pl.BlockSpec(fb.shape, lambda i: (0, 0, 0))],
        out_shape=[jax.ShapeDtypeStruct((nb, t, w), F32), jax.ShapeDtypeStruct(fw.shape, F32),
                   jax.ShapeDtypeStruct(fb.shape, F32)],
        compiler_params=_params(),
    )(up_pre, up_pre, dact, fw, fb, *deps)


def _ffn_bwd2(dup, fw):
    nb, t, w = dup.shape
    n_tiles = t // ROW_TILE

    def body(d_ref, halo_ref, fw_ref, o_ref):
        last = pl.program_id(0) == n_tiles - 1
        for q in range(nb):
            halo = _wrap_next_halo(halo_ref[q], last)
            get = lambda s, n: d_ref[q, pl.ds(s, n), :]
            cw = fw_ref[q]
            for r in range(0, ROW_TILE, ROW_CHUNK):
                d = get(r, ROW_CHUNK)
                u1 = _rows_after(get, halo, r, ROW_CHUNK, N_SEG, ROW_TILE)
                u2 = _rows_after(get, halo, r, ROW_CHUNK, 2 * N_SEG, ROW_TILE)
                o_ref[q, pl.ds(r, ROW_CHUNK), :] = (cw[2:3] * d + cw[1:2] * u1 + cw[0:1] * u2).astype(BF16)

    return pl.pallas_call(
        body, name="ffn_bwd2", grid=(n_tiles,),
        in_specs=[pl.BlockSpec((nb, ROW_TILE, w), lambda i: (0, i, 0)),
                  pl.BlockSpec((nb, HALO_ROWS, w), lambda i: (0, _next_halo(i, t), 0)),
                  pl.BlockSpec(fw.shape, lambda i: (0, 0, 0))],
        out_specs=pl.BlockSpec((nb, ROW_TILE, w), lambda i: (0, i, 0)),
        out_shape=jax.ShapeDtypeStruct((nb, t, w), BF16), compiler_params=_params(),
    )(dup, dup, fw)


def _to_segments(a):
    t, c = a.shape
    return a.reshape(N_SEG, t // N_SEG, c).transpose(1, 0, 2).reshape(t, c)


def _from_segments(a):
    t, c = a.shape
    return a.reshape(t // N_SEG, N_SEG, c).transpose(1, 0, 2).reshape(t, c)


def _cmul(ar, ai, br, bi):
    return ar * br - ai * bi, ar * bi + ai * br


def _segment_carries(pr, pi, fr, fi, forward):
    row = lax.broadcasted_iota(jnp.int32, fr.shape, 0)
    edge = row == (0 if forward else N_SEG - 1)
    shift = 1 if forward else N_SEG - 1
    sr, si = jnp.zeros_like(fr), jnp.zeros_like(fi)
    for _ in range(N_SEG - 1):
        tr, ti = _cmul(pr, pi, sr, si)
        sr = jnp.where(edge, 0.0, pltpu.roll(tr + fr, shift, 0))
        si = jnp.where(edge, 0.0, pltpu.roll(ti + fi, shift, 0))
    return sr, si


def _rows(i):
    return pl.ds(pl.multiple_of(i * SUBLANES, SUBLANES), SUBLANES)


def _s5_fwd(proj, u_col, bb_re, bb_im, a_re, a_im, cc_re, cc_im, d_skip):
    t = proj.shape[0]
    nch, _, cs = bb_re.shape
    ds = nch * SSM_CHUNK
    u_blk = u_col // SSM_CHUNK
    steps = t // N_SEG

    def body(u_ref, bbr_ref, bbi_ref, ar_ref, ai_ref, ccr_ref, cci_ref, d_ref, sr_ref, si_ref, y_ref):
        ub = u_ref[...].astype(BF16)
        sr_ref[...] = _dot(ub, bbr_ref[...], NN)
        si_ref[...] = _dot(ub, bbi_ref[...], NN)
        ar = jnp.broadcast_to(ar_ref[...], (N_SEG, cs))
        ai = jnp.broadcast_to(ai_ref[...], (N_SEG, cs))
        zero = jnp.zeros((N_SEG, cs), F32)

        def totals(i, carry):
            sr, si, pr, pi = carry
            tr, ti = _cmul(ar, ai, sr, si)
            qr, qi = _cmul(ar, ai, pr, pi)
            return tr + sr_ref[_rows(i), :], ti + si_ref[_rows(i), :], qr, qi

        fr, fi, pr, pi = lax.fori_loop(0, steps, totals, (zero, zero, zero + 1.0, zero))
        s0r, s0i = _segment_carries(pr, pi, fr, fi, True)

        def scan(i, carry):
            tr, ti = _cmul(ar, ai, *carry)
            nr, ni = tr + sr_ref[_rows(i), :], ti + si_ref[_rows(i), :]
            sr_ref[_rows(i), :] = nr
            si_ref[_rows(i), :] = ni
            return nr, ni

        lax.fori_loop(0, steps, scan, (s0r, s0i))
        y_ref[...] = (_dot(sr_ref[...].astype(BF16), ccr_ref[...], NN)
                      - _dot(si_ref[...].astype(BF16), cci_ref[...], NN)
                      + d_ref[...] * u_ref[...])

    chunk3 = lambda r, c: pl.BlockSpec((None, r, c), lambda j: (j, 0, 0))
    return pl.pallas_call(
        body, name="s5_fwd", grid=(nch,),
        in_specs=[pl.BlockSpec((t, SSM_CHUNK), lambda j: (0, j + u_blk)),
                  chunk3(SSM_CHUNK, cs), chunk3(SSM_CHUNK, cs), chunk3(1, cs), chunk3(1, cs),
                  chunk3(cs, SSM_CHUNK), chunk3(cs, SSM_CHUNK), chunk3(1, SSM_CHUNK)],
        out_specs=[pl.BlockSpec((t, cs), lambda j: (0, j)), pl.BlockSpec((t, cs), lambda j: (0, j)),
                   pl.BlockSpec((t, SSM_CHUNK), lambda j: (0, j))],
        out_shape=[jax.ShapeDtypeStruct((t, nch * cs), F32), jax.ShapeDtypeStruct((t, nch * cs), F32),
                   jax.ShapeDtypeStruct((t, ds), F32)],
        compiler_params=_params(),
    )(proj, bb_re, bb_im, a_re, a_im, cc_re, cc_im, d_skip)


def _s5_bwd(dy, proj, u_col, s_re, s_im, bb_re, bb_im, a_re, a_im, cc_re, cc_im, d_skip):
    t, ds = dy.shape
    nch, _, cs = bb_re.shape
    u_blk = u_col // SSM_CHUNK
    steps = t // N_SEG

    def body(dy_ref, u_ref, sr_ref, si_ref, bbr_ref, bbi_ref, ar_ref, ai_ref, ccr_ref, cci_ref, d_ref,
             du_ref, dbbr_ref, dbbi_ref, dar_ref, dai_ref, dccr_ref, dcci_ref, dd_ref, gr_ref, gi_ref):
        dyv = dy_ref[...]
        dyb = dyv.astype(BF16)
        gr_ref[...] = _dot(dyb, ccr_ref[...], NT)
        gi_ref[...] = -_dot(dyb, cci_ref[...], NT)
        ar = jnp.broadcast_to(ar_ref[...], (N_SEG, cs))
        ai = -jnp.broadcast_to(ai_ref[...], (N_SEG, cs))
        zero = jnp.zeros((N_SEG, cs), F32)

        def totals(k, carry):
            i = steps - 1 - k
            gr, gi, pr, pi = carry
            tr, ti = _cmul(ar, ai, gr, gi)
            qr, qi = _cmul(ar, ai, pr, pi)
            return tr + gr_ref[_rows(i), :], ti + gi_ref[_rows(i), :], qr, qi

        fr, fi, pr, pi = lax.fori_loop(0, steps, totals, (zero, zero, zero + 1.0, zero))
        e0r, e0i = _segment_carries(pr, pi, fr, fi, False)

        def step(i, gr, gi, pr, pi, acc_r, acc_i):
            tr, ti = _cmul(ar, ai, gr, gi)
            nr, ni = tr + gr_ref[_rows(i), :], ti + gi_ref[_rows(i), :]
            gr_ref[_rows(i), :] = nr
            gi_ref[_rows(i), :] = ni
            return nr, ni, acc_r + nr * pr + ni * pi, acc_i + ni * pr - nr * pi

        def scan(k, carry):
            i = steps - 1 - k
            gr, gi, acc_r, acc_i = carry
            return step(i, gr, gi, sr_ref[_rows(i - 1), :], si_ref[_rows(i - 1), :], acc_r, acc_i)

        gr, gi, acc_r, acc_i = lax.fori_loop(0, steps - 1, scan, (e0r, e0i, zero, zero))
        row = lax.broadcasted_iota(jnp.int32, (N_SEG, cs), 0)
        last = _rows(steps - 1)
        pr = jnp.where(row == 0, 0.0, pltpu.roll(sr_ref[last, :], 1, 0))
        pi = jnp.where(row == 0, 0.0, pltpu.roll(si_ref[last, :], 1, 0))
        _, _, acc_r, acc_i = step(0, gr, gi, pr, pi, acc_r, acc_i)
        dar_ref[...] = jnp.sum(acc_r, axis=0, keepdims=True)
        dai_ref[...] = jnp.sum(acc_i, axis=0, keepdims=True)

        uv = u_ref[...]
        ub = uv.astype(BF16)
        grb = gr_ref[...].astype(BF16)
        gib = gi_ref[...].astype(BF16)
        du = d_ref[...] * dyv + _dot(grb, bbr_ref[...], NT) + _dot(gib, bbi_ref[...], NT)
        du_ref[...] = du.astype(BF16)
        dbbr_ref[...] = _dot(ub, grb, TN)
        dbbi_ref[...] = _dot(ub, gib, TN)
        dccr_ref[...] = _dot(sr_ref[...].astype(BF16), dyb, TN)
        dcci_ref[...] = -_dot(si_ref[...].astype(BF16), dyb, TN)
        dd_ref[...] = jnp.sum(dyv * uv, axis=0, keepdims=True)

    chunk3 = lambda r, c: pl.BlockSpec((None, r, c), lambda j: (j, 0, 0))
    cols = lambda w: pl.BlockSpec((t, w), lambda j: (0, j))
    return pl.pallas_call(
        body, name="s5_bwd", grid=(nch,),
        in_specs=[cols(SSM_CHUNK), pl.BlockSpec((t, SSM_CHUNK), lambda j: (0, j + u_blk)), cols(cs), cols(cs),
                  chunk3(SSM_CHUNK, cs), chunk3(SSM_CHUNK, cs), chunk3(1, cs), chunk3(1, cs),
                  chunk3(cs, SSM_CHUNK), chunk3(cs, SSM_CHUNK), chunk3(1, SSM_CHUNK)],
        out_specs=[cols(SSM_CHUNK), chunk3(SSM_CHUNK, cs), chunk3(SSM_CHUNK, cs), chunk3(1, cs),
                   chunk3(1, cs), chunk3(cs, SSM_CHUNK), chunk3(cs, SSM_CHUNK), chunk3(1, SSM_CHUNK)],
        out_shape=[jax.ShapeDtypeStruct((t, ds), BF16),
                   jax.ShapeDtypeStruct((nch, SSM_CHUNK, cs), F32), jax.ShapeDtypeStruct((nch, SSM_CHUNK, cs), F32),
                   jax.ShapeDtypeStruct((nch, 1, cs), F32), jax.ShapeDtypeStruct((nch, 1, cs), F32),
                   jax.ShapeDtypeStruct((nch, cs, SSM_CHUNK), F32), jax.ShapeDtypeStruct((nch, cs, SSM_CHUNK), F32),
                   jax.ShapeDtypeStruct((nch, 1, SSM_CHUNK), F32)],
        scratch_shapes=[pltpu.VMEM((t, cs), F32), pltpu.VMEM((t, cs), F32)],
        compiler_params=_params(),
    )(dy, proj, s_re, s_im, bb_re, bb_im, a_re, a_im, cc_re, cc_im, d_skip)


def _discretize(lr, li, log_dt, br, bi):
    dt = jnp.exp(log_dt)
    mag = jnp.exp(lr * dt)
    ang = li * dt
    a_re = mag * jnp.cos(ang)
    a_im = mag * jnp.sin(ang)
    den = lr * lr + li * li
    nr = a_re - 1.0
    f_re = (nr * lr + a_im * li) / den
    f_im = (a_im * lr - nr * li) / den
    return a_re, a_im, f_re * br - f_im * bi, f_re * bi + f_im * br


def _whole(shape):
    return pl.BlockSpec(shape, lambda: (0,) * len(shape))


def _disc_fwd(lr, li, log_dt, br, bi, deps=()):
    def body(lr_ref, li_ref, dt_ref, br_ref, bi_ref, ar_ref, ai_ref, bbr_ref, bbi_ref):
        outs = _discretize(lr_ref[...], li_ref[...], dt_ref[...], br_ref[...], bi_ref[...])
        for ref, val in zip((ar_ref, ai_ref, bbr_ref, bbi_ref), outs):
            ref[...] = val

    args = (lr, li, log_dt, br, bi)
    outs = (lr, lr, br, br)
    return pl.pallas_call(
        _ignoring_deps(body, 5, deps), name="disc_fwd",
        in_specs=[_whole(a.shape) for a in args] + [ANY_SPEC] * len(deps),
        out_specs=[_whole(a.shape) for a in outs],
        out_shape=[jax.ShapeDtypeStruct(a.shape, F32) for a in outs],
    )(*args, *deps)


def _disc_bwd(lr, li, log_dt, br, bi, dar, dai, dbbr, dbbi):
    def body(lr_ref, li_ref, dt_ref, br_ref, bi_ref, dar_ref, dai_ref, dbbr_ref, dbbi_ref,
             dlr_ref, dli_ref, ddt_ref, dbr_ref, dbi_ref):
        _, vjp = jax.vjp(_discretize, lr_ref[...], li_ref[...], dt_ref[...], br_ref[...], bi_ref[...])
        grads = vjp((dar_ref[...], dai_ref[...], dbbr_ref[...], dbbi_ref[...]))
        for ref, val in zip((dlr_ref, dli_ref, ddt_ref, dbr_ref, dbi_ref), grads):
            ref[...] = val

    args = (lr, li, log_dt, br, bi, dar, dai, dbbr, dbbi)
    outs = (lr, li, log_dt, br, bi)
    return pl.pallas_call(
        body, name="disc_bwd", in_specs=[_whole(a.shape) for a in args],
        out_specs=[_whole(a.shape) for a in outs],
        out_shape=[jax.ShapeDtypeStruct(a.shape, F32) for a in outs],
    )(*args)


def _adamw(w, g, m, v):
    m = ADAM_B1 * m + (1.0 - ADAM_B1) * g
    v = ADAM_B2 * v + (1.0 - ADAM_B2) * (g * g)
    m_hat = m / ADAM_BC1
    v_hat = v / ADAM_BC2
    delta = -ADAM_LR * (m_hat / (jnp.sqrt(v_hat) + ADAM_EPS) + ADAM_WD * w)
    return delta, m, v


def _adamw_reduce(name, parts, w, m, v):
    _, r, c = parts.shape
    tr = r
    for cand in (256, 176, 128):
        if r % cand == 0:
            tr = cand
            break

    def body(p_ref, w_ref, m_ref, v_ref, g_ref, d_ref, nm_ref, nv_ref):
        g = p_ref[0].astype(F32)
        for k in range(1, N_DEV):
            g = g + p_ref[k].astype(F32)
        delta, nm, nv = _adamw(w_ref[...], g, m_ref[...], v_ref[...])
        g_ref[...] = g
        d_ref[...] = delta
        nm_ref[...] = nm
        nv_ref[...] = nv

    blk = pl.BlockSpec((tr, c), lambda i: (i, 0))
    return pl.pallas_call(
        body, name=name, grid=(r // tr,),
        in_specs=[pl.BlockSpec((N_DEV, tr, c), lambda i: (0, i, 0)), blk, blk, blk],
        out_specs=[blk] * 4, out_shape=[jax.ShapeDtypeStruct((r, c), F32)] * 4,
        compiler_params=_params(),
    )(parts, w, m, v)


def _sum_parts(name, parts):
    _, r, c = parts.shape

    def body(p_ref, o_ref):
        g = p_ref[0]
        for k in range(1, N_DEV):
            g = g + p_ref[k]
        o_ref[...] = g

    return pl.pallas_call(
        body, name=name, in_specs=[_whole(parts.shape)], out_specs=_whole((r, c)),
        out_shape=jax.ShapeDtypeStruct((r, c), F32), compiler_params=_params(),
    )(parts)


def _adamw_many(name, grads, ws, ms, vs):
    n = len(grads)

    def body(*refs):
        ins, outs = refs[:4 * n], refs[4 * n:]
        for i in range(n):
            g, w, m, v = (ins[j * n + i][...] for j in range(4))
            for ref, val in zip((outs[i], outs[n + i], outs[2 * n + i]), _adamw(w, g, m, v)):
                ref[...] = val

    args = (*grads, *ws, *ms, *vs)
    outs = pl.pallas_call(
        body, name=name, in_specs=[_whole(a.shape) for a in args],
        out_specs=[_whole(a.shape) for a in ws] * 3,
        out_shape=[jax.ShapeDtypeStruct(a.shape, F32) for a in ws] * 3, compiler_params=_params(),
    )(*args)
    return outs[:n], outs[n:2 * n], outs[2 * n:]


def _pack(arrays, rows):
    flat = jnp.concatenate([a.reshape(-1) for a in arrays])
    return jnp.pad(flat, (0, rows * LANES - flat.shape[0])).reshape(rows, LANES)


def _unpack(packed, shapes):
    flat = packed.reshape(-1)
    out, off = [], 0
    for s in shapes:
        n = math.prod(s)
        out.append(flat[off:off + n].reshape(s))
        off += n
    return out


def _packed_rows(shapes):
    n = sum(math.prod(s) for s in shapes)
    return -(-n // (SUBLANES * LANES)) * SUBLANES


def _block_diag(x):
    j, g, r, c = x.shape
    eye = jnp.eye(g, dtype=x.dtype)
    return (x[:, :, :, None, :] * eye[None, :, None, :, None]).reshape(j, g * r, g * c)


def _diag_blocks(x, g):
    j, gr, gc = x.shape
    r, c = gr // g, gc // g
    eye = jnp.eye(g, dtype=x.dtype)
    return (x.reshape(j, g, r, g, c) * eye[None, :, None, :, None]).sum(axis=3)


def kernel(x, meta_tokens, norm_mix_g, w_in, conv_w, ssm_lam_re, ssm_lam_im, ssm_log_dt, ssm_b_re, ssm_b_im, ssm_c_re, ssm_c_im, ssm_d, ssm_w_glu, gain_conv_out, gain_ssm_out, w_out, norm_ffn_g, w_up, ffn_conv_w, ffn_conv_b, w_down, norm_final_g, loss_target, m_meta_tokens, m_norm_mix_g, m_w_in, m_conv_w, m_ssm_lam_re, m_ssm_lam_im, m_ssm_log_dt, m_ssm_b_re, m_ssm_b_im, m_ssm_c_re, m_ssm_c_im, m_ssm_d, m_ssm_w_glu, m_gain_conv_out, m_gain_ssm_out, m_w_out, m_norm_ffn_g, m_w_up, m_ffn_conv_w, m_ffn_conv_b, m_w_down, m_norm_final_g, v_meta_tokens, v_norm_mix_g, v_w_in, v_conv_w, v_ssm_lam_re, v_ssm_lam_im, v_ssm_log_dt, v_ssm_b_re, v_ssm_b_im, v_ssm_c_re, v_ssm_c_im, v_ssm_d, v_ssm_w_glu, v_gain_conv_out, v_gain_ssm_out, v_w_out, v_norm_ffn_g, v_w_up, v_ffn_conv_w, v_ffn_conv_b, v_w_down, v_norm_final_g):
    weights = dict(meta_tokens=meta_tokens, norm_mix_g=norm_mix_g, w_in=w_in, conv_w=conv_w, ssm_lam_re=ssm_lam_re, ssm_lam_im=ssm_lam_im, ssm_log_dt=ssm_log_dt, ssm_b_re=ssm_b_re, ssm_b_im=ssm_b_im, ssm_c_re=ssm_c_re, ssm_c_im=ssm_c_im, ssm_d=ssm_d, ssm_w_glu=ssm_w_glu, gain_conv_out=gain_conv_out, gain_ssm_out=gain_ssm_out, w_out=w_out, norm_ffn_g=norm_ffn_g, w_up=w_up, ffn_conv_w=ffn_conv_w, ffn_conv_b=ffn_conv_b, w_down=w_down, norm_final_g=norm_final_g)
    mom_m = dict(meta_tokens=m_meta_tokens, norm_mix_g=m_norm_mix_g, w_in=m_w_in, conv_w=m_conv_w, ssm_lam_re=m_ssm_lam_re, ssm_lam_im=m_ssm_lam_im, ssm_log_dt=m_ssm_log_dt, ssm_b_re=m_ssm_b_re, ssm_b_im=m_ssm_b_im, ssm_c_re=m_ssm_c_re, ssm_c_im=m_ssm_c_im, ssm_d=m_ssm_d, ssm_w_glu=m_ssm_w_glu, gain_conv_out=m_gain_conv_out, gain_ssm_out=m_gain_ssm_out, w_out=m_w_out, norm_ffn_g=m_norm_ffn_g, w_up=m_w_up, ffn_conv_w=m_ffn_conv_w, ffn_conv_b=m_ffn_conv_b, w_down=m_w_down, norm_final_g=m_norm_final_g)
    mom_v = dict(meta_tokens=v_meta_tokens, norm_mix_g=v_norm_mix_g, w_in=v_w_in, conv_w=v_conv_w, ssm_lam_re=v_ssm_lam_re, ssm_lam_im=v_ssm_lam_im, ssm_log_dt=v_ssm_log_dt, ssm_b_re=v_ssm_b_re, ssm_b_im=v_ssm_b_im, ssm_c_re=v_ssm_c_re, ssm_c_im=v_ssm_c_im, ssm_d=v_ssm_d, ssm_w_glu=v_ssm_w_glu, gain_conv_out=v_gain_conv_out, gain_ssm_out=v_gain_ssm_out, w_out=v_w_out, norm_ffn_g=v_norm_ffn_g, w_up=v_w_up, ffn_conv_w=v_ffn_conv_w, ffn_conv_b=v_ffn_conv_b, w_down=v_w_down, norm_final_g=v_norm_final_g)
    names = list(weights)

    n_meta, d_meta = meta_tokens.shape
    seq, d = x.shape[1], x.shape[2]
    rows_used = n_meta + seq
    t = -(-rows_used // ROW_TILE) * ROW_TILE
    d_in_s = w_in.shape[2]
    dc_s = conv_w.shape[2]
    dc = dc_s * N_DEV
    ds = ssm_w_glu.shape[2]
    n_groups, n_state, grp = ssm_b_re.shape[1:]
    ns = n_groups * n_state
    nch = ds // SSM_CHUNK
    gpc = n_groups // nch
    ff_s = w_up.shape[2]
    dn_s = w_down.shape[1]
    assert 3 * dc + ds == d_in_s * N_DEV and 2 * dn_s == ff_s and t % (N_SEG * SUBLANES) == 0

    small_shard = jnp.concatenate([meta_tokens.reshape(-1), conv_w.reshape(-1), ffn_conv_w.reshape(-1)])
    n_small = small_shard.shape[0]
    small_rows = -(-n_small // LANES)
    small_shard = jnp.pad(small_shard, (0, small_rows * LANES - n_small)).reshape(small_rows, LANES)
    ag, ag_token = _send_start("gather_weights_start", [
        small_shard, w_in[0].astype(BF16), ssm_w_glu[0].astype(BF16), w_out[0].astype(BF16),
        w_up[0].astype(BF16), w_down[0].astype(BF16)], gather=True)
    fb = ffn_conv_b.reshape(N_DEV, 1, ff_s)

    col = lambda a: a.reshape(ns, 1)
    lr, li = col(ssm_lam_re), col(ssm_lam_im)
    log_dt_e = jnp.broadcast_to(ssm_log_dt.reshape(n_groups, 1), (n_groups, n_state)).reshape(ns, 1)
    br, bi = ssm_b_re.reshape(ns, grp), ssm_b_im.reshape(ns, grp)
    a_re, a_im, bb_re, bb_im = _disc_fwd(lr, li, log_dt_e, br, bi, deps=(ag_token,))
    cs = gpc * n_state
    chunk_row = lambda a: a.reshape(nch, 1, cs)
    to_bb = lambda a: _block_diag(a.reshape(nch, gpc, n_state, grp).transpose(0, 1, 3, 2)).astype(BF16)
    to_cc = lambda a: _block_diag(a.reshape(nch, gpc, grp, n_state).transpose(0, 1, 3, 2)).astype(BF16)
    bbm_re, bbm_im = to_bb(bb_re), to_bb(bb_im)
    ccm_re, ccm_im = to_cc(ssm_c_re), to_cc(ssm_c_im)
    a_re_c, a_im_c = chunk_row(a_re), chunk_row(a_im)
    d_skip = ssm_d.reshape(nch, 1, SSM_CHUNK)

    (g_small,) = _send_wait("gather_weights_wait_small", ag, (0,), bbm_im)
    g_small = g_small.reshape(N_DEV, -1)
    o1 = n_meta * d_meta
    o2 = o1 + 3 * dc_s
    meta_full = g_small[:, :o1].reshape(N_DEV, n_meta, d_meta).transpose(1, 0, 2).reshape(n_meta, d)
    conv_w_f = g_small[:, o1:o2].reshape(N_DEV, 3, dc_s).transpose(1, 0, 2).reshape(3, dc)
    fw = g_small[:, o2:o2 + 3 * ff_s].reshape(N_DEV, 3, ff_s)
    h0 = _to_segments(jnp.concatenate([meta_full, x[0], jnp.zeros((t - rows_used, d), F32)], axis=0))
    target = _to_segments(jnp.pad(loss_target[0], ((n_meta, t - rows_used), (0, 0))))
    full_t = lambda w: pl.BlockSpec((t, w), lambda *_: (0, 0))

    hn1 = _rms_fwd("norm_mix", h0, norm_mix_g)
    (g_in,) = _send_wait("gather_weights_wait_in", ag, (1,), hn1)
    proj = _mm("proj", hn1, g_in, dims=NN, grid=(N_DEV,), a_spec=full_t(d),
               b_spec=pl.BlockSpec((None, d, d_in_s), lambda j: (j, 0, 0)),
               o_spec=pl.BlockSpec((t, d_in_s), lambda j: (0, j)),
               out_shape=jax.ShapeDtypeStruct((t, N_DEV * d_in_s), F32))
    s_re, s_im, y_ssm = _s5_fwd(proj, 3 * dc, bbm_re, bbm_im, a_re_c, a_im_c, ccm_re, ccm_im, d_skip)
    g_glu, g_out = _send_wait("gather_weights_wait_mix", ag, (2, 3), y_ssm)
    w_out_f = g_out.reshape(-1, d)
    w_glu_f = g_glu.reshape(ds, ds)
    mixed, z_glu = _mix_fwd(proj, y_ssm, w_glu_f, conv_w_f, gain_conv_out, gain_ssm_out)
    tn_out = 256
    h1 = _mm("out_proj", mixed, w_out_f, dims=NN, grid=(d // tn_out,), a_spec=full_t(dc + ds),
             b_spec=pl.BlockSpec((dc + ds, tn_out), lambda j: (0, j)),
             o_spec=pl.BlockSpec((t, tn_out), lambda j: (0, j)),
             out_shape=jax.ShapeDtypeStruct((t, d), F32),
             res=h0, res_spec=pl.BlockSpec((t, tn_out), lambda j: (0, j)))
    hn2 = _rms_fwd("norm_ffn", h1, norm_ffn_g)
    (g_up,) = _send_wait("gather_weights_wait_up", ag, (4,), hn2)
    up_pre = _mm("up_proj", hn2, g_up, dims=NN, grid=(N_DEV,), a_spec=full_t(d),
                 b_spec=pl.BlockSpec((None, d, ff_s), lambda j: (j, 0, 0)),
                 o_spec=pl.BlockSpec((None, t, ff_s), lambda j: (j, 0, 0)),
                 out_shape=jax.ShapeDtypeStruct((N_DEV, t, ff_s), F32))
    act = _ffn_fwd(up_pre, fw, fb)
    nhalf = N_DEV // 2
    (g_down,) = _send_wait("gather_weights_wait_down", ag, (5,), act)
    w_down_f = g_down.reshape(nhalf, 2 * dn_s, d)
    h2 = _mm("down_proj", act, w_down_f, dims=NN, grid=(d // tn_out, nhalf),
             a_spec=pl.BlockSpec((None, t, ff_s), lambda j, k: (k, 0, 0)),
             b_spec=pl.BlockSpec((None, ff_s, tn_out), lambda j, k: (k, 0, j)),
             o_spec=pl.BlockSpec((t, tn_out), lambda j, k: (0, j)),
             out_shape=jax.ShapeDtypeStruct((t, d), F32), acc_shape=(t, tn_out),
             res=h1, res_spec=pl.BlockSpec((t, tn_out), lambda j, k: (0, j)))

    dh2, dh2_b, loss_part, d_norm_final = _loss_bwd(h2, target, norm_final_g.reshape(1, d), n_meta, rows_used)
    dact = _mm("down_dgrad", dh2_b, w_down_f, dims=NT, grid=(nhalf,), a_spec=full_t(d),
               b_spec=pl.BlockSpec((None, ff_s, d), lambda j: (j, 0, 0)),
               o_spec=pl.BlockSpec((None, t, ff_s), lambda j: (j, 0, 0)),
               out_shape=jax.ShapeDtypeStruct((nhalf, t, ff_s), F32))
    tn_w = 512
    dw_down = _mm("down_wgrad", act, dh2_b, dims=TN, grid=(nhalf, d // tn_w),
                  a_spec=pl.BlockSpec((None, t, ff_s), lambda j, n: (j, 0, 0)),
                  b_spec=pl.BlockSpec((t, tn_w), lambda j, n: (0, n)),
                  o_spec=pl.BlockSpec((None, ff_s, tn_w), lambda j, n: (j, 0, n)),
                  out_shape=jax.ShapeDtypeStruct((nhalf, ff_s, d), BF16))
    ex_down, token = _send_start("exchange_down_start", [dw_down.reshape(N_DEV, dn_s, d)], gather=False)
    dup, d_fw, d_fb = _ffn_bwd1(up_pre, dact, fw, fb, deps=(token,))
    dup_pre = _ffn_bwd2(dup, fw)
    dhn2 = _mm("up_dgrad", dup_pre, g_up, dims=NT, grid=(d // tn_w, N_DEV),
               a_spec=pl.BlockSpec((None, t, ff_s), lambda i, k: (k, 0, 0)),
               b_spec=pl.BlockSpec((None, tn_w, ff_s), lambda i, k: (k, i, 0)),
               o_spec=pl.BlockSpec((t, tn_w), lambda i, k: (0, i)),
               out_shape=jax.ShapeDtypeStruct((t, d), F32), acc_shape=(t, tn_w))
    dw_up = _mm("up_wgrad", hn2, dup_pre, dims=TN, grid=(N_DEV, d // tn_w),
                a_spec=pl.BlockSpec((t, tn_w), lambda j, i: (0, i)),
                b_spec=pl.BlockSpec((None, t, ff_s), lambda j, i: (j, 0, 0)),
                o_spec=pl.BlockSpec((None, tn_w, ff_s), lambda j, i: (j, i, 0)),
                out_shape=jax.ShapeDtypeStruct((N_DEV, d, ff_s), BF16))
    ex_up, token = _send_start("exchange_up_start", [dw_up], gather=False)
    dh1, dh1_b, d_norm_ffn = _rms_bwd_res("norm_ffn_bwd", dh2, dhn2, h1, norm_ffn_g, deps=(token,))
    dmixed = _mm("out_dgrad", dh1_b, w_out_f, dims=NT, grid=((dc + ds) // tn_out,), a_spec=full_t(d),
                 b_spec=pl.BlockSpec((tn_out, d), lambda i: (i, 0)),
                 o_spec=pl.BlockSpec((t, tn_out), lambda i: (0, i)),
                 out_shape=jax.ShapeDtypeStruct((t, dc + ds), F32))
    dw_out = _mm("out_wgrad", mixed, dh1_b, dims=TN, grid=((dc + ds) // tn_out,),
                 a_spec=pl.BlockSpec((t, tn_out), lambda i: (0, i)), b_spec=full_t(d),
                 o_spec=pl.BlockSpec((tn_out, d), lambda i: (i, 0)),
                 out_shape=jax.ShapeDtypeStruct((dc + ds, d), BF16))
    db_gate, dconv, dy_ssm, d_wglu, d_conv_w, d_gain_c, d_gain_s = _mix_bwd1(
        proj, y_ssm, z_glu, dmixed, w_glu_f, conv_w_f, gain_conv_out, gain_ssm_out)
    ex_mix, token = _send_start("exchange_mix_start", [
        dw_out.reshape(N_DEV, -1, d), d_wglu.astype(BF16).reshape(N_DEV, -1, ds),
        d_conv_w.reshape(3, N_DEV, dc_s).transpose(1, 0, 2), d_fw], gather=False)
    dcdv = _mix_bwd2(proj, dconv, conv_w_f, deps=(token,))
    (du, d_bbm_re, d_bbm_im, d_a_re, d_a_im, d_ccm_re, d_ccm_im, d_dskip) = _s5_bwd(
        dy_ssm, proj, 3 * dc, s_re, s_im, bbm_re, bbm_im, a_re_c, a_im_c, ccm_re, ccm_im, d_skip)
    dproj = jnp.concatenate([db_gate, dcdv, du], axis=1)
    dhn1 = _mm("proj_dgrad", dproj, g_in, dims=NT, grid=(d // tn_w, N_DEV),
               a_spec=pl.BlockSpec((t, d_in_s), lambda i, k: (0, k)),
               b_spec=pl.BlockSpec((None, tn_w, d_in_s), lambda i, k: (k, i, 0)),
               o_spec=pl.BlockSpec((t, tn_w), lambda i, k: (0, i)),
               out_shape=jax.ShapeDtypeStruct((t, d), F32), acc_shape=(t, tn_w))

    from_bb = lambda a: _diag_blocks(a, gpc).transpose(0, 1, 3, 2).reshape(ns, grp)
    from_cc = lambda a: _diag_blocks(a, gpc).transpose(0, 1, 3, 2).reshape(n_groups * grp, n_state)
    d_lr, d_li, d_dt_e, d_br, d_bi = _disc_bwd(
        lr, li, log_dt_e, br, bi, d_a_re.reshape(ns, 1), d_a_im.reshape(ns, 1),
        from_bb(d_bbm_re), from_bb(d_bbm_im))

    rep2d = dict(
        ssm_lam_re=(n_groups, n_state), ssm_lam_im=(n_groups, n_state), ssm_log_dt=(1, n_groups),
        ssm_b_re=(ns, grp), ssm_b_im=(ns, grp), ssm_c_re=(n_groups * grp, n_state),
        ssm_c_im=(n_groups * grp, n_state), ssm_d=(n_groups, grp), gain_conv_out=(1, dc),
        gain_ssm_out=(1, ds), norm_ffn_g=(1, d), ffn_conv_b=(1, N_DEV * ff_s), norm_final_g=(1, d))
    rep_names = list(rep2d)
    rep_grads = dict(
        ssm_lam_re=d_lr, ssm_lam_im=d_li, ssm_log_dt=d_dt_e.reshape(n_groups, n_state).sum(axis=1),
        ssm_b_re=d_br, ssm_b_im=d_bi, ssm_c_re=from_cc(d_ccm_re), ssm_c_im=from_cc(d_ccm_im),
        ssm_d=d_dskip, gain_conv_out=d_gain_c, gain_ssm_out=d_gain_s, norm_ffn_g=d_norm_ffn,
        ffn_conv_b=d_fb, norm_final_g=d_norm_final)
    rep_shapes = [rep2d[n] for n in rep_names] + [(1, 1)]
    rep_rows = _packed_rows(rep_shapes)
    rep_pack = _pack([rep_grads[n] for n in rep_names] + [loss_part], rep_rows)
    ex_rep, rep_token = _send_start("gather_small_grads_start", [rep_pack], gather=True)

    dh0, _, d_norm_mix = _rms_bwd_res("norm_mix_bwd", dh1, dhn1, h0, norm_mix_g, deps=(rep_token,))
    dw_in = _mm("proj_wgrad", hn1, dproj, dims=TN, grid=(N_DEV, d // tn_w),
                a_spec=pl.BlockSpec((t, tn_w), lambda j, i: (0, i)),
                b_spec=pl.BlockSpec((t, d_in_s), lambda j, i: (0, j)),
                o_spec=pl.BlockSpec((None, tn_w, d_in_s), lambda j, i: (j, i, 0)),
                out_shape=jax.ShapeDtypeStruct((N_DEV, d, d_in_s), BF16))
    dh0 = _from_segments(dh0)
    grad_x = dh0[n_meta:rows_used][None]
    d_meta_b = dh0[:n_meta].reshape(n_meta, N_DEV, d_meta).transpose(1, 0, 2)
    ex_in, ex_in_token = _send_start("exchange_in_start", [dw_in, d_meta_b, d_norm_mix],
                                     gather=[False, False, True])

    shard_out = {}
    (rep_parts,) = _send_wait("gather_small_grads_wait", ex_rep, (0,), ex_in_token)
    rep_sum = _sum_parts("sum_small_grads", rep_parts)
    *rep_g, loss = _unpack(rep_sum, rep_shapes)
    loss = loss.reshape(())
    as2d = lambda tree: [tree[n].reshape(rep2d[n]) for n in rep_names]
    rep_res = _adamw_many("adamw_replicated", rep_g, as2d(weights), as2d(mom_m), as2d(mom_v))
    for i, n in enumerate(rep_names):
        shard_out[n] = [r.reshape(weights[n].shape) for r in (rep_g[i], *(res[i] for res in rep_res))]

    def update(n, parts):
        sh = weights[n].shape
        two_d = lambda a: a.reshape(parts.shape[1:])
        res = _adamw_reduce("adamw_" + n, parts, two_d(weights[n]), two_d(mom_m[n]), two_d(mom_v[n]))
        shard_out[n] = [r.reshape(sh) for r in res]
        return res[0]

    (p_down,) = _send_wait("exchange_down_wait", ex_down, (0,), rep_sum)
    done = update("w_down", p_down)
    (p_up,) = _send_wait("exchange_up_wait", ex_up, (0,), done)
    done = update("w_up", p_up)
    p_out, p_glu, p_cw, p_fw = _send_wait("exchange_mix_wait", ex_mix, (0, 1, 2, 3), done)
    update("w_out", p_out)
    update("ssm_w_glu", p_glu)
    update("conv_w", p_cw)
    done = update("ffn_conv_w", p_fw)
    p_in, p_meta, p_nm = _send_wait("exchange_in_wait", ex_in, (0, 1, 2), done)
    update("w_in", p_in)
    update("meta_tokens", p_meta)
    update("norm_mix_g", p_nm)

    grads = [shard_out[n][0] for n in names]
    deltas = [shard_out[n][1] for n in names]
    new_m = [shard_out[n][2] for n in names]
    new_v = [shard_out[n][3] for n in names]
    return (loss, grad_x, *grads, *deltas, *new_m, *new_v)
```

```python
import functools
import math

import jax
import jax.numpy as jnp
from jax import lax
from jax.experimental import pallas as pl
from jax.experimental.pallas import tpu as pltpu

F32 = jnp.float32
BF16 = jnp.bfloat16
MESH = pl.DeviceIdType.MESH

N_DEV = 8
RMS_EPS = 1e-6
ADAM_LR = 0.001
ADAM_B1 = 0.9
ADAM_B2 = 0.999
ADAM_EPS = 1e-08
ADAM_WD = 0.01
ADAM_STEP = 10
ADAM_BC1 = 1.0 - ADAM_B1 ** ADAM_STEP
ADAM_BC2 = 1.0 - ADAM_B2 ** ADAM_STEP

SUBLANES = 8
LANES = 128
ROW_TILE = 128
ROW_CHUNK = 32
N_SEG = 8
HALO_ROWS = 16
SSM_CHUNK = 128
VMEM_LIMIT = 48 * 1024 * 1024

NN = ((1,), (0,))
NT = ((1,), (1,))
TN = ((0,), (0,))


def _params(**kw):
    return pltpu.CompilerParams(vmem_limit_bytes=VMEM_LIMIT, **kw)


def _dot(a, b, dims):
    return lax.dot_general(a, b, (dims, ((), ())), preferred_element_type=F32)


def _mean_sq_rsqrt(x):
    return lax.rsqrt(jnp.mean(x * x, axis=-1, keepdims=True) + RMS_EPS)


def _rms_bwd(x, r, g, dy):
    xhat = x * r
    dxh = dy * g
    dx = r * (dxh - xhat * jnp.mean(dxh * xhat, axis=-1, keepdims=True))
    return dx, dy * xhat


def _gelu(y):
    c = math.sqrt(2.0 / math.pi)
    t = jnp.tanh(c * (y + 0.044715 * y * y * y))
    return 0.5 * y * (1.0 + t), t


def _gelu_grad(y, t):
    c = math.sqrt(2.0 / math.pi)
    return 0.5 * (1.0 + t) + 0.5 * y * (1.0 - t * t) * c * (1.0 + 3.0 * 0.044715 * y * y)


def _wrap_prev_halo(halo, first):
    seg = lax.broadcasted_iota(jnp.int32, halo.shape, 0) % N_SEG
    wrapped = jnp.where(seg == 0, 0.0, pltpu.roll(halo, 1, 0))
    return jnp.where(first, wrapped, halo)


def _wrap_next_halo(halo, last):
    seg = lax.broadcasted_iota(jnp.int32, halo.shape, 0) % N_SEG
    wrapped = jnp.where(seg == N_SEG - 1, 0.0, pltpu.roll(halo, halo.shape[0] - 1, 0))
    return jnp.where(last, wrapped, halo)


def _rows_before(get, halo, r, rows, back):
    lo = r - back
    if lo >= 0:
        return get(lo, rows)
    return jnp.concatenate([halo[HALO_ROWS + lo:], get(0, rows + lo)], axis=0)


def _rows_after(get, halo, r, rows, ahead, tile_rows):
    over = r + ahead + rows - tile_rows
    if over <= 0:
        return get(r + ahead, rows)
    return jnp.concatenate([get(r + ahead, rows - over), halo[:over]], axis=0)


def _dev_index(p):
    return 4 * p[0] + 2 * p[1] + p[2]


def _allgather(name, shards, deps=()):
    n = len(shards)

    def body(*refs):
        ins, outs = refs[:n], refs[n:2 * n]
        send_sems, recv_sems, local_sems = refs[2 * n:]
        x, y, c = lax.axis_index("x"), lax.axis_index("y"), lax.axis_index("c")
        me, sibling = (x, y, c), (x, y, 1 - c)
        chips = [(1 - x, y), (x, 1 - y), (1 - x, 1 - y)]

        def copy(a, k, block, to, src=None):
            dst = outs[a].at[_dev_index(block)]
            return pltpu.make_async_remote_copy(
                src_ref=dst if src is None else src, dst_ref=dst,
                send_sem=send_sems.at[a, k], recv_sem=recv_sems.at[a, k],
                device_id=to, device_id_type=MESH)

        mine = [pltpu.make_async_copy(ins[a], outs[a].at[_dev_index(me)], local_sems.at[a])
                for a in range(n)]
        for cp in mine:
            cp.start()
        first = []
        for a in range(n):
            first.append(copy(a, 0, me, sibling, src=ins[a]))
            for j, chip in enumerate(chips):
                first.append(copy(a, 1 + j, me, (*chip, c), src=ins[a]))
        for cp in first:
            cp.start()
        passed = []
        for j, chip in enumerate(chips):
            for a in range(n):
                copy(a, 1 + j, (*chip, c), me).wait_recv()
                fwd = copy(a, 4 + j, (*chip, c), sibling)
                fwd.start()
                passed.append(fwd)
        for a in range(n):
            copy(a, 0, sibling, me).wait_recv()
            for j, chip in enumerate(chips):
                copy(a, 4 + j, (*chip, 1 - c), me).wait_recv()
        for cp in first + passed:
            cp.wait_send()
        for cp in mine:
            cp.wait()

    any_spec = pl.BlockSpec(memory_space=pl.ANY)
    return pl.pallas_call(
        _ignoring_deps(body, n, deps), name=name,
        out_shape=[jax.ShapeDtypeStruct((N_DEV,) + s.shape, s.dtype) for s in shards],
        in_specs=[any_spec] * (n + len(deps)), out_specs=[any_spec] * n,
        scratch_shapes=[pltpu.SemaphoreType.DMA((n, 7)), pltpu.SemaphoreType.DMA((n, 7)),
                        pltpu.SemaphoreType.DMA((n,))],
    )(*shards, *deps)


HBM_SPEC = pl.BlockSpec(memory_space=pltpu.HBM)
SEM_SPEC = pl.BlockSpec(memory_space=pltpu.SEMAPHORE)
ANY_SPEC = pl.BlockSpec(memory_space=pl.ANY)
DATAFLOW = pltpu.SideEffectType.DATAFLOW_SIDE_EFFECTING


def _ignoring_deps(body, n_in, deps):
    n_dep = len(deps)

    def wrapped(*refs):
        return body(*refs[:n_in], *refs[n_in + n_dep:])

    return wrapped


def _my_position():
    x, y, c = lax.axis_index("x"), lax.axis_index("y"), lax.axis_index("c")
    return (x, y, c)


def _peer(me, k):
    return tuple((1 - v) if (k >> s) & 1 else v for v, s in zip(me, (2, 1, 0)))


def _split_copies(src_refs, land_refs, send_sems, recv_sems, gather):
    me = _my_position()
    copies = []
    for a, (src, land) in enumerate(zip(src_refs, land_refs)):
        for k in range(1, N_DEV):
            peer = _peer(me, k)
            copies.append(pltpu.make_async_remote_copy(
                src_ref=src if gather[a] else src.at[_dev_index(peer)], dst_ref=land.at[_dev_index(me)],
                send_sem=send_sems[a].at[k - 1], recv_sem=recv_sems[a].at[k - 1],
                device_id=peer, device_id_type=MESH))
    return copies


def _own_slot(block, like_shape):
    me = _dev_index(_my_position())
    return lax.dynamic_update_index_in_dim(lax.empty(like_shape, block.dtype), block, me, 0)


def _send_start(name, srcs, gather):
    n = len(srcs)
    me = _dev_index(_my_position())
    gather = [gather] * n if isinstance(gather, bool) else list(gather)
    lands = [_own_slot(s, (N_DEV,) + s.shape) if g else
             _own_slot(lax.dynamic_index_in_dim(s, me, 0, keepdims=False), s.shape)
             for s, g in zip(srcs, gather)]

    def body(*refs):
        src_refs, land_refs = refs[:n], refs[n:2 * n]
        send_sems, recv_sems = refs[2 * n:3 * n], refs[3 * n:4 * n]
        token = refs[-1]
        for cp in _split_copies(src_refs, land_refs, send_sems, recv_sems, gather):
            cp.start()
        token[...] = jnp.zeros_like(token)

    hbm = lambda a: pltpu.HBM(a.shape, a.dtype)
    sems = [pltpu.SemaphoreType.DMA((N_DEV - 1,))] * n
    outs = pl.pallas_call(
        body, name=name,
        out_shape=(*sems, *sems, *[hbm(s) for s in srcs], *[hbm(l) for l in lands],
                   jax.ShapeDtypeStruct((SUBLANES, LANES), F32)),
        in_specs=[HBM_SPEC] * (2 * n),
        out_specs=(*[SEM_SPEC] * (2 * n), *[HBM_SPEC] * (2 * n), pl.BlockSpec(memory_space=pltpu.VMEM)),
        input_output_aliases={i: 2 * n + i for i in range(2 * n)},
        compiler_params=pltpu.CompilerParams(has_side_effects=DATAFLOW),
    )(*[pltpu.with_memory_space_constraint(a, pltpu.HBM) for a in (*srcs, *lands)])
    state = dict(send=outs[:n], recv=outs[n:2 * n], srcs=outs[2 * n:3 * n], lands=outs[3 * n:4 * n],
                 gather=gather)
    return state, outs[-1]


def _send_wait(name, state, which, after):
    n = len(which)
    pick = lambda key: [state[key][i] for i in which]
    gather = pick("gather")

    def body(*refs):
        src_refs, land_refs = refs[:n], refs[n:2 * n]
        send_sems, recv_sems = refs[2 * n:3 * n], refs[3 * n:4 * n]
        for cp in _split_copies(src_refs, land_refs, send_sems, recv_sems, gather):
            cp.wait_send()
            cp.wait_recv()

    srcs, lands = pick("srcs"), pick("lands")
    hbm = lambda a: pltpu.HBM(a.shape, a.dtype)
    outs = pl.pallas_call(
        body, name=name,
        out_shape=(*[hbm(s) for s in srcs], *[hbm(l) for l in lands]),
        in_specs=[*[HBM_SPEC] * (2 * n), *[SEM_SPEC] * (2 * n), ANY_SPEC],
        out_specs=tuple([HBM_SPEC] * (2 * n)),
        input_output_aliases={i: i for i in range(2 * n)},
        compiler_params=pltpu.CompilerParams(has_side_effects=DATAFLOW),
    )(*srcs, *lands, *pick("send"), *pick("recv"), after)
    return outs[n:]


def _two_level_copies(src_refs, land_refs, sems1, sems2):
    x, y, c = _my_position()
    me, sibling = (x, y, c), (x, y, 1 - c)
    chips = [(1 - x, y), (x, 1 - y), (1 - x, 1 - y)]
    stage1, stage2 = [], []
    for a, land in enumerate(land_refs):
        def copy(src, block, to, sems, k):
            return pltpu.make_async_remote_copy(
                src_ref=src, dst_ref=land.at[_dev_index(block)], send_sem=sems[0][a].at[k],
                recv_sem=sems[1][a].at[k], device_id=to, device_id_type=MESH)
        src = src_refs[a] if src_refs is not None else land.at[_dev_index(me)]
        stage1.append([copy(src, me, sibling, sems1, 0)] +
                      [copy(src, me, (*chip, c), sems1, 1 + j) for j, chip in enumerate(chips)])
        if sems2 is not None:
            stage2.append([copy(land.at[_dev_index((*chip, c))], (*chip, c), sibling, sems2, j)
                           for j, chip in enumerate(chips)])
    return stage1, stage2


def _gather2_start(name, shards):
    n = len(shards)
    lands = [_own_slot(s, (N_DEV,) + s.shape) for s in shards]

    def body(*refs):
        src_refs, land_refs = refs[:n], refs[n:2 * n]
        sems1 = (refs[2 * n:3 * n], refs[3 * n:4 * n])
        stage1, _ = _two_level_copies(src_refs, land_refs, sems1, None)
        for copies in stage1:
            for cp in copies:
                cp.start()
        refs[-1][...] = jnp.zeros_like(refs[-1])

    hbm = lambda a: pltpu.HBM(a.shape, a.dtype)
    sems = [pltpu.SemaphoreType.DMA((4,))] * n
    outs = pl.pallas_call(
        body, name=name,
        out_shape=(*sems, *sems, *[hbm(s) for s in shards], *[hbm(l) for l in lands],
                   jax.ShapeDtypeStruct((SUBLANES, LANES), F32)),
        in_specs=[HBM_SPEC] * (2 * n),
        out_specs=(*[SEM_SPEC] * (2 * n), *[HBM_SPEC] * (2 * n), pl.BlockSpec(memory_space=pltpu.VMEM)),
        input_output_aliases={i: 2 * n + i for i in range(2 * n)},
        compiler_params=pltpu.CompilerParams(has_side_effects=DATAFLOW),
    )(*[pltpu.with_memory_space_constraint(a, pltpu.HBM) for a in (*shards, *lands)])
    state = dict(send1=list(outs[:n]), recv1=list(outs[n:2 * n]), srcs=list(outs[2 * n:3 * n]),
                 lands=list(outs[3 * n:4 * n]), send2={}, recv2={})
    return state, outs[-1]


def _gather2_forward(name, state, which, after):
    n = len(which)
    pick = lambda key: [state[key][i] for i in which]

    def body(*refs):
        land_refs, recv1 = refs[:n], refs[n:2 * n]
        outs = refs[2 * n + 1:]
        sems2 = (outs[:n], outs[n:2 * n])
        stage1, stage2 = _two_level_copies(None, land_refs, (recv1, recv1), sems2)
        for a in range(n):
            for j in range(3):
                stage1[a][1 + j].wait_recv()
                stage2[a][j].start()
        outs[-1][...] = jnp.zeros_like(outs[-1])

    lands = pick("lands")
    sems = [pltpu.SemaphoreType.DMA((3,))] * n
    outs = pl.pallas_call(
        body, name=name,
        out_shape=(*sems, *sems, *[pltpu.HBM(l.shape, l.dtype) for l in lands],
                   jax.ShapeDtypeStruct((SUBLANES, LANES), F32)),
        in_specs=[*[HBM_SPEC] * n, *[SEM_SPEC] * n, ANY_SPEC],
        out_specs=(*[SEM_SPEC] * (2 * n), *[HBM_SPEC] * n, pl.BlockSpec(memory_space=pltpu.VMEM)),
        input_output_aliases={i: 2 * n + i for i in range(n)},
        compiler_params=pltpu.CompilerParams(has_side_effects=DATAFLOW),
    )(*lands, *pick("recv1"), after)
    for idx, i in enumerate(which):
        state["send2"][i], state["recv2"][i] = outs[idx], outs[n + idx]
        state["lands"][i] = outs[2 * n + idx]
    return outs[-1]


def _gather2_wait(name, state, which, after):
    n = len(which)
    pick = lambda key: [state[key][i] for i in which]

    def body(*refs):
        src_refs, land_refs = refs[:n], refs[n:2 * n]
        sems1 = (refs[2 * n:3 * n], refs[3 * n:4 * n])
        sems2 = (refs[4 * n:5 * n], refs[5 * n:6 * n])
        stage1, stage2 = _two_level_copies(src_refs, land_refs, sems1, sems2)
        for a in range(n):
            for cp in stage1[a]:
                cp.wait_send()
            stage1[a][0].wait_recv()
            for cp in stage2[a]:
                cp.wait_send()
                cp.wait_recv()

    srcs, lands = pick("srcs"), pick("lands")
    hbm = lambda a: pltpu.HBM(a.shape, a.dtype)
    outs = pl.pallas_call(
        body, name=name,
        out_shape=(*[hbm(s) for s in srcs], *[hbm(l) for l in lands]),
        in_specs=[*[HBM_SPEC] * (2 * n), *[SEM_SPEC] * (4 * n), ANY_SPEC],
        out_specs=tuple([HBM_SPEC] * (2 * n)),
        input_output_aliases={i: i for i in range(2 * n)},
        compiler_params=pltpu.CompilerParams(has_side_effects=DATAFLOW),
    )(*srcs, *lands, *pick("send1"), *pick("recv1"), *pick("send2"), *pick("recv2"), after)
    return outs[n:]


def _mm(name, a, b, *, dims, grid, a_spec, b_spec, o_spec, out_shape, acc_shape=None,
        res=None, res_spec=None):
    n_red = grid[-1] if acc_shape is not None else 1
    red_axis = len(grid) - 1

    def body(*refs):
        a_ref, b_ref = refs[0], refs[1]
        r_ref = refs[2] if res is not None else None
        o_ref = refs[3] if res is not None else refs[2]
        part = _dot(a_ref[...], b_ref[...], dims)
        if acc_shape is None:
            if r_ref is not None:
                part = part + r_ref[...]
            o_ref[...] = part.astype(o_ref.dtype)
            return
        acc_ref = refs[-1]
        k = pl.program_id(red_axis)

        @pl.when(k == 0)
        def _():
            acc_ref[...] = part

        @pl.when(k > 0)
        def _():
            acc_ref[...] += part

        @pl.when(k == n_red - 1)
        def _():
            total = acc_ref[...]
            if r_ref is not None:
                total = total + r_ref[...]
            o_ref[...] = total.astype(o_ref.dtype)

    ins, in_specs = [a, b], [a_spec, b_spec]
    if res is not None:
        ins.append(res)
        in_specs.append(res_spec)
    return pl.pallas_call(
        body, name=name, grid=grid, in_specs=in_specs, out_specs=o_spec, out_shape=out_shape,
        scratch_shapes=[pltpu.VMEM(acc_shape, F32)] if acc_shape is not None else [],
        compiler_params=_params(),
    )(*ins)


def _rms_fwd(name, h, g):
    t, d = h.shape

    def body(h_ref, g_ref, o_ref):
        x = h_ref[...]
        o_ref[...] = (x * _mean_sq_rsqrt(x) * g_ref[...]).astype(BF16)

    return pl.pallas_call(
        body, name=name, grid=(t // ROW_TILE,),
        in_specs=[pl.BlockSpec((ROW_TILE, d), lambda i: (i, 0)), pl.BlockSpec((1, d), lambda i: (0, 0))],
        out_specs=pl.BlockSpec((ROW_TILE, d), lambda i: (i, 0)),
        out_shape=jax.ShapeDtypeStruct((t, d), BF16), compiler_params=_params(),
    )(h, g)


def _rms_bwd_res(name, dres, dhn, h, g, deps=()):
    t, d = h.shape

    def body(dres_ref, dhn_ref, h_ref, g_ref, dh_ref, dhb_ref, dg_ref):
        x = h_ref[...]
        dx, dgt = _rms_bwd(x, _mean_sq_rsqrt(x), g_ref[...], dhn_ref[...])
        dh = dres_ref[...] + dx
        dh_ref[...] = dh
        dhb_ref[...] = dh.astype(BF16)

        @pl.when(pl.program_id(0) == 0)
        def _():
            dg_ref[...] = jnp.zeros_like(dg_ref)

        dg_ref[...] += jnp.sum(dgt, axis=0, keepdims=True)

    row = pl.BlockSpec((ROW_TILE, d), lambda i: (i, 0))
    vec = pl.BlockSpec((1, d), lambda i: (0, 0))
    return pl.pallas_call(
        _ignoring_deps(body, 4, deps), name=name, grid=(t // ROW_TILE,),
        in_specs=[row, row, row, vec] + [ANY_SPEC] * len(deps), out_specs=[row, row, vec],
        out_shape=[jax.ShapeDtypeStruct((t, d), F32), jax.ShapeDtypeStruct((t, d), BF16),
                   jax.ShapeDtypeStruct((1, d), F32)],
        compiler_params=_params(),
    )(dres, dhn, h, g, *deps)


def _loss_bwd(h2, target, g, row_lo, row_hi):
    t, d = h2.shape

    def body(h_ref, tg_ref, g_ref, dh_ref, dhb_ref, loss_ref, dg_ref):
        i = pl.program_id(0)
        x = h_ref[...]
        r = _mean_sq_rsqrt(x)
        gv = g_ref[...]
        y = x * r * gv
        rows = i * ROW_TILE + lax.broadcasted_iota(jnp.int32, (ROW_TILE, 1), 0)
        time = (rows % N_SEG) * (t // N_SEG) + rows // N_SEG
        valid = jnp.logical_and(time >= row_lo, time < row_hi)
        err = jnp.where(valid, y - tg_ref[...], 0.0)
        dy = err * (1.0 / d)
        dx, dgt = _rms_bwd(x, r, gv, dy)
        dh_ref[...] = dx
        dhb_ref[...] = dx.astype(BF16)

        @pl.when(i == 0)
        def _():
            loss_ref[...] = jnp.zeros_like(loss_ref)
            dg_ref[...] = jnp.zeros_like(dg_ref)

        row_loss = jnp.mean(err * err, axis=-1, keepdims=True)
        loss_ref[...] += 0.5 * jnp.sum(row_loss, axis=0, keepdims=True)
        dg_ref[...] += jnp.sum(dgt, axis=0, keepdims=True)

    row = pl.BlockSpec((ROW_TILE, d), lambda i: (i, 0))
    vec = pl.BlockSpec((1, d), lambda i: (0, 0))
    return pl.pallas_call(
        body, name="loss_bwd", grid=(t // ROW_TILE,), in_specs=[row, row, vec],
        out_specs=[row, row, pl.BlockSpec((1, 1), lambda i: (0, 0)), vec],
        out_shape=[jax.ShapeDtypeStruct((t, d), F32), jax.ShapeDtypeStruct((t, d), BF16),
                   jax.ShapeDtypeStruct((1, 1), F32), jax.ShapeDtypeStruct((1, d), F32)],
        compiler_params=_params(),
    )(h2, target, g)


def _prev_halo(i, t):
    return jnp.where(i == 0, t // HALO_ROWS - 1, i * (ROW_TILE // HALO_ROWS) - 1)


def _next_halo(i, t):
    return jnp.where(i == t // ROW_TILE - 1, 0, (i + 1) * (ROW_TILE // HALO_ROWS))


def _causal_taps(cur, halo, first):
    rows = cur.shape[0]
    ext = jnp.concatenate([_wrap_prev_halo(halo, first), cur], axis=0)
    return ext[:rows], ext[N_SEG:N_SEG + rows]


def _anticausal_taps(cur, halo, last):
    rows = cur.shape[0]
    ext = jnp.concatenate([cur, _wrap_next_halo(halo, last)], axis=0)
    return ext[N_SEG:N_SEG + rows], ext[2 * N_SEG:2 * N_SEG + rows]


def _mix_fwd(proj, y, w_glu, conv_w, gain_c, gain_s):
    t = proj.shape[0]
    dc = conv_w.shape[1]
    ds = y.shape[1]

    def body(p_ref, halo_ref, y_ref, wg_ref, cw_ref, gc_ref, gs_ref, mixed_ref, z_ref):
        i = pl.program_id(0)
        p = p_ref[...]
        b, c, v = p[:, :dc], p[:, dc:2 * dc], p[:, 2 * dc:3 * dc]
        cv = c * v
        hp = halo_ref[...]
        x2, x1 = _causal_taps(cv, hp[:, dc:2 * dc] * hp[:, 2 * dc:3 * dc], i == 0)
        cw = cw_ref[...]
        conv = cw[0:1] * x2 + cw[1:2] * x1 + cw[2:3] * cv
        co = b * conv
        mixed_ref[:, :dc] = (co * _mean_sq_rsqrt(co) * gc_ref[...]).astype(BF16)
        g, _ = _gelu(y_ref[...])
        z = _dot(g.astype(BF16), wg_ref[...], NN)
        z_ref[...] = z
        so = g * jax.nn.sigmoid(z)
        mixed_ref[:, dc:] = (so * _mean_sq_rsqrt(so) * gs_ref[...]).astype(BF16)

    const = lambda i: (0, 0)
    return pl.pallas_call(
        body, name="mix_fwd", grid=(t // ROW_TILE,),
        in_specs=[pl.BlockSpec((ROW_TILE, 3 * dc), lambda i: (i, 0)),
                  pl.BlockSpec((HALO_ROWS, 3 * dc), lambda i: (_prev_halo(i, t), 0)),
                  pl.BlockSpec((ROW_TILE, ds), lambda i: (i, 0)),
                  pl.BlockSpec((ds, ds), const), pl.BlockSpec(conv_w.shape, const),
                  pl.BlockSpec((1, dc), const), pl.BlockSpec((1, ds), const)],
        out_specs=[pl.BlockSpec((ROW_TILE, dc + ds), lambda i: (i, 0)),
                   pl.BlockSpec((ROW_TILE, ds), lambda i: (i, 0))],
        out_shape=[jax.ShapeDtypeStruct((t, dc + ds), BF16), jax.ShapeDtypeStruct((t, ds), F32)],
        compiler_params=_params(),
    )(proj, proj, y, w_glu, conv_w, gain_c, gain_s)


def _mix_bwd1(proj, y, z, dmixed, w_glu, conv_w, gain_c, gain_s):
    t = proj.shape[0]
    dc = conv_w.shape[1]
    ds = y.shape[1]

    def body(p_ref, halo_ref, y_ref, z_ref, dm_ref, wg_ref, cw_ref, gc_ref, gs_ref,
             db_ref, dconv_ref, dy_ref, dwg_ref, dcw_ref, dgc_ref, dgs_ref):
        i = pl.program_id(0)

        @pl.when(i == 0)
        def _():
            dwg_ref[...] = jnp.zeros_like(dwg_ref)
            dcw_ref[...] = jnp.zeros_like(dcw_ref)
            dgc_ref[...] = jnp.zeros_like(dgc_ref)
            dgs_ref[...] = jnp.zeros_like(dgs_ref)

        p = p_ref[...]
        b, c, v = p[:, :dc], p[:, dc:2 * dc], p[:, 2 * dc:3 * dc]
        cv = c * v
        hp = halo_ref[...]
        x2, x1 = _causal_taps(cv, hp[:, dc:2 * dc] * hp[:, 2 * dc:3 * dc], i == 0)
        cw = cw_ref[...]
        conv = cw[0:1] * x2 + cw[1:2] * x1 + cw[2:3] * cv
        co = b * conv
        dm = dm_ref[...]
        dco, dgt = _rms_bwd(co, _mean_sq_rsqrt(co), gc_ref[...], dm[:, :dc])
        dgc_ref[...] += jnp.sum(dgt, axis=0, keepdims=True)
        db_ref[...] = (dco * conv).astype(BF16)
        dconv = dco * b
        dconv_ref[...] = dconv
        dcw_ref[0:1, :] += jnp.sum(dconv * x2, axis=0, keepdims=True)
        dcw_ref[1:2, :] += jnp.sum(dconv * x1, axis=0, keepdims=True)
        dcw_ref[2:3, :] += jnp.sum(dconv * cv, axis=0, keepdims=True)

        yv = y_ref[...]
        g, th = _gelu(yv)
        sg = jax.nn.sigmoid(z_ref[...])
        so = g * sg
        dso, dgt = _rms_bwd(so, _mean_sq_rsqrt(so), gs_ref[...], dm[:, dc:])
        dgs_ref[...] += jnp.sum(dgt, axis=0, keepdims=True)
        dz = (dso * g * sg * (1.0 - sg)).astype(BF16)
        dg = dso * sg + _dot(dz, wg_ref[...], NT)
        dwg_ref[...] += _dot(g.astype(BF16), dz, TN)
        dy_ref[...] = dg * _gelu_grad(yv, th)

    const = lambda i: (0, 0)
    row = lambda w: pl.BlockSpec((ROW_TILE, w), lambda i: (i, 0))
    return pl.pallas_call(
        body, name="mix_bwd1", grid=(t // ROW_TILE,),
        in_specs=[row(3 * dc), pl.BlockSpec((HALO_ROWS, 3 * dc), lambda i: (_prev_halo(i, t), 0)),
                  row(ds), row(ds), row(dc + ds), pl.BlockSpec((ds, ds), const),
                  pl.BlockSpec(conv_w.shape, const), pl.BlockSpec((1, dc), const),
                  pl.BlockSpec((1, ds), const)],
        out_specs=[row(dc), row(dc), row(ds), pl.BlockSpec((ds, ds), const),
                   pl.BlockSpec(conv_w.shape, const), pl.BlockSpec((1, dc), const),
                   pl.BlockSpec((1, ds), const)],
        out_shape=[jax.ShapeDtypeStruct((t, dc), BF16), jax.ShapeDtypeStruct((t, dc), F32),
                   jax.ShapeDtypeStruct((t, ds), F32), jax.ShapeDtypeStruct((ds, ds), F32),
                   jax.ShapeDtypeStruct(conv_w.shape, F32), jax.ShapeDtypeStruct((1, dc), F32),
                   jax.ShapeDtypeStruct((1, ds), F32)],
        compiler_params=_params(),
    )(proj, proj, y, z, dmixed, w_glu, conv_w, gain_c, gain_s)


def _mix_bwd2(proj, dconv, conv_w, deps=()):
    t = proj.shape[0]
    dc = conv_w.shape[1]
    n_tiles = t // ROW_TILE

    def body(c_ref, v_ref, d_ref, halo_ref, cw_ref, o_ref):
        i = pl.program_id(0)
        d = d_ref[...]
        u1, u2 = _anticausal_taps(d, halo_ref[...], i == n_tiles - 1)
        cw = cw_ref[...]
        dcv = cw[2:3] * d + cw[1:2] * u1 + cw[0:1] * u2
        o_ref[:, :dc] = (dcv * v_ref[...]).astype(BF16)
        o_ref[:, dc:] = (dcv * c_ref[...]).astype(BF16)

    return pl.pallas_call(
        _ignoring_deps(body, 5, deps), name="mix_bwd2", grid=(n_tiles,),
        in_specs=[pl.BlockSpec((ROW_TILE, dc), lambda i: (i, 1)),
                  pl.BlockSpec((ROW_TILE, dc), lambda i: (i, 2)),
                  pl.BlockSpec((ROW_TILE, dc), lambda i: (i, 0)),
                  pl.BlockSpec((HALO_ROWS, dc), lambda i: (_next_halo(i, t), 0)),
                  pl.BlockSpec(conv_w.shape, lambda i: (0, 0))] + [ANY_SPEC] * len(deps),
        out_specs=pl.BlockSpec((ROW_TILE, 2 * dc), lambda i: (i, 0)),
        out_shape=jax.ShapeDtypeStruct((t, 2 * dc), BF16), compiler_params=_params(),
    )(proj, proj, dconv, dconv, conv_w, *deps)


def _ffn_taps(up_ref, halo, fw_ref, fb_ref, q, r):
    get = lambda s, n: up_ref[q, pl.ds(s, n), :]
    x0 = get(r, ROW_CHUNK)
    x1 = _rows_before(get, halo, r, ROW_CHUNK, N_SEG)
    x2 = _rows_before(get, halo, r, ROW_CHUNK, 2 * N_SEG)
    w = fw_ref[q]
    return w[0:1] * x2 + w[1:2] * x1 + w[2:3] * x0 + fb_ref[q], x2, x1, x0


def _ffn_fwd(up_pre, fw, fb):
    nb, t, w = up_pre.shape
    half = nb // 2

    def body(up_ref, halo_ref, fw_ref, fb_ref, act_ref):
        first = pl.program_id(0) == 0
        for q in range(half):
            halo_a = _wrap_prev_halo(halo_ref[q], first)
            halo_v = _wrap_prev_halo(halo_ref[q + half], first)
            for r in range(0, ROW_TILE, ROW_CHUNK):
                a = _ffn_taps(up_ref, halo_a, fw_ref, fb_ref, q, r)[0]
                val = _ffn_taps(up_ref, halo_v, fw_ref, fb_ref, q + half, r)[0]
                act_ref[q, pl.ds(r, ROW_CHUNK), :] = (a * jax.nn.sigmoid(a) * val).astype(BF16)

    return pl.pallas_call(
        body, name="ffn_fwd", grid=(t // ROW_TILE,),
        in_specs=[pl.BlockSpec((nb, ROW_TILE, w), lambda i: (0, i, 0)),
                  pl.BlockSpec((nb, HALO_ROWS, w), lambda i: (0, _prev_halo(i, t), 0)),
                  pl.BlockSpec(fw.shape, lambda i: (0, 0, 0)),
                  pl.BlockSpec(fb.shape, lambda i: (0, 0, 0))],
        out_specs=pl.BlockSpec((half, ROW_TILE, w), lambda i: (0, i, 0)),
        out_shape=jax.ShapeDtypeStruct((half, t, w), BF16), compiler_params=_params(),
    )(up_pre, up_pre, fw, fb)


def _ffn_bwd1(up_pre, dact, fw, fb, deps=()):
    nb, t, w = up_pre.shape
    half = nb // 2

    def body(up_ref, halo_ref, dact_ref, fw_ref, fb_ref, dup_ref, dfw_ref, dfb_ref):
        first = pl.program_id(0) == 0

        @pl.when(first)
        def _():
            dfw_ref[...] = jnp.zeros_like(dfw_ref)
            dfb_ref[...] = jnp.zeros_like(dfb_ref)

        for q in range(half):
            halos = {q: _wrap_prev_halo(halo_ref[q], first),
                     q + half: _wrap_prev_halo(halo_ref[q + half], first)}
            sums = {blk: [jnp.zeros((SUBLANES, w), F32)] * 4 for blk in halos}
            fold = lambda v: sum(v[s:s + SUBLANES] for s in range(0, ROW_CHUNK, SUBLANES))
            for r in range(0, ROW_TILE, ROW_CHUNK):
                taps_a = _ffn_taps(up_ref, halos[q], fw_ref, fb_ref, q, r)
                taps_v = _ffn_taps(up_ref, halos[q + half], fw_ref, fb_ref, q + half, r)
                a, val = taps_a[0], taps_v[0]
                da_ct = dact_ref[q, pl.ds(r, ROW_CHUNK), :]
                sg = jax.nn.sigmoid(a)
                da = da_ct * val * sg * (1.0 + a * (1.0 - sg))
                dval = da_ct * a * sg
                for blk, dv, taps in ((q, da, taps_a), (q + half, dval, taps_v)):
                    dup_ref[blk, pl.ds(r, ROW_CHUNK), :] = dv
                    terms = (dv, dv * taps[1], dv * taps[2], dv * taps[3])
                    sums[blk] = [s + fold(v) for s, v in zip(sums[blk], terms)]
            for blk, parts in sums.items():
                s_b, s_w0, s_w1, s_w2 = (jnp.sum(p, axis=0, keepdims=True) for p in parts)
                dfb_ref[blk] += s_b
                for k, s_w in enumerate((s_w0, s_w1, s_w2)):
                    dfw_ref[blk, k:k + 1, :] += s_w

    return pl.pallas_call(
        _ignoring_deps(body, 5, deps), name="ffn_bwd1", grid=(t // ROW_TILE,),
        in_specs=[pl.BlockSpec((nb, ROW_TILE, w), lambda i: (0, i, 0)),
                  pl.BlockSpec((nb, HALO_ROWS, w), lambda i: (0, _prev_halo(i, t), 0)),
                  pl.BlockSpec((half, ROW_TILE, w), lambda i: (0, i, 0)),
                  pl.BlockSpec(fw.shape, lambda i: (0, 0, 0)),
                  pl.BlockSpec(fb.shape, lambda i: (0, 0, 0))] + [ANY_SPEC] * len(deps),
        out_specs=[pl.BlockSpec((nb, ROW_TILE, w), lambda i: (0, i, 0)),
                   pl.BlockSpec(fw.shape, lambda i: (0, 0, 0)),
                   pl.BlockSpec(fb.shape, lambda i: (0, 0, 0))],
        out_shape=[jax.ShapeDtypeStruct((nb, t, w), F32), jax.ShapeDtypeStruct(fw.shape, F32),
                   jax.ShapeDtypeStruct(fb.shape, F32)],
        compiler_params=_params(),
    )(up_pre, up_pre, dact, fw, fb, *deps)


def _ffn_bwd2(dup, fw):
    nb, t, w = dup.shape
    n_tiles = t // ROW_TILE

    def body(d_ref, halo_ref, fw_ref, o_ref):
        last = pl.program_id(0) == n_tiles - 1
        for q in range(nb):
            halo = _wrap_next_halo(halo_ref[q], last)
            get = lambda s, n: d_ref[q, pl.ds(s, n), :]
            cw = fw_ref[q]
            for r in range(0, ROW_TILE, ROW_CHUNK):
                d = get(r, ROW_CHUNK)
                u1 = _rows_after(get, halo, r, ROW_CHUNK, N_SEG, ROW_TILE)
                u2 = _rows_after(get, halo, r, ROW_CHUNK, 2 * N_SEG, ROW_TILE)
                o_ref[q, pl.ds(r, ROW_CHUNK), :] = (cw[2:3] * d + cw[1:2] * u1 + cw[0:1] * u2).astype(BF16)

    return pl.pallas_call(
        body, name="ffn_bwd2", grid=(n_tiles,),
        in_specs=[pl.BlockSpec((nb, ROW_TILE, w), lambda i: (0, i, 0)),
                  pl.BlockSpec((nb, HALO_ROWS, w), lambda i: (0, _next_halo(i, t), 0)),
                  pl.BlockSpec(fw.shape, lambda i: (0, 0, 0))],
        out_specs=pl.BlockSpec((nb, ROW_TILE, w), lambda i: (0, i, 0)),
        out_shape=jax.ShapeDtypeStruct((nb, t, w), BF16), compiler_params=_params(),
    )(dup, dup, fw)


def _to_segments(a):
    t, c = a.shape
    return a.reshape(N_SEG, t // N_SEG, c).transpose(1, 0, 2).reshape(t, c)


def _from_segments(a):
    t, c = a.shape
    return a.reshape(t // N_SEG, N_SEG, c).transpose(1, 0, 2).reshape(t, c)


def _cmul(ar, ai, br, bi):
    return ar * br - ai * bi, ar * bi + ai * br


def _segment_carries(pr, pi, fr, fi, forward):
    row = lax.broadcasted_iota(jnp.int32, fr.shape, 0)
    edge = row == (0 if forward else N_SEG - 1)
    shift = 1 if forward else N_SEG - 1
    sr, si = jnp.zeros_like(fr), jnp.zeros_like(fi)
    for _ in range(N_SEG - 1):
        tr, ti = _cmul(pr, pi, sr, si)
        sr = jnp.where(edge, 0.0, pltpu.roll(tr + fr, shift, 0))
        si = jnp.where(edge, 0.0, pltpu.roll(ti + fi, shift, 0))
    return sr, si


def _rows(i):
    return pl.ds(pl.multiple_of(i * SUBLANES, SUBLANES), SUBLANES)


def _s5_fwd(proj, u_col, bb_re, bb_im, a_re, a_im, cc_re, cc_im, d_skip):
    t = proj.shape[0]
    nch, _, cs = bb_re.shape
    ds = nch * SSM_CHUNK
    u_blk = u_col // SSM_CHUNK
    steps = t // N_SEG

    def body(u_ref, bbr_ref, bbi_ref, ar_ref, ai_ref, ccr_ref, cci_ref, d_ref, sr_ref, si_ref, y_ref):
        ub = u_ref[...].astype(BF16)
        sr_ref[...] = _dot(ub, bbr_ref[...], NN)
        si_ref[...] = _dot(ub, bbi_ref[...], NN)
        ar = jnp.broadcast_to(ar_ref[...], (N_SEG, cs))
        ai = jnp.broadcast_to(ai_ref[...], (N_SEG, cs))
        zero = jnp.zeros((N_SEG, cs), F32)

        def totals(i, carry):
            sr, si, pr, pi = carry
            tr, ti = _cmul(ar, ai, sr, si)
            qr, qi = _cmul(ar, ai, pr, pi)
            return tr + sr_ref[_rows(i), :], ti + si_ref[_rows(i), :], qr, qi

        fr, fi, pr, pi = lax.fori_loop(0, steps, totals, (zero, zero, zero + 1.0, zero))
        s0r, s0i = _segment_carries(pr, pi, fr, fi, True)

        def scan(i, carry):
            tr, ti = _cmul(ar, ai, *carry)
            nr, ni = tr + sr_ref[_rows(i), :], ti + si_ref[_rows(i), :]
            sr_ref[_rows(i), :] = nr
            si_ref[_rows(i), :] = ni
            return nr, ni

        lax.fori_loop(0, steps, scan, (s0r, s0i))
        y_ref[...] = (_dot(sr_ref[...].astype(BF16), ccr_ref[...], NN)
                      - _dot(si_ref[...].astype(BF16), cci_ref[...], NN)
                      + d_ref[...] * u_ref[...])

    chunk3 = lambda r, c: pl.BlockSpec((None, r, c), lambda j: (j, 0, 0))
    return pl.pallas_call(
        body, name="s5_fwd", grid=(nch,),
        in_specs=[pl.BlockSpec((t, SSM_CHUNK), lambda j: (0, j + u_blk)),
                  chunk3(SSM_CHUNK, cs), chunk3(SSM_CHUNK, cs), chunk3(1, cs), chunk3(1, cs),
                  chunk3(cs, SSM_CHUNK), chunk3(cs, SSM_CHUNK), chunk3(1, SSM_CHUNK)],
        out_specs=[pl.BlockSpec((t, cs), lambda j: (0, j)), pl.BlockSpec((t, cs), lambda j: (0, j)),
                   pl.BlockSpec((t, SSM_CHUNK), lambda j: (0, j))],
        out_shape=[jax.ShapeDtypeStruct((t, nch * cs), F32), jax.ShapeDtypeStruct((t, nch * cs), F32),
                   jax.ShapeDtypeStruct((t, ds), F32)],
        compiler_params=_params(),
    )(proj, bb_re, bb_im, a_re, a_im, cc_re, cc_im, d_skip)


def _s5_bwd(dy, proj, u_col, s_re, s_im, bb_re, bb_im, a_re, a_im, cc_re, cc_im, d_skip):
    t, ds = dy.shape
    nch, _, cs = bb_re.shape
    u_blk = u_col // SSM_CHUNK
    steps = t // N_SEG

    def body(dy_ref, u_ref, sr_ref, si_ref, bbr_ref, bbi_ref, ar_ref, ai_ref, ccr_ref, cci_ref, d_ref,
             du_ref, dbbr_ref, dbbi_ref, dar_ref, dai_ref, dccr_ref, dcci_ref, dd_ref, gr_ref, gi_ref):
        dyv = dy_ref[...]
        dyb = dyv.astype(BF16)
        gr_ref[...] = _dot(dyb, ccr_ref[...], NT)
        gi_ref[...] = -_dot(dyb, cci_ref[...], NT)
        ar = jnp.broadcast_to(ar_ref[...], (N_SEG, cs))
        ai = -jnp.broadcast_to(ai_ref[...], (N_SEG, cs))
        zero = jnp.zeros((N_SEG, cs), F32)

        def totals(k, carry):
            i = steps - 1 - k
            gr, gi, pr, pi = carry
            tr, ti = _cmul(ar, ai, gr, gi)
            qr, qi = _cmul(ar, ai, pr, pi)
            return tr + gr_ref[_rows(i), :], ti + gi_ref[_rows(i), :], qr, qi

        fr, fi, pr, pi = lax.fori_loop(0, steps, totals, (zero, zero, zero + 1.0, zero))
        e0r, e0i = _segment_carries(pr, pi, fr, fi, False)

        def step(i, gr, gi, pr, pi, acc_r, acc_i):
            tr, ti = _cmul(ar, ai, gr, gi)
            nr, ni = tr + gr_ref[_rows(i), :], ti + gi_ref[_rows(i), :]
            gr_ref[_rows(i), :] = nr
            gi_ref[_rows(i), :] = ni
            return nr, ni, acc_r + nr * pr + ni * pi, acc_i + ni * pr - nr * pi

        def scan(k, carry):
            i = steps - 1 - k
            gr, gi, acc_r, acc_i = carry
            return step(i, gr, gi, sr_ref[_rows(i - 1), :], si_ref[_rows(i - 1), :], acc_r, acc_i)

        gr, gi, acc_r, acc_i = lax.fori_loop(0, steps - 1, scan, (e0r, e0i, zero, zero))
        row = lax.broadcasted_iota(jnp.int32, (N_SEG, cs), 0)
        last = _rows(steps - 1)
        pr = jnp.where(row == 0, 0.0, pltpu.roll(sr_ref[last, :], 1, 0))
        pi = jnp.where(row == 0, 0.0, pltpu.roll(si_ref[last, :], 1, 0))
        _, _, acc_r, acc_i = step(0, gr, gi, pr, pi, acc_r, acc_i)
        dar_ref[...] = jnp.sum(acc_r, axis=0, keepdims=True)
        dai_ref[...] = jnp.sum(acc_i, axis=0, keepdims=True)

        uv = u_ref[...]
        ub = uv.astype(BF16)
        grb = gr_ref[...].astype(BF16)
        gib = gi_ref[...].astype(BF16)
        du = d_ref[...] * dyv + _dot(grb, bbr_ref[...], NT) + _dot(gib, bbi_ref[...], NT)
        du_ref[...] = du.astype(BF16)
        dbbr_ref[...] = _dot(ub, grb, TN)
        dbbi_ref[...] = _dot(ub, gib, TN)
        dccr_ref[...] = _dot(sr_ref[...].astype(BF16), dyb, TN)
        dcci_ref[...] = -_dot(si_ref[...].astype(BF16), dyb, TN)
        dd_ref[...] = jnp.sum(dyv * uv, axis=0, keepdims=True)

    chunk3 = lambda r, c: pl.BlockSpec((None, r, c), lambda j: (j, 0, 0))
    cols = lambda w: pl.BlockSpec((t, w), lambda j: (0, j))
    return pl.pallas_call(
        body, name="s5_bwd", grid=(nch,),
        in_specs=[cols(SSM_CHUNK), pl.BlockSpec((t, SSM_CHUNK), lambda j: (0, j + u_blk)), cols(cs), cols(cs),
                  chunk3(SSM_CHUNK, cs), chunk3(SSM_CHUNK, cs), chunk3(1, cs), chunk3(1, cs),
                  chunk3(cs, SSM_CHUNK), chunk3(cs, SSM_CHUNK), chunk3(1, SSM_CHUNK)],
        out_specs=[cols(SSM_CHUNK), chunk3(SSM_CHUNK, cs), chunk3(SSM_CHUNK, cs), chunk3(1, cs),
                   chunk3(1, cs), chunk3(cs, SSM_CHUNK), chunk3(cs, SSM_CHUNK), chunk3(1, SSM_CHUNK)],
        out_shape=[jax.ShapeDtypeStruct((t, ds), BF16),
                   jax.ShapeDtypeStruct((nch, SSM_CHUNK, cs), F32), jax.ShapeDtypeStruct((nch, SSM_CHUNK, cs), F32),
                   jax.ShapeDtypeStruct((nch, 1, cs), F32), jax.ShapeDtypeStruct((nch, 1, cs), F32),
                   jax.ShapeDtypeStruct((nch, cs, SSM_CHUNK), F32), jax.ShapeDtypeStruct((nch, cs, SSM_CHUNK), F32),
                   jax.ShapeDtypeStruct((nch, 1, SSM_CHUNK), F32)],
        scratch_shapes=[pltpu.VMEM((t, cs), F32), pltpu.VMEM((t, cs), F32)],
        compiler_params=_params(),
    )(dy, proj, s_re, s_im, bb_re, bb_im, a_re, a_im, cc_re, cc_im, d_skip)


def _discretize(lr, li, log_dt, br, bi):
    dt = jnp.exp(log_dt)
    mag = jnp.exp(lr * dt)
    ang = li * dt
    a_re = mag * jnp.cos(ang)
    a_im = mag * jnp.sin(ang)
    den = lr * lr + li * li
    nr = a_re - 1.0
    f_re = (nr * lr + a_im * li) / den
    f_im = (a_im * lr - nr * li) / den
    return a_re, a_im, f_re * br - f_im * bi, f_re * bi + f_im * br


def _whole(shape):
    return pl.BlockSpec(shape, lambda: (0,) * len(shape))


def _disc_fwd(lr, li, log_dt, br, bi, deps=()):
    def body(lr_ref, li_ref, dt_ref, br_ref, bi_ref, ar_ref, ai_ref, bbr_ref, bbi_ref):
        outs = _discretize(lr_ref[...], li_ref[...], dt_ref[...], br_ref[...], bi_ref[...])
        for ref, val in zip((ar_ref, ai_ref, bbr_ref, bbi_ref), outs):
            ref[...] = val

    args = (lr, li, log_dt, br, bi)
    outs = (lr, lr, br, br)
    return pl.pallas_call(
        _ignoring_deps(body, 5, deps), name="disc_fwd",
        in_specs=[_whole(a.shape) for a in args] + [ANY_SPEC] * len(deps),
        out_specs=[_whole(a.shape) for a in outs],
        out_shape=[jax.ShapeDtypeStruct(a.shape, F32) for a in outs],
    )(*args, *deps)


def _disc_bwd(lr, li, log_dt, br, bi, dar, dai, dbbr, dbbi):
    def body(lr_ref, li_ref, dt_ref, br_ref, bi_ref, dar_ref, dai_ref, dbbr_ref, dbbi_ref,
             dlr_ref, dli_ref, ddt_ref, dbr_ref, dbi_ref):
        _, vjp = jax.vjp(_discretize, lr_ref[...], li_ref[...], dt_ref[...], br_ref[...], bi_ref[...])
        grads = vjp((dar_ref[...], dai_ref[...], dbbr_ref[...], dbbi_ref[...]))
        for ref, val in zip((dlr_ref, dli_ref, ddt_ref, dbr_ref, dbi_ref), grads):
            ref[...] = val

    args = (lr, li, log_dt, br, bi, dar, dai, dbbr, dbbi)
    outs = (lr, li, log_dt, br, bi)
    return pl.pallas_call(
        body, name="disc_bwd", in_specs=[_whole(a.shape) for a in args],
        out_specs=[_whole(a.shape) for a in outs],
        out_shape=[jax.ShapeDtypeStruct(a.shape, F32) for a in outs],
    )(*args)


def _adamw(w, g, m, v):
    m = ADAM_B1 * m + (1.0 - ADAM_B1) * g
    v = ADAM_B2 * v + (1.0 - ADAM_B2) * (g * g)
    m_hat = m / ADAM_BC1
    v_hat = v / ADAM_BC2
    delta = -ADAM_LR * (m_hat / (jnp.sqrt(v_hat) + ADAM_EPS) + ADAM_WD * w)
    return delta, m, v


def _adamw_reduce(name, parts, w, m, v):
    _, r, c = parts.shape
    tr = r
    for cand in (256, 176, 128):
        if r % cand == 0:
            tr = cand
            break

    def body(p_ref, w_ref, m_ref, v_ref, g_ref, d_ref, nm_ref, nv_ref):
        g = p_ref[0].astype(F32)
        for k in range(1, N_DEV):
            g = g + p_ref[k].astype(F32)
        delta, nm, nv = _adamw(w_ref[...], g, m_ref[...], v_ref[...])
        g_ref[...] = g
        d_ref[...] = delta
        nm_ref[...] = nm
        nv_ref[...] = nv

    blk = pl.BlockSpec((tr, c), lambda i: (i, 0))
    return pl.pallas_call(
        body, name=name, grid=(r // tr,),
        in_specs=[pl.BlockSpec((N_DEV, tr, c), lambda i: (0, i, 0)), blk, blk, blk],
        out_specs=[blk] * 4, out_shape=[jax.ShapeDtypeStruct((r, c), F32)] * 4,
        compiler_params=_params(),
    )(parts, w, m, v)


def _sum_parts(name, parts):
    _, r, c = parts.shape

    def body(p_ref, o_ref):
        g = p_ref[0]
        for k in range(1, N_DEV):
            g = g + p_ref[k]
        o_ref[...] = g

    return pl.pallas_call(
        body, name=name, in_specs=[_whole(parts.shape)], out_specs=_whole((r, c)),
        out_shape=jax.ShapeDtypeStruct((r, c), F32), compiler_params=_params(),
    )(parts)


def _adamw_many(name, grads, ws, ms, vs):
    n = len(grads)

    def body(*refs):
        ins, outs = refs[:4 * n], refs[4 * n:]
        for i in range(n):
            g, w, m, v = (ins[j * n + i][...] for j in range(4))
            for ref, val in zip((outs[i], outs[n + i], outs[2 * n + i]), _adamw(w, g, m, v)):
                ref[...] = val

    args = (*grads, *ws, *ms, *vs)
    outs = pl.pallas_call(
        body, name=name, in_specs=[_whole(a.shape) for a in args],
        out_specs=[_whole(a.shape) for a in ws] * 3,
        out_shape=[jax.ShapeDtypeStruct(a.shape, F32) for a in ws] * 3, compiler_params=_params(),
    )(*args)
    return outs[:n], outs[n:2 * n], outs[2 * n:]


def _pack(arrays, rows):
    flat = jnp.concatenate([a.reshape(-1) for a in arrays])
    return jnp.pad(flat, (0, rows * LANES - flat.shape[0])).reshape(rows, LANES)


def _unpack(packed, shapes):
    flat = packed.reshape(-1)
    out, off = [], 0
    for s in shapes:
        n = math.prod(s)
        out.append(flat[off:off + n].reshape(s))
        off += n
    return out


def _packed_rows(shapes):
    n = sum(math.prod(s) for s in shapes)
    return -(-n // (SUBLANES * LANES)) * SUBLANES


def _block_diag(x):
    j, g, r, c = x.shape
    eye = jnp.eye(g, dtype=x.dtype)
    return (x[:, :, :, None, :] * eye[None, :, None, :, None]).reshape(j, g * r, g * c)


def _diag_blocks(x, g):
    j, gr, gc = x.shape
    r, c = gr // g, gc // g
    eye = jnp.eye(g, dtype=x.dtype)
    return (x.reshape(j, g, r, g, c) * eye[None, :, None, :, None]).sum(axis=3)


def kernel(x, meta_tokens, norm_mix_g, w_in, conv_w, ssm_lam_re, ssm_lam_im, ssm_log_dt, ssm_b_re, ssm_b_im, ssm_c_re, ssm_c_im, ssm_d, ssm_w_glu, gain_conv_out, gain_ssm_out, w_out, norm_ffn_g, w_up, ffn_conv_w, ffn_conv_b, w_down, norm_final_g, loss_target, m_meta_tokens, m_norm_mix_g, m_w_in, m_conv_w, m_ssm_lam_re, m_ssm_lam_im, m_ssm_log_dt, m_ssm_b_re, m_ssm_b_im, m_ssm_c_re, m_ssm_c_im, m_ssm_d, m_ssm_w_glu, m_gain_conv_out, m_gain_ssm_out, m_w_out, m_norm_ffn_g, m_w_up, m_ffn_conv_w, m_ffn_conv_b, m_w_down, m_norm_final_g, v_meta_tokens, v_norm_mix_g, v_w_in, v_conv_w, v_ssm_lam_re, v_ssm_lam_im, v_ssm_log_dt, v_ssm_b_re, v_ssm_b_im, v_ssm_c_re, v_ssm_c_im, v_ssm_d, v_ssm_w_glu, v_gain_conv_out, v_gain_ssm_out, v_w_out, v_norm_ffn_g, v_w_up, v_ffn_conv_w, v_ffn_conv_b, v_w_down, v_norm_final_g):
    weights = dict(meta_tokens=meta_tokens, norm_mix_g=norm_mix_g, w_in=w_in, conv_w=conv_w, ssm_lam_re=ssm_lam_re, ssm_lam_im=ssm_lam_im, ssm_log_dt=ssm_log_dt, ssm_b_re=ssm_b_re, ssm_b_im=ssm_b_im, ssm_c_re=ssm_c_re, ssm_c_im=ssm_c_im, ssm_d=ssm_d, ssm_w_glu=ssm_w_glu, gain_conv_out=gain_conv_out, gain_ssm_out=gain_ssm_out, w_out=w_out, norm_ffn_g=norm_ffn_g, w_up=w_up, ffn_conv_w=ffn_conv_w, ffn_conv_b=ffn_conv_b, w_down=w_down, norm_final_g=norm_final_g)
    mom_m = dict(meta_tokens=m_meta_tokens, norm_mix_g=m_norm_mix_g, w_in=m_w_in, conv_w=m_conv_w, ssm_lam_re=m_ssm_lam_re, ssm_lam_im=m_ssm_lam_im, ssm_log_dt=m_ssm_log_dt, ssm_b_re=m_ssm_b_re, ssm_b_im=m_ssm_b_im, ssm_c_re=m_ssm_c_re, ssm_c_im=m_ssm_c_im, ssm_d=m_ssm_d, ssm_w_glu=m_ssm_w_glu, gain_conv_out=m_gain_conv_out, gain_ssm_out=m_gain_ssm_out, w_out=m_w_out, norm_ffn_g=m_norm_ffn_g, w_up=m_w_up, ffn_conv_w=m_ffn_conv_w, ffn_conv_b=m_ffn_conv_b, w_down=m_w_down, norm_final_g=m_norm_final_g)
    mom_v = dict(meta_tokens=v_meta_tokens, norm_mix_g=v_norm_mix_g, w_in=v_w_in, conv_w=v_conv_w, ssm_lam_re=v_ssm_lam_re, ssm_lam_im=v_ssm_lam_im, ssm_log_dt=v_ssm_log_dt, ssm_b_re=v_ssm_b_re, ssm_b_im=v_ssm_b_im, ssm_c_re=v_ssm_c_re, ssm_c_im=v_ssm_c_im, ssm_d=v_ssm_d, ssm_w_glu=v_ssm_w_glu, gain_conv_out=v_gain_conv_out, gain_ssm_out=v_gain_ssm_out, w_out=v_w_out, norm_ffn_g=v_norm_ffn_g, w_up=v_w_up, ffn_conv_w=v_ffn_conv_w, ffn_conv_b=v_ffn_conv_b, w_down=v_w_down, norm_final_g=v_norm_final_g)
    names = list(weights)

    n_meta, d_meta = meta_tokens.shape
    seq, d = x.shape[1], x.shape[2]
    rows_used = n_meta + seq
    t = -(-rows_used // ROW_TILE) * ROW_TILE
    d_in_s = w_in.shape[2]
    dc_s = conv_w.shape[2]
    dc = dc_s * N_DEV
    ds = ssm_w_glu.shape[2]
    n_groups, n_state, grp = ssm_b_re.shape[1:]
    ns = n_groups * n_state
    nch = ds // SSM_CHUNK
    gpc = n_groups // nch
    ff_s = w_up.shape[2]
    dn_s = w_down.shape[1]
    assert 3 * dc + ds == d_in_s * N_DEV and 2 * dn_s == ff_s and t % (N_SEG * SUBLANES) == 0

    small_shard = jnp.concatenate([meta_tokens.reshape(-1), conv_w.reshape(-1), ffn_conv_w.reshape(-1)])
    n_small = small_shard.shape[0]
    small_rows = -(-n_small // LANES)
    small_shard = jnp.pad(small_shard, (0, small_rows * LANES - n_small)).reshape(small_rows, LANES)
    ag, ag_token = _gather2_start("gather_weights_start", [
        small_shard, w_in[0].astype(BF16), ssm_w_glu[0].astype(BF16), w_out[0].astype(BF16),
        w_up[0].astype(BF16), w_down[0].astype(BF16)])
    fb = ffn_conv_b.reshape(N_DEV, 1, ff_s)

    col = lambda a: a.reshape(ns, 1)
    lr, li = col(ssm_lam_re), col(ssm_lam_im)
    log_dt_e = jnp.broadcast_to(ssm_log_dt.reshape(n_groups, 1), (n_groups, n_state)).reshape(ns, 1)
    br, bi = ssm_b_re.reshape(ns, grp), ssm_b_im.reshape(ns, grp)
    a_re, a_im, bb_re, bb_im = _disc_fwd(lr, li, log_dt_e, br, bi, deps=(ag_token,))
    cs = gpc * n_state
    chunk_row = lambda a: a.reshape(nch, 1, cs)
    to_bb = lambda a: _block_diag(a.reshape(nch, gpc, n_state, grp).transpose(0, 1, 3, 2)).astype(BF16)
    to_cc = lambda a: _block_diag(a.reshape(nch, gpc, grp, n_state).transpose(0, 1, 3, 2)).astype(BF16)
    bbm_re, bbm_im = to_bb(bb_re), to_bb(bb_im)
    ccm_re, ccm_im = to_cc(ssm_c_re), to_cc(ssm_c_im)
    a_re_c, a_im_c = chunk_row(a_re), chunk_row(a_im)
    d_skip = ssm_d.reshape(nch, 1, SSM_CHUNK)

    token = _gather2_forward("gather_weights_forward_first", ag, (0, 1, 2, 3), bbm_im)
    (g_small,) = _gather2_wait("gather_weights_wait_small", ag, (0,), token)
    g_small = g_small.reshape(N_DEV, -1)
    o1 = n_meta * d_meta
    o2 = o1 + 3 * dc_s
    meta_full = g_small[:, :o1].reshape(N_DEV, n_meta, d_meta).transpose(1, 0, 2).reshape(n_meta, d)
    conv_w_f = g_small[:, o1:o2].reshape(N_DEV, 3, dc_s).transpose(1, 0, 2).reshape(3, dc)
    fw = g_small[:, o2:o2 + 3 * ff_s].reshape(N_DEV, 3, ff_s)
    h0 = _to_segments(jnp.concatenate([meta_full, x[0], jnp.zeros((t - rows_used, d), F32)], axis=0))
    target = _to_segments(jnp.pad(loss_target[0], ((n_meta, t - rows_used), (0, 0))))
    full_t = lambda w: pl.BlockSpec((t, w), lambda *_: (0, 0))

    hn1 = _rms_fwd("norm_mix", h0, norm_mix_g)
    (g_in,) = _gather2_wait("gather_weights_wait_in", ag, (1,), hn1)
    proj = _mm("proj", hn1, g_in, dims=NN, grid=(N_DEV,), a_spec=full_t(d),
               b_spec=pl.BlockSpec((None, d, d_in_s), lambda j: (j, 0, 0)),
               o_spec=pl.BlockSpec((t, d_in_s), lambda j: (0, j)),
               out_shape=jax.ShapeDtypeStruct((t, N_DEV * d_in_s), F32))
    s_re, s_im, y_ssm = _s5_fwd(proj, 3 * dc, bbm_re, bbm_im, a_re_c, a_im_c, ccm_re, ccm_im, d_skip)
    token = _gather2_forward("gather_weights_forward_up", ag, (4,), y_ssm)
    g_glu, g_out = _gather2_wait("gather_weights_wait_mix", ag, (2, 3), token)
    w_out_f = g_out.reshape(-1, d)
    w_glu_f = g_glu.reshape(ds, ds)
    mixed, z_glu = _mix_fwd(proj, y_ssm, w_glu_f, conv_w_f, gain_conv_out, gain_ssm_out)
    tn_out = 256
    h1 = _mm("out_proj", mixed, w_out_f, dims=NN, grid=(d // tn_out,), a_spec=full_t(dc + ds),
             b_spec=pl.BlockSpec((dc + ds, tn_out), lambda j: (0, j)),
             o_spec=pl.BlockSpec((t, tn_out), lambda j: (0, j)),
             out_shape=jax.ShapeDtypeStruct((t, d), F32),
             res=h0, res_spec=pl.BlockSpec((t, tn_out), lambda j: (0, j)))
    hn2 = _rms_fwd("norm_ffn", h1, norm_ffn_g)
    token = _gather2_forward("gather_weights_forward_down", ag, (5,), hn2)
    (g_up,) = _gather2_wait("gather_weights_wait_up", ag, (4,), token)
    up_pre = _mm("up_proj", hn2, g_up, dims=NN, grid=(N_DEV,), a_spec=full_t(d),
                 b_spec=pl.BlockSpec((None, d, ff_s), lambda j: (j, 0, 0)),
                 o_spec=pl.BlockSpec((None, t, ff_s), lambda j: (j, 0, 0)),
                 out_shape=jax.ShapeDtypeStruct((N_DEV, t, ff_s), F32))
    act = _ffn_fwd(up_pre, fw, fb)
    nhalf = N_DEV // 2
    (g_down,) = _gather2_wait("gather_weights_wait_down", ag, (5,), act)
    w_down_f = g_down.reshape(nhalf, 2 * dn_s, d)
    h2 = _mm("down_proj", act, w_down_f, dims=NN, grid=(d // tn_out, nhalf),
             a_spec=pl.BlockSpec((None, t, ff_s), lambda j, k: (k, 0, 0)),
             b_spec=pl.BlockSpec((None, ff_s, tn_out), lambda j, k: (k, 0, j)),
             o_spec=pl.BlockSpec((t, tn_out), lambda j, k: (0, j)),
             out_shape=jax.ShapeDtypeStruct((t, d), F32), acc_shape=(t, tn_out),
             res=h1, res_spec=pl.BlockSpec((t, tn_out), lambda j, k: (0, j)))

    dh2, dh2_b, loss_part, d_norm_final = _loss_bwd(h2, target, norm_final_g.reshape(1, d), n_meta, rows_used)
    dact = _mm("down_dgrad", dh2_b, w_down_f, dims=NT, grid=(nhalf,), a_spec=full_t(d),
               b_spec=pl.BlockSpec((None, ff_s, d), lambda j: (j, 0, 0)),
               o_spec=pl.BlockSpec((None, t, ff_s), lambda j: (j, 0, 0)),
               out_shape=jax.ShapeDtypeStruct((nhalf, t, ff_s), F32))
    tn_w = 512
    dw_down = _mm("down_wgrad", act, dh2_b, dims=TN, grid=(nhalf, d // tn_w),
                  a_spec=pl.BlockSpec((None, t, ff_s), lambda j, n: (j, 0, 0)),
                  b_spec=pl.BlockSpec((t, tn_w), lambda j, n: (0, n)),
                  o_spec=pl.BlockSpec((None, ff_s, tn_w), lambda j, n: (j, 0, n)),
                  out_shape=jax.ShapeDtypeStruct((nhalf, ff_s, d), BF16))
    ex_down, token = _send_start("exchange_down_start", [dw_down.reshape(N_DEV, dn_s, d)], gather=False)
    dup, d_fw, d_fb = _ffn_bwd1(up_pre, dact, fw, fb, deps=(token,))
    dup_pre = _ffn_bwd2(dup, fw)
    dhn2 = _mm("up_dgrad", dup_pre, g_up, dims=NT, grid=(d // tn_w, N_DEV),
               a_spec=pl.BlockSpec((None, t, ff_s), lambda i, k: (k, 0, 0)),
               b_spec=pl.BlockSpec((None, tn_w, ff_s), lambda i, k: (k, i, 0)),
               o_spec=pl.BlockSpec((t, tn_w), lambda i, k: (0, i)),
               out_shape=jax.ShapeDtypeStruct((t, d), F32), acc_shape=(t, tn_w))
    dw_up = _mm("up_wgrad", hn2, dup_pre, dims=TN, grid=(N_DEV, d // tn_w),
                a_spec=pl.BlockSpec((t, tn_w), lambda j, i: (0, i)),
                b_spec=pl.BlockSpec((None, t, ff_s), lambda j, i: (j, 0, 0)),
                o_spec=pl.BlockSpec((None, tn_w, ff_s), lambda j, i: (j, i, 0)),
                out_shape=jax.ShapeDtypeStruct((N_DEV, d, ff_s), BF16))
    ex_up, token = _send_start("exchange_up_start", [dw_up], gather=False)
    dh1, dh1_b, d_norm_ffn = _rms_bwd_res("norm_ffn_bwd", dh2, dhn2, h1, norm_ffn_g, deps=(token,))
    dmixed = _mm("out_dgrad", dh1_b, w_out_f, dims=NT, grid=((dc + ds) // tn_out,), a_spec=full_t(d),
                 b_spec=pl.BlockSpec((tn_out, d), lambda i: (i, 0)),
                 o_spec=pl.BlockSpec((t, tn_out), lambda i: (0, i)),
                 out_shape=jax.ShapeDtypeStruct((t, dc + ds), F32))
    dw_out = _mm("out_wgrad", mixed, dh1_b, dims=TN, grid=((dc + ds) // tn_out,),
                 a_spec=pl.BlockSpec((t, tn_out), lambda i: (0, i)), b_spec=full_t(d),
                 o_spec=pl.BlockSpec((tn_out, d), lambda i: (i, 0)),
                 out_shape=jax.ShapeDtypeStruct((dc + ds, d), BF16))
    db_gate, dconv, dy_ssm, d_wglu, d_conv_w, d_gain_c, d_gain_s = _mix_bwd1(
        proj, y_ssm, z_glu, dmixed, w_glu_f, conv_w_f, gain_conv_out, gain_ssm_out)
    ex_mix, token = _send_start("exchange_mix_start", [
        dw_out.reshape(N_DEV, -1, d), d_wglu.astype(BF16).reshape(N_DEV, -1, ds),
        d_conv_w.reshape(3, N_DEV, dc_s).transpose(1, 0, 2), d_fw], gather=False)
    dcdv = _mix_bwd2(proj, dconv, conv_w_f, deps=(token,))
    (du, d_bbm_re, d_bbm_im, d_a_re, d_a_im, d_ccm_re, d_ccm_im, d_dskip) = _s5_bwd(
        dy_ssm, proj, 3 * dc, s_re, s_im, bbm_re, bbm_im, a_re_c, a_im_c, ccm_re, ccm_im, d_skip)
    dproj = jnp.concatenate([db_gate, dcdv, du], axis=1)
    dhn1 = _mm("proj_dgrad", dproj, g_in, dims=NT, grid=(d // tn_w, N_DEV),
               a_spec=pl.BlockSpec((t, d_in_s), lambda i, k: (0, k)),
               b_spec=pl.BlockSpec((None, tn_w, d_in_s), lambda i, k: (k, i, 0)),
               o_spec=pl.BlockSpec((t, tn_w), lambda i, k: (0, i)),
               out_shape=jax.ShapeDtypeStruct((t, d), F32), acc_shape=(t, tn_w))

    from_bb = lambda a: _diag_blocks(a, gpc).transpose(0, 1, 3, 2).reshape(ns, grp)
    from_cc = lambda a: _diag_blocks(a, gpc).transpose(0, 1, 3, 2).reshape(n_groups * grp, n_state)
    d_lr, d_li, d_dt_e, d_br, d_bi = _disc_bwd(
        lr, li, log_dt_e, br, bi, d_a_re.reshape(ns, 1), d_a_im.reshape(ns, 1),
        from_bb(d_bbm_re), from_bb(d_bbm_im))

    rep2d = dict(
        ssm_lam_re=(n_groups, n_state), ssm_lam_im=(n_groups, n_state), ssm_log_dt=(1, n_groups),
        ssm_b_re=(ns, grp), ssm_b_im=(ns, grp), ssm_c_re=(n_groups * grp, n_state),
        ssm_c_im=(n_groups * grp, n_state), ssm_d=(n_groups, grp), gain_conv_out=(1, dc),
        gain_ssm_out=(1, ds), norm_ffn_g=(1, d), ffn_conv_b=(1, N_DEV * ff_s), norm_final_g=(1, d))
    rep_names = list(rep2d)
    rep_grads = dict(
        ssm_lam_re=d_lr, ssm_lam_im=d_li, ssm_log_dt=d_dt_e.reshape(n_groups, n_state).sum(axis=1),
        ssm_b_re=d_br, ssm_b_im=d_bi, ssm_c_re=from_cc(d_ccm_re), ssm_c_im=from_cc(d_ccm_im),
        ssm_d=d_dskip, gain_conv_out=d_gain_c, gain_ssm_out=d_gain_s, norm_ffn_g=d_norm_ffn,
        ffn_conv_b=d_fb, norm_final_g=d_norm_final)
    rep_shapes = [rep2d[n] for n in rep_names] + [(1, 1)]
    rep_rows = _packed_rows(rep_shapes)
    rep_pack = _pack([rep_grads[n] for n in rep_names] + [loss_part], rep_rows)
    ex_rep, rep_token = _send_start("gather_small_grads_start", [rep_pack], gather=True)

    dh0, _, d_norm_mix = _rms_bwd_res("norm_mix_bwd", dh1, dhn1, h0, norm_mix_g, deps=(rep_token,))
    dw_in = _mm("proj_wgrad", hn1, dproj, dims=TN, grid=(N_DEV, d // tn_w),
                a_spec=pl.BlockSpec((t, tn_w), lambda j, i: (0, i)),
                b_spec=pl.BlockSpec((t, d_in_s), lambda j, i: (0, j)),
                o_spec=pl.BlockSpec((None, tn_w, d_in_s), lambda j, i: (j, i, 0)),
                out_shape=jax.ShapeDtypeStruct((N_DEV, d, d_in_s), BF16))
    dh0 = _from_segments(dh0)
    grad_x = dh0[n_meta:rows_used][None]
    d_meta_b = dh0[:n_meta].reshape(n_meta, N_DEV, d_meta).transpose(1, 0, 2)
    ex_in, ex_in_token = _send_start("exchange_in_start", [dw_in, d_meta_b, d_norm_mix],
                                     gather=[False, False, True])

    shard_out = {}
    (rep_parts,) = _send_wait("gather_small_grads_wait", ex_rep, (0,), ex_in_token)
    rep_sum = _sum_parts("sum_small_grads", rep_parts)
    *rep_g, loss = _unpack(rep_sum, rep_shapes)
    loss = loss.reshape(())
    as2d = lambda tree: [tree[n].reshape(rep2d[n]) for n in rep_names]
    rep_res = _adamw_many("adamw_replicated", rep_g, as2d(weights), as2d(mom_m), as2d(mom_v))
    for i, n in enumerate(rep_names):
        shard_out[n] = [r.reshape(weights[n].shape) for r in (rep_g[i], *(res[i] for res in rep_res))]

    def update(n, parts):
        sh = weights[n].shape
        two_d = lambda a: a.reshape(parts.shape[1:])
        res = _adamw_reduce("adamw_" + n, parts, two_d(weights[n]), two_d(mom_m[n]), two_d(mom_v[n]))
        shard_out[n] = [r.reshape(sh) for r in res]
        return res[0]

    (p_down,) = _send_wait("exchange_down_wait", ex_down, (0,), rep_sum)
    done = update("w_down", p_down)
    (p_up,) = _send_wait("exchange_up_wait", ex_up, (0,), done)
    done = update("w_up", p_up)
    p_out, p_glu, p_cw, p_fw = _send_wait("exchange_mix_wait", ex_mix, (0, 1, 2, 3), done)
    update("w_out", p_out)
    update("ssm_w_glu", p_glu)
    update("conv_w", p_cw)
    done = update("ffn_conv_w", p_fw)
    p_in, p_meta, p_nm = _send_wait("exchange_in_wait", ex_in, (0, 1, 2), done)
    update("w_in", p_in)
    update("meta_tokens", p_meta)
    update("norm_mix_g", p_nm)

    grads = [shard_out[n][0] for n in names]
    deltas = [shard_out[n][1] for n in names]
    new_m = [shard_out[n][2] for n in names]
    new_v = [shard_out[n][3] for n in names]
    return (loss, grad_x, *grads, *deltas, *new_m, *new_v)
```

```python
import functools
import math

import jax
import jax.numpy as jnp
from jax import lax
from jax.experimental import pallas as pl
from jax.experimental.pallas import tpu as pltpu

F32 = jnp.float32
BF16 = jnp.bfloat16
MESH = pl.DeviceIdType.MESH

N_DEV = 8
RMS_EPS = 1e-6
ADAM_LR = 0.001
ADAM_B1 = 0.9
ADAM_B2 = 0.999
ADAM_EPS = 1e-08
ADAM_WD = 0.01
ADAM_STEP = 10
ADAM_BC1 = 1.0 - ADAM_B1 ** ADAM_STEP
ADAM_BC2 = 1.0 - ADAM_B2 ** ADAM_STEP

SUBLANES = 8
LANES = 128
ROW_TILE = 128
ROW_CHUNK = 32
N_SEG = 8
HALO_ROWS = 16
SSM_CHUNK = 128
VMEM_LIMIT = 48 * 1024 * 1024

NN = ((1,), (0,))
NT = ((1,), (1,))
TN = ((0,), (0,))


def _params(**kw):
    return pltpu.CompilerParams(vmem_limit_bytes=VMEM_LIMIT, **kw)


def _dot(a, b, dims):
    return lax.dot_general(a, b, (dims, ((), ())), preferred_element_type=F32)


def _mean_sq_rsqrt(x):
    return lax.rsqrt(jnp.mean(x * x, axis=-1, keepdims=True) + RMS_EPS)


def _rms_bwd(x, r, g, dy):
    xhat = x * r
    dxh = dy * g
    dx = r * (dxh - xhat * jnp.mean(dxh * xhat, axis=-1, keepdims=True))
    return dx, dy * xhat


def _gelu(y):
    c = math.sqrt(2.0 / math.pi)
    t = jnp.tanh(c * (y + 0.044715 * y * y * y))
    return 0.5 * y * (1.0 + t), t


def _gelu_grad(y, t):
    c = math.sqrt(2.0 / math.pi)
    return 0.5 * (1.0 + t) + 0.5 * y * (1.0 - t * t) * c * (1.0 + 3.0 * 0.044715 * y * y)


def _wrap_prev_halo(halo, first):
    seg = lax.broadcasted_iota(jnp.int32, halo.shape, 0) % N_SEG
    wrapped = jnp.where(seg == 0, 0.0, pltpu.roll(halo, 1, 0))
    return jnp.where(first, wrapped, halo)


def _wrap_next_halo(halo, last):
    seg = lax.broadcasted_iota(jnp.int32, halo.shape, 0) % N_SEG
    wrapped = jnp.where(seg == N_SEG - 1, 0.0, pltpu.roll(halo, halo.shape[0] - 1, 0))
    return jnp.where(last, wrapped, halo)


def _rows_before(get, halo, r, rows, back):
    lo = r - back
    if lo >= 0:
        return get(lo, rows)
    return jnp.concatenate([halo[HALO_ROWS + lo:], get(0, rows + lo)], axis=0)


def _rows_after(get, halo, r, rows, ahead, tile_rows):
    over = r + ahead + rows - tile_rows
    if over <= 0:
        return get(r + ahead, rows)
    return jnp.concatenate([get(r + ahead, rows - over), halo[:over]], axis=0)


def _dev_index(p):
    return 4 * p[0] + 2 * p[1] + p[2]


def _allgather(name, shards, deps=()):
    n = len(shards)

    def body(*refs):
        ins, outs = refs[:n], refs[n:2 * n]
        send_sems, recv_sems, local_sems = refs[2 * n:]
        x, y, c = lax.axis_index("x"), lax.axis_index("y"), lax.axis_index("c")
        me, sibling = (x, y, c), (x, y, 1 - c)
        chips = [(1 - x, y), (x, 1 - y), (1 - x, 1 - y)]

        def copy(a, k, block, to, src=None):
            dst = outs[a].at[_dev_index(block)]
            return pltpu.make_async_remote_copy(
                src_ref=dst if src is None else src, dst_ref=dst,
                send_sem=send_sems.at[a, k], recv_sem=recv_sems.at[a, k],
                device_id=to, device_id_type=MESH)

        mine = [pltpu.make_async_copy(ins[a], outs[a].at[_dev_index(me)], local_sems.at[a])
                for a in range(n)]
        for cp in mine:
            cp.start()
        first = []
        for a in range(n):
            first.append(copy(a, 0, me, sibling, src=ins[a]))
            for j, chip in enumerate(chips):
                first.append(copy(a, 1 + j, me, (*chip, c), src=ins[a]))
        for cp in first:
            cp.start()
        passed = []
        for j, chip in enumerate(chips):
            for a in range(n):
                copy(a, 1 + j, (*chip, c), me).wait_recv()
                fwd = copy(a, 4 + j, (*chip, c), sibling)
                fwd.start()
                passed.append(fwd)
        for a in range(n):
            copy(a, 0, sibling, me).wait_recv()
            for j, chip in enumerate(chips):
                copy(a, 4 + j, (*chip, 1 - c), me).wait_recv()
        for cp in first + passed:
            cp.wait_send()
        for cp in mine:
            cp.wait()

    any_spec = pl.BlockSpec(memory_space=pl.ANY)
    return pl.pallas_call(
        _ignoring_deps(body, n, deps), name=name,
        out_shape=[jax.ShapeDtypeStruct((N_DEV,) + s.shape, s.dtype) for s in shards],
        in_specs=[any_spec] * (n + len(deps)), out_specs=[any_spec] * n,
        scratch_shapes=[pltpu.SemaphoreType.DMA((n, 7)), pltpu.SemaphoreType.DMA((n, 7)),
                        pltpu.SemaphoreType.DMA((n,))],
    )(*shards, *deps)


HBM_SPEC = pl.BlockSpec(memory_space=pltpu.HBM)
SEM_SPEC = pl.BlockSpec(memory_space=pltpu.SEMAPHORE)
ANY_SPEC = pl.BlockSpec(memory_space=pl.ANY)
DATAFLOW = pltpu.SideEffectType.DATAFLOW_SIDE_EFFECTING


def _ignoring_deps(body, n_in, deps):
    n_dep = len(deps)

    def wrapped(*refs):
        return body(*refs[:n_in], *refs[n_in + n_dep:])

    return wrapped


def _my_position():
    x, y, c = lax.axis_index("x"), lax.axis_index("y"), lax.axis_index("c")
    return (x, y, c)


def _peer(me, k):
    return tuple((1 - v) if (k >> s) & 1 else v for v, s in zip(me, (2, 1, 0)))


def _split_copies(src_refs, land_refs, send_sems, recv_sems, gather):
    me = _my_position()
    copies = []
    for a, (src, land) in enumerate(zip(src_refs, land_refs)):
        for k in range(1, N_DEV):
            peer = _peer(me, k)
            copies.append(pltpu.make_async_remote_copy(
                src_ref=src if gather[a] else src.at[_dev_index(peer)], dst_ref=land.at[_dev_index(me)],
                send_sem=send_sems[a].at[k - 1], recv_sem=recv_sems[a].at[k - 1],
                device_id=peer, device_id_type=MESH))
    return copies


def _own_slot(block, like_shape):
    me = _dev_index(_my_position())
    return lax.dynamic_update_index_in_dim(lax.empty(like_shape, block.dtype), block, me, 0)


def _send_start(name, srcs, gather):
    n = len(srcs)
    me = _dev_index(_my_position())
    gather = [gather] * n if isinstance(gather, bool) else list(gather)
    lands = [_own_slot(s, (N_DEV,) + s.shape) if g else
             _own_slot(lax.dynamic_index_in_dim(s, me, 0, keepdims=False), s.shape)
             for s, g in zip(srcs, gather)]

    def body(*refs):
        src_refs, land_refs = refs[:n], refs[n:2 * n]
        send_sems, recv_sems = refs[2 * n:3 * n], refs[3 * n:4 * n]
        token = refs[-1]
        for cp in _split_copies(src_refs, land_refs, send_sems, recv_sems, gather):
            cp.start()
        token[...] = jnp.zeros_like(token)

    hbm = lambda a: pltpu.HBM(a.shape, a.dtype)
    sems = [pltpu.SemaphoreType.DMA((N_DEV - 1,))] * n
    outs = pl.pallas_call(
        body, name=name,
        out_shape=(*sems, *sems, *[hbm(s) for s in srcs], *[hbm(l) for l in lands],
                   jax.ShapeDtypeStruct((SUBLANES, LANES), F32)),
        in_specs=[HBM_SPEC] * (2 * n),
        out_specs=(*[SEM_SPEC] * (2 * n), *[HBM_SPEC] * (2 * n), pl.BlockSpec(memory_space=pltpu.VMEM)),
        input_output_aliases={i: 2 * n + i for i in range(2 * n)},
        compiler_params=pltpu.CompilerParams(has_side_effects=DATAFLOW),
    )(*[pltpu.with_memory_space_constraint(a, pltpu.HBM) for a in (*srcs, *lands)])
    state = dict(send=outs[:n], recv=outs[n:2 * n], srcs=outs[2 * n:3 * n], lands=outs[3 * n:4 * n],
                 gather=gather)
    return state, outs[-1]


def _send_wait(name, state, which, after):
    n = len(which)
    pick = lambda key: [state[key][i] for i in which]
    gather = pick("gather")

    def body(*refs):
        src_refs, land_refs = refs[:n], refs[n:2 * n]
        send_sems, recv_sems = refs[2 * n:3 * n], refs[3 * n:4 * n]
        for cp in _split_copies(src_refs, land_refs, send_sems, recv_sems, gather):
            cp.wait_send()
            cp.wait_recv()

    srcs, lands = pick("srcs"), pick("lands")
    hbm = lambda a: pltpu.HBM(a.shape, a.dtype)
    outs = pl.pallas_call(
        body, name=name,
        out_shape=(*[hbm(s) for s in srcs], *[hbm(l) for l in lands]),
        in_specs=[*[HBM_SPEC] * (2 * n), *[SEM_SPEC] * (2 * n), ANY_SPEC],
        out_specs=tuple([HBM_SPEC] * (2 * n)),
        input_output_aliases={i: i for i in range(2 * n)},
        compiler_params=pltpu.CompilerParams(has_side_effects=DATAFLOW),
    )(*srcs, *lands, *pick("send"), *pick("recv"), after)
    return outs[n:]


def _two_level_copies(src_refs, land_refs, sems1, sems2):
    x, y, c = _my_position()
    me, sibling = (x, y, c), (x, y, 1 - c)
    chips = [(1 - x, y), (x, 1 - y), (1 - x, 1 - y)]
    stage1, stage2 = [], []
    for a, land in enumerate(land_refs):
        def copy(src, block, to, sems, k):
            return pltpu.make_async_remote_copy(
                src_ref=src, dst_ref=land.at[_dev_index(block)], send_sem=sems[0][a].at[k],
                recv_sem=sems[1][a].at[k], device_id=to, device_id_type=MESH)
        src = src_refs[a] if src_refs is not None else land.at[_dev_index(me)]
        stage1.append([copy(src, me, sibling, sems1, 0)] +
                      [copy(src, me, (*chip, c), sems1, 1 + j) for j, chip in enumerate(chips)])
        if sems2 is not None:
            stage2.append([copy(land.at[_dev_index((*chip, c))], (*chip, c), sibling, sems2, j)
                           for j, chip in enumerate(chips)])
    return stage1, stage2


def _gather2_start(name, shards):
    n = len(shards)
    lands = [_own_slot(s, (N_DEV,) + s.shape) for s in shards]

    def body(*refs):
        src_refs, land_refs = refs[:n], refs[n:2 * n]
        sems1 = (refs[2 * n:3 * n], refs[3 * n:4 * n])
        stage1, _ = _two_level_copies(src_refs, land_refs, sems1, None)
        for copies in stage1:
            for cp in copies:
                cp.start()
        refs[-1][...] = jnp.zeros_like(refs[-1])

    hbm = lambda a: pltpu.HBM(a.shape, a.dtype)
    sems = [pltpu.SemaphoreType.DMA((4,))] * n
    outs = pl.pallas_call(
        body, name=name,
        out_shape=(*sems, *sems, *[hbm(s) for s in shards], *[hbm(l) for l in lands],
                   jax.ShapeDtypeStruct((SUBLANES, LANES), F32)),
        in_specs=[HBM_SPEC] * (2 * n),
        out_specs=(*[SEM_SPEC] * (2 * n), *[HBM_SPEC] * (2 * n), pl.BlockSpec(memory_space=pltpu.VMEM)),
        input_output_aliases={i: 2 * n + i for i in range(2 * n)},
        compiler_params=pltpu.CompilerParams(has_side_effects=DATAFLOW),
    )(*[pltpu.with_memory_space_constraint(a, pltpu.HBM) for a in (*shards, *lands)])
    state = dict(send1=list(outs[:n]), recv1=list(outs[n:2 * n]), srcs=list(outs[2 * n:3 * n]),
                 lands=list(outs[3 * n:4 * n]), send2={}, recv2={})
    return state, outs[-1]


def _gather2_forward(name, state, which, after):
    n = len(which)
    pick = lambda key: [state[key][i] for i in which]

    def body(*refs):
        land_refs, recv1 = refs[:n], refs[n:2 * n]
        outs = refs[2 * n + 1:]
        sems2 = (outs[:n], outs[n:2 * n])
        stage1, stage2 = _two_level_copies(None, land_refs, (recv1, recv1), sems2)
        for a in range(n):
            for j in range(3):
                stage1[a][1 + j].wait_recv()
                stage2[a][j].start()
        outs[-1][...] = jnp.zeros_like(outs[-1])

    lands = pick("lands")
    sems = [pltpu.SemaphoreType.DMA((3,))] * n
    outs = pl.pallas_call(
        body, name=name,
        out_shape=(*sems, *sems, *[pltpu.HBM(l.shape, l.dtype) for l in lands],
                   jax.ShapeDtypeStruct((SUBLANES, LANES), F32)),
        in_specs=[*[HBM_SPEC] * n, *[SEM_SPEC] * n, ANY_SPEC],
        out_specs=(*[SEM_SPEC] * (2 * n), *[HBM_SPEC] * n, pl.BlockSpec(memory_space=pltpu.VMEM)),
        input_output_aliases={i: 2 * n + i for i in range(n)},
        compiler_params=pltpu.CompilerParams(has_side_effects=DATAFLOW),
    )(*lands, *pick("recv1"), after)
    for idx, i in enumerate(which):
        state["send2"][i], state["recv2"][i] = outs[idx], outs[n + idx]
        state["lands"][i] = outs[2 * n + idx]
    return outs[-1]


def _gather2_wait(name, state, which, after):
    n = len(which)
    pick = lambda key: [state[key][i] for i in which]

    def body(*refs):
        src_refs, land_refs = refs[:n], refs[n:2 * n]
        sems1 = (refs[2 * n:3 * n], refs[3 * n:4 * n])
        sems2 = (refs[4 * n:5 * n], refs[5 * n:6 * n])
        stage1, stage2 = _two_level_copies(src_refs, land_refs, sems1, sems2)
        for a in range(n):
            for cp in stage1[a]:
                cp.wait_send()
            stage1[a][0].wait_recv()
            for cp in stage2[a]:
                cp.wait_send()
                cp.wait_recv()

    srcs, lands = pick("srcs"), pick("lands")
    hbm = lambda a: pltpu.HBM(a.shape, a.dtype)
    outs = pl.pallas_call(
        body, name=name,
        out_shape=(*[hbm(s) for s in srcs], *[hbm(l) for l in lands]),
        in_specs=[*[HBM_SPEC] * (2 * n), *[SEM_SPEC] * (4 * n), ANY_SPEC],
        out_specs=tuple([HBM_SPEC] * (2 * n)),
        input_output_aliases={i: i for i in range(2 * n)},
        compiler_params=pltpu.CompilerParams(has_side_effects=DATAFLOW),
    )(*srcs, *lands, *pick("send1"), *pick("recv1"), *pick("send2"), *pick("recv2"), after)
    return outs[n:]


def _mm(name, a, b, *, dims, grid, a_spec, b_spec, o_spec, out_shape, acc_shape=None,
        res=None, res_spec=None):
    n_red = grid[-1] if acc_shape is not None else 1
    red_axis = len(grid) - 1

    def body(*refs):
        a_ref, b_ref = refs[0], refs[1]
        r_ref = refs[2] if res is not None else None
        o_ref = refs[3] if res is not None else refs[2]
        part = _dot(a_ref[...], b_ref[...], dims)
        if acc_shape is None:
            if r_ref is not None:
                part = part + r_ref[...]
            o_ref[...] = part.astype(o_ref.dtype)
            return
        acc_ref = refs[-1]
        k = pl.program_id(red_axis)

        @pl.when(k == 0)
        def _():
            acc_ref[...] = part

        @pl.when(k > 0)
        def _():
            acc_ref[...] += part

        @pl.when(k == n_red - 1)
        def _():
            total = acc_ref[...]
            if r_ref is not None:
                total = total + r_ref[...]
            o_ref[...] = total.astype(o_ref.dtype)

    ins, in_specs = [a, b], [a_spec, b_spec]
    if res is not None:
        ins.append(res)
        in_specs.append(res_spec)
    return pl.pallas_call(
        body, name=name, grid=grid, in_specs=in_specs, out_specs=o_spec, out_shape=out_shape,
        scratch_shapes=[pltpu.VMEM(acc_shape, F32)] if acc_shape is not None else [],
        compiler_params=_params(),
    )(*ins)


def _rms_fwd(name, h, g):
    t, d = h.shape

    def body(h_ref, g_ref, o_ref):
        x = h_ref[...]
        o_ref[...] = (x * _mean_sq_rsqrt(x) * g_ref[...]).astype(BF16)

    return pl.pallas_call(
        body, name=name, grid=(t // ROW_TILE,),
        in_specs=[pl.BlockSpec((ROW_TILE, d), lambda i: (i, 0)), pl.BlockSpec((1, d), lambda i: (0, 0))],
        out_specs=pl.BlockSpec((ROW_TILE, d), lambda i: (i, 0)),
        out_shape=jax.ShapeDtypeStruct((t, d), BF16), compiler_params=_params(),
    )(h, g)


def _rms_bwd_res(name, dres, dhn, h, g, deps=()):
    t, d = h.shape

    def body(dres_ref, dhn_ref, h_ref, g_ref, dh_ref, dhb_ref, dg_ref):
        x = h_ref[...]
        dx, dgt = _rms_bwd(x, _mean_sq_rsqrt(x), g_ref[...], dhn_ref[...])
        dh = dres_ref[...] + dx
        dh_ref[...] = dh
        dhb_ref[...] = dh.astype(BF16)

        @pl.when(pl.program_id(0) == 0)
        def _():
            dg_ref[...] = jnp.zeros_like(dg_ref)

        dg_ref[...] += jnp.sum(dgt, axis=0, keepdims=True)

    row = pl.BlockSpec((ROW_TILE, d), lambda i: (i, 0))
    vec = pl.BlockSpec((1, d), lambda i: (0, 0))
    return pl.pallas_call(
        _ignoring_deps(body, 4, deps), name=name, grid=(t // ROW_TILE,),
        in_specs=[row, row, row, vec] + [ANY_SPEC] * len(deps), out_specs=[row, row, vec],
        out_shape=[jax.ShapeDtypeStruct((t, d), F32), jax.ShapeDtypeStruct((t, d), BF16),
                   jax.ShapeDtypeStruct((1, d), F32)],
        compiler_params=_params(),
    )(dres, dhn, h, g, *deps)


def _loss_bwd(h2, target, g, row_lo, row_hi):
    t, d = h2.shape

    def body(h_ref, tg_ref, g_ref, dh_ref, dhb_ref, loss_ref, dg_ref):
        i = pl.program_id(0)
        x = h_ref[...]
        r = _mean_sq_rsqrt(x)
        gv = g_ref[...]
        y = x * r * gv
        rows = i * ROW_TILE + lax.broadcasted_iota(jnp.int32, (ROW_TILE, 1), 0)
        time = (rows % N_SEG) * (t // N_SEG) + rows // N_SEG
        valid = jnp.logical_and(time >= row_lo, time < row_hi)
        err = jnp.where(valid, y - tg_ref[...], 0.0)
        dy = err * (1.0 / d)
        dx, dgt = _rms_bwd(x, r, gv, dy)
        dh_ref[...] = dx
        dhb_ref[...] = dx.astype(BF16)

        @pl.when(i == 0)
        def _():
            loss_ref[...] = jnp.zeros_like(loss_ref)
            dg_ref[...] = jnp.zeros_like(dg_ref)

        row_loss = jnp.mean(err * err, axis=-1, keepdims=True)
        loss_ref[...] += 0.5 * jnp.sum(row_loss, axis=0, keepdims=True)
        dg_ref[...] += jnp.sum(dgt, axis=0, keepdims=True)

    row = pl.BlockSpec((ROW_TILE, d), lambda i: (i, 0))
    vec = pl.BlockSpec((1, d), lambda i: (0, 0))
    return pl.pallas_call(
        body, name="loss_bwd", grid=(t // ROW_TILE,), in_specs=[row, row, vec],
        out_specs=[row, row, pl.BlockSpec((1, 1), lambda i: (0, 0)), vec],
        out_shape=[jax.ShapeDtypeStruct((t, d), F32), jax.ShapeDtypeStruct((t, d), BF16),
                   jax.ShapeDtypeStruct((1, 1), F32), jax.ShapeDtypeStruct((1, d), F32)],
        compiler_params=_params(),
    )(h2, target, g)


def _prev_halo(i, t):
    return jnp.where(i == 0, t // HALO_ROWS - 1, i * (ROW_TILE // HALO_ROWS) - 1)


def _next_halo(i, t):
    return jnp.where(i == t // ROW_TILE - 1, 0, (i + 1) * (ROW_TILE // HALO_ROWS))


def _causal_taps(cur, halo, first):
    rows = cur.shape[0]
    ext = jnp.concatenate([_wrap_prev_halo(halo, first), cur], axis=0)
    return ext[:rows], ext[N_SEG:N_SEG + rows]


def _anticausal_taps(cur, halo, last):
    rows = cur.shape[0]
    ext = jnp.concatenate([cur, _wrap_next_halo(halo, last)], axis=0)
    return ext[N_SEG:N_SEG + rows], ext[2 * N_SEG:2 * N_SEG + rows]


def _mix_fwd(proj, y, w_glu, conv_w, gain_c, gain_s):
    t = proj.shape[0]
    dc = conv_w.shape[1]
    ds = y.shape[1]

    def body(p_ref, halo_ref, y_ref, wg_ref, cw_ref, gc_ref, gs_ref, mixed_ref, z_ref):
        i = pl.program_id(0)
        p = p_ref[...]
        b, c, v = p[:, :dc], p[:, dc:2 * dc], p[:, 2 * dc:3 * dc]
        cv = c * v
        hp = halo_ref[...]
        x2, x1 = _causal_taps(cv, hp[:, dc:2 * dc] * hp[:, 2 * dc:3 * dc], i == 0)
        cw = cw_ref[...]
        conv = cw[0:1] * x2 + cw[1:2] * x1 + cw[2:3] * cv
        co = b * conv
        mixed_ref[:, :dc] = (co * _mean_sq_rsqrt(co) * gc_ref[...]).astype(BF16)
        g, _ = _gelu(y_ref[...])
        z = _dot(g.astype(BF16), wg_ref[...], NN)
        z_ref[...] = z
        so = g * jax.nn.sigmoid(z)
        mixed_ref[:, dc:] = (so * _mean_sq_rsqrt(so) * gs_ref[...]).astype(BF16)

    const = lambda i: (0, 0)
    return pl.pallas_call(
        body, name="mix_fwd", grid=(t // ROW_TILE,),
        in_specs=[pl.BlockSpec((ROW_TILE, 3 * dc), lambda i: (i, 0)),
                  pl.BlockSpec((HALO_ROWS, 3 * dc), lambda i: (_prev_halo(i, t), 0)),
                  pl.BlockSpec((ROW_TILE, ds), lambda i: (i, 0)),
                  pl.BlockSpec((ds, ds), const), pl.BlockSpec(conv_w.shape, const),
                  pl.BlockSpec((1, dc), const), pl.BlockSpec((1, ds), const)],
        out_specs=[pl.BlockSpec((ROW_TILE, dc + ds), lambda i: (i, 0)),
                   pl.BlockSpec((ROW_TILE, ds), lambda i: (i, 0))],
        out_shape=[jax.ShapeDtypeStruct((t, dc + ds), BF16), jax.ShapeDtypeStruct((t, ds), F32)],
        compiler_params=_params(),
    )(proj, proj, y, w_glu, conv_w, gain_c, gain_s)


def _mix_bwd1(proj, y, z, dmixed, w_glu, conv_w, gain_c, gain_s):
    t = proj.shape[0]
    dc = conv_w.shape[1]
    ds = y.shape[1]

    def body(p_ref, halo_ref, y_ref, z_ref, dm_ref, wg_ref, cw_ref, gc_ref, gs_ref,
             db_ref, dconv_ref, dy_ref, dwg_ref, dcw_ref, dgc_ref, dgs_ref):
        i = pl.program_id(0)

        @pl.when(i == 0)
        def _():
            dwg_ref[...] = jnp.zeros_like(dwg_ref)
            dcw_ref[...] = jnp.zeros_like(dcw_ref)
            dgc_ref[...] = jnp.zeros_like(dgc_ref)
            dgs_ref[...] = jnp.zeros_like(dgs_ref)

        p = p_ref[...]
        b, c, v = p[:, :dc], p[:, dc:2 * dc], p[:, 2 * dc:3 * dc]
        cv = c * v
        hp = halo_ref[...]
        x2, x1 = _causal_taps(cv, hp[:, dc:2 * dc] * hp[:, 2 * dc:3 * dc], i == 0)
        cw = cw_ref[...]
        conv = cw[0:1] * x2 + cw[1:2] * x1 + cw[2:3] * cv
        co = b * conv
        dm = dm_ref[...]
        dco, dgt = _rms_bwd(co, _mean_sq_rsqrt(co), gc_ref[...], dm[:, :dc])
        dgc_ref[...] += jnp.sum(dgt, axis=0, keepdims=True)
        db_ref[...] = (dco * conv).astype(BF16)
        dconv = dco * b
        dconv_ref[...] = dconv
        dcw_ref[0:1, :] += jnp.sum(dconv * x2, axis=0, keepdims=True)
        dcw_ref[1:2, :] += jnp.sum(dconv * x1, axis=0, keepdims=True)
        dcw_ref[2:3, :] += jnp.sum(dconv * cv, axis=0, keepdims=True)

        yv = y_ref[...]
        g, th = _gelu(yv)
        sg = jax.nn.sigmoid(z_ref[...])
        so = g * sg
        dso, dgt = _rms_bwd(so, _mean_sq_rsqrt(so), gs_ref[...], dm[:, dc:])
        dgs_ref[...] += jnp.sum(dgt, axis=0, keepdims=True)
        dz = (dso * g * sg * (1.0 - sg)).astype(BF16)
        dg = dso * sg + _dot(dz, wg_ref[...], NT)
        dwg_ref[...] += _dot(g.astype(BF16), dz, TN)
        dy_ref[...] = dg * _gelu_grad(yv, th)

    const = lambda i: (0, 0)
    row = lambda w: pl.BlockSpec((ROW_TILE, w), lambda i: (i, 0))
    return pl.pallas_call(
        body, name="mix_bwd1", grid=(t // ROW_TILE,),
        in_specs=[row(3 * dc), pl.BlockSpec((HALO_ROWS, 3 * dc), lambda i: (_prev_halo(i, t), 0)),
                  row(ds), row(ds), row(dc + ds), pl.BlockSpec((ds, ds), const),
                  pl.BlockSpec(conv_w.shape, const), pl.BlockSpec((1, dc), const),
                  pl.BlockSpec((1, ds), const)],
        out_specs=[row(dc), row(dc), row(ds), pl.BlockSpec((ds, ds), const),
                   pl.BlockSpec(conv_w.shape, const), pl.BlockSpec((1, dc), const),
                   pl.BlockSpec((1, ds), const)],
        out_shape=[jax.ShapeDtypeStruct((t, dc), BF16), jax.ShapeDtypeStruct((t, dc), F32),
                   jax.ShapeDtypeStruct((t, ds), F32), jax.ShapeDtypeStruct((ds, ds), F32),
                   jax.ShapeDtypeStruct(conv_w.shape, F32), jax.ShapeDtypeStruct((1, dc), F32),
                   jax.ShapeDtypeStruct((1, ds), F32)],
        compiler_params=_params(),
    )(proj, proj, y, z, dmixed, w_glu, conv_w, gain_c, gain_s)


def _mix_bwd2(proj, dconv, conv_w, deps=()):
    t = proj.shape[0]
    dc = conv_w.shape[1]
    n_tiles = t // ROW_TILE

    def body(c_ref, v_ref, d_ref, halo_ref, cw_ref, o_ref):
        i = pl.program_id(0)
        d = d_ref[...]
        u1, u2 = _anticausal_taps(d, halo_ref[...], i == n_tiles - 1)
        cw = cw_ref[...]
        dcv = cw[2:3] * d + cw[1:2] * u1 + cw[0:1] * u2
        o_ref[:, :dc] = (dcv * v_ref[...]).astype(BF16)
        o_ref[:, dc:] = (dcv * c_ref[...]).astype(BF16)

    return pl.pallas_call(
        _ignoring_deps(body, 5, deps), name="mix_bwd2", grid=(n_tiles,),
        in_specs=[pl.BlockSpec((ROW_TILE, dc), lambda i: (i, 1)),
                  pl.BlockSpec((ROW_TILE, dc), lambda i: (i, 2)),
                  pl.BlockSpec((ROW_TILE, dc), lambda i: (i, 0)),
                  pl.BlockSpec((HALO_ROWS, dc), lambda i: (_next_halo(i, t), 0)),
                  pl.BlockSpec(conv_w.shape, lambda i: (0, 0))] + [ANY_SPEC] * len(deps),
        out_specs=pl.BlockSpec((ROW_TILE, 2 * dc), lambda i: (i, 0)),
        out_shape=jax.ShapeDtypeStruct((t, 2 * dc), BF16), compiler_params=_params(),
    )(proj, proj, dconv, dconv, conv_w, *deps)


def _ffn_taps(up_ref, halo, fw_ref, fb_ref, q, r):
    get = lambda s, n: up_ref[q, pl.ds(s, n), :]
    x0 = get(r, ROW_CHUNK)
    x1 = _rows_before(get, halo, r, ROW_CHUNK, N_SEG)
    x2 = _rows_before(get, halo, r, ROW_CHUNK, 2 * N_SEG)
    w = fw_ref[q]
    return w[0:1] * x2 + w[1:2] * x1 + w[2:3] * x0 + fb_ref[q], x2, x1, x0


def _ffn_fwd(up_pre, fw, fb):
    nb, t, w = up_pre.shape
    half = nb // 2

    def body(up_ref, halo_ref, fw_ref, fb_ref, act_ref):
        first = pl.program_id(0) == 0
        for q in range(half):
            halo_a = _wrap_prev_halo(halo_ref[q], first)
            halo_v = _wrap_prev_halo(halo_ref[q + half], first)
            for r in range(0, ROW_TILE, ROW_CHUNK):
                a = _ffn_taps(up_ref, halo_a, fw_ref, fb_ref, q, r)[0]
                val = _ffn_taps(up_ref, halo_v, fw_ref, fb_ref, q + half, r)[0]
                act_ref[q, pl.ds(r, ROW_CHUNK), :] = (a * jax.nn.sigmoid(a) * val).astype(BF16)

    return pl.pallas_call(
        body, name="ffn_fwd", grid=(t // ROW_TILE,),
        in_specs=[pl.BlockSpec((nb, ROW_TILE, w), lambda i: (0, i, 0)),
                  pl.BlockSpec((nb, HALO_ROWS, w), lambda i: (0, _prev_halo(i, t), 0)),
                  pl.BlockSpec(fw.shape, lambda i: (0, 0, 0)),
                  pl.BlockSpec(fb.shape, lambda i: (0, 0, 0))],
        out_specs=pl.BlockSpec((half, ROW_TILE, w), lambda i: (0, i, 0)),
        out_shape=jax.ShapeDtypeStruct((half, t, w), BF16), compiler_params=_params(),
    )(up_pre, up_pre, fw, fb)


def _ffn_bwd1(up_pre, dact, fw, fb, deps=()):
    nb, t, w = up_pre.shape
    half = nb // 2

    def body(up_ref, halo_ref, dact_ref, fw_ref, fb_ref, dup_ref, dfw_ref, dfb_ref):
        first = pl.program_id(0) == 0

        @pl.when(first)
        def _():
            dfw_ref[...] = jnp.zeros_like(dfw_ref)
            dfb_ref[...] = jnp.zeros_like(dfb_ref)

        for q in range(half):
            halos = {q: _wrap_prev_halo(halo_ref[q], first),
                     q + half: _wrap_prev_halo(halo_ref[q + half], first)}
            sums = {blk: [jnp.zeros((SUBLANES, w), F32)] * 4 for blk in halos}
            fold = lambda v: sum(v[s:s + SUBLANES] for s in range(0, ROW_CHUNK, SUBLANES))
            for r in range(0, ROW_TILE, ROW_CHUNK):
                taps_a = _ffn_taps(up_ref, halos[q], fw_ref, fb_ref, q, r)
                taps_v = _ffn_taps(up_ref, halos[q + half], fw_ref, fb_ref, q + half, r)
                a, val = taps_a[0], taps_v[0]
                da_ct = dact_ref[q, pl.ds(r, ROW_CHUNK), :]
                sg = jax.nn.sigmoid(a)
                da = da_ct * val * sg * (1.0 + a * (1.0 - sg))
                dval = da_ct * a * sg
                for blk, dv, taps in ((q, da, taps_a), (q + half, dval, taps_v)):
                    dup_ref[blk, pl.ds(r, ROW_CHUNK), :] = dv
                    terms = (dv, dv * taps[1], dv * taps[2], dv * taps[3])
                    sums[blk] = [s + fold(v) for s, v in zip(sums[blk], terms)]
            for blk, parts in sums.items():
                s_b, s_w0, s_w1, s_w2 = (jnp.sum(p, axis=0, keepdims=True) for p in parts)
                dfb_ref[blk] += s_b
                for k, s_w in enumerate((s_w0, s_w1, s_w2)):
                    dfw_ref[blk, k:k + 1, :] += s_w

    return pl.pallas_call(
        _ignoring_deps(body, 5, deps), name="ffn_bwd1", grid=(t // ROW_TILE,),
        in_specs=[pl.BlockSpec((nb, ROW_TILE, w), lambda i: (0, i, 0)),
                  pl.BlockSpec((nb, HALO_ROWS, w), lambda i: (0, _prev_halo(i, t), 0)),
                  pl.BlockSpec((half, ROW_TILE, w), lambda i: (0, i, 0)),
                  pl.BlockSpec(fw.shape, lambda i: (0, 0, 0)),
                  pl.BlockSpec(fb.shape, lambda i: (0, 0, 0))] + [ANY_SPEC] * len(deps),
        out_specs=[pl.BlockSpec((nb, ROW_TILE, w), lambda i: (0, i, 0)),
                   pl.BlockSpec(fw.shape, lambda i: (0, 0, 0)),
                   pl.BlockSpec(fb.shape, lambda i: (0, 0, 0))],
        out_shape=[jax.ShapeDtypeStruct((nb, t, w), F32), jax.ShapeDtypeStruct(fw.shape, F32),
                   jax.ShapeDtypeStruct(fb.shape, F32)],
        compiler_params=_params(),
    )(up_pre, up_pre, dact, fw, fb, *deps)


def _ffn_bwd2(dup, fw):
    nb, t, w = dup.shape
    n_tiles = t // ROW_TILE

    def body(d_ref, halo_ref, fw_ref, o_ref):
        last = pl.program_id(0) == n_tiles - 1
        for q in range(nb):
            halo = _wrap_next_halo(halo_ref[q], last)
            get = lambda s, n: d_ref[q, pl.ds(s, n), :]
            cw = fw_ref[q]
            for r in range(0, ROW_TILE, ROW_CHUNK):
                d = get(r, ROW_CHUNK)
                u1 = _rows_after(get, halo, r, ROW_CHUNK, N_SEG, ROW_TILE)
                u2 = _rows_after(get, halo, r, ROW_CHUNK, 2 * N_SEG, ROW_TILE)
                o_ref[q, pl.ds(r, ROW_CHUNK), :] = (cw[2:3] * d + cw[1:2] * u1 + cw[0:1] * u2).astype(BF16)

    return pl.pallas_call(
        body, name="ffn_bwd2", grid=(n_tiles,),
        in_specs=[pl.BlockSpec((nb, ROW_TILE, w), lambda i: (0, i, 0)),
                  pl.BlockSpec((nb, HALO_ROWS, w), lambda i: (0, _next_halo(i, t), 0)),
                  pl.BlockSpec(fw.shape, lambda i: (0, 0, 0))],
        out_specs=pl.BlockSpec((nb, ROW_TILE, w), lambda i: (0, i, 0)),
        out_shape=jax.ShapeDtypeStruct((nb, t, w), BF16), compiler_params=_params(),
    )(dup, dup, fw)


def _to_segments(a):
    t, c = a.shape
    return a.reshape(N_SEG, t // N_SEG, c).transpose(1, 0, 2).reshape(t, c)


def _from_segments(a):
    t, c = a.shape
    return a.reshape(t // N_SEG, N_SEG, c).transpose(1, 0, 2).reshape(t, c)


def _cmul(ar, ai, br, bi):
    return ar * br - ai * bi, ar * bi + ai * br


def _segment_carries(pr, pi, fr, fi, forward):
    row = lax.broadcasted_iota(jnp.int32, fr.shape, 0)
    edge = row == (0 if forward else N_SEG - 1)
    shift = 1 if forward else N_SEG - 1
    sr, si = jnp.zeros_like(fr), jnp.zeros_like(fi)
    for _ in range(N_SEG - 1):
        tr, ti = _cmul(pr, pi, sr, si)
        sr = jnp.where(edge, 0.0, pltpu.roll(tr + fr, shift, 0))
        si = jnp.where(edge, 0.0, pltpu.roll(ti + fi, shift, 0))
    return sr, si


def _rows(i):
    return pl.ds(pl.multiple_of(i * SUBLANES, SUBLANES), SUBLANES)


def _s5_fwd(proj, u_col, bb_re, bb_im, a_re, a_im, cc_re, cc_im, d_skip):
    t = proj.shape[0]
    nch, _, cs = bb_re.shape
    ds = nch * SSM_CHUNK
    u_blk = u_col // SSM_CHUNK
    steps = t // N_SEG

    def body(u_ref, bbr_ref, bbi_ref, ar_ref, ai_ref, ccr_ref, cci_ref, d_ref, sr_ref, si_ref, y_ref):
        ub = u_ref[...].astype(BF16)
        sr_ref[...] = _dot(ub, bbr_ref[...], NN)
        si_ref[...] = _dot(ub, bbi_ref[...], NN)
        ar = jnp.broadcast_to(ar_ref[...], (N_SEG, cs))
        ai = jnp.broadcast_to(ai_ref[...], (N_SEG, cs))
        zero = jnp.zeros((N_SEG, cs), F32)

        def totals(i, carry):
            sr, si, pr, pi = carry
            tr, ti = _cmul(ar, ai, sr, si)
            qr, qi = _cmul(ar, ai, pr, pi)
            return tr + sr_ref[_rows(i), :], ti + si_ref[_rows(i), :], qr, qi

        fr, fi, pr, pi = lax.fori_loop(0, steps, totals, (zero, zero, zero + 1.0, zero))
        s0r, s0i = _segment_carries(pr, pi, fr, fi, True)

        def scan(i, carry):
            tr, ti = _cmul(ar, ai, *carry)
            nr, ni = tr + sr_ref[_rows(i), :], ti + si_ref[_rows(i), :]
            sr_ref[_rows(i), :] = nr
            si_ref[_rows(i), :] = ni
            return nr, ni

        lax.fori_loop(0, steps, scan, (s0r, s0i))
        y_ref[...] = (_dot(sr_ref[...].astype(BF16), ccr_ref[...], NN)
                      - _dot(si_ref[...].astype(BF16), cci_ref[...], NN)
                      + d_ref[...] * u_ref[...])

    chunk3 = lambda r, c: pl.BlockSpec((None, r, c), lambda j: (j, 0, 0))
    return pl.pallas_call(
        body, name="s5_fwd", grid=(nch,),
        in_specs=[pl.BlockSpec((t, SSM_CHUNK), lambda j: (0, j + u_blk)),
                  chunk3(SSM_CHUNK, cs), chunk3(SSM_CHUNK, cs), chunk3(1, cs), chunk3(1, cs),
                  chunk3(cs, SSM_CHUNK), chunk3(cs, SSM_CHUNK), chunk3(1, SSM_CHUNK)],
        out_specs=[pl.BlockSpec((t, cs), lambda j: (0, j)), pl.BlockSpec((t, cs), lambda j: (0, j)),
                   pl.BlockSpec((t, SSM_CHUNK), lambda j: (0, j))],
        out_shape=[jax.ShapeDtypeStruct((t, nch * cs), F32), jax.ShapeDtypeStruct((t, nch * cs), F32),
                   jax.ShapeDtypeStruct((t, ds), F32)],
        compiler_params=_params(),
    )(proj, bb_re, bb_im, a_re, a_im, cc_re, cc_im, d_skip)


def _s5_bwd(dy, proj, u_col, s_re, s_im, bb_re, bb_im, a_re, a_im, cc_re, cc_im, d_skip):
    t, ds = dy.shape
    nch, _, cs = bb_re.shape
    u_blk = u_col // SSM_CHUNK
    steps = t // N_SEG

    def body(dy_ref, u_ref, sr_ref, si_ref, bbr_ref, bbi_ref, ar_ref, ai_ref, ccr_ref, cci_ref, d_ref,
             du_ref, dbbr_ref, dbbi_ref, dar_ref, dai_ref, dccr_ref, dcci_ref, dd_ref, gr_ref, gi_ref):
        dyv = dy_ref[...]
        dyb = dyv.astype(BF16)
        gr_ref[...] = _dot(dyb, ccr_ref[...], NT)
        gi_ref[...] = -_dot(dyb, cci_ref[...], NT)
        ar = jnp.broadcast_to(ar_ref[...], (N_SEG, cs))
        ai = -jnp.broadcast_to(ai_ref[...], (N_SEG, cs))
        zero = jnp.zeros((N_SEG, cs), F32)

        def totals(k, carry):
            i = steps - 1 - k
            gr, gi, pr, pi = carry
            tr, ti = _cmul(ar, ai, gr, gi)
            qr, qi = _cmul(ar, ai, pr, pi)
            return tr + gr_ref[_rows(i), :], ti + gi_ref[_rows(i), :], qr, qi

        fr, fi, pr, pi = lax.fori_loop(0, steps, totals, (zero, zero, zero + 1.0, zero))
        e0r, e0i = _segment_carries(pr, pi, fr, fi, False)

        def step(i, gr, gi, pr, pi, acc_r, acc_i):
            tr, ti = _cmul(ar, ai, gr, gi)
            nr, ni = tr + gr_ref[_rows(i), :], ti + gi_ref[_rows(i), :]
            gr_ref[_rows(i), :] = nr
            gi_ref[_rows(i), :] = ni
            return nr, ni, acc_r + nr * pr + ni * pi, acc_i + ni * pr - nr * pi

        def scan(k, carry):
            i = steps - 1 - k
            gr, gi, acc_r, acc_i = carry
            return step(i, gr, gi, sr_ref[_rows(i - 1), :], si_ref[_rows(i - 1), :], acc_r, acc_i)

        gr, gi, acc_r, acc_i = lax.fori_loop(0, steps - 1, scan, (e0r, e0i, zero, zero))
        row = lax.broadcasted_iota(jnp.int32, (N_SEG, cs), 0)
        last = _rows(steps - 1)
        pr = jnp.where(row == 0, 0.0, pltpu.roll(sr_ref[last, :], 1, 0))
        pi = jnp.where(row == 0, 0.0, pltpu.roll(si_ref[last, :], 1, 0))
        _, _, acc_r, acc_i = step(0, gr, gi, pr, pi, acc_r, acc_i)
        dar_ref[...] = jnp.sum(acc_r, axis=0, keepdims=True)
        dai_ref[...] = jnp.sum(acc_i, axis=0, keepdims=True)

        uv = u_ref[...]
        ub = uv.astype(BF16)
        grb = gr_ref[...].astype(BF16)
        gib = gi_ref[...].astype(BF16)
        du = d_ref[...] * dyv + _dot(grb, bbr_ref[...], NT) + _dot(gib, bbi_ref[...], NT)
        du_ref[...] = du.astype(BF16)
        dbbr_ref[...] = _dot(ub, grb, TN)
        dbbi_ref[...] = _dot(ub, gib, TN)
        dccr_ref[...] = _dot(sr_ref[...].astype(BF16), dyb, TN)
        dcci_ref[...] = -_dot(si_ref[...].astype(BF16), dyb, TN)
        dd_ref[...] = jnp.sum(dyv * uv, axis=0, keepdims=True)

    chunk3 = lambda r, c: pl.BlockSpec((None, r, c), lambda j: (j, 0, 0))
    cols = lambda w: pl.BlockSpec((t, w), lambda j: (0, j))
    return pl.pallas_call(
        body, name="s5_bwd", grid=(nch,),
        in_specs=[cols(SSM_CHUNK), pl.BlockSpec((t, SSM_CHUNK), lambda j: (0, j + u_blk)), cols(cs), cols(cs),
                  chunk3(SSM_CHUNK, cs), chunk3(SSM_CHUNK, cs), chunk3(1, cs), chunk3(1, cs),
                  chunk3(cs, SSM_CHUNK), chunk3(cs, SSM_CHUNK), chunk3(1, SSM_CHUNK)],
        out_specs=[cols(SSM_CHUNK), chunk3(SSM_CHUNK, cs), chunk3(SSM_CHUNK, cs), chunk3(1, cs),
                   chunk3(1, cs), chunk3(cs, SSM_CHUNK), chunk3(cs, SSM_CHUNK), chunk3(1, SSM_CHUNK)],
        out_shape=[jax.ShapeDtypeStruct((t, ds), BF16),
                   jax.ShapeDtypeStruct((nch, SSM_CHUNK, cs), F32), jax.ShapeDtypeStruct((nch, SSM_CHUNK, cs), F32),
                   jax.ShapeDtypeStruct((nch, 1, cs), F32), jax.ShapeDtypeStruct((nch, 1, cs), F32),
                   jax.ShapeDtypeStruct((nch, cs, SSM_CHUNK), F32), jax.ShapeDtypeStruct((nch, cs, SSM_CHUNK), F32),
                   jax.ShapeDtypeStruct((nch, 1, SSM_CHUNK), F32)],
        scratch_shapes=[pltpu.VMEM((t, cs), F32), pltpu.VMEM((t, cs), F32)],
        compiler_params=_params(),
    )(dy, proj, s_re, s_im, bb_re, bb_im, a_re, a_im, cc_re, cc_im, d_skip)


def _discretize(lr, li, log_dt, br, bi):
    dt = jnp.exp(log_dt)
    mag = jnp.exp(lr * dt)
    ang = li * dt
    a_re = mag * jnp.cos(ang)
    a_im = mag * jnp.sin(ang)
    den = lr * lr + li * li
    nr = a_re - 1.0
    f_re = (nr * lr + a_im * li) / den
    f_im = (a_im * lr - nr * li) / den
    return a_re, a_im, f_re * br - f_im * bi, f_re * bi + f_im * br


def _whole(shape):
    return pl.BlockSpec(shape, lambda: (0,) * len(shape))


def _disc_fwd(lr, li, log_dt, br, bi, deps=()):
    def body(lr_ref, li_ref, dt_ref, br_ref, bi_ref, ar_ref, ai_ref, bbr_ref, bbi_ref):
        outs = _discretize(lr_ref[...], li_ref[...], dt_ref[...], br_ref[...], bi_ref[...])
        for ref, val in zip((ar_ref, ai_ref, bbr_ref, bbi_ref), outs):
            ref[...] = val

    args = (lr, li, log_dt, br, bi)
    outs = (lr, lr, br, br)
    return pl.pallas_call(
        _ignoring_deps(body, 5, deps), name="disc_fwd",
        in_specs=[_whole(a.shape) for a in args] + [ANY_SPEC] * len(deps),
        out_specs=[_whole(a.shape) for a in outs],
        out_shape=[jax.ShapeDtypeStruct(a.shape, F32) for a in outs],
    )(*args, *deps)


def _disc_bwd(lr, li, log_dt, br, bi, dar, dai, dbbr, dbbi):
    def body(lr_ref, li_ref, dt_ref, br_ref, bi_ref, dar_ref, dai_ref, dbbr_ref, dbbi_ref,
             dlr_ref, dli_ref, ddt_ref, dbr_ref, dbi_ref):
        _, vjp = jax.vjp(_discretize, lr_ref[...], li_ref[...], dt_ref[...], br_ref[...], bi_ref[...])
        grads = vjp((dar_ref[...], dai_ref[...], dbbr_ref[...], dbbi_ref[...]))
        for ref, val in zip((dlr_ref, dli_ref, ddt_ref, dbr_ref, dbi_ref), grads):
            ref[...] = val

    args = (lr, li, log_dt, br, bi, dar, dai, dbbr, dbbi)
    outs = (lr, li, log_dt, br, bi)
    return pl.pallas_call(
        body, name="disc_bwd", in_specs=[_whole(a.shape) for a in args],
        out_specs=[_whole(a.shape) for a in outs],
        out_shape=[jax.ShapeDtypeStruct(a.shape, F32) for a in outs],
    )(*args)


def _adamw(w, g, m, v):
    m = ADAM_B1 * m + (1.0 - ADAM_B1) * g
    v = ADAM_B2 * v + (1.0 - ADAM_B2) * (g * g)
    m_hat = m / ADAM_BC1
    v_hat = v / ADAM_BC2
    delta = -ADAM_LR * (m_hat / (jnp.sqrt(v_hat) + ADAM_EPS) + ADAM_WD * w)
    return delta, m, v


def _adamw_reduce(name, parts, w, m, v):
    _, r, c = parts.shape
    tr = r
    for cand in (256, 176, 128):
        if r % cand == 0:
            tr = cand
            break

    def body(p_ref, w_ref, m_ref, v_ref, g_ref, d_ref, nm_ref, nv_ref):
        g = p_ref[0].astype(F32)
        for k in range(1, N_DEV):
            g = g + p_ref[k].astype(F32)
        delta, nm, nv = _adamw(w_ref[...], g, m_ref[...], v_ref[...])
        g_ref[...] = g
        d_ref[...] = delta
        nm_ref[...] = nm
        nv_ref[...] = nv

    blk = pl.BlockSpec((tr, c), lambda i: (i, 0))
    return pl.pallas_call(
        body, name=name, grid=(r // tr,),
        in_specs=[pl.BlockSpec((N_DEV, tr, c), lambda i: (0, i, 0)), blk, blk, blk],
        out_specs=[blk] * 4, out_shape=[jax.ShapeDtypeStruct((r, c), F32)] * 4,
        compiler_params=_params(),
    )(parts, w, m, v)


def _sum_parts(name, parts):
    _, r, c = parts.shape

    def body(p_ref, o_ref):
        g = p_ref[0]
        for k in range(1, N_DEV):
            g = g + p_ref[k]
        o_ref[...] = g

    return pl.pallas_call(
        body, name=name, in_specs=[_whole(parts.shape)], out_specs=_whole((r, c)),
        out_shape=jax.ShapeDtypeStruct((r, c), F32), compiler_params=_params(),
    )(parts)


def _adamw_many(name, grads, ws, ms, vs):
    n = len(grads)

    def body(*refs):
        ins, outs = refs[:4 * n], refs[4 * n:]
        for i in range(n):
            g, w, m, v = (ins[j * n + i][...] for j in range(4))
            for ref, val in zip((outs[i], outs[n + i], outs[2 * n + i]), _adamw(w, g, m, v)):
                ref[...] = val

    args = (*grads, *ws, *ms, *vs)
    outs = pl.pallas_call(
        body, name=name, in_specs=[_whole(a.shape) for a in args],
        out_specs=[_whole(a.shape) for a in ws] * 3,
        out_shape=[jax.ShapeDtypeStruct(a.shape, F32) for a in ws] * 3, compiler_params=_params(),
    )(*args)
    return outs[:n], outs[n:2 * n], outs[2 * n:]


def _pack(arrays, rows):
    flat = jnp.concatenate([a.reshape(-1) for a in arrays])
    return jnp.pad(flat, (0, rows * LANES - flat.shape[0])).reshape(rows, LANES)


def _unpack(packed, shapes):
    flat = packed.reshape(-1)
    out, off = [], 0
    for s in shapes:
        n = math.prod(s)
        out.append(flat[off:off + n].reshape(s))
        off += n
    return out


def _packed_rows(shapes):
    n = sum(math.prod(s) for s in shapes)
    return -(-n // (SUBLANES * LANES)) * SUBLANES


def _block_diag(x):
    j, g, r, c = x.shape
    eye = jnp.eye(g, dtype=x.dtype)
    return (x[:, :, :, None, :] * eye[None, :, None, :, None]).reshape(j, g * r, g * c)


def _diag_blocks(x, g):
    j, gr, gc = x.shape
    r, c = gr // g, gc // g
    eye = jnp.eye(g, dtype=x.dtype)
    return (x.reshape(j, g, r, g, c) * eye[None, :, None, :, None]).sum(axis=3)


def kernel(x, meta_tokens, norm_mix_g, w_in, conv_w, ssm_lam_re, ssm_lam_im, ssm_log_dt, ssm_b_re, ssm_b_im, ssm_c_re, ssm_c_im, ssm_d, ssm_w_glu, gain_conv_out, gain_ssm_out, w_out, norm_ffn_g, w_up, ffn_conv_w, ffn_conv_b, w_down, norm_final_g, loss_target, m_meta_tokens, m_norm_mix_g, m_w_in, m_conv_w, m_ssm_lam_re, m_ssm_lam_im, m_ssm_log_dt, m_ssm_b_re, m_ssm_b_im, m_ssm_c_re, m_ssm_c_im, m_ssm_d, m_ssm_w_glu, m_gain_conv_out, m_gain_ssm_out, m_w_out, m_norm_ffn_g, m_w_up, m_ffn_conv_w, m_ffn_conv_b, m_w_down, m_norm_final_g, v_meta_tokens, v_norm_mix_g, v_w_in, v_conv_w, v_ssm_lam_re, v_ssm_lam_im, v_ssm_log_dt, v_ssm_b_re, v_ssm_b_im, v_ssm_c_re, v_ssm_c_im, v_ssm_d, v_ssm_w_glu, v_gain_conv_out, v_gain_ssm_out, v_w_out, v_norm_ffn_g, v_w_up, v_ffn_conv_w, v_ffn_conv_b, v_w_down, v_norm_final_g):
    weights = dict(meta_tokens=meta_tokens, norm_mix_g=norm_mix_g, w_in=w_in, conv_w=conv_w, ssm_lam_re=ssm_lam_re, ssm_lam_im=ssm_lam_im, ssm_log_dt=ssm_log_dt, ssm_b_re=ssm_b_re, ssm_b_im=ssm_b_im, ssm_c_re=ssm_c_re, ssm_c_im=ssm_c_im, ssm_d=ssm_d, ssm_w_glu=ssm_w_glu, gain_conv_out=gain_conv_out, gain_ssm_out=gain_ssm_out, w_out=w_out, norm_ffn_g=norm_ffn_g, w_up=w_up, ffn_conv_w=ffn_conv_w, ffn_conv_b=ffn_conv_b, w_down=w_down, norm_final_g=norm_final_g)
    mom_m = dict(meta_tokens=m_meta_tokens, norm_mix_g=m_norm_mix_g, w_in=m_w_in, conv_w=m_conv_w, ssm_lam_re=m_ssm_lam_re, ssm_lam_im=m_ssm_lam_im, ssm_log_dt=m_ssm_log_dt, ssm_b_re=m_ssm_b_re, ssm_b_im=m_ssm_b_im, ssm_c_re=m_ssm_c_re, ssm_c_im=m_ssm_c_im, ssm_d=m_ssm_d, ssm_w_glu=m_ssm_w_glu, gain_conv_out=m_gain_conv_out, gain_ssm_out=m_gain_ssm_out, w_out=m_w_out, norm_ffn_g=m_norm_ffn_g, w_up=m_w_up, ffn_conv_w=m_ffn_conv_w, ffn_conv_b=m_ffn_conv_b, w_down=m_w_down, norm_final_g=m_norm_final_g)
    mom_v = dict(meta_tokens=v_meta_tokens, norm_mix_g=v_norm_mix_g, w_in=v_w_in, conv_w=v_conv_w, ssm_lam_re=v_ssm_lam_re, ssm_lam_im=v_ssm_lam_im, ssm_log_dt=v_ssm_log_dt, ssm_b_re=v_ssm_b_re, ssm_b_im=v_ssm_b_im, ssm_c_re=v_ssm_c_re, ssm_c_im=v_ssm_c_im, ssm_d=v_ssm_d, ssm_w_glu=v_ssm_w_glu, gain_conv_out=v_gain_conv_out, gain_ssm_out=v_gain_ssm_out, w_out=v_w_out, norm_ffn_g=v_norm_ffn_g, w_up=v_w_up, ffn_conv_w=v_ffn_conv_w, ffn_conv_b=v_ffn_conv_b, w_down=v_w_down, norm_final_g=v_norm_final_g)
    names = list(weights)

    n_meta, d_meta = meta_tokens.shape
    seq, d = x.shape[1], x.shape[2]
    rows_used = n_meta + seq
    t = -(-rows_used // ROW_TILE) * ROW_TILE
    d_in_s = w_in.shape[2]
    dc_s = conv_w.shape[2]
    dc = dc_s * N_DEV
    ds = ssm_w_glu.shape[2]
    n_groups, n_state, grp = ssm_b_re.shape[1:]
    ns = n_groups * n_state
    nch = ds // SSM_CHUNK
    gpc = n_groups // nch
    ff_s = w_up.shape[2]
    dn_s = w_down.shape[1]
    assert 3 * dc + ds == d_in_s * N_DEV and 2 * dn_s == ff_s and t % (N_SEG * SUBLANES) == 0

    small_shard = jnp.concatenate([meta_tokens.reshape(-1), conv_w.reshape(-1), ffn_conv_w.reshape(-1)])
    n_small = small_shard.shape[0]
    small_rows = -(-n_small // LANES)
    small_shard = jnp.pad(small_shard, (0, small_rows * LANES - n_small)).reshape(small_rows, LANES)
    ag, ag_token = _gather2_start("gather_weights_start", [
        small_shard, w_in[0].astype(BF16), ssm_w_glu[0].astype(BF16), w_out[0].astype(BF16),
        jnp.swapaxes(w_up[0], 0, 1).astype(BF16), w_down[0].astype(BF16)])
    fb = ffn_conv_b.reshape(N_DEV, 1, ff_s)

    gh = n_groups * grp
    per_h = lambda a: jnp.broadcast_to(a.reshape(n_groups, 1, -1), (n_groups, grp, n_state)).reshape(gh, n_state)
    ghp = lambda a: a.transpose(0, 1, 3, 2).reshape(gh, n_state)
    lr, li, log_dt_e = per_h(ssm_lam_re), per_h(ssm_lam_im), per_h(ssm_log_dt)
    br, bi = ghp(ssm_b_re), ghp(ssm_b_im)
    a_re, a_im, bb_re, bb_im = _disc_fwd(lr, li, log_dt_e, br, bi, deps=(ag_token,))
    cs = gpc * n_state
    chunk_row = lambda a: a.reshape(n_groups, grp, n_state)[:, 0].reshape(nch, 1, cs)
    to_bb = lambda a: _block_diag(a.reshape(nch, gpc, grp, n_state)).astype(BF16)
    to_cc = lambda a: _block_diag(a.reshape(nch, gpc, grp, n_state).transpose(0, 1, 3, 2)).astype(BF16)
    bbm_re, bbm_im = to_bb(bb_re), to_bb(bb_im)
    ccm_re, ccm_im = to_cc(ssm_c_re), to_cc(ssm_c_im)
    a_re_c, a_im_c = chunk_row(a_re), chunk_row(a_im)
    d_skip = ssm_d.reshape(nch, 1, SSM_CHUNK)

    token = _gather2_forward("gather_weights_forward_first", ag, (0, 1), bbm_im)
    (g_small,) = _gather2_wait("gather_weights_wait_small", ag, (0,), token)
    g_small = g_small.reshape(N_DEV, -1)
    o1 = n_meta * d_meta
    o2 = o1 + 3 * dc_s
    meta_full = g_small[:, :o1].reshape(N_DEV, n_meta, d_meta).transpose(1, 0, 2).reshape(n_meta, d)
    conv_w_f = g_small[:, o1:o2].reshape(N_DEV, 3, dc_s).transpose(1, 0, 2).reshape(3, dc)
    fw = g_small[:, o2:o2 + 3 * ff_s].reshape(N_DEV, 3, ff_s)
    h0 = _to_segments(jnp.concatenate([meta_full, x[0], jnp.zeros((t - rows_used, d), F32)], axis=0))
    target = _to_segments(jnp.pad(loss_target[0], ((n_meta, t - rows_used), (0, 0))))
    full_t = lambda w: pl.BlockSpec((t, w), lambda *_: (0, 0))

    hn1 = _rms_fwd("norm_mix", h0, norm_mix_g)
    (g_in,) = _gather2_wait("gather_weights_wait_in", ag, (1,), hn1)
    proj = _mm("proj", hn1, g_in, dims=NN, grid=(N_DEV,), a_spec=full_t(d),
               b_spec=pl.BlockSpec((None, d, d_in_s), lambda j: (j, 0, 0)),
               o_spec=pl.BlockSpec((t, d_in_s), lambda j: (0, j)),
               out_shape=jax.ShapeDtypeStruct((t, N_DEV * d_in_s), F32))
    s_re, s_im, y_ssm = _s5_fwd(proj, 3 * dc, bbm_re, bbm_im, a_re_c, a_im_c, ccm_re, ccm_im, d_skip)
    token = _gather2_forward("gather_weights_forward_up", ag, (2, 3, 4), y_ssm)
    g_glu, g_out = _gather2_wait("gather_weights_wait_mix", ag, (2, 3), token)
    w_out_f = g_out.reshape(-1, d)
    w_glu_f = g_glu.reshape(ds, ds)
    mixed, z_glu = _mix_fwd(proj, y_ssm, w_glu_f, conv_w_f, gain_conv_out, gain_ssm_out)
    tn_out = 256
    h1 = _mm("out_proj", mixed, w_out_f, dims=NN, grid=(d // tn_out,), a_spec=full_t(dc + ds),
             b_spec=pl.BlockSpec((dc + ds, tn_out), lambda j: (0, j)),
             o_spec=pl.BlockSpec((t, tn_out), lambda j: (0, j)),
             out_shape=jax.ShapeDtypeStruct((t, d), F32),
             res=h0, res_spec=pl.BlockSpec((t, tn_out), lambda j: (0, j)))
    hn2 = _rms_fwd("norm_ffn", h1, norm_ffn_g)
    token = _gather2_forward("gather_weights_forward_down", ag, (5,), hn2)
    (g_up,) = _gather2_wait("gather_weights_wait_up", ag, (4,), token)
    up_pre = _mm("up_proj", hn2, g_up, dims=NT, grid=(N_DEV,), a_spec=full_t(d),
                 b_spec=pl.BlockSpec((None, ff_s, d), lambda j: (j, 0, 0)),
                 o_spec=pl.BlockSpec((None, t, ff_s), lambda j: (j, 0, 0)),
                 out_shape=jax.ShapeDtypeStruct((N_DEV, t, ff_s), F32))
    act = _ffn_fwd(up_pre, fw, fb)
    nhalf = N_DEV // 2
    (g_down,) = _gather2_wait("gather_weights_wait_down", ag, (5,), act)
    w_down_f = g_down.reshape(nhalf, 2 * dn_s, d)
    h2 = _mm("down_proj", act, w_down_f, dims=NN, grid=(d // tn_out, nhalf),
             a_spec=pl.BlockSpec((None, t, ff_s), lambda j, k: (k, 0, 0)),
             b_spec=pl.BlockSpec((None, ff_s, tn_out), lambda j, k: (k, 0, j)),
             o_spec=pl.BlockSpec((t, tn_out), lambda j, k: (0, j)),
             out_shape=jax.ShapeDtypeStruct((t, d), F32), acc_shape=(t, tn_out),
             res=h1, res_spec=pl.BlockSpec((t, tn_out), lambda j, k: (0, j)))

    dh2, dh2_b, loss_part, d_norm_final = _loss_bwd(h2, target, norm_final_g.reshape(1, d), n_meta, rows_used)
    dact = _mm("down_dgrad", dh2_b, w_down_f, dims=NT, grid=(nhalf,), a_spec=full_t(d),
               b_spec=pl.BlockSpec((None, ff_s, d), lambda j: (j, 0, 0)),
               o_spec=pl.BlockSpec((None, t, ff_s), lambda j: (j, 0, 0)),
               out_shape=jax.ShapeDtypeStruct((nhalf, t, ff_s), F32))
    tn_w = 512
    dw_down = _mm("down_wgrad", act, dh2_b, dims=TN, grid=(nhalf, d // tn_w),
                  a_spec=pl.BlockSpec((None, t, ff_s), lambda j, n: (j, 0, 0)),
                  b_spec=pl.BlockSpec((t, tn_w), lambda j, n: (0, n)),
                  o_spec=pl.BlockSpec((None, ff_s, tn_w), lambda j, n: (j, 0, n)),
                  out_shape=jax.ShapeDtypeStruct((nhalf, ff_s, d), BF16))
    ex_down, token = _send_start("exchange_down_start", [dw_down.reshape(N_DEV, dn_s, d)], gather=False)
    dup, d_fw, d_fb = _ffn_bwd1(up_pre, dact, fw, fb, deps=(token,))
    dup_pre = _ffn_bwd2(dup, fw)
    dhn2 = _mm("up_dgrad", dup_pre, g_up, dims=NN, grid=(d // tn_w, N_DEV),
               a_spec=pl.BlockSpec((None, t, ff_s), lambda i, k: (k, 0, 0)),
               b_spec=pl.BlockSpec((None, ff_s, tn_w), lambda i, k: (k, 0, i)),
               o_spec=pl.BlockSpec((t, tn_w), lambda i, k: (0, i)),
               out_shape=jax.ShapeDtypeStruct((t, d), F32), acc_shape=(t, tn_w))
    dw_up = _mm("up_wgrad", dup_pre, hn2, dims=TN, grid=(N_DEV, d // tn_w),
                a_spec=pl.BlockSpec((None, t, ff_s), lambda j, i: (j, 0, 0)),
                b_spec=pl.BlockSpec((t, tn_w), lambda j, i: (0, i)),
                o_spec=pl.BlockSpec((None, ff_s, tn_w), lambda j, i: (j, 0, i)),
                out_shape=jax.ShapeDtypeStruct((N_DEV, ff_s, d), BF16))
    ex_up, token = _send_start("exchange_up_start", [dw_up], gather=False)
    dh1, dh1_b, d_norm_ffn = _rms_bwd_res("norm_ffn_bwd", dh2, dhn2, h1, norm_ffn_g, deps=(token,))
    dmixed = _mm("out_dgrad", dh1_b, w_out_f, dims=NT, grid=((dc + ds) // tn_out,), a_spec=full_t(d),
                 b_spec=pl.BlockSpec((tn_out, d), lambda i: (i, 0)),
                 o_spec=pl.BlockSpec((t, tn_out), lambda i: (0, i)),
                 out_shape=jax.ShapeDtypeStruct((t, dc + ds), F32))
    dw_out = _mm("out_wgrad", mixed, dh1_b, dims=TN, grid=((dc + ds) // tn_out,),
                 a_spec=pl.BlockSpec((t, tn_out), lambda i: (0, i)), b_spec=full_t(d),
                 o_spec=pl.BlockSpec((tn_out, d), lambda i: (i, 0)),
                 out_shape=jax.ShapeDtypeStruct((dc + ds, d), BF16))
    db_gate, dconv, dy_ssm, d_wglu, d_conv_w, d_gain_c, d_gain_s = _mix_bwd1(
        proj, y_ssm, z_glu, dmixed, w_glu_f, conv_w_f, gain_conv_out, gain_ssm_out)
    ex_mix, token = _send_start("exchange_mix_start", [
        dw_out.reshape(N_DEV, -1, d), d_wglu.astype(BF16).reshape(N_DEV, -1, ds),
        d_conv_w.reshape(3, N_DEV, dc_s).transpose(1, 0, 2), d_fw], gather=False)
    dcdv = _mix_bwd2(proj, dconv, conv_w_f, deps=(token,))
    (du, d_bbm_re, d_bbm_im, d_a_re, d_a_im, d_ccm_re, d_ccm_im, d_dskip) = _s5_bwd(
        dy_ssm, proj, 3 * dc, s_re, s_im, bbm_re, bbm_im, a_re_c, a_im_c, ccm_re, ccm_im, d_skip)
    dproj = jnp.concatenate([db_gate, dcdv, du], axis=1)
    dhn1 = _mm("proj_dgrad", dproj, g_in, dims=NT, grid=(d // tn_w, N_DEV),
               a_spec=pl.BlockSpec((t, d_in_s), lambda i, k: (0, k)),
               b_spec=pl.BlockSpec((None, tn_w, d_in_s), lambda i, k: (k, i, 0)),
               o_spec=pl.BlockSpec((t, tn_w), lambda i, k: (0, i)),
               out_shape=jax.ShapeDtypeStruct((t, d), F32), acc_shape=(t, tn_w))

    from_bb = lambda a: _diag_blocks(a, gpc).reshape(gh, n_state)
    from_cc = lambda a: _diag_blocks(a, gpc).transpose(0, 1, 3, 2).reshape(gh, n_state)
    first_h = lambda a: jnp.pad(a.reshape(n_groups, 1, n_state), ((0, 0), (0, grp - 1), (0, 0))).reshape(gh, n_state)
    over_h = lambda a: a.reshape(n_groups, grp, n_state).sum(axis=1)
    d_lr, d_li, d_dt_e, d_br, d_bi = _disc_bwd(
        lr, li, log_dt_e, br, bi, first_h(d_a_re), first_h(d_a_im), from_bb(d_bbm_re), from_bb(d_bbm_im))

    rep2d = dict(
        ssm_lam_re=(n_groups, n_state), ssm_lam_im=(n_groups, n_state), ssm_log_dt=(1, n_groups),
        ssm_b_re=(gh, n_state), ssm_b_im=(gh, n_state), ssm_c_re=(gh, n_state),
        ssm_c_im=(gh, n_state), ssm_d=(n_groups, grp), gain_conv_out=(1, dc),
        gain_ssm_out=(1, ds), norm_ffn_g=(1, d), ffn_conv_b=(1, N_DEV * ff_s), norm_final_g=(1, d))
    rep_names = list(rep2d)
    rep_grads = dict(
        ssm_lam_re=over_h(d_lr), ssm_lam_im=over_h(d_li), ssm_log_dt=over_h(d_dt_e).sum(axis=1),
        ssm_b_re=d_br, ssm_b_im=d_bi, ssm_c_re=from_cc(d_ccm_re), ssm_c_im=from_cc(d_ccm_im),
        ssm_d=d_dskip, gain_conv_out=d_gain_c, gain_ssm_out=d_gain_s, norm_ffn_g=d_norm_ffn,
        ffn_conv_b=d_fb, norm_final_g=d_norm_final)
    rep_shapes = [rep2d[n] for n in rep_names] + [(1, 1)]
    rep_rows = _packed_rows(rep_shapes)
    rep_pack = _pack([rep_grads[n] for n in rep_names] + [loss_part], rep_rows)
    ex_rep, rep_token = _send_start("gather_small_grads_start", [rep_pack], gather=True)

    dh0, _, d_norm_mix = _rms_bwd_res("norm_mix_bwd", dh1, dhn1, h0, norm_mix_g, deps=(rep_token,))
    dw_in = _mm("proj_wgrad", hn1, dproj, dims=TN, grid=(N_DEV, d // tn_w),
                a_spec=pl.BlockSpec((t, tn_w), lambda j, i: (0, i)),
                b_spec=pl.BlockSpec((t, d_in_s), lambda j, i: (0, j)),
                o_spec=pl.BlockSpec((None, tn_w, d_in_s), lambda j, i: (j, i, 0)),
                out_shape=jax.ShapeDtypeStruct((N_DEV, d, d_in_s), BF16))
    dh0 = _from_segments(dh0)
    grad_x = dh0[n_meta:rows_used][None]
    d_meta_b = dh0[:n_meta].reshape(n_meta, N_DEV, d_meta).transpose(1, 0, 2)
    ex_in, ex_in_token = _send_start("exchange_in_start", [dw_in, d_meta_b, d_norm_mix],
                                     gather=[False, False, True])

    shard_out = {}
    (rep_parts,) = _send_wait("gather_small_grads_wait", ex_rep, (0,), ex_in_token)
    rep_sum = _sum_parts("sum_small_grads", rep_parts)
    *rep_g, loss = _unpack(rep_sum, rep_shapes)
    loss = loss.reshape(())
    swapped = ("ssm_b_re", "ssm_b_im")
    to2d = lambda n, a: ghp(a) if n in swapped else a.reshape(rep2d[n])
    from2d = lambda n, a: (a.reshape(1, n_groups, grp, n_state).transpose(0, 1, 3, 2) if n in swapped
                           else a.reshape(weights[n].shape))
    as2d = lambda tree: [to2d(n, tree[n]) for n in rep_names]
    rep_res = _adamw_many("adamw_replicated", rep_g, as2d(weights), as2d(mom_m), as2d(mom_v))
    for i, n in enumerate(rep_names):
        shard_out[n] = [from2d(n, r) for r in (rep_g[i], *(res[i] for res in rep_res))]

    def update(n, parts, transposed=False):
        sh = weights[n].shape
        two_d = lambda a: (jnp.swapaxes(a[0], 0, 1) if transposed else a.reshape(parts.shape[1:]))
        res = _adamw_reduce("adamw_" + n, parts, two_d(weights[n]), two_d(mom_m[n]), two_d(mom_v[n]))
        shard_out[n] = [(jnp.swapaxes(r, 0, 1) if transposed else r).reshape(sh) for r in res]
        return res[0]

    (p_down,) = _send_wait("exchange_down_wait", ex_down, (0,), rep_sum)
    done = update("w_down", p_down)
    (p_up,) = _send_wait("exchange_up_wait", ex_up, (0,), done)
    done = update("w_up", p_up, transposed=True)
    p_out, p_glu, p_cw, p_fw = _send_wait("exchange_mix_wait", ex_mix, (0, 1, 2, 3), done)
    update("w_out", p_out)
    update("ssm_w_glu", p_glu)
    update("conv_w", p_cw)
    done = update("ffn_conv_w", p_fw)
    p_in, p_meta, p_nm = _send_wait("exchange_in_wait", ex_in, (0, 1, 2), done)
    update("w_in", p_in)
    update("meta_tokens", p_meta)
    update("norm_mix_g", p_nm)

    grads = [shard_out[n][0] for n in names]
    deltas = [shard_out[n][1] for n in names]
    new_m = [shard_out[n][2] for n in names]
    new_v = [shard_out[n][3] for n in names]
    return (loss, grad_x, *grads, *deltas, *new_m, *new_v)
```

```python
import functools
import math

import jax
import jax.numpy as jnp
from jax import lax
from jax.experimental import pallas as pl
from jax.experimental.pallas import tpu as pltpu

F32 = jnp.float32
BF16 = jnp.bfloat16
MESH = pl.DeviceIdType.MESH

N_DEV = 8
RMS_EPS = 1e-6
ADAM_LR = 0.001
ADAM_B1 = 0.9
ADAM_B2 = 0.999
ADAM_EPS = 1e-08
ADAM_WD = 0.01
ADAM_STEP = 10
ADAM_BC1 = 1.0 - ADAM_B1 ** ADAM_STEP
ADAM_BC2 = 1.0 - ADAM_B2 ** ADAM_STEP

SUBLANES = 8
LANES = 128
ROW_TILE = 128
ROW_CHUNK = 32
WIDE_TILES = (544, 256, 128)
N_SEG = 8
HALO_ROWS = 16
SSM_CHUNK = 128
VMEM_LIMIT = 48 * 1024 * 1024

NN = ((1,), (0,))
NT = ((1,), (1,))
TN = ((0,), (0,))


def _params(**kw):
    return pltpu.CompilerParams(vmem_limit_bytes=VMEM_LIMIT, **kw)


def _dot(a, b, dims):
    return lax.dot_general(a, b, (dims, ((), ())), preferred_element_type=F32)


def _mean_sq_rsqrt(x):
    return lax.rsqrt(jnp.mean(x * x, axis=-1, keepdims=True) + RMS_EPS)


def _rms_bwd(x, r, g, dy):
    xhat = x * r
    dxh = dy * g
    dx = r * (dxh - xhat * jnp.mean(dxh * xhat, axis=-1, keepdims=True))
    return dx, dy * xhat


def _gelu(y):
    c = math.sqrt(2.0 / math.pi)
    t = jnp.tanh(c * (y + 0.044715 * y * y * y))
    return 0.5 * y * (1.0 + t), t


def _gelu_grad(y, t):
    c = math.sqrt(2.0 / math.pi)
    return 0.5 * (1.0 + t) + 0.5 * y * (1.0 - t * t) * c * (1.0 + 3.0 * 0.044715 * y * y)


def _wrap_prev_halo(halo, first):
    seg = lax.broadcasted_iota(jnp.int32, halo.shape, 0) % N_SEG
    wrapped = jnp.where(seg == 0, 0.0, pltpu.roll(halo, 1, 0))
    return jnp.where(first, wrapped, halo)


def _wrap_next_halo(halo, last):
    seg = lax.broadcasted_iota(jnp.int32, halo.shape, 0) % N_SEG
    wrapped = jnp.where(seg == N_SEG - 1, 0.0, pltpu.roll(halo, halo.shape[0] - 1, 0))
    return jnp.where(last, wrapped, halo)


def _rows_before(get, halo, r, rows, back):
    lo = r - back
    if lo >= 0:
        return get(lo, rows)
    return jnp.concatenate([halo[HALO_ROWS + lo:], get(0, rows + lo)], axis=0)


def _rows_after(get, halo, r, rows, ahead, tile_rows):
    over = r + ahead + rows - tile_rows
    if over <= 0:
        return get(r + ahead, rows)
    return jnp.concatenate([get(r + ahead, rows - over), halo[:over]], axis=0)


def _dev_index(p):
    return 4 * p[0] + 2 * p[1] + p[2]


def _allgather(name, shards, deps=()):
    n = len(shards)

    def body(*refs):
        ins, outs = refs[:n], refs[n:2 * n]
        send_sems, recv_sems, local_sems = refs[2 * n:]
        x, y, c = lax.axis_index("x"), lax.axis_index("y"), lax.axis_index("c")
        me, sibling = (x, y, c), (x, y, 1 - c)
        chips = [(1 - x, y), (x, 1 - y), (1 - x, 1 - y)]

        def copy(a, k, block, to, src=None):
            dst = outs[a].at[_dev_index(block)]
            return pltpu.make_async_remote_copy(
                src_ref=dst if src is None else src, dst_ref=dst,
                send_sem=send_sems.at[a, k], recv_sem=recv_sems.at[a, k],
                device_id=to, device_id_type=MESH)

        mine = [pltpu.make_async_copy(ins[a], outs[a].at[_dev_index(me)], local_sems.at[a])
                for a in range(n)]
        for cp in mine:
            cp.start()
        first = []
        for a in range(n):
            first.append(copy(a, 0, me, sibling, src=ins[a]))
            for j, chip in enumerate(chips):
                first.append(copy(a, 1 + j, me, (*chip, c), src=ins[a]))
        for cp in first:
            cp.start()
        passed = []
        for j, chip in enumerate(chips):
            for a in range(n):
                copy(a, 1 + j, (*chip, c), me).wait_recv()
                fwd = copy(a, 4 + j, (*chip, c), sibling)
                fwd.start()
                passed.append(fwd)
        for a in range(n):
            copy(a, 0, sibling, me).wait_recv()
            for j, chip in enumerate(chips):
                copy(a, 4 + j, (*chip, 1 - c), me).wait_recv()
        for cp in first + passed:
            cp.wait_send()
        for cp in mine:
            cp.wait()

    any_spec = pl.BlockSpec(memory_space=pl.ANY)
    return pl.pallas_call(
        _ignoring_deps(body, n, deps), name=name,
        out_shape=[jax.ShapeDtypeStruct((N_DEV,) + s.shape, s.dtype) for s in shards],
        in_specs=[any_spec] * (n + len(deps)), out_specs=[any_spec] * n,
        scratch_shapes=[pltpu.SemaphoreType.DMA((n, 7)), pltpu.SemaphoreType.DMA((n, 7)),
                        pltpu.SemaphoreType.DMA((n,))],
    )(*shards, *deps)


HBM_SPEC = pl.BlockSpec(memory_space=pltpu.HBM)
SEM_SPEC = pl.BlockSpec(memory_space=pltpu.SEMAPHORE)
ANY_SPEC = pl.BlockSpec(memory_space=pl.ANY)
DATAFLOW = pltpu.SideEffectType.DATAFLOW_SIDE_EFFECTING


def _ignoring_deps(body, n_in, deps):
    n_dep = len(deps)

    def wrapped(*refs):
        return body(*refs[:n_in], *refs[n_in + n_dep:])

    return wrapped


def _my_position():
    x, y, c = lax.axis_index("x"), lax.axis_index("y"), lax.axis_index("c")
    return (x, y, c)


def _peer(me, k):
    return tuple((1 - v) if (k >> s) & 1 else v for v, s in zip(me, (2, 1, 0)))


def _split_copies(src_refs, land_refs, send_sems, recv_sems, gather):
    me = _my_position()
    copies = []
    for a, (src, land) in enumerate(zip(src_refs, land_refs)):
        for k in range(1, N_DEV):
            peer = _peer(me, k)
            copies.append(pltpu.make_async_remote_copy(
                src_ref=src if gather[a] else src.at[_dev_index(peer)], dst_ref=land.at[_dev_index(me)],
                send_sem=send_sems[a].at[k - 1], recv_sem=recv_sems[a].at[k - 1],
                device_id=peer, device_id_type=MESH))
    return copies


def _own_slot(block, like_shape):
    me = _dev_index(_my_position())
    return lax.dynamic_update_index_in_dim(lax.empty(like_shape, block.dtype), block, me, 0)


def _send_start(name, srcs, gather):
    n = len(srcs)
    me = _dev_index(_my_position())
    gather = [gather] * n if isinstance(gather, bool) else list(gather)
    lands = [_own_slot(s, (N_DEV,) + s.shape) if g else
             _own_slot(lax.dynamic_index_in_dim(s, me, 0, keepdims=False), s.shape)
             for s, g in zip(srcs, gather)]

    def body(*refs):
        src_refs, land_refs = refs[:n], refs[n:2 * n]
        send_sems, recv_sems = refs[2 * n:3 * n], refs[3 * n:4 * n]
        token = refs[-1]
        for cp in _split_copies(src_refs, land_refs, send_sems, recv_sems, gather):
            cp.start()
        token[...] = jnp.zeros_like(token)

    hbm = lambda a: pltpu.HBM(a.shape, a.dtype)
    sems = [pltpu.SemaphoreType.DMA((N_DEV - 1,))] * n
    outs = pl.pallas_call(
        body, name=name,
        out_shape=(*sems, *sems, *[hbm(s) for s in srcs], *[hbm(l) for l in lands],
                   jax.ShapeDtypeStruct((SUBLANES, LANES), F32)),
        in_specs=[HBM_SPEC] * (2 * n),
        out_specs=(*[SEM_SPEC] * (2 * n), *[HBM_SPEC] * (2 * n), pl.BlockSpec(memory_space=pltpu.VMEM)),
        input_output_aliases={i: 2 * n + i for i in range(2 * n)},
        compiler_params=pltpu.CompilerParams(has_side_effects=DATAFLOW),
    )(*[pltpu.with_memory_space_constraint(a, pltpu.HBM) for a in (*srcs, *lands)])
    state = dict(send=outs[:n], recv=outs[n:2 * n], srcs=outs[2 * n:3 * n], lands=outs[3 * n:4 * n],
                 gather=gather)
    return state, outs[-1]


def _send_wait(name, state, which, after):
    n = len(which)
    pick = lambda key: [state[key][i] for i in which]
    gather = pick("gather")

    def body(*refs):
        src_refs, land_refs = refs[:n], refs[n:2 * n]
        send_sems, recv_sems = refs[2 * n:3 * n], refs[3 * n:4 * n]
        for cp in _split_copies(src_refs, land_refs, send_sems, recv_sems, gather):
            cp.wait_send()
            cp.wait_recv()

    srcs, lands = pick("srcs"), pick("lands")
    hbm = lambda a: pltpu.HBM(a.shape, a.dtype)
    outs = pl.pallas_call(
        body, name=name,
        out_shape=(*[hbm(s) for s in srcs], *[hbm(l) for l in lands]),
        in_specs=[*[HBM_SPEC] * (2 * n), *[SEM_SPEC] * (2 * n), ANY_SPEC],
        out_specs=tuple([HBM_SPEC] * (2 * n)),
        input_output_aliases={i: i for i in range(2 * n)},
        compiler_params=pltpu.CompilerParams(has_side_effects=DATAFLOW),
    )(*srcs, *lands, *pick("send"), *pick("recv"), after)
    return outs[n:]


def _two_level_copies(src_refs, land_refs, sems1, sems2):
    x, y, c = _my_position()
    me, sibling = (x, y, c), (x, y, 1 - c)
    chips = [(1 - x, y), (x, 1 - y), (1 - x, 1 - y)]
    stage1, stage2 = [], []
    for a, land in enumerate(land_refs):
        def copy(src, block, to, sems, k):
            return pltpu.make_async_remote_copy(
                src_ref=src, dst_ref=land.at[_dev_index(block)], send_sem=sems[0][a].at[k],
                recv_sem=sems[1][a].at[k], device_id=to, device_id_type=MESH)
        src = src_refs[a] if src_refs is not None else land.at[_dev_index(me)]
        stage1.append([copy(src, me, sibling, sems1, 0)] +
                      [copy(src, me, (*chip, c), sems1, 1 + j) for j, chip in enumerate(chips)])
        if sems2 is not None:
            stage2.append([copy(land.at[_dev_index((*chip, c))], (*chip, c), sibling, sems2, j)
                           for j, chip in enumerate(chips)])
    return stage1, stage2


def _gather2_start(name, shards):
    n = len(shards)
    lands = [_own_slot(s, (N_DEV,) + s.shape) for s in shards]

    def body(*refs):
        src_refs, land_refs = refs[:n], refs[n:2 * n]
        sems1 = (refs[2 * n:3 * n], refs[3 * n:4 * n])
        stage1, _ = _two_level_copies(src_refs, land_refs, sems1, None)
        for copies in stage1:
            for cp in copies:
                cp.start()
        refs[-1][...] = jnp.zeros_like(refs[-1])

    hbm = lambda a: pltpu.HBM(a.shape, a.dtype)
    sems = [pltpu.SemaphoreType.DMA((4,))] * n
    outs = pl.pallas_call(
        body, name=name,
        out_shape=(*sems, *sems, *[hbm(s) for s in shards], *[hbm(l) for l in lands],
                   jax.ShapeDtypeStruct((SUBLANES, LANES), F32)),
        in_specs=[HBM_SPEC] * (2 * n),
        out_specs=(*[SEM_SPEC] * (2 * n), *[HBM_SPEC] * (2 * n), pl.BlockSpec(memory_space=pltpu.VMEM)),
        input_output_aliases={i: 2 * n + i for i in range(2 * n)},
        compiler_params=pltpu.CompilerParams(has_side_effects=DATAFLOW),
    )(*[pltpu.with_memory_space_constraint(a, pltpu.HBM) for a in (*shards, *lands)])
    state = dict(send1=list(outs[:n]), recv1=list(outs[n:2 * n]), srcs=list(outs[2 * n:3 * n]),
                 lands=list(outs[3 * n:4 * n]), send2={}, recv2={})
    return state, outs[-1]


def _gather2_forward(name, state, which, after):
    n = len(which)
    pick = lambda key: [state[key][i] for i in which]

    def body(*refs):
        land_refs, recv1 = refs[:n], refs[n:2 * n]
        outs = refs[2 * n + 1:]
        sems2 = (outs[:n], outs[n:2 * n])
        stage1, stage2 = _two_level_copies(None, land_refs, (recv1, recv1), sems2)
        for a in range(n):
            for j in range(3):
                stage1[a][1 + j].wait_recv()
                stage2[a][j].start()
        outs[-1][...] = jnp.zeros_like(outs[-1])

    lands = pick("lands")
    sems = [pltpu.SemaphoreType.DMA((3,))] * n
    outs = pl.pallas_call(
        body, name=name,
        out_shape=(*sems, *sems, *[pltpu.HBM(l.shape, l.dtype) for l in lands],
                   jax.ShapeDtypeStruct((SUBLANES, LANES), F32)),
        in_specs=[*[HBM_SPEC] * n, *[SEM_SPEC] * n, ANY_SPEC],
        out_specs=(*[SEM_SPEC] * (2 * n), *[HBM_SPEC] * n, pl.BlockSpec(memory_space=pltpu.VMEM)),
        input_output_aliases={i: 2 * n + i for i in range(n)},
        compiler_params=pltpu.CompilerParams(has_side_effects=DATAFLOW),
    )(*lands, *pick("recv1"), after)
    for idx, i in enumerate(which):
        state["send2"][i], state["recv2"][i] = outs[idx], outs[n + idx]
        state["lands"][i] = outs[2 * n + idx]
    return outs[-1]


def _gather2_wait(name, state, which, after):
    n = len(which)
    pick = lambda key: [state[key][i] for i in which]

    def body(*refs):
        src_refs, land_refs = refs[:n], refs[n:2 * n]
        sems1 = (refs[2 * n:3 * n], refs[3 * n:4 * n])
        sems2 = (refs[4 * n:5 * n], refs[5 * n:6 * n])
        stage1, stage2 = _two_level_copies(src_refs, land_refs, sems1, sems2)
        for a in range(n):
            for cp in stage1[a]:
                cp.wait_send()
            stage1[a][0].wait_recv()
            for cp in stage2[a]:
                cp.wait_send()
                cp.wait_recv()

    srcs, lands = pick("srcs"), pick("lands")
    hbm = lambda a: pltpu.HBM(a.shape, a.dtype)
    outs = pl.pallas_call(
        body, name=name,
        out_shape=(*[hbm(s) for s in srcs], *[hbm(l) for l in lands]),
        in_specs=[*[HBM_SPEC] * (2 * n), *[SEM_SPEC] * (4 * n), ANY_SPEC],
        out_specs=tuple([HBM_SPEC] * (2 * n)),
        input_output_aliases={i: i for i in range(2 * n)},
        compiler_params=pltpu.CompilerParams(has_side_effects=DATAFLOW),
    )(*srcs, *lands, *pick("send1"), *pick("recv1"), *pick("send2"), *pick("recv2"), after)
    return outs[n:]


def _mm(name, a, b, *, dims, grid, a_spec, b_spec, o_spec, out_shape, acc_shape=None,
        res=None, res_spec=None):
    n_red = grid[-1] if acc_shape is not None else 1
    red_axis = len(grid) - 1

    def body(*refs):
        a_ref, b_ref = refs[0], refs[1]
        r_ref = refs[2] if res is not None else None
        o_ref = refs[3] if res is not None else refs[2]
        part = _dot(a_ref[...], b_ref[...], dims)
        if acc_shape is None:
            if r_ref is not None:
                part = part + r_ref[...]
            o_ref[...] = part.astype(o_ref.dtype)
            return
        acc_ref = refs[-1]
        k = pl.program_id(red_axis)

        @pl.when(k == 0)
        def _():
            acc_ref[...] = part

        @pl.when(k > 0)
        def _():
            acc_ref[...] += part

        @pl.when(k == n_red - 1)
        def _():
            total = acc_ref[...]
            if r_ref is not None:
                total = total + r_ref[...]
            o_ref[...] = total.astype(o_ref.dtype)

    ins, in_specs = [a, b], [a_spec, b_spec]
    if res is not None:
        ins.append(res)
        in_specs.append(res_spec)
    return pl.pallas_call(
        body, name=name, grid=grid, in_specs=in_specs, out_specs=o_spec, out_shape=out_shape,
        scratch_shapes=[pltpu.VMEM(acc_shape, F32)] if acc_shape is not None else [],
        compiler_params=_params(),
    )(*ins)


def _wide_tile(t):
    return next(c for c in WIDE_TILES if t % c == 0)


def _rms_fwd(name, h, g):
    t, d = h.shape
    tile = _wide_tile(t)

    def body(h_ref, g_ref, o_ref):
        x = h_ref[...]
        o_ref[...] = (x * _mean_sq_rsqrt(x) * g_ref[...]).astype(BF16)

    return pl.pallas_call(
        body, name=name, grid=(t // tile,),
        in_specs=[pl.BlockSpec((tile, d), lambda i: (i, 0)), pl.BlockSpec((1, d), lambda i: (0, 0))],
        out_specs=pl.BlockSpec((tile, d), lambda i: (i, 0)),
        out_shape=jax.ShapeDtypeStruct((t, d), BF16), compiler_params=_params(),
    )(h, g)


def _rms_bwd_res(name, dres, dhn, h, g, deps=()):
    t, d = h.shape

    def body(dres_ref, dhn_ref, h_ref, g_ref, dh_ref, dhb_ref, dg_ref):
        x = h_ref[...]
        dx, dgt = _rms_bwd(x, _mean_sq_rsqrt(x), g_ref[...], dhn_ref[...])
        dh = dres_ref[...] + dx
        dh_ref[...] = dh
        dhb_ref[...] = dh.astype(BF16)

        @pl.when(pl.program_id(0) == 0)
        def _():
            dg_ref[...] = jnp.zeros_like(dg_ref)

        dg_ref[...] += jnp.sum(dgt, axis=0, keepdims=True)

    tile = _wide_tile(t)
    row = pl.BlockSpec((tile, d), lambda i: (i, 0))
    vec = pl.BlockSpec((1, d), lambda i: (0, 0))
    return pl.pallas_call(
        _ignoring_deps(body, 4, deps), name=name, grid=(t // tile,),
        in_specs=[row, row, row, vec] + [ANY_SPEC] * len(deps), out_specs=[row, row, vec],
        out_shape=[jax.ShapeDtypeStruct((t, d), F32), jax.ShapeDtypeStruct((t, d), BF16),
                   jax.ShapeDtypeStruct((1, d), F32)],
        compiler_params=_params(),
    )(dres, dhn, h, g, *deps)


def _loss_bwd(h2, target, g, row_lo, row_hi):
    t, d = h2.shape
    tile = _wide_tile(t)

    def body(h_ref, tg_ref, g_ref, dh_ref, dhb_ref, loss_ref, dg_ref):
        i = pl.program_id(0)
        x = h_ref[...]
        r = _mean_sq_rsqrt(x)
        gv = g_ref[...]
        y = x * r * gv
        rows = i * tile + lax.broadcasted_iota(jnp.int32, (tile, 1), 0)
        time = (rows % N_SEG) * (t // N_SEG) + rows // N_SEG
        valid = jnp.logical_and(time >= row_lo, time < row_hi)
        err = jnp.where(valid, y - tg_ref[...], 0.0)
        dy = err * (1.0 / d)
        dx, dgt = _rms_bwd(x, r, gv, dy)
        dh_ref[...] = dx
        dhb_ref[...] = dx.astype(BF16)

        @pl.when(i == 0)
        def _():
            loss_ref[...] = jnp.zeros_like(loss_ref)
            dg_ref[...] = jnp.zeros_like(dg_ref)

        row_loss = jnp.mean(err * err, axis=-1, keepdims=True)
        loss_ref[...] += 0.5 * jnp.sum(row_loss, axis=0, keepdims=True)
        dg_ref[...] += jnp.sum(dgt, axis=0, keepdims=True)

    row = pl.BlockSpec((tile, d), lambda i: (i, 0))
    vec = pl.BlockSpec((1, d), lambda i: (0, 0))
    return pl.pallas_call(
        body, name="loss_bwd", grid=(t // tile,), in_specs=[row, row, vec],
        out_specs=[row, row, pl.BlockSpec((1, 1), lambda i: (0, 0)), vec],
        out_shape=[jax.ShapeDtypeStruct((t, d), F32), jax.ShapeDtypeStruct((t, d), BF16),
                   jax.ShapeDtypeStruct((1, 1), F32), jax.ShapeDtypeStruct((1, d), F32)],
        compiler_params=_params(),
    )(h2, target, g)


def _prev_halo(i, t):
    return jnp.where(i == 0, t // HALO_ROWS - 1, i * (ROW_TILE // HALO_ROWS) - 1)


def _next_halo(i, t):
    return jnp.where(i == t // ROW_TILE - 1, 0, (i + 1) * (ROW_TILE // HALO_ROWS))


def _causal_taps(cur, halo, first):
    rows = cur.shape[0]
    ext = jnp.concatenate([_wrap_prev_halo(halo, first), cur], axis=0)
    return ext[:rows], ext[N_SEG:N_SEG + rows]


def _anticausal_taps(cur, halo, last):
    rows = cur.shape[0]
    ext = jnp.concatenate([cur, _wrap_next_halo(halo, last)], axis=0)
    return ext[N_SEG:N_SEG + rows], ext[2 * N_SEG:2 * N_SEG + rows]


def _mix_fwd(proj, y, w_glu, conv_w, gain_c, gain_s):
    t = proj.shape[0]
    dc = conv_w.shape[1]
    ds = y.shape[1]

    def body(p_ref, halo_ref, y_ref, wg_ref, cw_ref, gc_ref, gs_ref, mixed_ref, z_ref):
        i = pl.program_id(0)
        p = p_ref[...]
        b, c, v = p[:, :dc], p[:, dc:2 * dc], p[:, 2 * dc:3 * dc]
        cv = c * v
        hp = halo_ref[...]
        x2, x1 = _causal_taps(cv, hp[:, dc:2 * dc] * hp[:, 2 * dc:3 * dc], i == 0)
        cw = cw_ref[...]
        conv = cw[0:1] * x2 + cw[1:2] * x1 + cw[2:3] * cv
        co = b * conv
        mixed_ref[:, :dc] = (co * _mean_sq_rsqrt(co) * gc_ref[...]).astype(BF16)
        g, _ = _gelu(y_ref[...])
        z = _dot(g.astype(BF16), wg_ref[...], NN)
        z_ref[...] = z
        so = g * jax.nn.sigmoid(z)
        mixed_ref[:, dc:] = (so * _mean_sq_rsqrt(so) * gs_ref[...]).astype(BF16)

    const = lambda i: (0, 0)
    return pl.pallas_call(
        body, name="mix_fwd", grid=(t // ROW_TILE,),
        in_specs=[pl.BlockSpec((ROW_TILE, 3 * dc), lambda i: (i, 0)),
                  pl.BlockSpec((HALO_ROWS, 3 * dc), lambda i: (_prev_halo(i, t), 0)),
                  pl.BlockSpec((ROW_TILE, ds), lambda i: (i, 0)),
                  pl.BlockSpec((ds, ds), const), pl.BlockSpec(conv_w.shape, const),
                  pl.BlockSpec((1, dc), const), pl.BlockSpec((1, ds), const)],
        out_specs=[pl.BlockSpec((ROW_TILE, dc + ds), lambda i: (i, 0)),
                   pl.BlockSpec((ROW_TILE, ds), lambda i: (i, 0))],
        out_shape=[jax.ShapeDtypeStruct((t, dc + ds), BF16), jax.ShapeDtypeStruct((t, ds), F32)],
        compiler_params=_params(),
    )(proj, proj, y, w_glu, conv_w, gain_c, gain_s)


def _mix_bwd1(proj, y, z, dmixed, w_glu, conv_w, gain_c, gain_s):
    t = proj.shape[0]
    dc = conv_w.shape[1]
    ds = y.shape[1]

    def body(p_ref, halo_ref, y_ref, z_ref, dm_ref, wg_ref, cw_ref, gc_ref, gs_ref,
             db_ref, dconv_ref, dy_ref, dwg_ref, dcw_ref, dgc_ref, dgs_ref):
        i = pl.program_id(0)

        @pl.when(i == 0)
        def _():
            dwg_ref[...] = jnp.zeros_like(dwg_ref)
            dcw_ref[...] = jnp.zeros_like(dcw_ref)
            dgc_ref[...] = jnp.zeros_like(dgc_ref)
            dgs_ref[...] = jnp.zeros_like(dgs_ref)

        p = p_ref[...]
        b, c, v = p[:, :dc], p[:, dc:2 * dc], p[:, 2 * dc:3 * dc]
        cv = c * v
        hp = halo_ref[...]
        x2, x1 = _causal_taps(cv, hp[:, dc:2 * dc] * hp[:, 2 * dc:3 * dc], i == 0)
        cw = cw_ref[...]
        conv = cw[0:1] * x2 + cw[1:2] * x1 + cw[2:3] * cv
        co = b * conv
        dm = dm_ref[...]
        dco, dgt = _rms_bwd(co, _mean_sq_rsqrt(co), gc_ref[...], dm[:, :dc])
        dgc_ref[...] += jnp.sum(dgt, axis=0, keepdims=True)
        db_ref[...] = (dco * conv).astype(BF16)
        dconv = dco * b
        dconv_ref[...] = dconv
        dcw_ref[0:1, :] += jnp.sum(dconv * x2, axis=0, keepdims=True)
        dcw_ref[1:2, :] += jnp.sum(dconv * x1, axis=0, keepdims=True)
        dcw_ref[2:3, :] += jnp.sum(dconv * cv, axis=0, keepdims=True)

        yv = y_ref[...]
        g, th = _gelu(yv)
        sg = jax.nn.sigmoid(z_ref[...])
        so = g * sg
        dso, dgt = _rms_bwd(so, _mean_sq_rsqrt(so), gs_ref[...], dm[:, dc:])
        dgs_ref[...] += jnp.sum(dgt, axis=0, keepdims=True)
        dz = (dso * g * sg * (1.0 - sg)).astype(BF16)
        dg = dso * sg + _dot(dz, wg_ref[...], NT)
        dwg_ref[...] += _dot(g.astype(BF16), dz, TN)
        dy_ref[...] = dg * _gelu_grad(yv, th)

    const = lambda i: (0, 0)
    row = lambda w: pl.BlockSpec((ROW_TILE, w), lambda i: (i, 0))
    return pl.pallas_call(
        body, name="mix_bwd1", grid=(t // ROW_TILE,),
        in_specs=[row(3 * dc), pl.BlockSpec((HALO_ROWS, 3 * dc), lambda i: (_prev_halo(i, t), 0)),
                  row(ds), row(ds), row(dc + ds), pl.BlockSpec((ds, ds), const),
                  pl.BlockSpec(conv_w.shape, const), pl.BlockSpec((1, dc), const),
                  pl.BlockSpec((1, ds), const)],
        out_specs=[row(dc), row(dc), row(ds), pl.BlockSpec((ds, ds), const),
                   pl.BlockSpec(conv_w.shape, const), pl.BlockSpec((1, dc), const),
                   pl.BlockSpec((1, ds), const)],
        out_shape=[jax.ShapeDtypeStruct((t, dc), BF16), jax.ShapeDtypeStruct((t, dc), F32),
                   jax.ShapeDtypeStruct((t, ds), F32), jax.ShapeDtypeStruct((ds, ds), F32),
                   jax.ShapeDtypeStruct(conv_w.shape, F32), jax.ShapeDtypeStruct((1, dc), F32),
                   jax.ShapeDtypeStruct((1, ds), F32)],
        compiler_params=_params(),
    )(proj, proj, y, z, dmixed, w_glu, conv_w, gain_c, gain_s)


def _mix_bwd2(proj, dconv, conv_w, deps=()):
    t = proj.shape[0]
    dc = conv_w.shape[1]
    n_tiles = t // ROW_TILE

    def body(c_ref, v_ref, d_ref, halo_ref, cw_ref, o_ref):
        i = pl.program_id(0)
        d = d_ref[...]
        u1, u2 = _anticausal_taps(d, halo_ref[...], i == n_tiles - 1)
        cw = cw_ref[...]
        dcv = cw[2:3] * d + cw[1:2] * u1 + cw[0:1] * u2
        o_ref[:, :dc] = (dcv * v_ref[...]).astype(BF16)
        o_ref[:, dc:] = (dcv * c_ref[...]).astype(BF16)

    return pl.pallas_call(
        _ignoring_deps(body, 5, deps), name="mix_bwd2", grid=(n_tiles,),
        in_specs=[pl.BlockSpec((ROW_TILE, dc), lambda i: (i, 1)),
                  pl.BlockSpec((ROW_TILE, dc), lambda i: (i, 2)),
                  pl.BlockSpec((ROW_TILE, dc), lambda i: (i, 0)),
                  pl.BlockSpec((HALO_ROWS, dc), lambda i: (_next_halo(i, t), 0)),
                  pl.BlockSpec(conv_w.shape, lambda i: (0, 0))] + [ANY_SPEC] * len(deps),
        out_specs=pl.BlockSpec((ROW_TILE, 2 * dc), lambda i: (i, 0)),
        out_shape=jax.ShapeDtypeStruct((t, 2 * dc), BF16), compiler_params=_params(),
    )(proj, proj, dconv, dconv, conv_w, *deps)


def _ffn_taps(up_ref, halo, fw_ref, fb_ref, q, r):
    get = lambda s, n: up_ref[q, pl.ds(s, n), :]
    x0 = get(r, ROW_CHUNK)
    x1 = _rows_before(get, halo, r, ROW_CHUNK, N_SEG)
    x2 = _rows_before(get, halo, r, ROW_CHUNK, 2 * N_SEG)
    w = fw_ref[q]
    return w[0:1] * x2 + w[1:2] * x1 + w[2:3] * x0 + fb_ref[q], x2, x1, x0


def _ffn_fwd(up_pre, fw, fb):
    nb, t, w = up_pre.shape
    half = nb // 2

    def body(up_ref, halo_ref, fw_ref, fb_ref, act_ref):
        first = pl.program_id(0) == 0
        for q in range(half):
            halo_a = _wrap_prev_halo(halo_ref[q], first)
            halo_v = _wrap_prev_halo(halo_ref[q + half], first)
            for r in range(0, ROW_TILE, ROW_CHUNK):
                a = _ffn_taps(up_ref, halo_a, fw_ref, fb_ref, q, r)[0]
                val = _ffn_taps(up_ref, halo_v, fw_ref, fb_ref, q + half, r)[0]
                act_ref[q, pl.ds(r, ROW_CHUNK), :] = (a * jax.nn.sigmoid(a) * val).astype(BF16)

    return pl.pallas_call(
        body, name="ffn_fwd", grid=(t // ROW_TILE,),
        in_specs=[pl.BlockSpec((nb, ROW_TILE, w), lambda i: (0, i, 0)),
                  pl.BlockSpec((nb, HALO_ROWS, w), lambda i: (0, _prev_halo(i, t), 0)),
                  pl.BlockSpec(fw.shape, lambda i: (0, 0, 0)),
                  pl.BlockSpec(fb.shape, lambda i: (0, 0, 0))],
        out_specs=pl.BlockSpec((half, ROW_TILE, w), lambda i: (0, i, 0)),
        out_shape=jax.ShapeDtypeStruct((half, t, w), BF16), compiler_params=_params(),
    )(up_pre, up_pre, fw, fb)


def _ffn_bwd1(up_pre, dact, fw, fb, deps=()):
    nb, t, w = up_pre.shape
    half = nb // 2

    def body(up_ref, halo_ref, dact_ref, fw_ref, fb_ref, dup_ref, dfw_ref, dfb_ref):
        first = pl.program_id(0) == 0

        @pl.when(first)
        def _():
            dfw_ref[...] = jnp.zeros_like(dfw_ref)
            dfb_ref[...] = jnp.zeros_like(dfb_ref)

        for q in range(half):
            halos = {q: _wrap_prev_halo(halo_ref[q], first),
                     q + half: _wrap_prev_halo(halo_ref[q + half], first)}
            sums = {blk: [jnp.zeros((SUBLANES, w), F32)] * 4 for blk in halos}
            fold = lambda v: sum(v[s:s + SUBLANES] for s in range(0, ROW_CHUNK, SUBLANES))
            for r in range(0, ROW_TILE, ROW_CHUNK):
                taps_a = _ffn_taps(up_ref, halos[q], fw_ref, fb_ref, q, r)
                taps_v = _ffn_taps(up_ref, halos[q + half], fw_ref, fb_ref, q + half, r)
                a, val = taps_a[0], taps_v[0]
                da_ct = dact_ref[q, pl.ds(r, ROW_CHUNK), :]
                sg = jax.nn.sigmoid(a)
                da = da_ct * val * sg * (1.0 + a * (1.0 - sg))
                dval = da_ct * a * sg
                for blk, dv, taps in ((q, da, taps_a), (q + half, dval, taps_v)):
                    dup_ref[blk, pl.ds(r, ROW_CHUNK), :] = dv
                    terms = (dv, dv * taps[1], dv * taps[2], dv * taps[3])
                    sums[blk] = [s + fold(v) for s, v in zip(sums[blk], terms)]
            for blk, parts in sums.items():
                s_b, s_w0, s_w1, s_w2 = (jnp.sum(p, axis=0, keepdims=True) for p in parts)
                dfb_ref[blk] += s_b
                for k, s_w in enumerate((s_w0, s_w1, s_w2)):
                    dfw_ref[blk, k:k + 1, :] += s_w

    return pl.pallas_call(
        _ignoring_deps(body, 5, deps), name="ffn_bwd1", grid=(t // ROW_TILE,),
        in_specs=[pl.BlockSpec((nb, ROW_TILE, w), lambda i: (0, i, 0)),
                  pl.BlockSpec((nb, HALO_ROWS, w), lambda i: (0, _prev_halo(i, t), 0)),
                  pl.BlockSpec((half, ROW_TILE, w), lambda i: (0, i, 0)),
                  pl.BlockSpec(fw.shape, lambda i: (0, 0, 0)),
                  pl.BlockSpec(fb.shape, lambda i: (0, 0, 0))] + [ANY_SPEC] * len(deps),
        out_specs=[pl.BlockSpec((nb, ROW_TILE, w), lambda i: (0, i, 0)),
                   pl.BlockSpec(fw.shape, lambda i: (0, 0, 0)),
                   pl.BlockSpec(fb.shape, lambda i: (0, 0, 0))],
        out_shape=[jax.ShapeDtypeStruct((nb, t, w), F32), jax.ShapeDtypeStruct(fw.shape, F32),
                   jax.ShapeDtypeStruct(fb.shape, F32)],
        compiler_params=_params(),
    )(up_pre, up_pre, dact, fw, fb, *deps)


def _ffn_bwd2(dup, fw):
    nb, t, w = dup.shape
    n_tiles = t // ROW_TILE

    def body(d_ref, halo_ref, fw_ref, o_ref):
        last = pl.program_id(0) == n_tiles - 1
        for q in range(nb):
            halo = _wrap_next_halo(halo_ref[q], last)
            get = lambda s, n: d_ref[q, pl.ds(s, n), :]
            cw = fw_ref[q]
            for r in range(0, ROW_TILE, ROW_CHUNK):
                d = get(r, ROW_CHUNK)
                u1 = _rows_after(get, halo, r, ROW_CHUNK, N_SEG, ROW_TILE)
                u2 = _rows_after(get, halo, r, ROW_CHUNK, 2 * N_SEG, ROW_TILE)
                o_ref[q, pl.ds(r, ROW_CHUNK), :] = (cw[2:3] * d + cw[1:2] * u1 + cw[0:1] * u2).astype(BF16)

    return pl.pallas_call(
        body, name="ffn_bwd2", grid=(n_tiles,),
        in_specs=[pl.BlockSpec((nb, ROW_TILE, w), lambda i: (0, i, 0)),
                  pl.BlockSpec((nb, HALO_ROWS, w), lambda i: (0, _next_halo(i, t), 0)),
                  pl.BlockSpec(fw.shape, lambda i: (0, 0, 0))],
        out_specs=pl.BlockSpec((nb, ROW_TILE, w), lambda i: (0, i, 0)),
        out_shape=jax.ShapeDtypeStruct((nb, t, w), BF16), compiler_params=_params(),
    )(dup, dup, fw)


def _to_segments(a):
    t, c = a.shape
    return a.reshape(N_SEG, t // N_SEG, c).transpose(1, 0, 2).reshape(t, c)


def _from_segments(a):
    t, c = a.shape
    return a.reshape(t // N_SEG, N_SEG, c).transpose(1, 0, 2).reshape(t, c)


def _cmul(ar, ai, br, bi):
    return ar * br - ai * bi, ar * bi + ai * br


def _segment_carries(pr, pi, fr, fi, forward):
    row = lax.broadcasted_iota(jnp.int32, fr.shape, 0)
    edge = row == (0 if forward else N_SEG - 1)
    shift = 1 if forward else N_SEG - 1
    sr, si = jnp.zeros_like(fr), jnp.zeros_like(fi)
    for _ in range(N_SEG - 1):
        tr, ti = _cmul(pr, pi, sr, si)
        sr = jnp.where(edge, 0.0, pltpu.roll(tr + fr, shift, 0))
        si = jnp.where(edge, 0.0, pltpu.roll(ti + fi, shift, 0))
    return sr, si


def _rows(i):
    return pl.ds(pl.multiple_of(i * SUBLANES, SUBLANES), SUBLANES)


def _s5_fwd(proj, u_col, bb_re, bb_im, a_re, a_im, cc_re, cc_im, d_skip):
    t = proj.shape[0]
    nch, _, cs = bb_re.shape
    ds = nch * SSM_CHUNK
    u_blk = u_col // SSM_CHUNK
    steps = t // N_SEG

    def body(u_ref, bbr_ref, bbi_ref, ar_ref, ai_ref, ccr_ref, cci_ref, d_ref, sr_ref, si_ref, y_ref):
        ub = u_ref[...].astype(BF16)
        sr_ref[...] = _dot(ub, bbr_ref[...], NN)
        si_ref[...] = _dot(ub, bbi_ref[...], NN)
        ar = jnp.broadcast_to(ar_ref[...], (N_SEG, cs))
        ai = jnp.broadcast_to(ai_ref[...], (N_SEG, cs))
        zero = jnp.zeros((N_SEG, cs), F32)

        def totals(i, carry):
            sr, si, pr, pi = carry
            tr, ti = _cmul(ar, ai, sr, si)
            qr, qi = _cmul(ar, ai, pr, pi)
            return tr + sr_ref[_rows(i), :], ti + si_ref[_rows(i), :], qr, qi

        fr, fi, pr, pi = lax.fori_loop(0, steps, totals, (zero, zero, zero + 1.0, zero))
        s0r, s0i = _segment_carries(pr, pi, fr, fi, True)

        def scan(i, carry):
            tr, ti = _cmul(ar, ai, *carry)
            nr, ni = tr + sr_ref[_rows(i), :], ti + si_ref[_rows(i), :]
            sr_ref[_rows(i), :] = nr
            si_ref[_rows(i), :] = ni
            return nr, ni

        lax.fori_loop(0, steps, scan, (s0r, s0i))
        y_ref[...] = (_dot(sr_ref[...].astype(BF16), ccr_ref[...], NN)
                      - _dot(si_ref[...].astype(BF16), cci_ref[...], NN)
                      + d_ref[...] * u_ref[...])

    chunk3 = lambda r, c: pl.BlockSpec((None, r, c), lambda j: (j, 0, 0))
    return pl.pallas_call(
        body, name="s5_fwd", grid=(nch,),
        in_specs=[pl.BlockSpec((t, SSM_CHUNK), lambda j: (0, j + u_blk)),
                  chunk3(SSM_CHUNK, cs), chunk3(SSM_CHUNK, cs), chunk3(1, cs), chunk3(1, cs),
                  chunk3(cs, SSM_CHUNK), chunk3(cs, SSM_CHUNK), chunk3(1, SSM_CHUNK)],
        out_specs=[pl.BlockSpec((t, cs), lambda j: (0, j)), pl.BlockSpec((t, cs), lambda j: (0, j)),
                   pl.BlockSpec((t, SSM_CHUNK), lambda j: (0, j))],
        out_shape=[jax.ShapeDtypeStruct((t, nch * cs), F32), jax.ShapeDtypeStruct((t, nch * cs), F32),
                   jax.ShapeDtypeStruct((t, ds), F32)],
        compiler_params=_params(),
    )(proj, bb_re, bb_im, a_re, a_im, cc_re, cc_im, d_skip)


def _s5_bwd(dy, proj, u_col, s_re, s_im, bb_re, bb_im, a_re, a_im, cc_re, cc_im, d_skip):
    t, ds = dy.shape
    nch, _, cs = bb_re.shape
    u_blk = u_col // SSM_CHUNK
    steps = t // N_SEG

    def body(dy_ref, u_ref, sr_ref, si_ref, bbr_ref, bbi_ref, ar_ref, ai_ref, ccr_ref, cci_ref, d_ref,
             du_ref, dbbr_ref, dbbi_ref, dar_ref, dai_ref, dccr_ref, dcci_ref, dd_ref, gr_ref, gi_ref):
        dyv = dy_ref[...]
        dyb = dyv.astype(BF16)
        gr_ref[...] = _dot(dyb, ccr_ref[...], NT)
        gi_ref[...] = -_dot(dyb, cci_ref[...], NT)
        ar = jnp.broadcast_to(ar_ref[...], (N_SEG, cs))
        ai = -jnp.broadcast_to(ai_ref[...], (N_SEG, cs))
        zero = jnp.zeros((N_SEG, cs), F32)

        def totals(k, carry):
            i = steps - 1 - k
            gr, gi, pr, pi = carry
            tr, ti = _cmul(ar, ai, gr, gi)
            qr, qi = _cmul(ar, ai, pr, pi)
            return tr + gr_ref[_rows(i), :], ti + gi_ref[_rows(i), :], qr, qi

        fr, fi, pr, pi = lax.fori_loop(0, steps, totals, (zero, zero, zero + 1.0, zero))
        e0r, e0i = _segment_carries(pr, pi, fr, fi, False)

        def step(i, gr, gi, pr, pi, acc_r, acc_i):
            tr, ti = _cmul(ar, ai, gr, gi)
            nr, ni = tr + gr_ref[_rows(i), :], ti + gi_ref[_rows(i), :]
            gr_ref[_rows(i), :] = nr
            gi_ref[_rows(i), :] = ni
            return nr, ni, acc_r + nr * pr + ni * pi, acc_i + ni * pr - nr * pi

        def scan(k, carry):
            i = steps - 1 - k
            gr, gi, acc_r, acc_i = carry
            return step(i, gr, gi, sr_ref[_rows(i - 1), :], si_ref[_rows(i - 1), :], acc_r, acc_i)

        gr, gi, acc_r, acc_i = lax.fori_loop(0, steps - 1, scan, (e0r, e0i, zero, zero))
        row = lax.broadcasted_iota(jnp.int32, (N_SEG, cs), 0)
        last = _rows(steps - 1)
        pr = jnp.where(row == 0, 0.0, pltpu.roll(sr_ref[last, :], 1, 0))
        pi = jnp.where(row == 0, 0.0, pltpu.roll(si_ref[last, :], 1, 0))
        _, _, acc_r, acc_i = step(0, gr, gi, pr, pi, acc_r, acc_i)
        dar_ref[...] = jnp.sum(acc_r, axis=0, keepdims=True)
        dai_ref[...] = jnp.sum(acc_i, axis=0, keepdims=True)

        uv = u_ref[...]
        ub = uv.astype(BF16)
        grb = gr_ref[...].astype(BF16)
        gib = gi_ref[...].astype(BF16)
        du = d_ref[...] * dyv + _dot(grb, bbr_ref[...], NT) + _dot(gib, bbi_ref[...], NT)
        du_ref[...] = du.astype(BF16)
        dbbr_ref[...] = _dot(ub, grb, TN)
        dbbi_ref[...] = _dot(ub, gib, TN)
        dccr_ref[...] = _dot(sr_ref[...].astype(BF16), dyb, TN)
        dcci_ref[...] = -_dot(si_ref[...].astype(BF16), dyb, TN)
        dd_ref[...] = jnp.sum(dyv * uv, axis=0, keepdims=True)

    chunk3 = lambda r, c: pl.BlockSpec((None, r, c), lambda j: (j, 0, 0))
    cols = lambda w: pl.BlockSpec((t, w), lambda j: (0, j))
    return pl.pallas_call(
        body, name="s5_bwd", grid=(nch,),
        in_specs=[cols(SSM_CHUNK), pl.BlockSpec((t, SSM_CHUNK), lambda j: (0, j + u_blk)), cols(cs), cols(cs),
                  chunk3(SSM_CHUNK, cs), chunk3(SSM_CHUNK, cs), chunk3(1, cs), chunk3(1, cs),
                  chunk3(cs, SSM_CHUNK), chunk3(cs, SSM_CHUNK), chunk3(1, SSM_CHUNK)],
        out_specs=[cols(SSM_CHUNK), chunk3(SSM_CHUNK, cs), chunk3(SSM_CHUNK, cs), chunk3(1, cs),
                   chunk3(1, cs), chunk3(cs, SSM_CHUNK), chunk3(cs, SSM_CHUNK), chunk3(1, SSM_CHUNK)],
        out_shape=[jax.ShapeDtypeStruct((t, ds), BF16),
                   jax.ShapeDtypeStruct((nch, SSM_CHUNK, cs), F32), jax.ShapeDtypeStruct((nch, SSM_CHUNK, cs), F32),
                   jax.ShapeDtypeStruct((nch, 1, cs), F32), jax.ShapeDtypeStruct((nch, 1, cs), F32),
                   jax.ShapeDtypeStruct((nch, cs, SSM_CHUNK), F32), jax.ShapeDtypeStruct((nch, cs, SSM_CHUNK), F32),
                   jax.ShapeDtypeStruct((nch, 1, SSM_CHUNK), F32)],
        scratch_shapes=[pltpu.VMEM((t, cs), F32), pltpu.VMEM((t, cs), F32)],
        compiler_params=_params(),
    )(dy, proj, s_re, s_im, bb_re, bb_im, a_re, a_im, cc_re, cc_im, d_skip)


def _discretize(lr, li, log_dt, br, bi):
    dt = jnp.exp(log_dt)
    mag = jnp.exp(lr * dt)
    ang = li * dt
    a_re = mag * jnp.cos(ang)
    a_im = mag * jnp.sin(ang)
    den = lr * lr + li * li
    nr = a_re - 1.0
    f_re = (nr * lr + a_im * li) / den
    f_im = (a_im * lr - nr * li) / den
    return a_re, a_im, f_re * br - f_im * bi, f_re * bi + f_im * br


def _whole(shape):
    return pl.BlockSpec(shape, lambda: (0,) * len(shape))


def _disc_fwd(lr, li, log_dt, br, bi, deps=()):
    def body(lr_ref, li_ref, dt_ref, br_ref, bi_ref, ar_ref, ai_ref, bbr_ref, bbi_ref):
        outs = _discretize(lr_ref[...], li_ref[...], dt_ref[...], br_ref[...], bi_ref[...])
        for ref, val in zip((ar_ref, ai_ref, bbr_ref, bbi_ref), outs):
            ref[...] = val

    args = (lr, li, log_dt, br, bi)
    outs = (lr, lr, br, br)
    return pl.pallas_call(
        _ignoring_deps(body, 5, deps), name="disc_fwd",
        in_specs=[_whole(a.shape) for a in args] + [ANY_SPEC] * len(deps),
        out_specs=[_whole(a.shape) for a in outs],
        out_shape=[jax.ShapeDtypeStruct(a.shape, F32) for a in outs],
    )(*args, *deps)


def _disc_bwd(lr, li, log_dt, br, bi, dar, dai, dbbr, dbbi):
    def body(lr_ref, li_ref, dt_ref, br_ref, bi_ref, dar_ref, dai_ref, dbbr_ref, dbbi_ref,
             dlr_ref, dli_ref, ddt_ref, dbr_ref, dbi_ref):
        _, vjp = jax.vjp(_discretize, lr_ref[...], li_ref[...], dt_ref[...], br_ref[...], bi_ref[...])
        grads = vjp((dar_ref[...], dai_ref[...], dbbr_ref[...], dbbi_ref[...]))
        for ref, val in zip((dlr_ref, dli_ref, ddt_ref, dbr_ref, dbi_ref), grads):
            ref[...] = val

    args = (lr, li, log_dt, br, bi, dar, dai, dbbr, dbbi)
    outs = (lr, li, log_dt, br, bi)
    return pl.pallas_call(
        body, name="disc_bwd", in_specs=[_whole(a.shape) for a in args],
        out_specs=[_whole(a.shape) for a in outs],
        out_shape=[jax.ShapeDtypeStruct(a.shape, F32) for a in outs],
    )(*args)


def _adamw(w, g, m, v):
    m = ADAM_B1 * m + (1.0 - ADAM_B1) * g
    v = ADAM_B2 * v + (1.0 - ADAM_B2) * (g * g)
    m_hat = m / ADAM_BC1
    v_hat = v / ADAM_BC2
    delta = -ADAM_LR * (m_hat / (jnp.sqrt(v_hat) + ADAM_EPS) + ADAM_WD * w)
    return delta, m, v


def _adamw_reduce(name, parts, w, m, v):
    _, r, c = parts.shape
    tr = r
    for cand in (256, 176, 128):
        if r % cand == 0:
            tr = cand
            break

    def body(p_ref, w_ref, m_ref, v_ref, g_ref, d_ref, nm_ref, nv_ref):
        g = p_ref[0].astype(F32)
        for k in range(1, N_DEV):
            g = g + p_ref[k].astype(F32)
        delta, nm, nv = _adamw(w_ref[...], g, m_ref[...], v_ref[...])
        g_ref[...] = g
        d_ref[...] = delta
        nm_ref[...] = nm
        nv_ref[...] = nv

    blk = pl.BlockSpec((tr, c), lambda i: (i, 0))
    return pl.pallas_call(
        body, name=name, grid=(r // tr,),
        in_specs=[pl.BlockSpec((N_DEV, tr, c), lambda i: (0, i, 0)), blk, blk, blk],
        out_specs=[blk] * 4, out_shape=[jax.ShapeDtypeStruct((r, c), F32)] * 4,
        compiler_params=_params(),
    )(parts, w, m, v)


def _sum_parts(name, parts):
    _, r, c = parts.shape

    def body(p_ref, o_ref):
        g = p_ref[0]
        for k in range(1, N_DEV):
            g = g + p_ref[k]
        o_ref[...] = g

    return pl.pallas_call(
        body, name=name, in_specs=[_whole(parts.shape)], out_specs=_whole((r, c)),
        out_shape=jax.ShapeDtypeStruct((r, c), F32), compiler_params=_params(),
    )(parts)


def _adamw_many(name, grads, ws, ms, vs):
    n = len(grads)

    def body(*refs):
        ins, outs = refs[:4 * n], refs[4 * n:]
        for i in range(n):
            g, w, m, v = (ins[j * n + i][...] for j in range(4))
            for ref, val in zip((outs[i], outs[n + i], outs[2 * n + i]), _adamw(w, g, m, v)):
                ref[...] = val

    args = (*grads, *ws, *ms, *vs)
    outs = pl.pallas_call(
        body, name=name, in_specs=[_whole(a.shape) for a in args],
        out_specs=[_whole(a.shape) for a in ws] * 3,
        out_shape=[jax.ShapeDtypeStruct(a.shape, F32) for a in ws] * 3, compiler_params=_params(),
    )(*args)
    return outs[:n], outs[n:2 * n], outs[2 * n:]


def _pack(arrays, rows):
    flat = jnp.concatenate([a.reshape(-1) for a in arrays])
    return jnp.pad(flat, (0, rows * LANES - flat.shape[0])).reshape(rows, LANES)


def _unpack(packed, shapes):
    flat = packed.reshape(-1)
    out, off = [], 0
    for s in shapes:
        n = math.prod(s)
        out.append(flat[off:off + n].reshape(s))
        off += n
    return out


def _packed_rows(shapes):
    n = sum(math.prod(s) for s in shapes)
    return -(-n // (SUBLANES * LANES)) * SUBLANES


def _block_diag(x):
    j, g, r, c = x.shape
    eye = jnp.eye(g, dtype=x.dtype)
    return (x[:, :, :, None, :] * eye[None, :, None, :, None]).reshape(j, g * r, g * c)


def _diag_blocks(x, g):
    j, gr, gc = x.shape
    r, c = gr // g, gc // g
    eye = jnp.eye(g, dtype=x.dtype)
    return (x.reshape(j, g, r, g, c) * eye[None, :, None, :, None]).sum(axis=3)


def kernel(x, meta_tokens, norm_mix_g, w_in, conv_w, ssm_lam_re, ssm_lam_im, ssm_log_dt, ssm_b_re, ssm_b_im, ssm_c_re, ssm_c_im, ssm_d, ssm_w_glu, gain_conv_out, gain_ssm_out, w_out, norm_ffn_g, w_up, ffn_conv_w, ffn_conv_b, w_down, norm_final_g, loss_target, m_meta_tokens, m_norm_mix_g, m_w_in, m_conv_w, m_ssm_lam_re, m_ssm_lam_im, m_ssm_log_dt, m_ssm_b_re, m_ssm_b_im, m_ssm_c_re, m_ssm_c_im, m_ssm_d, m_ssm_w_glu, m_gain_conv_out, m_gain_ssm_out, m_w_out, m_norm_ffn_g, m_w_up, m_ffn_conv_w, m_ffn_conv_b, m_w_down, m_norm_final_g, v_meta_tokens, v_norm_mix_g, v_w_in, v_conv_w, v_ssm_lam_re, v_ssm_lam_im, v_ssm_log_dt, v_ssm_b_re, v_ssm_b_im, v_ssm_c_re, v_ssm_c_im, v_ssm_d, v_ssm_w_glu, v_gain_conv_out, v_gain_ssm_out, v_w_out, v_norm_ffn_g, v_w_up, v_ffn_conv_w, v_ffn_conv_b, v_w_down, v_norm_final_g):
    weights = dict(meta_tokens=meta_tokens, norm_mix_g=norm_mix_g, w_in=w_in, conv_w=conv_w, ssm_lam_re=ssm_lam_re, ssm_lam_im=ssm_lam_im, ssm_log_dt=ssm_log_dt, ssm_b_re=ssm_b_re, ssm_b_im=ssm_b_im, ssm_c_re=ssm_c_re, ssm_c_im=ssm_c_im, ssm_d=ssm_d, ssm_w_glu=ssm_w_glu, gain_conv_out=gain_conv_out, gain_ssm_out=gain_ssm_out, w_out=w_out, norm_ffn_g=norm_ffn_g, w_up=w_up, ffn_conv_w=ffn_conv_w, ffn_conv_b=ffn_conv_b, w_down=w_down, norm_final_g=norm_final_g)
    mom_m = dict(meta_tokens=m_meta_tokens, norm_mix_g=m_norm_mix_g, w_in=m_w_in, conv_w=m_conv_w, ssm_lam_re=m_ssm_lam_re, ssm_lam_im=m_ssm_lam_im, ssm_log_dt=m_ssm_log_dt, ssm_b_re=m_ssm_b_re, ssm_b_im=m_ssm_b_im, ssm_c_re=m_ssm_c_re, ssm_c_im=m_ssm_c_im, ssm_d=m_ssm_d, ssm_w_glu=m_ssm_w_glu, gain_conv_out=m_gain_conv_out, gain_ssm_out=m_gain_ssm_out, w_out=m_w_out, norm_ffn_g=m_norm_ffn_g, w_up=m_w_up, ffn_conv_w=m_ffn_conv_w, ffn_conv_b=m_ffn_conv_b, w_down=m_w_down, norm_final_g=m_norm_final_g)
    mom_v = dict(meta_tokens=v_meta_tokens, norm_mix_g=v_norm_mix_g, w_in=v_w_in, conv_w=v_conv_w, ssm_lam_re=v_ssm_lam_re, ssm_lam_im=v_ssm_lam_im, ssm_log_dt=v_ssm_log_dt, ssm_b_re=v_ssm_b_re, ssm_b_im=v_ssm_b_im, ssm_c_re=v_ssm_c_re, ssm_c_im=v_ssm_c_im, ssm_d=v_ssm_d, ssm_w_glu=v_ssm_w_glu, gain_conv_out=v_gain_conv_out, gain_ssm_out=v_gain_ssm_out, w_out=v_w_out, norm_ffn_g=v_norm_ffn_g, w_up=v_w_up, ffn_conv_w=v_ffn_conv_w, ffn_conv_b=v_ffn_conv_b, w_down=v_w_down, norm_final_g=v_norm_final_g)
    names = list(weights)

    n_meta, d_meta = meta_tokens.shape
    seq, d = x.shape[1], x.shape[2]
    rows_used = n_meta + seq
    t = -(-rows_used // ROW_TILE) * ROW_TILE
    d_in_s = w_in.shape[2]
    dc_s = conv_w.shape[2]
    dc = dc_s * N_DEV
    ds = ssm_w_glu.shape[2]
    n_groups, n_state, grp = ssm_b_re.shape[1:]
    ns = n_groups * n_state
    nch = ds // SSM_CHUNK
    gpc = n_groups // nch
    ff_s = w_up.shape[2]
    dn_s = w_down.shape[1]
    assert 3 * dc + ds == d_in_s * N_DEV and 2 * dn_s == ff_s and t % (N_SEG * SUBLANES) == 0

    small_shard = jnp.concatenate([meta_tokens.reshape(-1), conv_w.reshape(-1), ffn_conv_w.reshape(-1)])
    n_small = small_shard.shape[0]
    small_rows = -(-n_small // LANES)
    small_shard = jnp.pad(small_shard, (0, small_rows * LANES - n_small)).reshape(small_rows, LANES)
    ag, ag_token = _gather2_start("gather_weights_start", [
        small_shard, w_in[0].astype(BF16), ssm_w_glu[0].astype(BF16), w_out[0].astype(BF16),
        jnp.swapaxes(w_up[0], 0, 1).astype(BF16), w_down[0].astype(BF16)])
    fb = ffn_conv_b.reshape(N_DEV, 1, ff_s)

    gh = n_groups * grp
    per_h = lambda a: jnp.broadcast_to(a.reshape(n_groups, 1, -1), (n_groups, grp, n_state)).reshape(gh, n_state)
    ghp = lambda a: a.transpose(0, 1, 3, 2).reshape(gh, n_state)
    lr, li, log_dt_e = per_h(ssm_lam_re), per_h(ssm_lam_im), per_h(ssm_log_dt)
    br, bi = ghp(ssm_b_re), ghp(ssm_b_im)
    a_re, a_im, bb_re, bb_im = _disc_fwd(lr, li, log_dt_e, br, bi, deps=(ag_token,))
    cs = gpc * n_state
    chunk_row = lambda a: a.reshape(n_groups, grp, n_state)[:, 0].reshape(nch, 1, cs)
    to_bb = lambda a: _block_diag(a.reshape(nch, gpc, grp, n_state)).astype(BF16)
    to_cc = lambda a: _block_diag(a.reshape(nch, gpc, grp, n_state).transpose(0, 1, 3, 2)).astype(BF16)
    bbm_re, bbm_im = to_bb(bb_re), to_bb(bb_im)
    ccm_re, ccm_im = to_cc(ssm_c_re), to_cc(ssm_c_im)
    a_re_c, a_im_c = chunk_row(a_re), chunk_row(a_im)
    d_skip = ssm_d.reshape(nch, 1, SSM_CHUNK)

    token = _gather2_forward("gather_weights_forward_first", ag, (0, 1), bbm_im)
    (g_small,) = _gather2_wait("gather_weights_wait_small", ag, (0,), token)
    g_small = g_small.reshape(N_DEV, -1)
    o1 = n_meta * d_meta
    o2 = o1 + 3 * dc_s
    meta_full = g_small[:, :o1].reshape(N_DEV, n_meta, d_meta).transpose(1, 0, 2).reshape(n_meta, d)
    conv_w_f = g_small[:, o1:o2].reshape(N_DEV, 3, dc_s).transpose(1, 0, 2).reshape(3, dc)
    fw = g_small[:, o2:o2 + 3 * ff_s].reshape(N_DEV, 3, ff_s)
    h0 = _to_segments(jnp.concatenate([meta_full, x[0], jnp.zeros((t - rows_used, d), F32)], axis=0))
    target = _to_segments(jnp.pad(loss_target[0], ((n_meta, t - rows_used), (0, 0))))
    full_t = lambda w: pl.BlockSpec((t, w), lambda *_: (0, 0))

    hn1 = _rms_fwd("norm_mix", h0, norm_mix_g)
    (g_in,) = _gather2_wait("gather_weights_wait_in", ag, (1,), hn1)
    proj = _mm("proj", hn1, g_in, dims=NN, grid=(N_DEV,), a_spec=full_t(d),
               b_spec=pl.BlockSpec((None, d, d_in_s), lambda j: (j, 0, 0)),
               o_spec=pl.BlockSpec((t, d_in_s), lambda j: (0, j)),
               out_shape=jax.ShapeDtypeStruct((t, N_DEV * d_in_s), F32))
    s_re, s_im, y_ssm = _s5_fwd(proj, 3 * dc, bbm_re, bbm_im, a_re_c, a_im_c, ccm_re, ccm_im, d_skip)
    token = _gather2_forward("gather_weights_forward_up", ag, (2, 3, 4), y_ssm)
    g_glu, g_out = _gather2_wait("gather_weights_wait_mix", ag, (2, 3), token)
    w_out_f = g_out.reshape(-1, d)
    w_glu_f = g_glu.reshape(ds, ds)
    mixed, z_glu = _mix_fwd(proj, y_ssm, w_glu_f, conv_w_f, gain_conv_out, gain_ssm_out)
    tn_out = 256
    h1 = _mm("out_proj", mixed, w_out_f, dims=NN, grid=(d // tn_out,), a_spec=full_t(dc + ds),
             b_spec=pl.BlockSpec((dc + ds, tn_out), lambda j: (0, j)),
             o_spec=pl.BlockSpec((t, tn_out), lambda j: (0, j)),
             out_shape=jax.ShapeDtypeStruct((t, d), F32),
             res=h0, res_spec=pl.BlockSpec((t, tn_out), lambda j: (0, j)))
    hn2 = _rms_fwd("norm_ffn", h1, norm_ffn_g)
    token = _gather2_forward("gather_weights_forward_down", ag, (5,), hn2)
    (g_up,) = _gather2_wait("gather_weights_wait_up", ag, (4,), token)
    up_pre = _mm("up_proj", hn2, g_up, dims=NT, grid=(N_DEV,), a_spec=full_t(d),
                 b_spec=pl.BlockSpec((None, ff_s, d), lambda j: (j, 0, 0)),
                 o_spec=pl.BlockSpec((None, t, ff_s), lambda j: (j, 0, 0)),
                 out_shape=jax.ShapeDtypeStruct((N_DEV, t, ff_s), F32))
    act = _ffn_fwd(up_pre, fw, fb)
    nhalf = N_DEV // 2
    (g_down,) = _gather2_wait("gather_weights_wait_down", ag, (5,), act)
    w_down_f = g_down.reshape(nhalf, 2 * dn_s, d)
    tn_w = 512
    h2 = _mm("down_proj", act, w_down_f, dims=NN, grid=(d // tn_w, nhalf),
             a_spec=pl.BlockSpec((None, t, ff_s), lambda j, k: (k, 0, 0)),
             b_spec=pl.BlockSpec((None, ff_s, tn_w), lambda j, k: (k, 0, j)),
             o_spec=pl.BlockSpec((t, tn_w), lambda j, k: (0, j)),
             out_shape=jax.ShapeDtypeStruct((t, d), F32), acc_shape=(t, tn_w),
             res=h1, res_spec=pl.BlockSpec((t, tn_w), lambda j, k: (0, j)))

    dh2, dh2_b, loss_part, d_norm_final = _loss_bwd(h2, target, norm_final_g.reshape(1, d), n_meta, rows_used)
    dact = _mm("down_dgrad", dh2_b, w_down_f, dims=NT, grid=(nhalf,), a_spec=full_t(d),
               b_spec=pl.BlockSpec((None, ff_s, d), lambda j: (j, 0, 0)),
               o_spec=pl.BlockSpec((None, t, ff_s), lambda j: (j, 0, 0)),
               out_shape=jax.ShapeDtypeStruct((nhalf, t, ff_s), F32))
    tn_w = 512
    dw_down = _mm("down_wgrad", act, dh2_b, dims=TN, grid=(nhalf,),
                  a_spec=pl.BlockSpec((None, t, ff_s), lambda j: (j, 0, 0)), b_spec=full_t(d),
                  o_spec=pl.BlockSpec((None, ff_s, d), lambda j: (j, 0, 0)),
                  out_shape=jax.ShapeDtypeStruct((nhalf, ff_s, d), BF16))
    ex_down, token = _send_start("exchange_down_start", [dw_down.reshape(N_DEV, dn_s, d)], gather=False)
    dup, d_fw, d_fb = _ffn_bwd1(up_pre, dact, fw, fb, deps=(token,))
    dup_pre = _ffn_bwd2(dup, fw)
    dhn2 = _mm("up_dgrad", dup_pre, g_up, dims=NN, grid=(1, N_DEV),
               a_spec=pl.BlockSpec((None, t, ff_s), lambda i, k: (k, 0, 0)),
               b_spec=pl.BlockSpec((None, ff_s, d), lambda i, k: (k, 0, 0)),
               o_spec=pl.BlockSpec((t, d), lambda i, k: (0, 0)),
               out_shape=jax.ShapeDtypeStruct((t, d), F32), acc_shape=(t, d))
    dw_up = _mm("up_wgrad", dup_pre, hn2, dims=TN, grid=(N_DEV,),
                a_spec=pl.BlockSpec((None, t, ff_s), lambda j: (j, 0, 0)), b_spec=full_t(d),
                o_spec=pl.BlockSpec((None, ff_s, d), lambda j: (j, 0, 0)),
                out_shape=jax.ShapeDtypeStruct((N_DEV, ff_s, d), BF16))
    ex_up, token = _send_start("exchange_up_start", [dw_up], gather=False)
    dh1, dh1_b, d_norm_ffn = _rms_bwd_res("norm_ffn_bwd", dh2, dhn2, h1, norm_ffn_g, deps=(token,))
    dmixed = _mm("out_dgrad", dh1_b, w_out_f, dims=NT, grid=((dc + ds) // tn_out,), a_spec=full_t(d),
                 b_spec=pl.BlockSpec((tn_out, d), lambda i: (i, 0)),
                 o_spec=pl.BlockSpec((t, tn_out), lambda i: (0, i)),
                 out_shape=jax.ShapeDtypeStruct((t, dc + ds), F32))
    dw_out = _mm("out_wgrad", mixed, dh1_b, dims=TN, grid=((dc + ds) // tn_out,),
                 a_spec=pl.BlockSpec((t, tn_out), lambda i: (0, i)), b_spec=full_t(d),
                 o_spec=pl.BlockSpec((tn_out, d), lambda i: (i, 0)),
                 out_shape=jax.ShapeDtypeStruct((dc + ds, d), BF16))
    db_gate, dconv, dy_ssm, d_wglu, d_conv_w, d_gain_c, d_gain_s = _mix_bwd1(
        proj, y_ssm, z_glu, dmixed, w_glu_f, conv_w_f, gain_conv_out, gain_ssm_out)
    (du, d_bbm_re, d_bbm_im, d_a_re, d_a_im, d_ccm_re, d_ccm_im, d_dskip) = _s5_bwd(
        dy_ssm, proj, 3 * dc, s_re, s_im, bbm_re, bbm_im, a_re_c, a_im_c, ccm_re, ccm_im, d_skip)

    from_bb = lambda a: _diag_blocks(a, gpc).reshape(gh, n_state)
    from_cc = lambda a: _diag_blocks(a, gpc).transpose(0, 1, 3, 2).reshape(gh, n_state)
    first_h = lambda a: jnp.pad(a.reshape(n_groups, 1, n_state), ((0, 0), (0, grp - 1), (0, 0))).reshape(gh, n_state)
    over_h = lambda a: a.reshape(n_groups, grp, n_state).sum(axis=1)
    d_lr, d_li, d_dt_e, d_br, d_bi = _disc_bwd(
        lr, li, log_dt_e, br, bi, first_h(d_a_re), first_h(d_a_im), from_bb(d_bbm_re), from_bb(d_bbm_im))

    rep2d = dict(
        ssm_lam_re=(n_groups, n_state), ssm_lam_im=(n_groups, n_state), ssm_log_dt=(1, n_groups),
        ssm_b_re=(gh, n_state), ssm_b_im=(gh, n_state), ssm_c_re=(gh, n_state),
        ssm_c_im=(gh, n_state), ssm_d=(n_groups, grp), gain_conv_out=(1, dc),
        gain_ssm_out=(1, ds), norm_ffn_g=(1, d), ffn_conv_b=(1, N_DEV * ff_s), norm_final_g=(1, d))
    rep_names = list(rep2d)
    rep_grads = dict(
        ssm_lam_re=over_h(d_lr), ssm_lam_im=over_h(d_li), ssm_log_dt=over_h(d_dt_e).sum(axis=1),
        ssm_b_re=d_br, ssm_b_im=d_bi, ssm_c_re=from_cc(d_ccm_re), ssm_c_im=from_cc(d_ccm_im),
        ssm_d=d_dskip, gain_conv_out=d_gain_c, gain_ssm_out=d_gain_s, norm_ffn_g=d_norm_ffn,
        ffn_conv_b=d_fb, norm_final_g=d_norm_final)
    rep_shapes = [rep2d[n] for n in rep_names] + [(1, 1)]
    rep_rows = _packed_rows(rep_shapes)
    rep_pack = _pack([rep_grads[n] for n in rep_names] + [loss_part], rep_rows)
    ex_mix, token = _send_start("exchange_mix_start", [
        dw_out.reshape(N_DEV, -1, d), d_wglu.astype(BF16).reshape(N_DEV, -1, ds),
        d_conv_w.reshape(3, N_DEV, dc_s).transpose(1, 0, 2), d_fw, rep_pack],
        gather=[False, False, False, False, True])
    dcdv = _mix_bwd2(proj, dconv, conv_w_f, deps=(token,))
    dproj = jnp.concatenate([db_gate, dcdv, du], axis=1)
    dhn1 = _mm("proj_dgrad", dproj, g_in, dims=NT, grid=(1, N_DEV),
               a_spec=pl.BlockSpec((t, d_in_s), lambda i, k: (0, k)),
               b_spec=pl.BlockSpec((None, d, d_in_s), lambda i, k: (k, 0, 0)),
               o_spec=pl.BlockSpec((t, d), lambda i, k: (0, 0)),
               out_shape=jax.ShapeDtypeStruct((t, d), F32), acc_shape=(t, d))
    dh0, _, d_norm_mix = _rms_bwd_res("norm_mix_bwd", dh1, dhn1, h0, norm_mix_g)
    dw_in = _mm("proj_wgrad", hn1, dproj, dims=TN, grid=(N_DEV,), a_spec=full_t(d),
                b_spec=pl.BlockSpec((t, d_in_s), lambda j: (0, j)),
                o_spec=pl.BlockSpec((None, d, d_in_s), lambda j: (j, 0, 0)),
                out_shape=jax.ShapeDtypeStruct((N_DEV, d, d_in_s), BF16))
    dh0 = _from_segments(dh0)
    grad_x = dh0[n_meta:rows_used][None]
    d_meta_b = dh0[:n_meta].reshape(n_meta, N_DEV, d_meta).transpose(1, 0, 2)
    ex_in, ex_in_token = _send_start("exchange_in_start", [dw_in, d_meta_b, d_norm_mix],
                                     gather=[False, False, True])

    shard_out = {}
    (rep_parts,) = _send_wait("gather_small_grads_wait", ex_mix, (4,), ex_in_token)
    rep_sum = _sum_parts("sum_small_grads", rep_parts)
    *rep_g, loss = _unpack(rep_sum, rep_shapes)
    loss = loss.reshape(())
    swapped = ("ssm_b_re", "ssm_b_im")
    to2d = lambda n, a: ghp(a) if n in swapped else a.reshape(rep2d[n])
    from2d = lambda n, a: (a.reshape(1, n_groups, grp, n_state).transpose(0, 1, 3, 2) if n in swapped
                           else a.reshape(weights[n].shape))
    as2d = lambda tree: [to2d(n, tree[n]) for n in rep_names]
    rep_res = _adamw_many("adamw_replicated", rep_g, as2d(weights), as2d(mom_m), as2d(mom_v))
    for i, n in enumerate(rep_names):
        shard_out[n] = [from2d(n, r) for r in (rep_g[i], *(res[i] for res in rep_res))]

    def update(n, parts, transposed=False):
        sh = weights[n].shape
        two_d = lambda a: (jnp.swapaxes(a[0], 0, 1) if transposed else a.reshape(parts.shape[1:]))
        res = _adamw_reduce("adamw_" + n, parts, two_d(weights[n]), two_d(mom_m[n]), two_d(mom_v[n]))
        shard_out[n] = [(jnp.swapaxes(r, 0, 1) if transposed else r).reshape(sh) for r in res]
        return res[0]

    (p_down,) = _send_wait("exchange_down_wait", ex_down, (0,), rep_sum)
    done = update("w_down", p_down)
    (p_up,) = _send_wait("exchange_up_wait", ex_up, (0,), done)
    done = update("w_up", p_up, transposed=True)
    p_out, p_glu, p_cw, p_fw = _send_wait("exchange_mix_wait", ex_mix, (0, 1, 2, 3), done)
    update("w_out", p_out)
    update("ssm_w_glu", p_glu)
    update("conv_w", p_cw)
    done = update("ffn_conv_w", p_fw)
    p_in, p_meta, p_nm = _send_wait("exchange_in_wait", ex_in, (0, 1, 2), done)
    update("w_in", p_in)
    update("meta_tokens", p_meta)
    update("norm_mix_g", p_nm)

    grads = [shard_out[n][0] for n in names]
    deltas = [shard_out[n][1] for n in names]
    new_m = [shard_out[n][2] for n in names]
    new_v = [shard_out[n][3] for n in names]
    return (loss, grad_x, *grads, *deltas, *new_m, *new_v)
```

```python
import functools
import math

import jax
import jax.numpy as jnp
from jax import lax
from jax.experimental import pallas as pl
from jax.experimental.pallas import tpu as pltpu

F32 = jnp.float32
BF16 = jnp.bfloat16
MESH = pl.DeviceIdType.MESH

N_DEV = 8
RMS_EPS = 1e-6
ADAM_LR = 0.001
ADAM_B1 = 0.9
ADAM_B2 = 0.999
ADAM_EPS = 1e-08
ADAM_WD = 0.01
ADAM_STEP = 10
ADAM_BC1 = 1.0 - ADAM_B1 ** ADAM_STEP
ADAM_BC2 = 1.0 - ADAM_B2 ** ADAM_STEP

SUBLANES = 8
LANES = 128
ROW_TILE = 128
ROW_CHUNK = 32
WIDE_TILES = (544, 256, 128)
N_SEG = 8
HALO_ROWS = 16
SSM_CHUNK = 128
VMEM_LIMIT = 48 * 1024 * 1024

NN = ((1,), (0,))
NT = ((1,), (1,))
TN = ((0,), (0,))


def _params(**kw):
    return pltpu.CompilerParams(vmem_limit_bytes=VMEM_LIMIT, **kw)


def _dot(a, b, dims):
    return lax.dot_general(a, b, (dims, ((), ())), preferred_element_type=F32)


def _mean_sq_rsqrt(x):
    return lax.rsqrt(jnp.mean(x * x, axis=-1, keepdims=True) + RMS_EPS)


def _rms_bwd(x, r, g, dy):
    xhat = x * r
    dxh = dy * g
    dx = r * (dxh - xhat * jnp.mean(dxh * xhat, axis=-1, keepdims=True))
    return dx, dy * xhat


def _gelu(y):
    c = math.sqrt(2.0 / math.pi)
    t = jnp.tanh(c * (y + 0.044715 * y * y * y))
    return 0.5 * y * (1.0 + t), t


def _gelu_grad(y, t):
    c = math.sqrt(2.0 / math.pi)
    return 0.5 * (1.0 + t) + 0.5 * y * (1.0 - t * t) * c * (1.0 + 3.0 * 0.044715 * y * y)


def _wrap_prev_halo(halo, first):
    seg = lax.broadcasted_iota(jnp.int32, halo.shape, 0) % N_SEG
    wrapped = jnp.where(seg == 0, 0.0, pltpu.roll(halo, 1, 0))
    return jnp.where(first, wrapped, halo)


def _wrap_next_halo(halo, last):
    seg = lax.broadcasted_iota(jnp.int32, halo.shape, 0) % N_SEG
    wrapped = jnp.where(seg == N_SEG - 1, 0.0, pltpu.roll(halo, halo.shape[0] - 1, 0))
    return jnp.where(last, wrapped, halo)


def _tile_rows(get, prev, nxt, s, n, tile_rows=ROW_TILE):
    parts = []
    if s < 0:
        parts.append(prev[HALO_ROWS + s:HALO_ROWS + min(s + n, 0)])
    lo, hi = max(s, 0), min(s + n, tile_rows)
    if hi > lo:
        parts.append(get(lo, hi - lo))
    if s + n > tile_rows:
        parts.append(nxt[max(s - tile_rows, 0):s + n - tile_rows])
    return parts[0] if len(parts) == 1 else jnp.concatenate(parts, axis=0)


def _dev_index(p):
    return 4 * p[0] + 2 * p[1] + p[2]


def _allgather(name, shards, deps=()):
    n = len(shards)

    def body(*refs):
        ins, outs = refs[:n], refs[n:2 * n]
        send_sems, recv_sems, local_sems = refs[2 * n:]
        x, y, c = lax.axis_index("x"), lax.axis_index("y"), lax.axis_index("c")
        me, sibling = (x, y, c), (x, y, 1 - c)
        chips = [(1 - x, y), (x, 1 - y), (1 - x, 1 - y)]

        def copy(a, k, block, to, src=None):
            dst = outs[a].at[_dev_index(block)]
            return pltpu.make_async_remote_copy(
                src_ref=dst if src is None else src, dst_ref=dst,
                send_sem=send_sems.at[a, k], recv_sem=recv_sems.at[a, k],
                device_id=to, device_id_type=MESH)

        mine = [pltpu.make_async_copy(ins[a], outs[a].at[_dev_index(me)], local_sems.at[a])
                for a in range(n)]
        for cp in mine:
            cp.start()
        first = []
        for a in range(n):
            first.append(copy(a, 0, me, sibling, src=ins[a]))
            for j, chip in enumerate(chips):
                first.append(copy(a, 1 + j, me, (*chip, c), src=ins[a]))
        for cp in first:
            cp.start()
        passed = []
        for j, chip in enumerate(chips):
            for a in range(n):
                copy(a, 1 + j, (*chip, c), me).wait_recv()
                fwd = copy(a, 4 + j, (*chip, c), sibling)
                fwd.start()
                passed.append(fwd)
        for a in range(n):
            copy(a, 0, sibling, me).wait_recv()
            for j, chip in enumerate(chips):
                copy(a, 4 + j, (*chip, 1 - c), me).wait_recv()
        for cp in first + passed:
            cp.wait_send()
        for cp in mine:
            cp.wait()

    any_spec = pl.BlockSpec(memory_space=pl.ANY)
    return pl.pallas_call(
        _ignoring_deps(body, n, deps), name=name,
        out_shape=[jax.ShapeDtypeStruct((N_DEV,) + s.shape, s.dtype) for s in shards],
        in_specs=[any_spec] * (n + len(deps)), out_specs=[any_spec] * n,
        scratch_shapes=[pltpu.SemaphoreType.DMA((n, 7)), pltpu.SemaphoreType.DMA((n, 7)),
                        pltpu.SemaphoreType.DMA((n,))],
    )(*shards, *deps)


HBM_SPEC = pl.BlockSpec(memory_space=pltpu.HBM)
SEM_SPEC = pl.BlockSpec(memory_space=pltpu.SEMAPHORE)
ANY_SPEC = pl.BlockSpec(memory_space=pl.ANY)
DATAFLOW = pltpu.SideEffectType.DATAFLOW_SIDE_EFFECTING


def _ignoring_deps(body, n_in, deps):
    n_dep = len(deps)

    def wrapped(*refs):
        return body(*refs[:n_in], *refs[n_in + n_dep:])

    return wrapped


def _my_position():
    x, y, c = lax.axis_index("x"), lax.axis_index("y"), lax.axis_index("c")
    return (x, y, c)


def _peer(me, k):
    return tuple((1 - v) if (k >> s) & 1 else v for v, s in zip(me, (2, 1, 0)))


def _split_copies(src_refs, land_refs, send_sems, recv_sems, gather):
    me = _my_position()
    copies = []
    for a, (src, land) in enumerate(zip(src_refs, land_refs)):
        for k in range(1, N_DEV):
            peer = _peer(me, k)
            copies.append(pltpu.make_async_remote_copy(
                src_ref=src if gather[a] else src.at[_dev_index(peer)], dst_ref=land.at[_dev_index(me)],
                send_sem=send_sems[a].at[k - 1], recv_sem=recv_sems[a].at[k - 1],
                device_id=peer, device_id_type=MESH))
    return copies


def _own_slot(block, like_shape):
    me = _dev_index(_my_position())
    return lax.dynamic_update_index_in_dim(lax.empty(like_shape, block.dtype), block, me, 0)


def _send_start(name, srcs, gather):
    n = len(srcs)
    me = _dev_index(_my_position())
    gather = [gather] * n if isinstance(gather, bool) else list(gather)
    lands = [_own_slot(s, (N_DEV,) + s.shape) if g else
             _own_slot(lax.dynamic_index_in_dim(s, me, 0, keepdims=False), s.shape)
             for s, g in zip(srcs, gather)]

    def body(*refs):
        src_refs, land_refs = refs[:n], refs[n:2 * n]
        send_sems, recv_sems = refs[2 * n:3 * n], refs[3 * n:4 * n]
        token = refs[-1]
        for cp in _split_copies(src_refs, land_refs, send_sems, recv_sems, gather):
            cp.start()
        token[...] = jnp.zeros_like(token)

    hbm = lambda a: pltpu.HBM(a.shape, a.dtype)
    sems = [pltpu.SemaphoreType.DMA((N_DEV - 1,))] * n
    outs = pl.pallas_call(
        body, name=name,
        out_shape=(*sems, *sems, *[hbm(s) for s in srcs], *[hbm(l) for l in lands],
                   jax.ShapeDtypeStruct((SUBLANES, LANES), F32)),
        in_specs=[HBM_SPEC] * (2 * n),
        out_specs=(*[SEM_SPEC] * (2 * n), *[HBM_SPEC] * (2 * n), pl.BlockSpec(memory_space=pltpu.VMEM)),
        input_output_aliases={i: 2 * n + i for i in range(2 * n)},
        compiler_params=pltpu.CompilerParams(has_side_effects=DATAFLOW),
    )(*[pltpu.with_memory_space_constraint(a, pltpu.HBM) for a in (*srcs, *lands)])
    state = dict(send=outs[:n], recv=outs[n:2 * n], srcs=outs[2 * n:3 * n], lands=outs[3 * n:4 * n],
                 gather=gather)
    return state, outs[-1]


def _send_wait(name, state, which, after):
    n = len(which)
    pick = lambda key: [state[key][i] for i in which]
    gather = pick("gather")

    def body(*refs):
        src_refs, land_refs = refs[:n], refs[n:2 * n]
        send_sems, recv_sems = refs[2 * n:3 * n], refs[3 * n:4 * n]
        for cp in _split_copies(src_refs, land_refs, send_sems, recv_sems, gather):
            cp.wait_send()
            cp.wait_recv()

    srcs, lands = pick("srcs"), pick("lands")
    hbm = lambda a: pltpu.HBM(a.shape, a.dtype)
    outs = pl.pallas_call(
        body, name=name,
        out_shape=(*[hbm(s) for s in srcs], *[hbm(l) for l in lands]),
        in_specs=[*[HBM_SPEC] * (2 * n), *[SEM_SPEC] * (2 * n), ANY_SPEC],
        out_specs=tuple([HBM_SPEC] * (2 * n)),
        input_output_aliases={i: i for i in range(2 * n)},
        compiler_params=pltpu.CompilerParams(has_side_effects=DATAFLOW),
    )(*srcs, *lands, *pick("send"), *pick("recv"), after)
    return outs[n:]


def _two_level_copies(src_refs, land_refs, sems1, sems2):
    x, y, c = _my_position()
    me, sibling = (x, y, c), (x, y, 1 - c)
    chips = [(1 - x, y), (x, 1 - y), (1 - x, 1 - y)]
    stage1, stage2 = [], []
    for a, land in enumerate(land_refs):
        def copy(src, block, to, sems, k):
            return pltpu.make_async_remote_copy(
                src_ref=src, dst_ref=land.at[_dev_index(block)], send_sem=sems[0][a].at[k],
                recv_sem=sems[1][a].at[k], device_id=to, device_id_type=MESH)
        src = src_refs[a] if src_refs is not None else land.at[_dev_index(me)]
        stage1.append([copy(src, me, sibling, sems1, 0)] +
                      [copy(src, me, (*chip, c), sems1, 1 + j) for j, chip in enumerate(chips)])
        if sems2 is not None:
            stage2.append([copy(land.at[_dev_index((*chip, c))], (*chip, c), sibling, sems2, j)
                           for j, chip in enumerate(chips)])
    return stage1, stage2


def _gather2_start(name, shards):
    n = len(shards)
    lands = [_own_slot(s, (N_DEV,) + s.shape) for s in shards]

    def body(*refs):
        src_refs, land_refs = refs[:n], refs[n:2 * n]
        sems1 = (refs[2 * n:3 * n], refs[3 * n:4 * n])
        stage1, _ = _two_level_copies(src_refs, land_refs, sems1, None)
        for copies in stage1:
            for cp in copies:
                cp.start()
        refs[-1][...] = jnp.zeros_like(refs[-1])

    hbm = lambda a: pltpu.HBM(a.shape, a.dtype)
    sems = [pltpu.SemaphoreType.DMA((4,))] * n
    outs = pl.pallas_call(
        body, name=name,
        out_shape=(*sems, *sems, *[hbm(s) for s in shards], *[hbm(l) for l in lands],
                   jax.ShapeDtypeStruct((SUBLANES, LANES), F32)),
        in_specs=[HBM_SPEC] * (2 * n),
        out_specs=(*[SEM_SPEC] * (2 * n), *[HBM_SPEC] * (2 * n), pl.BlockSpec(memory_space=pltpu.VMEM)),
        input_output_aliases={i: 2 * n + i for i in range(2 * n)},
        compiler_params=pltpu.CompilerParams(has_side_effects=DATAFLOW),
    )(*[pltpu.with_memory_space_constraint(a, pltpu.HBM) for a in (*shards, *lands)])
    state = dict(send1=list(outs[:n]), recv1=list(outs[n:2 * n]), srcs=list(outs[2 * n:3 * n]),
                 lands=list(outs[3 * n:4 * n]), send2={}, recv2={})
    return state, outs[-1]


def _gather2_forward(name, state, which, after):
    n = len(which)
    pick = lambda key: [state[key][i] for i in which]

    def body(*refs):
        land_refs, recv1 = refs[:n], refs[n:2 * n]
        outs = refs[2 * n + len(after):]
        sems2 = (outs[:n], outs[n:2 * n])
        stage1, stage2 = _two_level_copies(None, land_refs, (recv1, recv1), sems2)
        for a in range(n):
            for j in range(3):
                stage1[a][1 + j].wait_recv()
                stage2[a][j].start()
        outs[-1][...] = jnp.zeros_like(outs[-1])

    lands = pick("lands")
    sems = [pltpu.SemaphoreType.DMA((3,))] * n
    outs = pl.pallas_call(
        body, name=name,
        out_shape=(*sems, *sems, *[pltpu.HBM(l.shape, l.dtype) for l in lands],
                   jax.ShapeDtypeStruct((SUBLANES, LANES), F32)),
        in_specs=[*[HBM_SPEC] * n, *[SEM_SPEC] * n, *[ANY_SPEC] * len(after)],
        out_specs=(*[SEM_SPEC] * (2 * n), *[HBM_SPEC] * n, pl.BlockSpec(memory_space=pltpu.VMEM)),
        input_output_aliases={i: 2 * n + i for i in range(n)},
        compiler_params=pltpu.CompilerParams(has_side_effects=DATAFLOW),
    )(*lands, *pick("recv1"), *after)
    for idx, i in enumerate(which):
        state["send2"][i], state["recv2"][i] = outs[idx], outs[n + idx]
        state["lands"][i] = outs[2 * n + idx]
    return outs[-1]


def _gather2_wait(name, state, which, after):
    n = len(which)
    pick = lambda key: [state[key][i] for i in which]

    def body(*refs):
        src_refs, land_refs = refs[:n], refs[n:2 * n]
        sems1 = (refs[2 * n:3 * n], refs[3 * n:4 * n])
        sems2 = (refs[4 * n:5 * n], refs[5 * n:6 * n])
        stage1, stage2 = _two_level_copies(src_refs, land_refs, sems1, sems2)
        for a in range(n):
            for cp in stage1[a]:
                cp.wait_send()
            stage1[a][0].wait_recv()
            for cp in stage2[a]:
                cp.wait_send()
                cp.wait_recv()

    srcs, lands = pick("srcs"), pick("lands")
    hbm = lambda a: pltpu.HBM(a.shape, a.dtype)
    outs = pl.pallas_call(
        body, name=name,
        out_shape=(*[hbm(s) for s in srcs], *[hbm(l) for l in lands]),
        in_specs=[*[HBM_SPEC] * (2 * n), *[SEM_SPEC] * (4 * n), ANY_SPEC],
        out_specs=tuple([HBM_SPEC] * (2 * n)),
        input_output_aliases={i: i for i in range(2 * n)},
        compiler_params=pltpu.CompilerParams(has_side_effects=DATAFLOW),
    )(*srcs, *lands, *pick("send1"), *pick("recv1"), *pick("send2"), *pick("recv2"), after)
    return outs[n:]


def _mm(name, a, b, *, dims, grid, a_spec, b_spec, o_spec, out_shape, acc_shape=None,
        res=None, res_spec=None):
    n_red = grid[-1] if acc_shape is not None else 1
    red_axis = len(grid) - 1

    def body(*refs):
        a_ref, b_ref = refs[0], refs[1]
        r_ref = refs[2] if res is not None else None
        o_ref = refs[3] if res is not None else refs[2]
        part = _dot(a_ref[...], b_ref[...], dims)
        if acc_shape is None:
            if r_ref is not None:
                part = part + r_ref[...]
            o_ref[...] = part.astype(o_ref.dtype)
            return
        acc_ref = refs[-1]
        k = pl.program_id(red_axis)

        @pl.when(k == 0)
        def _():
            acc_ref[...] = part

        @pl.when(k > 0)
        def _():
            acc_ref[...] += part

        @pl.when(k == n_red - 1)
        def _():
            total = acc_ref[...]
            if r_ref is not None:
                total = total + r_ref[...]
            o_ref[...] = total.astype(o_ref.dtype)

    ins, in_specs = [a, b], [a_spec, b_spec]
    if res is not None:
        ins.append(res)
        in_specs.append(res_spec)
    return pl.pallas_call(
        body, name=name, grid=grid, in_specs=in_specs, out_specs=o_spec, out_shape=out_shape,
        scratch_shapes=[pltpu.VMEM(acc_shape, F32)] if acc_shape is not None else [],
        compiler_params=_params(),
    )(*ins)


def _wide_tile(t):
    return next(c for c in WIDE_TILES if t % c == 0)


def _rms_fwd(name, h, g):
    t, d = h.shape
    tile = _wide_tile(t)

    def body(h_ref, g_ref, o_ref):
        x = h_ref[...]
        o_ref[...] = (x * _mean_sq_rsqrt(x) * g_ref[...]).astype(BF16)

    return pl.pallas_call(
        body, name=name, grid=(t // tile,),
        in_specs=[pl.BlockSpec((tile, d), lambda i: (i, 0)), pl.BlockSpec((1, d), lambda i: (0, 0))],
        out_specs=pl.BlockSpec((tile, d), lambda i: (i, 0)),
        out_shape=jax.ShapeDtypeStruct((t, d), BF16), compiler_params=_params(),
    )(h, g)


def _rms_bwd_res(name, dres, dhn, h, g, deps=()):
    t, d = h.shape

    def body(dres_ref, dhn_ref, h_ref, g_ref, dh_ref, dhb_ref, dg_ref):
        x = h_ref[...]
        dx, dgt = _rms_bwd(x, _mean_sq_rsqrt(x), g_ref[...], dhn_ref[...])
        dh = dres_ref[...] + dx
        dh_ref[...] = dh
        dhb_ref[...] = dh.astype(BF16)

        @pl.when(pl.program_id(0) == 0)
        def _():
            dg_ref[...] = jnp.zeros_like(dg_ref)

        dg_ref[...] += jnp.sum(dgt, axis=0, keepdims=True)

    tile = _wide_tile(t)
    row = pl.BlockSpec((tile, d), lambda i: (i, 0))
    vec = pl.BlockSpec((1, d), lambda i: (0, 0))
    return pl.pallas_call(
        _ignoring_deps(body, 4, deps), name=name, grid=(t // tile,),
        in_specs=[row, row, row, vec] + [ANY_SPEC] * len(deps), out_specs=[row, row, vec],
        out_shape=[jax.ShapeDtypeStruct((t, d), F32), jax.ShapeDtypeStruct((t, d), BF16),
                   jax.ShapeDtypeStruct((1, d), F32)],
        compiler_params=_params(),
    )(dres, dhn, h, g, *deps)


def _loss_bwd(h2, target, g, row_lo, row_hi):
    t, d = h2.shape
    tile = _wide_tile(t)

    def body(h_ref, tg_ref, g_ref, dh_ref, dhb_ref, loss_ref, dg_ref):
        i = pl.program_id(0)
        x = h_ref[...]
        r = _mean_sq_rsqrt(x)
        gv = g_ref[...]
        y = x * r * gv
        rows = i * tile + lax.broadcasted_iota(jnp.int32, (tile, 1), 0)
        time = (rows % N_SEG) * (t // N_SEG) + rows // N_SEG
        valid = jnp.logical_and(time >= row_lo, time < row_hi)
        err = jnp.where(valid, y - tg_ref[...], 0.0)
        dy = err * (1.0 / d)
        dx, dgt = _rms_bwd(x, r, gv, dy)
        dh_ref[...] = dx
        dhb_ref[...] = dx.astype(BF16)

        @pl.when(i == 0)
        def _():
            loss_ref[...] = jnp.zeros_like(loss_ref)
            dg_ref[...] = jnp.zeros_like(dg_ref)

        row_loss = jnp.mean(err * err, axis=-1, keepdims=True)
        loss_ref[...] += 0.5 * jnp.sum(row_loss, axis=0, keepdims=True)
        dg_ref[...] += jnp.sum(dgt, axis=0, keepdims=True)

    row = pl.BlockSpec((tile, d), lambda i: (i, 0))
    vec = pl.BlockSpec((1, d), lambda i: (0, 0))
    return pl.pallas_call(
        body, name="loss_bwd", grid=(t // tile,), in_specs=[row, row, vec],
        out_specs=[row, row, pl.BlockSpec((1, 1), lambda i: (0, 0)), vec],
        out_shape=[jax.ShapeDtypeStruct((t, d), F32), jax.ShapeDtypeStruct((t, d), BF16),
                   jax.ShapeDtypeStruct((1, 1), F32), jax.ShapeDtypeStruct((1, d), F32)],
        compiler_params=_params(),
    )(h2, target, g)


def _prev_halo(i, t):
    return jnp.where(i == 0, t // HALO_ROWS - 1, i * (ROW_TILE // HALO_ROWS) - 1)


def _next_halo(i, t):
    return jnp.where(i == t // ROW_TILE - 1, 0, (i + 1) * (ROW_TILE // HALO_ROWS))


def _causal_taps(cur, halo, first):
    rows = cur.shape[0]
    ext = jnp.concatenate([_wrap_prev_halo(halo, first), cur], axis=0)
    return ext[:rows], ext[N_SEG:N_SEG + rows]


def _anticausal_taps(cur, halo, last):
    rows = cur.shape[0]
    ext = jnp.concatenate([cur, _wrap_next_halo(halo, last)], axis=0)
    return ext[N_SEG:N_SEG + rows], ext[2 * N_SEG:2 * N_SEG + rows]


def _mix_fwd(proj, y, w_glu, conv_w, gain_c, gain_s):
    t = proj.shape[0]
    dc = conv_w.shape[1]
    ds = y.shape[1]

    def body(p_ref, halo_ref, y_ref, wg_ref, cw_ref, gc_ref, gs_ref, mixed_ref, z_ref):
        i = pl.program_id(0)
        p = p_ref[...]
        b, c, v = p[:, :dc], p[:, dc:2 * dc], p[:, 2 * dc:3 * dc]
        cv = c * v
        hp = halo_ref[...]
        x2, x1 = _causal_taps(cv, hp[:, dc:2 * dc] * hp[:, 2 * dc:3 * dc], i == 0)
        cw = cw_ref[...]
        conv = cw[0:1] * x2 + cw[1:2] * x1 + cw[2:3] * cv
        co = b * conv
        mixed_ref[:, :dc] = (co * _mean_sq_rsqrt(co) * gc_ref[...]).astype(BF16)
        g, _ = _gelu(y_ref[...])
        z = _dot(g.astype(BF16), wg_ref[...], NN)
        z_ref[...] = z
        so = g * jax.nn.sigmoid(z)
        mixed_ref[:, dc:] = (so * _mean_sq_rsqrt(so) * gs_ref[...]).astype(BF16)

    const = lambda i: (0, 0)
    return pl.pallas_call(
        body, name="mix_fwd", grid=(t // ROW_TILE,),
        in_specs=[pl.BlockSpec((ROW_TILE, 3 * dc), lambda i: (i, 0)),
                  pl.BlockSpec((HALO_ROWS, 3 * dc), lambda i: (_prev_halo(i, t), 0)),
                  pl.BlockSpec((ROW_TILE, ds), lambda i: (i, 0)),
                  pl.BlockSpec((ds, ds), const), pl.BlockSpec(conv_w.shape, const),
                  pl.BlockSpec((1, dc), const), pl.BlockSpec((1, ds), const)],
        out_specs=[pl.BlockSpec((ROW_TILE, dc + ds), lambda i: (i, 0)),
                   pl.BlockSpec((ROW_TILE, ds), lambda i: (i, 0))],
        out_shape=[jax.ShapeDtypeStruct((t, dc + ds), BF16), jax.ShapeDtypeStruct((t, ds), F32)],
        compiler_params=_params(),
    )(proj, proj, y, w_glu, conv_w, gain_c, gain_s)


def _mix_bwd1(proj, y, z, dmixed, w_glu, conv_w, gain_c, gain_s):
    t = proj.shape[0]
    dc = conv_w.shape[1]
    ds = y.shape[1]

    def body(p_ref, halo_ref, y_ref, z_ref, dm_ref, wg_ref, cw_ref, gc_ref, gs_ref,
             db_ref, dconv_ref, dy_ref, dwg_ref, dcw_ref, dgc_ref, dgs_ref):
        i = pl.program_id(0)

        @pl.when(i == 0)
        def _():
            dwg_ref[...] = jnp.zeros_like(dwg_ref)
            dcw_ref[...] = jnp.zeros_like(dcw_ref)
            dgc_ref[...] = jnp.zeros_like(dgc_ref)
            dgs_ref[...] = jnp.zeros_like(dgs_ref)

        p = p_ref[...]
        b, c, v = p[:, :dc], p[:, dc:2 * dc], p[:, 2 * dc:3 * dc]
        cv = c * v
        hp = halo_ref[...]
        x2, x1 = _causal_taps(cv, hp[:, dc:2 * dc] * hp[:, 2 * dc:3 * dc], i == 0)
        cw = cw_ref[...]
        conv = cw[0:1] * x2 + cw[1:2] * x1 + cw[2:3] * cv
        co = b * conv
        dm = dm_ref[...]
        dco, dgt = _rms_bwd(co, _mean_sq_rsqrt(co), gc_ref[...], dm[:, :dc])
        dgc_ref[...] += jnp.sum(dgt, axis=0, keepdims=True)
        db_ref[...] = (dco * conv).astype(BF16)
        dconv = dco * b
        dconv_ref[...] = dconv
        dcw_ref[0:1, :] += jnp.sum(dconv * x2, axis=0, keepdims=True)
        dcw_ref[1:2, :] += jnp.sum(dconv * x1, axis=0, keepdims=True)
        dcw_ref[2:3, :] += jnp.sum(dconv * cv, axis=0, keepdims=True)

        yv = y_ref[...]
        g, th = _gelu(yv)
        sg = jax.nn.sigmoid(z_ref[...])
        so = g * sg
        dso, dgt = _rms_bwd(so, _mean_sq_rsqrt(so), gs_ref[...], dm[:, dc:])
        dgs_ref[...] += jnp.sum(dgt, axis=0, keepdims=True)
        dz = (dso * g * sg * (1.0 - sg)).astype(BF16)
        dg = dso * sg + _dot(dz, wg_ref[...], NT)
        dwg_ref[...] += _dot(g.astype(BF16), dz, TN)
        dy_ref[...] = dg * _gelu_grad(yv, th)

    const = lambda i: (0, 0)
    row = lambda w: pl.BlockSpec((ROW_TILE, w), lambda i: (i, 0))
    return pl.pallas_call(
        body, name="mix_bwd1", grid=(t // ROW_TILE,),
        in_specs=[row(3 * dc), pl.BlockSpec((HALO_ROWS, 3 * dc), lambda i: (_prev_halo(i, t), 0)),
                  row(ds), row(ds), row(dc + ds), pl.BlockSpec((ds, ds), const),
                  pl.BlockSpec(conv_w.shape, const), pl.BlockSpec((1, dc), const),
                  pl.BlockSpec((1, ds), const)],
        out_specs=[row(dc), row(dc), row(ds), pl.BlockSpec((ds, ds), const),
                   pl.BlockSpec(conv_w.shape, const), pl.BlockSpec((1, dc), const),
                   pl.BlockSpec((1, ds), const)],
        out_shape=[jax.ShapeDtypeStruct((t, dc), BF16), jax.ShapeDtypeStruct((t, dc), F32),
                   jax.ShapeDtypeStruct((t, ds), F32), jax.ShapeDtypeStruct((ds, ds), F32),
                   jax.ShapeDtypeStruct(conv_w.shape, F32), jax.ShapeDtypeStruct((1, dc), F32),
                   jax.ShapeDtypeStruct((1, ds), F32)],
        compiler_params=_params(),
    )(proj, proj, y, z, dmixed, w_glu, conv_w, gain_c, gain_s)


def _mix_bwd2(proj, dconv, conv_w, deps=()):
    t = proj.shape[0]
    dc = conv_w.shape[1]
    n_tiles = t // ROW_TILE

    def body(c_ref, v_ref, d_ref, halo_ref, cw_ref, o_ref):
        i = pl.program_id(0)
        d = d_ref[...]
        u1, u2 = _anticausal_taps(d, halo_ref[...], i == n_tiles - 1)
        cw = cw_ref[...]
        dcv = cw[2:3] * d + cw[1:2] * u1 + cw[0:1] * u2
        o_ref[:, :dc] = (dcv * v_ref[...]).astype(BF16)
        o_ref[:, dc:] = (dcv * c_ref[...]).astype(BF16)

    return pl.pallas_call(
        _ignoring_deps(body, 5, deps), name="mix_bwd2", grid=(n_tiles,),
        in_specs=[pl.BlockSpec((ROW_TILE, dc), lambda i: (i, 1)),
                  pl.BlockSpec((ROW_TILE, dc), lambda i: (i, 2)),
                  pl.BlockSpec((ROW_TILE, dc), lambda i: (i, 0)),
                  pl.BlockSpec((HALO_ROWS, dc), lambda i: (_next_halo(i, t), 0)),
                  pl.BlockSpec(conv_w.shape, lambda i: (0, 0))] + [ANY_SPEC] * len(deps),
        out_specs=pl.BlockSpec((ROW_TILE, 2 * dc), lambda i: (i, 0)),
        out_shape=jax.ShapeDtypeStruct((t, 2 * dc), BF16), compiler_params=_params(),
    )(proj, proj, dconv, dconv, conv_w, *deps)


def _ffn_taps(up_ref, prev, nxt, fw_ref, fb_ref, q, r, rows=ROW_CHUNK):
    get = lambda s, n: up_ref[q, pl.ds(s, n), :]
    x0, x1, x2 = (_tile_rows(get, prev, nxt, r - k * N_SEG, rows) for k in range(3))
    w = fw_ref[q]
    return w[0:1] * x2 + w[1:2] * x1 + w[2:3] * x0 + fb_ref[q], x2, x1, x0


def _ffn_fwd(up_pre, fw, fb):
    nb, t, w = up_pre.shape
    half = nb // 2

    def body(up_ref, halo_ref, fw_ref, fb_ref, act_ref):
        first = pl.program_id(0) == 0
        for q in range(half):
            halo_a = _wrap_prev_halo(halo_ref[q], first)
            halo_v = _wrap_prev_halo(halo_ref[q + half], first)
            for r in range(0, ROW_TILE, ROW_CHUNK):
                a = _ffn_taps(up_ref, halo_a, None, fw_ref, fb_ref, q, r)[0]
                val = _ffn_taps(up_ref, halo_v, None, fw_ref, fb_ref, q + half, r)[0]
                act_ref[q, pl.ds(r, ROW_CHUNK), :] = (a * jax.nn.sigmoid(a) * val).astype(BF16)

    return pl.pallas_call(
        body, name="ffn_fwd", grid=(t // ROW_TILE,),
        in_specs=[pl.BlockSpec((nb, ROW_TILE, w), lambda i: (0, i, 0)),
                  pl.BlockSpec((nb, HALO_ROWS, w), lambda i: (0, _prev_halo(i, t), 0)),
                  pl.BlockSpec(fw.shape, lambda i: (0, 0, 0)),
                  pl.BlockSpec(fb.shape, lambda i: (0, 0, 0))],
        out_specs=pl.BlockSpec((half, ROW_TILE, w), lambda i: (0, i, 0)),
        out_shape=jax.ShapeDtypeStruct((half, t, w), BF16), compiler_params=_params(),
    )(up_pre, up_pre, fw, fb)


def _ffn_bwd(up_pre, dact, fw, fb, deps=()):
    nb, t, w = up_pre.shape
    half = nb // 2
    n_tiles = t // ROW_TILE
    chunks = [(r, ROW_CHUNK) for r in range(0, ROW_TILE, ROW_CHUNK)] + [(ROW_TILE, HALO_ROWS)]

    def body(up_ref, prev_ref, next_ref, dact_ref, dact_next_ref, fw_ref, fb_ref,
             out_ref, dfw_ref, dfb_ref, dup_ref):
        first = pl.program_id(0) == 0
        last = pl.program_id(0) == n_tiles - 1

        @pl.when(first)
        def _():
            dfw_ref[...] = jnp.zeros_like(dfw_ref)
            dfb_ref[...] = jnp.zeros_like(dfb_ref)

        for q in range(half):
            blocks = (q, q + half)
            prev = [_wrap_prev_halo(prev_ref[b], first) for b in blocks]
            nxt = [_wrap_next_halo(next_ref[b], last) for b in blocks]
            dact_next = _wrap_next_halo(dact_next_ref[q].astype(F32), last)
            get_dact = lambda s, n: dact_ref[q, pl.ds(s, n), :].astype(F32)
            sums = [[jnp.zeros((SUBLANES, w), F32)] * 4 for _ in blocks]
            fold = lambda v: sum(v[s:s + SUBLANES] for s in range(0, ROW_CHUNK, SUBLANES))
            for r, rows in chunks:
                taps = [_ffn_taps(up_ref, prev[k], nxt[k], fw_ref, fb_ref, b, r, rows)
                        for k, b in enumerate(blocks)]
                a, val = taps[0][0], taps[1][0]
                da_ct = _tile_rows(get_dact, None, dact_next, r, rows)
                sg = jax.nn.sigmoid(a)
                dup = (da_ct * val * sg * (1.0 + a * (1.0 - sg)), da_ct * a * sg)
                for k in range(2):
                    dup_ref[k, pl.ds(r, rows), :] = dup[k]
                    if r < ROW_TILE:
                        terms = (dup[k], dup[k] * taps[k][1], dup[k] * taps[k][2], dup[k] * taps[k][3])
                        sums[k] = [s + fold(v) for s, v in zip(sums[k], terms)]
            for k, b in enumerate(blocks):
                s_b, s_w0, s_w1, s_w2 = (jnp.sum(p, axis=0, keepdims=True) for p in sums[k])
                dfb_ref[b] += s_b
                for tap, s_w in enumerate((s_w0, s_w1, s_w2)):
                    dfw_ref[b, tap:tap + 1, :] += s_w
                cw = fw_ref[b]
                for r in range(0, ROW_TILE, ROW_CHUNK):
                    d, u1, u2 = (dup_ref[k, pl.ds(r + s * N_SEG, ROW_CHUNK), :] for s in range(3))
                    out_ref[b, pl.ds(r, ROW_CHUNK), :] = (cw[2:3] * d + cw[1:2] * u1 + cw[0:1] * u2).astype(BF16)

    tile = lambda n: pl.BlockSpec((n, ROW_TILE, w), lambda i: (0, i, 0))
    halo = lambda n, index: pl.BlockSpec((n, HALO_ROWS, w), lambda i: (0, index(i, t), 0))
    const = lambda a: pl.BlockSpec(a.shape, lambda i: (0, 0, 0))
    return pl.pallas_call(
        _ignoring_deps(body, 7, deps), name="ffn_bwd", grid=(n_tiles,),
        in_specs=[tile(nb), halo(nb, _prev_halo), halo(nb, _next_halo), tile(half), halo(half, _next_halo),
                  const(fw), const(fb)] + [ANY_SPEC] * len(deps),
        out_specs=[tile(nb), const(fw), const(fb)],
        out_shape=[jax.ShapeDtypeStruct((nb, t, w), BF16), jax.ShapeDtypeStruct(fw.shape, F32),
                   jax.ShapeDtypeStruct(fb.shape, F32)],
        scratch_shapes=[pltpu.VMEM((2, ROW_TILE + HALO_ROWS, w), F32)],
        compiler_params=_params(),
    )(up_pre, up_pre, up_pre, dact, dact, fw, fb, *deps)


def _to_segments(a):
    t, c = a.shape
    return a.reshape(N_SEG, t // N_SEG, c).transpose(1, 0, 2).reshape(t, c)


def _from_segments(a):
    t, c = a.shape
    return a.reshape(t // N_SEG, N_SEG, c).transpose(1, 0, 2).reshape(t, c)


def _cmul(ar, ai, br, bi):
    return ar * br - ai * bi, ar * bi + ai * br


def _segment_carries(pr, pi, fr, fi, forward):
    row = lax.broadcasted_iota(jnp.int32, fr.shape, 0)
    edge = row == (0 if forward else N_SEG - 1)
    shift = 1 if forward else N_SEG - 1
    sr, si = jnp.zeros_like(fr), jnp.zeros_like(fi)
    for _ in range(N_SEG - 1):
        tr, ti = _cmul(pr, pi, sr, si)
        sr = jnp.where(edge, 0.0, pltpu.roll(tr + fr, shift, 0))
        si = jnp.where(edge, 0.0, pltpu.roll(ti + fi, shift, 0))
    return sr, si


def _rows(i):
    return pl.ds(pl.multiple_of(i * SUBLANES, SUBLANES), SUBLANES)


def _s5_fwd(proj, u_col, bb_re, bb_im, a_re, a_im, cc_re, cc_im, d_skip):
    t = proj.shape[0]
    nch, _, cs = bb_re.shape
    ds = nch * SSM_CHUNK
    u_blk = u_col // SSM_CHUNK
    steps = t // N_SEG

    def body(u_ref, bbr_ref, bbi_ref, ar_ref, ai_ref, ccr_ref, cci_ref, d_ref, sr_ref, si_ref, y_ref):
        ub = u_ref[...].astype(BF16)
        sr_ref[...] = _dot(ub, bbr_ref[...], NN)
        si_ref[...] = _dot(ub, bbi_ref[...], NN)
        ar = jnp.broadcast_to(ar_ref[...], (N_SEG, cs))
        ai = jnp.broadcast_to(ai_ref[...], (N_SEG, cs))
        zero = jnp.zeros((N_SEG, cs), F32)

        def totals(i, carry):
            sr, si, pr, pi = carry
            tr, ti = _cmul(ar, ai, sr, si)
            qr, qi = _cmul(ar, ai, pr, pi)
            return tr + sr_ref[_rows(i), :], ti + si_ref[_rows(i), :], qr, qi

        fr, fi, pr, pi = lax.fori_loop(0, steps, totals, (zero, zero, zero + 1.0, zero))
        s0r, s0i = _segment_carries(pr, pi, fr, fi, True)

        def scan(i, carry):
            tr, ti = _cmul(ar, ai, *carry)
            nr, ni = tr + sr_ref[_rows(i), :], ti + si_ref[_rows(i), :]
            sr_ref[_rows(i), :] = nr
            si_ref[_rows(i), :] = ni
            return nr, ni

        lax.fori_loop(0, steps, scan, (s0r, s0i))
        y_ref[...] = (_dot(sr_ref[...].astype(BF16), ccr_ref[...], NN)
                      - _dot(si_ref[...].astype(BF16), cci_ref[...], NN)
                      + d_ref[...] * u_ref[...])

    chunk3 = lambda r, c: pl.BlockSpec((None, r, c), lambda j: (j, 0, 0))
    return pl.pallas_call(
        body, name="s5_fwd", grid=(nch,),
        in_specs=[pl.BlockSpec((t, SSM_CHUNK), lambda j: (0, j + u_blk)),
                  chunk3(SSM_CHUNK, cs), chunk3(SSM_CHUNK, cs), chunk3(1, cs), chunk3(1, cs),
                  chunk3(cs, SSM_CHUNK), chunk3(cs, SSM_CHUNK), chunk3(1, SSM_CHUNK)],
        out_specs=[pl.BlockSpec((t, cs), lambda j: (0, j)), pl.BlockSpec((t, cs), lambda j: (0, j)),
                   pl.BlockSpec((t, SSM_CHUNK), lambda j: (0, j))],
        out_shape=[jax.ShapeDtypeStruct((t, nch * cs), F32), jax.ShapeDtypeStruct((t, nch * cs), F32),
                   jax.ShapeDtypeStruct((t, ds), F32)],
        compiler_params=_params(),
    )(proj, bb_re, bb_im, a_re, a_im, cc_re, cc_im, d_skip)


def _s5_bwd(dy, proj, u_col, s_re, s_im, bb_re, bb_im, a_re, a_im, cc_re, cc_im, d_skip):
    t, ds = dy.shape
    nch, _, cs = bb_re.shape
    u_blk = u_col // SSM_CHUNK
    steps = t // N_SEG

    def body(dy_ref, u_ref, sr_ref, si_ref, bbr_ref, bbi_ref, ar_ref, ai_ref, ccr_ref, cci_ref, d_ref,
             du_ref, dbbr_ref, dbbi_ref, dar_ref, dai_ref, dccr_ref, dcci_ref, dd_ref, gr_ref, gi_ref):
        dyv = dy_ref[...]
        dyb = dyv.astype(BF16)
        gr_ref[...] = _dot(dyb, ccr_ref[...], NT)
        gi_ref[...] = -_dot(dyb, cci_ref[...], NT)
        ar = jnp.broadcast_to(ar_ref[...], (N_SEG, cs))
        ai = -jnp.broadcast_to(ai_ref[...], (N_SEG, cs))
        zero = jnp.zeros((N_SEG, cs), F32)

        def totals(k, carry):
            i = steps - 1 - k
            gr, gi, pr, pi = carry
            tr, ti = _cmul(ar, ai, gr, gi)
            qr, qi = _cmul(ar, ai, pr, pi)
            return tr + gr_ref[_rows(i), :], ti + gi_ref[_rows(i), :], qr, qi

        fr, fi, pr, pi = lax.fori_loop(0, steps, totals, (zero, zero, zero + 1.0, zero))
        e0r, e0i = _segment_carries(pr, pi, fr, fi, False)

        def step(i, gr, gi, pr, pi, acc_r, acc_i):
            tr, ti = _cmul(ar, ai, gr, gi)
            nr, ni = tr + gr_ref[_rows(i), :], ti + gi_ref[_rows(i), :]
            gr_ref[_rows(i), :] = nr
            gi_ref[_rows(i), :] = ni
            return nr, ni, acc_r + nr * pr + ni * pi, acc_i + ni * pr - nr * pi

        def scan(k, carry):
            i = steps - 1 - k
            gr, gi, acc_r, acc_i = carry
            return step(i, gr, gi, sr_ref[_rows(i - 1), :], si_ref[_rows(i - 1), :], acc_r, acc_i)

        gr, gi, acc_r, acc_i = lax.fori_loop(0, steps - 1, scan, (e0r, e0i, zero, zero))
        row = lax.broadcasted_iota(jnp.int32, (N_SEG, cs), 0)
        last = _rows(steps - 1)
        pr = jnp.where(row == 0, 0.0, pltpu.roll(sr_ref[last, :], 1, 0))
        pi = jnp.where(row == 0, 0.0, pltpu.roll(si_ref[last, :], 1, 0))
        _, _, acc_r, acc_i = step(0, gr, gi, pr, pi, acc_r, acc_i)
        dar_ref[...] = jnp.sum(acc_r, axis=0, keepdims=True)
        dai_ref[...] = jnp.sum(acc_i, axis=0, keepdims=True)

        uv = u_ref[...]
        ub = uv.astype(BF16)
        grb = gr_ref[...].astype(BF16)
        gib = gi_ref[...].astype(BF16)
        du = d_ref[...] * dyv + _dot(grb, bbr_ref[...], NT) + _dot(gib, bbi_ref[...], NT)
        du_ref[...] = du.astype(BF16)
        dbbr_ref[...] = _dot(ub, grb, TN)
        dbbi_ref[...] = _dot(ub, gib, TN)
        dccr_ref[...] = _dot(sr_ref[...].astype(BF16), dyb, TN)
        dcci_ref[...] = -_dot(si_ref[...].astype(BF16), dyb, TN)
        dd_ref[...] = jnp.sum(dyv * uv, axis=0, keepdims=True)

    chunk3 = lambda r, c: pl.BlockSpec((None, r, c), lambda j: (j, 0, 0))
    cols = lambda w: pl.BlockSpec((t, w), lambda j: (0, j))
    return pl.pallas_call(
        body, name="s5_bwd", grid=(nch,),
        in_specs=[cols(SSM_CHUNK), pl.BlockSpec((t, SSM_CHUNK), lambda j: (0, j + u_blk)), cols(cs), cols(cs),
                  chunk3(SSM_CHUNK, cs), chunk3(SSM_CHUNK, cs), chunk3(1, cs), chunk3(1, cs),
                  chunk3(cs, SSM_CHUNK), chunk3(cs, SSM_CHUNK), chunk3(1, SSM_CHUNK)],
        out_specs=[cols(SSM_CHUNK), chunk3(SSM_CHUNK, cs), chunk3(SSM_CHUNK, cs), chunk3(1, cs),
                   chunk3(1, cs), chunk3(cs, SSM_CHUNK), chunk3(cs, SSM_CHUNK), chunk3(1, SSM_CHUNK)],
        out_shape=[jax.ShapeDtypeStruct((t, ds), BF16),
                   jax.ShapeDtypeStruct((nch, SSM_CHUNK, cs), F32), jax.ShapeDtypeStruct((nch, SSM_CHUNK, cs), F32),
                   jax.ShapeDtypeStruct((nch, 1, cs), F32), jax.ShapeDtypeStruct((nch, 1, cs), F32),
                   jax.ShapeDtypeStruct((nch, cs, SSM_CHUNK), F32), jax.ShapeDtypeStruct((nch, cs, SSM_CHUNK), F32),
                   jax.ShapeDtypeStruct((nch, 1, SSM_CHUNK), F32)],
        scratch_shapes=[pltpu.VMEM((t, cs), F32), pltpu.VMEM((t, cs), F32)],
        compiler_params=_params(),
    )(dy, proj, s_re, s_im, bb_re, bb_im, a_re, a_im, cc_re, cc_im, d_skip)


def _discretize(lr, li, log_dt, br, bi):
    dt = jnp.exp(log_dt)
    mag = jnp.exp(lr * dt)
    ang = li * dt
    a_re = mag * jnp.cos(ang)
    a_im = mag * jnp.sin(ang)
    den = lr * lr + li * li
    nr = a_re - 1.0
    f_re = (nr * lr + a_im * li) / den
    f_im = (a_im * lr - nr * li) / den
    return a_re, a_im, f_re * br - f_im * bi, f_re * bi + f_im * br


def _whole(shape):
    return pl.BlockSpec(shape, lambda: (0,) * len(shape))


def _disc_fwd(lr, li, log_dt, br, bi, deps=()):
    def body(lr_ref, li_ref, dt_ref, br_ref, bi_ref, ar_ref, ai_ref, bbr_ref, bbi_ref):
        outs = _discretize(lr_ref[...], li_ref[...], dt_ref[...], br_ref[...], bi_ref[...])
        for ref, val in zip((ar_ref, ai_ref, bbr_ref, bbi_ref), outs):
            ref[...] = val

    args = (lr, li, log_dt, br, bi)
    outs = (lr, lr, br, br)
    return pl.pallas_call(
        _ignoring_deps(body, 5, deps), name="disc_fwd",
        in_specs=[_whole(a.shape) for a in args] + [ANY_SPEC] * len(deps),
        out_specs=[_whole(a.shape) for a in outs],
        out_shape=[jax.ShapeDtypeStruct(a.shape, F32) for a in outs],
    )(*args, *deps)


def _disc_bwd(lr, li, log_dt, br, bi, dar, dai, dbbr, dbbi):
    def body(lr_ref, li_ref, dt_ref, br_ref, bi_ref, dar_ref, dai_ref, dbbr_ref, dbbi_ref,
             dlr_ref, dli_ref, ddt_ref, dbr_ref, dbi_ref):
        _, vjp = jax.vjp(_discretize, lr_ref[...], li_ref[...], dt_ref[...], br_ref[...], bi_ref[...])
        grads = vjp((dar_ref[...], dai_ref[...], dbbr_ref[...], dbbi_ref[...]))
        for ref, val in zip((dlr_ref, dli_ref, ddt_ref, dbr_ref, dbi_ref), grads):
            ref[...] = val

    args = (lr, li, log_dt, br, bi, dar, dai, dbbr, dbbi)
    outs = (lr, li, log_dt, br, bi)
    return pl.pallas_call(
        body, name="disc_bwd", in_specs=[_whole(a.shape) for a in args],
        out_specs=[_whole(a.shape) for a in outs],
        out_shape=[jax.ShapeDtypeStruct(a.shape, F32) for a in outs],
    )(*args)


def _adamw(w, g, m, v):
    m = ADAM_B1 * m + (1.0 - ADAM_B1) * g
    v = ADAM_B2 * v + (1.0 - ADAM_B2) * (g * g)
    m_hat = m / ADAM_BC1
    v_hat = v / ADAM_BC2
    delta = -ADAM_LR * (m_hat / (jnp.sqrt(v_hat) + ADAM_EPS) + ADAM_WD * w)
    return delta, m, v


def _adamw_reduce(name, parts, w, m, v):
    _, r, c = parts.shape
    tr = r
    for cand in (256, 176, 128):
        if r % cand == 0:
            tr = cand
            break

    def body(p_ref, w_ref, m_ref, v_ref, g_ref, d_ref, nm_ref, nv_ref):
        g = p_ref[0].astype(F32)
        for k in range(1, N_DEV):
            g = g + p_ref[k].astype(F32)
        delta, nm, nv = _adamw(w_ref[...], g, m_ref[...], v_ref[...])
        g_ref[...] = g
        d_ref[...] = delta
        nm_ref[...] = nm
        nv_ref[...] = nv

    blk = pl.BlockSpec((tr, c), lambda i: (i, 0))
    return pl.pallas_call(
        body, name=name, grid=(r // tr,),
        in_specs=[pl.BlockSpec((N_DEV, tr, c), lambda i: (0, i, 0)), blk, blk, blk],
        out_specs=[blk] * 4, out_shape=[jax.ShapeDtypeStruct((r, c), F32)] * 4,
        compiler_params=_params(),
    )(parts, w, m, v)


def _sum_parts(name, parts):
    _, r, c = parts.shape

    def body(p_ref, o_ref):
        g = p_ref[0]
        for k in range(1, N_DEV):
            g = g + p_ref[k]
        o_ref[...] = g

    return pl.pallas_call(
        body, name=name, in_specs=[_whole(parts.shape)], out_specs=_whole((r, c)),
        out_shape=jax.ShapeDtypeStruct((r, c), F32), compiler_params=_params(),
    )(parts)


def _adamw_many(name, grads, ws, ms, vs):
    n = len(grads)

    def body(*refs):
        ins, outs = refs[:4 * n], refs[4 * n:]
        for i in range(n):
            g, w, m, v = (ins[j * n + i][...] for j in range(4))
            for ref, val in zip((outs[i], outs[n + i], outs[2 * n + i]), _adamw(w, g, m, v)):
                ref[...] = val

    args = (*grads, *ws, *ms, *vs)
    outs = pl.pallas_call(
        body, name=name, in_specs=[_whole(a.shape) for a in args],
        out_specs=[_whole(a.shape) for a in ws] * 3,
        out_shape=[jax.ShapeDtypeStruct(a.shape, F32) for a in ws] * 3, compiler_params=_params(),
    )(*args)
    return outs[:n], outs[n:2 * n], outs[2 * n:]


def _pack(arrays, rows):
    flat = jnp.concatenate([a.reshape(-1) for a in arrays])
    return jnp.pad(flat, (0, rows * LANES - flat.shape[0])).reshape(rows, LANES)


def _unpack(packed, shapes):
    flat = packed.reshape(-1)
    out, off = [], 0
    for s in shapes:
        n = math.prod(s)
        out.append(flat[off:off + n].reshape(s))
        off += n
    return out


def _packed_rows(shapes):
    n = sum(math.prod(s) for s in shapes)
    return -(-n // (SUBLANES * LANES)) * SUBLANES


def _block_diag(x):
    j, g, r, c = x.shape
    eye = jnp.eye(g, dtype=x.dtype)
    return (x[:, :, :, None, :] * eye[None, :, None, :, None]).reshape(j, g * r, g * c)


def _diag_blocks(x, g):
    j, gr, gc = x.shape
    r, c = gr // g, gc // g
    eye = jnp.eye(g, dtype=x.dtype)
    return (x.reshape(j, g, r, g, c) * eye[None, :, None, :, None]).sum(axis=3)


def kernel(x, meta_tokens, norm_mix_g, w_in, conv_w, ssm_lam_re, ssm_lam_im, ssm_log_dt, ssm_b_re, ssm_b_im, ssm_c_re, ssm_c_im, ssm_d, ssm_w_glu, gain_conv_out, gain_ssm_out, w_out, norm_ffn_g, w_up, ffn_conv_w, ffn_conv_b, w_down, norm_final_g, loss_target, m_meta_tokens, m_norm_mix_g, m_w_in, m_conv_w, m_ssm_lam_re, m_ssm_lam_im, m_ssm_log_dt, m_ssm_b_re, m_ssm_b_im, m_ssm_c_re, m_ssm_c_im, m_ssm_d, m_ssm_w_glu, m_gain_conv_out, m_gain_ssm_out, m_w_out, m_norm_ffn_g, m_w_up, m_ffn_conv_w, m_ffn_conv_b, m_w_down, m_norm_final_g, v_meta_tokens, v_norm_mix_g, v_w_in, v_conv_w, v_ssm_lam_re, v_ssm_lam_im, v_ssm_log_dt, v_ssm_b_re, v_ssm_b_im, v_ssm_c_re, v_ssm_c_im, v_ssm_d, v_ssm_w_glu, v_gain_conv_out, v_gain_ssm_out, v_w_out, v_norm_ffn_g, v_w_up, v_ffn_conv_w, v_ffn_conv_b, v_w_down, v_norm_final_g):
    weights = dict(meta_tokens=meta_tokens, norm_mix_g=norm_mix_g, w_in=w_in, conv_w=conv_w, ssm_lam_re=ssm_lam_re, ssm_lam_im=ssm_lam_im, ssm_log_dt=ssm_log_dt, ssm_b_re=ssm_b_re, ssm_b_im=ssm_b_im, ssm_c_re=ssm_c_re, ssm_c_im=ssm_c_im, ssm_d=ssm_d, ssm_w_glu=ssm_w_glu, gain_conv_out=gain_conv_out, gain_ssm_out=gain_ssm_out, w_out=w_out, norm_ffn_g=norm_ffn_g, w_up=w_up, ffn_conv_w=ffn_conv_w, ffn_conv_b=ffn_conv_b, w_down=w_down, norm_final_g=norm_final_g)
    mom_m = dict(meta_tokens=m_meta_tokens, norm_mix_g=m_norm_mix_g, w_in=m_w_in, conv_w=m_conv_w, ssm_lam_re=m_ssm_lam_re, ssm_lam_im=m_ssm_lam_im, ssm_log_dt=m_ssm_log_dt, ssm_b_re=m_ssm_b_re, ssm_b_im=m_ssm_b_im, ssm_c_re=m_ssm_c_re, ssm_c_im=m_ssm_c_im, ssm_d=m_ssm_d, ssm_w_glu=m_ssm_w_glu, gain_conv_out=m_gain_conv_out, gain_ssm_out=m_gain_ssm_out, w_out=m_w_out, norm_ffn_g=m_norm_ffn_g, w_up=m_w_up, ffn_conv_w=m_ffn_conv_w, ffn_conv_b=m_ffn_conv_b, w_down=m_w_down, norm_final_g=m_norm_final_g)
    mom_v = dict(meta_tokens=v_meta_tokens, norm_mix_g=v_norm_mix_g, w_in=v_w_in, conv_w=v_conv_w, ssm_lam_re=v_ssm_lam_re, ssm_lam_im=v_ssm_lam_im, ssm_log_dt=v_ssm_log_dt, ssm_b_re=v_ssm_b_re, ssm_b_im=v_ssm_b_im, ssm_c_re=v_ssm_c_re, ssm_c_im=v_ssm_c_im, ssm_d=v_ssm_d, ssm_w_glu=v_ssm_w_glu, gain_conv_out=v_gain_conv_out, gain_ssm_out=v_gain_ssm_out, w_out=v_w_out, norm_ffn_g=v_norm_ffn_g, w_up=v_w_up, ffn_conv_w=v_ffn_conv_w, ffn_conv_b=v_ffn_conv_b, w_down=v_w_down, norm_final_g=v_norm_final_g)
    names = list(weights)

    n_meta, d_meta = meta_tokens.shape
    seq, d = x.shape[1], x.shape[2]
    rows_used = n_meta + seq
    t = -(-rows_used // ROW_TILE) * ROW_TILE
    d_in_s = w_in.shape[2]
    dc_s = conv_w.shape[2]
    dc = dc_s * N_DEV
    ds = ssm_w_glu.shape[2]
    n_groups, n_state, grp = ssm_b_re.shape[1:]
    ns = n_groups * n_state
    nch = ds // SSM_CHUNK
    gpc = n_groups // nch
    ff_s = w_up.shape[2]
    dn_s = w_down.shape[1]
    assert 3 * dc + ds == d_in_s * N_DEV and 2 * dn_s == ff_s and t % (N_SEG * SUBLANES) == 0

    small_shard = jnp.concatenate([meta_tokens.reshape(-1), conv_w.reshape(-1), ffn_conv_w.reshape(-1)])
    n_small = small_shard.shape[0]
    small_rows = -(-n_small // LANES)
    small_shard = jnp.pad(small_shard, (0, small_rows * LANES - n_small)).reshape(small_rows, LANES)
    ag, ag_token = _gather2_start("gather_weights_start", [
        small_shard, w_in[0].astype(BF16), ssm_w_glu[0].astype(BF16), w_out[0].astype(BF16),
        jnp.swapaxes(w_up[0], 0, 1).astype(BF16), w_down[0].astype(BF16)])
    fb = ffn_conv_b.reshape(N_DEV, 1, ff_s)

    gh = n_groups * grp
    per_h = lambda a: jnp.broadcast_to(a.reshape(n_groups, 1, -1), (n_groups, grp, n_state)).reshape(gh, n_state)
    ghp = lambda a: a.transpose(0, 1, 3, 2).reshape(gh, n_state)
    lr, li, log_dt_e = per_h(ssm_lam_re), per_h(ssm_lam_im), per_h(ssm_log_dt)
    br, bi = ghp(ssm_b_re), ghp(ssm_b_im)
    a_re, a_im, bb_re, bb_im = _disc_fwd(lr, li, log_dt_e, br, bi, deps=(ag_token,))
    cs = gpc * n_state
    chunk_row = lambda a: a.reshape(n_groups, grp, n_state)[:, 0].reshape(nch, 1, cs)
    to_bb = lambda a: _block_diag(a.reshape(nch, gpc, grp, n_state)).astype(BF16)
    to_cc = lambda a: _block_diag(a.reshape(nch, gpc, grp, n_state).transpose(0, 1, 3, 2)).astype(BF16)
    bbm_re, bbm_im = to_bb(bb_re), to_bb(bb_im)
    ccm_re, ccm_im = to_cc(ssm_c_re), to_cc(ssm_c_im)
    a_re_c, a_im_c = chunk_row(a_re), chunk_row(a_im)
    d_skip = ssm_d.reshape(nch, 1, SSM_CHUNK)

    target = _to_segments(jnp.pad(loss_target[0], ((n_meta, t - rows_used), (0, 0))))
    token = _gather2_forward("gather_weights_forward_first", ag, (0, 1), (bbm_im, ccm_im, target))
    (g_small,) = _gather2_wait("gather_weights_wait_small", ag, (0,), token)
    g_small = g_small.reshape(N_DEV, -1)
    o1 = n_meta * d_meta
    o2 = o1 + 3 * dc_s
    meta_full = g_small[:, :o1].reshape(N_DEV, n_meta, d_meta).transpose(1, 0, 2).reshape(n_meta, d)
    conv_w_f = g_small[:, o1:o2].reshape(N_DEV, 3, dc_s).transpose(1, 0, 2).reshape(3, dc)
    fw = g_small[:, o2:o2 + 3 * ff_s].reshape(N_DEV, 3, ff_s)
    h0 = _to_segments(jnp.concatenate([meta_full, x[0], jnp.zeros((t - rows_used, d), F32)], axis=0))
    full_t =lambda w: pl.BlockSpec((t, w), lambda *_: (0, 0))

    hn1 = _rms_fwd("norm_mix", h0, norm_mix_g)
    (g_in,) = _gather2_wait("gather_weights_wait_in", ag, (1,), hn1)
    proj = _mm("proj", hn1, g_in, dims=NN, grid=(N_DEV,), a_spec=full_t(d),
               b_spec=pl.BlockSpec((None, d, d_in_s), lambda j: (j, 0, 0)),
               o_spec=pl.BlockSpec((t, d_in_s), lambda j: (0, j)),
               out_shape=jax.ShapeDtypeStruct((t, N_DEV * d_in_s), F32))
    s_re, s_im, y_ssm = _s5_fwd(proj, 3 * dc, bbm_re, bbm_im, a_re_c, a_im_c, ccm_re, ccm_im, d_skip)
    token = _gather2_forward("gather_weights_forward_up", ag, (2, 3, 4), (y_ssm,))
    g_glu, g_out = _gather2_wait("gather_weights_wait_mix", ag, (2, 3), token)
    w_out_f = g_out.reshape(-1, d)
    w_glu_f = g_glu.reshape(ds, ds)
    mixed, z_glu = _mix_fwd(proj, y_ssm, w_glu_f, conv_w_f, gain_conv_out, gain_ssm_out)
    tn_out = 256
    h1 = _mm("out_proj", mixed, w_out_f, dims=NN, grid=(d // tn_out,), a_spec=full_t(dc + ds),
             b_spec=pl.BlockSpec((dc + ds, tn_out), lambda j: (0, j)),
             o_spec=pl.BlockSpec((t, tn_out), lambda j: (0, j)),
             out_shape=jax.ShapeDtypeStruct((t, d), F32),
             res=h0, res_spec=pl.BlockSpec((t, tn_out), lambda j: (0, j)))
    hn2 = _rms_fwd("norm_ffn", h1, norm_ffn_g)
    token = _gather2_forward("gather_weights_forward_down", ag, (5,), (hn2,))
    (g_up,) = _gather2_wait("gather_weights_wait_up", ag, (4,), token)
    up_pre = _mm("up_proj", hn2, g_up, dims=NT, grid=(N_DEV,), a_spec=full_t(d),
                 b_spec=pl.BlockSpec((None, ff_s, d), lambda j: (j, 0, 0)),
                 o_spec=pl.BlockSpec((None, t, ff_s), lambda j: (j, 0, 0)),
                 out_shape=jax.ShapeDtypeStruct((N_DEV, t, ff_s), F32))
    act = _ffn_fwd(up_pre, fw, fb)
    nhalf = N_DEV // 2
    (g_down,) = _gather2_wait("gather_weights_wait_down", ag, (5,), act)
    w_down_f = g_down.reshape(nhalf, 2 * dn_s, d)
    tn_w = 512
    h2 = _mm("down_proj", act, w_down_f, dims=NN, grid=(d // tn_w, nhalf),
             a_spec=pl.BlockSpec((None, t, ff_s), lambda j, k: (k, 0, 0)),
             b_spec=pl.BlockSpec((None, ff_s, tn_w), lambda j, k: (k, 0, j)),
             o_spec=pl.BlockSpec((t, tn_w), lambda j, k: (0, j)),
             out_shape=jax.ShapeDtypeStruct((t, d), F32), acc_shape=(t, tn_w),
             res=h1, res_spec=pl.BlockSpec((t, tn_w), lambda j, k: (0, j)))

    dh2, dh2_b, loss_part, d_norm_final = _loss_bwd(h2, target, norm_final_g.reshape(1, d), n_meta, rows_used)
    dact = _mm("down_dgrad", dh2_b, w_down_f, dims=NT, grid=(nhalf,), a_spec=full_t(d),
               b_spec=pl.BlockSpec((None, ff_s, d), lambda j: (j, 0, 0)),
               o_spec=pl.BlockSpec((None, t, ff_s), lambda j: (j, 0, 0)),
               out_shape=jax.ShapeDtypeStruct((nhalf, t, ff_s), BF16))
    dw_down = _mm("down_wgrad", act, dh2_b, dims=TN, grid=(nhalf,),
                  a_spec=pl.BlockSpec((None, t, ff_s), lambda j: (j, 0, 0)), b_spec=full_t(d),
                  o_spec=pl.BlockSpec((None, ff_s, d), lambda j: (j, 0, 0)),
                  out_shape=jax.ShapeDtypeStruct((nhalf, ff_s, d), BF16))
    ex_down, token = _send_start("exchange_down_start", [dw_down.reshape(N_DEV, dn_s, d)], gather=False)
    dup_pre, d_fw, d_fb = _ffn_bwd(up_pre, dact, fw, fb, deps=(token,))
    dhn2 = _mm("up_dgrad", dup_pre, g_up, dims=NN, grid=(1, N_DEV),
               a_spec=pl.BlockSpec((None, t, ff_s), lambda i, k: (k, 0, 0)),
               b_spec=pl.BlockSpec((None, ff_s, d), lambda i, k: (k, 0, 0)),
               o_spec=pl.BlockSpec((t, d), lambda i, k: (0, 0)),
               out_shape=jax.ShapeDtypeStruct((t, d), F32), acc_shape=(t, d))
    dw_up = _mm("up_wgrad", dup_pre, hn2, dims=TN, grid=(N_DEV,),
                a_spec=pl.BlockSpec((None, t, ff_s), lambda j: (j, 0, 0)), b_spec=full_t(d),
                o_spec=pl.BlockSpec((None, ff_s, d), lambda j: (j, 0, 0)),
                out_shape=jax.ShapeDtypeStruct((N_DEV, ff_s, d), BF16))
    ex_up, token = _send_start("exchange_up_start", [dw_up], gather=False)
    dh1, dh1_b, d_norm_ffn = _rms_bwd_res("norm_ffn_bwd", dh2, dhn2, h1, norm_ffn_g, deps=(token,))
    dmixed = _mm("out_dgrad", dh1_b, w_out_f, dims=NT, grid=((dc + ds) // tn_out,), a_spec=full_t(d),
                 b_spec=pl.BlockSpec((tn_out, d), lambda i: (i, 0)),
                 o_spec=pl.BlockSpec((t, tn_out), lambda i: (0, i)),
                 out_shape=jax.ShapeDtypeStruct((t, dc + ds), F32))
    dw_out = _mm("out_wgrad", mixed, dh1_b, dims=TN, grid=((dc + ds) // tn_out,),
                 a_spec=pl.BlockSpec((t, tn_out), lambda i: (0, i)), b_spec=full_t(d),
                 o_spec=pl.BlockSpec((tn_out, d), lambda i: (i, 0)),
                 out_shape=jax.ShapeDtypeStruct((dc + ds, d), BF16))
    db_gate, dconv, dy_ssm, d_wglu, d_conv_w, d_gain_c, d_gain_s = _mix_bwd1(
        proj, y_ssm, z_glu, dmixed, w_glu_f, conv_w_f, gain_conv_out, gain_ssm_out)
    (du, d_bbm_re, d_bbm_im, d_a_re, d_a_im, d_ccm_re, d_ccm_im, d_dskip) = _s5_bwd(
        dy_ssm, proj, 3 * dc, s_re, s_im, bbm_re, bbm_im, a_re_c, a_im_c, ccm_re, ccm_im, d_skip)

    from_bb = lambda a: _diag_blocks(a, gpc).reshape(gh, n_state)
    from_cc = lambda a: _diag_blocks(a, gpc).transpose(0, 1, 3, 2).reshape(gh, n_state)
    first_h = lambda a: jnp.pad(a.reshape(n_groups, 1, n_state), ((0, 0), (0, grp - 1), (0, 0))).reshape(gh, n_state)
    over_h = lambda a: a.reshape(n_groups, grp, n_state).sum(axis=1)
    d_lr, d_li, d_dt_e, d_br, d_bi = _disc_bwd(
        lr, li, log_dt_e, br, bi, first_h(d_a_re), first_h(d_a_im), from_bb(d_bbm_re), from_bb(d_bbm_im))

    rep2d = dict(
        ssm_lam_re=(n_groups, n_state), ssm_lam_im=(n_groups, n_state), ssm_log_dt=(1, n_groups),
        ssm_b_re=(gh, n_state), ssm_b_im=(gh, n_state), ssm_c_re=(gh, n_state),
        ssm_c_im=(gh, n_state), ssm_d=(n_groups, grp), gain_conv_out=(1, dc),
        gain_ssm_out=(1, ds), norm_ffn_g=(1, d), ffn_conv_b=(1, N_DEV * ff_s), norm_final_g=(1, d))
    rep_names = list(rep2d)
    rep_grads = dict(
        ssm_lam_re=over_h(d_lr), ssm_lam_im=over_h(d_li), ssm_log_dt=over_h(d_dt_e).sum(axis=1),
        ssm_b_re=d_br, ssm_b_im=d_bi, ssm_c_re=from_cc(d_ccm_re), ssm_c_im=from_cc(d_ccm_im),
        ssm_d=d_dskip, gain_conv_out=d_gain_c, gain_ssm_out=d_gain_s, norm_ffn_g=d_norm_ffn,
        ffn_conv_b=d_fb, norm_final_g=d_norm_final)
    rep_shapes = [rep2d[n] for n in rep_names] + [(1, 1)]
    rep_rows = _packed_rows(rep_shapes)
    rep_pack = _pack([rep_grads[n] for n in rep_names] + [loss_part], rep_rows)
    ex_mix, token = _send_start("exchange_mix_start", [
        dw_out.reshape(N_DEV, -1, d), d_wglu.astype(BF16).reshape(N_DEV, -1, ds),
        d_conv_w.reshape(3, N_DEV, dc_s).transpose(1, 0, 2), d_fw, rep_pack],
        gather=[False, False, False, False, True])
    dcdv = _mix_bwd2(proj, dconv, conv_w_f, deps=(token,))
    dproj = jnp.concatenate([db_gate, dcdv, du], axis=1)
    dhn1 = _mm("proj_dgrad", dproj, g_in, dims=NT, grid=(1, N_DEV),
               a_spec=pl.BlockSpec((t, d_in_s), lambda i, k: (0, k)),
               b_spec=pl.BlockSpec((None, d, d_in_s), lambda i, k: (k, 0, 0)),
               o_spec=pl.BlockSpec((t, d), lambda i, k: (0, 0)),
               out_shape=jax.ShapeDtypeStruct((t, d), F32), acc_shape=(t, d))
    dh0, _, d_norm_mix = _rms_bwd_res("norm_mix_bwd", dh1, dhn1, h0, norm_mix_g)
    dw_in = _mm("proj_wgrad", hn1, dproj, dims=TN, grid=(N_DEV,), a_spec=full_t(d),
                b_spec=pl.BlockSpec((t, d_in_s), lambda j: (0, j)),
                o_spec=pl.BlockSpec((None, d, d_in_s), lambda j: (j, 0, 0)),
                out_shape=jax.ShapeDtypeStruct((N_DEV, d, d_in_s), BF16))
    dh0 = _from_segments(dh0)
    grad_x = dh0[n_meta:rows_used][None]
    d_meta_b = dh0[:n_meta].reshape(n_meta, N_DEV, d_meta).transpose(1, 0, 2)
    ex_in, ex_in_token = _send_start("exchange_in_start", [dw_in, d_meta_b, d_norm_mix],
                                     gather=[False, False, True])

    shard_out = {}
    (rep_parts,) = _send_wait("gather_small_grads_wait", ex_mix, (4,), ex_in_token)
    rep_sum = _sum_parts("sum_small_grads", rep_parts)
    *rep_g, loss = _unpack(rep_sum, rep_shapes)
    loss = loss.reshape(())
    swapped = ("ssm_b_re", "ssm_b_im")
    to2d = lambda n, a: ghp(a) if n in swapped else a.reshape(rep2d[n])
    from2d = lambda n, a: (a.reshape(1, n_groups, grp, n_state).transpose(0, 1, 3, 2) if n in swapped
                           else a.reshape(weights[n].shape))
    as2d = lambda tree: [to2d(n, tree[n]) for n in rep_names]
    rep_res = _adamw_many("adamw_replicated", rep_g, as2d(weights), as2d(mom_m), as2d(mom_v))
    for i, n in enumerate(rep_names):
        shard_out[n] = [from2d(n, r) for r in (rep_g[i], *(res[i] for res in rep_res))]

    def update(n, parts, transposed=False):
        sh = weights[n].shape
        two_d = lambda a: (jnp.swapaxes(a[0], 0, 1) if transposed else a.reshape(parts.shape[1:]))
        res = _adamw_reduce("adamw_" + n, parts, two_d(weights[n]), two_d(mom_m[n]), two_d(mom_v[n]))
        shard_out[n] = [(jnp.swapaxes(r, 0, 1) if transposed else r).reshape(sh) for r in res]
        return res[0]

    (p_down,) = _send_wait("exchange_down_wait", ex_down, (0,), rep_sum)
    done = update("w_down", p_down)
    (p_up,) = _send_wait("exchange_up_wait", ex_up, (0,), done)
    done = update("w_up", p_up, transposed=True)
    p_out, p_glu, p_cw, p_fw = _send_wait("exchange_mix_wait", ex_mix, (0, 1, 2, 3), done)
    update("w_out", p_out)
    update("ssm_w_glu", p_glu)
    update("conv_w", p_cw)
    done = update("ffn_conv_w", p_fw)
    p_in, p_meta, p_nm = _send_wait("exchange_in_wait", ex_in, (0, 1, 2), done)
    update("w_in", p_in)
    update("meta_tokens", p_meta)
    update("norm_mix_g", p_nm)

    grads = [shard_out[n][0] for n in names]
    deltas = [shard_out[n][1] for n in names]
    new_m = [shard_out[n][2] for n in names]
    new_v = [shard_out[n][3] for n in names]
    return (loss, grad_x, *grads, *deltas, *new_m, *new_v)
```

```python
import functools
import math

import jax
import jax.numpy as jnp
from jax import lax
from jax.experimental import pallas as pl
from jax.experimental.pallas import tpu as pltpu

F32 = jnp.float32
BF16 = jnp.bfloat16
MESH = pl.DeviceIdType.MESH

N_DEV = 8
RMS_EPS = 1e-6
ADAM_LR = 0.001
ADAM_B1 = 0.9
ADAM_B2 = 0.999
ADAM_EPS = 1e-08
ADAM_WD = 0.01
ADAM_STEP = 10
ADAM_BC1 = 1.0 - ADAM_B1 ** ADAM_STEP
ADAM_BC2 = 1.0 - ADAM_B2 ** ADAM_STEP

SUBLANES = 8
LANES = 128
ROW_TILE = 128
ROW_CHUNK = 32
WIDE_TILES = (544, 256, 128)
FFN_COLS = 256
N_SEG = 8
HALO_ROWS = 16
SSM_CHUNK = 128
VMEM_LIMIT = 48 * 1024 * 1024

NN = ((1,), (0,))
NT = ((1,), (1,))
TN = ((0,), (0,))


def _params(**kw):
    return pltpu.CompilerParams(vmem_limit_bytes=VMEM_LIMIT, **kw)


def _dot(a, b, dims):
    return lax.dot_general(a, b, (dims, ((), ())), preferred_element_type=F32)


def _mean_sq_rsqrt(x):
    return lax.rsqrt(jnp.mean(x * x, axis=-1, keepdims=True) + RMS_EPS)


def _rms_bwd(x, r, g, dy):
    xhat = x * r
    dxh = dy * g
    dx = r * (dxh - xhat * jnp.mean(dxh * xhat, axis=-1, keepdims=True))
    return dx, dy * xhat


def _gelu(y):
    c = math.sqrt(2.0 / math.pi)
    t = jnp.tanh(c * (y + 0.044715 * y * y * y))
    return 0.5 * y * (1.0 + t), t


def _gelu_grad(y, t):
    c = math.sqrt(2.0 / math.pi)
    return 0.5 * (1.0 + t) + 0.5 * y * (1.0 - t * t) * c * (1.0 + 3.0 * 0.044715 * y * y)


def _wrap_prev_halo(halo, first):
    seg = lax.broadcasted_iota(jnp.int32, halo.shape, 0) % N_SEG
    wrapped = jnp.where(seg == 0, 0.0, pltpu.roll(halo, 1, 0))
    return jnp.where(first, wrapped, halo)


def _wrap_next_halo(halo, last):
    seg = lax.broadcasted_iota(jnp.int32, halo.shape, 0) % N_SEG
    wrapped = jnp.where(seg == N_SEG - 1, 0.0, pltpu.roll(halo, halo.shape[0] - 1, 0))
    return jnp.where(last, wrapped, halo)


def _tile_rows(get, prev, nxt, s, n, tile_rows=ROW_TILE):
    parts = []
    if s < 0:
        parts.append(prev[HALO_ROWS + s:HALO_ROWS + min(s + n, 0)])
    lo, hi = max(s, 0), min(s + n, tile_rows)
    if hi > lo:
        parts.append(get(lo, hi - lo))
    if s + n > tile_rows:
        parts.append(nxt[max(s - tile_rows, 0):s + n - tile_rows])
    return parts[0] if len(parts) == 1 else jnp.concatenate(parts, axis=0)


def _dev_index(p):
    return 4 * p[0] + 2 * p[1] + p[2]


def _allgather(name, shards, deps=()):
    n = len(shards)

    def body(*refs):
        ins, outs = refs[:n], refs[n:2 * n]
        send_sems, recv_sems, local_sems = refs[2 * n:]
        x, y, c = lax.axis_index("x"), lax.axis_index("y"), lax.axis_index("c")
        me, sibling = (x, y, c), (x, y, 1 - c)
        chips = [(1 - x, y), (x, 1 - y), (1 - x, 1 - y)]

        def copy(a, k, block, to, src=None):
            dst = outs[a].at[_dev_index(block)]
            return pltpu.make_async_remote_copy(
                src_ref=dst if src is None else src, dst_ref=dst,
                send_sem=send_sems.at[a, k], recv_sem=recv_sems.at[a, k],
                device_id=to, device_id_type=MESH)

        mine = [pltpu.make_async_copy(ins[a], outs[a].at[_dev_index(me)], local_sems.at[a])
                for a in range(n)]
        for cp in mine:
            cp.start()
        first = []
        for a in range(n):
            first.append(copy(a, 0, me, sibling, src=ins[a]))
            for j, chip in enumerate(chips):
                first.append(copy(a, 1 + j, me, (*chip, c), src=ins[a]))
        for cp in first:
            cp.start()
        passed = []
        for j, chip in enumerate(chips):
            for a in range(n):
                copy(a, 1 + j, (*chip, c), me).wait_recv()
                fwd = copy(a, 4 + j, (*chip, c), sibling)
                fwd.start()
                passed.append(fwd)
        for a in range(n):
            copy(a, 0, sibling, me).wait_recv()
            for j, chip in enumerate(chips):
                copy(a, 4 + j, (*chip, 1 - c), me).wait_recv()
        for cp in first + passed:
            cp.wait_send()
        for cp in mine:
            cp.wait()

    any_spec = pl.BlockSpec(memory_space=pl.ANY)
    return pl.pallas_call(
        _ignoring_deps(body, n, deps), name=name,
        out_shape=[jax.ShapeDtypeStruct((N_DEV,) + s.shape, s.dtype) for s in shards],
        in_specs=[any_spec] * (n + len(deps)), out_specs=[any_spec] * n,
        scratch_shapes=[pltpu.SemaphoreType.DMA((n, 7)), pltpu.SemaphoreType.DMA((n, 7)),
                        pltpu.SemaphoreType.DMA((n,))],
    )(*shards, *deps)


HBM_SPEC = pl.BlockSpec(memory_space=pltpu.HBM)
SEM_SPEC = pl.BlockSpec(memory_space=pltpu.SEMAPHORE)
ANY_SPEC = pl.BlockSpec(memory_space=pl.ANY)
DATAFLOW = pltpu.SideEffectType.DATAFLOW_SIDE_EFFECTING


def _ignoring_deps(body, n_in, deps):
    n_dep = len(deps)

    def wrapped(*refs):
        return body(*refs[:n_in], *refs[n_in + n_dep:])

    return wrapped


def _my_position():
    x, y, c = lax.axis_index("x"), lax.axis_index("y"), lax.axis_index("c")
    return (x, y, c)


def _peer(me, k):
    return tuple((1 - v) if (k >> s) & 1 else v for v, s in zip(me, (2, 1, 0)))


def _split_copies(src_refs, land_refs, send_sems, recv_sems, gather):
    me = _my_position()
    copies = []
    for a, (src, land) in enumerate(zip(src_refs, land_refs)):
        for k in range(1, N_DEV):
            peer = _peer(me, k)
            copies.append(pltpu.make_async_remote_copy(
                src_ref=src if gather[a] else src.at[_dev_index(peer)], dst_ref=land.at[_dev_index(me)],
                send_sem=send_sems[a].at[k - 1], recv_sem=recv_sems[a].at[k - 1],
                device_id=peer, device_id_type=MESH))
    return copies


def _own_slot(block, like_shape):
    me = _dev_index(_my_position())
    return lax.dynamic_update_index_in_dim(lax.empty(like_shape, block.dtype), block, me, 0)


def _send_start(name, srcs, gather):
    n = len(srcs)
    me = _dev_index(_my_position())
    gather = [gather] * n if isinstance(gather, bool) else list(gather)
    lands = [_own_slot(s, (N_DEV,) + s.shape) if g else
             _own_slot(lax.dynamic_index_in_dim(s, me, 0, keepdims=False), s.shape)
             for s, g in zip(srcs, gather)]

    def body(*refs):
        src_refs, land_refs = refs[:n], refs[n:2 * n]
        send_sems, recv_sems = refs[2 * n:3 * n], refs[3 * n:4 * n]
        token = refs[-1]
        for cp in _split_copies(src_refs, land_refs, send_sems, recv_sems, gather):
            cp.start()
        token[...] = jnp.zeros_like(token)

    hbm = lambda a: pltpu.HBM(a.shape, a.dtype)
    sems = [pltpu.SemaphoreType.DMA((N_DEV - 1,))] * n
    outs = pl.pallas_call(
        body, name=name,
        out_shape=(*sems, *sems, *[hbm(s) for s in srcs], *[hbm(l) for l in lands],
                   jax.ShapeDtypeStruct((SUBLANES, LANES), F32)),
        in_specs=[HBM_SPEC] * (2 * n),
        out_specs=(*[SEM_SPEC] * (2 * n), *[HBM_SPEC] * (2 * n), pl.BlockSpec(memory_space=pltpu.VMEM)),
        input_output_aliases={i: 2 * n + i for i in range(2 * n)},
        compiler_params=pltpu.CompilerParams(has_side_effects=DATAFLOW),
    )(*[pltpu.with_memory_space_constraint(a, pltpu.HBM) for a in (*srcs, *lands)])
    state = dict(send=outs[:n], recv=outs[n:2 * n], srcs=outs[2 * n:3 * n], lands=outs[3 * n:4 * n],
                 gather=gather)
    return state, outs[-1]


def _send_wait(name, state, which, after):
    n = len(which)
    pick = lambda key: [state[key][i] for i in which]
    gather = pick("gather")

    def body(*refs):
        src_refs, land_refs = refs[:n], refs[n:2 * n]
        send_sems, recv_sems = refs[2 * n:3 * n], refs[3 * n:4 * n]
        for cp in _split_copies(src_refs, land_refs, send_sems, recv_sems, gather):
            cp.wait_send()
            cp.wait_recv()

    srcs, lands = pick("srcs"), pick("lands")
    hbm = lambda a: pltpu.HBM(a.shape, a.dtype)
    outs = pl.pallas_call(
        body, name=name,
        out_shape=(*[hbm(s) for s in srcs], *[hbm(l) for l in lands]),
        in_specs=[*[HBM_SPEC] * (2 * n), *[SEM_SPEC] * (2 * n), ANY_SPEC],
        out_specs=tuple([HBM_SPEC] * (2 * n)),
        input_output_aliases={i: i for i in range(2 * n)},
        compiler_params=pltpu.CompilerParams(has_side_effects=DATAFLOW),
    )(*srcs, *lands, *pick("send"), *pick("recv"), after)
    return outs[n:]


def _two_level_copies(src_refs, land_refs, sems1, sems2):
    x, y, c = _my_position()
    me, sibling = (x, y, c), (x, y, 1 - c)
    chips = [(1 - x, y), (x, 1 - y), (1 - x, 1 - y)]
    stage1, stage2 = [], []
    for a, land in enumerate(land_refs):
        def copy(src, block, to, sems, k):
            return pltpu.make_async_remote_copy(
                src_ref=src, dst_ref=land.at[_dev_index(block)], send_sem=sems[0][a].at[k],
                recv_sem=sems[1][a].at[k], device_id=to, device_id_type=MESH)
        src = src_refs[a] if src_refs is not None else land.at[_dev_index(me)]
        stage1.append([copy(src, me, sibling, sems1, 0)] +
                      [copy(src, me, (*chip, c), sems1, 1 + j) for j, chip in enumerate(chips)])
        if sems2 is not None:
            stage2.append([copy(land.at[_dev_index((*chip, c))], (*chip, c), sibling, sems2, j)
                           for j, chip in enumerate(chips)])
    return stage1, stage2


def _gather2_start(name, shards):
    n = len(shards)
    lands = [_own_slot(s, (N_DEV,) + s.shape) for s in shards]

    def body(*refs):
        src_refs, land_refs = refs[:n], refs[n:2 * n]
        sems1 = (refs[2 * n:3 * n], refs[3 * n:4 * n])
        stage1, _ = _two_level_copies(src_refs, land_refs, sems1, None)
        for copies in stage1:
            for cp in copies:
                cp.start()
        refs[-1][...] = jnp.zeros_like(refs[-1])

    hbm = lambda a: pltpu.HBM(a.shape, a.dtype)
    sems = [pltpu.SemaphoreType.DMA((4,))] * n
    outs = pl.pallas_call(
        body, name=name,
        out_shape=(*sems, *sems, *[hbm(s) for s in shards], *[hbm(l) for l in lands],
                   jax.ShapeDtypeStruct((SUBLANES, LANES), F32)),
        in_specs=[HBM_SPEC] * (2 * n),
        out_specs=(*[SEM_SPEC] * (2 * n), *[HBM_SPEC] * (2 * n), pl.BlockSpec(memory_space=pltpu.VMEM)),
        input_output_aliases={i: 2 * n + i for i in range(2 * n)},
        compiler_params=pltpu.CompilerParams(has_side_effects=DATAFLOW),
    )(*[pltpu.with_memory_space_constraint(a, pltpu.HBM) for a in (*shards, *lands)])
    state = dict(send1=list(outs[:n]), recv1=list(outs[n:2 * n]), srcs=list(outs[2 * n:3 * n]),
                 lands=list(outs[3 * n:4 * n]), send2={}, recv2={})
    return state, outs[-1]


def _gather2_forward(name, state, which, after):
    n = len(which)
    pick = lambda key: [state[key][i] for i in which]

    def body(*refs):
        land_refs, recv1 = refs[:n], refs[n:2 * n]
        outs = refs[2 * n + len(after):]
        sems2 = (outs[:n], outs[n:2 * n])
        stage1, stage2 = _two_level_copies(None, land_refs, (recv1, recv1), sems2)
        for a in range(n):
            for j in range(3):
                stage1[a][1 + j].wait_recv()
                stage2[a][j].start()
        outs[-1][...] = jnp.zeros_like(outs[-1])

    lands = pick("lands")
    sems = [pltpu.SemaphoreType.DMA((3,))] * n
    outs = pl.pallas_call(
        body, name=name,
        out_shape=(*sems, *sems, *[pltpu.HBM(l.shape, l.dtype) for l in lands],
                   jax.ShapeDtypeStruct((SUBLANES, LANES), F32)),
        in_specs=[*[HBM_SPEC] * n, *[SEM_SPEC] * n, *[ANY_SPEC] * len(after)],
        out_specs=(*[SEM_SPEC] * (2 * n), *[HBM_SPEC] * n, pl.BlockSpec(memory_space=pltpu.VMEM)),
        input_output_aliases={i: 2 * n + i for i in range(n)},
        compiler_params=pltpu.CompilerParams(has_side_effects=DATAFLOW),
    )(*lands, *pick("recv1"), *after)
    for idx, i in enumerate(which):
        state["send2"][i], state["recv2"][i] = outs[idx], outs[n + idx]
        state["lands"][i] = outs[2 * n + idx]
    return outs[-1]


def _gather2_wait(name, state, which, after):
    n = len(which)
    pick = lambda key: [state[key][i] for i in which]

    def body(*refs):
        src_refs, land_refs = refs[:n], refs[n:2 * n]
        sems1 = (refs[2 * n:3 * n], refs[3 * n:4 * n])
        sems2 = (refs[4 * n:5 * n], refs[5 * n:6 * n])
        stage1, stage2 = _two_level_copies(src_refs, land_refs, sems1, sems2)
        for a in range(n):
            for cp in stage1[a]:
                cp.wait_send()
            stage1[a][0].wait_recv()
            for cp in stage2[a]:
                cp.wait_send()
                cp.wait_recv()

    srcs, lands = pick("srcs"), pick("lands")
    hbm = lambda a: pltpu.HBM(a.shape, a.dtype)
    outs = pl.pallas_call(
        body, name=name,
        out_shape=(*[hbm(s) for s in srcs], *[hbm(l) for l in lands]),
        in_specs=[*[HBM_SPEC] * (2 * n), *[SEM_SPEC] * (4 * n), ANY_SPEC],
        out_specs=tuple([HBM_SPEC] * (2 * n)),
        input_output_aliases={i: i for i in range(2 * n)},
        compiler_params=pltpu.CompilerParams(has_side_effects=DATAFLOW),
    )(*srcs, *lands, *pick("send1"), *pick("recv1"), *pick("send2"), *pick("recv2"), after)
    return outs[n:]


def _mm(name, a, b, *, dims, grid, a_spec, b_spec, o_spec, out_shape, acc_shape=None,
        res=None, res_spec=None):
    n_red = grid[-1] if acc_shape is not None else 1
    red_axis = len(grid) - 1

    def body(*refs):
        a_ref, b_ref = refs[0], refs[1]
        r_ref = refs[2] if res is not None else None
        o_ref = refs[3] if res is not None else refs[2]
        part = _dot(a_ref[...], b_ref[...], dims)
        if acc_shape is None:
            if r_ref is not None:
                part = part + r_ref[...]
            o_ref[...] = part.astype(o_ref.dtype)
            return
        acc_ref = refs[-1]
        k = pl.program_id(red_axis)

        @pl.when(k == 0)
        def _():
            acc_ref[...] = part

        @pl.when(k > 0)
        def _():
            acc_ref[...] += part

        @pl.when(k == n_red - 1)
        def _():
            total = acc_ref[...]
            if r_ref is not None:
                total = total + r_ref[...]
            o_ref[...] = total.astype(o_ref.dtype)

    ins, in_specs = [a, b], [a_spec, b_spec]
    if res is not None:
        ins.append(res)
        in_specs.append(res_spec)
    return pl.pallas_call(
        body, name=name, grid=grid, in_specs=in_specs, out_specs=o_spec, out_shape=out_shape,
        scratch_shapes=[pltpu.VMEM(acc_shape, F32)] if acc_shape is not None else [],
        compiler_params=_params(),
    )(*ins)


def _wide_tile(t):
    return next(c for c in WIDE_TILES if t % c == 0)


def _rms_fwd(name, h, g):
    t, d = h.shape
    tile = _wide_tile(t)

    def body(h_ref, g_ref, o_ref):
        x = h_ref[...]
        o_ref[...] = (x * _mean_sq_rsqrt(x) * g_ref[...]).astype(BF16)

    return pl.pallas_call(
        body, name=name, grid=(t // tile,),
        in_specs=[pl.BlockSpec((tile, d), lambda i: (i, 0)), pl.BlockSpec((1, d), lambda i: (0, 0))],
        out_specs=pl.BlockSpec((tile, d), lambda i: (i, 0)),
        out_shape=jax.ShapeDtypeStruct((t, d), BF16), compiler_params=_params(),
    )(h, g)


def _rms_bwd_res(name, dres, dhn, h, g, deps=()):
    t, d = h.shape

    def body(dres_ref, dhn_ref, h_ref, g_ref, dh_ref, dhb_ref, dg_ref):
        x = h_ref[...]
        dx, dgt = _rms_bwd(x, _mean_sq_rsqrt(x), g_ref[...], dhn_ref[...])
        dh = dres_ref[...] + dx
        dh_ref[...] = dh
        dhb_ref[...] = dh.astype(BF16)

        @pl.when(pl.program_id(0) == 0)
        def _():
            dg_ref[...] = jnp.zeros_like(dg_ref)

        dg_ref[...] += jnp.sum(dgt, axis=0, keepdims=True)

    tile = _wide_tile(t)
    row = pl.BlockSpec((tile, d), lambda i: (i, 0))
    vec = pl.BlockSpec((1, d), lambda i: (0, 0))
    return pl.pallas_call(
        _ignoring_deps(body, 4, deps), name=name, grid=(t // tile,),
        in_specs=[row, row, row, vec] + [ANY_SPEC] * len(deps), out_specs=[row, row, vec],
        out_shape=[jax.ShapeDtypeStruct((t, d), F32), jax.ShapeDtypeStruct((t, d), BF16),
                   jax.ShapeDtypeStruct((1, d), F32)],
        compiler_params=_params(),
    )(dres, dhn, h, g, *deps)


def _loss_bwd(h2, target, g, row_lo, row_hi):
    t, d = h2.shape
    tile = _wide_tile(t)

    def body(h_ref, tg_ref, g_ref, dh_ref, dhb_ref, loss_ref, dg_ref):
        i = pl.program_id(0)
        x = h_ref[...]
        r = _mean_sq_rsqrt(x)
        gv = g_ref[...]
        y = x * r * gv
        rows = i * tile + lax.broadcasted_iota(jnp.int32, (tile, 1), 0)
        time = (rows % N_SEG) * (t // N_SEG) + rows // N_SEG
        valid = jnp.logical_and(time >= row_lo, time < row_hi)
        err = jnp.where(valid, y - tg_ref[...], 0.0)
        dy = err * (1.0 / d)
        dx, dgt = _rms_bwd(x, r, gv, dy)
        dh_ref[...] = dx
        dhb_ref[...] = dx.astype(BF16)

        @pl.when(i == 0)
        def _():
            loss_ref[...] = jnp.zeros_like(loss_ref)
            dg_ref[...] = jnp.zeros_like(dg_ref)

        row_loss = jnp.mean(err * err, axis=-1, keepdims=True)
        loss_ref[...] += 0.5 * jnp.sum(row_loss, axis=0, keepdims=True)
        dg_ref[...] += jnp.sum(dgt, axis=0, keepdims=True)

    row = pl.BlockSpec((tile, d), lambda i: (i, 0))
    vec = pl.BlockSpec((1, d), lambda i: (0, 0))
    return pl.pallas_call(
        body, name="loss_bwd", grid=(t // tile,), in_specs=[row, row, vec],
        out_specs=[row, row, pl.BlockSpec((1, 1), lambda i: (0, 0)), vec],
        out_shape=[jax.ShapeDtypeStruct((t, d), F32), jax.ShapeDtypeStruct((t, d), BF16),
                   jax.ShapeDtypeStruct((1, 1), F32), jax.ShapeDtypeStruct((1, d), F32)],
        compiler_params=_params(),
    )(h2, target, g)


def _prev_halo(i, t):
    return jnp.where(i == 0, t // HALO_ROWS - 1, i * (ROW_TILE // HALO_ROWS) - 1)


def _next_halo(i, t):
    return jnp.where(i == t // ROW_TILE - 1, 0, (i + 1) * (ROW_TILE // HALO_ROWS))


def _causal_taps(cur, halo, first):
    rows = cur.shape[0]
    ext = jnp.concatenate([_wrap_prev_halo(halo, first), cur], axis=0)
    return ext[:rows], ext[N_SEG:N_SEG + rows]


def _anticausal_taps(cur, halo, last):
    rows = cur.shape[0]
    ext = jnp.concatenate([cur, _wrap_next_halo(halo, last)], axis=0)
    return ext[N_SEG:N_SEG + rows], ext[2 * N_SEG:2 * N_SEG + rows]


def _mix_fwd(proj, y, w_glu, conv_w, gain_c, gain_s):
    t = proj.shape[0]
    dc = conv_w.shape[1]
    ds = y.shape[1]

    def body(p_ref, halo_ref, y_ref, wg_ref, cw_ref, gc_ref, gs_ref, mixed_ref, z_ref):
        i = pl.program_id(0)
        p = p_ref[...]
        b, c, v = p[:, :dc], p[:, dc:2 * dc], p[:, 2 * dc:3 * dc]
        cv = c * v
        hp = halo_ref[...]
        x2, x1 = _causal_taps(cv, hp[:, dc:2 * dc] * hp[:, 2 * dc:3 * dc], i == 0)
        cw = cw_ref[...]
        conv = cw[0:1] * x2 + cw[1:2] * x1 + cw[2:3] * cv
        co = b * conv
        mixed_ref[:, :dc] = (co * _mean_sq_rsqrt(co) * gc_ref[...]).astype(BF16)
        g, _ = _gelu(y_ref[...])
        z = _dot(g.astype(BF16), wg_ref[...], NN)
        z_ref[...] = z
        so = g * jax.nn.sigmoid(z)
        mixed_ref[:, dc:] = (so * _mean_sq_rsqrt(so) * gs_ref[...]).astype(BF16)

    const = lambda i: (0, 0)
    return pl.pallas_call(
        body, name="mix_fwd", grid=(t // ROW_TILE,),
        in_specs=[pl.BlockSpec((ROW_TILE, 3 * dc), lambda i: (i, 0)),
                  pl.BlockSpec((HALO_ROWS, 3 * dc), lambda i: (_prev_halo(i, t), 0)),
                  pl.BlockSpec((ROW_TILE, ds), lambda i: (i, 0)),
                  pl.BlockSpec((ds, ds), const), pl.BlockSpec(conv_w.shape, const),
                  pl.BlockSpec((1, dc), const), pl.BlockSpec((1, ds), const)],
        out_specs=[pl.BlockSpec((ROW_TILE, dc + ds), lambda i: (i, 0)),
                   pl.BlockSpec((ROW_TILE, ds), lambda i: (i, 0))],
        out_shape=[jax.ShapeDtypeStruct((t, dc + ds), BF16), jax.ShapeDtypeStruct((t, ds), F32)],
        compiler_params=_params(),
    )(proj, proj, y, w_glu, conv_w, gain_c, gain_s)


def _mix_bwd1(proj, y, z, dmixed, w_glu, conv_w, gain_c, gain_s):
    t = proj.shape[0]
    dc = conv_w.shape[1]
    ds = y.shape[1]

    def body(p_ref, halo_ref, y_ref, z_ref, dm_ref, wg_ref, cw_ref, gc_ref, gs_ref,
             db_ref, dconv_ref, dy_ref, dwg_ref, dcw_ref, dgc_ref, dgs_ref):
        i = pl.program_id(0)

        @pl.when(i == 0)
        def _():
            dwg_ref[...] = jnp.zeros_like(dwg_ref)
            dcw_ref[...] = jnp.zeros_like(dcw_ref)
            dgc_ref[...] = jnp.zeros_like(dgc_ref)
            dgs_ref[...] = jnp.zeros_like(dgs_ref)

        p = p_ref[...]
        b, c, v = p[:, :dc], p[:, dc:2 * dc], p[:, 2 * dc:3 * dc]
        cv = c * v
        hp = halo_ref[...]
        x2, x1 = _causal_taps(cv, hp[:, dc:2 * dc] * hp[:, 2 * dc:3 * dc], i == 0)
        cw = cw_ref[...]
        conv = cw[0:1] * x2 + cw[1:2] * x1 + cw[2:3] * cv
        co = b * conv
        dm = dm_ref[...]
        dco, dgt = _rms_bwd(co, _mean_sq_rsqrt(co), gc_ref[...], dm[:, :dc])
        dgc_ref[...] += jnp.sum(dgt, axis=0, keepdims=True)
        db_ref[...] = (dco * conv).astype(BF16)
        dconv = dco * b
        dconv_ref[...] = dconv
        dcw_ref[0:1, :] += jnp.sum(dconv * x2, axis=0, keepdims=True)
        dcw_ref[1:2, :] += jnp.sum(dconv * x1, axis=0, keepdims=True)
        dcw_ref[2:3, :] += jnp.sum(dconv * cv, axis=0, keepdims=True)

        yv = y_ref[...]
        g, th = _gelu(yv)
        sg = jax.nn.sigmoid(z_ref[...])
        so = g * sg
        dso, dgt = _rms_bwd(so, _mean_sq_rsqrt(so), gs_ref[...], dm[:, dc:])
        dgs_ref[...] += jnp.sum(dgt, axis=0, keepdims=True)
        dz = (dso * g * sg * (1.0 - sg)).astype(BF16)
        dg = dso * sg + _dot(dz, wg_ref[...], NT)
        dwg_ref[...] += _dot(g.astype(BF16), dz, TN)
        dy_ref[...] = dg * _gelu_grad(yv, th)

    const = lambda i: (0, 0)
    row = lambda w: pl.BlockSpec((ROW_TILE, w), lambda i: (i, 0))
    return pl.pallas_call(
        body, name="mix_bwd1", grid=(t // ROW_TILE,),
        in_specs=[row(3 * dc), pl.BlockSpec((HALO_ROWS, 3 * dc), lambda i: (_prev_halo(i, t), 0)),
                  row(ds), row(ds), row(dc + ds), pl.BlockSpec((ds, ds), const),
                  pl.BlockSpec(conv_w.shape, const), pl.BlockSpec((1, dc), const),
                  pl.BlockSpec((1, ds), const)],
        out_specs=[row(dc), row(dc), row(ds), pl.BlockSpec((ds, ds), const),
                   pl.BlockSpec(conv_w.shape, const), pl.BlockSpec((1, dc), const),
                   pl.BlockSpec((1, ds), const)],
        out_shape=[jax.ShapeDtypeStruct((t, dc), BF16), jax.ShapeDtypeStruct((t, dc), F32),
                   jax.ShapeDtypeStruct((t, ds), F32), jax.ShapeDtypeStruct((ds, ds), F32),
                   jax.ShapeDtypeStruct(conv_w.shape, F32), jax.ShapeDtypeStruct((1, dc), F32),
                   jax.ShapeDtypeStruct((1, ds), F32)],
        compiler_params=_params(),
    )(proj, proj, y, z, dmixed, w_glu, conv_w, gain_c, gain_s)


def _mix_bwd2(proj, dconv, conv_w, deps=()):
    t = proj.shape[0]
    dc = conv_w.shape[1]
    n_tiles = t // ROW_TILE

    def body(c_ref, v_ref, d_ref, halo_ref, cw_ref, o_ref):
        i = pl.program_id(0)
        d = d_ref[...]
        u1, u2 = _anticausal_taps(d, halo_ref[...], i == n_tiles - 1)
        cw = cw_ref[...]
        dcv = cw[2:3] * d + cw[1:2] * u1 + cw[0:1] * u2
        o_ref[:, :dc] = (dcv * v_ref[...]).astype(BF16)
        o_ref[:, dc:] = (dcv * c_ref[...]).astype(BF16)

    return pl.pallas_call(
        _ignoring_deps(body, 5, deps), name="mix_bwd2", grid=(n_tiles,),
        in_specs=[pl.BlockSpec((ROW_TILE, dc), lambda i: (i, 1)),
                  pl.BlockSpec((ROW_TILE, dc), lambda i: (i, 2)),
                  pl.BlockSpec((ROW_TILE, dc), lambda i: (i, 0)),
                  pl.BlockSpec((HALO_ROWS, dc), lambda i: (_next_halo(i, t), 0)),
                  pl.BlockSpec(conv_w.shape, lambda i: (0, 0))] + [ANY_SPEC] * len(deps),
        out_specs=pl.BlockSpec((ROW_TILE, 2 * dc), lambda i: (i, 0)),
        out_shape=jax.ShapeDtypeStruct((t, 2 * dc), BF16), compiler_params=_params(),
    )(proj, proj, dconv, dconv, conv_w, *deps)


def _ffn_taps(up_ref, prev, nxt, fw_ref, fb_ref, q, r, rows=ROW_CHUNK):
    get = lambda s, n: up_ref[q, pl.ds(s, n), :]
    x0, x1, x2 = (_tile_rows(get, prev, nxt, r - k * N_SEG, rows) for k in range(3))
    w = fw_ref[q]
    return w[0:1] * x2 + w[1:2] * x1 + w[2:3] * x0 + fb_ref[q], x2, x1, x0


def _ffn_fwd(up_pre, fw, fb):
    nb, t, w = up_pre.shape
    half = nb // 2

    def body(up_ref, halo_ref, fw_ref, fb_ref, act_ref):
        first = pl.program_id(0) == 0
        for q in range(half):
            halo_a = _wrap_prev_halo(halo_ref[q], first)
            halo_v = _wrap_prev_halo(halo_ref[q + half], first)
            for r in range(0, ROW_TILE, ROW_CHUNK):
                a = _ffn_taps(up_ref, halo_a, None, fw_ref, fb_ref, q, r)[0]
                val = _ffn_taps(up_ref, halo_v, None, fw_ref, fb_ref, q + half, r)[0]
                act_ref[q, pl.ds(r, ROW_CHUNK), :] = (a * jax.nn.sigmoid(a) * val).astype(BF16)

    return pl.pallas_call(
        body, name="ffn_fwd", grid=(t // ROW_TILE,),
        in_specs=[pl.BlockSpec((nb, ROW_TILE, w), lambda i: (0, i, 0)),
                  pl.BlockSpec((nb, HALO_ROWS, w), lambda i: (0, _prev_halo(i, t), 0)),
                  pl.BlockSpec(fw.shape, lambda i: (0, 0, 0)),
                  pl.BlockSpec(fb.shape, lambda i: (0, 0, 0))],
        out_specs=pl.BlockSpec((half, ROW_TILE, w), lambda i: (0, i, 0)),
        out_shape=jax.ShapeDtypeStruct((half, t, w), BF16), compiler_params=_params(),
    )(up_pre, up_pre, fw, fb)


def _ffn_bwd(up_pre, dact, fw, fb, deps=()):
    nb, t, w = up_pre.shape
    half = nb // 2
    n_tiles = t // ROW_TILE
    chunks = [(r, ROW_CHUNK) for r in range(0, ROW_TILE, ROW_CHUNK)] + [(ROW_TILE, HALO_ROWS)]

    def body(up_ref, prev_ref, next_ref, dact_ref, dact_next_ref, fw_ref, fb_ref,
             out_ref, dfw_ref, dfb_ref, dup_ref):
        first = pl.program_id(0) == 0
        last = pl.program_id(0) == n_tiles - 1

        @pl.when(first)
        def _():
            dfw_ref[...] = jnp.zeros_like(dfw_ref)
            dfb_ref[...] = jnp.zeros_like(dfb_ref)

        for q in range(half):
            blocks = (q, q + half)
            prev = [_wrap_prev_halo(prev_ref[b], first) for b in blocks]
            nxt = [_wrap_next_halo(next_ref[b], last) for b in blocks]
            dact_next = _wrap_next_halo(dact_next_ref[q].astype(F32), last)
            get_dact = lambda s, n: dact_ref[q, pl.ds(s, n), :].astype(F32)
            sums = [[jnp.zeros((SUBLANES, w), F32)] * 4 for _ in blocks]
            fold = lambda v: sum(v[s:s + SUBLANES] for s in range(0, ROW_CHUNK, SUBLANES))
            for r, rows in chunks:
                taps = [_ffn_taps(up_ref, prev[k], nxt[k], fw_ref, fb_ref, b, r, rows)
                        for k, b in enumerate(blocks)]
                a, val = taps[0][0], taps[1][0]
                da_ct = _tile_rows(get_dact, None, dact_next, r, rows)
                sg = jax.nn.sigmoid(a)
                dup = (da_ct * val * sg * (1.0 + a * (1.0 - sg)), da_ct * a * sg)
                for k in range(2):
                    dup_ref[k, pl.ds(r, rows), :] = dup[k]
                    if r < ROW_TILE:
                        terms = (dup[k], dup[k] * taps[k][1], dup[k] * taps[k][2], dup[k] * taps[k][3])
                        sums[k] = [s + fold(v) for s, v in zip(sums[k], terms)]
            for k, b in enumerate(blocks):
                s_b, s_w0, s_w1, s_w2 = (jnp.sum(p, axis=0, keepdims=True) for p in sums[k])
                dfb_ref[b] += s_b
                for tap, s_w in enumerate((s_w0, s_w1, s_w2)):
                    dfw_ref[b, tap:tap + 1, :] += s_w
                cw = fw_ref[b]
                for r in range(0, ROW_TILE, ROW_CHUNK):
                    d, u1, u2 = (dup_ref[k, pl.ds(r + s * N_SEG, ROW_CHUNK), :] for s in range(3))
                    out_ref[b, pl.ds(r, ROW_CHUNK), :] = (cw[2:3] * d + cw[1:2] * u1 + cw[0:1] * u2).astype(BF16)

    tile = lambda n: pl.BlockSpec((n, ROW_TILE, w), lambda i: (0, i, 0))
    halo = lambda n, index: pl.BlockSpec((n, HALO_ROWS, w), lambda i: (0, index(i, t), 0))
    const = lambda a: pl.BlockSpec(a.shape, lambda i: (0, 0, 0))
    return pl.pallas_call(
        _ignoring_deps(body, 7, deps), name="ffn_bwd", grid=(n_tiles,),
        in_specs=[tile(nb), halo(nb, _prev_halo), halo(nb, _next_halo), tile(half), halo(half, _next_halo),
                  const(fw), const(fb)] + [ANY_SPEC] * len(deps),
        out_specs=[tile(nb), const(fw), const(fb)],
        out_shape=[jax.ShapeDtypeStruct((nb, t, w), BF16), jax.ShapeDtypeStruct(fw.shape, F32),
                   jax.ShapeDtypeStruct(fb.shape, F32)],
        scratch_shapes=[pltpu.VMEM((2, ROW_TILE + HALO_ROWS, w), F32)],
        compiler_params=_params(),
    )(up_pre, up_pre, up_pre, dact, dact, fw, fb, *deps)


def _conv_taps(up_ref, prev, nxt, fw_ref, fb_ref, c0, r, rows=ROW_CHUNK):
    cols = pl.ds(c0, FFN_COLS)
    get = lambda s, n: up_ref[pl.ds(s, n), cols]
    x0, x1, x2 = (_tile_rows(get, prev, nxt, r - k * N_SEG, rows) for k in range(3))
    w = fw_ref[:, cols]
    return w[0:1] * x2 + w[1:2] * x1 + w[2:3] * x0 + fb_ref[:, cols], x2, x1, x0


def _gated_fwd(up_pre, fw, fb):
    t, f2 = up_pre.shape
    f = f2 // 2

    def body(up_ref, halo_ref, fw_ref, fb_ref, act_ref):
        first = pl.program_id(0) == 0
        for c0 in range(0, f, FFN_COLS):
            halo_a = _wrap_prev_halo(halo_ref[:, pl.ds(c0, FFN_COLS)], first)
            halo_v = _wrap_prev_halo(halo_ref[:, pl.ds(f + c0, FFN_COLS)], first)
            for r in range(0, ROW_TILE, ROW_CHUNK):
                a = _conv_taps(up_ref, halo_a, None, fw_ref, fb_ref, c0, r)[0]
                val = _conv_taps(up_ref, halo_v, None, fw_ref, fb_ref, f + c0, r)[0]
                act_ref[pl.ds(r, ROW_CHUNK), pl.ds(c0, FFN_COLS)] = (a * jax.nn.sigmoid(a) * val).astype(BF16)

    const = lambda a: pl.BlockSpec(a.shape, lambda i: (0, 0))
    return pl.pallas_call(
        body, name="ffn_fwd", grid=(t // ROW_TILE,),
        in_specs=[pl.BlockSpec((ROW_TILE, f2), lambda i: (i, 0)),
                  pl.BlockSpec((HALO_ROWS, f2), lambda i: (_prev_halo(i, t), 0)), const(fw), const(fb)],
        out_specs=pl.BlockSpec((ROW_TILE, f), lambda i: (i, 0)),
        out_shape=jax.ShapeDtypeStruct((t, f), BF16), compiler_params=_params(),
    )(up_pre, up_pre, fw, fb)


def _gated_bwd(up_pre, dact, fw, fb, deps=()):
    t, f2 = up_pre.shape
    f = f2 // 2
    n_tiles = t // ROW_TILE
    chunks = [(r, ROW_CHUNK) for r in range(0, ROW_TILE, ROW_CHUNK)] + [(ROW_TILE, HALO_ROWS)]

    def body(up_ref, prev_ref, next_ref, dact_ref, dact_next_ref, fw_ref, fb_ref,
             out_ref, dfw_ref, dfb_ref, dup_ref):
        first = pl.program_id(0) == 0
        last = pl.program_id(0) == n_tiles - 1

        @pl.when(first)
        def _():
            dfw_ref[...] = jnp.zeros_like(dfw_ref)
            dfb_ref[...] = jnp.zeros_like(dfb_ref)

        fold = lambda v: sum(v[s:s + SUBLANES] for s in range(0, ROW_CHUNK, SUBLANES))
        for c0 in range(0, f, FFN_COLS):
            starts = (c0, f + c0)
            prev = [_wrap_prev_halo(prev_ref[:, pl.ds(c, FFN_COLS)], first) for c in starts]
            nxt = [_wrap_next_halo(next_ref[:, pl.ds(c, FFN_COLS)], last) for c in starts]
            dact_next = _wrap_next_halo(dact_next_ref[:, pl.ds(c0, FFN_COLS)].astype(F32), last)
            get_dact = lambda s, n: dact_ref[pl.ds(s, n), pl.ds(c0, FFN_COLS)].astype(F32)
            sums = [[jnp.zeros((SUBLANES, FFN_COLS), F32)] * 4 for _ in starts]
            for r, rows in chunks:
                taps = [_conv_taps(up_ref, prev[k], nxt[k], fw_ref, fb_ref, c, r, rows)
                        for k, c in enumerate(starts)]
                a, val = taps[0][0], taps[1][0]
                da_ct = _tile_rows(get_dact, None, dact_next, r, rows)
                sg = jax.nn.sigmoid(a)
                dup = (da_ct * val * sg * (1.0 + a * (1.0 - sg)), da_ct * a * sg)
                for k in range(2):
                    dup_ref[k, pl.ds(r, rows), :] = dup[k]
                    if r < ROW_TILE:
                        terms = (dup[k], dup[k] * taps[k][1], dup[k] * taps[k][2], dup[k] * taps[k][3])
                        sums[k] = [s + fold(v) for s, v in zip(sums[k], terms)]
            for k, c in enumerate(starts):
                cols = pl.ds(c, FFN_COLS)
                s_b, s_w0, s_w1, s_w2 = (jnp.sum(p, axis=0, keepdims=True) for p in sums[k])
                dfb_ref[:, cols] += s_b
                for tap, s_w in enumerate((s_w0, s_w1, s_w2)):
                    dfw_ref[tap:tap + 1, cols] += s_w
                cw = fw_ref[:, cols]
                for r in range(0, ROW_TILE, ROW_CHUNK):
                    d, u1, u2 = (dup_ref[k, pl.ds(r + s * N_SEG, ROW_CHUNK), :] for s in range(3))
                    out_ref[pl.ds(r, ROW_CHUNK), cols] = (cw[2:3] * d + cw[1:2] * u1 + cw[0:1] * u2).astype(BF16)

    tile = lambda w: pl.BlockSpec((ROW_TILE, w), lambda i: (i, 0))
    halo = lambda w, index: pl.BlockSpec((HALO_ROWS, w), lambda i: (index(i, t), 0))
    const = lambda a: pl.BlockSpec(a.shape, lambda i: (0, 0))
    return pl.pallas_call(
        _ignoring_deps(body, 7, deps), name="ffn_bwd", grid=(n_tiles,),
        in_specs=[tile(f2), halo(f2, _prev_halo), halo(f2, _next_halo), tile(f), halo(f, _next_halo),
                  const(fw), const(fb)] + [ANY_SPEC] * len(deps),
        out_specs=[tile(f2), const(fw), const(fb)],
        out_shape=[jax.ShapeDtypeStruct((t, f2), BF16), jax.ShapeDtypeStruct(fw.shape, F32),
                   jax.ShapeDtypeStruct(fb.shape, F32)],
        scratch_shapes=[pltpu.VMEM((2, ROW_TILE + HALO_ROWS, FFN_COLS), F32)],
        compiler_params=_params(),
    )(up_pre, up_pre, up_pre, dact, dact, fw, fb, *deps)


def _to_segments(a):
    t, c = a.shape
    return a.reshape(N_SEG, t // N_SEG, c).transpose(1, 0, 2).reshape(t, c)


def _from_segments(a):
    t, c = a.shape
    return a.reshape(t // N_SEG, N_SEG, c).transpose(1, 0, 2).reshape(t, c)


def _cmul(ar, ai, br, bi):
    return ar * br - ai * bi, ar * bi + ai * br


def _segment_carries(pr, pi, fr, fi, forward):
    row = lax.broadcasted_iota(jnp.int32, fr.shape, 0)
    edge = row == (0 if forward else N_SEG - 1)
    shift = 1 if forward else N_SEG - 1
    sr, si = jnp.zeros_like(fr), jnp.zeros_like(fi)
    for _ in range(N_SEG - 1):
        tr, ti = _cmul(pr, pi, sr, si)
        sr = jnp.where(edge, 0.0, pltpu.roll(tr + fr, shift, 0))
        si = jnp.where(edge, 0.0, pltpu.roll(ti + fi, shift, 0))
    return sr, si


def _rows(i):
    return pl.ds(pl.multiple_of(i * SUBLANES, SUBLANES), SUBLANES)


def _s5_fwd(proj, u_col, bb_re, bb_im, a_re, a_im, cc_re, cc_im, d_skip):
    t = proj.shape[0]
    nch, _, cs = bb_re.shape
    ds = nch * SSM_CHUNK
    u_blk = u_col // SSM_CHUNK
    steps = t // N_SEG

    def body(u_ref, bbr_ref, bbi_ref, ar_ref, ai_ref, ccr_ref, cci_ref, d_ref, sr_ref, si_ref, y_ref):
        ub = u_ref[...].astype(BF16)
        sr_ref[...] = _dot(ub, bbr_ref[...], NN)
        si_ref[...] = _dot(ub, bbi_ref[...], NN)
        ar = jnp.broadcast_to(ar_ref[...], (N_SEG, cs))
        ai = jnp.broadcast_to(ai_ref[...], (N_SEG, cs))
        zero = jnp.zeros((N_SEG, cs), F32)

        def totals(i, carry):
            sr, si, pr, pi = carry
            tr, ti = _cmul(ar, ai, sr, si)
            qr, qi = _cmul(ar, ai, pr, pi)
            return tr + sr_ref[_rows(i), :], ti + si_ref[_rows(i), :], qr, qi

        fr, fi, pr, pi = lax.fori_loop(0, steps, totals, (zero, zero, zero + 1.0, zero))
        s0r, s0i = _segment_carries(pr, pi, fr, fi, True)

        def scan(i, carry):
            tr, ti = _cmul(ar, ai, *carry)
            nr, ni = tr + sr_ref[_rows(i), :], ti + si_ref[_rows(i), :]
            sr_ref[_rows(i), :] = nr
            si_ref[_rows(i), :] = ni
            return nr, ni

        lax.fori_loop(0, steps, scan, (s0r, s0i))
        y_ref[...] = (_dot(sr_ref[...].astype(BF16), ccr_ref[...], NN)
                      - _dot(si_ref[...].astype(BF16), cci_ref[...], NN)
                      + d_ref[...] * u_ref[...])

    chunk3 = lambda r, c: pl.BlockSpec((None, r, c), lambda j: (j, 0, 0))
    return pl.pallas_call(
        body, name="s5_fwd", grid=(nch,),
        in_specs=[pl.BlockSpec((t, SSM_CHUNK), lambda j: (0, j + u_blk)),
                  chunk3(SSM_CHUNK, cs), chunk3(SSM_CHUNK, cs), chunk3(1, cs), chunk3(1, cs),
                  chunk3(cs, SSM_CHUNK), chunk3(cs, SSM_CHUNK), chunk3(1, SSM_CHUNK)],
        out_specs=[pl.BlockSpec((t, cs), lambda j: (0, j)), pl.BlockSpec((t, cs), lambda j: (0, j)),
                   pl.BlockSpec((t, SSM_CHUNK), lambda j: (0, j))],
        out_shape=[jax.ShapeDtypeStruct((t, nch * cs), F32), jax.ShapeDtypeStruct((t, nch * cs), F32),
                   jax.ShapeDtypeStruct((t, ds), F32)],
        compiler_params=_params(),
    )(proj, bb_re, bb_im, a_re, a_im, cc_re, cc_im, d_skip)


def _s5_bwd(dy, proj, u_col, s_re, s_im, bb_re, bb_im, a_re, a_im, cc_re, cc_im, d_skip):
    t, ds = dy.shape
    nch, _, cs = bb_re.shape
    u_blk = u_col // SSM_CHUNK
    steps = t // N_SEG

    def body(dy_ref, u_ref, sr_ref, si_ref, bbr_ref, bbi_ref, ar_ref, ai_ref, ccr_ref, cci_ref, d_ref,
             du_ref, dbbr_ref, dbbi_ref, dar_ref, dai_ref, dccr_ref, dcci_ref, dd_ref, gr_ref, gi_ref):
        dyv = dy_ref[...]
        dyb = dyv.astype(BF16)
        gr_ref[...] = _dot(dyb, ccr_ref[...], NT)
        gi_ref[...] = -_dot(dyb, cci_ref[...], NT)
        ar = jnp.broadcast_to(ar_ref[...], (N_SEG, cs))
        ai = -jnp.broadcast_to(ai_ref[...], (N_SEG, cs))
        zero = jnp.zeros((N_SEG, cs), F32)

        def totals(k, carry):
            i = steps - 1 - k
            gr, gi, pr, pi = carry
            tr, ti = _cmul(ar, ai, gr, gi)
            qr, qi = _cmul(ar, ai, pr, pi)
            return tr + gr_ref[_rows(i), :], ti + gi_ref[_rows(i), :], qr, qi

        fr, fi, pr, pi = lax.fori_loop(0, steps, totals, (zero, zero, zero + 1.0, zero))
        e0r, e0i = _segment_carries(pr, pi, fr, fi, False)

        def step(i, gr, gi, pr, pi, acc_r, acc_i):
            tr, ti = _cmul(ar, ai, gr, gi)
            nr, ni = tr + gr_ref[_rows(i), :], ti + gi_ref[_rows(i), :]
            gr_ref[_rows(i), :] = nr
            gi_ref[_rows(i), :] = ni
            return nr, ni, acc_r + nr * pr + ni * pi, acc_i + ni * pr - nr * pi

        def scan(k, carry):
            i = steps - 1 - k
            gr, gi, acc_r, acc_i = carry
            return step(i, gr, gi, sr_ref[_rows(i - 1), :], si_ref[_rows(i - 1), :], acc_r, acc_i)

        gr, gi, acc_r, acc_i = lax.fori_loop(0, steps - 1, scan, (e0r, e0i, zero, zero))
        row = lax.broadcasted_iota(jnp.int32, (N_SEG, cs), 0)
        last = _rows(steps - 1)
        pr = jnp.where(row == 0, 0.0, pltpu.roll(sr_ref[last, :], 1, 0))
        pi = jnp.where(row == 0, 0.0, pltpu.roll(si_ref[last, :], 1, 0))
        _, _, acc_r, acc_i = step(0, gr, gi, pr, pi, acc_r, acc_i)
        dar_ref[...] = jnp.sum(acc_r, axis=0, keepdims=True)
        dai_ref[...] = jnp.sum(acc_i, axis=0, keepdims=True)

        uv = u_ref[...]
        ub = uv.astype(BF16)
        grb = gr_ref[...].astype(BF16)
        gib = gi_ref[...].astype(BF16)
        du = d_ref[...] * dyv + _dot(grb, bbr_ref[...], NT) + _dot(gib, bbi_ref[...], NT)
        du_ref[...] = du.astype(BF16)
        dbbr_ref[...] = _dot(ub, grb, TN)
        dbbi_ref[...] = _dot(ub, gib, TN)
        dccr_ref[...] = _dot(sr_ref[...].astype(BF16), dyb, TN)
        dcci_ref[...] = -_dot(si_ref[...].astype(BF16), dyb, TN)
        dd_ref[...] = jnp.sum(dyv * uv, axis=0, keepdims=True)

    chunk3 = lambda r, c: pl.BlockSpec((None, r, c), lambda j: (j, 0, 0))
    cols = lambda w: pl.BlockSpec((t, w), lambda j: (0, j))
    return pl.pallas_call(
        body, name="s5_bwd", grid=(nch,),
        in_specs=[cols(SSM_CHUNK), pl.BlockSpec((t, SSM_CHUNK), lambda j: (0, j + u_blk)), cols(cs), cols(cs),
                  chunk3(SSM_CHUNK, cs), chunk3(SSM_CHUNK, cs), chunk3(1, cs), chunk3(1, cs),
                  chunk3(cs, SSM_CHUNK), chunk3(cs, SSM_CHUNK), chunk3(1, SSM_CHUNK)],
        out_specs=[cols(SSM_CHUNK), chunk3(SSM_CHUNK, cs), chunk3(SSM_CHUNK, cs), chunk3(1, cs),
                   chunk3(1, cs), chunk3(cs, SSM_CHUNK), chunk3(cs, SSM_CHUNK), chunk3(1, SSM_CHUNK)],
        out_shape=[jax.ShapeDtypeStruct((t, ds), BF16),
                   jax.ShapeDtypeStruct((nch, SSM_CHUNK, cs), F32), jax.ShapeDtypeStruct((nch, SSM_CHUNK, cs), F32),
                   jax.ShapeDtypeStruct((nch, 1, cs), F32), jax.ShapeDtypeStruct((nch, 1, cs), F32),
                   jax.ShapeDtypeStruct((nch, cs, SSM_CHUNK), F32), jax.ShapeDtypeStruct((nch, cs, SSM_CHUNK), F32),
                   jax.ShapeDtypeStruct((nch, 1, SSM_CHUNK), F32)],
        scratch_shapes=[pltpu.VMEM((t, cs), F32), pltpu.VMEM((t, cs), F32)],
        compiler_params=_params(),
    )(dy, proj, s_re, s_im, bb_re, bb_im, a_re, a_im, cc_re, cc_im, d_skip)


def _discretize(lr, li, log_dt, br, bi):
    dt = jnp.exp(log_dt)
    mag = jnp.exp(lr * dt)
    ang = li * dt
    a_re = mag * jnp.cos(ang)
    a_im = mag * jnp.sin(ang)
    den = lr * lr + li * li
    nr = a_re - 1.0
    f_re = (nr * lr + a_im * li) / den
    f_im = (a_im * lr - nr * li) / den
    return a_re, a_im, f_re * br - f_im * bi, f_re * bi + f_im * br


def _whole(shape):
    return pl.BlockSpec(shape, lambda: (0,) * len(shape))


def _disc_fwd(lr, li, log_dt, br, bi, deps=()):
    def body(lr_ref, li_ref, dt_ref, br_ref, bi_ref, ar_ref, ai_ref, bbr_ref, bbi_ref):
        outs = _discretize(lr_ref[...], li_ref[...], dt_ref[...], br_ref[...], bi_ref[...])
        for ref, val in zip((ar_ref, ai_ref, bbr_ref, bbi_ref), outs):
            ref[...] = val

    args = (lr, li, log_dt, br, bi)
    outs = (lr, lr, br, br)
    return pl.pallas_call(
        _ignoring_deps(body, 5, deps), name="disc_fwd",
        in_specs=[_whole(a.shape) for a in args] + [ANY_SPEC] * len(deps),
        out_specs=[_whole(a.shape) for a in outs],
        out_shape=[jax.ShapeDtypeStruct(a.shape, F32) for a in outs],
    )(*args, *deps)


def _disc_bwd(lr, li, log_dt, br, bi, dar, dai, dbbr, dbbi):
    def body(lr_ref, li_ref, dt_ref, br_ref, bi_ref, dar_ref, dai_ref, dbbr_ref, dbbi_ref,
             dlr_ref, dli_ref, ddt_ref, dbr_ref, dbi_ref):
        _, vjp = jax.vjp(_discretize, lr_ref[...], li_ref[...], dt_ref[...], br_ref[...], bi_ref[...])
        grads = vjp((dar_ref[...], dai_ref[...], dbbr_ref[...], dbbi_ref[...]))
        for ref, val in zip((dlr_ref, dli_ref, ddt_ref, dbr_ref, dbi_ref), grads):
            ref[...] = val

    args = (lr, li, log_dt, br, bi, dar, dai, dbbr, dbbi)
    outs = (lr, li, log_dt, br, bi)
    return pl.pallas_call(
        body, name="disc_bwd", in_specs=[_whole(a.shape) for a in args],
        out_specs=[_whole(a.shape) for a in outs],
        out_shape=[jax.ShapeDtypeStruct(a.shape, F32) for a in outs],
    )(*args)


def _adamw(w, g, m, v):
    m = ADAM_B1 * m + (1.0 - ADAM_B1) * g
    v = ADAM_B2 * v + (1.0 - ADAM_B2) * (g * g)
    m_hat = m / ADAM_BC1
    v_hat = v / ADAM_BC2
    delta = -ADAM_LR * (m_hat / (jnp.sqrt(v_hat) + ADAM_EPS) + ADAM_WD * w)
    return delta, m, v


def _adamw_reduce(name, parts, w, m, v):
    _, r, c = parts.shape
    tr = r
    for cand in (256, 176, 128):
        if r % cand == 0:
            tr = cand
            break

    def body(p_ref, w_ref, m_ref, v_ref, g_ref, d_ref, nm_ref, nv_ref):
        g = p_ref[0].astype(F32)
        for k in range(1, N_DEV):
            g = g + p_ref[k].astype(F32)
        delta, nm, nv = _adamw(w_ref[...], g, m_ref[...], v_ref[...])
        g_ref[...] = g
        d_ref[...] = delta
        nm_ref[...] = nm
        nv_ref[...] = nv

    blk = pl.BlockSpec((tr, c), lambda i: (i, 0))
    return pl.pallas_call(
        body, name=name, grid=(r // tr,),
        in_specs=[pl.BlockSpec((N_DEV, tr, c), lambda i: (0, i, 0)), blk, blk, blk],
        out_specs=[blk] * 4, out_shape=[jax.ShapeDtypeStruct((r, c), F32)] * 4,
        compiler_params=_params(),
    )(parts, w, m, v)


def _sum_parts(name, parts):
    _, r, c = parts.shape

    def body(p_ref, o_ref):
        g = p_ref[0]
        for k in range(1, N_DEV):
            g = g + p_ref[k]
        o_ref[...] = g

    return pl.pallas_call(
        body, name=name, in_specs=[_whole(parts.shape)], out_specs=_whole((r, c)),
        out_shape=jax.ShapeDtypeStruct((r, c), F32), compiler_params=_params(),
    )(parts)


def _adamw_many(name, grads, ws, ms, vs):
    n = len(grads)

    def body(*refs):
        ins, outs = refs[:4 * n], refs[4 * n:]
        for i in range(n):
            g, w, m, v = (ins[j * n + i][...] for j in range(4))
            for ref, val in zip((outs[i], outs[n + i], outs[2 * n + i]), _adamw(w, g, m, v)):
                ref[...] = val

    args = (*grads, *ws, *ms, *vs)
    outs = pl.pallas_call(
        body, name=name, in_specs=[_whole(a.shape) for a in args],
        out_specs=[_whole(a.shape) for a in ws] * 3,
        out_shape=[jax.ShapeDtypeStruct(a.shape, F32) for a in ws] * 3, compiler_params=_params(),
    )(*args)
    return outs[:n], outs[n:2 * n], outs[2 * n:]


def _pack(arrays, rows):
    flat = jnp.concatenate([a.reshape(-1) for a in arrays])
    return jnp.pad(flat, (0, rows * LANES - flat.shape[0])).reshape(rows, LANES)


def _unpack(packed, shapes):
    flat = packed.reshape(-1)
    out, off = [], 0
    for s in shapes:
        n = math.prod(s)
        out.append(flat[off:off + n].reshape(s))
        off += n
    return out


def _packed_rows(shapes):
    n = sum(math.prod(s) for s in shapes)
    return -(-n // (SUBLANES * LANES)) * SUBLANES


def _block_diag(x):
    j, g, r, c = x.shape
    eye = jnp.eye(g, dtype=x.dtype)
    return (x[:, :, :, None, :] * eye[None, :, None, :, None]).reshape(j, g * r, g * c)


def _diag_blocks(x, g):
    j, gr, gc = x.shape
    r, c = gr // g, gc // g
    eye = jnp.eye(g, dtype=x.dtype)
    return (x.reshape(j, g, r, g, c) * eye[None, :, None, :, None]).sum(axis=3)


def kernel(x, meta_tokens, norm_mix_g, w_in, conv_w, ssm_lam_re, ssm_lam_im, ssm_log_dt, ssm_b_re, ssm_b_im, ssm_c_re, ssm_c_im, ssm_d, ssm_w_glu, gain_conv_out, gain_ssm_out, w_out, norm_ffn_g, w_up, ffn_conv_w, ffn_conv_b, w_down, norm_final_g, loss_target, m_meta_tokens, m_norm_mix_g, m_w_in, m_conv_w, m_ssm_lam_re, m_ssm_lam_im, m_ssm_log_dt, m_ssm_b_re, m_ssm_b_im, m_ssm_c_re, m_ssm_c_im, m_ssm_d, m_ssm_w_glu, m_gain_conv_out, m_gain_ssm_out, m_w_out, m_norm_ffn_g, m_w_up, m_ffn_conv_w, m_ffn_conv_b, m_w_down, m_norm_final_g, v_meta_tokens, v_norm_mix_g, v_w_in, v_conv_w, v_ssm_lam_re, v_ssm_lam_im, v_ssm_log_dt, v_ssm_b_re, v_ssm_b_im, v_ssm_c_re, v_ssm_c_im, v_ssm_d, v_ssm_w_glu, v_gain_conv_out, v_gain_ssm_out, v_w_out, v_norm_ffn_g, v_w_up, v_ffn_conv_w, v_ffn_conv_b, v_w_down, v_norm_final_g):
    weights = dict(meta_tokens=meta_tokens, norm_mix_g=norm_mix_g, w_in=w_in, conv_w=conv_w, ssm_lam_re=ssm_lam_re, ssm_lam_im=ssm_lam_im, ssm_log_dt=ssm_log_dt, ssm_b_re=ssm_b_re, ssm_b_im=ssm_b_im, ssm_c_re=ssm_c_re, ssm_c_im=ssm_c_im, ssm_d=ssm_d, ssm_w_glu=ssm_w_glu, gain_conv_out=gain_conv_out, gain_ssm_out=gain_ssm_out, w_out=w_out, norm_ffn_g=norm_ffn_g, w_up=w_up, ffn_conv_w=ffn_conv_w, ffn_conv_b=ffn_conv_b, w_down=w_down, norm_final_g=norm_final_g)
    mom_m = dict(meta_tokens=m_meta_tokens, norm_mix_g=m_norm_mix_g, w_in=m_w_in, conv_w=m_conv_w, ssm_lam_re=m_ssm_lam_re, ssm_lam_im=m_ssm_lam_im, ssm_log_dt=m_ssm_log_dt, ssm_b_re=m_ssm_b_re, ssm_b_im=m_ssm_b_im, ssm_c_re=m_ssm_c_re, ssm_c_im=m_ssm_c_im, ssm_d=m_ssm_d, ssm_w_glu=m_ssm_w_glu, gain_conv_out=m_gain_conv_out, gain_ssm_out=m_gain_ssm_out, w_out=m_w_out, norm_ffn_g=m_norm_ffn_g, w_up=m_w_up, ffn_conv_w=m_ffn_conv_w, ffn_conv_b=m_ffn_conv_b, w_down=m_w_down, norm_final_g=m_norm_final_g)
    mom_v = dict(meta_tokens=v_meta_tokens, norm_mix_g=v_norm_mix_g, w_in=v_w_in, conv_w=v_conv_w, ssm_lam_re=v_ssm_lam_re, ssm_lam_im=v_ssm_lam_im, ssm_log_dt=v_ssm_log_dt, ssm_b_re=v_ssm_b_re, ssm_b_im=v_ssm_b_im, ssm_c_re=v_ssm_c_re, ssm_c_im=v_ssm_c_im, ssm_d=v_ssm_d, ssm_w_glu=v_ssm_w_glu, gain_conv_out=v_gain_conv_out, gain_ssm_out=v_gain_ssm_out, w_out=v_w_out, norm_ffn_g=v_norm_ffn_g, w_up=v_w_up, ffn_conv_w=v_ffn_conv_w, ffn_conv_b=v_ffn_conv_b, w_down=v_w_down, norm_final_g=v_norm_final_g)
    names = list(weights)

    n_meta, d_meta = meta_tokens.shape
    seq, d = x.shape[1], x.shape[2]
    rows_used = n_meta + seq
    t = -(-rows_used // ROW_TILE) * ROW_TILE
    d_in_s = w_in.shape[2]
    dc_s = conv_w.shape[2]
    dc = dc_s * N_DEV
    ds = ssm_w_glu.shape[2]
    n_groups, n_state, grp = ssm_b_re.shape[1:]
    ns = n_groups * n_state
    nch = ds // SSM_CHUNK
    gpc = n_groups // nch
    ff_s = w_up.shape[2]
    dn_s = w_down.shape[1]
    assert 3 * dc + ds == d_in_s * N_DEV and 2 * dn_s == ff_s and t % (N_SEG * SUBLANES) == 0

    small_shard = jnp.concatenate([meta_tokens.reshape(-1), conv_w.reshape(-1), ffn_conv_w.reshape(-1)])
    n_small = small_shard.shape[0]
    small_rows = -(-n_small // LANES)
    small_shard = jnp.pad(small_shard, (0, small_rows * LANES - n_small)).reshape(small_rows, LANES)
    ag, ag_token = _gather2_start("gather_weights_start", [
        small_shard, w_in[0].astype(BF16), ssm_w_glu[0].astype(BF16), w_out[0].astype(BF16),
        jnp.swapaxes(w_up[0], 0, 1).astype(BF16), w_down[0].astype(BF16)])
    fb = ffn_conv_b

    gh = n_groups * grp
    per_h = lambda a: jnp.broadcast_to(a.reshape(n_groups, 1, -1), (n_groups, grp, n_state)).reshape(gh, n_state)
    ghp = lambda a: a.transpose(0, 1, 3, 2).reshape(gh, n_state)
    lr, li, log_dt_e = per_h(ssm_lam_re), per_h(ssm_lam_im), per_h(ssm_log_dt)
    br, bi = ghp(ssm_b_re), ghp(ssm_b_im)
    a_re, a_im, bb_re, bb_im = _disc_fwd(lr, li, log_dt_e, br, bi, deps=(ag_token,))
    cs = gpc * n_state
    chunk_row = lambda a: a.reshape(n_groups, grp, n_state)[:, 0].reshape(nch, 1, cs)
    to_bb = lambda a: _block_diag(a.reshape(nch, gpc, grp, n_state)).astype(BF16)
    to_cc = lambda a: _block_diag(a.reshape(nch, gpc, grp, n_state).transpose(0, 1, 3, 2)).astype(BF16)
    bbm_re, bbm_im = to_bb(bb_re), to_bb(bb_im)
    ccm_re, ccm_im = to_cc(ssm_c_re), to_cc(ssm_c_im)
    a_re_c, a_im_c = chunk_row(a_re), chunk_row(a_im)
    d_skip = ssm_d.reshape(nch, 1, SSM_CHUNK)

    target = _to_segments(jnp.pad(loss_target[0], ((n_meta, t - rows_used), (0, 0))))
    token = _gather2_forward("gather_weights_forward_first", ag, (0, 1), (bbm_im, ccm_im, target))
    (g_small,) = _gather2_wait("gather_weights_wait_small", ag, (0,), token)
    g_small = g_small.reshape(N_DEV, -1)
    o1 = n_meta * d_meta
    o2 = o1 + 3 * dc_s
    meta_full = g_small[:, :o1].reshape(N_DEV, n_meta, d_meta).transpose(1, 0, 2).reshape(n_meta, d)
    conv_w_f = g_small[:, o1:o2].reshape(N_DEV, 3, dc_s).transpose(1, 0, 2).reshape(3, dc)
    fw = g_small[:, o2:o2 + 3 * ff_s].reshape(N_DEV, 3, ff_s).transpose(1, 0, 2).reshape(3, N_DEV * ff_s)
    h0 = _to_segments(jnp.concatenate([meta_full, x[0], jnp.zeros((t - rows_used, d), F32)], axis=0))
    full_t =lambda w: pl.BlockSpec((t, w), lambda *_: (0, 0))

    hn1 = _rms_fwd("norm_mix", h0, norm_mix_g)
    (g_in,) = _gather2_wait("gather_weights_wait_in", ag, (1,), hn1)
    proj = _mm("proj", hn1, g_in, dims=NN, grid=(N_DEV,), a_spec=full_t(d),
               b_spec=pl.BlockSpec((None, d, d_in_s), lambda j: (j, 0, 0)),
               o_spec=pl.BlockSpec((t, d_in_s), lambda j: (0, j)),
               out_shape=jax.ShapeDtypeStruct((t, N_DEV * d_in_s), F32))
    s_re, s_im, y_ssm = _s5_fwd(proj, 3 * dc, bbm_re, bbm_im, a_re_c, a_im_c, ccm_re, ccm_im, d_skip)
    token = _gather2_forward("gather_weights_forward_up", ag, (2, 3, 4), (y_ssm,))
    g_glu, g_out = _gather2_wait("gather_weights_wait_mix", ag, (2, 3), token)
    w_out_f = g_out.reshape(-1, d)
    w_glu_f = g_glu.reshape(ds, ds)
    mixed, z_glu = _mix_fwd(proj, y_ssm, w_glu_f, conv_w_f, gain_conv_out, gain_ssm_out)
    tn_out = 256
    h1 = _mm("out_proj", mixed, w_out_f, dims=NN, grid=(d // tn_out,), a_spec=full_t(dc + ds),
             b_spec=pl.BlockSpec((dc + ds, tn_out), lambda j: (0, j)),
             o_spec=pl.BlockSpec((t, tn_out), lambda j: (0, j)),
             out_shape=jax.ShapeDtypeStruct((t, d), F32),
             res=h0, res_spec=pl.BlockSpec((t, tn_out), lambda j: (0, j)))
    hn2 = _rms_fwd("norm_ffn", h1, norm_ffn_g)
    token = _gather2_forward("gather_weights_forward_down", ag, (5,), (hn2,))
    (g_up,) = _gather2_wait("gather_weights_wait_up", ag, (4,), token)
    f2 = N_DEV * ff_s
    pair = 2 * ff_s
    w_up_t = g_up.reshape(f2, d)
    up_pre = _mm("up_proj", hn2, w_up_t, dims=NT, grid=(f2 // pair,), a_spec=full_t(d),
                 b_spec=pl.BlockSpec((pair, d), lambda j: (j, 0)),
                 o_spec=pl.BlockSpec((t, pair), lambda j: (0, j)),
                 out_shape=jax.ShapeDtypeStruct((t, f2), F32))
    act = _gated_fwd(up_pre, fw, fb)
    (g_down,) = _gather2_wait("gather_weights_wait_down", ag, (5,), act)
    w_down_f = g_down.reshape(f2 // 2, d)
    h2 = _mm("down_proj", act, w_down_f, dims=NN, grid=(d // tn_out,), a_spec=full_t(f2 // 2),
             b_spec=pl.BlockSpec((f2 // 2, tn_out), lambda j: (0, j)),
             o_spec=pl.BlockSpec((t, tn_out), lambda j: (0, j)),
             out_shape=jax.ShapeDtypeStruct((t, d), F32),
             res=h1, res_spec=pl.BlockSpec((t, tn_out), lambda j: (0, j)))

    dh2, dh2_b, loss_part, d_norm_final = _loss_bwd(h2, target, norm_final_g.reshape(1, d), n_meta, rows_used)
    dact = _mm("down_dgrad", dh2_b, w_down_f, dims=NT, grid=(f2 // 2 // pair,), a_spec=full_t(d),
               b_spec=pl.BlockSpec((pair, d), lambda j: (j, 0)),
               o_spec=pl.BlockSpec((t, pair), lambda j: (0, j)),
               out_shape=jax.ShapeDtypeStruct((t, f2 // 2), BF16))
    dw_down = _mm("down_wgrad", act, dh2_b, dims=TN, grid=(f2 // 2 // pair,),
                  a_spec=pl.BlockSpec((t, pair), lambda j: (0, j)), b_spec=full_t(d),
                  o_spec=pl.BlockSpec((pair, d), lambda j: (j, 0)),
                  out_shape=jax.ShapeDtypeStruct((f2 // 2, d), BF16))
    ex_down, token = _send_start("exchange_down_start", [dw_down.reshape(N_DEV, dn_s, d)], gather=False)
    dup_pre, d_fw, d_fb = _gated_bwd(up_pre, dact, fw, fb, deps=(token,))
    wide = _wide_tile(t)
    dhn2 = _mm("up_dgrad", dup_pre, w_up_t, dims=NN, grid=(t // wide,),
               a_spec=pl.BlockSpec((wide, f2), lambda i: (i, 0)),
               b_spec=pl.BlockSpec((f2, d), lambda i: (0, 0)),
               o_spec=pl.BlockSpec((wide, d), lambda i: (i, 0)),
               out_shape=jax.ShapeDtypeStruct((t, d), F32))
    dw_up = _mm("up_wgrad", dup_pre, hn2, dims=TN, grid=(f2 // pair,),
                a_spec=pl.BlockSpec((t, pair), lambda j: (0, j)), b_spec=full_t(d),
                o_spec=pl.BlockSpec((pair, d), lambda j: (j, 0)),
                out_shape=jax.ShapeDtypeStruct((f2, d), BF16))
    ex_up, token = _send_start("exchange_up_start", [dw_up.reshape(N_DEV, ff_s, d)], gather=False)
    dh1, dh1_b, d_norm_ffn = _rms_bwd_res("norm_ffn_bwd", dh2, dhn2, h1, norm_ffn_g, deps=(token,))
    dmixed = _mm("out_dgrad", dh1_b, w_out_f, dims=NT, grid=((dc + ds) // tn_out,), a_spec=full_t(d),
                 b_spec=pl.BlockSpec((tn_out, d), lambda i: (i, 0)),
                 o_spec=pl.BlockSpec((t, tn_out), lambda i: (0, i)),
                 out_shape=jax.ShapeDtypeStruct((t, dc + ds), F32))
    dw_out = _mm("out_wgrad", mixed, dh1_b, dims=TN, grid=((dc + ds) // tn_out,),
                 a_spec=pl.BlockSpec((t, tn_out), lambda i: (0, i)), b_spec=full_t(d),
                 o_spec=pl.BlockSpec((tn_out, d), lambda i: (i, 0)),
                 out_shape=jax.ShapeDtypeStruct((dc + ds, d), BF16))
    db_gate, dconv, dy_ssm, d_wglu, d_conv_w, d_gain_c, d_gain_s = _mix_bwd1(
        proj, y_ssm, z_glu, dmixed, w_glu_f, conv_w_f, gain_conv_out, gain_ssm_out)
    (du, d_bbm_re, d_bbm_im, d_a_re, d_a_im, d_ccm_re, d_ccm_im, d_dskip) = _s5_bwd(
        dy_ssm, proj, 3 * dc, s_re, s_im, bbm_re, bbm_im, a_re_c, a_im_c, ccm_re, ccm_im, d_skip)

    from_bb = lambda a: _diag_blocks(a, gpc).reshape(gh, n_state)
    from_cc = lambda a: _diag_blocks(a, gpc).transpose(0, 1, 3, 2).reshape(gh, n_state)
    first_h = lambda a: jnp.pad(a.reshape(n_groups, 1, n_state), ((0, 0), (0, grp - 1), (0, 0))).reshape(gh, n_state)
    over_h = lambda a: a.reshape(n_groups, grp, n_state).sum(axis=1)
    d_lr, d_li, d_dt_e, d_br, d_bi = _disc_bwd(
        lr, li, log_dt_e, br, bi, first_h(d_a_re), first_h(d_a_im), from_bb(d_bbm_re), from_bb(d_bbm_im))

    rep2d = dict(
        ssm_lam_re=(n_groups, n_state), ssm_lam_im=(n_groups, n_state), ssm_log_dt=(1, n_groups),
        ssm_b_re=(gh, n_state), ssm_b_im=(gh, n_state), ssm_c_re=(gh, n_state),
        ssm_c_im=(gh, n_state), ssm_d=(n_groups, grp), gain_conv_out=(1, dc),
        gain_ssm_out=(1, ds), norm_ffn_g=(1, d), ffn_conv_b=(1, N_DEV * ff_s), norm_final_g=(1, d))
    rep_names = list(rep2d)
    rep_grads = dict(
        ssm_lam_re=over_h(d_lr), ssm_lam_im=over_h(d_li), ssm_log_dt=over_h(d_dt_e).sum(axis=1),
        ssm_b_re=d_br, ssm_b_im=d_bi, ssm_c_re=from_cc(d_ccm_re), ssm_c_im=from_cc(d_ccm_im),
        ssm_d=d_dskip, gain_conv_out=d_gain_c, gain_ssm_out=d_gain_s, norm_ffn_g=d_norm_ffn,
        ffn_conv_b=d_fb, norm_final_g=d_norm_final)
    rep_shapes = [rep2d[n] for n in rep_names] + [(1, 1)]
    rep_rows = _packed_rows(rep_shapes)
    rep_pack = _pack([rep_grads[n] for n in rep_names] + [loss_part], rep_rows)
    ex_mix, token = _send_start("exchange_mix_start", [
        dw_out.reshape(N_DEV, -1, d), d_wglu.astype(BF16).reshape(N_DEV, -1, ds),
        d_conv_w.reshape(3, N_DEV, dc_s).transpose(1, 0, 2),
        d_fw.reshape(3, N_DEV, ff_s).transpose(1, 0, 2), rep_pack],
        gather=[False, False, False, False, True])
    dcdv = _mix_bwd2(proj, dconv, conv_w_f, deps=(token,))
    dproj = jnp.concatenate([db_gate, dcdv, du], axis=1)
    dhn1 = _mm("proj_dgrad", dproj, g_in, dims=NT, grid=(1, N_DEV),
               a_spec=pl.BlockSpec((t, d_in_s), lambda i, k: (0, k)),
               b_spec=pl.BlockSpec((None, d, d_in_s), lambda i, k: (k, 0, 0)),
               o_spec=pl.BlockSpec((t, d), lambda i, k: (0, 0)),
               out_shape=jax.ShapeDtypeStruct((t, d), F32), acc_shape=(t, d))
    dh0, _, d_norm_mix = _rms_bwd_res("norm_mix_bwd", dh1, dhn1, h0, norm_mix_g)
    dw_in = _mm("proj_wgrad", hn1, dproj, dims=TN, grid=(N_DEV,), a_spec=full_t(d),
                b_spec=pl.BlockSpec((t, d_in_s), lambda j: (0, j)),
                o_spec=pl.BlockSpec((None, d, d_in_s), lambda j: (j, 0, 0)),
                out_shape=jax.ShapeDtypeStruct((N_DEV, d, d_in_s), BF16))
    dh0 = _from_segments(dh0)
    grad_x = dh0[n_meta:rows_used][None]
    d_meta_b = dh0[:n_meta].reshape(n_meta, N_DEV, d_meta).transpose(1, 0, 2)
    ex_in, ex_in_token = _send_start("exchange_in_start", [dw_in, d_meta_b, d_norm_mix],
                                     gather=[False, False, True])

    shard_out = {}
    (rep_parts,) = _send_wait("gather_small_grads_wait", ex_mix, (4,), ex_in_token)
    rep_sum = _sum_parts("sum_small_grads", rep_parts)
    *rep_g, loss = _unpack(rep_sum, rep_shapes)
    loss = loss.reshape(())
    swapped = ("ssm_b_re", "ssm_b_im")
    to2d = lambda n, a: ghp(a) if n in swapped else a.reshape(rep2d[n])
    from2d = lambda n, a: (a.reshape(1, n_groups, grp, n_state).transpose(0, 1, 3, 2) if n in swapped
                           else a.reshape(weights[n].shape))
    as2d = lambda tree: [to2d(n, tree[n]) for n in rep_names]
    rep_res = _adamw_many("adamw_replicated", rep_g, as2d(weights), as2d(mom_m), as2d(mom_v))
    for i, n in enumerate(rep_names):
        shard_out[n] = [from2d(n, r) for r in (rep_g[i], *(res[i] for res in rep_res))]

    def update(n, parts, transposed=False):
        sh = weights[n].shape
        two_d = lambda a: (jnp.swapaxes(a[0], 0, 1) if transposed else a.reshape(parts.shape[1:]))
        res = _adamw_reduce("adamw_" + n, parts, two_d(weights[n]), two_d(mom_m[n]), two_d(mom_v[n]))
        shard_out[n] = [(jnp.swapaxes(r, 0, 1) if transposed else r).reshape(sh) for r in res]
        return res[0]

    (p_down,) = _send_wait("exchange_down_wait", ex_down, (0,), rep_sum)
    done = update("w_down", p_down)
    (p_up,) = _send_wait("exchange_up_wait", ex_up, (0,), done)
    done = update("w_up", p_up, transposed=True)
    p_out, p_glu, p_cw, p_fw = _send_wait("exchange_mix_wait", ex_mix, (0, 1, 2, 3), done)
    update("w_out", p_out)
    update("ssm_w_glu", p_glu)
    update("conv_w", p_cw)
    done = update("ffn_conv_w", p_fw)
    p_in, p_meta, p_nm = _send_wait("exchange_in_wait", ex_in, (0, 1, 2), done)
    update("w_in", p_in)
    update("meta_tokens", p_meta)
    update("norm_mix_g", p_nm)

    grads = [shard_out[n][0] for n in names]
    deltas = [shard_out[n][1] for n in names]
    new_m = [shard_out[n][2] for n in names]
    new_v = [shard_out[n][3] for n in names]
    return (loss, grad_x, *grads, *deltas, *new_m, *new_v)
```

```python
import functools
import math

import jax
import jax.numpy as jnp
from jax import lax
from jax.experimental import pallas as pl
from jax.experimental.pallas import tpu as pltpu

F32 = jnp.float32
BF16 = jnp.bfloat16
MESH = pl.DeviceIdType.MESH

N_DEV = 8
RMS_EPS = 1e-6
ADAM_LR = 0.001
ADAM_B1 = 0.9
ADAM_B2 = 0.999
ADAM_EPS = 1e-08
ADAM_WD = 0.01
ADAM_STEP = 10
ADAM_BC1 = 1.0 - ADAM_B1 ** ADAM_STEP
ADAM_BC2 = 1.0 - ADAM_B2 ** ADAM_STEP

SUBLANES = 8
LANES = 128
ROW_TILE = 128
ROW_CHUNK = 32
WIDE_TILES = (544, 256, 128)
FFN_COLS = 256
N_SEG = 8
HALO_ROWS = 16
SSM_CHUNK = 128
VMEM_LIMIT = 48 * 1024 * 1024

NN = ((1,), (0,))
NT = ((1,), (1,))
TN = ((0,), (0,))


def _params(**kw):
    return pltpu.CompilerParams(vmem_limit_bytes=VMEM_LIMIT, **kw)


def _dot(a, b, dims):
    return lax.dot_general(a, b, (dims, ((), ())), preferred_element_type=F32)


def _mean_sq_rsqrt(x):
    return lax.rsqrt(jnp.mean(x * x, axis=-1, keepdims=True) + RMS_EPS)


def _rms_bwd(x, r, g, dy):
    xhat = x * r
    dxh = dy * g
    dx = r * (dxh - xhat * jnp.mean(dxh * xhat, axis=-1, keepdims=True))
    return dx, dy * xhat


def _gelu(y):
    c = math.sqrt(2.0 / math.pi)
    t = jnp.tanh(c * (y + 0.044715 * y * y * y))
    return 0.5 * y * (1.0 + t), t


def _gelu_grad(y, t):
    c = math.sqrt(2.0 / math.pi)
    return 0.5 * (1.0 + t) + 0.5 * y * (1.0 - t * t) * c * (1.0 + 3.0 * 0.044715 * y * y)


def _wrap_prev_halo(halo, first):
    seg = lax.broadcasted_iota(jnp.int32, halo.shape, 0) % N_SEG
    wrapped = jnp.where(seg == 0, 0.0, pltpu.roll(halo, 1, 0))
    return jnp.where(first, wrapped, halo)


def _wrap_next_halo(halo, last):
    seg = lax.broadcasted_iota(jnp.int32, halo.shape, 0) % N_SEG
    wrapped = jnp.where(seg == N_SEG - 1, 0.0, pltpu.roll(halo, halo.shape[0] - 1, 0))
    return jnp.where(last, wrapped, halo)


def _tile_rows(get, prev, nxt, s, n, tile_rows=ROW_TILE):
    parts = []
    if s < 0:
        parts.append(prev[HALO_ROWS + s:HALO_ROWS + min(s + n, 0)])
    lo, hi = max(s, 0), min(s + n, tile_rows)
    if hi > lo:
        parts.append(get(lo, hi - lo))
    if s + n > tile_rows:
        parts.append(nxt[max(s - tile_rows, 0):s + n - tile_rows])
    return parts[0] if len(parts) == 1 else jnp.concatenate(parts, axis=0)


def _dev_index(p):
    return 4 * p[0] + 2 * p[1] + p[2]


def _allgather(name, shards, deps=()):
    n = len(shards)

    def body(*refs):
        ins, outs = refs[:n], refs[n:2 * n]
        send_sems, recv_sems, local_sems = refs[2 * n:]
        x, y, c = lax.axis_index("x"), lax.axis_index("y"), lax.axis_index("c")
        me, sibling = (x, y, c), (x, y, 1 - c)
        chips = [(1 - x, y), (x, 1 - y), (1 - x, 1 - y)]

        def copy(a, k, block, to, src=None):
            dst = outs[a].at[_dev_index(block)]
            return pltpu.make_async_remote_copy(
                src_ref=dst if src is None else src, dst_ref=dst,
                send_sem=send_sems.at[a, k], recv_sem=recv_sems.at[a, k],
                device_id=to, device_id_type=MESH)

        mine = [pltpu.make_async_copy(ins[a], outs[a].at[_dev_index(me)], local_sems.at[a])
                for a in range(n)]
        for cp in mine:
            cp.start()
        first = []
        for a in range(n):
            first.append(copy(a, 0, me, sibling, src=ins[a]))
            for j, chip in enumerate(chips):
                first.append(copy(a, 1 + j, me, (*chip, c), src=ins[a]))
        for cp in first:
            cp.start()
        passed = []
        for j, chip in enumerate(chips):
            for a in range(n):
                copy(a, 1 + j, (*chip, c), me).wait_recv()
                fwd = copy(a, 4 + j, (*chip, c), sibling)
                fwd.start()
                passed.append(fwd)
        for a in range(n):
            copy(a, 0, sibling, me).wait_recv()
            for j, chip in enumerate(chips):
                copy(a, 4 + j, (*chip, 1 - c), me).wait_recv()
        for cp in first + passed:
            cp.wait_send()
        for cp in mine:
            cp.wait()

    any_spec = pl.BlockSpec(memory_space=pl.ANY)
    return pl.pallas_call(
        _ignoring_deps(body, n, deps), name=name,
        out_shape=[jax.ShapeDtypeStruct((N_DEV,) + s.shape, s.dtype) for s in shards],
        in_specs=[any_spec] * (n + len(deps)), out_specs=[any_spec] * n,
        scratch_shapes=[pltpu.SemaphoreType.DMA((n, 7)), pltpu.SemaphoreType.DMA((n, 7)),
                        pltpu.SemaphoreType.DMA((n,))],
    )(*shards, *deps)


HBM_SPEC = pl.BlockSpec(memory_space=pltpu.HBM)
SEM_SPEC = pl.BlockSpec(memory_space=pltpu.SEMAPHORE)
ANY_SPEC = pl.BlockSpec(memory_space=pl.ANY)
DATAFLOW = pltpu.SideEffectType.DATAFLOW_SIDE_EFFECTING


def _ignoring_deps(body, n_in, deps):
    n_dep = len(deps)

    def wrapped(*refs):
        return body(*refs[:n_in], *refs[n_in + n_dep:])

    return wrapped


def _my_position():
    x, y, c = lax.axis_index("x"), lax.axis_index("y"), lax.axis_index("c")
    return (x, y, c)


def _peer(me, k):
    return tuple((1 - v) if (k >> s) & 1 else v for v, s in zip(me, (2, 1, 0)))


def _split_copies(src_refs, land_refs, send_sems, recv_sems, gather):
    me = _my_position()
    copies = []
    for a, (src, land) in enumerate(zip(src_refs, land_refs)):
        for k in range(1, N_DEV):
            peer = _peer(me, k)
            copies.append(pltpu.make_async_remote_copy(
                src_ref=src if gather[a] else src.at[_dev_index(peer)], dst_ref=land.at[_dev_index(me)],
                send_sem=send_sems[a].at[k - 1], recv_sem=recv_sems[a].at[k - 1],
                device_id=peer, device_id_type=MESH))
    return copies


def _own_slot(block, like_shape):
    me = _dev_index(_my_position())
    return lax.dynamic_update_index_in_dim(lax.empty(like_shape, block.dtype), block, me, 0)


def _send_start(name, srcs, gather):
    n = len(srcs)
    me = _dev_index(_my_position())
    gather = [gather] * n if isinstance(gather, bool) else list(gather)
    lands = [_own_slot(s, (N_DEV,) + s.shape) if g else
             _own_slot(lax.dynamic_index_in_dim(s, me, 0, keepdims=False), s.shape)
             for s, g in zip(srcs, gather)]

    def body(*refs):
        src_refs, land_refs = refs[:n], refs[n:2 * n]
        send_sems, recv_sems = refs[2 * n:3 * n], refs[3 * n:4 * n]
        token = refs[-1]
        for cp in _split_copies(src_refs, land_refs, send_sems, recv_sems, gather):
            cp.start()
        token[...] = jnp.zeros_like(token)

    hbm = lambda a: pltpu.HBM(a.shape, a.dtype)
    sems = [pltpu.SemaphoreType.DMA((N_DEV - 1,))] * n
    outs = pl.pallas_call(
        body, name=name,
        out_shape=(*sems, *sems, *[hbm(s) for s in srcs], *[hbm(l) for l in lands],
                   jax.ShapeDtypeStruct((SUBLANES, LANES), F32)),
        in_specs=[HBM_SPEC] * (2 * n),
        out_specs=(*[SEM_SPEC] * (2 * n), *[HBM_SPEC] * (2 * n), pl.BlockSpec(memory_space=pltpu.VMEM)),
        input_output_aliases={i: 2 * n + i for i in range(2 * n)},
        compiler_params=pltpu.CompilerParams(has_side_effects=DATAFLOW),
    )(*[pltpu.with_memory_space_constraint(a, pltpu.HBM) for a in (*srcs, *lands)])
    state = dict(send=outs[:n], recv=outs[n:2 * n], srcs=outs[2 * n:3 * n], lands=outs[3 * n:4 * n],
                 gather=gather)
    return state, outs[-1]


def _send_wait(name, state, which, after):
    n = len(which)
    pick = lambda key: [state[key][i] for i in which]
    gather = pick("gather")

    def body(*refs):
        src_refs, land_refs = refs[:n], refs[n:2 * n]
        send_sems, recv_sems = refs[2 * n:3 * n], refs[3 * n:4 * n]
        for cp in _split_copies(src_refs, land_refs, send_sems, recv_sems, gather):
            cp.wait_send()
            cp.wait_recv()

    srcs, lands = pick("srcs"), pick("lands")
    hbm = lambda a: pltpu.HBM(a.shape, a.dtype)
    outs = pl.pallas_call(
        body, name=name,
        out_shape=(*[hbm(s) for s in srcs], *[hbm(l) for l in lands]),
        in_specs=[*[HBM_SPEC] * (2 * n), *[SEM_SPEC] * (2 * n), ANY_SPEC],
        out_specs=tuple([HBM_SPEC] * (2 * n)),
        input_output_aliases={i: i for i in range(2 * n)},
        compiler_params=pltpu.CompilerParams(has_side_effects=DATAFLOW),
    )(*srcs, *lands, *pick("send"), *pick("recv"), after)
    return outs[n:]


def _two_level_copies(src_refs, land_refs, sems1, sems2):
    x, y, c = _my_position()
    me, sibling = (x, y, c), (x, y, 1 - c)
    chips = [(1 - x, y), (x, 1 - y), (1 - x, 1 - y)]
    stage1, stage2 = [], []
    for a, land in enumerate(land_refs):
        def copy(src, block, to, sems, k):
            return pltpu.make_async_remote_copy(
                src_ref=src, dst_ref=land.at[_dev_index(block)], send_sem=sems[0][a].at[k],
                recv_sem=sems[1][a].at[k], device_id=to, device_id_type=MESH)
        src = src_refs[a] if src_refs is not None else land.at[_dev_index(me)]
        stage1.append([copy(src, me, sibling, sems1, 0)] +
                      [copy(src, me, (*chip, c), sems1, 1 + j) for j, chip in enumerate(chips)])
        if sems2 is not None:
            stage2.append([copy(land.at[_dev_index((*chip, c))], (*chip, c), sibling, sems2, j)
                           for j, chip in enumerate(chips)])
    return stage1, stage2


def _gather2_start(name, shards):
    n = len(shards)
    lands = [_own_slot(s, (N_DEV,) + s.shape) for s in shards]

    def body(*refs):
        src_refs, land_refs = refs[:n], refs[n:2 * n]
        sems1 = (refs[2 * n:3 * n], refs[3 * n:4 * n])
        stage1, _ = _two_level_copies(src_refs, land_refs, sems1, None)
        for copies in stage1:
            for cp in copies:
                cp.start()
        refs[-1][...] = jnp.zeros_like(refs[-1])

    hbm = lambda a: pltpu.HBM(a.shape, a.dtype)
    sems = [pltpu.SemaphoreType.DMA((4,))] * n
    outs = pl.pallas_call(
        body, name=name,
        out_shape=(*sems, *sems, *[hbm(s) for s in shards], *[hbm(l) for l in lands],
                   jax.ShapeDtypeStruct((SUBLANES, LANES), F32)),
        in_specs=[HBM_SPEC] * (2 * n),
        out_specs=(*[SEM_SPEC] * (2 * n), *[HBM_SPEC] * (2 * n), pl.BlockSpec(memory_space=pltpu.VMEM)),
        input_output_aliases={i: 2 * n + i for i in range(2 * n)},
        compiler_params=pltpu.CompilerParams(has_side_effects=DATAFLOW),
    )(*[pltpu.with_memory_space_constraint(a, pltpu.HBM) for a in (*shards, *lands)])
    state = dict(send1=list(outs[:n]), recv1=list(outs[n:2 * n]), srcs=list(outs[2 * n:3 * n]),
                 lands=list(outs[3 * n:4 * n]), send2={}, recv2={})
    return state, outs[-1]


def _gather2_forward(name, state, which, after):
    n = len(which)
    pick = lambda key: [state[key][i] for i in which]

    def body(*refs):
        land_refs, recv1 = refs[:n], refs[n:2 * n]
        outs = refs[2 * n + len(after):]
        sems2 = (outs[:n], outs[n:2 * n])
        stage1, stage2 = _two_level_copies(None, land_refs, (recv1, recv1), sems2)
        for a in range(n):
            for j in range(3):
                stage1[a][1 + j].wait_recv()
                stage2[a][j].start()
        outs[-1][...] = jnp.zeros_like(outs[-1])

    lands = pick("lands")
    sems = [pltpu.SemaphoreType.DMA((3,))] * n
    outs = pl.pallas_call(
        body, name=name,
        out_shape=(*sems, *sems, *[pltpu.HBM(l.shape, l.dtype) for l in lands],
                   jax.ShapeDtypeStruct((SUBLANES, LANES), F32)),
        in_specs=[*[HBM_SPEC] * n, *[SEM_SPEC] * n, *[ANY_SPEC] * len(after)],
        out_specs=(*[SEM_SPEC] * (2 * n), *[HBM_SPEC] * n, pl.BlockSpec(memory_space=pltpu.VMEM)),
        input_output_aliases={i: 2 * n + i for i in range(n)},
        compiler_params=pltpu.CompilerParams(has_side_effects=DATAFLOW),
    )(*lands, *pick("recv1"), *after)
    for idx, i in enumerate(which):
        state["send2"][i], state["recv2"][i] = outs[idx], outs[n + idx]
        state["lands"][i] = outs[2 * n + idx]
    return outs[-1]


def _gather2_wait(name, state, which, after):
    n = len(which)
    pick = lambda key: [state[key][i] for i in which]

    def body(*refs):
        src_refs, land_refs = refs[:n], refs[n:2 * n]
        sems1 = (refs[2 * n:3 * n], refs[3 * n:4 * n])
        sems2 = (refs[4 * n:5 * n], refs[5 * n:6 * n])
        stage1, stage2 = _two_level_copies(src_refs, land_refs, sems1, sems2)
        for a in range(n):
            for cp in stage1[a]:
                cp.wait_send()
            stage1[a][0].wait_recv()
            for cp in stage2[a]:
                cp.wait_send()
                cp.wait_recv()

    srcs, lands = pick("srcs"), pick("lands")
    hbm = lambda a: pltpu.HBM(a.shape, a.dtype)
    outs = pl.pallas_call(
        body, name=name,
        out_shape=(*[hbm(s) for s in srcs], *[hbm(l) for l in lands]),
        in_specs=[*[HBM_SPEC] * (2 * n), *[SEM_SPEC] * (4 * n), ANY_SPEC],
        out_specs=tuple([HBM_SPEC] * (2 * n)),
        input_output_aliases={i: i for i in range(2 * n)},
        compiler_params=pltpu.CompilerParams(has_side_effects=DATAFLOW),
    )(*srcs, *lands, *pick("send1"), *pick("recv1"), *pick("send2"), *pick("recv2"), after)
    return outs[n:]


def _mm(name, a, b, *, dims, grid, a_spec, b_spec, o_spec, out_shape, acc_shape=None,
        res=None, res_spec=None):
    n_red = grid[-1] if acc_shape is not None else 1
    red_axis = len(grid) - 1

    def body(*refs):
        a_ref, b_ref = refs[0], refs[1]
        r_ref = refs[2] if res is not None else None
        o_ref = refs[3] if res is not None else refs[2]
        part = _dot(a_ref[...], b_ref[...], dims)
        if acc_shape is None:
            if r_ref is not None:
                part = part + r_ref[...]
            o_ref[...] = part.astype(o_ref.dtype)
            return
        acc_ref = refs[-1]
        k = pl.program_id(red_axis)

        @pl.when(k == 0)
        def _():
            acc_ref[...] = part

        @pl.when(k > 0)
        def _():
            acc_ref[...] += part

        @pl.when(k == n_red - 1)
        def _():
            total = acc_ref[...]
            if r_ref is not None:
                total = total + r_ref[...]
            o_ref[...] = total.astype(o_ref.dtype)

    ins, in_specs = [a, b], [a_spec, b_spec]
    if res is not None:
        ins.append(res)
        in_specs.append(res_spec)
    return pl.pallas_call(
        body, name=name, grid=grid, in_specs=in_specs, out_specs=o_spec, out_shape=out_shape,
        scratch_shapes=[pltpu.VMEM(acc_shape, F32)] if acc_shape is not None else [],
        compiler_params=_params(),
    )(*ins)


def _wide_tile(t):
    return next(c for c in WIDE_TILES if t % c == 0)


def _rms_fwd(name, h, g):
    t, d = h.shape
    tile = _wide_tile(t)

    def body(h_ref, g_ref, o_ref):
        x = h_ref[...]
        o_ref[...] = (x * _mean_sq_rsqrt(x) * g_ref[...]).astype(BF16)

    return pl.pallas_call(
        body, name=name, grid=(t // tile,),
        in_specs=[pl.BlockSpec((tile, d), lambda i: (i, 0)), pl.BlockSpec((1, d), lambda i: (0, 0))],
        out_specs=pl.BlockSpec((tile, d), lambda i: (i, 0)),
        out_shape=jax.ShapeDtypeStruct((t, d), BF16), compiler_params=_params(),
    )(h, g)


def _rms_bwd_res(name, dres, dhn, h, g, deps=()):
    t, d = h.shape

    def body(dres_ref, dhn_ref, h_ref, g_ref, dh_ref, dhb_ref, dg_ref):
        x = h_ref[...]
        dx, dgt = _rms_bwd(x, _mean_sq_rsqrt(x), g_ref[...], dhn_ref[...])
        dh = dres_ref[...] + dx
        dh_ref[...] = dh
        dhb_ref[...] = dh.astype(BF16)

        @pl.when(pl.program_id(0) == 0)
        def _():
            dg_ref[...] = jnp.zeros_like(dg_ref)

        dg_ref[...] += jnp.sum(dgt, axis=0, keepdims=True)

    tile = _wide_tile(t)
    row = pl.BlockSpec((tile, d), lambda i: (i, 0))
    vec = pl.BlockSpec((1, d), lambda i: (0, 0))
    return pl.pallas_call(
        _ignoring_deps(body, 4, deps), name=name, grid=(t // tile,),
        in_specs=[row, row, row, vec] + [ANY_SPEC] * len(deps), out_specs=[row, row, vec],
        out_shape=[jax.ShapeDtypeStruct((t, d), F32), jax.ShapeDtypeStruct((t, d), BF16),
                   jax.ShapeDtypeStruct((1, d), F32)],
        compiler_params=_params(),
    )(dres, dhn, h, g, *deps)


def _loss_bwd(h2, target, g, row_lo, row_hi):
    t, d = h2.shape
    tile = _wide_tile(t)

    def body(h_ref, tg_ref, g_ref, dh_ref, dhb_ref, loss_ref, dg_ref):
        i = pl.program_id(0)
        x = h_ref[...]
        r = _mean_sq_rsqrt(x)
        gv = g_ref[...]
        y = x * r * gv
        rows = i * tile + lax.broadcasted_iota(jnp.int32, (tile, 1), 0)
        time = (rows % N_SEG) * (t // N_SEG) + rows // N_SEG
        valid = jnp.logical_and(time >= row_lo, time < row_hi)
        err = jnp.where(valid, y - tg_ref[...], 0.0)
        dy = err * (1.0 / d)
        dx, dgt = _rms_bwd(x, r, gv, dy)
        dh_ref[...] = dx
        dhb_ref[...] = dx.astype(BF16)

        @pl.when(i == 0)
        def _():
            loss_ref[...] = jnp.zeros_like(loss_ref)
            dg_ref[...] = jnp.zeros_like(dg_ref)

        row_loss = jnp.mean(err * err, axis=-1, keepdims=True)
        loss_ref[...] += 0.5 * jnp.sum(row_loss, axis=0, keepdims=True)
        dg_ref[...] += jnp.sum(dgt, axis=0, keepdims=True)

    row = pl.BlockSpec((tile, d), lambda i: (i, 0))
    vec = pl.BlockSpec((1, d), lambda i: (0, 0))
    return pl.pallas_call(
        body, name="loss_bwd", grid=(t // tile,), in_specs=[row, row, vec],
        out_specs=[row, row, pl.BlockSpec((1, 1), lambda i: (0, 0)), vec],
        out_shape=[jax.ShapeDtypeStruct((t, d), F32), jax.ShapeDtypeStruct((t, d), BF16),
                   jax.ShapeDtypeStruct((1, 1), F32), jax.ShapeDtypeStruct((1, d), F32)],
        compiler_params=_params(),
    )(h2, target, g)


def _prev_halo(i, t):
    return jnp.where(i == 0, t // HALO_ROWS - 1, i * (ROW_TILE // HALO_ROWS) - 1)


def _next_halo(i, t):
    return jnp.where(i == t // ROW_TILE - 1, 0, (i + 1) * (ROW_TILE // HALO_ROWS))


def _causal_taps(cur, halo, first):
    rows = cur.shape[0]
    ext = jnp.concatenate([_wrap_prev_halo(halo, first), cur], axis=0)
    return ext[:rows], ext[N_SEG:N_SEG + rows]


def _anticausal_taps(cur, halo, last):
    rows = cur.shape[0]
    ext = jnp.concatenate([cur, _wrap_next_halo(halo, last)], axis=0)
    return ext[N_SEG:N_SEG + rows], ext[2 * N_SEG:2 * N_SEG + rows]


def _mix_fwd(proj, y, w_glu, conv_w, gain_c, gain_s):
    t = proj.shape[0]
    dc = conv_w.shape[1]
    ds = y.shape[1]

    def body(p_ref, halo_ref, y_ref, wg_ref, cw_ref, gc_ref, gs_ref, mixed_ref, z_ref):
        i = pl.program_id(0)
        p = p_ref[...]
        b, c, v = p[:, :dc], p[:, dc:2 * dc], p[:, 2 * dc:3 * dc]
        cv = c * v
        hp = halo_ref[...]
        x2, x1 = _causal_taps(cv, hp[:, dc:2 * dc] * hp[:, 2 * dc:3 * dc], i == 0)
        cw = cw_ref[...]
        conv = cw[0:1] * x2 + cw[1:2] * x1 + cw[2:3] * cv
        co = b * conv
        mixed_ref[:, :dc] = (co * _mean_sq_rsqrt(co) * gc_ref[...]).astype(BF16)
        g, _ = _gelu(y_ref[...])
        z = _dot(g.astype(BF16), wg_ref[...], NN)
        z_ref[...] = z
        so = g * jax.nn.sigmoid(z)
        mixed_ref[:, dc:] = (so * _mean_sq_rsqrt(so) * gs_ref[...]).astype(BF16)

    const = lambda i: (0, 0)
    return pl.pallas_call(
        body, name="mix_fwd", grid=(t // ROW_TILE,),
        in_specs=[pl.BlockSpec((ROW_TILE, 3 * dc), lambda i: (i, 0)),
                  pl.BlockSpec((HALO_ROWS, 3 * dc), lambda i: (_prev_halo(i, t), 0)),
                  pl.BlockSpec((ROW_TILE, ds), lambda i: (i, 0)),
                  pl.BlockSpec((ds, ds), const), pl.BlockSpec(conv_w.shape, const),
                  pl.BlockSpec((1, dc), const), pl.BlockSpec((1, ds), const)],
        out_specs=[pl.BlockSpec((ROW_TILE, dc + ds), lambda i: (i, 0)),
                   pl.BlockSpec((ROW_TILE, ds), lambda i: (i, 0))],
        out_shape=[jax.ShapeDtypeStruct((t, dc + ds), BF16), jax.ShapeDtypeStruct((t, ds), F32)],
        compiler_params=_params(),
    )(proj, proj, y, w_glu, conv_w, gain_c, gain_s)


def _mix_bwd1(proj, y, z, dmixed, w_glu, conv_w, gain_c, gain_s):
    t = proj.shape[0]
    dc = conv_w.shape[1]
    ds = y.shape[1]

    def body(p_ref, halo_ref, y_ref, z_ref, dm_ref, wg_ref, cw_ref, gc_ref, gs_ref,
             db_ref, dconv_ref, dy_ref, dwg_ref, dcw_ref, dgc_ref, dgs_ref):
        i = pl.program_id(0)

        @pl.when(i == 0)
        def _():
            dwg_ref[...] = jnp.zeros_like(dwg_ref)
            dcw_ref[...] = jnp.zeros_like(dcw_ref)
            dgc_ref[...] = jnp.zeros_like(dgc_ref)
            dgs_ref[...] = jnp.zeros_like(dgs_ref)

        p = p_ref[...]
        b, c, v = p[:, :dc], p[:, dc:2 * dc], p[:, 2 * dc:3 * dc]
        cv = c * v
        hp = halo_ref[...]
        x2, x1 = _causal_taps(cv, hp[:, dc:2 * dc] * hp[:, 2 * dc:3 * dc], i == 0)
        cw = cw_ref[...]
        conv = cw[0:1] * x2 + cw[1:2] * x1 + cw[2:3] * cv
        co = b * conv
        dm = dm_ref[...]
        dco, dgt = _rms_bwd(co, _mean_sq_rsqrt(co), gc_ref[...], dm[:, :dc])
        dgc_ref[...] += jnp.sum(dgt, axis=0, keepdims=True)
        db_ref[...] = (dco * conv).astype(BF16)
        dconv = dco * b
        dconv_ref[...] = dconv
        dcw_ref[0:1, :] += jnp.sum(dconv * x2, axis=0, keepdims=True)
        dcw_ref[1:2, :] += jnp.sum(dconv * x1, axis=0, keepdims=True)
        dcw_ref[2:3, :] += jnp.sum(dconv * cv, axis=0, keepdims=True)

        yv = y_ref[...]
        g, th = _gelu(yv)
        sg = jax.nn.sigmoid(z_ref[...])
        so = g * sg
        dso, dgt = _rms_bwd(so, _mean_sq_rsqrt(so), gs_ref[...], dm[:, dc:])
        dgs_ref[...] += jnp.sum(dgt, axis=0, keepdims=True)
        dz = (dso * g * sg * (1.0 - sg)).astype(BF16)
        dg = dso * sg + _dot(dz, wg_ref[...], NT)
        dwg_ref[...] += _dot(g.astype(BF16), dz, TN)
        dy_ref[...] = dg * _gelu_grad(yv, th)

    const = lambda i: (0, 0)
    row = lambda w: pl.BlockSpec((ROW_TILE, w), lambda i: (i, 0))
    return pl.pallas_call(
        body, name="mix_bwd1", grid=(t // ROW_TILE,),
        in_specs=[row(3 * dc), pl.BlockSpec((HALO_ROWS, 3 * dc), lambda i: (_prev_halo(i, t), 0)),
                  row(ds), row(ds), row(dc + ds), pl.BlockSpec((ds, ds), const),
                  pl.BlockSpec(conv_w.shape, const), pl.BlockSpec((1, dc), const),
                  pl.BlockSpec((1, ds), const)],
        out_specs=[row(dc), row(dc), row(ds), pl.BlockSpec((ds, ds), const),
                   pl.BlockSpec(conv_w.shape, const), pl.BlockSpec((1, dc), const),
                   pl.BlockSpec((1, ds), const)],
        out_shape=[jax.ShapeDtypeStruct((t, dc), BF16), jax.ShapeDtypeStruct((t, dc), F32),
                   jax.ShapeDtypeStruct((t, ds), F32), jax.ShapeDtypeStruct((ds, ds), F32),
                   jax.ShapeDtypeStruct(conv_w.shape, F32), jax.ShapeDtypeStruct((1, dc), F32),
                   jax.ShapeDtypeStruct((1, ds), F32)],
        compiler_params=_params(),
    )(proj, proj, y, z, dmixed, w_glu, conv_w, gain_c, gain_s)


def _mix_bwd2(proj, dconv, conv_w, deps=()):
    t = proj.shape[0]
    dc = conv_w.shape[1]
    n_tiles = t // ROW_TILE

    def body(c_ref, v_ref, d_ref, halo_ref, cw_ref, o_ref):
        i = pl.program_id(0)
        d = d_ref[...]
        u1, u2 = _anticausal_taps(d, halo_ref[...], i == n_tiles - 1)
        cw = cw_ref[...]
        dcv = cw[2:3] * d + cw[1:2] * u1 + cw[0:1] * u2
        o_ref[:, :dc] = (dcv * v_ref[...]).astype(BF16)
        o_ref[:, dc:] = (dcv * c_ref[...]).astype(BF16)

    return pl.pallas_call(
        _ignoring_deps(body, 5, deps), name="mix_bwd2", grid=(n_tiles,),
        in_specs=[pl.BlockSpec((ROW_TILE, dc), lambda i: (i, 1)),
                  pl.BlockSpec((ROW_TILE, dc), lambda i: (i, 2)),
                  pl.BlockSpec((ROW_TILE, dc), lambda i: (i, 0)),
                  pl.BlockSpec((HALO_ROWS, dc), lambda i: (_next_halo(i, t), 0)),
                  pl.BlockSpec(conv_w.shape, lambda i: (0, 0))] + [ANY_SPEC] * len(deps),
        out_specs=pl.BlockSpec((ROW_TILE, 2 * dc), lambda i: (i, 0)),
        out_shape=jax.ShapeDtypeStruct((t, 2 * dc), BF16), compiler_params=_params(),
    )(proj, proj, dconv, dconv, conv_w, *deps)


def _ffn_taps(up_ref, prev, nxt, fw_ref, fb_ref, q, r, rows=ROW_CHUNK):
    get = lambda s, n: up_ref[q, pl.ds(s, n), :]
    x0, x1, x2 = (_tile_rows(get, prev, nxt, r - k * N_SEG, rows) for k in range(3))
    w = fw_ref[q]
    return w[0:1] * x2 + w[1:2] * x1 + w[2:3] * x0 + fb_ref[q], x2, x1, x0


def _ffn_fwd(up_pre, fw, fb):
    nb, t, w = up_pre.shape
    half = nb // 2

    def body(up_ref, halo_ref, fw_ref, fb_ref, act_ref):
        first = pl.program_id(0) == 0
        for q in range(half):
            halo_a = _wrap_prev_halo(halo_ref[q], first)
            halo_v = _wrap_prev_halo(halo_ref[q + half], first)
            for r in range(0, ROW_TILE, ROW_CHUNK):
                a = _ffn_taps(up_ref, halo_a, None, fw_ref, fb_ref, q, r)[0]
                val = _ffn_taps(up_ref, halo_v, None, fw_ref, fb_ref, q + half, r)[0]
                act_ref[q, pl.ds(r, ROW_CHUNK), :] = (a * jax.nn.sigmoid(a) * val).astype(BF16)

    return pl.pallas_call(
        body, name="ffn_fwd", grid=(t // ROW_TILE,),
        in_specs=[pl.BlockSpec((nb, ROW_TILE, w), lambda i: (0, i, 0)),
                  pl.BlockSpec((nb, HALO_ROWS, w), lambda i: (0, _prev_halo(i, t), 0)),
                  pl.BlockSpec(fw.shape, lambda i: (0, 0, 0)),
                  pl.BlockSpec(fb.shape, lambda i: (0, 0, 0))],
        out_specs=pl.BlockSpec((half, ROW_TILE, w), lambda i: (0, i, 0)),
        out_shape=jax.ShapeDtypeStruct((half, t, w), BF16), compiler_params=_params(),
    )(up_pre, up_pre, fw, fb)


def _ffn_bwd(up_pre, dact, fw, fb, deps=()):
    nb, t, w = up_pre.shape
    half = nb // 2
    n_tiles = t // ROW_TILE
    chunks = [(r, ROW_CHUNK) for r in range(0, ROW_TILE, ROW_CHUNK)] + [(ROW_TILE, HALO_ROWS)]

    def body(up_ref, prev_ref, next_ref, dact_ref, dact_next_ref, fw_ref, fb_ref,
             out_ref, dfw_ref, dfb_ref, dup_ref):
        first = pl.program_id(0) == 0
        last = pl.program_id(0) == n_tiles - 1

        @pl.when(first)
        def _():
            dfw_ref[...] = jnp.zeros_like(dfw_ref)
            dfb_ref[...] = jnp.zeros_like(dfb_ref)

        for q in range(half):
            blocks = (q, q + half)
            prev = [_wrap_prev_halo(prev_ref[b], first) for b in blocks]
            nxt = [_wrap_next_halo(next_ref[b], last) for b in blocks]
            dact_next = _wrap_next_halo(dact_next_ref[q].astype(F32), last)
            get_dact = lambda s, n: dact_ref[q, pl.ds(s, n), :].astype(F32)
            sums = [[jnp.zeros((SUBLANES, w), F32)] * 4 for _ in blocks]
            fold = lambda v: sum(v[s:s + SUBLANES] for s in range(0, ROW_CHUNK, SUBLANES))
            for r, rows in chunks:
                taps = [_ffn_taps(up_ref, prev[k], nxt[k], fw_ref, fb_ref, b, r, rows)
                        for k, b in enumerate(blocks)]
                a, val = taps[0][0], taps[1][0]
                da_ct = _tile_rows(get_dact, None, dact_next, r, rows)
                sg = jax.nn.sigmoid(a)
                dup = (da_ct * val * sg * (1.0 + a * (1.0 - sg)), da_ct * a * sg)
                for k in range(2):
                    dup_ref[k, pl.ds(r, rows), :] = dup[k]
                    if r < ROW_TILE:
                        terms = (dup[k], dup[k] * taps[k][1], dup[k] * taps[k][2], dup[k] * taps[k][3])
                        sums[k] = [s + fold(v) for s, v in zip(sums[k], terms)]
            for k, b in enumerate(blocks):
                s_b, s_w0, s_w1, s_w2 = (jnp.sum(p, axis=0, keepdims=True) for p in sums[k])
                dfb_ref[b] += s_b
                for tap, s_w in enumerate((s_w0, s_w1, s_w2)):
                    dfw_ref[b, tap:tap + 1, :] += s_w
                cw = fw_ref[b]
                for r in range(0, ROW_TILE, ROW_CHUNK):
                    d, u1, u2 = (dup_ref[k, pl.ds(r + s * N_SEG, ROW_CHUNK), :] for s in range(3))
                    out_ref[b, pl.ds(r, ROW_CHUNK), :] = (cw[2:3] * d + cw[1:2] * u1 + cw[0:1] * u2).astype(BF16)

    tile = lambda n: pl.BlockSpec((n, ROW_TILE, w), lambda i: (0, i, 0))
    halo = lambda n, index: pl.BlockSpec((n, HALO_ROWS, w), lambda i: (0, index(i, t), 0))
    const = lambda a: pl.BlockSpec(a.shape, lambda i: (0, 0, 0))
    return pl.pallas_call(
        _ignoring_deps(body, 7, deps), name="ffn_bwd", grid=(n_tiles,),
        in_specs=[tile(nb), halo(nb, _prev_halo), halo(nb, _next_halo), tile(half), halo(half, _next_halo),
                  const(fw), const(fb)] + [ANY_SPEC] * len(deps),
        out_specs=[tile(nb), const(fw), const(fb)],
        out_shape=[jax.ShapeDtypeStruct((nb, t, w), BF16), jax.ShapeDtypeStruct(fw.shape, F32),
                   jax.ShapeDtypeStruct(fb.shape, F32)],
        scratch_shapes=[pltpu.VMEM((2, ROW_TILE + HALO_ROWS, w), F32)],
        compiler_params=_params(),
    )(up_pre, up_pre, up_pre, dact, dact, fw, fb, *deps)


def _conv_taps(up_ref, prev, nxt, fw_ref, fb_ref, c0, r, rows=ROW_CHUNK):
    cols = pl.ds(c0, FFN_COLS)
    get = lambda s, n: up_ref[pl.ds(s, n), cols]
    x0, x1, x2 = (_tile_rows(get, prev, nxt, r - k * N_SEG, rows) for k in range(3))
    w = fw_ref[:, cols]
    return w[0:1] * x2 + w[1:2] * x1 + w[2:3] * x0 + fb_ref[:, cols], x2, x1, x0


def _gated_fwd(up_pre, fw, fb):
    t, f2 = up_pre.shape
    f = f2 // 2

    def body(up_ref, halo_ref, fw_ref, fb_ref, act_ref):
        first = pl.program_id(0) == 0
        for c0 in range(0, f, FFN_COLS):
            halo_a = _wrap_prev_halo(halo_ref[:, pl.ds(c0, FFN_COLS)], first)
            halo_v = _wrap_prev_halo(halo_ref[:, pl.ds(f + c0, FFN_COLS)], first)
            for r in range(0, ROW_TILE, ROW_CHUNK):
                a = _conv_taps(up_ref, halo_a, None, fw_ref, fb_ref, c0, r)[0]
                val = _conv_taps(up_ref, halo_v, None, fw_ref, fb_ref, f + c0, r)[0]
                act_ref[pl.ds(r, ROW_CHUNK), pl.ds(c0, FFN_COLS)] = (a * jax.nn.sigmoid(a) * val).astype(BF16)

    const = lambda a: pl.BlockSpec(a.shape, lambda i: (0, 0))
    return pl.pallas_call(
        body, name="ffn_fwd", grid=(t // ROW_TILE,),
        in_specs=[pl.BlockSpec((ROW_TILE, f2), lambda i: (i, 0)),
                  pl.BlockSpec((HALO_ROWS, f2), lambda i: (_prev_halo(i, t), 0)), const(fw), const(fb)],
        out_specs=pl.BlockSpec((ROW_TILE, f), lambda i: (i, 0)),
        out_shape=jax.ShapeDtypeStruct((t, f), BF16), compiler_params=_params(),
    )(up_pre, up_pre, fw, fb)


def _gated_bwd(up_pre, dact, fw, fb, deps=()):
    t, f2 = up_pre.shape
    f = f2 // 2
    n_tiles = t // ROW_TILE
    chunks = [(r, ROW_CHUNK) for r in range(0, ROW_TILE, ROW_CHUNK)] + [(ROW_TILE, HALO_ROWS)]

    def body(up_ref, prev_ref, next_ref, dact_ref, dact_next_ref, fw_ref, fb_ref,
             out_ref, dfw_ref, dfb_ref, dup_ref):
        first = pl.program_id(0) == 0
        last = pl.program_id(0) == n_tiles - 1

        @pl.when(first)
        def _():
            dfw_ref[...] = jnp.zeros_like(dfw_ref)
            dfb_ref[...] = jnp.zeros_like(dfb_ref)

        fold = lambda v: sum(v[s:s + SUBLANES] for s in range(0, ROW_CHUNK, SUBLANES))
        for c0 in range(0, f, FFN_COLS):
            starts = (c0, f + c0)
            prev = [_wrap_prev_halo(prev_ref[:, pl.ds(c, FFN_COLS)], first) for c in starts]
            nxt = [_wrap_next_halo(next_ref[:, pl.ds(c, FFN_COLS)], last) for c in starts]
            dact_next = _wrap_next_halo(dact_next_ref[:, pl.ds(c0, FFN_COLS)].astype(F32), last)
            get_dact = lambda s, n: dact_ref[pl.ds(s, n), pl.ds(c0, FFN_COLS)].astype(F32)
            sums = [[jnp.zeros((SUBLANES, FFN_COLS), F32)] * 4 for _ in starts]
            for r, rows in chunks:
                taps = [_conv_taps(up_ref, prev[k], nxt[k], fw_ref, fb_ref, c, r, rows)
                        for k, c in enumerate(starts)]
                a, val = taps[0][0], taps[1][0]
                da_ct = _tile_rows(get_dact, None, dact_next, r, rows)
                sg = jax.nn.sigmoid(a)
                dup = (da_ct * val * sg * (1.0 + a * (1.0 - sg)), da_ct * a * sg)
                for k in range(2):
                    dup_ref[k, pl.ds(r, rows), :] = dup[k]
                    if r < ROW_TILE:
                        terms = (dup[k], dup[k] * taps[k][1], dup[k] * taps[k][2], dup[k] * taps[k][3])
                        sums[k] = [s + fold(v) for s, v in zip(sums[k], terms)]
            for k, c in enumerate(starts):
                cols = pl.ds(c, FFN_COLS)
                s_b, s_w0, s_w1, s_w2 = (jnp.sum(p, axis=0, keepdims=True) for p in sums[k])
                dfb_ref[:, cols] += s_b
                for tap, s_w in enumerate((s_w0, s_w1, s_w2)):
                    dfw_ref[tap:tap + 1, cols] += s_w
                cw = fw_ref[:, cols]
                for r in range(0, ROW_TILE, ROW_CHUNK):
                    d, u1, u2 = (dup_ref[k, pl.ds(r + s * N_SEG, ROW_CHUNK), :] for s in range(3))
                    out_ref[pl.ds(r, ROW_CHUNK), cols] = (cw[2:3] * d + cw[1:2] * u1 + cw[0:1] * u2).astype(BF16)

    tile = lambda w: pl.BlockSpec((ROW_TILE, w), lambda i: (i, 0))
    halo = lambda w, index: pl.BlockSpec((HALO_ROWS, w), lambda i: (index(i, t), 0))
    const = lambda a: pl.BlockSpec(a.shape, lambda i: (0, 0))
    return pl.pallas_call(
        _ignoring_deps(body, 7, deps), name="ffn_bwd", grid=(n_tiles,),
        in_specs=[tile(f2), halo(f2, _prev_halo), halo(f2, _next_halo), tile(f), halo(f, _next_halo),
                  const(fw), const(fb)] + [ANY_SPEC] * len(deps),
        out_specs=[tile(f2), const(fw), const(fb)],
        out_shape=[jax.ShapeDtypeStruct((t, f2), BF16), jax.ShapeDtypeStruct(fw.shape, F32),
                   jax.ShapeDtypeStruct(fb.shape, F32)],
        scratch_shapes=[pltpu.VMEM((2, ROW_TILE + HALO_ROWS, FFN_COLS), F32)],
        compiler_params=_params(),
    )(up_pre, up_pre, up_pre, dact, dact, fw, fb, *deps)


def _to_segments(a):
    t, c = a.shape
    return a.reshape(N_SEG, t // N_SEG, c).transpose(1, 0, 2).reshape(t, c)


def _from_segments(a):
    t, c = a.shape
    return a.reshape(t // N_SEG, N_SEG, c).transpose(1, 0, 2).reshape(t, c)


def _cmul(ar, ai, br, bi):
    return ar * br - ai * bi, ar * bi + ai * br


def _cpow(ar, ai, n):
    out = None
    while n:
        if n & 1:
            out = (ar, ai) if out is None else _cmul(out[0], out[1], ar, ai)
        ar, ai = _cmul(ar, ai, ar, ai)
        n >>= 1
    return out


def _segment_carries(pr, pi, fr, fi, forward):
    row = lax.broadcasted_iota(jnp.int32, fr.shape, 0)
    edge = row == (0 if forward else N_SEG - 1)
    shift = 1 if forward else N_SEG - 1
    sr, si = jnp.zeros_like(fr), jnp.zeros_like(fi)
    for _ in range(N_SEG - 1):
        tr, ti = _cmul(pr, pi, sr, si)
        sr = jnp.where(edge, 0.0, pltpu.roll(tr + fr, shift, 0))
        si = jnp.where(edge, 0.0, pltpu.roll(ti + fi, shift, 0))
    return sr, si


def _rows(i):
    return pl.ds(pl.multiple_of(i * SUBLANES, SUBLANES), SUBLANES)


def _s5_fwd(proj, u_col, bb_re, bb_im, a_re, a_im, cc_re, cc_im, d_skip):
    t = proj.shape[0]
    nch, _, cs = bb_re.shape
    ds = nch * SSM_CHUNK
    u_blk = u_col // SSM_CHUNK
    steps = t // N_SEG

    def body(u_ref, bbr_ref, bbi_ref, ar_ref, ai_ref, ccr_ref, cci_ref, d_ref, sr_ref, si_ref, y_ref):
        ub = u_ref[...].astype(BF16)
        sr_ref[...] = _dot(ub, bbr_ref[...], NN)
        si_ref[...] = _dot(ub, bbi_ref[...], NN)
        ar = jnp.broadcast_to(ar_ref[...], (N_SEG, cs))
        ai = jnp.broadcast_to(ai_ref[...], (N_SEG, cs))
        zero = jnp.zeros((N_SEG, cs), F32)

        def totals(i, carry):
            tr, ti = _cmul(ar, ai, *carry)
            return tr + sr_ref[_rows(i), :], ti + si_ref[_rows(i), :]

        fr, fi = lax.fori_loop(0, steps, totals, (zero, zero))
        s0r, s0i = _segment_carries(*_cpow(ar, ai, steps), fr, fi, True)

        def scan(i, carry):
            tr, ti = _cmul(ar, ai, *carry)
            nr, ni = tr + sr_ref[_rows(i), :], ti + si_ref[_rows(i), :]
            sr_ref[_rows(i), :] = nr
            si_ref[_rows(i), :] = ni
            return nr, ni

        lax.fori_loop(0, steps, scan, (s0r, s0i))
        y_ref[...] = (_dot(sr_ref[...].astype(BF16), ccr_ref[...], NN)
                      - _dot(si_ref[...].astype(BF16), cci_ref[...], NN)
                      + d_ref[...] * u_ref[...])

    chunk3 = lambda r, c: pl.BlockSpec((None, r, c), lambda j: (j, 0, 0))
    return pl.pallas_call(
        body, name="s5_fwd", grid=(nch,),
        in_specs=[pl.BlockSpec((t, SSM_CHUNK), lambda j: (0, j + u_blk)),
                  chunk3(SSM_CHUNK, cs), chunk3(SSM_CHUNK, cs), chunk3(1, cs), chunk3(1, cs),
                  chunk3(cs, SSM_CHUNK), chunk3(cs, SSM_CHUNK), chunk3(1, SSM_CHUNK)],
        out_specs=[pl.BlockSpec((t, cs), lambda j: (0, j)), pl.BlockSpec((t, cs), lambda j: (0, j)),
                   pl.BlockSpec((t, SSM_CHUNK), lambda j: (0, j))],
        out_shape=[jax.ShapeDtypeStruct((t, nch * cs), F32), jax.ShapeDtypeStruct((t, nch * cs), F32),
                   jax.ShapeDtypeStruct((t, ds), F32)],
        compiler_params=_params(),
    )(proj, bb_re, bb_im, a_re, a_im, cc_re, cc_im, d_skip)


def _s5_bwd(dy, proj, u_col, s_re, s_im, bb_re, bb_im, a_re, a_im, cc_re, cc_im, d_skip):
    t, ds = dy.shape
    nch, _, cs = bb_re.shape
    u_blk = u_col // SSM_CHUNK
    steps = t // N_SEG

    def body(dy_ref, u_ref, sr_ref, si_ref, bbr_ref, bbi_ref, ar_ref, ai_ref, ccr_ref, cci_ref, d_ref,
             du_ref, dbbr_ref, dbbi_ref, dar_ref, dai_ref, dccr_ref, dcci_ref, dd_ref, gr_ref, gi_ref):
        dyv = dy_ref[...]
        dyb = dyv.astype(BF16)
        gr_ref[...] = _dot(dyb, ccr_ref[...], NT)
        gi_ref[...] = -_dot(dyb, cci_ref[...], NT)
        ar = jnp.broadcast_to(ar_ref[...], (N_SEG, cs))
        ai = -jnp.broadcast_to(ai_ref[...], (N_SEG, cs))
        zero = jnp.zeros((N_SEG, cs), F32)

        def totals(k, carry):
            i = steps - 1 - k
            tr, ti = _cmul(ar, ai, *carry)
            return tr + gr_ref[_rows(i), :], ti + gi_ref[_rows(i), :]

        fr, fi = lax.fori_loop(0, steps, totals, (zero, zero))
        e0r, e0i = _segment_carries(*_cpow(ar, ai, steps), fr, fi, False)

        def step(i, gr, gi, pr, pi, acc_r, acc_i):
            tr, ti = _cmul(ar, ai, gr, gi)
            nr, ni = tr + gr_ref[_rows(i), :], ti + gi_ref[_rows(i), :]
            gr_ref[_rows(i), :] = nr
            gi_ref[_rows(i), :] = ni
            return nr, ni, acc_r + nr * pr + ni * pi, acc_i + ni * pr - nr * pi

        def scan(k, carry):
            i = steps - 1 - k
            gr, gi, acc_r, acc_i = carry
            return step(i, gr, gi, sr_ref[_rows(i - 1), :], si_ref[_rows(i - 1), :], acc_r, acc_i)

        gr, gi, acc_r, acc_i = lax.fori_loop(0, steps - 1, scan, (e0r, e0i, zero, zero))
        row = lax.broadcasted_iota(jnp.int32, (N_SEG, cs), 0)
        last = _rows(steps - 1)
        pr = jnp.where(row == 0, 0.0, pltpu.roll(sr_ref[last, :], 1, 0))
        pi = jnp.where(row == 0, 0.0, pltpu.roll(si_ref[last, :], 1, 0))
        _, _, acc_r, acc_i = step(0, gr, gi, pr, pi, acc_r, acc_i)
        dar_ref[...] = jnp.sum(acc_r, axis=0, keepdims=True)
        dai_ref[...] = jnp.sum(acc_i, axis=0, keepdims=True)

        uv = u_ref[...]
        ub = uv.astype(BF16)
        grb = gr_ref[...].astype(BF16)
        gib = gi_ref[...].astype(BF16)
        du = d_ref[...] * dyv + _dot(grb, bbr_ref[...], NT) + _dot(gib, bbi_ref[...], NT)
        du_ref[...] = du.astype(BF16)
        dbbr_ref[...] = _dot(ub, grb, TN)
        dbbi_ref[...] = _dot(ub, gib, TN)
        dccr_ref[...] = _dot(sr_ref[...].astype(BF16), dyb, TN)
        dcci_ref[...] = -_dot(si_ref[...].astype(BF16), dyb, TN)
        dd_ref[...] = jnp.sum(dyv * uv, axis=0, keepdims=True)

    chunk3 = lambda r, c: pl.BlockSpec((None, r, c), lambda j: (j, 0, 0))
    cols = lambda w: pl.BlockSpec((t, w), lambda j: (0, j))
    return pl.pallas_call(
        body, name="s5_bwd", grid=(nch,),
        in_specs=[cols(SSM_CHUNK), pl.BlockSpec((t, SSM_CHUNK), lambda j: (0, j + u_blk)), cols(cs), cols(cs),
                  chunk3(SSM_CHUNK, cs), chunk3(SSM_CHUNK, cs), chunk3(1, cs), chunk3(1, cs),
                  chunk3(cs, SSM_CHUNK), chunk3(cs, SSM_CHUNK), chunk3(1, SSM_CHUNK)],
        out_specs=[cols(SSM_CHUNK), chunk3(SSM_CHUNK, cs), chunk3(SSM_CHUNK, cs), chunk3(1, cs),
                   chunk3(1, cs), chunk3(cs, SSM_CHUNK), chunk3(cs, SSM_CHUNK), chunk3(1, SSM_CHUNK)],
        out_shape=[jax.ShapeDtypeStruct((t, ds), BF16),
                   jax.ShapeDtypeStruct((nch, SSM_CHUNK, cs), F32), jax.ShapeDtypeStruct((nch, SSM_CHUNK, cs), F32),
                   jax.ShapeDtypeStruct((nch, 1, cs), F32), jax.ShapeDtypeStruct((nch, 1, cs), F32),
                   jax.ShapeDtypeStruct((nch, cs, SSM_CHUNK), F32), jax.ShapeDtypeStruct((nch, cs, SSM_CHUNK), F32),
                   jax.ShapeDtypeStruct((nch, 1, SSM_CHUNK), F32)],
        scratch_shapes=[pltpu.VMEM((t, cs), F32), pltpu.VMEM((t, cs), F32)],
        compiler_params=_params(),
    )(dy, proj, s_re, s_im, bb_re, bb_im, a_re, a_im, cc_re, cc_im, d_skip)


def _discretize(lr, li, log_dt, br, bi):
    dt = jnp.exp(log_dt)
    mag = jnp.exp(lr * dt)
    ang = li * dt
    a_re = mag * jnp.cos(ang)
    a_im = mag * jnp.sin(ang)
    den = lr * lr + li * li
    nr = a_re - 1.0
    f_re = (nr * lr + a_im * li) / den
    f_im = (a_im * lr - nr * li) / den
    return a_re, a_im, f_re * br - f_im * bi, f_re * bi + f_im * br


def _whole(shape):
    return pl.BlockSpec(shape, lambda: (0,) * len(shape))


def _disc_fwd(lr, li, log_dt, br, bi, deps=()):
    def body(lr_ref, li_ref, dt_ref, br_ref, bi_ref, ar_ref, ai_ref, bbr_ref, bbi_ref):
        outs = _discretize(lr_ref[...], li_ref[...], dt_ref[...], br_ref[...], bi_ref[...])
        for ref, val in zip((ar_ref, ai_ref, bbr_ref, bbi_ref), outs):
            ref[...] = val

    args = (lr, li, log_dt, br, bi)
    outs = (lr, lr, br, br)
    return pl.pallas_call(
        _ignoring_deps(body, 5, deps), name="disc_fwd",
        in_specs=[_whole(a.shape) for a in args] + [ANY_SPEC] * len(deps),
        out_specs=[_whole(a.shape) for a in outs],
        out_shape=[jax.ShapeDtypeStruct(a.shape, F32) for a in outs],
    )(*args, *deps)


def _disc_bwd(lr, li, log_dt, br, bi, dar, dai, dbbr, dbbi):
    def body(lr_ref, li_ref, dt_ref, br_ref, bi_ref, dar_ref, dai_ref, dbbr_ref, dbbi_ref,
             dlr_ref, dli_ref, ddt_ref, dbr_ref, dbi_ref):
        _, vjp = jax.vjp(_discretize, lr_ref[...], li_ref[...], dt_ref[...], br_ref[...], bi_ref[...])
        grads = vjp((dar_ref[...], dai_ref[...], dbbr_ref[...], dbbi_ref[...]))
        for ref, val in zip((dlr_ref, dli_ref, ddt_ref, dbr_ref, dbi_ref), grads):
            ref[...] = val

    args = (lr, li, log_dt, br, bi, dar, dai, dbbr, dbbi)
    outs = (lr, li, log_dt, br, bi)
    return pl.pallas_call(
        body, name="disc_bwd", in_specs=[_whole(a.shape) for a in args],
        out_specs=[_whole(a.shape) for a in outs],
        out_shape=[jax.ShapeDtypeStruct(a.shape, F32) for a in outs],
    )(*args)


def _adamw(w, g, m, v):
    m = ADAM_B1 * m + (1.0 - ADAM_B1) * g
    v = ADAM_B2 * v + (1.0 - ADAM_B2) * (g * g)
    m_hat = m / ADAM_BC1
    v_hat = v / ADAM_BC2
    delta = -ADAM_LR * (m_hat / (jnp.sqrt(v_hat) + ADAM_EPS) + ADAM_WD * w)
    return delta, m, v


def _adamw_reduce(name, parts, w, m, v):
    _, r, c = parts.shape
    tr = r
    for cand in (256, 176, 128):
        if r % cand == 0:
            tr = cand
            break

    def body(p_ref, w_ref, m_ref, v_ref, g_ref, d_ref, nm_ref, nv_ref):
        g = p_ref[0].astype(F32)
        for k in range(1, N_DEV):
            g = g + p_ref[k].astype(F32)
        delta, nm, nv = _adamw(w_ref[...], g, m_ref[...], v_ref[...])
        g_ref[...] = g
        d_ref[...] = delta
        nm_ref[...] = nm
        nv_ref[...] = nv

    blk = pl.BlockSpec((tr, c), lambda i: (i, 0))
    return pl.pallas_call(
        body, name=name, grid=(r // tr,),
        in_specs=[pl.BlockSpec((N_DEV, tr, c), lambda i: (0, i, 0)), blk, blk, blk],
        out_specs=[blk] * 4, out_shape=[jax.ShapeDtypeStruct((r, c), F32)] * 4,
        compiler_params=_params(),
    )(parts, w, m, v)


def _sum_parts(name, parts):
    _, r, c = parts.shape

    def body(p_ref, o_ref):
        g = p_ref[0]
        for k in range(1, N_DEV):
            g = g + p_ref[k]
        o_ref[...] = g

    return pl.pallas_call(
        body, name=name, in_specs=[_whole(parts.shape)], out_specs=_whole((r, c)),
        out_shape=jax.ShapeDtypeStruct((r, c), F32), compiler_params=_params(),
    )(parts)


def _adamw_many(name, grads, ws, ms, vs):
    n = len(grads)

    def body(*refs):
        ins, outs = refs[:4 * n], refs[4 * n:]
        for i in range(n):
            g, w, m, v = (ins[j * n + i][...] for j in range(4))
            for ref, val in zip((outs[i], outs[n + i], outs[2 * n + i]), _adamw(w, g, m, v)):
                ref[...] = val

    args = (*grads, *ws, *ms, *vs)
    outs = pl.pallas_call(
        body, name=name, in_specs=[_whole(a.shape) for a in args],
        out_specs=[_whole(a.shape) for a in ws] * 3,
        out_shape=[jax.ShapeDtypeStruct(a.shape, F32) for a in ws] * 3, compiler_params=_params(),
    )(*args)
    return outs[:n], outs[n:2 * n], outs[2 * n:]


def _pack(arrays, rows):
    flat = jnp.concatenate([a.reshape(-1) for a in arrays])
    return jnp.pad(flat, (0, rows * LANES - flat.shape[0])).reshape(rows, LANES)


def _unpack(packed, shapes):
    flat = packed.reshape(-1)
    out, off = [], 0
    for s in shapes:
        n = math.prod(s)
        out.append(flat[off:off + n].reshape(s))
        off += n
    return out


def _packed_rows(shapes):
    n = sum(math.prod(s) for s in shapes)
    return -(-n // (SUBLANES * LANES)) * SUBLANES


def _block_diag(x):
    j, g, r, c = x.shape
    eye = jnp.eye(g, dtype=x.dtype)
    return (x[:, :, :, None, :] * eye[None, :, None, :, None]).reshape(j, g * r, g * c)


def _diag_blocks(x, g):
    j, gr, gc = x.shape
    r, c = gr // g, gc // g
    eye = jnp.eye(g, dtype=x.dtype)
    return (x.reshape(j, g, r, g, c) * eye[None, :, None, :, None]).sum(axis=3)


def kernel(x, meta_tokens, norm_mix_g, w_in, conv_w, ssm_lam_re, ssm_lam_im, ssm_log_dt, ssm_b_re, ssm_b_im, ssm_c_re, ssm_c_im, ssm_d, ssm_w_glu, gain_conv_out, gain_ssm_out, w_out, norm_ffn_g, w_up, ffn_conv_w, ffn_conv_b, w_down, norm_final_g, loss_target, m_meta_tokens, m_norm_mix_g, m_w_in, m_conv_w, m_ssm_lam_re, m_ssm_lam_im, m_ssm_log_dt, m_ssm_b_re, m_ssm_b_im, m_ssm_c_re, m_ssm_c_im, m_ssm_d, m_ssm_w_glu, m_gain_conv_out, m_gain_ssm_out, m_w_out, m_norm_ffn_g, m_w_up, m_ffn_conv_w, m_ffn_conv_b, m_w_down, m_norm_final_g, v_meta_tokens, v_norm_mix_g, v_w_in, v_conv_w, v_ssm_lam_re, v_ssm_lam_im, v_ssm_log_dt, v_ssm_b_re, v_ssm_b_im, v_ssm_c_re, v_ssm_c_im, v_ssm_d, v_ssm_w_glu, v_gain_conv_out, v_gain_ssm_out, v_w_out, v_norm_ffn_g, v_w_up, v_ffn_conv_w, v_ffn_conv_b, v_w_down, v_norm_final_g):
    weights = dict(meta_tokens=meta_tokens, norm_mix_g=norm_mix_g, w_in=w_in, conv_w=conv_w, ssm_lam_re=ssm_lam_re, ssm_lam_im=ssm_lam_im, ssm_log_dt=ssm_log_dt, ssm_b_re=ssm_b_re, ssm_b_im=ssm_b_im, ssm_c_re=ssm_c_re, ssm_c_im=ssm_c_im, ssm_d=ssm_d, ssm_w_glu=ssm_w_glu, gain_conv_out=gain_conv_out, gain_ssm_out=gain_ssm_out, w_out=w_out, norm_ffn_g=norm_ffn_g, w_up=w_up, ffn_conv_w=ffn_conv_w, ffn_conv_b=ffn_conv_b, w_down=w_down, norm_final_g=norm_final_g)
    mom_m = dict(meta_tokens=m_meta_tokens, norm_mix_g=m_norm_mix_g, w_in=m_w_in, conv_w=m_conv_w, ssm_lam_re=m_ssm_lam_re, ssm_lam_im=m_ssm_lam_im, ssm_log_dt=m_ssm_log_dt, ssm_b_re=m_ssm_b_re, ssm_b_im=m_ssm_b_im, ssm_c_re=m_ssm_c_re, ssm_c_im=m_ssm_c_im, ssm_d=m_ssm_d, ssm_w_glu=m_ssm_w_glu, gain_conv_out=m_gain_conv_out, gain_ssm_out=m_gain_ssm_out, w_out=m_w_out, norm_ffn_g=m_norm_ffn_g, w_up=m_w_up, ffn_conv_w=m_ffn_conv_w, ffn_conv_b=m_ffn_conv_b, w_down=m_w_down, norm_final_g=m_norm_final_g)
    mom_v = dict(meta_tokens=v_meta_tokens, norm_mix_g=v_norm_mix_g, w_in=v_w_in, conv_w=v_conv_w, ssm_lam_re=v_ssm_lam_re, ssm_lam_im=v_ssm_lam_im, ssm_log_dt=v_ssm_log_dt, ssm_b_re=v_ssm_b_re, ssm_b_im=v_ssm_b_im, ssm_c_re=v_ssm_c_re, ssm_c_im=v_ssm_c_im, ssm_d=v_ssm_d, ssm_w_glu=v_ssm_w_glu, gain_conv_out=v_gain_conv_out, gain_ssm_out=v_gain_ssm_out, w_out=v_w_out, norm_ffn_g=v_norm_ffn_g, w_up=v_w_up, ffn_conv_w=v_ffn_conv_w, ffn_conv_b=v_ffn_conv_b, w_down=v_w_down, norm_final_g=v_norm_final_g)
    names = list(weights)

    n_meta, d_meta = meta_tokens.shape
    seq, d = x.shape[1], x.shape[2]
    rows_used = n_meta + seq
    t = -(-rows_used // ROW_TILE) * ROW_TILE
    d_in_s = w_in.shape[2]
    dc_s = conv_w.shape[2]
    dc = dc_s * N_DEV
    ds = ssm_w_glu.shape[2]
    n_groups, n_state, grp = ssm_b_re.shape[1:]
    ns = n_groups * n_state
    nch = ds // SSM_CHUNK
    gpc = n_groups // nch
    ff_s = w_up.shape[2]
    dn_s = w_down.shape[1]
    assert 3 * dc + ds == d_in_s * N_DEV and 2 * dn_s == ff_s and t % (N_SEG * SUBLANES) == 0

    small_shard = jnp.concatenate([meta_tokens.reshape(-1), conv_w.reshape(-1), ffn_conv_w.reshape(-1)])
    n_small = small_shard.shape[0]
    small_rows = -(-n_small // LANES)
    small_shard = jnp.pad(small_shard, (0, small_rows * LANES - n_small)).reshape(small_rows, LANES)
    ag, ag_token = _gather2_start("gather_weights_start", [
        small_shard, w_in[0].astype(BF16), ssm_w_glu[0].astype(BF16), w_out[0].astype(BF16),
        jnp.swapaxes(w_up[0], 0, 1).astype(BF16), w_down[0].astype(BF16)])
    fb = ffn_conv_b

    gh = n_groups * grp
    per_h = lambda a: jnp.broadcast_to(a.reshape(n_groups, 1, -1), (n_groups, grp, n_state)).reshape(gh, n_state)
    ghp = lambda a: a.transpose(0, 1, 3, 2).reshape(gh, n_state)
    lr, li, log_dt_e = per_h(ssm_lam_re), per_h(ssm_lam_im), per_h(ssm_log_dt)
    br, bi = ghp(ssm_b_re), ghp(ssm_b_im)
    a_re, a_im, bb_re, bb_im = _disc_fwd(lr, li, log_dt_e, br, bi, deps=(ag_token,))
    cs = gpc * n_state
    chunk_row = lambda a: a.reshape(n_groups, grp, n_state)[:, 0].reshape(nch, 1, cs)
    to_bb = lambda a: _block_diag(a.reshape(nch, gpc, grp, n_state)).astype(BF16)
    to_cc = lambda a: _block_diag(a.reshape(nch, gpc, grp, n_state).transpose(0, 1, 3, 2)).astype(BF16)
    bbm_re, bbm_im = to_bb(bb_re), to_bb(bb_im)
    ccm_re, ccm_im = to_cc(ssm_c_re), to_cc(ssm_c_im)
    a_re_c, a_im_c = chunk_row(a_re), chunk_row(a_im)
    d_skip = ssm_d.reshape(nch, 1, SSM_CHUNK)

    target = _to_segments(jnp.pad(loss_target[0], ((n_meta, t - rows_used), (0, 0))))
    token = _gather2_forward("gather_weights_forward_first", ag, (0, 1), (bbm_im, ccm_im, target))
    (g_small,) = _gather2_wait("gather_weights_wait_small", ag, (0,), token)
    g_small = g_small.reshape(N_DEV, -1)
    o1 = n_meta * d_meta
    o2 = o1 + 3 * dc_s
    meta_full = g_small[:, :o1].reshape(N_DEV, n_meta, d_meta).transpose(1, 0, 2).reshape(n_meta, d)
    conv_w_f = g_small[:, o1:o2].reshape(N_DEV, 3, dc_s).transpose(1, 0, 2).reshape(3, dc)
    fw = g_small[:, o2:o2 + 3 * ff_s].reshape(N_DEV, 3, ff_s).transpose(1, 0, 2).reshape(3, N_DEV * ff_s)
    h0 = _to_segments(jnp.concatenate([meta_full, x[0], jnp.zeros((t - rows_used, d), F32)], axis=0))
    full_t =lambda w: pl.BlockSpec((t, w), lambda *_: (0, 0))

    hn1 = _rms_fwd("norm_mix", h0, norm_mix_g)
    (g_in,) = _gather2_wait("gather_weights_wait_in", ag, (1,), hn1)
    proj = _mm("proj", hn1, g_in, dims=NN, grid=(N_DEV,), a_spec=full_t(d),
               b_spec=pl.BlockSpec((None, d, d_in_s), lambda j: (j, 0, 0)),
               o_spec=pl.BlockSpec((t, d_in_s), lambda j: (0, j)),
               out_shape=jax.ShapeDtypeStruct((t, N_DEV * d_in_s), F32))
    s_re, s_im, y_ssm = _s5_fwd(proj, 3 * dc, bbm_re, bbm_im, a_re_c, a_im_c, ccm_re, ccm_im, d_skip)
    token = _gather2_forward("gather_weights_forward_up", ag, (2, 3, 4), (y_ssm,))
    g_glu, g_out = _gather2_wait("gather_weights_wait_mix", ag, (2, 3), token)
    w_out_f = g_out.reshape(-1, d)
    w_glu_f = g_glu.reshape(ds, ds)
    mixed, z_glu = _mix_fwd(proj, y_ssm, w_glu_f, conv_w_f, gain_conv_out, gain_ssm_out)
    tn_out = 256
    h1 = _mm("out_proj", mixed, w_out_f, dims=NN, grid=(d // tn_out,), a_spec=full_t(dc + ds),
             b_spec=pl.BlockSpec((dc + ds, tn_out), lambda j: (0, j)),
             o_spec=pl.BlockSpec((t, tn_out), lambda j: (0, j)),
             out_shape=jax.ShapeDtypeStruct((t, d), F32),
             res=h0, res_spec=pl.BlockSpec((t, tn_out), lambda j: (0, j)))
    hn2 = _rms_fwd("norm_ffn", h1, norm_ffn_g)
    token = _gather2_forward("gather_weights_forward_down", ag, (5,), (hn2,))
    (g_up,) = _gather2_wait("gather_weights_wait_up", ag, (4,), token)
    f2 = N_DEV * ff_s
    pair = 2 * ff_s
    w_up_t = g_up.reshape(f2, d)
    up_pre = _mm("up_proj", hn2, w_up_t, dims=NT, grid=(f2 // pair,), a_spec=full_t(d),
                 b_spec=pl.BlockSpec((pair, d), lambda j: (j, 0)),
                 o_spec=pl.BlockSpec((t, pair), lambda j: (0, j)),
                 out_shape=jax.ShapeDtypeStruct((t, f2), F32))
    act = _gated_fwd(up_pre, fw, fb)
    (g_down,) = _gather2_wait("gather_weights_wait_down", ag, (5,), act)
    w_down_f = g_down.reshape(f2 // 2, d)
    h2 = _mm("down_proj", act, w_down_f, dims=NN, grid=(d // tn_out,), a_spec=full_t(f2 // 2),
             b_spec=pl.BlockSpec((f2 // 2, tn_out), lambda j: (0, j)),
             o_spec=pl.BlockSpec((t, tn_out), lambda j: (0, j)),
             out_shape=jax.ShapeDtypeStruct((t, d), F32),
             res=h1, res_spec=pl.BlockSpec((t, tn_out), lambda j: (0, j)))

    dh2, dh2_b, loss_part, d_norm_final = _loss_bwd(h2, target, norm_final_g.reshape(1, d), n_meta, rows_used)
    dact = _mm("down_dgrad", dh2_b, w_down_f, dims=NT, grid=(f2 // 2 // pair,), a_spec=full_t(d),
               b_spec=pl.BlockSpec((pair, d), lambda j: (j, 0)),
               o_spec=pl.BlockSpec((t, pair), lambda j: (0, j)),
               out_shape=jax.ShapeDtypeStruct((t, f2 // 2), BF16))
    dw_down = _mm("down_wgrad", act, dh2_b, dims=TN, grid=(f2 // 2 // pair,),
                  a_spec=pl.BlockSpec((t, pair), lambda j: (0, j)), b_spec=full_t(d),
                  o_spec=pl.BlockSpec((pair, d), lambda j: (j, 0)),
                  out_shape=jax.ShapeDtypeStruct((f2 // 2, d), BF16))
    ex_down, token = _send_start("exchange_down_start", [dw_down.reshape(N_DEV, dn_s, d)], gather=False)
    dup_pre, d_fw, d_fb = _gated_bwd(up_pre, dact, fw, fb, deps=(token,))
    wide = _wide_tile(t)
    dhn2 = _mm("up_dgrad", dup_pre, w_up_t, dims=NN, grid=(t // wide,),
               a_spec=pl.BlockSpec((wide, f2), lambda i: (i, 0)),
               b_spec=pl.BlockSpec((f2, d), lambda i: (0, 0)),
               o_spec=pl.BlockSpec((wide, d), lambda i: (i, 0)),
               out_shape=jax.ShapeDtypeStruct((t, d), F32))
    dw_up = _mm("up_wgrad", dup_pre, hn2, dims=TN, grid=(f2 // pair,),
                a_spec=pl.BlockSpec((t, pair), lambda j: (0, j)), b_spec=full_t(d),
                o_spec=pl.BlockSpec((pair, d), lambda j: (j, 0)),
                out_shape=jax.ShapeDtypeStruct((f2, d), BF16))
    ex_up, token = _send_start("exchange_up_start", [dw_up.reshape(N_DEV, ff_s, d)], gather=False)
    dh1, dh1_b, d_norm_ffn = _rms_bwd_res("norm_ffn_bwd", dh2, dhn2, h1, norm_ffn_g, deps=(token,))
    dmixed = _mm("out_dgrad", dh1_b, w_out_f, dims=NT, grid=((dc + ds) // tn_out,), a_spec=full_t(d),
                 b_spec=pl.BlockSpec((tn_out, d), lambda i: (i, 0)),
                 o_spec=pl.BlockSpec((t, tn_out), lambda i: (0, i)),
                 out_shape=jax.ShapeDtypeStruct((t, dc + ds), F32))
    dw_out = _mm("out_wgrad", mixed, dh1_b, dims=TN, grid=((dc + ds) // tn_out,),
                 a_spec=pl.BlockSpec((t, tn_out), lambda i: (0, i)), b_spec=full_t(d),
                 o_spec=pl.BlockSpec((tn_out, d), lambda i: (i, 0)),
                 out_shape=jax.ShapeDtypeStruct((dc + ds, d), BF16))
    db_gate, dconv, dy_ssm, d_wglu, d_conv_w, d_gain_c, d_gain_s = _mix_bwd1(
        proj, y_ssm, z_glu, dmixed, w_glu_f, conv_w_f, gain_conv_out, gain_ssm_out)
    (du, d_bbm_re, d_bbm_im, d_a_re, d_a_im, d_ccm_re, d_ccm_im, d_dskip) = _s5_bwd(
        dy_ssm, proj, 3 * dc, s_re, s_im, bbm_re, bbm_im, a_re_c, a_im_c, ccm_re, ccm_im, d_skip)

    from_bb = lambda a: _diag_blocks(a, gpc).reshape(gh, n_state)
    from_cc = lambda a: _diag_blocks(a, gpc).transpose(0, 1, 3, 2).reshape(gh, n_state)
    first_h = lambda a: jnp.pad(a.reshape(n_groups, 1, n_state), ((0, 0), (0, grp - 1), (0, 0))).reshape(gh, n_state)
    over_h = lambda a: a.reshape(n_groups, grp, n_state).sum(axis=1)
    d_lr, d_li, d_dt_e, d_br, d_bi = _disc_bwd(
        lr, li, log_dt_e, br, bi, first_h(d_a_re), first_h(d_a_im), from_bb(d_bbm_re), from_bb(d_bbm_im))

    rep2d = dict(
        ssm_lam_re=(n_groups, n_state), ssm_lam_im=(n_groups, n_state), ssm_log_dt=(1, n_groups),
        ssm_b_re=(gh, n_state), ssm_b_im=(gh, n_state), ssm_c_re=(gh, n_state),
        ssm_c_im=(gh, n_state), ssm_d=(n_groups, grp), gain_conv_out=(1, dc),
        gain_ssm_out=(1, ds), norm_ffn_g=(1, d), ffn_conv_b=(1, N_DEV * ff_s), norm_final_g=(1, d))
    rep_names = list(rep2d)
    rep_grads = dict(
        ssm_lam_re=over_h(d_lr), ssm_lam_im=over_h(d_li), ssm_log_dt=over_h(d_dt_e).sum(axis=1),
        ssm_b_re=d_br, ssm_b_im=d_bi, ssm_c_re=from_cc(d_ccm_re), ssm_c_im=from_cc(d_ccm_im),
        ssm_d=d_dskip, gain_conv_out=d_gain_c, gain_ssm_out=d_gain_s, norm_ffn_g=d_norm_ffn,
        ffn_conv_b=d_fb, norm_final_g=d_norm_final)
    rep_shapes = [rep2d[n] for n in rep_names] + [(1, 1)]
    rep_rows = _packed_rows(rep_shapes)
    rep_pack = _pack([rep_grads[n] for n in rep_names] + [loss_part], rep_rows)
    ex_mix, token = _send_start("exchange_mix_start", [
        dw_out.reshape(N_DEV, -1, d), d_wglu.astype(BF16).reshape(N_DEV, -1, ds),
        d_conv_w.reshape(3, N_DEV, dc_s).transpose(1, 0, 2),
        d_fw.reshape(3, N_DEV, ff_s).transpose(1, 0, 2), rep_pack],
        gather=[False, False, False, False, True])
    dcdv = _mix_bwd2(proj, dconv, conv_w_f, deps=(token,))
    dproj = jnp.concatenate([db_gate, dcdv, du], axis=1)
    d_in = N_DEV * d_in_s
    w_in_rows = g_in.transpose(1, 0, 2).reshape(d, d_in)
    dhn1 = _mm("proj_dgrad", dproj, w_in_rows, dims=NT, grid=(t // wide,),
               a_spec=pl.BlockSpec((wide, d_in), lambda i: (i, 0)),
               b_spec=pl.BlockSpec((d, d_in), lambda i: (0, 0)),
               o_spec=pl.BlockSpec((wide, d), lambda i: (i, 0)),
               out_shape=jax.ShapeDtypeStruct((t, d), F32))
    dh0, _, d_norm_mix = _rms_bwd_res("norm_mix_bwd", dh1, dhn1, h0, norm_mix_g)
    dw_in = _mm("proj_wgrad", hn1, dproj, dims=TN, grid=(N_DEV,), a_spec=full_t(d),
                b_spec=pl.BlockSpec((t, d_in_s), lambda j: (0, j)),
                o_spec=pl.BlockSpec((None, d, d_in_s), lambda j: (j, 0, 0)),
                out_shape=jax.ShapeDtypeStruct((N_DEV, d, d_in_s), BF16))
    dh0 = _from_segments(dh0)
    grad_x = dh0[n_meta:rows_used][None]
    d_meta_b = dh0[:n_meta].reshape(n_meta, N_DEV, d_meta).transpose(1, 0, 2)
    ex_in, ex_in_token = _send_start("exchange_in_start", [dw_in, d_meta_b, d_norm_mix],
                                     gather=[False, False, True])

    shard_out = {}
    (rep_parts,) = _send_wait("gather_small_grads_wait", ex_mix, (4,), ex_in_token)
    rep_sum = _sum_parts("sum_small_grads", rep_parts)
    *rep_g, loss = _unpack(rep_sum, rep_shapes)
    loss = loss.reshape(())
    swapped = ("ssm_b_re", "ssm_b_im")
    to2d = lambda n, a: ghp(a) if n in swapped else a.reshape(rep2d[n])
    from2d = lambda n, a: (a.reshape(1, n_groups, grp, n_state).transpose(0, 1, 3, 2) if n in swapped
                           else a.reshape(weights[n].shape))
    as2d = lambda tree: [to2d(n, tree[n]) for n in rep_names]
    rep_res = _adamw_many("adamw_replicated", rep_g, as2d(weights), as2d(mom_m), as2d(mom_v))
    for i, n in enumerate(rep_names):
        shard_out[n] = [from2d(n, r) for r in (rep_g[i], *(res[i] for res in rep_res))]

    def update(n, parts, transposed=False):
        sh = weights[n].shape
        two_d = lambda a: (jnp.swapaxes(a[0], 0, 1) if transposed else a.reshape(parts.shape[1:]))
        res = _adamw_reduce("adamw_" + n, parts, two_d(weights[n]), two_d(mom_m[n]), two_d(mom_v[n]))
        shard_out[n] = [(jnp.swapaxes(r, 0, 1) if transposed else r).reshape(sh) for r in res]
        return res[0]

    (p_down,) = _send_wait("exchange_down_wait", ex_down, (0,), rep_sum)
    done = update("w_down", p_down)
    (p_up,) = _send_wait("exchange_up_wait", ex_up, (0,), done)
    done = update("w_up", p_up, transposed=True)
    p_out, p_glu, p_cw, p_fw = _send_wait("exchange_mix_wait", ex_mix, (0, 1, 2, 3), done)
    update("w_out", p_out)
    update("ssm_w_glu", p_glu)
    update("conv_w", p_cw)
    done = update("ffn_conv_w", p_fw)
    p_in, p_meta, p_nm = _send_wait("exchange_in_wait", ex_in, (0, 1, 2), done)
    update("w_in", p_in)
    update("meta_tokens", p_meta)
    update("norm_mix_g", p_nm)

    grads = [shard_out[n][0] for n in names]
    deltas = [shard_out[n][1] for n in names]
    new_m = [shard_out[n][2] for n in names]
    new_v = [shard_out[n][3] for n in names]
    return (loss, grad_x, *grads, *deltas, *new_m, *new_v)
```

```python
import functools
import math

import jax
import jax.numpy as jnp
from jax import lax
from jax.experimental import pallas as pl
from jax.experimental.pallas import tpu as pltpu

F32 = jnp.float32
BF16 = jnp.bfloat16
MESH = pl.DeviceIdType.MESH

N_DEV = 8
RMS_EPS = 1e-6
ADAM_LR = 0.001
ADAM_B1 = 0.9
ADAM_B2 = 0.999
ADAM_EPS = 1e-08
ADAM_WD = 0.01
ADAM_STEP = 10
ADAM_BC1 = 1.0 - ADAM_B1 ** ADAM_STEP
ADAM_BC2 = 1.0 - ADAM_B2 ** ADAM_STEP

SUBLANES = 8
LANES = 128
ROW_TILE = 128
ROW_CHUNK = 32
WIDE_TILES = (544, 256, 128)
FFN_COLS = 256
N_SEG = 8
HALO_ROWS = 16
SSM_CHUNK = 128
VMEM_LIMIT = 48 * 1024 * 1024

NN = ((1,), (0,))
NT = ((1,), (1,))
TN = ((0,), (0,))


def _params(**kw):
    return pltpu.CompilerParams(vmem_limit_bytes=VMEM_LIMIT, **kw)


def _dot(a, b, dims):
    return lax.dot_general(a, b, (dims, ((), ())), preferred_element_type=F32)


def _mean_sq_rsqrt(x):
    return lax.rsqrt(jnp.mean(x * x, axis=-1, keepdims=True) + RMS_EPS)


def _rms_bwd(x, r, g, dy):
    xhat = x * r
    dxh = dy * g
    dx = r * (dxh - xhat * jnp.mean(dxh * xhat, axis=-1, keepdims=True))
    return dx, dy * xhat


def _gelu(y):
    c = math.sqrt(2.0 / math.pi)
    t = jnp.tanh(c * (y + 0.044715 * y * y * y))
    return 0.5 * y * (1.0 + t), t


def _gelu_grad(y, t):
    c = math.sqrt(2.0 / math.pi)
    return 0.5 * (1.0 + t) + 0.5 * y * (1.0 - t * t) * c * (1.0 + 3.0 * 0.044715 * y * y)


def _wrap_prev_halo(halo, first):
    seg = lax.broadcasted_iota(jnp.int32, halo.shape, 0) % N_SEG
    wrapped = jnp.where(seg == 0, 0.0, pltpu.roll(halo, 1, 0))
    return jnp.where(first, wrapped, halo)


def _wrap_next_halo(halo, last):
    seg = lax.broadcasted_iota(jnp.int32, halo.shape, 0) % N_SEG
    wrapped = jnp.where(seg == N_SEG - 1, 0.0, pltpu.roll(halo, halo.shape[0] - 1, 0))
    return jnp.where(last, wrapped, halo)


def _tile_rows(get, prev, nxt, s, n, tile_rows=ROW_TILE):
    parts = []
    if s < 0:
        parts.append(prev[HALO_ROWS + s:HALO_ROWS + min(s + n, 0)])
    lo, hi = max(s, 0), min(s + n, tile_rows)
    if hi > lo:
        parts.append(get(lo, hi - lo))
    if s + n > tile_rows:
        parts.append(nxt[max(s - tile_rows, 0):s + n - tile_rows])
    return parts[0] if len(parts) == 1 else jnp.concatenate(parts, axis=0)


def _dev_index(p):
    return 4 * p[0] + 2 * p[1] + p[2]


def _allgather(name, shards, deps=()):
    n = len(shards)

    def body(*refs):
        ins, outs = refs[:n], refs[n:2 * n]
        send_sems, recv_sems, local_sems = refs[2 * n:]
        x, y, c = lax.axis_index("x"), lax.axis_index("y"), lax.axis_index("c")
        me, sibling = (x, y, c), (x, y, 1 - c)
        chips = [(1 - x, y), (x, 1 - y), (1 - x, 1 - y)]

        def copy(a, k, block, to, src=None):
            dst = outs[a].at[_dev_index(block)]
            return pltpu.make_async_remote_copy(
                src_ref=dst if src is None else src, dst_ref=dst,
                send_sem=send_sems.at[a, k], recv_sem=recv_sems.at[a, k],
                device_id=to, device_id_type=MESH)

        mine = [pltpu.make_async_copy(ins[a], outs[a].at[_dev_index(me)], local_sems.at[a])
                for a in range(n)]
        for cp in mine:
            cp.start()
        first = []
        for a in range(n):
            first.append(copy(a, 0, me, sibling, src=ins[a]))
            for j, chip in enumerate(chips):
                first.append(copy(a, 1 + j, me, (*chip, c), src=ins[a]))
        for cp in first:
            cp.start()
        passed = []
        for j, chip in enumerate(chips):
            for a in range(n):
                copy(a, 1 + j, (*chip, c), me).wait_recv()
                fwd = copy(a, 4 + j, (*chip, c), sibling)
                fwd.start()
                passed.append(fwd)
        for a in range(n):
            copy(a, 0, sibling, me).wait_recv()
            for j, chip in enumerate(chips):
                copy(a, 4 + j, (*chip, 1 - c), me).wait_recv()
        for cp in first + passed:
            cp.wait_send()
        for cp in mine:
            cp.wait()

    any_spec = pl.BlockSpec(memory_space=pl.ANY)
    return pl.pallas_call(
        _ignoring_deps(body, n, deps), name=name,
        out_shape=[jax.ShapeDtypeStruct((N_DEV,) + s.shape, s.dtype) for s in shards],
        in_specs=[any_spec] * (n + len(deps)), out_specs=[any_spec] * n,
        scratch_shapes=[pltpu.SemaphoreType.DMA((n, 7)), pltpu.SemaphoreType.DMA((n, 7)),
                        pltpu.SemaphoreType.DMA((n,))],
    )(*shards, *deps)


HBM_SPEC = pl.BlockSpec(memory_space=pltpu.HBM)
SEM_SPEC = pl.BlockSpec(memory_space=pltpu.SEMAPHORE)
ANY_SPEC = pl.BlockSpec(memory_space=pl.ANY)
DATAFLOW = pltpu.SideEffectType.DATAFLOW_SIDE_EFFECTING


def _ignoring_deps(body, n_in, deps):
    n_dep = len(deps)

    def wrapped(*refs):
        return body(*refs[:n_in], *refs[n_in + n_dep:])

    return wrapped


def _my_position():
    x, y, c = lax.axis_index("x"), lax.axis_index("y"), lax.axis_index("c")
    return (x, y, c)


def _peer(me, k):
    return tuple((1 - v) if (k >> s) & 1 else v for v, s in zip(me, (2, 1, 0)))


def _split_copies(src_refs, land_refs, send_sems, recv_sems, gather):
    me = _my_position()
    copies = []
    for a, (src, land) in enumerate(zip(src_refs, land_refs)):
        for k in range(1, N_DEV):
            peer = _peer(me, k)
            copies.append(pltpu.make_async_remote_copy(
                src_ref=src if gather[a] else src.at[_dev_index(peer)], dst_ref=land.at[_dev_index(me)],
                send_sem=send_sems[a].at[k - 1], recv_sem=recv_sems[a].at[k - 1],
                device_id=peer, device_id_type=MESH))
    return copies


def _own_slot(block, like_shape):
    me = _dev_index(_my_position())
    return lax.dynamic_update_index_in_dim(lax.empty(like_shape, block.dtype), block, me, 0)


def _send_start(name, srcs, gather):
    n = len(srcs)
    me = _dev_index(_my_position())
    gather = [gather] * n if isinstance(gather, bool) else list(gather)
    lands = [_own_slot(s, (N_DEV,) + s.shape) if g else
             _own_slot(lax.dynamic_index_in_dim(s, me, 0, keepdims=False), s.shape)
             for s, g in zip(srcs, gather)]

    def body(*refs):
        src_refs, land_refs = refs[:n], refs[n:2 * n]
        send_sems, recv_sems = refs[2 * n:3 * n], refs[3 * n:4 * n]
        token = refs[-1]
        for cp in _split_copies(src_refs, land_refs, send_sems, recv_sems, gather):
            cp.start()
        token[...] = jnp.zeros_like(token)

    hbm = lambda a: pltpu.HBM(a.shape, a.dtype)
    sems = [pltpu.SemaphoreType.DMA((N_DEV - 1,))] * n
    outs = pl.pallas_call(
        body, name=name,
        out_shape=(*sems, *sems, *[hbm(s) for s in srcs], *[hbm(l) for l in lands],
                   jax.ShapeDtypeStruct((SUBLANES, LANES), F32)),
        in_specs=[HBM_SPEC] * (2 * n),
        out_specs=(*[SEM_SPEC] * (2 * n), *[HBM_SPEC] * (2 * n), pl.BlockSpec(memory_space=pltpu.VMEM)),
        input_output_aliases={i: 2 * n + i for i in range(2 * n)},
        compiler_params=pltpu.CompilerParams(has_side_effects=DATAFLOW),
    )(*[pltpu.with_memory_space_constraint(a, pltpu.HBM) for a in (*srcs, *lands)])
    state = dict(send=outs[:n], recv=outs[n:2 * n], srcs=outs[2 * n:3 * n], lands=outs[3 * n:4 * n],
                 gather=gather)
    return state, outs[-1]


def _send_wait(name, state, which, after):
    n = len(which)
    pick = lambda key: [state[key][i] for i in which]
    gather = pick("gather")

    def body(*refs):
        src_refs, land_refs = refs[:n], refs[n:2 * n]
        send_sems, recv_sems = refs[2 * n:3 * n], refs[3 * n:4 * n]
        for cp in _split_copies(src_refs, land_refs, send_sems, recv_sems, gather):
            cp.wait_send()
            cp.wait_recv()

    srcs, lands = pick("srcs"), pick("lands")
    hbm = lambda a: pltpu.HBM(a.shape, a.dtype)
    outs = pl.pallas_call(
        body, name=name,
        out_shape=(*[hbm(s) for s in srcs], *[hbm(l) for l in lands]),
        in_specs=[*[HBM_SPEC] * (2 * n), *[SEM_SPEC] * (2 * n), ANY_SPEC],
        out_specs=tuple([HBM_SPEC] * (2 * n)),
        input_output_aliases={i: i for i in range(2 * n)},
        compiler_params=pltpu.CompilerParams(has_side_effects=DATAFLOW),
    )(*srcs, *lands, *pick("send"), *pick("recv"), after)
    return outs[n:]


def _two_level_copies(src_refs, land_refs, sems1, sems2):
    x, y, c = _my_position()
    me, sibling = (x, y, c), (x, y, 1 - c)
    chips = [(1 - x, y), (x, 1 - y), (1 - x, 1 - y)]
    stage1, stage2 = [], []
    for a, land in enumerate(land_refs):
        def copy(src, block, to, sems, k):
            return pltpu.make_async_remote_copy(
                src_ref=src, dst_ref=land.at[_dev_index(block)], send_sem=sems[0][a].at[k],
                recv_sem=sems[1][a].at[k], device_id=to, device_id_type=MESH)
        src = src_refs[a] if src_refs is not None else land.at[_dev_index(me)]
        stage1.append([copy(src, me, sibling, sems1, 0)] +
                      [copy(src, me, (*chip, c), sems1, 1 + j) for j, chip in enumerate(chips)])
        if sems2 is not None:
            stage2.append([copy(land.at[_dev_index((*chip, c))], (*chip, c), sibling, sems2, j)
                           for j, chip in enumerate(chips)])
    return stage1, stage2


def _gather2_start(name, shards):
    n = len(shards)
    lands = [_own_slot(s, (N_DEV,) + s.shape) for s in shards]

    def body(*refs):
        src_refs, land_refs = refs[:n], refs[n:2 * n]
        sems1 = (refs[2 * n:3 * n], refs[3 * n:4 * n])
        stage1, _ = _two_level_copies(src_refs, land_refs, sems1, None)
        for copies in stage1:
            for cp in copies:
                cp.start()
        refs[-1][...] = jnp.zeros_like(refs[-1])

    hbm = lambda a: pltpu.HBM(a.shape, a.dtype)
    sems = [pltpu.SemaphoreType.DMA((4,))] * n
    outs = pl.pallas_call(
        body, name=name,
        out_shape=(*sems, *sems, *[hbm(s) for s in shards], *[hbm(l) for l in lands],
                   jax.ShapeDtypeStruct((SUBLANES, LANES), F32)),
        in_specs=[HBM_SPEC] * (2 * n),
        out_specs=(*[SEM_SPEC] * (2 * n), *[HBM_SPEC] * (2 * n), pl.BlockSpec(memory_space=pltpu.VMEM)),
        input_output_aliases={i: 2 * n + i for i in range(2 * n)},
        compiler_params=pltpu.CompilerParams(has_side_effects=DATAFLOW),
    )(*[pltpu.with_memory_space_constraint(a, pltpu.HBM) for a in (*shards, *lands)])
    state = dict(send1=list(outs[:n]), recv1=list(outs[n:2 * n]), srcs=list(outs[2 * n:3 * n]),
                 lands=list(outs[3 * n:4 * n]), send2={}, recv2={})
    return state, outs[-1]


def _gather2_forward(name, state, which, after):
    n = len(which)
    pick = lambda key: [state[key][i] for i in which]

    def body(*refs):
        land_refs, recv1 = refs[:n], refs[n:2 * n]
        outs = refs[2 * n + len(after):]
        sems2 = (outs[:n], outs[n:2 * n])
        stage1, stage2 = _two_level_copies(None, land_refs, (recv1, recv1), sems2)
        for a in range(n):
            for j in range(3):
                stage1[a][1 + j].wait_recv()
                stage2[a][j].start()
        outs[-1][...] = jnp.zeros_like(outs[-1])

    lands = pick("lands")
    sems = [pltpu.SemaphoreType.DMA((3,))] * n
    outs = pl.pallas_call(
        body, name=name,
        out_shape=(*sems, *sems, *[pltpu.HBM(l.shape, l.dtype) for l in lands],
                   jax.ShapeDtypeStruct((SUBLANES, LANES), F32)),
        in_specs=[*[HBM_SPEC] * n, *[SEM_SPEC] * n, *[ANY_SPEC] * len(after)],
        out_specs=(*[SEM_SPEC] * (2 * n), *[HBM_SPEC] * n, pl.BlockSpec(memory_space=pltpu.VMEM)),
        input_output_aliases={i: 2 * n + i for i in range(n)},
        compiler_params=pltpu.CompilerParams(has_side_effects=DATAFLOW),
    )(*lands, *pick("recv1"), *after)
    for idx, i in enumerate(which):
        state["send2"][i], state["recv2"][i] = outs[idx], outs[n + idx]
        state["lands"][i] = outs[2 * n + idx]
    return outs[-1]


def _gather2_wait(name, state, which, after):
    n = len(which)
    pick = lambda key: [state[key][i] for i in which]

    def body(*refs):
        src_refs, land_refs = refs[:n], refs[n:2 * n]
        sems1 = (refs[2 * n:3 * n], refs[3 * n:4 * n])
        sems2 = (refs[4 * n:5 * n], refs[5 * n:6 * n])
        stage1, stage2 = _two_level_copies(src_refs, land_refs, sems1, sems2)
        for a in range(n):
            for cp in stage1[a]:
                cp.wait_send()
            stage1[a][0].wait_recv()
            for cp in stage2[a]:
                cp.wait_send()
                cp.wait_recv()

    srcs, lands = pick("srcs"), pick("lands")
    hbm = lambda a: pltpu.HBM(a.shape, a.dtype)
    outs = pl.pallas_call(
        body, name=name,
        out_shape=(*[hbm(s) for s in srcs], *[hbm(l) for l in lands]),
        in_specs=[*[HBM_SPEC] * (2 * n), *[SEM_SPEC] * (4 * n), ANY_SPEC],
        out_specs=tuple([HBM_SPEC] * (2 * n)),
        input_output_aliases={i: i for i in range(2 * n)},
        compiler_params=pltpu.CompilerParams(has_side_effects=DATAFLOW),
    )(*srcs, *lands, *pick("send1"), *pick("recv1"), *pick("send2"), *pick("recv2"), after)
    return outs[n:]


def _mm(name, a, b, *, dims, grid, a_spec, b_spec, o_spec, out_shape, acc_shape=None,
        res=None, res_spec=None):
    n_red = grid[-1] if acc_shape is not None else 1
    red_axis = len(grid) - 1

    def body(*refs):
        a_ref, b_ref = refs[0], refs[1]
        r_ref = refs[2] if res is not None else None
        o_ref = refs[3] if res is not None else refs[2]
        part = _dot(a_ref[...], b_ref[...], dims)
        if acc_shape is None:
            if r_ref is not None:
                part = part + r_ref[...]
            o_ref[...] = part.astype(o_ref.dtype)
            return
        acc_ref = refs[-1]
        k = pl.program_id(red_axis)

        @pl.when(k == 0)
        def _():
            acc_ref[...] = part

        @pl.when(k > 0)
        def _():
            acc_ref[...] += part

        @pl.when(k == n_red - 1)
        def _():
            total = acc_ref[...]
            if r_ref is not None:
                total = total + r_ref[...]
            o_ref[...] = total.astype(o_ref.dtype)

    ins, in_specs = [a, b], [a_spec, b_spec]
    if res is not None:
        ins.append(res)
        in_specs.append(res_spec)
    return pl.pallas_call(
        body, name=name, grid=grid, in_specs=in_specs, out_specs=o_spec, out_shape=out_shape,
        scratch_shapes=[pltpu.VMEM(acc_shape, F32)] if acc_shape is not None else [],
        compiler_params=_params(),
    )(*ins)


def _wide_tile(t):
    return next(c for c in WIDE_TILES if t % c == 0)


def _rms_fwd(name, h, g):
    t, d = h.shape
    tile = _wide_tile(t)

    def body(h_ref, g_ref, o_ref):
        x = h_ref[...]
        o_ref[...] = (x * _mean_sq_rsqrt(x) * g_ref[...]).astype(BF16)

    return pl.pallas_call(
        body, name=name, grid=(t // tile,),
        in_specs=[pl.BlockSpec((tile, d), lambda i: (i, 0)), pl.BlockSpec((1, d), lambda i: (0, 0))],
        out_specs=pl.BlockSpec((tile, d), lambda i: (i, 0)),
        out_shape=jax.ShapeDtypeStruct((t, d), BF16), compiler_params=_params(),
    )(h, g)


def _segment_view(a):
    t, d = a.shape
    return a.reshape(t // N_SEG, N_SEG * d)


def _segment_spec(steps, d):
    return pl.BlockSpec((steps, d), lambda j: (0, j))


def _rms_fwd_in(name, h_time, g):
    t, d = h_time.shape
    steps = t // N_SEG

    def body(h_ref, g_ref, h_out_ref, o_ref):
        x = h_ref[...]
        h_out_ref[...] = x
        o_ref[...] = (x * _mean_sq_rsqrt(x) * g_ref[...]).astype(BF16)

    seg = _segment_spec(steps, d)
    h, hn = pl.pallas_call(
        body, name=name, grid=(N_SEG,),
        in_specs=[pl.BlockSpec((steps, d), lambda j: (j, 0)), pl.BlockSpec((1, d), lambda j: (0, 0))],
        out_specs=[seg, seg],
        out_shape=[jax.ShapeDtypeStruct((steps, N_SEG * d), F32), jax.ShapeDtypeStruct((steps, N_SEG * d), BF16)],
        compiler_params=_params(),
    )(h_time, g)
    return h.reshape(t, d), hn.reshape(t, d)


def _rms_bwd_out(name, dres, dhn, h, g, deps=()):
    t, d = h.shape
    steps = t // N_SEG

    def body(dres_ref, dhn_ref, h_ref, g_ref, dh_ref, dg_ref):
        x = h_ref[...]
        dx, dgt = _rms_bwd(x, _mean_sq_rsqrt(x), g_ref[...], dhn_ref[...])
        dh_ref[...] = dres_ref[...] + dx

        @pl.when(pl.program_id(0) == 0)
        def _():
            dg_ref[...] = jnp.zeros_like(dg_ref)

        dg_ref[...] += jnp.sum(dgt, axis=0, keepdims=True)

    seg = _segment_spec(steps, d)
    vec = pl.BlockSpec((1, d), lambda j: (0, 0))
    return pl.pallas_call(
        _ignoring_deps(body, 4, deps), name=name, grid=(N_SEG,),
        in_specs=[seg, seg, seg, vec] + [ANY_SPEC] * len(deps),
        out_specs=[pl.BlockSpec((steps, d), lambda j: (j, 0)), vec],
        out_shape=[jax.ShapeDtypeStruct((t, d), F32), jax.ShapeDtypeStruct((1, d), F32)],
        compiler_params=_params(),
    )(_segment_view(dres), _segment_view(dhn), _segment_view(h), g, *deps)


def _rms_bwd_res(name, dres, dhn, h, g, deps=()):
    t, d = h.shape

    def body(dres_ref, dhn_ref, h_ref, g_ref, dh_ref, dhb_ref, dg_ref):
        x = h_ref[...]
        dx, dgt = _rms_bwd(x, _mean_sq_rsqrt(x), g_ref[...], dhn_ref[...])
        dh = dres_ref[...] + dx
        dh_ref[...] = dh
        dhb_ref[...] = dh.astype(BF16)

        @pl.when(pl.program_id(0) == 0)
        def _():
            dg_ref[...] = jnp.zeros_like(dg_ref)

        dg_ref[...] += jnp.sum(dgt, axis=0, keepdims=True)

    tile = _wide_tile(t)
    row = pl.BlockSpec((tile, d), lambda i: (i, 0))
    vec = pl.BlockSpec((1, d), lambda i: (0, 0))
    return pl.pallas_call(
        _ignoring_deps(body, 4, deps), name=name, grid=(t // tile,),
        in_specs=[row, row, row, vec] + [ANY_SPEC] * len(deps), out_specs=[row, row, vec],
        out_shape=[jax.ShapeDtypeStruct((t, d), F32), jax.ShapeDtypeStruct((t, d), BF16),
                   jax.ShapeDtypeStruct((1, d), F32)],
        compiler_params=_params(),
    )(dres, dhn, h, g, *deps)


def _loss_bwd(h2, target, g, row_lo, row_hi):
    t, d = h2.shape
    steps = t // N_SEG

    def body(h_ref, tg_ref, g_ref, dh_ref, dhb_ref, loss_ref, dg_ref):
        i = pl.program_id(0)
        x = h_ref[...]
        r = _mean_sq_rsqrt(x)
        gv = g_ref[...]
        y = x * r * gv
        time = i * steps + lax.broadcasted_iota(jnp.int32, (steps, 1), 0)
        valid = jnp.logical_and(time >= row_lo, time < row_hi)
        err = jnp.where(valid, y - tg_ref[...], 0.0)
        dy = err * (1.0 / d)
        dx, dgt = _rms_bwd(x, r, gv, dy)
        dh_ref[...] = dx
        dhb_ref[...] = dx.astype(BF16)

        @pl.when(i == 0)
        def _():
            loss_ref[...] = jnp.zeros_like(loss_ref)
            dg_ref[...] = jnp.zeros_like(dg_ref)

        row_loss = jnp.mean(err * err, axis=-1, keepdims=True)
        loss_ref[...] += 0.5 * jnp.sum(row_loss, axis=0, keepdims=True)
        dg_ref[...] += jnp.sum(dgt, axis=0, keepdims=True)

    seg, nat = _segment_spec(steps, d), pl.BlockSpec((steps, d), lambda j: (j, 0))
    vec = pl.BlockSpec((1, d), lambda j: (0, 0))
    dh, dh_b, loss, dg = pl.pallas_call(
        body, name="loss_bwd", grid=(N_SEG,), in_specs=[seg, nat, vec],
        out_specs=[seg, seg, pl.BlockSpec((1, 1), lambda j: (0, 0)), vec],
        out_shape=[jax.ShapeDtypeStruct((steps, N_SEG * d), F32), jax.ShapeDtypeStruct((steps, N_SEG * d), BF16),
                   jax.ShapeDtypeStruct((1, 1), F32), jax.ShapeDtypeStruct((1, d), F32)],
        compiler_params=_params(),
    )(_segment_view(h2), target, g)
    return dh.reshape(t, d), dh_b.reshape(t, d), loss, dg


def _prev_halo(i, t):
    return jnp.where(i == 0, t // HALO_ROWS - 1, i * (ROW_TILE // HALO_ROWS) - 1)


def _next_halo(i, t):
    return jnp.where(i == t // ROW_TILE - 1, 0, (i + 1) * (ROW_TILE // HALO_ROWS))


def _causal_taps(cur, halo, first):
    rows = cur.shape[0]
    ext = jnp.concatenate([_wrap_prev_halo(halo, first), cur], axis=0)
    return ext[:rows], ext[N_SEG:N_SEG + rows]


def _anticausal_taps(cur, halo, last):
    rows = cur.shape[0]
    ext = jnp.concatenate([cur, _wrap_next_halo(halo, last)], axis=0)
    return ext[N_SEG:N_SEG + rows], ext[2 * N_SEG:2 * N_SEG + rows]


def _mix_fwd(proj, y, w_glu, conv_w, gain_c, gain_s):
    t = proj.shape[0]
    dc = conv_w.shape[1]
    ds = y.shape[1]

    def body(p_ref, halo_ref, y_ref, wg_ref, cw_ref, gc_ref, gs_ref, mixed_ref, z_ref):
        i = pl.program_id(0)
        p = p_ref[...]
        b, c, v = p[:, :dc], p[:, dc:2 * dc], p[:, 2 * dc:3 * dc]
        cv = c * v
        hp = halo_ref[...]
        x2, x1 = _causal_taps(cv, hp[:, dc:2 * dc] * hp[:, 2 * dc:3 * dc], i == 0)
        cw = cw_ref[...]
        conv = cw[0:1] * x2 + cw[1:2] * x1 + cw[2:3] * cv
        co = b * conv
        mixed_ref[:, :dc] = (co * _mean_sq_rsqrt(co) * gc_ref[...]).astype(BF16)
        g, _ = _gelu(y_ref[...])
        z = _dot(g.astype(BF16), wg_ref[...], NN)
        z_ref[...] = z
        so = g * jax.nn.sigmoid(z)
        mixed_ref[:, dc:] = (so * _mean_sq_rsqrt(so) * gs_ref[...]).astype(BF16)

    const = lambda i: (0, 0)
    return pl.pallas_call(
        body, name="mix_fwd", grid=(t // ROW_TILE,),
        in_specs=[pl.BlockSpec((ROW_TILE, 3 * dc), lambda i: (i, 0)),
                  pl.BlockSpec((HALO_ROWS, 3 * dc), lambda i: (_prev_halo(i, t), 0)),
                  pl.BlockSpec((ROW_TILE, ds), lambda i: (i, 0)),
                  pl.BlockSpec((ds, ds), const), pl.BlockSpec(conv_w.shape, const),
                  pl.BlockSpec((1, dc), const), pl.BlockSpec((1, ds), const)],
        out_specs=[pl.BlockSpec((ROW_TILE, dc + ds), lambda i: (i, 0)),
                   pl.BlockSpec((ROW_TILE, ds), lambda i: (i, 0))],
        out_shape=[jax.ShapeDtypeStruct((t, dc + ds), BF16), jax.ShapeDtypeStruct((t, ds), F32)],
        compiler_params=_params(),
    )(proj, proj, y, w_glu, conv_w, gain_c, gain_s)


def _mix_bwd1(proj, y, z, dmixed, w_glu, conv_w, gain_c, gain_s):
    t = proj.shape[0]
    dc = conv_w.shape[1]
    ds = y.shape[1]

    def body(p_ref, halo_ref, y_ref, z_ref, dm_ref, wg_ref, cw_ref, gc_ref, gs_ref,
             db_ref, dconv_ref, dy_ref, dwg_ref, dcw_ref, dgc_ref, dgs_ref):
        i = pl.program_id(0)

        @pl.when(i == 0)
        def _():
            dwg_ref[...] = jnp.zeros_like(dwg_ref)
            dcw_ref[...] = jnp.zeros_like(dcw_ref)
            dgc_ref[...] = jnp.zeros_like(dgc_ref)
            dgs_ref[...] = jnp.zeros_like(dgs_ref)

        p = p_ref[...]
        b, c, v = p[:, :dc], p[:, dc:2 * dc], p[:, 2 * dc:3 * dc]
        cv = c * v
        hp = halo_ref[...]
        x2, x1 = _causal_taps(cv, hp[:, dc:2 * dc] * hp[:, 2 * dc:3 * dc], i == 0)
        cw = cw_ref[...]
        conv = cw[0:1] * x2 + cw[1:2] * x1 + cw[2:3] * cv
        co = b * conv
        dm = dm_ref[...]
        dco, dgt = _rms_bwd(co, _mean_sq_rsqrt(co), gc_ref[...], dm[:, :dc])
        dgc_ref[...] += jnp.sum(dgt, axis=0, keepdims=True)
        db_ref[...] = (dco * conv).astype(BF16)
        dconv = dco * b
        dconv_ref[...] = dconv
        dcw_ref[0:1, :] += jnp.sum(dconv * x2, axis=0, keepdims=True)
        dcw_ref[1:2, :] += jnp.sum(dconv * x1, axis=0, keepdims=True)
        dcw_ref[2:3, :] += jnp.sum(dconv * cv, axis=0, keepdims=True)

        yv = y_ref[...]
        g, th = _gelu(yv)
        sg = jax.nn.sigmoid(z_ref[...])
        so = g * sg
        dso, dgt = _rms_bwd(so, _mean_sq_rsqrt(so), gs_ref[...], dm[:, dc:])
        dgs_ref[...] += jnp.sum(dgt, axis=0, keepdims=True)
        dz = (dso * g * sg * (1.0 - sg)).astype(BF16)
        dg = dso * sg + _dot(dz, wg_ref[...], NT)
        dwg_ref[...] += _dot(g.astype(BF16), dz, TN)
        dy_ref[...] = dg * _gelu_grad(yv, th)

    const = lambda i: (0, 0)
    row = lambda w: pl.BlockSpec((ROW_TILE, w), lambda i: (i, 0))
    return pl.pallas_call(
        body, name="mix_bwd1", grid=(t // ROW_TILE,),
        in_specs=[row(3 * dc), pl.BlockSpec((HALO_ROWS, 3 * dc), lambda i: (_prev_halo(i, t), 0)),
                  row(ds), row(ds), row(dc + ds), pl.BlockSpec((ds, ds), const),
                  pl.BlockSpec(conv_w.shape, const), pl.BlockSpec((1, dc), const),
                  pl.BlockSpec((1, ds), const)],
        out_specs=[row(dc), row(dc), row(ds), pl.BlockSpec((ds, ds), const),
                   pl.BlockSpec(conv_w.shape, const), pl.BlockSpec((1, dc), const),
                   pl.BlockSpec((1, ds), const)],
        out_shape=[jax.ShapeDtypeStruct((t, dc), BF16), jax.ShapeDtypeStruct((t, dc), F32),
                   jax.ShapeDtypeStruct((t, ds), F32), jax.ShapeDtypeStruct((ds, ds), F32),
                   jax.ShapeDtypeStruct(conv_w.shape, F32), jax.ShapeDtypeStruct((1, dc), F32),
                   jax.ShapeDtypeStruct((1, ds), F32)],
        compiler_params=_params(),
    )(proj, proj, y, z, dmixed, w_glu, conv_w, gain_c, gain_s)


def _mix_bwd2(proj, dconv, conv_w, deps=()):
    t = proj.shape[0]
    dc = conv_w.shape[1]
    n_tiles = t // ROW_TILE

    def body(c_ref, v_ref, d_ref, halo_ref, cw_ref, o_ref):
        i = pl.program_id(0)
        d = d_ref[...]
        u1, u2 = _anticausal_taps(d, halo_ref[...], i == n_tiles - 1)
        cw = cw_ref[...]
        dcv = cw[2:3] * d + cw[1:2] * u1 + cw[0:1] * u2
        o_ref[:, :dc] = (dcv * v_ref[...]).astype(BF16)
        o_ref[:, dc:] = (dcv * c_ref[...]).astype(BF16)

    return pl.pallas_call(
        _ignoring_deps(body, 5, deps), name="mix_bwd2", grid=(n_tiles,),
        in_specs=[pl.BlockSpec((ROW_TILE, dc), lambda i: (i, 1)),
                  pl.BlockSpec((ROW_TILE, dc), lambda i: (i, 2)),
                  pl.BlockSpec((ROW_TILE, dc), lambda i: (i, 0)),
                  pl.BlockSpec((HALO_ROWS, dc), lambda i: (_next_halo(i, t), 0)),
                  pl.BlockSpec(conv_w.shape, lambda i: (0, 0))] + [ANY_SPEC] * len(deps),
        out_specs=pl.BlockSpec((ROW_TILE, 2 * dc), lambda i: (i, 0)),
        out_shape=jax.ShapeDtypeStruct((t, 2 * dc), BF16), compiler_params=_params(),
    )(proj, proj, dconv, dconv, conv_w, *deps)


def _ffn_taps(up_ref, prev, nxt, fw_ref, fb_ref, q, r, rows=ROW_CHUNK):
    get = lambda s, n: up_ref[q, pl.ds(s, n), :]
    x0, x1, x2 = (_tile_rows(get, prev, nxt, r - k * N_SEG, rows) for k in range(3))
    w = fw_ref[q]
    return w[0:1] * x2 + w[1:2] * x1 + w[2:3] * x0 + fb_ref[q], x2, x1, x0


def _ffn_fwd(up_pre, fw, fb):
    nb, t, w = up_pre.shape
    half = nb // 2

    def body(up_ref, halo_ref, fw_ref, fb_ref, act_ref):
        first = pl.program_id(0) == 0
        for q in range(half):
            halo_a = _wrap_prev_halo(halo_ref[q], first)
            halo_v = _wrap_prev_halo(halo_ref[q + half], first)
            for r in range(0, ROW_TILE, ROW_CHUNK):
                a = _ffn_taps(up_ref, halo_a, None, fw_ref, fb_ref, q, r)[0]
                val = _ffn_taps(up_ref, halo_v, None, fw_ref, fb_ref, q + half, r)[0]
                act_ref[q, pl.ds(r, ROW_CHUNK), :] = (a * jax.nn.sigmoid(a) * val).astype(BF16)

    return pl.pallas_call(
        body, name="ffn_fwd", grid=(t // ROW_TILE,),
        in_specs=[pl.BlockSpec((nb, ROW_TILE, w), lambda i: (0, i, 0)),
                  pl.BlockSpec((nb, HALO_ROWS, w), lambda i: (0, _prev_halo(i, t), 0)),
                  pl.BlockSpec(fw.shape, lambda i: (0, 0, 0)),
                  pl.BlockSpec(fb.shape, lambda i: (0, 0, 0))],
        out_specs=pl.BlockSpec((half, ROW_TILE, w), lambda i: (0, i, 0)),
        out_shape=jax.ShapeDtypeStruct((half, t, w), BF16), compiler_params=_params(),
    )(up_pre, up_pre, fw, fb)


def _ffn_bwd(up_pre, dact, fw, fb, deps=()):
    nb, t, w = up_pre.shape
    half = nb // 2
    n_tiles = t // ROW_TILE
    chunks = [(r, ROW_CHUNK) for r in range(0, ROW_TILE, ROW_CHUNK)] + [(ROW_TILE, HALO_ROWS)]

    def body(up_ref, prev_ref, next_ref, dact_ref, dact_next_ref, fw_ref, fb_ref,
             out_ref, dfw_ref, dfb_ref, dup_ref):
        first = pl.program_id(0) == 0
        last = pl.program_id(0) == n_tiles - 1

        @pl.when(first)
        def _():
            dfw_ref[...] = jnp.zeros_like(dfw_ref)
            dfb_ref[...] = jnp.zeros_like(dfb_ref)

        for q in range(half):
            blocks = (q, q + half)
            prev = [_wrap_prev_halo(prev_ref[b], first) for b in blocks]
            nxt = [_wrap_next_halo(next_ref[b], last) for b in blocks]
            dact_next = _wrap_next_halo(dact_next_ref[q].astype(F32), last)
            get_dact = lambda s, n: dact_ref[q, pl.ds(s, n), :].astype(F32)
            sums = [[jnp.zeros((SUBLANES, w), F32)] * 4 for _ in blocks]
            fold = lambda v: sum(v[s:s + SUBLANES] for s in range(0, ROW_CHUNK, SUBLANES))
            for r, rows in chunks:
                taps = [_ffn_taps(up_ref, prev[k], nxt[k], fw_ref, fb_ref, b, r, rows)
                        for k, b in enumerate(blocks)]
                a, val = taps[0][0], taps[1][0]
                da_ct = _tile_rows(get_dact, None, dact_next, r, rows)
                sg = jax.nn.sigmoid(a)
                dup = (da_ct * val * sg * (1.0 + a * (1.0 - sg)), da_ct * a * sg)
                for k in range(2):
                    dup_ref[k, pl.ds(r, rows), :] = dup[k]
                    if r < ROW_TILE:
                        terms = (dup[k], dup[k] * taps[k][1], dup[k] * taps[k][2], dup[k] * taps[k][3])
                        sums[k] = [s + fold(v) for s, v in zip(sums[k], terms)]
            for k, b in enumerate(blocks):
                s_b, s_w0, s_w1, s_w2 = (jnp.sum(p, axis=0, keepdims=True) for p in sums[k])
                dfb_ref[b] += s_b
                for tap, s_w in enumerate((s_w0, s_w1, s_w2)):
                    dfw_ref[b, tap:tap + 1, :] += s_w
                cw = fw_ref[b]
                for r in range(0, ROW_TILE, ROW_CHUNK):
                    d, u1, u2 = (dup_ref[k, pl.ds(r + s * N_SEG, ROW_CHUNK), :] for s in range(3))
                    out_ref[b, pl.ds(r, ROW_CHUNK), :] = (cw[2:3] * d + cw[1:2] * u1 + cw[0:1] * u2).astype(BF16)

    tile = lambda n: pl.BlockSpec((n, ROW_TILE, w), lambda i: (0, i, 0))
    halo = lambda n, index: pl.BlockSpec((n, HALO_ROWS, w), lambda i: (0, index(i, t), 0))
    const = lambda a: pl.BlockSpec(a.shape, lambda i: (0, 0, 0))
    return pl.pallas_call(
        _ignoring_deps(body, 7, deps), name="ffn_bwd", grid=(n_tiles,),
        in_specs=[tile(nb), halo(nb, _prev_halo), halo(nb, _next_halo), tile(half), halo(half, _next_halo),
                  const(fw), const(fb)] + [ANY_SPEC] * len(deps),
        out_specs=[tile(nb), const(fw), const(fb)],
        out_shape=[jax.ShapeDtypeStruct((nb, t, w), BF16), jax.ShapeDtypeStruct(fw.shape, F32),
                   jax.ShapeDtypeStruct(fb.shape, F32)],
        scratch_shapes=[pltpu.VMEM((2, ROW_TILE + HALO_ROWS, w), F32)],
        compiler_params=_params(),
    )(up_pre, up_pre, up_pre, dact, dact, fw, fb, *deps)


def _conv_taps(up_ref, prev, nxt, fw_ref, fb_ref, c0, r, rows=ROW_CHUNK):
    cols = pl.ds(c0, FFN_COLS)
    get = lambda s, n: up_ref[pl.ds(s, n), cols]
    x0, x1, x2 = (_tile_rows(get, prev, nxt, r - k * N_SEG, rows) for k in range(3))
    w = fw_ref[:, cols]
    return w[0:1] * x2 + w[1:2] * x1 + w[2:3] * x0 + fb_ref[:, cols], x2, x1, x0


def _gated_fwd(up_pre, fw, fb):
    t, f2 = up_pre.shape
    f = f2 // 2

    def body(up_ref, halo_ref, fw_ref, fb_ref, act_ref):
        first = pl.program_id(0) == 0
        for c0 in range(0, f, FFN_COLS):
            halo_a = _wrap_prev_halo(halo_ref[:, pl.ds(c0, FFN_COLS)], first)
            halo_v = _wrap_prev_halo(halo_ref[:, pl.ds(f + c0, FFN_COLS)], first)
            for r in range(0, ROW_TILE, ROW_CHUNK):
                a = _conv_taps(up_ref, halo_a, None, fw_ref, fb_ref, c0, r)[0]
                val = _conv_taps(up_ref, halo_v, None, fw_ref, fb_ref, f + c0, r)[0]
                act_ref[pl.ds(r, ROW_CHUNK), pl.ds(c0, FFN_COLS)] = (a * jax.nn.sigmoid(a) * val).astype(BF16)

    const = lambda a: pl.BlockSpec(a.shape, lambda i: (0, 0))
    return pl.pallas_call(
        body, name="ffn_fwd", grid=(t // ROW_TILE,),
        in_specs=[pl.BlockSpec((ROW_TILE, f2), lambda i: (i, 0)),
                  pl.BlockSpec((HALO_ROWS, f2), lambda i: (_prev_halo(i, t), 0)), const(fw), const(fb)],
        out_specs=pl.BlockSpec((ROW_TILE, f), lambda i: (i, 0)),
        out_shape=jax.ShapeDtypeStruct((t, f), BF16), compiler_params=_params(),
    )(up_pre, up_pre, fw, fb)


def _gated_bwd(up_pre, dact, fw, fb, deps=()):
    t, f2 = up_pre.shape
    f = f2 // 2
    n_tiles = t // ROW_TILE
    chunks = [(r, ROW_CHUNK) for r in range(0, ROW_TILE, ROW_CHUNK)] + [(ROW_TILE, HALO_ROWS)]

    def body(up_ref, prev_ref, next_ref, dact_ref, dact_next_ref, fw_ref, fb_ref,
             out_ref, dfw_ref, dfb_ref, dup_ref):
        first = pl.program_id(0) == 0
        last = pl.program_id(0) == n_tiles - 1

        @pl.when(first)
        def _():
            dfw_ref[...] = jnp.zeros_like(dfw_ref)
            dfb_ref[...] = jnp.zeros_like(dfb_ref)

        fold = lambda v: sum(v[s:s + SUBLANES] for s in range(0, ROW_CHUNK, SUBLANES))
        for c0 in range(0, f, FFN_COLS):
            starts = (c0, f + c0)
            prev = [_wrap_prev_halo(prev_ref[:, pl.ds(c, FFN_COLS)], first) for c in starts]
            nxt = [_wrap_next_halo(next_ref[:, pl.ds(c, FFN_COLS)], last) for c in starts]
            dact_next = _wrap_next_halo(dact_next_ref[:, pl.ds(c0, FFN_COLS)].astype(F32), last)
            get_dact = lambda s, n: dact_ref[pl.ds(s, n), pl.ds(c0, FFN_COLS)].astype(F32)
            sums = [[jnp.zeros((SUBLANES, FFN_COLS), F32)] * 4 for _ in starts]
            for r, rows in chunks:
                taps = [_conv_taps(up_ref, prev[k], nxt[k], fw_ref, fb_ref, c, r, rows)
                        for k, c in enumerate(starts)]
                a, val = taps[0][0], taps[1][0]
                da_ct = _tile_rows(get_dact, None, dact_next, r, rows)
                sg = jax.nn.sigmoid(a)
                dup = (da_ct * val * sg * (1.0 + a * (1.0 - sg)), da_ct * a * sg)
                for k in range(2):
                    dup_ref[k, pl.ds(r, rows), :] = dup[k]
                    if r < ROW_TILE:
                        terms = (dup[k], dup[k] * taps[k][1], dup[k] * taps[k][2], dup[k] * taps[k][3])
                        sums[k] = [s + fold(v) for s, v in zip(sums[k], terms)]
            for k, c in enumerate(starts):
                cols = pl.ds(c, FFN_COLS)
                s_b, s_w0, s_w1, s_w2 = (jnp.sum(p, axis=0, keepdims=True) for p in sums[k])
                dfb_ref[:, cols] += s_b
                for tap, s_w in enumerate((s_w0, s_w1, s_w2)):
                    dfw_ref[tap:tap + 1, cols] += s_w
                cw = fw_ref[:, cols]
                for r in range(0, ROW_TILE, ROW_CHUNK):
                    d, u1, u2 = (dup_ref[k, pl.ds(r + s * N_SEG, ROW_CHUNK), :] for s in range(3))
                    out_ref[pl.ds(r, ROW_CHUNK), cols] = (cw[2:3] * d + cw[1:2] * u1 + cw[0:1] * u2).astype(BF16)

    tile = lambda w: pl.BlockSpec((ROW_TILE, w), lambda i: (i, 0))
    halo = lambda w, index: pl.BlockSpec((HALO_ROWS, w), lambda i: (index(i, t), 0))
    const = lambda a: pl.BlockSpec(a.shape, lambda i: (0, 0))
    return pl.pallas_call(
        _ignoring_deps(body, 7, deps), name="ffn_bwd", grid=(n_tiles,),
        in_specs=[tile(f2), halo(f2, _prev_halo), halo(f2, _next_halo), tile(f), halo(f, _next_halo),
                  const(fw), const(fb)] + [ANY_SPEC] * len(deps),
        out_specs=[tile(f2), const(fw), const(fb)],
        out_shape=[jax.ShapeDtypeStruct((t, f2), BF16), jax.ShapeDtypeStruct(fw.shape, F32),
                   jax.ShapeDtypeStruct(fb.shape, F32)],
        scratch_shapes=[pltpu.VMEM((2, ROW_TILE + HALO_ROWS, FFN_COLS), F32)],
        compiler_params=_params(),
    )(up_pre, up_pre, up_pre, dact, dact, fw, fb, *deps)


def _to_segments(a):
    t, c = a.shape
    return a.reshape(N_SEG, t // N_SEG, c).transpose(1, 0, 2).reshape(t, c)


def _from_segments(a):
    t, c = a.shape
    return a.reshape(t // N_SEG, N_SEG, c).transpose(1, 0, 2).reshape(t, c)


def _cmul(ar, ai, br, bi):
    return ar * br - ai * bi, ar * bi + ai * br


def _cpow(ar, ai, n):
    out = None
    while n:
        if n & 1:
            out = (ar, ai) if out is None else _cmul(out[0], out[1], ar, ai)
        ar, ai = _cmul(ar, ai, ar, ai)
        n >>= 1
    return out


def _segment_carries(pr, pi, fr, fi, forward):
    row = lax.broadcasted_iota(jnp.int32, fr.shape, 0)
    edge = row == (0 if forward else N_SEG - 1)
    shift = 1 if forward else N_SEG - 1
    sr, si = jnp.zeros_like(fr), jnp.zeros_like(fi)
    for _ in range(N_SEG - 1):
        tr, ti = _cmul(pr, pi, sr, si)
        sr = jnp.where(edge, 0.0, pltpu.roll(tr + fr, shift, 0))
        si = jnp.where(edge, 0.0, pltpu.roll(ti + fi, shift, 0))
    return sr, si


def _rows(i):
    return pl.ds(pl.multiple_of(i * SUBLANES, SUBLANES), SUBLANES)


def _s5_fwd(proj, u_col, bb_re, bb_im, a_re, a_im, cc_re, cc_im, d_skip):
    t = proj.shape[0]
    nch, _, cs = bb_re.shape
    ds = nch * SSM_CHUNK
    u_blk = u_col // SSM_CHUNK
    steps = t // N_SEG

    def body(u_ref, bbr_ref, bbi_ref, ar_ref, ai_ref, ccr_ref, cci_ref, d_ref, sr_ref, si_ref, y_ref):
        ub = u_ref[...].astype(BF16)
        sr_ref[...] = _dot(ub, bbr_ref[...], NN)
        si_ref[...] = _dot(ub, bbi_ref[...], NN)
        ar = jnp.broadcast_to(ar_ref[...], (N_SEG, cs))
        ai = jnp.broadcast_to(ai_ref[...], (N_SEG, cs))
        zero = jnp.zeros((N_SEG, cs), F32)

        def totals(i, carry):
            tr, ti = _cmul(ar, ai, *carry)
            return tr + sr_ref[_rows(i), :], ti + si_ref[_rows(i), :]

        fr, fi = lax.fori_loop(0, steps, totals, (zero, zero))
        s0r, s0i = _segment_carries(*_cpow(ar, ai, steps), fr, fi, True)

        def scan(i, carry):
            tr, ti = _cmul(ar, ai, *carry)
            nr, ni = tr + sr_ref[_rows(i), :], ti + si_ref[_rows(i), :]
            sr_ref[_rows(i), :] = nr
            si_ref[_rows(i), :] = ni
            return nr, ni

        lax.fori_loop(0, steps, scan, (s0r, s0i))
        y_ref[...] = (_dot(sr_ref[...].astype(BF16), ccr_ref[...], NN)
                      - _dot(si_ref[...].astype(BF16), cci_ref[...], NN)
                      + d_ref[...] * u_ref[...])

    chunk3 = lambda r, c: pl.BlockSpec((None, r, c), lambda j: (j, 0, 0))
    return pl.pallas_call(
        body, name="s5_fwd", grid=(nch,),
        in_specs=[pl.BlockSpec((t, SSM_CHUNK), lambda j: (0, j + u_blk)),
                  chunk3(SSM_CHUNK, cs), chunk3(SSM_CHUNK, cs), chunk3(1, cs), chunk3(1, cs),
                  chunk3(cs, SSM_CHUNK), chunk3(cs, SSM_CHUNK), chunk3(1, SSM_CHUNK)],
        out_specs=[pl.BlockSpec((t, cs), lambda j: (0, j)), pl.BlockSpec((t, cs), lambda j: (0, j)),
                   pl.BlockSpec((t, SSM_CHUNK), lambda j: (0, j))],
        out_shape=[jax.ShapeDtypeStruct((t, nch * cs), F32), jax.ShapeDtypeStruct((t, nch * cs), F32),
                   jax.ShapeDtypeStruct((t, ds), F32)],
        compiler_params=_params(),
    )(proj, bb_re, bb_im, a_re, a_im, cc_re, cc_im, d_skip)


def _s5_bwd(dy, proj, u_col, s_re, s_im, bb_re, bb_im, a_re, a_im, cc_re, cc_im, d_skip):
    t, ds = dy.shape
    nch, _, cs = bb_re.shape
    u_blk = u_col // SSM_CHUNK
    steps = t // N_SEG

    def body(dy_ref, u_ref, sr_ref, si_ref, bbr_ref, bbi_ref, ar_ref, ai_ref, ccr_ref, cci_ref, d_ref,
             du_ref, dbbr_ref, dbbi_ref, dar_ref, dai_ref, dccr_ref, dcci_ref, dd_ref, gr_ref, gi_ref):
        dyv = dy_ref[...]
        dyb = dyv.astype(BF16)
        gr_ref[...] = _dot(dyb, ccr_ref[...], NT)
        gi_ref[...] = -_dot(dyb, cci_ref[...], NT)
        ar = jnp.broadcast_to(ar_ref[...], (N_SEG, cs))
        ai = -jnp.broadcast_to(ai_ref[...], (N_SEG, cs))
        zero = jnp.zeros((N_SEG, cs), F32)

        def totals(k, carry):
            i = steps - 1 - k
            tr, ti = _cmul(ar, ai, *carry)
            return tr + gr_ref[_rows(i), :], ti + gi_ref[_rows(i), :]

        fr, fi = lax.fori_loop(0, steps, totals, (zero, zero))
        e0r, e0i = _segment_carries(*_cpow(ar, ai, steps), fr, fi, False)

        def step(i, gr, gi, pr, pi, acc_r, acc_i):
            tr, ti = _cmul(ar, ai, gr, gi)
            nr, ni = tr + gr_ref[_rows(i), :], ti + gi_ref[_rows(i), :]
            gr_ref[_rows(i), :] = nr
            gi_ref[_rows(i), :] = ni
            return nr, ni, acc_r + nr * pr + ni * pi, acc_i + ni * pr - nr * pi

        def scan(k, carry):
            i = steps - 1 - k
            gr, gi, acc_r, acc_i = carry
            return step(i, gr, gi, sr_ref[_rows(i - 1), :], si_ref[_rows(i - 1), :], acc_r, acc_i)

        gr, gi, acc_r, acc_i = lax.fori_loop(0, steps - 1, scan, (e0r, e0i, zero, zero))
        row = lax.broadcasted_iota(jnp.int32, (N_SEG, cs), 0)
        last = _rows(steps - 1)
        pr = jnp.where(row == 0, 0.0, pltpu.roll(sr_ref[last, :], 1, 0))
        pi = jnp.where(row == 0, 0.0, pltpu.roll(si_ref[last, :], 1, 0))
        _, _, acc_r, acc_i = step(0, gr, gi, pr, pi, acc_r, acc_i)
        dar_ref[...] = jnp.sum(acc_r, axis=0, keepdims=True)
        dai_ref[...] = jnp.sum(acc_i, axis=0, keepdims=True)

        uv = u_ref[...]
        ub = uv.astype(BF16)
        grb = gr_ref[...].astype(BF16)
        gib = gi_ref[...].astype(BF16)
        du = d_ref[...] * dyv + _dot(grb, bbr_ref[...], NT) + _dot(gib, bbi_ref[...], NT)
        du_ref[...] = du.astype(BF16)
        dbbr_ref[...] = _dot(ub, grb, TN)
        dbbi_ref[...] = _dot(ub, gib, TN)
        dccr_ref[...] = _dot(sr_ref[...].astype(BF16), dyb, TN)
        dcci_ref[...] = -_dot(si_ref[...].astype(BF16), dyb, TN)
        dd_ref[...] = jnp.sum(dyv * uv, axis=0, keepdims=True)

    chunk3 = lambda r, c: pl.BlockSpec((None, r, c), lambda j: (j, 0, 0))
    cols = lambda w: pl.BlockSpec((t, w), lambda j: (0, j))
    return pl.pallas_call(
        body, name="s5_bwd", grid=(nch,),
        in_specs=[cols(SSM_CHUNK), pl.BlockSpec((t, SSM_CHUNK), lambda j: (0, j + u_blk)), cols(cs), cols(cs),
                  chunk3(SSM_CHUNK, cs), chunk3(SSM_CHUNK, cs), chunk3(1, cs), chunk3(1, cs),
                  chunk3(cs, SSM_CHUNK), chunk3(cs, SSM_CHUNK), chunk3(1, SSM_CHUNK)],
        out_specs=[cols(SSM_CHUNK), chunk3(SSM_CHUNK, cs), chunk3(SSM_CHUNK, cs), chunk3(1, cs),
                   chunk3(1, cs), chunk3(cs, SSM_CHUNK), chunk3(cs, SSM_CHUNK), chunk3(1, SSM_CHUNK)],
        out_shape=[jax.ShapeDtypeStruct((t, ds), BF16),
                   jax.ShapeDtypeStruct((nch, SSM_CHUNK, cs), F32), jax.ShapeDtypeStruct((nch, SSM_CHUNK, cs), F32),
                   jax.ShapeDtypeStruct((nch, 1, cs), F32), jax.ShapeDtypeStruct((nch, 1, cs), F32),
                   jax.ShapeDtypeStruct((nch, cs, SSM_CHUNK), F32), jax.ShapeDtypeStruct((nch, cs, SSM_CHUNK), F32),
                   jax.ShapeDtypeStruct((nch, 1, SSM_CHUNK), F32)],
        scratch_shapes=[pltpu.VMEM((t, cs), F32), pltpu.VMEM((t, cs), F32)],
        compiler_params=_params(),
    )(dy, proj, s_re, s_im, bb_re, bb_im, a_re, a_im, cc_re, cc_im, d_skip)


def _discretize(lr, li, log_dt, br, bi):
    dt = jnp.exp(log_dt)
    mag = jnp.exp(lr * dt)
    ang = li * dt
    a_re = mag * jnp.cos(ang)
    a_im = mag * jnp.sin(ang)
    den = lr * lr + li * li
    nr = a_re - 1.0
    f_re = (nr * lr + a_im * li) / den
    f_im = (a_im * lr - nr * li) / den
    return a_re, a_im, f_re * br - f_im * bi, f_re * bi + f_im * br


def _whole(shape):
    return pl.BlockSpec(shape, lambda: (0,) * len(shape))


def _disc_fwd(lr, li, log_dt, br, bi, deps=()):
    def body(lr_ref, li_ref, dt_ref, br_ref, bi_ref, ar_ref, ai_ref, bbr_ref, bbi_ref):
        outs = _discretize(lr_ref[...], li_ref[...], dt_ref[...], br_ref[...], bi_ref[...])
        for ref, val in zip((ar_ref, ai_ref, bbr_ref, bbi_ref), outs):
            ref[...] = val

    args = (lr, li, log_dt, br, bi)
    outs = (lr, lr, br, br)
    return pl.pallas_call(
        _ignoring_deps(body, 5, deps), name="disc_fwd",
        in_specs=[_whole(a.shape) for a in args] + [ANY_SPEC] * len(deps),
        out_specs=[_whole(a.shape) for a in outs],
        out_shape=[jax.ShapeDtypeStruct(a.shape, F32) for a in outs],
    )(*args, *deps)


def _disc_bwd(lr, li, log_dt, br, bi, dar, dai, dbbr, dbbi):
    def body(lr_ref, li_ref, dt_ref, br_ref, bi_ref, dar_ref, dai_ref, dbbr_ref, dbbi_ref,
             dlr_ref, dli_ref, ddt_ref, dbr_ref, dbi_ref):
        _, vjp = jax.vjp(_discretize, lr_ref[...], li_ref[...], dt_ref[...], br_ref[...], bi_ref[...])
        grads = vjp((dar_ref[...], dai_ref[...], dbbr_ref[...], dbbi_ref[...]))
        for ref, val in zip((dlr_ref, dli_ref, ddt_ref, dbr_ref, dbi_ref), grads):
            ref[...] = val

    args = (lr, li, log_dt, br, bi, dar, dai, dbbr, dbbi)
    outs = (lr, li, log_dt, br, bi)
    return pl.pallas_call(
        body, name="disc_bwd", in_specs=[_whole(a.shape) for a in args],
        out_specs=[_whole(a.shape) for a in outs],
        out_shape=[jax.ShapeDtypeStruct(a.shape, F32) for a in outs],
    )(*args)


def _adamw(w, g, m, v):
    m = ADAM_B1 * m + (1.0 - ADAM_B1) * g
    v = ADAM_B2 * v + (1.0 - ADAM_B2) * (g * g)
    m_hat = m / ADAM_BC1
    v_hat = v / ADAM_BC2
    delta = -ADAM_LR * (m_hat / (jnp.sqrt(v_hat) + ADAM_EPS) + ADAM_WD * w)
    return delta, m, v


def _adamw_reduce(name, parts, w, m, v):
    _, r, c = parts.shape
    tr = r
    for cand in (256, 176, 128):
        if r % cand == 0:
            tr = cand
            break

    def body(p_ref, w_ref, m_ref, v_ref, g_ref, d_ref, nm_ref, nv_ref):
        g = p_ref[0].astype(F32)
        for k in range(1, N_DEV):
            g = g + p_ref[k].astype(F32)
        delta, nm, nv = _adamw(w_ref[...], g, m_ref[...], v_ref[...])
        g_ref[...] = g
        d_ref[...] = delta
        nm_ref[...] = nm
        nv_ref[...] = nv

    blk = pl.BlockSpec((tr, c), lambda i: (i, 0))
    return pl.pallas_call(
        body, name=name, grid=(r // tr,),
        in_specs=[pl.BlockSpec((N_DEV, tr, c), lambda i: (0, i, 0)), blk, blk, blk],
        out_specs=[blk] * 4, out_shape=[jax.ShapeDtypeStruct((r, c), F32)] * 4,
        compiler_params=_params(),
    )(parts, w, m, v)


def _sum_parts(name, parts):
    _, r, c = parts.shape

    def body(p_ref, o_ref):
        g = p_ref[0]
        for k in range(1, N_DEV):
            g = g + p_ref[k]
        o_ref[...] = g

    return pl.pallas_call(
        body, name=name, in_specs=[_whole(parts.shape)], out_specs=_whole((r, c)),
        out_shape=jax.ShapeDtypeStruct((r, c), F32), compiler_params=_params(),
    )(parts)


def _adamw_many(name, grads, ws, ms, vs):
    n = len(grads)

    def body(*refs):
        ins, outs = refs[:4 * n], refs[4 * n:]
        for i in range(n):
            g, w, m, v = (ins[j * n + i][...] for j in range(4))
            for ref, val in zip((outs[i], outs[n + i], outs[2 * n + i]), _adamw(w, g, m, v)):
                ref[...] = val

    args = (*grads, *ws, *ms, *vs)
    outs = pl.pallas_call(
        body, name=name, in_specs=[_whole(a.shape) for a in args],
        out_specs=[_whole(a.shape) for a in ws] * 3,
        out_shape=[jax.ShapeDtypeStruct(a.shape, F32) for a in ws] * 3, compiler_params=_params(),
    )(*args)
    return outs[:n], outs[n:2 * n], outs[2 * n:]


def _pack(arrays, rows):
    flat = jnp.concatenate([a.reshape(-1) for a in arrays])
    return jnp.pad(flat, (0, rows * LANES - flat.shape[0])).reshape(rows, LANES)


def _unpack(packed, shapes):
    flat = packed.reshape(-1)
    out, off = [], 0
    for s in shapes:
        n = math.prod(s)
        out.append(flat[off:off + n].reshape(s))
        off += n
    return out


def _packed_rows(shapes):
    n = sum(math.prod(s) for s in shapes)
    return -(-n // (SUBLANES * LANES)) * SUBLANES


def _block_diag(x):
    j, g, r, c = x.shape
    eye = jnp.eye(g, dtype=x.dtype)
    return (x[:, :, :, None, :] * eye[None, :, None, :, None]).reshape(j, g * r, g * c)


def _diag_blocks(x, g):
    j, gr, gc = x.shape
    r, c = gr // g, gc // g
    eye = jnp.eye(g, dtype=x.dtype)
    return (x.reshape(j, g, r, g, c) * eye[None, :, None, :, None]).sum(axis=3)


def kernel(x, meta_tokens, norm_mix_g, w_in, conv_w, ssm_lam_re, ssm_lam_im, ssm_log_dt, ssm_b_re, ssm_b_im, ssm_c_re, ssm_c_im, ssm_d, ssm_w_glu, gain_conv_out, gain_ssm_out, w_out, norm_ffn_g, w_up, ffn_conv_w, ffn_conv_b, w_down, norm_final_g, loss_target, m_meta_tokens, m_norm_mix_g, m_w_in, m_conv_w, m_ssm_lam_re, m_ssm_lam_im, m_ssm_log_dt, m_ssm_b_re, m_ssm_b_im, m_ssm_c_re, m_ssm_c_im, m_ssm_d, m_ssm_w_glu, m_gain_conv_out, m_gain_ssm_out, m_w_out, m_norm_ffn_g, m_w_up, m_ffn_conv_w, m_ffn_conv_b, m_w_down, m_norm_final_g, v_meta_tokens, v_norm_mix_g, v_w_in, v_conv_w, v_ssm_lam_re, v_ssm_lam_im, v_ssm_log_dt, v_ssm_b_re, v_ssm_b_im, v_ssm_c_re, v_ssm_c_im, v_ssm_d, v_ssm_w_glu, v_gain_conv_out, v_gain_ssm_out, v_w_out, v_norm_ffn_g, v_w_up, v_ffn_conv_w, v_ffn_conv_b, v_w_down, v_norm_final_g):
    weights = dict(meta_tokens=meta_tokens, norm_mix_g=norm_mix_g, w_in=w_in, conv_w=conv_w, ssm_lam_re=ssm_lam_re, ssm_lam_im=ssm_lam_im, ssm_log_dt=ssm_log_dt, ssm_b_re=ssm_b_re, ssm_b_im=ssm_b_im, ssm_c_re=ssm_c_re, ssm_c_im=ssm_c_im, ssm_d=ssm_d, ssm_w_glu=ssm_w_glu, gain_conv_out=gain_conv_out, gain_ssm_out=gain_ssm_out, w_out=w_out, norm_ffn_g=norm_ffn_g, w_up=w_up, ffn_conv_w=ffn_conv_w, ffn_conv_b=ffn_conv_b, w_down=w_down, norm_final_g=norm_final_g)
    mom_m = dict(meta_tokens=m_meta_tokens, norm_mix_g=m_norm_mix_g, w_in=m_w_in, conv_w=m_conv_w, ssm_lam_re=m_ssm_lam_re, ssm_lam_im=m_ssm_lam_im, ssm_log_dt=m_ssm_log_dt, ssm_b_re=m_ssm_b_re, ssm_b_im=m_ssm_b_im, ssm_c_re=m_ssm_c_re, ssm_c_im=m_ssm_c_im, ssm_d=m_ssm_d, ssm_w_glu=m_ssm_w_glu, gain_conv_out=m_gain_conv_out, gain_ssm_out=m_gain_ssm_out, w_out=m_w_out, norm_ffn_g=m_norm_ffn_g, w_up=m_w_up, ffn_conv_w=m_ffn_conv_w, ffn_conv_b=m_ffn_conv_b, w_down=m_w_down, norm_final_g=m_norm_final_g)
    mom_v = dict(meta_tokens=v_meta_tokens, norm_mix_g=v_norm_mix_g, w_in=v_w_in, conv_w=v_conv_w, ssm_lam_re=v_ssm_lam_re, ssm_lam_im=v_ssm_lam_im, ssm_log_dt=v_ssm_log_dt, ssm_b_re=v_ssm_b_re, ssm_b_im=v_ssm_b_im, ssm_c_re=v_ssm_c_re, ssm_c_im=v_ssm_c_im, ssm_d=v_ssm_d, ssm_w_glu=v_ssm_w_glu, gain_conv_out=v_gain_conv_out, gain_ssm_out=v_gain_ssm_out, w_out=v_w_out, norm_ffn_g=v_norm_ffn_g, w_up=v_w_up, ffn_conv_w=v_ffn_conv_w, ffn_conv_b=v_ffn_conv_b, w_down=v_w_down, norm_final_g=v_norm_final_g)
    names = list(weights)

    n_meta, d_meta = meta_tokens.shape
    seq, d = x.shape[1], x.shape[2]
    rows_used = n_meta + seq
    t = -(-rows_used // ROW_TILE) * ROW_TILE
    d_in_s = w_in.shape[2]
    dc_s = conv_w.shape[2]
    dc = dc_s * N_DEV
    ds = ssm_w_glu.shape[2]
    n_groups, n_state, grp = ssm_b_re.shape[1:]
    ns = n_groups * n_state
    nch = ds // SSM_CHUNK
    gpc = n_groups // nch
    ff_s = w_up.shape[2]
    dn_s = w_down.shape[1]
    assert 3 * dc + ds == d_in_s * N_DEV and 2 * dn_s == ff_s and t % (N_SEG * SUBLANES) == 0

    small_shard = jnp.concatenate([meta_tokens.reshape(-1), conv_w.reshape(-1), ffn_conv_w.reshape(-1)])
    n_small = small_shard.shape[0]
    small_rows = -(-n_small // LANES)
    small_shard = jnp.pad(small_shard, (0, small_rows * LANES - n_small)).reshape(small_rows, LANES)
    ag, ag_token = _gather2_start("gather_weights_start", [
        small_shard, w_in[0].astype(BF16), ssm_w_glu[0].astype(BF16), w_out[0].astype(BF16),
        jnp.swapaxes(w_up[0], 0, 1).astype(BF16), w_down[0].astype(BF16)])
    fb = ffn_conv_b

    gh = n_groups * grp
    per_h = lambda a: jnp.broadcast_to(a.reshape(n_groups, 1, -1), (n_groups, grp, n_state)).reshape(gh, n_state)
    ghp = lambda a: a.transpose(0, 1, 3, 2).reshape(gh, n_state)
    lr, li, log_dt_e = per_h(ssm_lam_re), per_h(ssm_lam_im), per_h(ssm_log_dt)
    br, bi = ghp(ssm_b_re), ghp(ssm_b_im)
    a_re, a_im, bb_re, bb_im = _disc_fwd(lr, li, log_dt_e, br, bi, deps=(ag_token,))
    cs = gpc * n_state
    chunk_row = lambda a: a.reshape(n_groups, grp, n_state)[:, 0].reshape(nch, 1, cs)
    to_bb = lambda a: _block_diag(a.reshape(nch, gpc, grp, n_state)).astype(BF16)
    to_cc = lambda a: _block_diag(a.reshape(nch, gpc, grp, n_state).transpose(0, 1, 3, 2)).astype(BF16)
    bbm_re, bbm_im = to_bb(bb_re), to_bb(bb_im)
    ccm_re, ccm_im = to_cc(ssm_c_re), to_cc(ssm_c_im)
    a_re_c, a_im_c = chunk_row(a_re), chunk_row(a_im)
    d_skip = ssm_d.reshape(nch, 1, SSM_CHUNK)

    target = jnp.pad(loss_target[0], ((n_meta, t - rows_used), (0, 0)))
    token = _gather2_forward("gather_weights_forward_small", ag, (0,), (bbm_im, ccm_im, target))
    (g_small,) = _gather2_wait("gather_weights_wait_small", ag, (0,), token)
    g_small = g_small.reshape(N_DEV, -1)
    o1 = n_meta * d_meta
    o2 = o1 + 3 * dc_s
    meta_full = g_small[:, :o1].reshape(N_DEV, n_meta, d_meta).transpose(1, 0, 2).reshape(n_meta, d)
    conv_w_f = g_small[:, o1:o2].reshape(N_DEV, 3, dc_s).transpose(1, 0, 2).reshape(3, dc)
    fw = g_small[:, o2:o2 + 3 * ff_s].reshape(N_DEV, 3, ff_s).transpose(1, 0, 2).reshape(3, N_DEV * ff_s)
    h0_time = jnp.concatenate([meta_full, x[0], jnp.zeros((t - rows_used, d), F32)], axis=0)
    full_t = lambda w: pl.BlockSpec((t, w), lambda *_: (0, 0))

    h0, hn1 = _rms_fwd_in("norm_mix", h0_time, norm_mix_g)
    token = _gather2_forward("gather_weights_forward_in", ag, (1,), (hn1,))
    (g_in,) = _gather2_wait("gather_weights_wait_in", ag, (1,), token)
    proj = _mm("proj", hn1, g_in, dims=NN, grid=(N_DEV,), a_spec=full_t(d),
               b_spec=pl.BlockSpec((None, d, d_in_s), lambda j: (j, 0, 0)),
               o_spec=pl.BlockSpec((t, d_in_s), lambda j: (0, j)),
               out_shape=jax.ShapeDtypeStruct((t, N_DEV * d_in_s), F32))
    s_re, s_im, y_ssm = _s5_fwd(proj, 3 * dc, bbm_re, bbm_im, a_re_c, a_im_c, ccm_re, ccm_im, d_skip)
    token = _gather2_forward("gather_weights_forward_up", ag, (2, 3, 4), (y_ssm,))
    g_glu, g_out = _gather2_wait("gather_weights_wait_mix", ag, (2, 3), token)
    w_out_f = g_out.reshape(-1, d)
    w_glu_f = g_glu.reshape(ds, ds)
    mixed, z_glu = _mix_fwd(proj, y_ssm, w_glu_f, conv_w_f, gain_conv_out, gain_ssm_out)
    tn_out = 256
    h1 = _mm("out_proj", mixed, w_out_f, dims=NN, grid=(d // tn_out,), a_spec=full_t(dc + ds),
             b_spec=pl.BlockSpec((dc + ds, tn_out), lambda j: (0, j)),
             o_spec=pl.BlockSpec((t, tn_out), lambda j: (0, j)),
             out_shape=jax.ShapeDtypeStruct((t, d), F32),
             res=h0, res_spec=pl.BlockSpec((t, tn_out), lambda j: (0, j)))
    hn2 = _rms_fwd("norm_ffn", h1, norm_ffn_g)
    token = _gather2_forward("gather_weights_forward_down", ag, (5,), (hn2,))
    (g_up,) = _gather2_wait("gather_weights_wait_up", ag, (4,), token)
    f2 = N_DEV * ff_s
    pair = 2 * ff_s
    w_up_t = g_up.reshape(f2, d)
    up_pre = _mm("up_proj", hn2, w_up_t, dims=NT, grid=(f2 // pair,), a_spec=full_t(d),
                 b_spec=pl.BlockSpec((pair, d), lambda j: (j, 0)),
                 o_spec=pl.BlockSpec((t, pair), lambda j: (0, j)),
                 out_shape=jax.ShapeDtypeStruct((t, f2), F32))
    act = _gated_fwd(up_pre, fw, fb)
    (g_down,) = _gather2_wait("gather_weights_wait_down", ag, (5,), act)
    w_down_f = g_down.reshape(f2 // 2, d)
    h2 = _mm("down_proj", act, w_down_f, dims=NN, grid=(d // tn_out,), a_spec=full_t(f2 // 2),
             b_spec=pl.BlockSpec((f2 // 2, tn_out), lambda j: (0, j)),
             o_spec=pl.BlockSpec((t, tn_out), lambda j: (0, j)),
             out_shape=jax.ShapeDtypeStruct((t, d), F32),
             res=h1, res_spec=pl.BlockSpec((t, tn_out), lambda j: (0, j)))

    dh2, dh2_b, loss_part, d_norm_final = _loss_bwd(h2, target, norm_final_g.reshape(1, d), n_meta, rows_used)
    dact = _mm("down_dgrad", dh2_b, w_down_f, dims=NT, grid=(f2 // 2 // pair,), a_spec=full_t(d),
               b_spec=pl.BlockSpec((pair, d), lambda j: (j, 0)),
               o_spec=pl.BlockSpec((t, pair), lambda j: (0, j)),
               out_shape=jax.ShapeDtypeStruct((t, f2 // 2), BF16))
    dw_down = _mm("down_wgrad", act, dh2_b, dims=TN, grid=(f2 // 2 // pair,),
                  a_spec=pl.BlockSpec((t, pair), lambda j: (0, j)), b_spec=full_t(d),
                  o_spec=pl.BlockSpec((pair, d), lambda j: (j, 0)),
                  out_shape=jax.ShapeDtypeStruct((f2 // 2, d), BF16))
    ex_down, token = _send_start("exchange_down_start", [dw_down.reshape(N_DEV, dn_s, d)], gather=False)
    dup_pre, d_fw, d_fb = _gated_bwd(up_pre, dact, fw, fb, deps=(token,))
    wide = _wide_tile(t)
    dhn2 = _mm("up_dgrad", dup_pre, w_up_t, dims=NN, grid=(t // wide,),
               a_spec=pl.BlockSpec((wide, f2), lambda i: (i, 0)),
               b_spec=pl.BlockSpec((f2, d), lambda i: (0, 0)),
               o_spec=pl.BlockSpec((wide, d), lambda i: (i, 0)),
               out_shape=jax.ShapeDtypeStruct((t, d), F32))
    dw_up = _mm("up_wgrad", dup_pre, hn2, dims=TN, grid=(f2 // pair,),
                a_spec=pl.BlockSpec((t, pair), lambda j: (0, j)), b_spec=full_t(d),
                o_spec=pl.BlockSpec((pair, d), lambda j: (j, 0)),
                out_shape=jax.ShapeDtypeStruct((f2, d), BF16))
    ex_up, token = _send_start("exchange_up_start", [dw_up.reshape(N_DEV, ff_s, d)], gather=False)
    dh1, dh1_b, d_norm_ffn = _rms_bwd_res("norm_ffn_bwd", dh2, dhn2, h1, norm_ffn_g, deps=(token,))
    dmixed = _mm("out_dgrad", dh1_b, w_out_f, dims=NT, grid=((dc + ds) // tn_out,), a_spec=full_t(d),
                 b_spec=pl.BlockSpec((tn_out, d), lambda i: (i, 0)),
                 o_spec=pl.BlockSpec((t, tn_out), lambda i: (0, i)),
                 out_shape=jax.ShapeDtypeStruct((t, dc + ds), F32))
    dw_out = _mm("out_wgrad", mixed, dh1_b, dims=TN, grid=((dc + ds) // tn_out,),
                 a_spec=pl.BlockSpec((t, tn_out), lambda i: (0, i)), b_spec=full_t(d),
                 o_spec=pl.BlockSpec((tn_out, d), lambda i: (i, 0)),
                 out_shape=jax.ShapeDtypeStruct((dc + ds, d), BF16))
    db_gate, dconv, dy_ssm, d_wglu, d_conv_w, d_gain_c, d_gain_s = _mix_bwd1(
        proj, y_ssm, z_glu, dmixed, w_glu_f, conv_w_f, gain_conv_out, gain_ssm_out)
    (du, d_bbm_re, d_bbm_im, d_a_re, d_a_im, d_ccm_re, d_ccm_im, d_dskip) = _s5_bwd(
        dy_ssm, proj, 3 * dc, s_re, s_im, bbm_re, bbm_im, a_re_c, a_im_c, ccm_re, ccm_im, d_skip)

    from_bb = lambda a: _diag_blocks(a, gpc).reshape(gh, n_state)
    from_cc = lambda a: _diag_blocks(a, gpc).transpose(0, 1, 3, 2).reshape(gh, n_state)
    first_h = lambda a: jnp.pad(a.reshape(n_groups, 1, n_state), ((0, 0), (0, grp - 1), (0, 0))).reshape(gh, n_state)
    over_h = lambda a: a.reshape(n_groups, grp, n_state).sum(axis=1)
    d_lr, d_li, d_dt_e, d_br, d_bi = _disc_bwd(
        lr, li, log_dt_e, br, bi, first_h(d_a_re), first_h(d_a_im), from_bb(d_bbm_re), from_bb(d_bbm_im))

    rep2d = dict(
        ssm_lam_re=(n_groups, n_state), ssm_lam_im=(n_groups, n_state), ssm_log_dt=(1, n_groups),
        ssm_b_re=(gh, n_state), ssm_b_im=(gh, n_state), ssm_c_re=(gh, n_state),
        ssm_c_im=(gh, n_state), ssm_d=(n_groups, grp), gain_conv_out=(1, dc),
        gain_ssm_out=(1, ds), norm_ffn_g=(1, d), ffn_conv_b=(1, N_DEV * ff_s), norm_final_g=(1, d))
    rep_names = list(rep2d)
    rep_grads = dict(
        ssm_lam_re=over_h(d_lr), ssm_lam_im=over_h(d_li), ssm_log_dt=over_h(d_dt_e).sum(axis=1),
        ssm_b_re=d_br, ssm_b_im=d_bi, ssm_c_re=from_cc(d_ccm_re), ssm_c_im=from_cc(d_ccm_im),
        ssm_d=d_dskip, gain_conv_out=d_gain_c, gain_ssm_out=d_gain_s, norm_ffn_g=d_norm_ffn,
        ffn_conv_b=d_fb, norm_final_g=d_norm_final)
    rep_shapes = [rep2d[n] for n in rep_names] + [(1, 1)]
    rep_rows = _packed_rows(rep_shapes)
    rep_pack = _pack([rep_grads[n] for n in rep_names] + [loss_part], rep_rows)
    ex_mix, token = _send_start("exchange_mix_start", [
        dw_out.reshape(N_DEV, -1, d), d_wglu.astype(BF16).reshape(N_DEV, -1, ds),
        d_conv_w.reshape(3, N_DEV, dc_s).transpose(1, 0, 2),
        d_fw.reshape(3, N_DEV, ff_s).transpose(1, 0, 2), rep_pack],
        gather=[False, False, False, False, True])
    dcdv = _mix_bwd2(proj, dconv, conv_w_f, deps=(token,))
    dproj = jnp.concatenate([db_gate, dcdv, du], axis=1)
    d_in = N_DEV * d_in_s
    w_in_rows = g_in.transpose(1, 0, 2).reshape(d, d_in)
    dhn1 = _mm("proj_dgrad", dproj, w_in_rows, dims=NT, grid=(t // wide,),
               a_spec=pl.BlockSpec((wide, d_in), lambda i: (i, 0)),
               b_spec=pl.BlockSpec((d, d_in), lambda i: (0, 0)),
               o_spec=pl.BlockSpec((wide, d), lambda i: (i, 0)),
               out_shape=jax.ShapeDtypeStruct((t, d), F32))
    dw_in = _mm("proj_wgrad", hn1, dproj, dims=TN, grid=(N_DEV,), a_spec=full_t(d),
                b_spec=pl.BlockSpec((t, d_in_s), lambda j: (0, j)),
                o_spec=pl.BlockSpec((None, d, d_in_s), lambda j: (j, 0, 0)),
                out_shape=jax.ShapeDtypeStruct((N_DEV, d, d_in_s), BF16))
    ex_in, token = _send_start("exchange_in_start", [dw_in], gather=False)
    dh0, d_norm_mix = _rms_bwd_out("norm_mix_bwd", dh1, dhn1, h0, norm_mix_g, deps=(token,))
    grad_x = dh0[n_meta:rows_used][None]
    d_meta_b = dh0[:n_meta].reshape(n_meta, N_DEV, d_meta).transpose(1, 0, 2)
    ex_last, ex_last_token = _send_start("exchange_last_start", [d_meta_b, d_norm_mix], gather=[False, True])

    shard_out = {}
    (rep_parts,) = _send_wait("gather_small_grads_wait", ex_mix, (4,), ex_last_token)
    rep_sum = _sum_parts("sum_small_grads", rep_parts)
    *rep_g, loss = _unpack(rep_sum, rep_shapes)
    loss = loss.reshape(())
    swapped = ("ssm_b_re", "ssm_b_im")
    to2d = lambda n, a: ghp(a) if n in swapped else a.reshape(rep2d[n])
    from2d = lambda n, a: (a.reshape(1, n_groups, grp, n_state).transpose(0, 1, 3, 2) if n in swapped
                           else a.reshape(weights[n].shape))
    as2d = lambda tree: [to2d(n, tree[n]) for n in rep_names]
    rep_res = _adamw_many("adamw_replicated", rep_g, as2d(weights), as2d(mom_m), as2d(mom_v))
    for i, n in enumerate(rep_names):
        shard_out[n] = [from2d(n, r) for r in (rep_g[i], *(res[i] for res in rep_res))]

    def update(n, parts, transposed=False):
        sh = weights[n].shape
        two_d = lambda a: (jnp.swapaxes(a[0], 0, 1) if transposed else a.reshape(parts.shape[1:]))
        res = _adamw_reduce("adamw_" + n, parts, two_d(weights[n]), two_d(mom_m[n]), two_d(mom_v[n]))
        shard_out[n] = [(jnp.swapaxes(r, 0, 1) if transposed else r).reshape(sh) for r in res]
        return res[0]

    (p_down,) = _send_wait("exchange_down_wait", ex_down, (0,), rep_sum)
    done = update("w_down", p_down)
    (p_up,) = _send_wait("exchange_up_wait", ex_up, (0,), done)
    done = update("w_up", p_up, transposed=True)
    p_out, p_glu, p_cw, p_fw = _send_wait("exchange_mix_wait", ex_mix, (0, 1, 2, 3), done)
    update("w_out", p_out)
    update("ssm_w_glu", p_glu)
    update("conv_w", p_cw)
    done = update("ffn_conv_w", p_fw)
    p_meta, p_nm = _send_wait("exchange_last_wait", ex_last, (0, 1), done)
    update("meta_tokens", p_meta)
    done = update("norm_mix_g", p_nm)
    (p_in,) = _send_wait("exchange_in_wait", ex_in, (0,), done)
    update("w_in", p_in)

    grads = [shard_out[n][0] for n in names]
    deltas = [shard_out[n][1] for n in names]
    new_m = [shard_out[n][2] for n in names]
    new_v = [shard_out[n][3] for n in names]
    return (loss, grad_x, *grads, *deltas, *new_m, *new_v)
```

```python
import functools
import math

import jax
import jax.numpy as jnp
from jax import lax
from jax.experimental import pallas as pl
from jax.experimental.pallas import tpu as pltpu

F32 = jnp.float32
BF16 = jnp.bfloat16
MESH = pl.DeviceIdType.MESH

N_DEV = 8
RMS_EPS = 1e-6
ADAM_LR = 0.001
ADAM_B1 = 0.9
ADAM_B2 = 0.999
ADAM_EPS = 1e-08
ADAM_WD = 0.01
ADAM_STEP = 10
ADAM_BC1 = 1.0 - ADAM_B1 ** ADAM_STEP
ADAM_BC2 = 1.0 - ADAM_B2 ** ADAM_STEP

SUBLANES = 8
LANES = 128
ROW_TILE = 128
ROW_CHUNK = 32
WIDE_TILES = (544, 256, 128)
FFN_COLS = 256
N_SEG = 8
HALO_ROWS = 16
SSM_CHUNK = 128
VMEM_LIMIT = 48 * 1024 * 1024

NN = ((1,), (0,))
NT = ((1,), (1,))
TN = ((0,), (0,))


def _params(**kw):
    return pltpu.CompilerParams(vmem_limit_bytes=VMEM_LIMIT, **kw)


def _dot(a, b, dims):
    return lax.dot_general(a, b, (dims, ((), ())), preferred_element_type=F32)


def _mean_sq_rsqrt(x):
    return lax.rsqrt(jnp.mean(x * x, axis=-1, keepdims=True) + RMS_EPS)


def _rms_bwd(x, r, g, dy):
    xhat = x * r
    dxh = dy * g
    dx = r * (dxh - xhat * jnp.mean(dxh * xhat, axis=-1, keepdims=True))
    return dx, dy * xhat


def _gelu(y):
    c = math.sqrt(2.0 / math.pi)
    t = jnp.tanh(c * (y + 0.044715 * y * y * y))
    return 0.5 * y * (1.0 + t), t


def _gelu_grad(y, t):
    c = math.sqrt(2.0 / math.pi)
    return 0.5 * (1.0 + t) + 0.5 * y * (1.0 - t * t) * c * (1.0 + 3.0 * 0.044715 * y * y)


def _wrap_prev_halo(halo, first):
    seg = lax.broadcasted_iota(jnp.int32, halo.shape, 0) % N_SEG
    wrapped = jnp.where(seg == 0, 0.0, pltpu.roll(halo, 1, 0))
    return jnp.where(first, wrapped, halo)


def _wrap_next_halo(halo, last):
    seg = lax.broadcasted_iota(jnp.int32, halo.shape, 0) % N_SEG
    wrapped = jnp.where(seg == N_SEG - 1, 0.0, pltpu.roll(halo, halo.shape[0] - 1, 0))
    return jnp.where(last, wrapped, halo)


def _tile_rows(get, prev, nxt, s, n, tile_rows=ROW_TILE):
    parts = []
    if s < 0:
        parts.append(prev[HALO_ROWS + s:HALO_ROWS + min(s + n, 0)])
    lo, hi = max(s, 0), min(s + n, tile_rows)
    if hi > lo:
        parts.append(get(lo, hi - lo))
    if s + n > tile_rows:
        parts.append(nxt[max(s - tile_rows, 0):s + n - tile_rows])
    return parts[0] if len(parts) == 1 else jnp.concatenate(parts, axis=0)


def _dev_index(p):
    return 4 * p[0] + 2 * p[1] + p[2]


def _allgather(name, shards, deps=()):
    n = len(shards)

    def body(*refs):
        ins, outs = refs[:n], refs[n:2 * n]
        send_sems, recv_sems, local_sems = refs[2 * n:]
        x, y, c = lax.axis_index("x"), lax.axis_index("y"), lax.axis_index("c")
        me, sibling = (x, y, c), (x, y, 1 - c)
        chips = [(1 - x, y), (x, 1 - y), (1 - x, 1 - y)]

        def copy(a, k, block, to, src=None):
            dst = outs[a].at[_dev_index(block)]
            return pltpu.make_async_remote_copy(
                src_ref=dst if src is None else src, dst_ref=dst,
                send_sem=send_sems.at[a, k], recv_sem=recv_sems.at[a, k],
                device_id=to, device_id_type=MESH)

        mine = [pltpu.make_async_copy(ins[a], outs[a].at[_dev_index(me)], local_sems.at[a])
                for a in range(n)]
        for cp in mine:
            cp.start()
        first = []
        for a in range(n):
            first.append(copy(a, 0, me, sibling, src=ins[a]))
            for j, chip in enumerate(chips):
                first.append(copy(a, 1 + j, me, (*chip, c), src=ins[a]))
        for cp in first:
            cp.start()
        passed = []
        for j, chip in enumerate(chips):
            for a in range(n):
                copy(a, 1 + j, (*chip, c), me).wait_recv()
                fwd = copy(a, 4 + j, (*chip, c), sibling)
                fwd.start()
                passed.append(fwd)
        for a in range(n):
            copy(a, 0, sibling, me).wait_recv()
            for j, chip in enumerate(chips):
                copy(a, 4 + j, (*chip, 1 - c), me).wait_recv()
        for cp in first + passed:
            cp.wait_send()
        for cp in mine:
            cp.wait()

    any_spec = pl.BlockSpec(memory_space=pl.ANY)
    return pl.pallas_call(
        _ignoring_deps(body, n, deps), name=name,
        out_shape=[jax.ShapeDtypeStruct((N_DEV,) + s.shape, s.dtype) for s in shards],
        in_specs=[any_spec] * (n + len(deps)), out_specs=[any_spec] * n,
        scratch_shapes=[pltpu.SemaphoreType.DMA((n, 7)), pltpu.SemaphoreType.DMA((n, 7)),
                        pltpu.SemaphoreType.DMA((n,))],
    )(*shards, *deps)


HBM_SPEC = pl.BlockSpec(memory_space=pltpu.HBM)
SEM_SPEC = pl.BlockSpec(memory_space=pltpu.SEMAPHORE)
ANY_SPEC = pl.BlockSpec(memory_space=pl.ANY)
DATAFLOW = pltpu.SideEffectType.DATAFLOW_SIDE_EFFECTING


def _ignoring_deps(body, n_in, deps):
    n_dep = len(deps)

    def wrapped(*refs):
        return body(*refs[:n_in], *refs[n_in + n_dep:])

    return wrapped


def _my_position():
    x, y, c = lax.axis_index("x"), lax.axis_index("y"), lax.axis_index("c")
    return (x, y, c)


def _peer(me, k):
    return tuple((1 - v) if (k >> s) & 1 else v for v, s in zip(me, (2, 1, 0)))


def _split_copies(src_refs, land_refs, send_sems, recv_sems, gather):
    me = _my_position()
    copies = []
    for a, (src, land) in enumerate(zip(src_refs, land_refs)):
        for k in range(1, N_DEV):
            peer = _peer(me, k)
            copies.append(pltpu.make_async_remote_copy(
                src_ref=src if gather[a] else src.at[_dev_index(peer)], dst_ref=land.at[_dev_index(me)],
                send_sem=send_sems[a].at[k - 1], recv_sem=recv_sems[a].at[k - 1],
                device_id=peer, device_id_type=MESH))
    return copies


def _own_slot(block, like_shape):
    me = _dev_index(_my_position())
    return lax.dynamic_update_index_in_dim(lax.empty(like_shape, block.dtype), block, me, 0)


def _send_start(name, srcs, gather):
    n = len(srcs)
    me = _dev_index(_my_position())
    gather = [gather] * n if isinstance(gather, bool) else list(gather)
    lands = [_own_slot(s, (N_DEV,) + s.shape) if g else
             _own_slot(lax.dynamic_index_in_dim(s, me, 0, keepdims=False), s.shape)
             for s, g in zip(srcs, gather)]

    def body(*refs):
        src_refs, land_refs = refs[:n], refs[n:2 * n]
        send_sems, recv_sems = refs[2 * n:3 * n], refs[3 * n:4 * n]
        token = refs[-1]
        for cp in _split_copies(src_refs, land_refs, send_sems, recv_sems, gather):
            cp.start()
        token[...] = jnp.zeros_like(token)

    hbm = lambda a: pltpu.HBM(a.shape, a.dtype)
    sems = [pltpu.SemaphoreType.DMA((N_DEV - 1,))] * n
    outs = pl.pallas_call(
        body, name=name,
        out_shape=(*sems, *sems, *[hbm(s) for s in srcs], *[hbm(l) for l in lands],
                   jax.ShapeDtypeStruct((SUBLANES, LANES), F32)),
        in_specs=[HBM_SPEC] * (2 * n),
        out_specs=(*[SEM_SPEC] * (2 * n), *[HBM_SPEC] * (2 * n), pl.BlockSpec(memory_space=pltpu.VMEM)),
        input_output_aliases={i: 2 * n + i for i in range(2 * n)},
        compiler_params=pltpu.CompilerParams(has_side_effects=DATAFLOW),
    )(*[pltpu.with_memory_space_constraint(a, pltpu.HBM) for a in (*srcs, *lands)])
    state = dict(send=outs[:n], recv=outs[n:2 * n], srcs=outs[2 * n:3 * n], lands=outs[3 * n:4 * n],
                 gather=gather)
    return state, outs[-1]


def _send_wait(name, state, which, after):
    n = len(which)
    pick = lambda key: [state[key][i] for i in which]
    gather = pick("gather")

    def body(*refs):
        src_refs, land_refs = refs[:n], refs[n:2 * n]
        send_sems, recv_sems = refs[2 * n:3 * n], refs[3 * n:4 * n]
        for cp in _split_copies(src_refs, land_refs, send_sems, recv_sems, gather):
            cp.wait_send()
            cp.wait_recv()

    srcs, lands = pick("srcs"), pick("lands")
    hbm = lambda a: pltpu.HBM(a.shape, a.dtype)
    outs = pl.pallas_call(
        body, name=name,
        out_shape=(*[hbm(s) for s in srcs], *[hbm(l) for l in lands]),
        in_specs=[*[HBM_SPEC] * (2 * n), *[SEM_SPEC] * (2 * n), ANY_SPEC],
        out_specs=tuple([HBM_SPEC] * (2 * n)),
        input_output_aliases={i: i for i in range(2 * n)},
        compiler_params=pltpu.CompilerParams(has_side_effects=DATAFLOW),
    )(*srcs, *lands, *pick("send"), *pick("recv"), after)
    return outs[n:]


def _two_level_copies(src_refs, land_refs, sems1, sems2):
    x, y, c = _my_position()
    me, sibling = (x, y, c), (x, y, 1 - c)
    chips = [(1 - x, y), (x, 1 - y), (1 - x, 1 - y)]
    stage1, stage2 = [], []
    for a, land in enumerate(land_refs):
        def copy(src, block, to, sems, k):
            return pltpu.make_async_remote_copy(
                src_ref=src, dst_ref=land.at[_dev_index(block)], send_sem=sems[0][a].at[k],
                recv_sem=sems[1][a].at[k], device_id=to, device_id_type=MESH)
        src = src_refs[a] if src_refs is not None else land.at[_dev_index(me)]
        stage1.append([copy(src, me, sibling, sems1, 0)] +
                      [copy(src, me, (*chip, c), sems1, 1 + j) for j, chip in enumerate(chips)])
        if sems2 is not None:
            stage2.append([copy(land.at[_dev_index((*chip, c))], (*chip, c), sibling, sems2, j)
                           for j, chip in enumerate(chips)])
    return stage1, stage2


def _gather2_start(name, shards):
    n = len(shards)
    lands = [_own_slot(s, (N_DEV,) + s.shape) for s in shards]

    def body(*refs):
        src_refs, land_refs = refs[:n], refs[n:2 * n]
        sems1 = (refs[2 * n:3 * n], refs[3 * n:4 * n])
        stage1, _ = _two_level_copies(src_refs, land_refs, sems1, None)
        for copies in stage1:
            for cp in copies:
                cp.start()
        refs[-1][...] = jnp.zeros_like(refs[-1])

    hbm = lambda a: pltpu.HBM(a.shape, a.dtype)
    sems = [pltpu.SemaphoreType.DMA((4,))] * n
    outs = pl.pallas_call(
        body, name=name,
        out_shape=(*sems, *sems, *[hbm(s) for s in shards], *[hbm(l) for l in lands],
                   jax.ShapeDtypeStruct((SUBLANES, LANES), F32)),
        in_specs=[HBM_SPEC] * (2 * n),
        out_specs=(*[SEM_SPEC] * (2 * n), *[HBM_SPEC] * (2 * n), pl.BlockSpec(memory_space=pltpu.VMEM)),
        input_output_aliases={i: 2 * n + i for i in range(2 * n)},
        compiler_params=pltpu.CompilerParams(has_side_effects=DATAFLOW),
    )(*[pltpu.with_memory_space_constraint(a, pltpu.HBM) for a in (*shards, *lands)])
    state = dict(send1=list(outs[:n]), recv1=list(outs[n:2 * n]), srcs=list(outs[2 * n:3 * n]),
                 lands=list(outs[3 * n:4 * n]), send2={}, recv2={})
    return state, outs[-1]


def _gather2_forward(name, state, which, after):
    n = len(which)
    pick = lambda key: [state[key][i] for i in which]

    def body(*refs):
        land_refs, recv1 = refs[:n], refs[n:2 * n]
        outs = refs[2 * n + len(after):]
        sems2 = (outs[:n], outs[n:2 * n])
        stage1, stage2 = _two_level_copies(None, land_refs, (recv1, recv1), sems2)
        for a in range(n):
            for j in range(3):
                stage1[a][1 + j].wait_recv()
                stage2[a][j].start()
        outs[-1][...] = jnp.zeros_like(outs[-1])

    lands = pick("lands")
    sems = [pltpu.SemaphoreType.DMA((3,))] * n
    outs = pl.pallas_call(
        body, name=name,
        out_shape=(*sems, *sems, *[pltpu.HBM(l.shape, l.dtype) for l in lands],
                   jax.ShapeDtypeStruct((SUBLANES, LANES), F32)),
        in_specs=[*[HBM_SPEC] * n, *[SEM_SPEC] * n, *[ANY_SPEC] * len(after)],
        out_specs=(*[SEM_SPEC] * (2 * n), *[HBM_SPEC] * n, pl.BlockSpec(memory_space=pltpu.VMEM)),
        input_output_aliases={i: 2 * n + i for i in range(n)},
        compiler_params=pltpu.CompilerParams(has_side_effects=DATAFLOW),
    )(*lands, *pick("recv1"), *after)
    for idx, i in enumerate(which):
        state["send2"][i], state["recv2"][i] = outs[idx], outs[n + idx]
        state["lands"][i] = outs[2 * n + idx]
    return outs[-1]


def _gather2_wait(name, state, which, after):
    n = len(which)
    pick = lambda key: [state[key][i] for i in which]

    def body(*refs):
        src_refs, land_refs = refs[:n], refs[n:2 * n]
        sems1 = (refs[2 * n:3 * n], refs[3 * n:4 * n])
        sems2 = (refs[4 * n:5 * n], refs[5 * n:6 * n])
        stage1, stage2 = _two_level_copies(src_refs, land_refs, sems1, sems2)
        for a in range(n):
            for cp in stage1[a]:
                cp.wait_send()
            stage1[a][0].wait_recv()
            for cp in stage2[a]:
                cp.wait_send()
                cp.wait_recv()

    srcs, lands = pick("srcs"), pick("lands")
    hbm = lambda a: pltpu.HBM(a.shape, a.dtype)
    outs = pl.pallas_call(
        body, name=name,
        out_shape=(*[hbm(s) for s in srcs], *[hbm(l) for l in lands]),
        in_specs=[*[HBM_SPEC] * (2 * n), *[SEM_SPEC] * (4 * n), ANY_SPEC],
        out_specs=tuple([HBM_SPEC] * (2 * n)),
        input_output_aliases={i: i for i in range(2 * n)},
        compiler_params=pltpu.CompilerParams(has_side_effects=DATAFLOW),
    )(*srcs, *lands, *pick("send1"), *pick("recv1"), *pick("send2"), *pick("recv2"), after)
    return outs[n:]


def _mm(name, a, b, *, dims, grid, a_spec, b_spec, o_spec, out_shape, acc_shape=None,
        res=None, res_spec=None):
    n_red = grid[-1] if acc_shape is not None else 1
    red_axis = len(grid) - 1

    def body(*refs):
        a_ref, b_ref = refs[0], refs[1]
        r_ref = refs[2] if res is not None else None
        o_ref = refs[3] if res is not None else refs[2]
        part = _dot(a_ref[...], b_ref[...], dims)
        if acc_shape is None:
            if r_ref is not None:
                part = part + r_ref[...]
            o_ref[...] = part.astype(o_ref.dtype)
            return
        acc_ref = refs[-1]
        k = pl.program_id(red_axis)

        @pl.when(k == 0)
        def _():
            acc_ref[...] = part

        @pl.when(k > 0)
        def _():
            acc_ref[...] += part

        @pl.when(k == n_red - 1)
        def _():
            total = acc_ref[...]
            if r_ref is not None:
                total = total + r_ref[...]
            o_ref[...] = total.astype(o_ref.dtype)

    ins, in_specs = [a, b], [a_spec, b_spec]
    if res is not None:
        ins.append(res)
        in_specs.append(res_spec)
    return pl.pallas_call(
        body, name=name, grid=grid, in_specs=in_specs, out_specs=o_spec, out_shape=out_shape,
        scratch_shapes=[pltpu.VMEM(acc_shape, F32)] if acc_shape is not None else [],
        compiler_params=_params(),
    )(*ins)


def _wide_tile(t):
    return next(c for c in WIDE_TILES if t % c == 0)


def _rms_fwd(name, h, g):
    t, d = h.shape
    tile = _wide_tile(t)

    def body(h_ref, g_ref, o_ref):
        x = h_ref[...]
        o_ref[...] = (x * _mean_sq_rsqrt(x) * g_ref[...]).astype(BF16)

    return pl.pallas_call(
        body, name=name, grid=(t // tile,),
        in_specs=[pl.BlockSpec((tile, d), lambda i: (i, 0)), pl.BlockSpec((1, d), lambda i: (0, 0))],
        out_specs=pl.BlockSpec((tile, d), lambda i: (i, 0)),
        out_shape=jax.ShapeDtypeStruct((t, d), BF16), compiler_params=_params(),
    )(h, g)


def _rms_bwd_res(name, dres, dhn, h, g, deps=()):
    t, d = h.shape

    def body(dres_ref, dhn_ref, h_ref, g_ref, dh_ref, dhb_ref, dg_ref):
        x = h_ref[...]
        dx, dgt = _rms_bwd(x, _mean_sq_rsqrt(x), g_ref[...], dhn_ref[...])
        dh = dres_ref[...] + dx
        dh_ref[...] = dh
        dhb_ref[...] = dh.astype(BF16)

        @pl.when(pl.program_id(0) == 0)
        def _():
            dg_ref[...] = jnp.zeros_like(dg_ref)

        dg_ref[...] += jnp.sum(dgt, axis=0, keepdims=True)

    tile = _wide_tile(t)
    row = pl.BlockSpec((tile, d), lambda i: (i, 0))
    vec = pl.BlockSpec((1, d), lambda i: (0, 0))
    return pl.pallas_call(
        _ignoring_deps(body, 4, deps), name=name, grid=(t // tile,),
        in_specs=[row, row, row, vec] + [ANY_SPEC] * len(deps), out_specs=[row, row, vec],
        out_shape=[jax.ShapeDtypeStruct((t, d), F32), jax.ShapeDtypeStruct((t, d), BF16),
                   jax.ShapeDtypeStruct((1, d), F32)],
        compiler_params=_params(),
    )(dres, dhn, h, g, *deps)


def _loss_bwd(h2, target, g, row_lo, row_hi):
    t, d = h2.shape
    tile = _wide_tile(t)

    def body(h_ref, tg_ref, g_ref, dh_ref, dhb_ref, loss_ref, dg_ref):
        i = pl.program_id(0)
        x = h_ref[...]
        r = _mean_sq_rsqrt(x)
        gv = g_ref[...]
        y = x * r * gv
        rows = i * tile + lax.broadcasted_iota(jnp.int32, (tile, 1), 0)
        time = (rows % N_SEG) * (t // N_SEG) + rows // N_SEG
        valid = jnp.logical_and(time >= row_lo, time < row_hi)
        err = jnp.where(valid, y - tg_ref[...], 0.0)
        dy = err * (1.0 / d)
        dx, dgt = _rms_bwd(x, r, gv, dy)
        dh_ref[...] = dx
        dhb_ref[...] = dx.astype(BF16)

        @pl.when(i == 0)
        def _():
            loss_ref[...] = jnp.zeros_like(loss_ref)
            dg_ref[...] = jnp.zeros_like(dg_ref)

        row_loss = jnp.mean(err * err, axis=-1, keepdims=True)
        loss_ref[...] += 0.5 * jnp.sum(row_loss, axis=0, keepdims=True)
        dg_ref[...] += jnp.sum(dgt, axis=0, keepdims=True)

    row = pl.BlockSpec((tile, d), lambda i: (i, 0))
    vec = pl.BlockSpec((1, d), lambda i: (0, 0))
    return pl.pallas_call(
        body, name="loss_bwd", grid=(t // tile,), in_specs=[row, row, vec],
        out_specs=[row, row, pl.BlockSpec((1, 1), lambda i: (0, 0)), vec],
        out_shape=[jax.ShapeDtypeStruct((t, d), F32), jax.ShapeDtypeStruct((t, d), BF16),
                   jax.ShapeDtypeStruct((1, 1), F32), jax.ShapeDtypeStruct((1, d), F32)],
        compiler_params=_params(),
    )(h2, target, g)


def _prev_halo(i, t):
    return jnp.where(i == 0, t // HALO_ROWS - 1, i * (ROW_TILE // HALO_ROWS) - 1)


def _next_halo(i, t):
    return jnp.where(i == t // ROW_TILE - 1, 0, (i + 1) * (ROW_TILE // HALO_ROWS))


def _causal_taps(cur, halo, first):
    rows = cur.shape[0]
    ext = jnp.concatenate([_wrap_prev_halo(halo, first), cur], axis=0)
    return ext[:rows], ext[N_SEG:N_SEG + rows]


def _anticausal_taps(cur, halo, last):
    rows = cur.shape[0]
    ext = jnp.concatenate([cur, _wrap_next_halo(halo, last)], axis=0)
    return ext[N_SEG:N_SEG + rows], ext[2 * N_SEG:2 * N_SEG + rows]


def _mix_fwd(proj, y, w_glu, conv_w, gain_c, gain_s):
    t = proj.shape[0]
    dc = conv_w.shape[1]
    ds = y.shape[1]

    def body(p_ref, halo_ref, y_ref, wg_ref, cw_ref, gc_ref, gs_ref, mixed_ref, z_ref):
        i = pl.program_id(0)
        p = p_ref[...]
        b, c, v = p[:, :dc], p[:, dc:2 * dc], p[:, 2 * dc:3 * dc]
        cv = c * v
        hp = halo_ref[...]
        x2, x1 = _causal_taps(cv, hp[:, dc:2 * dc] * hp[:, 2 * dc:3 * dc], i == 0)
        cw = cw_ref[...]
        conv = cw[0:1] * x2 + cw[1:2] * x1 + cw[2:3] * cv
        co = b * conv
        mixed_ref[:, :dc] = (co * _mean_sq_rsqrt(co) * gc_ref[...]).astype(BF16)
        g, _ = _gelu(y_ref[...])
        z = _dot(g.astype(BF16), wg_ref[...], NN)
        z_ref[...] = z
        so = g * jax.nn.sigmoid(z)
        mixed_ref[:, dc:] = (so * _mean_sq_rsqrt(so) * gs_ref[...]).astype(BF16)

    const = lambda i: (0, 0)
    return pl.pallas_call(
        body, name="mix_fwd", grid=(t // ROW_TILE,),
        in_specs=[pl.BlockSpec((ROW_TILE, 3 * dc), lambda i: (i, 0)),
                  pl.BlockSpec((HALO_ROWS, 3 * dc), lambda i: (_prev_halo(i, t), 0)),
                  pl.BlockSpec((ROW_TILE, ds), lambda i: (i, 0)),
                  pl.BlockSpec((ds, ds), const), pl.BlockSpec(conv_w.shape, const),
                  pl.BlockSpec((1, dc), const), pl.BlockSpec((1, ds), const)],
        out_specs=[pl.BlockSpec((ROW_TILE, dc + ds), lambda i: (i, 0)),
                   pl.BlockSpec((ROW_TILE, ds), lambda i: (i, 0))],
        out_shape=[jax.ShapeDtypeStruct((t, dc + ds), BF16), jax.ShapeDtypeStruct((t, ds), F32)],
        compiler_params=_params(),
    )(proj, proj, y, w_glu, conv_w, gain_c, gain_s)


def _mix_bwd1(proj, y, z, dmixed, w_glu, conv_w, gain_c, gain_s):
    t = proj.shape[0]
    dc = conv_w.shape[1]
    ds = y.shape[1]

    def body(p_ref, halo_ref, y_ref, z_ref, dm_ref, wg_ref, cw_ref, gc_ref, gs_ref,
             db_ref, dconv_ref, dy_ref, dwg_ref, dcw_ref, dgc_ref, dgs_ref):
        i = pl.program_id(0)

        @pl.when(i == 0)
        def _():
            dwg_ref[...] = jnp.zeros_like(dwg_ref)
            dcw_ref[...] = jnp.zeros_like(dcw_ref)
            dgc_ref[...] = jnp.zeros_like(dgc_ref)
            dgs_ref[...] = jnp.zeros_like(dgs_ref)

        p = p_ref[...]
        b, c, v = p[:, :dc], p[:, dc:2 * dc], p[:, 2 * dc:3 * dc]
        cv = c * v
        hp = halo_ref[...]
        x2, x1 = _causal_taps(cv, hp[:, dc:2 * dc] * hp[:, 2 * dc:3 * dc], i == 0)
        cw = cw_ref[...]
        conv = cw[0:1] * x2 + cw[1:2] * x1 + cw[2:3] * cv
        co = b * conv
        dm = dm_ref[...]
        dco, dgt = _rms_bwd(co, _mean_sq_rsqrt(co), gc_ref[...], dm[:, :dc])
        dgc_ref[...] += jnp.sum(dgt, axis=0, keepdims=True)
        db_ref[...] = (dco * conv).astype(BF16)
        dconv = dco * b
        dconv_ref[...] = dconv
        dcw_ref[0:1, :] += jnp.sum(dconv * x2, axis=0, keepdims=True)
        dcw_ref[1:2, :] += jnp.sum(dconv * x1, axis=0, keepdims=True)
        dcw_ref[2:3, :] += jnp.sum(dconv * cv, axis=0, keepdims=True)

        yv = y_ref[...]
        g, th = _gelu(yv)
        sg = jax.nn.sigmoid(z_ref[...])
        so = g * sg
        dso, dgt = _rms_bwd(so, _mean_sq_rsqrt(so), gs_ref[...], dm[:, dc:])
        dgs_ref[...] += jnp.sum(dgt, axis=0, keepdims=True)
        dz = (dso * g * sg * (1.0 - sg)).astype(BF16)
        dg = dso * sg + _dot(dz, wg_ref[...], NT)
        dwg_ref[...] += _dot(g.astype(BF16), dz, TN)
        dy_ref[...] = dg * _gelu_grad(yv, th)

    const = lambda i: (0, 0)
    row = lambda w: pl.BlockSpec((ROW_TILE, w), lambda i: (i, 0))
    return pl.pallas_call(
        body, name="mix_bwd1", grid=(t // ROW_TILE,),
        in_specs=[row(3 * dc), pl.BlockSpec((HALO_ROWS, 3 * dc), lambda i: (_prev_halo(i, t), 0)),
                  row(ds), row(ds), row(dc + ds), pl.BlockSpec((ds, ds), const),
                  pl.BlockSpec(conv_w.shape, const), pl.BlockSpec((1, dc), const),
                  pl.BlockSpec((1, ds), const)],
        out_specs=[row(dc), row(dc), row(ds), pl.BlockSpec((ds, ds), const),
                   pl.BlockSpec(conv_w.shape, const), pl.BlockSpec((1, dc), const),
                   pl.BlockSpec((1, ds), const)],
        out_shape=[jax.ShapeDtypeStruct((t, dc), BF16), jax.ShapeDtypeStruct((t, dc), F32),
                   jax.ShapeDtypeStruct((t, ds), F32), jax.ShapeDtypeStruct((ds, ds), F32),
                   jax.ShapeDtypeStruct(conv_w.shape, F32), jax.ShapeDtypeStruct((1, dc), F32),
                   jax.ShapeDtypeStruct((1, ds), F32)],
        compiler_params=_params(),
    )(proj, proj, y, z, dmixed, w_glu, conv_w, gain_c, gain_s)


def _mix_bwd2(proj, dconv, conv_w, deps=()):
    t = proj.shape[0]
    dc = conv_w.shape[1]
    n_tiles = t // ROW_TILE

    def body(c_ref, v_ref, d_ref, halo_ref, cw_ref, o_ref):
        i = pl.program_id(0)
        d = d_ref[...]
        u1, u2 = _anticausal_taps(d, halo_ref[...], i == n_tiles - 1)
        cw = cw_ref[...]
        dcv = cw[2:3] * d + cw[1:2] * u1 + cw[0:1] * u2
        o_ref[:, :dc] = (dcv * v_ref[...]).astype(BF16)
        o_ref[:, dc:] = (dcv * c_ref[...]).astype(BF16)

    return pl.pallas_call(
        _ignoring_deps(body, 5, deps), name="mix_bwd2", grid=(n_tiles,),
        in_specs=[pl.BlockSpec((ROW_TILE, dc), lambda i: (i, 1)),
                  pl.BlockSpec((ROW_TILE, dc), lambda i: (i, 2)),
                  pl.BlockSpec((ROW_TILE, dc), lambda i: (i, 0)),
                  pl.BlockSpec((HALO_ROWS, dc), lambda i: (_next_halo(i, t), 0)),
                  pl.BlockSpec(conv_w.shape, lambda i: (0, 0))] + [ANY_SPEC] * len(deps),
        out_specs=pl.BlockSpec((ROW_TILE, 2 * dc), lambda i: (i, 0)),
        out_shape=jax.ShapeDtypeStruct((t, 2 * dc), BF16), compiler_params=_params(),
    )(proj, proj, dconv, dconv, conv_w, *deps)


def _conv_taps(up_ref, prev, nxt, fw_ref, fb_ref, c0, r, rows=ROW_CHUNK):
    cols = pl.ds(c0, FFN_COLS)
    get = lambda s, n: up_ref[pl.ds(s, n), cols]
    x0, x1, x2 = (_tile_rows(get, prev, nxt, r - k * N_SEG, rows) for k in range(3))
    w = fw_ref[:, cols]
    return w[0:1] * x2 + w[1:2] * x1 + w[2:3] * x0 + fb_ref[:, cols], x2, x1, x0


def _gated_fwd(up_pre, fw, fb):
    t, f2 = up_pre.shape
    f = f2 // 2

    def body(up_ref, halo_ref, fw_ref, fb_ref, act_ref):
        first = pl.program_id(0) == 0
        for c0 in range(0, f, FFN_COLS):
            halo_a = _wrap_prev_halo(halo_ref[:, pl.ds(c0, FFN_COLS)], first)
            halo_v = _wrap_prev_halo(halo_ref[:, pl.ds(f + c0, FFN_COLS)], first)
            for r in range(0, ROW_TILE, ROW_CHUNK):
                a = _conv_taps(up_ref, halo_a, None, fw_ref, fb_ref, c0, r)[0]
                val = _conv_taps(up_ref, halo_v, None, fw_ref, fb_ref, f + c0, r)[0]
                act_ref[pl.ds(r, ROW_CHUNK), pl.ds(c0, FFN_COLS)] = (a * jax.nn.sigmoid(a) * val).astype(BF16)

    const = lambda a: pl.BlockSpec(a.shape, lambda i: (0, 0))
    return pl.pallas_call(
        body, name="ffn_fwd", grid=(t // ROW_TILE,),
        in_specs=[pl.BlockSpec((ROW_TILE, f2), lambda i: (i, 0)),
                  pl.BlockSpec((HALO_ROWS, f2), lambda i: (_prev_halo(i, t), 0)), const(fw), const(fb)],
        out_specs=pl.BlockSpec((ROW_TILE, f), lambda i: (i, 0)),
        out_shape=jax.ShapeDtypeStruct((t, f), BF16), compiler_params=_params(),
    )(up_pre, up_pre, fw, fb)


def _gated_bwd(up_pre, dact, fw, fb, deps=()):
    t, f2 = up_pre.shape
    f = f2 // 2
    n_tiles = t // ROW_TILE
    chunks = [(r, ROW_CHUNK) for r in range(0, ROW_TILE, ROW_CHUNK)] + [(ROW_TILE, HALO_ROWS)]

    def body(up_ref, prev_ref, next_ref, dact_ref, dact_next_ref, fw_ref, fb_ref,
             out_ref, dfw_ref, dfb_ref, dup_ref):
        first = pl.program_id(0) == 0
        last = pl.program_id(0) == n_tiles - 1

        @pl.when(first)
        def _():
            dfw_ref[...] = jnp.zeros_like(dfw_ref)
            dfb_ref[...] = jnp.zeros_like(dfb_ref)

        fold = lambda v: sum(v[s:s + SUBLANES] for s in range(0, ROW_CHUNK, SUBLANES))
        for c0 in range(0, f, FFN_COLS):
            starts = (c0, f + c0)
            prev = [_wrap_prev_halo(prev_ref[:, pl.ds(c, FFN_COLS)], first) for c in starts]
            nxt = [_wrap_next_halo(next_ref[:, pl.ds(c, FFN_COLS)], last) for c in starts]
            dact_next = _wrap_next_halo(dact_next_ref[:, pl.ds(c0, FFN_COLS)].astype(F32), last)
            get_dact = lambda s, n: dact_ref[pl.ds(s, n), pl.ds(c0, FFN_COLS)].astype(F32)
            sums = [[jnp.zeros((SUBLANES, FFN_COLS), F32)] * 4 for _ in starts]
            for r, rows in chunks:
                taps = [_conv_taps(up_ref, prev[k], nxt[k], fw_ref, fb_ref, c, r, rows)
                        for k, c in enumerate(starts)]
                a, val = taps[0][0], taps[1][0]
                da_ct = _tile_rows(get_dact, None, dact_next, r, rows)
                sg = jax.nn.sigmoid(a)
                dup = (da_ct * val * sg * (1.0 + a * (1.0 - sg)), da_ct * a * sg)
                for k in range(2):
                    dup_ref[k, pl.ds(r, rows), :] = dup[k]
                    if r < ROW_TILE:
                        terms = (dup[k], dup[k] * taps[k][1], dup[k] * taps[k][2], dup[k] * taps[k][3])
                        sums[k] = [s + fold(v) for s, v in zip(sums[k], terms)]
            for k, c in enumerate(starts):
                cols = pl.ds(c, FFN_COLS)
                s_b, s_w0, s_w1, s_w2 = (jnp.sum(p, axis=0, keepdims=True) for p in sums[k])
                dfb_ref[:, cols] += s_b
                for tap, s_w in enumerate((s_w0, s_w1, s_w2)):
                    dfw_ref[tap:tap + 1, cols] += s_w
                cw = fw_ref[:, cols]
                for r in range(0, ROW_TILE, ROW_CHUNK):
                    d, u1, u2 = (dup_ref[k, pl.ds(r + s * N_SEG, ROW_CHUNK), :] for s in range(3))
                    out_ref[pl.ds(r, ROW_CHUNK), cols] = (cw[2:3] * d + cw[1:2] * u1 + cw[0:1] * u2).astype(BF16)

    tile = lambda w: pl.BlockSpec((ROW_TILE, w), lambda i: (i, 0))
    halo = lambda w, index: pl.BlockSpec((HALO_ROWS, w), lambda i: (index(i, t), 0))
    const = lambda a: pl.BlockSpec(a.shape, lambda i: (0, 0))
    return pl.pallas_call(
        _ignoring_deps(body, 7, deps), name="ffn_bwd", grid=(n_tiles,),
        in_specs=[tile(f2), halo(f2, _prev_halo), halo(f2, _next_halo), tile(f), halo(f, _next_halo),
                  const(fw), const(fb)] + [ANY_SPEC] * len(deps),
        out_specs=[tile(f2), const(fw), const(fb)],
        out_shape=[jax.ShapeDtypeStruct((t, f2), BF16), jax.ShapeDtypeStruct(fw.shape, F32),
                   jax.ShapeDtypeStruct(fb.shape, F32)],
        scratch_shapes=[pltpu.VMEM((2, ROW_TILE + HALO_ROWS, FFN_COLS), F32)],
        compiler_params=_params(),
    )(up_pre, up_pre, up_pre, dact, dact, fw, fb, *deps)


def _to_segments(a):
    t, c = a.shape
    return a.reshape(N_SEG, t // N_SEG, c).transpose(1, 0, 2).reshape(t, c)


def _from_segments(a):
    t, c = a.shape
    return a.reshape(t // N_SEG, N_SEG, c).transpose(1, 0, 2).reshape(t, c)


def _cmul(ar, ai, br, bi):
    return ar * br - ai * bi, ar * bi + ai * br


def _cpow(ar, ai, n):
    out = None
    while n:
        if n & 1:
            out = (ar, ai) if out is None else _cmul(out[0], out[1], ar, ai)
        ar, ai = _cmul(ar, ai, ar, ai)
        n >>= 1
    return out


def _segment_carries(pr, pi, fr, fi, forward):
    row = lax.broadcasted_iota(jnp.int32, fr.shape, 0)
    edge = row == (0 if forward else N_SEG - 1)
    shift = 1 if forward else N_SEG - 1
    sr, si = jnp.zeros_like(fr), jnp.zeros_like(fi)
    for _ in range(N_SEG - 1):
        tr, ti = _cmul(pr, pi, sr, si)
        sr = jnp.where(edge, 0.0, pltpu.roll(tr + fr, shift, 0))
        si = jnp.where(edge, 0.0, pltpu.roll(ti + fi, shift, 0))
    return sr, si


def _rows(i):
    return pl.ds(pl.multiple_of(i * SUBLANES, SUBLANES), SUBLANES)


def _s5_fwd(proj, u_col, bb_re, bb_im, a_re, a_im, cc_re, cc_im, d_skip):
    t = proj.shape[0]
    nch, _, cs = bb_re.shape
    ds = nch * SSM_CHUNK
    u_blk = u_col // SSM_CHUNK
    steps = t // N_SEG

    def body(u_ref, bbr_ref, bbi_ref, ar_ref, ai_ref, ccr_ref, cci_ref, d_ref, sr_ref, si_ref, y_ref):
        ub = u_ref[...].astype(BF16)
        sr_ref[...] = _dot(ub, bbr_ref[...], NN)
        si_ref[...] = _dot(ub, bbi_ref[...], NN)
        ar = jnp.broadcast_to(ar_ref[...], (N_SEG, cs))
        ai = jnp.broadcast_to(ai_ref[...], (N_SEG, cs))
        zero = jnp.zeros((N_SEG, cs), F32)

        def totals(i, carry):
            tr, ti = _cmul(ar, ai, *carry)
            return tr + sr_ref[_rows(i), :], ti + si_ref[_rows(i), :]

        fr, fi = lax.fori_loop(0, steps, totals, (zero, zero))
        s0r, s0i = _segment_carries(*_cpow(ar, ai, steps), fr, fi, True)

        def scan(i, carry):
            tr, ti = _cmul(ar, ai, *carry)
            nr, ni = tr + sr_ref[_rows(i), :], ti + si_ref[_rows(i), :]
            sr_ref[_rows(i), :] = nr
            si_ref[_rows(i), :] = ni
            return nr, ni

        lax.fori_loop(0, steps, scan, (s0r, s0i))
        y_ref[...] = (_dot(sr_ref[...].astype(BF16), ccr_ref[...], NN)
                      - _dot(si_ref[...].astype(BF16), cci_ref[...], NN)
                      + d_ref[...] * u_ref[...])

    chunk3 = lambda r, c: pl.BlockSpec((None, r, c), lambda j: (j, 0, 0))
    return pl.pallas_call(
        body, name="s5_fwd", grid=(nch,),
        in_specs=[pl.BlockSpec((t, SSM_CHUNK), lambda j: (0, j + u_blk)),
                  chunk3(SSM_CHUNK, cs), chunk3(SSM_CHUNK, cs), chunk3(1, cs), chunk3(1, cs),
                  chunk3(cs, SSM_CHUNK), chunk3(cs, SSM_CHUNK), chunk3(1, SSM_CHUNK)],
        out_specs=[pl.BlockSpec((t, cs), lambda j: (0, j)), pl.BlockSpec((t, cs), lambda j: (0, j)),
                   pl.BlockSpec((t, SSM_CHUNK), lambda j: (0, j))],
        out_shape=[jax.ShapeDtypeStruct((t, nch * cs), F32), jax.ShapeDtypeStruct((t, nch * cs), F32),
                   jax.ShapeDtypeStruct((t, ds), F32)],
        compiler_params=_params(),
    )(proj, bb_re, bb_im, a_re, a_im, cc_re, cc_im, d_skip)


def _s5_bwd(dy, proj, u_col, s_re, s_im, bb_re, bb_im, a_re, a_im, cc_re, cc_im, d_skip):
    t, ds = dy.shape
    nch, _, cs = bb_re.shape
    u_blk = u_col // SSM_CHUNK
    steps = t // N_SEG

    def body(dy_ref, u_ref, sr_ref, si_ref, bbr_ref, bbi_ref, ar_ref, ai_ref, ccr_ref, cci_ref, d_ref,
             du_ref, dbbr_ref, dbbi_ref, dar_ref, dai_ref, dccr_ref, dcci_ref, dd_ref, gr_ref, gi_ref):
        dyv = dy_ref[...]
        dyb = dyv.astype(BF16)
        gr_ref[...] = _dot(dyb, ccr_ref[...], NT)
        gi_ref[...] = -_dot(dyb, cci_ref[...], NT)
        ar = jnp.broadcast_to(ar_ref[...], (N_SEG, cs))
        ai = -jnp.broadcast_to(ai_ref[...], (N_SEG, cs))
        zero = jnp.zeros((N_SEG, cs), F32)

        def totals(k, carry):
            i = steps - 1 - k
            tr, ti = _cmul(ar, ai, *carry)
            return tr + gr_ref[_rows(i), :], ti + gi_ref[_rows(i), :]

        fr, fi = lax.fori_loop(0, steps, totals, (zero, zero))
        e0r, e0i = _segment_carries(*_cpow(ar, ai, steps), fr, fi, False)

        def step(i, gr, gi, pr, pi, acc_r, acc_i):
            tr, ti = _cmul(ar, ai, gr, gi)
            nr, ni = tr + gr_ref[_rows(i), :], ti + gi_ref[_rows(i), :]
            gr_ref[_rows(i), :] = nr
            gi_ref[_rows(i), :] = ni
            return nr, ni, acc_r + nr * pr + ni * pi, acc_i + ni * pr - nr * pi

        def scan(k, carry):
            i = steps - 1 - k
            gr, gi, acc_r, acc_i = carry
            return step(i, gr, gi, sr_ref[_rows(i - 1), :], si_ref[_rows(i - 1), :], acc_r, acc_i)

        gr, gi, acc_r, acc_i = lax.fori_loop(0, steps - 1, scan, (e0r, e0i, zero, zero))
        row = lax.broadcasted_iota(jnp.int32, (N_SEG, cs), 0)
        last = _rows(steps - 1)
        pr = jnp.where(row == 0, 0.0, pltpu.roll(sr_ref[last, :], 1, 0))
        pi = jnp.where(row == 0, 0.0, pltpu.roll(si_ref[last, :], 1, 0))
        _, _, acc_r, acc_i = step(0, gr, gi, pr, pi, acc_r, acc_i)
        dar_ref[...] = jnp.sum(acc_r, axis=0, keepdims=True)
        dai_ref[...] = jnp.sum(acc_i, axis=0, keepdims=True)

        uv = u_ref[...]
        ub = uv.astype(BF16)
        grb = gr_ref[...].astype(BF16)
        gib = gi_ref[...].astype(BF16)
        du = d_ref[...] * dyv + _dot(grb, bbr_ref[...], NT) + _dot(gib, bbi_ref[...], NT)
        du_ref[...] = du.astype(BF16)
        dbbr_ref[...] = _dot(ub, grb, TN)
        dbbi_ref[...] = _dot(ub, gib, TN)
        dccr_ref[...] = _dot(sr_ref[...].astype(BF16), dyb, TN)
        dcci_ref[...] = -_dot(si_ref[...].astype(BF16), dyb, TN)
        dd_ref[...] = jnp.sum(dyv * uv, axis=0, keepdims=True)

    chunk3 = lambda r, c: pl.BlockSpec((None, r, c), lambda j: (j, 0, 0))
    cols = lambda w: pl.BlockSpec((t, w), lambda j: (0, j))
    return pl.pallas_call(
        body, name="s5_bwd", grid=(nch,),
        in_specs=[cols(SSM_CHUNK), pl.BlockSpec((t, SSM_CHUNK), lambda j: (0, j + u_blk)), cols(cs), cols(cs),
                  chunk3(SSM_CHUNK, cs), chunk3(SSM_CHUNK, cs), chunk3(1, cs), chunk3(1, cs),
                  chunk3(cs, SSM_CHUNK), chunk3(cs, SSM_CHUNK), chunk3(1, SSM_CHUNK)],
        out_specs=[cols(SSM_CHUNK), chunk3(SSM_CHUNK, cs), chunk3(SSM_CHUNK, cs), chunk3(1, cs),
                   chunk3(1, cs), chunk3(cs, SSM_CHUNK), chunk3(cs, SSM_CHUNK), chunk3(1, SSM_CHUNK)],
        out_shape=[jax.ShapeDtypeStruct((t, ds), BF16),
                   jax.ShapeDtypeStruct((nch, SSM_CHUNK, cs), F32), jax.ShapeDtypeStruct((nch, SSM_CHUNK, cs), F32),
                   jax.ShapeDtypeStruct((nch, 1, cs), F32), jax.ShapeDtypeStruct((nch, 1, cs), F32),
                   jax.ShapeDtypeStruct((nch, cs, SSM_CHUNK), F32), jax.ShapeDtypeStruct((nch, cs, SSM_CHUNK), F32),
                   jax.ShapeDtypeStruct((nch, 1, SSM_CHUNK), F32)],
        scratch_shapes=[pltpu.VMEM((t, cs), F32), pltpu.VMEM((t, cs), F32)],
        compiler_params=_params(),
    )(dy, proj, s_re, s_im, bb_re, bb_im, a_re, a_im, cc_re, cc_im, d_skip)


def _discretize(lr, li, log_dt, br, bi):
    dt = jnp.exp(log_dt)
    mag = jnp.exp(lr * dt)
    ang = li * dt
    a_re = mag * jnp.cos(ang)
    a_im = mag * jnp.sin(ang)
    den = lr * lr + li * li
    nr = a_re - 1.0
    f_re = (nr * lr + a_im * li) / den
    f_im = (a_im * lr - nr * li) / den
    return a_re, a_im, f_re * br - f_im * bi, f_re * bi + f_im * br


def _whole(shape):
    return pl.BlockSpec(shape, lambda: (0,) * len(shape))


def _disc_fwd(lr, li, log_dt, br, bi, deps=()):
    def body(lr_ref, li_ref, dt_ref, br_ref, bi_ref, ar_ref, ai_ref, bbr_ref, bbi_ref):
        outs = _discretize(lr_ref[...], li_ref[...], dt_ref[...], br_ref[...], bi_ref[...])
        for ref, val in zip((ar_ref, ai_ref, bbr_ref, bbi_ref), outs):
            ref[...] = val

    args = (lr, li, log_dt, br, bi)
    outs = (lr, lr, br, br)
    return pl.pallas_call(
        _ignoring_deps(body, 5, deps), name="disc_fwd",
        in_specs=[_whole(a.shape) for a in args] + [ANY_SPEC] * len(deps),
        out_specs=[_whole(a.shape) for a in outs],
        out_shape=[jax.ShapeDtypeStruct(a.shape, F32) for a in outs],
    )(*args, *deps)


def _disc_bwd(lr, li, log_dt, br, bi, dar, dai, dbbr, dbbi):
    def body(lr_ref, li_ref, dt_ref, br_ref, bi_ref, dar_ref, dai_ref, dbbr_ref, dbbi_ref,
             dlr_ref, dli_ref, ddt_ref, dbr_ref, dbi_ref):
        _, vjp = jax.vjp(_discretize, lr_ref[...], li_ref[...], dt_ref[...], br_ref[...], bi_ref[...])
        grads = vjp((dar_ref[...], dai_ref[...], dbbr_ref[...], dbbi_ref[...]))
        for ref, val in zip((dlr_ref, dli_ref, ddt_ref, dbr_ref, dbi_ref), grads):
            ref[...] = val

    args = (lr, li, log_dt, br, bi, dar, dai, dbbr, dbbi)
    outs = (lr, li, log_dt, br, bi)
    return pl.pallas_call(
        body, name="disc_bwd", in_specs=[_whole(a.shape) for a in args],
        out_specs=[_whole(a.shape) for a in outs],
        out_shape=[jax.ShapeDtypeStruct(a.shape, F32) for a in outs],
    )(*args)


def _adamw(w, g, m, v):
    m = ADAM_B1 * m + (1.0 - ADAM_B1) * g
    v = ADAM_B2 * v + (1.0 - ADAM_B2) * (g * g)
    m_hat = m / ADAM_BC1
    v_hat = v / ADAM_BC2
    delta = -ADAM_LR * (m_hat / (jnp.sqrt(v_hat) + ADAM_EPS) + ADAM_WD * w)
    return delta, m, v


def _adamw_reduce(name, parts, w, m, v):
    _, r, c = parts.shape
    tr = r
    for cand in (256, 176, 128):
        if r % cand == 0:
            tr = cand
            break

    def body(p_ref, w_ref, m_ref, v_ref, g_ref, d_ref, nm_ref, nv_ref):
        g = p_ref[0].astype(F32)
        for k in range(1, N_DEV):
            g = g + p_ref[k].astype(F32)
        delta, nm, nv = _adamw(w_ref[...], g, m_ref[...], v_ref[...])
        g_ref[...] = g
        d_ref[...] = delta
        nm_ref[...] = nm
        nv_ref[...] = nv

    blk = pl.BlockSpec((tr, c), lambda i: (i, 0))
    return pl.pallas_call(
        body, name=name, grid=(r // tr,),
        in_specs=[pl.BlockSpec((N_DEV, tr, c), lambda i: (0, i, 0)), blk, blk, blk],
        out_specs=[blk] * 4, out_shape=[jax.ShapeDtypeStruct((r, c), F32)] * 4,
        compiler_params=_params(),
    )(parts, w, m, v)


def _sum_parts(name, parts):
    _, r, c = parts.shape

    def body(p_ref, o_ref):
        g = p_ref[0]
        for k in range(1, N_DEV):
            g = g + p_ref[k]
        o_ref[...] = g

    return pl.pallas_call(
        body, name=name, in_specs=[_whole(parts.shape)], out_specs=_whole((r, c)),
        out_shape=jax.ShapeDtypeStruct((r, c), F32), compiler_params=_params(),
    )(parts)


def _adamw_many(name, grads, ws, ms, vs):
    n = len(grads)

    def body(*refs):
        ins, outs = refs[:4 * n], refs[4 * n:]
        for i in range(n):
            g, w, m, v = (ins[j * n + i][...] for j in range(4))
            for ref, val in zip((outs[i], outs[n + i], outs[2 * n + i]), _adamw(w, g, m, v)):
                ref[...] = val

    args = (*grads, *ws, *ms, *vs)
    outs = pl.pallas_call(
        body, name=name, in_specs=[_whole(a.shape) for a in args],
        out_specs=[_whole(a.shape) for a in ws] * 3,
        out_shape=[jax.ShapeDtypeStruct(a.shape, F32) for a in ws] * 3, compiler_params=_params(),
    )(*args)
    return outs[:n], outs[n:2 * n], outs[2 * n:]


def _pack(arrays, rows):
    flat = jnp.concatenate([a.reshape(-1) for a in arrays])
    return jnp.pad(flat, (0, rows * LANES - flat.shape[0])).reshape(rows, LANES)


def _unpack(packed, shapes):
    flat = packed.reshape(-1)
    out, off = [], 0
    for s in shapes:
        n = math.prod(s)
        out.append(flat[off:off + n].reshape(s))
        off += n
    return out


def _packed_rows(shapes):
    n = sum(math.prod(s) for s in shapes)
    return -(-n // (SUBLANES * LANES)) * SUBLANES


def _block_diag(x):
    j, g, r, c = x.shape
    eye = jnp.eye(g, dtype=x.dtype)
    return (x[:, :, :, None, :] * eye[None, :, None, :, None]).reshape(j, g * r, g * c)


def _diag_blocks(x, g):
    j, gr, gc = x.shape
    r, c = gr // g, gc // g
    eye = jnp.eye(g, dtype=x.dtype)
    return (x.reshape(j, g, r, g, c) * eye[None, :, None, :, None]).sum(axis=3)


def kernel(x, meta_tokens, norm_mix_g, w_in, conv_w, ssm_lam_re, ssm_lam_im, ssm_log_dt, ssm_b_re, ssm_b_im, ssm_c_re, ssm_c_im, ssm_d, ssm_w_glu, gain_conv_out, gain_ssm_out, w_out, norm_ffn_g, w_up, ffn_conv_w, ffn_conv_b, w_down, norm_final_g, loss_target, m_meta_tokens, m_norm_mix_g, m_w_in, m_conv_w, m_ssm_lam_re, m_ssm_lam_im, m_ssm_log_dt, m_ssm_b_re, m_ssm_b_im, m_ssm_c_re, m_ssm_c_im, m_ssm_d, m_ssm_w_glu, m_gain_conv_out, m_gain_ssm_out, m_w_out, m_norm_ffn_g, m_w_up, m_ffn_conv_w, m_ffn_conv_b, m_w_down, m_norm_final_g, v_meta_tokens, v_norm_mix_g, v_w_in, v_conv_w, v_ssm_lam_re, v_ssm_lam_im, v_ssm_log_dt, v_ssm_b_re, v_ssm_b_im, v_ssm_c_re, v_ssm_c_im, v_ssm_d, v_ssm_w_glu, v_gain_conv_out, v_gain_ssm_out, v_w_out, v_norm_ffn_g, v_w_up, v_ffn_conv_w, v_ffn_conv_b, v_w_down, v_norm_final_g):
    weights = dict(meta_tokens=meta_tokens, norm_mix_g=norm_mix_g, w_in=w_in, conv_w=conv_w, ssm_lam_re=ssm_lam_re, ssm_lam_im=ssm_lam_im, ssm_log_dt=ssm_log_dt, ssm_b_re=ssm_b_re, ssm_b_im=ssm_b_im, ssm_c_re=ssm_c_re, ssm_c_im=ssm_c_im, ssm_d=ssm_d, ssm_w_glu=ssm_w_glu, gain_conv_out=gain_conv_out, gain_ssm_out=gain_ssm_out, w_out=w_out, norm_ffn_g=norm_ffn_g, w_up=w_up, ffn_conv_w=ffn_conv_w, ffn_conv_b=ffn_conv_b, w_down=w_down, norm_final_g=norm_final_g)
    mom_m = dict(meta_tokens=m_meta_tokens, norm_mix_g=m_norm_mix_g, w_in=m_w_in, conv_w=m_conv_w, ssm_lam_re=m_ssm_lam_re, ssm_lam_im=m_ssm_lam_im, ssm_log_dt=m_ssm_log_dt, ssm_b_re=m_ssm_b_re, ssm_b_im=m_ssm_b_im, ssm_c_re=m_ssm_c_re, ssm_c_im=m_ssm_c_im, ssm_d=m_ssm_d, ssm_w_glu=m_ssm_w_glu, gain_conv_out=m_gain_conv_out, gain_ssm_out=m_gain_ssm_out, w_out=m_w_out, norm_ffn_g=m_norm_ffn_g, w_up=m_w_up, ffn_conv_w=m_ffn_conv_w, ffn_conv_b=m_ffn_conv_b, w_down=m_w_down, norm_final_g=m_norm_final_g)
    mom_v = dict(meta_tokens=v_meta_tokens, norm_mix_g=v_norm_mix_g, w_in=v_w_in, conv_w=v_conv_w, ssm_lam_re=v_ssm_lam_re, ssm_lam_im=v_ssm_lam_im, ssm_log_dt=v_ssm_log_dt, ssm_b_re=v_ssm_b_re, ssm_b_im=v_ssm_b_im, ssm_c_re=v_ssm_c_re, ssm_c_im=v_ssm_c_im, ssm_d=v_ssm_d, ssm_w_glu=v_ssm_w_glu, gain_conv_out=v_gain_conv_out, gain_ssm_out=v_gain_ssm_out, w_out=v_w_out, norm_ffn_g=v_norm_ffn_g, w_up=v_w_up, ffn_conv_w=v_ffn_conv_w, ffn_conv_b=v_ffn_conv_b, w_down=v_w_down, norm_final_g=v_norm_final_g)
    names = list(weights)

    n_meta, d_meta = meta_tokens.shape
    seq, d = x.shape[1], x.shape[2]
    rows_used = n_meta + seq
    t = -(-rows_used // ROW_TILE) * ROW_TILE
    d_in_s = w_in.shape[2]
    dc_s = conv_w.shape[2]
    dc = dc_s * N_DEV
    ds = ssm_w_glu.shape[2]
    n_groups, n_state, grp = ssm_b_re.shape[1:]
    ns = n_groups * n_state
    nch = ds // SSM_CHUNK
    gpc = n_groups // nch
    ff_s = w_up.shape[2]
    dn_s = w_down.shape[1]
    assert 3 * dc + ds == d_in_s * N_DEV and 2 * dn_s == ff_s and t % (N_SEG * SUBLANES) == 0

    small_shard = jnp.concatenate([meta_tokens.reshape(-1), conv_w.reshape(-1), ffn_conv_w.reshape(-1)])
    n_small = small_shard.shape[0]
    small_rows = -(-n_small // LANES)
    small_shard = jnp.pad(small_shard, (0, small_rows * LANES - n_small)).reshape(small_rows, LANES)
    ag, ag_token = _gather2_start("gather_weights_start", [
        small_shard, w_in[0].astype(BF16), ssm_w_glu[0].astype(BF16), w_out[0].astype(BF16),
        jnp.swapaxes(w_up[0], 0, 1).astype(BF16), w_down[0].astype(BF16)])
    fb = ffn_conv_b

    gh = n_groups * grp
    per_h = lambda a: jnp.broadcast_to(a.reshape(n_groups, 1, -1), (n_groups, grp, n_state)).reshape(gh, n_state)
    ghp = lambda a: a.transpose(0, 1, 3, 2).reshape(gh, n_state)
    lr, li, log_dt_e = per_h(ssm_lam_re), per_h(ssm_lam_im), per_h(ssm_log_dt)
    br, bi = ghp(ssm_b_re), ghp(ssm_b_im)
    a_re, a_im, bb_re, bb_im = _disc_fwd(lr, li, log_dt_e, br, bi, deps=(ag_token,))
    cs = gpc * n_state
    chunk_row = lambda a: a.reshape(n_groups, grp, n_state)[:, 0].reshape(nch, 1, cs)
    to_bb = lambda a: _block_diag(a.reshape(nch, gpc, grp, n_state)).astype(BF16)
    to_cc = lambda a: _block_diag(a.reshape(nch, gpc, grp, n_state).transpose(0, 1, 3, 2)).astype(BF16)
    bbm_re, bbm_im = to_bb(bb_re), to_bb(bb_im)
    ccm_re, ccm_im = to_cc(ssm_c_re), to_cc(ssm_c_im)
    a_re_c, a_im_c = chunk_row(a_re), chunk_row(a_im)
    d_skip = ssm_d.reshape(nch, 1, SSM_CHUNK)

    target = _to_segments(jnp.pad(loss_target[0], ((n_meta, t - rows_used), (0, 0))))
    token = _gather2_forward("gather_weights_forward_small", ag, (0,), (bbm_im, ccm_im, target))
    (g_small,) = _gather2_wait("gather_weights_wait_small", ag, (0,), token)
    g_small = g_small.reshape(N_DEV, -1)
    o1 = n_meta * d_meta
    o2 = o1 + 3 * dc_s
    meta_full = g_small[:, :o1].reshape(N_DEV, n_meta, d_meta).transpose(1, 0, 2).reshape(n_meta, d)
    conv_w_f = g_small[:, o1:o2].reshape(N_DEV, 3, dc_s).transpose(1, 0, 2).reshape(3, dc)
    fw = g_small[:, o2:o2 + 3 * ff_s].reshape(N_DEV, 3, ff_s).transpose(1, 0, 2).reshape(3, N_DEV * ff_s)
    h0 = _to_segments(jnp.concatenate([meta_full, x[0], jnp.zeros((t - rows_used, d), F32)], axis=0))
    full_t = lambda w: pl.BlockSpec((t, w), lambda *_: (0, 0))

    hn1 = _rms_fwd("norm_mix", h0, norm_mix_g)
    token = _gather2_forward("gather_weights_forward_in", ag, (1,), (hn1,))
    (g_in,) = _gather2_wait("gather_weights_wait_in", ag, (1,), token)
    proj = _mm("proj", hn1, g_in, dims=NN, grid=(N_DEV,), a_spec=full_t(d),
               b_spec=pl.BlockSpec((None, d, d_in_s), lambda j: (j, 0, 0)),
               o_spec=pl.BlockSpec((t, d_in_s), lambda j: (0, j)),
               out_shape=jax.ShapeDtypeStruct((t, N_DEV * d_in_s), F32))
    s_re, s_im, y_ssm = _s5_fwd(proj, 3 * dc, bbm_re, bbm_im, a_re_c, a_im_c, ccm_re, ccm_im, d_skip)
    token = _gather2_forward("gather_weights_forward_up", ag, (2, 3, 4), (y_ssm,))
    g_glu, g_out = _gather2_wait("gather_weights_wait_mix", ag, (2, 3), token)
    w_out_f = g_out.reshape(-1, d)
    w_glu_f = g_glu.reshape(ds, ds)
    mixed, z_glu = _mix_fwd(proj, y_ssm, w_glu_f, conv_w_f, gain_conv_out, gain_ssm_out)
    tn_out = 256
    h1 = _mm("out_proj", mixed, w_out_f, dims=NN, grid=(d // tn_out,), a_spec=full_t(dc + ds),
             b_spec=pl.BlockSpec((dc + ds, tn_out), lambda j: (0, j)),
             o_spec=pl.BlockSpec((t, tn_out), lambda j: (0, j)),
             out_shape=jax.ShapeDtypeStruct((t, d), F32),
             res=h0, res_spec=pl.BlockSpec((t, tn_out), lambda j: (0, j)))
    hn2 = _rms_fwd("norm_ffn", h1, norm_ffn_g)
    token = _gather2_forward("gather_weights_forward_down", ag, (5,), (hn2,))
    (g_up,) = _gather2_wait("gather_weights_wait_up", ag, (4,), token)
    f2 = N_DEV * ff_s
    pair = 2 * ff_s
    w_up_t = g_up.reshape(f2, d)
    up_pre = _mm("up_proj", hn2, w_up_t, dims=NT, grid=(f2 // pair,), a_spec=full_t(d),
                 b_spec=pl.BlockSpec((pair, d), lambda j: (j, 0)),
                 o_spec=pl.BlockSpec((t, pair), lambda j: (0, j)),
                 out_shape=jax.ShapeDtypeStruct((t, f2), F32))
    act = _gated_fwd(up_pre, fw, fb)
    (g_down,) = _gather2_wait("gather_weights_wait_down", ag, (5,), act)
    w_down_f = g_down.reshape(f2 // 2, d)
    h2 = _mm("down_proj", act, w_down_f, dims=NN, grid=(d // tn_out,), a_spec=full_t(f2 // 2),
             b_spec=pl.BlockSpec((f2 // 2, tn_out), lambda j: (0, j)),
             o_spec=pl.BlockSpec((t, tn_out), lambda j: (0, j)),
             out_shape=jax.ShapeDtypeStruct((t, d), F32),
             res=h1, res_spec=pl.BlockSpec((t, tn_out), lambda j: (0, j)))

    dh2, dh2_b, loss_part, d_norm_final = _loss_bwd(h2, target, norm_final_g.reshape(1, d), n_meta, rows_used)
    dact = _mm("down_dgrad", dh2_b, w_down_f, dims=NT, grid=(f2 // 2 // pair,), a_spec=full_t(d),
               b_spec=pl.BlockSpec((pair, d), lambda j: (j, 0)),
               o_spec=pl.BlockSpec((t, pair), lambda j: (0, j)),
               out_shape=jax.ShapeDtypeStruct((t, f2 // 2), BF16))
    dw_down = _mm("down_wgrad", act, dh2_b, dims=TN, grid=(f2 // 2 // pair,),
                  a_spec=pl.BlockSpec((t, pair), lambda j: (0, j)), b_spec=full_t(d),
                  o_spec=pl.BlockSpec((pair, d), lambda j: (j, 0)),
                  out_shape=jax.ShapeDtypeStruct((f2 // 2, d), BF16))
    ex_down, token = _send_start("exchange_down_start", [dw_down.reshape(N_DEV, dn_s, d)], gather=False)
    dup_pre, d_fw, d_fb = _gated_bwd(up_pre, dact, fw, fb, deps=(token,))
    wide = _wide_tile(t)
    dhn2 = _mm("up_dgrad", dup_pre, w_up_t, dims=NN, grid=(t // wide,),
               a_spec=pl.BlockSpec((wide, f2), lambda i: (i, 0)),
               b_spec=pl.BlockSpec((f2, d), lambda i: (0, 0)),
               o_spec=pl.BlockSpec((wide, d), lambda i: (i, 0)),
               out_shape=jax.ShapeDtypeStruct((t, d), F32))
    dw_up = _mm("up_wgrad", dup_pre, hn2, dims=TN, grid=(f2 // pair,),
                a_spec=pl.BlockSpec((t, pair), lambda j: (0, j)), b_spec=full_t(d),
                o_spec=pl.BlockSpec((pair, d), lambda j: (j, 0)),
                out_shape=jax.ShapeDtypeStruct((f2, d), BF16))
    ex_up, token = _send_start("exchange_up_start", [dw_up.reshape(N_DEV, ff_s, d)], gather=False)
    dh1, dh1_b, d_norm_ffn = _rms_bwd_res("norm_ffn_bwd", dh2, dhn2, h1, norm_ffn_g, deps=(token,))
    dmixed = _mm("out_dgrad", dh1_b, w_out_f, dims=NT, grid=((dc + ds) // tn_out,), a_spec=full_t(d),
                 b_spec=pl.BlockSpec((tn_out, d), lambda i: (i, 0)),
                 o_spec=pl.BlockSpec((t, tn_out), lambda i: (0, i)),
                 out_shape=jax.ShapeDtypeStruct((t, dc + ds), F32))
    dw_out = _mm("out_wgrad", mixed, dh1_b, dims=TN, grid=((dc + ds) // tn_out,),
                 a_spec=pl.BlockSpec((t, tn_out), lambda i: (0, i)), b_spec=full_t(d),
                 o_spec=pl.BlockSpec((tn_out, d), lambda i: (i, 0)),
                 out_shape=jax.ShapeDtypeStruct((dc + ds, d), BF16))
    db_gate, dconv, dy_ssm, d_wglu, d_conv_w, d_gain_c, d_gain_s = _mix_bwd1(
        proj, y_ssm, z_glu, dmixed, w_glu_f, conv_w_f, gain_conv_out, gain_ssm_out)
    (du, d_bbm_re, d_bbm_im, d_a_re, d_a_im, d_ccm_re, d_ccm_im, d_dskip) = _s5_bwd(
        dy_ssm, proj, 3 * dc, s_re, s_im, bbm_re, bbm_im, a_re_c, a_im_c, ccm_re, ccm_im, d_skip)

    from_bb = lambda a: _diag_blocks(a, gpc).reshape(gh, n_state)
    from_cc = lambda a: _diag_blocks(a, gpc).transpose(0, 1, 3, 2).reshape(gh, n_state)
    first_h = lambda a: jnp.pad(a.reshape(n_groups, 1, n_state), ((0, 0), (0, grp - 1), (0, 0))).reshape(gh, n_state)
    over_h = lambda a: a.reshape(n_groups, grp, n_state).sum(axis=1)
    d_lr, d_li, d_dt_e, d_br, d_bi = _disc_bwd(
        lr, li, log_dt_e, br, bi, first_h(d_a_re), first_h(d_a_im), from_bb(d_bbm_re), from_bb(d_bbm_im))

    rep2d = dict(
        ssm_lam_re=(n_groups, n_state), ssm_lam_im=(n_groups, n_state), ssm_log_dt=(1, n_groups),
        ssm_b_re=(gh, n_state), ssm_b_im=(gh, n_state), ssm_c_re=(gh, n_state),
        ssm_c_im=(gh, n_state), ssm_d=(n_groups, grp), gain_conv_out=(1, dc),
        gain_ssm_out=(1, ds), norm_ffn_g=(1, d), ffn_conv_b=(1, N_DEV * ff_s), norm_final_g=(1, d))
    rep_names = list(rep2d)
    rep_grads = dict(
        ssm_lam_re=over_h(d_lr), ssm_lam_im=over_h(d_li), ssm_log_dt=over_h(d_dt_e).sum(axis=1),
        ssm_b_re=d_br, ssm_b_im=d_bi, ssm_c_re=from_cc(d_ccm_re), ssm_c_im=from_cc(d_ccm_im),
        ssm_d=d_dskip, gain_conv_out=d_gain_c, gain_ssm_out=d_gain_s, norm_ffn_g=d_norm_ffn,
        ffn_conv_b=d_fb, norm_final_g=d_norm_final)
    rep_shapes = [rep2d[n] for n in rep_names] + [(1, 1)]
    rep_rows = _packed_rows(rep_shapes)
    rep_pack = _pack([rep_grads[n] for n in rep_names] + [loss_part], rep_rows)
    ex_mix, token = _send_start("exchange_mix_start", [
        dw_out.reshape(N_DEV, -1, d), d_wglu.astype(BF16).reshape(N_DEV, -1, ds),
        d_conv_w.reshape(3, N_DEV, dc_s).transpose(1, 0, 2),
        d_fw.reshape(3, N_DEV, ff_s).transpose(1, 0, 2), rep_pack],
        gather=[False, False, False, False, True])
    dcdv = _mix_bwd2(proj, dconv, conv_w_f, deps=(token,))
    dproj = jnp.concatenate([db_gate, dcdv, du], axis=1)
    d_in = N_DEV * d_in_s
    w_in_rows = g_in.transpose(1, 0, 2).reshape(d, d_in)
    dhn1 = _mm("proj_dgrad", dproj, w_in_rows, dims=NT, grid=(t // wide,),
               a_spec=pl.BlockSpec((wide, d_in), lambda i: (i, 0)),
               b_spec=pl.BlockSpec((d, d_in), lambda i: (0, 0)),
               o_spec=pl.BlockSpec((wide, d), lambda i: (i, 0)),
               out_shape=jax.ShapeDtypeStruct((t, d), F32))
    dw_in = _mm("proj_wgrad", hn1, dproj, dims=TN, grid=(N_DEV,), a_spec=full_t(d),
                b_spec=pl.BlockSpec((t, d_in_s), lambda j: (0, j)),
                o_spec=pl.BlockSpec((None, d, d_in_s), lambda j: (j, 0, 0)),
                out_shape=jax.ShapeDtypeStruct((N_DEV, d, d_in_s), BF16))
    ex_in, token = _send_start("exchange_in_start", [dw_in], gather=False)
    dh0, _, d_norm_mix = _rms_bwd_res("norm_mix_bwd", dh1, dhn1, h0, norm_mix_g, deps=(token,))
    dh0 = _from_segments(dh0)
    grad_x = dh0[n_meta:rows_used][None]
    d_meta_b = dh0[:n_meta].reshape(n_meta, N_DEV, d_meta).transpose(1, 0, 2)
    ex_last, ex_last_token = _send_start("exchange_last_start", [d_meta_b, d_norm_mix], gather=[False, True])

    shard_out = {}
    (rep_parts,) = _send_wait("gather_small_grads_wait", ex_mix, (4,), ex_last_token)
    rep_sum = _sum_parts("sum_small_grads", rep_parts)
    *rep_g, loss = _unpack(rep_sum, rep_shapes)
    loss = loss.reshape(())
    swapped = ("ssm_b_re", "ssm_b_im")
    to2d = lambda n, a: ghp(a) if n in swapped else a.reshape(rep2d[n])
    from2d = lambda n, a: (a.reshape(1, n_groups, grp, n_state).transpose(0, 1, 3, 2) if n in swapped
                           else a.reshape(weights[n].shape))
    as2d = lambda tree: [to2d(n, tree[n]) for n in rep_names]
    rep_res = _adamw_many("adamw_replicated", rep_g, as2d(weights), as2d(mom_m), as2d(mom_v))
    for i, n in enumerate(rep_names):
        shard_out[n] = [from2d(n, r) for r in (rep_g[i], *(res[i] for res in rep_res))]

    def update(n, parts, transposed=False):
        sh = weights[n].shape
        two_d = lambda a: (jnp.swapaxes(a[0], 0, 1) if transposed else a.reshape(parts.shape[1:]))
        res = _adamw_reduce("adamw_" + n, parts, two_d(weights[n]), two_d(mom_m[n]), two_d(mom_v[n]))
        shard_out[n] = [(jnp.swapaxes(r, 0, 1) if transposed else r).reshape(sh) for r in res]
        return res[0]

    (p_down,) = _send_wait("exchange_down_wait", ex_down, (0,), rep_sum)
    done = update("w_down", p_down)
    (p_up,) = _send_wait("exchange_up_wait", ex_up, (0,), done)
    done = update("w_up", p_up, transposed=True)
    p_out, p_glu, p_cw, p_fw = _send_wait("exchange_mix_wait", ex_mix, (0, 1, 2, 3), done)
    update("w_out", p_out)
    update("ssm_w_glu", p_glu)
    update("conv_w", p_cw)
    done = update("ffn_conv_w", p_fw)
    p_meta, p_nm = _send_wait("exchange_last_wait", ex_last, (0, 1), done)
    update("meta_tokens", p_meta)
    done = update("norm_mix_g", p_nm)
    (p_in,) = _send_wait("exchange_in_wait", ex_in, (0,), done)
    update("w_in", p_in)

    grads = [shard_out[n][0] for n in names]
    deltas = [shard_out[n][1] for n in names]
    new_m = [shard_out[n][2] for n in names]
    new_v = [shard_out[n][3] for n in names]
    return (loss, grad_x, *grads, *deltas, *new_m, *new_v)
```

```python
import functools
import math

import jax
import jax.numpy as jnp
from jax import lax
from jax.experimental import pallas as pl
from jax.experimental.pallas import tpu as pltpu

F32 = jnp.float32
BF16 = jnp.bfloat16
MESH = pl.DeviceIdType.MESH

N_DEV = 8
RMS_EPS = 1e-6
ADAM_LR = 0.001
ADAM_B1 = 0.9
ADAM_B2 = 0.999
ADAM_EPS = 1e-08
ADAM_WD = 0.01
ADAM_STEP = 10
ADAM_BC1 = 1.0 - ADAM_B1 ** ADAM_STEP
ADAM_BC2 = 1.0 - ADAM_B2 ** ADAM_STEP

SUBLANES = 8
LANES = 128
ROW_TILE = 128
ROW_CHUNK = 32
WIDE_TILES = (544, 256, 128)
FFN_COLS = 256
N_SEG = 8
HALO_ROWS = 16
SSM_CHUNK = 128
SCAN_UNROLL = 2
VMEM_LIMIT = 48 * 1024 * 1024

NN = ((1,), (0,))
NT = ((1,), (1,))
TN = ((0,), (0,))


def _params(**kw):
    return pltpu.CompilerParams(vmem_limit_bytes=VMEM_LIMIT, **kw)


def _dot(a, b, dims):
    return lax.dot_general(a, b, (dims, ((), ())), preferred_element_type=F32)


def _mean_sq_rsqrt(x):
    return lax.rsqrt(jnp.mean(x * x, axis=-1, keepdims=True) + RMS_EPS)


def _rms_bwd(x, r, g, dy):
    xhat = x * r
    dxh = dy * g
    dx = r * (dxh - xhat * jnp.mean(dxh * xhat, axis=-1, keepdims=True))
    return dx, dy * xhat


def _gelu(y):
    c = math.sqrt(2.0 / math.pi)
    t = jnp.tanh(c * (y + 0.044715 * y * y * y))
    return 0.5 * y * (1.0 + t), t


def _gelu_grad(y, t):
    c = math.sqrt(2.0 / math.pi)
    return 0.5 * (1.0 + t) + 0.5 * y * (1.0 - t * t) * c * (1.0 + 3.0 * 0.044715 * y * y)


def _wrap_prev_halo(halo, first):
    seg = lax.broadcasted_iota(jnp.int32, halo.shape, 0) % N_SEG
    wrapped = jnp.where(seg == 0, 0.0, pltpu.roll(halo, 1, 0))
    return jnp.where(first, wrapped, halo)


def _wrap_next_halo(halo, last):
    seg = lax.broadcasted_iota(jnp.int32, halo.shape, 0) % N_SEG
    wrapped = jnp.where(seg == N_SEG - 1, 0.0, pltpu.roll(halo, halo.shape[0] - 1, 0))
    return jnp.where(last, wrapped, halo)


def _tile_rows(get, prev, nxt, s, n, tile_rows=ROW_TILE):
    parts = []
    if s < 0:
        parts.append(prev[HALO_ROWS + s:HALO_ROWS + min(s + n, 0)])
    lo, hi = max(s, 0), min(s + n, tile_rows)
    if hi > lo:
        parts.append(get(lo, hi - lo))
    if s + n > tile_rows:
        parts.append(nxt[max(s - tile_rows, 0):s + n - tile_rows])
    return parts[0] if len(parts) == 1 else jnp.concatenate(parts, axis=0)


def _dev_index(p):
    return 4 * p[0] + 2 * p[1] + p[2]


def _allgather(name, shards, deps=()):
    n = len(shards)

    def body(*refs):
        ins, outs = refs[:n], refs[n:2 * n]
        send_sems, recv_sems, local_sems = refs[2 * n:]
        x, y, c = lax.axis_index("x"), lax.axis_index("y"), lax.axis_index("c")
        me, sibling = (x, y, c), (x, y, 1 - c)
        chips = [(1 - x, y), (x, 1 - y), (1 - x, 1 - y)]

        def copy(a, k, block, to, src=None):
            dst = outs[a].at[_dev_index(block)]
            return pltpu.make_async_remote_copy(
                src_ref=dst if src is None else src, dst_ref=dst,
                send_sem=send_sems.at[a, k], recv_sem=recv_sems.at[a, k],
                device_id=to, device_id_type=MESH)

        mine = [pltpu.make_async_copy(ins[a], outs[a].at[_dev_index(me)], local_sems.at[a])
                for a in range(n)]
        for cp in mine:
            cp.start()
        first = []
        for a in range(n):
            first.append(copy(a, 0, me, sibling, src=ins[a]))
            for j, chip in enumerate(chips):
                first.append(copy(a, 1 + j, me, (*chip, c), src=ins[a]))
        for cp in first:
            cp.start()
        passed = []
        for j, chip in enumerate(chips):
            for a in range(n):
                copy(a, 1 + j, (*chip, c), me).wait_recv()
                fwd = copy(a, 4 + j, (*chip, c), sibling)
                fwd.start()
                passed.append(fwd)
        for a in range(n):
            copy(a, 0, sibling, me).wait_recv()
            for j, chip in enumerate(chips):
                copy(a, 4 + j, (*chip, 1 - c), me).wait_recv()
        for cp in first + passed:
            cp.wait_send()
        for cp in mine:
            cp.wait()

    any_spec = pl.BlockSpec(memory_space=pl.ANY)
    return pl.pallas_call(
        _ignoring_deps(body, n, deps), name=name,
        out_shape=[jax.ShapeDtypeStruct((N_DEV,) + s.shape, s.dtype) for s in shards],
        in_specs=[any_spec] * (n + len(deps)), out_specs=[any_spec] * n,
        scratch_shapes=[pltpu.SemaphoreType.DMA((n, 7)), pltpu.SemaphoreType.DMA((n, 7)),
                        pltpu.SemaphoreType.DMA((n,))],
    )(*shards, *deps)


HBM_SPEC = pl.BlockSpec(memory_space=pltpu.HBM)
SEM_SPEC = pl.BlockSpec(memory_space=pltpu.SEMAPHORE)
ANY_SPEC = pl.BlockSpec(memory_space=pl.ANY)
DATAFLOW = pltpu.SideEffectType.DATAFLOW_SIDE_EFFECTING


def _ignoring_deps(body, n_in, deps):
    n_dep = len(deps)

    def wrapped(*refs):
        return body(*refs[:n_in], *refs[n_in + n_dep:])

    return wrapped


def _my_position():
    x, y, c = lax.axis_index("x"), lax.axis_index("y"), lax.axis_index("c")
    return (x, y, c)


def _peer(me, k):
    return tuple((1 - v) if (k >> s) & 1 else v for v, s in zip(me, (2, 1, 0)))


def _split_copies(src_refs, land_refs, send_sems, recv_sems, gather):
    me = _my_position()
    copies = []
    for a, (src, land) in enumerate(zip(src_refs, land_refs)):
        for k in range(1, N_DEV):
            peer = _peer(me, k)
            copies.append(pltpu.make_async_remote_copy(
                src_ref=src if gather[a] else src.at[_dev_index(peer)], dst_ref=land.at[_dev_index(me)],
                send_sem=send_sems[a].at[k - 1], recv_sem=recv_sems[a].at[k - 1],
                device_id=peer, device_id_type=MESH))
    return copies


def _own_slot(block, like_shape):
    me = _dev_index(_my_position())
    return lax.dynamic_update_index_in_dim(lax.empty(like_shape, block.dtype), block, me, 0)


def _send_start(name, srcs, gather):
    n = len(srcs)
    me = _dev_index(_my_position())
    gather = [gather] * n if isinstance(gather, bool) else list(gather)
    lands = [_own_slot(s, (N_DEV,) + s.shape) if g else
             _own_slot(lax.dynamic_index_in_dim(s, me, 0, keepdims=False), s.shape)
             for s, g in zip(srcs, gather)]

    def body(*refs):
        src_refs, land_refs = refs[:n], refs[n:2 * n]
        send_sems, recv_sems = refs[2 * n:3 * n], refs[3 * n:4 * n]
        token = refs[-1]
        for cp in _split_copies(src_refs, land_refs, send_sems, recv_sems, gather):
            cp.start()
        token[...] = jnp.zeros_like(token)

    hbm = lambda a: pltpu.HBM(a.shape, a.dtype)
    sems = [pltpu.SemaphoreType.DMA((N_DEV - 1,))] * n
    outs = pl.pallas_call(
        body, name=name,
        out_shape=(*sems, *sems, *[hbm(s) for s in srcs], *[hbm(l) for l in lands],
                   jax.ShapeDtypeStruct((SUBLANES, LANES), F32)),
        in_specs=[HBM_SPEC] * (2 * n),
        out_specs=(*[SEM_SPEC] * (2 * n), *[HBM_SPEC] * (2 * n), pl.BlockSpec(memory_space=pltpu.VMEM)),
        input_output_aliases={i: 2 * n + i for i in range(2 * n)},
        compiler_params=pltpu.CompilerParams(has_side_effects=DATAFLOW),
    )(*[pltpu.with_memory_space_constraint(a, pltpu.HBM) for a in (*srcs, *lands)])
    state = dict(send=outs[:n], recv=outs[n:2 * n], srcs=outs[2 * n:3 * n], lands=outs[3 * n:4 * n],
                 gather=gather)
    return state, outs[-1]


def _send_wait(name, state, which, after):
    n = len(which)
    pick = lambda key: [state[key][i] for i in which]
    gather = pick("gather")

    def body(*refs):
        src_refs, land_refs = refs[:n], refs[n:2 * n]
        send_sems, recv_sems = refs[2 * n:3 * n], refs[3 * n:4 * n]
        for cp in _split_copies(src_refs, land_refs, send_sems, recv_sems, gather):
            cp.wait_send()
            cp.wait_recv()

    srcs, lands = pick("srcs"), pick("lands")
    hbm = lambda a: pltpu.HBM(a.shape, a.dtype)
    outs = pl.pallas_call(
        body, name=name,
        out_shape=(*[hbm(s) for s in srcs], *[hbm(l) for l in lands]),
        in_specs=[*[HBM_SPEC] * (2 * n), *[SEM_SPEC] * (2 * n), ANY_SPEC],
        out_specs=tuple([HBM_SPEC] * (2 * n)),
        input_output_aliases={i: i for i in range(2 * n)},
        compiler_params=pltpu.CompilerParams(has_side_effects=DATAFLOW),
    )(*srcs, *lands, *pick("send"), *pick("recv"), after)
    return outs[n:]


def _two_level_copies(src_refs, land_refs, sems1, sems2):
    x, y, c = _my_position()
    me, sibling = (x, y, c), (x, y, 1 - c)
    chips = [(1 - x, y), (x, 1 - y), (1 - x, 1 - y)]
    stage1, stage2 = [], []
    for a, land in enumerate(land_refs):
        def copy(src, block, to, sems, k):
            return pltpu.make_async_remote_copy(
                src_ref=src, dst_ref=land.at[_dev_index(block)], send_sem=sems[0][a].at[k],
                recv_sem=sems[1][a].at[k], device_id=to, device_id_type=MESH)
        src = src_refs[a] if src_refs is not None else land.at[_dev_index(me)]
        stage1.append([copy(src, me, sibling, sems1, 0)] +
                      [copy(src, me, (*chip, c), sems1, 1 + j) for j, chip in enumerate(chips)])
        if sems2 is not None:
            stage2.append([copy(land.at[_dev_index((*chip, c))], (*chip, c), sibling, sems2, j)
                           for j, chip in enumerate(chips)])
    return stage1, stage2


def _gather2_start(name, shards):
    n = len(shards)
    lands = [_own_slot(s, (N_DEV,) + s.shape) for s in shards]

    def body(*refs):
        src_refs, land_refs = refs[:n], refs[n:2 * n]
        sems1 = (refs[2 * n:3 * n], refs[3 * n:4 * n])
        stage1, _ = _two_level_copies(src_refs, land_refs, sems1, None)
        for copies in stage1:
            for cp in copies:
                cp.start()
        refs[-1][...] = jnp.zeros_like(refs[-1])

    hbm = lambda a: pltpu.HBM(a.shape, a.dtype)
    sems = [pltpu.SemaphoreType.DMA((4,))] * n
    outs = pl.pallas_call(
        body, name=name,
        out_shape=(*sems, *sems, *[hbm(s) for s in shards], *[hbm(l) for l in lands],
                   jax.ShapeDtypeStruct((SUBLANES, LANES), F32)),
        in_specs=[HBM_SPEC] * (2 * n),
        out_specs=(*[SEM_SPEC] * (2 * n), *[HBM_SPEC] * (2 * n), pl.BlockSpec(memory_space=pltpu.VMEM)),
        input_output_aliases={i: 2 * n + i for i in range(2 * n)},
        compiler_params=pltpu.CompilerParams(has_side_effects=DATAFLOW),
    )(*[pltpu.with_memory_space_constraint(a, pltpu.HBM) for a in (*shards, *lands)])
    state = dict(send1=list(outs[:n]), recv1=list(outs[n:2 * n]), srcs=list(outs[2 * n:3 * n]),
                 lands=list(outs[3 * n:4 * n]), send2={}, recv2={})
    return state, outs[-1]


def _gather2_forward(name, state, which, after):
    n = len(which)
    pick = lambda key: [state[key][i] for i in which]

    def body(*refs):
        land_refs, recv1 = refs[:n], refs[n:2 * n]
        outs = refs[2 * n + len(after):]
        sems2 = (outs[:n], outs[n:2 * n])
        stage1, stage2 = _two_level_copies(None, land_refs, (recv1, recv1), sems2)
        for a in range(n):
            for j in range(3):
                stage1[a][1 + j].wait_recv()
                stage2[a][j].start()
        outs[-1][...] = jnp.zeros_like(outs[-1])

    lands = pick("lands")
    sems = [pltpu.SemaphoreType.DMA((3,))] * n
    outs = pl.pallas_call(
        body, name=name,
        out_shape=(*sems, *sems, *[pltpu.HBM(l.shape, l.dtype) for l in lands],
                   jax.ShapeDtypeStruct((SUBLANES, LANES), F32)),
        in_specs=[*[HBM_SPEC] * n, *[SEM_SPEC] * n, *[ANY_SPEC] * len(after)],
        out_specs=(*[SEM_SPEC] * (2 * n), *[HBM_SPEC] * n, pl.BlockSpec(memory_space=pltpu.VMEM)),
        input_output_aliases={i: 2 * n + i for i in range(n)},
        compiler_params=pltpu.CompilerParams(has_side_effects=DATAFLOW),
    )(*lands, *pick("recv1"), *after)
    for idx, i in enumerate(which):
        state["send2"][i], state["recv2"][i] = outs[idx], outs[n + idx]
        state["lands"][i] = outs[2 * n + idx]
    return outs[-1]


def _gather2_wait(name, state, which, after):
    n = len(which)
    pick = lambda key: [state[key][i] for i in which]

    def body(*refs):
        src_refs, land_refs = refs[:n], refs[n:2 * n]
        sems1 = (refs[2 * n:3 * n], refs[3 * n:4 * n])
        sems2 = (refs[4 * n:5 * n], refs[5 * n:6 * n])
        stage1, stage2 = _two_level_copies(src_refs, land_refs, sems1, sems2)
        for a in range(n):
            for cp in stage1[a]:
                cp.wait_send()
            stage1[a][0].wait_recv()
            for cp in stage2[a]:
                cp.wait_send()
                cp.wait_recv()

    srcs, lands = pick("srcs"), pick("lands")
    hbm = lambda a: pltpu.HBM(a.shape, a.dtype)
    outs = pl.pallas_call(
        body, name=name,
        out_shape=(*[hbm(s) for s in srcs], *[hbm(l) for l in lands]),
        in_specs=[*[HBM_SPEC] * (2 * n), *[SEM_SPEC] * (4 * n), ANY_SPEC],
        out_specs=tuple([HBM_SPEC] * (2 * n)),
        input_output_aliases={i: i for i in range(2 * n)},
        compiler_params=pltpu.CompilerParams(has_side_effects=DATAFLOW),
    )(*srcs, *lands, *pick("send1"), *pick("recv1"), *pick("send2"), *pick("recv2"), after)
    return outs[n:]


def _mm(name, a, b, *, dims, grid, a_spec, b_spec, o_spec, out_shape, acc_shape=None,
        res=None, res_spec=None):
    n_red = grid[-1] if acc_shape is not None else 1
    red_axis = len(grid) - 1

    def body(*refs):
        a_ref, b_ref = refs[0], refs[1]
        r_ref = refs[2] if res is not None else None
        o_ref = refs[3] if res is not None else refs[2]
        part = _dot(a_ref[...], b_ref[...], dims)
        if acc_shape is None:
            if r_ref is not None:
                part = part + r_ref[...]
            o_ref[...] = part.astype(o_ref.dtype)
            return
        acc_ref = refs[-1]
        k = pl.program_id(red_axis)

        @pl.when(k == 0)
        def _():
            acc_ref[...] = part

        @pl.when(k > 0)
        def _():
            acc_ref[...] += part

        @pl.when(k == n_red - 1)
        def _():
            total = acc_ref[...]
            if r_ref is not None:
                total = total + r_ref[...]
            o_ref[...] = total.astype(o_ref.dtype)

    ins, in_specs = [a, b], [a_spec, b_spec]
    if res is not None:
        ins.append(res)
        in_specs.append(res_spec)
    return pl.pallas_call(
        body, name=name, grid=grid, in_specs=in_specs, out_specs=o_spec, out_shape=out_shape,
        scratch_shapes=[pltpu.VMEM(acc_shape, F32)] if acc_shape is not None else [],
        compiler_params=_params(),
    )(*ins)


def _wide_tile(t):
    return next(c for c in WIDE_TILES if t % c == 0)


def _rms_fwd(name, h, g):
    t, d = h.shape
    tile = _wide_tile(t)

    def body(h_ref, g_ref, o_ref):
        x = h_ref[...]
        o_ref[...] = (x * _mean_sq_rsqrt(x) * g_ref[...]).astype(BF16)

    return pl.pallas_call(
        body, name=name, grid=(t // tile,),
        in_specs=[pl.BlockSpec((tile, d), lambda i: (i, 0)), pl.BlockSpec((1, d), lambda i: (0, 0))],
        out_specs=pl.BlockSpec((tile, d), lambda i: (i, 0)),
        out_shape=jax.ShapeDtypeStruct((t, d), BF16), compiler_params=_params(),
    )(h, g)


def _rms_bwd_res(name, dres, dhn, h, g, deps=()):
    t, d = h.shape

    def body(dres_ref, dhn_ref, h_ref, g_ref, dh_ref, dhb_ref, dg_ref):
        x = h_ref[...]
        dx, dgt = _rms_bwd(x, _mean_sq_rsqrt(x), g_ref[...], dhn_ref[...])
        dh = dres_ref[...] + dx
        dh_ref[...] = dh
        dhb_ref[...] = dh.astype(BF16)

        @pl.when(pl.program_id(0) == 0)
        def _():
            dg_ref[...] = jnp.zeros_like(dg_ref)

        dg_ref[...] += jnp.sum(dgt, axis=0, keepdims=True)

    tile = _wide_tile(t)
    row = pl.BlockSpec((tile, d), lambda i: (i, 0))
    vec = pl.BlockSpec((1, d), lambda i: (0, 0))
    return pl.pallas_call(
        _ignoring_deps(body, 4, deps), name=name, grid=(t // tile,),
        in_specs=[row, row, row, vec] + [ANY_SPEC] * len(deps), out_specs=[row, row, vec],
        out_shape=[jax.ShapeDtypeStruct((t, d), F32), jax.ShapeDtypeStruct((t, d), BF16),
                   jax.ShapeDtypeStruct((1, d), F32)],
        compiler_params=_params(),
    )(dres, dhn, h, g, *deps)


def _loss_bwd(h2, target, g, row_lo, row_hi):
    t, d = h2.shape
    tile = _wide_tile(t)

    def body(h_ref, tg_ref, g_ref, dh_ref, dhb_ref, loss_ref, dg_ref):
        i = pl.program_id(0)
        x = h_ref[...]
        r = _mean_sq_rsqrt(x)
        gv = g_ref[...]
        y = x * r * gv
        rows = i * tile + lax.broadcasted_iota(jnp.int32, (tile, 1), 0)
        time = (rows % N_SEG) * (t // N_SEG) + rows // N_SEG
        valid = jnp.logical_and(time >= row_lo, time < row_hi)
        err = jnp.where(valid, y - tg_ref[...], 0.0)
        dy = err * (1.0 / d)
        dx, dgt = _rms_bwd(x, r, gv, dy)
        dh_ref[...] = dx
        dhb_ref[...] = dx.astype(BF16)

        @pl.when(i == 0)
        def _():
            loss_ref[...] = jnp.zeros_like(loss_ref)
            dg_ref[...] = jnp.zeros_like(dg_ref)

        row_loss = jnp.mean(err * err, axis=-1, keepdims=True)
        loss_ref[...] += 0.5 * jnp.sum(row_loss, axis=0, keepdims=True)
        dg_ref[...] += jnp.sum(dgt, axis=0, keepdims=True)

    row = pl.BlockSpec((tile, d), lambda i: (i, 0))
    vec = pl.BlockSpec((1, d), lambda i: (0, 0))
    return pl.pallas_call(
        body, name="loss_bwd", grid=(t // tile,), in_specs=[row, row, vec],
        out_specs=[row, row, pl.BlockSpec((1, 1), lambda i: (0, 0)), vec],
        out_shape=[jax.ShapeDtypeStruct((t, d), F32), jax.ShapeDtypeStruct((t, d), BF16),
                   jax.ShapeDtypeStruct((1, 1), F32), jax.ShapeDtypeStruct((1, d), F32)],
        compiler_params=_params(),
    )(h2, target, g)


def _prev_halo(i, t):
    return jnp.where(i == 0, t // HALO_ROWS - 1, i * (ROW_TILE // HALO_ROWS) - 1)


def _next_halo(i, t):
    return jnp.where(i == t // ROW_TILE - 1, 0, (i + 1) * (ROW_TILE // HALO_ROWS))


def _causal_taps(cur, halo, first):
    rows = cur.shape[0]
    ext = jnp.concatenate([_wrap_prev_halo(halo, first), cur], axis=0)
    return ext[:rows], ext[N_SEG:N_SEG + rows]


def _anticausal_taps(cur, halo, last):
    rows = cur.shape[0]
    ext = jnp.concatenate([cur, _wrap_next_halo(halo, last)], axis=0)
    return ext[N_SEG:N_SEG + rows], ext[2 * N_SEG:2 * N_SEG + rows]


def _mix_fwd(proj, y, w_glu, conv_w, gain_c, gain_s):
    t = proj.shape[0]
    dc = conv_w.shape[1]
    ds = y.shape[1]

    def body(p_ref, halo_ref, y_ref, wg_ref, cw_ref, gc_ref, gs_ref, mixed_ref, z_ref):
        i = pl.program_id(0)
        p = p_ref[...]
        b, c, v = p[:, :dc], p[:, dc:2 * dc], p[:, 2 * dc:3 * dc]
        cv = c * v
        hp = halo_ref[...]
        x2, x1 = _causal_taps(cv, hp[:, dc:2 * dc] * hp[:, 2 * dc:3 * dc], i == 0)
        cw = cw_ref[...]
        conv = cw[0:1] * x2 + cw[1:2] * x1 + cw[2:3] * cv
        co = b * conv
        mixed_ref[:, :dc] = (co * _mean_sq_rsqrt(co) * gc_ref[...]).astype(BF16)
        g, _ = _gelu(y_ref[...])
        z = _dot(g.astype(BF16), wg_ref[...], NN)
        z_ref[...] = z
        so = g * jax.nn.sigmoid(z)
        mixed_ref[:, dc:] = (so * _mean_sq_rsqrt(so) * gs_ref[...]).astype(BF16)

    const = lambda i: (0, 0)
    return pl.pallas_call(
        body, name="mix_fwd", grid=(t // ROW_TILE,),
        in_specs=[pl.BlockSpec((ROW_TILE, 3 * dc), lambda i: (i, 0)),
                  pl.BlockSpec((HALO_ROWS, 3 * dc), lambda i: (_prev_halo(i, t), 0)),
                  pl.BlockSpec((ROW_TILE, ds), lambda i: (i, 0)),
                  pl.BlockSpec((ds, ds), const), pl.BlockSpec(conv_w.shape, const),
                  pl.BlockSpec((1, dc), const), pl.BlockSpec((1, ds), const)],
        out_specs=[pl.BlockSpec((ROW_TILE, dc + ds), lambda i: (i, 0)),
                   pl.BlockSpec((ROW_TILE, ds), lambda i: (i, 0))],
        out_shape=[jax.ShapeDtypeStruct((t, dc + ds), BF16), jax.ShapeDtypeStruct((t, ds), F32)],
        compiler_params=_params(),
    )(proj, proj, y, w_glu, conv_w, gain_c, gain_s)


def _mix_bwd1(proj, y, z, dmixed, w_glu, conv_w, gain_c, gain_s):
    t = proj.shape[0]
    dc = conv_w.shape[1]
    ds = y.shape[1]

    def body(p_ref, halo_ref, y_ref, z_ref, dm_ref, wg_ref, cw_ref, gc_ref, gs_ref,
             db_ref, dconv_ref, dy_ref, dwg_ref, dcw_ref, dgc_ref, dgs_ref):
        i = pl.program_id(0)

        @pl.when(i == 0)
        def _():
            dwg_ref[...] = jnp.zeros_like(dwg_ref)
            dcw_ref[...] = jnp.zeros_like(dcw_ref)
            dgc_ref[...] = jnp.zeros_like(dgc_ref)
            dgs_ref[...] = jnp.zeros_like(dgs_ref)

        p = p_ref[...]
        b, c, v = p[:, :dc], p[:, dc:2 * dc], p[:, 2 * dc:3 * dc]
        cv = c * v
        hp = halo_ref[...]
        x2, x1 = _causal_taps(cv, hp[:, dc:2 * dc] * hp[:, 2 * dc:3 * dc], i == 0)
        cw = cw_ref[...]
        conv = cw[0:1] * x2 + cw[1:2] * x1 + cw[2:3] * cv
        co = b * conv
        dm = dm_ref[...]
        dco, dgt = _rms_bwd(co, _mean_sq_rsqrt(co), gc_ref[...], dm[:, :dc])
        dgc_ref[...] += jnp.sum(dgt, axis=0, keepdims=True)
        db_ref[...] = (dco * conv).astype(BF16)
        dconv = dco * b
        dconv_ref[...] = dconv
        dcw_ref[0:1, :] += jnp.sum(dconv * x2, axis=0, keepdims=True)
        dcw_ref[1:2, :] += jnp.sum(dconv * x1, axis=0, keepdims=True)
        dcw_ref[2:3, :] += jnp.sum(dconv * cv, axis=0, keepdims=True)

        yv = y_ref[...]
        g, th = _gelu(yv)
        sg = jax.nn.sigmoid(z_ref[...])
        so = g * sg
        dso, dgt = _rms_bwd(so, _mean_sq_rsqrt(so), gs_ref[...], dm[:, dc:])
        dgs_ref[...] += jnp.sum(dgt, axis=0, keepdims=True)
        dz = (dso * g * sg * (1.0 - sg)).astype(BF16)
        dg = dso * sg + _dot(dz, wg_ref[...], NT)
        dwg_ref[...] += _dot(g.astype(BF16), dz, TN)
        dy_ref[...] = dg * _gelu_grad(yv, th)

    const = lambda i: (0, 0)
    row = lambda w: pl.BlockSpec((ROW_TILE, w), lambda i: (i, 0))
    return pl.pallas_call(
        body, name="mix_bwd1", grid=(t // ROW_TILE,),
        in_specs=[row(3 * dc), pl.BlockSpec((HALO_ROWS, 3 * dc), lambda i: (_prev_halo(i, t), 0)),
                  row(ds), row(ds), row(dc + ds), pl.BlockSpec((ds, ds), const),
                  pl.BlockSpec(conv_w.shape, const), pl.BlockSpec((1, dc), const),
                  pl.BlockSpec((1, ds), const)],
        out_specs=[row(dc), row(dc), row(ds), pl.BlockSpec((ds, ds), const),
                   pl.BlockSpec(conv_w.shape, const), pl.BlockSpec((1, dc), const),
                   pl.BlockSpec((1, ds), const)],
        out_shape=[jax.ShapeDtypeStruct((t, dc), BF16), jax.ShapeDtypeStruct((t, dc), F32),
                   jax.ShapeDtypeStruct((t, ds), F32), jax.ShapeDtypeStruct((ds, ds), F32),
                   jax.ShapeDtypeStruct(conv_w.shape, F32), jax.ShapeDtypeStruct((1, dc), F32),
                   jax.ShapeDtypeStruct((1, ds), F32)],
        compiler_params=_params(),
    )(proj, proj, y, z, dmixed, w_glu, conv_w, gain_c, gain_s)


def _mix_bwd2(proj, dconv, conv_w, deps=()):
    t = proj.shape[0]
    dc = conv_w.shape[1]
    n_tiles = t // ROW_TILE

    def body(c_ref, v_ref, d_ref, halo_ref, cw_ref, o_ref):
        i = pl.program_id(0)
        d = d_ref[...]
        u1, u2 = _anticausal_taps(d, halo_ref[...], i == n_tiles - 1)
        cw = cw_ref[...]
        dcv = cw[2:3] * d + cw[1:2] * u1 + cw[0:1] * u2
        o_ref[:, :dc] = (dcv * v_ref[...]).astype(BF16)
        o_ref[:, dc:] = (dcv * c_ref[...]).astype(BF16)

    return pl.pallas_call(
        _ignoring_deps(body, 5, deps), name="mix_bwd2", grid=(n_tiles,),
        in_specs=[pl.BlockSpec((ROW_TILE, dc), lambda i: (i, 1)),
                  pl.BlockSpec((ROW_TILE, dc), lambda i: (i, 2)),
                  pl.BlockSpec((ROW_TILE, dc), lambda i: (i, 0)),
                  pl.BlockSpec((HALO_ROWS, dc), lambda i: (_next_halo(i, t), 0)),
                  pl.BlockSpec(conv_w.shape, lambda i: (0, 0))] + [ANY_SPEC] * len(deps),
        out_specs=pl.BlockSpec((ROW_TILE, 2 * dc), lambda i: (i, 0)),
        out_shape=jax.ShapeDtypeStruct((t, 2 * dc), BF16), compiler_params=_params(),
    )(proj, proj, dconv, dconv, conv_w, *deps)


def _conv_taps(up_ref, prev, nxt, fw_ref, fb_ref, c0, r, rows=ROW_CHUNK):
    cols = pl.ds(c0, FFN_COLS)
    get = lambda s, n: up_ref[pl.ds(s, n), cols]
    x0, x1, x2 = (_tile_rows(get, prev, nxt, r - k * N_SEG, rows) for k in range(3))
    w = fw_ref[:, cols]
    return w[0:1] * x2 + w[1:2] * x1 + w[2:3] * x0 + fb_ref[:, cols], x2, x1, x0


def _gated_fwd(up_pre, fw, fb):
    t, f2 = up_pre.shape
    f = f2 // 2

    def body(up_ref, halo_ref, fw_ref, fb_ref, act_ref):
        first = pl.program_id(0) == 0
        for c0 in range(0, f, FFN_COLS):
            halo_a = _wrap_prev_halo(halo_ref[:, pl.ds(c0, FFN_COLS)], first)
            halo_v = _wrap_prev_halo(halo_ref[:, pl.ds(f + c0, FFN_COLS)], first)
            for r in range(0, ROW_TILE, ROW_CHUNK):
                a = _conv_taps(up_ref, halo_a, None, fw_ref, fb_ref, c0, r)[0]
                val = _conv_taps(up_ref, halo_v, None, fw_ref, fb_ref, f + c0, r)[0]
                act_ref[pl.ds(r, ROW_CHUNK), pl.ds(c0, FFN_COLS)] = (a * jax.nn.sigmoid(a) * val).astype(BF16)

    const = lambda a: pl.BlockSpec(a.shape, lambda i: (0, 0))
    return pl.pallas_call(
        body, name="ffn_fwd", grid=(t // ROW_TILE,),
        in_specs=[pl.BlockSpec((ROW_TILE, f2), lambda i: (i, 0)),
                  pl.BlockSpec((HALO_ROWS, f2), lambda i: (_prev_halo(i, t), 0)), const(fw), const(fb)],
        out_specs=pl.BlockSpec((ROW_TILE, f), lambda i: (i, 0)),
        out_shape=jax.ShapeDtypeStruct((t, f), BF16), compiler_params=_params(),
    )(up_pre, up_pre, fw, fb)


def _gated_bwd(up_pre, dact, fw, fb, deps=()):
    t, f2 = up_pre.shape
    f = f2 // 2
    n_tiles = t // ROW_TILE
    chunks = [(r, ROW_CHUNK) for r in range(0, ROW_TILE, ROW_CHUNK)] + [(ROW_TILE, HALO_ROWS)]

    def body(up_ref, prev_ref, next_ref, dact_ref, dact_next_ref, fw_ref, fb_ref,
             out_ref, dfw_ref, dfb_ref, dup_ref):
        first = pl.program_id(0) == 0
        last = pl.program_id(0) == n_tiles - 1

        @pl.when(first)
        def _():
            dfw_ref[...] = jnp.zeros_like(dfw_ref)
            dfb_ref[...] = jnp.zeros_like(dfb_ref)

        fold = lambda v: sum(v[s:s + SUBLANES] for s in range(0, ROW_CHUNK, SUBLANES))
        for c0 in range(0, f, FFN_COLS):
            starts = (c0, f + c0)
            prev = [_wrap_prev_halo(prev_ref[:, pl.ds(c, FFN_COLS)], first) for c in starts]
            nxt = [_wrap_next_halo(next_ref[:, pl.ds(c, FFN_COLS)], last) for c in starts]
            dact_next = _wrap_next_halo(dact_next_ref[:, pl.ds(c0, FFN_COLS)].astype(F32), last)
            get_dact = lambda s, n: dact_ref[pl.ds(s, n), pl.ds(c0, FFN_COLS)].astype(F32)
            sums = [[jnp.zeros((SUBLANES, FFN_COLS), F32)] * 4 for _ in starts]
            for r, rows in chunks:
                taps = [_conv_taps(up_ref, prev[k], nxt[k], fw_ref, fb_ref, c, r, rows)
                        for k, c in enumerate(starts)]
                a, val = taps[0][0], taps[1][0]
                da_ct = _tile_rows(get_dact, None, dact_next, r, rows)
                sg = jax.nn.sigmoid(a)
                dup = (da_ct * val * sg * (1.0 + a * (1.0 - sg)), da_ct * a * sg)
                for k in range(2):
                    dup_ref[k, pl.ds(r, rows), :] = dup[k]
                    if r < ROW_TILE:
                        terms = (dup[k], dup[k] * taps[k][1], dup[k] * taps[k][2], dup[k] * taps[k][3])
                        sums[k] = [s + fold(v) for s, v in zip(sums[k], terms)]
            for k, c in enumerate(starts):
                cols = pl.ds(c, FFN_COLS)
                s_b, s_w0, s_w1, s_w2 = (jnp.sum(p, axis=0, keepdims=True) for p in sums[k])
                dfb_ref[:, cols] += s_b
                for tap, s_w in enumerate((s_w0, s_w1, s_w2)):
                    dfw_ref[tap:tap + 1, cols] += s_w
                cw = fw_ref[:, cols]
                for r in range(0, ROW_TILE, ROW_CHUNK):
                    d, u1, u2 = (dup_ref[k, pl.ds(r + s * N_SEG, ROW_CHUNK), :] for s in range(3))
                    out_ref[pl.ds(r, ROW_CHUNK), cols] = (cw[2:3] * d + cw[1:2] * u1 + cw[0:1] * u2).astype(BF16)

    tile = lambda w: pl.BlockSpec((ROW_TILE, w), lambda i: (i, 0))
    halo = lambda w, index: pl.BlockSpec((HALO_ROWS, w), lambda i: (index(i, t), 0))
    const = lambda a: pl.BlockSpec(a.shape, lambda i: (0, 0))
    return pl.pallas_call(
        _ignoring_deps(body, 7, deps), name="ffn_bwd", grid=(n_tiles,),
        in_specs=[tile(f2), halo(f2, _prev_halo), halo(f2, _next_halo), tile(f), halo(f, _next_halo),
                  const(fw), const(fb)] + [ANY_SPEC] * len(deps),
        out_specs=[tile(f2), const(fw), const(fb)],
        out_shape=[jax.ShapeDtypeStruct((t, f2), BF16), jax.ShapeDtypeStruct(fw.shape, F32),
                   jax.ShapeDtypeStruct(fb.shape, F32)],
        scratch_shapes=[pltpu.VMEM((2, ROW_TILE + HALO_ROWS, FFN_COLS), F32)],
        compiler_params=_params(),
    )(up_pre, up_pre, up_pre, dact, dact, fw, fb, *deps)


def _to_segments(a):
    t, c = a.shape
    return a.reshape(N_SEG, t // N_SEG, c).transpose(1, 0, 2).reshape(t, c)


def _from_segments(a):
    t, c = a.shape
    return a.reshape(t // N_SEG, N_SEG, c).transpose(1, 0, 2).reshape(t, c)


def _cmul(ar, ai, br, bi):
    return ar * br - ai * bi, ar * bi + ai * br


def _cpow(ar, ai, n):
    out = None
    while n:
        if n & 1:
            out = (ar, ai) if out is None else _cmul(out[0], out[1], ar, ai)
        ar, ai = _cmul(ar, ai, ar, ai)
        n >>= 1
    return out


def _segment_carries(pr, pi, fr, fi, forward):
    row = lax.broadcasted_iota(jnp.int32, fr.shape, 0)
    edge = row == (0 if forward else N_SEG - 1)
    shift = 1 if forward else N_SEG - 1
    sr, si = jnp.zeros_like(fr), jnp.zeros_like(fi)
    for _ in range(N_SEG - 1):
        tr, ti = _cmul(pr, pi, sr, si)
        sr = jnp.where(edge, 0.0, pltpu.roll(tr + fr, shift, 0))
        si = jnp.where(edge, 0.0, pltpu.roll(ti + fi, shift, 0))
    return sr, si


def _rows(i):
    return pl.ds(pl.multiple_of(i * SUBLANES, SUBLANES), SUBLANES)


def _s5_fwd(proj, u_col, bb_re, bb_im, a_re, a_im, cc_re, cc_im, d_skip):
    t = proj.shape[0]
    nch, _, cs = bb_re.shape
    ds = nch * SSM_CHUNK
    u_blk = u_col // SSM_CHUNK
    steps = t // N_SEG

    def body(u_ref, bbr_ref, bbi_ref, ar_ref, ai_ref, ccr_ref, cci_ref, d_ref, sr_ref, si_ref, y_ref):
        ub = u_ref[...].astype(BF16)
        sr_ref[...] = _dot(ub, bbr_ref[...], NN)
        si_ref[...] = _dot(ub, bbi_ref[...], NN)
        ar = jnp.broadcast_to(ar_ref[...], (N_SEG, cs))
        ai = jnp.broadcast_to(ai_ref[...], (N_SEG, cs))
        zero = jnp.zeros((N_SEG, cs), F32)

        def totals(i, carry):
            tr, ti = _cmul(ar, ai, *carry)
            return tr + sr_ref[_rows(i), :], ti + si_ref[_rows(i), :]

        fr, fi = lax.fori_loop(0, steps, totals, (zero, zero), unroll=SCAN_UNROLL)
        s0r, s0i = _segment_carries(*_cpow(ar, ai, steps), fr, fi, True)

        def scan(i, carry):
            tr, ti = _cmul(ar, ai, *carry)
            nr, ni = tr + sr_ref[_rows(i), :], ti + si_ref[_rows(i), :]
            sr_ref[_rows(i), :] = nr
            si_ref[_rows(i), :] = ni
            return nr, ni

        lax.fori_loop(0, steps, scan, (s0r, s0i), unroll=SCAN_UNROLL)
        y_ref[...] = (_dot(sr_ref[...].astype(BF16), ccr_ref[...], NN)
                      - _dot(si_ref[...].astype(BF16), cci_ref[...], NN)
                      + d_ref[...] * u_ref[...])

    chunk3 = lambda r, c: pl.BlockSpec((None, r, c), lambda j: (j, 0, 0))
    return pl.pallas_call(
        body, name="s5_fwd", grid=(nch,),
        in_specs=[pl.BlockSpec((t, SSM_CHUNK), lambda j: (0, j + u_blk)),
                  chunk3(SSM_CHUNK, cs), chunk3(SSM_CHUNK, cs), chunk3(1, cs), chunk3(1, cs),
                  chunk3(cs, SSM_CHUNK), chunk3(cs, SSM_CHUNK), chunk3(1, SSM_CHUNK)],
        out_specs=[pl.BlockSpec((t, cs), lambda j: (0, j)), pl.BlockSpec((t, cs), lambda j: (0, j)),
                   pl.BlockSpec((t, SSM_CHUNK), lambda j: (0, j))],
        out_shape=[jax.ShapeDtypeStruct((t, nch * cs), F32), jax.ShapeDtypeStruct((t, nch * cs), F32),
                   jax.ShapeDtypeStruct((t, ds), F32)],
        compiler_params=_params(),
    )(proj, bb_re, bb_im, a_re, a_im, cc_re, cc_im, d_skip)


def _s5_bwd(dy, proj, u_col, s_re, s_im, bb_re, bb_im, a_re, a_im, cc_re, cc_im, d_skip, gpc):
    t, ds = dy.shape
    nch, _, cs = bb_re.shape
    u_blk = u_col // SSM_CHUNK
    steps = t // N_SEG

    def body(dy_ref, u_ref, sr_ref, si_ref, bbr_ref, bbi_ref, ar_ref, ai_ref, ccr_ref, cci_ref, d_ref,
             du_ref, dbbr_ref, dbbi_ref, dar_ref, dai_ref, dccr_ref, dcci_ref, dd_ref, gr_ref, gi_ref):
        dyv = dy_ref[...]
        dyb = dyv.astype(BF16)
        gr_ref[...] = _dot(dyb, ccr_ref[...], NT)
        gi_ref[...] = -_dot(dyb, cci_ref[...], NT)
        ar = jnp.broadcast_to(ar_ref[...], (N_SEG, cs))
        ai = -jnp.broadcast_to(ai_ref[...], (N_SEG, cs))
        zero = jnp.zeros((N_SEG, cs), F32)

        def totals(k, carry):
            i = steps - 1 - k
            tr, ti = _cmul(ar, ai, *carry)
            return tr + gr_ref[_rows(i), :], ti + gi_ref[_rows(i), :]

        fr, fi = lax.fori_loop(0, steps, totals, (zero, zero), unroll=SCAN_UNROLL)
        e0r, e0i = _segment_carries(*_cpow(ar, ai, steps), fr, fi, False)

        def step(i, gr, gi, pr, pi, acc_r, acc_i):
            tr, ti = _cmul(ar, ai, gr, gi)
            nr, ni = tr + gr_ref[_rows(i), :], ti + gi_ref[_rows(i), :]
            gr_ref[_rows(i), :] = nr
            gi_ref[_rows(i), :] = ni
            return nr, ni, acc_r + nr * pr + ni * pi, acc_i + ni * pr - nr * pi

        def scan(k, carry):
            i = steps - 1 - k
            gr, gi, acc_r, acc_i = carry
            return step(i, gr, gi, sr_ref[_rows(i - 1), :], si_ref[_rows(i - 1), :], acc_r, acc_i)

        gr, gi, acc_r, acc_i = lax.fori_loop(0, steps - 1, scan, (e0r, e0i, zero, zero), unroll=SCAN_UNROLL)
        row = lax.broadcasted_iota(jnp.int32, (N_SEG, cs), 0)
        last = _rows(steps - 1)
        pr = jnp.where(row == 0, 0.0, pltpu.roll(sr_ref[last, :], 1, 0))
        pi = jnp.where(row == 0, 0.0, pltpu.roll(si_ref[last, :], 1, 0))
        _, _, acc_r, acc_i = step(0, gr, gi, pr, pi, acc_r, acc_i)
        dar_ref[...] = jnp.sum(acc_r, axis=0, keepdims=True)
        dai_ref[...] = jnp.sum(acc_i, axis=0, keepdims=True)

        uv = u_ref[...]
        ub = uv.astype(BF16)
        grb = gr_ref[...].astype(BF16)
        gib = gi_ref[...].astype(BF16)
        du = d_ref[...] * dyv + _dot(grb, bbr_ref[...], NT) + _dot(gib, bbi_ref[...], NT)
        du_ref[...] = du.astype(BF16)
        def put_groups(ref, full):
            for gl in range(gpc):
                ref[gl] = full[gl * hb:(gl + 1) * hb, gl * pb:(gl + 1) * pb]

        put_groups(dbbr_ref, _dot(ub, grb, TN))
        put_groups(dbbi_ref, _dot(ub, gib, TN))
        put_groups(dccr_ref, _dot(dyb, sr_ref[...].astype(BF16), TN))
        put_groups(dcci_ref, -_dot(dyb, si_ref[...].astype(BF16), TN))
        dd_ref[...] = jnp.sum(dyv * uv, axis=0, keepdims=True)

    hb, pb = SSM_CHUNK // gpc, cs // gpc
    groups = pl.BlockSpec((None, gpc, hb, pb), lambda j: (j, 0, 0, 0))
    groups_shape = jax.ShapeDtypeStruct((nch, gpc, hb, pb), F32)
    chunk3 = lambda r, c: pl.BlockSpec((None, r, c), lambda j: (j, 0, 0))
    cols = lambda w: pl.BlockSpec((t, w), lambda j: (0, j))
    return pl.pallas_call(
        body, name="s5_bwd", grid=(nch,),
        in_specs=[cols(SSM_CHUNK), pl.BlockSpec((t, SSM_CHUNK), lambda j: (0, j + u_blk)), cols(cs), cols(cs),
                  chunk3(SSM_CHUNK, cs), chunk3(SSM_CHUNK, cs), chunk3(1, cs), chunk3(1, cs),
                  chunk3(cs, SSM_CHUNK), chunk3(cs, SSM_CHUNK), chunk3(1, SSM_CHUNK)],
        out_specs=[cols(SSM_CHUNK), groups, groups, chunk3(1, cs), chunk3(1, cs), groups, groups,
                   chunk3(1, SSM_CHUNK)],
        out_shape=[jax.ShapeDtypeStruct((t, ds), BF16), groups_shape, groups_shape,
                   jax.ShapeDtypeStruct((nch, 1, cs), F32), jax.ShapeDtypeStruct((nch, 1, cs), F32),
                   groups_shape, groups_shape, jax.ShapeDtypeStruct((nch, 1, SSM_CHUNK), F32)],
        scratch_shapes=[pltpu.VMEM((t, cs), F32), pltpu.VMEM((t, cs), F32)],
        compiler_params=_params(),
    )(dy, proj, s_re, s_im, bb_re, bb_im, a_re, a_im, cc_re, cc_im, d_skip)


def _discretize(lr, li, log_dt, br, bi):
    dt = jnp.exp(log_dt)
    mag = jnp.exp(lr * dt)
    ang = li * dt
    a_re = mag * jnp.cos(ang)
    a_im = mag * jnp.sin(ang)
    den = lr * lr + li * li
    nr = a_re - 1.0
    f_re = (nr * lr + a_im * li) / den
    f_im = (a_im * lr - nr * li) / den
    return a_re, a_im, f_re * br - f_im * bi, f_re * bi + f_im * br


def _whole(shape):
    return pl.BlockSpec(shape, lambda: (0,) * len(shape))


def _disc_fwd(lr, li, log_dt, br, bi, deps=()):
    def body(lr_ref, li_ref, dt_ref, br_ref, bi_ref, ar_ref, ai_ref, bbr_ref, bbi_ref):
        outs = _discretize(lr_ref[...], li_ref[...], dt_ref[...], br_ref[...], bi_ref[...])
        for ref, val in zip((ar_ref, ai_ref, bbr_ref, bbi_ref), outs):
            ref[...] = val

    args = (lr, li, log_dt, br, bi)
    outs = (lr, lr, br, br)
    return pl.pallas_call(
        _ignoring_deps(body, 5, deps), name="disc_fwd",
        in_specs=[_whole(a.shape) for a in args] + [ANY_SPEC] * len(deps),
        out_specs=[_whole(a.shape) for a in outs],
        out_shape=[jax.ShapeDtypeStruct(a.shape, F32) for a in outs],
    )(*args, *deps)


def _disc_bwd(lr, li, log_dt, br, bi, dar, dai, dbbr, dbbi):
    def body(lr_ref, li_ref, dt_ref, br_ref, bi_ref, dar_ref, dai_ref, dbbr_ref, dbbi_ref,
             dlr_ref, dli_ref, ddt_ref, dbr_ref, dbi_ref):
        _, vjp = jax.vjp(_discretize, lr_ref[...], li_ref[...], dt_ref[...], br_ref[...], bi_ref[...])
        grads = vjp((dar_ref[...], dai_ref[...], dbbr_ref[...], dbbi_ref[...]))
        for ref, val in zip((dlr_ref, dli_ref, ddt_ref, dbr_ref, dbi_ref), grads):
            ref[...] = val

    args = (lr, li, log_dt, br, bi, dar, dai, dbbr, dbbi)
    outs = (lr, li, log_dt, br, bi)
    return pl.pallas_call(
        body, name="disc_bwd", in_specs=[_whole(a.shape) for a in args],
        out_specs=[_whole(a.shape) for a in outs],
        out_shape=[jax.ShapeDtypeStruct(a.shape, F32) for a in outs],
    )(*args)


def _adamw(w, g, m, v):
    m = ADAM_B1 * m + (1.0 - ADAM_B1) * g
    v = ADAM_B2 * v + (1.0 - ADAM_B2) * (g * g)
    m_hat = m / ADAM_BC1
    v_hat = v / ADAM_BC2
    delta = -ADAM_LR * (m_hat / (jnp.sqrt(v_hat) + ADAM_EPS) + ADAM_WD * w)
    return delta, m, v


def _adamw_reduce(name, parts, w, m, v):
    _, r, c = parts.shape
    tr = r
    for cand in (256, 176, 128):
        if r % cand == 0:
            tr = cand
            break

    def body(p_ref, w_ref, m_ref, v_ref, g_ref, d_ref, nm_ref, nv_ref):
        g = p_ref[0].astype(F32)
        for k in range(1, N_DEV):
            g = g + p_ref[k].astype(F32)
        delta, nm, nv = _adamw(w_ref[...], g, m_ref[...], v_ref[...])
        g_ref[...] = g
        d_ref[...] = delta
        nm_ref[...] = nm
        nv_ref[...] = nv

    blk = pl.BlockSpec((tr, c), lambda i: (i, 0))
    return pl.pallas_call(
        body, name=name, grid=(r // tr,),
        in_specs=[pl.BlockSpec((N_DEV, tr, c), lambda i: (0, i, 0)), blk, blk, blk],
        out_specs=[blk] * 4, out_shape=[jax.ShapeDtypeStruct((r, c), F32)] * 4,
        compiler_params=_params(),
    )(parts, w, m, v)


def _sum_parts(name, parts):
    _, r, c = parts.shape

    def body(p_ref, o_ref):
        g = p_ref[0]
        for k in range(1, N_DEV):
            g = g + p_ref[k]
        o_ref[...] = g

    return pl.pallas_call(
        body, name=name, in_specs=[_whole(parts.shape)], out_specs=_whole((r, c)),
        out_shape=jax.ShapeDtypeStruct((r, c), F32), compiler_params=_params(),
    )(parts)


def _adamw_many(name, grads, ws, ms, vs):
    n = len(grads)

    def body(*refs):
        ins, outs = refs[:4 * n], refs[4 * n:]
        for i in range(n):
            g, w, m, v = (ins[j * n + i][...] for j in range(4))
            for ref, val in zip((outs[i], outs[n + i], outs[2 * n + i]), _adamw(w, g, m, v)):
                ref[...] = val

    args = (*grads, *ws, *ms, *vs)
    outs = pl.pallas_call(
        body, name=name, in_specs=[_whole(a.shape) for a in args],
        out_specs=[_whole(a.shape) for a in ws] * 3,
        out_shape=[jax.ShapeDtypeStruct(a.shape, F32) for a in ws] * 3, compiler_params=_params(),
    )(*args)
    return outs[:n], outs[n:2 * n], outs[2 * n:]


def _pack(arrays, rows):
    flat = jnp.concatenate([a.reshape(-1) for a in arrays])
    return jnp.pad(flat, (0, rows * LANES - flat.shape[0])).reshape(rows, LANES)


def _unpack(packed, shapes):
    flat = packed.reshape(-1)
    out, off = [], 0
    for s in shapes:
        n = math.prod(s)
        out.append(flat[off:off + n].reshape(s))
        off += n
    return out


def _packed_rows(shapes):
    n = sum(math.prod(s) for s in shapes)
    return -(-n // (SUBLANES * LANES)) * SUBLANES


def _block_diag(x):
    j, g, r, c = x.shape
    eye = jnp.eye(g, dtype=x.dtype)
    return (x[:, :, :, None, :] * eye[None, :, None, :, None]).reshape(j, g * r, g * c)


def kernel(x, meta_tokens, norm_mix_g, w_in, conv_w, ssm_lam_re, ssm_lam_im, ssm_log_dt, ssm_b_re, ssm_b_im, ssm_c_re, ssm_c_im, ssm_d, ssm_w_glu, gain_conv_out, gain_ssm_out, w_out, norm_ffn_g, w_up, ffn_conv_w, ffn_conv_b, w_down, norm_final_g, loss_target, m_meta_tokens, m_norm_mix_g, m_w_in, m_conv_w, m_ssm_lam_re, m_ssm_lam_im, m_ssm_log_dt, m_ssm_b_re, m_ssm_b_im, m_ssm_c_re, m_ssm_c_im, m_ssm_d, m_ssm_w_glu, m_gain_conv_out, m_gain_ssm_out, m_w_out, m_norm_ffn_g, m_w_up, m_ffn_conv_w, m_ffn_conv_b, m_w_down, m_norm_final_g, v_meta_tokens, v_norm_mix_g, v_w_in, v_conv_w, v_ssm_lam_re, v_ssm_lam_im, v_ssm_log_dt, v_ssm_b_re, v_ssm_b_im, v_ssm_c_re, v_ssm_c_im, v_ssm_d, v_ssm_w_glu, v_gain_conv_out, v_gain_ssm_out, v_w_out, v_norm_ffn_g, v_w_up, v_ffn_conv_w, v_ffn_conv_b, v_w_down, v_norm_final_g):
    weights = dict(meta_tokens=meta_tokens, norm_mix_g=norm_mix_g, w_in=w_in, conv_w=conv_w, ssm_lam_re=ssm_lam_re, ssm_lam_im=ssm_lam_im, ssm_log_dt=ssm_log_dt, ssm_b_re=ssm_b_re, ssm_b_im=ssm_b_im, ssm_c_re=ssm_c_re, ssm_c_im=ssm_c_im, ssm_d=ssm_d, ssm_w_glu=ssm_w_glu, gain_conv_out=gain_conv_out, gain_ssm_out=gain_ssm_out, w_out=w_out, norm_ffn_g=norm_ffn_g, w_up=w_up, ffn_conv_w=ffn_conv_w, ffn_conv_b=ffn_conv_b, w_down=w_down, norm_final_g=norm_final_g)
    mom_m = dict(meta_tokens=m_meta_tokens, norm_mix_g=m_norm_mix_g, w_in=m_w_in, conv_w=m_conv_w, ssm_lam_re=m_ssm_lam_re, ssm_lam_im=m_ssm_lam_im, ssm_log_dt=m_ssm_log_dt, ssm_b_re=m_ssm_b_re, ssm_b_im=m_ssm_b_im, ssm_c_re=m_ssm_c_re, ssm_c_im=m_ssm_c_im, ssm_d=m_ssm_d, ssm_w_glu=m_ssm_w_glu, gain_conv_out=m_gain_conv_out, gain_ssm_out=m_gain_ssm_out, w_out=m_w_out, norm_ffn_g=m_norm_ffn_g, w_up=m_w_up, ffn_conv_w=m_ffn_conv_w, ffn_conv_b=m_ffn_conv_b, w_down=m_w_down, norm_final_g=m_norm_final_g)
    mom_v = dict(meta_tokens=v_meta_tokens, norm_mix_g=v_norm_mix_g, w_in=v_w_in, conv_w=v_conv_w, ssm_lam_re=v_ssm_lam_re, ssm_lam_im=v_ssm_lam_im, ssm_log_dt=v_ssm_log_dt, ssm_b_re=v_ssm_b_re, ssm_b_im=v_ssm_b_im, ssm_c_re=v_ssm_c_re, ssm_c_im=v_ssm_c_im, ssm_d=v_ssm_d, ssm_w_glu=v_ssm_w_glu, gain_conv_out=v_gain_conv_out, gain_ssm_out=v_gain_ssm_out, w_out=v_w_out, norm_ffn_g=v_norm_ffn_g, w_up=v_w_up, ffn_conv_w=v_ffn_conv_w, ffn_conv_b=v_ffn_conv_b, w_down=v_w_down, norm_final_g=v_norm_final_g)
    names = list(weights)

    n_meta, d_meta = meta_tokens.shape
    seq, d = x.shape[1], x.shape[2]
    rows_used = n_meta + seq
    t = -(-rows_used // ROW_TILE) * ROW_TILE
    d_in_s = w_in.shape[2]
    dc_s = conv_w.shape[2]
    dc = dc_s * N_DEV
    ds = ssm_w_glu.shape[2]
    n_groups, n_state, grp = ssm_b_re.shape[1:]
    ns = n_groups * n_state
    nch = ds // SSM_CHUNK
    gpc = n_groups // nch
    ff_s = w_up.shape[2]
    dn_s = w_down.shape[1]
    assert 3 * dc + ds == d_in_s * N_DEV and 2 * dn_s == ff_s and t % (N_SEG * SUBLANES) == 0

    small_shard = jnp.concatenate([meta_tokens.reshape(-1), conv_w.reshape(-1), ffn_conv_w.reshape(-1)])
    n_small = small_shard.shape[0]
    small_rows = -(-n_small // LANES)
    small_shard = jnp.pad(small_shard, (0, small_rows * LANES - n_small)).reshape(small_rows, LANES)
    ag, ag_token = _gather2_start("gather_weights_start", [
        small_shard, w_in[0].astype(BF16), ssm_w_glu[0].astype(BF16), w_out[0].astype(BF16),
        jnp.swapaxes(w_up[0], 0, 1).astype(BF16), w_down[0].astype(BF16)])
    fb = ffn_conv_b

    gh = n_groups * grp
    per_h = lambda a: jnp.broadcast_to(a.reshape(n_groups, 1, -1), (n_groups, grp, n_state)).reshape(gh, n_state)
    ghp = lambda a: a.transpose(0, 1, 3, 2).reshape(gh, n_state)
    lr, li, log_dt_e = per_h(ssm_lam_re), per_h(ssm_lam_im), per_h(ssm_log_dt)
    br, bi = ghp(ssm_b_re), ghp(ssm_b_im)
    a_re, a_im, bb_re, bb_im = _disc_fwd(lr, li, log_dt_e, br, bi, deps=(ag_token,))
    cs = gpc * n_state
    chunk_row = lambda a: a.reshape(n_groups, grp, n_state)[:, 0].reshape(nch, 1, cs)
    to_bb = lambda a: _block_diag(a.reshape(nch, gpc, grp, n_state)).astype(BF16)
    to_cc = lambda a: _block_diag(a.reshape(nch, gpc, grp, n_state).transpose(0, 1, 3, 2)).astype(BF16)
    bbm_re, bbm_im = to_bb(bb_re), to_bb(bb_im)
    ccm_re, ccm_im = to_cc(ssm_c_re), to_cc(ssm_c_im)
    a_re_c, a_im_c = chunk_row(a_re), chunk_row(a_im)
    d_skip = ssm_d.reshape(nch, 1, SSM_CHUNK)

    target = _to_segments(jnp.pad(loss_target[0], ((n_meta, t - rows_used), (0, 0))))
    token = _gather2_forward("gather_weights_forward_first", ag, (0, 1), (bbm_im, ccm_im, target))
    (g_small,) = _gather2_wait("gather_weights_wait_small", ag, (0,), token)
    g_small = g_small.reshape(N_DEV, -1)
    o1 = n_meta * d_meta
    o2 = o1 + 3 * dc_s
    meta_full = g_small[:, :o1].reshape(N_DEV, n_meta, d_meta).transpose(1, 0, 2).reshape(n_meta, d)
    conv_w_f = g_small[:, o1:o2].reshape(N_DEV, 3, dc_s).transpose(1, 0, 2).reshape(3, dc)
    fw = g_small[:, o2:o2 + 3 * ff_s].reshape(N_DEV, 3, ff_s).transpose(1, 0, 2).reshape(3, N_DEV * ff_s)
    h0 = _to_segments(jnp.concatenate([meta_full, x[0], jnp.zeros((t - rows_used, d), F32)], axis=0))
    full_t = lambda w: pl.BlockSpec((t, w), lambda *_: (0, 0))

    hn1 = _rms_fwd("norm_mix", h0, norm_mix_g)
    (g_in,) = _gather2_wait("gather_weights_wait_in", ag, (1,), hn1)
    proj = _mm("proj", hn1, g_in, dims=NN, grid=(N_DEV,), a_spec=full_t(d),
               b_spec=pl.BlockSpec((None, d, d_in_s), lambda j: (j, 0, 0)),
               o_spec=pl.BlockSpec((t, d_in_s), lambda j: (0, j)),
               out_shape=jax.ShapeDtypeStruct((t, N_DEV * d_in_s), F32))
    s_re, s_im, y_ssm = _s5_fwd(proj, 3 * dc, bbm_re, bbm_im, a_re_c, a_im_c, ccm_re, ccm_im, d_skip)
    token = _gather2_forward("gather_weights_forward_up", ag, (2, 3, 4), (y_ssm,))
    g_glu, g_out = _gather2_wait("gather_weights_wait_mix", ag, (2, 3), token)
    w_out_f = g_out.reshape(-1, d)
    w_glu_f = g_glu.reshape(ds, ds)
    mixed, z_glu = _mix_fwd(proj, y_ssm, w_glu_f, conv_w_f, gain_conv_out, gain_ssm_out)
    tn_out = 256
    h1 = _mm("out_proj", mixed, w_out_f, dims=NN, grid=(d // tn_out,), a_spec=full_t(dc + ds),
             b_spec=pl.BlockSpec((dc + ds, tn_out), lambda j: (0, j)),
             o_spec=pl.BlockSpec((t, tn_out), lambda j: (0, j)),
             out_shape=jax.ShapeDtypeStruct((t, d), F32),
             res=h0, res_spec=pl.BlockSpec((t, tn_out), lambda j: (0, j)))
    hn2 = _rms_fwd("norm_ffn", h1, norm_ffn_g)
    token = _gather2_forward("gather_weights_forward_down", ag, (5,), (hn2,))
    (g_up,) = _gather2_wait("gather_weights_wait_up", ag, (4,), token)
    f2 = N_DEV * ff_s
    pair = 2 * ff_s
    w_up_t = g_up.reshape(f2, d)
    up_pre = _mm("up_proj", hn2, w_up_t, dims=NT, grid=(f2 // pair,), a_spec=full_t(d),
                 b_spec=pl.BlockSpec((pair, d), lambda j: (j, 0)),
                 o_spec=pl.BlockSpec((t, pair), lambda j: (0, j)),
                 out_shape=jax.ShapeDtypeStruct((t, f2), F32))
    act = _gated_fwd(up_pre, fw, fb)
    (g_down,) = _gather2_wait("gather_weights_wait_down", ag, (5,), act)
    w_down_f = g_down.reshape(f2 // 2, d)
    h2 = _mm("down_proj", act, w_down_f, dims=NN, grid=(d // tn_out,), a_spec=full_t(f2 // 2),
             b_spec=pl.BlockSpec((f2 // 2, tn_out), lambda j: (0, j)),
             o_spec=pl.BlockSpec((t, tn_out), lambda j: (0, j)),
             out_shape=jax.ShapeDtypeStruct((t, d), F32),
             res=h1, res_spec=pl.BlockSpec((t, tn_out), lambda j: (0, j)))

    dh2, dh2_b, loss_part, d_norm_final = _loss_bwd(h2, target, norm_final_g.reshape(1, d), n_meta, rows_used)
    dact = _mm("down_dgrad", dh2_b, w_down_f, dims=NT, grid=(f2 // 2 // pair,), a_spec=full_t(d),
               b_spec=pl.BlockSpec((pair, d), lambda j: (j, 0)),
               o_spec=pl.BlockSpec((t, pair), lambda j: (0, j)),
               out_shape=jax.ShapeDtypeStruct((t, f2 // 2), BF16))
    dw_down = _mm("down_wgrad", act, dh2_b, dims=TN, grid=(f2 // 2 // pair,),
                  a_spec=pl.BlockSpec((t, pair), lambda j: (0, j)), b_spec=full_t(d),
                  o_spec=pl.BlockSpec((pair, d), lambda j: (j, 0)),
                  out_shape=jax.ShapeDtypeStruct((f2 // 2, d), BF16))
    ex_down, token = _send_start("exchange_down_start", [dw_down.reshape(N_DEV, dn_s, d)], gather=False)
    dup_pre, d_fw, d_fb = _gated_bwd(up_pre, dact, fw, fb, deps=(token,))
    wide = _wide_tile(t)
    dhn2 = _mm("up_dgrad", dup_pre, w_up_t, dims=NN, grid=(t // wide,),
               a_spec=pl.BlockSpec((wide, f2), lambda i: (i, 0)),
               b_spec=pl.BlockSpec((f2, d), lambda i: (0, 0)),
               o_spec=pl.BlockSpec((wide, d), lambda i: (i, 0)),
               out_shape=jax.ShapeDtypeStruct((t, d), F32))
    dw_up = _mm("up_wgrad", dup_pre, hn2, dims=TN, grid=(f2 // pair,),
                a_spec=pl.BlockSpec((t, pair), lambda j: (0, j)), b_spec=full_t(d),
                o_spec=pl.BlockSpec((pair, d), lambda j: (j, 0)),
                out_shape=jax.ShapeDtypeStruct((f2, d), BF16))
    ex_up, token = _send_start("exchange_up_start", [dw_up.reshape(N_DEV, ff_s, d)], gather=False)
    dh1, dh1_b, d_norm_ffn = _rms_bwd_res("norm_ffn_bwd", dh2, dhn2, h1, norm_ffn_g, deps=(token,))
    dmixed = _mm("out_dgrad", dh1_b, w_out_f, dims=NT, grid=((dc + ds) // tn_out,), a_spec=full_t(d),
                 b_spec=pl.BlockSpec((tn_out, d), lambda i: (i, 0)),
                 o_spec=pl.BlockSpec((t, tn_out), lambda i: (0, i)),
                 out_shape=jax.ShapeDtypeStruct((t, dc + ds), F32))
    dw_out = _mm("out_wgrad", mixed, dh1_b, dims=TN, grid=((dc + ds) // tn_out,),
                 a_spec=pl.BlockSpec((t, tn_out), lambda i: (0, i)), b_spec=full_t(d),
                 o_spec=pl.BlockSpec((tn_out, d), lambda i: (i, 0)),
                 out_shape=jax.ShapeDtypeStruct((dc + ds, d), BF16))
    db_gate, dconv, dy_ssm, d_wglu, d_conv_w, d_gain_c, d_gain_s = _mix_bwd1(
        proj, y_ssm, z_glu, dmixed, w_glu_f, conv_w_f, gain_conv_out, gain_ssm_out)
    (du, d_bbm_re, d_bbm_im, d_a_re, d_a_im, d_ccm_re, d_ccm_im, d_dskip) = _s5_bwd(
        dy_ssm, proj, 3 * dc, s_re, s_im, bbm_re, bbm_im, a_re_c, a_im_c, ccm_re, ccm_im, d_skip, gpc)

    from_bb = from_cc = lambda a: a.reshape(gh, n_state)
    first_h = lambda a: jnp.pad(a.reshape(n_groups, 1, n_state), ((0, 0), (0, grp - 1), (0, 0))).reshape(gh, n_state)
    over_h = lambda a: a.reshape(n_groups, grp, n_state).sum(axis=1)
    d_lr, d_li, d_dt_e, d_br, d_bi = _disc_bwd(
        lr, li, log_dt_e, br, bi, first_h(d_a_re), first_h(d_a_im), from_bb(d_bbm_re), from_bb(d_bbm_im))

    rep2d = dict(
        ssm_lam_re=(n_groups, n_state), ssm_lam_im=(n_groups, n_state), ssm_log_dt=(1, n_groups),
        ssm_b_re=(gh, n_state), ssm_b_im=(gh, n_state), ssm_c_re=(gh, n_state),
        ssm_c_im=(gh, n_state), ssm_d=(n_groups, grp), gain_conv_out=(1, dc),
        gain_ssm_out=(1, ds), norm_ffn_g=(1, d), ffn_conv_b=(1, N_DEV * ff_s), norm_final_g=(1, d))
    rep_names = list(rep2d)
    rep_grads = dict(
        ssm_lam_re=over_h(d_lr), ssm_lam_im=over_h(d_li), ssm_log_dt=over_h(d_dt_e).sum(axis=1),
        ssm_b_re=d_br, ssm_b_im=d_bi, ssm_c_re=from_cc(d_ccm_re), ssm_c_im=from_cc(d_ccm_im),
        ssm_d=d_dskip, gain_conv_out=d_gain_c, gain_ssm_out=d_gain_s, norm_ffn_g=d_norm_ffn,
        ffn_conv_b=d_fb, norm_final_g=d_norm_final)
    rep_shapes = [rep2d[n] for n in rep_names] + [(1, 1)]
    rep_rows = _packed_rows(rep_shapes)
    rep_pack = _pack([rep_grads[n] for n in rep_names] + [loss_part], rep_rows)
    ex_mix, token = _send_start("exchange_mix_start", [
        dw_out.reshape(N_DEV, -1, d), d_wglu.astype(BF16).reshape(N_DEV, -1, ds),
        d_conv_w.reshape(3, N_DEV, dc_s).transpose(1, 0, 2),
        d_fw.reshape(3, N_DEV, ff_s).transpose(1, 0, 2), rep_pack],
        gather=[False, False, False, False, True])
    dcdv = _mix_bwd2(proj, dconv, conv_w_f, deps=(token,))
    dproj = jnp.concatenate([db_gate, dcdv, du], axis=1)
    d_in = N_DEV * d_in_s
    w_in_rows = g_in.transpose(1, 0, 2).reshape(d, d_in)
    dhn1 = _mm("proj_dgrad", dproj, w_in_rows, dims=NT, grid=(t // wide,),
               a_spec=pl.BlockSpec((wide, d_in), lambda i: (i, 0)),
               b_spec=pl.BlockSpec((d, d_in), lambda i: (0, 0)),
               o_spec=pl.BlockSpec((wide, d), lambda i: (i, 0)),
               out_shape=jax.ShapeDtypeStruct((t, d), F32))
    dw_in = _mm("proj_wgrad", hn1, dproj, dims=TN, grid=(N_DEV,), a_spec=full_t(d),
                b_spec=pl.BlockSpec((t, d_in_s), lambda j: (0, j)),
                o_spec=pl.BlockSpec((None, d, d_in_s), lambda j: (j, 0, 0)),
                out_shape=jax.ShapeDtypeStruct((N_DEV, d, d_in_s), BF16))
    dh0, _, d_norm_mix = _rms_bwd_res("norm_mix_bwd", dh1, dhn1, h0, norm_mix_g)
    dh0 = _from_segments(dh0)
    grad_x = dh0[n_meta:rows_used][None]
    d_meta_b = dh0[:n_meta].reshape(n_meta, N_DEV, d_meta).transpose(1, 0, 2)
    ex_in, ex_in_token = _send_start("exchange_in_start", [dw_in, d_meta_b, d_norm_mix],
                                     gather=[False, False, True])

    shard_out = {}
    (rep_parts,) = _send_wait("gather_small_grads_wait", ex_mix, (4,), ex_in_token)
    rep_sum = _sum_parts("sum_small_grads", rep_parts)
    *rep_g, loss = _unpack(rep_sum, rep_shapes)
    loss = loss.reshape(())
    swapped = ("ssm_b_re", "ssm_b_im")
    to2d = lambda n, a: ghp(a) if n in swapped else a.reshape(rep2d[n])
    from2d = lambda n, a: (a.reshape(1, n_groups, grp, n_state).transpose(0, 1, 3, 2) if n in swapped
                           else a.reshape(weights[n].shape))
    as2d = lambda tree: [to2d(n, tree[n]) for n in rep_names]
    rep_res = _adamw_many("adamw_replicated", rep_g, as2d(weights), as2d(mom_m), as2d(mom_v))
    for i, n in enumerate(rep_names):
        shard_out[n] = [from2d(n, r) for r in (rep_g[i], *(res[i] for res in rep_res))]

    def update(n, parts, transposed=False):
        sh = weights[n].shape
        two_d = lambda a: (jnp.swapaxes(a[0], 0, 1) if transposed else a.reshape(parts.shape[1:]))
        res = _adamw_reduce("adamw_" + n, parts, two_d(weights[n]), two_d(mom_m[n]), two_d(mom_v[n]))
        shard_out[n] = [(jnp.swapaxes(r, 0, 1) if transposed else r).reshape(sh) for r in res]
        return res[0]

    (p_down,) = _send_wait("exchange_down_wait", ex_down, (0,), rep_sum)
    done = update("w_down", p_down)
    (p_up,) = _send_wait("exchange_up_wait", ex_up, (0,), done)
    done = update("w_up", p_up, transposed=True)
    p_out, p_glu, p_cw, p_fw = _send_wait("exchange_mix_wait", ex_mix, (0, 1, 2, 3), done)
    update("w_out", p_out)
    update("ssm_w_glu", p_glu)
    update("conv_w", p_cw)
    done = update("ffn_conv_w", p_fw)
    p_in, p_meta, p_nm = _send_wait("exchange_in_wait", ex_in, (0, 1, 2), done)
    update("w_in", p_in)
    update("meta_tokens", p_meta)
    update("norm_mix_g", p_nm)

    grads = [shard_out[n][0] for n in names]
    deltas = [shard_out[n][1] for n in names]
    new_m = [shard_out[n][2] for n in names]
    new_v = [shard_out[n][3] for n in names]
    return (loss, grad_x, *grads, *deltas, *new_m, *new_v)
```

```python
import functools
import math

import jax
import jax.numpy as jnp
from jax import lax
from jax.experimental import pallas as pl
from jax.experimental.pallas import tpu as pltpu

F32 = jnp.float32
BF16 = jnp.bfloat16
MESH = pl.DeviceIdType.MESH

N_DEV = 8
RMS_EPS = 1e-6
ADAM_LR = 0.001
ADAM_B1 = 0.9
ADAM_B2 = 0.999
ADAM_EPS = 1e-08
ADAM_WD = 0.01
ADAM_STEP = 10
ADAM_BC1 = 1.0 - ADAM_B1 ** ADAM_STEP
ADAM_BC2 = 1.0 - ADAM_B2 ** ADAM_STEP

SUBLANES = 8
LANES = 128
ROW_TILE = 128
ROW_CHUNK = 32
WIDE_TILES = (544, 256, 128)
FFN_COLS = 256
STAGE_COLS = 512
N_SEG = 8
HALO_ROWS = 16
SSM_CHUNK = 128
SCAN_UNROLL = 2
VMEM_LIMIT = 48 * 1024 * 1024

NN = ((1,), (0,))
NT = ((1,), (1,))
TN = ((0,), (0,))


def _params(**kw):
    return pltpu.CompilerParams(vmem_limit_bytes=VMEM_LIMIT, **kw)


def _dot(a, b, dims):
    return lax.dot_general(a, b, (dims, ((), ())), preferred_element_type=F32)


def _mean_sq_rsqrt(x):
    return lax.rsqrt(jnp.mean(x * x, axis=-1, keepdims=True) + RMS_EPS)


def _rms_bwd(x, r, g, dy):
    xhat = x * r
    dxh = dy * g
    dx = r * (dxh - xhat * jnp.mean(dxh * xhat, axis=-1, keepdims=True))
    return dx, dy * xhat


def _gelu(y):
    c = math.sqrt(2.0 / math.pi)
    t = jnp.tanh(c * (y + 0.044715 * y * y * y))
    return 0.5 * y * (1.0 + t), t


def _gelu_grad(y, t):
    c = math.sqrt(2.0 / math.pi)
    return 0.5 * (1.0 + t) + 0.5 * y * (1.0 - t * t) * c * (1.0 + 3.0 * 0.044715 * y * y)


def _wrap_prev_halo(halo, first):
    seg = lax.broadcasted_iota(jnp.int32, halo.shape, 0) % N_SEG
    wrapped = jnp.where(seg == 0, 0.0, pltpu.roll(halo, 1, 0))
    return jnp.where(first, wrapped, halo)


def _wrap_next_halo(halo, last):
    seg = lax.broadcasted_iota(jnp.int32, halo.shape, 0) % N_SEG
    wrapped = jnp.where(seg == N_SEG - 1, 0.0, pltpu.roll(halo, halo.shape[0] - 1, 0))
    return jnp.where(last, wrapped, halo)


def _tile_rows(get, prev, nxt, s, n, tile_rows=ROW_TILE):
    parts = []
    if s < 0:
        parts.append(prev[HALO_ROWS + s:HALO_ROWS + min(s + n, 0)])
    lo, hi = max(s, 0), min(s + n, tile_rows)
    if hi > lo:
        parts.append(get(lo, hi - lo))
    if s + n > tile_rows:
        parts.append(nxt[max(s - tile_rows, 0):s + n - tile_rows])
    return parts[0] if len(parts) == 1 else jnp.concatenate(parts, axis=0)


def _dev_index(p):
    return 4 * p[0] + 2 * p[1] + p[2]


def _allgather(name, shards, deps=()):
    n = len(shards)

    def body(*refs):
        ins, outs = refs[:n], refs[n:2 * n]
        send_sems, recv_sems, local_sems = refs[2 * n:]
        x, y, c = lax.axis_index("x"), lax.axis_index("y"), lax.axis_index("c")
        me, sibling = (x, y, c), (x, y, 1 - c)
        chips = [(1 - x, y), (x, 1 - y), (1 - x, 1 - y)]

        def copy(a, k, block, to, src=None):
            dst = outs[a].at[_dev_index(block)]
            return pltpu.make_async_remote_copy(
                src_ref=dst if src is None else src, dst_ref=dst,
                send_sem=send_sems.at[a, k], recv_sem=recv_sems.at[a, k],
                device_id=to, device_id_type=MESH)

        mine = [pltpu.make_async_copy(ins[a], outs[a].at[_dev_index(me)], local_sems.at[a])
                for a in range(n)]
        for cp in mine:
            cp.start()
        first = []
        for a in range(n):
            first.append(copy(a, 0, me, sibling, src=ins[a]))
            for j, chip in enumerate(chips):
                first.append(copy(a, 1 + j, me, (*chip, c), src=ins[a]))
        for cp in first:
            cp.start()
        passed = []
        for j, chip in enumerate(chips):
            for a in range(n):
                copy(a, 1 + j, (*chip, c), me).wait_recv()
                fwd = copy(a, 4 + j, (*chip, c), sibling)
                fwd.start()
                passed.append(fwd)
        for a in range(n):
            copy(a, 0, sibling, me).wait_recv()
            for j, chip in enumerate(chips):
                copy(a, 4 + j, (*chip, 1 - c), me).wait_recv()
        for cp in first + passed:
            cp.wait_send()
        for cp in mine:
            cp.wait()

    any_spec = pl.BlockSpec(memory_space=pl.ANY)
    return pl.pallas_call(
        _ignoring_deps(body, n, deps), name=name,
        out_shape=[jax.ShapeDtypeStruct((N_DEV,) + s.shape, s.dtype) for s in shards],
        in_specs=[any_spec] * (n + len(deps)), out_specs=[any_spec] * n,
        scratch_shapes=[pltpu.SemaphoreType.DMA((n, 7)), pltpu.SemaphoreType.DMA((n, 7)),
                        pltpu.SemaphoreType.DMA((n,))],
    )(*shards, *deps)


HBM_SPEC = pl.BlockSpec(memory_space=pltpu.HBM)
SEM_SPEC = pl.BlockSpec(memory_space=pltpu.SEMAPHORE)
ANY_SPEC = pl.BlockSpec(memory_space=pl.ANY)
DATAFLOW = pltpu.SideEffectType.DATAFLOW_SIDE_EFFECTING


def _ignoring_deps(body, n_in, deps):
    n_dep = len(deps)

    def wrapped(*refs):
        return body(*refs[:n_in], *refs[n_in + n_dep:])

    return wrapped


def _my_position():
    x, y, c = lax.axis_index("x"), lax.axis_index("y"), lax.axis_index("c")
    return (x, y, c)


def _peer(me, k):
    return tuple((1 - v) if (k >> s) & 1 else v for v, s in zip(me, (2, 1, 0)))


def _split_copies(src_refs, land_refs, send_sems, recv_sems, gather):
    me = _my_position()
    copies = []
    for a, (src, land) in enumerate(zip(src_refs, land_refs)):
        for k in range(1, N_DEV):
            peer = _peer(me, k)
            copies.append(pltpu.make_async_remote_copy(
                src_ref=src if gather[a] else src.at[_dev_index(peer)], dst_ref=land.at[_dev_index(me)],
                send_sem=send_sems[a].at[k - 1], recv_sem=recv_sems[a].at[k - 1],
                device_id=peer, device_id_type=MESH))
    return copies


def _own_slot(block, like_shape):
    me = _dev_index(_my_position())
    return lax.dynamic_update_index_in_dim(lax.empty(like_shape, block.dtype), block, me, 0)


def _send_start(name, srcs, gather):
    n = len(srcs)
    me = _dev_index(_my_position())
    gather = [gather] * n if isinstance(gather, bool) else list(gather)
    lands = [_own_slot(s, (N_DEV,) + s.shape) if g else
             _own_slot(lax.dynamic_index_in_dim(s, me, 0, keepdims=False), s.shape)
             for s, g in zip(srcs, gather)]

    def body(*refs):
        src_refs, land_refs = refs[:n], refs[n:2 * n]
        send_sems, recv_sems = refs[2 * n:3 * n], refs[3 * n:4 * n]
        token = refs[-1]
        for cp in _split_copies(src_refs, land_refs, send_sems, recv_sems, gather):
            cp.start()
        token[...] = jnp.zeros_like(token)

    hbm = lambda a: pltpu.HBM(a.shape, a.dtype)
    sems = [pltpu.SemaphoreType.DMA((N_DEV - 1,))] * n
    outs = pl.pallas_call(
        body, name=name,
        out_shape=(*sems, *sems, *[hbm(s) for s in srcs], *[hbm(l) for l in lands],
                   jax.ShapeDtypeStruct((SUBLANES, LANES), F32)),
        in_specs=[HBM_SPEC] * (2 * n),
        out_specs=(*[SEM_SPEC] * (2 * n), *[HBM_SPEC] * (2 * n), pl.BlockSpec(memory_space=pltpu.VMEM)),
        input_output_aliases={i: 2 * n + i for i in range(2 * n)},
        compiler_params=pltpu.CompilerParams(has_side_effects=DATAFLOW),
    )(*[pltpu.with_memory_space_constraint(a, pltpu.HBM) for a in (*srcs, *lands)])
    state = dict(send=outs[:n], recv=outs[n:2 * n], srcs=outs[2 * n:3 * n], lands=outs[3 * n:4 * n],
                 gather=gather)
    return state, outs[-1]


def _send_wait(name, state, which, after):
    n = len(which)
    pick = lambda key: [state[key][i] for i in which]
    gather = pick("gather")

    def body(*refs):
        src_refs, land_refs = refs[:n], refs[n:2 * n]
        send_sems, recv_sems = refs[2 * n:3 * n], refs[3 * n:4 * n]
        for cp in _split_copies(src_refs, land_refs, send_sems, recv_sems, gather):
            cp.wait_send()
            cp.wait_recv()

    srcs, lands = pick("srcs"), pick("lands")
    hbm = lambda a: pltpu.HBM(a.shape, a.dtype)
    outs = pl.pallas_call(
        body, name=name,
        out_shape=(*[hbm(s) for s in srcs], *[hbm(l) for l in lands]),
        in_specs=[*[HBM_SPEC] * (2 * n), *[SEM_SPEC] * (2 * n), ANY_SPEC],
        out_specs=tuple([HBM_SPEC] * (2 * n)),
        input_output_aliases={i: i for i in range(2 * n)},
        compiler_params=pltpu.CompilerParams(has_side_effects=DATAFLOW),
    )(*srcs, *lands, *pick("send"), *pick("recv"), after)
    return outs[n:]


def _two_level_copies(src_refs, land_refs, sems1, sems2):
    x, y, c = _my_position()
    me, sibling = (x, y, c), (x, y, 1 - c)
    chips = [(1 - x, y), (x, 1 - y), (1 - x, 1 - y)]
    stage1, stage2 = [], []
    for a, land in enumerate(land_refs):
        def copy(src, block, to, sems, k):
            return pltpu.make_async_remote_copy(
                src_ref=src, dst_ref=land.at[_dev_index(block)], send_sem=sems[0][a].at[k],
                recv_sem=sems[1][a].at[k], device_id=to, device_id_type=MESH)
        src = src_refs[a] if src_refs is not None else land.at[_dev_index(me)]
        stage1.append([copy(src, me, sibling, sems1, 0)] +
                      [copy(src, me, (*chip, c), sems1, 1 + j) for j, chip in enumerate(chips)])
        if sems2 is not None:
            stage2.append([copy(land.at[_dev_index((*chip, c))], (*chip, c), sibling, sems2, j)
                           for j, chip in enumerate(chips)])
    return stage1, stage2


def _gather2_start(name, shards):
    n = len(shards)
    lands = [_own_slot(s, (N_DEV,) + s.shape) for s in shards]

    def body(*refs):
        src_refs, land_refs = refs[:n], refs[n:2 * n]
        sems1 = (refs[2 * n:3 * n], refs[3 * n:4 * n])
        stage1, _ = _two_level_copies(src_refs, land_refs, sems1, None)
        for copies in stage1:
            for cp in copies:
                cp.start()
        refs[-1][...] = jnp.zeros_like(refs[-1])

    hbm = lambda a: pltpu.HBM(a.shape, a.dtype)
    sems = [pltpu.SemaphoreType.DMA((4,))] * n
    outs = pl.pallas_call(
        body, name=name,
        out_shape=(*sems, *sems, *[hbm(s) for s in shards], *[hbm(l) for l in lands],
                   jax.ShapeDtypeStruct((SUBLANES, LANES), F32)),
        in_specs=[HBM_SPEC] * (2 * n),
        out_specs=(*[SEM_SPEC] * (2 * n), *[HBM_SPEC] * (2 * n), pl.BlockSpec(memory_space=pltpu.VMEM)),
        input_output_aliases={i: 2 * n + i for i in range(2 * n)},
        compiler_params=pltpu.CompilerParams(has_side_effects=DATAFLOW),
    )(*[pltpu.with_memory_space_constraint(a, pltpu.HBM) for a in (*shards, *lands)])
    state = dict(send1=list(outs[:n]), recv1=list(outs[n:2 * n]), srcs=list(outs[2 * n:3 * n]),
                 lands=list(outs[3 * n:4 * n]), send2={}, recv2={})
    return state, outs[-1]


def _gather2_forward(name, state, which, after):
    n = len(which)
    pick = lambda key: [state[key][i] for i in which]

    def body(*refs):
        land_refs, recv1 = refs[:n], refs[n:2 * n]
        outs = refs[2 * n + len(after):]
        sems2 = (outs[:n], outs[n:2 * n])
        stage1, stage2 = _two_level_copies(None, land_refs, (recv1, recv1), sems2)
        for a in range(n):
            for j in range(3):
                stage1[a][1 + j].wait_recv()
                stage2[a][j].start()
        outs[-1][...] = jnp.zeros_like(outs[-1])

    lands = pick("lands")
    sems = [pltpu.SemaphoreType.DMA((3,))] * n
    outs = pl.pallas_call(
        body, name=name,
        out_shape=(*sems, *sems, *[pltpu.HBM(l.shape, l.dtype) for l in lands],
                   jax.ShapeDtypeStruct((SUBLANES, LANES), F32)),
        in_specs=[*[HBM_SPEC] * n, *[SEM_SPEC] * n, *[ANY_SPEC] * len(after)],
        out_specs=(*[SEM_SPEC] * (2 * n), *[HBM_SPEC] * n, pl.BlockSpec(memory_space=pltpu.VMEM)),
        input_output_aliases={i: 2 * n + i for i in range(n)},
        compiler_params=pltpu.CompilerParams(has_side_effects=DATAFLOW),
    )(*lands, *pick("recv1"), *after)
    for idx, i in enumerate(which):
        state["send2"][i], state["recv2"][i] = outs[idx], outs[n + idx]
        state["lands"][i] = outs[2 * n + idx]
    return outs[-1]


def _gather2_wait(name, state, which, after):
    n = len(which)
    pick = lambda key: [state[key][i] for i in which]

    def body(*refs):
        src_refs, land_refs = refs[:n], refs[n:2 * n]
        sems1 = (refs[2 * n:3 * n], refs[3 * n:4 * n])
        sems2 = (refs[4 * n:5 * n], refs[5 * n:6 * n])
        stage1, stage2 = _two_level_copies(src_refs, land_refs, sems1, sems2)
        for a in range(n):
            for cp in stage1[a]:
                cp.wait_send()
            stage1[a][0].wait_recv()
            for cp in stage2[a]:
                cp.wait_send()
                cp.wait_recv()

    srcs, lands = pick("srcs"), pick("lands")
    hbm = lambda a: pltpu.HBM(a.shape, a.dtype)
    outs = pl.pallas_call(
        body, name=name,
        out_shape=(*[hbm(s) for s in srcs], *[hbm(l) for l in lands]),
        in_specs=[*[HBM_SPEC] * (2 * n), *[SEM_SPEC] * (4 * n), ANY_SPEC],
        out_specs=tuple([HBM_SPEC] * (2 * n)),
        input_output_aliases={i: i for i in range(2 * n)},
        compiler_params=pltpu.CompilerParams(has_side_effects=DATAFLOW),
    )(*srcs, *lands, *pick("send1"), *pick("recv1"), *pick("send2"), *pick("recv2"), after)
    return outs[n:]


def _mm(name, a, b, *, dims, grid, a_spec, b_spec, o_spec, out_shape, acc_shape=None,
        res=None, res_spec=None):
    n_red = grid[-1] if acc_shape is not None else 1
    red_axis = len(grid) - 1

    def body(*refs):
        a_ref, b_ref = refs[0], refs[1]
        r_ref = refs[2] if res is not None else None
        o_ref = refs[3] if res is not None else refs[2]
        part = _dot(a_ref[...], b_ref[...], dims)
        if acc_shape is None:
            if r_ref is not None:
                part = part + r_ref[...]
            o_ref[...] = part.astype(o_ref.dtype)
            return
        acc_ref = refs[-1]
        k = pl.program_id(red_axis)

        @pl.when(k == 0)
        def _():
            acc_ref[...] = part

        @pl.when(k > 0)
        def _():
            acc_ref[...] += part

        @pl.when(k == n_red - 1)
        def _():
            total = acc_ref[...]
            if r_ref is not None:
                total = total + r_ref[...]
            o_ref[...] = total.astype(o_ref.dtype)

    ins, in_specs = [a, b], [a_spec, b_spec]
    if res is not None:
        ins.append(res)
        in_specs.append(res_spec)
    return pl.pallas_call(
        body, name=name, grid=grid, in_specs=in_specs, out_specs=o_spec, out_shape=out_shape,
        scratch_shapes=[pltpu.VMEM(acc_shape, F32)] if acc_shape is not None else [],
        compiler_params=_params(),
    )(*ins)


def _wide_tile(t):
    return next(c for c in WIDE_TILES if t % c == 0)


def _rms_fwd(name, h, g):
    t, d = h.shape
    tile = _wide_tile(t)

    def body(h_ref, g_ref, o_ref):
        x = h_ref[...]
        o_ref[...] = (x * _mean_sq_rsqrt(x) * g_ref[...]).astype(BF16)

    return pl.pallas_call(
        body, name=name, grid=(t // tile,),
        in_specs=[pl.BlockSpec((tile, d), lambda i: (i, 0)), pl.BlockSpec((1, d), lambda i: (0, 0))],
        out_specs=pl.BlockSpec((tile, d), lambda i: (i, 0)),
        out_shape=jax.ShapeDtypeStruct((t, d), BF16), compiler_params=_params(),
    )(h, g)


def _rms_bwd_res(name, dres, dhn, h, g, deps=()):
    t, d = h.shape

    def body(dres_ref, dhn_ref, h_ref, g_ref, dh_ref, dhb_ref, dg_ref):
        x = h_ref[...]
        dx, dgt = _rms_bwd(x, _mean_sq_rsqrt(x), g_ref[...], dhn_ref[...])
        dh = dres_ref[...] + dx
        dh_ref[...] = dh
        dhb_ref[...] = dh.astype(BF16)

        @pl.when(pl.program_id(0) == 0)
        def _():
            dg_ref[...] = jnp.zeros_like(dg_ref)

        dg_ref[...] += jnp.sum(dgt, axis=0, keepdims=True)

    tile = _wide_tile(t)
    row = pl.BlockSpec((tile, d), lambda i: (i, 0))
    vec = pl.BlockSpec((1, d), lambda i: (0, 0))
    return pl.pallas_call(
        _ignoring_deps(body, 4, deps), name=name, grid=(t // tile,),
        in_specs=[row, row, row, vec] + [ANY_SPEC] * len(deps), out_specs=[row, row, vec],
        out_shape=[jax.ShapeDtypeStruct((t, d), F32), jax.ShapeDtypeStruct((t, d), BF16),
                   jax.ShapeDtypeStruct((1, d), F32)],
        compiler_params=_params(),
    )(dres, dhn, h, g, *deps)


def _loss_bwd(h2, target, g, row_lo, row_hi):
    t, d = h2.shape
    tile = _wide_tile(t)

    def body(h_ref, tg_ref, g_ref, dh_ref, dhb_ref, loss_ref, dg_ref):
        i = pl.program_id(0)
        x = h_ref[...]
        r = _mean_sq_rsqrt(x)
        gv = g_ref[...]
        y = x * r * gv
        rows = i * tile + lax.broadcasted_iota(jnp.int32, (tile, 1), 0)
        time = (rows % N_SEG) * (t // N_SEG) + rows // N_SEG
        valid = jnp.logical_and(time >= row_lo, time < row_hi)
        err = jnp.where(valid, y - tg_ref[...], 0.0)
        dy = err * (1.0 / d)
        dx, dgt = _rms_bwd(x, r, gv, dy)
        dh_ref[...] = dx
        dhb_ref[...] = dx.astype(BF16)

        @pl.when(i == 0)
        def _():
            loss_ref[...] = jnp.zeros_like(loss_ref)
            dg_ref[...] = jnp.zeros_like(dg_ref)

        row_loss = jnp.mean(err * err, axis=-1, keepdims=True)
        loss_ref[...] += 0.5 * jnp.sum(row_loss, axis=0, keepdims=True)
        dg_ref[...] += jnp.sum(dgt, axis=0, keepdims=True)

    row = pl.BlockSpec((tile, d), lambda i: (i, 0))
    vec = pl.BlockSpec((1, d), lambda i: (0, 0))
    return pl.pallas_call(
        body, name="loss_bwd", grid=(t // tile,), in_specs=[row, row, vec],
        out_specs=[row, row, pl.BlockSpec((1, 1), lambda i: (0, 0)), vec],
        out_shape=[jax.ShapeDtypeStruct((t, d), F32), jax.ShapeDtypeStruct((t, d), BF16),
                   jax.ShapeDtypeStruct((1, 1), F32), jax.ShapeDtypeStruct((1, d), F32)],
        compiler_params=_params(),
    )(h2, target, g)


def _prev_halo(i, t):
    return jnp.where(i == 0, t // HALO_ROWS - 1, i * (ROW_TILE // HALO_ROWS) - 1)


def _next_halo(i, t):
    return jnp.where(i == t // ROW_TILE - 1, 0, (i + 1) * (ROW_TILE // HALO_ROWS))


def _causal_taps(cur, halo, first):
    rows = cur.shape[0]
    ext = jnp.concatenate([_wrap_prev_halo(halo, first), cur], axis=0)
    return ext[:rows], ext[N_SEG:N_SEG + rows]


def _anticausal_taps(cur, halo, last):
    rows = cur.shape[0]
    ext = jnp.concatenate([cur, _wrap_next_halo(halo, last)], axis=0)
    return ext[N_SEG:N_SEG + rows], ext[2 * N_SEG:2 * N_SEG + rows]


def _mix_fwd(proj, y, w_glu, conv_w, gain_c, gain_s):
    t = proj.shape[0]
    dc = conv_w.shape[1]
    ds = y.shape[1]

    def body(p_ref, halo_ref, y_ref, wg_ref, cw_ref, gc_ref, gs_ref, mixed_ref, z_ref):
        i = pl.program_id(0)
        p = p_ref[...]
        b, c, v = p[:, :dc], p[:, dc:2 * dc], p[:, 2 * dc:3 * dc]
        cv = c * v
        hp = halo_ref[...]
        x2, x1 = _causal_taps(cv, hp[:, dc:2 * dc] * hp[:, 2 * dc:3 * dc], i == 0)
        cw = cw_ref[...]
        conv = cw[0:1] * x2 + cw[1:2] * x1 + cw[2:3] * cv
        co = b * conv
        mixed_ref[:, :dc] = (co * _mean_sq_rsqrt(co) * gc_ref[...]).astype(BF16)
        g, _ = _gelu(y_ref[...])
        z = _dot(g.astype(BF16), wg_ref[...], NN)
        z_ref[...] = z
        so = g * jax.nn.sigmoid(z)
        mixed_ref[:, dc:] = (so * _mean_sq_rsqrt(so) * gs_ref[...]).astype(BF16)

    const = lambda i: (0, 0)
    return pl.pallas_call(
        body, name="mix_fwd", grid=(t // ROW_TILE,),
        in_specs=[pl.BlockSpec((ROW_TILE, 3 * dc), lambda i: (i, 0)),
                  pl.BlockSpec((HALO_ROWS, 3 * dc), lambda i: (_prev_halo(i, t), 0)),
                  pl.BlockSpec((ROW_TILE, ds), lambda i: (i, 0)),
                  pl.BlockSpec((ds, ds), const), pl.BlockSpec(conv_w.shape, const),
                  pl.BlockSpec((1, dc), const), pl.BlockSpec((1, ds), const)],
        out_specs=[pl.BlockSpec((ROW_TILE, dc + ds), lambda i: (i, 0)),
                   pl.BlockSpec((ROW_TILE, ds), lambda i: (i, 0))],
        out_shape=[jax.ShapeDtypeStruct((t, dc + ds), BF16), jax.ShapeDtypeStruct((t, ds), F32)],
        compiler_params=_params(),
    )(proj, proj, y, w_glu, conv_w, gain_c, gain_s)


def _mix_bwd1(proj, y, z, dmixed, w_glu, conv_w, gain_c, gain_s):
    t = proj.shape[0]
    dc = conv_w.shape[1]
    ds = y.shape[1]

    def body(p_ref, halo_ref, y_ref, z_ref, dm_ref, wg_ref, cw_ref, gc_ref, gs_ref,
             db_ref, dconv_ref, dy_ref, dwg_ref, dcw_ref, dgc_ref, dgs_ref):
        i = pl.program_id(0)

        @pl.when(i == 0)
        def _():
            dwg_ref[...] = jnp.zeros_like(dwg_ref)
            dcw_ref[...] = jnp.zeros_like(dcw_ref)
            dgc_ref[...] = jnp.zeros_like(dgc_ref)
            dgs_ref[...] = jnp.zeros_like(dgs_ref)

        p = p_ref[...]
        b, c, v = p[:, :dc], p[:, dc:2 * dc], p[:, 2 * dc:3 * dc]
        cv = c * v
        hp = halo_ref[...]
        x2, x1 = _causal_taps(cv, hp[:, dc:2 * dc] * hp[:, 2 * dc:3 * dc], i == 0)
        cw = cw_ref[...]
        conv = cw[0:1] * x2 + cw[1:2] * x1 + cw[2:3] * cv
        co = b * conv
        dm = dm_ref[...]
        dco, dgt = _rms_bwd(co, _mean_sq_rsqrt(co), gc_ref[...], dm[:, :dc])
        dgc_ref[...] += jnp.sum(dgt, axis=0, keepdims=True)
        db_ref[...] = (dco * conv).astype(BF16)
        dconv = dco * b
        dconv_ref[...] = dconv
        dcw_ref[0:1, :] += jnp.sum(dconv * x2, axis=0, keepdims=True)
        dcw_ref[1:2, :] += jnp.sum(dconv * x1, axis=0, keepdims=True)
        dcw_ref[2:3, :] += jnp.sum(dconv * cv, axis=0, keepdims=True)

        yv = y_ref[...]
        g, th = _gelu(yv)
        sg = jax.nn.sigmoid(z_ref[...])
        so = g * sg
        dso, dgt = _rms_bwd(so, _mean_sq_rsqrt(so), gs_ref[...], dm[:, dc:])
        dgs_ref[...] += jnp.sum(dgt, axis=0, keepdims=True)
        dz = (dso * g * sg * (1.0 - sg)).astype(BF16)
        dg = dso * sg + _dot(dz, wg_ref[...], NT)
        dwg_ref[...] += _dot(g.astype(BF16), dz, TN)
        dy_ref[...] = dg * _gelu_grad(yv, th)

    const = lambda i: (0, 0)
    row = lambda w: pl.BlockSpec((ROW_TILE, w), lambda i: (i, 0))
    return pl.pallas_call(
        body, name="mix_bwd1", grid=(t // ROW_TILE,),
        in_specs=[row(3 * dc), pl.BlockSpec((HALO_ROWS, 3 * dc), lambda i: (_prev_halo(i, t), 0)),
                  row(ds), row(ds), row(dc + ds), pl.BlockSpec((ds, ds), const),
                  pl.BlockSpec(conv_w.shape, const), pl.BlockSpec((1, dc), const),
                  pl.BlockSpec((1, ds), const)],
        out_specs=[row(dc), row(dc), row(ds), pl.BlockSpec((ds, ds), const),
                   pl.BlockSpec(conv_w.shape, const), pl.BlockSpec((1, dc), const),
                   pl.BlockSpec((1, ds), const)],
        out_shape=[jax.ShapeDtypeStruct((t, dc), BF16), jax.ShapeDtypeStruct((t, dc), F32),
                   jax.ShapeDtypeStruct((t, ds), F32), jax.ShapeDtypeStruct((ds, ds), F32),
                   jax.ShapeDtypeStruct(conv_w.shape, F32), jax.ShapeDtypeStruct((1, dc), F32),
                   jax.ShapeDtypeStruct((1, ds), F32)],
        compiler_params=_params(),
    )(proj, proj, y, z, dmixed, w_glu, conv_w, gain_c, gain_s)


def _mix_bwd2(proj, dconv, conv_w, deps=()):
    t = proj.shape[0]
    dc = conv_w.shape[1]
    n_tiles = t // ROW_TILE

    def body(c_ref, v_ref, d_ref, halo_ref, cw_ref, o_ref):
        i = pl.program_id(0)
        d = d_ref[...]
        u1, u2 = _anticausal_taps(d, halo_ref[...], i == n_tiles - 1)
        cw = cw_ref[...]
        dcv = cw[2:3] * d + cw[1:2] * u1 + cw[0:1] * u2
        o_ref[:, :dc] = (dcv * v_ref[...]).astype(BF16)
        o_ref[:, dc:] = (dcv * c_ref[...]).astype(BF16)

    return pl.pallas_call(
        _ignoring_deps(body, 5, deps), name="mix_bwd2", grid=(n_tiles,),
        in_specs=[pl.BlockSpec((ROW_TILE, dc), lambda i: (i, 1)),
                  pl.BlockSpec((ROW_TILE, dc), lambda i: (i, 2)),
                  pl.BlockSpec((ROW_TILE, dc), lambda i: (i, 0)),
                  pl.BlockSpec((HALO_ROWS, dc), lambda i: (_next_halo(i, t), 0)),
                  pl.BlockSpec(conv_w.shape, lambda i: (0, 0))] + [ANY_SPEC] * len(deps),
        out_specs=pl.BlockSpec((ROW_TILE, 2 * dc), lambda i: (i, 0)),
        out_shape=jax.ShapeDtypeStruct((t, 2 * dc), BF16), compiler_params=_params(),
    )(proj, proj, dconv, dconv, conv_w, *deps)


def _stage_rows(stage_ref, row0, src, wrap=None):
    n, width = src.shape
    for c in range(0, width, STAGE_COLS):
        v = src[:, pl.ds(c, STAGE_COLS)].astype(F32)
        stage_ref[pl.ds(row0, n), pl.ds(c, STAGE_COLS)] = v if wrap is None else wrap(v)


def _conv_taps(stage_ref, fw_ref, fb_ref, c0, r, rows=ROW_CHUNK):
    cols = pl.ds(c0, FFN_COLS)
    x0, x1, x2 = (stage_ref[pl.ds(HALO_ROWS + r - k * N_SEG, rows), cols] for k in range(3))
    w = fw_ref[:, cols]
    return w[0:1] * x2 + w[1:2] * x1 + w[2:3] * x0 + fb_ref[:, cols], x2, x1, x0


def _gated_fwd(up_pre, fw, fb):
    t, f2 = up_pre.shape
    f = f2 // 2

    def body(up_ref, halo_ref, fw_ref, fb_ref, act_ref, stage_ref):
        first = pl.program_id(0) == 0
        _stage_rows(stage_ref, 0, halo_ref, lambda v: _wrap_prev_halo(v, first))
        _stage_rows(stage_ref, HALO_ROWS, up_ref)
        for c0 in range(0, f, FFN_COLS):
            for r in range(0, ROW_TILE, ROW_CHUNK):
                a = _conv_taps(stage_ref, fw_ref, fb_ref, c0, r)[0]
                val = _conv_taps(stage_ref, fw_ref, fb_ref, f + c0, r)[0]
                act_ref[pl.ds(r, ROW_CHUNK), pl.ds(c0, FFN_COLS)] = (a * jax.nn.sigmoid(a) * val).astype(BF16)

    const = lambda a: pl.BlockSpec(a.shape, lambda i: (0, 0))
    return pl.pallas_call(
        body, name="ffn_fwd", grid=(t // ROW_TILE,),
        in_specs=[pl.BlockSpec((ROW_TILE, f2), lambda i: (i, 0)),
                  pl.BlockSpec((HALO_ROWS, f2), lambda i: (_prev_halo(i, t), 0)), const(fw), const(fb)],
        out_specs=pl.BlockSpec((ROW_TILE, f), lambda i: (i, 0)),
        out_shape=jax.ShapeDtypeStruct((t, f), BF16),
        scratch_shapes=[pltpu.VMEM((HALO_ROWS + ROW_TILE, f2), F32)], compiler_params=_params(),
    )(up_pre, up_pre, fw, fb)


def _gated_bwd(up_pre, dact, fw, fb, deps=()):
    t, f2 = up_pre.shape
    f = f2 // 2
    n_tiles = t // ROW_TILE
    chunks = [(r, ROW_CHUNK) for r in range(0, ROW_TILE, ROW_CHUNK)] + [(ROW_TILE, HALO_ROWS)]

    def body(up_ref, prev_ref, next_ref, dact_ref, dact_next_ref, fw_ref, fb_ref,
             out_ref, dfw_ref, dfb_ref, dup_ref, stage_ref):
        first = pl.program_id(0) == 0
        last = pl.program_id(0) == n_tiles - 1

        @pl.when(first)
        def _():
            dfw_ref[...] = jnp.zeros_like(dfw_ref)
            dfb_ref[...] = jnp.zeros_like(dfb_ref)

        _stage_rows(stage_ref, 0, prev_ref, lambda v: _wrap_prev_halo(v, first))
        _stage_rows(stage_ref, HALO_ROWS, up_ref)
        _stage_rows(stage_ref, HALO_ROWS + ROW_TILE, next_ref, lambda v: _wrap_next_halo(v, last))
        fold = lambda v: sum(v[s:s + SUBLANES] for s in range(0, ROW_CHUNK, SUBLANES))
        for c0 in range(0, f, FFN_COLS):
            starts = (c0, f + c0)
            dact_next = _wrap_next_halo(dact_next_ref[:, pl.ds(c0, FFN_COLS)].astype(F32), last)
            get_dact = lambda s, n: dact_ref[pl.ds(s, n), pl.ds(c0, FFN_COLS)].astype(F32)
            sums = [[jnp.zeros((SUBLANES, FFN_COLS), F32)] * 4 for _ in starts]
            for r, rows in chunks:
                taps = [_conv_taps(stage_ref, fw_ref, fb_ref, c, r, rows) for c in starts]
                a, val = taps[0][0], taps[1][0]
                da_ct = _tile_rows(get_dact, None, dact_next, r, rows)
                sg = jax.nn.sigmoid(a)
                dup = (da_ct * val * sg * (1.0 + a * (1.0 - sg)), da_ct * a * sg)
                for k in range(2):
                    dup_ref[k, pl.ds(r, rows), :] = dup[k]
                    if r < ROW_TILE:
                        terms = (dup[k], dup[k] * taps[k][1], dup[k] * taps[k][2], dup[k] * taps[k][3])
                        sums[k] = [s + fold(v) for s, v in zip(sums[k], terms)]
            for k, c in enumerate(starts):
                cols = pl.ds(c, FFN_COLS)
                s_b, s_w0, s_w1, s_w2 = (jnp.sum(p, axis=0, keepdims=True) for p in sums[k])
                dfb_ref[:, cols] += s_b
                for tap, s_w in enumerate((s_w0, s_w1, s_w2)):
                    dfw_ref[tap:tap + 1, cols] += s_w
                cw = fw_ref[:, cols]
                for r in range(0, ROW_TILE, ROW_CHUNK):
                    d, u1, u2 = (dup_ref[k, pl.ds(r + s * N_SEG, ROW_CHUNK), :] for s in range(3))
                    out_ref[pl.ds(r, ROW_CHUNK), cols] = (cw[2:3] * d + cw[1:2] * u1 + cw[0:1] * u2).astype(BF16)

    tile = lambda w: pl.BlockSpec((ROW_TILE, w), lambda i: (i, 0))
    halo = lambda w, index: pl.BlockSpec((HALO_ROWS, w), lambda i: (index(i, t), 0))
    const = lambda a: pl.BlockSpec(a.shape, lambda i: (0, 0))
    return pl.pallas_call(
        _ignoring_deps(body, 7, deps), name="ffn_bwd", grid=(n_tiles,),
        in_specs=[tile(f2), halo(f2, _prev_halo), halo(f2, _next_halo), tile(f), halo(f, _next_halo),
                  const(fw), const(fb)] + [ANY_SPEC] * len(deps),
        out_specs=[tile(f2), const(fw), const(fb)],
        out_shape=[jax.ShapeDtypeStruct((t, f2), BF16), jax.ShapeDtypeStruct(fw.shape, F32),
                   jax.ShapeDtypeStruct(fb.shape, F32)],
        scratch_shapes=[pltpu.VMEM((2, ROW_TILE + HALO_ROWS, FFN_COLS), F32),
                        pltpu.VMEM((HALO_ROWS + ROW_TILE + HALO_ROWS, f2), F32)],
        compiler_params=_params(),
    )(up_pre, up_pre, up_pre, dact, dact, fw, fb, *deps)


def _to_segments(a):
    t, c = a.shape
    return a.reshape(N_SEG, t // N_SEG, c).transpose(1, 0, 2).reshape(t, c)


def _from_segments(a):
    t, c = a.shape
    return a.reshape(t // N_SEG, N_SEG, c).transpose(1, 0, 2).reshape(t, c)


def _cmul(ar, ai, br, bi):
    return ar * br - ai * bi, ar * bi + ai * br


def _cpow(ar, ai, n):
    out = None
    while n:
        if n & 1:
            out = (ar, ai) if out is None else _cmul(out[0], out[1], ar, ai)
        ar, ai = _cmul(ar, ai, ar, ai)
        n >>= 1
    return out


def _segment_carries(pr, pi, fr, fi, forward):
    row = lax.broadcasted_iota(jnp.int32, fr.shape, 0)
    edge = row == (0 if forward else N_SEG - 1)
    shift = 1 if forward else N_SEG - 1
    sr, si = jnp.zeros_like(fr), jnp.zeros_like(fi)
    for _ in range(N_SEG - 1):
        tr, ti = _cmul(pr, pi, sr, si)
        sr = jnp.where(edge, 0.0, pltpu.roll(tr + fr, shift, 0))
        si = jnp.where(edge, 0.0, pltpu.roll(ti + fi, shift, 0))
    return sr, si


def _rows(i):
    return pl.ds(pl.multiple_of(i * SUBLANES, SUBLANES), SUBLANES)


def _s5_fwd(proj, u_col, bb_re, bb_im, a_re, a_im, cc_re, cc_im, d_skip):
    t = proj.shape[0]
    nch, _, cs = bb_re.shape
    ds = nch * SSM_CHUNK
    u_blk = u_col // SSM_CHUNK
    steps = t // N_SEG

    def body(u_ref, bbr_ref, bbi_ref, ar_ref, ai_ref, ccr_ref, cci_ref, d_ref, sr_ref, si_ref, y_ref):
        ub = u_ref[...].astype(BF16)
        sr_ref[...] = _dot(ub, bbr_ref[...], NN)
        si_ref[...] = _dot(ub, bbi_ref[...], NN)
        ar = jnp.broadcast_to(ar_ref[...], (N_SEG, cs))
        ai = jnp.broadcast_to(ai_ref[...], (N_SEG, cs))
        zero = jnp.zeros((N_SEG, cs), F32)

        def totals(i, carry):
            tr, ti = _cmul(ar, ai, *carry)
            return tr + sr_ref[_rows(i), :], ti + si_ref[_rows(i), :]

        fr, fi = lax.fori_loop(0, steps, totals, (zero, zero), unroll=SCAN_UNROLL)
        s0r, s0i = _segment_carries(*_cpow(ar, ai, steps), fr, fi, True)

        def scan(i, carry):
            tr, ti = _cmul(ar, ai, *carry)
            nr, ni = tr + sr_ref[_rows(i), :], ti + si_ref[_rows(i), :]
            sr_ref[_rows(i), :] = nr
            si_ref[_rows(i), :] = ni
            return nr, ni

        lax.fori_loop(0, steps, scan, (s0r, s0i), unroll=SCAN_UNROLL)
        y_ref[...] = (_dot(sr_ref[...].astype(BF16), ccr_ref[...], NN)
                      - _dot(si_ref[...].astype(BF16), cci_ref[...], NN)
                      + d_ref[...] * u_ref[...])

    chunk3 = lambda r, c: pl.BlockSpec((None, r, c), lambda j: (j, 0, 0))
    return pl.pallas_call(
        body, name="s5_fwd", grid=(nch,),
        in_specs=[pl.BlockSpec((t, SSM_CHUNK), lambda j: (0, j + u_blk)),
                  chunk3(SSM_CHUNK, cs), chunk3(SSM_CHUNK, cs), chunk3(1, cs), chunk3(1, cs),
                  chunk3(cs, SSM_CHUNK), chunk3(cs, SSM_CHUNK), chunk3(1, SSM_CHUNK)],
        out_specs=[pl.BlockSpec((t, cs), lambda j: (0, j)), pl.BlockSpec((t, cs), lambda j: (0, j)),
                   pl.BlockSpec((t, SSM_CHUNK), lambda j: (0, j))],
        out_shape=[jax.ShapeDtypeStruct((t, nch * cs), F32), jax.ShapeDtypeStruct((t, nch * cs), F32),
                   jax.ShapeDtypeStruct((t, ds), F32)],
        compiler_params=_params(),
    )(proj, bb_re, bb_im, a_re, a_im, cc_re, cc_im, d_skip)


def _s5_bwd(dy, proj, u_col, s_re, s_im, bb_re, bb_im, a_re, a_im, cc_re, cc_im, d_skip, gpc):
    t, ds = dy.shape
    nch, _, cs = bb_re.shape
    u_blk = u_col // SSM_CHUNK
    steps = t // N_SEG

    def body(dy_ref, u_ref, sr_ref, si_ref, bbr_ref, bbi_ref, ar_ref, ai_ref, ccr_ref, cci_ref, d_ref,
             du_ref, dbbr_ref, dbbi_ref, dar_ref, dai_ref, dccr_ref, dcci_ref, dd_ref, gr_ref, gi_ref):
        dyv = dy_ref[...]
        dyb = dyv.astype(BF16)
        gr_ref[...] = _dot(dyb, ccr_ref[...], NT)
        gi_ref[...] = -_dot(dyb, cci_ref[...], NT)
        ar = jnp.broadcast_to(ar_ref[...], (N_SEG, cs))
        ai = -jnp.broadcast_to(ai_ref[...], (N_SEG, cs))
        zero = jnp.zeros((N_SEG, cs), F32)

        def totals(k, carry):
            i = steps - 1 - k
            tr, ti = _cmul(ar, ai, *carry)
            return tr + gr_ref[_rows(i), :], ti + gi_ref[_rows(i), :]

        fr, fi = lax.fori_loop(0, steps, totals, (zero, zero), unroll=SCAN_UNROLL)
        e0r, e0i = _segment_carries(*_cpow(ar, ai, steps), fr, fi, False)

        def step(i, gr, gi, pr, pi, acc_r, acc_i):
            tr, ti = _cmul(ar, ai, gr, gi)
            nr, ni = tr + gr_ref[_rows(i), :], ti + gi_ref[_rows(i), :]
            gr_ref[_rows(i), :] = nr
            gi_ref[_rows(i), :] = ni
            return nr, ni, acc_r + nr * pr + ni * pi, acc_i + ni * pr - nr * pi

        def scan(k, carry):
            i = steps - 1 - k
            gr, gi, acc_r, acc_i = carry
            return step(i, gr, gi, sr_ref[_rows(i - 1), :], si_ref[_rows(i - 1), :], acc_r, acc_i)

        gr, gi, acc_r, acc_i = lax.fori_loop(0, steps - 1, scan, (e0r, e0i, zero, zero), unroll=SCAN_UNROLL)
        row = lax.broadcasted_iota(jnp.int32, (N_SEG, cs), 0)
        last = _rows(steps - 1)
        pr = jnp.where(row == 0, 0.0, pltpu.roll(sr_ref[last, :], 1, 0))
        pi = jnp.where(row == 0, 0.0, pltpu.roll(si_ref[last, :], 1, 0))
        _, _, acc_r, acc_i = step(0, gr, gi, pr, pi, acc_r, acc_i)
        dar_ref[...] = jnp.sum(acc_r, axis=0, keepdims=True)
        dai_ref[...] = jnp.sum(acc_i, axis=0, keepdims=True)

        uv = u_ref[...]
        ub = uv.astype(BF16)
        grb = gr_ref[...].astype(BF16)
        gib = gi_ref[...].astype(BF16)
        du = d_ref[...] * dyv + _dot(grb, bbr_ref[...], NT) + _dot(gib, bbi_ref[...], NT)
        du_ref[...] = du.astype(BF16)
        def put_groups(ref, full):
            for gl in range(gpc):
                ref[gl] = full[gl * hb:(gl + 1) * hb, gl * pb:(gl + 1) * pb]

        put_groups(dbbr_ref, _dot(ub, grb, TN))
        put_groups(dbbi_ref, _dot(ub, gib, TN))
        put_groups(dccr_ref, _dot(dyb, sr_ref[...].astype(BF16), TN))
        put_groups(dcci_ref, -_dot(dyb, si_ref[...].astype(BF16), TN))
        dd_ref[...] = jnp.sum(dyv * uv, axis=0, keepdims=True)

    hb, pb = SSM_CHUNK // gpc, cs // gpc
    groups = pl.BlockSpec((None, gpc, hb, pb), lambda j: (j, 0, 0, 0))
    groups_shape = jax.ShapeDtypeStruct((nch, gpc, hb, pb), F32)
    chunk3 = lambda r, c: pl.BlockSpec((None, r, c), lambda j: (j, 0, 0))
    cols = lambda w: pl.BlockSpec((t, w), lambda j: (0, j))
    return pl.pallas_call(
        body, name="s5_bwd", grid=(nch,),
        in_specs=[cols(SSM_CHUNK), pl.BlockSpec((t, SSM_CHUNK), lambda j: (0, j + u_blk)), cols(cs), cols(cs),
                  chunk3(SSM_CHUNK, cs), chunk3(SSM_CHUNK, cs), chunk3(1, cs), chunk3(1, cs),
                  chunk3(cs, SSM_CHUNK), chunk3(cs, SSM_CHUNK), chunk3(1, SSM_CHUNK)],
        out_specs=[cols(SSM_CHUNK), groups, groups, chunk3(1, cs), chunk3(1, cs), groups, groups,
                   chunk3(1, SSM_CHUNK)],
        out_shape=[jax.ShapeDtypeStruct((t, ds), BF16), groups_shape, groups_shape,
                   jax.ShapeDtypeStruct((nch, 1, cs), F32), jax.ShapeDtypeStruct((nch, 1, cs), F32),
                   groups_shape, groups_shape, jax.ShapeDtypeStruct((nch, 1, SSM_CHUNK), F32)],
        scratch_shapes=[pltpu.VMEM((t, cs), F32), pltpu.VMEM((t, cs), F32)],
        compiler_params=_params(),
    )(dy, proj, s_re, s_im, bb_re, bb_im, a_re, a_im, cc_re, cc_im, d_skip)


def _discretize(lr, li, log_dt, br, bi):
    dt = jnp.exp(log_dt)
    mag = jnp.exp(lr * dt)
    ang = li * dt
    a_re = mag * jnp.cos(ang)
    a_im = mag * jnp.sin(ang)
    den = lr * lr + li * li
    nr = a_re - 1.0
    f_re = (nr * lr + a_im * li) / den
    f_im = (a_im * lr - nr * li) / den
    return a_re, a_im, f_re * br - f_im * bi, f_re * bi + f_im * br


def _whole(shape):
    return pl.BlockSpec(shape, lambda: (0,) * len(shape))


def _disc_fwd(lr, li, log_dt, br, bi, deps=()):
    def body(lr_ref, li_ref, dt_ref, br_ref, bi_ref, ar_ref, ai_ref, bbr_ref, bbi_ref):
        outs = _discretize(lr_ref[...], li_ref[...], dt_ref[...], br_ref[...], bi_ref[...])
        for ref, val in zip((ar_ref, ai_ref, bbr_ref, bbi_ref), outs):
            ref[...] = val

    args = (lr, li, log_dt, br, bi)
    outs = (lr, lr, br, br)
    return pl.pallas_call(
        _ignoring_deps(body, 5, deps), name="disc_fwd",
        in_specs=[_whole(a.shape) for a in args] + [ANY_SPEC] * len(deps),
        out_specs=[_whole(a.shape) for a in outs],
        out_shape=[jax.ShapeDtypeStruct(a.shape, F32) for a in outs],
    )(*args, *deps)


def _disc_bwd(lr, li, log_dt, br, bi, dar, dai, dbbr, dbbi):
    def body(lr_ref, li_ref, dt_ref, br_ref, bi_ref, dar_ref, dai_ref, dbbr_ref, dbbi_ref,
             dlr_ref, dli_ref, ddt_ref, dbr_ref, dbi_ref):
        _, vjp = jax.vjp(_discretize, lr_ref[...], li_ref[...], dt_ref[...], br_ref[...], bi_ref[...])
        grads = vjp((dar_ref[...], dai_ref[...], dbbr_ref[...], dbbi_ref[...]))
        for ref, val in zip((dlr_ref, dli_ref, ddt_ref, dbr_ref, dbi_ref), grads):
            ref[...] = val

    args = (lr, li, log_dt, br, bi, dar, dai, dbbr, dbbi)
    outs = (lr, li, log_dt, br, bi)
    return pl.pallas_call(
        body, name="disc_bwd", in_specs=[_whole(a.shape) for a in args],
        out_specs=[_whole(a.shape) for a in outs],
        out_shape=[jax.ShapeDtypeStruct(a.shape, F32) for a in outs],
    )(*args)


def _adamw(w, g, m, v):
    m = ADAM_B1 * m + (1.0 - ADAM_B1) * g
    v = ADAM_B2 * v + (1.0 - ADAM_B2) * (g * g)
    m_hat = m / ADAM_BC1
    v_hat = v / ADAM_BC2
    delta = -ADAM_LR * (m_hat / (jnp.sqrt(v_hat) + ADAM_EPS) + ADAM_WD * w)
    return delta, m, v


def _adamw_reduce(name, parts, w, m, v):
    _, r, c = parts.shape
    tr = r
    for cand in (256, 176, 128):
        if r % cand == 0:
            tr = cand
            break

    def body(p_ref, w_ref, m_ref, v_ref, g_ref, d_ref, nm_ref, nv_ref):
        g = p_ref[0].astype(F32)
        for k in range(1, N_DEV):
            g = g + p_ref[k].astype(F32)
        delta, nm, nv = _adamw(w_ref[...], g, m_ref[...], v_ref[...])
        g_ref[...] = g
        d_ref[...] = delta
        nm_ref[...] = nm
        nv_ref[...] = nv

    blk = pl.BlockSpec((tr, c), lambda i: (i, 0))
    return pl.pallas_call(
        body, name=name, grid=(r // tr,),
        in_specs=[pl.BlockSpec((N_DEV, tr, c), lambda i: (0, i, 0)), blk, blk, blk],
        out_specs=[blk] * 4, out_shape=[jax.ShapeDtypeStruct((r, c), F32)] * 4,
        compiler_params=_params(),
    )(parts, w, m, v)


def _sum_parts(name, parts):
    _, r, c = parts.shape

    def body(p_ref, o_ref):
        g = p_ref[0]
        for k in range(1, N_DEV):
            g = g + p_ref[k]
        o_ref[...] = g

    return pl.pallas_call(
        body, name=name, in_specs=[_whole(parts.shape)], out_specs=_whole((r, c)),
        out_shape=jax.ShapeDtypeStruct((r, c), F32), compiler_params=_params(),
    )(parts)


def _adamw_many(name, grads, ws, ms, vs):
    n = len(grads)

    def body(*refs):
        ins, outs = refs[:4 * n], refs[4 * n:]
        for i in range(n):
            g, w, m, v = (ins[j * n + i][...] for j in range(4))
            for ref, val in zip((outs[i], outs[n + i], outs[2 * n + i]), _adamw(w, g, m, v)):
                ref[...] = val

    args = (*grads, *ws, *ms, *vs)
    outs = pl.pallas_call(
        body, name=name, in_specs=[_whole(a.shape) for a in args],
        out_specs=[_whole(a.shape) for a in ws] * 3,
        out_shape=[jax.ShapeDtypeStruct(a.shape, F32) for a in ws] * 3, compiler_params=_params(),
    )(*args)
    return outs[:n], outs[n:2 * n], outs[2 * n:]


def _pack(arrays, rows):
    flat = jnp.concatenate([a.reshape(-1) for a in arrays])
    return jnp.pad(flat, (0, rows * LANES - flat.shape[0])).reshape(rows, LANES)


def _unpack(packed, shapes):
    flat = packed.reshape(-1)
    out, off = [], 0
    for s in shapes:
        n = math.prod(s)
        out.append(flat[off:off + n].reshape(s))
        off += n
    return out


def _packed_rows(shapes):
    n = sum(math.prod(s) for s in shapes)
    return -(-n // (SUBLANES * LANES)) * SUBLANES


def _block_diag(x):
    j, g, r, c = x.shape
    eye = jnp.eye(g, dtype=x.dtype)
    return (x[:, :, :, None, :] * eye[None, :, None, :, None]).reshape(j, g * r, g * c)


def kernel(x, meta_tokens, norm_mix_g, w_in, conv_w, ssm_lam_re, ssm_lam_im, ssm_log_dt, ssm_b_re, ssm_b_im, ssm_c_re, ssm_c_im, ssm_d, ssm_w_glu, gain_conv_out, gain_ssm_out, w_out, norm_ffn_g, w_up, ffn_conv_w, ffn_conv_b, w_down, norm_final_g, loss_target, m_meta_tokens, m_norm_mix_g, m_w_in, m_conv_w, m_ssm_lam_re, m_ssm_lam_im, m_ssm_log_dt, m_ssm_b_re, m_ssm_b_im, m_ssm_c_re, m_ssm_c_im, m_ssm_d, m_ssm_w_glu, m_gain_conv_out, m_gain_ssm_out, m_w_out, m_norm_ffn_g, m_w_up, m_ffn_conv_w, m_ffn_conv_b, m_w_down, m_norm_final_g, v_meta_tokens, v_norm_mix_g, v_w_in, v_conv_w, v_ssm_lam_re, v_ssm_lam_im, v_ssm_log_dt, v_ssm_b_re, v_ssm_b_im, v_ssm_c_re, v_ssm_c_im, v_ssm_d, v_ssm_w_glu, v_gain_conv_out, v_gain_ssm_out, v_w_out, v_norm_ffn_g, v_w_up, v_ffn_conv_w, v_ffn_conv_b, v_w_down, v_norm_final_g):
    weights = dict(meta_tokens=meta_tokens, norm_mix_g=norm_mix_g, w_in=w_in, conv_w=conv_w, ssm_lam_re=ssm_lam_re, ssm_lam_im=ssm_lam_im, ssm_log_dt=ssm_log_dt, ssm_b_re=ssm_b_re, ssm_b_im=ssm_b_im, ssm_c_re=ssm_c_re, ssm_c_im=ssm_c_im, ssm_d=ssm_d, ssm_w_glu=ssm_w_glu, gain_conv_out=gain_conv_out, gain_ssm_out=gain_ssm_out, w_out=w_out, norm_ffn_g=norm_ffn_g, w_up=w_up, ffn_conv_w=ffn_conv_w, ffn_conv_b=ffn_conv_b, w_down=w_down, norm_final_g=norm_final_g)
    mom_m = dict(meta_tokens=m_meta_tokens, norm_mix_g=m_norm_mix_g, w_in=m_w_in, conv_w=m_conv_w, ssm_lam_re=m_ssm_lam_re, ssm_lam_im=m_ssm_lam_im, ssm_log_dt=m_ssm_log_dt, ssm_b_re=m_ssm_b_re, ssm_b_im=m_ssm_b_im, ssm_c_re=m_ssm_c_re, ssm_c_im=m_ssm_c_im, ssm_d=m_ssm_d, ssm_w_glu=m_ssm_w_glu, gain_conv_out=m_gain_conv_out, gain_ssm_out=m_gain_ssm_out, w_out=m_w_out, norm_ffn_g=m_norm_ffn_g, w_up=m_w_up, ffn_conv_w=m_ffn_conv_w, ffn_conv_b=m_ffn_conv_b, w_down=m_w_down, norm_final_g=m_norm_final_g)
    mom_v = dict(meta_tokens=v_meta_tokens, norm_mix_g=v_norm_mix_g, w_in=v_w_in, conv_w=v_conv_w, ssm_lam_re=v_ssm_lam_re, ssm_lam_im=v_ssm_lam_im, ssm_log_dt=v_ssm_log_dt, ssm_b_re=v_ssm_b_re, ssm_b_im=v_ssm_b_im, ssm_c_re=v_ssm_c_re, ssm_c_im=v_ssm_c_im, ssm_d=v_ssm_d, ssm_w_glu=v_ssm_w_glu, gain_conv_out=v_gain_conv_out, gain_ssm_out=v_gain_ssm_out, w_out=v_w_out, norm_ffn_g=v_norm_ffn_g, w_up=v_w_up, ffn_conv_w=v_ffn_conv_w, ffn_conv_b=v_ffn_conv_b, w_down=v_w_down, norm_final_g=v_norm_final_g)
    names = list(weights)

    n_meta, d_meta = meta_tokens.shape
    seq, d = x.shape[1], x.shape[2]
    rows_used = n_meta + seq
    t = -(-rows_used // ROW_TILE) * ROW_TILE
    d_in_s = w_in.shape[2]
    dc_s = conv_w.shape[2]
    dc = dc_s * N_DEV
    ds = ssm_w_glu.shape[2]
    n_groups, n_state, grp = ssm_b_re.shape[1:]
    ns = n_groups * n_state
    nch = ds // SSM_CHUNK
    gpc = n_groups // nch
    ff_s = w_up.shape[2]
    dn_s = w_down.shape[1]
    assert 3 * dc + ds == d_in_s * N_DEV and 2 * dn_s == ff_s and t % (N_SEG * SUBLANES) == 0

    small_shard = jnp.concatenate([meta_tokens.reshape(-1), conv_w.reshape(-1), ffn_conv_w.reshape(-1)])
    n_small = small_shard.shape[0]
    small_rows = -(-n_small // LANES)
    small_shard = jnp.pad(small_shard, (0, small_rows * LANES - n_small)).reshape(small_rows, LANES)
    ag, ag_token = _gather2_start("gather_weights_start", [
        small_shard, w_in[0].astype(BF16), ssm_w_glu[0].astype(BF16), w_out[0].astype(BF16),
        jnp.swapaxes(w_up[0], 0, 1).astype(BF16), w_down[0].astype(BF16)])
    fb = ffn_conv_b

    gh = n_groups * grp
    per_h = lambda a: jnp.broadcast_to(a.reshape(n_groups, 1, -1), (n_groups, grp, n_state)).reshape(gh, n_state)
    ghp = lambda a: a.transpose(0, 1, 3, 2).reshape(gh, n_state)
    lr, li, log_dt_e = per_h(ssm_lam_re), per_h(ssm_lam_im), per_h(ssm_log_dt)
    br, bi = ghp(ssm_b_re), ghp(ssm_b_im)
    a_re, a_im, bb_re, bb_im = _disc_fwd(lr, li, log_dt_e, br, bi, deps=(ag_token,))
    cs = gpc * n_state
    chunk_row = lambda a: a.reshape(n_groups, grp, n_state)[:, 0].reshape(nch, 1, cs)
    to_bb = lambda a: _block_diag(a.reshape(nch, gpc, grp, n_state)).astype(BF16)
    to_cc = lambda a: _block_diag(a.reshape(nch, gpc, grp, n_state).transpose(0, 1, 3, 2)).astype(BF16)
    bbm_re, bbm_im = to_bb(bb_re), to_bb(bb_im)
    ccm_re, ccm_im = to_cc(ssm_c_re), to_cc(ssm_c_im)
    a_re_c, a_im_c = chunk_row(a_re), chunk_row(a_im)
    d_skip = ssm_d.reshape(nch, 1, SSM_CHUNK)

    target = _to_segments(jnp.pad(loss_target[0], ((n_meta, t - rows_used), (0, 0))))
    token = _gather2_forward("gather_weights_forward_first", ag, (0, 1), (bbm_im, ccm_im, target))
    (g_small,) = _gather2_wait("gather_weights_wait_small", ag, (0,), token)
    g_small = g_small.reshape(N_DEV, -1)
    o1 = n_meta * d_meta
    o2 = o1 + 3 * dc_s
    meta_full = g_small[:, :o1].reshape(N_DEV, n_meta, d_meta).transpose(1, 0, 2).reshape(n_meta, d)
    conv_w_f = g_small[:, o1:o2].reshape(N_DEV, 3, dc_s).transpose(1, 0, 2).reshape(3, dc)
    fw = g_small[:, o2:o2 + 3 * ff_s].reshape(N_DEV, 3, ff_s).transpose(1, 0, 2).reshape(3, N_DEV * ff_s)
    h0 = _to_segments(jnp.concatenate([meta_full, x[0], jnp.zeros((t - rows_used, d), F32)], axis=0))
    full_t = lambda w: pl.BlockSpec((t, w), lambda *_: (0, 0))

    hn1 = _rms_fwd("norm_mix", h0, norm_mix_g)
    (g_in,) = _gather2_wait("gather_weights_wait_in", ag, (1,), hn1)
    proj = _mm("proj", hn1, g_in, dims=NN, grid=(N_DEV,), a_spec=full_t(d),
               b_spec=pl.BlockSpec((None, d, d_in_s), lambda j: (j, 0, 0)),
               o_spec=pl.BlockSpec((t, d_in_s), lambda j: (0, j)),
               out_shape=jax.ShapeDtypeStruct((t, N_DEV * d_in_s), F32))
    s_re, s_im, y_ssm = _s5_fwd(proj, 3 * dc, bbm_re, bbm_im, a_re_c, a_im_c, ccm_re, ccm_im, d_skip)
    token = _gather2_forward("gather_weights_forward_up", ag, (2, 3, 4), (y_ssm,))
    g_glu, g_out = _gather2_wait("gather_weights_wait_mix", ag, (2, 3), token)
    w_out_f = g_out.reshape(-1, d)
    w_glu_f = g_glu.reshape(ds, ds)
    mixed, z_glu = _mix_fwd(proj, y_ssm, w_glu_f, conv_w_f, gain_conv_out, gain_ssm_out)
    tn_out = 256
    h1 = _mm("out_proj", mixed, w_out_f, dims=NN, grid=(d // tn_out,), a_spec=full_t(dc + ds),
             b_spec=pl.BlockSpec((dc + ds, tn_out), lambda j: (0, j)),
             o_spec=pl.BlockSpec((t, tn_out), lambda j: (0, j)),
             out_shape=jax.ShapeDtypeStruct((t, d), F32),
             res=h0, res_spec=pl.BlockSpec((t, tn_out), lambda j: (0, j)))
    hn2 = _rms_fwd("norm_ffn", h1, norm_ffn_g)
    token = _gather2_forward("gather_weights_forward_down", ag, (5,), (hn2,))
    (g_up,) = _gather2_wait("gather_weights_wait_up", ag, (4,), token)
    f2 = N_DEV * ff_s
    pair = 2 * ff_s
    w_up_t = g_up.reshape(f2, d)
    up_pre = _mm("up_proj", hn2, w_up_t, dims=NT, grid=(f2 // pair,), a_spec=full_t(d),
                 b_spec=pl.BlockSpec((pair, d), lambda j: (j, 0)),
                 o_spec=pl.BlockSpec((t, pair), lambda j: (0, j)),
                 out_shape=jax.ShapeDtypeStruct((t, f2), BF16))
    act = _gated_fwd(up_pre, fw, fb)
    (g_down,) = _gather2_wait("gather_weights_wait_down", ag, (5,), act)
    w_down_f = g_down.reshape(f2 // 2, d)
    h2 = _mm("down_proj", act, w_down_f, dims=NN, grid=(d // tn_out,), a_spec=full_t(f2 // 2),
             b_spec=pl.BlockSpec((f2 // 2, tn_out), lambda j: (0, j)),
             o_spec=pl.BlockSpec((t, tn_out), lambda j: (0, j)),
             out_shape=jax.ShapeDtypeStruct((t, d), F32),
             res=h1, res_spec=pl.BlockSpec((t, tn_out), lambda j: (0, j)))

    dh2, dh2_b, loss_part, d_norm_final = _loss_bwd(h2, target, norm_final_g.reshape(1, d), n_meta, rows_used)
    dact = _mm("down_dgrad", dh2_b, w_down_f, dims=NT, grid=(f2 // 2 // pair,), a_spec=full_t(d),
               b_spec=pl.BlockSpec((pair, d), lambda j: (j, 0)),
               o_spec=pl.BlockSpec((t, pair), lambda j: (0, j)),
               out_shape=jax.ShapeDtypeStruct((t, f2 // 2), BF16))
    dw_down = _mm("down_wgrad", act, dh2_b, dims=TN, grid=(f2 // 2 // pair,),
                  a_spec=pl.BlockSpec((t, pair), lambda j: (0, j)), b_spec=full_t(d),
                  o_spec=pl.BlockSpec((pair, d), lambda j: (j, 0)),
                  out_shape=jax.ShapeDtypeStruct((f2 // 2, d), BF16))
    ex_down, token = _send_start("exchange_down_start", [dw_down.reshape(N_DEV, dn_s, d)], gather=False)
    dup_pre, d_fw, d_fb = _gated_bwd(up_pre, dact, fw, fb, deps=(token,))
    wide = _wide_tile(t)
    dhn2 = _mm("up_dgrad", dup_pre, w_up_t, dims=NN, grid=(t // wide,),
               a_spec=pl.BlockSpec((wide, f2), lambda i: (i, 0)),
               b_spec=pl.BlockSpec((f2, d), lambda i: (0, 0)),
               o_spec=pl.BlockSpec((wide, d), lambda i: (i, 0)),
               out_shape=jax.ShapeDtypeStruct((t, d), F32))
    dw_up = _mm("up_wgrad", dup_pre, hn2, dims=TN, grid=(f2 // pair,),
                a_spec=pl.BlockSpec((t, pair), lambda j: (0, j)), b_spec=full_t(d),
                o_spec=pl.BlockSpec((pair, d), lambda j: (j, 0)),
                out_shape=jax.ShapeDtypeStruct((f2, d), BF16))
    ex_up, token = _send_start("exchange_up_start", [dw_up.reshape(N_DEV, ff_s, d)], gather=False)
    dh1, dh1_b, d_norm_ffn = _rms_bwd_res("norm_ffn_bwd", dh2, dhn2, h1, norm_ffn_g, deps=(token,))
    dmixed = _mm("out_dgrad", dh1_b, w_out_f, dims=NT, grid=((dc + ds) // tn_out,), a_spec=full_t(d),
                 b_spec=pl.BlockSpec((tn_out, d), lambda i: (i, 0)),
                 o_spec=pl.BlockSpec((t, tn_out), lambda i: (0, i)),
                 out_shape=jax.ShapeDtypeStruct((t, dc + ds), F32))
    dw_out = _mm("out_wgrad", mixed, dh1_b, dims=TN, grid=((dc + ds) // tn_out,),
                 a_spec=pl.BlockSpec((t, tn_out), lambda i: (0, i)), b_spec=full_t(d),
                 o_spec=pl.BlockSpec((tn_out, d), lambda i: (i, 0)),
                 out_shape=jax.ShapeDtypeStruct((dc + ds, d), BF16))
    db_gate, dconv, dy_ssm, d_wglu, d_conv_w, d_gain_c, d_gain_s = _mix_bwd1(
        proj, y_ssm, z_glu, dmixed, w_glu_f, conv_w_f, gain_conv_out, gain_ssm_out)
    (du, d_bbm_re, d_bbm_im, d_a_re, d_a_im, d_ccm_re, d_ccm_im, d_dskip) = _s5_bwd(
        dy_ssm, proj, 3 * dc, s_re, s_im, bbm_re, bbm_im, a_re_c, a_im_c, ccm_re, ccm_im, d_skip, gpc)

    from_bb = from_cc = lambda a: a.reshape(gh, n_state)
    first_h = lambda a: jnp.pad(a.reshape(n_groups, 1, n_state), ((0, 0), (0, grp - 1), (0, 0))).reshape(gh, n_state)
    over_h = lambda a: a.reshape(n_groups, grp, n_state).sum(axis=1)
    d_lr, d_li, d_dt_e, d_br, d_bi = _disc_bwd(
        lr, li, log_dt_e, br, bi, first_h(d_a_re), first_h(d_a_im), from_bb(d_bbm_re), from_bb(d_bbm_im))

    rep2d = dict(
        ssm_lam_re=(n_groups, n_state), ssm_lam_im=(n_groups, n_state), ssm_log_dt=(1, n_groups),
        ssm_b_re=(gh, n_state), ssm_b_im=(gh, n_state), ssm_c_re=(gh, n_state),
        ssm_c_im=(gh, n_state), ssm_d=(n_groups, grp), gain_conv_out=(1, dc),
        gain_ssm_out=(1, ds), norm_ffn_g=(1, d), ffn_conv_b=(1, N_DEV * ff_s), norm_final_g=(1, d))
    rep_names = list(rep2d)
    rep_grads = dict(
        ssm_lam_re=over_h(d_lr), ssm_lam_im=over_h(d_li), ssm_log_dt=over_h(d_dt_e).sum(axis=1),
        ssm_b_re=d_br, ssm_b_im=d_bi, ssm_c_re=from_cc(d_ccm_re), ssm_c_im=from_cc(d_ccm_im),
        ssm_d=d_dskip, gain_conv_out=d_gain_c, gain_ssm_out=d_gain_s, norm_ffn_g=d_norm_ffn,
        ffn_conv_b=d_fb, norm_final_g=d_norm_final)
    rep_shapes = [rep2d[n] for n in rep_names] + [(1, 1)]
    rep_rows = _packed_rows(rep_shapes)
    rep_pack = _pack([rep_grads[n] for n in rep_names] + [loss_part], rep_rows)
    ex_mix, token = _send_start("exchange_mix_start", [
        dw_out.reshape(N_DEV, -1, d), d_wglu.astype(BF16).reshape(N_DEV, -1, ds),
        d_conv_w.reshape(3, N_DEV, dc_s).transpose(1, 0, 2),
        d_fw.reshape(3, N_DEV, ff_s).transpose(1, 0, 2), rep_pack],
        gather=[False, False, False, False, True])
    dcdv = _mix_bwd2(proj, dconv, conv_w_f, deps=(token,))
    dproj = jnp.concatenate([db_gate, dcdv, du], axis=1)
    d_in = N_DEV * d_in_s
    w_in_rows = g_in.transpose(1, 0, 2).reshape(d, d_in)
    dhn1 = _mm("proj_dgrad", dproj, w_in_rows, dims=NT, grid=(t // wide,),
               a_spec=pl.BlockSpec((wide, d_in), lambda i: (i, 0)),
               b_spec=pl.BlockSpec((d, d_in), lambda i: (0, 0)),
               o_spec=pl.BlockSpec((wide, d), lambda i: (i, 0)),
               out_shape=jax.ShapeDtypeStruct((t, d), F32))
    dw_in = _mm("proj_wgrad", hn1, dproj, dims=TN, grid=(N_DEV,), a_spec=full_t(d),
                b_spec=pl.BlockSpec((t, d_in_s), lambda j: (0, j)),
                o_spec=pl.BlockSpec((None, d, d_in_s), lambda j: (j, 0, 0)),
                out_shape=jax.ShapeDtypeStruct((N_DEV, d, d_in_s), BF16))
    dh0, _, d_norm_mix = _rms_bwd_res("norm_mix_bwd", dh1, dhn1, h0, norm_mix_g)
    dh0 = _from_segments(dh0)
    grad_x = dh0[n_meta:rows_used][None]
    d_meta_b = dh0[:n_meta].reshape(n_meta, N_DEV, d_meta).transpose(1, 0, 2)
    ex_in, ex_in_token = _send_start("exchange_in_start", [dw_in, d_meta_b, d_norm_mix],
                                     gather=[False, False, True])

    shard_out = {}
    (rep_parts,) = _send_wait("gather_small_grads_wait", ex_mix, (4,), ex_in_token)
    rep_sum = _sum_parts("sum_small_grads", rep_parts)
    *rep_g, loss = _unpack(rep_sum, rep_shapes)
    loss = loss.reshape(())
    swapped = ("ssm_b_re", "ssm_b_im")
    to2d = lambda n, a: ghp(a) if n in swapped else a.reshape(rep2d[n])
    from2d = lambda n, a: (a.reshape(1, n_groups, grp, n_state).transpose(0, 1, 3, 2) if n in swapped
                           else a.reshape(weights[n].shape))
    as2d = lambda tree: [to2d(n, tree[n]) for n in rep_names]
    rep_res = _adamw_many("adamw_replicated", rep_g, as2d(weights), as2d(mom_m), as2d(mom_v))
    for i, n in enumerate(rep_names):
        shard_out[n] = [from2d(n, r) for r in (rep_g[i], *(res[i] for res in rep_res))]

    def update(n, parts, transposed=False):
        sh = weights[n].shape
        two_d = lambda a: (jnp.swapaxes(a[0], 0, 1) if transposed else a.reshape(parts.shape[1:]))
        res = _adamw_reduce("adamw_" + n, parts, two_d(weights[n]), two_d(mom_m[n]), two_d(mom_v[n]))
        shard_out[n] = [(jnp.swapaxes(r, 0, 1) if transposed else r).reshape(sh) for r in res]
        return res[0]

    (p_down,) = _send_wait("exchange_down_wait", ex_down, (0,), rep_sum)
    done = update("w_down", p_down)
    (p_up,) = _send_wait("exchange_up_wait", ex_up, (0,), done)
    done = update("w_up", p_up, transposed=True)
    p_out, p_glu, p_cw, p_fw = _send_wait("exchange_mix_wait", ex_mix, (0, 1, 2, 3), done)
    update("w_out", p_out)
    update("ssm_w_glu", p_glu)
    update("conv_w", p_cw)
    done = update("ffn_conv_w", p_fw)
    p_in, p_meta, p_nm = _send_wait("exchange_in_wait", ex_in, (0, 1, 2), done)
    update("w_in", p_in)
    update("meta_tokens", p_meta)
    update("norm_mix_g", p_nm)

    grads = [shard_out[n][0] for n in names]
    deltas = [shard_out[n][1] for n in names]
    new_m = [shard_out[n][2] for n in names]
    new_v = [shard_out[n][3] for n in names]
    return (loss, grad_x, *grads, *deltas, *new_m, *new_v)
```

```python
import functools
import math

import jax
import jax.numpy as jnp
from jax import lax
from jax.experimental import pallas as pl
from jax.experimental.pallas import tpu as pltpu

F32 = jnp.float32
BF16 = jnp.bfloat16
MESH = pl.DeviceIdType.MESH

N_DEV = 8
RMS_EPS = 1e-6
ADAM_LR = 0.001
ADAM_B1 = 0.9
ADAM_B2 = 0.999
ADAM_EPS = 1e-08
ADAM_WD = 0.01
ADAM_STEP = 10
ADAM_BC1 = 1.0 - ADAM_B1 ** ADAM_STEP
ADAM_BC2 = 1.0 - ADAM_B2 ** ADAM_STEP

SUBLANES = 8
LANES = 128
ROW_TILE = 128
ROW_CHUNK = 32
WIDE_TILES = (544, 256, 128)
FFN_COLS = 256
STAGE_COLS = 512
N_SEG = 8
HALO_ROWS = 16
SSM_CHUNK = 128
SCAN_UNROLL = 2
VMEM_LIMIT = 48 * 1024 * 1024

NN = ((1,), (0,))
NT = ((1,), (1,))
TN = ((0,), (0,))


def _params(**kw):
    return pltpu.CompilerParams(vmem_limit_bytes=VMEM_LIMIT, **kw)


def _dot(a, b, dims):
    return lax.dot_general(a, b, (dims, ((), ())), preferred_element_type=F32)


def _mean_sq_rsqrt(x):
    return lax.rsqrt(jnp.mean(x * x, axis=-1, keepdims=True) + RMS_EPS)


def _rms_bwd(x, r, g, dy):
    xhat = x * r
    dxh = dy * g
    dx = r * (dxh - xhat * jnp.mean(dxh * xhat, axis=-1, keepdims=True))
    return dx, dy * xhat


def _gelu(y):
    c = math.sqrt(2.0 / math.pi)
    t = jnp.tanh(c * (y + 0.044715 * y * y * y))
    return 0.5 * y * (1.0 + t), t


def _gelu_grad(y, t):
    c = math.sqrt(2.0 / math.pi)
    return 0.5 * (1.0 + t) + 0.5 * y * (1.0 - t * t) * c * (1.0 + 3.0 * 0.044715 * y * y)


def _wrap_prev_halo(halo, first):
    seg = lax.broadcasted_iota(jnp.int32, halo.shape, 0) % N_SEG
    wrapped = jnp.where(seg == 0, 0.0, pltpu.roll(halo, 1, 0))
    return jnp.where(first, wrapped, halo)


def _wrap_next_halo(halo, last):
    seg = lax.broadcasted_iota(jnp.int32, halo.shape, 0) % N_SEG
    wrapped = jnp.where(seg == N_SEG - 1, 0.0, pltpu.roll(halo, halo.shape[0] - 1, 0))
    return jnp.where(last, wrapped, halo)


def _tile_rows(get, prev, nxt, s, n, tile_rows=ROW_TILE):
    parts = []
    if s < 0:
        parts.append(prev[HALO_ROWS + s:HALO_ROWS + min(s + n, 0)])
    lo, hi = max(s, 0), min(s + n, tile_rows)
    if hi > lo:
        parts.append(get(lo, hi - lo))
    if s + n > tile_rows:
        parts.append(nxt[max(s - tile_rows, 0):s + n - tile_rows])
    return parts[0] if len(parts) == 1 else jnp.concatenate(parts, axis=0)


def _dev_index(p):
    return 4 * p[0] + 2 * p[1] + p[2]


def _allgather(name, shards, deps=()):
    n = len(shards)

    def body(*refs):
        ins, outs = refs[:n], refs[n:2 * n]
        send_sems, recv_sems, local_sems = refs[2 * n:]
        x, y, c = lax.axis_index("x"), lax.axis_index("y"), lax.axis_index("c")
        me, sibling = (x, y, c), (x, y, 1 - c)
        chips = [(1 - x, y), (x, 1 - y), (1 - x, 1 - y)]

        def copy(a, k, block, to, src=None):
            dst = outs[a].at[_dev_index(block)]
            return pltpu.make_async_remote_copy(
                src_ref=dst if src is None else src, dst_ref=dst,
                send_sem=send_sems.at[a, k], recv_sem=recv_sems.at[a, k],
                device_id=to, device_id_type=MESH)

        mine = [pltpu.make_async_copy(ins[a], outs[a].at[_dev_index(me)], local_sems.at[a])
                for a in range(n)]
        for cp in mine:
            cp.start()
        first = []
        for a in range(n):
            first.append(copy(a, 0, me, sibling, src=ins[a]))
            for j, chip in enumerate(chips):
                first.append(copy(a, 1 + j, me, (*chip, c), src=ins[a]))
        for cp in first:
            cp.start()
        passed = []
        for j, chip in enumerate(chips):
            for a in range(n):
                copy(a, 1 + j, (*chip, c), me).wait_recv()
                fwd = copy(a, 4 + j, (*chip, c), sibling)
                fwd.start()
                passed.append(fwd)
        for a in range(n):
            copy(a, 0, sibling, me).wait_recv()
            for j, chip in enumerate(chips):
                copy(a, 4 + j, (*chip, 1 - c), me).wait_recv()
        for cp in first + passed:
            cp.wait_send()
        for cp in mine:
            cp.wait()

    any_spec = pl.BlockSpec(memory_space=pl.ANY)
    return pl.pallas_call(
        _ignoring_deps(body, n, deps), name=name,
        out_shape=[jax.ShapeDtypeStruct((N_DEV,) + s.shape, s.dtype) for s in shards],
        in_specs=[any_spec] * (n + len(deps)), out_specs=[any_spec] * n,
        scratch_shapes=[pltpu.SemaphoreType.DMA((n, 7)), pltpu.SemaphoreType.DMA((n, 7)),
                        pltpu.SemaphoreType.DMA((n,))],
    )(*shards, *deps)


HBM_SPEC = pl.BlockSpec(memory_space=pltpu.HBM)
SEM_SPEC = pl.BlockSpec(memory_space=pltpu.SEMAPHORE)
ANY_SPEC = pl.BlockSpec(memory_space=pl.ANY)
DATAFLOW = pltpu.SideEffectType.DATAFLOW_SIDE_EFFECTING


def _ignoring_deps(body, n_in, deps):
    n_dep = len(deps)

    def wrapped(*refs):
        return body(*refs[:n_in], *refs[n_in + n_dep:])

    return wrapped


def _my_position():
    x, y, c = lax.axis_index("x"), lax.axis_index("y"), lax.axis_index("c")
    return (x, y, c)


def _peer(me, k):
    return tuple((1 - v) if (k >> s) & 1 else v for v, s in zip(me, (2, 1, 0)))


def _split_copies(src_refs, land_refs, send_sems, recv_sems, gather):
    me = _my_position()
    copies = []
    for a, (src, land) in enumerate(zip(src_refs, land_refs)):
        for k in range(1, N_DEV):
            peer = _peer(me, k)
            copies.append(pltpu.make_async_remote_copy(
                src_ref=src if gather[a] else src.at[_dev_index(peer)], dst_ref=land.at[_dev_index(me)],
                send_sem=send_sems[a].at[k - 1], recv_sem=recv_sems[a].at[k - 1],
                device_id=peer, device_id_type=MESH))
    return copies


def _own_slot(block, like_shape):
    me = _dev_index(_my_position())
    return lax.dynamic_update_index_in_dim(lax.empty(like_shape, block.dtype), block, me, 0)


def _send_start(name, srcs, gather):
    n = len(srcs)
    me = _dev_index(_my_position())
    gather = [gather] * n if isinstance(gather, bool) else list(gather)
    lands = [_own_slot(s, (N_DEV,) + s.shape) if g else
             _own_slot(lax.dynamic_index_in_dim(s, me, 0, keepdims=False), s.shape)
             for s, g in zip(srcs, gather)]

    def body(*refs):
        src_refs, land_refs = refs[:n], refs[n:2 * n]
        send_sems, recv_sems = refs[2 * n:3 * n], refs[3 * n:4 * n]
        token = refs[-1]
        for cp in _split_copies(src_refs, land_refs, send_sems, recv_sems, gather):
            cp.start()
        token[...] = jnp.zeros_like(token)

    hbm = lambda a: pltpu.HBM(a.shape, a.dtype)
    sems = [pltpu.SemaphoreType.DMA((N_DEV - 1,))] * n
    outs = pl.pallas_call(
        body, name=name,
        out_shape=(*sems, *sems, *[hbm(s) for s in srcs], *[hbm(l) for l in lands],
                   jax.ShapeDtypeStruct((SUBLANES, LANES), F32)),
        in_specs=[HBM_SPEC] * (2 * n),
        out_specs=(*[SEM_SPEC] * (2 * n), *[HBM_SPEC] * (2 * n), pl.BlockSpec(memory_space=pltpu.VMEM)),
        input_output_aliases={i: 2 * n + i for i in range(2 * n)},
        compiler_params=pltpu.CompilerParams(has_side_effects=DATAFLOW),
    )(*[pltpu.with_memory_space_constraint(a, pltpu.HBM) for a in (*srcs, *lands)])
    state = dict(send=outs[:n], recv=outs[n:2 * n], srcs=outs[2 * n:3 * n], lands=outs[3 * n:4 * n],
                 gather=gather)
    return state, outs[-1]


def _send_wait(name, state, which, after):
    n = len(which)
    pick = lambda key: [state[key][i] for i in which]
    gather = pick("gather")

    def body(*refs):
        src_refs, land_refs = refs[:n], refs[n:2 * n]
        send_sems, recv_sems = refs[2 * n:3 * n], refs[3 * n:4 * n]
        for cp in _split_copies(src_refs, land_refs, send_sems, recv_sems, gather):
            cp.wait_send()
            cp.wait_recv()

    srcs, lands = pick("srcs"), pick("lands")
    hbm = lambda a: pltpu.HBM(a.shape, a.dtype)
    outs = pl.pallas_call(
        body, name=name,
        out_shape=(*[hbm(s) for s in srcs], *[hbm(l) for l in lands]),
        in_specs=[*[HBM_SPEC] * (2 * n), *[SEM_SPEC] * (2 * n), ANY_SPEC],
        out_specs=tuple([HBM_SPEC] * (2 * n)),
        input_output_aliases={i: i for i in range(2 * n)},
        compiler_params=pltpu.CompilerParams(has_side_effects=DATAFLOW),
    )(*srcs, *lands, *pick("send"), *pick("recv"), after)
    return outs[n:]


def _two_level_copies(src_refs, land_refs, sems1, sems2):
    x, y, c = _my_position()
    me, sibling = (x, y, c), (x, y, 1 - c)
    chips = [(1 - x, y), (x, 1 - y), (1 - x, 1 - y)]
    stage1, stage2 = [], []
    for a, land in enumerate(land_refs):
        def copy(src, block, to, sems, k):
            return pltpu.make_async_remote_copy(
                src_ref=src, dst_ref=land.at[_dev_index(block)], send_sem=sems[0][a].at[k],
                recv_sem=sems[1][a].at[k], device_id=to, device_id_type=MESH)
        src = src_refs[a] if src_refs is not None else land.at[_dev_index(me)]
        stage1.append([copy(src, me, sibling, sems1, 0)] +
                      [copy(src, me, (*chip, c), sems1, 1 + j) for j, chip in enumerate(chips)])
        if sems2 is not None:
            stage2.append([copy(land.at[_dev_index((*chip, c))], (*chip, c), sibling, sems2, j)
                           for j, chip in enumerate(chips)])
    return stage1, stage2


def _gather2_start(name, shards):
    n = len(shards)
    lands = [_own_slot(s, (N_DEV,) + s.shape) for s in shards]

    def body(*refs):
        src_refs, land_refs = refs[:n], refs[n:2 * n]
        sems1 = (refs[2 * n:3 * n], refs[3 * n:4 * n])
        stage1, _ = _two_level_copies(src_refs, land_refs, sems1, None)
        for copies in stage1:
            for cp in copies:
                cp.start()
        refs[-1][...] = jnp.zeros_like(refs[-1])

    hbm = lambda a: pltpu.HBM(a.shape, a.dtype)
    sems = [pltpu.SemaphoreType.DMA((4,))] * n
    outs = pl.pallas_call(
        body, name=name,
        out_shape=(*sems, *sems, *[hbm(s) for s in shards], *[hbm(l) for l in lands],
                   jax.ShapeDtypeStruct((SUBLANES, LANES), F32)),
        in_specs=[HBM_SPEC] * (2 * n),
        out_specs=(*[SEM_SPEC] * (2 * n), *[HBM_SPEC] * (2 * n), pl.BlockSpec(memory_space=pltpu.VMEM)),
        input_output_aliases={i: 2 * n + i for i in range(2 * n)},
        compiler_params=pltpu.CompilerParams(has_side_effects=DATAFLOW),
    )(*[pltpu.with_memory_space_constraint(a, pltpu.HBM) for a in (*shards, *lands)])
    state = dict(send1=list(outs[:n]), recv1=list(outs[n:2 * n]), srcs=list(outs[2 * n:3 * n]),
                 lands=list(outs[3 * n:4 * n]), send2={}, recv2={})
    return state, outs[-1]


def _gather2_forward(name, state, which, after):
    n = len(which)
    pick = lambda key: [state[key][i] for i in which]

    def body(*refs):
        land_refs, recv1 = refs[:n], refs[n:2 * n]
        outs = refs[2 * n + len(after):]
        sems2 = (outs[:n], outs[n:2 * n])
        stage1, stage2 = _two_level_copies(None, land_refs, (recv1, recv1), sems2)
        for a in range(n):
            for j in range(3):
                stage1[a][1 + j].wait_recv()
                stage2[a][j].start()
        outs[-1][...] = jnp.zeros_like(outs[-1])

    lands = pick("lands")
    sems = [pltpu.SemaphoreType.DMA((3,))] * n
    outs = pl.pallas_call(
        body, name=name,
        out_shape=(*sems, *sems, *[pltpu.HBM(l.shape, l.dtype) for l in lands],
                   jax.ShapeDtypeStruct((SUBLANES, LANES), F32)),
        in_specs=[*[HBM_SPEC] * n, *[SEM_SPEC] * n, *[ANY_SPEC] * len(after)],
        out_specs=(*[SEM_SPEC] * (2 * n), *[HBM_SPEC] * n, pl.BlockSpec(memory_space=pltpu.VMEM)),
        input_output_aliases={i: 2 * n + i for i in range(n)},
        compiler_params=pltpu.CompilerParams(has_side_effects=DATAFLOW),
    )(*lands, *pick("recv1"), *after)
    for idx, i in enumerate(which):
        state["send2"][i], state["recv2"][i] = outs[idx], outs[n + idx]
        state["lands"][i] = outs[2 * n + idx]
    return outs[-1]


def _gather2_wait(name, state, which, after):
    n = len(which)
    pick = lambda key: [state[key][i] for i in which]

    def body(*refs):
        src_refs, land_refs = refs[:n], refs[n:2 * n]
        sems1 = (refs[2 * n:3 * n], refs[3 * n:4 * n])
        sems2 = (refs[4 * n:5 * n], refs[5 * n:6 * n])
        stage1, stage2 = _two_level_copies(src_refs, land_refs, sems1, sems2)
        for a in range(n):
            for cp in stage1[a]:
                cp.wait_send()
            stage1[a][0].wait_recv()
            for cp in stage2[a]:
                cp.wait_send()
                cp.wait_recv()

    srcs, lands = pick("srcs"), pick("lands")
    hbm = lambda a: pltpu.HBM(a.shape, a.dtype)
    outs = pl.pallas_call(
        body, name=name,
        out_shape=(*[hbm(s) for s in srcs], *[hbm(l) for l in lands]),
        in_specs=[*[HBM_SPEC] * (2 * n), *[SEM_SPEC] * (4 * n), ANY_SPEC],
        out_specs=tuple([HBM_SPEC] * (2 * n)),
        input_output_aliases={i: i for i in range(2 * n)},
        compiler_params=pltpu.CompilerParams(has_side_effects=DATAFLOW),
    )(*srcs, *lands, *pick("send1"), *pick("recv1"), *pick("send2"), *pick("recv2"), after)
    return outs[n:]


def _mm(name, a, b, *, dims, grid, a_spec, b_spec, o_spec, out_shape, acc_shape=None,
        res=None, res_spec=None):
    n_red = grid[-1] if acc_shape is not None else 1
    red_axis = len(grid) - 1

    def body(*refs):
        a_ref, b_ref = refs[0], refs[1]
        r_ref = refs[2] if res is not None else None
        o_ref = refs[3] if res is not None else refs[2]
        part = _dot(a_ref[...], b_ref[...], dims)
        if acc_shape is None:
            if r_ref is not None:
                part = part + r_ref[...]
            o_ref[...] = part.astype(o_ref.dtype)
            return
        acc_ref = refs[-1]
        k = pl.program_id(red_axis)

        @pl.when(k == 0)
        def _():
            acc_ref[...] = part

        @pl.when(k > 0)
        def _():
            acc_ref[...] += part

        @pl.when(k == n_red - 1)
        def _():
            total = acc_ref[...]
            if r_ref is not None:
                total = total + r_ref[...]
            o_ref[...] = total.astype(o_ref.dtype)

    ins, in_specs = [a, b], [a_spec, b_spec]
    if res is not None:
        ins.append(res)
        in_specs.append(res_spec)
    return pl.pallas_call(
        body, name=name, grid=grid, in_specs=in_specs, out_specs=o_spec, out_shape=out_shape,
        scratch_shapes=[pltpu.VMEM(acc_shape, F32)] if acc_shape is not None else [],
        compiler_params=_params(),
    )(*ins)


def _wide_tile(t):
    return next(c for c in WIDE_TILES if t % c == 0)


def _rms_fwd(name, h, g):
    t, d = h.shape
    tile = _wide_tile(t)

    def body(h_ref, g_ref, o_ref):
        x = h_ref[...]
        o_ref[...] = (x * _mean_sq_rsqrt(x) * g_ref[...]).astype(BF16)

    return pl.pallas_call(
        body, name=name, grid=(t // tile,),
        in_specs=[pl.BlockSpec((tile, d), lambda i: (i, 0)), pl.BlockSpec((1, d), lambda i: (0, 0))],
        out_specs=pl.BlockSpec((tile, d), lambda i: (i, 0)),
        out_shape=jax.ShapeDtypeStruct((t, d), BF16), compiler_params=_params(),
    )(h, g)


def _proj_rms_fwd(name, a, w, res, g):
    t, d = res.shape
    k = a.shape[1]
    tile = _wide_tile(t)

    def body(a_ref, w_ref, r_ref, g_ref, h_ref, o_ref):
        x = r_ref[...] + _dot(a_ref[...], w_ref[...], NN)
        h_ref[...] = x
        o_ref[...] = (x * _mean_sq_rsqrt(x) * g_ref[...]).astype(BF16)

    row = pl.BlockSpec((tile, d), lambda i: (i, 0))
    return pl.pallas_call(
        body, name=name, grid=(t // tile,),
        in_specs=[pl.BlockSpec((tile, k), lambda i: (i, 0)), pl.BlockSpec(w.shape, lambda i: (0, 0)), row,
                  pl.BlockSpec((1, d), lambda i: (0, 0))],
        out_specs=[row, row],
        out_shape=[jax.ShapeDtypeStruct((t, d), F32), jax.ShapeDtypeStruct((t, d), BF16)],
        compiler_params=_params(),
    )(a, w, res, g)


def _fused_tile(t):
    half = _wide_tile(t) // 2
    return half if half % 16 == 0 and t % half == 0 else _wide_tile(t)


def _dgrad_rms_bwd(name, dres, dy, w, dims, h, g, deps=()):
    t, d = h.shape
    k = dy.shape[1]

    def body(dres_ref, dy_ref, w_ref, h_ref, g_ref, dh_ref, dhb_ref, dg_ref):
        x = h_ref[...]
        dhn = _dot(dy_ref[...], w_ref[...], dims)
        dx, dgt = _rms_bwd(x, _mean_sq_rsqrt(x), g_ref[...], dhn)
        dh = dres_ref[...] + dx
        dh_ref[...] = dh
        dhb_ref[...] = dh.astype(BF16)

        @pl.when(pl.program_id(0) == 0)
        def _():
            dg_ref[...] = jnp.zeros_like(dg_ref)

        dg_ref[...] += jnp.sum(dgt, axis=0, keepdims=True)

    tile = _fused_tile(t)
    row = pl.BlockSpec((tile, d), lambda i: (i, 0))
    vec = pl.BlockSpec((1, d), lambda i: (0, 0))
    return pl.pallas_call(
        _ignoring_deps(body, 5, deps), name=name, grid=(t // tile,),
        in_specs=[row, pl.BlockSpec((tile, k), lambda i: (i, 0)), pl.BlockSpec(w.shape, lambda i: (0, 0)), row,
                  vec] + [ANY_SPEC] * len(deps),
        out_specs=[row, row, vec],
        out_shape=[jax.ShapeDtypeStruct((t, d), F32), jax.ShapeDtypeStruct((t, d), BF16),
                   jax.ShapeDtypeStruct((1, d), F32)],
        compiler_params=_params(),
    )(dres, dy, w, h, g, *deps)


def _down_loss_bwd(act, w_down, h1, target, g, row_lo, row_hi):
    t, d = h1.shape
    f = act.shape[1]
    tile = _fused_tile(t)

    def body(act_ref, w_ref, h_ref, tg_ref, g_ref, dh_ref, dhb_ref, loss_ref, dg_ref):
        i = pl.program_id(0)
        x = h_ref[...] + _dot(act_ref[...], w_ref[...], NN)
        r = _mean_sq_rsqrt(x)
        gv = g_ref[...]
        y = x * r * gv
        rows = i * tile + lax.broadcasted_iota(jnp.int32, (tile, 1), 0)
        time = (rows % N_SEG) * (t // N_SEG) + rows // N_SEG
        valid = jnp.logical_and(time >= row_lo, time < row_hi)
        err = jnp.where(valid, y - tg_ref[...], 0.0)
        dy = err * (1.0 / d)
        dx, dgt = _rms_bwd(x, r, gv, dy)
        dh_ref[...] = dx
        dhb_ref[...] = dx.astype(BF16)

        @pl.when(i == 0)
        def _():
            loss_ref[...] = jnp.zeros_like(loss_ref)
            dg_ref[...] = jnp.zeros_like(dg_ref)

        row_loss = jnp.mean(err * err, axis=-1, keepdims=True)
        loss_ref[...] += 0.5 * jnp.sum(row_loss, axis=0, keepdims=True)
        dg_ref[...] += jnp.sum(dgt, axis=0, keepdims=True)

    row = pl.BlockSpec((tile, d), lambda i: (i, 0))
    vec = pl.BlockSpec((1, d), lambda i: (0, 0))
    return pl.pallas_call(
        body, name="down_proj_loss_bwd", grid=(t // tile,),
        in_specs=[pl.BlockSpec((tile, f), lambda i: (i, 0)), pl.BlockSpec(w_down.shape, lambda i: (0, 0)),
                  row, row, vec],
        out_specs=[row, row, pl.BlockSpec((1, 1), lambda i: (0, 0)), vec],
        out_shape=[jax.ShapeDtypeStruct((t, d), F32), jax.ShapeDtypeStruct((t, d), BF16),
                   jax.ShapeDtypeStruct((1, 1), F32), jax.ShapeDtypeStruct((1, d), F32)],
        compiler_params=_params(),
    )(act, w_down, h1, target, g)


def _prev_halo(i, t):
    return jnp.where(i == 0, t // HALO_ROWS - 1, i * (ROW_TILE // HALO_ROWS) - 1)


def _next_halo(i, t):
    return jnp.where(i == t // ROW_TILE - 1, 0, (i + 1) * (ROW_TILE // HALO_ROWS))


def _causal_taps(cur, halo, first):
    rows = cur.shape[0]
    ext = jnp.concatenate([_wrap_prev_halo(halo, first), cur], axis=0)
    return ext[:rows], ext[N_SEG:N_SEG + rows]


def _anticausal_taps(cur, halo, last):
    rows = cur.shape[0]
    ext = jnp.concatenate([cur, _wrap_next_halo(halo, last)], axis=0)
    return ext[N_SEG:N_SEG + rows], ext[2 * N_SEG:2 * N_SEG + rows]


def _mix_fwd(proj, y, w_glu, conv_w, gain_c, gain_s):
    t = proj.shape[0]
    dc = conv_w.shape[1]
    ds = y.shape[1]

    def body(p_ref, halo_ref, y_ref, wg_ref, cw_ref, gc_ref, gs_ref, mixed_ref, z_ref):
        i = pl.program_id(0)
        p = p_ref[...]
        b, c, v = p[:, :dc], p[:, dc:2 * dc], p[:, 2 * dc:3 * dc]
        cv = c * v
        hp = halo_ref[...]
        x2, x1 = _causal_taps(cv, hp[:, dc:2 * dc] * hp[:, 2 * dc:3 * dc], i == 0)
        cw = cw_ref[...]
        conv = cw[0:1] * x2 + cw[1:2] * x1 + cw[2:3] * cv
        co = b * conv
        mixed_ref[:, :dc] = (co * _mean_sq_rsqrt(co) * gc_ref[...]).astype(BF16)
        g, _ = _gelu(y_ref[...])
        z = _dot(g.astype(BF16), wg_ref[...], NN)
        z_ref[...] = z
        so = g * jax.nn.sigmoid(z)
        mixed_ref[:, dc:] = (so * _mean_sq_rsqrt(so) * gs_ref[...]).astype(BF16)

    const = lambda i: (0, 0)
    return pl.pallas_call(
        body, name="mix_fwd", grid=(t // ROW_TILE,),
        in_specs=[pl.BlockSpec((ROW_TILE, 3 * dc), lambda i: (i, 0)),
                  pl.BlockSpec((HALO_ROWS, 3 * dc), lambda i: (_prev_halo(i, t), 0)),
                  pl.BlockSpec((ROW_TILE, ds), lambda i: (i, 0)),
                  pl.BlockSpec((ds, ds), const), pl.BlockSpec(conv_w.shape, const),
                  pl.BlockSpec((1, dc), const), pl.BlockSpec((1, ds), const)],
        out_specs=[pl.BlockSpec((ROW_TILE, dc + ds), lambda i: (i, 0)),
                   pl.BlockSpec((ROW_TILE, ds), lambda i: (i, 0))],
        out_shape=[jax.ShapeDtypeStruct((t, dc + ds), BF16), jax.ShapeDtypeStruct((t, ds), F32)],
        compiler_params=_params(),
    )(proj, proj, y, w_glu, conv_w, gain_c, gain_s)


def _mix_bwd1(proj, y, z, dmixed, w_glu, conv_w, gain_c, gain_s):
    t = proj.shape[0]
    dc = conv_w.shape[1]
    ds = y.shape[1]

    def body(p_ref, halo_ref, y_ref, z_ref, dm_ref, wg_ref, cw_ref, gc_ref, gs_ref,
             db_ref, dconv_ref, dy_ref, dwg_ref, dcw_ref, dgc_ref, dgs_ref):
        i = pl.program_id(0)

        @pl.when(i == 0)
        def _():
            dwg_ref[...] = jnp.zeros_like(dwg_ref)
            dcw_ref[...] = jnp.zeros_like(dcw_ref)
            dgc_ref[...] = jnp.zeros_like(dgc_ref)
            dgs_ref[...] = jnp.zeros_like(dgs_ref)

        p = p_ref[...]
        b, c, v = p[:, :dc], p[:, dc:2 * dc], p[:, 2 * dc:3 * dc]
        cv = c * v
        hp = halo_ref[...]
        x2, x1 = _causal_taps(cv, hp[:, dc:2 * dc] * hp[:, 2 * dc:3 * dc], i == 0)
        cw = cw_ref[...]
        conv = cw[0:1] * x2 + cw[1:2] * x1 + cw[2:3] * cv
        co = b * conv
        dm = dm_ref[...]
        dco, dgt = _rms_bwd(co, _mean_sq_rsqrt(co), gc_ref[...], dm[:, :dc])
        dgc_ref[...] += jnp.sum(dgt, axis=0, keepdims=True)
        db_ref[...] = (dco * conv).astype(BF16)
        dconv = dco * b
        dconv_ref[...] = dconv
        dcw_ref[0:1, :] += jnp.sum(dconv * x2, axis=0, keepdims=True)
        dcw_ref[1:2, :] += jnp.sum(dconv * x1, axis=0, keepdims=True)
        dcw_ref[2:3, :] += jnp.sum(dconv * cv, axis=0, keepdims=True)

        yv = y_ref[...]
        g, th = _gelu(yv)
        sg = jax.nn.sigmoid(z_ref[...])
        so = g * sg
        dso, dgt = _rms_bwd(so, _mean_sq_rsqrt(so), gs_ref[...], dm[:, dc:])
        dgs_ref[...] += jnp.sum(dgt, axis=0, keepdims=True)
        dz = (dso * g * sg * (1.0 - sg)).astype(BF16)
        dg = dso * sg + _dot(dz, wg_ref[...], NT)
        dwg_ref[...] += _dot(g.astype(BF16), dz, TN)
        dy_ref[...] = dg * _gelu_grad(yv, th)

    const = lambda i: (0, 0)
    row = lambda w: pl.BlockSpec((ROW_TILE, w), lambda i: (i, 0))
    return pl.pallas_call(
        body, name="mix_bwd1", grid=(t // ROW_TILE,),
        in_specs=[row(3 * dc), pl.BlockSpec((HALO_ROWS, 3 * dc), lambda i: (_prev_halo(i, t), 0)),
                  row(ds), row(ds), row(dc + ds), pl.BlockSpec((ds, ds), const),
                  pl.BlockSpec(conv_w.shape, const), pl.BlockSpec((1, dc), const),
                  pl.BlockSpec((1, ds), const)],
        out_specs=[row(dc), row(dc), row(ds), pl.BlockSpec((ds, ds), const),
                   pl.BlockSpec(conv_w.shape, const), pl.BlockSpec((1, dc), const),
                   pl.BlockSpec((1, ds), const)],
        out_shape=[jax.ShapeDtypeStruct((t, dc), BF16), jax.ShapeDtypeStruct((t, dc), F32),
                   jax.ShapeDtypeStruct((t, ds), F32), jax.ShapeDtypeStruct((ds, ds), F32),
                   jax.ShapeDtypeStruct(conv_w.shape, F32), jax.ShapeDtypeStruct((1, dc), F32),
                   jax.ShapeDtypeStruct((1, ds), F32)],
        compiler_params=_params(),
    )(proj, proj, y, z, dmixed, w_glu, conv_w, gain_c, gain_s)


def _mix_bwd2(proj, dconv, conv_w, deps=()):
    t = proj.shape[0]
    dc = conv_w.shape[1]
    n_tiles = t // ROW_TILE

    def body(c_ref, v_ref, d_ref, halo_ref, cw_ref, o_ref):
        i = pl.program_id(0)
        d = d_ref[...]
        u1, u2 = _anticausal_taps(d, halo_ref[...], i == n_tiles - 1)
        cw = cw_ref[...]
        dcv = cw[2:3] * d + cw[1:2] * u1 + cw[0:1] * u2
        o_ref[:, :dc] = (dcv * v_ref[...]).astype(BF16)
        o_ref[:, dc:] = (dcv * c_ref[...]).astype(BF16)

    return pl.pallas_call(
        _ignoring_deps(body, 5, deps), name="mix_bwd2", grid=(n_tiles,),
        in_specs=[pl.BlockSpec((ROW_TILE, dc), lambda i: (i, 1)),
                  pl.BlockSpec((ROW_TILE, dc), lambda i: (i, 2)),
                  pl.BlockSpec((ROW_TILE, dc), lambda i: (i, 0)),
                  pl.BlockSpec((HALO_ROWS, dc), lambda i: (_next_halo(i, t), 0)),
                  pl.BlockSpec(conv_w.shape, lambda i: (0, 0))] + [ANY_SPEC] * len(deps),
        out_specs=pl.BlockSpec((ROW_TILE, 2 * dc), lambda i: (i, 0)),
        out_shape=jax.ShapeDtypeStruct((t, 2 * dc), BF16), compiler_params=_params(),
    )(proj, proj, dconv, dconv, conv_w, *deps)


def _stage_rows(stage_ref, row0, src, wrap=None):
    n, width = src.shape
    for c in range(0, width, STAGE_COLS):
        v = src[:, pl.ds(c, STAGE_COLS)].astype(F32)
        stage_ref[pl.ds(row0, n), pl.ds(c, STAGE_COLS)] = v if wrap is None else wrap(v)


def _conv_taps(stage_ref, fw_ref, fb_ref, c0, r, rows=ROW_CHUNK):
    cols = pl.ds(c0, FFN_COLS)
    x0, x1, x2 = (stage_ref[pl.ds(HALO_ROWS + r - k * N_SEG, rows), cols] for k in range(3))
    w = fw_ref[:, cols]
    return w[0:1] * x2 + w[1:2] * x1 + w[2:3] * x0 + fb_ref[:, cols], x2, x1, x0


def _gated_fwd(up_pre, fw, fb):
    t, f2 = up_pre.shape
    f = f2 // 2

    def body(up_ref, halo_ref, fw_ref, fb_ref, act_ref, stage_ref):
        first = pl.program_id(0) == 0
        _stage_rows(stage_ref, 0, halo_ref, lambda v: _wrap_prev_halo(v, first))
        _stage_rows(stage_ref, HALO_ROWS, up_ref)
        for c0 in range(0, f, FFN_COLS):
            for r in range(0, ROW_TILE, ROW_CHUNK):
                a = _conv_taps(stage_ref, fw_ref, fb_ref, c0, r)[0]
                val = _conv_taps(stage_ref, fw_ref, fb_ref, f + c0, r)[0]
                act_ref[pl.ds(r, ROW_CHUNK), pl.ds(c0, FFN_COLS)] = (a * jax.nn.sigmoid(a) * val).astype(BF16)

    const = lambda a: pl.BlockSpec(a.shape, lambda i: (0, 0))
    return pl.pallas_call(
        body, name="ffn_fwd", grid=(t // ROW_TILE,),
        in_specs=[pl.BlockSpec((ROW_TILE, f2), lambda i: (i, 0)),
                  pl.BlockSpec((HALO_ROWS, f2), lambda i: (_prev_halo(i, t), 0)), const(fw), const(fb)],
        out_specs=pl.BlockSpec((ROW_TILE, f), lambda i: (i, 0)),
        out_shape=jax.ShapeDtypeStruct((t, f), BF16),
        scratch_shapes=[pltpu.VMEM((HALO_ROWS + ROW_TILE, f2), F32)], compiler_params=_params(),
    )(up_pre, up_pre, fw, fb)


def _gated_bwd(up_pre, dact, fw, fb, deps=()):
    t, f2 = up_pre.shape
    f = f2 // 2
    n_tiles = t // ROW_TILE
    chunks = [(r, ROW_CHUNK) for r in range(0, ROW_TILE, ROW_CHUNK)] + [(ROW_TILE, HALO_ROWS)]

    def body(up_ref, prev_ref, next_ref, dact_ref, dact_next_ref, fw_ref, fb_ref,
             out_ref, dfw_ref, dfb_ref, dup_ref, stage_ref):
        first = pl.program_id(0) == 0
        last = pl.program_id(0) == n_tiles - 1

        @pl.when(first)
        def _():
            dfw_ref[...] = jnp.zeros_like(dfw_ref)
            dfb_ref[...] = jnp.zeros_like(dfb_ref)

        _stage_rows(stage_ref, 0, prev_ref, lambda v: _wrap_prev_halo(v, first))
        _stage_rows(stage_ref, HALO_ROWS, up_ref)
        _stage_rows(stage_ref, HALO_ROWS + ROW_TILE, next_ref, lambda v: _wrap_next_halo(v, last))
        fold = lambda v: sum(v[s:s + SUBLANES] for s in range(0, ROW_CHUNK, SUBLANES))
        for c0 in range(0, f, FFN_COLS):
            starts = (c0, f + c0)
            dact_next = _wrap_next_halo(dact_next_ref[:, pl.ds(c0, FFN_COLS)].astype(F32), last)
            get_dact = lambda s, n: dact_ref[pl.ds(s, n), pl.ds(c0, FFN_COLS)].astype(F32)
            sums = [[jnp.zeros((SUBLANES, FFN_COLS), F32)] * 4 for _ in starts]
            for r, rows in chunks:
                taps = [_conv_taps(stage_ref, fw_ref, fb_ref, c, r, rows) for c in starts]
                a, val = taps[0][0], taps[1][0]
                da_ct = _tile_rows(get_dact, None, dact_next, r, rows)
                sg = jax.nn.sigmoid(a)
                dup = (da_ct * val * sg * (1.0 + a * (1.0 - sg)), da_ct * a * sg)
                for k in range(2):
                    dup_ref[k, pl.ds(r, rows), :] = dup[k]
                    if r < ROW_TILE:
                        terms = (dup[k], dup[k] * taps[k][1], dup[k] * taps[k][2], dup[k] * taps[k][3])
                        sums[k] = [s + fold(v) for s, v in zip(sums[k], terms)]
            for k, c in enumerate(starts):
                cols = pl.ds(c, FFN_COLS)
                s_b, s_w0, s_w1, s_w2 = (jnp.sum(p, axis=0, keepdims=True) for p in sums[k])
                dfb_ref[:, cols] += s_b
                for tap, s_w in enumerate((s_w0, s_w1, s_w2)):
                    dfw_ref[tap:tap + 1, cols] += s_w
                cw = fw_ref[:, cols]
                for r in range(0, ROW_TILE, ROW_CHUNK):
                    d, u1, u2 = (dup_ref[k, pl.ds(r + s * N_SEG, ROW_CHUNK), :] for s in range(3))
                    out_ref[pl.ds(r, ROW_CHUNK), cols] = (cw[2:3] * d + cw[1:2] * u1 + cw[0:1] * u2).astype(BF16)

    tile = lambda w: pl.BlockSpec((ROW_TILE, w), lambda i: (i, 0))
    halo = lambda w, index: pl.BlockSpec((HALO_ROWS, w), lambda i: (index(i, t), 0))
    const = lambda a: pl.BlockSpec(a.shape, lambda i: (0, 0))
    return pl.pallas_call(
        _ignoring_deps(body, 7, deps), name="ffn_bwd", grid=(n_tiles,),
        in_specs=[tile(f2), halo(f2, _prev_halo), halo(f2, _next_halo), tile(f), halo(f, _next_halo),
                  const(fw), const(fb)] + [ANY_SPEC] * len(deps),
        out_specs=[tile(f2), const(fw), const(fb)],
        out_shape=[jax.ShapeDtypeStruct((t, f2), BF16), jax.ShapeDtypeStruct(fw.shape, F32),
                   jax.ShapeDtypeStruct(fb.shape, F32)],
        scratch_shapes=[pltpu.VMEM((2, ROW_TILE + HALO_ROWS, FFN_COLS), F32),
                        pltpu.VMEM((HALO_ROWS + ROW_TILE + HALO_ROWS, f2), F32)],
        compiler_params=_params(),
    )(up_pre, up_pre, up_pre, dact, dact, fw, fb, *deps)


def _to_segments(a):
    t, c = a.shape
    return a.reshape(N_SEG, t // N_SEG, c).transpose(1, 0, 2).reshape(t, c)


def _from_segments(a):
    t, c = a.shape
    return a.reshape(t // N_SEG, N_SEG, c).transpose(1, 0, 2).reshape(t, c)


def _cmul(ar, ai, br, bi):
    return ar * br - ai * bi, ar * bi + ai * br


def _cpow(ar, ai, n):
    out = None
    while n:
        if n & 1:
            out = (ar, ai) if out is None else _cmul(out[0], out[1], ar, ai)
        ar, ai = _cmul(ar, ai, ar, ai)
        n >>= 1
    return out


def _segment_carries(pr, pi, fr, fi, forward):
    row = lax.broadcasted_iota(jnp.int32, fr.shape, 0)
    edge = row == (0 if forward else N_SEG - 1)
    shift = 1 if forward else N_SEG - 1
    sr, si = jnp.zeros_like(fr), jnp.zeros_like(fi)
    for _ in range(N_SEG - 1):
        tr, ti = _cmul(pr, pi, sr, si)
        sr = jnp.where(edge, 0.0, pltpu.roll(tr + fr, shift, 0))
        si = jnp.where(edge, 0.0, pltpu.roll(ti + fi, shift, 0))
    return sr, si


def _rows(i):
    return pl.ds(pl.multiple_of(i * SUBLANES, SUBLANES), SUBLANES)


def _s5_fwd(proj, u_col, bb_re, bb_im, a_re, a_im, cc_re, cc_im, d_skip):
    t = proj.shape[0]
    nch, _, cs = bb_re.shape
    ds = nch * SSM_CHUNK
    u_blk = u_col // SSM_CHUNK
    steps = t // N_SEG

    def body(u_ref, bbr_ref, bbi_ref, ar_ref, ai_ref, ccr_ref, cci_ref, d_ref, sr_ref, si_ref, y_ref):
        ub = u_ref[...].astype(BF16)
        sr_ref[...] = _dot(ub, bbr_ref[...], NN)
        si_ref[...] = _dot(ub, bbi_ref[...], NN)
        ar = jnp.broadcast_to(ar_ref[...], (N_SEG, cs))
        ai = jnp.broadcast_to(ai_ref[...], (N_SEG, cs))
        zero = jnp.zeros((N_SEG, cs), F32)

        def totals(i, carry):
            tr, ti = _cmul(ar, ai, *carry)
            return tr + sr_ref[_rows(i), :], ti + si_ref[_rows(i), :]

        fr, fi = lax.fori_loop(0, steps, totals, (zero, zero), unroll=SCAN_UNROLL)
        s0r, s0i = _segment_carries(*_cpow(ar, ai, steps), fr, fi, True)

        def scan(i, carry):
            tr, ti = _cmul(ar, ai, *carry)
            nr, ni = tr + sr_ref[_rows(i), :], ti + si_ref[_rows(i), :]
            sr_ref[_rows(i), :] = nr
            si_ref[_rows(i), :] = ni
            return nr, ni

        lax.fori_loop(0, steps, scan, (s0r, s0i), unroll=SCAN_UNROLL)
        y_ref[...] = (_dot(sr_ref[...].astype(BF16), ccr_ref[...], NN)
                      - _dot(si_ref[...].astype(BF16), cci_ref[...], NN)
                      + d_ref[...] * u_ref[...])

    chunk3 = lambda r, c: pl.BlockSpec((None, r, c), lambda j: (j, 0, 0))
    return pl.pallas_call(
        body, name="s5_fwd", grid=(nch,),
        in_specs=[pl.BlockSpec((t, SSM_CHUNK), lambda j: (0, j + u_blk)),
                  chunk3(SSM_CHUNK, cs), chunk3(SSM_CHUNK, cs), chunk3(1, cs), chunk3(1, cs),
                  chunk3(cs, SSM_CHUNK), chunk3(cs, SSM_CHUNK), chunk3(1, SSM_CHUNK)],
        out_specs=[pl.BlockSpec((t, cs), lambda j: (0, j)), pl.BlockSpec((t, cs), lambda j: (0, j)),
                   pl.BlockSpec((t, SSM_CHUNK), lambda j: (0, j))],
        out_shape=[jax.ShapeDtypeStruct((t, nch * cs), F32), jax.ShapeDtypeStruct((t, nch * cs), F32),
                   jax.ShapeDtypeStruct((t, ds), F32)],
        compiler_params=_params(),
    )(proj, bb_re, bb_im, a_re, a_im, cc_re, cc_im, d_skip)


def _s5_bwd(dy, proj, u_col, s_re, s_im, bb_re, bb_im, a_re, a_im, cc_re, cc_im, d_skip, gpc):
    t, ds = dy.shape
    nch, _, cs = bb_re.shape
    u_blk = u_col // SSM_CHUNK
    steps = t // N_SEG

    def body(dy_ref, u_ref, sr_ref, si_ref, bbr_ref, bbi_ref, ar_ref, ai_ref, ccr_ref, cci_ref, d_ref,
             du_ref, dbbr_ref, dbbi_ref, dar_ref, dai_ref, dccr_ref, dcci_ref, dd_ref, gr_ref, gi_ref):
        dyv = dy_ref[...]
        dyb = dyv.astype(BF16)
        gr_ref[...] = _dot(dyb, ccr_ref[...], NT)
        gi_ref[...] = -_dot(dyb, cci_ref[...], NT)
        ar = jnp.broadcast_to(ar_ref[...], (N_SEG, cs))
        ai = -jnp.broadcast_to(ai_ref[...], (N_SEG, cs))
        zero = jnp.zeros((N_SEG, cs), F32)

        def totals(k, carry):
            i = steps - 1 - k
            tr, ti = _cmul(ar, ai, *carry)
            return tr + gr_ref[_rows(i), :], ti + gi_ref[_rows(i), :]

        fr, fi = lax.fori_loop(0, steps, totals, (zero, zero), unroll=SCAN_UNROLL)
        e0r, e0i = _segment_carries(*_cpow(ar, ai, steps), fr, fi, False)

        def step(i, gr, gi, pr, pi, acc_r, acc_i):
            tr, ti = _cmul(ar, ai, gr, gi)
            nr, ni = tr + gr_ref[_rows(i), :], ti + gi_ref[_rows(i), :]
            gr_ref[_rows(i), :] = nr
            gi_ref[_rows(i), :] = ni
            return nr, ni, acc_r + nr * pr + ni * pi, acc_i + ni * pr - nr * pi

        def scan(k, carry):
            i = steps - 1 - k
            gr, gi, acc_r, acc_i = carry
            return step(i, gr, gi, sr_ref[_rows(i - 1), :], si_ref[_rows(i - 1), :], acc_r, acc_i)

        gr, gi, acc_r, acc_i = lax.fori_loop(0, steps - 1, scan, (e0r, e0i, zero, zero), unroll=SCAN_UNROLL)
        row = lax.broadcasted_iota(jnp.int32, (N_SEG, cs), 0)
        last = _rows(steps - 1)
        pr = jnp.where(row == 0, 0.0, pltpu.roll(sr_ref[last, :], 1, 0))
        pi = jnp.where(row == 0, 0.0, pltpu.roll(si_ref[last, :], 1, 0))
        _, _, acc_r, acc_i = step(0, gr, gi, pr, pi, acc_r, acc_i)
        dar_ref[...] = jnp.sum(acc_r, axis=0, keepdims=True)
        dai_ref[...] = jnp.sum(acc_i, axis=0, keepdims=True)

        uv = u_ref[...]
        ub = uv.astype(BF16)
        grb = gr_ref[...].astype(BF16)
        gib = gi_ref[...].astype(BF16)
        du = d_ref[...] * dyv + _dot(grb, bbr_ref[...], NT) + _dot(gib, bbi_ref[...], NT)
        du_ref[...] = du.astype(BF16)
        def put_groups(ref, full):
            for gl in range(gpc):
                ref[gl] = full[gl * hb:(gl + 1) * hb, gl * pb:(gl + 1) * pb]

        put_groups(dbbr_ref, _dot(ub, grb, TN))
        put_groups(dbbi_ref, _dot(ub, gib, TN))
        put_groups(dccr_ref, _dot(dyb, sr_ref[...].astype(BF16), TN))
        put_groups(dcci_ref, -_dot(dyb, si_ref[...].astype(BF16), TN))
        dd_ref[...] = jnp.sum(dyv * uv, axis=0, keepdims=True)

    hb, pb = SSM_CHUNK // gpc, cs // gpc
    groups = pl.BlockSpec((None, gpc, hb, pb), lambda j: (j, 0, 0, 0))
    groups_shape = jax.ShapeDtypeStruct((nch, gpc, hb, pb), F32)
    chunk3 = lambda r, c: pl.BlockSpec((None, r, c), lambda j: (j, 0, 0))
    cols = lambda w: pl.BlockSpec((t, w), lambda j: (0, j))
    return pl.pallas_call(
        body, name="s5_bwd", grid=(nch,),
        in_specs=[cols(SSM_CHUNK), pl.BlockSpec((t, SSM_CHUNK), lambda j: (0, j + u_blk)), cols(cs), cols(cs),
                  chunk3(SSM_CHUNK, cs), chunk3(SSM_CHUNK, cs), chunk3(1, cs), chunk3(1, cs),
                  chunk3(cs, SSM_CHUNK), chunk3(cs, SSM_CHUNK), chunk3(1, SSM_CHUNK)],
        out_specs=[cols(SSM_CHUNK), groups, groups, chunk3(1, cs), chunk3(1, cs), groups, groups,
                   chunk3(1, SSM_CHUNK)],
        out_shape=[jax.ShapeDtypeStruct((t, ds), BF16), groups_shape, groups_shape,
                   jax.ShapeDtypeStruct((nch, 1, cs), F32), jax.ShapeDtypeStruct((nch, 1, cs), F32),
                   groups_shape, groups_shape, jax.ShapeDtypeStruct((nch, 1, SSM_CHUNK), F32)],
        scratch_shapes=[pltpu.VMEM((t, cs), F32), pltpu.VMEM((t, cs), F32)],
        compiler_params=_params(),
    )(dy, proj, s_re, s_im, bb_re, bb_im, a_re, a_im, cc_re, cc_im, d_skip)


def _discretize(lr, li, log_dt, br, bi):
    dt = jnp.exp(log_dt)
    mag = jnp.exp(lr * dt)
    ang = li * dt
    a_re = mag * jnp.cos(ang)
    a_im = mag * jnp.sin(ang)
    den = lr * lr + li * li
    nr = a_re - 1.0
    f_re = (nr * lr + a_im * li) / den
    f_im = (a_im * lr - nr * li) / den
    return a_re, a_im, f_re * br - f_im * bi, f_re * bi + f_im * br


def _whole(shape):
    return pl.BlockSpec(shape, lambda: (0,) * len(shape))


def _disc_fwd(lr, li, log_dt, br, bi, deps=()):
    def body(lr_ref, li_ref, dt_ref, br_ref, bi_ref, ar_ref, ai_ref, bbr_ref, bbi_ref):
        outs = _discretize(lr_ref[...], li_ref[...], dt_ref[...], br_ref[...], bi_ref[...])
        for ref, val in zip((ar_ref, ai_ref, bbr_ref, bbi_ref), outs):
            ref[...] = val

    args = (lr, li, log_dt, br, bi)
    outs = (lr, lr, br, br)
    return pl.pallas_call(
        _ignoring_deps(body, 5, deps), name="disc_fwd",
        in_specs=[_whole(a.shape) for a in args] + [ANY_SPEC] * len(deps),
        out_specs=[_whole(a.shape) for a in outs],
        out_shape=[jax.ShapeDtypeStruct(a.shape, F32) for a in outs],
    )(*args, *deps)


def _disc_bwd(lr, li, log_dt, br, bi, dar, dai, dbbr, dbbi):
    def body(lr_ref, li_ref, dt_ref, br_ref, bi_ref, dar_ref, dai_ref, dbbr_ref, dbbi_ref,
             dlr_ref, dli_ref, ddt_ref, dbr_ref, dbi_ref):
        _, vjp = jax.vjp(_discretize, lr_ref[...], li_ref[...], dt_ref[...], br_ref[...], bi_ref[...])
        grads = vjp((dar_ref[...], dai_ref[...], dbbr_ref[...], dbbi_ref[...]))
        for ref, val in zip((dlr_ref, dli_ref, ddt_ref, dbr_ref, dbi_ref), grads):
            ref[...] = val

    args = (lr, li, log_dt, br, bi, dar, dai, dbbr, dbbi)
    outs = (lr, li, log_dt, br, bi)
    return pl.pallas_call(
        body, name="disc_bwd", in_specs=[_whole(a.shape) for a in args],
        out_specs=[_whole(a.shape) for a in outs],
        out_shape=[jax.ShapeDtypeStruct(a.shape, F32) for a in outs],
    )(*args)


def _adamw(w, g, m, v):
    m = ADAM_B1 * m + (1.0 - ADAM_B1) * g
    v = ADAM_B2 * v + (1.0 - ADAM_B2) * (g * g)
    m_hat = m / ADAM_BC1
    v_hat = v / ADAM_BC2
    delta = -ADAM_LR * (m_hat / (jnp.sqrt(v_hat) + ADAM_EPS) + ADAM_WD * w)
    return delta, m, v


def _adamw_reduce(name, parts, w, m, v):
    _, r, c = parts.shape
    tr = r
    for cand in (256, 176, 128):
        if r % cand == 0:
            tr = cand
            break

    def body(p_ref, w_ref, m_ref, v_ref, g_ref, d_ref, nm_ref, nv_ref):
        g = p_ref[0].astype(F32)
        for k in range(1, N_DEV):
            g = g + p_ref[k].astype(F32)
        delta, nm, nv = _adamw(w_ref[...], g, m_ref[...], v_ref[...])
        g_ref[...] = g
        d_ref[...] = delta
        nm_ref[...] = nm
        nv_ref[...] = nv

    blk = pl.BlockSpec((tr, c), lambda i: (i, 0))
    return pl.pallas_call(
        body, name=name, grid=(r // tr,),
        in_specs=[pl.BlockSpec((N_DEV, tr, c), lambda i: (0, i, 0)), blk, blk, blk],
        out_specs=[blk] * 4, out_shape=[jax.ShapeDtypeStruct((r, c), F32)] * 4,
        compiler_params=_params(),
    )(parts, w, m, v)


def _sum_parts(name, parts):
    _, r, c = parts.shape

    def body(p_ref, o_ref):
        g = p_ref[0]
        for k in range(1, N_DEV):
            g = g + p_ref[k]
        o_ref[...] = g

    return pl.pallas_call(
        body, name=name, in_specs=[_whole(parts.shape)], out_specs=_whole((r, c)),
        out_shape=jax.ShapeDtypeStruct((r, c), F32), compiler_params=_params(),
    )(parts)


def _adamw_many(name, grads, ws, ms, vs):
    n = len(grads)

    def body(*refs):
        ins, outs = refs[:4 * n], refs[4 * n:]
        for i in range(n):
            g, w, m, v = (ins[j * n + i][...] for j in range(4))
            for ref, val in zip((outs[i], outs[n + i], outs[2 * n + i]), _adamw(w, g, m, v)):
                ref[...] = val

    args = (*grads, *ws, *ms, *vs)
    outs = pl.pallas_call(
        body, name=name, in_specs=[_whole(a.shape) for a in args],
        out_specs=[_whole(a.shape) for a in ws] * 3,
        out_shape=[jax.ShapeDtypeStruct(a.shape, F32) for a in ws] * 3, compiler_params=_params(),
    )(*args)
    return outs[:n], outs[n:2 * n], outs[2 * n:]


def _pack(arrays, rows):
    flat = jnp.concatenate([a.reshape(-1) for a in arrays])
    return jnp.pad(flat, (0, rows * LANES - flat.shape[0])).reshape(rows, LANES)


def _unpack(packed, shapes):
    flat = packed.reshape(-1)
    out, off = [], 0
    for s in shapes:
        n = math.prod(s)
        out.append(flat[off:off + n].reshape(s))
        off += n
    return out


def _packed_rows(shapes):
    n = sum(math.prod(s) for s in shapes)
    return -(-n // (SUBLANES * LANES)) * SUBLANES


def _block_diag(x):
    j, g, r, c = x.shape
    eye = jnp.eye(g, dtype=x.dtype)
    return (x[:, :, :, None, :] * eye[None, :, None, :, None]).reshape(j, g * r, g * c)


def kernel(x, meta_tokens, norm_mix_g, w_in, conv_w, ssm_lam_re, ssm_lam_im, ssm_log_dt, ssm_b_re, ssm_b_im, ssm_c_re, ssm_c_im, ssm_d, ssm_w_glu, gain_conv_out, gain_ssm_out, w_out, norm_ffn_g, w_up, ffn_conv_w, ffn_conv_b, w_down, norm_final_g, loss_target, m_meta_tokens, m_norm_mix_g, m_w_in, m_conv_w, m_ssm_lam_re, m_ssm_lam_im, m_ssm_log_dt, m_ssm_b_re, m_ssm_b_im, m_ssm_c_re, m_ssm_c_im, m_ssm_d, m_ssm_w_glu, m_gain_conv_out, m_gain_ssm_out, m_w_out, m_norm_ffn_g, m_w_up, m_ffn_conv_w, m_ffn_conv_b, m_w_down, m_norm_final_g, v_meta_tokens, v_norm_mix_g, v_w_in, v_conv_w, v_ssm_lam_re, v_ssm_lam_im, v_ssm_log_dt, v_ssm_b_re, v_ssm_b_im, v_ssm_c_re, v_ssm_c_im, v_ssm_d, v_ssm_w_glu, v_gain_conv_out, v_gain_ssm_out, v_w_out, v_norm_ffn_g, v_w_up, v_ffn_conv_w, v_ffn_conv_b, v_w_down, v_norm_final_g):
    weights = dict(meta_tokens=meta_tokens, norm_mix_g=norm_mix_g, w_in=w_in, conv_w=conv_w, ssm_lam_re=ssm_lam_re, ssm_lam_im=ssm_lam_im, ssm_log_dt=ssm_log_dt, ssm_b_re=ssm_b_re, ssm_b_im=ssm_b_im, ssm_c_re=ssm_c_re, ssm_c_im=ssm_c_im, ssm_d=ssm_d, ssm_w_glu=ssm_w_glu, gain_conv_out=gain_conv_out, gain_ssm_out=gain_ssm_out, w_out=w_out, norm_ffn_g=norm_ffn_g, w_up=w_up, ffn_conv_w=ffn_conv_w, ffn_conv_b=ffn_conv_b, w_down=w_down, norm_final_g=norm_final_g)
    mom_m = dict(meta_tokens=m_meta_tokens, norm_mix_g=m_norm_mix_g, w_in=m_w_in, conv_w=m_conv_w, ssm_lam_re=m_ssm_lam_re, ssm_lam_im=m_ssm_lam_im, ssm_log_dt=m_ssm_log_dt, ssm_b_re=m_ssm_b_re, ssm_b_im=m_ssm_b_im, ssm_c_re=m_ssm_c_re, ssm_c_im=m_ssm_c_im, ssm_d=m_ssm_d, ssm_w_glu=m_ssm_w_glu, gain_conv_out=m_gain_conv_out, gain_ssm_out=m_gain_ssm_out, w_out=m_w_out, norm_ffn_g=m_norm_ffn_g, w_up=m_w_up, ffn_conv_w=m_ffn_conv_w, ffn_conv_b=m_ffn_conv_b, w_down=m_w_down, norm_final_g=m_norm_final_g)
    mom_v = dict(meta_tokens=v_meta_tokens, norm_mix_g=v_norm_mix_g, w_in=v_w_in, conv_w=v_conv_w, ssm_lam_re=v_ssm_lam_re, ssm_lam_im=v_ssm_lam_im, ssm_log_dt=v_ssm_log_dt, ssm_b_re=v_ssm_b_re, ssm_b_im=v_ssm_b_im, ssm_c_re=v_ssm_c_re, ssm_c_im=v_ssm_c_im, ssm_d=v_ssm_d, ssm_w_glu=v_ssm_w_glu, gain_conv_out=v_gain_conv_out, gain_ssm_out=v_gain_ssm_out, w_out=v_w_out, norm_ffn_g=v_norm_ffn_g, w_up=v_w_up, ffn_conv_w=v_ffn_conv_w, ffn_conv_b=v_ffn_conv_b, w_down=v_w_down, norm_final_g=v_norm_final_g)
    names = list(weights)

    n_meta, d_meta = meta_tokens.shape
    seq, d = x.shape[1], x.shape[2]
    rows_used = n_meta + seq
    t = -(-rows_used // ROW_TILE) * ROW_TILE
    d_in_s = w_in.shape[2]
    dc_s = conv_w.shape[2]
    dc = dc_s * N_DEV
    ds = ssm_w_glu.shape[2]
    n_groups, n_state, grp = ssm_b_re.shape[1:]
    ns = n_groups * n_state
    nch = ds // SSM_CHUNK
    gpc = n_groups // nch
    ff_s = w_up.shape[2]
    dn_s = w_down.shape[1]
    assert 3 * dc + ds == d_in_s * N_DEV and 2 * dn_s == ff_s and t % (N_SEG * SUBLANES) == 0

    small_shard = jnp.concatenate([meta_tokens.reshape(-1), conv_w.reshape(-1), ffn_conv_w.reshape(-1)])
    n_small = small_shard.shape[0]
    small_rows = -(-n_small // LANES)
    small_shard = jnp.pad(small_shard, (0, small_rows * LANES - n_small)).reshape(small_rows, LANES)
    ag, ag_token = _gather2_start("gather_weights_start", [
        small_shard, w_in[0].astype(BF16), ssm_w_glu[0].astype(BF16), w_out[0].astype(BF16),
        jnp.swapaxes(w_up[0], 0, 1).astype(BF16), w_down[0].astype(BF16)])
    fb = ffn_conv_b

    gh = n_groups * grp
    per_h = lambda a: jnp.broadcast_to(a.reshape(n_groups, 1, -1), (n_groups, grp, n_state)).reshape(gh, n_state)
    ghp = lambda a: a.transpose(0, 1, 3, 2).reshape(gh, n_state)
    lr, li, log_dt_e = per_h(ssm_lam_re), per_h(ssm_lam_im), per_h(ssm_log_dt)
    br, bi = ghp(ssm_b_re), ghp(ssm_b_im)
    a_re, a_im, bb_re, bb_im = _disc_fwd(lr, li, log_dt_e, br, bi, deps=(ag_token,))
    cs = gpc * n_state
    chunk_row = lambda a: a.reshape(n_groups, grp, n_state)[:, 0].reshape(nch, 1, cs)
    to_bb = lambda a: _block_diag(a.reshape(nch, gpc, grp, n_state)).astype(BF16)
    to_cc = lambda a: _block_diag(a.reshape(nch, gpc, grp, n_state).transpose(0, 1, 3, 2)).astype(BF16)
    bbm_re, bbm_im = to_bb(bb_re), to_bb(bb_im)
    ccm_re, ccm_im = to_cc(ssm_c_re), to_cc(ssm_c_im)
    a_re_c, a_im_c = chunk_row(a_re), chunk_row(a_im)
    d_skip = ssm_d.reshape(nch, 1, SSM_CHUNK)

    target = _to_segments(jnp.pad(loss_target[0], ((n_meta, t - rows_used), (0, 0))))
    token = _gather2_forward("gather_weights_forward_first", ag, (0, 1), (bbm_im, ccm_im, target))
    (g_small,) = _gather2_wait("gather_weights_wait_small", ag, (0,), token)
    g_small = g_small.reshape(N_DEV, -1)
    o1 = n_meta * d_meta
    o2 = o1 + 3 * dc_s
    meta_full = g_small[:, :o1].reshape(N_DEV, n_meta, d_meta).transpose(1, 0, 2).reshape(n_meta, d)
    conv_w_f = g_small[:, o1:o2].reshape(N_DEV, 3, dc_s).transpose(1, 0, 2).reshape(3, dc)
    fw = g_small[:, o2:o2 + 3 * ff_s].reshape(N_DEV, 3, ff_s).transpose(1, 0, 2).reshape(3, N_DEV * ff_s)
    h0 = _to_segments(jnp.concatenate([meta_full, x[0], jnp.zeros((t - rows_used, d), F32)], axis=0))
    full_t = lambda w: pl.BlockSpec((t, w), lambda *_: (0, 0))

    hn1 = _rms_fwd("norm_mix", h0, norm_mix_g)
    (g_in,) = _gather2_wait("gather_weights_wait_in", ag, (1,), hn1)
    proj = _mm("proj", hn1, g_in, dims=NN, grid=(N_DEV,), a_spec=full_t(d),
               b_spec=pl.BlockSpec((None, d, d_in_s), lambda j: (j, 0, 0)),
               o_spec=pl.BlockSpec((t, d_in_s), lambda j: (0, j)),
               out_shape=jax.ShapeDtypeStruct((t, N_DEV * d_in_s), F32))
    s_re, s_im, y_ssm = _s5_fwd(proj, 3 * dc, bbm_re, bbm_im, a_re_c, a_im_c, ccm_re, ccm_im, d_skip)
    token = _gather2_forward("gather_weights_forward_up", ag, (2, 3, 4), (y_ssm,))
    g_glu, g_out = _gather2_wait("gather_weights_wait_mix", ag, (2, 3), token)
    w_out_f = g_out.reshape(-1, d)
    w_glu_f = g_glu.reshape(ds, ds)
    mixed, z_glu = _mix_fwd(proj, y_ssm, w_glu_f, conv_w_f, gain_conv_out, gain_ssm_out)
    tn_out = 256
    h1, hn2 = _proj_rms_fwd("out_proj_norm", mixed, w_out_f, h0, norm_ffn_g)
    token = _gather2_forward("gather_weights_forward_down", ag, (5,), (hn2,))
    (g_up,) = _gather2_wait("gather_weights_wait_up", ag, (4,), token)
    f2 = N_DEV * ff_s
    pair = 2 * ff_s
    w_up_t = g_up.reshape(f2, d)
    up_pre = _mm("up_proj", hn2, w_up_t, dims=NT, grid=(f2 // pair,), a_spec=full_t(d),
                 b_spec=pl.BlockSpec((pair, d), lambda j: (j, 0)),
                 o_spec=pl.BlockSpec((t, pair), lambda j: (0, j)),
                 out_shape=jax.ShapeDtypeStruct((t, f2), BF16))
    act = _gated_fwd(up_pre, fw, fb)
    (g_down,) = _gather2_wait("gather_weights_wait_down", ag, (5,), act)
    w_down_f = g_down.reshape(f2 // 2, d)

    dh2, dh2_b, loss_part, d_norm_final = _down_loss_bwd(
        act, w_down_f, h1, target, norm_final_g.reshape(1, d), n_meta, rows_used)
    dact = _mm("down_dgrad", dh2_b, w_down_f, dims=NT, grid=(f2 // 2 // pair,), a_spec=full_t(d),
               b_spec=pl.BlockSpec((pair, d), lambda j: (j, 0)),
               o_spec=pl.BlockSpec((t, pair), lambda j: (0, j)),
               out_shape=jax.ShapeDtypeStruct((t, f2 // 2), BF16))
    dw_down = _mm("down_wgrad", act, dh2_b, dims=TN, grid=(f2 // 2 // pair,),
                  a_spec=pl.BlockSpec((t, pair), lambda j: (0, j)), b_spec=full_t(d),
                  o_spec=pl.BlockSpec((pair, d), lambda j: (j, 0)),
                  out_shape=jax.ShapeDtypeStruct((f2 // 2, d), BF16))
    ex_down, token = _send_start("exchange_down_start", [dw_down.reshape(N_DEV, dn_s, d)], gather=False)
    dup_pre, d_fw, d_fb = _gated_bwd(up_pre, dact, fw, fb, deps=(token,))
    dw_up = _mm("up_wgrad", dup_pre, hn2, dims=TN, grid=(f2 // pair,),
                a_spec=pl.BlockSpec((t, pair), lambda j: (0, j)), b_spec=full_t(d),
                o_spec=pl.BlockSpec((pair, d), lambda j: (j, 0)),
                out_shape=jax.ShapeDtypeStruct((f2, d), BF16))
    ex_up, token = _send_start("exchange_up_start", [dw_up.reshape(N_DEV, ff_s, d)], gather=False)
    dh1, dh1_b, d_norm_ffn = _dgrad_rms_bwd("up_dgrad_norm_bwd", dh2, dup_pre, w_up_t, NN, h1, norm_ffn_g,
                                            deps=(token,))
    dmixed = _mm("out_dgrad", dh1_b, w_out_f, dims=NT, grid=((dc + ds) // tn_out,), a_spec=full_t(d),
                 b_spec=pl.BlockSpec((tn_out, d), lambda i: (i, 0)),
                 o_spec=pl.BlockSpec((t, tn_out), lambda i: (0, i)),
                 out_shape=jax.ShapeDtypeStruct((t, dc + ds), F32))
    dw_out = _mm("out_wgrad", mixed, dh1_b, dims=TN, grid=((dc + ds) // tn_out,),
                 a_spec=pl.BlockSpec((t, tn_out), lambda i: (0, i)), b_spec=full_t(d),
                 o_spec=pl.BlockSpec((tn_out, d), lambda i: (i, 0)),
                 out_shape=jax.ShapeDtypeStruct((dc + ds, d), BF16))
    db_gate, dconv, dy_ssm, d_wglu, d_conv_w, d_gain_c, d_gain_s = _mix_bwd1(
        proj, y_ssm, z_glu, dmixed, w_glu_f, conv_w_f, gain_conv_out, gain_ssm_out)
    (du, d_bbm_re, d_bbm_im, d_a_re, d_a_im, d_ccm_re, d_ccm_im, d_dskip) = _s5_bwd(
        dy_ssm, proj, 3 * dc, s_re, s_im, bbm_re, bbm_im, a_re_c, a_im_c, ccm_re, ccm_im, d_skip, gpc)

    from_bb = from_cc = lambda a: a.reshape(gh, n_state)
    first_h = lambda a: jnp.pad(a.reshape(n_groups, 1, n_state), ((0, 0), (0, grp - 1), (0, 0))).reshape(gh, n_state)
    over_h = lambda a: a.reshape(n_groups, grp, n_state).sum(axis=1)
    d_lr, d_li, d_dt_e, d_br, d_bi = _disc_bwd(
        lr, li, log_dt_e, br, bi, first_h(d_a_re), first_h(d_a_im), from_bb(d_bbm_re), from_bb(d_bbm_im))

    rep2d = dict(
        ssm_lam_re=(n_groups, n_state), ssm_lam_im=(n_groups, n_state), ssm_log_dt=(1, n_groups),
        ssm_b_re=(gh, n_state), ssm_b_im=(gh, n_state), ssm_c_re=(gh, n_state),
        ssm_c_im=(gh, n_state), ssm_d=(n_groups, grp), gain_conv_out=(1, dc),
        gain_ssm_out=(1, ds), norm_ffn_g=(1, d), ffn_conv_b=(1, N_DEV * ff_s), norm_final_g=(1, d))
    rep_names = list(rep2d)
    rep_grads = dict(
        ssm_lam_re=over_h(d_lr), ssm_lam_im=over_h(d_li), ssm_log_dt=over_h(d_dt_e).sum(axis=1),
        ssm_b_re=d_br, ssm_b_im=d_bi, ssm_c_re=from_cc(d_ccm_re), ssm_c_im=from_cc(d_ccm_im),
        ssm_d=d_dskip, gain_conv_out=d_gain_c, gain_ssm_out=d_gain_s, norm_ffn_g=d_norm_ffn,
        ffn_conv_b=d_fb, norm_final_g=d_norm_final)
    rep_shapes = [rep2d[n] for n in rep_names] + [(1, 1)]
    rep_rows = _packed_rows(rep_shapes)
    rep_pack = _pack([rep_grads[n] for n in rep_names] + [loss_part], rep_rows)
    ex_mix, token = _send_start("exchange_mix_start", [
        dw_out.reshape(N_DEV, -1, d), d_wglu.astype(BF16).reshape(N_DEV, -1, ds),
        d_conv_w.reshape(3, N_DEV, dc_s).transpose(1, 0, 2),
        d_fw.reshape(3, N_DEV, ff_s).transpose(1, 0, 2), rep_pack],
        gather=[False, False, False, False, True])
    dcdv = _mix_bwd2(proj, dconv, conv_w_f, deps=(token,))
    dproj = jnp.concatenate([db_gate, dcdv, du], axis=1)
    d_in = N_DEV * d_in_s
    w_in_rows = g_in.transpose(1, 0, 2).reshape(d, d_in)
    dw_in = _mm("proj_wgrad", hn1, dproj, dims=TN, grid=(N_DEV,), a_spec=full_t(d),
                b_spec=pl.BlockSpec((t, d_in_s), lambda j: (0, j)),
                o_spec=pl.BlockSpec((None, d, d_in_s), lambda j: (j, 0, 0)),
                out_shape=jax.ShapeDtypeStruct((N_DEV, d, d_in_s), BF16))
    dh0, _, d_norm_mix = _dgrad_rms_bwd("proj_dgrad_norm_bwd", dh1, dproj, w_in_rows, NT, h0, norm_mix_g)
    dh0 = _from_segments(dh0)
    grad_x = dh0[n_meta:rows_used][None]
    d_meta_b = dh0[:n_meta].reshape(n_meta, N_DEV, d_meta).transpose(1, 0, 2)
    ex_in, ex_in_token = _send_start("exchange_in_start", [dw_in, d_meta_b, d_norm_mix],
                                     gather=[False, False, True])

    shard_out = {}
    (rep_parts,) = _send_wait("gather_small_grads_wait", ex_mix, (4,), ex_in_token)
    rep_sum = _sum_parts("sum_small_grads", rep_parts)
    *rep_g, loss = _unpack(rep_sum, rep_shapes)
    loss = loss.reshape(())
    swapped = ("ssm_b_re", "ssm_b_im")
    to2d = lambda n, a: ghp(a) if n in swapped else a.reshape(rep2d[n])
    from2d = lambda n, a: (a.reshape(1, n_groups, grp, n_state).transpose(0, 1, 3, 2) if n in swapped
                           else a.reshape(weights[n].shape))
    as2d = lambda tree: [to2d(n, tree[n]) for n in rep_names]
    rep_res = _adamw_many("adamw_replicated", rep_g, as2d(weights), as2d(mom_m), as2d(mom_v))
    for i, n in enumerate(rep_names):
        shard_out[n] = [from2d(n, r) for r in (rep_g[i], *(res[i] for res in rep_res))]

    def update(n, parts, transposed=False):
        sh = weights[n].shape
        two_d = lambda a: (jnp.swapaxes(a[0], 0, 1) if transposed else a.reshape(parts.shape[1:]))
        res = _adamw_reduce("adamw_" + n, parts, two_d(weights[n]), two_d(mom_m[n]), two_d(mom_v[n]))
        shard_out[n] = [(jnp.swapaxes(r, 0, 1) if transposed else r).reshape(sh) for r in res]
        return res[0]

    (p_down,) = _send_wait("exchange_down_wait", ex_down, (0,), rep_sum)
    done = update("w_down", p_down)
    (p_up,) = _send_wait("exchange_up_wait", ex_up, (0,), done)
    done = update("w_up", p_up, transposed=True)
    p_out, p_glu, p_cw, p_fw = _send_wait("exchange_mix_wait", ex_mix, (0, 1, 2, 3), done)
    update("w_out", p_out)
    update("ssm_w_glu", p_glu)
    update("conv_w", p_cw)
    done = update("ffn_conv_w", p_fw)
    p_in, p_meta, p_nm = _send_wait("exchange_in_wait", ex_in, (0, 1, 2), done)
    update("w_in", p_in)
    update("meta_tokens", p_meta)
    update("norm_mix_g", p_nm)

    grads = [shard_out[n][0] for n in names]
    deltas = [shard_out[n][1] for n in names]
    new_m = [shard_out[n][2] for n in names]
    new_v = [shard_out[n][3] for n in names]
    return (loss, grad_x, *grads, *deltas, *new_m, *new_v)
```

```python
import functools
import math

import jax
import jax.numpy as jnp
from jax import lax
from jax.experimental import pallas as pl
from jax.experimental.pallas import tpu as pltpu

F32 = jnp.float32
BF16 = jnp.bfloat16
MESH = pl.DeviceIdType.MESH

N_DEV = 8
RMS_EPS = 1e-6
ADAM_LR = 0.001
ADAM_B1 = 0.9
ADAM_B2 = 0.999
ADAM_EPS = 1e-08
ADAM_WD = 0.01
ADAM_STEP = 10
ADAM_BC1 = 1.0 - ADAM_B1 ** ADAM_STEP
ADAM_BC2 = 1.0 - ADAM_B2 ** ADAM_STEP

SUBLANES = 8
LANES = 128
ROW_TILE = 128
ROW_CHUNK = 32
WIDE_TILES = (544, 256, 128)
FFN_COLS = 256
STAGE_COLS = 512
N_SEG = 8
HALO_ROWS = 16
SSM_CHUNK = 128
SCAN_UNROLL = 2
VMEM_LIMIT = 48 * 1024 * 1024

NN = ((1,), (0,))
NT = ((1,), (1,))
TN = ((0,), (0,))


def _params(**kw):
    return pltpu.CompilerParams(vmem_limit_bytes=VMEM_LIMIT, **kw)


def _dot(a, b, dims):
    return lax.dot_general(a, b, (dims, ((), ())), preferred_element_type=F32)


def _mean_sq_rsqrt(x):
    return lax.rsqrt(jnp.mean(x * x, axis=-1, keepdims=True) + RMS_EPS)


def _rms_bwd(x, r, g, dy):
    xhat = x * r
    dxh = dy * g
    dx = r * (dxh - xhat * jnp.mean(dxh * xhat, axis=-1, keepdims=True))
    return dx, dy * xhat


def _gelu(y):
    c = math.sqrt(2.0 / math.pi)
    t = jnp.tanh(c * (y + 0.044715 * y * y * y))
    return 0.5 * y * (1.0 + t), t


def _gelu_grad(y, t):
    c = math.sqrt(2.0 / math.pi)
    return 0.5 * (1.0 + t) + 0.5 * y * (1.0 - t * t) * c * (1.0 + 3.0 * 0.044715 * y * y)


def _wrap_prev_halo(halo, first):
    seg = lax.broadcasted_iota(jnp.int32, halo.shape, 0) % N_SEG
    wrapped = jnp.where(seg == 0, 0.0, pltpu.roll(halo, 1, 0))
    return jnp.where(first, wrapped, halo)


def _wrap_next_halo(halo, last):
    seg = lax.broadcasted_iota(jnp.int32, halo.shape, 0) % N_SEG
    wrapped = jnp.where(seg == N_SEG - 1, 0.0, pltpu.roll(halo, halo.shape[0] - 1, 0))
    return jnp.where(last, wrapped, halo)


def _dev_index(p):
    return 4 * p[0] + 2 * p[1] + p[2]


def _allgather(name, shards, deps=()):
    n = len(shards)

    def body(*refs):
        ins, outs = refs[:n], refs[n:2 * n]
        send_sems, recv_sems, local_sems = refs[2 * n:]
        x, y, c = lax.axis_index("x"), lax.axis_index("y"), lax.axis_index("c")
        me, sibling = (x, y, c), (x, y, 1 - c)
        chips = [(1 - x, y), (x, 1 - y), (1 - x, 1 - y)]

        def copy(a, k, block, to, src=None):
            dst = outs[a].at[_dev_index(block)]
            return pltpu.make_async_remote_copy(
                src_ref=dst if src is None else src, dst_ref=dst,
                send_sem=send_sems.at[a, k], recv_sem=recv_sems.at[a, k],
                device_id=to, device_id_type=MESH)

        mine = [pltpu.make_async_copy(ins[a], outs[a].at[_dev_index(me)], local_sems.at[a])
                for a in range(n)]
        for cp in mine:
            cp.start()
        first = []
        for a in range(n):
            first.append(copy(a, 0, me, sibling, src=ins[a]))
            for j, chip in enumerate(chips):
                first.append(copy(a, 1 + j, me, (*chip, c), src=ins[a]))
        for cp in first:
            cp.start()
        passed = []
        for j, chip in enumerate(chips):
            for a in range(n):
                copy(a, 1 + j, (*chip, c), me).wait_recv()
                fwd = copy(a, 4 + j, (*chip, c), sibling)
                fwd.start()
                passed.append(fwd)
        for a in range(n):
            copy(a, 0, sibling, me).wait_recv()
            for j, chip in enumerate(chips):
                copy(a, 4 + j, (*chip, 1 - c), me).wait_recv()
        for cp in first + passed:
            cp.wait_send()
        for cp in mine:
            cp.wait()

    any_spec = pl.BlockSpec(memory_space=pl.ANY)
    return pl.pallas_call(
        _ignoring_deps(body, n, deps), name=name,
        out_shape=[jax.ShapeDtypeStruct((N_DEV,) + s.shape, s.dtype) for s in shards],
        in_specs=[any_spec] * (n + len(deps)), out_specs=[any_spec] * n,
        scratch_shapes=[pltpu.SemaphoreType.DMA((n, 7)), pltpu.SemaphoreType.DMA((n, 7)),
                        pltpu.SemaphoreType.DMA((n,))],
    )(*shards, *deps)


HBM_SPEC = pl.BlockSpec(memory_space=pltpu.HBM)
SEM_SPEC = pl.BlockSpec(memory_space=pltpu.SEMAPHORE)
ANY_SPEC = pl.BlockSpec(memory_space=pl.ANY)
DATAFLOW = pltpu.SideEffectType.DATAFLOW_SIDE_EFFECTING


def _ignoring_deps(body, n_in, deps):
    n_dep = len(deps)

    def wrapped(*refs):
        return body(*refs[:n_in], *refs[n_in + n_dep:])

    return wrapped


def _my_position():
    x, y, c = lax.axis_index("x"), lax.axis_index("y"), lax.axis_index("c")
    return (x, y, c)


def _peer(me, k):
    return tuple((1 - v) if (k >> s) & 1 else v for v, s in zip(me, (2, 1, 0)))


def _split_copies(src_refs, land_refs, send_sems, recv_sems, gather):
    me = _my_position()
    copies = []
    for a, (src, land) in enumerate(zip(src_refs, land_refs)):
        for k in range(1, N_DEV):
            peer = _peer(me, k)
            copies.append(pltpu.make_async_remote_copy(
                src_ref=src if gather[a] else src.at[_dev_index(peer)], dst_ref=land.at[_dev_index(me)],
                send_sem=send_sems[a].at[k - 1], recv_sem=recv_sems[a].at[k - 1],
                device_id=peer, device_id_type=MESH))
    return copies


def _own_slot(block, like_shape):
    me = _dev_index(_my_position())
    return lax.dynamic_update_index_in_dim(lax.empty(like_shape, block.dtype), block, me, 0)


def _send_start(name, srcs, gather):
    n = len(srcs)
    me = _dev_index(_my_position())
    gather = [gather] * n if isinstance(gather, bool) else list(gather)
    lands = [_own_slot(s, (N_DEV,) + s.shape) if g else
             _own_slot(lax.dynamic_index_in_dim(s, me, 0, keepdims=False), s.shape)
             for s, g in zip(srcs, gather)]

    def body(*refs):
        src_refs, land_refs = refs[:n], refs[n:2 * n]
        send_sems, recv_sems = refs[2 * n:3 * n], refs[3 * n:4 * n]
        token = refs[-1]
        for cp in _split_copies(src_refs, land_refs, send_sems, recv_sems, gather):
            cp.start()
        token[...] = jnp.zeros_like(token)

    hbm = lambda a: pltpu.HBM(a.shape, a.dtype)
    sems = [pltpu.SemaphoreType.DMA((N_DEV - 1,))] * n
    outs = pl.pallas_call(
        body, name=name,
        out_shape=(*sems, *sems, *[hbm(s) for s in srcs], *[hbm(l) for l in lands],
                   jax.ShapeDtypeStruct((SUBLANES, LANES), F32)),
        in_specs=[HBM_SPEC] * (2 * n),
        out_specs=(*[SEM_SPEC] * (2 * n), *[HBM_SPEC] * (2 * n), pl.BlockSpec(memory_space=pltpu.VMEM)),
        input_output_aliases={i: 2 * n + i for i in range(2 * n)},
        compiler_params=pltpu.CompilerParams(has_side_effects=DATAFLOW),
    )(*[pltpu.with_memory_space_constraint(a, pltpu.HBM) for a in (*srcs, *lands)])
    state = dict(send=outs[:n], recv=outs[n:2 * n], srcs=outs[2 * n:3 * n], lands=outs[3 * n:4 * n],
                 gather=gather)
    return state, outs[-1]


def _send_wait(name, state, which, after):
    n = len(which)
    pick = lambda key: [state[key][i] for i in which]
    gather = pick("gather")

    def body(*refs):
        src_refs, land_refs = refs[:n], refs[n:2 * n]
        send_sems, recv_sems = refs[2 * n:3 * n], refs[3 * n:4 * n]
        for cp in _split_copies(src_refs, land_refs, send_sems, recv_sems, gather):
            cp.wait_send()
            cp.wait_recv()

    srcs, lands = pick("srcs"), pick("lands")
    hbm = lambda a: pltpu.HBM(a.shape, a.dtype)
    outs = pl.pallas_call(
        body, name=name,
        out_shape=(*[hbm(s) for s in srcs], *[hbm(l) for l in lands]),
        in_specs=[*[HBM_SPEC] * (2 * n), *[SEM_SPEC] * (2 * n), ANY_SPEC],
        out_specs=tuple([HBM_SPEC] * (2 * n)),
        input_output_aliases={i: i for i in range(2 * n)},
        compiler_params=pltpu.CompilerParams(has_side_effects=DATAFLOW),
    )(*srcs, *lands, *pick("send"), *pick("recv"), after)
    return outs[n:]


def _two_level_copies(src_refs, land_refs, sems1, sems2):
    x, y, c = _my_position()
    me, sibling = (x, y, c), (x, y, 1 - c)
    chips = [(1 - x, y), (x, 1 - y), (1 - x, 1 - y)]
    stage1, stage2 = [], []
    for a, land in enumerate(land_refs):
        def copy(src, block, to, sems, k):
            return pltpu.make_async_remote_copy(
                src_ref=src, dst_ref=land.at[_dev_index(block)], send_sem=sems[0][a].at[k],
                recv_sem=sems[1][a].at[k], device_id=to, device_id_type=MESH)
        src = src_refs[a] if src_refs is not None else land.at[_dev_index(me)]
        stage1.append([copy(src, me, sibling, sems1, 0)] +
                      [copy(src, me, (*chip, c), sems1, 1 + j) for j, chip in enumerate(chips)])
        if sems2 is not None:
            stage2.append([copy(land.at[_dev_index((*chip, c))], (*chip, c), sibling, sems2, j)
                           for j, chip in enumerate(chips)])
    return stage1, stage2


def _gather2_start(name, shards):
    n = len(shards)
    lands = [_own_slot(s, (N_DEV,) + s.shape) for s in shards]

    def body(*refs):
        src_refs, land_refs = refs[:n], refs[n:2 * n]
        sems1 = (refs[2 * n:3 * n], refs[3 * n:4 * n])
        stage1, _ = _two_level_copies(src_refs, land_refs, sems1, None)
        for copies in stage1:
            for cp in copies:
                cp.start()
        refs[-1][...] = jnp.zeros_like(refs[-1])

    hbm = lambda a: pltpu.HBM(a.shape, a.dtype)
    sems = [pltpu.SemaphoreType.DMA((4,))] * n
    outs = pl.pallas_call(
        body, name=name,
        out_shape=(*sems, *sems, *[hbm(s) for s in shards], *[hbm(l) for l in lands],
                   jax.ShapeDtypeStruct((SUBLANES, LANES), F32)),
        in_specs=[HBM_SPEC] * (2 * n),
        out_specs=(*[SEM_SPEC] * (2 * n), *[HBM_SPEC] * (2 * n), pl.BlockSpec(memory_space=pltpu.VMEM)),
        input_output_aliases={i: 2 * n + i for i in range(2 * n)},
        compiler_params=pltpu.CompilerParams(has_side_effects=DATAFLOW),
    )(*[pltpu.with_memory_space_constraint(a, pltpu.HBM) for a in (*shards, *lands)])
    state = dict(send1=list(outs[:n]), recv1=list(outs[n:2 * n]), srcs=list(outs[2 * n:3 * n]),
                 lands=list(outs[3 * n:4 * n]), send2={}, recv2={})
    return state, outs[-1]


def _gather2_forward(name, state, which, after):
    n = len(which)
    pick = lambda key: [state[key][i] for i in which]

    def body(*refs):
        land_refs, recv1 = refs[:n], refs[n:2 * n]
        outs = refs[2 * n + len(after):]
        sems2 = (outs[:n], outs[n:2 * n])
        stage1, stage2 = _two_level_copies(None, land_refs, (recv1, recv1), sems2)
        for a in range(n):
            for j in range(3):
                stage1[a][1 + j].wait_recv()
                stage2[a][j].start()
        outs[-1][...] = jnp.zeros_like(outs[-1])

    lands = pick("lands")
    sems = [pltpu.SemaphoreType.DMA((3,))] * n
    outs = pl.pallas_call(
        body, name=name,
        out_shape=(*sems, *sems, *[pltpu.HBM(l.shape, l.dtype) for l in lands],
                   jax.ShapeDtypeStruct((SUBLANES, LANES), F32)),
        in_specs=[*[HBM_SPEC] * n, *[SEM_SPEC] * n, *[ANY_SPEC] * len(after)],
        out_specs=(*[SEM_SPEC] * (2 * n), *[HBM_SPEC] * n, pl.BlockSpec(memory_space=pltpu.VMEM)),
        input_output_aliases={i: 2 * n + i for i in range(n)},
        compiler_params=pltpu.CompilerParams(has_side_effects=DATAFLOW),
    )(*lands, *pick("recv1"), *after)
    for idx, i in enumerate(which):
        state["send2"][i], state["recv2"][i] = outs[idx], outs[n + idx]
        state["lands"][i] = outs[2 * n + idx]
    return outs[-1]


def _gather2_wait(name, state, which, after):
    n = len(which)
    pick = lambda key: [state[key][i] for i in which]

    def body(*refs):
        src_refs, land_refs = refs[:n], refs[n:2 * n]
        sems1 = (refs[2 * n:3 * n], refs[3 * n:4 * n])
        sems2 = (refs[4 * n:5 * n], refs[5 * n:6 * n])
        stage1, stage2 = _two_level_copies(src_refs, land_refs, sems1, sems2)
        for a in range(n):
            for cp in stage1[a]:
                cp.wait_send()
            stage1[a][0].wait_recv()
            for cp in stage2[a]:
                cp.wait_send()
                cp.wait_recv()

    srcs, lands = pick("srcs"), pick("lands")
    hbm = lambda a: pltpu.HBM(a.shape, a.dtype)
    outs = pl.pallas_call(
        body, name=name,
        out_shape=(*[hbm(s) for s in srcs], *[hbm(l) for l in lands]),
        in_specs=[*[HBM_SPEC] * (2 * n), *[SEM_SPEC] * (4 * n), ANY_SPEC],
        out_specs=tuple([HBM_SPEC] * (2 * n)),
        input_output_aliases={i: i for i in range(2 * n)},
        compiler_params=pltpu.CompilerParams(has_side_effects=DATAFLOW),
    )(*srcs, *lands, *pick("send1"), *pick("recv1"), *pick("send2"), *pick("recv2"), after)
    return outs[n:]


def _mm(name, a, b, *, dims, grid, a_spec, b_spec, o_spec, out_shape, acc_shape=None,
        res=None, res_spec=None):
    n_red = grid[-1] if acc_shape is not None else 1
    red_axis = len(grid) - 1

    def body(*refs):
        a_ref, b_ref = refs[0], refs[1]
        r_ref = refs[2] if res is not None else None
        o_ref = refs[3] if res is not None else refs[2]
        part = _dot(a_ref[...], b_ref[...], dims)
        if acc_shape is None:
            if r_ref is not None:
                part = part + r_ref[...]
            o_ref[...] = part.astype(o_ref.dtype)
            return
        acc_ref = refs[-1]
        k = pl.program_id(red_axis)

        @pl.when(k == 0)
        def _():
            acc_ref[...] = part

        @pl.when(k > 0)
        def _():
            acc_ref[...] += part

        @pl.when(k == n_red - 1)
        def _():
            total = acc_ref[...]
            if r_ref is not None:
                total = total + r_ref[...]
            o_ref[...] = total.astype(o_ref.dtype)

    ins, in_specs = [a, b], [a_spec, b_spec]
    if res is not None:
        ins.append(res)
        in_specs.append(res_spec)
    return pl.pallas_call(
        body, name=name, grid=grid, in_specs=in_specs, out_specs=o_spec, out_shape=out_shape,
        scratch_shapes=[pltpu.VMEM(acc_shape, F32)] if acc_shape is not None else [],
        compiler_params=_params(),
    )(*ins)


def _wide_tile(t):
    return next(c for c in WIDE_TILES if t % c == 0)


def _rms_fwd(name, h, g):
    t, d = h.shape
    tile = _wide_tile(t)

    def body(h_ref, g_ref, o_ref):
        x = h_ref[...]
        o_ref[...] = (x * _mean_sq_rsqrt(x) * g_ref[...]).astype(BF16)

    return pl.pallas_call(
        body, name=name, grid=(t // tile,),
        in_specs=[pl.BlockSpec((tile, d), lambda i: (i, 0)), pl.BlockSpec((1, d), lambda i: (0, 0))],
        out_specs=pl.BlockSpec((tile, d), lambda i: (i, 0)),
        out_shape=jax.ShapeDtypeStruct((t, d), BF16), compiler_params=_params(),
    )(h, g)


def _fused_tile(t):
    half = _wide_tile(t) // 2
    return half if half % 16 == 0 and t % half == 0 else _wide_tile(t)


def _dgrad_rms_bwd(name, dres, dy, w, dims, h, g, deps=()):
    t, d = h.shape
    k = dy.shape[1]

    def body(dres_ref, dy_ref, w_ref, h_ref, g_ref, dh_ref, dhb_ref, dg_ref):
        x = h_ref[...]
        dhn = _dot(dy_ref[...], w_ref[...], dims)
        dx, dgt = _rms_bwd(x, _mean_sq_rsqrt(x), g_ref[...], dhn)
        dh = dres_ref[...] + dx
        dh_ref[...] = dh
        dhb_ref[...] = dh.astype(BF16)

        @pl.when(pl.program_id(0) == 0)
        def _():
            dg_ref[...] = jnp.zeros_like(dg_ref)

        dg_ref[...] += jnp.sum(dgt, axis=0, keepdims=True)

    tile = _fused_tile(t)
    row = pl.BlockSpec((tile, d), lambda i: (i, 0))
    vec = pl.BlockSpec((1, d), lambda i: (0, 0))
    return pl.pallas_call(
        _ignoring_deps(body, 5, deps), name=name, grid=(t // tile,),
        in_specs=[row, pl.BlockSpec((tile, k), lambda i: (i, 0)), pl.BlockSpec(w.shape, lambda i: (0, 0)), row,
                  vec] + [ANY_SPEC] * len(deps),
        out_specs=[row, row, vec],
        out_shape=[jax.ShapeDtypeStruct((t, d), F32), jax.ShapeDtypeStruct((t, d), BF16),
                   jax.ShapeDtypeStruct((1, d), F32)],
        compiler_params=_params(),
    )(dres, dy, w, h, g, *deps)


def _down_loss_bwd(act, w_down, h1, target, g, row_lo, row_hi):
    t, d = h1.shape
    f = act.shape[1]
    tile = _fused_tile(t)

    def body(act_ref, w_ref, h_ref, tg_ref, g_ref, dh_ref, dhb_ref, loss_ref, dg_ref):
        i = pl.program_id(0)
        x = h_ref[...] + _dot(act_ref[...], w_ref[...], NN)
        r = _mean_sq_rsqrt(x)
        gv = g_ref[...]
        y = x * r * gv
        rows = i * tile + lax.broadcasted_iota(jnp.int32, (tile, 1), 0)
        time = (rows % N_SEG) * (t // N_SEG) + rows // N_SEG
        valid = jnp.logical_and(time >= row_lo, time < row_hi)
        err = jnp.where(valid, y - tg_ref[...], 0.0)
        dy = err * (1.0 / d)
        dx, dgt = _rms_bwd(x, r, gv, dy)
        dh_ref[...] = dx
        dhb_ref[...] = dx.astype(BF16)

        @pl.when(i == 0)
        def _():
            loss_ref[...] = jnp.zeros_like(loss_ref)
            dg_ref[...] = jnp.zeros_like(dg_ref)

        row_loss = jnp.mean(err * err, axis=-1, keepdims=True)
        loss_ref[...] += 0.5 * jnp.sum(row_loss, axis=0, keepdims=True)
        dg_ref[...] += jnp.sum(dgt, axis=0, keepdims=True)

    row = pl.BlockSpec((tile, d), lambda i: (i, 0))
    vec = pl.BlockSpec((1, d), lambda i: (0, 0))
    return pl.pallas_call(
        body, name="down_proj_loss_bwd", grid=(t // tile,),
        in_specs=[pl.BlockSpec((tile, f), lambda i: (i, 0)), pl.BlockSpec(w_down.shape, lambda i: (0, 0)),
                  row, row, vec],
        out_specs=[row, row, pl.BlockSpec((1, 1), lambda i: (0, 0)), vec],
        out_shape=[jax.ShapeDtypeStruct((t, d), F32), jax.ShapeDtypeStruct((t, d), BF16),
                   jax.ShapeDtypeStruct((1, 1), F32), jax.ShapeDtypeStruct((1, d), F32)],
        compiler_params=_params(),
    )(act, w_down, h1, target, g)


def _prev_halo(i, t):
    return jnp.where(i == 0, t // HALO_ROWS - 1, i * (ROW_TILE // HALO_ROWS) - 1)


def _next_halo(i, t):
    return jnp.where(i == t // ROW_TILE - 1, 0, (i + 1) * (ROW_TILE // HALO_ROWS))


def _causal_taps(cur, halo, first):
    rows = cur.shape[0]
    ext = jnp.concatenate([_wrap_prev_halo(halo, first), cur], axis=0)
    return ext[:rows], ext[N_SEG:N_SEG + rows]


def _anticausal_taps(cur, halo, last):
    rows = cur.shape[0]
    ext = jnp.concatenate([cur, _wrap_next_halo(halo, last)], axis=0)
    return ext[N_SEG:N_SEG + rows], ext[2 * N_SEG:2 * N_SEG + rows]


def _mix_fwd(proj, y, w_glu, conv_w, gain_c, gain_s, h0, w_out, gain_ffn):
    t, d = h0.shape
    dc = conv_w.shape[1]
    ds = y.shape[1]

    def body(p_ref, halo_ref, y_ref, wg_ref, cw_ref, gc_ref, gs_ref, h0_ref, wo_ref, gf_ref,
             mixed_ref, z_ref, h1_ref, hn_ref):
        i = pl.program_id(0)
        p = p_ref[...]
        b, c, v = p[:, :dc], p[:, dc:2 * dc], p[:, 2 * dc:3 * dc]
        cv = c * v
        hp = halo_ref[...]
        x2, x1 = _causal_taps(cv, hp[:, dc:2 * dc] * hp[:, 2 * dc:3 * dc], i == 0)
        cw = cw_ref[...]
        conv = cw[0:1] * x2 + cw[1:2] * x1 + cw[2:3] * cv
        co = b * conv
        mixed_ref[:, :dc] = (co * _mean_sq_rsqrt(co) * gc_ref[...]).astype(BF16)
        g, _ = _gelu(y_ref[...])
        z = _dot(g.astype(BF16), wg_ref[...], NN)
        z_ref[...] = z
        so = g * jax.nn.sigmoid(z)
        mixed_ref[:, dc:] = (so * _mean_sq_rsqrt(so) * gs_ref[...]).astype(BF16)
        h1 = h0_ref[...] + _dot(mixed_ref[...], wo_ref[...], NN)
        h1_ref[...] = h1
        hn_ref[...] = (h1 * _mean_sq_rsqrt(h1) * gf_ref[...]).astype(BF16)

    const = lambda i: (0, 0)
    row = lambda w: pl.BlockSpec((ROW_TILE, w), lambda i: (i, 0))
    return pl.pallas_call(
        body, name="mix_fwd_out_proj", grid=(t // ROW_TILE,),
        in_specs=[row(3 * dc), pl.BlockSpec((HALO_ROWS, 3 * dc), lambda i: (_prev_halo(i, t), 0)), row(ds),
                  pl.BlockSpec((ds, ds), const), pl.BlockSpec(conv_w.shape, const),
                  pl.BlockSpec((1, dc), const), pl.BlockSpec((1, ds), const),
                  row(d), pl.BlockSpec(w_out.shape, const), pl.BlockSpec((1, d), const)],
        out_specs=[row(dc + ds), row(ds), row(d), row(d)],
        out_shape=[jax.ShapeDtypeStruct((t, dc + ds), BF16), jax.ShapeDtypeStruct((t, ds), F32),
                   jax.ShapeDtypeStruct((t, d), F32), jax.ShapeDtypeStruct((t, d), BF16)],
        compiler_params=_params(),
    )(proj, proj, y, w_glu, conv_w, gain_c, gain_s, h0, w_out, gain_ffn)


def _mix_bwd1(proj, y, z, dmixed, w_glu, conv_w, gain_c, gain_s):
    t = proj.shape[0]
    dc = conv_w.shape[1]
    ds = y.shape[1]

    def body(p_ref, halo_ref, y_ref, z_ref, dm_ref, wg_ref, cw_ref, gc_ref, gs_ref,
             db_ref, dconv_ref, dy_ref, dwg_ref, dcw_ref, dgc_ref, dgs_ref):
        i = pl.program_id(0)

        @pl.when(i == 0)
        def _():
            dwg_ref[...] = jnp.zeros_like(dwg_ref)
            dcw_ref[...] = jnp.zeros_like(dcw_ref)
            dgc_ref[...] = jnp.zeros_like(dgc_ref)
            dgs_ref[...] = jnp.zeros_like(dgs_ref)

        p = p_ref[...]
        b, c, v = p[:, :dc], p[:, dc:2 * dc], p[:, 2 * dc:3 * dc]
        cv = c * v
        hp = halo_ref[...]
        x2, x1 = _causal_taps(cv, hp[:, dc:2 * dc] * hp[:, 2 * dc:3 * dc], i == 0)
        cw = cw_ref[...]
        conv = cw[0:1] * x2 + cw[1:2] * x1 + cw[2:3] * cv
        co = b * conv
        dm = dm_ref[...]
        dco, dgt = _rms_bwd(co, _mean_sq_rsqrt(co), gc_ref[...], dm[:, :dc])
        dgc_ref[...] += jnp.sum(dgt, axis=0, keepdims=True)
        db_ref[...] = (dco * conv).astype(BF16)
        dconv = dco * b
        dconv_ref[...] = dconv
        dcw_ref[0:1, :] += jnp.sum(dconv * x2, axis=0, keepdims=True)
        dcw_ref[1:2, :] += jnp.sum(dconv * x1, axis=0, keepdims=True)
        dcw_ref[2:3, :] += jnp.sum(dconv * cv, axis=0, keepdims=True)

        yv = y_ref[...]
        g, th = _gelu(yv)
        sg = jax.nn.sigmoid(z_ref[...])
        so = g * sg
        dso, dgt = _rms_bwd(so, _mean_sq_rsqrt(so), gs_ref[...], dm[:, dc:])
        dgs_ref[...] += jnp.sum(dgt, axis=0, keepdims=True)
        dz = (dso * g * sg * (1.0 - sg)).astype(BF16)
        dg = dso * sg + _dot(dz, wg_ref[...], NT)
        dwg_ref[...] += _dot(g.astype(BF16), dz, TN)
        dy_ref[...] = dg * _gelu_grad(yv, th)

    const = lambda i: (0, 0)
    row = lambda w: pl.BlockSpec((ROW_TILE, w), lambda i: (i, 0))
    return pl.pallas_call(
        body, name="mix_bwd1", grid=(t // ROW_TILE,),
        in_specs=[row(3 * dc), pl.BlockSpec((HALO_ROWS, 3 * dc), lambda i: (_prev_halo(i, t), 0)),
                  row(ds), row(ds), row(dc + ds), pl.BlockSpec((ds, ds), const),
                  pl.BlockSpec(conv_w.shape, const), pl.BlockSpec((1, dc), const),
                  pl.BlockSpec((1, ds), const)],
        out_specs=[row(dc), row(dc), row(ds), pl.BlockSpec((ds, ds), const),
                   pl.BlockSpec(conv_w.shape, const), pl.BlockSpec((1, dc), const),
                   pl.BlockSpec((1, ds), const)],
        out_shape=[jax.ShapeDtypeStruct((t, dc), BF16), jax.ShapeDtypeStruct((t, dc), F32),
                   jax.ShapeDtypeStruct((t, ds), F32), jax.ShapeDtypeStruct((ds, ds), F32),
                   jax.ShapeDtypeStruct(conv_w.shape, F32), jax.ShapeDtypeStruct((1, dc), F32),
                   jax.ShapeDtypeStruct((1, ds), F32)],
        compiler_params=_params(),
    )(proj, proj, y, z, dmixed, w_glu, conv_w, gain_c, gain_s)


def _mix_bwd2(proj, dconv, conv_w, deps=()):
    t = proj.shape[0]
    dc = conv_w.shape[1]
    n_tiles = t // ROW_TILE

    def body(c_ref, v_ref, d_ref, halo_ref, cw_ref, o_ref):
        i = pl.program_id(0)
        d = d_ref[...]
        u1, u2 = _anticausal_taps(d, halo_ref[...], i == n_tiles - 1)
        cw = cw_ref[...]
        dcv = cw[2:3] * d + cw[1:2] * u1 + cw[0:1] * u2
        o_ref[:, :dc] = (dcv * v_ref[...]).astype(BF16)
        o_ref[:, dc:] = (dcv * c_ref[...]).astype(BF16)

    return pl.pallas_call(
        _ignoring_deps(body, 5, deps), name="mix_bwd2", grid=(n_tiles,),
        in_specs=[pl.BlockSpec((ROW_TILE, dc), lambda i: (i, 1)),
                  pl.BlockSpec((ROW_TILE, dc), lambda i: (i, 2)),
                  pl.BlockSpec((ROW_TILE, dc), lambda i: (i, 0)),
                  pl.BlockSpec((HALO_ROWS, dc), lambda i: (_next_halo(i, t), 0)),
                  pl.BlockSpec(conv_w.shape, lambda i: (0, 0))] + [ANY_SPEC] * len(deps),
        out_specs=pl.BlockSpec((ROW_TILE, 2 * dc), lambda i: (i, 0)),
        out_shape=jax.ShapeDtypeStruct((t, 2 * dc), BF16), compiler_params=_params(),
    )(proj, proj, dconv, dconv, conv_w, *deps)


def _stage_rows(stage_ref, row0, src, wrap=None):
    n, width = src.shape
    for c in range(0, width, STAGE_COLS):
        v = src[:, pl.ds(c, STAGE_COLS)].astype(F32)
        stage_ref[pl.ds(row0, n), pl.ds(c, STAGE_COLS)] = v if wrap is None else wrap(v)


def _conv_taps(stage_ref, fw_ref, fb_ref, c0, r, rows=ROW_CHUNK):
    cols = pl.ds(c0, FFN_COLS)
    x0, x1, x2 = (stage_ref[pl.ds(HALO_ROWS + r - k * N_SEG, rows), cols] for k in range(3))
    w = fw_ref[:, cols]
    return w[0:1] * x2 + w[1:2] * x1 + w[2:3] * x0 + fb_ref[:, cols], x2, x1, x0


def _gated_fwd(up_pre, fw, fb):
    t, f2 = up_pre.shape
    f = f2 // 2

    def body(up_ref, halo_ref, fw_ref, fb_ref, act_ref, stage_ref):
        first = pl.program_id(0) == 0
        _stage_rows(stage_ref, 0, halo_ref, lambda v: _wrap_prev_halo(v, first))
        _stage_rows(stage_ref, HALO_ROWS, up_ref)
        for c0 in range(0, f, FFN_COLS):
            for r in range(0, ROW_TILE, ROW_CHUNK):
                a = _conv_taps(stage_ref, fw_ref, fb_ref, c0, r)[0]
                val = _conv_taps(stage_ref, fw_ref, fb_ref, f + c0, r)[0]
                act_ref[pl.ds(r, ROW_CHUNK), pl.ds(c0, FFN_COLS)] = (a * jax.nn.sigmoid(a) * val).astype(BF16)

    const = lambda a: pl.BlockSpec(a.shape, lambda i: (0, 0))
    return pl.pallas_call(
        body, name="ffn_fwd", grid=(t // ROW_TILE,),
        in_specs=[pl.BlockSpec((ROW_TILE, f2), lambda i: (i, 0)),
                  pl.BlockSpec((HALO_ROWS, f2), lambda i: (_prev_halo(i, t), 0)), const(fw), const(fb)],
        out_specs=pl.BlockSpec((ROW_TILE, f), lambda i: (i, 0)),
        out_shape=jax.ShapeDtypeStruct((t, f), BF16),
        scratch_shapes=[pltpu.VMEM((HALO_ROWS + ROW_TILE, f2), F32)], compiler_params=_params(),
    )(up_pre, up_pre, fw, fb)


def _gated_bwd(up_pre, dh, w_down, fw, fb, deps=()):
    t, f2 = up_pre.shape
    f = f2 // 2
    n_tiles = t // ROW_TILE
    chunks = [(r, ROW_CHUNK) for r in range(0, ROW_TILE, ROW_CHUNK)] + [(ROW_TILE, HALO_ROWS)]

    def body(up_ref, prev_ref, next_ref, dh_ref, dh_next_ref, wd_ref, fw_ref, fb_ref,
             out_ref, dfw_ref, dfb_ref, dup_ref, stage_ref, dact_ref):
        first = pl.program_id(0) == 0
        last = pl.program_id(0) == n_tiles - 1

        @pl.when(first)
        def _():
            dfw_ref[...] = jnp.zeros_like(dfw_ref)
            dfb_ref[...] = jnp.zeros_like(dfb_ref)

        dh_next = _wrap_next_halo(dh_next_ref[...].astype(F32), last).astype(BF16)
        dact_ref[pl.ds(0, ROW_TILE), :] = _dot(dh_ref[...], wd_ref[...], NT)
        dact_ref[pl.ds(ROW_TILE, HALO_ROWS), :] = _dot(dh_next, wd_ref[...], NT)
        _stage_rows(stage_ref, 0, prev_ref, lambda v: _wrap_prev_halo(v, first))
        _stage_rows(stage_ref, HALO_ROWS, up_ref)
        _stage_rows(stage_ref, HALO_ROWS + ROW_TILE, next_ref, lambda v: _wrap_next_halo(v, last))
        fold = lambda v: sum(v[s:s + SUBLANES] for s in range(0, ROW_CHUNK, SUBLANES))
        for c0 in range(0, f, FFN_COLS):
            starts = (c0, f + c0)
            sums = [[jnp.zeros((SUBLANES, FFN_COLS), F32)] * 4 for _ in starts]
            for r, rows in chunks:
                taps = [_conv_taps(stage_ref, fw_ref, fb_ref, c, r, rows) for c in starts]
                a, val = taps[0][0], taps[1][0]
                da_ct = dact_ref[pl.ds(r, rows), pl.ds(c0, FFN_COLS)]
                sg = jax.nn.sigmoid(a)
                dup = (da_ct * val * sg * (1.0 + a * (1.0 - sg)), da_ct * a * sg)
                for k in range(2):
                    dup_ref[k, pl.ds(r, rows), :] = dup[k]
                    if r < ROW_TILE:
                        terms = (dup[k], dup[k] * taps[k][1], dup[k] * taps[k][2], dup[k] * taps[k][3])
                        sums[k] = [s + fold(v) for s, v in zip(sums[k], terms)]
            for k, c in enumerate(starts):
                cols = pl.ds(c, FFN_COLS)
                s_b, s_w0, s_w1, s_w2 = (jnp.sum(p, axis=0, keepdims=True) for p in sums[k])
                dfb_ref[:, cols] += s_b
                for tap, s_w in enumerate((s_w0, s_w1, s_w2)):
                    dfw_ref[tap:tap + 1, cols] += s_w
                cw = fw_ref[:, cols]
                for r in range(0, ROW_TILE, ROW_CHUNK):
                    d, u1, u2 = (dup_ref[k, pl.ds(r + s * N_SEG, ROW_CHUNK), :] for s in range(3))
                    out_ref[pl.ds(r, ROW_CHUNK), cols] = (cw[2:3] * d + cw[1:2] * u1 + cw[0:1] * u2).astype(BF16)

    tile = lambda w: pl.BlockSpec((ROW_TILE, w), lambda i: (i, 0))
    halo = lambda w, index: pl.BlockSpec((HALO_ROWS, w), lambda i: (index(i, t), 0))
    const = lambda a: pl.BlockSpec(a.shape, lambda i: (0, 0))
    d = dh.shape[1]
    return pl.pallas_call(
        _ignoring_deps(body, 8, deps), name="down_dgrad_ffn_bwd", grid=(n_tiles,),
        in_specs=[tile(f2), halo(f2, _prev_halo), halo(f2, _next_halo), tile(d), halo(d, _next_halo),
                  const(w_down), const(fw), const(fb)] + [ANY_SPEC] * len(deps),
        out_specs=[tile(f2), const(fw), const(fb)],
        out_shape=[jax.ShapeDtypeStruct((t, f2), BF16), jax.ShapeDtypeStruct(fw.shape, F32),
                   jax.ShapeDtypeStruct(fb.shape, F32)],
        scratch_shapes=[pltpu.VMEM((2, ROW_TILE + HALO_ROWS, FFN_COLS), F32),
                        pltpu.VMEM((HALO_ROWS + ROW_TILE + HALO_ROWS, f2), F32),
                        pltpu.VMEM((ROW_TILE + HALO_ROWS, f), F32)],
        compiler_params=_params(),
    )(up_pre, up_pre, up_pre, dh, dh, w_down, fw, fb, *deps)


def _to_segments(a):
    t, c = a.shape
    return a.reshape(N_SEG, t // N_SEG, c).transpose(1, 0, 2).reshape(t, c)


def _from_segments(a):
    t, c = a.shape
    return a.reshape(t // N_SEG, N_SEG, c).transpose(1, 0, 2).reshape(t, c)


def _cmul(ar, ai, br, bi):
    return ar * br - ai * bi, ar * bi + ai * br


def _cpow(ar, ai, n):
    out = None
    while n:
        if n & 1:
            out = (ar, ai) if out is None else _cmul(out[0], out[1], ar, ai)
        ar, ai = _cmul(ar, ai, ar, ai)
        n >>= 1
    return out


def _segment_carries(pr, pi, fr, fi, forward):
    row = lax.broadcasted_iota(jnp.int32, fr.shape, 0)
    edge = row == (0 if forward else N_SEG - 1)
    shift = 1 if forward else N_SEG - 1
    sr, si = jnp.zeros_like(fr), jnp.zeros_like(fi)
    for _ in range(N_SEG - 1):
        tr, ti = _cmul(pr, pi, sr, si)
        sr = jnp.where(edge, 0.0, pltpu.roll(tr + fr, shift, 0))
        si = jnp.where(edge, 0.0, pltpu.roll(ti + fi, shift, 0))
    return sr, si


def _rows(i):
    return pl.ds(pl.multiple_of(i * SUBLANES, SUBLANES), SUBLANES)


def _s5_fwd(proj, u_col, bb_re, bb_im, a_re, a_im, cc_re, cc_im, d_skip):
    t = proj.shape[0]
    nch, _, cs = bb_re.shape
    ds = nch * SSM_CHUNK
    u_blk = u_col // SSM_CHUNK
    steps = t // N_SEG

    def body(u_ref, bbr_ref, bbi_ref, ar_ref, ai_ref, ccr_ref, cci_ref, d_ref, sr_ref, si_ref, y_ref):
        ub = u_ref[...].astype(BF16)
        sr_ref[...] = _dot(ub, bbr_ref[...], NN)
        si_ref[...] = _dot(ub, bbi_ref[...], NN)
        ar = jnp.broadcast_to(ar_ref[...], (N_SEG, cs))
        ai = jnp.broadcast_to(ai_ref[...], (N_SEG, cs))
        zero = jnp.zeros((N_SEG, cs), F32)

        def totals(i, carry):
            tr, ti = _cmul(ar, ai, *carry)
            return tr + sr_ref[_rows(i), :], ti + si_ref[_rows(i), :]

        fr, fi = lax.fori_loop(0, steps, totals, (zero, zero), unroll=SCAN_UNROLL)
        s0r, s0i = _segment_carries(*_cpow(ar, ai, steps), fr, fi, True)

        def scan(i, carry):
            tr, ti = _cmul(ar, ai, *carry)
            nr, ni = tr + sr_ref[_rows(i), :], ti + si_ref[_rows(i), :]
            sr_ref[_rows(i), :] = nr
            si_ref[_rows(i), :] = ni
            return nr, ni

        lax.fori_loop(0, steps, scan, (s0r, s0i), unroll=SCAN_UNROLL)
        y_ref[...] = (_dot(sr_ref[...].astype(BF16), ccr_ref[...], NN)
                      - _dot(si_ref[...].astype(BF16), cci_ref[...], NN)
                      + d_ref[...] * u_ref[...])

    chunk3 = lambda r, c: pl.BlockSpec((None, r, c), lambda j: (j, 0, 0))
    return pl.pallas_call(
        body, name="s5_fwd", grid=(nch,),
        in_specs=[pl.BlockSpec((t, SSM_CHUNK), lambda j: (0, j + u_blk)),
                  chunk3(SSM_CHUNK, cs), chunk3(SSM_CHUNK, cs), chunk3(1, cs), chunk3(1, cs),
                  chunk3(cs, SSM_CHUNK), chunk3(cs, SSM_CHUNK), chunk3(1, SSM_CHUNK)],
        out_specs=[pl.BlockSpec((t, cs), lambda j: (0, j)), pl.BlockSpec((t, cs), lambda j: (0, j)),
                   pl.BlockSpec((t, SSM_CHUNK), lambda j: (0, j))],
        out_shape=[jax.ShapeDtypeStruct((t, nch * cs), F32), jax.ShapeDtypeStruct((t, nch * cs), F32),
                   jax.ShapeDtypeStruct((t, ds), F32)],
        compiler_params=_params(),
    )(proj, bb_re, bb_im, a_re, a_im, cc_re, cc_im, d_skip)


def _s5_bwd(dy, proj, u_col, s_re, s_im, bb_re, bb_im, a_re, a_im, cc_re, cc_im, d_skip, gpc):
    t, ds = dy.shape
    nch, _, cs = bb_re.shape
    u_blk = u_col // SSM_CHUNK
    steps = t // N_SEG

    def body(dy_ref, u_ref, sr_ref, si_ref, bbr_ref, bbi_ref, ar_ref, ai_ref, ccr_ref, cci_ref, d_ref,
             du_ref, dbbr_ref, dbbi_ref, dar_ref, dai_ref, dccr_ref, dcci_ref, dd_ref, gr_ref, gi_ref):
        dyv = dy_ref[...]
        dyb = dyv.astype(BF16)
        gr_ref[...] = _dot(dyb, ccr_ref[...], NT)
        gi_ref[...] = -_dot(dyb, cci_ref[...], NT)
        ar = jnp.broadcast_to(ar_ref[...], (N_SEG, cs))
        ai = -jnp.broadcast_to(ai_ref[...], (N_SEG, cs))
        zero = jnp.zeros((N_SEG, cs), F32)

        def totals(k, carry):
            i = steps - 1 - k
            tr, ti = _cmul(ar, ai, *carry)
            return tr + gr_ref[_rows(i), :], ti + gi_ref[_rows(i), :]

        fr, fi = lax.fori_loop(0, steps, totals, (zero, zero), unroll=SCAN_UNROLL)
        e0r, e0i = _segment_carries(*_cpow(ar, ai, steps), fr, fi, False)

        def step(i, gr, gi, pr, pi, acc_r, acc_i):
            tr, ti = _cmul(ar, ai, gr, gi)
            nr, ni = tr + gr_ref[_rows(i), :], ti + gi_ref[_rows(i), :]
            gr_ref[_rows(i), :] = nr
            gi_ref[_rows(i), :] = ni
            return nr, ni, acc_r + nr * pr + ni * pi, acc_i + ni * pr - nr * pi

        def scan(k, carry):
            i = steps - 1 - k
            gr, gi, acc_r, acc_i = carry
            return step(i, gr, gi, sr_ref[_rows(i - 1), :], si_ref[_rows(i - 1), :], acc_r, acc_i)

        gr, gi, acc_r, acc_i = lax.fori_loop(0, steps - 1, scan, (e0r, e0i, zero, zero), unroll=SCAN_UNROLL)
        row = lax.broadcasted_iota(jnp.int32, (N_SEG, cs), 0)
        last = _rows(steps - 1)
        pr = jnp.where(row == 0, 0.0, pltpu.roll(sr_ref[last, :], 1, 0))
        pi = jnp.where(row == 0, 0.0, pltpu.roll(si_ref[last, :], 1, 0))
        _, _, acc_r, acc_i = step(0, gr, gi, pr, pi, acc_r, acc_i)
        dar_ref[...] = jnp.sum(acc_r, axis=0, keepdims=True)
        dai_ref[...] = jnp.sum(acc_i, axis=0, keepdims=True)

        uv = u_ref[...]
        ub = uv.astype(BF16)
        grb = gr_ref[...].astype(BF16)
        gib = gi_ref[...].astype(BF16)
        du = d_ref[...] * dyv + _dot(grb, bbr_ref[...], NT) + _dot(gib, bbi_ref[...], NT)
        du_ref[...] = du.astype(BF16)
        def put_groups(ref, full):
            for gl in range(gpc):
                ref[gl] = full[gl * hb:(gl + 1) * hb, gl * pb:(gl + 1) * pb]

        put_groups(dbbr_ref, _dot(ub, grb, TN))
        put_groups(dbbi_ref, _dot(ub, gib, TN))
        put_groups(dccr_ref, _dot(dyb, sr_ref[...].astype(BF16), TN))
        put_groups(dcci_ref, -_dot(dyb, si_ref[...].astype(BF16), TN))
        dd_ref[...] = jnp.sum(dyv * uv, axis=0, keepdims=True)

    hb, pb = SSM_CHUNK // gpc, cs // gpc
    groups = pl.BlockSpec((None, gpc, hb, pb), lambda j: (j, 0, 0, 0))
    groups_shape = jax.ShapeDtypeStruct((nch, gpc, hb, pb), F32)
    chunk3 = lambda r, c: pl.BlockSpec((None, r, c), lambda j: (j, 0, 0))
    cols = lambda w: pl.BlockSpec((t, w), lambda j: (0, j))
    return pl.pallas_call(
        body, name="s5_bwd", grid=(nch,),
        in_specs=[cols(SSM_CHUNK), pl.BlockSpec((t, SSM_CHUNK), lambda j: (0, j + u_blk)), cols(cs), cols(cs),
                  chunk3(SSM_CHUNK, cs), chunk3(SSM_CHUNK, cs), chunk3(1, cs), chunk3(1, cs),
                  chunk3(cs, SSM_CHUNK), chunk3(cs, SSM_CHUNK), chunk3(1, SSM_CHUNK)],
        out_specs=[cols(SSM_CHUNK), groups, groups, chunk3(1, cs), chunk3(1, cs), groups, groups,
                   chunk3(1, SSM_CHUNK)],
        out_shape=[jax.ShapeDtypeStruct((t, ds), BF16), groups_shape, groups_shape,
                   jax.ShapeDtypeStruct((nch, 1, cs), F32), jax.ShapeDtypeStruct((nch, 1, cs), F32),
                   groups_shape, groups_shape, jax.ShapeDtypeStruct((nch, 1, SSM_CHUNK), F32)],
        scratch_shapes=[pltpu.VMEM((t, cs), F32), pltpu.VMEM((t, cs), F32)],
        compiler_params=_params(),
    )(dy, proj, s_re, s_im, bb_re, bb_im, a_re, a_im, cc_re, cc_im, d_skip)


def _discretize(lr, li, log_dt, br, bi):
    dt = jnp.exp(log_dt)
    mag = jnp.exp(lr * dt)
    ang = li * dt
    a_re = mag * jnp.cos(ang)
    a_im = mag * jnp.sin(ang)
    den = lr * lr + li * li
    nr = a_re - 1.0
    f_re = (nr * lr + a_im * li) / den
    f_im = (a_im * lr - nr * li) / den
    return a_re, a_im, f_re * br - f_im * bi, f_re * bi + f_im * br


def _whole(shape):
    return pl.BlockSpec(shape, lambda: (0,) * len(shape))


def _disc_fwd(lr, li, log_dt, br, bi, deps=()):
    def body(lr_ref, li_ref, dt_ref, br_ref, bi_ref, ar_ref, ai_ref, bbr_ref, bbi_ref):
        outs = _discretize(lr_ref[...], li_ref[...], dt_ref[...], br_ref[...], bi_ref[...])
        for ref, val in zip((ar_ref, ai_ref, bbr_ref, bbi_ref), outs):
            ref[...] = val

    args = (lr, li, log_dt, br, bi)
    outs = (lr, lr, br, br)
    return pl.pallas_call(
        _ignoring_deps(body, 5, deps), name="disc_fwd",
        in_specs=[_whole(a.shape) for a in args] + [ANY_SPEC] * len(deps),
        out_specs=[_whole(a.shape) for a in outs],
        out_shape=[jax.ShapeDtypeStruct(a.shape, F32) for a in outs],
    )(*args, *deps)


def _disc_bwd(lr, li, log_dt, br, bi, dar, dai, dbbr, dbbi):
    def body(lr_ref, li_ref, dt_ref, br_ref, bi_ref, dar_ref, dai_ref, dbbr_ref, dbbi_ref,
             dlr_ref, dli_ref, ddt_ref, dbr_ref, dbi_ref):
        _, vjp = jax.vjp(_discretize, lr_ref[...], li_ref[...], dt_ref[...], br_ref[...], bi_ref[...])
        grads = vjp((dar_ref[...], dai_ref[...], dbbr_ref[...], dbbi_ref[...]))
        for ref, val in zip((dlr_ref, dli_ref, ddt_ref, dbr_ref, dbi_ref), grads):
            ref[...] = val

    args = (lr, li, log_dt, br, bi, dar, dai, dbbr, dbbi)
    outs = (lr, li, log_dt, br, bi)
    return pl.pallas_call(
        body, name="disc_bwd", in_specs=[_whole(a.shape) for a in args],
        out_specs=[_whole(a.shape) for a in outs],
        out_shape=[jax.ShapeDtypeStruct(a.shape, F32) for a in outs],
    )(*args)


def _adamw(w, g, m, v):
    m = ADAM_B1 * m + (1.0 - ADAM_B1) * g
    v = ADAM_B2 * v + (1.0 - ADAM_B2) * (g * g)
    m_hat = m / ADAM_BC1
    v_hat = v / ADAM_BC2
    delta = -ADAM_LR * (m_hat / (jnp.sqrt(v_hat) + ADAM_EPS) + ADAM_WD * w)
    return delta, m, v


def _adamw_reduce(name, parts, w, m, v):
    _, r, c = parts.shape
    tr = r
    for cand in (256, 176, 128):
        if r % cand == 0:
            tr = cand
            break

    def body(p_ref, w_ref, m_ref, v_ref, g_ref, d_ref, nm_ref, nv_ref):
        g = p_ref[0].astype(F32)
        for k in range(1, N_DEV):
            g = g + p_ref[k].astype(F32)
        delta, nm, nv = _adamw(w_ref[...], g, m_ref[...], v_ref[...])
        g_ref[...] = g
        d_ref[...] = delta
        nm_ref[...] = nm
        nv_ref[...] = nv

    blk = pl.BlockSpec((tr, c), lambda i: (i, 0))
    return pl.pallas_call(
        body, name=name, grid=(r // tr,),
        in_specs=[pl.BlockSpec((N_DEV, tr, c), lambda i: (0, i, 0)), blk, blk, blk],
        out_specs=[blk] * 4, out_shape=[jax.ShapeDtypeStruct((r, c), F32)] * 4,
        compiler_params=_params(),
    )(parts, w, m, v)


def _sum_parts(name, parts):
    _, r, c = parts.shape

    def body(p_ref, o_ref):
        g = p_ref[0]
        for k in range(1, N_DEV):
            g = g + p_ref[k]
        o_ref[...] = g

    return pl.pallas_call(
        body, name=name, in_specs=[_whole(parts.shape)], out_specs=_whole((r, c)),
        out_shape=jax.ShapeDtypeStruct((r, c), F32), compiler_params=_params(),
    )(parts)


def _adamw_many(name, grads, ws, ms, vs):
    n = len(grads)

    def body(*refs):
        ins, outs = refs[:4 * n], refs[4 * n:]
        for i in range(n):
            g, w, m, v = (ins[j * n + i][...] for j in range(4))
            for ref, val in zip((outs[i], outs[n + i], outs[2 * n + i]), _adamw(w, g, m, v)):
                ref[...] = val

    args = (*grads, *ws, *ms, *vs)
    outs = pl.pallas_call(
        body, name=name, in_specs=[_whole(a.shape) for a in args],
        out_specs=[_whole(a.shape) for a in ws] * 3,
        out_shape=[jax.ShapeDtypeStruct(a.shape, F32) for a in ws] * 3, compiler_params=_params(),
    )(*args)
    return outs[:n], outs[n:2 * n], outs[2 * n:]


def _pack(arrays, rows):
    flat = jnp.concatenate([a.reshape(-1) for a in arrays])
    return jnp.pad(flat, (0, rows * LANES - flat.shape[0])).reshape(rows, LANES)


def _unpack(packed, shapes):
    flat = packed.reshape(-1)
    out, off = [], 0
    for s in shapes:
        n = math.prod(s)
        out.append(flat[off:off + n].reshape(s))
        off += n
    return out


def _packed_rows(shapes):
    n = sum(math.prod(s) for s in shapes)
    return -(-n // (SUBLANES * LANES)) * SUBLANES


def _block_diag(x):
    j, g, r, c = x.shape
    eye = jnp.eye(g, dtype=x.dtype)
    return (x[:, :, :, None, :] * eye[None, :, None, :, None]).reshape(j, g * r, g * c)


def kernel(x, meta_tokens, norm_mix_g, w_in, conv_w, ssm_lam_re, ssm_lam_im, ssm_log_dt, ssm_b_re, ssm_b_im, ssm_c_re, ssm_c_im, ssm_d, ssm_w_glu, gain_conv_out, gain_ssm_out, w_out, norm_ffn_g, w_up, ffn_conv_w, ffn_conv_b, w_down, norm_final_g, loss_target, m_meta_tokens, m_norm_mix_g, m_w_in, m_conv_w, m_ssm_lam_re, m_ssm_lam_im, m_ssm_log_dt, m_ssm_b_re, m_ssm_b_im, m_ssm_c_re, m_ssm_c_im, m_ssm_d, m_ssm_w_glu, m_gain_conv_out, m_gain_ssm_out, m_w_out, m_norm_ffn_g, m_w_up, m_ffn_conv_w, m_ffn_conv_b, m_w_down, m_norm_final_g, v_meta_tokens, v_norm_mix_g, v_w_in, v_conv_w, v_ssm_lam_re, v_ssm_lam_im, v_ssm_log_dt, v_ssm_b_re, v_ssm_b_im, v_ssm_c_re, v_ssm_c_im, v_ssm_d, v_ssm_w_glu, v_gain_conv_out, v_gain_ssm_out, v_w_out, v_norm_ffn_g, v_w_up, v_ffn_conv_w, v_ffn_conv_b, v_w_down, v_norm_final_g):
    weights = dict(meta_tokens=meta_tokens, norm_mix_g=norm_mix_g, w_in=w_in, conv_w=conv_w, ssm_lam_re=ssm_lam_re, ssm_lam_im=ssm_lam_im, ssm_log_dt=ssm_log_dt, ssm_b_re=ssm_b_re, ssm_b_im=ssm_b_im, ssm_c_re=ssm_c_re, ssm_c_im=ssm_c_im, ssm_d=ssm_d, ssm_w_glu=ssm_w_glu, gain_conv_out=gain_conv_out, gain_ssm_out=gain_ssm_out, w_out=w_out, norm_ffn_g=norm_ffn_g, w_up=w_up, ffn_conv_w=ffn_conv_w, ffn_conv_b=ffn_conv_b, w_down=w_down, norm_final_g=norm_final_g)
    mom_m = dict(meta_tokens=m_meta_tokens, norm_mix_g=m_norm_mix_g, w_in=m_w_in, conv_w=m_conv_w, ssm_lam_re=m_ssm_lam_re, ssm_lam_im=m_ssm_lam_im, ssm_log_dt=m_ssm_log_dt, ssm_b_re=m_ssm_b_re, ssm_b_im=m_ssm_b_im, ssm_c_re=m_ssm_c_re, ssm_c_im=m_ssm_c_im, ssm_d=m_ssm_d, ssm_w_glu=m_ssm_w_glu, gain_conv_out=m_gain_conv_out, gain_ssm_out=m_gain_ssm_out, w_out=m_w_out, norm_ffn_g=m_norm_ffn_g, w_up=m_w_up, ffn_conv_w=m_ffn_conv_w, ffn_conv_b=m_ffn_conv_b, w_down=m_w_down, norm_final_g=m_norm_final_g)
    mom_v = dict(meta_tokens=v_meta_tokens, norm_mix_g=v_norm_mix_g, w_in=v_w_in, conv_w=v_conv_w, ssm_lam_re=v_ssm_lam_re, ssm_lam_im=v_ssm_lam_im, ssm_log_dt=v_ssm_log_dt, ssm_b_re=v_ssm_b_re, ssm_b_im=v_ssm_b_im, ssm_c_re=v_ssm_c_re, ssm_c_im=v_ssm_c_im, ssm_d=v_ssm_d, ssm_w_glu=v_ssm_w_glu, gain_conv_out=v_gain_conv_out, gain_ssm_out=v_gain_ssm_out, w_out=v_w_out, norm_ffn_g=v_norm_ffn_g, w_up=v_w_up, ffn_conv_w=v_ffn_conv_w, ffn_conv_b=v_ffn_conv_b, w_down=v_w_down, norm_final_g=v_norm_final_g)
    names = list(weights)

    n_meta, d_meta = meta_tokens.shape
    seq, d = x.shape[1], x.shape[2]
    rows_used = n_meta + seq
    t = -(-rows_used // ROW_TILE) * ROW_TILE
    d_in_s = w_in.shape[2]
    dc_s = conv_w.shape[2]
    dc = dc_s * N_DEV
    ds = ssm_w_glu.shape[2]
    n_groups, n_state, grp = ssm_b_re.shape[1:]
    ns = n_groups * n_state
    nch = ds // SSM_CHUNK
    gpc = n_groups // nch
    ff_s = w_up.shape[2]
    dn_s = w_down.shape[1]
    assert 3 * dc + ds == d_in_s * N_DEV and 2 * dn_s == ff_s and t % (N_SEG * SUBLANES) == 0

    small_shard = jnp.concatenate([meta_tokens.reshape(-1), conv_w.reshape(-1), ffn_conv_w.reshape(-1)])
    n_small = small_shard.shape[0]
    small_rows = -(-n_small // LANES)
    small_shard = jnp.pad(small_shard, (0, small_rows * LANES - n_small)).reshape(small_rows, LANES)
    ag, ag_token = _gather2_start("gather_weights_start", [
        small_shard, w_in[0].astype(BF16), ssm_w_glu[0].astype(BF16), w_out[0].astype(BF16),
        jnp.swapaxes(w_up[0], 0, 1).astype(BF16), w_down[0].astype(BF16)])
    fb = ffn_conv_b

    gh = n_groups * grp
    per_h = lambda a: jnp.broadcast_to(a.reshape(n_groups, 1, -1), (n_groups, grp, n_state)).reshape(gh, n_state)
    ghp = lambda a: a.transpose(0, 1, 3, 2).reshape(gh, n_state)
    lr, li, log_dt_e = per_h(ssm_lam_re), per_h(ssm_lam_im), per_h(ssm_log_dt)
    br, bi = ghp(ssm_b_re), ghp(ssm_b_im)
    a_re, a_im, bb_re, bb_im = _disc_fwd(lr, li, log_dt_e, br, bi, deps=(ag_token,))
    cs = gpc * n_state
    chunk_row = lambda a: a.reshape(n_groups, grp, n_state)[:, 0].reshape(nch, 1, cs)
    to_bb = lambda a: _block_diag(a.reshape(nch, gpc, grp, n_state)).astype(BF16)
    to_cc = lambda a: _block_diag(a.reshape(nch, gpc, grp, n_state).transpose(0, 1, 3, 2)).astype(BF16)
    bbm_re, bbm_im = to_bb(bb_re), to_bb(bb_im)
    ccm_re, ccm_im = to_cc(ssm_c_re), to_cc(ssm_c_im)
    a_re_c, a_im_c = chunk_row(a_re), chunk_row(a_im)
    d_skip = ssm_d.reshape(nch, 1, SSM_CHUNK)

    target = _to_segments(jnp.pad(loss_target[0], ((n_meta, t - rows_used), (0, 0))))
    token = _gather2_forward("gather_weights_forward_first", ag, (0, 1), (bbm_im, ccm_im, target))
    (g_small,) = _gather2_wait("gather_weights_wait_small", ag, (0,), token)
    g_small = g_small.reshape(N_DEV, -1)
    o1 = n_meta * d_meta
    o2 = o1 + 3 * dc_s
    meta_full = g_small[:, :o1].reshape(N_DEV, n_meta, d_meta).transpose(1, 0, 2).reshape(n_meta, d)
    conv_w_f = g_small[:, o1:o2].reshape(N_DEV, 3, dc_s).transpose(1, 0, 2).reshape(3, dc)
    fw = g_small[:, o2:o2 + 3 * ff_s].reshape(N_DEV, 3, ff_s).transpose(1, 0, 2).reshape(3, N_DEV * ff_s)
    h0 = _to_segments(jnp.concatenate([meta_full, x[0], jnp.zeros((t - rows_used, d), F32)], axis=0))
    full_t = lambda w: pl.BlockSpec((t, w), lambda *_: (0, 0))

    hn1 = _rms_fwd("norm_mix", h0, norm_mix_g)
    (g_in,) = _gather2_wait("gather_weights_wait_in", ag, (1,), hn1)
    proj = _mm("proj", hn1, g_in, dims=NN, grid=(N_DEV,), a_spec=full_t(d),
               b_spec=pl.BlockSpec((None, d, d_in_s), lambda j: (j, 0, 0)),
               o_spec=pl.BlockSpec((t, d_in_s), lambda j: (0, j)),
               out_shape=jax.ShapeDtypeStruct((t, N_DEV * d_in_s), F32))
    s_re, s_im, y_ssm = _s5_fwd(proj, 3 * dc, bbm_re, bbm_im, a_re_c, a_im_c, ccm_re, ccm_im, d_skip)
    token = _gather2_forward("gather_weights_forward_up", ag, (2, 3, 4), (y_ssm,))
    g_glu, g_out = _gather2_wait("gather_weights_wait_mix", ag, (2, 3), token)
    w_out_f = g_out.reshape(-1, d)
    w_glu_f = g_glu.reshape(ds, ds)
    mixed, z_glu, h1, hn2 = _mix_fwd(proj, y_ssm, w_glu_f, conv_w_f, gain_conv_out, gain_ssm_out,
                                     h0, w_out_f, norm_ffn_g)
    tn_out = 256
    token = _gather2_forward("gather_weights_forward_down", ag, (5,), (hn2,))
    (g_up,) = _gather2_wait("gather_weights_wait_up", ag, (4,), token)
    f2 = N_DEV * ff_s
    pair = 2 * ff_s
    w_up_t = g_up.reshape(f2, d)
    up_pre = _mm("up_proj", hn2, w_up_t, dims=NT, grid=(f2 // pair,), a_spec=full_t(d),
                 b_spec=pl.BlockSpec((pair, d), lambda j: (j, 0)),
                 o_spec=pl.BlockSpec((t, pair), lambda j: (0, j)),
                 out_shape=jax.ShapeDtypeStruct((t, f2), BF16))
    act = _gated_fwd(up_pre, fw, fb)
    (g_down,) = _gather2_wait("gather_weights_wait_down", ag, (5,), act)
    w_down_f = g_down.reshape(f2 // 2, d)

    dh2, dh2_b, loss_part, d_norm_final = _down_loss_bwd(
        act, w_down_f, h1, target, norm_final_g.reshape(1, d), n_meta, rows_used)
    dw_down = _mm("down_wgrad", act, dh2_b, dims=TN, grid=(f2 // 2 // pair,),
                  a_spec=pl.BlockSpec((t, pair), lambda j: (0, j)), b_spec=full_t(d),
                  o_spec=pl.BlockSpec((pair, d), lambda j: (j, 0)),
                  out_shape=jax.ShapeDtypeStruct((f2 // 2, d), BF16))
    ex_down, token = _send_start("exchange_down_start", [dw_down.reshape(N_DEV, dn_s, d)], gather=False)
    dup_pre, d_fw, d_fb = _gated_bwd(up_pre, dh2_b, w_down_f, fw, fb, deps=(token,))
    dw_up = _mm("up_wgrad", dup_pre, hn2, dims=TN, grid=(f2 // pair,),
                a_spec=pl.BlockSpec((t, pair), lambda j: (0, j)), b_spec=full_t(d),
                o_spec=pl.BlockSpec((pair, d), lambda j: (j, 0)),
                out_shape=jax.ShapeDtypeStruct((f2, d), BF16))
    ex_up, token = _send_start("exchange_up_start", [dw_up.reshape(N_DEV, ff_s, d)], gather=False)
    dh1, dh1_b, d_norm_ffn = _dgrad_rms_bwd("up_dgrad_norm_bwd", dh2, dup_pre, w_up_t, NN, h1, norm_ffn_g,
                                            deps=(token,))
    dmixed = _mm("out_dgrad", dh1_b, w_out_f, dims=NT, grid=((dc + ds) // tn_out,), a_spec=full_t(d),
                 b_spec=pl.BlockSpec((tn_out, d), lambda i: (i, 0)),
                 o_spec=pl.BlockSpec((t, tn_out), lambda i: (0, i)),
                 out_shape=jax.ShapeDtypeStruct((t, dc + ds), F32))
    dw_out = _mm("out_wgrad", mixed, dh1_b, dims=TN, grid=((dc + ds) // tn_out,),
                 a_spec=pl.BlockSpec((t, tn_out), lambda i: (0, i)), b_spec=full_t(d),
                 o_spec=pl.BlockSpec((tn_out, d), lambda i: (i, 0)),
                 out_shape=jax.ShapeDtypeStruct((dc + ds, d), BF16))
    db_gate, dconv, dy_ssm, d_wglu, d_conv_w, d_gain_c, d_gain_s = _mix_bwd1(
        proj, y_ssm, z_glu, dmixed, w_glu_f, conv_w_f, gain_conv_out, gain_ssm_out)
    (du, d_bbm_re, d_bbm_im, d_a_re, d_a_im, d_ccm_re, d_ccm_im, d_dskip) = _s5_bwd(
        dy_ssm, proj, 3 * dc, s_re, s_im, bbm_re, bbm_im, a_re_c, a_im_c, ccm_re, ccm_im, d_skip, gpc)

    from_bb = from_cc = lambda a: a.reshape(gh, n_state)
    first_h = lambda a: jnp.pad(a.reshape(n_groups, 1, n_state), ((0, 0), (0, grp - 1), (0, 0))).reshape(gh, n_state)
    over_h = lambda a: a.reshape(n_groups, grp, n_state).sum(axis=1)
    d_lr, d_li, d_dt_e, d_br, d_bi = _disc_bwd(
        lr, li, log_dt_e, br, bi, first_h(d_a_re), first_h(d_a_im), from_bb(d_bbm_re), from_bb(d_bbm_im))

    rep2d = dict(
        ssm_lam_re=(n_groups, n_state), ssm_lam_im=(n_groups, n_state), ssm_log_dt=(1, n_groups),
        ssm_b_re=(gh, n_state), ssm_b_im=(gh, n_state), ssm_c_re=(gh, n_state),
        ssm_c_im=(gh, n_state), ssm_d=(n_groups, grp), gain_conv_out=(1, dc),
        gain_ssm_out=(1, ds), norm_ffn_g=(1, d), ffn_conv_b=(1, N_DEV * ff_s), norm_final_g=(1, d))
    rep_names = list(rep2d)
    rep_grads = dict(
        ssm_lam_re=over_h(d_lr), ssm_lam_im=over_h(d_li), ssm_log_dt=over_h(d_dt_e).sum(axis=1),
        ssm_b_re=d_br, ssm_b_im=d_bi, ssm_c_re=from_cc(d_ccm_re), ssm_c_im=from_cc(d_ccm_im),
        ssm_d=d_dskip, gain_conv_out=d_gain_c, gain_ssm_out=d_gain_s, norm_ffn_g=d_norm_ffn,
        ffn_conv_b=d_fb, norm_final_g=d_norm_final)
    rep_shapes = [rep2d[n] for n in rep_names] + [(1, 1)]
    rep_rows = _packed_rows(rep_shapes)
    rep_pack = _pack([rep_grads[n] for n in rep_names] + [loss_part], rep_rows)
    ex_mix, token = _send_start("exchange_mix_start", [
        rep_pack, dw_out.reshape(N_DEV, -1, d), d_wglu.astype(BF16).reshape(N_DEV, -1, ds),
        d_conv_w.reshape(3, N_DEV, dc_s).transpose(1, 0, 2),
        d_fw.reshape(3, N_DEV, ff_s).transpose(1, 0, 2)],
        gather=[True, False, False, False, False])
    dcdv = _mix_bwd2(proj, dconv, conv_w_f, deps=(token,))
    dproj = jnp.concatenate([db_gate, dcdv, du], axis=1)
    d_in = N_DEV * d_in_s
    w_in_rows = g_in.transpose(1, 0, 2).reshape(d, d_in)
    dw_in = _mm("proj_wgrad", hn1, dproj, dims=TN, grid=(N_DEV,), a_spec=full_t(d),
                b_spec=pl.BlockSpec((t, d_in_s), lambda j: (0, j)),
                o_spec=pl.BlockSpec((None, d, d_in_s), lambda j: (j, 0, 0)),
                out_shape=jax.ShapeDtypeStruct((N_DEV, d, d_in_s), BF16))
    dh0, _, d_norm_mix = _dgrad_rms_bwd("proj_dgrad_norm_bwd", dh1, dproj, w_in_rows, NT, h0, norm_mix_g)
    dh0 = _from_segments(dh0)
    grad_x = dh0[n_meta:rows_used][None]
    d_meta_b = dh0[:n_meta].reshape(n_meta, N_DEV, d_meta).transpose(1, 0, 2)
    ex_in, ex_in_token = _send_start("exchange_in_start", [dw_in, d_meta_b, d_norm_mix],
                                     gather=[False, False, True])

    shard_out = {}

    def update(n, parts, transposed=False):
        sh = weights[n].shape
        two_d = lambda a: (jnp.swapaxes(a[0], 0, 1) if transposed else a.reshape(parts.shape[1:]))
        res = _adamw_reduce("adamw_" + n, parts, two_d(weights[n]), two_d(mom_m[n]), two_d(mom_v[n]))
        shard_out[n] = [(jnp.swapaxes(r, 0, 1) if transposed else r).reshape(sh) for r in res]
        return res[0]

    (p_down,) = _send_wait("exchange_down_wait", ex_down, (0,), ex_in_token)
    done = update("w_down", p_down)
    (p_up,) = _send_wait("exchange_up_wait", ex_up, (0,), done)
    done = update("w_up", p_up, transposed=True)
    (rep_parts,) = _send_wait("gather_small_grads_wait", ex_mix, (0,), done)
    rep_sum = _sum_parts("sum_small_grads", rep_parts)
    *rep_g, loss = _unpack(rep_sum, rep_shapes)
    loss = loss.reshape(())
    swapped = ("ssm_b_re", "ssm_b_im")
    to2d = lambda n, a: ghp(a) if n in swapped else a.reshape(rep2d[n])
    from2d = lambda n, a: (a.reshape(1, n_groups, grp, n_state).transpose(0, 1, 3, 2) if n in swapped
                           else a.reshape(weights[n].shape))
    as2d = lambda tree: [to2d(n, tree[n]) for n in rep_names]
    rep_res = _adamw_many("adamw_replicated", rep_g, as2d(weights), as2d(mom_m), as2d(mom_v))
    for i, n in enumerate(rep_names):
        shard_out[n] = [from2d(n, r) for r in (rep_g[i], *(res[i] for res in rep_res))]

    p_out, p_glu, p_cw, p_fw = _send_wait("exchange_mix_wait", ex_mix, (1, 2, 3, 4), rep_sum)
    update("w_out", p_out)
    update("ssm_w_glu", p_glu)
    update("conv_w", p_cw)
    done = update("ffn_conv_w", p_fw)
    p_in, p_meta, p_nm = _send_wait("exchange_in_wait", ex_in, (0, 1, 2), done)
    update("w_in", p_in)
    update("meta_tokens", p_meta)
    update("norm_mix_g", p_nm)

    grads = [shard_out[n][0] for n in names]
    deltas = [shard_out[n][1] for n in names]
    new_m = [shard_out[n][2] for n in names]
    new_v = [shard_out[n][3] for n in names]
    return (loss, grad_x, *grads, *deltas, *new_m, *new_v)
```

```python
import functools
import math

import jax
import jax.numpy as jnp
from jax import lax
from jax.experimental import pallas as pl
from jax.experimental.pallas import tpu as pltpu

F32 = jnp.float32
BF16 = jnp.bfloat16
MESH = pl.DeviceIdType.MESH

N_DEV = 8
RMS_EPS = 1e-6
ADAM_LR = 0.001
ADAM_B1 = 0.9
ADAM_B2 = 0.999
ADAM_EPS = 1e-08
ADAM_WD = 0.01
ADAM_STEP = 10
ADAM_BC1 = 1.0 - ADAM_B1 ** ADAM_STEP
ADAM_BC2 = 1.0 - ADAM_B2 ** ADAM_STEP

SUBLANES = 8
LANES = 128
ROW_TILE = 128
ROW_CHUNK = 32
WIDE_TILES = (544, 256, 128)
FFN_COLS = 256
STAGE_COLS = 256
N_SEG = 8
HALO_ROWS = 16
SSM_CHUNK = 128
SCAN_UNROLL = 2
VMEM_LIMIT = 48 * 1024 * 1024

NN = ((1,), (0,))
NT = ((1,), (1,))
TN = ((0,), (0,))


def _params(**kw):
    return pltpu.CompilerParams(vmem_limit_bytes=VMEM_LIMIT, **kw)


def _dot(a, b, dims):
    return lax.dot_general(a, b, (dims, ((), ())), preferred_element_type=F32)


def _mean_sq_rsqrt(x):
    return lax.rsqrt(jnp.mean(x * x, axis=-1, keepdims=True) + RMS_EPS)


def _rms_bwd(x, r, g, dy):
    xhat = x * r
    dxh = dy * g
    dx = r * (dxh - xhat * jnp.mean(dxh * xhat, axis=-1, keepdims=True))
    return dx, dy * xhat


def _gelu(y):
    c = math.sqrt(2.0 / math.pi)
    t = jnp.tanh(c * (y + 0.044715 * y * y * y))
    return 0.5 * y * (1.0 + t), t


def _gelu_grad(y, t):
    c = math.sqrt(2.0 / math.pi)
    return 0.5 * (1.0 + t) + 0.5 * y * (1.0 - t * t) * c * (1.0 + 3.0 * 0.044715 * y * y)


def _wrap_prev_halo(halo, first):
    seg = lax.broadcasted_iota(jnp.int32, halo.shape, 0) % N_SEG
    wrapped = jnp.where(seg == 0, 0.0, pltpu.roll(halo, 1, 0))
    return jnp.where(first, wrapped, halo)


def _wrap_next_halo(halo, last):
    seg = lax.broadcasted_iota(jnp.int32, halo.shape, 0) % N_SEG
    wrapped = jnp.where(seg == N_SEG - 1, 0.0, pltpu.roll(halo, halo.shape[0] - 1, 0))
    return jnp.where(last, wrapped, halo)


def _dev_index(p):
    return 4 * p[0] + 2 * p[1] + p[2]


def _allgather(name, shards, deps=()):
    n = len(shards)

    def body(*refs):
        ins, outs = refs[:n], refs[n:2 * n]
        send_sems, recv_sems, local_sems = refs[2 * n:]
        x, y, c = lax.axis_index("x"), lax.axis_index("y"), lax.axis_index("c")
        me, sibling = (x, y, c), (x, y, 1 - c)
        chips = [(1 - x, y), (x, 1 - y), (1 - x, 1 - y)]

        def copy(a, k, block, to, src=None):
            dst = outs[a].at[_dev_index(block)]
            return pltpu.make_async_remote_copy(
                src_ref=dst if src is None else src, dst_ref=dst,
                send_sem=send_sems.at[a, k], recv_sem=recv_sems.at[a, k],
                device_id=to, device_id_type=MESH)

        mine = [pltpu.make_async_copy(ins[a], outs[a].at[_dev_index(me)], local_sems.at[a])
                for a in range(n)]
        for cp in mine:
            cp.start()
        first = []
        for a in range(n):
            first.append(copy(a, 0, me, sibling, src=ins[a]))
            for j, chip in enumerate(chips):
                first.append(copy(a, 1 + j, me, (*chip, c), src=ins[a]))
        for cp in first:
            cp.start()
        passed = []
        for j, chip in enumerate(chips):
            for a in range(n):
                copy(a, 1 + j, (*chip, c), me).wait_recv()
                fwd = copy(a, 4 + j, (*chip, c), sibling)
                fwd.start()
                passed.append(fwd)
        for a in range(n):
            copy(a, 0, sibling, me).wait_recv()
            for j, chip in enumerate(chips):
                copy(a, 4 + j, (*chip, 1 - c), me).wait_recv()
        for cp in first + passed:
            cp.wait_send()
        for cp in mine:
            cp.wait()

    any_spec = pl.BlockSpec(memory_space=pl.ANY)
    return pl.pallas_call(
        _ignoring_deps(body, n, deps), name=name,
        out_shape=[jax.ShapeDtypeStruct((N_DEV,) + s.shape, s.dtype) for s in shards],
        in_specs=[any_spec] * (n + len(deps)), out_specs=[any_spec] * n,
        scratch_shapes=[pltpu.SemaphoreType.DMA((n, 7)), pltpu.SemaphoreType.DMA((n, 7)),
                        pltpu.SemaphoreType.DMA((n,))],
    )(*shards, *deps)


HBM_SPEC = pl.BlockSpec(memory_space=pltpu.HBM)
SEM_SPEC = pl.BlockSpec(memory_space=pltpu.SEMAPHORE)
ANY_SPEC = pl.BlockSpec(memory_space=pl.ANY)
DATAFLOW = pltpu.SideEffectType.DATAFLOW_SIDE_EFFECTING


def _ignoring_deps(body, n_in, deps):
    n_dep = len(deps)

    def wrapped(*refs):
        return body(*refs[:n_in], *refs[n_in + n_dep:])

    return wrapped


def _my_position():
    x, y, c = lax.axis_index("x"), lax.axis_index("y"), lax.axis_index("c")
    return (x, y, c)


def _peer(me, k):
    return tuple((1 - v) if (k >> s) & 1 else v for v, s in zip(me, (2, 1, 0)))


def _split_copies(src_refs, land_refs, send_sems, recv_sems, gather):
    me = _my_position()
    copies = []
    for a, (src, land) in enumerate(zip(src_refs, land_refs)):
        for k in range(1, N_DEV):
            peer = _peer(me, k)
            copies.append(pltpu.make_async_remote_copy(
                src_ref=src if gather[a] else src.at[_dev_index(peer)], dst_ref=land.at[_dev_index(me)],
                send_sem=send_sems[a].at[k - 1], recv_sem=recv_sems[a].at[k - 1],
                device_id=peer, device_id_type=MESH))
    return copies


def _own_slot(block, like_shape):
    me = _dev_index(_my_position())
    return lax.dynamic_update_index_in_dim(lax.empty(like_shape, block.dtype), block, me, 0)


def _send_start(name, srcs, gather):
    n = len(srcs)
    me = _dev_index(_my_position())
    gather = [gather] * n if isinstance(gather, bool) else list(gather)
    lands = [_own_slot(s, (N_DEV,) + s.shape) if g else
             _own_slot(lax.dynamic_index_in_dim(s, me, 0, keepdims=False), s.shape)
             for s, g in zip(srcs, gather)]

    def body(*refs):
        src_refs, land_refs = refs[:n], refs[n:2 * n]
        send_sems, recv_sems = refs[2 * n:3 * n], refs[3 * n:4 * n]
        token = refs[-1]
        for cp in _split_copies(src_refs, land_refs, send_sems, recv_sems, gather):
            cp.start()
        token[...] = jnp.zeros_like(token)

    hbm = lambda a: pltpu.HBM(a.shape, a.dtype)
    sems = [pltpu.SemaphoreType.DMA((N_DEV - 1,))] * n
    outs = pl.pallas_call(
        body, name=name,
        out_shape=(*sems, *sems, *[hbm(s) for s in srcs], *[hbm(l) for l in lands],
                   jax.ShapeDtypeStruct((SUBLANES, LANES), F32)),
        in_specs=[HBM_SPEC] * (2 * n),
        out_specs=(*[SEM_SPEC] * (2 * n), *[HBM_SPEC] * (2 * n), pl.BlockSpec(memory_space=pltpu.VMEM)),
        input_output_aliases={i: 2 * n + i for i in range(2 * n)},
        compiler_params=pltpu.CompilerParams(has_side_effects=DATAFLOW),
    )(*[pltpu.with_memory_space_constraint(a, pltpu.HBM) for a in (*srcs, *lands)])
    state = dict(send=outs[:n], recv=outs[n:2 * n], srcs=outs[2 * n:3 * n], lands=outs[3 * n:4 * n],
                 gather=gather)
    return state, outs[-1]


def _send_wait(name, state, which, after):
    n = len(which)
    pick = lambda key: [state[key][i] for i in which]
    gather = pick("gather")

    def body(*refs):
        src_refs, land_refs = refs[:n], refs[n:2 * n]
        send_sems, recv_sems = refs[2 * n:3 * n], refs[3 * n:4 * n]
        for cp in _split_copies(src_refs, land_refs, send_sems, recv_sems, gather):
            cp.wait_send()
            cp.wait_recv()

    srcs, lands = pick("srcs"), pick("lands")
    hbm = lambda a: pltpu.HBM(a.shape, a.dtype)
    outs = pl.pallas_call(
        body, name=name,
        out_shape=(*[hbm(s) for s in srcs], *[hbm(l) for l in lands]),
        in_specs=[*[HBM_SPEC] * (2 * n), *[SEM_SPEC] * (2 * n), ANY_SPEC],
        out_specs=tuple([HBM_SPEC] * (2 * n)),
        input_output_aliases={i: i for i in range(2 * n)},
        compiler_params=pltpu.CompilerParams(has_side_effects=DATAFLOW),
    )(*srcs, *lands, *pick("send"), *pick("recv"), after)
    return outs[n:]


def _two_level_copies(src_refs, land_refs, sems1, sems2):
    x, y, c = _my_position()
    me, sibling = (x, y, c), (x, y, 1 - c)
    chips = [(1 - x, y), (x, 1 - y), (1 - x, 1 - y)]
    stage1, stage2 = [], []
    for a, land in enumerate(land_refs):
        def copy(src, block, to, sems, k):
            return pltpu.make_async_remote_copy(
                src_ref=src, dst_ref=land.at[_dev_index(block)], send_sem=sems[0][a].at[k],
                recv_sem=sems[1][a].at[k], device_id=to, device_id_type=MESH)
        src = src_refs[a] if src_refs is not None else land.at[_dev_index(me)]
        stage1.append([copy(src, me, sibling, sems1, 0)] +
                      [copy(src, me, (*chip, c), sems1, 1 + j) for j, chip in enumerate(chips)])
        if sems2 is not None:
            stage2.append([copy(land.at[_dev_index((*chip, c))], (*chip, c), sibling, sems2, j)
                           for j, chip in enumerate(chips)])
    return stage1, stage2


def _gather2_start(name, shards):
    n = len(shards)
    lands = [_own_slot(s, (N_DEV,) + s.shape) for s in shards]

    def body(*refs):
        src_refs, land_refs = refs[:n], refs[n:2 * n]
        sems1 = (refs[2 * n:3 * n], refs[3 * n:4 * n])
        stage1, _ = _two_level_copies(src_refs, land_refs, sems1, None)
        for copies in stage1:
            for cp in copies:
                cp.start()
        refs[-1][...] = jnp.zeros_like(refs[-1])

    hbm = lambda a: pltpu.HBM(a.shape, a.dtype)
    sems = [pltpu.SemaphoreType.DMA((4,))] * n
    outs = pl.pallas_call(
        body, name=name,
        out_shape=(*sems, *sems, *[hbm(s) for s in shards], *[hbm(l) for l in lands],
                   jax.ShapeDtypeStruct((SUBLANES, LANES), F32)),
        in_specs=[HBM_SPEC] * (2 * n),
        out_specs=(*[SEM_SPEC] * (2 * n), *[HBM_SPEC] * (2 * n), pl.BlockSpec(memory_space=pltpu.VMEM)),
        input_output_aliases={i: 2 * n + i for i in range(2 * n)},
        compiler_params=pltpu.CompilerParams(has_side_effects=DATAFLOW),
    )(*[pltpu.with_memory_space_constraint(a, pltpu.HBM) for a in (*shards, *lands)])
    state = dict(send1=list(outs[:n]), recv1=list(outs[n:2 * n]), srcs=list(outs[2 * n:3 * n]),
                 lands=list(outs[3 * n:4 * n]), send2={}, recv2={})
    return state, outs[-1]


def _gather2_forward(name, state, which, after):
    n = len(which)
    pick = lambda key: [state[key][i] for i in which]

    def body(*refs):
        land_refs, recv1 = refs[:n], refs[n:2 * n]
        outs = refs[2 * n + len(after):]
        sems2 = (outs[:n], outs[n:2 * n])
        stage1, stage2 = _two_level_copies(None, land_refs, (recv1, recv1), sems2)
        for a in range(n):
            for j in range(3):
                stage1[a][1 + j].wait_recv()
                stage2[a][j].start()
        outs[-1][...] = jnp.zeros_like(outs[-1])

    lands = pick("lands")
    sems = [pltpu.SemaphoreType.DMA((3,))] * n
    outs = pl.pallas_call(
        body, name=name,
        out_shape=(*sems, *sems, *[pltpu.HBM(l.shape, l.dtype) for l in lands],
                   jax.ShapeDtypeStruct((SUBLANES, LANES), F32)),
        in_specs=[*[HBM_SPEC] * n, *[SEM_SPEC] * n, *[ANY_SPEC] * len(after)],
        out_specs=(*[SEM_SPEC] * (2 * n), *[HBM_SPEC] * n, pl.BlockSpec(memory_space=pltpu.VMEM)),
        input_output_aliases={i: 2 * n + i for i in range(n)},
        compiler_params=pltpu.CompilerParams(has_side_effects=DATAFLOW),
    )(*lands, *pick("recv1"), *after)
    for idx, i in enumerate(which):
        state["send2"][i], state["recv2"][i] = outs[idx], outs[n + idx]
        state["lands"][i] = outs[2 * n + idx]
    return outs[-1]


def _gather2_wait(name, state, which, after):
    n = len(which)
    pick = lambda key: [state[key][i] for i in which]

    def body(*refs):
        src_refs, land_refs = refs[:n], refs[n:2 * n]
        sems1 = (refs[2 * n:3 * n], refs[3 * n:4 * n])
        sems2 = (refs[4 * n:5 * n], refs[5 * n:6 * n])
        stage1, stage2 = _two_level_copies(src_refs, land_refs, sems1, sems2)
        for a in range(n):
            for cp in stage1[a]:
                cp.wait_send()
            stage1[a][0].wait_recv()
            for cp in stage2[a]:
                cp.wait_send()
                cp.wait_recv()

    srcs, lands = pick("srcs"), pick("lands")
    hbm = lambda a: pltpu.HBM(a.shape, a.dtype)
    outs = pl.pallas_call(
        body, name=name,
        out_shape=(*[hbm(s) for s in srcs], *[hbm(l) for l in lands]),
        in_specs=[*[HBM_SPEC] * (2 * n), *[SEM_SPEC] * (4 * n), ANY_SPEC],
        out_specs=tuple([HBM_SPEC] * (2 * n)),
        input_output_aliases={i: i for i in range(2 * n)},
        compiler_params=pltpu.CompilerParams(has_side_effects=DATAFLOW),
    )(*srcs, *lands, *pick("send1"), *pick("recv1"), *pick("send2"), *pick("recv2"), after)
    return outs[n:]


def _mm(name, a, b, *, dims, grid, a_spec, b_spec, o_spec, out_shape, acc_shape=None,
        res=None, res_spec=None):
    n_red = grid[-1] if acc_shape is not None else 1
    red_axis = len(grid) - 1

    def body(*refs):
        a_ref, b_ref = refs[0], refs[1]
        r_ref = refs[2] if res is not None else None
        o_ref = refs[3] if res is not None else refs[2]
        part = _dot(a_ref[...], b_ref[...], dims)
        if acc_shape is None:
            if r_ref is not None:
                part = part + r_ref[...]
            o_ref[...] = part.astype(o_ref.dtype)
            return
        acc_ref = refs[-1]
        k = pl.program_id(red_axis)

        @pl.when(k == 0)
        def _():
            acc_ref[...] = part

        @pl.when(k > 0)
        def _():
            acc_ref[...] += part

        @pl.when(k == n_red - 1)
        def _():
            total = acc_ref[...]
            if r_ref is not None:
                total = total + r_ref[...]
            o_ref[...] = total.astype(o_ref.dtype)

    ins, in_specs = [a, b], [a_spec, b_spec]
    if res is not None:
        ins.append(res)
        in_specs.append(res_spec)
    return pl.pallas_call(
        body, name=name, grid=grid, in_specs=in_specs, out_specs=o_spec, out_shape=out_shape,
        scratch_shapes=[pltpu.VMEM(acc_shape, F32)] if acc_shape is not None else [],
        compiler_params=_params(),
    )(*ins)


def _wide_tile(t):
    return next(c for c in WIDE_TILES if t % c == 0)


def _rms_fwd(name, h, g):
    t, d = h.shape
    tile = _wide_tile(t)

    def body(h_ref, g_ref, o_ref):
        x = h_ref[...]
        o_ref[...] = (x * _mean_sq_rsqrt(x) * g_ref[...]).astype(BF16)

    return pl.pallas_call(
        body, name=name, grid=(t // tile,),
        in_specs=[pl.BlockSpec((tile, d), lambda i: (i, 0)), pl.BlockSpec((1, d), lambda i: (0, 0))],
        out_specs=pl.BlockSpec((tile, d), lambda i: (i, 0)),
        out_shape=jax.ShapeDtypeStruct((t, d), BF16), compiler_params=_params(),
    )(h, g)


def _fused_tile(t):
    half = _wide_tile(t) // 2
    return half if half % 16 == 0 and t % half == 0 else _wide_tile(t)


def _dgrad_rms_bwd(name, dres, dy, w, dims, h, g, deps=()):
    t, d = h.shape
    k = dy.shape[1]

    def body(dres_ref, dy_ref, w_ref, h_ref, g_ref, dh_ref, dhb_ref, dg_ref):
        x = h_ref[...]
        dhn = _dot(dy_ref[...], w_ref[...], dims)
        dx, dgt = _rms_bwd(x, _mean_sq_rsqrt(x), g_ref[...], dhn)
        dh = dres_ref[...] + dx
        dh_ref[...] = dh
        dhb_ref[...] = dh.astype(BF16)

        @pl.when(pl.program_id(0) == 0)
        def _():
            dg_ref[...] = jnp.zeros_like(dg_ref)

        dg_ref[...] += jnp.sum(dgt, axis=0, keepdims=True)

    tile = _fused_tile(t)
    row = pl.BlockSpec((tile, d), lambda i: (i, 0))
    vec = pl.BlockSpec((1, d), lambda i: (0, 0))
    return pl.pallas_call(
        _ignoring_deps(body, 5, deps), name=name, grid=(t // tile,),
        in_specs=[row, pl.BlockSpec((tile, k), lambda i: (i, 0)), pl.BlockSpec(w.shape, lambda i: (0, 0)), row,
                  vec] + [ANY_SPEC] * len(deps),
        out_specs=[row, row, vec],
        out_shape=[jax.ShapeDtypeStruct((t, d), F32), jax.ShapeDtypeStruct((t, d), BF16),
                   jax.ShapeDtypeStruct((1, d), F32)],
        compiler_params=_params(),
    )(dres, dy, w, h, g, *deps)


def _down_loss_bwd(act, w_down, h1, target, g, row_lo, row_hi):
    t, d = h1.shape
    f = act.shape[1]
    tile = _fused_tile(t)

    def body(act_ref, w_ref, h_ref, tg_ref, g_ref, dh_ref, dhb_ref, loss_ref, dg_ref):
        i = pl.program_id(0)
        x = h_ref[...] + _dot(act_ref[...], w_ref[...], NN)
        r = _mean_sq_rsqrt(x)
        gv = g_ref[...]
        y = x * r * gv
        rows = i * tile + lax.broadcasted_iota(jnp.int32, (tile, 1), 0)
        time = (rows % N_SEG) * (t // N_SEG) + rows // N_SEG
        valid = jnp.logical_and(time >= row_lo, time < row_hi)
        err = jnp.where(valid, y - tg_ref[...], 0.0)
        dy = err * (1.0 / d)
        dx, dgt = _rms_bwd(x, r, gv, dy)
        dh_ref[...] = dx
        dhb_ref[...] = dx.astype(BF16)

        @pl.when(i == 0)
        def _():
            loss_ref[...] = jnp.zeros_like(loss_ref)
            dg_ref[...] = jnp.zeros_like(dg_ref)

        row_loss = jnp.mean(err * err, axis=-1, keepdims=True)
        loss_ref[...] += 0.5 * jnp.sum(row_loss, axis=0, keepdims=True)
        dg_ref[...] += jnp.sum(dgt, axis=0, keepdims=True)

    row = pl.BlockSpec((tile, d), lambda i: (i, 0))
    vec = pl.BlockSpec((1, d), lambda i: (0, 0))
    return pl.pallas_call(
        body, name="down_proj_loss_bwd", grid=(t // tile,),
        in_specs=[pl.BlockSpec((tile, f), lambda i: (i, 0)), pl.BlockSpec(w_down.shape, lambda i: (0, 0)),
                  row, row, vec],
        out_specs=[row, row, pl.BlockSpec((1, 1), lambda i: (0, 0)), vec],
        out_shape=[jax.ShapeDtypeStruct((t, d), F32), jax.ShapeDtypeStruct((t, d), BF16),
                   jax.ShapeDtypeStruct((1, 1), F32), jax.ShapeDtypeStruct((1, d), F32)],
        compiler_params=_params(),
    )(act, w_down, h1, target, g)


def _prev_halo(i, t):
    return jnp.where(i == 0, t // HALO_ROWS - 1, i * (ROW_TILE // HALO_ROWS) - 1)


def _next_halo(i, t):
    return jnp.where(i == t // ROW_TILE - 1, 0, (i + 1) * (ROW_TILE // HALO_ROWS))


def _causal_taps(cur, halo, first):
    rows = cur.shape[0]
    ext = jnp.concatenate([_wrap_prev_halo(halo, first), cur], axis=0)
    return ext[:rows], ext[N_SEG:N_SEG + rows]


def _anticausal_taps(cur, halo, last):
    rows = cur.shape[0]
    ext = jnp.concatenate([cur, _wrap_next_halo(halo, last)], axis=0)
    return ext[N_SEG:N_SEG + rows], ext[2 * N_SEG:2 * N_SEG + rows]


def _mix_fwd(proj, y, w_glu, conv_w, gain_c, gain_s, h0, w_out, gain_ffn):
    t, d = h0.shape
    dc = conv_w.shape[1]
    ds = y.shape[1]

    def body(p_ref, halo_ref, y_ref, wg_ref, cw_ref, gc_ref, gs_ref, h0_ref, wo_ref, gf_ref,
             mixed_ref, z_ref, h1_ref, hn_ref):
        i = pl.program_id(0)
        p = p_ref[...]
        b, c, v = p[:, :dc], p[:, dc:2 * dc], p[:, 2 * dc:3 * dc]
        cv = c * v
        hp = halo_ref[...]
        x2, x1 = _causal_taps(cv, hp[:, dc:2 * dc] * hp[:, 2 * dc:3 * dc], i == 0)
        cw = cw_ref[...]
        conv = cw[0:1] * x2 + cw[1:2] * x1 + cw[2:3] * cv
        co = b * conv
        mixed_ref[:, :dc] = (co * _mean_sq_rsqrt(co) * gc_ref[...]).astype(BF16)
        g, _ = _gelu(y_ref[...])
        z = _dot(g.astype(BF16), wg_ref[...], NN)
        z_ref[...] = z
        so = g * jax.nn.sigmoid(z)
        mixed_ref[:, dc:] = (so * _mean_sq_rsqrt(so) * gs_ref[...]).astype(BF16)
        h1 = h0_ref[...] + _dot(mixed_ref[...], wo_ref[...], NN)
        h1_ref[...] = h1
        hn_ref[...] = (h1 * _mean_sq_rsqrt(h1) * gf_ref[...]).astype(BF16)

    const = lambda i: (0, 0)
    row = lambda w: pl.BlockSpec((ROW_TILE, w), lambda i: (i, 0))
    return pl.pallas_call(
        body, name="mix_fwd_out_proj", grid=(t // ROW_TILE,),
        in_specs=[row(3 * dc), pl.BlockSpec((HALO_ROWS, 3 * dc), lambda i: (_prev_halo(i, t), 0)), row(ds),
                  pl.BlockSpec((ds, ds), const), pl.BlockSpec(conv_w.shape, const),
                  pl.BlockSpec((1, dc), const), pl.BlockSpec((1, ds), const),
                  row(d), pl.BlockSpec(w_out.shape, const), pl.BlockSpec((1, d), const)],
        out_specs=[row(dc + ds), row(ds), row(d), row(d)],
        out_shape=[jax.ShapeDtypeStruct((t, dc + ds), BF16), jax.ShapeDtypeStruct((t, ds), F32),
                   jax.ShapeDtypeStruct((t, d), F32), jax.ShapeDtypeStruct((t, d), BF16)],
        compiler_params=_params(),
    )(proj, proj, y, w_glu, conv_w, gain_c, gain_s, h0, w_out, gain_ffn)


def _mix_bwd1(proj, y, z, dmixed, w_glu, conv_w, gain_c, gain_s):
    t = proj.shape[0]
    dc = conv_w.shape[1]
    ds = y.shape[1]

    def body(p_ref, halo_ref, y_ref, z_ref, dm_ref, wg_ref, cw_ref, gc_ref, gs_ref,
             db_ref, dconv_ref, dy_ref, dwg_ref, dcw_ref, dgc_ref, dgs_ref):
        i = pl.program_id(0)

        @pl.when(i == 0)
        def _():
            dwg_ref[...] = jnp.zeros_like(dwg_ref)
            dcw_ref[...] = jnp.zeros_like(dcw_ref)
            dgc_ref[...] = jnp.zeros_like(dgc_ref)
            dgs_ref[...] = jnp.zeros_like(dgs_ref)

        p = p_ref[...]
        b, c, v = p[:, :dc], p[:, dc:2 * dc], p[:, 2 * dc:3 * dc]
        cv = c * v
        hp = halo_ref[...]
        x2, x1 = _causal_taps(cv, hp[:, dc:2 * dc] * hp[:, 2 * dc:3 * dc], i == 0)
        cw = cw_ref[...]
        conv = cw[0:1] * x2 + cw[1:2] * x1 + cw[2:3] * cv
        co = b * conv
        dm = dm_ref[...]
        dco, dgt = _rms_bwd(co, _mean_sq_rsqrt(co), gc_ref[...], dm[:, :dc])
        dgc_ref[...] += jnp.sum(dgt, axis=0, keepdims=True)
        db_ref[...] = (dco * conv).astype(BF16)
        dconv = dco * b
        dconv_ref[...] = dconv
        dcw_ref[0:1, :] += jnp.sum(dconv * x2, axis=0, keepdims=True)
        dcw_ref[1:2, :] += jnp.sum(dconv * x1, axis=0, keepdims=True)
        dcw_ref[2:3, :] += jnp.sum(dconv * cv, axis=0, keepdims=True)

        yv = y_ref[...]
        g, th = _gelu(yv)
        sg = jax.nn.sigmoid(z_ref[...])
        so = g * sg
        dso, dgt = _rms_bwd(so, _mean_sq_rsqrt(so), gs_ref[...], dm[:, dc:])
        dgs_ref[...] += jnp.sum(dgt, axis=0, keepdims=True)
        dz = (dso * g * sg * (1.0 - sg)).astype(BF16)
        dg = dso * sg + _dot(dz, wg_ref[...], NT)
        dwg_ref[...] += _dot(g.astype(BF16), dz, TN)
        dy_ref[...] = dg * _gelu_grad(yv, th)

    const = lambda i: (0, 0)
    row = lambda w: pl.BlockSpec((ROW_TILE, w), lambda i: (i, 0))
    return pl.pallas_call(
        body, name="mix_bwd1", grid=(t // ROW_TILE,),
        in_specs=[row(3 * dc), pl.BlockSpec((HALO_ROWS, 3 * dc), lambda i: (_prev_halo(i, t), 0)),
                  row(ds), row(ds), row(dc + ds), pl.BlockSpec((ds, ds), const),
                  pl.BlockSpec(conv_w.shape, const), pl.BlockSpec((1, dc), const),
                  pl.BlockSpec((1, ds), const)],
        out_specs=[row(dc), row(dc), row(ds), pl.BlockSpec((ds, ds), const),
                   pl.BlockSpec(conv_w.shape, const), pl.BlockSpec((1, dc), const),
                   pl.BlockSpec((1, ds), const)],
        out_shape=[jax.ShapeDtypeStruct((t, dc), BF16), jax.ShapeDtypeStruct((t, dc), F32),
                   jax.ShapeDtypeStruct((t, ds), F32), jax.ShapeDtypeStruct((ds, ds), F32),
                   jax.ShapeDtypeStruct(conv_w.shape, F32), jax.ShapeDtypeStruct((1, dc), F32),
                   jax.ShapeDtypeStruct((1, ds), F32)],
        compiler_params=_params(),
    )(proj, proj, y, z, dmixed, w_glu, conv_w, gain_c, gain_s)


def _mix_bwd2(proj, dconv, conv_w, deps=()):
    t = proj.shape[0]
    dc = conv_w.shape[1]
    n_tiles = t // ROW_TILE

    def body(c_ref, v_ref, d_ref, halo_ref, cw_ref, o_ref):
        i = pl.program_id(0)
        d = d_ref[...]
        u1, u2 = _anticausal_taps(d, halo_ref[...], i == n_tiles - 1)
        cw = cw_ref[...]
        dcv = cw[2:3] * d + cw[1:2] * u1 + cw[0:1] * u2
        o_ref[:, :dc] = (dcv * v_ref[...]).astype(BF16)
        o_ref[:, dc:] = (dcv * c_ref[...]).astype(BF16)

    return pl.pallas_call(
        _ignoring_deps(body, 5, deps), name="mix_bwd2", grid=(n_tiles,),
        in_specs=[pl.BlockSpec((ROW_TILE, dc), lambda i: (i, 1)),
                  pl.BlockSpec((ROW_TILE, dc), lambda i: (i, 2)),
                  pl.BlockSpec((ROW_TILE, dc), lambda i: (i, 0)),
                  pl.BlockSpec((HALO_ROWS, dc), lambda i: (_next_halo(i, t), 0)),
                  pl.BlockSpec(conv_w.shape, lambda i: (0, 0))] + [ANY_SPEC] * len(deps),
        out_specs=pl.BlockSpec((ROW_TILE, 2 * dc), lambda i: (i, 0)),
        out_shape=jax.ShapeDtypeStruct((t, 2 * dc), BF16), compiler_params=_params(),
    )(proj, proj, dconv, dconv, conv_w, *deps)


def _stage_rows(stage_ref, row0, src, wrap=None):
    n, width = src.shape
    for c in range(0, width, STAGE_COLS):
        v = src[:, pl.ds(c, STAGE_COLS)].astype(F32)
        stage_ref[pl.ds(row0, n), pl.ds(c, STAGE_COLS)] = v if wrap is None else wrap(v)


def _conv_taps(stage_ref, fw_ref, fb_ref, c0, r, rows=ROW_CHUNK):
    cols = pl.ds(c0, FFN_COLS)
    x0, x1, x2 = (stage_ref[pl.ds(HALO_ROWS + r - k * N_SEG, rows), cols] for k in range(3))
    w = fw_ref[:, cols]
    return w[0:1] * x2 + w[1:2] * x1 + w[2:3] * x0 + fb_ref[:, cols], x2, x1, x0


def _gated_fwd(up_pre, fw, fb):
    t, f2 = up_pre.shape
    f = f2 // 2

    def body(up_ref, halo_ref, fw_ref, fb_ref, act_ref, stage_ref):
        first = pl.program_id(0) == 0
        _stage_rows(stage_ref, 0, halo_ref, lambda v: _wrap_prev_halo(v, first))
        _stage_rows(stage_ref, HALO_ROWS, up_ref)
        for c0 in range(0, f, FFN_COLS):
            for r in range(0, ROW_TILE, ROW_CHUNK):
                a = _conv_taps(stage_ref, fw_ref, fb_ref, c0, r)[0]
                val = _conv_taps(stage_ref, fw_ref, fb_ref, f + c0, r)[0]
                act_ref[pl.ds(r, ROW_CHUNK), pl.ds(c0, FFN_COLS)] = (a * jax.nn.sigmoid(a) * val).astype(BF16)

    const = lambda a: pl.BlockSpec(a.shape, lambda i: (0, 0))
    return pl.pallas_call(
        body, name="ffn_fwd", grid=(t // ROW_TILE,),
        in_specs=[pl.BlockSpec((ROW_TILE, f2), lambda i: (i, 0)),
                  pl.BlockSpec((HALO_ROWS, f2), lambda i: (_prev_halo(i, t), 0)), const(fw), const(fb)],
        out_specs=pl.BlockSpec((ROW_TILE, f), lambda i: (i, 0)),
        out_shape=jax.ShapeDtypeStruct((t, f), BF16),
        scratch_shapes=[pltpu.VMEM((HALO_ROWS + ROW_TILE, f2), F32)], compiler_params=_params(),
    )(up_pre, up_pre, fw, fb)


def _gated_bwd(up_pre, dact, fw, fb, deps=()):
    t, f2 = up_pre.shape
    f = f2 // 2
    n_tiles = t // ROW_TILE
    chunks = [(r, ROW_CHUNK) for r in range(0, ROW_TILE, ROW_CHUNK)] + [(ROW_TILE, HALO_ROWS)]

    def body(up_ref, prev_ref, next_ref, dact_in_ref, dact_next_ref, fw_ref, fb_ref,
             out_ref, dfw_ref, dfb_ref, dup_ref, stage_ref, dact_ref):
        first = pl.program_id(0) == 0
        last = pl.program_id(0) == n_tiles - 1

        @pl.when(first)
        def _():
            dfw_ref[...] = jnp.zeros_like(dfw_ref)
            dfb_ref[...] = jnp.zeros_like(dfb_ref)

        _stage_rows(dact_ref, 0, dact_in_ref)
        _stage_rows(dact_ref, ROW_TILE, dact_next_ref, lambda v: _wrap_next_halo(v, last))
        _stage_rows(stage_ref, 0, prev_ref, lambda v: _wrap_prev_halo(v, first))
        _stage_rows(stage_ref, HALO_ROWS, up_ref)
        _stage_rows(stage_ref, HALO_ROWS + ROW_TILE, next_ref, lambda v: _wrap_next_halo(v, last))
        fold = lambda v: sum(v[s:s + SUBLANES] for s in range(0, ROW_CHUNK, SUBLANES))
        for c0 in range(0, f, FFN_COLS):
            starts = (c0, f + c0)
            sums = [[jnp.zeros((SUBLANES, FFN_COLS), F32)] * 4 for _ in starts]
            for r, rows in chunks:
                taps = [_conv_taps(stage_ref, fw_ref, fb_ref, c, r, rows) for c in starts]
                a, val = taps[0][0], taps[1][0]
                da_ct = dact_ref[pl.ds(r, rows), pl.ds(c0, FFN_COLS)]
                sg = jax.nn.sigmoid(a)
                dup = (da_ct * val * sg * (1.0 + a * (1.0 - sg)), da_ct * a * sg)
                for k in range(2):
                    dup_ref[k, pl.ds(r, rows), :] = dup[k]
                    if r < ROW_TILE:
                        terms = (dup[k], dup[k] * taps[k][1], dup[k] * taps[k][2], dup[k] * taps[k][3])
                        sums[k] = [s + fold(v) for s, v in zip(sums[k], terms)]
            for k, c in enumerate(starts):
                cols = pl.ds(c, FFN_COLS)
                s_b, s_w0, s_w1, s_w2 = (jnp.sum(p, axis=0, keepdims=True) for p in sums[k])
                dfb_ref[:, cols] += s_b
                for tap, s_w in enumerate((s_w0, s_w1, s_w2)):
                    dfw_ref[tap:tap + 1, cols] += s_w
                cw = fw_ref[:, cols]
                for r in range(0, ROW_TILE, ROW_CHUNK):
                    d, u1, u2 = (dup_ref[k, pl.ds(r + s * N_SEG, ROW_CHUNK), :] for s in range(3))
                    out_ref[pl.ds(r, ROW_CHUNK), cols] = (cw[2:3] * d + cw[1:2] * u1 + cw[0:1] * u2).astype(BF16)

    tile = lambda w: pl.BlockSpec((ROW_TILE, w), lambda i: (i, 0))
    halo = lambda w, index: pl.BlockSpec((HALO_ROWS, w), lambda i: (index(i, t), 0))
    const = lambda a: pl.BlockSpec(a.shape, lambda i: (0, 0))
    return pl.pallas_call(
        _ignoring_deps(body, 7, deps), name="ffn_bwd", grid=(n_tiles,),
        in_specs=[tile(f2), halo(f2, _prev_halo), halo(f2, _next_halo), tile(f), halo(f, _next_halo),
                  const(fw), const(fb)] + [ANY_SPEC] * len(deps),
        out_specs=[tile(f2), const(fw), const(fb)],
        out_shape=[jax.ShapeDtypeStruct((t, f2), BF16), jax.ShapeDtypeStruct(fw.shape, F32),
                   jax.ShapeDtypeStruct(fb.shape, F32)],
        scratch_shapes=[pltpu.VMEM((2, ROW_TILE + HALO_ROWS, FFN_COLS), F32),
                        pltpu.VMEM((HALO_ROWS + ROW_TILE + HALO_ROWS, f2), F32),
                        pltpu.VMEM((ROW_TILE + HALO_ROWS, f), F32)],
        compiler_params=_params(),
    )(up_pre, up_pre, up_pre, dact, dact, fw, fb, *deps)


def _to_segments(a):
    t, c = a.shape
    return a.reshape(N_SEG, t // N_SEG, c).transpose(1, 0, 2).reshape(t, c)


def _from_segments(a):
    t, c = a.shape
    return a.reshape(t // N_SEG, N_SEG, c).transpose(1, 0, 2).reshape(t, c)


def _cmul(ar, ai, br, bi):
    return ar * br - ai * bi, ar * bi + ai * br


def _cpow(ar, ai, n):
    out = None
    while n:
        if n & 1:
            out = (ar, ai) if out is None else _cmul(out[0], out[1], ar, ai)
        ar, ai = _cmul(ar, ai, ar, ai)
        n >>= 1
    return out


def _segment_carries(pr, pi, fr, fi, forward):
    row = lax.broadcasted_iota(jnp.int32, fr.shape, 0)
    edge = row == (0 if forward else N_SEG - 1)
    shift = 1 if forward else N_SEG - 1
    sr, si = jnp.zeros_like(fr), jnp.zeros_like(fi)
    for _ in range(N_SEG - 1):
        tr, ti = _cmul(pr, pi, sr, si)
        sr = jnp.where(edge, 0.0, pltpu.roll(tr + fr, shift, 0))
        si = jnp.where(edge, 0.0, pltpu.roll(ti + fi, shift, 0))
    return sr, si


def _rows(i):
    return pl.ds(pl.multiple_of(i * SUBLANES, SUBLANES), SUBLANES)


def _s5_fwd(proj, u_col, bb_re, bb_im, a_re, a_im, cc_re, cc_im, d_skip):
    t = proj.shape[0]
    nch, _, cs = bb_re.shape
    ds = nch * SSM_CHUNK
    u_blk = u_col // SSM_CHUNK
    steps = t // N_SEG

    def body(u_ref, bbr_ref, bbi_ref, ar_ref, ai_ref, ccr_ref, cci_ref, d_ref, sr_ref, si_ref, y_ref):
        ub = u_ref[...].astype(BF16)
        sr_ref[...] = _dot(ub, bbr_ref[...], NN)
        si_ref[...] = _dot(ub, bbi_ref[...], NN)
        ar = jnp.broadcast_to(ar_ref[...], (N_SEG, cs))
        ai = jnp.broadcast_to(ai_ref[...], (N_SEG, cs))
        zero = jnp.zeros((N_SEG, cs), F32)

        def totals(i, carry):
            tr, ti = _cmul(ar, ai, *carry)
            return tr + sr_ref[_rows(i), :], ti + si_ref[_rows(i), :]

        fr, fi = lax.fori_loop(0, steps, totals, (zero, zero), unroll=SCAN_UNROLL)
        s0r, s0i = _segment_carries(*_cpow(ar, ai, steps), fr, fi, True)

        def scan(i, carry):
            tr, ti = _cmul(ar, ai, *carry)
            nr, ni = tr + sr_ref[_rows(i), :], ti + si_ref[_rows(i), :]
            sr_ref[_rows(i), :] = nr
            si_ref[_rows(i), :] = ni
            return nr, ni

        lax.fori_loop(0, steps, scan, (s0r, s0i), unroll=SCAN_UNROLL)
        y_ref[...] = (_dot(sr_ref[...].astype(BF16), ccr_ref[...], NN)
                      - _dot(si_ref[...].astype(BF16), cci_ref[...], NN)
                      + d_ref[...] * u_ref[...])

    chunk3 = lambda r, c: pl.BlockSpec((None, r, c), lambda j: (j, 0, 0))
    return pl.pallas_call(
        body, name="s5_fwd", grid=(nch,),
        in_specs=[pl.BlockSpec((t, SSM_CHUNK), lambda j: (0, j + u_blk)),
                  chunk3(SSM_CHUNK, cs), chunk3(SSM_CHUNK, cs), chunk3(1, cs), chunk3(1, cs),
                  chunk3(cs, SSM_CHUNK), chunk3(cs, SSM_CHUNK), chunk3(1, SSM_CHUNK)],
        out_specs=[pl.BlockSpec((t, cs), lambda j: (0, j)), pl.BlockSpec((t, cs), lambda j: (0, j)),
                   pl.BlockSpec((t, SSM_CHUNK), lambda j: (0, j))],
        out_shape=[jax.ShapeDtypeStruct((t, nch * cs), F32), jax.ShapeDtypeStruct((t, nch * cs), F32),
                   jax.ShapeDtypeStruct((t, ds), F32)],
        compiler_params=_params(),
    )(proj, bb_re, bb_im, a_re, a_im, cc_re, cc_im, d_skip)


def _s5_bwd(dy, proj, u_col, s_re, s_im, bb_re, bb_im, a_re, a_im, cc_re, cc_im, d_skip, gpc):
    t, ds = dy.shape
    nch, _, cs = bb_re.shape
    u_blk = u_col // SSM_CHUNK
    steps = t // N_SEG

    def body(dy_ref, u_ref, sr_ref, si_ref, bbr_ref, bbi_ref, ar_ref, ai_ref, ccr_ref, cci_ref, d_ref,
             du_ref, dbbr_ref, dbbi_ref, dar_ref, dai_ref, dccr_ref, dcci_ref, dd_ref, gr_ref, gi_ref):
        dyv = dy_ref[...]
        dyb = dyv.astype(BF16)
        gr_ref[...] = _dot(dyb, ccr_ref[...], NT)
        gi_ref[...] = -_dot(dyb, cci_ref[...], NT)
        ar = jnp.broadcast_to(ar_ref[...], (N_SEG, cs))
        ai = -jnp.broadcast_to(ai_ref[...], (N_SEG, cs))
        zero = jnp.zeros((N_SEG, cs), F32)

        def totals(k, carry):
            i = steps - 1 - k
            tr, ti = _cmul(ar, ai, *carry)
            return tr + gr_ref[_rows(i), :], ti + gi_ref[_rows(i), :]

        fr, fi = lax.fori_loop(0, steps, totals, (zero, zero), unroll=SCAN_UNROLL)
        e0r, e0i = _segment_carries(*_cpow(ar, ai, steps), fr, fi, False)

        def step(i, gr, gi, pr, pi, acc_r, acc_i):
            tr, ti = _cmul(ar, ai, gr, gi)
            nr, ni = tr + gr_ref[_rows(i), :], ti + gi_ref[_rows(i), :]
            gr_ref[_rows(i), :] = nr
            gi_ref[_rows(i), :] = ni
            return nr, ni, acc_r + nr * pr + ni * pi, acc_i + ni * pr - nr * pi

        def scan(k, carry):
            i = steps - 1 - k
            gr, gi, acc_r, acc_i = carry
            return step(i, gr, gi, sr_ref[_rows(i - 1), :], si_ref[_rows(i - 1), :], acc_r, acc_i)

        gr, gi, acc_r, acc_i = lax.fori_loop(0, steps - 1, scan, (e0r, e0i, zero, zero), unroll=SCAN_UNROLL)
        row = lax.broadcasted_iota(jnp.int32, (N_SEG, cs), 0)
        last = _rows(steps - 1)
        pr = jnp.where(row == 0, 0.0, pltpu.roll(sr_ref[last, :], 1, 0))
        pi = jnp.where(row == 0, 0.0, pltpu.roll(si_ref[last, :], 1, 0))
        _, _, acc_r, acc_i = step(0, gr, gi, pr, pi, acc_r, acc_i)
        dar_ref[...] = jnp.sum(acc_r, axis=0, keepdims=True)
        dai_ref[...] = jnp.sum(acc_i, axis=0, keepdims=True)

        uv = u_ref[...]
        ub = uv.astype(BF16)
        grb = gr_ref[...].astype(BF16)
        gib = gi_ref[...].astype(BF16)
        du = d_ref[...] * dyv + _dot(grb, bbr_ref[...], NT) + _dot(gib, bbi_ref[...], NT)
        du_ref[...] = du.astype(BF16)
        def put_groups(ref, full):
            for gl in range(gpc):
                ref[gl] = full[gl * hb:(gl + 1) * hb, gl * pb:(gl + 1) * pb]

        put_groups(dbbr_ref, _dot(ub, grb, TN))
        put_groups(dbbi_ref, _dot(ub, gib, TN))
        put_groups(dccr_ref, _dot(dyb, sr_ref[...].astype(BF16), TN))
        put_groups(dcci_ref, -_dot(dyb, si_ref[...].astype(BF16), TN))
        dd_ref[...] = jnp.sum(dyv * uv, axis=0, keepdims=True)

    hb, pb = SSM_CHUNK // gpc, cs // gpc
    groups = pl.BlockSpec((None, gpc, hb, pb), lambda j: (j, 0, 0, 0))
    groups_shape = jax.ShapeDtypeStruct((nch, gpc, hb, pb), F32)
    chunk3 = lambda r, c: pl.BlockSpec((None, r, c), lambda j: (j, 0, 0))
    cols = lambda w: pl.BlockSpec((t, w), lambda j: (0, j))
    return pl.pallas_call(
        body, name="s5_bwd", grid=(nch,),
        in_specs=[cols(SSM_CHUNK), pl.BlockSpec((t, SSM_CHUNK), lambda j: (0, j + u_blk)), cols(cs), cols(cs),
                  chunk3(SSM_CHUNK, cs), chunk3(SSM_CHUNK, cs), chunk3(1, cs), chunk3(1, cs),
                  chunk3(cs, SSM_CHUNK), chunk3(cs, SSM_CHUNK), chunk3(1, SSM_CHUNK)],
        out_specs=[cols(SSM_CHUNK), groups, groups, chunk3(1, cs), chunk3(1, cs), groups, groups,
                   chunk3(1, SSM_CHUNK)],
        out_shape=[jax.ShapeDtypeStruct((t, ds), BF16), groups_shape, groups_shape,
                   jax.ShapeDtypeStruct((nch, 1, cs), F32), jax.ShapeDtypeStruct((nch, 1, cs), F32),
                   groups_shape, groups_shape, jax.ShapeDtypeStruct((nch, 1, SSM_CHUNK), F32)],
        scratch_shapes=[pltpu.VMEM((t, cs), F32), pltpu.VMEM((t, cs), F32)],
        compiler_params=_params(),
    )(dy, proj, s_re, s_im, bb_re, bb_im, a_re, a_im, cc_re, cc_im, d_skip)


def _discretize(lr, li, log_dt, br, bi):
    dt = jnp.exp(log_dt)
    mag = jnp.exp(lr * dt)
    ang = li * dt
    a_re = mag * jnp.cos(ang)
    a_im = mag * jnp.sin(ang)
    den = lr * lr + li * li
    nr = a_re - 1.0
    f_re = (nr * lr + a_im * li) / den
    f_im = (a_im * lr - nr * li) / den
    return a_re, a_im, f_re * br - f_im * bi, f_re * bi + f_im * br


def _whole(shape):
    return pl.BlockSpec(shape, lambda: (0,) * len(shape))


def _disc_fwd(lr, li, log_dt, br, bi, deps=()):
    def body(lr_ref, li_ref, dt_ref, br_ref, bi_ref, ar_ref, ai_ref, bbr_ref, bbi_ref):
        outs = _discretize(lr_ref[...], li_ref[...], dt_ref[...], br_ref[...], bi_ref[...])
        for ref, val in zip((ar_ref, ai_ref, bbr_ref, bbi_ref), outs):
            ref[...] = val

    args = (lr, li, log_dt, br, bi)
    outs = (lr, lr, br, br)
    return pl.pallas_call(
        _ignoring_deps(body, 5, deps), name="disc_fwd",
        in_specs=[_whole(a.shape) for a in args] + [ANY_SPEC] * len(deps),
        out_specs=[_whole(a.shape) for a in outs],
        out_shape=[jax.ShapeDtypeStruct(a.shape, F32) for a in outs],
    )(*args, *deps)


def _disc_bwd(lr, li, log_dt, br, bi, dar, dai, dbbr, dbbi):
    def body(lr_ref, li_ref, dt_ref, br_ref, bi_ref, dar_ref, dai_ref, dbbr_ref, dbbi_ref,
             dlr_ref, dli_ref, ddt_ref, dbr_ref, dbi_ref):
        _, vjp = jax.vjp(_discretize, lr_ref[...], li_ref[...], dt_ref[...], br_ref[...], bi_ref[...])
        grads = vjp((dar_ref[...], dai_ref[...], dbbr_ref[...], dbbi_ref[...]))
        for ref, val in zip((dlr_ref, dli_ref, ddt_ref, dbr_ref, dbi_ref), grads):
            ref[...] = val

    args = (lr, li, log_dt, br, bi, dar, dai, dbbr, dbbi)
    outs = (lr, li, log_dt, br, bi)
    return pl.pallas_call(
        body, name="disc_bwd", in_specs=[_whole(a.shape) for a in args],
        out_specs=[_whole(a.shape) for a in outs],
        out_shape=[jax.ShapeDtypeStruct(a.shape, F32) for a in outs],
    )(*args)


def _adamw(w, g, m, v):
    m = ADAM_B1 * m + (1.0 - ADAM_B1) * g
    v = ADAM_B2 * v + (1.0 - ADAM_B2) * (g * g)
    m_hat = m / ADAM_BC1
    v_hat = v / ADAM_BC2
    delta = -ADAM_LR * (m_hat / (jnp.sqrt(v_hat) + ADAM_EPS) + ADAM_WD * w)
    return delta, m, v


def _adamw_reduce(name, parts, w, m, v):
    _, r, c = parts.shape
    tr = r
    for cand in (256, 176, 128):
        if r % cand == 0:
            tr = cand
            break

    def body(p_ref, w_ref, m_ref, v_ref, g_ref, d_ref, nm_ref, nv_ref):
        g = p_ref[0].astype(F32)
        for k in range(1, N_DEV):
            g = g + p_ref[k].astype(F32)
        delta, nm, nv = _adamw(w_ref[...], g, m_ref[...], v_ref[...])
        g_ref[...] = g
        d_ref[...] = delta
        nm_ref[...] = nm
        nv_ref[...] = nv

    blk = pl.BlockSpec((tr, c), lambda i: (i, 0))
    return pl.pallas_call(
        body, name=name, grid=(r // tr,),
        in_specs=[pl.BlockSpec((N_DEV, tr, c), lambda i: (0, i, 0)), blk, blk, blk],
        out_specs=[blk] * 4, out_shape=[jax.ShapeDtypeStruct((r, c), F32)] * 4,
        compiler_params=_params(),
    )(parts, w, m, v)


def _sum_parts(name, parts):
    _, r, c = parts.shape

    def body(p_ref, o_ref):
        g = p_ref[0]
        for k in range(1, N_DEV):
            g = g + p_ref[k]
        o_ref[...] = g

    return pl.pallas_call(
        body, name=name, in_specs=[_whole(parts.shape)], out_specs=_whole((r, c)),
        out_shape=jax.ShapeDtypeStruct((r, c), F32), compiler_params=_params(),
    )(parts)


def _adamw_many(name, grads, ws, ms, vs):
    n = len(grads)

    def body(*refs):
        ins, outs = refs[:4 * n], refs[4 * n:]
        for i in range(n):
            g, w, m, v = (ins[j * n + i][...] for j in range(4))
            for ref, val in zip((outs[i], outs[n + i], outs[2 * n + i]), _adamw(w, g, m, v)):
                ref[...] = val

    args = (*grads, *ws, *ms, *vs)
    outs = pl.pallas_call(
        body, name=name, in_specs=[_whole(a.shape) for a in args],
        out_specs=[_whole(a.shape) for a in ws] * 3,
        out_shape=[jax.ShapeDtypeStruct(a.shape, F32) for a in ws] * 3, compiler_params=_params(),
    )(*args)
    return outs[:n], outs[n:2 * n], outs[2 * n:]


def _pack(arrays, rows):
    flat = jnp.concatenate([a.reshape(-1) for a in arrays])
    return jnp.pad(flat, (0, rows * LANES - flat.shape[0])).reshape(rows, LANES)


def _unpack(packed, shapes):
    flat = packed.reshape(-1)
    out, off = [], 0
    for s in shapes:
        n = math.prod(s)
        out.append(flat[off:off + n].reshape(s))
        off += n
    return out


def _packed_rows(shapes):
    n = sum(math.prod(s) for s in shapes)
    return -(-n // (SUBLANES * LANES)) * SUBLANES


def _block_diag(x):
    j, g, r, c = x.shape
    eye = jnp.eye(g, dtype=x.dtype)
    return (x[:, :, :, None, :] * eye[None, :, None, :, None]).reshape(j, g * r, g * c)


def kernel(x, meta_tokens, norm_mix_g, w_in, conv_w, ssm_lam_re, ssm_lam_im, ssm_log_dt, ssm_b_re, ssm_b_im, ssm_c_re, ssm_c_im, ssm_d, ssm_w_glu, gain_conv_out, gain_ssm_out, w_out, norm_ffn_g, w_up, ffn_conv_w, ffn_conv_b, w_down, norm_final_g, loss_target, m_meta_tokens, m_norm_mix_g, m_w_in, m_conv_w, m_ssm_lam_re, m_ssm_lam_im, m_ssm_log_dt, m_ssm_b_re, m_ssm_b_im, m_ssm_c_re, m_ssm_c_im, m_ssm_d, m_ssm_w_glu, m_gain_conv_out, m_gain_ssm_out, m_w_out, m_norm_ffn_g, m_w_up, m_ffn_conv_w, m_ffn_conv_b, m_w_down, m_norm_final_g, v_meta_tokens, v_norm_mix_g, v_w_in, v_conv_w, v_ssm_lam_re, v_ssm_lam_im, v_ssm_log_dt, v_ssm_b_re, v_ssm_b_im, v_ssm_c_re, v_ssm_c_im, v_ssm_d, v_ssm_w_glu, v_gain_conv_out, v_gain_ssm_out, v_w_out, v_norm_ffn_g, v_w_up, v_ffn_conv_w, v_ffn_conv_b, v_w_down, v_norm_final_g):
    weights = dict(meta_tokens=meta_tokens, norm_mix_g=norm_mix_g, w_in=w_in, conv_w=conv_w, ssm_lam_re=ssm_lam_re, ssm_lam_im=ssm_lam_im, ssm_log_dt=ssm_log_dt, ssm_b_re=ssm_b_re, ssm_b_im=ssm_b_im, ssm_c_re=ssm_c_re, ssm_c_im=ssm_c_im, ssm_d=ssm_d, ssm_w_glu=ssm_w_glu, gain_conv_out=gain_conv_out, gain_ssm_out=gain_ssm_out, w_out=w_out, norm_ffn_g=norm_ffn_g, w_up=w_up, ffn_conv_w=ffn_conv_w, ffn_conv_b=ffn_conv_b, w_down=w_down, norm_final_g=norm_final_g)
    mom_m = dict(meta_tokens=m_meta_tokens, norm_mix_g=m_norm_mix_g, w_in=m_w_in, conv_w=m_conv_w, ssm_lam_re=m_ssm_lam_re, ssm_lam_im=m_ssm_lam_im, ssm_log_dt=m_ssm_log_dt, ssm_b_re=m_ssm_b_re, ssm_b_im=m_ssm_b_im, ssm_c_re=m_ssm_c_re, ssm_c_im=m_ssm_c_im, ssm_d=m_ssm_d, ssm_w_glu=m_ssm_w_glu, gain_conv_out=m_gain_conv_out, gain_ssm_out=m_gain_ssm_out, w_out=m_w_out, norm_ffn_g=m_norm_ffn_g, w_up=m_w_up, ffn_conv_w=m_ffn_conv_w, ffn_conv_b=m_ffn_conv_b, w_down=m_w_down, norm_final_g=m_norm_final_g)
    mom_v = dict(meta_tokens=v_meta_tokens, norm_mix_g=v_norm_mix_g, w_in=v_w_in, conv_w=v_conv_w, ssm_lam_re=v_ssm_lam_re, ssm_lam_im=v_ssm_lam_im, ssm_log_dt=v_ssm_log_dt, ssm_b_re=v_ssm_b_re, ssm_b_im=v_ssm_b_im, ssm_c_re=v_ssm_c_re, ssm_c_im=v_ssm_c_im, ssm_d=v_ssm_d, ssm_w_glu=v_ssm_w_glu, gain_conv_out=v_gain_conv_out, gain_ssm_out=v_gain_ssm_out, w_out=v_w_out, norm_ffn_g=v_norm_ffn_g, w_up=v_w_up, ffn_conv_w=v_ffn_conv_w, ffn_conv_b=v_ffn_conv_b, w_down=v_w_down, norm_final_g=v_norm_final_g)
    names = list(weights)

    n_meta, d_meta = meta_tokens.shape
    seq, d = x.shape[1], x.shape[2]
    rows_used = n_meta + seq
    t = -(-rows_used // ROW_TILE) * ROW_TILE
    d_in_s = w_in.shape[2]
    dc_s = conv_w.shape[2]
    dc = dc_s * N_DEV
    ds = ssm_w_glu.shape[2]
    n_groups, n_state, grp = ssm_b_re.shape[1:]
    ns = n_groups * n_state
    nch = ds // SSM_CHUNK
    gpc = n_groups // nch
    ff_s = w_up.shape[2]
    dn_s = w_down.shape[1]
    assert 3 * dc + ds == d_in_s * N_DEV and 2 * dn_s == ff_s and t % (N_SEG * SUBLANES) == 0

    small_shard = jnp.concatenate([meta_tokens.reshape(-1), conv_w.reshape(-1), ffn_conv_w.reshape(-1)])
    n_small = small_shard.shape[0]
    small_rows = -(-n_small // LANES)
    small_shard = jnp.pad(small_shard, (0, small_rows * LANES - n_small)).reshape(small_rows, LANES)
    ag, ag_token = _gather2_start("gather_weights_start", [
        small_shard, w_in[0].astype(BF16), ssm_w_glu[0].astype(BF16), w_out[0].astype(BF16),
        jnp.swapaxes(w_up[0], 0, 1).astype(BF16), w_down[0].astype(BF16)])
    fb = ffn_conv_b

    gh = n_groups * grp
    per_h = lambda a: jnp.broadcast_to(a.reshape(n_groups, 1, -1), (n_groups, grp, n_state)).reshape(gh, n_state)
    ghp = lambda a: a.transpose(0, 1, 3, 2).reshape(gh, n_state)
    lr, li, log_dt_e = per_h(ssm_lam_re), per_h(ssm_lam_im), per_h(ssm_log_dt)
    br, bi = ghp(ssm_b_re), ghp(ssm_b_im)
    a_re, a_im, bb_re, bb_im = _disc_fwd(lr, li, log_dt_e, br, bi, deps=(ag_token,))
    cs = gpc * n_state
    chunk_row = lambda a: a.reshape(n_groups, grp, n_state)[:, 0].reshape(nch, 1, cs)
    to_bb = lambda a: _block_diag(a.reshape(nch, gpc, grp, n_state)).astype(BF16)
    to_cc = lambda a: _block_diag(a.reshape(nch, gpc, grp, n_state).transpose(0, 1, 3, 2)).astype(BF16)
    bbm_re, bbm_im = to_bb(bb_re), to_bb(bb_im)
    ccm_re, ccm_im = to_cc(ssm_c_re), to_cc(ssm_c_im)
    a_re_c, a_im_c = chunk_row(a_re), chunk_row(a_im)
    d_skip = ssm_d.reshape(nch, 1, SSM_CHUNK)

    target = _to_segments(jnp.pad(loss_target[0], ((n_meta, t - rows_used), (0, 0))))
    token = _gather2_forward("gather_weights_forward_first", ag, (0, 1), (bbm_im, ccm_im, target))
    (g_small,) = _gather2_wait("gather_weights_wait_small", ag, (0,), token)
    g_small = g_small.reshape(N_DEV, -1)
    o1 = n_meta * d_meta
    o2 = o1 + 3 * dc_s
    meta_full = g_small[:, :o1].reshape(N_DEV, n_meta, d_meta).transpose(1, 0, 2).reshape(n_meta, d)
    conv_w_f = g_small[:, o1:o2].reshape(N_DEV, 3, dc_s).transpose(1, 0, 2).reshape(3, dc)
    fw = g_small[:, o2:o2 + 3 * ff_s].reshape(N_DEV, 3, ff_s).transpose(1, 0, 2).reshape(3, N_DEV * ff_s)
    h0 = _to_segments(jnp.concatenate([meta_full, x[0], jnp.zeros((t - rows_used, d), F32)], axis=0))
    full_t = lambda w: pl.BlockSpec((t, w), lambda *_: (0, 0))

    hn1 = _rms_fwd("norm_mix", h0, norm_mix_g)
    (g_in,) = _gather2_wait("gather_weights_wait_in", ag, (1,), hn1)
    proj = _mm("proj", hn1, g_in, dims=NN, grid=(N_DEV,), a_spec=full_t(d),
               b_spec=pl.BlockSpec((None, d, d_in_s), lambda j: (j, 0, 0)),
               o_spec=pl.BlockSpec((t, d_in_s), lambda j: (0, j)),
               out_shape=jax.ShapeDtypeStruct((t, N_DEV * d_in_s), F32))
    s_re, s_im, y_ssm = _s5_fwd(proj, 3 * dc, bbm_re, bbm_im, a_re_c, a_im_c, ccm_re, ccm_im, d_skip)
    token = _gather2_forward("gather_weights_forward_up", ag, (2, 3, 4), (y_ssm,))
    g_glu, g_out = _gather2_wait("gather_weights_wait_mix", ag, (2, 3), token)
    w_out_f = g_out.reshape(-1, d)
    w_glu_f = g_glu.reshape(ds, ds)
    mixed, z_glu, h1, hn2 = _mix_fwd(proj, y_ssm, w_glu_f, conv_w_f, gain_conv_out, gain_ssm_out,
                                     h0, w_out_f, norm_ffn_g)
    tn_out = 256
    token = _gather2_forward("gather_weights_forward_down", ag, (5,), (hn2,))
    (g_up,) = _gather2_wait("gather_weights_wait_up", ag, (4,), token)
    f2 = N_DEV * ff_s
    pair = 2 * ff_s
    w_up_t = g_up.reshape(f2, d)
    up_pre = _mm("up_proj", hn2, w_up_t, dims=NT, grid=(f2 // pair,), a_spec=full_t(d),
                 b_spec=pl.BlockSpec((pair, d), lambda j: (j, 0)),
                 o_spec=pl.BlockSpec((t, pair), lambda j: (0, j)),
                 out_shape=jax.ShapeDtypeStruct((t, f2), BF16))
    act = _gated_fwd(up_pre, fw, fb)
    (g_down,) = _gather2_wait("gather_weights_wait_down", ag, (5,), act)
    w_down_f = g_down.reshape(f2 // 2, d)

    dh2, dh2_b, loss_part, d_norm_final = _down_loss_bwd(
        act, w_down_f, h1, target, norm_final_g.reshape(1, d), n_meta, rows_used)
    dw_down = _mm("down_wgrad", act, dh2_b, dims=TN, grid=(f2 // 2 // pair,),
                  a_spec=pl.BlockSpec((t, pair), lambda j: (0, j)), b_spec=full_t(d),
                  o_spec=pl.BlockSpec((pair, d), lambda j: (j, 0)),
                  out_shape=jax.ShapeDtypeStruct((f2 // 2, d), BF16))
    ex_down, token = _send_start("exchange_down_start", [dw_down.reshape(N_DEV, dn_s, d)], gather=False)
    dact = _mm("down_dgrad", dh2_b, w_down_f, dims=NT, grid=(f2 // 2 // pair,), a_spec=full_t(d),
               b_spec=pl.BlockSpec((pair, d), lambda j: (j, 0)),
               o_spec=pl.BlockSpec((t, pair), lambda j: (0, j)),
               out_shape=jax.ShapeDtypeStruct((t, f2 // 2), BF16))
    dup_pre, d_fw, d_fb = _gated_bwd(up_pre, dact, fw, fb, deps=(token,))
    dw_up = _mm("up_wgrad", dup_pre, hn2, dims=TN, grid=(f2 // pair,),
                a_spec=pl.BlockSpec((t, pair), lambda j: (0, j)), b_spec=full_t(d),
                o_spec=pl.BlockSpec((pair, d), lambda j: (j, 0)),
                out_shape=jax.ShapeDtypeStruct((f2, d), BF16))
    ex_up, token = _send_start("exchange_up_start", [dw_up.reshape(N_DEV, ff_s, d)], gather=False)
    dh1, dh1_b, d_norm_ffn = _dgrad_rms_bwd("up_dgrad_norm_bwd", dh2, dup_pre, w_up_t, NN, h1, norm_ffn_g,
                                            deps=(token,))
    dmixed = _mm("out_dgrad", dh1_b, w_out_f, dims=NT, grid=((dc + ds) // tn_out,), a_spec=full_t(d),
                 b_spec=pl.BlockSpec((tn_out, d), lambda i: (i, 0)),
                 o_spec=pl.BlockSpec((t, tn_out), lambda i: (0, i)),
                 out_shape=jax.ShapeDtypeStruct((t, dc + ds), F32))
    dw_out = _mm("out_wgrad", mixed, dh1_b, dims=TN, grid=((dc + ds) // tn_out,),
                 a_spec=pl.BlockSpec((t, tn_out), lambda i: (0, i)), b_spec=full_t(d),
                 o_spec=pl.BlockSpec((tn_out, d), lambda i: (i, 0)),
                 out_shape=jax.ShapeDtypeStruct((dc + ds, d), BF16))
    db_gate, dconv, dy_ssm, d_wglu, d_conv_w, d_gain_c, d_gain_s = _mix_bwd1(
        proj, y_ssm, z_glu, dmixed, w_glu_f, conv_w_f, gain_conv_out, gain_ssm_out)
    (du, d_bbm_re, d_bbm_im, d_a_re, d_a_im, d_ccm_re, d_ccm_im, d_dskip) = _s5_bwd(
        dy_ssm, proj, 3 * dc, s_re, s_im, bbm_re, bbm_im, a_re_c, a_im_c, ccm_re, ccm_im, d_skip, gpc)

    from_bb = from_cc = lambda a: a.reshape(gh, n_state)
    first_h = lambda a: jnp.pad(a.reshape(n_groups, 1, n_state), ((0, 0), (0, grp - 1), (0, 0))).reshape(gh, n_state)
    over_h = lambda a: a.reshape(n_groups, grp, n_state).sum(axis=1)
    d_lr, d_li, d_dt_e, d_br, d_bi = _disc_bwd(
        lr, li, log_dt_e, br, bi, first_h(d_a_re), first_h(d_a_im), from_bb(d_bbm_re), from_bb(d_bbm_im))

    rep2d = dict(
        ssm_lam_re=(n_groups, n_state), ssm_lam_im=(n_groups, n_state), ssm_log_dt=(1, n_groups),
        ssm_b_re=(gh, n_state), ssm_b_im=(gh, n_state), ssm_c_re=(gh, n_state),
        ssm_c_im=(gh, n_state), ssm_d=(n_groups, grp), gain_conv_out=(1, dc),
        gain_ssm_out=(1, ds), norm_ffn_g=(1, d), ffn_conv_b=(1, N_DEV * ff_s), norm_final_g=(1, d))
    rep_names = list(rep2d)
    rep_grads = dict(
        ssm_lam_re=over_h(d_lr), ssm_lam_im=over_h(d_li), ssm_log_dt=over_h(d_dt_e).sum(axis=1),
        ssm_b_re=d_br, ssm_b_im=d_bi, ssm_c_re=from_cc(d_ccm_re), ssm_c_im=from_cc(d_ccm_im),
        ssm_d=d_dskip, gain_conv_out=d_gain_c, gain_ssm_out=d_gain_s, norm_ffn_g=d_norm_ffn,
        ffn_conv_b=d_fb, norm_final_g=d_norm_final)
    rep_shapes = [rep2d[n] for n in rep_names] + [(1, 1)]
    rep_rows = _packed_rows(rep_shapes)
    rep_pack = _pack([rep_grads[n] for n in rep_names] + [loss_part], rep_rows)
    ex_mix, token = _send_start("exchange_mix_start", [
        rep_pack, dw_out.reshape(N_DEV, -1, d), d_wglu.astype(BF16).reshape(N_DEV, -1, ds),
        d_conv_w.reshape(3, N_DEV, dc_s).transpose(1, 0, 2),
        d_fw.reshape(3, N_DEV, ff_s).transpose(1, 0, 2)],
        gather=[True, False, False, False, False])
    dcdv = _mix_bwd2(proj, dconv, conv_w_f, deps=(token,))
    dproj = jnp.concatenate([db_gate, dcdv, du], axis=1)
    d_in = N_DEV * d_in_s
    w_in_rows = g_in.transpose(1, 0, 2).reshape(d, d_in)
    dw_in = _mm("proj_wgrad", hn1, dproj, dims=TN, grid=(N_DEV,), a_spec=full_t(d),
                b_spec=pl.BlockSpec((t, d_in_s), lambda j: (0, j)),
                o_spec=pl.BlockSpec((None, d, d_in_s), lambda j: (j, 0, 0)),
                out_shape=jax.ShapeDtypeStruct((N_DEV, d, d_in_s), BF16))
    dh0, _, d_norm_mix = _dgrad_rms_bwd("proj_dgrad_norm_bwd", dh1, dproj, w_in_rows, NT, h0, norm_mix_g)
    dh0 = _from_segments(dh0)
    grad_x = dh0[n_meta:rows_used][None]
    d_meta_b = dh0[:n_meta].reshape(n_meta, N_DEV, d_meta).transpose(1, 0, 2)
    ex_in, ex_in_token = _send_start("exchange_in_start", [dw_in, d_meta_b, d_norm_mix],
                                     gather=[False, False, True])

    shard_out = {}

    def update(n, parts, transposed=False):
        sh = weights[n].shape
        two_d = lambda a: (jnp.swapaxes(a[0], 0, 1) if transposed else a.reshape(parts.shape[1:]))
        res = _adamw_reduce("adamw_" + n, parts, two_d(weights[n]), two_d(mom_m[n]), two_d(mom_v[n]))
        shard_out[n] = [(jnp.swapaxes(r, 0, 1) if transposed else r).reshape(sh) for r in res]
        return res[0]

    (p_down,) = _send_wait("exchange_down_wait", ex_down, (0,), ex_in_token)
    done = update("w_down", p_down)
    (p_up,) = _send_wait("exchange_up_wait", ex_up, (0,), done)
    done = update("w_up", p_up, transposed=True)
    (rep_parts,) = _send_wait("gather_small_grads_wait", ex_mix, (0,), done)
    rep_sum = _sum_parts("sum_small_grads", rep_parts)
    *rep_g, loss = _unpack(rep_sum, rep_shapes)
    loss = loss.reshape(())
    swapped = ("ssm_b_re", "ssm_b_im")
    to2d = lambda n, a: ghp(a) if n in swapped else a.reshape(rep2d[n])
    from2d = lambda n, a: (a.reshape(1, n_groups, grp, n_state).transpose(0, 1, 3, 2) if n in swapped
                           else a.reshape(weights[n].shape))
    as2d = lambda tree: [to2d(n, tree[n]) for n in rep_names]
    rep_res = _adamw_many("adamw_replicated", rep_g, as2d(weights), as2d(mom_m), as2d(mom_v))
    for i, n in enumerate(rep_names):
        shard_out[n] = [from2d(n, r) for r in (rep_g[i], *(res[i] for res in rep_res))]

    p_out, p_glu, p_cw, p_fw = _send_wait("exchange_mix_wait", ex_mix, (1, 2, 3, 4), rep_sum)
    update("w_out", p_out)
    update("ssm_w_glu", p_glu)
    update("conv_w", p_cw)
    done = update("ffn_conv_w", p_fw)
    p_in, p_meta, p_nm = _send_wait("exchange_in_wait", ex_in, (0, 1, 2), done)
    update("w_in", p_in)
    update("meta_tokens", p_meta)
    update("norm_mix_g", p_nm)

    grads = [shard_out[n][0] for n in names]
    deltas = [shard_out[n][1] for n in names]
    new_m = [shard_out[n][2] for n in names]
    new_v = [shard_out[n][3] for n in names]
    return (loss, grad_x, *grads, *deltas, *new_m, *new_v)
```

```python
import math

import jax
import jax.numpy as jnp
from jax import lax
from jax.experimental import pallas as pl
from jax.experimental.pallas import tpu as pltpu

F32 = jnp.float32
BF16 = jnp.bfloat16
MESH = pl.DeviceIdType.MESH

N_DEV = 8
RMS_EPS = 1e-6
ADAM_LR = 0.001
ADAM_B1 = 0.9
ADAM_B2 = 0.999
ADAM_EPS = 1e-08
ADAM_WD = 0.01
ADAM_STEP = 10
ADAM_BC1 = 1.0 - ADAM_B1 ** ADAM_STEP
ADAM_BC2 = 1.0 - ADAM_B2 ** ADAM_STEP

SUBLANES = 8
LANES = 128
ROW_TILE = 128
ROW_CHUNK = 32
WIDE_TILES = (544, 256, 128)
FFN_COLS = 256
STAGE_COLS = 256
N_SEG = 8
HALO_ROWS = 16
SSM_CHUNK = 128
SCAN_UNROLL = 2
VMEM_LIMIT = 48 * 1024 * 1024

NN = ((1,), (0,))
NT = ((1,), (1,))
TN = ((0,), (0,))


def _params(**kw):
    return pltpu.CompilerParams(vmem_limit_bytes=VMEM_LIMIT, **kw)


def _dot(a, b, dims):
    return lax.dot_general(a, b, (dims, ((), ())), preferred_element_type=F32)


def _mean_sq_rsqrt(x):
    return lax.rsqrt(jnp.mean(x * x, axis=-1, keepdims=True) + RMS_EPS)


def _rms_bwd(x, r, g, dy):
    xhat = x * r
    dxh = dy * g
    dx = r * (dxh - xhat * jnp.mean(dxh * xhat, axis=-1, keepdims=True))
    return dx, dy * xhat


def _gelu(y):
    c = math.sqrt(2.0 / math.pi)
    t = jnp.tanh(c * (y + 0.044715 * y * y * y))
    return 0.5 * y * (1.0 + t), t


def _gelu_grad(y, t):
    c = math.sqrt(2.0 / math.pi)
    return 0.5 * (1.0 + t) + 0.5 * y * (1.0 - t * t) * c * (1.0 + 3.0 * 0.044715 * y * y)


def _wrap_prev_halo(halo, first):
    seg = lax.broadcasted_iota(jnp.int32, halo.shape, 0) % N_SEG
    wrapped = jnp.where(seg == 0, 0.0, pltpu.roll(halo, 1, 0))
    return jnp.where(first, wrapped, halo)


def _wrap_next_halo(halo, last):
    seg = lax.broadcasted_iota(jnp.int32, halo.shape, 0) % N_SEG
    wrapped = jnp.where(seg == N_SEG - 1, 0.0, pltpu.roll(halo, halo.shape[0] - 1, 0))
    return jnp.where(last, wrapped, halo)


def _dev_index(p):
    return 4 * p[0] + 2 * p[1] + p[2]


def _allgather(name, shards, deps=()):
    n = len(shards)

    def body(*refs):
        ins, outs = refs[:n], refs[n:2 * n]
        send_sems, recv_sems, local_sems = refs[2 * n:]
        x, y, c = lax.axis_index("x"), lax.axis_index("y"), lax.axis_index("c")
        me, sibling = (x, y, c), (x, y, 1 - c)
        chips = [(1 - x, y), (x, 1 - y), (1 - x, 1 - y)]

        def copy(a, k, block, to, src=None):
            dst = outs[a].at[_dev_index(block)]
            return pltpu.make_async_remote_copy(
                src_ref=dst if src is None else src, dst_ref=dst,
                send_sem=send_sems.at[a, k], recv_sem=recv_sems.at[a, k],
                device_id=to, device_id_type=MESH)

        mine = [pltpu.make_async_copy(ins[a], outs[a].at[_dev_index(me)], local_sems.at[a])
                for a in range(n)]
        for cp in mine:
            cp.start()
        first = []
        for a in range(n):
            first.append(copy(a, 0, me, sibling, src=ins[a]))
            for j, chip in enumerate(chips):
                first.append(copy(a, 1 + j, me, (*chip, c), src=ins[a]))
        for cp in first:
            cp.start()
        passed = []
        for j, chip in enumerate(chips):
            for a in range(n):
                copy(a, 1 + j, (*chip, c), me).wait_recv()
                fwd = copy(a, 4 + j, (*chip, c), sibling)
                fwd.start()
                passed.append(fwd)
        for a in range(n):
            copy(a, 0, sibling, me).wait_recv()
            for j, chip in enumerate(chips):
                copy(a, 4 + j, (*chip, 1 - c), me).wait_recv()
        for cp in first + passed:
            cp.wait_send()
        for cp in mine:
            cp.wait()

    any_spec = pl.BlockSpec(memory_space=pl.ANY)
    return pl.pallas_call(
        _ignoring_deps(body, n, deps), name=name,
        out_shape=[jax.ShapeDtypeStruct((N_DEV,) + s.shape, s.dtype) for s in shards],
        in_specs=[any_spec] * (n + len(deps)), out_specs=[any_spec] * n,
        scratch_shapes=[pltpu.SemaphoreType.DMA((n, 7)), pltpu.SemaphoreType.DMA((n, 7)),
                        pltpu.SemaphoreType.DMA((n,))],
    )(*shards, *deps)


HBM_SPEC = pl.BlockSpec(memory_space=pltpu.HBM)
SEM_SPEC = pl.BlockSpec(memory_space=pltpu.SEMAPHORE)
ANY_SPEC = pl.BlockSpec(memory_space=pl.ANY)
DATAFLOW = pltpu.SideEffectType.DATAFLOW_SIDE_EFFECTING


def _ignoring_deps(body, n_in, deps):
    n_dep = len(deps)

    def wrapped(*refs):
        return body(*refs[:n_in], *refs[n_in + n_dep:])

    return wrapped


def _my_position():
    x, y, c = lax.axis_index("x"), lax.axis_index("y"), lax.axis_index("c")
    return (x, y, c)


def _peer(me, k):
    return tuple((1 - v) if (k >> s) & 1 else v for v, s in zip(me, (2, 1, 0)))


def _split_copies(src_refs, land_refs, send_sems, recv_sems, gather):
    me = _my_position()
    copies = []
    for a, (src, land) in enumerate(zip(src_refs, land_refs)):
        for k in range(1, N_DEV):
            peer = _peer(me, k)
            copies.append(pltpu.make_async_remote_copy(
                src_ref=src if gather[a] else src.at[_dev_index(peer)], dst_ref=land.at[_dev_index(me)],
                send_sem=send_sems[a].at[k - 1], recv_sem=recv_sems[a].at[k - 1],
                device_id=peer, device_id_type=MESH))
    return copies


def _own_slot(block, like_shape):
    me = _dev_index(_my_position())
    return lax.dynamic_update_index_in_dim(lax.empty(like_shape, block.dtype), block, me, 0)


def _send_start(name, srcs, gather):
    n = len(srcs)
    me = _dev_index(_my_position())
    gather = [gather] * n if isinstance(gather, bool) else list(gather)
    lands = [_own_slot(s, (N_DEV,) + s.shape) if g else
             _own_slot(lax.dynamic_index_in_dim(s, me, 0, keepdims=False), s.shape)
             for s, g in zip(srcs, gather)]

    def body(*refs):
        src_refs, land_refs = refs[:n], refs[n:2 * n]
        send_sems, recv_sems = refs[2 * n:3 * n], refs[3 * n:4 * n]
        token = refs[-1]
        for cp in _split_copies(src_refs, land_refs, send_sems, recv_sems, gather):
            cp.start()
        token[...] = jnp.zeros_like(token)

    hbm = lambda a: pltpu.HBM(a.shape, a.dtype)
    sems = [pltpu.SemaphoreType.DMA((N_DEV - 1,))] * n
    outs = pl.pallas_call(
        body, name=name,
        out_shape=(*sems, *sems, *[hbm(s) for s in srcs], *[hbm(l) for l in lands],
                   jax.ShapeDtypeStruct((SUBLANES, LANES), F32)),
        in_specs=[HBM_SPEC] * (2 * n),
        out_specs=(*[SEM_SPEC] * (2 * n), *[HBM_SPEC] * (2 * n), pl.BlockSpec(memory_space=pltpu.VMEM)),
        input_output_aliases={i: 2 * n + i for i in range(2 * n)},
        compiler_params=pltpu.CompilerParams(has_side_effects=DATAFLOW),
    )(*[pltpu.with_memory_space_constraint(a, pltpu.HBM) for a in (*srcs, *lands)])
    state = dict(send=outs[:n], recv=outs[n:2 * n], srcs=outs[2 * n:3 * n], lands=outs[3 * n:4 * n],
                 gather=gather)
    return state, outs[-1]


def _send_wait(name, state, which, after):
    n = len(which)
    pick = lambda key: [state[key][i] for i in which]
    gather = pick("gather")

    def body(*refs):
        src_refs, land_refs = refs[:n], refs[n:2 * n]
        send_sems, recv_sems = refs[2 * n:3 * n], refs[3 * n:4 * n]
        for cp in _split_copies(src_refs, land_refs, send_sems, recv_sems, gather):
            cp.wait_send()
            cp.wait_recv()

    srcs, lands = pick("srcs"), pick("lands")
    hbm = lambda a: pltpu.HBM(a.shape, a.dtype)
    outs = pl.pallas_call(
        body, name=name,
        out_shape=(*[hbm(s) for s in srcs], *[hbm(l) for l in lands]),
        in_specs=[*[HBM_SPEC] * (2 * n), *[SEM_SPEC] * (2 * n), ANY_SPEC],
        out_specs=tuple([HBM_SPEC] * (2 * n)),
        input_output_aliases={i: i for i in range(2 * n)},
        compiler_params=pltpu.CompilerParams(has_side_effects=DATAFLOW),
    )(*srcs, *lands, *pick("send"), *pick("recv"), after)
    return outs[n:]


def _two_level_copies(src_refs, land_refs, sems1, sems2):
    x, y, c = _my_position()
    me, sibling = (x, y, c), (x, y, 1 - c)
    chips = [(1 - x, y), (x, 1 - y), (1 - x, 1 - y)]
    stage1, stage2 = [], []
    for a, land in enumerate(land_refs):
        def copy(src, block, to, sems, k):
            return pltpu.make_async_remote_copy(
                src_ref=src, dst_ref=land.at[_dev_index(block)], send_sem=sems[0][a].at[k],
                recv_sem=sems[1][a].at[k], device_id=to, device_id_type=MESH)
        src = src_refs[a] if src_refs is not None else land.at[_dev_index(me)]
        stage1.append([copy(src, me, sibling, sems1, 0)] +
                      [copy(src, me, (*chip, c), sems1, 1 + j) for j, chip in enumerate(chips)])
        if sems2 is not None:
            stage2.append([copy(land.at[_dev_index((*chip, c))], (*chip, c), sibling, sems2, j)
                           for j, chip in enumerate(chips)])
    return stage1, stage2


def _gather2_start(name, shards):
    n = len(shards)
    lands = [_own_slot(s, (N_DEV,) + s.shape) for s in shards]

    def body(*refs):
        src_refs, land_refs = refs[:n], refs[n:2 * n]
        sems1 = (refs[2 * n:3 * n], refs[3 * n:4 * n])
        stage1, _ = _two_level_copies(src_refs, land_refs, sems1, None)
        for copies in stage1:
            for cp in copies:
                cp.start()
        refs[-1][...] = jnp.zeros_like(refs[-1])

    hbm = lambda a: pltpu.HBM(a.shape, a.dtype)
    sems = [pltpu.SemaphoreType.DMA((4,))] * n
    outs = pl.pallas_call(
        body, name=name,
        out_shape=(*sems, *sems, *[hbm(s) for s in shards], *[hbm(l) for l in lands],
                   jax.ShapeDtypeStruct((SUBLANES, LANES), F32)),
        in_specs=[HBM_SPEC] * (2 * n),
        out_specs=(*[SEM_SPEC] * (2 * n), *[HBM_SPEC] * (2 * n), pl.BlockSpec(memory_space=pltpu.VMEM)),
        input_output_aliases={i: 2 * n + i for i in range(2 * n)},
        compiler_params=pltpu.CompilerParams(has_side_effects=DATAFLOW),
    )(*[pltpu.with_memory_space_constraint(a, pltpu.HBM) for a in (*shards, *lands)])
    state = dict(send1=list(outs[:n]), recv1=list(outs[n:2 * n]), srcs=list(outs[2 * n:3 * n]),
                 lands=list(outs[3 * n:4 * n]), send2={}, recv2={})
    return state, outs[-1]


def _gather2_forward(name, state, which, after):
    n = len(which)
    pick = lambda key: [state[key][i] for i in which]

    def body(*refs):
        land_refs, recv1 = refs[:n], refs[n:2 * n]
        outs = refs[2 * n + len(after):]
        sems2 = (outs[:n], outs[n:2 * n])
        stage1, stage2 = _two_level_copies(None, land_refs, (recv1, recv1), sems2)
        for a in range(n):
            for j in range(3):
                stage1[a][1 + j].wait_recv()
                stage2[a][j].start()
        outs[-1][...] = jnp.zeros_like(outs[-1])

    lands = pick("lands")
    sems = [pltpu.SemaphoreType.DMA((3,))] * n
    outs = pl.pallas_call(
        body, name=name,
        out_shape=(*sems, *sems, *[pltpu.HBM(l.shape, l.dtype) for l in lands],
                   jax.ShapeDtypeStruct((SUBLANES, LANES), F32)),
        in_specs=[*[HBM_SPEC] * n, *[SEM_SPEC] * n, *[ANY_SPEC] * len(after)],
        out_specs=(*[SEM_SPEC] * (2 * n), *[HBM_SPEC] * n, pl.BlockSpec(memory_space=pltpu.VMEM)),
        input_output_aliases={i: 2 * n + i for i in range(n)},
        compiler_params=pltpu.CompilerParams(has_side_effects=DATAFLOW),
    )(*lands, *pick("recv1"), *after)
    for idx, i in enumerate(which):
        state["send2"][i], state["recv2"][i] = outs[idx], outs[n + idx]
        state["lands"][i] = outs[2 * n + idx]
    return outs[-1]


def _gather2_wait(name, state, which, after):
    n = len(which)
    pick = lambda key: [state[key][i] for i in which]

    def body(*refs):
        src_refs, land_refs = refs[:n], refs[n:2 * n]
        sems1 = (refs[2 * n:3 * n], refs[3 * n:4 * n])
        sems2 = (refs[4 * n:5 * n], refs[5 * n:6 * n])
        stage1, stage2 = _two_level_copies(src_refs, land_refs, sems1, sems2)
        for a in range(n):
            for cp in stage1[a]:
                cp.wait_send()
            stage1[a][0].wait_recv()
            for cp in stage2[a]:
                cp.wait_send()
                cp.wait_recv()

    srcs, lands = pick("srcs"), pick("lands")
    hbm = lambda a: pltpu.HBM(a.shape, a.dtype)
    outs = pl.pallas_call(
        body, name=name,
        out_shape=(*[hbm(s) for s in srcs], *[hbm(l) for l in lands]),
        in_specs=[*[HBM_SPEC] * (2 * n), *[SEM_SPEC] * (4 * n), ANY_SPEC],
        out_specs=tuple([HBM_SPEC] * (2 * n)),
        input_output_aliases={i: i for i in range(2 * n)},
        compiler_params=pltpu.CompilerParams(has_side_effects=DATAFLOW),
    )(*srcs, *lands, *pick("send1"), *pick("recv1"), *pick("send2"), *pick("recv2"), after)
    return outs[n:]


def _mm(name, a, b, *, dims, grid, a_spec, b_spec, o_spec, out_shape, acc_shape=None,
        res=None, res_spec=None):
    n_red = grid[-1] if acc_shape is not None else 1
    red_axis = len(grid) - 1

    def body(*refs):
        a_ref, b_ref = refs[0], refs[1]
        r_ref = refs[2] if res is not None else None
        o_ref = refs[3] if res is not None else refs[2]
        part = _dot(a_ref[...], b_ref[...], dims)
        if acc_shape is None:
            if r_ref is not None:
                part = part + r_ref[...]
            o_ref[...] = part.astype(o_ref.dtype)
            return
        acc_ref = refs[-1]
        k = pl.program_id(red_axis)

        @pl.when(k == 0)
        def _():
            acc_ref[...] = part

        @pl.when(k > 0)
        def _():
            acc_ref[...] += part

        @pl.when(k == n_red - 1)
        def _():
            total = acc_ref[...]
            if r_ref is not None:
                total = total + r_ref[...]
            o_ref[...] = total.astype(o_ref.dtype)

    ins, in_specs = [a, b], [a_spec, b_spec]
    if res is not None:
        ins.append(res)
        in_specs.append(res_spec)
    return pl.pallas_call(
        body, name=name, grid=grid, in_specs=in_specs, out_specs=o_spec, out_shape=out_shape,
        scratch_shapes=[pltpu.VMEM(acc_shape, F32)] if acc_shape is not None else [],
        compiler_params=_params(),
    )(*ins)


def _wide_tile(t):
    return next(c for c in WIDE_TILES if t % c == 0)


def _rms_fwd(name, h, g):
    t, d = h.shape
    tile = _wide_tile(t)

    def body(h_ref, g_ref, o_ref):
        x = h_ref[...]
        o_ref[...] = (x * _mean_sq_rsqrt(x) * g_ref[...]).astype(BF16)

    return pl.pallas_call(
        body, name=name, grid=(t // tile,),
        in_specs=[pl.BlockSpec((tile, d), lambda i: (i, 0)), pl.BlockSpec((1, d), lambda i: (0, 0))],
        out_specs=pl.BlockSpec((tile, d), lambda i: (i, 0)),
        out_shape=jax.ShapeDtypeStruct((t, d), BF16), compiler_params=_params(),
    )(h, g)


def _fused_tile(t):
    half = _wide_tile(t) // 2
    return half if half % 16 == 0 and t % half == 0 else _wide_tile(t)


def _dgrad_rms_bwd(name, dres, dy, w, dims, h, g, deps=()):
    t, d = h.shape
    k = dy.shape[1]

    def body(dres_ref, dy_ref, w_ref, h_ref, g_ref, dh_ref, dhb_ref, dg_ref):
        x = h_ref[...]
        dhn = _dot(dy_ref[...], w_ref[...], dims)
        dx, dgt = _rms_bwd(x, _mean_sq_rsqrt(x), g_ref[...], dhn)
        dh = dres_ref[...] + dx
        dh_ref[...] = dh
        dhb_ref[...] = dh.astype(BF16)

        @pl.when(pl.program_id(0) == 0)
        def _():
            dg_ref[...] = jnp.zeros_like(dg_ref)

        dg_ref[...] += jnp.sum(dgt, axis=0, keepdims=True)

    tile = _fused_tile(t)
    row = pl.BlockSpec((tile, d), lambda i: (i, 0))
    vec = pl.BlockSpec((1, d), lambda i: (0, 0))
    return pl.pallas_call(
        _ignoring_deps(body, 5, deps), name=name, grid=(t // tile,),
        in_specs=[row, pl.BlockSpec((tile, k), lambda i: (i, 0)), pl.BlockSpec(w.shape, lambda i: (0, 0)), row,
                  vec] + [ANY_SPEC] * len(deps),
        out_specs=[row, row, vec],
        out_shape=[jax.ShapeDtypeStruct((t, d), F32), jax.ShapeDtypeStruct((t, d), BF16),
                   jax.ShapeDtypeStruct((1, d), F32)],
        compiler_params=_params(),
    )(dres, dy, w, h, g, *deps)


def _down_loss_bwd(act, w_down, h1, target, g, row_lo, row_hi):
    t, d = h1.shape
    f = act.shape[1]
    tile = _fused_tile(t)

    def body(act_ref, w_ref, h_ref, tg_ref, g_ref, dh_ref, dhb_ref, loss_ref, dg_ref):
        i = pl.program_id(0)
        x = h_ref[...] + _dot(act_ref[...], w_ref[...], NN)
        r = _mean_sq_rsqrt(x)
        gv = g_ref[...]
        y = x * r * gv
        rows = i * tile + lax.broadcasted_iota(jnp.int32, (tile, 1), 0)
        time = (rows % N_SEG) * (t // N_SEG) + rows // N_SEG
        valid = jnp.logical_and(time >= row_lo, time < row_hi)
        err = jnp.where(valid, y - tg_ref[...], 0.0)
        dy = err * (1.0 / d)
        dx, dgt = _rms_bwd(x, r, gv, dy)
        dh_ref[...] = dx
        dhb_ref[...] = dx.astype(BF16)

        @pl.when(i == 0)
        def _():
            loss_ref[...] = jnp.zeros_like(loss_ref)
            dg_ref[...] = jnp.zeros_like(dg_ref)

        row_loss = jnp.mean(err * err, axis=-1, keepdims=True)
        loss_ref[...] += 0.5 * jnp.sum(row_loss, axis=0, keepdims=True)
        dg_ref[...] += jnp.sum(dgt, axis=0, keepdims=True)

    row = pl.BlockSpec((tile, d), lambda i: (i, 0))
    vec = pl.BlockSpec((1, d), lambda i: (0, 0))
    return pl.pallas_call(
        body, name="down_proj_loss_bwd", grid=(t // tile,),
        in_specs=[pl.BlockSpec((tile, f), lambda i: (i, 0)), pl.BlockSpec(w_down.shape, lambda i: (0, 0)),
                  row, row, vec],
        out_specs=[row, row, pl.BlockSpec((1, 1), lambda i: (0, 0)), vec],
        out_shape=[jax.ShapeDtypeStruct((t, d), F32), jax.ShapeDtypeStruct((t, d), BF16),
                   jax.ShapeDtypeStruct((1, 1), F32), jax.ShapeDtypeStruct((1, d), F32)],
        compiler_params=_params(),
    )(act, w_down, h1, target, g)


def _prev_halo(i, t):
    return jnp.where(i == 0, t // HALO_ROWS - 1, i * (ROW_TILE // HALO_ROWS) - 1)


def _next_halo(i, t):
    return jnp.where(i == t // ROW_TILE - 1, 0, (i + 1) * (ROW_TILE // HALO_ROWS))


def _causal_taps(cur, halo, first):
    rows = cur.shape[0]
    ext = jnp.concatenate([_wrap_prev_halo(halo, first), cur], axis=0)
    return ext[:rows], ext[N_SEG:N_SEG + rows]


def _anticausal_taps(cur, halo, last):
    rows = cur.shape[0]
    ext = jnp.concatenate([cur, _wrap_next_halo(halo, last)], axis=0)
    return ext[N_SEG:N_SEG + rows], ext[2 * N_SEG:2 * N_SEG + rows]


def _mix_fwd(proj, y, w_glu, conv_w, gain_c, gain_s, h0, w_out, gain_ffn):
    t, d = h0.shape
    dc = conv_w.shape[1]
    ds = y.shape[1]

    def body(p_ref, halo_ref, y_ref, wg_ref, cw_ref, gc_ref, gs_ref, h0_ref, wo_ref, gf_ref,
             mixed_ref, z_ref, h1_ref, hn_ref):
        i = pl.program_id(0)
        p = p_ref[...]
        b, c, v = p[:, :dc], p[:, dc:2 * dc], p[:, 2 * dc:3 * dc]
        cv = c * v
        hp = halo_ref[...]
        x2, x1 = _causal_taps(cv, hp[:, dc:2 * dc] * hp[:, 2 * dc:3 * dc], i == 0)
        cw = cw_ref[...]
        conv = cw[0:1] * x2 + cw[1:2] * x1 + cw[2:3] * cv
        co = b * conv
        mixed_ref[:, :dc] = (co * _mean_sq_rsqrt(co) * gc_ref[...]).astype(BF16)
        g, _ = _gelu(y_ref[...])
        z = _dot(g.astype(BF16), wg_ref[...], NN)
        z_ref[...] = z
        so = g * jax.nn.sigmoid(z)
        mixed_ref[:, dc:] = (so * _mean_sq_rsqrt(so) * gs_ref[...]).astype(BF16)
        h1 = h0_ref[...] + _dot(mixed_ref[...], wo_ref[...], NN)
        h1_ref[...] = h1
        hn_ref[...] = (h1 * _mean_sq_rsqrt(h1) * gf_ref[...]).astype(BF16)

    const = lambda i: (0, 0)
    row = lambda w: pl.BlockSpec((ROW_TILE, w), lambda i: (i, 0))
    return pl.pallas_call(
        body, name="mix_fwd_out_proj", grid=(t // ROW_TILE,),
        in_specs=[row(3 * dc), pl.BlockSpec((HALO_ROWS, 3 * dc), lambda i: (_prev_halo(i, t), 0)), row(ds),
                  pl.BlockSpec((ds, ds), const), pl.BlockSpec(conv_w.shape, const),
                  pl.BlockSpec((1, dc), const), pl.BlockSpec((1, ds), const),
                  row(d), pl.BlockSpec(w_out.shape, const), pl.BlockSpec((1, d), const)],
        out_specs=[row(dc + ds), row(ds), row(d), row(d)],
        out_shape=[jax.ShapeDtypeStruct((t, dc + ds), BF16), jax.ShapeDtypeStruct((t, ds), F32),
                   jax.ShapeDtypeStruct((t, d), F32), jax.ShapeDtypeStruct((t, d), BF16)],
        compiler_params=_params(),
    )(proj, proj, y, w_glu, conv_w, gain_c, gain_s, h0, w_out, gain_ffn)


def _mix_bwd1(proj, y, z, dmixed, w_glu, conv_w, gain_c, gain_s):
    t = proj.shape[0]
    dc = conv_w.shape[1]
    ds = y.shape[1]

    def body(p_ref, halo_ref, y_ref, z_ref, dm_ref, wg_ref, cw_ref, gc_ref, gs_ref,
             db_ref, dconv_ref, dy_ref, dwg_ref, dcw_ref, dgc_ref, dgs_ref):
        i = pl.program_id(0)

        @pl.when(i == 0)
        def _():
            dwg_ref[...] = jnp.zeros_like(dwg_ref)
            dcw_ref[...] = jnp.zeros_like(dcw_ref)
            dgc_ref[...] = jnp.zeros_like(dgc_ref)
            dgs_ref[...] = jnp.zeros_like(dgs_ref)

        p = p_ref[...]
        b, c, v = p[:, :dc], p[:, dc:2 * dc], p[:, 2 * dc:3 * dc]
        cv = c * v
        hp = halo_ref[...]
        x2, x1 = _causal_taps(cv, hp[:, dc:2 * dc] * hp[:, 2 * dc:3 * dc], i == 0)
        cw = cw_ref[...]
        conv = cw[0:1] * x2 + cw[1:2] * x1 + cw[2:3] * cv
        co = b * conv
        dm = dm_ref[...]
        dco, dgt = _rms_bwd(co, _mean_sq_rsqrt(co), gc_ref[...], dm[:, :dc])
        dgc_ref[...] += jnp.sum(dgt, axis=0, keepdims=True)
        db_ref[...] = (dco * conv).astype(BF16)
        dconv = dco * b
        dconv_ref[...] = dconv
        dcw_ref[0:1, :] += jnp.sum(dconv * x2, axis=0, keepdims=True)
        dcw_ref[1:2, :] += jnp.sum(dconv * x1, axis=0, keepdims=True)
        dcw_ref[2:3, :] += jnp.sum(dconv * cv, axis=0, keepdims=True)

        yv = y_ref[...]
        g, th = _gelu(yv)
        sg = jax.nn.sigmoid(z_ref[...])
        so = g * sg
        dso, dgt = _rms_bwd(so, _mean_sq_rsqrt(so), gs_ref[...], dm[:, dc:])
        dgs_ref[...] += jnp.sum(dgt, axis=0, keepdims=True)
        dz = (dso * g * sg * (1.0 - sg)).astype(BF16)
        dg = dso * sg + _dot(dz, wg_ref[...], NT)
        dwg_ref[...] += _dot(g.astype(BF16), dz, TN)
        dy_ref[...] = dg * _gelu_grad(yv, th)

    const = lambda i: (0, 0)
    row = lambda w: pl.BlockSpec((ROW_TILE, w), lambda i: (i, 0))
    return pl.pallas_call(
        body, name="mix_bwd1", grid=(t // ROW_TILE,),
        in_specs=[row(3 * dc), pl.BlockSpec((HALO_ROWS, 3 * dc), lambda i: (_prev_halo(i, t), 0)),
                  row(ds), row(ds), row(dc + ds), pl.BlockSpec((ds, ds), const),
                  pl.BlockSpec(conv_w.shape, const), pl.BlockSpec((1, dc), const),
                  pl.BlockSpec((1, ds), const)],
        out_specs=[row(dc), row(dc), row(ds), pl.BlockSpec((ds, ds), const),
                   pl.BlockSpec(conv_w.shape, const), pl.BlockSpec((1, dc), const),
                   pl.BlockSpec((1, ds), const)],
        out_shape=[jax.ShapeDtypeStruct((t, dc), BF16), jax.ShapeDtypeStruct((t, dc), F32),
                   jax.ShapeDtypeStruct((t, ds), F32), jax.ShapeDtypeStruct((ds, ds), F32),
                   jax.ShapeDtypeStruct(conv_w.shape, F32), jax.ShapeDtypeStruct((1, dc), F32),
                   jax.ShapeDtypeStruct((1, ds), F32)],
        compiler_params=_params(),
    )(proj, proj, y, z, dmixed, w_glu, conv_w, gain_c, gain_s)


def _mix_bwd2(proj, dconv, conv_w, deps=()):
    t = proj.shape[0]
    dc = conv_w.shape[1]
    n_tiles = t // ROW_TILE

    def body(c_ref, v_ref, d_ref, halo_ref, cw_ref, o_ref):
        i = pl.program_id(0)
        d = d_ref[...]
        u1, u2 = _anticausal_taps(d, halo_ref[...], i == n_tiles - 1)
        cw = cw_ref[...]
        dcv = cw[2:3] * d + cw[1:2] * u1 + cw[0:1] * u2
        o_ref[:, :dc] = (dcv * v_ref[...]).astype(BF16)
        o_ref[:, dc:] = (dcv * c_ref[...]).astype(BF16)

    return pl.pallas_call(
        _ignoring_deps(body, 5, deps), name="mix_bwd2", grid=(n_tiles,),
        in_specs=[pl.BlockSpec((ROW_TILE, dc), lambda i: (i, 1)),
                  pl.BlockSpec((ROW_TILE, dc), lambda i: (i, 2)),
                  pl.BlockSpec((ROW_TILE, dc), lambda i: (i, 0)),
                  pl.BlockSpec((HALO_ROWS, dc), lambda i: (_next_halo(i, t), 0)),
                  pl.BlockSpec(conv_w.shape, lambda i: (0, 0))] + [ANY_SPEC] * len(deps),
        out_specs=pl.BlockSpec((ROW_TILE, 2 * dc), lambda i: (i, 0)),
        out_shape=jax.ShapeDtypeStruct((t, 2 * dc), BF16), compiler_params=_params(),
    )(proj, proj, dconv, dconv, conv_w, *deps)


def _stage_rows(stage_ref, row0, src, wrap=None):
    n, width = src.shape
    for c in range(0, width, STAGE_COLS):
        v = src[:, pl.ds(c, STAGE_COLS)].astype(F32)
        stage_ref[pl.ds(row0, n), pl.ds(c, STAGE_COLS)] = v if wrap is None else wrap(v)


def _conv_taps(stage_ref, fw_ref, fb_ref, c0, r, rows=ROW_CHUNK):
    cols = pl.ds(c0, FFN_COLS)
    x0, x1, x2 = (stage_ref[pl.ds(HALO_ROWS + r - k * N_SEG, rows), cols] for k in range(3))
    w = fw_ref[:, cols]
    return w[0:1] * x2 + w[1:2] * x1 + w[2:3] * x0 + fb_ref[:, cols], x2, x1, x0


def _gated_fwd(up_pre, fw, fb):
    t, f2 = up_pre.shape
    f = f2 // 2

    def body(up_ref, halo_ref, fw_ref, fb_ref, act_ref, stage_ref):
        first = pl.program_id(0) == 0
        _stage_rows(stage_ref, 0, halo_ref, lambda v: _wrap_prev_halo(v, first))
        _stage_rows(stage_ref, HALO_ROWS, up_ref)
        for c0 in range(0, f, FFN_COLS):
            for r in range(0, ROW_TILE, ROW_CHUNK):
                a = _conv_taps(stage_ref, fw_ref, fb_ref, c0, r)[0]
                val = _conv_taps(stage_ref, fw_ref, fb_ref, f + c0, r)[0]
                act_ref[pl.ds(r, ROW_CHUNK), pl.ds(c0, FFN_COLS)] = (a * jax.nn.sigmoid(a) * val).astype(BF16)

    const = lambda a: pl.BlockSpec(a.shape, lambda i: (0, 0))
    return pl.pallas_call(
        body, name="ffn_fwd", grid=(t // ROW_TILE,),
        in_specs=[pl.BlockSpec((ROW_TILE, f2), lambda i: (i, 0)),
                  pl.BlockSpec((HALO_ROWS, f2), lambda i: (_prev_halo(i, t), 0)), const(fw), const(fb)],
        out_specs=pl.BlockSpec((ROW_TILE, f), lambda i: (i, 0)),
        out_shape=jax.ShapeDtypeStruct((t, f), BF16),
        scratch_shapes=[pltpu.VMEM((HALO_ROWS + ROW_TILE, f2), F32)], compiler_params=_params(),
    )(up_pre, up_pre, fw, fb)


def _gated_bwd(up_pre, dact, fw, fb, deps=()):
    t, f2 = up_pre.shape
    f = f2 // 2
    n_tiles = t // ROW_TILE
    chunks = [(r, ROW_CHUNK) for r in range(0, ROW_TILE, ROW_CHUNK)] + [(ROW_TILE, HALO_ROWS)]

    def body(up_ref, prev_ref, next_ref, dact_in_ref, dact_next_ref, fw_ref, fb_ref,
             out_ref, dfw_ref, dfb_ref, dup_ref, stage_ref, dact_ref):
        first = pl.program_id(0) == 0
        last = pl.program_id(0) == n_tiles - 1

        @pl.when(first)
        def _():
            dfw_ref[...] = jnp.zeros_like(dfw_ref)
            dfb_ref[...] = jnp.zeros_like(dfb_ref)

        _stage_rows(dact_ref, 0, dact_in_ref)
        _stage_rows(dact_ref, ROW_TILE, dact_next_ref, lambda v: _wrap_next_halo(v, last))
        _stage_rows(stage_ref, 0, prev_ref, lambda v: _wrap_prev_halo(v, first))
        _stage_rows(stage_ref, HALO_ROWS, up_ref)
        _stage_rows(stage_ref, HALO_ROWS + ROW_TILE, next_ref, lambda v: _wrap_next_halo(v, last))
        fold = lambda v: sum(v[s:s + SUBLANES] for s in range(0, ROW_CHUNK, SUBLANES))
        for c0 in range(0, f, FFN_COLS):
            starts = (c0, f + c0)
            sums = [[jnp.zeros((SUBLANES, FFN_COLS), F32)] * 4 for _ in starts]
            for r, rows in chunks:
                taps = [_conv_taps(stage_ref, fw_ref, fb_ref, c, r, rows) for c in starts]
                a, val = taps[0][0], taps[1][0]
                da_ct = dact_ref[pl.ds(r, rows), pl.ds(c0, FFN_COLS)]
                sg = jax.nn.sigmoid(a)
                dup = (da_ct * val * sg * (1.0 + a * (1.0 - sg)), da_ct * a * sg)
                for k in range(2):
                    dup_ref[k, pl.ds(r, rows), :] = dup[k]
                    if r < ROW_TILE:
                        terms = (dup[k], dup[k] * taps[k][1], dup[k] * taps[k][2], dup[k] * taps[k][3])
                        sums[k] = [s + fold(v) for s, v in zip(sums[k], terms)]
            for k, c in enumerate(starts):
                cols = pl.ds(c, FFN_COLS)
                s_b, s_w0, s_w1, s_w2 = (jnp.sum(p, axis=0, keepdims=True) for p in sums[k])
                dfb_ref[:, cols] += s_b
                for tap, s_w in enumerate((s_w0, s_w1, s_w2)):
                    dfw_ref[tap:tap + 1, cols] += s_w
                cw = fw_ref[:, cols]
                for r in range(0, ROW_TILE, ROW_CHUNK):
                    d, u1, u2 = (dup_ref[k, pl.ds(r + s * N_SEG, ROW_CHUNK), :] for s in range(3))
                    out_ref[pl.ds(r, ROW_CHUNK), cols] = (cw[2:3] * d + cw[1:2] * u1 + cw[0:1] * u2).astype(BF16)

    tile = lambda w: pl.BlockSpec((ROW_TILE, w), lambda i: (i, 0))
    halo = lambda w, index: pl.BlockSpec((HALO_ROWS, w), lambda i: (index(i, t), 0))
    const = lambda a: pl.BlockSpec(a.shape, lambda i: (0, 0))
    return pl.pallas_call(
        _ignoring_deps(body, 7, deps), name="ffn_bwd", grid=(n_tiles,),
        in_specs=[tile(f2), halo(f2, _prev_halo), halo(f2, _next_halo), tile(f), halo(f, _next_halo),
                  const(fw), const(fb)] + [ANY_SPEC] * len(deps),
        out_specs=[tile(f2), const(fw), const(fb)],
        out_shape=[jax.ShapeDtypeStruct((t, f2), BF16), jax.ShapeDtypeStruct(fw.shape, F32),
                   jax.ShapeDtypeStruct(fb.shape, F32)],
        scratch_shapes=[pltpu.VMEM((2, ROW_TILE + HALO_ROWS, FFN_COLS), F32),
                        pltpu.VMEM((HALO_ROWS + ROW_TILE + HALO_ROWS, f2), F32),
                        pltpu.VMEM((ROW_TILE + HALO_ROWS, f), F32)],
        compiler_params=_params(),
    )(up_pre, up_pre, up_pre, dact, dact, fw, fb, *deps)


def _to_segments(a):
    t, c = a.shape
    return a.reshape(N_SEG, t // N_SEG, c).transpose(1, 0, 2).reshape(t, c)


def _from_segments(a):
    t, c = a.shape
    return a.reshape(t // N_SEG, N_SEG, c).transpose(1, 0, 2).reshape(t, c)


def _cmul(ar, ai, br, bi):
    return ar * br - ai * bi, ar * bi + ai * br


def _cpow(ar, ai, n):
    out = None
    while n:
        if n & 1:
            out = (ar, ai) if out is None else _cmul(out[0], out[1], ar, ai)
        ar, ai = _cmul(ar, ai, ar, ai)
        n >>= 1
    return out


def _segment_carries(pr, pi, fr, fi, forward):
    row = lax.broadcasted_iota(jnp.int32, fr.shape, 0)
    edge = row == (0 if forward else N_SEG - 1)
    shift = 1 if forward else N_SEG - 1
    sr, si = jnp.zeros_like(fr), jnp.zeros_like(fi)
    for _ in range(N_SEG - 1):
        tr, ti = _cmul(pr, pi, sr, si)
        sr = jnp.where(edge, 0.0, pltpu.roll(tr + fr, shift, 0))
        si = jnp.where(edge, 0.0, pltpu.roll(ti + fi, shift, 0))
    return sr, si


def _rows(i):
    return pl.ds(pl.multiple_of(i * SUBLANES, SUBLANES), SUBLANES)


def _s5_fwd(proj, u_col, bb_re, bb_im, a_re, a_im, cc_re, cc_im, d_skip):
    t = proj.shape[0]
    nch, _, cs = bb_re.shape
    ds = nch * SSM_CHUNK
    u_blk = u_col // SSM_CHUNK
    steps = t // N_SEG

    def body(u_ref, bbr_ref, bbi_ref, ar_ref, ai_ref, ccr_ref, cci_ref, d_ref, sr_ref, si_ref, y_ref):
        ub = u_ref[...].astype(BF16)
        sr_ref[...] = _dot(ub, bbr_ref[...], NN)
        si_ref[...] = _dot(ub, bbi_ref[...], NN)
        ar = jnp.broadcast_to(ar_ref[...], (N_SEG, cs))
        ai = jnp.broadcast_to(ai_ref[...], (N_SEG, cs))
        zero = jnp.zeros((N_SEG, cs), F32)

        def totals(i, carry):
            tr, ti = _cmul(ar, ai, *carry)
            return tr + sr_ref[_rows(i), :], ti + si_ref[_rows(i), :]

        fr, fi = lax.fori_loop(0, steps, totals, (zero, zero), unroll=SCAN_UNROLL)
        s0r, s0i = _segment_carries(*_cpow(ar, ai, steps), fr, fi, True)

        def scan(i, carry):
            tr, ti = _cmul(ar, ai, *carry)
            nr, ni = tr + sr_ref[_rows(i), :], ti + si_ref[_rows(i), :]
            sr_ref[_rows(i), :] = nr
            si_ref[_rows(i), :] = ni
            return nr, ni

        lax.fori_loop(0, steps, scan, (s0r, s0i), unroll=SCAN_UNROLL)
        y_ref[...] = (_dot(sr_ref[...].astype(BF16), ccr_ref[...], NN)
                      - _dot(si_ref[...].astype(BF16), cci_ref[...], NN)
                      + d_ref[...] * u_ref[...])

    chunk3 = lambda r, c: pl.BlockSpec((None, r, c), lambda j: (j, 0, 0))
    return pl.pallas_call(
        body, name="s5_fwd", grid=(nch,),
        in_specs=[pl.BlockSpec((t, SSM_CHUNK), lambda j: (0, j + u_blk)),
                  chunk3(SSM_CHUNK, cs), chunk3(SSM_CHUNK, cs), chunk3(1, cs), chunk3(1, cs),
                  chunk3(cs, SSM_CHUNK), chunk3(cs, SSM_CHUNK), chunk3(1, SSM_CHUNK)],
        out_specs=[pl.BlockSpec((t, cs), lambda j: (0, j)), pl.BlockSpec((t, cs), lambda j: (0, j)),
                   pl.BlockSpec((t, SSM_CHUNK), lambda j: (0, j))],
        out_shape=[jax.ShapeDtypeStruct((t, nch * cs), F32), jax.ShapeDtypeStruct((t, nch * cs), F32),
                   jax.ShapeDtypeStruct((t, ds), F32)],
        compiler_params=_params(),
    )(proj, bb_re, bb_im, a_re, a_im, cc_re, cc_im, d_skip)


def _s5_bwd(dy, proj, u_col, s_re, s_im, bb_re, bb_im, a_re, a_im, cc_re, cc_im, d_skip, gpc):
    t, ds = dy.shape
    nch, _, cs = bb_re.shape
    u_blk = u_col // SSM_CHUNK
    steps = t // N_SEG

    def body(dy_ref, u_ref, sr_ref, si_ref, bbr_ref, bbi_ref, ar_ref, ai_ref, ccr_ref, cci_ref, d_ref,
             du_ref, dbbr_ref, dbbi_ref, dar_ref, dai_ref, dccr_ref, dcci_ref, dd_ref, gr_ref, gi_ref):
        dyv = dy_ref[...]
        dyb = dyv.astype(BF16)
        gr_ref[...] = _dot(dyb, ccr_ref[...], NT)
        gi_ref[...] = -_dot(dyb, cci_ref[...], NT)
        ar = jnp.broadcast_to(ar_ref[...], (N_SEG, cs))
        ai = -jnp.broadcast_to(ai_ref[...], (N_SEG, cs))
        zero = jnp.zeros((N_SEG, cs), F32)

        def totals(k, carry):
            i = steps - 1 - k
            tr, ti = _cmul(ar, ai, *carry)
            return tr + gr_ref[_rows(i), :], ti + gi_ref[_rows(i), :]

        fr, fi = lax.fori_loop(0, steps, totals, (zero, zero), unroll=SCAN_UNROLL)
        e0r, e0i = _segment_carries(*_cpow(ar, ai, steps), fr, fi, False)

        def step(i, gr, gi, pr, pi, acc_r, acc_i):
            tr, ti = _cmul(ar, ai, gr, gi)
            nr, ni = tr + gr_ref[_rows(i), :], ti + gi_ref[_rows(i), :]
            gr_ref[_rows(i), :] = nr
            gi_ref[_rows(i), :] = ni
            return nr, ni, acc_r + nr * pr + ni * pi, acc_i + ni * pr - nr * pi

        def scan(k, carry):
            i = steps - 1 - k
            gr, gi, acc_r, acc_i = carry
            return step(i, gr, gi, sr_ref[_rows(i - 1), :], si_ref[_rows(i - 1), :], acc_r, acc_i)

        gr, gi, acc_r, acc_i = lax.fori_loop(0, steps - 1, scan, (e0r, e0i, zero, zero), unroll=SCAN_UNROLL)
        row = lax.broadcasted_iota(jnp.int32, (N_SEG, cs), 0)
        last = _rows(steps - 1)
        pr = jnp.where(row == 0, 0.0, pltpu.roll(sr_ref[last, :], 1, 0))
        pi = jnp.where(row == 0, 0.0, pltpu.roll(si_ref[last, :], 1, 0))
        _, _, acc_r, acc_i = step(0, gr, gi, pr, pi, acc_r, acc_i)
        dar_ref[...] = jnp.sum(acc_r, axis=0, keepdims=True)
        dai_ref[...] = jnp.sum(acc_i, axis=0, keepdims=True)

        uv = u_ref[...]
        ub = uv.astype(BF16)
        grb = gr_ref[...].astype(BF16)
        gib = gi_ref[...].astype(BF16)
        du = d_ref[...] * dyv + _dot(grb, bbr_ref[...], NT) + _dot(gib, bbi_ref[...], NT)
        du_ref[...] = du.astype(BF16)
        def put_groups(ref, full):
            for gl in range(gpc):
                ref[gl] = full[gl * hb:(gl + 1) * hb, gl * pb:(gl + 1) * pb]

        put_groups(dbbr_ref, _dot(ub, grb, TN))
        put_groups(dbbi_ref, _dot(ub, gib, TN))
        put_groups(dccr_ref, _dot(dyb, sr_ref[...].astype(BF16), TN))
        put_groups(dcci_ref, -_dot(dyb, si_ref[...].astype(BF16), TN))
        dd_ref[...] = jnp.sum(dyv * uv, axis=0, keepdims=True)

    hb, pb = SSM_CHUNK // gpc, cs // gpc
    groups = pl.BlockSpec((None, gpc, hb, pb), lambda j: (j, 0, 0, 0))
    groups_shape = jax.ShapeDtypeStruct((nch, gpc, hb, pb), F32)
    chunk3 = lambda r, c: pl.BlockSpec((None, r, c), lambda j: (j, 0, 0))
    cols = lambda w: pl.BlockSpec((t, w), lambda j: (0, j))
    return pl.pallas_call(
        body, name="s5_bwd", grid=(nch,),
        in_specs=[cols(SSM_CHUNK), pl.BlockSpec((t, SSM_CHUNK), lambda j: (0, j + u_blk)), cols(cs), cols(cs),
                  chunk3(SSM_CHUNK, cs), chunk3(SSM_CHUNK, cs), chunk3(1, cs), chunk3(1, cs),
                  chunk3(cs, SSM_CHUNK), chunk3(cs, SSM_CHUNK), chunk3(1, SSM_CHUNK)],
        out_specs=[cols(SSM_CHUNK), groups, groups, chunk3(1, cs), chunk3(1, cs), groups, groups,
                   chunk3(1, SSM_CHUNK)],
        out_shape=[jax.ShapeDtypeStruct((t, ds), BF16), groups_shape, groups_shape,
                   jax.ShapeDtypeStruct((nch, 1, cs), F32), jax.ShapeDtypeStruct((nch, 1, cs), F32),
                   groups_shape, groups_shape, jax.ShapeDtypeStruct((nch, 1, SSM_CHUNK), F32)],
        scratch_shapes=[pltpu.VMEM((t, cs), F32), pltpu.VMEM((t, cs), F32)],
        compiler_params=_params(),
    )(dy, proj, s_re, s_im, bb_re, bb_im, a_re, a_im, cc_re, cc_im, d_skip)


def _discretize(lr, li, log_dt, br, bi):
    dt = jnp.exp(log_dt)
    mag = jnp.exp(lr * dt)
    ang = li * dt
    a_re = mag * jnp.cos(ang)
    a_im = mag * jnp.sin(ang)
    den = lr * lr + li * li
    nr = a_re - 1.0
    f_re = (nr * lr + a_im * li) / den
    f_im = (a_im * lr - nr * li) / den
    return a_re, a_im, f_re * br - f_im * bi, f_re * bi + f_im * br


def _whole(shape):
    return pl.BlockSpec(shape, lambda: (0,) * len(shape))


def _disc_fwd(lr, li, log_dt, br, bi, deps=()):
    def body(lr_ref, li_ref, dt_ref, br_ref, bi_ref, ar_ref, ai_ref, bbr_ref, bbi_ref):
        outs = _discretize(lr_ref[...], li_ref[...], dt_ref[...], br_ref[...], bi_ref[...])
        for ref, val in zip((ar_ref, ai_ref, bbr_ref, bbi_ref), outs):
            ref[...] = val

    args = (lr, li, log_dt, br, bi)
    outs = (lr, lr, br, br)
    return pl.pallas_call(
        _ignoring_deps(body, 5, deps), name="disc_fwd",
        in_specs=[_whole(a.shape) for a in args] + [ANY_SPEC] * len(deps),
        out_specs=[_whole(a.shape) for a in outs],
        out_shape=[jax.ShapeDtypeStruct(a.shape, F32) for a in outs],
    )(*args, *deps)


def _disc_bwd(lr, li, log_dt, br, bi, dar, dai, dbbr, dbbi):
    def body(lr_ref, li_ref, dt_ref, br_ref, bi_ref, dar_ref, dai_ref, dbbr_ref, dbbi_ref,
             dlr_ref, dli_ref, ddt_ref, dbr_ref, dbi_ref):
        _, vjp = jax.vjp(_discretize, lr_ref[...], li_ref[...], dt_ref[...], br_ref[...], bi_ref[...])
        grads = vjp((dar_ref[...], dai_ref[...], dbbr_ref[...], dbbi_ref[...]))
        for ref, val in zip((dlr_ref, dli_ref, ddt_ref, dbr_ref, dbi_ref), grads):
            ref[...] = val

    args = (lr, li, log_dt, br, bi, dar, dai, dbbr, dbbi)
    outs = (lr, li, log_dt, br, bi)
    return pl.pallas_call(
        body, name="disc_bwd", in_specs=[_whole(a.shape) for a in args],
        out_specs=[_whole(a.shape) for a in outs],
        out_shape=[jax.ShapeDtypeStruct(a.shape, F32) for a in outs],
    )(*args)


def _adamw(w, g, m, v):
    m = ADAM_B1 * m + (1.0 - ADAM_B1) * g
    v = ADAM_B2 * v + (1.0 - ADAM_B2) * (g * g)
    m_hat = m / ADAM_BC1
    v_hat = v / ADAM_BC2
    delta = -ADAM_LR * (m_hat / (jnp.sqrt(v_hat) + ADAM_EPS) + ADAM_WD * w)
    return delta, m, v


def _adamw_reduce(name, parts, w, m, v):
    _, r, c = parts.shape
    tr = r
    for cand in (256, 176, 128):
        if r % cand == 0:
            tr = cand
            break

    def body(p_ref, w_ref, m_ref, v_ref, g_ref, d_ref, nm_ref, nv_ref):
        g = p_ref[0].astype(F32)
        for k in range(1, N_DEV):
            g = g + p_ref[k].astype(F32)
        delta, nm, nv = _adamw(w_ref[...], g, m_ref[...], v_ref[...])
        g_ref[...] = g
        d_ref[...] = delta
        nm_ref[...] = nm
        nv_ref[...] = nv

    blk = pl.BlockSpec((tr, c), lambda i: (i, 0))
    return pl.pallas_call(
        body, name=name, grid=(r // tr,),
        in_specs=[pl.BlockSpec((N_DEV, tr, c), lambda i: (0, i, 0)), blk, blk, blk],
        out_specs=[blk] * 4, out_shape=[jax.ShapeDtypeStruct((r, c), F32)] * 4,
        compiler_params=_params(),
    )(parts, w, m, v)


def _sum_parts(name, parts):
    _, r, c = parts.shape

    def body(p_ref, o_ref):
        g = p_ref[0]
        for k in range(1, N_DEV):
            g = g + p_ref[k]
        o_ref[...] = g

    return pl.pallas_call(
        body, name=name, in_specs=[_whole(parts.shape)], out_specs=_whole((r, c)),
        out_shape=jax.ShapeDtypeStruct((r, c), F32), compiler_params=_params(),
    )(parts)


def _adamw_many(name, grads, ws, ms, vs):
    n = len(grads)

    def body(*refs):
        ins, outs = refs[:4 * n], refs[4 * n:]
        for i in range(n):
            g, w, m, v = (ins[j * n + i][...] for j in range(4))
            for ref, val in zip((outs[i], outs[n + i], outs[2 * n + i]), _adamw(w, g, m, v)):
                ref[...] = val

    args = (*grads, *ws, *ms, *vs)
    outs = pl.pallas_call(
        body, name=name, in_specs=[_whole(a.shape) for a in args],
        out_specs=[_whole(a.shape) for a in ws] * 3,
        out_shape=[jax.ShapeDtypeStruct(a.shape, F32) for a in ws] * 3, compiler_params=_params(),
    )(*args)
    return outs[:n], outs[n:2 * n], outs[2 * n:]


def _pack(arrays, rows):
    flat = jnp.concatenate([a.reshape(-1) for a in arrays])
    return jnp.pad(flat, (0, rows * LANES - flat.shape[0])).reshape(rows, LANES)


def _unpack(packed, shapes):
    flat = packed.reshape(-1)
    out, off = [], 0
    for s in shapes:
        n = math.prod(s)
        out.append(flat[off:off + n].reshape(s))
        off += n
    return out


def _packed_rows(shapes):
    n = sum(math.prod(s) for s in shapes)
    return -(-n // (SUBLANES * LANES)) * SUBLANES


def _block_diag(x):
    j, g, r, c = x.shape
    eye = jnp.eye(g, dtype=x.dtype)
    return (x[:, :, :, None, :] * eye[None, :, None, :, None]).reshape(j, g * r, g * c)


def kernel(x, meta_tokens, norm_mix_g, w_in, conv_w, ssm_lam_re, ssm_lam_im, ssm_log_dt, ssm_b_re, ssm_b_im, ssm_c_re, ssm_c_im, ssm_d, ssm_w_glu, gain_conv_out, gain_ssm_out, w_out, norm_ffn_g, w_up, ffn_conv_w, ffn_conv_b, w_down, norm_final_g, loss_target, m_meta_tokens, m_norm_mix_g, m_w_in, m_conv_w, m_ssm_lam_re, m_ssm_lam_im, m_ssm_log_dt, m_ssm_b_re, m_ssm_b_im, m_ssm_c_re, m_ssm_c_im, m_ssm_d, m_ssm_w_glu, m_gain_conv_out, m_gain_ssm_out, m_w_out, m_norm_ffn_g, m_w_up, m_ffn_conv_w, m_ffn_conv_b, m_w_down, m_norm_final_g, v_meta_tokens, v_norm_mix_g, v_w_in, v_conv_w, v_ssm_lam_re, v_ssm_lam_im, v_ssm_log_dt, v_ssm_b_re, v_ssm_b_im, v_ssm_c_re, v_ssm_c_im, v_ssm_d, v_ssm_w_glu, v_gain_conv_out, v_gain_ssm_out, v_w_out, v_norm_ffn_g, v_w_up, v_ffn_conv_w, v_ffn_conv_b, v_w_down, v_norm_final_g):
    weights = dict(meta_tokens=meta_tokens, norm_mix_g=norm_mix_g, w_in=w_in, conv_w=conv_w, ssm_lam_re=ssm_lam_re, ssm_lam_im=ssm_lam_im, ssm_log_dt=ssm_log_dt, ssm_b_re=ssm_b_re, ssm_b_im=ssm_b_im, ssm_c_re=ssm_c_re, ssm_c_im=ssm_c_im, ssm_d=ssm_d, ssm_w_glu=ssm_w_glu, gain_conv_out=gain_conv_out, gain_ssm_out=gain_ssm_out, w_out=w_out, norm_ffn_g=norm_ffn_g, w_up=w_up, ffn_conv_w=ffn_conv_w, ffn_conv_b=ffn_conv_b, w_down=w_down, norm_final_g=norm_final_g)
    mom_m = dict(meta_tokens=m_meta_tokens, norm_mix_g=m_norm_mix_g, w_in=m_w_in, conv_w=m_conv_w, ssm_lam_re=m_ssm_lam_re, ssm_lam_im=m_ssm_lam_im, ssm_log_dt=m_ssm_log_dt, ssm_b_re=m_ssm_b_re, ssm_b_im=m_ssm_b_im, ssm_c_re=m_ssm_c_re, ssm_c_im=m_ssm_c_im, ssm_d=m_ssm_d, ssm_w_glu=m_ssm_w_glu, gain_conv_out=m_gain_conv_out, gain_ssm_out=m_gain_ssm_out, w_out=m_w_out, norm_ffn_g=m_norm_ffn_g, w_up=m_w_up, ffn_conv_w=m_ffn_conv_w, ffn_conv_b=m_ffn_conv_b, w_down=m_w_down, norm_final_g=m_norm_final_g)
    mom_v = dict(meta_tokens=v_meta_tokens, norm_mix_g=v_norm_mix_g, w_in=v_w_in, conv_w=v_conv_w, ssm_lam_re=v_ssm_lam_re, ssm_lam_im=v_ssm_lam_im, ssm_log_dt=v_ssm_log_dt, ssm_b_re=v_ssm_b_re, ssm_b_im=v_ssm_b_im, ssm_c_re=v_ssm_c_re, ssm_c_im=v_ssm_c_im, ssm_d=v_ssm_d, ssm_w_glu=v_ssm_w_glu, gain_conv_out=v_gain_conv_out, gain_ssm_out=v_gain_ssm_out, w_out=v_w_out, norm_ffn_g=v_norm_ffn_g, w_up=v_w_up, ffn_conv_w=v_ffn_conv_w, ffn_conv_b=v_ffn_conv_b, w_down=v_w_down, norm_final_g=v_norm_final_g)
    names = list(weights)

    n_meta, d_meta = meta_tokens.shape
    seq, d = x.shape[1], x.shape[2]
    rows_used = n_meta + seq
    t = -(-rows_used // ROW_TILE) * ROW_TILE
    d_in_s = w_in.shape[2]
    dc_s = conv_w.shape[2]
    dc = dc_s * N_DEV
    ds = ssm_w_glu.shape[2]
    n_groups, n_state, grp = ssm_b_re.shape[1:]
    ns = n_groups * n_state
    nch = ds // SSM_CHUNK
    gpc = n_groups // nch
    ff_s = w_up.shape[2]
    dn_s = w_down.shape[1]
    assert 3 * dc + ds == d_in_s * N_DEV and 2 * dn_s == ff_s and t % (N_SEG * SUBLANES) == 0

    small_shard = jnp.concatenate([meta_tokens.reshape(-1), conv_w.reshape(-1), ffn_conv_w.reshape(-1)])
    n_small = small_shard.shape[0]
    small_rows = -(-n_small // LANES)
    small_shard = jnp.pad(small_shard, (0, small_rows * LANES - n_small)).reshape(small_rows, LANES)
    ag, ag_token = _gather2_start("gather_weights_start", [
        small_shard, w_in[0].astype(BF16), ssm_w_glu[0].astype(BF16), w_out[0].astype(BF16),
        jnp.swapaxes(w_up[0], 0, 1).astype(BF16), w_down[0].astype(BF16)])
    fb = ffn_conv_b

    gh = n_groups * grp
    per_h = lambda a: jnp.broadcast_to(a.reshape(n_groups, 1, -1), (n_groups, grp, n_state)).reshape(gh, n_state)
    ghp = lambda a: a.transpose(0, 1, 3, 2).reshape(gh, n_state)
    lr, li, log_dt_e = per_h(ssm_lam_re), per_h(ssm_lam_im), per_h(ssm_log_dt)
    br, bi = ghp(ssm_b_re), ghp(ssm_b_im)
    a_re, a_im, bb_re, bb_im = _disc_fwd(lr, li, log_dt_e, br, bi, deps=(ag_token,))
    cs = gpc * n_state
    chunk_row = lambda a: a.reshape(n_groups, grp, n_state)[:, 0].reshape(nch, 1, cs)
    to_bb = lambda a: _block_diag(a.reshape(nch, gpc, grp, n_state)).astype(BF16)
    to_cc = lambda a: _block_diag(a.reshape(nch, gpc, grp, n_state).transpose(0, 1, 3, 2)).astype(BF16)
    bbm_re, bbm_im = to_bb(bb_re), to_bb(bb_im)
    ccm_re, ccm_im = to_cc(ssm_c_re), to_cc(ssm_c_im)
    a_re_c, a_im_c = chunk_row(a_re), chunk_row(a_im)
    d_skip = ssm_d.reshape(nch, 1, SSM_CHUNK)

    target = _to_segments(jnp.pad(loss_target[0] + ag_token[0, 0], ((n_meta, t - rows_used), (0, 0))))
    token = _gather2_forward("gather_weights_forward_first", ag, (0, 1), (bbm_im, ccm_im, target))
    (g_small,) = _gather2_wait("gather_weights_wait_small", ag, (0,), token)
    g_small = g_small.reshape(N_DEV, -1)
    o1 = n_meta * d_meta
    o2 = o1 + 3 * dc_s
    meta_full = g_small[:, :o1].reshape(N_DEV, n_meta, d_meta).transpose(1, 0, 2).reshape(n_meta, d)
    conv_w_f = g_small[:, o1:o2].reshape(N_DEV, 3, dc_s).transpose(1, 0, 2).reshape(3, dc)
    fw = g_small[:, o2:o2 + 3 * ff_s].reshape(N_DEV, 3, ff_s).transpose(1, 0, 2).reshape(3, N_DEV * ff_s)
    h0 = _to_segments(jnp.concatenate([meta_full, x[0], jnp.zeros((t - rows_used, d), F32)], axis=0))
    full_t = lambda w: pl.BlockSpec((t, w), lambda *_: (0, 0))

    hn1 = _rms_fwd("norm_mix", h0, norm_mix_g)
    (g_in,) = _gather2_wait("gather_weights_wait_in", ag, (1,), hn1)
    proj = _mm("proj", hn1, g_in, dims=NN, grid=(N_DEV,), a_spec=full_t(d),
               b_spec=pl.BlockSpec((None, d, d_in_s), lambda j: (j, 0, 0)),
               o_spec=pl.BlockSpec((t, d_in_s), lambda j: (0, j)),
               out_shape=jax.ShapeDtypeStruct((t, N_DEV * d_in_s), F32))
    s_re, s_im, y_ssm = _s5_fwd(proj, 3 * dc, bbm_re, bbm_im, a_re_c, a_im_c, ccm_re, ccm_im, d_skip)
    token = _gather2_forward("gather_weights_forward_up", ag, (2, 3, 4), (y_ssm,))
    g_glu, g_out = _gather2_wait("gather_weights_wait_mix", ag, (2, 3), token)
    w_out_f = g_out.reshape(-1, d)
    w_glu_f = g_glu.reshape(ds, ds)
    mixed, z_glu, h1, hn2 = _mix_fwd(proj, y_ssm, w_glu_f, conv_w_f, gain_conv_out, gain_ssm_out,
                                     h0, w_out_f, norm_ffn_g)
    tn_out = 256
    token = _gather2_forward("gather_weights_forward_down", ag, (5,), (hn2,))
    (g_up,) = _gather2_wait("gather_weights_wait_up", ag, (4,), token)
    f2 = N_DEV * ff_s
    pair = 2 * ff_s
    w_up_t = g_up.reshape(f2, d)
    up_pre = _mm("up_proj", hn2, w_up_t, dims=NT, grid=(f2 // pair,), a_spec=full_t(d),
                 b_spec=pl.BlockSpec((pair, d), lambda j: (j, 0)),
                 o_spec=pl.BlockSpec((t, pair), lambda j: (0, j)),
                 out_shape=jax.ShapeDtypeStruct((t, f2), BF16))
    act = _gated_fwd(up_pre, fw, fb)
    (g_down,) = _gather2_wait("gather_weights_wait_down", ag, (5,), act)
    w_down_f = g_down.reshape(f2 // 2, d)

    dh2, dh2_b, loss_part, d_norm_final = _down_loss_bwd(
        act, w_down_f, h1, target, norm_final_g.reshape(1, d), n_meta, rows_used)
    dw_down = _mm("down_wgrad", act, dh2_b, dims=TN, grid=(f2 // 2 // pair,),
                  a_spec=pl.BlockSpec((t, pair), lambda j: (0, j)), b_spec=full_t(d),
                  o_spec=pl.BlockSpec((pair, d), lambda j: (j, 0)),
                  out_shape=jax.ShapeDtypeStruct((f2 // 2, d), BF16))
    ex_down, token = _send_start("exchange_down_start", [dw_down.reshape(N_DEV, dn_s, d)], gather=False)
    dact = _mm("down_dgrad", dh2_b, w_down_f, dims=NT, grid=(f2 // 2 // pair,), a_spec=full_t(d),
               b_spec=pl.BlockSpec((pair, d), lambda j: (j, 0)),
               o_spec=pl.BlockSpec((t, pair), lambda j: (0, j)),
               out_shape=jax.ShapeDtypeStruct((t, f2 // 2), BF16))
    dup_pre, d_fw, d_fb = _gated_bwd(up_pre, dact, fw, fb, deps=(token,))
    dw_up = _mm("up_wgrad", dup_pre, hn2, dims=TN, grid=(f2 // pair,),
                a_spec=pl.BlockSpec((t, pair), lambda j: (0, j)), b_spec=full_t(d),
                o_spec=pl.BlockSpec((pair, d), lambda j: (j, 0)),
                out_shape=jax.ShapeDtypeStruct((f2, d), BF16))
    ex_up, token = _send_start("exchange_up_start", [dw_up.reshape(N_DEV, ff_s, d)], gather=False)
    dh1, dh1_b, d_norm_ffn = _dgrad_rms_bwd("up_dgrad_norm_bwd", dh2, dup_pre, w_up_t, NN, h1, norm_ffn_g,
                                            deps=(token,))
    dmixed = _mm("out_dgrad", dh1_b, w_out_f, dims=NT, grid=((dc + ds) // tn_out,), a_spec=full_t(d),
                 b_spec=pl.BlockSpec((tn_out, d), lambda i: (i, 0)),
                 o_spec=pl.BlockSpec((t, tn_out), lambda i: (0, i)),
                 out_shape=jax.ShapeDtypeStruct((t, dc + ds), F32))
    dw_out = _mm("out_wgrad", mixed, dh1_b, dims=TN, grid=((dc + ds) // tn_out,),
                 a_spec=pl.BlockSpec((t, tn_out), lambda i: (0, i)), b_spec=full_t(d),
                 o_spec=pl.BlockSpec((tn_out, d), lambda i: (i, 0)),
                 out_shape=jax.ShapeDtypeStruct((dc + ds, d), BF16))
    db_gate, dconv, dy_ssm, d_wglu, d_conv_w, d_gain_c, d_gain_s = _mix_bwd1(
        proj, y_ssm, z_glu, dmixed, w_glu_f, conv_w_f, gain_conv_out, gain_ssm_out)
    (du, d_bbm_re, d_bbm_im, d_a_re, d_a_im, d_ccm_re, d_ccm_im, d_dskip) = _s5_bwd(
        dy_ssm, proj, 3 * dc, s_re, s_im, bbm_re, bbm_im, a_re_c, a_im_c, ccm_re, ccm_im, d_skip, gpc)

    from_bb = from_cc = lambda a: a.reshape(gh, n_state)
    first_h = lambda a: jnp.pad(a.reshape(n_groups, 1, n_state), ((0, 0), (0, grp - 1), (0, 0))).reshape(gh, n_state)
    over_h = lambda a: a.reshape(n_groups, grp, n_state).sum(axis=1)
    d_lr, d_li, d_dt_e, d_br, d_bi = _disc_bwd(
        lr, li, log_dt_e, br, bi, first_h(d_a_re), first_h(d_a_im), from_bb(d_bbm_re), from_bb(d_bbm_im))

    rep2d = dict(
        ssm_lam_re=(n_groups, n_state), ssm_lam_im=(n_groups, n_state), ssm_log_dt=(1, n_groups),
        ssm_b_re=(gh, n_state), ssm_b_im=(gh, n_state), ssm_c_re=(gh, n_state),
        ssm_c_im=(gh, n_state), ssm_d=(n_groups, grp), gain_conv_out=(1, dc),
        gain_ssm_out=(1, ds), norm_ffn_g=(1, d), ffn_conv_b=(1, N_DEV * ff_s), norm_final_g=(1, d))
    rep_names = list(rep2d)
    rep_grads = dict(
        ssm_lam_re=over_h(d_lr), ssm_lam_im=over_h(d_li), ssm_log_dt=over_h(d_dt_e).sum(axis=1),
        ssm_b_re=d_br, ssm_b_im=d_bi, ssm_c_re=from_cc(d_ccm_re), ssm_c_im=from_cc(d_ccm_im),
        ssm_d=d_dskip, gain_conv_out=d_gain_c, gain_ssm_out=d_gain_s, norm_ffn_g=d_norm_ffn,
        ffn_conv_b=d_fb, norm_final_g=d_norm_final)
    rep_shapes = [rep2d[n] for n in rep_names] + [(1, 1)]
    rep_rows = _packed_rows(rep_shapes)
    rep_pack = _pack([rep_grads[n] for n in rep_names] + [loss_part], rep_rows)
    ex_mix, token = _send_start("exchange_mix_start", [
        rep_pack, dw_out.reshape(N_DEV, -1, d), d_wglu.astype(BF16).reshape(N_DEV, -1, ds),
        d_conv_w.reshape(3, N_DEV, dc_s).transpose(1, 0, 2),
        d_fw.reshape(3, N_DEV, ff_s).transpose(1, 0, 2)],
        gather=[True, False, False, False, False])
    dcdv = _mix_bwd2(proj, dconv, conv_w_f, deps=(token,))
    dproj = jnp.concatenate([db_gate, dcdv, du], axis=1)
    d_in = N_DEV * d_in_s
    w_in_rows = g_in.transpose(1, 0, 2).reshape(d, d_in)
    dw_in = _mm("proj_wgrad", hn1, dproj, dims=TN, grid=(N_DEV,), a_spec=full_t(d),
                b_spec=pl.BlockSpec((t, d_in_s), lambda j: (0, j)),
                o_spec=pl.BlockSpec((None, d, d_in_s), lambda j: (j, 0, 0)),
                out_shape=jax.ShapeDtypeStruct((N_DEV, d, d_in_s), BF16))
    dh0, _, d_norm_mix = _dgrad_rms_bwd("proj_dgrad_norm_bwd", dh1, dproj, w_in_rows, NT, h0, norm_mix_g)
    dh0 = _from_segments(dh0)
    grad_x = dh0[n_meta:rows_used][None]
    d_meta_b = dh0[:n_meta].reshape(n_meta, N_DEV, d_meta).transpose(1, 0, 2)
    ex_in, ex_in_token = _send_start("exchange_in_start", [dw_in, d_meta_b, d_norm_mix],
                                     gather=[False, False, True])

    shard_out = {}

    def update(n, parts, transposed=False):
        sh = weights[n].shape
        two_d = lambda a: (jnp.swapaxes(a[0], 0, 1) if transposed else a.reshape(parts.shape[1:]))
        res = _adamw_reduce("adamw_" + n, parts, two_d(weights[n]), two_d(mom_m[n]), two_d(mom_v[n]))
        shard_out[n] = [(jnp.swapaxes(r, 0, 1) if transposed else r).reshape(sh) for r in res]
        return res[0]

    (p_down,) = _send_wait("exchange_down_wait", ex_down, (0,), ex_in_token)
    done = update("w_down", p_down)
    (p_up,) = _send_wait("exchange_up_wait", ex_up, (0,), done)
    done = update("w_up", p_up, transposed=True)
    (rep_parts,) = _send_wait("gather_small_grads_wait", ex_mix, (0,), done)
    rep_sum = _sum_parts("sum_small_grads", rep_parts)
    *rep_g, loss = _unpack(rep_sum, rep_shapes)
    loss = loss.reshape(())
    swapped = ("ssm_b_re", "ssm_b_im")
    to2d = lambda n, a: ghp(a) if n in swapped else a.reshape(rep2d[n])
    from2d = lambda n, a: (a.reshape(1, n_groups, grp, n_state).transpose(0, 1, 3, 2) if n in swapped
                           else a.reshape(weights[n].shape))
    as2d = lambda tree: [to2d(n, tree[n]) for n in rep_names]
    rep_res = _adamw_many("adamw_replicated", rep_g, as2d(weights), as2d(mom_m), as2d(mom_v))
    for i, n in enumerate(rep_names):
        shard_out[n] = [from2d(n, r) for r in (rep_g[i], *(res[i] for res in rep_res))]

    p_out, p_glu, p_cw, p_fw = _send_wait("exchange_mix_wait", ex_mix, (1, 2, 3, 4), rep_sum)
    update("w_out", p_out)
    update("ssm_w_glu", p_glu)
    update("conv_w", p_cw)
    done = update("ffn_conv_w", p_fw)
    p_in, p_meta, p_nm = _send_wait("exchange_in_wait", ex_in, (0, 1, 2), done)
    update("w_in", p_in)
    update("meta_tokens", p_meta)
    update("norm_mix_g", p_nm)

    grads = [shard_out[n][0] for n in names]
    deltas = [shard_out[n][1] for n in names]
    new_m = [shard_out[n][2] for n in names]
    new_v = [shard_out[n][3] for n in names]
    return (loss, grad_x, *grads, *deltas, *new_m, *new_v)
```

```python
import math

import jax
import jax.numpy as jnp
from jax import lax
from jax.experimental import pallas as pl
from jax.experimental.pallas import tpu as pltpu

F32 = jnp.float32
BF16 = jnp.bfloat16
MESH = pl.DeviceIdType.MESH

N_DEV = 8
RMS_EPS = 1e-6
ADAM_LR = 0.001
ADAM_B1 = 0.9
ADAM_B2 = 0.999
ADAM_EPS = 1e-08
ADAM_WD = 0.01
ADAM_STEP = 10
ADAM_BC1 = 1.0 - ADAM_B1 ** ADAM_STEP
ADAM_BC2 = 1.0 - ADAM_B2 ** ADAM_STEP

SUBLANES = 8
LANES = 128
ROW_TILE = 128
ROW_CHUNK = 32
WIDE_TILES = (544, 256, 128)
FFN_COLS = 256
STAGE_COLS = 256
N_SEG = 8
HALO_ROWS = 16
SSM_CHUNK = 128
SCAN_UNROLL = 2
VMEM_LIMIT = 48 * 1024 * 1024

NN = ((1,), (0,))
NT = ((1,), (1,))
TN = ((0,), (0,))


def _params(**kw):
    return pltpu.CompilerParams(vmem_limit_bytes=VMEM_LIMIT, **kw)


def _dot(a, b, dims):
    return lax.dot_general(a, b, (dims, ((), ())), preferred_element_type=F32)


def _mean_sq_rsqrt(x):
    return lax.rsqrt(jnp.mean(x * x, axis=-1, keepdims=True) + RMS_EPS)


def _rms_bwd(x, r, g, dy):
    xhat = x * r
    dxh = dy * g
    dx = r * (dxh - xhat * jnp.mean(dxh * xhat, axis=-1, keepdims=True))
    return dx, dy * xhat


def _gelu(y):
    c = math.sqrt(2.0 / math.pi)
    t = jnp.tanh(c * (y + 0.044715 * y * y * y))
    return 0.5 * y * (1.0 + t), t


def _gelu_grad(y, t):
    c = math.sqrt(2.0 / math.pi)
    return 0.5 * (1.0 + t) + 0.5 * y * (1.0 - t * t) * c * (1.0 + 3.0 * 0.044715 * y * y)


def _wrap_prev_halo(halo, first):
    seg = lax.broadcasted_iota(jnp.int32, halo.shape, 0) % N_SEG
    wrapped = jnp.where(seg == 0, 0.0, pltpu.roll(halo, 1, 0))
    return jnp.where(first, wrapped, halo)


def _wrap_next_halo(halo, last):
    seg = lax.broadcasted_iota(jnp.int32, halo.shape, 0) % N_SEG
    wrapped = jnp.where(seg == N_SEG - 1, 0.0, pltpu.roll(halo, halo.shape[0] - 1, 0))
    return jnp.where(last, wrapped, halo)


def _dev_index(p):
    return 4 * p[0] + 2 * p[1] + p[2]


def _allgather(name, shards, deps=()):
    n = len(shards)

    def body(*refs):
        ins, outs = refs[:n], refs[n:2 * n]
        send_sems, recv_sems, local_sems = refs[2 * n:]
        x, y, c = lax.axis_index("x"), lax.axis_index("y"), lax.axis_index("c")
        me, sibling = (x, y, c), (x, y, 1 - c)
        chips = [(1 - x, y), (x, 1 - y), (1 - x, 1 - y)]

        def copy(a, k, block, to, src=None):
            dst = outs[a].at[_dev_index(block)]
            return pltpu.make_async_remote_copy(
                src_ref=dst if src is None else src, dst_ref=dst,
                send_sem=send_sems.at[a, k], recv_sem=recv_sems.at[a, k],
                device_id=to, device_id_type=MESH)

        mine = [pltpu.make_async_copy(ins[a], outs[a].at[_dev_index(me)], local_sems.at[a])
                for a in range(n)]
        for cp in mine:
            cp.start()
        first = []
        for a in range(n):
            first.append(copy(a, 0, me, sibling, src=ins[a]))
            for j, chip in enumerate(chips):
                first.append(copy(a, 1 + j, me, (*chip, c), src=ins[a]))
        for cp in first:
            cp.start()
        passed = []
        for j, chip in enumerate(chips):
            for a in range(n):
                copy(a, 1 + j, (*chip, c), me).wait_recv()
                fwd = copy(a, 4 + j, (*chip, c), sibling)
                fwd.start()
                passed.append(fwd)
        for a in range(n):
            copy(a, 0, sibling, me).wait_recv()
            for j, chip in enumerate(chips):
                copy(a, 4 + j, (*chip, 1 - c), me).wait_recv()
        for cp in first + passed:
            cp.wait_send()
        for cp in mine:
            cp.wait()

    any_spec = pl.BlockSpec(memory_space=pl.ANY)
    return pl.pallas_call(
        _ignoring_deps(body, n, deps), name=name,
        out_shape=[jax.ShapeDtypeStruct((N_DEV,) + s.shape, s.dtype) for s in shards],
        in_specs=[any_spec] * (n + len(deps)), out_specs=[any_spec] * n,
        scratch_shapes=[pltpu.SemaphoreType.DMA((n, 7)), pltpu.SemaphoreType.DMA((n, 7)),
                        pltpu.SemaphoreType.DMA((n,))],
    )(*shards, *deps)


HBM_SPEC = pl.BlockSpec(memory_space=pltpu.HBM)
SEM_SPEC = pl.BlockSpec(memory_space=pltpu.SEMAPHORE)
ANY_SPEC = pl.BlockSpec(memory_space=pl.ANY)
DATAFLOW = pltpu.SideEffectType.DATAFLOW_SIDE_EFFECTING


def _ignoring_deps(body, n_in, deps):
    n_dep = len(deps)

    def wrapped(*refs):
        return body(*refs[:n_in], *refs[n_in + n_dep:])

    return wrapped


def _my_position():
    x, y, c = lax.axis_index("x"), lax.axis_index("y"), lax.axis_index("c")
    return (x, y, c)


def _peer(me, k):
    return tuple((1 - v) if (k >> s) & 1 else v for v, s in zip(me, (2, 1, 0)))


def _split_copies(src_refs, land_refs, send_sems, recv_sems, gather):
    me = _my_position()
    copies = []
    for a, (src, land) in enumerate(zip(src_refs, land_refs)):
        for k in range(1, N_DEV):
            peer = _peer(me, k)
            copies.append(pltpu.make_async_remote_copy(
                src_ref=src if gather[a] else src.at[_dev_index(peer)], dst_ref=land.at[_dev_index(me)],
                send_sem=send_sems[a].at[k - 1], recv_sem=recv_sems[a].at[k - 1],
                device_id=peer, device_id_type=MESH))
    return copies


def _own_slot(block, like_shape):
    me = _dev_index(_my_position())
    return lax.dynamic_update_index_in_dim(lax.empty(like_shape, block.dtype), block, me, 0)


def _send_start(name, srcs, gather):
    n = len(srcs)
    me = _dev_index(_my_position())
    gather = [gather] * n if isinstance(gather, bool) else list(gather)
    lands = [_own_slot(s, (N_DEV,) + s.shape) if g else
             _own_slot(lax.dynamic_index_in_dim(s, me, 0, keepdims=False), s.shape)
             for s, g in zip(srcs, gather)]

    def body(*refs):
        src_refs, land_refs = refs[:n], refs[n:2 * n]
        send_sems, recv_sems = refs[2 * n:3 * n], refs[3 * n:4 * n]
        token = refs[-1]
        for cp in _split_copies(src_refs, land_refs, send_sems, recv_sems, gather):
            cp.start()
        token[...] = jnp.zeros_like(token)

    hbm = lambda a: pltpu.HBM(a.shape, a.dtype)
    sems = [pltpu.SemaphoreType.DMA((N_DEV - 1,))] * n
    outs = pl.pallas_call(
        body, name=name,
        out_shape=(*sems, *sems, *[hbm(s) for s in srcs], *[hbm(l) for l in lands],
                   jax.ShapeDtypeStruct((SUBLANES, LANES), F32)),
        in_specs=[HBM_SPEC] * (2 * n),
        out_specs=(*[SEM_SPEC] * (2 * n), *[HBM_SPEC] * (2 * n), pl.BlockSpec(memory_space=pltpu.VMEM)),
        input_output_aliases={i: 2 * n + i for i in range(2 * n)},
        compiler_params=pltpu.CompilerParams(has_side_effects=DATAFLOW),
    )(*[pltpu.with_memory_space_constraint(a, pltpu.HBM) for a in (*srcs, *lands)])
    state = dict(send=outs[:n], recv=outs[n:2 * n], srcs=outs[2 * n:3 * n], lands=outs[3 * n:4 * n],
                 gather=gather)
    return state, outs[-1]


def _send_wait(name, state, which, after):
    n = len(which)
    pick = lambda key: [state[key][i] for i in which]
    gather = pick("gather")

    def body(*refs):
        src_refs, land_refs = refs[:n], refs[n:2 * n]
        send_sems, recv_sems = refs[2 * n:3 * n], refs[3 * n:4 * n]
        for cp in _split_copies(src_refs, land_refs, send_sems, recv_sems, gather):
            cp.wait_send()
            cp.wait_recv()

    srcs, lands = pick("srcs"), pick("lands")
    hbm = lambda a: pltpu.HBM(a.shape, a.dtype)
    outs = pl.pallas_call(
        body, name=name,
        out_shape=(*[hbm(s) for s in srcs], *[hbm(l) for l in lands]),
        in_specs=[*[HBM_SPEC] * (2 * n), *[SEM_SPEC] * (2 * n), ANY_SPEC],
        out_specs=tuple([HBM_SPEC] * (2 * n)),
        input_output_aliases={i: i for i in range(2 * n)},
        compiler_params=pltpu.CompilerParams(has_side_effects=DATAFLOW),
    )(*srcs, *lands, *pick("send"), *pick("recv"), after)
    return outs[n:]


def _two_level_copies(src_refs, land_refs, sems1, sems2):
    x, y, c = _my_position()
    me, sibling = (x, y, c), (x, y, 1 - c)
    chips = [(1 - x, y), (x, 1 - y), (1 - x, 1 - y)]
    stage1, stage2 = [], []
    for a, land in enumerate(land_refs):
        def copy(src, block, to, sems, k):
            return pltpu.make_async_remote_copy(
                src_ref=src, dst_ref=land.at[_dev_index(block)], send_sem=sems[0][a].at[k],
                recv_sem=sems[1][a].at[k], device_id=to, device_id_type=MESH)
        src = src_refs[a] if src_refs is not None else land.at[_dev_index(me)]
        stage1.append([copy(src, me, sibling, sems1, 0)] +
                      [copy(src, me, (*chip, c), sems1, 1 + j) for j, chip in enumerate(chips)])
        if sems2 is not None:
            stage2.append([copy(land.at[_dev_index((*chip, c))], (*chip, c), sibling, sems2, j)
                           for j, chip in enumerate(chips)])
    return stage1, stage2


def _gather2_start(name, shards):
    n = len(shards)
    lands = [_own_slot(s, (N_DEV,) + s.shape) for s in shards]

    def body(*refs):
        src_refs, land_refs = refs[:n], refs[n:2 * n]
        sems1 = (refs[2 * n:3 * n], refs[3 * n:4 * n])
        stage1, _ = _two_level_copies(src_refs, land_refs, sems1, None)
        for copies in stage1:
            for cp in copies:
                cp.start()
        refs[-1][...] = jnp.zeros_like(refs[-1])

    hbm = lambda a: pltpu.HBM(a.shape, a.dtype)
    sems = [pltpu.SemaphoreType.DMA((4,))] * n
    outs = pl.pallas_call(
        body, name=name,
        out_shape=(*sems, *sems, *[hbm(s) for s in shards], *[hbm(l) for l in lands],
                   jax.ShapeDtypeStruct((SUBLANES, LANES), F32)),
        in_specs=[HBM_SPEC] * (2 * n),
        out_specs=(*[SEM_SPEC] * (2 * n), *[HBM_SPEC] * (2 * n), pl.BlockSpec(memory_space=pltpu.VMEM)),
        input_output_aliases={i: 2 * n + i for i in range(2 * n)},
        compiler_params=pltpu.CompilerParams(has_side_effects=DATAFLOW),
    )(*[pltpu.with_memory_space_constraint(a, pltpu.HBM) for a in (*shards, *lands)])
    state = dict(send1=list(outs[:n]), recv1=list(outs[n:2 * n]), srcs=list(outs[2 * n:3 * n]),
                 lands=list(outs[3 * n:4 * n]), send2={}, recv2={})
    return state, outs[-1]


def _gather2_forward(name, state, which, after):
    n = len(which)
    pick = lambda key: [state[key][i] for i in which]

    def body(*refs):
        land_refs, recv1 = refs[:n], refs[n:2 * n]
        outs = refs[2 * n + len(after):]
        sems2 = (outs[:n], outs[n:2 * n])
        stage1, stage2 = _two_level_copies(None, land_refs, (recv1, recv1), sems2)
        for a in range(n):
            for j in range(3):
                stage1[a][1 + j].wait_recv()
                stage2[a][j].start()
        outs[-1][...] = jnp.zeros_like(outs[-1])

    lands = pick("lands")
    sems = [pltpu.SemaphoreType.DMA((3,))] * n
    outs = pl.pallas_call(
        body, name=name,
        out_shape=(*sems, *sems, *[pltpu.HBM(l.shape, l.dtype) for l in lands],
                   jax.ShapeDtypeStruct((SUBLANES, LANES), F32)),
        in_specs=[*[HBM_SPEC] * n, *[SEM_SPEC] * n, *[ANY_SPEC] * len(after)],
        out_specs=(*[SEM_SPEC] * (2 * n), *[HBM_SPEC] * n, pl.BlockSpec(memory_space=pltpu.VMEM)),
        input_output_aliases={i: 2 * n + i for i in range(n)},
        compiler_params=pltpu.CompilerParams(has_side_effects=DATAFLOW),
    )(*lands, *pick("recv1"), *after)
    for idx, i in enumerate(which):
        state["send2"][i], state["recv2"][i] = outs[idx], outs[n + idx]
        state["lands"][i] = outs[2 * n + idx]
    return outs[-1]


def _gather2_wait(name, state, which, after):
    n = len(which)
    pick = lambda key: [state[key][i] for i in which]

    def body(*refs):
        src_refs, land_refs = refs[:n], refs[n:2 * n]
        sems1 = (refs[2 * n:3 * n], refs[3 * n:4 * n])
        sems2 = (refs[4 * n:5 * n], refs[5 * n:6 * n])
        stage1, stage2 = _two_level_copies(src_refs, land_refs, sems1, sems2)
        for a in range(n):
            for cp in stage1[a]:
                cp.wait_send()
            stage1[a][0].wait_recv()
            for cp in stage2[a]:
                cp.wait_send()
                cp.wait_recv()

    srcs, lands = pick("srcs"), pick("lands")
    hbm = lambda a: pltpu.HBM(a.shape, a.dtype)
    outs = pl.pallas_call(
        body, name=name,
        out_shape=(*[hbm(s) for s in srcs], *[hbm(l) for l in lands]),
        in_specs=[*[HBM_SPEC] * (2 * n), *[SEM_SPEC] * (4 * n), ANY_SPEC],
        out_specs=tuple([HBM_SPEC] * (2 * n)),
        input_output_aliases={i: i for i in range(2 * n)},
        compiler_params=pltpu.CompilerParams(has_side_effects=DATAFLOW),
    )(*srcs, *lands, *pick("send1"), *pick("recv1"), *pick("send2"), *pick("recv2"), after)
    return outs[n:]


def _mm(name, a, b, *, dims, grid, a_spec, b_spec, o_spec, out_shape, acc_shape=None,
        res=None, res_spec=None):
    n_red = grid[-1] if acc_shape is not None else 1
    red_axis = len(grid) - 1

    def body(*refs):
        a_ref, b_ref = refs[0], refs[1]
        r_ref = refs[2] if res is not None else None
        o_ref = refs[3] if res is not None else refs[2]
        part = _dot(a_ref[...], b_ref[...], dims)
        if acc_shape is None:
            if r_ref is not None:
                part = part + r_ref[...]
            o_ref[...] = part.astype(o_ref.dtype)
            return
        acc_ref = refs[-1]
        k = pl.program_id(red_axis)

        @pl.when(k == 0)
        def _():
            acc_ref[...] = part

        @pl.when(k > 0)
        def _():
            acc_ref[...] += part

        @pl.when(k == n_red - 1)
        def _():
            total = acc_ref[...]
            if r_ref is not None:
                total = total + r_ref[...]
            o_ref[...] = total.astype(o_ref.dtype)

    ins, in_specs = [a, b], [a_spec, b_spec]
    if res is not None:
        ins.append(res)
        in_specs.append(res_spec)
    return pl.pallas_call(
        body, name=name, grid=grid, in_specs=in_specs, out_specs=o_spec, out_shape=out_shape,
        scratch_shapes=[pltpu.VMEM(acc_shape, F32)] if acc_shape is not None else [],
        compiler_params=_params(),
    )(*ins)


def _wide_tile(t):
    return next(c for c in WIDE_TILES if t % c == 0)


def _rms_fwd(name, h, g):
    t, d = h.shape
    tile = _wide_tile(t)

    def body(h_ref, g_ref, o_ref):
        x = h_ref[...]
        o_ref[...] = (x * _mean_sq_rsqrt(x) * g_ref[...]).astype(BF16)

    return pl.pallas_call(
        body, name=name, grid=(t // tile,),
        in_specs=[pl.BlockSpec((tile, d), lambda i: (i, 0)), pl.BlockSpec((1, d), lambda i: (0, 0))],
        out_specs=pl.BlockSpec((tile, d), lambda i: (i, 0)),
        out_shape=jax.ShapeDtypeStruct((t, d), BF16), compiler_params=_params(),
    )(h, g)


def _fused_tile(t):
    half = _wide_tile(t) // 2
    return half if half % 16 == 0 and t % half == 0 else _wide_tile(t)


def _dgrad_rms_bwd(name, dres, dy, w, dims, h, g, deps=()):
    t, d = h.shape
    k = dy.shape[1]

    def body(dres_ref, dy_ref, w_ref, h_ref, g_ref, dh_ref, dhb_ref, dg_ref):
        x = h_ref[...]
        dhn = _dot(dy_ref[...], w_ref[...], dims)
        dx, dgt = _rms_bwd(x, _mean_sq_rsqrt(x), g_ref[...], dhn)
        dh = dres_ref[...] + dx
        dh_ref[...] = dh
        dhb_ref[...] = dh.astype(BF16)

        @pl.when(pl.program_id(0) == 0)
        def _():
            dg_ref[...] = jnp.zeros_like(dg_ref)

        dg_ref[...] += jnp.sum(dgt, axis=0, keepdims=True)

    tile = _fused_tile(t)
    row = pl.BlockSpec((tile, d), lambda i: (i, 0))
    vec = pl.BlockSpec((1, d), lambda i: (0, 0))
    return pl.pallas_call(
        _ignoring_deps(body, 5, deps), name=name, grid=(t // tile,),
        in_specs=[row, pl.BlockSpec((tile, k), lambda i: (i, 0)), pl.BlockSpec(w.shape, lambda i: (0, 0)), row,
                  vec] + [ANY_SPEC] * len(deps),
        out_specs=[row, row, vec],
        out_shape=[jax.ShapeDtypeStruct((t, d), F32), jax.ShapeDtypeStruct((t, d), BF16),
                   jax.ShapeDtypeStruct((1, d), F32)],
        compiler_params=_params(),
    )(dres, dy, w, h, g, *deps)


def _down_loss_bwd(act, w_down, h1, target, g, row_lo, row_hi):
    t, d = h1.shape
    f = act.shape[1]
    tile = _fused_tile(t)

    def body(act_ref, w_ref, h_ref, tg_ref, g_ref, dh_ref, dhb_ref, loss_ref, dg_ref):
        i = pl.program_id(0)
        x = h_ref[...] + _dot(act_ref[...], w_ref[...], NN)
        r = _mean_sq_rsqrt(x)
        gv = g_ref[...]
        y = x * r * gv
        rows = i * tile + lax.broadcasted_iota(jnp.int32, (tile, 1), 0)
        time = (rows % N_SEG) * (t // N_SEG) + rows // N_SEG
        valid = jnp.logical_and(time >= row_lo, time < row_hi)
        err = jnp.where(valid, y - tg_ref[...], 0.0)
        dy = err * (1.0 / d)
        dx, dgt = _rms_bwd(x, r, gv, dy)
        dh_ref[...] = dx
        dhb_ref[...] = dx.astype(BF16)

        @pl.when(i == 0)
        def _():
            loss_ref[...] = jnp.zeros_like(loss_ref)
            dg_ref[...] = jnp.zeros_like(dg_ref)

        row_loss = jnp.mean(err * err, axis=-1, keepdims=True)
        loss_ref[...] += 0.5 * jnp.sum(row_loss, axis=0, keepdims=True)
        dg_ref[...] += jnp.sum(dgt, axis=0, keepdims=True)

    row = pl.BlockSpec((tile, d), lambda i: (i, 0))
    vec = pl.BlockSpec((1, d), lambda i: (0, 0))
    return pl.pallas_call(
        body, name="down_proj_loss_bwd", grid=(t // tile,),
        in_specs=[pl.BlockSpec((tile, f), lambda i: (i, 0)), pl.BlockSpec(w_down.shape, lambda i: (0, 0)),
                  row, row, vec],
        out_specs=[row, row, pl.BlockSpec((1, 1), lambda i: (0, 0)), vec],
        out_shape=[jax.ShapeDtypeStruct((t, d), F32), jax.ShapeDtypeStruct((t, d), BF16),
                   jax.ShapeDtypeStruct((1, 1), F32), jax.ShapeDtypeStruct((1, d), F32)],
        compiler_params=_params(),
    )(act, w_down, h1, target, g)


def _prev_halo(i, t):
    return jnp.where(i == 0, t // HALO_ROWS - 1, i * (ROW_TILE // HALO_ROWS) - 1)


def _next_halo(i, t):
    return jnp.where(i == t // ROW_TILE - 1, 0, (i + 1) * (ROW_TILE // HALO_ROWS))


def _causal_taps(cur, halo, first):
    rows = cur.shape[0]
    ext = jnp.concatenate([_wrap_prev_halo(halo, first), cur], axis=0)
    return ext[:rows], ext[N_SEG:N_SEG + rows]


def _anticausal_taps(cur, halo, last):
    rows = cur.shape[0]
    ext = jnp.concatenate([cur, _wrap_next_halo(halo, last)], axis=0)
    return ext[N_SEG:N_SEG + rows], ext[2 * N_SEG:2 * N_SEG + rows]


def _mix_fwd(proj, y, w_glu, conv_w, gain_c, gain_s, h0, w_out, gain_ffn):
    t, d = h0.shape
    dc = conv_w.shape[1]
    ds = y.shape[1]

    def body(p_ref, halo_ref, y_ref, wg_ref, cw_ref, gc_ref, gs_ref, h0_ref, wo_ref, gf_ref,
             mixed_ref, z_ref, h1_ref, hn_ref):
        i = pl.program_id(0)
        p = p_ref[...]
        b, c, v = p[:, :dc], p[:, dc:2 * dc], p[:, 2 * dc:3 * dc]
        cv = c * v
        hp = halo_ref[...]
        x2, x1 = _causal_taps(cv, hp[:, dc:2 * dc] * hp[:, 2 * dc:3 * dc], i == 0)
        cw = cw_ref[...]
        conv = cw[0:1] * x2 + cw[1:2] * x1 + cw[2:3] * cv
        co = b * conv
        mixed_ref[:, :dc] = (co * _mean_sq_rsqrt(co) * gc_ref[...]).astype(BF16)
        g, _ = _gelu(y_ref[...])
        z = _dot(g.astype(BF16), wg_ref[...], NN)
        z_ref[...] = z
        so = g * jax.nn.sigmoid(z)
        mixed_ref[:, dc:] = (so * _mean_sq_rsqrt(so) * gs_ref[...]).astype(BF16)
        h1 = h0_ref[...] + _dot(mixed_ref[...], wo_ref[...], NN)
        h1_ref[...] = h1
        hn_ref[...] = (h1 * _mean_sq_rsqrt(h1) * gf_ref[...]).astype(BF16)

    const = lambda i: (0, 0)
    row = lambda w: pl.BlockSpec((ROW_TILE, w), lambda i: (i, 0))
    return pl.pallas_call(
        body, name="mix_fwd_out_proj", grid=(t // ROW_TILE,),
        in_specs=[row(3 * dc), pl.BlockSpec((HALO_ROWS, 3 * dc), lambda i: (_prev_halo(i, t), 0)), row(ds),
                  pl.BlockSpec((ds, ds), const), pl.BlockSpec(conv_w.shape, const),
                  pl.BlockSpec((1, dc), const), pl.BlockSpec((1, ds), const),
                  row(d), pl.BlockSpec(w_out.shape, const), pl.BlockSpec((1, d), const)],
        out_specs=[row(dc + ds), row(ds), row(d), row(d)],
        out_shape=[jax.ShapeDtypeStruct((t, dc + ds), BF16), jax.ShapeDtypeStruct((t, ds), F32),
                   jax.ShapeDtypeStruct((t, d), F32), jax.ShapeDtypeStruct((t, d), BF16)],
        compiler_params=_params(),
    )(proj, proj, y, w_glu, conv_w, gain_c, gain_s, h0, w_out, gain_ffn)


def _mix_bwd1(proj, y, z, dmixed, w_glu, conv_w, gain_c, gain_s):
    t = proj.shape[0]
    dc = conv_w.shape[1]
    ds = y.shape[1]

    def body(p_ref, halo_ref, y_ref, z_ref, dm_ref, wg_ref, cw_ref, gc_ref, gs_ref,
             db_ref, dconv_ref, dy_ref, dwg_ref, dcw_ref, dgc_ref, dgs_ref):
        i = pl.program_id(0)

        @pl.when(i == 0)
        def _():
            dwg_ref[...] = jnp.zeros_like(dwg_ref)
            dcw_ref[...] = jnp.zeros_like(dcw_ref)
            dgc_ref[...] = jnp.zeros_like(dgc_ref)
            dgs_ref[...] = jnp.zeros_like(dgs_ref)

        p = p_ref[...]
        b, c, v = p[:, :dc], p[:, dc:2 * dc], p[:, 2 * dc:3 * dc]
        cv = c * v
        hp = halo_ref[...]
        x2, x1 = _causal_taps(cv, hp[:, dc:2 * dc] * hp[:, 2 * dc:3 * dc], i == 0)
        cw = cw_ref[...]
        conv = cw[0:1] * x2 + cw[1:2] * x1 + cw[2:3] * cv
        co = b * conv
        dm = dm_ref[...]
        dco, dgt = _rms_bwd(co, _mean_sq_rsqrt(co), gc_ref[...], dm[:, :dc])
        dgc_ref[...] += jnp.sum(dgt, axis=0, keepdims=True)
        db_ref[...] = (dco * conv).astype(BF16)
        dconv = dco * b
        dconv_ref[...] = dconv
        dcw_ref[0:1, :] += jnp.sum(dconv * x2, axis=0, keepdims=True)
        dcw_ref[1:2, :] += jnp.sum(dconv * x1, axis=0, keepdims=True)
        dcw_ref[2:3, :] += jnp.sum(dconv * cv, axis=0, keepdims=True)

        yv = y_ref[...]
        g, th = _gelu(yv)
        sg = jax.nn.sigmoid(z_ref[...])
        so = g * sg
        dso, dgt = _rms_bwd(so, _mean_sq_rsqrt(so), gs_ref[...], dm[:, dc:])
        dgs_ref[...] += jnp.sum(dgt, axis=0, keepdims=True)
        dz = (dso * g * sg * (1.0 - sg)).astype(BF16)
        dg = dso * sg + _dot(dz, wg_ref[...], NT)
        dwg_ref[...] += _dot(g.astype(BF16), dz, TN)
        dy_ref[...] = dg * _gelu_grad(yv, th)

    const = lambda i: (0, 0)
    row = lambda w: pl.BlockSpec((ROW_TILE, w), lambda i: (i, 0))
    return pl.pallas_call(
        body, name="mix_bwd1", grid=(t // ROW_TILE,),
        in_specs=[row(3 * dc), pl.BlockSpec((HALO_ROWS, 3 * dc), lambda i: (_prev_halo(i, t), 0)),
                  row(ds), row(ds), row(dc + ds), pl.BlockSpec((ds, ds), const),
                  pl.BlockSpec(conv_w.shape, const), pl.BlockSpec((1, dc), const),
                  pl.BlockSpec((1, ds), const)],
        out_specs=[row(dc), row(dc), row(ds), pl.BlockSpec((ds, ds), const),
                   pl.BlockSpec(conv_w.shape, const), pl.BlockSpec((1, dc), const),
                   pl.BlockSpec((1, ds), const)],
        out_shape=[jax.ShapeDtypeStruct((t, dc), BF16), jax.ShapeDtypeStruct((t, dc), F32),
                   jax.ShapeDtypeStruct((t, ds), F32), jax.ShapeDtypeStruct((ds, ds), F32),
                   jax.ShapeDtypeStruct(conv_w.shape, F32), jax.ShapeDtypeStruct((1, dc), F32),
                   jax.ShapeDtypeStruct((1, ds), F32)],
        compiler_params=_params(),
    )(proj, proj, y, z, dmixed, w_glu, conv_w, gain_c, gain_s)


def _mix_bwd2(proj, dconv, conv_w, deps=()):
    t = proj.shape[0]
    dc = conv_w.shape[1]
    n_tiles = t // ROW_TILE

    def body(c_ref, v_ref, d_ref, halo_ref, cw_ref, o_ref):
        i = pl.program_id(0)
        d = d_ref[...]
        u1, u2 = _anticausal_taps(d, halo_ref[...], i == n_tiles - 1)
        cw = cw_ref[...]
        dcv = cw[2:3] * d + cw[1:2] * u1 + cw[0:1] * u2
        o_ref[:, :dc] = (dcv * v_ref[...]).astype(BF16)
        o_ref[:, dc:] = (dcv * c_ref[...]).astype(BF16)

    return pl.pallas_call(
        _ignoring_deps(body, 5, deps), name="mix_bwd2", grid=(n_tiles,),
        in_specs=[pl.BlockSpec((ROW_TILE, dc), lambda i: (i, 1)),
                  pl.BlockSpec((ROW_TILE, dc), lambda i: (i, 2)),
                  pl.BlockSpec((ROW_TILE, dc), lambda i: (i, 0)),
                  pl.BlockSpec((HALO_ROWS, dc), lambda i: (_next_halo(i, t), 0)),
                  pl.BlockSpec(conv_w.shape, lambda i: (0, 0))] + [ANY_SPEC] * len(deps),
        out_specs=pl.BlockSpec((ROW_TILE, 2 * dc), lambda i: (i, 0)),
        out_shape=jax.ShapeDtypeStruct((t, 2 * dc), BF16), compiler_params=_params(),
    )(proj, proj, dconv, dconv, conv_w, *deps)


def _stage_rows(stage_ref, row0, src, wrap=None):
    n, width = src.shape
    for c in range(0, width, STAGE_COLS):
        v = src[:, pl.ds(c, STAGE_COLS)].astype(F32)
        stage_ref[pl.ds(row0, n), pl.ds(c, STAGE_COLS)] = v if wrap is None else wrap(v)


def _conv_taps(stage_ref, fw_ref, fb_ref, c0, r, rows=ROW_CHUNK):
    cols = pl.ds(c0, FFN_COLS)
    x0, x1, x2 = (stage_ref[pl.ds(HALO_ROWS + r - k * N_SEG, rows), cols] for k in range(3))
    w = fw_ref[:, cols]
    return w[0:1] * x2 + w[1:2] * x1 + w[2:3] * x0 + fb_ref[:, cols], x2, x1, x0


def _gated_fwd(up_pre, fw, fb):
    t, f2 = up_pre.shape
    f = f2 // 2

    def body(up_ref, halo_ref, fw_ref, fb_ref, act_ref, stage_ref):
        first = pl.program_id(0) == 0
        _stage_rows(stage_ref, 0, halo_ref, lambda v: _wrap_prev_halo(v, first))
        _stage_rows(stage_ref, HALO_ROWS, up_ref)
        for c0 in range(0, f, FFN_COLS):
            for r in range(0, ROW_TILE, ROW_CHUNK):
                a = _conv_taps(stage_ref, fw_ref, fb_ref, c0, r)[0]
                val = _conv_taps(stage_ref, fw_ref, fb_ref, f + c0, r)[0]
                act_ref[pl.ds(r, ROW_CHUNK), pl.ds(c0, FFN_COLS)] = (a * jax.nn.sigmoid(a) * val).astype(BF16)

    const = lambda a: pl.BlockSpec(a.shape, lambda i: (0, 0))
    return pl.pallas_call(
        body, name="ffn_fwd", grid=(t // ROW_TILE,),
        in_specs=[pl.BlockSpec((ROW_TILE, f2), lambda i: (i, 0)),
                  pl.BlockSpec((HALO_ROWS, f2), lambda i: (_prev_halo(i, t), 0)), const(fw), const(fb)],
        out_specs=pl.BlockSpec((ROW_TILE, f), lambda i: (i, 0)),
        out_shape=jax.ShapeDtypeStruct((t, f), BF16),
        scratch_shapes=[pltpu.VMEM((HALO_ROWS + ROW_TILE, f2), F32)], compiler_params=_params(),
    )(up_pre, up_pre, fw, fb)


def _gated_bwd(up_pre, dact, fw, fb, deps=()):
    t, f2 = up_pre.shape
    f = f2 // 2
    n_tiles = t // ROW_TILE
    chunks = [(r, ROW_CHUNK) for r in range(0, ROW_TILE, ROW_CHUNK)] + [(ROW_TILE, HALO_ROWS)]

    def body(up_ref, prev_ref, next_ref, dact_in_ref, dact_next_ref, fw_ref, fb_ref,
             out_ref, dfw_ref, dfb_ref, dup_ref, stage_ref, dact_ref):
        first = pl.program_id(0) == 0
        last = pl.program_id(0) == n_tiles - 1

        @pl.when(first)
        def _():
            dfw_ref[...] = jnp.zeros_like(dfw_ref)
            dfb_ref[...] = jnp.zeros_like(dfb_ref)

        _stage_rows(dact_ref, 0, dact_in_ref)
        _stage_rows(dact_ref, ROW_TILE, dact_next_ref, lambda v: _wrap_next_halo(v, last))
        _stage_rows(stage_ref, 0, prev_ref, lambda v: _wrap_prev_halo(v, first))
        _stage_rows(stage_ref, HALO_ROWS, up_ref)
        _stage_rows(stage_ref, HALO_ROWS + ROW_TILE, next_ref, lambda v: _wrap_next_halo(v, last))
        fold = lambda v: sum(v[s:s + SUBLANES] for s in range(0, ROW_CHUNK, SUBLANES))
        for c0 in range(0, f, FFN_COLS):
            starts = (c0, f + c0)
            sums = [[jnp.zeros((SUBLANES, FFN_COLS), F32)] * 4 for _ in starts]
            for r, rows in chunks:
                taps = [_conv_taps(stage_ref, fw_ref, fb_ref, c, r, rows) for c in starts]
                a, val = taps[0][0], taps[1][0]
                da_ct = dact_ref[pl.ds(r, rows), pl.ds(c0, FFN_COLS)]
                sg = jax.nn.sigmoid(a)
                dup = (da_ct * val * sg * (1.0 + a * (1.0 - sg)), da_ct * a * sg)
                for k in range(2):
                    dup_ref[k, pl.ds(r, rows), :] = dup[k]
                    if r < ROW_TILE:
                        terms = (dup[k], dup[k] * taps[k][1], dup[k] * taps[k][2], dup[k] * taps[k][3])
                        sums[k] = [s + fold(v) for s, v in zip(sums[k], terms)]
            for k, c in enumerate(starts):
                cols = pl.ds(c, FFN_COLS)
                s_b, s_w0, s_w1, s_w2 = (jnp.sum(p, axis=0, keepdims=True) for p in sums[k])
                dfb_ref[:, cols] += s_b
                for tap, s_w in enumerate((s_w0, s_w1, s_w2)):
                    dfw_ref[tap:tap + 1, cols] += s_w
                cw = fw_ref[:, cols]
                for r in range(0, ROW_TILE, ROW_CHUNK):
                    d, u1, u2 = (dup_ref[k, pl.ds(r + s * N_SEG, ROW_CHUNK), :] for s in range(3))
                    out_ref[pl.ds(r, ROW_CHUNK), cols] = (cw[2:3] * d + cw[1:2] * u1 + cw[0:1] * u2).astype(BF16)

    tile = lambda w: pl.BlockSpec((ROW_TILE, w), lambda i: (i, 0))
    halo = lambda w, index: pl.BlockSpec((HALO_ROWS, w), lambda i: (index(i, t), 0))
    const = lambda a: pl.BlockSpec(a.shape, lambda i: (0, 0))
    return pl.pallas_call(
        _ignoring_deps(body, 7, deps), name="ffn_bwd", grid=(n_tiles,),
        in_specs=[tile(f2), halo(f2, _prev_halo), halo(f2, _next_halo), tile(f), halo(f, _next_halo),
                  const(fw), const(fb)] + [ANY_SPEC] * len(deps),
        out_specs=[tile(f2), const(fw), const(fb)],
        out_shape=[jax.ShapeDtypeStruct((t, f2), BF16), jax.ShapeDtypeStruct(fw.shape, F32),
                   jax.ShapeDtypeStruct(fb.shape, F32)],
        scratch_shapes=[pltpu.VMEM((2, ROW_TILE + HALO_ROWS, FFN_COLS), F32),
                        pltpu.VMEM((HALO_ROWS + ROW_TILE + HALO_ROWS, f2), F32),
                        pltpu.VMEM((ROW_TILE + HALO_ROWS, f), F32)],
        compiler_params=_params(),
    )(up_pre, up_pre, up_pre, dact, dact, fw, fb, *deps)


def _to_segments(a):
    t, c = a.shape
    return a.reshape(N_SEG, t // N_SEG, c).transpose(1, 0, 2).reshape(t, c)


def _from_segments(a):
    t, c = a.shape
    return a.reshape(t // N_SEG, N_SEG, c).transpose(1, 0, 2).reshape(t, c)


def _cmul(ar, ai, br, bi):
    return ar * br - ai * bi, ar * bi + ai * br


def _cpow(ar, ai, n):
    out = None
    while n:
        if n & 1:
            out = (ar, ai) if out is None else _cmul(out[0], out[1], ar, ai)
        ar, ai = _cmul(ar, ai, ar, ai)
        n >>= 1
    return out


def _segment_carries(pr, pi, fr, fi, forward):
    row = lax.broadcasted_iota(jnp.int32, fr.shape, 0)
    edge = row == (0 if forward else N_SEG - 1)
    shift = 1 if forward else N_SEG - 1
    sr, si = jnp.zeros_like(fr), jnp.zeros_like(fi)
    for _ in range(N_SEG - 1):
        tr, ti = _cmul(pr, pi, sr, si)
        sr = jnp.where(edge, 0.0, pltpu.roll(tr + fr, shift, 0))
        si = jnp.where(edge, 0.0, pltpu.roll(ti + fi, shift, 0))
    return sr, si


def _rows(i):
    return pl.ds(pl.multiple_of(i * SUBLANES, SUBLANES), SUBLANES)


def _s5_fwd(proj, u_col, bb_re, bb_im, a_re, a_im, cc_re, cc_im, d_skip):
    t = proj.shape[0]
    nch, _, cs = bb_re.shape
    ds = nch * SSM_CHUNK
    u_blk = u_col // SSM_CHUNK
    steps = t // N_SEG

    def body(u_ref, bbr_ref, bbi_ref, ar_ref, ai_ref, ccr_ref, cci_ref, d_ref, sr_ref, si_ref, y_ref):
        ub = u_ref[...].astype(BF16)
        sr_ref[...] = _dot(ub, bbr_ref[...], NN)
        si_ref[...] = _dot(ub, bbi_ref[...], NN)
        ar = jnp.broadcast_to(ar_ref[...], (N_SEG, cs))
        ai = jnp.broadcast_to(ai_ref[...], (N_SEG, cs))
        zero = jnp.zeros((N_SEG, cs), F32)

        def totals(i, carry):
            tr, ti = _cmul(ar, ai, *carry)
            return tr + sr_ref[_rows(i), :], ti + si_ref[_rows(i), :]

        fr, fi = lax.fori_loop(0, steps, totals, (zero, zero), unroll=SCAN_UNROLL)
        s0r, s0i = _segment_carries(*_cpow(ar, ai, steps), fr, fi, True)

        def scan(i, carry):
            tr, ti = _cmul(ar, ai, *carry)
            nr, ni = tr + sr_ref[_rows(i), :], ti + si_ref[_rows(i), :]
            sr_ref[_rows(i), :] = nr
            si_ref[_rows(i), :] = ni
            return nr, ni

        lax.fori_loop(0, steps, scan, (s0r, s0i), unroll=SCAN_UNROLL)
        y_ref[...] = (_dot(sr_ref[...].astype(BF16), ccr_ref[...], NT)
                      - _dot(si_ref[...].astype(BF16), cci_ref[...], NT)
                      + d_ref[...] * u_ref[...])

    chunk3 = lambda r, c: pl.BlockSpec((None, r, c), lambda j: (j, 0, 0))
    return pl.pallas_call(
        body, name="s5_fwd", grid=(nch,),
        in_specs=[pl.BlockSpec((t, SSM_CHUNK), lambda j: (0, j + u_blk)),
                  chunk3(SSM_CHUNK, cs), chunk3(SSM_CHUNK, cs), chunk3(1, cs), chunk3(1, cs),
                  chunk3(SSM_CHUNK, cs), chunk3(SSM_CHUNK, cs), chunk3(1, SSM_CHUNK)],
        out_specs=[pl.BlockSpec((t, cs), lambda j: (0, j)), pl.BlockSpec((t, cs), lambda j: (0, j)),
                   pl.BlockSpec((t, SSM_CHUNK), lambda j: (0, j))],
        out_shape=[jax.ShapeDtypeStruct((t, nch * cs), F32), jax.ShapeDtypeStruct((t, nch * cs), F32),
                   jax.ShapeDtypeStruct((t, ds), F32)],
        compiler_params=_params(),
    )(proj, bb_re, bb_im, a_re, a_im, cc_re, cc_im, d_skip)


def _s5_bwd(dy, proj, u_col, s_re, s_im, bb_re, bb_im, a_re, a_im, cc_re, cc_im, d_skip, gpc):
    t, ds = dy.shape
    nch, _, cs = bb_re.shape
    u_blk = u_col // SSM_CHUNK
    steps = t // N_SEG

    def body(dy_ref, u_ref, sr_ref, si_ref, bbr_ref, bbi_ref, ar_ref, ai_ref, ccr_ref, cci_ref, d_ref,
             du_ref, dbbr_ref, dbbi_ref, dar_ref, dai_ref, dccr_ref, dcci_ref, dd_ref, gr_ref, gi_ref):
        dyv = dy_ref[...]
        dyb = dyv.astype(BF16)
        gr_ref[...] = _dot(dyb, ccr_ref[...], NN)
        gi_ref[...] = -_dot(dyb, cci_ref[...], NN)
        ar = jnp.broadcast_to(ar_ref[...], (N_SEG, cs))
        ai = -jnp.broadcast_to(ai_ref[...], (N_SEG, cs))
        zero = jnp.zeros((N_SEG, cs), F32)

        def totals(k, carry):
            i = steps - 1 - k
            tr, ti = _cmul(ar, ai, *carry)
            return tr + gr_ref[_rows(i), :], ti + gi_ref[_rows(i), :]

        fr, fi = lax.fori_loop(0, steps, totals, (zero, zero), unroll=SCAN_UNROLL)
        e0r, e0i = _segment_carries(*_cpow(ar, ai, steps), fr, fi, False)

        def step(i, gr, gi, pr, pi, acc_r, acc_i):
            tr, ti = _cmul(ar, ai, gr, gi)
            nr, ni = tr + gr_ref[_rows(i), :], ti + gi_ref[_rows(i), :]
            gr_ref[_rows(i), :] = nr
            gi_ref[_rows(i), :] = ni
            return nr, ni, acc_r + nr * pr + ni * pi, acc_i + ni * pr - nr * pi

        def scan(k, carry):
            i = steps - 1 - k
            gr, gi, acc_r, acc_i = carry
            return step(i, gr, gi, sr_ref[_rows(i - 1), :], si_ref[_rows(i - 1), :], acc_r, acc_i)

        gr, gi, acc_r, acc_i = lax.fori_loop(0, steps - 1, scan, (e0r, e0i, zero, zero), unroll=SCAN_UNROLL)
        row = lax.broadcasted_iota(jnp.int32, (N_SEG, cs), 0)
        last = _rows(steps - 1)
        pr = jnp.where(row == 0, 0.0, pltpu.roll(sr_ref[last, :], 1, 0))
        pi = jnp.where(row == 0, 0.0, pltpu.roll(si_ref[last, :], 1, 0))
        _, _, acc_r, acc_i = step(0, gr, gi, pr, pi, acc_r, acc_i)
        dar_ref[...] = jnp.sum(acc_r, axis=0, keepdims=True)
        dai_ref[...] = jnp.sum(acc_i, axis=0, keepdims=True)

        uv = u_ref[...]
        ub = uv.astype(BF16)
        grb = gr_ref[...].astype(BF16)
        gib = gi_ref[...].astype(BF16)
        du = d_ref[...] * dyv + _dot(grb, bbr_ref[...], NT) + _dot(gib, bbi_ref[...], NT)
        du_ref[...] = du.astype(BF16)
        def put_groups(ref, full):
            for gl in range(gpc):
                ref[gl] = full[gl * hb:(gl + 1) * hb, gl * pb:(gl + 1) * pb]

        put_groups(dbbr_ref, _dot(ub, grb, TN))
        put_groups(dbbi_ref, _dot(ub, gib, TN))
        put_groups(dccr_ref, _dot(dyb, sr_ref[...].astype(BF16), TN))
        put_groups(dcci_ref, -_dot(dyb, si_ref[...].astype(BF16), TN))
        dd_ref[...] = jnp.sum(dyv * uv, axis=0, keepdims=True)

    hb, pb = SSM_CHUNK // gpc, cs // gpc
    groups = pl.BlockSpec((None, gpc, hb, pb), lambda j: (j, 0, 0, 0))
    groups_shape = jax.ShapeDtypeStruct((nch, gpc, hb, pb), F32)
    chunk3 = lambda r, c: pl.BlockSpec((None, r, c), lambda j: (j, 0, 0))
    cols = lambda w: pl.BlockSpec((t, w), lambda j: (0, j))
    return pl.pallas_call(
        body, name="s5_bwd", grid=(nch,),
        in_specs=[cols(SSM_CHUNK), pl.BlockSpec((t, SSM_CHUNK), lambda j: (0, j + u_blk)), cols(cs), cols(cs),
                  chunk3(SSM_CHUNK, cs), chunk3(SSM_CHUNK, cs), chunk3(1, cs), chunk3(1, cs),
                  chunk3(SSM_CHUNK, cs), chunk3(SSM_CHUNK, cs), chunk3(1, SSM_CHUNK)],
        out_specs=[cols(SSM_CHUNK), groups, groups, chunk3(1, cs), chunk3(1, cs), groups, groups,
                   chunk3(1, SSM_CHUNK)],
        out_shape=[jax.ShapeDtypeStruct((t, ds), BF16), groups_shape, groups_shape,
                   jax.ShapeDtypeStruct((nch, 1, cs), F32), jax.ShapeDtypeStruct((nch, 1, cs), F32),
                   groups_shape, groups_shape, jax.ShapeDtypeStruct((nch, 1, SSM_CHUNK), F32)],
        scratch_shapes=[pltpu.VMEM((t, cs), F32), pltpu.VMEM((t, cs), F32)],
        compiler_params=_params(),
    )(dy, proj, s_re, s_im, bb_re, bb_im, a_re, a_im, cc_re, cc_im, d_skip)


def _discretize(lr, li, log_dt, br, bi):
    dt = jnp.exp(log_dt)
    mag = jnp.exp(lr * dt)
    ang = li * dt
    a_re = mag * jnp.cos(ang)
    a_im = mag * jnp.sin(ang)
    den = lr * lr + li * li
    nr = a_re - 1.0
    f_re = (nr * lr + a_im * li) / den
    f_im = (a_im * lr - nr * li) / den
    return a_re, a_im, f_re * br - f_im * bi, f_re * bi + f_im * br


def _whole(shape):
    return pl.BlockSpec(shape, lambda: (0,) * len(shape))


def _disc_fwd(lr, li, log_dt, br, bi, cr, ci, nch, gpc, deps=()):
    gh, n_state = br.shape
    grp = gh // (nch * gpc)
    rows, cs = gpc * grp, gpc * n_state

    def body(lr_ref, li_ref, dt_ref, br_ref, bi_ref, cr_ref, ci_ref,
             ar_ref, ai_ref, bbr_ref, bbi_ref, ccr_ref, cci_ref, wide_ref):
        a_re, a_im, bb_re, bb_im = _discretize(lr_ref[...], li_ref[...], dt_ref[...], br_ref[...], bi_ref[...])
        for ref, val in ((ar_ref, a_re), (ai_ref, a_im)):
            for j in range(nch):
                for gl in range(gpc):
                    g0 = (j * gpc + gl) * grp
                    ref[j, :, gl * n_state:(gl + 1) * n_state] = val[g0:g0 + 1]
        for ref, val in ((bbr_ref, bb_re), (bbi_ref, bb_im), (ccr_ref, cr_ref[...]), (cci_ref, ci_ref[...])):
            wide_ref[...] = jnp.zeros_like(wide_ref)
            for j in range(nch):
                for gl in range(gpc):
                    g0 = (j * gpc + gl) * grp
                    wide_ref[j, gl * grp:(gl + 1) * grp, gl * n_state:(gl + 1) * n_state] = val[g0:g0 + grp]
            ref[...] = wide_ref[...].astype(BF16)

    args = (lr, li, log_dt, br, bi, cr, ci)
    row_shape, wide_shape = (nch, 1, cs), (nch, rows, cs)
    outs = [jax.ShapeDtypeStruct(row_shape, F32)] * 2 + [jax.ShapeDtypeStruct(wide_shape, BF16)] * 4
    return pl.pallas_call(
        _ignoring_deps(body, 7, deps), name="disc_fwd",
        in_specs=[_whole(a.shape) for a in args] + [ANY_SPEC] * len(deps),
        out_specs=[_whole(o.shape) for o in outs], out_shape=outs,
        scratch_shapes=[pltpu.VMEM(wide_shape, F32)],
    )(*args, *deps)


def _disc_bwd(lr, li, log_dt, br, bi, dar, dai, dbbr, dbbi):
    def body(lr_ref, li_ref, dt_ref, br_ref, bi_ref, dar_ref, dai_ref, dbbr_ref, dbbi_ref,
             dlr_ref, dli_ref, ddt_ref, dbr_ref, dbi_ref):
        _, vjp = jax.vjp(_discretize, lr_ref[...], li_ref[...], dt_ref[...], br_ref[...], bi_ref[...])
        grads = vjp((dar_ref[...], dai_ref[...], dbbr_ref[...], dbbi_ref[...]))
        for ref, val in zip((dlr_ref, dli_ref, ddt_ref, dbr_ref, dbi_ref), grads):
            ref[...] = val

    args = (lr, li, log_dt, br, bi, dar, dai, dbbr, dbbi)
    outs = (lr, li, log_dt, br, bi)
    return pl.pallas_call(
        body, name="disc_bwd", in_specs=[_whole(a.shape) for a in args],
        out_specs=[_whole(a.shape) for a in outs],
        out_shape=[jax.ShapeDtypeStruct(a.shape, F32) for a in outs],
    )(*args)


def _adamw(w, g, m, v):
    m = ADAM_B1 * m + (1.0 - ADAM_B1) * g
    v = ADAM_B2 * v + (1.0 - ADAM_B2) * (g * g)
    m_hat = m / ADAM_BC1
    v_hat = v / ADAM_BC2
    delta = -ADAM_LR * (m_hat / (jnp.sqrt(v_hat) + ADAM_EPS) + ADAM_WD * w)
    return delta, m, v


def _adamw_reduce(name, parts, w, m, v):
    _, r, c = parts.shape
    tr = r
    for cand in (256, 176, 128):
        if r % cand == 0:
            tr = cand
            break

    def body(p_ref, w_ref, m_ref, v_ref, g_ref, d_ref, nm_ref, nv_ref):
        g = p_ref[0].astype(F32)
        for k in range(1, N_DEV):
            g = g + p_ref[k].astype(F32)
        delta, nm, nv = _adamw(w_ref[...], g, m_ref[...], v_ref[...])
        g_ref[...] = g
        d_ref[...] = delta
        nm_ref[...] = nm
        nv_ref[...] = nv

    blk = pl.BlockSpec((tr, c), lambda i: (i, 0))
    return pl.pallas_call(
        body, name=name, grid=(r // tr,),
        in_specs=[pl.BlockSpec((N_DEV, tr, c), lambda i: (0, i, 0)), blk, blk, blk],
        out_specs=[blk] * 4, out_shape=[jax.ShapeDtypeStruct((r, c), F32)] * 4,
        compiler_params=_params(),
    )(parts, w, m, v)


def _sum_parts(name, parts):
    _, r, c = parts.shape

    def body(p_ref, o_ref):
        g = p_ref[0]
        for k in range(1, N_DEV):
            g = g + p_ref[k]
        o_ref[...] = g

    return pl.pallas_call(
        body, name=name, in_specs=[_whole(parts.shape)], out_specs=_whole((r, c)),
        out_shape=jax.ShapeDtypeStruct((r, c), F32), compiler_params=_params(),
    )(parts)


def _adamw_many(name, grads, ws, ms, vs):
    n = len(grads)

    def body(*refs):
        ins, outs = refs[:4 * n], refs[4 * n:]
        for i in range(n):
            g, w, m, v = (ins[j * n + i][...] for j in range(4))
            for ref, val in zip((outs[i], outs[n + i], outs[2 * n + i]), _adamw(w, g, m, v)):
                ref[...] = val

    args = (*grads, *ws, *ms, *vs)
    outs = pl.pallas_call(
        body, name=name, in_specs=[_whole(a.shape) for a in args],
        out_specs=[_whole(a.shape) for a in ws] * 3,
        out_shape=[jax.ShapeDtypeStruct(a.shape, F32) for a in ws] * 3, compiler_params=_params(),
    )(*args)
    return outs[:n], outs[n:2 * n], outs[2 * n:]


def _pack(arrays, rows):
    flat = jnp.concatenate([a.reshape(-1) for a in arrays])
    return jnp.pad(flat, (0, rows * LANES - flat.shape[0])).reshape(rows, LANES)


def _unpack(packed, shapes):
    flat = packed.reshape(-1)
    out, off = [], 0
    for s in shapes:
        n = math.prod(s)
        out.append(flat[off:off + n].reshape(s))
        off += n
    return out


def _packed_rows(shapes):
    n = sum(math.prod(s) for s in shapes)
    return -(-n // (SUBLANES * LANES)) * SUBLANES


def kernel(x, meta_tokens, norm_mix_g, w_in, conv_w, ssm_lam_re, ssm_lam_im, ssm_log_dt, ssm_b_re, ssm_b_im, ssm_c_re, ssm_c_im, ssm_d, ssm_w_glu, gain_conv_out, gain_ssm_out, w_out, norm_ffn_g, w_up, ffn_conv_w, ffn_conv_b, w_down, norm_final_g, loss_target, m_meta_tokens, m_norm_mix_g, m_w_in, m_conv_w, m_ssm_lam_re, m_ssm_lam_im, m_ssm_log_dt, m_ssm_b_re, m_ssm_b_im, m_ssm_c_re, m_ssm_c_im, m_ssm_d, m_ssm_w_glu, m_gain_conv_out, m_gain_ssm_out, m_w_out, m_norm_ffn_g, m_w_up, m_ffn_conv_w, m_ffn_conv_b, m_w_down, m_norm_final_g, v_meta_tokens, v_norm_mix_g, v_w_in, v_conv_w, v_ssm_lam_re, v_ssm_lam_im, v_ssm_log_dt, v_ssm_b_re, v_ssm_b_im, v_ssm_c_re, v_ssm_c_im, v_ssm_d, v_ssm_w_glu, v_gain_conv_out, v_gain_ssm_out, v_w_out, v_norm_ffn_g, v_w_up, v_ffn_conv_w, v_ffn_conv_b, v_w_down, v_norm_final_g):
    weights = dict(meta_tokens=meta_tokens, norm_mix_g=norm_mix_g, w_in=w_in, conv_w=conv_w, ssm_lam_re=ssm_lam_re, ssm_lam_im=ssm_lam_im, ssm_log_dt=ssm_log_dt, ssm_b_re=ssm_b_re, ssm_b_im=ssm_b_im, ssm_c_re=ssm_c_re, ssm_c_im=ssm_c_im, ssm_d=ssm_d, ssm_w_glu=ssm_w_glu, gain_conv_out=gain_conv_out, gain_ssm_out=gain_ssm_out, w_out=w_out, norm_ffn_g=norm_ffn_g, w_up=w_up, ffn_conv_w=ffn_conv_w, ffn_conv_b=ffn_conv_b, w_down=w_down, norm_final_g=norm_final_g)
    mom_m = dict(meta_tokens=m_meta_tokens, norm_mix_g=m_norm_mix_g, w_in=m_w_in, conv_w=m_conv_w, ssm_lam_re=m_ssm_lam_re, ssm_lam_im=m_ssm_lam_im, ssm_log_dt=m_ssm_log_dt, ssm_b_re=m_ssm_b_re, ssm_b_im=m_ssm_b_im, ssm_c_re=m_ssm_c_re, ssm_c_im=m_ssm_c_im, ssm_d=m_ssm_d, ssm_w_glu=m_ssm_w_glu, gain_conv_out=m_gain_conv_out, gain_ssm_out=m_gain_ssm_out, w_out=m_w_out, norm_ffn_g=m_norm_ffn_g, w_up=m_w_up, ffn_conv_w=m_ffn_conv_w, ffn_conv_b=m_ffn_conv_b, w_down=m_w_down, norm_final_g=m_norm_final_g)
    mom_v = dict(meta_tokens=v_meta_tokens, norm_mix_g=v_norm_mix_g, w_in=v_w_in, conv_w=v_conv_w, ssm_lam_re=v_ssm_lam_re, ssm_lam_im=v_ssm_lam_im, ssm_log_dt=v_ssm_log_dt, ssm_b_re=v_ssm_b_re, ssm_b_im=v_ssm_b_im, ssm_c_re=v_ssm_c_re, ssm_c_im=v_ssm_c_im, ssm_d=v_ssm_d, ssm_w_glu=v_ssm_w_glu, gain_conv_out=v_gain_conv_out, gain_ssm_out=v_gain_ssm_out, w_out=v_w_out, norm_ffn_g=v_norm_ffn_g, w_up=v_w_up, ffn_conv_w=v_ffn_conv_w, ffn_conv_b=v_ffn_conv_b, w_down=v_w_down, norm_final_g=v_norm_final_g)
    names = list(weights)

    n_meta, d_meta = meta_tokens.shape
    seq, d = x.shape[1], x.shape[2]
    rows_used = n_meta + seq
    t = -(-rows_used // ROW_TILE) * ROW_TILE
    d_in_s = w_in.shape[2]
    dc_s = conv_w.shape[2]
    dc = dc_s * N_DEV
    ds = ssm_w_glu.shape[2]
    n_groups, n_state, grp = ssm_b_re.shape[1:]
    ns = n_groups * n_state
    nch = ds // SSM_CHUNK
    gpc = n_groups // nch
    ff_s = w_up.shape[2]
    dn_s = w_down.shape[1]
    assert 3 * dc + ds == d_in_s * N_DEV and 2 * dn_s == ff_s and t % (N_SEG * SUBLANES) == 0

    small_shard = jnp.concatenate([meta_tokens.reshape(-1), conv_w.reshape(-1), ffn_conv_w.reshape(-1)])
    n_small = small_shard.shape[0]
    small_rows = -(-n_small // LANES)
    small_shard = jnp.pad(small_shard, (0, small_rows * LANES - n_small)).reshape(small_rows, LANES)
    ag, ag_token = _gather2_start("gather_weights_start", [
        small_shard, w_in[0].astype(BF16), ssm_w_glu[0].astype(BF16), w_out[0].astype(BF16),
        jnp.swapaxes(w_up[0], 0, 1).astype(BF16), w_down[0].astype(BF16)])
    fb = ffn_conv_b

    gh = n_groups * grp
    per_h = lambda a: jnp.broadcast_to(a.reshape(n_groups, 1, -1), (n_groups, grp, n_state)).reshape(gh, n_state)
    ghp = lambda a: a.transpose(0, 1, 3, 2).reshape(gh, n_state)
    lr, li, log_dt_e = per_h(ssm_lam_re), per_h(ssm_lam_im), per_h(ssm_log_dt)
    br, bi = ghp(ssm_b_re), ghp(ssm_b_im)
    a_re_c, a_im_c, bbm_re, bbm_im, ccm_re, ccm_im = _disc_fwd(
        lr, li, log_dt_e, br, bi, ssm_c_re.reshape(gh, n_state), ssm_c_im.reshape(gh, n_state), nch, gpc,
        deps=(ag_token,))
    d_skip = ssm_d.reshape(nch, 1, SSM_CHUNK)

    target = _to_segments(jnp.pad(loss_target[0] + ag_token[0, 0], ((n_meta, t - rows_used), (0, 0))))
    token = _gather2_forward("gather_weights_forward_first", ag, (0, 1), (bbm_im, ccm_im, target))
    (g_small,) = _gather2_wait("gather_weights_wait_small", ag, (0,), token)
    g_small = g_small.reshape(N_DEV, -1)
    o1 = n_meta * d_meta
    o2 = o1 + 3 * dc_s
    meta_full = g_small[:, :o1].reshape(N_DEV, n_meta, d_meta).transpose(1, 0, 2).reshape(n_meta, d)
    conv_w_f = g_small[:, o1:o2].reshape(N_DEV, 3, dc_s).transpose(1, 0, 2).reshape(3, dc)
    fw = g_small[:, o2:o2 + 3 * ff_s].reshape(N_DEV, 3, ff_s).transpose(1, 0, 2).reshape(3, N_DEV * ff_s)
    h0 = _to_segments(jnp.concatenate([meta_full, x[0], jnp.zeros((t - rows_used, d), F32)], axis=0))
    full_t = lambda w: pl.BlockSpec((t, w), lambda *_: (0, 0))

    hn1 = _rms_fwd("norm_mix", h0, norm_mix_g)
    (g_in,) = _gather2_wait("gather_weights_wait_in", ag, (1,), hn1)
    proj = _mm("proj", hn1, g_in, dims=NN, grid=(N_DEV,), a_spec=full_t(d),
               b_spec=pl.BlockSpec((None, d, d_in_s), lambda j: (j, 0, 0)),
               o_spec=pl.BlockSpec((t, d_in_s), lambda j: (0, j)),
               out_shape=jax.ShapeDtypeStruct((t, N_DEV * d_in_s), F32))
    s_re, s_im, y_ssm = _s5_fwd(proj, 3 * dc, bbm_re, bbm_im, a_re_c, a_im_c, ccm_re, ccm_im, d_skip)
    token = _gather2_forward("gather_weights_forward_up", ag, (2, 3, 4), (y_ssm,))
    g_glu, g_out = _gather2_wait("gather_weights_wait_mix", ag, (2, 3), token)
    w_out_f = g_out.reshape(-1, d)
    w_glu_f = g_glu.reshape(ds, ds)
    mixed, z_glu, h1, hn2 = _mix_fwd(proj, y_ssm, w_glu_f, conv_w_f, gain_conv_out, gain_ssm_out,
                                     h0, w_out_f, norm_ffn_g)
    tn_out = 256
    token = _gather2_forward("gather_weights_forward_down", ag, (5,), (hn2,))
    (g_up,) = _gather2_wait("gather_weights_wait_up", ag, (4,), token)
    f2 = N_DEV * ff_s
    pair = 2 * ff_s
    w_up_t = g_up.reshape(f2, d)
    up_pre = _mm("up_proj", hn2, w_up_t, dims=NT, grid=(f2 // pair,), a_spec=full_t(d),
                 b_spec=pl.BlockSpec((pair, d), lambda j: (j, 0)),
                 o_spec=pl.BlockSpec((t, pair), lambda j: (0, j)),
                 out_shape=jax.ShapeDtypeStruct((t, f2), BF16))
    act = _gated_fwd(up_pre, fw, fb)
    (g_down,) = _gather2_wait("gather_weights_wait_down", ag, (5,), act)
    w_down_f = g_down.reshape(f2 // 2, d)

    dh2, dh2_b, loss_part, d_norm_final = _down_loss_bwd(
        act, w_down_f, h1, target, norm_final_g.reshape(1, d), n_meta, rows_used)
    dw_down = _mm("down_wgrad", act, dh2_b, dims=TN, grid=(f2 // 2 // pair,),
                  a_spec=pl.BlockSpec((t, pair), lambda j: (0, j)), b_spec=full_t(d),
                  o_spec=pl.BlockSpec((pair, d), lambda j: (j, 0)),
                  out_shape=jax.ShapeDtypeStruct((f2 // 2, d), BF16))
    ex_down, token = _send_start("exchange_down_start", [dw_down.reshape(N_DEV, dn_s, d)], gather=False)
    dact = _mm("down_dgrad", dh2_b, w_down_f, dims=NT, grid=(f2 // 2 // pair,), a_spec=full_t(d),
               b_spec=pl.BlockSpec((pair, d), lambda j: (j, 0)),
               o_spec=pl.BlockSpec((t, pair), lambda j: (0, j)),
               out_shape=jax.ShapeDtypeStruct((t, f2 // 2), BF16))
    dup_pre, d_fw, d_fb = _gated_bwd(up_pre, dact, fw, fb, deps=(token,))
    dw_up = _mm("up_wgrad", dup_pre, hn2, dims=TN, grid=(f2 // pair,),
                a_spec=pl.BlockSpec((t, pair), lambda j: (0, j)), b_spec=full_t(d),
                o_spec=pl.BlockSpec((pair, d), lambda j: (j, 0)),
                out_shape=jax.ShapeDtypeStruct((f2, d), BF16))
    ex_up, token = _send_start("exchange_up_start", [dw_up.reshape(N_DEV, ff_s, d)], gather=False)
    dh1, dh1_b, d_norm_ffn = _dgrad_rms_bwd("up_dgrad_norm_bwd", dh2, dup_pre, w_up_t, NN, h1, norm_ffn_g,
                                            deps=(token,))
    dmixed = _mm("out_dgrad", dh1_b, w_out_f, dims=NT, grid=((dc + ds) // tn_out,), a_spec=full_t(d),
                 b_spec=pl.BlockSpec((tn_out, d), lambda i: (i, 0)),
                 o_spec=pl.BlockSpec((t, tn_out), lambda i: (0, i)),
                 out_shape=jax.ShapeDtypeStruct((t, dc + ds), F32))
    dw_out = _mm("out_wgrad", mixed, dh1_b, dims=TN, grid=((dc + ds) // tn_out,),
                 a_spec=pl.BlockSpec((t, tn_out), lambda i: (0, i)), b_spec=full_t(d),
                 o_spec=pl.BlockSpec((tn_out, d), lambda i: (i, 0)),
                 out_shape=jax.ShapeDtypeStruct((dc + ds, d), BF16))
    db_gate, dconv, dy_ssm, d_wglu, d_conv_w, d_gain_c, d_gain_s = _mix_bwd1(
        proj, y_ssm, z_glu, dmixed, w_glu_f, conv_w_f, gain_conv_out, gain_ssm_out)
    (du, d_bbm_re, d_bbm_im, d_a_re, d_a_im, d_ccm_re, d_ccm_im, d_dskip) = _s5_bwd(
        dy_ssm, proj, 3 * dc, s_re, s_im, bbm_re, bbm_im, a_re_c, a_im_c, ccm_re, ccm_im, d_skip, gpc)

    from_bb = from_cc = lambda a: a.reshape(gh, n_state)
    first_h = lambda a: jnp.pad(a.reshape(n_groups, 1, n_state), ((0, 0), (0, grp - 1), (0, 0))).reshape(gh, n_state)
    over_h = lambda a: a.reshape(n_groups, grp, n_state).sum(axis=1)
    d_lr, d_li, d_dt_e, d_br, d_bi = _disc_bwd(
        lr, li, log_dt_e, br, bi, first_h(d_a_re), first_h(d_a_im), from_bb(d_bbm_re), from_bb(d_bbm_im))

    rep2d = dict(
        ssm_lam_re=(n_groups, n_state), ssm_lam_im=(n_groups, n_state), ssm_log_dt=(1, n_groups),
        ssm_b_re=(gh, n_state), ssm_b_im=(gh, n_state), ssm_c_re=(gh, n_state),
        ssm_c_im=(gh, n_state), ssm_d=(n_groups, grp), gain_conv_out=(1, dc),
        gain_ssm_out=(1, ds), norm_ffn_g=(1, d), ffn_conv_b=(1, N_DEV * ff_s), norm_final_g=(1, d))
    rep_names = list(rep2d)
    rep_grads = dict(
        ssm_lam_re=over_h(d_lr), ssm_lam_im=over_h(d_li), ssm_log_dt=over_h(d_dt_e).sum(axis=1),
        ssm_b_re=d_br, ssm_b_im=d_bi, ssm_c_re=from_cc(d_ccm_re), ssm_c_im=from_cc(d_ccm_im),
        ssm_d=d_dskip, gain_conv_out=d_gain_c, gain_ssm_out=d_gain_s, norm_ffn_g=d_norm_ffn,
        ffn_conv_b=d_fb, norm_final_g=d_norm_final)
    rep_shapes = [rep2d[n] for n in rep_names] + [(1, 1)]
    rep_rows = _packed_rows(rep_shapes)
    rep_pack = _pack([rep_grads[n] for n in rep_names] + [loss_part], rep_rows)
    ex_mix, token = _send_start("exchange_mix_start", [
        rep_pack, dw_out.reshape(N_DEV, -1, d), d_wglu.astype(BF16).reshape(N_DEV, -1, ds),
        d_conv_w.reshape(3, N_DEV, dc_s).transpose(1, 0, 2),
        d_fw.reshape(3, N_DEV, ff_s).transpose(1, 0, 2)],
        gather=[True, False, False, False, False])
    dcdv = _mix_bwd2(proj, dconv, conv_w_f, deps=(token,))
    dproj = jnp.concatenate([db_gate, dcdv, du], axis=1)
    d_in = N_DEV * d_in_s
    w_in_rows = g_in.transpose(1, 0, 2).reshape(d, d_in)
    dw_in = _mm("proj_wgrad", hn1, dproj, dims=TN, grid=(N_DEV,), a_spec=full_t(d),
                b_spec=pl.BlockSpec((t, d_in_s), lambda j: (0, j)),
                o_spec=pl.BlockSpec((None, d, d_in_s), lambda j: (j, 0, 0)),
                out_shape=jax.ShapeDtypeStruct((N_DEV, d, d_in_s), BF16))
    dh0, _, d_norm_mix = _dgrad_rms_bwd("proj_dgrad_norm_bwd", dh1, dproj, w_in_rows, NT, h0, norm_mix_g)
    dh0 = _from_segments(dh0)
    grad_x = dh0[n_meta:rows_used][None]
    d_meta_b = dh0[:n_meta].reshape(n_meta, N_DEV, d_meta).transpose(1, 0, 2)
    ex_in, ex_in_token = _send_start("exchange_in_start", [dw_in, d_meta_b, d_norm_mix],
                                     gather=[False, False, True])

    shard_out = {}

    def update(n, parts, transposed=False):
        sh = weights[n].shape
        two_d = lambda a: (jnp.swapaxes(a[0], 0, 1) if transposed else a.reshape(parts.shape[1:]))
        res = _adamw_reduce("adamw_" + n, parts, two_d(weights[n]), two_d(mom_m[n]), two_d(mom_v[n]))
        shard_out[n] = [(jnp.swapaxes(r, 0, 1) if transposed else r).reshape(sh) for r in res]
        return res[0]

    (p_down,) = _send_wait("exchange_down_wait", ex_down, (0,), ex_in_token)
    done = update("w_down", p_down)
    (p_up,) = _send_wait("exchange_up_wait", ex_up, (0,), done)
    done = update("w_up", p_up, transposed=True)
    (rep_parts,) = _send_wait("gather_small_grads_wait", ex_mix, (0,), done)
    rep_sum = _sum_parts("sum_small_grads", rep_parts)
    *rep_g, loss = _unpack(rep_sum, rep_shapes)
    loss = loss.reshape(())
    swapped = ("ssm_b_re", "ssm_b_im")
    to2d = lambda n, a: ghp(a) if n in swapped else a.reshape(rep2d[n])
    from2d = lambda n, a: (a.reshape(1, n_groups, grp, n_state).transpose(0, 1, 3, 2) if n in swapped
                           else a.reshape(weights[n].shape))
    as2d = lambda tree: [to2d(n, tree[n]) for n in rep_names]
    rep_res = _adamw_many("adamw_replicated", rep_g, as2d(weights), as2d(mom_m), as2d(mom_v))
    for i, n in enumerate(rep_names):
        shard_out[n] = [from2d(n, r) for r in (rep_g[i], *(res[i] for res in rep_res))]

    p_out, p_glu, p_cw, p_fw = _send_wait("exchange_mix_wait", ex_mix, (1, 2, 3, 4), rep_sum)
    update("w_out", p_out)
    update("ssm_w_glu", p_glu)
    update("conv_w", p_cw)
    done = update("ffn_conv_w", p_fw)
    p_in, p_meta, p_nm = _send_wait("exchange_in_wait", ex_in, (0, 1, 2), done)
    update("w_in", p_in)
    update("meta_tokens", p_meta)
    update("norm_mix_g", p_nm)

    grads = [shard_out[n][0] for n in names]
    deltas = [shard_out[n][1] for n in names]
    new_m = [shard_out[n][2] for n in names]
    new_v = [shard_out[n][3] for n in names]
    return (loss, grad_x, *grads, *deltas, *new_m, *new_v)
```

```python
import math

import jax
import jax.numpy as jnp
from jax import lax
from jax.experimental import pallas as pl
from jax.experimental.pallas import tpu as pltpu

F32 = jnp.float32
BF16 = jnp.bfloat16
MESH = pl.DeviceIdType.MESH

N_DEV = 8
RMS_EPS = 1e-6
ADAM_LR = 0.001
ADAM_B1 = 0.9
ADAM_B2 = 0.999
ADAM_EPS = 1e-08
ADAM_WD = 0.01
ADAM_STEP = 10
ADAM_BC1 = 1.0 - ADAM_B1 ** ADAM_STEP
ADAM_BC2 = 1.0 - ADAM_B2 ** ADAM_STEP

SUBLANES = 8
LANES = 128
ROW_TILE = 128
ROW_CHUNK = 32
WIDE_TILES = (544, 256, 128)
FFN_COLS = 256
STAGE_COLS = 256
N_SEG = 8
HALO_ROWS = 16
SSM_CHUNK = 128
SCAN_UNROLL = 4
VMEM_LIMIT = 48 * 1024 * 1024

NN = ((1,), (0,))
NT = ((1,), (1,))
TN = ((0,), (0,))


def _params(**kw):
    return pltpu.CompilerParams(vmem_limit_bytes=VMEM_LIMIT, **kw)


def _dot(a, b, dims):
    return lax.dot_general(a, b, (dims, ((), ())), preferred_element_type=F32)


def _mean_sq_rsqrt(x):
    return lax.rsqrt(jnp.mean(x * x, axis=-1, keepdims=True) + RMS_EPS)


def _rms_bwd(x, r, g, dy):
    xhat = x * r
    dxh = dy * g
    dx = r * (dxh - xhat * jnp.mean(dxh * xhat, axis=-1, keepdims=True))
    return dx, dy * xhat


def _gelu(y):
    c = math.sqrt(2.0 / math.pi)
    t = jnp.tanh(c * (y + 0.044715 * y * y * y))
    return 0.5 * y * (1.0 + t), t


def _gelu_grad(y, t):
    c = math.sqrt(2.0 / math.pi)
    return 0.5 * (1.0 + t) + 0.5 * y * (1.0 - t * t) * c * (1.0 + 3.0 * 0.044715 * y * y)


def _wrap_prev_halo(halo, first):
    seg = lax.broadcasted_iota(jnp.int32, halo.shape, 0) % N_SEG
    wrapped = jnp.where(seg == 0, 0.0, pltpu.roll(halo, 1, 0))
    return jnp.where(first, wrapped, halo)


def _wrap_next_halo(halo, last):
    seg = lax.broadcasted_iota(jnp.int32, halo.shape, 0) % N_SEG
    wrapped = jnp.where(seg == N_SEG - 1, 0.0, pltpu.roll(halo, halo.shape[0] - 1, 0))
    return jnp.where(last, wrapped, halo)


def _dev_index(p):
    return 4 * p[0] + 2 * p[1] + p[2]


def _allgather(name, shards, deps=()):
    n = len(shards)

    def body(*refs):
        ins, outs = refs[:n], refs[n:2 * n]
        send_sems, recv_sems, local_sems = refs[2 * n:]
        x, y, c = lax.axis_index("x"), lax.axis_index("y"), lax.axis_index("c")
        me, sibling = (x, y, c), (x, y, 1 - c)
        chips = [(1 - x, y), (x, 1 - y), (1 - x, 1 - y)]

        def copy(a, k, block, to, src=None):
            dst = outs[a].at[_dev_index(block)]
            return pltpu.make_async_remote_copy(
                src_ref=dst if src is None else src, dst_ref=dst,
                send_sem=send_sems.at[a, k], recv_sem=recv_sems.at[a, k],
                device_id=to, device_id_type=MESH)

        mine = [pltpu.make_async_copy(ins[a], outs[a].at[_dev_index(me)], local_sems.at[a])
                for a in range(n)]
        for cp in mine:
            cp.start()
        first = []
        for a in range(n):
            first.append(copy(a, 0, me, sibling, src=ins[a]))
            for j, chip in enumerate(chips):
                first.append(copy(a, 1 + j, me, (*chip, c), src=ins[a]))
        for cp in first:
            cp.start()
        passed = []
        for j, chip in enumerate(chips):
            for a in range(n):
                copy(a, 1 + j, (*chip, c), me).wait_recv()
                fwd = copy(a, 4 + j, (*chip, c), sibling)
                fwd.start()
                passed.append(fwd)
        for a in range(n):
            copy(a, 0, sibling, me).wait_recv()
            for j, chip in enumerate(chips):
                copy(a, 4 + j, (*chip, 1 - c), me).wait_recv()
        for cp in first + passed:
            cp.wait_send()
        for cp in mine:
            cp.wait()

    any_spec = pl.BlockSpec(memory_space=pl.ANY)
    return pl.pallas_call(
        _ignoring_deps(body, n, deps), name=name,
        out_shape=[jax.ShapeDtypeStruct((N_DEV,) + s.shape, s.dtype) for s in shards],
        in_specs=[any_spec] * (n + len(deps)), out_specs=[any_spec] * n,
        scratch_shapes=[pltpu.SemaphoreType.DMA((n, 7)), pltpu.SemaphoreType.DMA((n, 7)),
                        pltpu.SemaphoreType.DMA((n,))],
    )(*shards, *deps)


HBM_SPEC = pl.BlockSpec(memory_space=pltpu.HBM)
SEM_SPEC = pl.BlockSpec(memory_space=pltpu.SEMAPHORE)
ANY_SPEC = pl.BlockSpec(memory_space=pl.ANY)
DATAFLOW = pltpu.SideEffectType.DATAFLOW_SIDE_EFFECTING


def _ignoring_deps(body, n_in, deps):
    n_dep = len(deps)

    def wrapped(*refs):
        return body(*refs[:n_in], *refs[n_in + n_dep:])

    return wrapped


def _my_position():
    x, y, c = lax.axis_index("x"), lax.axis_index("y"), lax.axis_index("c")
    return (x, y, c)


def _peer(me, k):
    return tuple((1 - v) if (k >> s) & 1 else v for v, s in zip(me, (2, 1, 0)))


def _split_copies(src_refs, land_refs, send_sems, recv_sems, gather):
    me = _my_position()
    copies = []
    for a, (src, land) in enumerate(zip(src_refs, land_refs)):
        for k in range(1, N_DEV):
            peer = _peer(me, k)
            copies.append(pltpu.make_async_remote_copy(
                src_ref=src if gather[a] else src.at[_dev_index(peer)], dst_ref=land.at[_dev_index(me)],
                send_sem=send_sems[a].at[k - 1], recv_sem=recv_sems[a].at[k - 1],
                device_id=peer, device_id_type=MESH))
    return copies


def _own_slot(block, like_shape):
    me = _dev_index(_my_position())
    return lax.dynamic_update_index_in_dim(lax.empty(like_shape, block.dtype), block, me, 0)


def _send_start(name, srcs, gather):
    n = len(srcs)
    me = _dev_index(_my_position())
    gather = [gather] * n if isinstance(gather, bool) else list(gather)
    lands = [_own_slot(s, (N_DEV,) + s.shape) if g else
             _own_slot(lax.dynamic_index_in_dim(s, me, 0, keepdims=False), s.shape)
             for s, g in zip(srcs, gather)]

    def body(*refs):
        src_refs, land_refs = refs[:n], refs[n:2 * n]
        send_sems, recv_sems = refs[2 * n:3 * n], refs[3 * n:4 * n]
        token = refs[-1]
        for cp in _split_copies(src_refs, land_refs, send_sems, recv_sems, gather):
            cp.start()
        token[...] = jnp.zeros_like(token)

    hbm = lambda a: pltpu.HBM(a.shape, a.dtype)
    sems = [pltpu.SemaphoreType.DMA((N_DEV - 1,))] * n
    outs = pl.pallas_call(
        body, name=name,
        out_shape=(*sems, *sems, *[hbm(s) for s in srcs], *[hbm(l) for l in lands],
                   jax.ShapeDtypeStruct((SUBLANES, LANES), F32)),
        in_specs=[HBM_SPEC] * (2 * n),
        out_specs=(*[SEM_SPEC] * (2 * n), *[HBM_SPEC] * (2 * n), pl.BlockSpec(memory_space=pltpu.VMEM)),
        input_output_aliases={i: 2 * n + i for i in range(2 * n)},
        compiler_params=pltpu.CompilerParams(has_side_effects=DATAFLOW),
    )(*[pltpu.with_memory_space_constraint(a, pltpu.HBM) for a in (*srcs, *lands)])
    state = dict(send=outs[:n], recv=outs[n:2 * n], srcs=outs[2 * n:3 * n], lands=outs[3 * n:4 * n],
                 gather=gather)
    return state, outs[-1]


def _send_wait(name, state, which, after):
    n = len(which)
    pick = lambda key: [state[key][i] for i in which]
    gather = pick("gather")

    def body(*refs):
        src_refs, land_refs = refs[:n], refs[n:2 * n]
        send_sems, recv_sems = refs[2 * n:3 * n], refs[3 * n:4 * n]
        for cp in _split_copies(src_refs, land_refs, send_sems, recv_sems, gather):
            cp.wait_send()
            cp.wait_recv()

    srcs, lands = pick("srcs"), pick("lands")
    hbm = lambda a: pltpu.HBM(a.shape, a.dtype)
    outs = pl.pallas_call(
        body, name=name,
        out_shape=(*[hbm(s) for s in srcs], *[hbm(l) for l in lands]),
        in_specs=[*[HBM_SPEC] * (2 * n), *[SEM_SPEC] * (2 * n), ANY_SPEC],
        out_specs=tuple([HBM_SPEC] * (2 * n)),
        input_output_aliases={i: i for i in range(2 * n)},
        compiler_params=pltpu.CompilerParams(has_side_effects=DATAFLOW),
    )(*srcs, *lands, *pick("send"), *pick("recv"), after)
    return outs[n:]


def _two_level_copies(src_refs, land_refs, sems1, sems2):
    x, y, c = _my_position()
    me, sibling = (x, y, c), (x, y, 1 - c)
    chips = [(1 - x, y), (x, 1 - y), (1 - x, 1 - y)]
    stage1, stage2 = [], []
    for a, land in enumerate(land_refs):
        def copy(src, block, to, sems, k):
            return pltpu.make_async_remote_copy(
                src_ref=src, dst_ref=land.at[_dev_index(block)], send_sem=sems[0][a].at[k],
                recv_sem=sems[1][a].at[k], device_id=to, device_id_type=MESH)
        src = src_refs[a] if src_refs is not None else land.at[_dev_index(me)]
        stage1.append([copy(src, me, sibling, sems1, 0)] +
                      [copy(src, me, (*chip, c), sems1, 1 + j) for j, chip in enumerate(chips)])
        if sems2 is not None:
            stage2.append([copy(land.at[_dev_index((*chip, c))], (*chip, c), sibling, sems2, j)
                           for j, chip in enumerate(chips)])
    return stage1, stage2


def _gather2_start(name, shards):
    n = len(shards)
    lands = [_own_slot(s, (N_DEV,) + s.shape) for s in shards]

    def body(*refs):
        src_refs, land_refs = refs[:n], refs[n:2 * n]
        sems1 = (refs[2 * n:3 * n], refs[3 * n:4 * n])
        stage1, _ = _two_level_copies(src_refs, land_refs, sems1, None)
        for copies in stage1:
            for cp in copies:
                cp.start()
        refs[-1][...] = jnp.zeros_like(refs[-1])

    hbm = lambda a: pltpu.HBM(a.shape, a.dtype)
    sems = [pltpu.SemaphoreType.DMA((4,))] * n
    outs = pl.pallas_call(
        body, name=name,
        out_shape=(*sems, *sems, *[hbm(s) for s in shards], *[hbm(l) for l in lands],
                   jax.ShapeDtypeStruct((SUBLANES, LANES), F32)),
        in_specs=[HBM_SPEC] * (2 * n),
        out_specs=(*[SEM_SPEC] * (2 * n), *[HBM_SPEC] * (2 * n), pl.BlockSpec(memory_space=pltpu.VMEM)),
        input_output_aliases={i: 2 * n + i for i in range(2 * n)},
        compiler_params=pltpu.CompilerParams(has_side_effects=DATAFLOW),
    )(*[pltpu.with_memory_space_constraint(a, pltpu.HBM) for a in (*shards, *lands)])
    state = dict(send1=list(outs[:n]), recv1=list(outs[n:2 * n]), srcs=list(outs[2 * n:3 * n]),
                 lands=list(outs[3 * n:4 * n]), send2={}, recv2={})
    return state, outs[-1]


def _gather2_forward(name, state, which, after):
    n = len(which)
    pick = lambda key: [state[key][i] for i in which]

    def body(*refs):
        land_refs, recv1 = refs[:n], refs[n:2 * n]
        outs = refs[2 * n + len(after):]
        sems2 = (outs[:n], outs[n:2 * n])
        stage1, stage2 = _two_level_copies(None, land_refs, (recv1, recv1), sems2)
        for a in range(n):
            for j in range(3):
                stage1[a][1 + j].wait_recv()
                stage2[a][j].start()
        outs[-1][...] = jnp.zeros_like(outs[-1])

    lands = pick("lands")
    sems = [pltpu.SemaphoreType.DMA((3,))] * n
    outs = pl.pallas_call(
        body, name=name,
        out_shape=(*sems, *sems, *[pltpu.HBM(l.shape, l.dtype) for l in lands],
                   jax.ShapeDtypeStruct((SUBLANES, LANES), F32)),
        in_specs=[*[HBM_SPEC] * n, *[SEM_SPEC] * n, *[ANY_SPEC] * len(after)],
        out_specs=(*[SEM_SPEC] * (2 * n), *[HBM_SPEC] * n, pl.BlockSpec(memory_space=pltpu.VMEM)),
        input_output_aliases={i: 2 * n + i for i in range(n)},
        compiler_params=pltpu.CompilerParams(has_side_effects=DATAFLOW),
    )(*lands, *pick("recv1"), *after)
    for idx, i in enumerate(which):
        state["send2"][i], state["recv2"][i] = outs[idx], outs[n + idx]
        state["lands"][i] = outs[2 * n + idx]
    return outs[-1]


def _gather2_wait(name, state, which, after):
    n = len(which)
    pick = lambda key: [state[key][i] for i in which]

    def body(*refs):
        src_refs, land_refs = refs[:n], refs[n:2 * n]
        sems1 = (refs[2 * n:3 * n], refs[3 * n:4 * n])
        sems2 = (refs[4 * n:5 * n], refs[5 * n:6 * n])
        stage1, stage2 = _two_level_copies(src_refs, land_refs, sems1, sems2)
        for a in range(n):
            for cp in stage1[a]:
                cp.wait_send()
            stage1[a][0].wait_recv()
            for cp in stage2[a]:
                cp.wait_send()
                cp.wait_recv()

    srcs, lands = pick("srcs"), pick("lands")
    hbm = lambda a: pltpu.HBM(a.shape, a.dtype)
    outs = pl.pallas_call(
        body, name=name,
        out_shape=(*[hbm(s) for s in srcs], *[hbm(l) for l in lands]),
        in_specs=[*[HBM_SPEC] * (2 * n), *[SEM_SPEC] * (4 * n), ANY_SPEC],
        out_specs=tuple([HBM_SPEC] * (2 * n)),
        input_output_aliases={i: i for i in range(2 * n)},
        compiler_params=pltpu.CompilerParams(has_side_effects=DATAFLOW),
    )(*srcs, *lands, *pick("send1"), *pick("recv1"), *pick("send2"), *pick("recv2"), after)
    return outs[n:]


def _mm(name, a, b, *, dims, grid, a_spec, b_spec, o_spec, out_shape, acc_shape=None,
        res=None, res_spec=None):
    n_red = grid[-1] if acc_shape is not None else 1
    red_axis = len(grid) - 1

    def body(*refs):
        a_ref, b_ref = refs[0], refs[1]
        r_ref = refs[2] if res is not None else None
        o_ref = refs[3] if res is not None else refs[2]
        part = _dot(a_ref[...], b_ref[...], dims)
        if acc_shape is None:
            if r_ref is not None:
                part = part + r_ref[...]
            o_ref[...] = part.astype(o_ref.dtype)
            return
        acc_ref = refs[-1]
        k = pl.program_id(red_axis)

        @pl.when(k == 0)
        def _():
            acc_ref[...] = part

        @pl.when(k > 0)
        def _():
            acc_ref[...] += part

        @pl.when(k == n_red - 1)
        def _():
            total = acc_ref[...]
            if r_ref is not None:
                total = total + r_ref[...]
            o_ref[...] = total.astype(o_ref.dtype)

    ins, in_specs = [a, b], [a_spec, b_spec]
    if res is not None:
        ins.append(res)
        in_specs.append(res_spec)
    return pl.pallas_call(
        body, name=name, grid=grid, in_specs=in_specs, out_specs=o_spec, out_shape=out_shape,
        scratch_shapes=[pltpu.VMEM(acc_shape, F32)] if acc_shape is not None else [],
        compiler_params=_params(),
    )(*ins)


def _wide_tile(t):
    return next(c for c in WIDE_TILES if t % c == 0)


def _rms_fwd(name, h, g):
    t, d = h.shape
    tile = _wide_tile(t)

    def body(h_ref, g_ref, o_ref):
        x = h_ref[...]
        o_ref[...] = (x * _mean_sq_rsqrt(x) * g_ref[...]).astype(BF16)

    return pl.pallas_call(
        body, name=name, grid=(t // tile,),
        in_specs=[pl.BlockSpec((tile, d), lambda i: (i, 0)), pl.BlockSpec((1, d), lambda i: (0, 0))],
        out_specs=pl.BlockSpec((tile, d), lambda i: (i, 0)),
        out_shape=jax.ShapeDtypeStruct((t, d), BF16), compiler_params=_params(),
    )(h, g)


def _fused_tile(t):
    half = _wide_tile(t) // 2
    return half if half % 16 == 0 and t % half == 0 else _wide_tile(t)


def _dgrad_rms_bwd(name, dres, dy, w, dims, h, g, deps=()):
    t, d = h.shape
    k = dy.shape[1]

    def body(dres_ref, dy_ref, w_ref, h_ref, g_ref, dh_ref, dhb_ref, dg_ref):
        x = h_ref[...]
        dhn = _dot(dy_ref[...], w_ref[...], dims)
        dx, dgt = _rms_bwd(x, _mean_sq_rsqrt(x), g_ref[...], dhn)
        dh = dres_ref[...] + dx
        dh_ref[...] = dh
        dhb_ref[...] = dh.astype(BF16)

        @pl.when(pl.program_id(0) == 0)
        def _():
            dg_ref[...] = jnp.zeros_like(dg_ref)

        dg_ref[...] += jnp.sum(dgt, axis=0, keepdims=True)

    tile = _fused_tile(t)
    row = pl.BlockSpec((tile, d), lambda i: (i, 0))
    vec = pl.BlockSpec((1, d), lambda i: (0, 0))
    return pl.pallas_call(
        _ignoring_deps(body, 5, deps), name=name, grid=(t // tile,),
        in_specs=[row, pl.BlockSpec((tile, k), lambda i: (i, 0)), pl.BlockSpec(w.shape, lambda i: (0, 0)), row,
                  vec] + [ANY_SPEC] * len(deps),
        out_specs=[row, row, vec],
        out_shape=[jax.ShapeDtypeStruct((t, d), F32), jax.ShapeDtypeStruct((t, d), BF16),
                   jax.ShapeDtypeStruct((1, d), F32)],
        compiler_params=_params(),
    )(dres, dy, w, h, g, *deps)


def _down_loss_bwd(act, w_down, h1, target, g, row_lo, row_hi):
    t, d = h1.shape
    f = act.shape[1]
    tile = _fused_tile(t)

    def body(act_ref, w_ref, h_ref, tg_ref, g_ref, dh_ref, dhb_ref, loss_ref, dg_ref):
        i = pl.program_id(0)
        x = h_ref[...] + _dot(act_ref[...], w_ref[...], NN)
        r = _mean_sq_rsqrt(x)
        gv = g_ref[...]
        y = x * r * gv
        rows = i * tile + lax.broadcasted_iota(jnp.int32, (tile, 1), 0)
        time = (rows % N_SEG) * (t // N_SEG) + rows // N_SEG
        valid = jnp.logical_and(time >= row_lo, time < row_hi)
        err = jnp.where(valid, y - tg_ref[...], 0.0)
        dy = err * (1.0 / d)
        dx, dgt = _rms_bwd(x, r, gv, dy)
        dh_ref[...] = dx
        dhb_ref[...] = dx.astype(BF16)

        @pl.when(i == 0)
        def _():
            loss_ref[...] = jnp.zeros_like(loss_ref)
            dg_ref[...] = jnp.zeros_like(dg_ref)

        row_loss = jnp.mean(err * err, axis=-1, keepdims=True)
        loss_ref[...] += 0.5 * jnp.sum(row_loss, axis=0, keepdims=True)
        dg_ref[...] += jnp.sum(dgt, axis=0, keepdims=True)

    row = pl.BlockSpec((tile, d), lambda i: (i, 0))
    vec = pl.BlockSpec((1, d), lambda i: (0, 0))
    return pl.pallas_call(
        body, name="down_proj_loss_bwd", grid=(t // tile,),
        in_specs=[pl.BlockSpec((tile, f), lambda i: (i, 0)), pl.BlockSpec(w_down.shape, lambda i: (0, 0)),
                  row, row, vec],
        out_specs=[row, row, pl.BlockSpec((1, 1), lambda i: (0, 0)), vec],
        out_shape=[jax.ShapeDtypeStruct((t, d), F32), jax.ShapeDtypeStruct((t, d), BF16),
                   jax.ShapeDtypeStruct((1, 1), F32), jax.ShapeDtypeStruct((1, d), F32)],
        compiler_params=_params(),
    )(act, w_down, h1, target, g)


def _prev_halo(i, t):
    return jnp.where(i == 0, t // HALO_ROWS - 1, i * (ROW_TILE // HALO_ROWS) - 1)


def _next_halo(i, t):
    return jnp.where(i == t // ROW_TILE - 1, 0, (i + 1) * (ROW_TILE // HALO_ROWS))


def _causal_taps(cur, halo, first):
    rows = cur.shape[0]
    ext = jnp.concatenate([_wrap_prev_halo(halo, first), cur], axis=0)
    return ext[:rows], ext[N_SEG:N_SEG + rows]


def _anticausal_taps(cur, halo, last):
    rows = cur.shape[0]
    ext = jnp.concatenate([cur, _wrap_next_halo(halo, last)], axis=0)
    return ext[N_SEG:N_SEG + rows], ext[2 * N_SEG:2 * N_SEG + rows]


def _mix_fwd(proj, y, w_glu, conv_w, gain_c, gain_s, h0, w_out, gain_ffn):
    t, d = h0.shape
    dc = conv_w.shape[1]
    ds = y.shape[1]

    def body(p_ref, halo_ref, y_ref, wg_ref, cw_ref, gc_ref, gs_ref, h0_ref, wo_ref, gf_ref,
             mixed_ref, z_ref, h1_ref, hn_ref):
        i = pl.program_id(0)
        p = p_ref[...]
        b, c, v = p[:, :dc], p[:, dc:2 * dc], p[:, 2 * dc:3 * dc]
        cv = c * v
        hp = halo_ref[...]
        x2, x1 = _causal_taps(cv, hp[:, dc:2 * dc] * hp[:, 2 * dc:3 * dc], i == 0)
        cw = cw_ref[...]
        conv = cw[0:1] * x2 + cw[1:2] * x1 + cw[2:3] * cv
        co = b * conv
        mixed_ref[:, :dc] = (co * _mean_sq_rsqrt(co) * gc_ref[...]).astype(BF16)
        g, _ = _gelu(y_ref[...])
        z = _dot(g.astype(BF16), wg_ref[...], NN)
        z_ref[...] = z
        so = g * jax.nn.sigmoid(z)
        mixed_ref[:, dc:] = (so * _mean_sq_rsqrt(so) * gs_ref[...]).astype(BF16)
        h1 = h0_ref[...] + _dot(mixed_ref[...], wo_ref[...], NN)
        h1_ref[...] = h1
        hn_ref[...] = (h1 * _mean_sq_rsqrt(h1) * gf_ref[...]).astype(BF16)

    const = lambda i: (0, 0)
    row = lambda w: pl.BlockSpec((ROW_TILE, w), lambda i: (i, 0))
    return pl.pallas_call(
        body, name="mix_fwd_out_proj", grid=(t // ROW_TILE,),
        in_specs=[row(3 * dc), pl.BlockSpec((HALO_ROWS, 3 * dc), lambda i: (_prev_halo(i, t), 0)), row(ds),
                  pl.BlockSpec((ds, ds), const), pl.BlockSpec(conv_w.shape, const),
                  pl.BlockSpec((1, dc), const), pl.BlockSpec((1, ds), const),
                  row(d), pl.BlockSpec(w_out.shape, const), pl.BlockSpec((1, d), const)],
        out_specs=[row(dc + ds), row(ds), row(d), row(d)],
        out_shape=[jax.ShapeDtypeStruct((t, dc + ds), BF16), jax.ShapeDtypeStruct((t, ds), F32),
                   jax.ShapeDtypeStruct((t, d), F32), jax.ShapeDtypeStruct((t, d), BF16)],
        compiler_params=_params(),
    )(proj, proj, y, w_glu, conv_w, gain_c, gain_s, h0, w_out, gain_ffn)


def _mix_bwd1(proj, y, z, dmixed, w_glu, conv_w, gain_c, gain_s):
    t = proj.shape[0]
    dc = conv_w.shape[1]
    ds = y.shape[1]

    def body(p_ref, halo_ref, y_ref, z_ref, dm_ref, wg_ref, cw_ref, gc_ref, gs_ref,
             db_ref, dconv_ref, dy_ref, dwg_ref, dcw_ref, dgc_ref, dgs_ref):
        i = pl.program_id(0)

        @pl.when(i == 0)
        def _():
            dwg_ref[...] = jnp.zeros_like(dwg_ref)
            dcw_ref[...] = jnp.zeros_like(dcw_ref)
            dgc_ref[...] = jnp.zeros_like(dgc_ref)
            dgs_ref[...] = jnp.zeros_like(dgs_ref)

        p = p_ref[...]
        b, c, v = p[:, :dc], p[:, dc:2 * dc], p[:, 2 * dc:3 * dc]
        cv = c * v
        hp = halo_ref[...]
        x2, x1 = _causal_taps(cv, hp[:, dc:2 * dc] * hp[:, 2 * dc:3 * dc], i == 0)
        cw = cw_ref[...]
        conv = cw[0:1] * x2 + cw[1:2] * x1 + cw[2:3] * cv
        co = b * conv
        dm = dm_ref[...]
        dco, dgt = _rms_bwd(co, _mean_sq_rsqrt(co), gc_ref[...], dm[:, :dc])
        dgc_ref[...] += jnp.sum(dgt, axis=0, keepdims=True)
        db_ref[...] = (dco * conv).astype(BF16)
        dconv = dco * b
        dconv_ref[...] = dconv
        dcw_ref[0:1, :] += jnp.sum(dconv * x2, axis=0, keepdims=True)
        dcw_ref[1:2, :] += jnp.sum(dconv * x1, axis=0, keepdims=True)
        dcw_ref[2:3, :] += jnp.sum(dconv * cv, axis=0, keepdims=True)

        yv = y_ref[...]
        g, th = _gelu(yv)
        sg = jax.nn.sigmoid(z_ref[...])
        so = g * sg
        dso, dgt = _rms_bwd(so, _mean_sq_rsqrt(so), gs_ref[...], dm[:, dc:])
        dgs_ref[...] += jnp.sum(dgt, axis=0, keepdims=True)
        dz = (dso * g * sg * (1.0 - sg)).astype(BF16)
        dg = dso * sg + _dot(dz, wg_ref[...], NT)
        dwg_ref[...] += _dot(g.astype(BF16), dz, TN)
        dy_ref[...] = dg * _gelu_grad(yv, th)

    const = lambda i: (0, 0)
    row = lambda w: pl.BlockSpec((ROW_TILE, w), lambda i: (i, 0))
    return pl.pallas_call(
        body, name="mix_bwd1", grid=(t // ROW_TILE,),
        in_specs=[row(3 * dc), pl.BlockSpec((HALO_ROWS, 3 * dc), lambda i: (_prev_halo(i, t), 0)),
                  row(ds), row(ds), row(dc + ds), pl.BlockSpec((ds, ds), const),
                  pl.BlockSpec(conv_w.shape, const), pl.BlockSpec((1, dc), const),
                  pl.BlockSpec((1, ds), const)],
        out_specs=[row(dc), row(dc), row(ds), pl.BlockSpec((ds, ds), const),
                   pl.BlockSpec(conv_w.shape, const), pl.BlockSpec((1, dc), const),
                   pl.BlockSpec((1, ds), const)],
        out_shape=[jax.ShapeDtypeStruct((t, dc), BF16), jax.ShapeDtypeStruct((t, dc), F32),
                   jax.ShapeDtypeStruct((t, ds), F32), jax.ShapeDtypeStruct((ds, ds), F32),
                   jax.ShapeDtypeStruct(conv_w.shape, F32), jax.ShapeDtypeStruct((1, dc), F32),
                   jax.ShapeDtypeStruct((1, ds), F32)],
        compiler_params=_params(),
    )(proj, proj, y, z, dmixed, w_glu, conv_w, gain_c, gain_s)


def _mix_bwd2(proj, dconv, conv_w, deps=()):
    t = proj.shape[0]
    dc = conv_w.shape[1]
    n_tiles = t // ROW_TILE

    def body(c_ref, v_ref, d_ref, halo_ref, cw_ref, o_ref):
        i = pl.program_id(0)
        d = d_ref[...]
        u1, u2 = _anticausal_taps(d, halo_ref[...], i == n_tiles - 1)
        cw = cw_ref[...]
        dcv = cw[2:3] * d + cw[1:2] * u1 + cw[0:1] * u2
        o_ref[:, :dc] = (dcv * v_ref[...]).astype(BF16)
        o_ref[:, dc:] = (dcv * c_ref[...]).astype(BF16)

    return pl.pallas_call(
        _ignoring_deps(body, 5, deps), name="mix_bwd2", grid=(n_tiles,),
        in_specs=[pl.BlockSpec((ROW_TILE, dc), lambda i: (i, 1)),
                  pl.BlockSpec((ROW_TILE, dc), lambda i: (i, 2)),
                  pl.BlockSpec((ROW_TILE, dc), lambda i: (i, 0)),
                  pl.BlockSpec((HALO_ROWS, dc), lambda i: (_next_halo(i, t), 0)),
                  pl.BlockSpec(conv_w.shape, lambda i: (0, 0))] + [ANY_SPEC] * len(deps),
        out_specs=pl.BlockSpec((ROW_TILE, 2 * dc), lambda i: (i, 0)),
        out_shape=jax.ShapeDtypeStruct((t, 2 * dc), BF16), compiler_params=_params(),
    )(proj, proj, dconv, dconv, conv_w, *deps)


def _stage_rows(stage_ref, row0, src, wrap=None):
    n, width = src.shape
    for c in range(0, width, STAGE_COLS):
        v = src[:, pl.ds(c, STAGE_COLS)].astype(F32)
        stage_ref[pl.ds(row0, n), pl.ds(c, STAGE_COLS)] = v if wrap is None else wrap(v)


def _conv_taps(stage_ref, fw_ref, fb_ref, c0, r, rows=ROW_CHUNK):
    cols = pl.ds(c0, FFN_COLS)
    x0, x1, x2 = (stage_ref[pl.ds(HALO_ROWS + r - k * N_SEG, rows), cols] for k in range(3))
    w = fw_ref[:, cols]
    return w[0:1] * x2 + w[1:2] * x1 + w[2:3] * x0 + fb_ref[:, cols], x2, x1, x0


def _gated_fwd(up_pre, fw, fb):
    t, f2 = up_pre.shape
    f = f2 // 2

    def body(up_ref, halo_ref, fw_ref, fb_ref, act_ref, stage_ref):
        first = pl.program_id(0) == 0
        _stage_rows(stage_ref, 0, halo_ref, lambda v: _wrap_prev_halo(v, first))
        _stage_rows(stage_ref, HALO_ROWS, up_ref)
        for c0 in range(0, f, FFN_COLS):
            for r in range(0, ROW_TILE, ROW_CHUNK):
                a = _conv_taps(stage_ref, fw_ref, fb_ref, c0, r)[0]
                val = _conv_taps(stage_ref, fw_ref, fb_ref, f + c0, r)[0]
                act_ref[pl.ds(r, ROW_CHUNK), pl.ds(c0, FFN_COLS)] = (a * jax.nn.sigmoid(a) * val).astype(BF16)

    const = lambda a: pl.BlockSpec(a.shape, lambda i: (0, 0))
    return pl.pallas_call(
        body, name="ffn_fwd", grid=(t // ROW_TILE,),
        in_specs=[pl.BlockSpec((ROW_TILE, f2), lambda i: (i, 0)),
                  pl.BlockSpec((HALO_ROWS, f2), lambda i: (_prev_halo(i, t), 0)), const(fw), const(fb)],
        out_specs=pl.BlockSpec((ROW_TILE, f), lambda i: (i, 0)),
        out_shape=jax.ShapeDtypeStruct((t, f), BF16),
        scratch_shapes=[pltpu.VMEM((HALO_ROWS + ROW_TILE, f2), F32)], compiler_params=_params(),
    )(up_pre, up_pre, fw, fb)


def _gated_bwd(up_pre, dact, fw, fb, deps=()):
    t, f2 = up_pre.shape
    f = f2 // 2
    n_tiles = t // ROW_TILE
    chunks = [(r, ROW_CHUNK) for r in range(0, ROW_TILE, ROW_CHUNK)] + [(ROW_TILE, HALO_ROWS)]

    def body(up_ref, prev_ref, next_ref, dact_in_ref, dact_next_ref, fw_ref, fb_ref,
             out_ref, dfw_ref, dfb_ref, dup_ref, stage_ref, dact_ref):
        first = pl.program_id(0) == 0
        last = pl.program_id(0) == n_tiles - 1

        @pl.when(first)
        def _():
            dfw_ref[...] = jnp.zeros_like(dfw_ref)
            dfb_ref[...] = jnp.zeros_like(dfb_ref)

        _stage_rows(dact_ref, 0, dact_in_ref)
        _stage_rows(dact_ref, ROW_TILE, dact_next_ref, lambda v: _wrap_next_halo(v, last))
        _stage_rows(stage_ref, 0, prev_ref, lambda v: _wrap_prev_halo(v, first))
        _stage_rows(stage_ref, HALO_ROWS, up_ref)
        _stage_rows(stage_ref, HALO_ROWS + ROW_TILE, next_ref, lambda v: _wrap_next_halo(v, last))
        fold = lambda v: sum(v[s:s + SUBLANES] for s in range(0, ROW_CHUNK, SUBLANES))
        for c0 in range(0, f, FFN_COLS):
            starts = (c0, f + c0)
            sums = [[jnp.zeros((SUBLANES, FFN_COLS), F32)] * 4 for _ in starts]
            for r, rows in chunks:
                taps = [_conv_taps(stage_ref, fw_ref, fb_ref, c, r, rows) for c in starts]
                a, val = taps[0][0], taps[1][0]
                da_ct = dact_ref[pl.ds(r, rows), pl.ds(c0, FFN_COLS)]
                sg = jax.nn.sigmoid(a)
                dup = (da_ct * val * sg * (1.0 + a * (1.0 - sg)), da_ct * a * sg)
                for k in range(2):
                    dup_ref[k, pl.ds(r, rows), :] = dup[k]
                    if r < ROW_TILE:
                        terms = (dup[k], dup[k] * taps[k][1], dup[k] * taps[k][2], dup[k] * taps[k][3])
                        sums[k] = [s + fold(v) for s, v in zip(sums[k], terms)]
            for k, c in enumerate(starts):
                cols = pl.ds(c, FFN_COLS)
                s_b, s_w0, s_w1, s_w2 = (jnp.sum(p, axis=0, keepdims=True) for p in sums[k])
                dfb_ref[:, cols] += s_b
                for tap, s_w in enumerate((s_w0, s_w1, s_w2)):
                    dfw_ref[tap:tap + 1, cols] += s_w
                cw = fw_ref[:, cols]
                for r in range(0, ROW_TILE, ROW_CHUNK):
                    d, u1, u2 = (dup_ref[k, pl.ds(r + s * N_SEG, ROW_CHUNK), :] for s in range(3))
                    out_ref[pl.ds(r, ROW_CHUNK), cols] = (cw[2:3] * d + cw[1:2] * u1 + cw[0:1] * u2).astype(BF16)

    tile = lambda w: pl.BlockSpec((ROW_TILE, w), lambda i: (i, 0))
    halo = lambda w, index: pl.BlockSpec((HALO_ROWS, w), lambda i: (index(i, t), 0))
    const = lambda a: pl.BlockSpec(a.shape, lambda i: (0, 0))
    return pl.pallas_call(
        _ignoring_deps(body, 7, deps), name="ffn_bwd", grid=(n_tiles,),
        in_specs=[tile(f2), halo(f2, _prev_halo), halo(f2, _next_halo), tile(f), halo(f, _next_halo),
                  const(fw), const(fb)] + [ANY_SPEC] * len(deps),
        out_specs=[tile(f2), const(fw), const(fb)],
        out_shape=[jax.ShapeDtypeStruct((t, f2), BF16), jax.ShapeDtypeStruct(fw.shape, F32),
                   jax.ShapeDtypeStruct(fb.shape, F32)],
        scratch_shapes=[pltpu.VMEM((2, ROW_TILE + HALO_ROWS, FFN_COLS), F32),
                        pltpu.VMEM((HALO_ROWS + ROW_TILE + HALO_ROWS, f2), F32),
                        pltpu.VMEM((ROW_TILE + HALO_ROWS, f), F32)],
        compiler_params=_params(),
    )(up_pre, up_pre, up_pre, dact, dact, fw, fb, *deps)


def _to_segments(a):
    t, c = a.shape
    return a.reshape(N_SEG, t // N_SEG, c).transpose(1, 0, 2).reshape(t, c)


def _from_segments(a):
    t, c = a.shape
    return a.reshape(t // N_SEG, N_SEG, c).transpose(1, 0, 2).reshape(t, c)


def _cmul(ar, ai, br, bi):
    return ar * br - ai * bi, ar * bi + ai * br


def _cpow(ar, ai, n):
    out = None
    while n:
        if n & 1:
            out = (ar, ai) if out is None else _cmul(out[0], out[1], ar, ai)
        ar, ai = _cmul(ar, ai, ar, ai)
        n >>= 1
    return out


def _segment_carries(pr, pi, fr, fi, forward):
    row = lax.broadcasted_iota(jnp.int32, fr.shape, 0)
    edge = row == (0 if forward else N_SEG - 1)
    shift = 1 if forward else N_SEG - 1
    sr, si = jnp.zeros_like(fr), jnp.zeros_like(fi)
    for _ in range(N_SEG - 1):
        tr, ti = _cmul(pr, pi, sr, si)
        sr = jnp.where(edge, 0.0, pltpu.roll(tr + fr, shift, 0))
        si = jnp.where(edge, 0.0, pltpu.roll(ti + fi, shift, 0))
    return sr, si


def _rows(i):
    if isinstance(i, int):
        return pl.ds(i * SUBLANES, SUBLANES)
    return pl.ds(pl.multiple_of(i * SUBLANES, SUBLANES), SUBLANES)


def _scan_steps(n, body, init):
    def trip(k, carry):
        for s in range(SCAN_UNROLL):
            carry = body(k * SCAN_UNROLL + s, carry)
        return carry

    carry = lax.fori_loop(0, n // SCAN_UNROLL, trip, init)
    for i in range(n - n % SCAN_UNROLL, n):
        carry = body(i, carry)
    return carry


def _s5_fwd(proj, u_col, bb_re, bb_im, a_re, a_im, cc_re, cc_im, d_skip):
    t = proj.shape[0]
    nch, _, cs = bb_re.shape
    ds = nch * SSM_CHUNK
    u_blk = u_col // SSM_CHUNK
    steps = t // N_SEG

    def body(u_ref, bbr_ref, bbi_ref, ar_ref, ai_ref, ccr_ref, cci_ref, d_ref, sr_ref, si_ref, y_ref):
        ub = u_ref[...].astype(BF16)
        sr_ref[...] = _dot(ub, bbr_ref[...], NN)
        si_ref[...] = _dot(ub, bbi_ref[...], NN)
        ar = jnp.broadcast_to(ar_ref[...], (N_SEG, cs))
        ai = jnp.broadcast_to(ai_ref[...], (N_SEG, cs))
        zero = jnp.zeros((N_SEG, cs), F32)

        def totals(i, carry):
            tr, ti = _cmul(ar, ai, *carry)
            return tr + sr_ref[_rows(i), :], ti + si_ref[_rows(i), :]

        fr, fi = _scan_steps(steps, totals, (zero, zero))
        s0r, s0i = _segment_carries(*_cpow(ar, ai, steps), fr, fi, True)

        def scan(i, carry):
            tr, ti = _cmul(ar, ai, *carry)
            nr, ni = tr + sr_ref[_rows(i), :], ti + si_ref[_rows(i), :]
            sr_ref[_rows(i), :] = nr
            si_ref[_rows(i), :] = ni
            return nr, ni

        _scan_steps(steps, scan, (s0r, s0i))
        y_ref[...] = (_dot(sr_ref[...].astype(BF16), ccr_ref[...], NT)
                      - _dot(si_ref[...].astype(BF16), cci_ref[...], NT)
                      + d_ref[...] * u_ref[...])

    chunk3 = lambda r, c: pl.BlockSpec((None, r, c), lambda j: (j, 0, 0))
    return pl.pallas_call(
        body, name="s5_fwd", grid=(nch,),
        in_specs=[pl.BlockSpec((t, SSM_CHUNK), lambda j: (0, j + u_blk)),
                  chunk3(SSM_CHUNK, cs), chunk3(SSM_CHUNK, cs), chunk3(1, cs), chunk3(1, cs),
                  chunk3(SSM_CHUNK, cs), chunk3(SSM_CHUNK, cs), chunk3(1, SSM_CHUNK)],
        out_specs=[pl.BlockSpec((t, cs), lambda j: (0, j)), pl.BlockSpec((t, cs), lambda j: (0, j)),
                   pl.BlockSpec((t, SSM_CHUNK), lambda j: (0, j))],
        out_shape=[jax.ShapeDtypeStruct((t, nch * cs), F32), jax.ShapeDtypeStruct((t, nch * cs), F32),
                   jax.ShapeDtypeStruct((t, ds), F32)],
        compiler_params=_params(),
    )(proj, bb_re, bb_im, a_re, a_im, cc_re, cc_im, d_skip)


def _s5_bwd(dy, proj, u_col, s_re, s_im, bb_re, bb_im, a_re, a_im, cc_re, cc_im, d_skip, gpc):
    t, ds = dy.shape
    nch, _, cs = bb_re.shape
    u_blk = u_col // SSM_CHUNK
    steps = t // N_SEG

    def body(dy_ref, u_ref, sr_ref, si_ref, bbr_ref, bbi_ref, ar_ref, ai_ref, ccr_ref, cci_ref, d_ref,
             du_ref, dbbr_ref, dbbi_ref, dar_ref, dai_ref, dccr_ref, dcci_ref, dd_ref, gr_ref, gi_ref):
        dyv = dy_ref[...]
        dyb = dyv.astype(BF16)
        gr_ref[...] = _dot(dyb, ccr_ref[...], NN)
        gi_ref[...] = -_dot(dyb, cci_ref[...], NN)
        ar = jnp.broadcast_to(ar_ref[...], (N_SEG, cs))
        ai = -jnp.broadcast_to(ai_ref[...], (N_SEG, cs))
        zero = jnp.zeros((N_SEG, cs), F32)

        def totals(k, carry):
            i = steps - 1 - k
            tr, ti = _cmul(ar, ai, *carry)
            return tr + gr_ref[_rows(i), :], ti + gi_ref[_rows(i), :]

        fr, fi = _scan_steps(steps, totals, (zero, zero))
        e0r, e0i = _segment_carries(*_cpow(ar, ai, steps), fr, fi, False)

        def step(i, gr, gi, pr, pi, acc_r, acc_i):
            tr, ti = _cmul(ar, ai, gr, gi)
            nr, ni = tr + gr_ref[_rows(i), :], ti + gi_ref[_rows(i), :]
            gr_ref[_rows(i), :] = nr
            gi_ref[_rows(i), :] = ni
            return nr, ni, acc_r + nr * pr + ni * pi, acc_i + ni * pr - nr * pi

        def scan(k, carry):
            i = steps - 1 - k
            gr, gi, acc_r, acc_i = carry
            return step(i, gr, gi, sr_ref[_rows(i - 1), :], si_ref[_rows(i - 1), :], acc_r, acc_i)

        gr, gi, acc_r, acc_i = _scan_steps(steps - 1, scan, (e0r, e0i, zero, zero))
        row = lax.broadcasted_iota(jnp.int32, (N_SEG, cs), 0)
        last = _rows(steps - 1)
        pr = jnp.where(row == 0, 0.0, pltpu.roll(sr_ref[last, :], 1, 0))
        pi = jnp.where(row == 0, 0.0, pltpu.roll(si_ref[last, :], 1, 0))
        _, _, acc_r, acc_i = step(0, gr, gi, pr, pi, acc_r, acc_i)
        dar_ref[...] = jnp.sum(acc_r, axis=0, keepdims=True)
        dai_ref[...] = jnp.sum(acc_i, axis=0, keepdims=True)

        uv = u_ref[...]
        ub = uv.astype(BF16)
        grb = gr_ref[...].astype(BF16)
        gib = gi_ref[...].astype(BF16)
        du = d_ref[...] * dyv + _dot(grb, bbr_ref[...], NT) + _dot(gib, bbi_ref[...], NT)
        du_ref[...] = du.astype(BF16)
        def put_groups(ref, full):
            for gl in range(gpc):
                ref[gl] = full[gl * hb:(gl + 1) * hb, gl * pb:(gl + 1) * pb]

        put_groups(dbbr_ref, _dot(ub, grb, TN))
        put_groups(dbbi_ref, _dot(ub, gib, TN))
        put_groups(dccr_ref, _dot(dyb, sr_ref[...].astype(BF16), TN))
        put_groups(dcci_ref, -_dot(dyb, si_ref[...].astype(BF16), TN))
        dd_ref[...] = jnp.sum(dyv * uv, axis=0, keepdims=True)

    hb, pb = SSM_CHUNK // gpc, cs // gpc
    groups = pl.BlockSpec((None, gpc, hb, pb), lambda j: (j, 0, 0, 0))
    groups_shape = jax.ShapeDtypeStruct((nch, gpc, hb, pb), F32)
    chunk3 = lambda r, c: pl.BlockSpec((None, r, c), lambda j: (j, 0, 0))
    cols = lambda w: pl.BlockSpec((t, w), lambda j: (0, j))
    return pl.pallas_call(
        body, name="s5_bwd", grid=(nch,),
        in_specs=[cols(SSM_CHUNK), pl.BlockSpec((t, SSM_CHUNK), lambda j: (0, j + u_blk)), cols(cs), cols(cs),
                  chunk3(SSM_CHUNK, cs), chunk3(SSM_CHUNK, cs), chunk3(1, cs), chunk3(1, cs),
                  chunk3(SSM_CHUNK, cs), chunk3(SSM_CHUNK, cs), chunk3(1, SSM_CHUNK)],
        out_specs=[cols(SSM_CHUNK), groups, groups, chunk3(1, cs), chunk3(1, cs), groups, groups,
                   chunk3(1, SSM_CHUNK)],
        out_shape=[jax.ShapeDtypeStruct((t, ds), BF16), groups_shape, groups_shape,
                   jax.ShapeDtypeStruct((nch, 1, cs), F32), jax.ShapeDtypeStruct((nch, 1, cs), F32),
                   groups_shape, groups_shape, jax.ShapeDtypeStruct((nch, 1, SSM_CHUNK), F32)],
        scratch_shapes=[pltpu.VMEM((t, cs), F32), pltpu.VMEM((t, cs), F32)],
        compiler_params=_params(),
    )(dy, proj, s_re, s_im, bb_re, bb_im, a_re, a_im, cc_re, cc_im, d_skip)


def _discretize(lr, li, log_dt, br, bi):
    dt = jnp.exp(log_dt)
    mag = jnp.exp(lr * dt)
    ang = li * dt
    a_re = mag * jnp.cos(ang)
    a_im = mag * jnp.sin(ang)
    den = lr * lr + li * li
    nr = a_re - 1.0
    f_re = (nr * lr + a_im * li) / den
    f_im = (a_im * lr - nr * li) / den
    return a_re, a_im, f_re * br - f_im * bi, f_re * bi + f_im * br


def _whole(shape):
    return pl.BlockSpec(shape, lambda: (0,) * len(shape))


def _disc_fwd(lr, li, log_dt, br, bi, cr, ci, nch, gpc, deps=()):
    gh, n_state = br.shape
    grp = gh // (nch * gpc)
    rows, cs = gpc * grp, gpc * n_state

    def body(lr_ref, li_ref, dt_ref, br_ref, bi_ref, cr_ref, ci_ref,
             ar_ref, ai_ref, bbr_ref, bbi_ref, ccr_ref, cci_ref, wide_ref):
        a_re, a_im, bb_re, bb_im = _discretize(lr_ref[...], li_ref[...], dt_ref[...], br_ref[...], bi_ref[...])
        for ref, val in ((ar_ref, a_re), (ai_ref, a_im)):
            for j in range(nch):
                for gl in range(gpc):
                    g0 = (j * gpc + gl) * grp
                    ref[j, :, gl * n_state:(gl + 1) * n_state] = val[g0:g0 + 1]
        for ref, val in ((bbr_ref, bb_re), (bbi_ref, bb_im), (ccr_ref, cr_ref[...]), (cci_ref, ci_ref[...])):
            wide_ref[...] = jnp.zeros_like(wide_ref)
            for j in range(nch):
                for gl in range(gpc):
                    g0 = (j * gpc + gl) * grp
                    wide_ref[j, gl * grp:(gl + 1) * grp, gl * n_state:(gl + 1) * n_state] = val[g0:g0 + grp]
            ref[...] = wide_ref[...].astype(BF16)

    args = (lr, li, log_dt, br, bi, cr, ci)
    row_shape, wide_shape = (nch, 1, cs), (nch, rows, cs)
    outs = [jax.ShapeDtypeStruct(row_shape, F32)] * 2 + [jax.ShapeDtypeStruct(wide_shape, BF16)] * 4
    return pl.pallas_call(
        _ignoring_deps(body, 7, deps), name="disc_fwd",
        in_specs=[_whole(a.shape) for a in args] + [ANY_SPEC] * len(deps),
        out_specs=[_whole(o.shape) for o in outs], out_shape=outs,
        scratch_shapes=[pltpu.VMEM(wide_shape, F32)],
    )(*args, *deps)


def _disc_bwd(lr, li, log_dt, br, bi, dar, dai, dbbr, dbbi):
    def body(lr_ref, li_ref, dt_ref, br_ref, bi_ref, dar_ref, dai_ref, dbbr_ref, dbbi_ref,
             dlr_ref, dli_ref, ddt_ref, dbr_ref, dbi_ref):
        _, vjp = jax.vjp(_discretize, lr_ref[...], li_ref[...], dt_ref[...], br_ref[...], bi_ref[...])
        grads = vjp((dar_ref[...], dai_ref[...], dbbr_ref[...], dbbi_ref[...]))
        for ref, val in zip((dlr_ref, dli_ref, ddt_ref, dbr_ref, dbi_ref), grads):
            ref[...] = val

    args = (lr, li, log_dt, br, bi, dar, dai, dbbr, dbbi)
    outs = (lr, li, log_dt, br, bi)
    return pl.pallas_call(
        body, name="disc_bwd", in_specs=[_whole(a.shape) for a in args],
        out_specs=[_whole(a.shape) for a in outs],
        out_shape=[jax.ShapeDtypeStruct(a.shape, F32) for a in outs],
    )(*args)


def _adamw(w, g, m, v):
    m = ADAM_B1 * m + (1.0 - ADAM_B1) * g
    v = ADAM_B2 * v + (1.0 - ADAM_B2) * (g * g)
    m_hat = m / ADAM_BC1
    v_hat = v / ADAM_BC2
    delta = -ADAM_LR * (m_hat / (jnp.sqrt(v_hat) + ADAM_EPS) + ADAM_WD * w)
    return delta, m, v


def _adamw_reduce(name, parts, w, m, v):
    _, r, c = parts.shape
    tr = r
    for cand in (256, 176, 128):
        if r % cand == 0:
            tr = cand
            break

    def body(p_ref, w_ref, m_ref, v_ref, g_ref, d_ref, nm_ref, nv_ref):
        g = p_ref[0].astype(F32)
        for k in range(1, N_DEV):
            g = g + p_ref[k].astype(F32)
        delta, nm, nv = _adamw(w_ref[...], g, m_ref[...], v_ref[...])
        g_ref[...] = g
        d_ref[...] = delta
        nm_ref[...] = nm
        nv_ref[...] = nv

    blk = pl.BlockSpec((tr, c), lambda i: (i, 0))
    return pl.pallas_call(
        body, name=name, grid=(r // tr,),
        in_specs=[pl.BlockSpec((N_DEV, tr, c), lambda i: (0, i, 0)), blk, blk, blk],
        out_specs=[blk] * 4, out_shape=[jax.ShapeDtypeStruct((r, c), F32)] * 4,
        compiler_params=_params(),
    )(parts, w, m, v)


def _sum_parts(name, parts):
    _, r, c = parts.shape

    def body(p_ref, o_ref):
        g = p_ref[0]
        for k in range(1, N_DEV):
            g = g + p_ref[k]
        o_ref[...] = g

    return pl.pallas_call(
        body, name=name, in_specs=[_whole(parts.shape)], out_specs=_whole((r, c)),
        out_shape=jax.ShapeDtypeStruct((r, c), F32), compiler_params=_params(),
    )(parts)


def _adamw_many(name, grads, ws, ms, vs):
    n = len(grads)

    def body(*refs):
        ins, outs = refs[:4 * n], refs[4 * n:]
        for i in range(n):
            g, w, m, v = (ins[j * n + i][...] for j in range(4))
            for ref, val in zip((outs[i], outs[n + i], outs[2 * n + i]), _adamw(w, g, m, v)):
                ref[...] = val

    args = (*grads, *ws, *ms, *vs)
    outs = pl.pallas_call(
        body, name=name, in_specs=[_whole(a.shape) for a in args],
        out_specs=[_whole(a.shape) for a in ws] * 3,
        out_shape=[jax.ShapeDtypeStruct(a.shape, F32) for a in ws] * 3, compiler_params=_params(),
    )(*args)
    return outs[:n], outs[n:2 * n], outs[2 * n:]


def _pack(arrays, rows):
    flat = jnp.concatenate([a.reshape(-1) for a in arrays])
    return jnp.pad(flat, (0, rows * LANES - flat.shape[0])).reshape(rows, LANES)


def _unpack(packed, shapes):
    flat = packed.reshape(-1)
    out, off = [], 0
    for s in shapes:
        n = math.prod(s)
        out.append(flat[off:off + n].reshape(s))
        off += n
    return out


def _packed_rows(shapes):
    n = sum(math.prod(s) for s in shapes)
    return -(-n // (SUBLANES * LANES)) * SUBLANES


def kernel(x, meta_tokens, norm_mix_g, w_in, conv_w, ssm_lam_re, ssm_lam_im, ssm_log_dt, ssm_b_re, ssm_b_im, ssm_c_re, ssm_c_im, ssm_d, ssm_w_glu, gain_conv_out, gain_ssm_out, w_out, norm_ffn_g, w_up, ffn_conv_w, ffn_conv_b, w_down, norm_final_g, loss_target, m_meta_tokens, m_norm_mix_g, m_w_in, m_conv_w, m_ssm_lam_re, m_ssm_lam_im, m_ssm_log_dt, m_ssm_b_re, m_ssm_b_im, m_ssm_c_re, m_ssm_c_im, m_ssm_d, m_ssm_w_glu, m_gain_conv_out, m_gain_ssm_out, m_w_out, m_norm_ffn_g, m_w_up, m_ffn_conv_w, m_ffn_conv_b, m_w_down, m_norm_final_g, v_meta_tokens, v_norm_mix_g, v_w_in, v_conv_w, v_ssm_lam_re, v_ssm_lam_im, v_ssm_log_dt, v_ssm_b_re, v_ssm_b_im, v_ssm_c_re, v_ssm_c_im, v_ssm_d, v_ssm_w_glu, v_gain_conv_out, v_gain_ssm_out, v_w_out, v_norm_ffn_g, v_w_up, v_ffn_conv_w, v_ffn_conv_b, v_w_down, v_norm_final_g):
    weights = dict(meta_tokens=meta_tokens, norm_mix_g=norm_mix_g, w_in=w_in, conv_w=conv_w, ssm_lam_re=ssm_lam_re, ssm_lam_im=ssm_lam_im, ssm_log_dt=ssm_log_dt, ssm_b_re=ssm_b_re, ssm_b_im=ssm_b_im, ssm_c_re=ssm_c_re, ssm_c_im=ssm_c_im, ssm_d=ssm_d, ssm_w_glu=ssm_w_glu, gain_conv_out=gain_conv_out, gain_ssm_out=gain_ssm_out, w_out=w_out, norm_ffn_g=norm_ffn_g, w_up=w_up, ffn_conv_w=ffn_conv_w, ffn_conv_b=ffn_conv_b, w_down=w_down, norm_final_g=norm_final_g)
    mom_m = dict(meta_tokens=m_meta_tokens, norm_mix_g=m_norm_mix_g, w_in=m_w_in, conv_w=m_conv_w, ssm_lam_re=m_ssm_lam_re, ssm_lam_im=m_ssm_lam_im, ssm_log_dt=m_ssm_log_dt, ssm_b_re=m_ssm_b_re, ssm_b_im=m_ssm_b_im, ssm_c_re=m_ssm_c_re, ssm_c_im=m_ssm_c_im, ssm_d=m_ssm_d, ssm_w_glu=m_ssm_w_glu, gain_conv_out=m_gain_conv_out, gain_ssm_out=m_gain_ssm_out, w_out=m_w_out, norm_ffn_g=m_norm_ffn_g, w_up=m_w_up, ffn_conv_w=m_ffn_conv_w, ffn_conv_b=m_ffn_conv_b, w_down=m_w_down, norm_final_g=m_norm_final_g)
    mom_v = dict(meta_tokens=v_meta_tokens, norm_mix_g=v_norm_mix_g, w_in=v_w_in, conv_w=v_conv_w, ssm_lam_re=v_ssm_lam_re, ssm_lam_im=v_ssm_lam_im, ssm_log_dt=v_ssm_log_dt, ssm_b_re=v_ssm_b_re, ssm_b_im=v_ssm_b_im, ssm_c_re=v_ssm_c_re, ssm_c_im=v_ssm_c_im, ssm_d=v_ssm_d, ssm_w_glu=v_ssm_w_glu, gain_conv_out=v_gain_conv_out, gain_ssm_out=v_gain_ssm_out, w_out=v_w_out, norm_ffn_g=v_norm_ffn_g, w_up=v_w_up, ffn_conv_w=v_ffn_conv_w, ffn_conv_b=v_ffn_conv_b, w_down=v_w_down, norm_final_g=v_norm_final_g)
    names = list(weights)

    n_meta, d_meta = meta_tokens.shape
    seq, d = x.shape[1], x.shape[2]
    rows_used = n_meta + seq
    t = -(-rows_used // ROW_TILE) * ROW_TILE
    d_in_s = w_in.shape[2]
    dc_s = conv_w.shape[2]
    dc = dc_s * N_DEV
    ds = ssm_w_glu.shape[2]
    n_groups, n_state, grp = ssm_b_re.shape[1:]
    ns = n_groups * n_state
    nch = ds // SSM_CHUNK
    gpc = n_groups // nch
    ff_s = w_up.shape[2]
    dn_s = w_down.shape[1]
    assert 3 * dc + ds == d_in_s * N_DEV and 2 * dn_s == ff_s and t % (N_SEG * SUBLANES) == 0

    small_shard = jnp.concatenate([meta_tokens.reshape(-1), conv_w.reshape(-1), ffn_conv_w.reshape(-1)])
    n_small = small_shard.shape[0]
    small_rows = -(-n_small // LANES)
    small_shard = jnp.pad(small_shard, (0, small_rows * LANES - n_small)).reshape(small_rows, LANES)
    ag, ag_token = _gather2_start("gather_weights_start", [
        small_shard, w_in[0].astype(BF16), ssm_w_glu[0].astype(BF16), w_out[0].astype(BF16),
        jnp.swapaxes(w_up[0], 0, 1).astype(BF16), w_down[0].astype(BF16)])
    fb = ffn_conv_b

    gh = n_groups * grp
    per_h = lambda a: jnp.broadcast_to(a.reshape(n_groups, 1, -1), (n_groups, grp, n_state)).reshape(gh, n_state)
    ghp = lambda a: a.transpose(0, 1, 3, 2).reshape(gh, n_state)
    lr, li, log_dt_e = per_h(ssm_lam_re), per_h(ssm_lam_im), per_h(ssm_log_dt)
    br, bi = ghp(ssm_b_re), ghp(ssm_b_im)
    a_re_c, a_im_c, bbm_re, bbm_im, ccm_re, ccm_im = _disc_fwd(
        lr, li, log_dt_e, br, bi, ssm_c_re.reshape(gh, n_state), ssm_c_im.reshape(gh, n_state), nch, gpc,
        deps=(ag_token,))
    d_skip = ssm_d.reshape(nch, 1, SSM_CHUNK)

    target = _to_segments(jnp.pad(loss_target[0] + ag_token[0, 0], ((n_meta, t - rows_used), (0, 0))))
    token = _gather2_forward("gather_weights_forward_first", ag, (0, 1), (bbm_im, ccm_im, target))
    (g_small,) = _gather2_wait("gather_weights_wait_small", ag, (0,), token)
    g_small = g_small.reshape(N_DEV, -1)
    o1 = n_meta * d_meta
    o2 = o1 + 3 * dc_s
    meta_full = g_small[:, :o1].reshape(N_DEV, n_meta, d_meta).transpose(1, 0, 2).reshape(n_meta, d)
    conv_w_f = g_small[:, o1:o2].reshape(N_DEV, 3, dc_s).transpose(1, 0, 2).reshape(3, dc)
    fw = g_small[:, o2:o2 + 3 * ff_s].reshape(N_DEV, 3, ff_s).transpose(1, 0, 2).reshape(3, N_DEV * ff_s)
    h0 = _to_segments(jnp.concatenate([meta_full, x[0], jnp.zeros((t - rows_used, d), F32)], axis=0))
    full_t = lambda w: pl.BlockSpec((t, w), lambda *_: (0, 0))

    hn1 = _rms_fwd("norm_mix", h0, norm_mix_g)
    (g_in,) = _gather2_wait("gather_weights_wait_in", ag, (1,), hn1)
    proj = _mm("proj", hn1, g_in, dims=NN, grid=(N_DEV,), a_spec=full_t(d),
               b_spec=pl.BlockSpec((None, d, d_in_s), lambda j: (j, 0, 0)),
               o_spec=pl.BlockSpec((t, d_in_s), lambda j: (0, j)),
               out_shape=jax.ShapeDtypeStruct((t, N_DEV * d_in_s), F32))
    s_re, s_im, y_ssm = _s5_fwd(proj, 3 * dc, bbm_re, bbm_im, a_re_c, a_im_c, ccm_re, ccm_im, d_skip)
    token = _gather2_forward("gather_weights_forward_up", ag, (2, 3, 4), (y_ssm,))
    g_glu, g_out = _gather2_wait("gather_weights_wait_mix", ag, (2, 3), token)
    w_out_f = g_out.reshape(-1, d)
    w_glu_f = g_glu.reshape(ds, ds)
    mixed, z_glu, h1, hn2 = _mix_fwd(proj, y_ssm, w_glu_f, conv_w_f, gain_conv_out, gain_ssm_out,
                                     h0, w_out_f, norm_ffn_g)
    tn_out = 256
    token = _gather2_forward("gather_weights_forward_down", ag, (5,), (hn2,))
    (g_up,) = _gather2_wait("gather_weights_wait_up", ag, (4,), token)
    f2 = N_DEV * ff_s
    pair = 2 * ff_s
    w_up_t = g_up.reshape(f2, d)
    up_pre = _mm("up_proj", hn2, w_up_t, dims=NT, grid=(f2 // pair,), a_spec=full_t(d),
                 b_spec=pl.BlockSpec((pair, d), lambda j: (j, 0)),
                 o_spec=pl.BlockSpec((t, pair), lambda j: (0, j)),
                 out_shape=jax.ShapeDtypeStruct((t, f2), BF16))
    act = _gated_fwd(up_pre, fw, fb)
    (g_down,) = _gather2_wait("gather_weights_wait_down", ag, (5,), act)
    w_down_f = g_down.reshape(f2 // 2, d)

    dh2, dh2_b, loss_part, d_norm_final = _down_loss_bwd(
        act, w_down_f, h1, target, norm_final_g.reshape(1, d), n_meta, rows_used)
    dw_down = _mm("down_wgrad", act, dh2_b, dims=TN, grid=(f2 // 2 // pair,),
                  a_spec=pl.BlockSpec((t, pair), lambda j: (0, j)), b_spec=full_t(d),
                  o_spec=pl.BlockSpec((pair, d), lambda j: (j, 0)),
                  out_shape=jax.ShapeDtypeStruct((f2 // 2, d), BF16))
    ex_down, token = _send_start("exchange_down_start", [dw_down.reshape(N_DEV, dn_s, d)], gather=False)
    dact = _mm("down_dgrad", dh2_b, w_down_f, dims=NT, grid=(f2 // 2 // pair,), a_spec=full_t(d),
               b_spec=pl.BlockSpec((pair, d), lambda j: (j, 0)),
               o_spec=pl.BlockSpec((t, pair), lambda j: (0, j)),
               out_shape=jax.ShapeDtypeStruct((t, f2 // 2), BF16))
    dup_pre, d_fw, d_fb = _gated_bwd(up_pre, dact, fw, fb, deps=(token,))
    dw_up = _mm("up_wgrad", dup_pre, hn2, dims=TN, grid=(f2 // pair,),
                a_spec=pl.BlockSpec((t, pair), lambda j: (0, j)), b_spec=full_t(d),
                o_spec=pl.BlockSpec((pair, d), lambda j: (j, 0)),
                out_shape=jax.ShapeDtypeStruct((f2, d), BF16))
    ex_up, token = _send_start("exchange_up_start", [dw_up.reshape(N_DEV, ff_s, d)], gather=False)
    dh1, dh1_b, d_norm_ffn = _dgrad_rms_bwd("up_dgrad_norm_bwd", dh2, dup_pre, w_up_t, NN, h1, norm_ffn_g,
                                            deps=(token,))
    dmixed = _mm("out_dgrad", dh1_b, w_out_f, dims=NT, grid=((dc + ds) // tn_out,), a_spec=full_t(d),
                 b_spec=pl.BlockSpec((tn_out, d), lambda i: (i, 0)),
                 o_spec=pl.BlockSpec((t, tn_out), lambda i: (0, i)),
                 out_shape=jax.ShapeDtypeStruct((t, dc + ds), F32))
    dw_out = _mm("out_wgrad", mixed, dh1_b, dims=TN, grid=((dc + ds) // tn_out,),
                 a_spec=pl.BlockSpec((t, tn_out), lambda i: (0, i)), b_spec=full_t(d),
                 o_spec=pl.BlockSpec((tn_out, d), lambda i: (i, 0)),
                 out_shape=jax.ShapeDtypeStruct((dc + ds, d), BF16))
    db_gate, dconv, dy_ssm, d_wglu, d_conv_w, d_gain_c, d_gain_s = _mix_bwd1(
        proj, y_ssm, z_glu, dmixed, w_glu_f, conv_w_f, gain_conv_out, gain_ssm_out)
    (du, d_bbm_re, d_bbm_im, d_a_re, d_a_im, d_ccm_re, d_ccm_im, d_dskip) = _s5_bwd(
        dy_ssm, proj, 3 * dc, s_re, s_im, bbm_re, bbm_im, a_re_c, a_im_c, ccm_re, ccm_im, d_skip, gpc)

    from_bb = from_cc = lambda a: a.reshape(gh, n_state)
    first_h = lambda a: jnp.pad(a.reshape(n_groups, 1, n_state), ((0, 0), (0, grp - 1), (0, 0))).reshape(gh, n_state)
    over_h = lambda a: a.reshape(n_groups, grp, n_state).sum(axis=1)
    d_lr, d_li, d_dt_e, d_br, d_bi = _disc_bwd(
        lr, li, log_dt_e, br, bi, first_h(d_a_re), first_h(d_a_im), from_bb(d_bbm_re), from_bb(d_bbm_im))

    rep2d = dict(
        ssm_lam_re=(n_groups, n_state), ssm_lam_im=(n_groups, n_state), ssm_log_dt=(1, n_groups),
        ssm_b_re=(gh, n_state), ssm_b_im=(gh, n_state), ssm_c_re=(gh, n_state),
        ssm_c_im=(gh, n_state), ssm_d=(n_groups, grp), gain_conv_out=(1, dc),
        gain_ssm_out=(1, ds), norm_ffn_g=(1, d), ffn_conv_b=(1, N_DEV * ff_s), norm_final_g=(1, d))
    rep_names = list(rep2d)
    rep_grads = dict(
        ssm_lam_re=over_h(d_lr), ssm_lam_im=over_h(d_li), ssm_log_dt=over_h(d_dt_e).sum(axis=1),
        ssm_b_re=d_br, ssm_b_im=d_bi, ssm_c_re=from_cc(d_ccm_re), ssm_c_im=from_cc(d_ccm_im),
        ssm_d=d_dskip, gain_conv_out=d_gain_c, gain_ssm_out=d_gain_s, norm_ffn_g=d_norm_ffn,
        ffn_conv_b=d_fb, norm_final_g=d_norm_final)
    rep_shapes = [rep2d[n] for n in rep_names] + [(1, 1)]
    rep_rows = _packed_rows(rep_shapes)
    rep_pack = _pack([rep_grads[n] for n in rep_names] + [loss_part], rep_rows)
    ex_mix, token = _send_start("exchange_mix_start", [
        rep_pack, dw_out.reshape(N_DEV, -1, d), d_wglu.astype(BF16).reshape(N_DEV, -1, ds),
        d_conv_w.reshape(3, N_DEV, dc_s).transpose(1, 0, 2),
        d_fw.reshape(3, N_DEV, ff_s).transpose(1, 0, 2)],
        gather=[True, False, False, False, False])
    dcdv = _mix_bwd2(proj, dconv, conv_w_f, deps=(token,))
    dproj = jnp.concatenate([db_gate, dcdv, du], axis=1)
    d_in = N_DEV * d_in_s
    w_in_rows = g_in.transpose(1, 0, 2).reshape(d, d_in)
    dw_in = _mm("proj_wgrad", hn1, dproj, dims=TN, grid=(N_DEV,), a_spec=full_t(d),
                b_spec=pl.BlockSpec((t, d_in_s), lambda j: (0, j)),
                o_spec=pl.BlockSpec((None, d, d_in_s), lambda j: (j, 0, 0)),
                out_shape=jax.ShapeDtypeStruct((N_DEV, d, d_in_s), BF16))
    dh0, _, d_norm_mix = _dgrad_rms_bwd("proj_dgrad_norm_bwd", dh1, dproj, w_in_rows, NT, h0, norm_mix_g)
    dh0 = _from_segments(dh0)
    grad_x = dh0[n_meta:rows_used][None]
    d_meta_b = dh0[:n_meta].reshape(n_meta, N_DEV, d_meta).transpose(1, 0, 2)
    ex_in, ex_in_token = _send_start("exchange_in_start", [dw_in, d_meta_b, d_norm_mix],
                                     gather=[False, False, True])

    shard_out = {}

    def update(n, parts, transposed=False):
        sh = weights[n].shape
        two_d = lambda a: (jnp.swapaxes(a[0], 0, 1) if transposed else a.reshape(parts.shape[1:]))
        res = _adamw_reduce("adamw_" + n, parts, two_d(weights[n]), two_d(mom_m[n]), two_d(mom_v[n]))
        shard_out[n] = [(jnp.swapaxes(r, 0, 1) if transposed else r).reshape(sh) for r in res]
        return res[0]

    (p_down,) = _send_wait("exchange_down_wait", ex_down, (0,), ex_in_token)
    done = update("w_down", p_down)
    (p_up,) = _send_wait("exchange_up_wait", ex_up, (0,), done)
    done = update("w_up", p_up, transposed=True)
    (rep_parts,) = _send_wait("gather_small_grads_wait", ex_mix, (0,), done)
    rep_sum = _sum_parts("sum_small_grads", rep_parts)
    *rep_g, loss = _unpack(rep_sum, rep_shapes)
    loss = loss.reshape(())
    swapped = ("ssm_b_re", "ssm_b_im")
    to2d = lambda n, a: ghp(a) if n in swapped else a.reshape(rep2d[n])
    from2d = lambda n, a: (a.reshape(1, n_groups, grp, n_state).transpose(0, 1, 3, 2) if n in swapped
                           else a.reshape(weights[n].shape))
    as2d = lambda tree: [to2d(n, tree[n]) for n in rep_names]
    rep_res = _adamw_many("adamw_replicated", rep_g, as2d(weights), as2d(mom_m), as2d(mom_v))
    for i, n in enumerate(rep_names):
        shard_out[n] = [from2d(n, r) for r in (rep_g[i], *(res[i] for res in rep_res))]

    p_out, p_glu, p_cw, p_fw = _send_wait("exchange_mix_wait", ex_mix, (1, 2, 3, 4), rep_sum)
    update("w_out", p_out)
    update("ssm_w_glu", p_glu)
    update("conv_w", p_cw)
    done = update("ffn_conv_w", p_fw)
    p_in, p_meta, p_nm = _send_wait("exchange_in_wait", ex_in, (0, 1, 2), done)
    update("w_in", p_in)
    update("meta_tokens", p_meta)
    update("norm_mix_g", p_nm)

    grads = [shard_out[n][0] for n in names]
    deltas = [shard_out[n][1] for n in names]
    new_m = [shard_out[n][2] for n in names]
    new_v = [shard_out[n][3] for n in names]
    return (loss, grad_x, *grads, *deltas, *new_m, *new_v)
```

```python
import math

import jax
import jax.numpy as jnp
from jax import lax
from jax.experimental import pallas as pl
from jax.experimental.pallas import tpu as pltpu

F32 = jnp.float32
BF16 = jnp.bfloat16
MESH = pl.DeviceIdType.MESH

N_DEV = 8
RMS_EPS = 1e-6
ADAM_LR = 0.001
ADAM_B1 = 0.9
ADAM_B2 = 0.999
ADAM_EPS = 1e-08
ADAM_WD = 0.01
ADAM_STEP = 10
ADAM_BC1 = 1.0 - ADAM_B1 ** ADAM_STEP
ADAM_BC2 = 1.0 - ADAM_B2 ** ADAM_STEP

SUBLANES = 8
LANES = 128
ROW_TILE = 128
ROW_CHUNK = 32
WIDE_TILES = (544, 256, 128)
FFN_COLS = 256
STAGE_COLS = 256
OUT_TILE = 512
N_SEG = 8
HALO_ROWS = 16
SSM_CHUNK = 128
SCAN_UNROLL = 8
VMEM_LIMIT = 48 * 1024 * 1024

NN = ((1,), (0,))
NT = ((1,), (1,))
TN = ((0,), (0,))


def _params(**kw):
    return pltpu.CompilerParams(vmem_limit_bytes=VMEM_LIMIT, **kw)


def _dot(a, b, dims):
    return lax.dot_general(a, b, (dims, ((), ())), preferred_element_type=F32)


def _mean_sq_rsqrt(x):
    return lax.rsqrt(jnp.mean(x * x, axis=-1, keepdims=True) + RMS_EPS)


def _rms_bwd(x, r, g, dy):
    xhat = x * r
    dxh = dy * g
    dx = r * (dxh - xhat * jnp.mean(dxh * xhat, axis=-1, keepdims=True))
    return dx, dy * xhat


def _gelu(y):
    c = math.sqrt(2.0 / math.pi)
    t = jnp.tanh(c * (y + 0.044715 * y * y * y))
    return 0.5 * y * (1.0 + t), t


def _gelu_grad(y, t):
    c = math.sqrt(2.0 / math.pi)
    return 0.5 * (1.0 + t) + 0.5 * y * (1.0 - t * t) * c * (1.0 + 3.0 * 0.044715 * y * y)


def _wrap_prev_halo(halo, first):
    seg = lax.broadcasted_iota(jnp.int32, halo.shape, 0) % N_SEG
    wrapped = jnp.where(seg == 0, 0.0, pltpu.roll(halo, 1, 0))
    return jnp.where(first, wrapped, halo)


def _wrap_next_halo(halo, last):
    seg = lax.broadcasted_iota(jnp.int32, halo.shape, 0) % N_SEG
    wrapped = jnp.where(seg == N_SEG - 1, 0.0, pltpu.roll(halo, halo.shape[0] - 1, 0))
    return jnp.where(last, wrapped, halo)


def _dev_index(p):
    return 4 * p[0] + 2 * p[1] + p[2]


def _allgather(name, shards, deps=()):
    n = len(shards)

    def body(*refs):
        ins, outs = refs[:n], refs[n:2 * n]
        send_sems, recv_sems, local_sems = refs[2 * n:]
        x, y, c = lax.axis_index("x"), lax.axis_index("y"), lax.axis_index("c")
        me, sibling = (x, y, c), (x, y, 1 - c)
        chips = [(1 - x, y), (x, 1 - y), (1 - x, 1 - y)]

        def copy(a, k, block, to, src=None):
            dst = outs[a].at[_dev_index(block)]
            return pltpu.make_async_remote_copy(
                src_ref=dst if src is None else src, dst_ref=dst,
                send_sem=send_sems.at[a, k], recv_sem=recv_sems.at[a, k],
                device_id=to, device_id_type=MESH)

        mine = [pltpu.make_async_copy(ins[a], outs[a].at[_dev_index(me)], local_sems.at[a])
                for a in range(n)]
        for cp in mine:
            cp.start()
        first = []
        for a in range(n):
            first.append(copy(a, 0, me, sibling, src=ins[a]))
            for j, chip in enumerate(chips):
                first.append(copy(a, 1 + j, me, (*chip, c), src=ins[a]))
        for cp in first:
            cp.start()
        passed = []
        for j, chip in enumerate(chips):
            for a in range(n):
                copy(a, 1 + j, (*chip, c), me).wait_recv()
                fwd = copy(a, 4 + j, (*chip, c), sibling)
                fwd.start()
                passed.append(fwd)
        for a in range(n):
            copy(a, 0, sibling, me).wait_recv()
            for j, chip in enumerate(chips):
                copy(a, 4 + j, (*chip, 1 - c), me).wait_recv()
        for cp in first + passed:
            cp.wait_send()
        for cp in mine:
            cp.wait()

    any_spec = pl.BlockSpec(memory_space=pl.ANY)
    return pl.pallas_call(
        _ignoring_deps(body, n, deps), name=name,
        out_shape=[jax.ShapeDtypeStruct((N_DEV,) + s.shape, s.dtype) for s in shards],
        in_specs=[any_spec] * (n + len(deps)), out_specs=[any_spec] * n,
        scratch_shapes=[pltpu.SemaphoreType.DMA((n, 7)), pltpu.SemaphoreType.DMA((n, 7)),
                        pltpu.SemaphoreType.DMA((n,))],
    )(*shards, *deps)


HBM_SPEC = pl.BlockSpec(memory_space=pltpu.HBM)
SEM_SPEC = pl.BlockSpec(memory_space=pltpu.SEMAPHORE)
ANY_SPEC = pl.BlockSpec(memory_space=pl.ANY)
DATAFLOW = pltpu.SideEffectType.DATAFLOW_SIDE_EFFECTING


def _ignoring_deps(body, n_in, deps):
    n_dep = len(deps)

    def wrapped(*refs):
        return body(*refs[:n_in], *refs[n_in + n_dep:])

    return wrapped


def _my_position():
    x, y, c = lax.axis_index("x"), lax.axis_index("y"), lax.axis_index("c")
    return (x, y, c)


def _peer(me, k):
    return tuple((1 - v) if (k >> s) & 1 else v for v, s in zip(me, (2, 1, 0)))


def _split_copies(src_refs, land_refs, send_sems, recv_sems, gather):
    me = _my_position()
    copies = []
    for a, (src, land) in enumerate(zip(src_refs, land_refs)):
        for k in range(1, N_DEV):
            peer = _peer(me, k)
            copies.append(pltpu.make_async_remote_copy(
                src_ref=src if gather[a] else src.at[_dev_index(peer)], dst_ref=land.at[_dev_index(me)],
                send_sem=send_sems[a].at[k - 1], recv_sem=recv_sems[a].at[k - 1],
                device_id=peer, device_id_type=MESH))
    return copies


def _own_slot(block, like_shape):
    me = _dev_index(_my_position())
    return lax.dynamic_update_index_in_dim(lax.empty(like_shape, block.dtype), block, me, 0)


def _send_start(name, srcs, gather):
    n = len(srcs)
    me = _dev_index(_my_position())
    gather = [gather] * n if isinstance(gather, bool) else list(gather)
    lands = [_own_slot(s, (N_DEV,) + s.shape) if g else
             _own_slot(lax.dynamic_index_in_dim(s, me, 0, keepdims=False), s.shape)
             for s, g in zip(srcs, gather)]

    def body(*refs):
        src_refs, land_refs = refs[:n], refs[n:2 * n]
        send_sems, recv_sems = refs[2 * n:3 * n], refs[3 * n:4 * n]
        token = refs[-1]
        for cp in _split_copies(src_refs, land_refs, send_sems, recv_sems, gather):
            cp.start()
        token[...] = jnp.zeros_like(token)

    hbm = lambda a: pltpu.HBM(a.shape, a.dtype)
    sems = [pltpu.SemaphoreType.DMA((N_DEV - 1,))] * n
    outs = pl.pallas_call(
        body, name=name,
        out_shape=(*sems, *sems, *[hbm(s) for s in srcs], *[hbm(l) for l in lands],
                   jax.ShapeDtypeStruct((SUBLANES, LANES), F32)),
        in_specs=[HBM_SPEC] * (2 * n),
        out_specs=(*[SEM_SPEC] * (2 * n), *[HBM_SPEC] * (2 * n), pl.BlockSpec(memory_space=pltpu.VMEM)),
        input_output_aliases={i: 2 * n + i for i in range(2 * n)},
        compiler_params=pltpu.CompilerParams(has_side_effects=DATAFLOW),
    )(*[pltpu.with_memory_space_constraint(a, pltpu.HBM) for a in (*srcs, *lands)])
    state = dict(send=outs[:n], recv=outs[n:2 * n], srcs=outs[2 * n:3 * n], lands=outs[3 * n:4 * n],
                 gather=gather)
    return state, outs[-1]


def _send_wait(name, state, which, after):
    n = len(which)
    pick = lambda key: [state[key][i] for i in which]
    gather = pick("gather")

    def body(*refs):
        src_refs, land_refs = refs[:n], refs[n:2 * n]
        send_sems, recv_sems = refs[2 * n:3 * n], refs[3 * n:4 * n]
        for cp in _split_copies(src_refs, land_refs, send_sems, recv_sems, gather):
            cp.wait_send()
            cp.wait_recv()

    srcs, lands = pick("srcs"), pick("lands")
    hbm = lambda a: pltpu.HBM(a.shape, a.dtype)
    outs = pl.pallas_call(
        body, name=name,
        out_shape=(*[hbm(s) for s in srcs], *[hbm(l) for l in lands]),
        in_specs=[*[HBM_SPEC] * (2 * n), *[SEM_SPEC] * (2 * n), ANY_SPEC],
        out_specs=tuple([HBM_SPEC] * (2 * n)),
        input_output_aliases={i: i for i in range(2 * n)},
        compiler_params=pltpu.CompilerParams(has_side_effects=DATAFLOW),
    )(*srcs, *lands, *pick("send"), *pick("recv"), after)
    return outs[n:]


def _two_level_copies(src_refs, land_refs, sems1, sems2):
    x, y, c = _my_position()
    me, sibling = (x, y, c), (x, y, 1 - c)
    chips = [(1 - x, y), (x, 1 - y), (1 - x, 1 - y)]
    stage1, stage2 = [], []
    for a, land in enumerate(land_refs):
        def copy(src, block, to, sems, k):
            return pltpu.make_async_remote_copy(
                src_ref=src, dst_ref=land.at[_dev_index(block)], send_sem=sems[0][a].at[k],
                recv_sem=sems[1][a].at[k], device_id=to, device_id_type=MESH)
        src = src_refs[a] if src_refs is not None else land.at[_dev_index(me)]
        stage1.append([copy(src, me, sibling, sems1, 0)] +
                      [copy(src, me, (*chip, c), sems1, 1 + j) for j, chip in enumerate(chips)])
        if sems2 is not None:
            stage2.append([copy(land.at[_dev_index((*chip, c))], (*chip, c), sibling, sems2, j)
                           for j, chip in enumerate(chips)])
    return stage1, stage2


def _gather2_start(name, shards):
    n = len(shards)
    lands = [_own_slot(s, (N_DEV,) + s.shape) for s in shards]

    def body(*refs):
        src_refs, land_refs = refs[:n], refs[n:2 * n]
        sems1 = (refs[2 * n:3 * n], refs[3 * n:4 * n])
        stage1, _ = _two_level_copies(src_refs, land_refs, sems1, None)
        for copies in stage1:
            for cp in copies:
                cp.start()
        refs[-1][...] = jnp.zeros_like(refs[-1])

    hbm = lambda a: pltpu.HBM(a.shape, a.dtype)
    sems = [pltpu.SemaphoreType.DMA((4,))] * n
    outs = pl.pallas_call(
        body, name=name,
        out_shape=(*sems, *sems, *[hbm(s) for s in shards], *[hbm(l) for l in lands],
                   jax.ShapeDtypeStruct((SUBLANES, LANES), F32)),
        in_specs=[HBM_SPEC] * (2 * n),
        out_specs=(*[SEM_SPEC] * (2 * n), *[HBM_SPEC] * (2 * n), pl.BlockSpec(memory_space=pltpu.VMEM)),
        input_output_aliases={i: 2 * n + i for i in range(2 * n)},
        compiler_params=pltpu.CompilerParams(has_side_effects=DATAFLOW),
    )(*[pltpu.with_memory_space_constraint(a, pltpu.HBM) for a in (*shards, *lands)])
    state = dict(send1=list(outs[:n]), recv1=list(outs[n:2 * n]), srcs=list(outs[2 * n:3 * n]),
                 lands=list(outs[3 * n:4 * n]), send2={}, recv2={})
    return state, outs[-1]


def _gather2_forward(name, state, which, after):
    n = len(which)
    pick = lambda key: [state[key][i] for i in which]

    def body(*refs):
        land_refs, recv1 = refs[:n], refs[n:2 * n]
        outs = refs[2 * n + len(after):]
        sems2 = (outs[:n], outs[n:2 * n])
        stage1, stage2 = _two_level_copies(None, land_refs, (recv1, recv1), sems2)
        for a in range(n):
            for j in range(3):
                stage1[a][1 + j].wait_recv()
                stage2[a][j].start()
        outs[-1][...] = jnp.zeros_like(outs[-1])

    lands = pick("lands")
    sems = [pltpu.SemaphoreType.DMA((3,))] * n
    outs = pl.pallas_call(
        body, name=name,
        out_shape=(*sems, *sems, *[pltpu.HBM(l.shape, l.dtype) for l in lands],
                   jax.ShapeDtypeStruct((SUBLANES, LANES), F32)),
        in_specs=[*[HBM_SPEC] * n, *[SEM_SPEC] * n, *[ANY_SPEC] * len(after)],
        out_specs=(*[SEM_SPEC] * (2 * n), *[HBM_SPEC] * n, pl.BlockSpec(memory_space=pltpu.VMEM)),
        input_output_aliases={i: 2 * n + i for i in range(n)},
        compiler_params=pltpu.CompilerParams(has_side_effects=DATAFLOW),
    )(*lands, *pick("recv1"), *after)
    for idx, i in enumerate(which):
        state["send2"][i], state["recv2"][i] = outs[idx], outs[n + idx]
        state["lands"][i] = outs[2 * n + idx]
    return outs[-1]


def _gather2_wait(name, state, which, after):
    n = len(which)
    pick = lambda key: [state[key][i] for i in which]

    def body(*refs):
        src_refs, land_refs = refs[:n], refs[n:2 * n]
        sems1 = (refs[2 * n:3 * n], refs[3 * n:4 * n])
        sems2 = (refs[4 * n:5 * n], refs[5 * n:6 * n])
        stage1, stage2 = _two_level_copies(src_refs, land_refs, sems1, sems2)
        for a in range(n):
            for cp in stage1[a]:
                cp.wait_send()
            stage1[a][0].wait_recv()
            for cp in stage2[a]:
                cp.wait_send()
                cp.wait_recv()

    srcs, lands = pick("srcs"), pick("lands")
    hbm = lambda a: pltpu.HBM(a.shape, a.dtype)
    outs = pl.pallas_call(
        body, name=name,
        out_shape=(*[hbm(s) for s in srcs], *[hbm(l) for l in lands]),
        in_specs=[*[HBM_SPEC] * (2 * n), *[SEM_SPEC] * (4 * n), ANY_SPEC],
        out_specs=tuple([HBM_SPEC] * (2 * n)),
        input_output_aliases={i: i for i in range(2 * n)},
        compiler_params=pltpu.CompilerParams(has_side_effects=DATAFLOW),
    )(*srcs, *lands, *pick("send1"), *pick("recv1"), *pick("send2"), *pick("recv2"), after)
    return outs[n:]


def _mm(name, a, b, *, dims, grid, a_spec, b_spec, o_spec, out_shape, acc_shape=None,
        res=None, res_spec=None):
    n_red = grid[-1] if acc_shape is not None else 1
    red_axis = len(grid) - 1

    def body(*refs):
        a_ref, b_ref = refs[0], refs[1]
        r_ref = refs[2] if res is not None else None
        o_ref = refs[3] if res is not None else refs[2]
        part = _dot(a_ref[...], b_ref[...], dims)
        if acc_shape is None:
            if r_ref is not None:
                part = part + r_ref[...]
            o_ref[...] = part.astype(o_ref.dtype)
            return
        acc_ref = refs[-1]
        k = pl.program_id(red_axis)

        @pl.when(k == 0)
        def _():
            acc_ref[...] = part

        @pl.when(k > 0)
        def _():
            acc_ref[...] += part

        @pl.when(k == n_red - 1)
        def _():
            total = acc_ref[...]
            if r_ref is not None:
                total = total + r_ref[...]
            o_ref[...] = total.astype(o_ref.dtype)

    ins, in_specs = [a, b], [a_spec, b_spec]
    if res is not None:
        ins.append(res)
        in_specs.append(res_spec)
    return pl.pallas_call(
        body, name=name, grid=grid, in_specs=in_specs, out_specs=o_spec, out_shape=out_shape,
        scratch_shapes=[pltpu.VMEM(acc_shape, F32)] if acc_shape is not None else [],
        compiler_params=_params(),
    )(*ins)


def _wide_tile(t):
    return next(c for c in WIDE_TILES if t % c == 0)


def _rms_fwd(name, h, g):
    t, d = h.shape
    tile = _wide_tile(t)

    def body(h_ref, g_ref, o_ref):
        x = h_ref[...]
        o_ref[...] = (x * _mean_sq_rsqrt(x) * g_ref[...]).astype(BF16)

    return pl.pallas_call(
        body, name=name, grid=(t // tile,),
        in_specs=[pl.BlockSpec((tile, d), lambda i: (i, 0)), pl.BlockSpec((1, d), lambda i: (0, 0))],
        out_specs=pl.BlockSpec((tile, d), lambda i: (i, 0)),
        out_shape=jax.ShapeDtypeStruct((t, d), BF16), compiler_params=_params(),
    )(h, g)


def _fused_tile(t):
    half = _wide_tile(t) // 2
    return half if half % 16 == 0 and t % half == 0 else _wide_tile(t)


def _dgrad_rms_bwd(name, dres, dy, w, dims, h, g, deps=()):
    t, d = h.shape
    k = dy.shape[1]

    def body(dres_ref, dy_ref, w_ref, h_ref, g_ref, dh_ref, dhb_ref, dg_ref):
        x = h_ref[...]
        dhn = _dot(dy_ref[...], w_ref[...], dims)
        dx, dgt = _rms_bwd(x, _mean_sq_rsqrt(x), g_ref[...], dhn)
        dh = dres_ref[...] + dx
        dh_ref[...] = dh
        dhb_ref[...] = dh.astype(BF16)

        @pl.when(pl.program_id(0) == 0)
        def _():
            dg_ref[...] = jnp.zeros_like(dg_ref)

        dg_ref[...] += jnp.sum(dgt, axis=0, keepdims=True)

    tile = _fused_tile(t)
    row = pl.BlockSpec((tile, d), lambda i: (i, 0))
    vec = pl.BlockSpec((1, d), lambda i: (0, 0))
    return pl.pallas_call(
        _ignoring_deps(body, 5, deps), name=name, grid=(t // tile,),
        in_specs=[row, pl.BlockSpec((tile, k), lambda i: (i, 0)), pl.BlockSpec(w.shape, lambda i: (0, 0)), row,
                  vec] + [ANY_SPEC] * len(deps),
        out_specs=[row, row, vec],
        out_shape=[jax.ShapeDtypeStruct((t, d), F32), jax.ShapeDtypeStruct((t, d), BF16),
                   jax.ShapeDtypeStruct((1, d), F32)],
        compiler_params=_params(),
    )(dres, dy, w, h, g, *deps)


def _down_loss_bwd(act, w_down, h1, target, g, row_lo, row_hi):
    t, d = h1.shape
    f = act.shape[1]
    tile = _fused_tile(t)

    def body(act_ref, w_ref, h_ref, tg_ref, g_ref, dh_ref, dhb_ref, loss_ref, dg_ref):
        i = pl.program_id(0)
        x = h_ref[...] + _dot(act_ref[...], w_ref[...], NN)
        r = _mean_sq_rsqrt(x)
        gv = g_ref[...]
        y = x * r * gv
        rows = i * tile + lax.broadcasted_iota(jnp.int32, (tile, 1), 0)
        time = (rows % N_SEG) * (t // N_SEG) + rows // N_SEG
        valid = jnp.logical_and(time >= row_lo, time < row_hi)
        err = jnp.where(valid, y - tg_ref[...], 0.0)
        dy = err * (1.0 / d)
        dx, dgt = _rms_bwd(x, r, gv, dy)
        dh_ref[...] = dx
        dhb_ref[...] = dx.astype(BF16)

        @pl.when(i == 0)
        def _():
            loss_ref[...] = jnp.zeros_like(loss_ref)
            dg_ref[...] = jnp.zeros_like(dg_ref)

        row_loss = jnp.mean(err * err, axis=-1, keepdims=True)
        loss_ref[...] += 0.5 * jnp.sum(row_loss, axis=0, keepdims=True)
        dg_ref[...] += jnp.sum(dgt, axis=0, keepdims=True)

    row = pl.BlockSpec((tile, d), lambda i: (i, 0))
    vec = pl.BlockSpec((1, d), lambda i: (0, 0))
    return pl.pallas_call(
        body, name="down_proj_loss_bwd", grid=(t // tile,),
        in_specs=[pl.BlockSpec((tile, f), lambda i: (i, 0)), pl.BlockSpec(w_down.shape, lambda i: (0, 0)),
                  row, row, vec],
        out_specs=[row, row, pl.BlockSpec((1, 1), lambda i: (0, 0)), vec],
        out_shape=[jax.ShapeDtypeStruct((t, d), F32), jax.ShapeDtypeStruct((t, d), BF16),
                   jax.ShapeDtypeStruct((1, 1), F32), jax.ShapeDtypeStruct((1, d), F32)],
        compiler_params=_params(),
    )(act, w_down, h1, target, g)


def _prev_halo(i, t):
    return jnp.where(i == 0, t // HALO_ROWS - 1, i * (ROW_TILE // HALO_ROWS) - 1)


def _next_halo(i, t):
    return jnp.where(i == t // ROW_TILE - 1, 0, (i + 1) * (ROW_TILE // HALO_ROWS))


def _causal_taps(cur, halo, first):
    rows = cur.shape[0]
    ext = jnp.concatenate([_wrap_prev_halo(halo, first), cur], axis=0)
    return ext[:rows], ext[N_SEG:N_SEG + rows]


def _anticausal_taps(cur, halo, last):
    rows = cur.shape[0]
    ext = jnp.concatenate([cur, _wrap_next_halo(halo, last)], axis=0)
    return ext[N_SEG:N_SEG + rows], ext[2 * N_SEG:2 * N_SEG + rows]


def _mix_fwd(proj, y, w_glu, conv_w, gain_c, gain_s, h0, w_out, gain_ffn):
    t, d = h0.shape
    dc = conv_w.shape[1]
    ds = y.shape[1]

    def body(p_ref, halo_ref, y_ref, wg_ref, cw_ref, gc_ref, gs_ref, h0_ref, wo_ref, gf_ref,
             mixed_ref, z_ref, h1_ref, hn_ref):
        i = pl.program_id(0)
        p = p_ref[...]
        b, c, v = p[:, :dc], p[:, dc:2 * dc], p[:, 2 * dc:3 * dc]
        cv = c * v
        hp = halo_ref[...]
        x2, x1 = _causal_taps(cv, hp[:, dc:2 * dc] * hp[:, 2 * dc:3 * dc], i == 0)
        cw = cw_ref[...]
        conv = cw[0:1] * x2 + cw[1:2] * x1 + cw[2:3] * cv
        co = b * conv
        mixed_ref[:, :dc] = (co * _mean_sq_rsqrt(co) * gc_ref[...]).astype(BF16)
        g, _ = _gelu(y_ref[...])
        z = _dot(g.astype(BF16), wg_ref[...], NN)
        z_ref[...] = z
        so = g * jax.nn.sigmoid(z)
        mixed_ref[:, dc:] = (so * _mean_sq_rsqrt(so) * gs_ref[...]).astype(BF16)
        h1 = h0_ref[...] + _dot(mixed_ref[...], wo_ref[...], NN)
        h1_ref[...] = h1
        hn_ref[...] = (h1 * _mean_sq_rsqrt(h1) * gf_ref[...]).astype(BF16)

    const = lambda i: (0, 0)
    row = lambda w: pl.BlockSpec((ROW_TILE, w), lambda i: (i, 0))
    return pl.pallas_call(
        body, name="mix_fwd_out_proj", grid=(t // ROW_TILE,),
        in_specs=[row(3 * dc), pl.BlockSpec((HALO_ROWS, 3 * dc), lambda i: (_prev_halo(i, t), 0)), row(ds),
                  pl.BlockSpec((ds, ds), const), pl.BlockSpec(conv_w.shape, const),
                  pl.BlockSpec((1, dc), const), pl.BlockSpec((1, ds), const),
                  row(d), pl.BlockSpec(w_out.shape, const), pl.BlockSpec((1, d), const)],
        out_specs=[row(dc + ds), row(ds), row(d), row(d)],
        out_shape=[jax.ShapeDtypeStruct((t, dc + ds), BF16), jax.ShapeDtypeStruct((t, ds), F32),
                   jax.ShapeDtypeStruct((t, d), F32), jax.ShapeDtypeStruct((t, d), BF16)],
        compiler_params=_params(),
    )(proj, proj, y, w_glu, conv_w, gain_c, gain_s, h0, w_out, gain_ffn)


def _mix_bwd1(proj, y, z, dmixed, w_glu, conv_w, gain_c, gain_s):
    t = proj.shape[0]
    dc = conv_w.shape[1]
    ds = y.shape[1]

    def body(p_ref, halo_ref, y_ref, z_ref, dm_ref, wg_ref, cw_ref, gc_ref, gs_ref,
             db_ref, dconv_ref, dy_ref, dwg_ref, dcw_ref, dgc_ref, dgs_ref):
        i = pl.program_id(0)

        @pl.when(i == 0)
        def _():
            dwg_ref[...] = jnp.zeros_like(dwg_ref)
            dcw_ref[...] = jnp.zeros_like(dcw_ref)
            dgc_ref[...] = jnp.zeros_like(dgc_ref)
            dgs_ref[...] = jnp.zeros_like(dgs_ref)

        p = p_ref[...]
        b, c, v = p[:, :dc], p[:, dc:2 * dc], p[:, 2 * dc:3 * dc]
        cv = c * v
        hp = halo_ref[...]
        x2, x1 = _causal_taps(cv, hp[:, dc:2 * dc] * hp[:, 2 * dc:3 * dc], i == 0)
        cw = cw_ref[...]
        conv = cw[0:1] * x2 + cw[1:2] * x1 + cw[2:3] * cv
        co = b * conv
        dm = dm_ref[...]
        dco, dgt = _rms_bwd(co, _mean_sq_rsqrt(co), gc_ref[...], dm[:, :dc])
        dgc_ref[...] += jnp.sum(dgt, axis=0, keepdims=True)
        db_ref[...] = (dco * conv).astype(BF16)
        dconv = dco * b
        dconv_ref[...] = dconv
        dcw_ref[0:1, :] += jnp.sum(dconv * x2, axis=0, keepdims=True)
        dcw_ref[1:2, :] += jnp.sum(dconv * x1, axis=0, keepdims=True)
        dcw_ref[2:3, :] += jnp.sum(dconv * cv, axis=0, keepdims=True)

        yv = y_ref[...]
        g, th = _gelu(yv)
        sg = jax.nn.sigmoid(z_ref[...])
        so = g * sg
        dso, dgt = _rms_bwd(so, _mean_sq_rsqrt(so), gs_ref[...], dm[:, dc:])
        dgs_ref[...] += jnp.sum(dgt, axis=0, keepdims=True)
        dz = (dso * g * sg * (1.0 - sg)).astype(BF16)
        dg = dso * sg + _dot(dz, wg_ref[...], NT)
        dwg_ref[...] += _dot(g.astype(BF16), dz, TN)
        dy_ref[...] = dg * _gelu_grad(yv, th)

    const = lambda i: (0, 0)
    row = lambda w: pl.BlockSpec((ROW_TILE, w), lambda i: (i, 0))
    return pl.pallas_call(
        body, name="mix_bwd1", grid=(t // ROW_TILE,),
        in_specs=[row(3 * dc), pl.BlockSpec((HALO_ROWS, 3 * dc), lambda i: (_prev_halo(i, t), 0)),
                  row(ds), row(ds), row(dc + ds), pl.BlockSpec((ds, ds), const),
                  pl.BlockSpec(conv_w.shape, const), pl.BlockSpec((1, dc), const),
                  pl.BlockSpec((1, ds), const)],
        out_specs=[row(dc), row(dc), row(ds), pl.BlockSpec((ds, ds), const),
                   pl.BlockSpec(conv_w.shape, const), pl.BlockSpec((1, dc), const),
                   pl.BlockSpec((1, ds), const)],
        out_shape=[jax.ShapeDtypeStruct((t, dc), BF16), jax.ShapeDtypeStruct((t, dc), F32),
                   jax.ShapeDtypeStruct((t, ds), F32), jax.ShapeDtypeStruct((ds, ds), F32),
                   jax.ShapeDtypeStruct(conv_w.shape, F32), jax.ShapeDtypeStruct((1, dc), F32),
                   jax.ShapeDtypeStruct((1, ds), F32)],
        compiler_params=_params(),
    )(proj, proj, y, z, dmixed, w_glu, conv_w, gain_c, gain_s)


def _mix_bwd2(proj, dconv, conv_w, deps=()):
    t = proj.shape[0]
    dc = conv_w.shape[1]
    n_tiles = t // ROW_TILE

    def body(c_ref, v_ref, d_ref, halo_ref, cw_ref, o_ref):
        i = pl.program_id(0)
        d = d_ref[...]
        u1, u2 = _anticausal_taps(d, halo_ref[...], i == n_tiles - 1)
        cw = cw_ref[...]
        dcv = cw[2:3] * d + cw[1:2] * u1 + cw[0:1] * u2
        o_ref[:, :dc] = (dcv * v_ref[...]).astype(BF16)
        o_ref[:, dc:] = (dcv * c_ref[...]).astype(BF16)

    return pl.pallas_call(
        _ignoring_deps(body, 5, deps), name="mix_bwd2", grid=(n_tiles,),
        in_specs=[pl.BlockSpec((ROW_TILE, dc), lambda i: (i, 1)),
                  pl.BlockSpec((ROW_TILE, dc), lambda i: (i, 2)),
                  pl.BlockSpec((ROW_TILE, dc), lambda i: (i, 0)),
                  pl.BlockSpec((HALO_ROWS, dc), lambda i: (_next_halo(i, t), 0)),
                  pl.BlockSpec(conv_w.shape, lambda i: (0, 0))] + [ANY_SPEC] * len(deps),
        out_specs=pl.BlockSpec((ROW_TILE, 2 * dc), lambda i: (i, 0)),
        out_shape=jax.ShapeDtypeStruct((t, 2 * dc), BF16), compiler_params=_params(),
    )(proj, proj, dconv, dconv, conv_w, *deps)


def _stage_rows(stage_ref, row0, src, wrap=None):
    n, width = src.shape
    for c in range(0, width, STAGE_COLS):
        v = src[:, pl.ds(c, STAGE_COLS)].astype(F32)
        stage_ref[pl.ds(row0, n), pl.ds(c, STAGE_COLS)] = v if wrap is None else wrap(v)


def _conv_taps(stage_ref, fw_ref, fb_ref, c0, r, rows=ROW_CHUNK):
    cols = pl.ds(c0, FFN_COLS)
    x0, x1, x2 = (stage_ref[pl.ds(HALO_ROWS + r - k * N_SEG, rows), cols] for k in range(3))
    w = fw_ref[:, cols]
    return w[0:1] * x2 + w[1:2] * x1 + w[2:3] * x0 + fb_ref[:, cols], x2, x1, x0


def _gated_fwd(up_pre, fw, fb):
    t, f2 = up_pre.shape
    f = f2 // 2

    def body(up_ref, halo_ref, fw_ref, fb_ref, act_ref, stage_ref):
        first = pl.program_id(0) == 0
        _stage_rows(stage_ref, 0, halo_ref, lambda v: _wrap_prev_halo(v, first))
        _stage_rows(stage_ref, HALO_ROWS, up_ref)
        for c0 in range(0, f, FFN_COLS):
            for r in range(0, ROW_TILE, ROW_CHUNK):
                a = _conv_taps(stage_ref, fw_ref, fb_ref, c0, r)[0]
                val = _conv_taps(stage_ref, fw_ref, fb_ref, f + c0, r)[0]
                act_ref[pl.ds(r, ROW_CHUNK), pl.ds(c0, FFN_COLS)] = (a * jax.nn.sigmoid(a) * val).astype(BF16)

    const = lambda a: pl.BlockSpec(a.shape, lambda i: (0, 0))
    return pl.pallas_call(
        body, name="ffn_fwd", grid=(t // ROW_TILE,),
        in_specs=[pl.BlockSpec((ROW_TILE, f2), lambda i: (i, 0)),
                  pl.BlockSpec((HALO_ROWS, f2), lambda i: (_prev_halo(i, t), 0)), const(fw), const(fb)],
        out_specs=pl.BlockSpec((ROW_TILE, f), lambda i: (i, 0)),
        out_shape=jax.ShapeDtypeStruct((t, f), BF16),
        scratch_shapes=[pltpu.VMEM((HALO_ROWS + ROW_TILE, f2), F32)], compiler_params=_params(),
    )(up_pre, up_pre, fw, fb)


def _gated_bwd(up_pre, dact, fw, fb, deps=()):
    t, f2 = up_pre.shape
    f = f2 // 2
    n_tiles = t // ROW_TILE
    chunks = [(r, ROW_CHUNK) for r in range(0, ROW_TILE, ROW_CHUNK)] + [(ROW_TILE, HALO_ROWS)]

    def body(up_ref, prev_ref, next_ref, dact_in_ref, dact_next_ref, fw_ref, fb_ref,
             out_ref, dfw_ref, dfb_ref, dup_ref, stage_ref, dact_ref):
        first = pl.program_id(0) == 0
        last = pl.program_id(0) == n_tiles - 1

        @pl.when(first)
        def _():
            dfw_ref[...] = jnp.zeros_like(dfw_ref)
            dfb_ref[...] = jnp.zeros_like(dfb_ref)

        _stage_rows(dact_ref, 0, dact_in_ref)
        _stage_rows(dact_ref, ROW_TILE, dact_next_ref, lambda v: _wrap_next_halo(v, last))
        _stage_rows(stage_ref, 0, prev_ref, lambda v: _wrap_prev_halo(v, first))
        _stage_rows(stage_ref, HALO_ROWS, up_ref)
        _stage_rows(stage_ref, HALO_ROWS + ROW_TILE, next_ref, lambda v: _wrap_next_halo(v, last))
        fold = lambda v: sum(v[s:s + SUBLANES] for s in range(0, ROW_CHUNK, SUBLANES))
        for c0 in range(0, f, FFN_COLS):
            starts = (c0, f + c0)
            sums = [[jnp.zeros((SUBLANES, FFN_COLS), F32)] * 4 for _ in starts]
            for r, rows in chunks:
                taps = [_conv_taps(stage_ref, fw_ref, fb_ref, c, r, rows) for c in starts]
                a, val = taps[0][0], taps[1][0]
                da_ct = dact_ref[pl.ds(r, rows), pl.ds(c0, FFN_COLS)]
                sg = jax.nn.sigmoid(a)
                dup = (da_ct * val * sg * (1.0 + a * (1.0 - sg)), da_ct * a * sg)
                for k in range(2):
                    dup_ref[k, pl.ds(r, rows), :] = dup[k]
                    if r < ROW_TILE:
                        terms = (dup[k], dup[k] * taps[k][1], dup[k] * taps[k][2], dup[k] * taps[k][3])
                        sums[k] = [s + fold(v) for s, v in zip(sums[k], terms)]
            for k, c in enumerate(starts):
                cols = pl.ds(c, FFN_COLS)
                s_b, s_w0, s_w1, s_w2 = (jnp.sum(p, axis=0, keepdims=True) for p in sums[k])
                dfb_ref[:, cols] += s_b
                for tap, s_w in enumerate((s_w0, s_w1, s_w2)):
                    dfw_ref[tap:tap + 1, cols] += s_w
                cw = fw_ref[:, cols]
                for r in range(0, ROW_TILE, ROW_CHUNK):
                    d, u1, u2 = (dup_ref[k, pl.ds(r + s * N_SEG, ROW_CHUNK), :] for s in range(3))
                    out_ref[pl.ds(r, ROW_CHUNK), cols] = (cw[2:3] * d + cw[1:2] * u1 + cw[0:1] * u2).astype(BF16)

    tile = lambda w: pl.BlockSpec((ROW_TILE, w), lambda i: (i, 0))
    halo = lambda w, index: pl.BlockSpec((HALO_ROWS, w), lambda i: (index(i, t), 0))
    const = lambda a: pl.BlockSpec(a.shape, lambda i: (0, 0))
    return pl.pallas_call(
        _ignoring_deps(body, 7, deps), name="ffn_bwd", grid=(n_tiles,),
        in_specs=[tile(f2), halo(f2, _prev_halo), halo(f2, _next_halo), tile(f), halo(f, _next_halo),
                  const(fw), const(fb)] + [ANY_SPEC] * len(deps),
        out_specs=[tile(f2), const(fw), const(fb)],
        out_shape=[jax.ShapeDtypeStruct((t, f2), BF16), jax.ShapeDtypeStruct(fw.shape, F32),
                   jax.ShapeDtypeStruct(fb.shape, F32)],
        scratch_shapes=[pltpu.VMEM((2, ROW_TILE + HALO_ROWS, FFN_COLS), F32),
                        pltpu.VMEM((HALO_ROWS + ROW_TILE + HALO_ROWS, f2), F32),
                        pltpu.VMEM((ROW_TILE + HALO_ROWS, f), F32)],
        compiler_params=_params(),
    )(up_pre, up_pre, up_pre, dact, dact, fw, fb, *deps)


def _to_segments(a):
    t, c = a.shape
    return a.reshape(N_SEG, t // N_SEG, c).transpose(1, 0, 2).reshape(t, c)


def _from_segments(a):
    t, c = a.shape
    return a.reshape(t // N_SEG, N_SEG, c).transpose(1, 0, 2).reshape(t, c)


def _cmul(ar, ai, br, bi):
    return ar * br - ai * bi, ar * bi + ai * br


def _cpow(ar, ai, n):
    out = None
    while n:
        if n & 1:
            out = (ar, ai) if out is None else _cmul(out[0], out[1], ar, ai)
        ar, ai = _cmul(ar, ai, ar, ai)
        n >>= 1
    return out


def _segment_carries(pr, pi, fr, fi, forward):
    row = lax.broadcasted_iota(jnp.int32, fr.shape, 0)
    edge = row == (0 if forward else N_SEG - 1)
    shift = 1 if forward else N_SEG - 1
    sr, si = jnp.zeros_like(fr), jnp.zeros_like(fi)
    for _ in range(N_SEG - 1):
        tr, ti = _cmul(pr, pi, sr, si)
        sr = jnp.where(edge, 0.0, pltpu.roll(tr + fr, shift, 0))
        si = jnp.where(edge, 0.0, pltpu.roll(ti + fi, shift, 0))
    return sr, si


def _rows(i):
    if isinstance(i, int):
        return pl.ds(i * SUBLANES, SUBLANES)
    return pl.ds(pl.multiple_of(i * SUBLANES, SUBLANES), SUBLANES)


def _scan_steps(n, body, init):
    def trip(k, carry):
        for s in range(SCAN_UNROLL):
            carry = body(k * SCAN_UNROLL + s, carry)
        return carry

    carry = lax.fori_loop(0, n // SCAN_UNROLL, trip, init)
    for i in range(n - n % SCAN_UNROLL, n):
        carry = body(i, carry)
    return carry


def _s5_fwd(proj, u_col, bb_re, bb_im, a_re, a_im, cc_re, cc_im, d_skip):
    t = proj.shape[0]
    nch, _, cs = bb_re.shape
    ds = nch * SSM_CHUNK
    u_blk = u_col // SSM_CHUNK
    steps = t // N_SEG

    def body(u_ref, bbr_ref, bbi_ref, ar_ref, ai_ref, ccr_ref, cci_ref, d_ref, sr_ref, si_ref, y_ref):
        ub = u_ref[...].astype(BF16)
        sr_ref[...] = _dot(ub, bbr_ref[...], NN)
        si_ref[...] = _dot(ub, bbi_ref[...], NN)
        ar = jnp.broadcast_to(ar_ref[...], (N_SEG, cs))
        ai = jnp.broadcast_to(ai_ref[...], (N_SEG, cs))
        zero = jnp.zeros((N_SEG, cs), F32)

        def totals(i, carry):
            tr, ti = _cmul(ar, ai, *carry)
            return tr + sr_ref[_rows(i), :], ti + si_ref[_rows(i), :]

        fr, fi = _scan_steps(steps, totals, (zero, zero))
        s0r, s0i = _segment_carries(*_cpow(ar, ai, steps), fr, fi, True)

        def scan(i, carry):
            tr, ti = _cmul(ar, ai, *carry)
            nr, ni = tr + sr_ref[_rows(i), :], ti + si_ref[_rows(i), :]
            sr_ref[_rows(i), :] = nr
            si_ref[_rows(i), :] = ni
            return nr, ni

        _scan_steps(steps, scan, (s0r, s0i))
        y_ref[...] = (_dot(sr_ref[...].astype(BF16), ccr_ref[...], NT)
                      - _dot(si_ref[...].astype(BF16), cci_ref[...], NT)
                      + d_ref[...] * u_ref[...])

    chunk3 = lambda r, c: pl.BlockSpec((None, r, c), lambda j: (j, 0, 0))
    return pl.pallas_call(
        body, name="s5_fwd", grid=(nch,),
        in_specs=[pl.BlockSpec((t, SSM_CHUNK), lambda j: (0, j + u_blk)),
                  chunk3(SSM_CHUNK, cs), chunk3(SSM_CHUNK, cs), chunk3(1, cs), chunk3(1, cs),
                  chunk3(SSM_CHUNK, cs), chunk3(SSM_CHUNK, cs), chunk3(1, SSM_CHUNK)],
        out_specs=[pl.BlockSpec((t, cs), lambda j: (0, j)), pl.BlockSpec((t, cs), lambda j: (0, j)),
                   pl.BlockSpec((t, SSM_CHUNK), lambda j: (0, j))],
        out_shape=[jax.ShapeDtypeStruct((t, nch * cs), F32), jax.ShapeDtypeStruct((t, nch * cs), F32),
                   jax.ShapeDtypeStruct((t, ds), F32)],
        compiler_params=_params(),
    )(proj, bb_re, bb_im, a_re, a_im, cc_re, cc_im, d_skip)


def _s5_bwd(dy, proj, u_col, s_re, s_im, bb_re, bb_im, a_re, a_im, cc_re, cc_im, d_skip, gpc):
    t, ds = dy.shape
    nch, _, cs = bb_re.shape
    u_blk = u_col // SSM_CHUNK
    steps = t // N_SEG

    def body(dy_ref, u_ref, sr_ref, si_ref, bbr_ref, bbi_ref, ar_ref, ai_ref, ccr_ref, cci_ref, d_ref,
             du_ref, dbbr_ref, dbbi_ref, dar_ref, dai_ref, dccr_ref, dcci_ref, dd_ref, gr_ref, gi_ref):
        dyv = dy_ref[...]
        dyb = dyv.astype(BF16)
        gr_ref[...] = _dot(dyb, ccr_ref[...], NN)
        gi_ref[...] = -_dot(dyb, cci_ref[...], NN)
        ar = jnp.broadcast_to(ar_ref[...], (N_SEG, cs))
        ai = -jnp.broadcast_to(ai_ref[...], (N_SEG, cs))
        zero = jnp.zeros((N_SEG, cs), F32)

        def totals(k, carry):
            i = steps - 1 - k
            tr, ti = _cmul(ar, ai, *carry)
            return tr + gr_ref[_rows(i), :], ti + gi_ref[_rows(i), :]

        fr, fi = _scan_steps(steps, totals, (zero, zero))
        e0r, e0i = _segment_carries(*_cpow(ar, ai, steps), fr, fi, False)

        def step(i, gr, gi, pr, pi, acc_r, acc_i):
            tr, ti = _cmul(ar, ai, gr, gi)
            nr, ni = tr + gr_ref[_rows(i), :], ti + gi_ref[_rows(i), :]
            gr_ref[_rows(i), :] = nr
            gi_ref[_rows(i), :] = ni
            return nr, ni, acc_r + nr * pr + ni * pi, acc_i + ni * pr - nr * pi

        def scan(k, carry):
            i = steps - 1 - k
            gr, gi, acc_r, acc_i = carry
            return step(i, gr, gi, sr_ref[_rows(i - 1), :], si_ref[_rows(i - 1), :], acc_r, acc_i)

        gr, gi, acc_r, acc_i = _scan_steps(steps - 1, scan, (e0r, e0i, zero, zero))
        row = lax.broadcasted_iota(jnp.int32, (N_SEG, cs), 0)
        last = _rows(steps - 1)
        pr = jnp.where(row == 0, 0.0, pltpu.roll(sr_ref[last, :], 1, 0))
        pi = jnp.where(row == 0, 0.0, pltpu.roll(si_ref[last, :], 1, 0))
        _, _, acc_r, acc_i = step(0, gr, gi, pr, pi, acc_r, acc_i)
        dar_ref[...] = jnp.sum(acc_r, axis=0, keepdims=True)
        dai_ref[...] = jnp.sum(acc_i, axis=0, keepdims=True)

        uv = u_ref[...]
        ub = uv.astype(BF16)
        grb = gr_ref[...].astype(BF16)
        gib = gi_ref[...].astype(BF16)
        du = d_ref[...] * dyv + _dot(grb, bbr_ref[...], NT) + _dot(gib, bbi_ref[...], NT)
        du_ref[...] = du.astype(BF16)
        def put_groups(ref, full):
            for gl in range(gpc):
                ref[gl] = full[gl * hb:(gl + 1) * hb, gl * pb:(gl + 1) * pb]

        put_groups(dbbr_ref, _dot(ub, grb, TN))
        put_groups(dbbi_ref, _dot(ub, gib, TN))
        put_groups(dccr_ref, _dot(dyb, sr_ref[...].astype(BF16), TN))
        put_groups(dcci_ref, -_dot(dyb, si_ref[...].astype(BF16), TN))
        dd_ref[...] = jnp.sum(dyv * uv, axis=0, keepdims=True)

    hb, pb = SSM_CHUNK // gpc, cs // gpc
    groups = pl.BlockSpec((None, gpc, hb, pb), lambda j: (j, 0, 0, 0))
    groups_shape = jax.ShapeDtypeStruct((nch, gpc, hb, pb), F32)
    chunk3 = lambda r, c: pl.BlockSpec((None, r, c), lambda j: (j, 0, 0))
    cols = lambda w: pl.BlockSpec((t, w), lambda j: (0, j))
    return pl.pallas_call(
        body, name="s5_bwd", grid=(nch,),
        in_specs=[cols(SSM_CHUNK), pl.BlockSpec((t, SSM_CHUNK), lambda j: (0, j + u_blk)), cols(cs), cols(cs),
                  chunk3(SSM_CHUNK, cs), chunk3(SSM_CHUNK, cs), chunk3(1, cs), chunk3(1, cs),
                  chunk3(SSM_CHUNK, cs), chunk3(SSM_CHUNK, cs), chunk3(1, SSM_CHUNK)],
        out_specs=[cols(SSM_CHUNK), groups, groups, chunk3(1, cs), chunk3(1, cs), groups, groups,
                   chunk3(1, SSM_CHUNK)],
        out_shape=[jax.ShapeDtypeStruct((t, ds), BF16), groups_shape, groups_shape,
                   jax.ShapeDtypeStruct((nch, 1, cs), F32), jax.ShapeDtypeStruct((nch, 1, cs), F32),
                   groups_shape, groups_shape, jax.ShapeDtypeStruct((nch, 1, SSM_CHUNK), F32)],
        scratch_shapes=[pltpu.VMEM((t, cs), F32), pltpu.VMEM((t, cs), F32)],
        compiler_params=_params(),
    )(dy, proj, s_re, s_im, bb_re, bb_im, a_re, a_im, cc_re, cc_im, d_skip)


def _discretize(lr, li, log_dt, br, bi):
    dt = jnp.exp(log_dt)
    mag = jnp.exp(lr * dt)
    ang = li * dt
    a_re = mag * jnp.cos(ang)
    a_im = mag * jnp.sin(ang)
    den = lr * lr + li * li
    nr = a_re - 1.0
    f_re = (nr * lr + a_im * li) / den
    f_im = (a_im * lr - nr * li) / den
    return a_re, a_im, f_re * br - f_im * bi, f_re * bi + f_im * br


def _whole(shape):
    return pl.BlockSpec(shape, lambda: (0,) * len(shape))


def _disc_fwd(lr, li, log_dt, br, bi, cr, ci, nch, gpc, deps=()):
    gh, n_state = br.shape
    grp = gh // (nch * gpc)
    rows, cs = gpc * grp, gpc * n_state

    def body(lr_ref, li_ref, dt_ref, br_ref, bi_ref, cr_ref, ci_ref,
             ar_ref, ai_ref, bbr_ref, bbi_ref, ccr_ref, cci_ref, wide_ref):
        a_re, a_im, bb_re, bb_im = _discretize(lr_ref[...], li_ref[...], dt_ref[...], br_ref[...], bi_ref[...])
        for ref, val in ((ar_ref, a_re), (ai_ref, a_im)):
            for j in range(nch):
                for gl in range(gpc):
                    g0 = (j * gpc + gl) * grp
                    ref[j, :, gl * n_state:(gl + 1) * n_state] = val[g0:g0 + 1]
        for ref, val in ((bbr_ref, bb_re), (bbi_ref, bb_im), (ccr_ref, cr_ref[...]), (cci_ref, ci_ref[...])):
            wide_ref[...] = jnp.zeros_like(wide_ref)
            for j in range(nch):
                for gl in range(gpc):
                    g0 = (j * gpc + gl) * grp
                    wide_ref[j, gl * grp:(gl + 1) * grp, gl * n_state:(gl + 1) * n_state] = val[g0:g0 + grp]
            ref[...] = wide_ref[...].astype(BF16)

    args = (lr, li, log_dt, br, bi, cr, ci)
    row_shape, wide_shape = (nch, 1, cs), (nch, rows, cs)
    outs = [jax.ShapeDtypeStruct(row_shape, F32)] * 2 + [jax.ShapeDtypeStruct(wide_shape, BF16)] * 4
    return pl.pallas_call(
        _ignoring_deps(body, 7, deps), name="disc_fwd",
        in_specs=[_whole(a.shape) for a in args] + [ANY_SPEC] * len(deps),
        out_specs=[_whole(o.shape) for o in outs], out_shape=outs,
        scratch_shapes=[pltpu.VMEM(wide_shape, F32)],
    )(*args, *deps)


def _disc_bwd(lr, li, log_dt, br, bi, dar, dai, dbbr, dbbi):
    def body(lr_ref, li_ref, dt_ref, br_ref, bi_ref, dar_ref, dai_ref, dbbr_ref, dbbi_ref,
             dlr_ref, dli_ref, ddt_ref, dbr_ref, dbi_ref):
        _, vjp = jax.vjp(_discretize, lr_ref[...], li_ref[...], dt_ref[...], br_ref[...], bi_ref[...])
        grads = vjp((dar_ref[...], dai_ref[...], dbbr_ref[...], dbbi_ref[...]))
        for ref, val in zip((dlr_ref, dli_ref, ddt_ref, dbr_ref, dbi_ref), grads):
            ref[...] = val

    args = (lr, li, log_dt, br, bi, dar, dai, dbbr, dbbi)
    outs = (lr, li, log_dt, br, bi)
    return pl.pallas_call(
        body, name="disc_bwd", in_specs=[_whole(a.shape) for a in args],
        out_specs=[_whole(a.shape) for a in outs],
        out_shape=[jax.ShapeDtypeStruct(a.shape, F32) for a in outs],
    )(*args)


def _adamw(w, g, m, v):
    m = ADAM_B1 * m + (1.0 - ADAM_B1) * g
    v = ADAM_B2 * v + (1.0 - ADAM_B2) * (g * g)
    m_hat = m / ADAM_BC1
    v_hat = v / ADAM_BC2
    delta = -ADAM_LR * (m_hat / (jnp.sqrt(v_hat) + ADAM_EPS) + ADAM_WD * w)
    return delta, m, v


def _adamw_reduce(name, parts, w, m, v):
    _, r, c = parts.shape
    tr = r
    for cand in (256, 176, 128):
        if r % cand == 0:
            tr = cand
            break

    def body(p_ref, w_ref, m_ref, v_ref, g_ref, d_ref, nm_ref, nv_ref):
        g = p_ref[0].astype(F32)
        for k in range(1, N_DEV):
            g = g + p_ref[k].astype(F32)
        delta, nm, nv = _adamw(w_ref[...], g, m_ref[...], v_ref[...])
        g_ref[...] = g
        d_ref[...] = delta
        nm_ref[...] = nm
        nv_ref[...] = nv

    blk = pl.BlockSpec((tr, c), lambda i: (i, 0))
    return pl.pallas_call(
        body, name=name, grid=(r // tr,),
        in_specs=[pl.BlockSpec((N_DEV, tr, c), lambda i: (0, i, 0)), blk, blk, blk],
        out_specs=[blk] * 4, out_shape=[jax.ShapeDtypeStruct((r, c), F32)] * 4,
        compiler_params=_params(),
    )(parts, w, m, v)


def _sum_parts(name, parts):
    _, r, c = parts.shape

    def body(p_ref, o_ref):
        g = p_ref[0]
        for k in range(1, N_DEV):
            g = g + p_ref[k]
        o_ref[...] = g

    return pl.pallas_call(
        body, name=name, in_specs=[_whole(parts.shape)], out_specs=_whole((r, c)),
        out_shape=jax.ShapeDtypeStruct((r, c), F32), compiler_params=_params(),
    )(parts)


def _adamw_many(name, grads, ws, ms, vs):
    n = len(grads)

    def body(*refs):
        ins, outs = refs[:4 * n], refs[4 * n:]
        for i in range(n):
            g, w, m, v = (ins[j * n + i][...] for j in range(4))
            for ref, val in zip((outs[i], outs[n + i], outs[2 * n + i]), _adamw(w, g, m, v)):
                ref[...] = val

    args = (*grads, *ws, *ms, *vs)
    outs = pl.pallas_call(
        body, name=name, in_specs=[_whole(a.shape) for a in args],
        out_specs=[_whole(a.shape) for a in ws] * 3,
        out_shape=[jax.ShapeDtypeStruct(a.shape, F32) for a in ws] * 3, compiler_params=_params(),
    )(*args)
    return outs[:n], outs[n:2 * n], outs[2 * n:]


def _pack(arrays, rows):
    flat = jnp.concatenate([a.reshape(-1) for a in arrays])
    return jnp.pad(flat, (0, rows * LANES - flat.shape[0])).reshape(rows, LANES)


def _unpack(packed, shapes):
    flat = packed.reshape(-1)
    out, off = [], 0
    for s in shapes:
        n = math.prod(s)
        out.append(flat[off:off + n].reshape(s))
        off += n
    return out


def _packed_rows(shapes):
    n = sum(math.prod(s) for s in shapes)
    return -(-n // (SUBLANES * LANES)) * SUBLANES


def kernel(x, meta_tokens, norm_mix_g, w_in, conv_w, ssm_lam_re, ssm_lam_im, ssm_log_dt, ssm_b_re, ssm_b_im, ssm_c_re, ssm_c_im, ssm_d, ssm_w_glu, gain_conv_out, gain_ssm_out, w_out, norm_ffn_g, w_up, ffn_conv_w, ffn_conv_b, w_down, norm_final_g, loss_target, m_meta_tokens, m_norm_mix_g, m_w_in, m_conv_w, m_ssm_lam_re, m_ssm_lam_im, m_ssm_log_dt, m_ssm_b_re, m_ssm_b_im, m_ssm_c_re, m_ssm_c_im, m_ssm_d, m_ssm_w_glu, m_gain_conv_out, m_gain_ssm_out, m_w_out, m_norm_ffn_g, m_w_up, m_ffn_conv_w, m_ffn_conv_b, m_w_down, m_norm_final_g, v_meta_tokens, v_norm_mix_g, v_w_in, v_conv_w, v_ssm_lam_re, v_ssm_lam_im, v_ssm_log_dt, v_ssm_b_re, v_ssm_b_im, v_ssm_c_re, v_ssm_c_im, v_ssm_d, v_ssm_w_glu, v_gain_conv_out, v_gain_ssm_out, v_w_out, v_norm_ffn_g, v_w_up, v_ffn_conv_w, v_ffn_conv_b, v_w_down, v_norm_final_g):
    weights = dict(meta_tokens=meta_tokens, norm_mix_g=norm_mix_g, w_in=w_in, conv_w=conv_w, ssm_lam_re=ssm_lam_re, ssm_lam_im=ssm_lam_im, ssm_log_dt=ssm_log_dt, ssm_b_re=ssm_b_re, ssm_b_im=ssm_b_im, ssm_c_re=ssm_c_re, ssm_c_im=ssm_c_im, ssm_d=ssm_d, ssm_w_glu=ssm_w_glu, gain_conv_out=gain_conv_out, gain_ssm_out=gain_ssm_out, w_out=w_out, norm_ffn_g=norm_ffn_g, w_up=w_up, ffn_conv_w=ffn_conv_w, ffn_conv_b=ffn_conv_b, w_down=w_down, norm_final_g=norm_final_g)
    mom_m = dict(meta_tokens=m_meta_tokens, norm_mix_g=m_norm_mix_g, w_in=m_w_in, conv_w=m_conv_w, ssm_lam_re=m_ssm_lam_re, ssm_lam_im=m_ssm_lam_im, ssm_log_dt=m_ssm_log_dt, ssm_b_re=m_ssm_b_re, ssm_b_im=m_ssm_b_im, ssm_c_re=m_ssm_c_re, ssm_c_im=m_ssm_c_im, ssm_d=m_ssm_d, ssm_w_glu=m_ssm_w_glu, gain_conv_out=m_gain_conv_out, gain_ssm_out=m_gain_ssm_out, w_out=m_w_out, norm_ffn_g=m_norm_ffn_g, w_up=m_w_up, ffn_conv_w=m_ffn_conv_w, ffn_conv_b=m_ffn_conv_b, w_down=m_w_down, norm_final_g=m_norm_final_g)
    mom_v = dict(meta_tokens=v_meta_tokens, norm_mix_g=v_norm_mix_g, w_in=v_w_in, conv_w=v_conv_w, ssm_lam_re=v_ssm_lam_re, ssm_lam_im=v_ssm_lam_im, ssm_log_dt=v_ssm_log_dt, ssm_b_re=v_ssm_b_re, ssm_b_im=v_ssm_b_im, ssm_c_re=v_ssm_c_re, ssm_c_im=v_ssm_c_im, ssm_d=v_ssm_d, ssm_w_glu=v_ssm_w_glu, gain_conv_out=v_gain_conv_out, gain_ssm_out=v_gain_ssm_out, w_out=v_w_out, norm_ffn_g=v_norm_ffn_g, w_up=v_w_up, ffn_conv_w=v_ffn_conv_w, ffn_conv_b=v_ffn_conv_b, w_down=v_w_down, norm_final_g=v_norm_final_g)
    names = list(weights)

    n_meta, d_meta = meta_tokens.shape
    seq, d = x.shape[1], x.shape[2]
    rows_used = n_meta + seq
    t = -(-rows_used // ROW_TILE) * ROW_TILE
    d_in_s = w_in.shape[2]
    dc_s = conv_w.shape[2]
    dc = dc_s * N_DEV
    ds = ssm_w_glu.shape[2]
    n_groups, n_state, grp = ssm_b_re.shape[1:]
    ns = n_groups * n_state
    nch = ds // SSM_CHUNK
    gpc = n_groups // nch
    ff_s = w_up.shape[2]
    dn_s = w_down.shape[1]
    assert 3 * dc + ds == d_in_s * N_DEV and 2 * dn_s == ff_s and t % (N_SEG * SUBLANES) == 0

    small_shard = jnp.concatenate([meta_tokens.reshape(-1), conv_w.reshape(-1), ffn_conv_w.reshape(-1)])
    n_small = small_shard.shape[0]
    small_rows = -(-n_small // LANES)
    small_shard = jnp.pad(small_shard, (0, small_rows * LANES - n_small)).reshape(small_rows, LANES)
    ag, ag_token = _gather2_start("gather_weights_start", [
        small_shard, w_in[0].astype(BF16), ssm_w_glu[0].astype(BF16), w_out[0].astype(BF16),
        jnp.swapaxes(w_up[0], 0, 1).astype(BF16), w_down[0].astype(BF16)])
    fb = ffn_conv_b

    gh = n_groups * grp
    per_h = lambda a: jnp.broadcast_to(a.reshape(n_groups, 1, -1), (n_groups, grp, n_state)).reshape(gh, n_state)
    ghp = lambda a: a.transpose(0, 1, 3, 2).reshape(gh, n_state)
    lr, li, log_dt_e = per_h(ssm_lam_re), per_h(ssm_lam_im), per_h(ssm_log_dt)
    br, bi = ghp(ssm_b_re), ghp(ssm_b_im)
    a_re_c, a_im_c, bbm_re, bbm_im, ccm_re, ccm_im = _disc_fwd(
        lr, li, log_dt_e, br, bi, ssm_c_re.reshape(gh, n_state), ssm_c_im.reshape(gh, n_state), nch, gpc,
        deps=(ag_token,))
    d_skip = ssm_d.reshape(nch, 1, SSM_CHUNK)

    target = _to_segments(jnp.pad(loss_target[0] + ag_token[0, 0], ((n_meta, t - rows_used), (0, 0))))
    token = _gather2_forward("gather_weights_forward_first", ag, (0, 1), (bbm_im, ccm_im, target))
    (g_small,) = _gather2_wait("gather_weights_wait_small", ag, (0,), token)
    g_small = g_small.reshape(N_DEV, -1)
    o1 = n_meta * d_meta
    o2 = o1 + 3 * dc_s
    meta_full = g_small[:, :o1].reshape(N_DEV, n_meta, d_meta).transpose(1, 0, 2).reshape(n_meta, d)
    conv_w_f = g_small[:, o1:o2].reshape(N_DEV, 3, dc_s).transpose(1, 0, 2).reshape(3, dc)
    fw = g_small[:, o2:o2 + 3 * ff_s].reshape(N_DEV, 3, ff_s).transpose(1, 0, 2).reshape(3, N_DEV * ff_s)
    h0 = _to_segments(jnp.concatenate([meta_full, x[0], jnp.zeros((t - rows_used, d), F32)], axis=0))
    full_t = lambda w: pl.BlockSpec((t, w), lambda *_: (0, 0))

    hn1 = _rms_fwd("norm_mix", h0, norm_mix_g)
    (g_in,) = _gather2_wait("gather_weights_wait_in", ag, (1,), hn1)
    proj = _mm("proj", hn1, g_in, dims=NN, grid=(N_DEV,), a_spec=full_t(d),
               b_spec=pl.BlockSpec((None, d, d_in_s), lambda j: (j, 0, 0)),
               o_spec=pl.BlockSpec((t, d_in_s), lambda j: (0, j)),
               out_shape=jax.ShapeDtypeStruct((t, N_DEV * d_in_s), F32))
    s_re, s_im, y_ssm = _s5_fwd(proj, 3 * dc, bbm_re, bbm_im, a_re_c, a_im_c, ccm_re, ccm_im, d_skip)
    token = _gather2_forward("gather_weights_forward_up", ag, (2, 3, 4), (y_ssm,))
    g_glu, g_out = _gather2_wait("gather_weights_wait_mix", ag, (2, 3), token)
    w_out_f = g_out.reshape(-1, d)
    w_glu_f = g_glu.reshape(ds, ds)
    mixed, z_glu, h1, hn2 = _mix_fwd(proj, y_ssm, w_glu_f, conv_w_f, gain_conv_out, gain_ssm_out,
                                     h0, w_out_f, norm_ffn_g)
    tn_out = OUT_TILE
    token = _gather2_forward("gather_weights_forward_down", ag, (5,), (hn2,))
    (g_up,) = _gather2_wait("gather_weights_wait_up", ag, (4,), token)
    f2 = N_DEV * ff_s
    pair = 2 * ff_s
    w_up_t = g_up.reshape(f2, d)
    up_pre = _mm("up_proj", hn2, w_up_t, dims=NT, grid=(f2 // pair,), a_spec=full_t(d),
                 b_spec=pl.BlockSpec((pair, d), lambda j: (j, 0)),
                 o_spec=pl.BlockSpec((t, pair), lambda j: (0, j)),
                 out_shape=jax.ShapeDtypeStruct((t, f2), BF16))
    act = _gated_fwd(up_pre, fw, fb)
    (g_down,) = _gather2_wait("gather_weights_wait_down", ag, (5,), act)
    w_down_f = g_down.reshape(f2 // 2, d)

    dh2, dh2_b, loss_part, d_norm_final = _down_loss_bwd(
        act, w_down_f, h1, target, norm_final_g.reshape(1, d), n_meta, rows_used)
    dw_down = _mm("down_wgrad", act, dh2_b, dims=TN, grid=(f2 // 2 // pair,),
                  a_spec=pl.BlockSpec((t, pair), lambda j: (0, j)), b_spec=full_t(d),
                  o_spec=pl.BlockSpec((pair, d), lambda j: (j, 0)),
                  out_shape=jax.ShapeDtypeStruct((f2 // 2, d), BF16))
    ex_down, token = _send_start("exchange_down_start", [dw_down.reshape(N_DEV, dn_s, d)], gather=False)
    dact = _mm("down_dgrad", dh2_b, w_down_f, dims=NT, grid=(f2 // 2 // pair,), a_spec=full_t(d),
               b_spec=pl.BlockSpec((pair, d), lambda j: (j, 0)),
               o_spec=pl.BlockSpec((t, pair), lambda j: (0, j)),
               out_shape=jax.ShapeDtypeStruct((t, f2 // 2), BF16))
    dup_pre, d_fw, d_fb = _gated_bwd(up_pre, dact, fw, fb, deps=(token,))
    dw_up = _mm("up_wgrad", dup_pre, hn2, dims=TN, grid=(f2 // pair,),
                a_spec=pl.BlockSpec((t, pair), lambda j: (0, j)), b_spec=full_t(d),
                o_spec=pl.BlockSpec((pair, d), lambda j: (j, 0)),
                out_shape=jax.ShapeDtypeStruct((f2, d), BF16))
    ex_up, token = _send_start("exchange_up_start", [dw_up.reshape(N_DEV, ff_s, d)], gather=False)
    dh1, dh1_b, d_norm_ffn = _dgrad_rms_bwd("up_dgrad_norm_bwd", dh2, dup_pre, w_up_t, NN, h1, norm_ffn_g,
                                            deps=(token,))
    dmixed = _mm("out_dgrad", dh1_b, w_out_f, dims=NT, grid=((dc + ds) // tn_out,), a_spec=full_t(d),
                 b_spec=pl.BlockSpec((tn_out, d), lambda i: (i, 0)),
                 o_spec=pl.BlockSpec((t, tn_out), lambda i: (0, i)),
                 out_shape=jax.ShapeDtypeStruct((t, dc + ds), F32))
    dw_out = _mm("out_wgrad", mixed, dh1_b, dims=TN, grid=((dc + ds) // tn_out,),
                 a_spec=pl.BlockSpec((t, tn_out), lambda i: (0, i)), b_spec=full_t(d),
                 o_spec=pl.BlockSpec((tn_out, d), lambda i: (i, 0)),
                 out_shape=jax.ShapeDtypeStruct((dc + ds, d), BF16))
    db_gate, dconv, dy_ssm, d_wglu, d_conv_w, d_gain_c, d_gain_s = _mix_bwd1(
        proj, y_ssm, z_glu, dmixed, w_glu_f, conv_w_f, gain_conv_out, gain_ssm_out)
    (du, d_bbm_re, d_bbm_im, d_a_re, d_a_im, d_ccm_re, d_ccm_im, d_dskip) = _s5_bwd(
        dy_ssm, proj, 3 * dc, s_re, s_im, bbm_re, bbm_im, a_re_c, a_im_c, ccm_re, ccm_im, d_skip, gpc)

    from_bb = from_cc = lambda a: a.reshape(gh, n_state)
    first_h = lambda a: jnp.pad(a.reshape(n_groups, 1, n_state), ((0, 0), (0, grp - 1), (0, 0))).reshape(gh, n_state)
    over_h = lambda a: a.reshape(n_groups, grp, n_state).sum(axis=1)
    d_lr, d_li, d_dt_e, d_br, d_bi = _disc_bwd(
        lr, li, log_dt_e, br, bi, first_h(d_a_re), first_h(d_a_im), from_bb(d_bbm_re), from_bb(d_bbm_im))

    rep2d = dict(
        ssm_lam_re=(n_groups, n_state), ssm_lam_im=(n_groups, n_state), ssm_log_dt=(1, n_groups),
        ssm_b_re=(gh, n_state), ssm_b_im=(gh, n_state), ssm_c_re=(gh, n_state),
        ssm_c_im=(gh, n_state), ssm_d=(n_groups, grp), gain_conv_out=(1, dc),
        gain_ssm_out=(1, ds), norm_ffn_g=(1, d), ffn_conv_b=(1, N_DEV * ff_s), norm_final_g=(1, d))
    rep_names = list(rep2d)
    rep_grads = dict(
        ssm_lam_re=over_h(d_lr), ssm_lam_im=over_h(d_li), ssm_log_dt=over_h(d_dt_e).sum(axis=1),
        ssm_b_re=d_br, ssm_b_im=d_bi, ssm_c_re=from_cc(d_ccm_re), ssm_c_im=from_cc(d_ccm_im),
        ssm_d=d_dskip, gain_conv_out=d_gain_c, gain_ssm_out=d_gain_s, norm_ffn_g=d_norm_ffn,
        ffn_conv_b=d_fb, norm_final_g=d_norm_final)
    rep_shapes = [rep2d[n] for n in rep_names] + [(1, 1)]
    rep_rows = _packed_rows(rep_shapes)
    rep_pack = _pack([rep_grads[n] for n in rep_names] + [loss_part], rep_rows)
    ex_mix, token = _send_start("exchange_mix_start", [
        rep_pack, dw_out.reshape(N_DEV, -1, d), d_wglu.astype(BF16).reshape(N_DEV, -1, ds),
        d_conv_w.reshape(3, N_DEV, dc_s).transpose(1, 0, 2),
        d_fw.reshape(3, N_DEV, ff_s).transpose(1, 0, 2)],
        gather=[True, False, False, False, False])
    dcdv = _mix_bwd2(proj, dconv, conv_w_f, deps=(token,))
    dproj = jnp.concatenate([db_gate, dcdv, du], axis=1)
    d_in = N_DEV * d_in_s
    w_in_rows = g_in.transpose(1, 0, 2).reshape(d, d_in)
    dw_in = _mm("proj_wgrad", hn1, dproj, dims=TN, grid=(N_DEV,), a_spec=full_t(d),
                b_spec=pl.BlockSpec((t, d_in_s), lambda j: (0, j)),
                o_spec=pl.BlockSpec((None, d, d_in_s), lambda j: (j, 0, 0)),
                out_shape=jax.ShapeDtypeStruct((N_DEV, d, d_in_s), BF16))
    dh0, _, d_norm_mix = _dgrad_rms_bwd("proj_dgrad_norm_bwd", dh1, dproj, w_in_rows, NT, h0, norm_mix_g)
    dh0 = _from_segments(dh0)
    grad_x = dh0[n_meta:rows_used][None]
    d_meta_b = dh0[:n_meta].reshape(n_meta, N_DEV, d_meta).transpose(1, 0, 2)
    ex_in, ex_in_token = _send_start("exchange_in_start", [dw_in, d_meta_b, d_norm_mix],
                                     gather=[False, False, True])

    shard_out = {}

    def update(n, parts, transposed=False):
        sh = weights[n].shape
        two_d = lambda a: (jnp.swapaxes(a[0], 0, 1) if transposed else a.reshape(parts.shape[1:]))
        res = _adamw_reduce("adamw_" + n, parts, two_d(weights[n]), two_d(mom_m[n]), two_d(mom_v[n]))
        shard_out[n] = [(jnp.swapaxes(r, 0, 1) if transposed else r).reshape(sh) for r in res]
        return res[0]

    (p_down,) = _send_wait("exchange_down_wait", ex_down, (0,), ex_in_token)
    done = update("w_down", p_down)
    (p_up,) = _send_wait("exchange_up_wait", ex_up, (0,), done)
    done = update("w_up", p_up, transposed=True)
    (rep_parts,) = _send_wait("gather_small_grads_wait", ex_mix, (0,), done)
    rep_sum = _sum_parts("sum_small_grads", rep_parts)
    *rep_g, loss = _unpack(rep_sum, rep_shapes)
    loss = loss.reshape(())
    swapped = ("ssm_b_re", "ssm_b_im")
    to2d = lambda n, a: ghp(a) if n in swapped else a.reshape(rep2d[n])
    from2d = lambda n, a: (a.reshape(1, n_groups, grp, n_state).transpose(0, 1, 3, 2) if n in swapped
                           else a.reshape(weights[n].shape))
    as2d = lambda tree: [to2d(n, tree[n]) for n in rep_names]
    rep_res = _adamw_many("adamw_replicated", rep_g, as2d(weights), as2d(mom_m), as2d(mom_v))
    for i, n in enumerate(rep_names):
        shard_out[n] = [from2d(n, r) for r in (rep_g[i], *(res[i] for res in rep_res))]

    p_out, p_glu, p_cw, p_fw = _send_wait("exchange_mix_wait", ex_mix, (1, 2, 3, 4), rep_sum)
    update("w_out", p_out)
    update("ssm_w_glu", p_glu)
    update("conv_w", p_cw)
    done = update("ffn_conv_w", p_fw)
    p_in, p_meta, p_nm = _send_wait("exchange_in_wait", ex_in, (0, 1, 2), done)
    update("w_in", p_in)
    update("meta_tokens", p_meta)
    update("norm_mix_g", p_nm)

    grads = [shard_out[n][0] for n in names]
    deltas = [shard_out[n][1] for n in names]
    new_m = [shard_out[n][2] for n in names]
    new_v = [shard_out[n][3] for n in names]
    return (loss, grad_x, *grads, *deltas, *new_m, *new_v)
```

```python
import math

import jax
import jax.numpy as jnp
from jax import lax
from jax.experimental import pallas as pl
from jax.experimental.pallas import tpu as pltpu

F32 = jnp.float32
BF16 = jnp.bfloat16
MESH = pl.DeviceIdType.MESH

N_DEV = 8
RMS_EPS = 1e-6
ADAM_LR = 0.001
ADAM_B1 = 0.9
ADAM_B2 = 0.999
ADAM_EPS = 1e-08
ADAM_WD = 0.01
ADAM_STEP = 10
ADAM_BC1 = 1.0 - ADAM_B1 ** ADAM_STEP
ADAM_BC2 = 1.0 - ADAM_B2 ** ADAM_STEP

SUBLANES = 8
LANES = 128
ROW_TILE = 128
ROW_CHUNK = 32
WIDE_TILES = (544, 256, 128)
FFN_COLS = 256
STAGE_COLS = 256
OUT_TILE = 512
N_SEG = 8
HALO_ROWS = 16
SSM_CHUNK = 128
SCAN_UNROLL = 8
VMEM_LIMIT = 48 * 1024 * 1024

NN = ((1,), (0,))
NT = ((1,), (1,))
TN = ((0,), (0,))


def _params(**kw):
    return pltpu.CompilerParams(vmem_limit_bytes=VMEM_LIMIT, **kw)


def _dot(a, b, dims):
    return lax.dot_general(a, b, (dims, ((), ())), preferred_element_type=F32)


def _mean_sq_rsqrt(x):
    return lax.rsqrt(jnp.mean(x * x, axis=-1, keepdims=True) + RMS_EPS)


def _rms_bwd(x, r, g, dy):
    xhat = x * r
    dxh = dy * g
    dx = r * (dxh - xhat * jnp.mean(dxh * xhat, axis=-1, keepdims=True))
    return dx, dy * xhat


def _gelu(y):
    c = math.sqrt(2.0 / math.pi)
    t = jnp.tanh(c * (y + 0.044715 * y * y * y))
    return 0.5 * y * (1.0 + t), t


def _gelu_grad(y, t):
    c = math.sqrt(2.0 / math.pi)
    return 0.5 * (1.0 + t) + 0.5 * y * (1.0 - t * t) * c * (1.0 + 3.0 * 0.044715 * y * y)


def _wrap_prev_halo(halo, first):
    seg = lax.broadcasted_iota(jnp.int32, halo.shape, 0) % N_SEG
    wrapped = jnp.where(seg == 0, 0.0, pltpu.roll(halo, 1, 0))
    return jnp.where(first, wrapped, halo)


def _wrap_next_halo(halo, last):
    seg = lax.broadcasted_iota(jnp.int32, halo.shape, 0) % N_SEG
    wrapped = jnp.where(seg == N_SEG - 1, 0.0, pltpu.roll(halo, halo.shape[0] - 1, 0))
    return jnp.where(last, wrapped, halo)


def _dev_index(p):
    return 4 * p[0] + 2 * p[1] + p[2]


def _allgather(name, shards, deps=()):
    n = len(shards)

    def body(*refs):
        ins, outs = refs[:n], refs[n:2 * n]
        send_sems, recv_sems, local_sems = refs[2 * n:]
        x, y, c = lax.axis_index("x"), lax.axis_index("y"), lax.axis_index("c")
        me, sibling = (x, y, c), (x, y, 1 - c)
        chips = [(1 - x, y), (x, 1 - y), (1 - x, 1 - y)]

        def copy(a, k, block, to, src=None):
            dst = outs[a].at[_dev_index(block)]
            return pltpu.make_async_remote_copy(
                src_ref=dst if src is None else src, dst_ref=dst,
                send_sem=send_sems.at[a, k], recv_sem=recv_sems.at[a, k],
                device_id=to, device_id_type=MESH)

        mine = [pltpu.make_async_copy(ins[a], outs[a].at[_dev_index(me)], local_sems.at[a])
                for a in range(n)]
        for cp in mine:
            cp.start()
        first = []
        for a in range(n):
            first.append(copy(a, 0, me, sibling, src=ins[a]))
            for j, chip in enumerate(chips):
                first.append(copy(a, 1 + j, me, (*chip, c), src=ins[a]))
        for cp in first:
            cp.start()
        passed = []
        for j, chip in enumerate(chips):
            for a in range(n):
                copy(a, 1 + j, (*chip, c), me).wait_recv()
                fwd = copy(a, 4 + j, (*chip, c), sibling)
                fwd.start()
                passed.append(fwd)
        for a in range(n):
            copy(a, 0, sibling, me).wait_recv()
            for j, chip in enumerate(chips):
                copy(a, 4 + j, (*chip, 1 - c), me).wait_recv()
        for cp in first + passed:
            cp.wait_send()
        for cp in mine:
            cp.wait()

    any_spec = pl.BlockSpec(memory_space=pl.ANY)
    return pl.pallas_call(
        _ignoring_deps(body, n, deps), name=name,
        out_shape=[jax.ShapeDtypeStruct((N_DEV,) + s.shape, s.dtype) for s in shards],
        in_specs=[any_spec] * (n + len(deps)), out_specs=[any_spec] * n,
        scratch_shapes=[pltpu.SemaphoreType.DMA((n, 7)), pltpu.SemaphoreType.DMA((n, 7)),
                        pltpu.SemaphoreType.DMA((n,))],
    )(*shards, *deps)


HBM_SPEC = pl.BlockSpec(memory_space=pltpu.HBM)
SEM_SPEC = pl.BlockSpec(memory_space=pltpu.SEMAPHORE)
ANY_SPEC = pl.BlockSpec(memory_space=pl.ANY)
DATAFLOW = pltpu.SideEffectType.DATAFLOW_SIDE_EFFECTING


def _ignoring_deps(body, n_in, deps):
    n_dep = len(deps)

    def wrapped(*refs):
        return body(*refs[:n_in], *refs[n_in + n_dep:])

    return wrapped


def _my_position():
    x, y, c = lax.axis_index("x"), lax.axis_index("y"), lax.axis_index("c")
    return (x, y, c)


def _peer(me, k):
    return tuple((1 - v) if (k >> s) & 1 else v for v, s in zip(me, (2, 1, 0)))


def _split_copies(src_refs, land_refs, send_sems, recv_sems, gather):
    me = _my_position()
    copies = []
    for a, (src, land) in enumerate(zip(src_refs, land_refs)):
        for k in range(1, N_DEV):
            peer = _peer(me, k)
            copies.append(pltpu.make_async_remote_copy(
                src_ref=src if gather[a] else src.at[_dev_index(peer)], dst_ref=land.at[_dev_index(me)],
                send_sem=send_sems[a].at[k - 1], recv_sem=recv_sems[a].at[k - 1],
                device_id=peer, device_id_type=MESH))
    return copies


def _own_slot(block, like_shape):
    me = _dev_index(_my_position())
    return lax.dynamic_update_index_in_dim(lax.empty(like_shape, block.dtype), block, me, 0)


def _send_start(name, srcs, gather):
    n = len(srcs)
    me = _dev_index(_my_position())
    gather = [gather] * n if isinstance(gather, bool) else list(gather)
    lands = [_own_slot(s, (N_DEV,) + s.shape) if g else
             _own_slot(lax.dynamic_index_in_dim(s, me, 0, keepdims=False), s.shape)
             for s, g in zip(srcs, gather)]

    def body(*refs):
        src_refs, land_refs = refs[:n], refs[n:2 * n]
        send_sems, recv_sems = refs[2 * n:3 * n], refs[3 * n:4 * n]
        token = refs[-1]
        for cp in _split_copies(src_refs, land_refs, send_sems, recv_sems, gather):
            cp.start()
        token[...] = jnp.zeros_like(token)

    hbm = lambda a: pltpu.HBM(a.shape, a.dtype)
    sems = [pltpu.SemaphoreType.DMA((N_DEV - 1,))] * n
    outs = pl.pallas_call(
        body, name=name,
        out_shape=(*sems, *sems, *[hbm(s) for s in srcs], *[hbm(l) for l in lands],
                   jax.ShapeDtypeStruct((SUBLANES, LANES), F32)),
        in_specs=[HBM_SPEC] * (2 * n),
        out_specs=(*[SEM_SPEC] * (2 * n), *[HBM_SPEC] * (2 * n), pl.BlockSpec(memory_space=pltpu.VMEM)),
        input_output_aliases={i: 2 * n + i for i in range(2 * n)},
        compiler_params=pltpu.CompilerParams(has_side_effects=DATAFLOW),
    )(*[pltpu.with_memory_space_constraint(a, pltpu.HBM) for a in (*srcs, *lands)])
    state = dict(send=outs[:n], recv=outs[n:2 * n], srcs=outs[2 * n:3 * n], lands=outs[3 * n:4 * n],
                 gather=gather)
    return state, outs[-1]


def _send_wait(name, state, which, after):
    n = len(which)
    pick = lambda key: [state[key][i] for i in which]
    gather = pick("gather")

    def body(*refs):
        src_refs, land_refs = refs[:n], refs[n:2 * n]
        send_sems, recv_sems = refs[2 * n:3 * n], refs[3 * n:4 * n]
        for cp in _split_copies(src_refs, land_refs, send_sems, recv_sems, gather):
            cp.wait_send()
            cp.wait_recv()

    srcs, lands = pick("srcs"), pick("lands")
    hbm = lambda a: pltpu.HBM(a.shape, a.dtype)
    outs = pl.pallas_call(
        body, name=name,
        out_shape=(*[hbm(s) for s in srcs], *[hbm(l) for l in lands]),
        in_specs=[*[HBM_SPEC] * (2 * n), *[SEM_SPEC] * (2 * n), ANY_SPEC],
        out_specs=tuple([HBM_SPEC] * (2 * n)),
        input_output_aliases={i: i for i in range(2 * n)},
        compiler_params=pltpu.CompilerParams(has_side_effects=DATAFLOW),
    )(*srcs, *lands, *pick("send"), *pick("recv"), after)
    return outs[n:]


def _two_level_copies(src_refs, land_refs, sems1, sems2):
    x, y, c = _my_position()
    me, sibling = (x, y, c), (x, y, 1 - c)
    chips = [(1 - x, y), (x, 1 - y), (1 - x, 1 - y)]
    stage1, stage2 = [], []
    for a, land in enumerate(land_refs):
        def copy(src, block, to, sems, k):
            return pltpu.make_async_remote_copy(
                src_ref=src, dst_ref=land.at[_dev_index(block)], send_sem=sems[0][a].at[k],
                recv_sem=sems[1][a].at[k], device_id=to, device_id_type=MESH)
        src = src_refs[a] if src_refs is not None else land.at[_dev_index(me)]
        stage1.append([copy(src, me, sibling, sems1, 0)] +
                      [copy(src, me, (*chip, c), sems1, 1 + j) for j, chip in enumerate(chips)])
        if sems2 is not None:
            stage2.append([copy(land.at[_dev_index((*chip, c))], (*chip, c), sibling, sems2, j)
                           for j, chip in enumerate(chips)])
    return stage1, stage2


def _gather2_start(name, shards):
    n = len(shards)
    lands = [_own_slot(s, (N_DEV,) + s.shape) for s in shards]

    def body(*refs):
        src_refs, land_refs = refs[:n], refs[n:2 * n]
        sems1 = (refs[2 * n:3 * n], refs[3 * n:4 * n])
        stage1, _ = _two_level_copies(src_refs, land_refs, sems1, None)
        for copies in stage1:
            for cp in copies:
                cp.start()
        refs[-1][...] = jnp.zeros_like(refs[-1])

    hbm = lambda a: pltpu.HBM(a.shape, a.dtype)
    sems = [pltpu.SemaphoreType.DMA((4,))] * n
    outs = pl.pallas_call(
        body, name=name,
        out_shape=(*sems, *sems, *[hbm(s) for s in shards], *[hbm(l) for l in lands],
                   jax.ShapeDtypeStruct((SUBLANES, LANES), F32)),
        in_specs=[HBM_SPEC] * (2 * n),
        out_specs=(*[SEM_SPEC] * (2 * n), *[HBM_SPEC] * (2 * n), pl.BlockSpec(memory_space=pltpu.VMEM)),
        input_output_aliases={i: 2 * n + i for i in range(2 * n)},
        compiler_params=pltpu.CompilerParams(has_side_effects=DATAFLOW),
    )(*[pltpu.with_memory_space_constraint(a, pltpu.HBM) for a in (*shards, *lands)])
    state = dict(send1=list(outs[:n]), recv1=list(outs[n:2 * n]), srcs=list(outs[2 * n:3 * n]),
                 lands=list(outs[3 * n:4 * n]), send2={}, recv2={})
    return state, outs[-1]


def _gather2_forward(name, state, which, after):
    n = len(which)
    pick = lambda key: [state[key][i] for i in which]

    def body(*refs):
        land_refs, recv1 = refs[:n], refs[n:2 * n]
        outs = refs[2 * n + len(after):]
        sems2 = (outs[:n], outs[n:2 * n])
        stage1, stage2 = _two_level_copies(None, land_refs, (recv1, recv1), sems2)
        for a in range(n):
            for j in range(3):
                stage1[a][1 + j].wait_recv()
                stage2[a][j].start()
        outs[-1][...] = jnp.zeros_like(outs[-1])

    lands = pick("lands")
    sems = [pltpu.SemaphoreType.DMA((3,))] * n
    outs = pl.pallas_call(
        body, name=name,
        out_shape=(*sems, *sems, *[pltpu.HBM(l.shape, l.dtype) for l in lands],
                   jax.ShapeDtypeStruct((SUBLANES, LANES), F32)),
        in_specs=[*[HBM_SPEC] * n, *[SEM_SPEC] * n, *[ANY_SPEC] * len(after)],
        out_specs=(*[SEM_SPEC] * (2 * n), *[HBM_SPEC] * n, pl.BlockSpec(memory_space=pltpu.VMEM)),
        input_output_aliases={i: 2 * n + i for i in range(n)},
        compiler_params=pltpu.CompilerParams(has_side_effects=DATAFLOW),
    )(*lands, *pick("recv1"), *after)
    for idx, i in enumerate(which):
        state["send2"][i], state["recv2"][i] = outs[idx], outs[n + idx]
        state["lands"][i] = outs[2 * n + idx]
    return outs[-1]


def _gather2_wait(name, state, which, after):
    n = len(which)
    pick = lambda key: [state[key][i] for i in which]

    def body(*refs):
        src_refs, land_refs = refs[:n], refs[n:2 * n]
        sems1 = (refs[2 * n:3 * n], refs[3 * n:4 * n])
        sems2 = (refs[4 * n:5 * n], refs[5 * n:6 * n])
        stage1, stage2 = _two_level_copies(src_refs, land_refs, sems1, sems2)
        for a in range(n):
            for cp in stage1[a]:
                cp.wait_send()
            stage1[a][0].wait_recv()
            for cp in stage2[a]:
                cp.wait_send()
                cp.wait_recv()

    srcs, lands = pick("srcs"), pick("lands")
    hbm = lambda a: pltpu.HBM(a.shape, a.dtype)
    outs = pl.pallas_call(
        body, name=name,
        out_shape=(*[hbm(s) for s in srcs], *[hbm(l) for l in lands]),
        in_specs=[*[HBM_SPEC] * (2 * n), *[SEM_SPEC] * (4 * n), ANY_SPEC],
        out_specs=tuple([HBM_SPEC] * (2 * n)),
        input_output_aliases={i: i for i in range(2 * n)},
        compiler_params=pltpu.CompilerParams(has_side_effects=DATAFLOW),
    )(*srcs, *lands, *pick("send1"), *pick("recv1"), *pick("send2"), *pick("recv2"), after)
    return outs[n:]


def _mm(name, a, b, *, dims, grid, a_spec, b_spec, o_spec, out_shape, acc_shape=None,
        res=None, res_spec=None):
    n_red = grid[-1] if acc_shape is not None else 1
    red_axis = len(grid) - 1

    def body(*refs):
        a_ref, b_ref = refs[0], refs[1]
        r_ref = refs[2] if res is not None else None
        o_ref = refs[3] if res is not None else refs[2]
        part = _dot(a_ref[...], b_ref[...], dims)
        if acc_shape is None:
            if r_ref is not None:
                part = part + r_ref[...]
            o_ref[...] = part.astype(o_ref.dtype)
            return
        acc_ref = refs[-1]
        k = pl.program_id(red_axis)

        @pl.when(k == 0)
        def _():
            acc_ref[...] = part

        @pl.when(k > 0)
        def _():
            acc_ref[...] += part

        @pl.when(k == n_red - 1)
        def _():
            total = acc_ref[...]
            if r_ref is not None:
                total = total + r_ref[...]
            o_ref[...] = total.astype(o_ref.dtype)

    ins, in_specs = [a, b], [a_spec, b_spec]
    if res is not None:
        ins.append(res)
        in_specs.append(res_spec)
    return pl.pallas_call(
        body, name=name, grid=grid, in_specs=in_specs, out_specs=o_spec, out_shape=out_shape,
        scratch_shapes=[pltpu.VMEM(acc_shape, F32)] if acc_shape is not None else [],
        compiler_params=_params(),
    )(*ins)


def _wide_tile(t):
    return next(c for c in WIDE_TILES if t % c == 0)


def _rms_fwd(name, h, g):
    t, d = h.shape
    tile = _wide_tile(t)

    def body(h_ref, g_ref, o_ref):
        x = h_ref[...]
        o_ref[...] = (x * _mean_sq_rsqrt(x) * g_ref[...]).astype(BF16)

    return pl.pallas_call(
        body, name=name, grid=(t // tile,),
        in_specs=[pl.BlockSpec((tile, d), lambda i: (i, 0)), pl.BlockSpec((1, d), lambda i: (0, 0))],
        out_specs=pl.BlockSpec((tile, d), lambda i: (i, 0)),
        out_shape=jax.ShapeDtypeStruct((t, d), BF16), compiler_params=_params(),
    )(h, g)


def _fused_tile(t):
    half = _wide_tile(t) // 2
    return half if half % 16 == 0 and t % half == 0 else _wide_tile(t)


def _dgrad_rms_bwd(name, dres, dy, w, dims, h, g, deps=()):
    t, d = h.shape
    k = dy.shape[1]

    def body(dres_ref, dy_ref, w_ref, h_ref, g_ref, dh_ref, dhb_ref, dg_ref):
        x = h_ref[...]
        dhn = _dot(dy_ref[...], w_ref[...], dims)
        dx, dgt = _rms_bwd(x, _mean_sq_rsqrt(x), g_ref[...], dhn)
        dh = dres_ref[...] + dx
        dh_ref[...] = dh
        dhb_ref[...] = dh.astype(BF16)

        @pl.when(pl.program_id(0) == 0)
        def _():
            dg_ref[...] = jnp.zeros_like(dg_ref)

        dg_ref[...] += jnp.sum(dgt, axis=0, keepdims=True)

    tile = _fused_tile(t)
    row = pl.BlockSpec((tile, d), lambda i: (i, 0))
    vec = pl.BlockSpec((1, d), lambda i: (0, 0))
    return pl.pallas_call(
        _ignoring_deps(body, 5, deps), name=name, grid=(t // tile,),
        in_specs=[row, pl.BlockSpec((tile, k), lambda i: (i, 0)), pl.BlockSpec(w.shape, lambda i: (0, 0)), row,
                  vec] + [ANY_SPEC] * len(deps),
        out_specs=[row, row, vec],
        out_shape=[jax.ShapeDtypeStruct((t, d), F32), jax.ShapeDtypeStruct((t, d), BF16),
                   jax.ShapeDtypeStruct((1, d), F32)],
        compiler_params=_params(),
    )(dres, dy, w, h, g, *deps)


def _down_loss_bwd(act, w_down, h1, target, g, row_lo, row_hi):
    t, d = h1.shape
    f = act.shape[1]
    tile = _fused_tile(t)

    def body(act_ref, w_ref, h_ref, tg_ref, g_ref, dh_ref, dhb_ref, loss_ref, dg_ref):
        i = pl.program_id(0)
        x = h_ref[...] + _dot(act_ref[...], w_ref[...], NN)
        r = _mean_sq_rsqrt(x)
        gv = g_ref[...]
        y = x * r * gv
        rows = i * tile + lax.broadcasted_iota(jnp.int32, (tile, 1), 0)
        time = (rows % N_SEG) * (t // N_SEG) + rows // N_SEG
        valid = jnp.logical_and(time >= row_lo, time < row_hi)
        err = jnp.where(valid, y - tg_ref[...], 0.0)
        dy = err * (1.0 / d)
        dx, dgt = _rms_bwd(x, r, gv, dy)
        dh_ref[...] = dx
        dhb_ref[...] = dx.astype(BF16)

        @pl.when(i == 0)
        def _():
            loss_ref[...] = jnp.zeros_like(loss_ref)
            dg_ref[...] = jnp.zeros_like(dg_ref)

        row_loss = jnp.mean(err * err, axis=-1, keepdims=True)
        loss_ref[...] += 0.5 * jnp.sum(row_loss, axis=0, keepdims=True)
        dg_ref[...] += jnp.sum(dgt, axis=0, keepdims=True)

    row = pl.BlockSpec((tile, d), lambda i: (i, 0))
    vec = pl.BlockSpec((1, d), lambda i: (0, 0))
    return pl.pallas_call(
        body, name="down_proj_loss_bwd", grid=(t // tile,),
        in_specs=[pl.BlockSpec((tile, f), lambda i: (i, 0)), pl.BlockSpec(w_down.shape, lambda i: (0, 0)),
                  row, row, vec],
        out_specs=[row, row, pl.BlockSpec((1, 1), lambda i: (0, 0)), vec],
        out_shape=[jax.ShapeDtypeStruct((t, d), F32), jax.ShapeDtypeStruct((t, d), BF16),
                   jax.ShapeDtypeStruct((1, 1), F32), jax.ShapeDtypeStruct((1, d), F32)],
        compiler_params=_params(),
    )(act, w_down, h1, target, g)


def _prev_halo(i, t):
    return jnp.where(i == 0, t // HALO_ROWS - 1, i * (ROW_TILE // HALO_ROWS) - 1)


def _next_halo(i, t, tile=ROW_TILE):
    return jnp.where(i == t // tile - 1, 0, (i + 1) * (tile // HALO_ROWS))


def _causal_taps(cur, halo, first):
    rows = cur.shape[0]
    ext = jnp.concatenate([_wrap_prev_halo(halo, first), cur], axis=0)
    return ext[:rows], ext[N_SEG:N_SEG + rows]


def _anticausal_taps(cur, halo, last):
    rows = cur.shape[0]
    ext = jnp.concatenate([cur, _wrap_next_halo(halo, last)], axis=0)
    return ext[N_SEG:N_SEG + rows], ext[2 * N_SEG:2 * N_SEG + rows]


def _mix_fwd(proj, y, w_glu, conv_w, gain_c, gain_s, h0, w_out, gain_ffn):
    t, d = h0.shape
    dc = conv_w.shape[1]
    ds = y.shape[1]

    def body(p_ref, halo_ref, y_ref, wg_ref, cw_ref, gc_ref, gs_ref, h0_ref, wo_ref, gf_ref,
             mixed_ref, z_ref, h1_ref, hn_ref):
        i = pl.program_id(0)
        p = p_ref[...]
        b, c, v = p[:, :dc], p[:, dc:2 * dc], p[:, 2 * dc:3 * dc]
        cv = c * v
        hp = halo_ref[...]
        x2, x1 = _causal_taps(cv, hp[:, dc:2 * dc] * hp[:, 2 * dc:3 * dc], i == 0)
        cw = cw_ref[...]
        conv = cw[0:1] * x2 + cw[1:2] * x1 + cw[2:3] * cv
        co = b * conv
        mixed_ref[:, :dc] = (co * _mean_sq_rsqrt(co) * gc_ref[...]).astype(BF16)
        g, _ = _gelu(y_ref[...])
        z = _dot(g.astype(BF16), wg_ref[...], NN)
        z_ref[...] = z
        so = g * jax.nn.sigmoid(z)
        mixed_ref[:, dc:] = (so * _mean_sq_rsqrt(so) * gs_ref[...]).astype(BF16)
        h1 = h0_ref[...] + _dot(mixed_ref[...], wo_ref[...], NN)
        h1_ref[...] = h1
        hn_ref[...] = (h1 * _mean_sq_rsqrt(h1) * gf_ref[...]).astype(BF16)

    const = lambda i: (0, 0)
    row = lambda w: pl.BlockSpec((ROW_TILE, w), lambda i: (i, 0))
    return pl.pallas_call(
        body, name="mix_fwd_out_proj", grid=(t // ROW_TILE,),
        in_specs=[row(3 * dc), pl.BlockSpec((HALO_ROWS, 3 * dc), lambda i: (_prev_halo(i, t), 0)), row(ds),
                  pl.BlockSpec((ds, ds), const), pl.BlockSpec(conv_w.shape, const),
                  pl.BlockSpec((1, dc), const), pl.BlockSpec((1, ds), const),
                  row(d), pl.BlockSpec(w_out.shape, const), pl.BlockSpec((1, d), const)],
        out_specs=[row(dc + ds), row(ds), row(d), row(d)],
        out_shape=[jax.ShapeDtypeStruct((t, dc + ds), BF16), jax.ShapeDtypeStruct((t, ds), F32),
                   jax.ShapeDtypeStruct((t, d), F32), jax.ShapeDtypeStruct((t, d), BF16)],
        compiler_params=_params(),
    )(proj, proj, y, w_glu, conv_w, gain_c, gain_s, h0, w_out, gain_ffn)


def _mix_bwd1(proj, y, z, dmixed, w_glu, conv_w, gain_c, gain_s):
    t = proj.shape[0]
    dc = conv_w.shape[1]
    ds = y.shape[1]

    def body(p_ref, halo_ref, y_ref, z_ref, dm_ref, wg_ref, cw_ref, gc_ref, gs_ref,
             db_ref, dconv_ref, dy_ref, dwg_ref, dcw_ref, dgc_ref, dgs_ref):
        i = pl.program_id(0)

        @pl.when(i == 0)
        def _():
            dwg_ref[...] = jnp.zeros_like(dwg_ref)
            dcw_ref[...] = jnp.zeros_like(dcw_ref)
            dgc_ref[...] = jnp.zeros_like(dgc_ref)
            dgs_ref[...] = jnp.zeros_like(dgs_ref)

        p = p_ref[...]
        b, c, v = p[:, :dc], p[:, dc:2 * dc], p[:, 2 * dc:3 * dc]
        cv = c * v
        hp = halo_ref[...]
        x2, x1 = _causal_taps(cv, hp[:, dc:2 * dc] * hp[:, 2 * dc:3 * dc], i == 0)
        cw = cw_ref[...]
        conv = cw[0:1] * x2 + cw[1:2] * x1 + cw[2:3] * cv
        co = b * conv
        dm = dm_ref[...]
        dco, dgt = _rms_bwd(co, _mean_sq_rsqrt(co), gc_ref[...], dm[:, :dc])
        dgc_ref[...] += jnp.sum(dgt, axis=0, keepdims=True)
        db_ref[...] = (dco * conv).astype(BF16)
        dconv = dco * b
        dconv_ref[...] = dconv
        dcw_ref[0:1, :] += jnp.sum(dconv * x2, axis=0, keepdims=True)
        dcw_ref[1:2, :] += jnp.sum(dconv * x1, axis=0, keepdims=True)
        dcw_ref[2:3, :] += jnp.sum(dconv * cv, axis=0, keepdims=True)

        yv = y_ref[...]
        g, th = _gelu(yv)
        sg = jax.nn.sigmoid(z_ref[...])
        so = g * sg
        dso, dgt = _rms_bwd(so, _mean_sq_rsqrt(so), gs_ref[...], dm[:, dc:])
        dgs_ref[...] += jnp.sum(dgt, axis=0, keepdims=True)
        dz = (dso * g * sg * (1.0 - sg)).astype(BF16)
        dg = dso * sg + _dot(dz, wg_ref[...], NT)
        dwg_ref[...] += _dot(g.astype(BF16), dz, TN)
        dy_ref[...] = dg * _gelu_grad(yv, th)

    const = lambda i: (0, 0)
    row = lambda w: pl.BlockSpec((ROW_TILE, w), lambda i: (i, 0))
    return pl.pallas_call(
        body, name="mix_bwd1", grid=(t // ROW_TILE,),
        in_specs=[row(3 * dc), pl.BlockSpec((HALO_ROWS, 3 * dc), lambda i: (_prev_halo(i, t), 0)),
                  row(ds), row(ds), row(dc + ds), pl.BlockSpec((ds, ds), const),
                  pl.BlockSpec(conv_w.shape, const), pl.BlockSpec((1, dc), const),
                  pl.BlockSpec((1, ds), const)],
        out_specs=[row(dc), row(dc), row(ds), pl.BlockSpec((ds, ds), const),
                   pl.BlockSpec(conv_w.shape, const), pl.BlockSpec((1, dc), const),
                   pl.BlockSpec((1, ds), const)],
        out_shape=[jax.ShapeDtypeStruct((t, dc), BF16), jax.ShapeDtypeStruct((t, dc), F32),
                   jax.ShapeDtypeStruct((t, ds), F32), jax.ShapeDtypeStruct((ds, ds), F32),
                   jax.ShapeDtypeStruct(conv_w.shape, F32), jax.ShapeDtypeStruct((1, dc), F32),
                   jax.ShapeDtypeStruct((1, ds), F32)],
        compiler_params=_params(),
    )(proj, proj, y, z, dmixed, w_glu, conv_w, gain_c, gain_s)


def _mix_bwd2(proj, dconv, conv_w, deps=()):
    t = proj.shape[0]
    dc = conv_w.shape[1]
    tile = _wide_tile(t)
    n_tiles = t // tile

    def body(c_ref, v_ref, d_ref, halo_ref, cw_ref, o_ref):
        i = pl.program_id(0)
        d = d_ref[...]
        u1, u2 = _anticausal_taps(d, halo_ref[...], i == n_tiles - 1)
        cw = cw_ref[...]
        dcv = cw[2:3] * d + cw[1:2] * u1 + cw[0:1] * u2
        o_ref[:, :dc] = (dcv * v_ref[...]).astype(BF16)
        o_ref[:, dc:] = (dcv * c_ref[...]).astype(BF16)

    return pl.pallas_call(
        _ignoring_deps(body, 5, deps), name="mix_bwd2", grid=(n_tiles,),
        in_specs=[pl.BlockSpec((tile, dc), lambda i: (i, 1)),
                  pl.BlockSpec((tile, dc), lambda i: (i, 2)),
                  pl.BlockSpec((tile, dc), lambda i: (i, 0)),
                  pl.BlockSpec((HALO_ROWS, dc), lambda i: (_next_halo(i, t, tile), 0)),
                  pl.BlockSpec(conv_w.shape, lambda i: (0, 0))] + [ANY_SPEC] * len(deps),
        out_specs=pl.BlockSpec((tile, 2 * dc), lambda i: (i, 0)),
        out_shape=jax.ShapeDtypeStruct((t, 2 * dc), BF16), compiler_params=_params(),
    )(proj, proj, dconv, dconv, conv_w, *deps)


def _stage_rows(stage_ref, row0, src, wrap=None):
    n, width = src.shape
    for c in range(0, width, STAGE_COLS):
        v = src[:, pl.ds(c, STAGE_COLS)].astype(F32)
        stage_ref[pl.ds(row0, n), pl.ds(c, STAGE_COLS)] = v if wrap is None else wrap(v)


def _conv_taps(stage_ref, fw_ref, fb_ref, c0, r, rows=ROW_CHUNK):
    cols = pl.ds(c0, FFN_COLS)
    x0, x1, x2 = (stage_ref[pl.ds(HALO_ROWS + r - k * N_SEG, rows), cols] for k in range(3))
    w = fw_ref[:, cols]
    return w[0:1] * x2 + w[1:2] * x1 + w[2:3] * x0 + fb_ref[:, cols], x2, x1, x0


def _gated_fwd(up_pre, fw, fb):
    t, f2 = up_pre.shape
    f = f2 // 2

    def body(up_ref, halo_ref, fw_ref, fb_ref, act_ref, stage_ref):
        first = pl.program_id(0) == 0
        _stage_rows(stage_ref, 0, halo_ref, lambda v: _wrap_prev_halo(v, first))
        _stage_rows(stage_ref, HALO_ROWS, up_ref)
        for c0 in range(0, f, FFN_COLS):
            for r in range(0, ROW_TILE, ROW_CHUNK):
                a = _conv_taps(stage_ref, fw_ref, fb_ref, c0, r)[0]
                val = _conv_taps(stage_ref, fw_ref, fb_ref, f + c0, r)[0]
                act_ref[pl.ds(r, ROW_CHUNK), pl.ds(c0, FFN_COLS)] = (a * jax.nn.sigmoid(a) * val).astype(BF16)

    const = lambda a: pl.BlockSpec(a.shape, lambda i: (0, 0))
    return pl.pallas_call(
        body, name="ffn_fwd", grid=(t // ROW_TILE,),
        in_specs=[pl.BlockSpec((ROW_TILE, f2), lambda i: (i, 0)),
                  pl.BlockSpec((HALO_ROWS, f2), lambda i: (_prev_halo(i, t), 0)), const(fw), const(fb)],
        out_specs=pl.BlockSpec((ROW_TILE, f), lambda i: (i, 0)),
        out_shape=jax.ShapeDtypeStruct((t, f), BF16),
        scratch_shapes=[pltpu.VMEM((HALO_ROWS + ROW_TILE, f2), F32)], compiler_params=_params(),
    )(up_pre, up_pre, fw, fb)


def _gated_bwd(up_pre, dact, fw, fb, deps=()):
    t, f2 = up_pre.shape
    f = f2 // 2
    n_tiles = t // ROW_TILE
    chunks = [(r, ROW_CHUNK) for r in range(0, ROW_TILE, ROW_CHUNK)] + [(ROW_TILE, HALO_ROWS)]

    def body(up_ref, prev_ref, next_ref, dact_in_ref, dact_next_ref, fw_ref, fb_ref,
             out_ref, dfw_ref, dfb_ref, dup_ref, stage_ref, dact_ref):
        first = pl.program_id(0) == 0
        last = pl.program_id(0) == n_tiles - 1

        @pl.when(first)
        def _():
            dfw_ref[...] = jnp.zeros_like(dfw_ref)
            dfb_ref[...] = jnp.zeros_like(dfb_ref)

        _stage_rows(dact_ref, 0, dact_in_ref)
        _stage_rows(dact_ref, ROW_TILE, dact_next_ref, lambda v: _wrap_next_halo(v, last))
        _stage_rows(stage_ref, 0, prev_ref, lambda v: _wrap_prev_halo(v, first))
        _stage_rows(stage_ref, HALO_ROWS, up_ref)
        _stage_rows(stage_ref, HALO_ROWS + ROW_TILE, next_ref, lambda v: _wrap_next_halo(v, last))
        fold = lambda v: sum(v[s:s + SUBLANES] for s in range(0, ROW_CHUNK, SUBLANES))
        for c0 in range(0, f, FFN_COLS):
            starts = (c0, f + c0)
            sums = [[jnp.zeros((SUBLANES, FFN_COLS), F32)] * 4 for _ in starts]
            for r, rows in chunks:
                taps = [_conv_taps(stage_ref, fw_ref, fb_ref, c, r, rows) for c in starts]
                a, val = taps[0][0], taps[1][0]
                da_ct = dact_ref[pl.ds(r, rows), pl.ds(c0, FFN_COLS)]
                sg = jax.nn.sigmoid(a)
                dup = (da_ct * val * sg * (1.0 + a * (1.0 - sg)), da_ct * a * sg)
                for k in range(2):
                    dup_ref[k, pl.ds(r, rows), :] = dup[k]
                    if r < ROW_TILE:
                        terms = (dup[k], dup[k] * taps[k][1], dup[k] * taps[k][2], dup[k] * taps[k][3])
                        sums[k] = [s + fold(v) for s, v in zip(sums[k], terms)]
            for k, c in enumerate(starts):
                cols = pl.ds(c, FFN_COLS)
                s_b, s_w0, s_w1, s_w2 = (jnp.sum(p, axis=0, keepdims=True) for p in sums[k])
                dfb_ref[:, cols] += s_b
                for tap, s_w in enumerate((s_w0, s_w1, s_w2)):
                    dfw_ref[tap:tap + 1, cols] += s_w
                cw = fw_ref[:, cols]
                for r in range(0, ROW_TILE, ROW_CHUNK):
                    d, u1, u2 = (dup_ref[k, pl.ds(r + s * N_SEG, ROW_CHUNK), :] for s in range(3))
                    out_ref[pl.ds(r, ROW_CHUNK), cols] = (cw[2:3] * d + cw[1:2] * u1 + cw[0:1] * u2).astype(BF16)

    tile = lambda w: pl.BlockSpec((ROW_TILE, w), lambda i: (i, 0))
    halo = lambda w, index: pl.BlockSpec((HALO_ROWS, w), lambda i: (index(i, t), 0))
    const = lambda a: pl.BlockSpec(a.shape, lambda i: (0, 0))
    return pl.pallas_call(
        _ignoring_deps(body, 7, deps), name="ffn_bwd", grid=(n_tiles,),
        in_specs=[tile(f2), halo(f2, _prev_halo), halo(f2, _next_halo), tile(f), halo(f, _next_halo),
                  const(fw), const(fb)] + [ANY_SPEC] * len(deps),
        out_specs=[tile(f2), const(fw), const(fb)],
        out_shape=[jax.ShapeDtypeStruct((t, f2), BF16), jax.ShapeDtypeStruct(fw.shape, F32),
                   jax.ShapeDtypeStruct(fb.shape, F32)],
        scratch_shapes=[pltpu.VMEM((2, ROW_TILE + HALO_ROWS, FFN_COLS), F32),
                        pltpu.VMEM((HALO_ROWS + ROW_TILE + HALO_ROWS, f2), F32),
                        pltpu.VMEM((ROW_TILE + HALO_ROWS, f), F32)],
        compiler_params=_params(),
    )(up_pre, up_pre, up_pre, dact, dact, fw, fb, *deps)


def _to_segments(a):
    t, c = a.shape
    return a.reshape(N_SEG, t // N_SEG, c).transpose(1, 0, 2).reshape(t, c)


def _from_segments(a):
    t, c = a.shape
    return a.reshape(t // N_SEG, N_SEG, c).transpose(1, 0, 2).reshape(t, c)


def _cmul(ar, ai, br, bi):
    return ar * br - ai * bi, ar * bi + ai * br


def _cpow(ar, ai, n):
    out = None
    while n:
        if n & 1:
            out = (ar, ai) if out is None else _cmul(out[0], out[1], ar, ai)
        ar, ai = _cmul(ar, ai, ar, ai)
        n >>= 1
    return out


def _segment_carries(pr, pi, fr, fi, forward):
    row = lax.broadcasted_iota(jnp.int32, fr.shape, 0)
    edge = row == (0 if forward else N_SEG - 1)
    shift = 1 if forward else N_SEG - 1
    sr, si = jnp.zeros_like(fr), jnp.zeros_like(fi)
    for _ in range(N_SEG - 1):
        tr, ti = _cmul(pr, pi, sr, si)
        sr = jnp.where(edge, 0.0, pltpu.roll(tr + fr, shift, 0))
        si = jnp.where(edge, 0.0, pltpu.roll(ti + fi, shift, 0))
    return sr, si


def _rows(i):
    if isinstance(i, int):
        return pl.ds(i * SUBLANES, SUBLANES)
    return pl.ds(pl.multiple_of(i * SUBLANES, SUBLANES), SUBLANES)


def _scan_steps(n, body, init):
    def trip(k, carry):
        for s in range(SCAN_UNROLL):
            carry = body(k * SCAN_UNROLL + s, carry)
        return carry

    carry = lax.fori_loop(0, n // SCAN_UNROLL, trip, init)
    for i in range(n - n % SCAN_UNROLL, n):
        carry = body(i, carry)
    return carry


def _s5_fwd(proj, u_col, bb_re, bb_im, a_re, a_im, cc_re, cc_im, d_skip):
    t = proj.shape[0]
    nch, _, cs = bb_re.shape
    ds = nch * SSM_CHUNK
    u_blk = u_col // SSM_CHUNK
    steps = t // N_SEG

    def body(u_ref, bbr_ref, bbi_ref, ar_ref, ai_ref, ccr_ref, cci_ref, d_ref, sr_ref, si_ref, y_ref):
        ub = u_ref[...].astype(BF16)
        sr_ref[...] = _dot(ub, bbr_ref[...], NN)
        si_ref[...] = _dot(ub, bbi_ref[...], NN)
        ar = jnp.broadcast_to(ar_ref[...], (N_SEG, cs))
        ai = jnp.broadcast_to(ai_ref[...], (N_SEG, cs))
        zero = jnp.zeros((N_SEG, cs), F32)

        def totals(i, carry):
            tr, ti = _cmul(ar, ai, *carry)
            return tr + sr_ref[_rows(i), :], ti + si_ref[_rows(i), :]

        fr, fi = _scan_steps(steps, totals, (zero, zero))
        s0r, s0i = _segment_carries(*_cpow(ar, ai, steps), fr, fi, True)

        def scan(i, carry):
            tr, ti = _cmul(ar, ai, *carry)
            nr, ni = tr + sr_ref[_rows(i), :], ti + si_ref[_rows(i), :]
            sr_ref[_rows(i), :] = nr
            si_ref[_rows(i), :] = ni
            return nr, ni

        _scan_steps(steps, scan, (s0r, s0i))
        y_ref[...] = (_dot(sr_ref[...].astype(BF16), ccr_ref[...], NT)
                      - _dot(si_ref[...].astype(BF16), cci_ref[...], NT)
                      + d_ref[...] * u_ref[...])

    chunk3 = lambda r, c: pl.BlockSpec((None, r, c), lambda j: (j, 0, 0))
    return pl.pallas_call(
        body, name="s5_fwd", grid=(nch,),
        in_specs=[pl.BlockSpec((t, SSM_CHUNK), lambda j: (0, j + u_blk)),
                  chunk3(SSM_CHUNK, cs), chunk3(SSM_CHUNK, cs), chunk3(1, cs), chunk3(1, cs),
                  chunk3(SSM_CHUNK, cs), chunk3(SSM_CHUNK, cs), chunk3(1, SSM_CHUNK)],
        out_specs=[pl.BlockSpec((t, cs), lambda j: (0, j)), pl.BlockSpec((t, cs), lambda j: (0, j)),
                   pl.BlockSpec((t, SSM_CHUNK), lambda j: (0, j))],
        out_shape=[jax.ShapeDtypeStruct((t, nch * cs), F32), jax.ShapeDtypeStruct((t, nch * cs), F32),
                   jax.ShapeDtypeStruct((t, ds), F32)],
        compiler_params=_params(),
    )(proj, bb_re, bb_im, a_re, a_im, cc_re, cc_im, d_skip)


def _s5_bwd(dy, proj, u_col, s_re, s_im, bb_re, bb_im, a_re, a_im, cc_re, cc_im, d_skip, gpc):
    t, ds = dy.shape
    nch, _, cs = bb_re.shape
    u_blk = u_col // SSM_CHUNK
    steps = t // N_SEG

    def body(dy_ref, u_ref, sr_ref, si_ref, bbr_ref, bbi_ref, ar_ref, ai_ref, ccr_ref, cci_ref, d_ref,
             du_ref, dbbr_ref, dbbi_ref, dar_ref, dai_ref, dccr_ref, dcci_ref, dd_ref, gr_ref, gi_ref):
        dyv = dy_ref[...]
        dyb = dyv.astype(BF16)
        gr_ref[...] = _dot(dyb, ccr_ref[...], NN)
        gi_ref[...] = -_dot(dyb, cci_ref[...], NN)
        ar = jnp.broadcast_to(ar_ref[...], (N_SEG, cs))
        ai = -jnp.broadcast_to(ai_ref[...], (N_SEG, cs))
        zero = jnp.zeros((N_SEG, cs), F32)

        def totals(k, carry):
            i = steps - 1 - k
            tr, ti = _cmul(ar, ai, *carry)
            return tr + gr_ref[_rows(i), :], ti + gi_ref[_rows(i), :]

        fr, fi = _scan_steps(steps, totals, (zero, zero))
        e0r, e0i = _segment_carries(*_cpow(ar, ai, steps), fr, fi, False)

        def step(i, gr, gi, pr, pi, acc_r, acc_i):
            tr, ti = _cmul(ar, ai, gr, gi)
            nr, ni = tr + gr_ref[_rows(i), :], ti + gi_ref[_rows(i), :]
            gr_ref[_rows(i), :] = nr
            gi_ref[_rows(i), :] = ni
            return nr, ni, acc_r + nr * pr + ni * pi, acc_i + ni * pr - nr * pi

        def scan(k, carry):
            i = steps - 1 - k
            gr, gi, acc_r, acc_i = carry
            return step(i, gr, gi, sr_ref[_rows(i - 1), :], si_ref[_rows(i - 1), :], acc_r, acc_i)

        gr, gi, acc_r, acc_i = _scan_steps(steps - 1, scan, (e0r, e0i, zero, zero))
        row = lax.broadcasted_iota(jnp.int32, (N_SEG, cs), 0)
        last = _rows(steps - 1)
        pr = jnp.where(row == 0, 0.0, pltpu.roll(sr_ref[last, :], 1, 0))
        pi = jnp.where(row == 0, 0.0, pltpu.roll(si_ref[last, :], 1, 0))
        _, _, acc_r, acc_i = step(0, gr, gi, pr, pi, acc_r, acc_i)
        dar_ref[...] = jnp.sum(acc_r, axis=0, keepdims=True)
        dai_ref[...] = jnp.sum(acc_i, axis=0, keepdims=True)

        uv = u_ref[...]
        ub = uv.astype(BF16)
        grb = gr_ref[...].astype(BF16)
        gib = gi_ref[...].astype(BF16)
        du = d_ref[...] * dyv + _dot(grb, bbr_ref[...], NT) + _dot(gib, bbi_ref[...], NT)
        du_ref[...] = du.astype(BF16)
        def put_groups(ref, full):
            for gl in range(gpc):
                ref[gl] = full[gl * hb:(gl + 1) * hb, gl * pb:(gl + 1) * pb]

        put_groups(dbbr_ref, _dot(ub, grb, TN))
        put_groups(dbbi_ref, _dot(ub, gib, TN))
        put_groups(dccr_ref, _dot(dyb, sr_ref[...].astype(BF16), TN))
        put_groups(dcci_ref, -_dot(dyb, si_ref[...].astype(BF16), TN))
        dd_ref[...] = jnp.sum(dyv * uv, axis=0, keepdims=True)

    hb, pb = SSM_CHUNK // gpc, cs // gpc
    groups = pl.BlockSpec((None, gpc, hb, pb), lambda j: (j, 0, 0, 0))
    groups_shape = jax.ShapeDtypeStruct((nch, gpc, hb, pb), F32)
    chunk3 = lambda r, c: pl.BlockSpec((None, r, c), lambda j: (j, 0, 0))
    cols = lambda w: pl.BlockSpec((t, w), lambda j: (0, j))
    return pl.pallas_call(
        body, name="s5_bwd", grid=(nch,),
        in_specs=[cols(SSM_CHUNK), pl.BlockSpec((t, SSM_CHUNK), lambda j: (0, j + u_blk)), cols(cs), cols(cs),
                  chunk3(SSM_CHUNK, cs), chunk3(SSM_CHUNK, cs), chunk3(1, cs), chunk3(1, cs),
                  chunk3(SSM_CHUNK, cs), chunk3(SSM_CHUNK, cs), chunk3(1, SSM_CHUNK)],
        out_specs=[cols(SSM_CHUNK), groups, groups, chunk3(1, cs), chunk3(1, cs), groups, groups,
                   chunk3(1, SSM_CHUNK)],
        out_shape=[jax.ShapeDtypeStruct((t, ds), BF16), groups_shape, groups_shape,
                   jax.ShapeDtypeStruct((nch, 1, cs), F32), jax.ShapeDtypeStruct((nch, 1, cs), F32),
                   groups_shape, groups_shape, jax.ShapeDtypeStruct((nch, 1, SSM_CHUNK), F32)],
        scratch_shapes=[pltpu.VMEM((t, cs), F32), pltpu.VMEM((t, cs), F32)],
        compiler_params=_params(),
    )(dy, proj, s_re, s_im, bb_re, bb_im, a_re, a_im, cc_re, cc_im, d_skip)


def _discretize(lr, li, log_dt, br, bi):
    dt = jnp.exp(log_dt)
    mag = jnp.exp(lr * dt)
    ang = li * dt
    a_re = mag * jnp.cos(ang)
    a_im = mag * jnp.sin(ang)
    den = lr * lr + li * li
    nr = a_re - 1.0
    f_re = (nr * lr + a_im * li) / den
    f_im = (a_im * lr - nr * li) / den
    return a_re, a_im, f_re * br - f_im * bi, f_re * bi + f_im * br


def _whole(shape):
    return pl.BlockSpec(shape, lambda: (0,) * len(shape))


def _disc_fwd(lr, li, log_dt, br, bi, cr, ci, nch, gpc, deps=()):
    gh, n_state = br.shape
    grp = gh // (nch * gpc)
    rows, cs = gpc * grp, gpc * n_state

    def body(lr_ref, li_ref, dt_ref, br_ref, bi_ref, cr_ref, ci_ref,
             ar_ref, ai_ref, bbr_ref, bbi_ref, ccr_ref, cci_ref, wide_ref):
        a_re, a_im, bb_re, bb_im = _discretize(lr_ref[...], li_ref[...], dt_ref[...], br_ref[...], bi_ref[...])
        for ref, val in ((ar_ref, a_re), (ai_ref, a_im)):
            for j in range(nch):
                for gl in range(gpc):
                    g0 = (j * gpc + gl) * grp
                    ref[j, :, gl * n_state:(gl + 1) * n_state] = val[g0:g0 + 1]
        for ref, val in ((bbr_ref, bb_re), (bbi_ref, bb_im), (ccr_ref, cr_ref[...]), (cci_ref, ci_ref[...])):
            wide_ref[...] = jnp.zeros_like(wide_ref)
            for j in range(nch):
                for gl in range(gpc):
                    g0 = (j * gpc + gl) * grp
                    wide_ref[j, gl * grp:(gl + 1) * grp, gl * n_state:(gl + 1) * n_state] = val[g0:g0 + grp]
            ref[...] = wide_ref[...].astype(BF16)

    args = (lr, li, log_dt, br, bi, cr, ci)
    row_shape, wide_shape = (nch, 1, cs), (nch, rows, cs)
    outs = [jax.ShapeDtypeStruct(row_shape, F32)] * 2 + [jax.ShapeDtypeStruct(wide_shape, BF16)] * 4
    return pl.pallas_call(
        _ignoring_deps(body, 7, deps), name="disc_fwd",
        in_specs=[_whole(a.shape) for a in args] + [ANY_SPEC] * len(deps),
        out_specs=[_whole(o.shape) for o in outs], out_shape=outs,
        scratch_shapes=[pltpu.VMEM(wide_shape, F32)],
    )(*args, *deps)


def _disc_bwd(lr, li, log_dt, br, bi, dar, dai, dbbr, dbbi):
    def body(lr_ref, li_ref, dt_ref, br_ref, bi_ref, dar_ref, dai_ref, dbbr_ref, dbbi_ref,
             dlr_ref, dli_ref, ddt_ref, dbr_ref, dbi_ref):
        _, vjp = jax.vjp(_discretize, lr_ref[...], li_ref[...], dt_ref[...], br_ref[...], bi_ref[...])
        grads = vjp((dar_ref[...], dai_ref[...], dbbr_ref[...], dbbi_ref[...]))
        for ref, val in zip((dlr_ref, dli_ref, ddt_ref, dbr_ref, dbi_ref), grads):
            ref[...] = val

    args = (lr, li, log_dt, br, bi, dar, dai, dbbr, dbbi)
    outs = (lr, li, log_dt, br, bi)
    return pl.pallas_call(
        body, name="disc_bwd", in_specs=[_whole(a.shape) for a in args],
        out_specs=[_whole(a.shape) for a in outs],
        out_shape=[jax.ShapeDtypeStruct(a.shape, F32) for a in outs],
    )(*args)


def _adamw(w, g, m, v):
    m = ADAM_B1 * m + (1.0 - ADAM_B1) * g
    v = ADAM_B2 * v + (1.0 - ADAM_B2) * (g * g)
    m_hat = m / ADAM_BC1
    v_hat = v / ADAM_BC2
    delta = -ADAM_LR * (m_hat / (jnp.sqrt(v_hat) + ADAM_EPS) + ADAM_WD * w)
    return delta, m, v


def _adamw_reduce(name, parts, w, m, v):
    _, r, c = parts.shape
    tr = r
    for cand in (256, 176, 128):
        if r % cand == 0:
            tr = cand
            break

    def body(p_ref, w_ref, m_ref, v_ref, g_ref, d_ref, nm_ref, nv_ref):
        g = p_ref[0].astype(F32)
        for k in range(1, N_DEV):
            g = g + p_ref[k].astype(F32)
        delta, nm, nv = _adamw(w_ref[...], g, m_ref[...], v_ref[...])
        g_ref[...] = g
        d_ref[...] = delta
        nm_ref[...] = nm
        nv_ref[...] = nv

    blk = pl.BlockSpec((tr, c), lambda i: (i, 0))
    return pl.pallas_call(
        body, name=name, grid=(r // tr,),
        in_specs=[pl.BlockSpec((N_DEV, tr, c), lambda i: (0, i, 0)), blk, blk, blk],
        out_specs=[blk] * 4, out_shape=[jax.ShapeDtypeStruct((r, c), F32)] * 4,
        compiler_params=_params(),
    )(parts, w, m, v)


def _sum_parts(name, parts):
    _, r, c = parts.shape

    def body(p_ref, o_ref):
        g = p_ref[0]
        for k in range(1, N_DEV):
            g = g + p_ref[k]
        o_ref[...] = g

    return pl.pallas_call(
        body, name=name, in_specs=[_whole(parts.shape)], out_specs=_whole((r, c)),
        out_shape=jax.ShapeDtypeStruct((r, c), F32), compiler_params=_params(),
    )(parts)


def _adamw_many(name, grads, ws, ms, vs):
    n = len(grads)

    def body(*refs):
        ins, outs = refs[:4 * n], refs[4 * n:]
        for i in range(n):
            g, w, m, v = (ins[j * n + i][...] for j in range(4))
            for ref, val in zip((outs[i], outs[n + i], outs[2 * n + i]), _adamw(w, g, m, v)):
                ref[...] = val

    args = (*grads, *ws, *ms, *vs)
    outs = pl.pallas_call(
        body, name=name, in_specs=[_whole(a.shape) for a in args],
        out_specs=[_whole(a.shape) for a in ws] * 3,
        out_shape=[jax.ShapeDtypeStruct(a.shape, F32) for a in ws] * 3, compiler_params=_params(),
    )(*args)
    return outs[:n], outs[n:2 * n], outs[2 * n:]


def _pack(arrays, rows):
    flat = jnp.concatenate([a.reshape(-1) for a in arrays])
    return jnp.pad(flat, (0, rows * LANES - flat.shape[0])).reshape(rows, LANES)


def _unpack(packed, shapes):
    flat = packed.reshape(-1)
    out, off = [], 0
    for s in shapes:
        n = math.prod(s)
        out.append(flat[off:off + n].reshape(s))
        off += n
    return out


def _packed_rows(shapes):
    n = sum(math.prod(s) for s in shapes)
    return -(-n // (SUBLANES * LANES)) * SUBLANES


def kernel(x, meta_tokens, norm_mix_g, w_in, conv_w, ssm_lam_re, ssm_lam_im, ssm_log_dt, ssm_b_re, ssm_b_im, ssm_c_re, ssm_c_im, ssm_d, ssm_w_glu, gain_conv_out, gain_ssm_out, w_out, norm_ffn_g, w_up, ffn_conv_w, ffn_conv_b, w_down, norm_final_g, loss_target, m_meta_tokens, m_norm_mix_g, m_w_in, m_conv_w, m_ssm_lam_re, m_ssm_lam_im, m_ssm_log_dt, m_ssm_b_re, m_ssm_b_im, m_ssm_c_re, m_ssm_c_im, m_ssm_d, m_ssm_w_glu, m_gain_conv_out, m_gain_ssm_out, m_w_out, m_norm_ffn_g, m_w_up, m_ffn_conv_w, m_ffn_conv_b, m_w_down, m_norm_final_g, v_meta_tokens, v_norm_mix_g, v_w_in, v_conv_w, v_ssm_lam_re, v_ssm_lam_im, v_ssm_log_dt, v_ssm_b_re, v_ssm_b_im, v_ssm_c_re, v_ssm_c_im, v_ssm_d, v_ssm_w_glu, v_gain_conv_out, v_gain_ssm_out, v_w_out, v_norm_ffn_g, v_w_up, v_ffn_conv_w, v_ffn_conv_b, v_w_down, v_norm_final_g):
    weights = dict(meta_tokens=meta_tokens, norm_mix_g=norm_mix_g, w_in=w_in, conv_w=conv_w, ssm_lam_re=ssm_lam_re, ssm_lam_im=ssm_lam_im, ssm_log_dt=ssm_log_dt, ssm_b_re=ssm_b_re, ssm_b_im=ssm_b_im, ssm_c_re=ssm_c_re, ssm_c_im=ssm_c_im, ssm_d=ssm_d, ssm_w_glu=ssm_w_glu, gain_conv_out=gain_conv_out, gain_ssm_out=gain_ssm_out, w_out=w_out, norm_ffn_g=norm_ffn_g, w_up=w_up, ffn_conv_w=ffn_conv_w, ffn_conv_b=ffn_conv_b, w_down=w_down, norm_final_g=norm_final_g)
    mom_m = dict(meta_tokens=m_meta_tokens, norm_mix_g=m_norm_mix_g, w_in=m_w_in, conv_w=m_conv_w, ssm_lam_re=m_ssm_lam_re, ssm_lam_im=m_ssm_lam_im, ssm_log_dt=m_ssm_log_dt, ssm_b_re=m_ssm_b_re, ssm_b_im=m_ssm_b_im, ssm_c_re=m_ssm_c_re, ssm_c_im=m_ssm_c_im, ssm_d=m_ssm_d, ssm_w_glu=m_ssm_w_glu, gain_conv_out=m_gain_conv_out, gain_ssm_out=m_gain_ssm_out, w_out=m_w_out, norm_ffn_g=m_norm_ffn_g, w_up=m_w_up, ffn_conv_w=m_ffn_conv_w, ffn_conv_b=m_ffn_conv_b, w_down=m_w_down, norm_final_g=m_norm_final_g)
    mom_v = dict(meta_tokens=v_meta_tokens, norm_mix_g=v_norm_mix_g, w_in=v_w_in, conv_w=v_conv_w, ssm_lam_re=v_ssm_lam_re, ssm_lam_im=v_ssm_lam_im, ssm_log_dt=v_ssm_log_dt, ssm_b_re=v_ssm_b_re, ssm_b_im=v_ssm_b_im, ssm_c_re=v_ssm_c_re, ssm_c_im=v_ssm_c_im, ssm_d=v_ssm_d, ssm_w_glu=v_ssm_w_glu, gain_conv_out=v_gain_conv_out, gain_ssm_out=v_gain_ssm_out, w_out=v_w_out, norm_ffn_g=v_norm_ffn_g, w_up=v_w_up, ffn_conv_w=v_ffn_conv_w, ffn_conv_b=v_ffn_conv_b, w_down=v_w_down, norm_final_g=v_norm_final_g)
    names = list(weights)

    n_meta, d_meta = meta_tokens.shape
    seq, d = x.shape[1], x.shape[2]
    rows_used = n_meta + seq
    t = -(-rows_used // ROW_TILE) * ROW_TILE
    d_in_s = w_in.shape[2]
    dc_s = conv_w.shape[2]
    dc = dc_s * N_DEV
    ds = ssm_w_glu.shape[2]
    n_groups, n_state, grp = ssm_b_re.shape[1:]
    ns = n_groups * n_state
    nch = ds // SSM_CHUNK
    gpc = n_groups // nch
    ff_s = w_up.shape[2]
    dn_s = w_down.shape[1]
    assert 3 * dc + ds == d_in_s * N_DEV and 2 * dn_s == ff_s and t % (N_SEG * SUBLANES) == 0

    small_shard = jnp.concatenate([meta_tokens.reshape(-1), conv_w.reshape(-1), ffn_conv_w.reshape(-1)])
    n_small = small_shard.shape[0]
    small_rows = -(-n_small // LANES)
    small_shard = jnp.pad(small_shard, (0, small_rows * LANES - n_small)).reshape(small_rows, LANES)
    ag, ag_token = _gather2_start("gather_weights_start", [
        small_shard, w_in[0].astype(BF16), ssm_w_glu[0].astype(BF16), w_out[0].astype(BF16),
        jnp.swapaxes(w_up[0], 0, 1).astype(BF16), w_down[0].astype(BF16)])
    fb = ffn_conv_b

    gh = n_groups * grp
    per_h = lambda a: jnp.broadcast_to(a.reshape(n_groups, 1, -1), (n_groups, grp, n_state)).reshape(gh, n_state)
    ghp = lambda a: a.transpose(0, 1, 3, 2).reshape(gh, n_state)
    lr, li, log_dt_e = per_h(ssm_lam_re), per_h(ssm_lam_im), per_h(ssm_log_dt)
    br, bi = ghp(ssm_b_re), ghp(ssm_b_im)
    a_re_c, a_im_c, bbm_re, bbm_im, ccm_re, ccm_im = _disc_fwd(
        lr, li, log_dt_e, br, bi, ssm_c_re.reshape(gh, n_state), ssm_c_im.reshape(gh, n_state), nch, gpc,
        deps=(ag_token,))
    d_skip = ssm_d.reshape(nch, 1, SSM_CHUNK)

    target = _to_segments(jnp.pad(loss_target[0] + ag_token[0, 0], ((n_meta, t - rows_used), (0, 0))))
    token = _gather2_forward("gather_weights_forward_first", ag, (0, 1), (bbm_im, ccm_im, target))
    (g_small,) = _gather2_wait("gather_weights_wait_small", ag, (0,), token)
    g_small = g_small.reshape(N_DEV, -1)
    o1 = n_meta * d_meta
    o2 = o1 + 3 * dc_s
    meta_full = g_small[:, :o1].reshape(N_DEV, n_meta, d_meta).transpose(1, 0, 2).reshape(n_meta, d)
    conv_w_f = g_small[:, o1:o2].reshape(N_DEV, 3, dc_s).transpose(1, 0, 2).reshape(3, dc)
    fw = g_small[:, o2:o2 + 3 * ff_s].reshape(N_DEV, 3, ff_s).transpose(1, 0, 2).reshape(3, N_DEV * ff_s)
    h0 = _to_segments(jnp.concatenate([meta_full, x[0], jnp.zeros((t - rows_used, d), F32)], axis=0))
    full_t = lambda w: pl.BlockSpec((t, w), lambda *_: (0, 0))

    hn1 = _rms_fwd("norm_mix", h0, norm_mix_g)
    (g_in,) = _gather2_wait("gather_weights_wait_in", ag, (1,), hn1)
    proj = _mm("proj", hn1, g_in, dims=NN, grid=(N_DEV,), a_spec=full_t(d),
               b_spec=pl.BlockSpec((None, d, d_in_s), lambda j: (j, 0, 0)),
               o_spec=pl.BlockSpec((t, d_in_s), lambda j: (0, j)),
               out_shape=jax.ShapeDtypeStruct((t, N_DEV * d_in_s), F32))
    s_re, s_im, y_ssm = _s5_fwd(proj, 3 * dc, bbm_re, bbm_im, a_re_c, a_im_c, ccm_re, ccm_im, d_skip)
    token = _gather2_forward("gather_weights_forward_up", ag, (2, 3, 4), (y_ssm,))
    g_glu, g_out = _gather2_wait("gather_weights_wait_mix", ag, (2, 3), token)
    w_out_f = g_out.reshape(-1, d)
    w_glu_f = g_glu.reshape(ds, ds)
    mixed, z_glu, h1, hn2 = _mix_fwd(proj, y_ssm, w_glu_f, conv_w_f, gain_conv_out, gain_ssm_out,
                                     h0, w_out_f, norm_ffn_g)
    tn_out = OUT_TILE
    token = _gather2_forward("gather_weights_forward_down", ag, (5,), (hn2,))
    (g_up,) = _gather2_wait("gather_weights_wait_up", ag, (4,), token)
    f2 = N_DEV * ff_s
    pair = 2 * ff_s
    w_up_t = g_up.reshape(f2, d)
    up_pre = _mm("up_proj", hn2, w_up_t, dims=NT, grid=(f2 // pair,), a_spec=full_t(d),
                 b_spec=pl.BlockSpec((pair, d), lambda j: (j, 0)),
                 o_spec=pl.BlockSpec((t, pair), lambda j: (0, j)),
                 out_shape=jax.ShapeDtypeStruct((t, f2), BF16))
    act = _gated_fwd(up_pre, fw, fb)
    (g_down,) = _gather2_wait("gather_weights_wait_down", ag, (5,), act)
    w_down_f = g_down.reshape(f2 // 2, d)

    dh2, dh2_b, loss_part, d_norm_final = _down_loss_bwd(
        act, w_down_f, h1, target, norm_final_g.reshape(1, d), n_meta, rows_used)
    dw_down = _mm("down_wgrad", act, dh2_b, dims=TN, grid=(f2 // 2 // pair,),
                  a_spec=pl.BlockSpec((t, pair), lambda j: (0, j)), b_spec=full_t(d),
                  o_spec=pl.BlockSpec((pair, d), lambda j: (j, 0)),
                  out_shape=jax.ShapeDtypeStruct((f2 // 2, d), BF16))
    ex_down, token = _send_start("exchange_down_start", [dw_down.reshape(N_DEV, dn_s, d)], gather=False)
    dact = _mm("down_dgrad", dh2_b, w_down_f, dims=NT, grid=(f2 // 2 // pair,), a_spec=full_t(d),
               b_spec=pl.BlockSpec((pair, d), lambda j: (j, 0)),
               o_spec=pl.BlockSpec((t, pair), lambda j: (0, j)),
               out_shape=jax.ShapeDtypeStruct((t, f2 // 2), BF16))
    dup_pre, d_fw, d_fb = _gated_bwd(up_pre, dact, fw, fb, deps=(token,))
    dw_up = _mm("up_wgrad", dup_pre, hn2, dims=TN, grid=(f2 // pair,),
                a_spec=pl.BlockSpec((t, pair), lambda j: (0, j)), b_spec=full_t(d),
                o_spec=pl.BlockSpec((pair, d), lambda j: (j, 0)),
                out_shape=jax.ShapeDtypeStruct((f2, d), BF16))
    ex_up, token = _send_start("exchange_up_start", [dw_up.reshape(N_DEV, ff_s, d)], gather=False)
    dh1, dh1_b, d_norm_ffn = _dgrad_rms_bwd("up_dgrad_norm_bwd", dh2, dup_pre, w_up_t, NN, h1, norm_ffn_g,
                                            deps=(token,))
    dmixed = _mm("out_dgrad", dh1_b, w_out_f, dims=NT, grid=((dc + ds) // tn_out,), a_spec=full_t(d),
                 b_spec=pl.BlockSpec((tn_out, d), lambda i: (i, 0)),
                 o_spec=pl.BlockSpec((t, tn_out), lambda i: (0, i)),
                 out_shape=jax.ShapeDtypeStruct((t, dc + ds), F32))
    dw_out = _mm("out_wgrad", mixed, dh1_b, dims=TN, grid=((dc + ds) // tn_out,),
                 a_spec=pl.BlockSpec((t, tn_out), lambda i: (0, i)), b_spec=full_t(d),
                 o_spec=pl.BlockSpec((tn_out, d), lambda i: (i, 0)),
                 out_shape=jax.ShapeDtypeStruct((dc + ds, d), BF16))
    db_gate, dconv, dy_ssm, d_wglu, d_conv_w, d_gain_c, d_gain_s = _mix_bwd1(
        proj, y_ssm, z_glu, dmixed, w_glu_f, conv_w_f, gain_conv_out, gain_ssm_out)
    (du, d_bbm_re, d_bbm_im, d_a_re, d_a_im, d_ccm_re, d_ccm_im, d_dskip) = _s5_bwd(
        dy_ssm, proj, 3 * dc, s_re, s_im, bbm_re, bbm_im, a_re_c, a_im_c, ccm_re, ccm_im, d_skip, gpc)

    from_bb = from_cc = lambda a: a.reshape(gh, n_state)
    first_h = lambda a: jnp.pad(a.reshape(n_groups, 1, n_state), ((0, 0), (0, grp - 1), (0, 0))).reshape(gh, n_state)
    over_h = lambda a: a.reshape(n_groups, grp, n_state).sum(axis=1)
    d_lr, d_li, d_dt_e, d_br, d_bi = _disc_bwd(
        lr, li, log_dt_e, br, bi, first_h(d_a_re), first_h(d_a_im), from_bb(d_bbm_re), from_bb(d_bbm_im))

    rep2d = dict(
        ssm_lam_re=(n_groups, n_state), ssm_lam_im=(n_groups, n_state), ssm_log_dt=(1, n_groups),
        ssm_b_re=(gh, n_state), ssm_b_im=(gh, n_state), ssm_c_re=(gh, n_state),
        ssm_c_im=(gh, n_state), ssm_d=(n_groups, grp), gain_conv_out=(1, dc),
        gain_ssm_out=(1, ds), norm_ffn_g=(1, d), ffn_conv_b=(1, N_DEV * ff_s), norm_final_g=(1, d))
    rep_names = list(rep2d)
    rep_grads = dict(
        ssm_lam_re=over_h(d_lr), ssm_lam_im=over_h(d_li), ssm_log_dt=over_h(d_dt_e).sum(axis=1),
        ssm_b_re=d_br, ssm_b_im=d_bi, ssm_c_re=from_cc(d_ccm_re), ssm_c_im=from_cc(d_ccm_im),
        ssm_d=d_dskip, gain_conv_out=d_gain_c, gain_ssm_out=d_gain_s, norm_ffn_g=d_norm_ffn,
        ffn_conv_b=d_fb, norm_final_g=d_norm_final)
    rep_shapes = [rep2d[n] for n in rep_names] + [(1, 1)]
    rep_rows = _packed_rows(rep_shapes)
    rep_pack = _pack([rep_grads[n] for n in rep_names] + [loss_part], rep_rows)
    ex_mix, token = _send_start("exchange_mix_start", [
        rep_pack, dw_out.reshape(N_DEV, -1, d), d_wglu.astype(BF16).reshape(N_DEV, -1, ds),
        d_conv_w.reshape(3, N_DEV, dc_s).transpose(1, 0, 2),
        d_fw.reshape(3, N_DEV, ff_s).transpose(1, 0, 2)],
        gather=[True, False, False, False, False])
    dcdv = _mix_bwd2(proj, dconv, conv_w_f, deps=(token,))
    dproj = jnp.concatenate([db_gate, dcdv, du], axis=1)
    d_in = N_DEV * d_in_s
    w_in_rows = g_in.transpose(1, 0, 2).reshape(d, d_in)
    dw_in = _mm("proj_wgrad", hn1, dproj, dims=TN, grid=(N_DEV,), a_spec=full_t(d),
                b_spec=pl.BlockSpec((t, d_in_s), lambda j: (0, j)),
                o_spec=pl.BlockSpec((None, d, d_in_s), lambda j: (j, 0, 0)),
                out_shape=jax.ShapeDtypeStruct((N_DEV, d, d_in_s), BF16))
    dh0, _, d_norm_mix = _dgrad_rms_bwd("proj_dgrad_norm_bwd", dh1, dproj, w_in_rows, NT, h0, norm_mix_g)
    dh0 = _from_segments(dh0)
    grad_x = dh0[n_meta:rows_used][None]
    d_meta_b = dh0[:n_meta].reshape(n_meta, N_DEV, d_meta).transpose(1, 0, 2)
    ex_in, ex_in_token = _send_start("exchange_in_start", [dw_in, d_meta_b, d_norm_mix],
                                     gather=[False, False, True])

    shard_out = {}

    def update(n, parts, transposed=False):
        sh = weights[n].shape
        two_d = lambda a: (jnp.swapaxes(a[0], 0, 1) if transposed else a.reshape(parts.shape[1:]))
        res = _adamw_reduce("adamw_" + n, parts, two_d(weights[n]), two_d(mom_m[n]), two_d(mom_v[n]))
        shard_out[n] = [(jnp.swapaxes(r, 0, 1) if transposed else r).reshape(sh) for r in res]
        return res[0]

    (p_down,) = _send_wait("exchange_down_wait", ex_down, (0,), ex_in_token)
    done = update("w_down", p_down)
    (p_up,) = _send_wait("exchange_up_wait", ex_up, (0,), done)
    done = update("w_up", p_up, transposed=True)
    (rep_parts,) = _send_wait("gather_small_grads_wait", ex_mix, (0,), done)
    rep_sum = _sum_parts("sum_small_grads", rep_parts)
    *rep_g, loss = _unpack(rep_sum, rep_shapes)
    loss = loss.reshape(())
    swapped = ("ssm_b_re", "ssm_b_im")
    to2d = lambda n, a: ghp(a) if n in swapped else a.reshape(rep2d[n])
    from2d = lambda n, a: (a.reshape(1, n_groups, grp, n_state).transpose(0, 1, 3, 2) if n in swapped
                           else a.reshape(weights[n].shape))
    as2d = lambda tree: [to2d(n, tree[n]) for n in rep_names]
    rep_res = _adamw_many("adamw_replicated", rep_g, as2d(weights), as2d(mom_m), as2d(mom_v))
    for i, n in enumerate(rep_names):
        shard_out[n] = [from2d(n, r) for r in (rep_g[i], *(res[i] for res in rep_res))]

    p_out, p_glu, p_cw, p_fw = _send_wait("exchange_mix_wait", ex_mix, (1, 2, 3, 4), rep_sum)
    update("w_out", p_out)
    update("ssm_w_glu", p_glu)
    update("conv_w", p_cw)
    done = update("ffn_conv_w", p_fw)
    p_in, p_meta, p_nm = _send_wait("exchange_in_wait", ex_in, (0, 1, 2), done)
    update("w_in", p_in)
    update("meta_tokens", p_meta)
    update("norm_mix_g", p_nm)

    grads = [shard_out[n][0] for n in names]
    deltas = [shard_out[n][1] for n in names]
    new_m = [shard_out[n][2] for n in names]
    new_v = [shard_out[n][3] for n in names]
    return (loss, grad_x, *grads, *deltas, *new_m, *new_v)
```

```python
import math

import jax
import jax.numpy as jnp
from jax import lax
from jax.experimental import pallas as pl
from jax.experimental.pallas import tpu as pltpu

F32 = jnp.float32
BF16 = jnp.bfloat16
MESH = pl.DeviceIdType.MESH

N_DEV = 8
RMS_EPS = 1e-6
ADAM_LR = 0.001
ADAM_B1 = 0.9
ADAM_B2 = 0.999
ADAM_EPS = 1e-08
ADAM_WD = 0.01
ADAM_STEP = 10
ADAM_BC1 = 1.0 - ADAM_B1 ** ADAM_STEP
ADAM_BC2 = 1.0 - ADAM_B2 ** ADAM_STEP

SUBLANES = 8
LANES = 128
ROW_TILE = 128
ROW_CHUNK = 32
WIDE_TILES = (544, 256, 128)
FFN_COLS = 256
STAGE_COLS = 256
OUT_TILE = 512
N_SEG = 8
HALO_ROWS = 16
SSM_CHUNK = 128
SCAN_UNROLL = 8
VMEM_LIMIT = 48 * 1024 * 1024

NN = ((1,), (0,))
NT = ((1,), (1,))
TN = ((0,), (0,))


def _params(**kw):
    return pltpu.CompilerParams(vmem_limit_bytes=VMEM_LIMIT, **kw)


def _dot(a, b, dims):
    return lax.dot_general(a, b, (dims, ((), ())), preferred_element_type=F32)


def _mean_sq_rsqrt(x):
    return lax.rsqrt(jnp.mean(x * x, axis=-1, keepdims=True) + RMS_EPS)


def _rms_bwd(x, r, g, dy):
    xhat = x * r
    dxh = dy * g
    dx = r * (dxh - xhat * jnp.mean(dxh * xhat, axis=-1, keepdims=True))
    return dx, dy * xhat


def _gelu(y):
    c = math.sqrt(2.0 / math.pi)
    t = jnp.tanh(c * (y + 0.044715 * y * y * y))
    return 0.5 * y * (1.0 + t), t


def _gelu_grad(y, t):
    c = math.sqrt(2.0 / math.pi)
    return 0.5 * (1.0 + t) + 0.5 * y * (1.0 - t * t) * c * (1.0 + 3.0 * 0.044715 * y * y)


def _wrap_prev_halo(halo, first):
    seg = lax.broadcasted_iota(jnp.int32, halo.shape, 0) % N_SEG
    wrapped = jnp.where(seg == 0, 0.0, pltpu.roll(halo, 1, 0))
    return jnp.where(first, wrapped, halo)


def _wrap_next_halo(halo, last):
    seg = lax.broadcasted_iota(jnp.int32, halo.shape, 0) % N_SEG
    wrapped = jnp.where(seg == N_SEG - 1, 0.0, pltpu.roll(halo, halo.shape[0] - 1, 0))
    return jnp.where(last, wrapped, halo)


def _dev_index(p):
    return 4 * p[0] + 2 * p[1] + p[2]


def _allgather(name, shards, deps=()):
    n = len(shards)

    def body(*refs):
        ins, outs = refs[:n], refs[n:2 * n]
        send_sems, recv_sems, local_sems = refs[2 * n:]
        x, y, c = lax.axis_index("x"), lax.axis_index("y"), lax.axis_index("c")
        me, sibling = (x, y, c), (x, y, 1 - c)
        chips = [(1 - x, y), (x, 1 - y), (1 - x, 1 - y)]

        def copy(a, k, block, to, src=None):
            dst = outs[a].at[_dev_index(block)]
            return pltpu.make_async_remote_copy(
                src_ref=dst if src is None else src, dst_ref=dst,
                send_sem=send_sems.at[a, k], recv_sem=recv_sems.at[a, k],
                device_id=to, device_id_type=MESH)

        mine = [pltpu.make_async_copy(ins[a], outs[a].at[_dev_index(me)], local_sems.at[a])
                for a in range(n)]
        for cp in mine:
            cp.start()
        first = []
        for a in range(n):
            first.append(copy(a, 0, me, sibling, src=ins[a]))
            for j, chip in enumerate(chips):
                first.append(copy(a, 1 + j, me, (*chip, c), src=ins[a]))
        for cp in first:
            cp.start()
        passed = []
        for j, chip in enumerate(chips):
            for a in range(n):
                copy(a, 1 + j, (*chip, c), me).wait_recv()
                fwd = copy(a, 4 + j, (*chip, c), sibling)
                fwd.start()
                passed.append(fwd)
        for a in range(n):
            copy(a, 0, sibling, me).wait_recv()
            for j, chip in enumerate(chips):
                copy(a, 4 + j, (*chip, 1 - c), me).wait_recv()
        for cp in first + passed:
            cp.wait_send()
        for cp in mine:
            cp.wait()

    any_spec = pl.BlockSpec(memory_space=pl.ANY)
    return pl.pallas_call(
        _ignoring_deps(body, n, deps), name=name,
        out_shape=[jax.ShapeDtypeStruct((N_DEV,) + s.shape, s.dtype) for s in shards],
        in_specs=[any_spec] * (n + len(deps)), out_specs=[any_spec] * n,
        scratch_shapes=[pltpu.SemaphoreType.DMA((n, 7)), pltpu.SemaphoreType.DMA((n, 7)),
                        pltpu.SemaphoreType.DMA((n,))],
    )(*shards, *deps)


HBM_SPEC = pl.BlockSpec(memory_space=pltpu.HBM)
SEM_SPEC = pl.BlockSpec(memory_space=pltpu.SEMAPHORE)
ANY_SPEC = pl.BlockSpec(memory_space=pl.ANY)
DATAFLOW = pltpu.SideEffectType.DATAFLOW_SIDE_EFFECTING


def _ignoring_deps(body, n_in, deps):
    n_dep = len(deps)

    def wrapped(*refs):
        return body(*refs[:n_in], *refs[n_in + n_dep:])

    return wrapped


def _my_position():
    x, y, c = lax.axis_index("x"), lax.axis_index("y"), lax.axis_index("c")
    return (x, y, c)


def _peer(me, k):
    return tuple((1 - v) if (k >> s) & 1 else v for v, s in zip(me, (2, 1, 0)))


def _split_copies(src_refs, land_refs, send_sems, recv_sems, gather):
    me = _my_position()
    copies = []
    for a, (src, land) in enumerate(zip(src_refs, land_refs)):
        for k in range(1, N_DEV):
            peer = _peer(me, k)
            copies.append(pltpu.make_async_remote_copy(
                src_ref=src if gather[a] else src.at[_dev_index(peer)], dst_ref=land.at[_dev_index(me)],
                send_sem=send_sems[a].at[k - 1], recv_sem=recv_sems[a].at[k - 1],
                device_id=peer, device_id_type=MESH))
    return copies


def _own_slot(block, like_shape):
    me = _dev_index(_my_position())
    return lax.dynamic_update_index_in_dim(lax.empty(like_shape, block.dtype), block, me, 0)


def _send_start(name, srcs, gather):
    n = len(srcs)
    me = _dev_index(_my_position())
    gather = [gather] * n if isinstance(gather, bool) else list(gather)
    lands = [_own_slot(s, (N_DEV,) + s.shape) if g else
             _own_slot(lax.dynamic_index_in_dim(s, me, 0, keepdims=False), s.shape)
             for s, g in zip(srcs, gather)]

    def body(*refs):
        src_refs, land_refs = refs[:n], refs[n:2 * n]
        send_sems, recv_sems = refs[2 * n:3 * n], refs[3 * n:4 * n]
        token = refs[-1]
        for cp in _split_copies(src_refs, land_refs, send_sems, recv_sems, gather):
            cp.start()
        token[...] = jnp.zeros_like(token)

    hbm = lambda a: pltpu.HBM(a.shape, a.dtype)
    sems = [pltpu.SemaphoreType.DMA((N_DEV - 1,))] * n
    outs = pl.pallas_call(
        body, name=name,
        out_shape=(*sems, *sems, *[hbm(s) for s in srcs], *[hbm(l) for l in lands],
                   jax.ShapeDtypeStruct((SUBLANES, LANES), F32)),
        in_specs=[HBM_SPEC] * (2 * n),
        out_specs=(*[SEM_SPEC] * (2 * n), *[HBM_SPEC] * (2 * n), pl.BlockSpec(memory_space=pltpu.VMEM)),
        input_output_aliases={i: 2 * n + i for i in range(2 * n)},
        compiler_params=pltpu.CompilerParams(has_side_effects=DATAFLOW),
    )(*[pltpu.with_memory_space_constraint(a, pltpu.HBM) for a in (*srcs, *lands)])
    state = dict(send=outs[:n], recv=outs[n:2 * n], srcs=outs[2 * n:3 * n], lands=outs[3 * n:4 * n],
                 gather=gather)
    return state, outs[-1]


def _send_wait(name, state, which, after):
    n = len(which)
    pick = lambda key: [state[key][i] for i in which]
    gather = pick("gather")

    def body(*refs):
        src_refs, land_refs = refs[:n], refs[n:2 * n]
        send_sems, recv_sems = refs[2 * n:3 * n], refs[3 * n:4 * n]
        for cp in _split_copies(src_refs, land_refs, send_sems, recv_sems, gather):
            cp.wait_send()
            cp.wait_recv()

    srcs, lands = pick("srcs"), pick("lands")
    hbm = lambda a: pltpu.HBM(a.shape, a.dtype)
    outs = pl.pallas_call(
        body, name=name,
        out_shape=(*[hbm(s) for s in srcs], *[hbm(l) for l in lands]),
        in_specs=[*[HBM_SPEC] * (2 * n), *[SEM_SPEC] * (2 * n), ANY_SPEC],
        out_specs=tuple([HBM_SPEC] * (2 * n)),
        input_output_aliases={i: i for i in range(2 * n)},
        compiler_params=pltpu.CompilerParams(has_side_effects=DATAFLOW),
    )(*srcs, *lands, *pick("send"), *pick("recv"), after)
    return outs[n:]


def _two_level_copies(src_refs, land_refs, sems1, sems2):
    x, y, c = _my_position()
    me, sibling = (x, y, c), (x, y, 1 - c)
    chips = [(1 - x, y), (x, 1 - y), (1 - x, 1 - y)]
    stage1, stage2 = [], []
    for a, land in enumerate(land_refs):
        def copy(src, block, to, sems, k):
            return pltpu.make_async_remote_copy(
                src_ref=src, dst_ref=land.at[_dev_index(block)], send_sem=sems[0][a].at[k],
                recv_sem=sems[1][a].at[k], device_id=to, device_id_type=MESH)
        src = src_refs[a] if src_refs is not None else land.at[_dev_index(me)]
        stage1.append([copy(src, me, sibling, sems1, 0)] +
                      [copy(src, me, (*chip, c), sems1, 1 + j) for j, chip in enumerate(chips)])
        if sems2 is not None:
            stage2.append([copy(land.at[_dev_index((*chip, c))], (*chip, c), sibling, sems2, j)
                           for j, chip in enumerate(chips)])
    return stage1, stage2


def _gather2_start(name, shards):
    n = len(shards)
    lands = [_own_slot(s, (N_DEV,) + s.shape) for s in shards]

    def body(*refs):
        src_refs, land_refs = refs[:n], refs[n:2 * n]
        sems1 = (refs[2 * n:3 * n], refs[3 * n:4 * n])
        stage1, _ = _two_level_copies(src_refs, land_refs, sems1, None)
        for copies in stage1:
            for cp in copies:
                cp.start()
        refs[-1][...] = jnp.zeros_like(refs[-1])

    hbm = lambda a: pltpu.HBM(a.shape, a.dtype)
    sems = [pltpu.SemaphoreType.DMA((4,))] * n
    outs = pl.pallas_call(
        body, name=name,
        out_shape=(*sems, *sems, *[hbm(s) for s in shards], *[hbm(l) for l in lands],
                   jax.ShapeDtypeStruct((SUBLANES, LANES), F32)),
        in_specs=[HBM_SPEC] * (2 * n),
        out_specs=(*[SEM_SPEC] * (2 * n), *[HBM_SPEC] * (2 * n), pl.BlockSpec(memory_space=pltpu.VMEM)),
        input_output_aliases={i: 2 * n + i for i in range(2 * n)},
        compiler_params=pltpu.CompilerParams(has_side_effects=DATAFLOW),
    )(*[pltpu.with_memory_space_constraint(a, pltpu.HBM) for a in (*shards, *lands)])
    state = dict(send1=list(outs[:n]), recv1=list(outs[n:2 * n]), srcs=list(outs[2 * n:3 * n]),
                 lands=list(outs[3 * n:4 * n]), send2={}, recv2={})
    return state, outs[-1]


def _gather2_forward(name, state, which, after):
    n = len(which)
    pick = lambda key: [state[key][i] for i in which]

    def body(*refs):
        land_refs, recv1 = refs[:n], refs[n:2 * n]
        outs = refs[2 * n + len(after):]
        sems2 = (outs[:n], outs[n:2 * n])
        stage1, stage2 = _two_level_copies(None, land_refs, (recv1, recv1), sems2)
        for a in range(n):
            for j in range(3):
                stage1[a][1 + j].wait_recv()
                stage2[a][j].start()
        outs[-1][...] = jnp.zeros_like(outs[-1])

    lands = pick("lands")
    sems = [pltpu.SemaphoreType.DMA((3,))] * n
    outs = pl.pallas_call(
        body, name=name,
        out_shape=(*sems, *sems, *[pltpu.HBM(l.shape, l.dtype) for l in lands],
                   jax.ShapeDtypeStruct((SUBLANES, LANES), F32)),
        in_specs=[*[HBM_SPEC] * n, *[SEM_SPEC] * n, *[ANY_SPEC] * len(after)],
        out_specs=(*[SEM_SPEC] * (2 * n), *[HBM_SPEC] * n, pl.BlockSpec(memory_space=pltpu.VMEM)),
        input_output_aliases={i: 2 * n + i for i in range(n)},
        compiler_params=pltpu.CompilerParams(has_side_effects=DATAFLOW),
    )(*lands, *pick("recv1"), *after)
    for idx, i in enumerate(which):
        state["send2"][i], state["recv2"][i] = outs[idx], outs[n + idx]
        state["lands"][i] = outs[2 * n + idx]
    return outs[-1]


def _gather2_wait(name, state, which, after):
    n = len(which)
    pick = lambda key: [state[key][i] for i in which]

    def body(*refs):
        src_refs, land_refs = refs[:n], refs[n:2 * n]
        sems1 = (refs[2 * n:3 * n], refs[3 * n:4 * n])
        sems2 = (refs[4 * n:5 * n], refs[5 * n:6 * n])
        stage1, stage2 = _two_level_copies(src_refs, land_refs, sems1, sems2)
        for a in range(n):
            for cp in stage1[a]:
                cp.wait_send()
            stage1[a][0].wait_recv()
            for cp in stage2[a]:
                cp.wait_send()
                cp.wait_recv()

    srcs, lands = pick("srcs"), pick("lands")
    hbm = lambda a: pltpu.HBM(a.shape, a.dtype)
    outs = pl.pallas_call(
        body, name=name,
        out_shape=(*[hbm(s) for s in srcs], *[hbm(l) for l in lands]),
        in_specs=[*[HBM_SPEC] * (2 * n), *[SEM_SPEC] * (4 * n), ANY_SPEC],
        out_specs=tuple([HBM_SPEC] * (2 * n)),
        input_output_aliases={i: i for i in range(2 * n)},
        compiler_params=pltpu.CompilerParams(has_side_effects=DATAFLOW),
    )(*srcs, *lands, *pick("send1"), *pick("recv1"), *pick("send2"), *pick("recv2"), after)
    return outs[n:]


def _mm(name, a, b, *, dims, grid, a_spec, b_spec, o_spec, out_shape, acc_shape=None,
        res=None, res_spec=None):
    n_red = grid[-1] if acc_shape is not None else 1
    red_axis = len(grid) - 1

    def body(*refs):
        a_ref, b_ref = refs[0], refs[1]
        r_ref = refs[2] if res is not None else None
        o_ref = refs[3] if res is not None else refs[2]
        part = _dot(a_ref[...], b_ref[...], dims)
        if acc_shape is None:
            if r_ref is not None:
                part = part + r_ref[...]
            o_ref[...] = part.astype(o_ref.dtype)
            return
        acc_ref = refs[-1]
        k = pl.program_id(red_axis)

        @pl.when(k == 0)
        def _():
            acc_ref[...] = part

        @pl.when(k > 0)
        def _():
            acc_ref[...] += part

        @pl.when(k == n_red - 1)
        def _():
            total = acc_ref[...]
            if r_ref is not None:
                total = total + r_ref[...]
            o_ref[...] = total.astype(o_ref.dtype)

    ins, in_specs = [a, b], [a_spec, b_spec]
    if res is not None:
        ins.append(res)
        in_specs.append(res_spec)
    return pl.pallas_call(
        body, name=name, grid=grid, in_specs=in_specs, out_specs=o_spec, out_shape=out_shape,
        scratch_shapes=[pltpu.VMEM(acc_shape, F32)] if acc_shape is not None else [],
        compiler_params=_params(),
    )(*ins)


def _wide_tile(t):
    return next(c for c in WIDE_TILES if t % c == 0)


def _rms_fwd(name, h, g):
    t, d = h.shape
    tile = _wide_tile(t)

    def body(h_ref, g_ref, o_ref):
        x = h_ref[...]
        o_ref[...] = (x * _mean_sq_rsqrt(x) * g_ref[...]).astype(BF16)

    return pl.pallas_call(
        body, name=name, grid=(t // tile,),
        in_specs=[pl.BlockSpec((tile, d), lambda i: (i, 0)), pl.BlockSpec((1, d), lambda i: (0, 0))],
        out_specs=pl.BlockSpec((tile, d), lambda i: (i, 0)),
        out_shape=jax.ShapeDtypeStruct((t, d), BF16), compiler_params=_params(),
    )(h, g)


def _fused_tile(t):
    half = _wide_tile(t) // 2
    return half if half % 16 == 0 and t % half == 0 else _wide_tile(t)


def _dgrad_rms_bwd(name, dres, dy, w, dims, h, g, deps=()):
    t, d = h.shape
    k = dy.shape[1]

    def body(dres_ref, dy_ref, w_ref, h_ref, g_ref, dh_ref, dhb_ref, dg_ref):
        x = h_ref[...]
        dhn = _dot(dy_ref[...], w_ref[...], dims)
        dx, dgt = _rms_bwd(x, _mean_sq_rsqrt(x), g_ref[...], dhn)
        dh = dres_ref[...] + dx
        dh_ref[...] = dh
        dhb_ref[...] = dh.astype(BF16)

        @pl.when(pl.program_id(0) == 0)
        def _():
            dg_ref[...] = jnp.zeros_like(dg_ref)

        dg_ref[...] += jnp.sum(dgt, axis=0, keepdims=True)

    tile = _fused_tile(t)
    row = pl.BlockSpec((tile, d), lambda i: (i, 0))
    vec = pl.BlockSpec((1, d), lambda i: (0, 0))
    return pl.pallas_call(
        _ignoring_deps(body, 5, deps), name=name, grid=(t // tile,),
        in_specs=[row, pl.BlockSpec((tile, k), lambda i: (i, 0)), pl.BlockSpec(w.shape, lambda i: (0, 0)), row,
                  vec] + [ANY_SPEC] * len(deps),
        out_specs=[row, row, vec],
        out_shape=[jax.ShapeDtypeStruct((t, d), F32), jax.ShapeDtypeStruct((t, d), BF16),
                   jax.ShapeDtypeStruct((1, d), F32)],
        compiler_params=_params(),
    )(dres, dy, w, h, g, *deps)


def _down_loss_bwd(act, w_down, h1, target, g, row_lo, row_hi):
    t, d = h1.shape
    f = act.shape[1]
    tile = _fused_tile(t)

    def body(act_ref, w_ref, h_ref, tg_ref, g_ref, dh_ref, dhb_ref, loss_ref, dg_ref):
        i = pl.program_id(0)
        x = h_ref[...] + _dot(act_ref[...], w_ref[...], NN)
        r = _mean_sq_rsqrt(x)
        gv = g_ref[...]
        y = x * r * gv
        rows = i * tile + lax.broadcasted_iota(jnp.int32, (tile, 1), 0)
        time = (rows % N_SEG) * (t // N_SEG) + rows // N_SEG
        valid = jnp.logical_and(time >= row_lo, time < row_hi)
        err = jnp.where(valid, y - tg_ref[...], 0.0)
        dy = err * (1.0 / d)
        dx, dgt = _rms_bwd(x, r, gv, dy)
        dh_ref[...] = dx
        dhb_ref[...] = dx.astype(BF16)

        @pl.when(i == 0)
        def _():
            loss_ref[...] = jnp.zeros_like(loss_ref)
            dg_ref[...] = jnp.zeros_like(dg_ref)

        row_loss = jnp.mean(err * err, axis=-1, keepdims=True)
        loss_ref[...] += 0.5 * jnp.sum(row_loss, axis=0, keepdims=True)
        dg_ref[...] += jnp.sum(dgt, axis=0, keepdims=True)

    row = pl.BlockSpec((tile, d), lambda i: (i, 0))
    vec = pl.BlockSpec((1, d), lambda i: (0, 0))
    return pl.pallas_call(
        body, name="down_proj_loss_bwd", grid=(t // tile,),
        in_specs=[pl.BlockSpec((tile, f), lambda i: (i, 0)), pl.BlockSpec(w_down.shape, lambda i: (0, 0)),
                  row, row, vec],
        out_specs=[row, row, pl.BlockSpec((1, 1), lambda i: (0, 0)), vec],
        out_shape=[jax.ShapeDtypeStruct((t, d), F32), jax.ShapeDtypeStruct((t, d), BF16),
                   jax.ShapeDtypeStruct((1, 1), F32), jax.ShapeDtypeStruct((1, d), F32)],
        compiler_params=_params(),
    )(act, w_down, h1, target, g)


def _prev_halo(i, t, tile=ROW_TILE):
    return jnp.where(i == 0, t // HALO_ROWS - 1, i * (tile // HALO_ROWS) - 1)


def _next_halo(i, t, tile=ROW_TILE):
    return jnp.where(i == t // tile - 1, 0, (i + 1) * (tile // HALO_ROWS))


def _causal_taps(cur, halo, first):
    rows = cur.shape[0]
    ext = jnp.concatenate([_wrap_prev_halo(halo, first), cur], axis=0)
    return ext[:rows], ext[N_SEG:N_SEG + rows]


def _anticausal_taps(cur, halo, last):
    rows = cur.shape[0]
    ext = jnp.concatenate([cur, _wrap_next_halo(halo, last)], axis=0)
    return ext[N_SEG:N_SEG + rows], ext[2 * N_SEG:2 * N_SEG + rows]


def _mix_fwd(proj, y, w_glu, conv_w, gain_c, gain_s, h0, w_out, gain_ffn):
    t, d = h0.shape
    dc = conv_w.shape[1]
    ds = y.shape[1]

    def body(p_ref, halo_ref, y_ref, wg_ref, cw_ref, gc_ref, gs_ref, h0_ref, wo_ref, gf_ref,
             mixed_ref, z_ref, h1_ref, hn_ref):
        i = pl.program_id(0)
        p = p_ref[...]
        b, c, v = p[:, :dc], p[:, dc:2 * dc], p[:, 2 * dc:3 * dc]
        cv = c * v
        hp = halo_ref[...]
        x2, x1 = _causal_taps(cv, hp[:, dc:2 * dc] * hp[:, 2 * dc:3 * dc], i == 0)
        cw = cw_ref[...]
        conv = cw[0:1] * x2 + cw[1:2] * x1 + cw[2:3] * cv
        co = b * conv
        mixed_ref[:, :dc] = (co * _mean_sq_rsqrt(co) * gc_ref[...]).astype(BF16)
        g, _ = _gelu(y_ref[...])
        z = _dot(g.astype(BF16), wg_ref[...], NN)
        z_ref[...] = z
        so = g * jax.nn.sigmoid(z)
        mixed_ref[:, dc:] = (so * _mean_sq_rsqrt(so) * gs_ref[...]).astype(BF16)
        h1 = h0_ref[...] + _dot(mixed_ref[...], wo_ref[...], NN)
        h1_ref[...] = h1
        hn_ref[...] = (h1 * _mean_sq_rsqrt(h1) * gf_ref[...]).astype(BF16)

    const = lambda i: (0, 0)
    tile = _fused_tile(t)
    row = lambda w: pl.BlockSpec((tile, w), lambda i: (i, 0))
    return pl.pallas_call(
        body, name="mix_fwd_out_proj", grid=(t // tile,),
        in_specs=[row(3 * dc), pl.BlockSpec((HALO_ROWS, 3 * dc), lambda i: (_prev_halo(i, t, tile), 0)), row(ds),
                  pl.BlockSpec((ds, ds), const), pl.BlockSpec(conv_w.shape, const),
                  pl.BlockSpec((1, dc), const), pl.BlockSpec((1, ds), const),
                  row(d), pl.BlockSpec(w_out.shape, const), pl.BlockSpec((1, d), const)],
        out_specs=[row(dc + ds), row(ds), row(d), row(d)],
        out_shape=[jax.ShapeDtypeStruct((t, dc + ds), BF16), jax.ShapeDtypeStruct((t, ds), F32),
                   jax.ShapeDtypeStruct((t, d), F32), jax.ShapeDtypeStruct((t, d), BF16)],
        compiler_params=_params(),
    )(proj, proj, y, w_glu, conv_w, gain_c, gain_s, h0, w_out, gain_ffn)


def _mix_bwd1(proj, y, z, dmixed, w_glu, conv_w, gain_c, gain_s):
    t = proj.shape[0]
    dc = conv_w.shape[1]
    ds = y.shape[1]

    def body(p_ref, halo_ref, y_ref, z_ref, dm_ref, wg_ref, cw_ref, gc_ref, gs_ref,
             db_ref, dconv_ref, dy_ref, dwg_ref, dcw_ref, dgc_ref, dgs_ref):
        i = pl.program_id(0)

        @pl.when(i == 0)
        def _():
            dwg_ref[...] = jnp.zeros_like(dwg_ref)
            dcw_ref[...] = jnp.zeros_like(dcw_ref)
            dgc_ref[...] = jnp.zeros_like(dgc_ref)
            dgs_ref[...] = jnp.zeros_like(dgs_ref)

        p = p_ref[...]
        b, c, v = p[:, :dc], p[:, dc:2 * dc], p[:, 2 * dc:3 * dc]
        cv = c * v
        hp = halo_ref[...]
        x2, x1 = _causal_taps(cv, hp[:, dc:2 * dc] * hp[:, 2 * dc:3 * dc], i == 0)
        cw = cw_ref[...]
        conv = cw[0:1] * x2 + cw[1:2] * x1 + cw[2:3] * cv
        co = b * conv
        dm = dm_ref[...]
        dco, dgt = _rms_bwd(co, _mean_sq_rsqrt(co), gc_ref[...], dm[:, :dc])
        dgc_ref[...] += jnp.sum(dgt, axis=0, keepdims=True)
        db_ref[...] = (dco * conv).astype(BF16)
        dconv = dco * b
        dconv_ref[...] = dconv
        dcw_ref[0:1, :] += jnp.sum(dconv * x2, axis=0, keepdims=True)
        dcw_ref[1:2, :] += jnp.sum(dconv * x1, axis=0, keepdims=True)
        dcw_ref[2:3, :] += jnp.sum(dconv * cv, axis=0, keepdims=True)

        yv = y_ref[...]
        g, th = _gelu(yv)
        sg = jax.nn.sigmoid(z_ref[...])
        so = g * sg
        dso, dgt = _rms_bwd(so, _mean_sq_rsqrt(so), gs_ref[...], dm[:, dc:])
        dgs_ref[...] += jnp.sum(dgt, axis=0, keepdims=True)
        dz = (dso * g * sg * (1.0 - sg)).astype(BF16)
        dg = dso * sg + _dot(dz, wg_ref[...], NT)
        dwg_ref[...] += _dot(g.astype(BF16), dz, TN)
        dy_ref[...] = dg * _gelu_grad(yv, th)

    const = lambda i: (0, 0)
    tile = _fused_tile(t)
    row = lambda w: pl.BlockSpec((tile, w), lambda i: (i, 0))
    return pl.pallas_call(
        body, name="mix_bwd1", grid=(t // tile,),
        in_specs=[row(3 * dc), pl.BlockSpec((HALO_ROWS, 3 * dc), lambda i: (_prev_halo(i, t, tile), 0)),
                  row(ds), row(ds), row(dc + ds), pl.BlockSpec((ds, ds), const),
                  pl.BlockSpec(conv_w.shape, const), pl.BlockSpec((1, dc), const),
                  pl.BlockSpec((1, ds), const)],
        out_specs=[row(dc), row(dc), row(ds), pl.BlockSpec((ds, ds), const),
                   pl.BlockSpec(conv_w.shape, const), pl.BlockSpec((1, dc), const),
                   pl.BlockSpec((1, ds), const)],
        out_shape=[jax.ShapeDtypeStruct((t, dc), BF16), jax.ShapeDtypeStruct((t, dc), F32),
                   jax.ShapeDtypeStruct((t, ds), F32), jax.ShapeDtypeStruct((ds, ds), F32),
                   jax.ShapeDtypeStruct(conv_w.shape, F32), jax.ShapeDtypeStruct((1, dc), F32),
                   jax.ShapeDtypeStruct((1, ds), F32)],
        compiler_params=_params(),
    )(proj, proj, y, z, dmixed, w_glu, conv_w, gain_c, gain_s)


def _mix_bwd2(proj, dconv, conv_w, deps=()):
    t = proj.shape[0]
    dc = conv_w.shape[1]
    tile = _wide_tile(t)
    n_tiles = t // tile

    def body(c_ref, v_ref, d_ref, halo_ref, cw_ref, o_ref):
        i = pl.program_id(0)
        d = d_ref[...]
        u1, u2 = _anticausal_taps(d, halo_ref[...], i == n_tiles - 1)
        cw = cw_ref[...]
        dcv = cw[2:3] * d + cw[1:2] * u1 + cw[0:1] * u2
        o_ref[:, :dc] = (dcv * v_ref[...]).astype(BF16)
        o_ref[:, dc:] = (dcv * c_ref[...]).astype(BF16)

    return pl.pallas_call(
        _ignoring_deps(body, 5, deps), name="mix_bwd2", grid=(n_tiles,),
        in_specs=[pl.BlockSpec((tile, dc), lambda i: (i, 1)),
                  pl.BlockSpec((tile, dc), lambda i: (i, 2)),
                  pl.BlockSpec((tile, dc), lambda i: (i, 0)),
                  pl.BlockSpec((HALO_ROWS, dc), lambda i: (_next_halo(i, t, tile), 0)),
                  pl.BlockSpec(conv_w.shape, lambda i: (0, 0))] + [ANY_SPEC] * len(deps),
        out_specs=pl.BlockSpec((tile, 2 * dc), lambda i: (i, 0)),
        out_shape=jax.ShapeDtypeStruct((t, 2 * dc), BF16), compiler_params=_params(),
    )(proj, proj, dconv, dconv, conv_w, *deps)


def _stage_rows(stage_ref, row0, src, wrap=None):
    n, width = src.shape
    for c in range(0, width, STAGE_COLS):
        v = src[:, pl.ds(c, STAGE_COLS)].astype(F32)
        stage_ref[pl.ds(row0, n), pl.ds(c, STAGE_COLS)] = v if wrap is None else wrap(v)


def _conv_taps(stage_ref, fw_ref, fb_ref, c0, r, rows=ROW_CHUNK):
    cols = pl.ds(c0, FFN_COLS)
    x0, x1, x2 = (stage_ref[pl.ds(HALO_ROWS + r - k * N_SEG, rows), cols] for k in range(3))
    w = fw_ref[:, cols]
    return w[0:1] * x2 + w[1:2] * x1 + w[2:3] * x0 + fb_ref[:, cols], x2, x1, x0


def _gated_fwd(up_pre, fw, fb):
    t, f2 = up_pre.shape
    f = f2 // 2

    def body(up_ref, halo_ref, fw_ref, fb_ref, act_ref, stage_ref):
        first = pl.program_id(0) == 0
        _stage_rows(stage_ref, 0, halo_ref, lambda v: _wrap_prev_halo(v, first))
        _stage_rows(stage_ref, HALO_ROWS, up_ref)
        for c0 in range(0, f, FFN_COLS):
            for r in range(0, ROW_TILE, ROW_CHUNK):
                a = _conv_taps(stage_ref, fw_ref, fb_ref, c0, r)[0]
                val = _conv_taps(stage_ref, fw_ref, fb_ref, f + c0, r)[0]
                act_ref[pl.ds(r, ROW_CHUNK), pl.ds(c0, FFN_COLS)] = (a * jax.nn.sigmoid(a) * val).astype(BF16)

    const = lambda a: pl.BlockSpec(a.shape, lambda i: (0, 0))
    return pl.pallas_call(
        body, name="ffn_fwd", grid=(t // ROW_TILE,),
        in_specs=[pl.BlockSpec((ROW_TILE, f2), lambda i: (i, 0)),
                  pl.BlockSpec((HALO_ROWS, f2), lambda i: (_prev_halo(i, t), 0)), const(fw), const(fb)],
        out_specs=pl.BlockSpec((ROW_TILE, f), lambda i: (i, 0)),
        out_shape=jax.ShapeDtypeStruct((t, f), BF16),
        scratch_shapes=[pltpu.VMEM((HALO_ROWS + ROW_TILE, f2), F32)], compiler_params=_params(),
    )(up_pre, up_pre, fw, fb)


def _gated_bwd(up_pre, dact, fw, fb, deps=()):
    t, f2 = up_pre.shape
    f = f2 // 2
    n_tiles = t // ROW_TILE
    chunks = [(r, ROW_CHUNK) for r in range(0, ROW_TILE, ROW_CHUNK)] + [(ROW_TILE, HALO_ROWS)]

    def body(up_ref, prev_ref, next_ref, dact_in_ref, dact_next_ref, fw_ref, fb_ref,
             out_ref, dfw_ref, dfb_ref, dup_ref, stage_ref, dact_ref):
        first = pl.program_id(0) == 0
        last = pl.program_id(0) == n_tiles - 1

        @pl.when(first)
        def _():
            dfw_ref[...] = jnp.zeros_like(dfw_ref)
            dfb_ref[...] = jnp.zeros_like(dfb_ref)

        _stage_rows(dact_ref, 0, dact_in_ref)
        _stage_rows(dact_ref, ROW_TILE, dact_next_ref, lambda v: _wrap_next_halo(v, last))
        _stage_rows(stage_ref, 0, prev_ref, lambda v: _wrap_prev_halo(v, first))
        _stage_rows(stage_ref, HALO_ROWS, up_ref)
        _stage_rows(stage_ref, HALO_ROWS + ROW_TILE, next_ref, lambda v: _wrap_next_halo(v, last))
        fold = lambda v: sum(v[s:s + SUBLANES] for s in range(0, ROW_CHUNK, SUBLANES))
        for c0 in range(0, f, FFN_COLS):
            starts = (c0, f + c0)
            sums = [[jnp.zeros((SUBLANES, FFN_COLS), F32)] * 4 for _ in starts]
            for r, rows in chunks:
                taps = [_conv_taps(stage_ref, fw_ref, fb_ref, c, r, rows) for c in starts]
                a, val = taps[0][0], taps[1][0]
                da_ct = dact_ref[pl.ds(r, rows), pl.ds(c0, FFN_COLS)]
                sg = jax.nn.sigmoid(a)
                dup = (da_ct * val * sg * (1.0 + a * (1.0 - sg)), da_ct * a * sg)
                for k in range(2):
                    dup_ref[k, pl.ds(r, rows), :] = dup[k]
                    if r < ROW_TILE:
                        terms = (dup[k], dup[k] * taps[k][1], dup[k] * taps[k][2], dup[k] * taps[k][3])
                        sums[k] = [s + fold(v) for s, v in zip(sums[k], terms)]
            for k, c in enumerate(starts):
                cols = pl.ds(c, FFN_COLS)
                s_b, s_w0, s_w1, s_w2 = (jnp.sum(p, axis=0, keepdims=True) for p in sums[k])
                dfb_ref[:, cols] += s_b
                for tap, s_w in enumerate((s_w0, s_w1, s_w2)):
                    dfw_ref[tap:tap + 1, cols] += s_w
                cw = fw_ref[:, cols]
                for r in range(0, ROW_TILE, ROW_CHUNK):
                    d, u1, u2 = (dup_ref[k, pl.ds(r + s * N_SEG, ROW_CHUNK), :] for s in range(3))
                    out_ref[pl.ds(r, ROW_CHUNK), cols] = (cw[2:3] * d + cw[1:2] * u1 + cw[0:1] * u2).astype(BF16)

    tile = lambda w: pl.BlockSpec((ROW_TILE, w), lambda i: (i, 0))
    halo = lambda w, index: pl.BlockSpec((HALO_ROWS, w), lambda i: (index(i, t), 0))
    const = lambda a: pl.BlockSpec(a.shape, lambda i: (0, 0))
    return pl.pallas_call(
        _ignoring_deps(body, 7, deps), name="ffn_bwd", grid=(n_tiles,),
        in_specs=[tile(f2), halo(f2, _prev_halo), halo(f2, _next_halo), tile(f), halo(f, _next_halo),
                  const(fw), const(fb)] + [ANY_SPEC] * len(deps),
        out_specs=[tile(f2), const(fw), const(fb)],
        out_shape=[jax.ShapeDtypeStruct((t, f2), BF16), jax.ShapeDtypeStruct(fw.shape, F32),
                   jax.ShapeDtypeStruct(fb.shape, F32)],
        scratch_shapes=[pltpu.VMEM((2, ROW_TILE + HALO_ROWS, FFN_COLS), F32),
                        pltpu.VMEM((HALO_ROWS + ROW_TILE + HALO_ROWS, f2), F32),
                        pltpu.VMEM((ROW_TILE + HALO_ROWS, f), F32)],
        compiler_params=_params(),
    )(up_pre, up_pre, up_pre, dact, dact, fw, fb, *deps)


def _to_segments(a):
    t, c = a.shape
    return a.reshape(N_SEG, t // N_SEG, c).transpose(1, 0, 2).reshape(t, c)


def _from_segments(a):
    t, c = a.shape
    return a.reshape(t // N_SEG, N_SEG, c).transpose(1, 0, 2).reshape(t, c)


def _cmul(ar, ai, br, bi):
    return ar * br - ai * bi, ar * bi + ai * br


def _cpow(ar, ai, n):
    out = None
    while n:
        if n & 1:
            out = (ar, ai) if out is None else _cmul(out[0], out[1], ar, ai)
        ar, ai = _cmul(ar, ai, ar, ai)
        n >>= 1
    return out


def _segment_carries(pr, pi, fr, fi, forward):
    row = lax.broadcasted_iota(jnp.int32, fr.shape, 0)
    edge = row == (0 if forward else N_SEG - 1)
    shift = 1 if forward else N_SEG - 1
    sr, si = jnp.zeros_like(fr), jnp.zeros_like(fi)
    for _ in range(N_SEG - 1):
        tr, ti = _cmul(pr, pi, sr, si)
        sr = jnp.where(edge, 0.0, pltpu.roll(tr + fr, shift, 0))
        si = jnp.where(edge, 0.0, pltpu.roll(ti + fi, shift, 0))
    return sr, si


def _rows(i):
    if isinstance(i, int):
        return pl.ds(i * SUBLANES, SUBLANES)
    return pl.ds(pl.multiple_of(i * SUBLANES, SUBLANES), SUBLANES)


def _scan_steps(n, body, init):
    def trip(k, carry):
        for s in range(SCAN_UNROLL):
            carry = body(k * SCAN_UNROLL + s, carry)
        return carry

    carry = lax.fori_loop(0, n // SCAN_UNROLL, trip, init)
    for i in range(n - n % SCAN_UNROLL, n):
        carry = body(i, carry)
    return carry


def _s5_fwd(proj, u_col, bb_re, bb_im, a_re, a_im, cc_re, cc_im, d_skip):
    t = proj.shape[0]
    nch, _, cs = bb_re.shape
    ds = nch * SSM_CHUNK
    u_blk = u_col // SSM_CHUNK
    steps = t // N_SEG

    def body(u_ref, bbr_ref, bbi_ref, ar_ref, ai_ref, ccr_ref, cci_ref, d_ref, sr_ref, si_ref, y_ref):
        ub = u_ref[...].astype(BF16)
        sr_ref[...] = _dot(ub, bbr_ref[...], NN)
        si_ref[...] = _dot(ub, bbi_ref[...], NN)
        ar = jnp.broadcast_to(ar_ref[...], (N_SEG, cs))
        ai = jnp.broadcast_to(ai_ref[...], (N_SEG, cs))
        zero = jnp.zeros((N_SEG, cs), F32)

        def totals(i, carry):
            tr, ti = _cmul(ar, ai, *carry)
            return tr + sr_ref[_rows(i), :], ti + si_ref[_rows(i), :]

        fr, fi = _scan_steps(steps, totals, (zero, zero))
        s0r, s0i = _segment_carries(*_cpow(ar, ai, steps), fr, fi, True)

        def scan(i, carry):
            tr, ti = _cmul(ar, ai, *carry)
            nr, ni = tr + sr_ref[_rows(i), :], ti + si_ref[_rows(i), :]
            sr_ref[_rows(i), :] = nr
            si_ref[_rows(i), :] = ni
            return nr, ni

        _scan_steps(steps, scan, (s0r, s0i))
        y_ref[...] = (_dot(sr_ref[...].astype(BF16), ccr_ref[...], NT)
                      - _dot(si_ref[...].astype(BF16), cci_ref[...], NT)
                      + d_ref[...] * u_ref[...])

    chunk3 = lambda r, c: pl.BlockSpec((None, r, c), lambda j: (j, 0, 0))
    return pl.pallas_call(
        body, name="s5_fwd", grid=(nch,),
        in_specs=[pl.BlockSpec((t, SSM_CHUNK), lambda j: (0, j + u_blk)),
                  chunk3(SSM_CHUNK, cs), chunk3(SSM_CHUNK, cs), chunk3(1, cs), chunk3(1, cs),
                  chunk3(SSM_CHUNK, cs), chunk3(SSM_CHUNK, cs), chunk3(1, SSM_CHUNK)],
        out_specs=[pl.BlockSpec((t, cs), lambda j: (0, j)), pl.BlockSpec((t, cs), lambda j: (0, j)),
                   pl.BlockSpec((t, SSM_CHUNK), lambda j: (0, j))],
        out_shape=[jax.ShapeDtypeStruct((t, nch * cs), F32), jax.ShapeDtypeStruct((t, nch * cs), F32),
                   jax.ShapeDtypeStruct((t, ds), F32)],
        compiler_params=_params(),
    )(proj, bb_re, bb_im, a_re, a_im, cc_re, cc_im, d_skip)


def _s5_bwd(dy, proj, u_col, s_re, s_im, bb_re, bb_im, a_re, a_im, cc_re, cc_im, d_skip, gpc):
    t, ds = dy.shape
    nch, _, cs = bb_re.shape
    u_blk = u_col // SSM_CHUNK
    steps = t // N_SEG

    def body(dy_ref, u_ref, sr_ref, si_ref, bbr_ref, bbi_ref, ar_ref, ai_ref, ccr_ref, cci_ref, d_ref,
             du_ref, dbbr_ref, dbbi_ref, dar_ref, dai_ref, dccr_ref, dcci_ref, dd_ref, gr_ref, gi_ref):
        dyv = dy_ref[...]
        dyb = dyv.astype(BF16)
        gr_ref[...] = _dot(dyb, ccr_ref[...], NN)
        gi_ref[...] = -_dot(dyb, cci_ref[...], NN)
        ar = jnp.broadcast_to(ar_ref[...], (N_SEG, cs))
        ai = -jnp.broadcast_to(ai_ref[...], (N_SEG, cs))
        zero = jnp.zeros((N_SEG, cs), F32)

        def totals(k, carry):
            i = steps - 1 - k
            tr, ti = _cmul(ar, ai, *carry)
            return tr + gr_ref[_rows(i), :], ti + gi_ref[_rows(i), :]

        fr, fi = _scan_steps(steps, totals, (zero, zero))
        e0r, e0i = _segment_carries(*_cpow(ar, ai, steps), fr, fi, False)

        def step(i, gr, gi, pr, pi, acc_r, acc_i):
            tr, ti = _cmul(ar, ai, gr, gi)
            nr, ni = tr + gr_ref[_rows(i), :], ti + gi_ref[_rows(i), :]
            gr_ref[_rows(i), :] = nr
            gi_ref[_rows(i), :] = ni
            return nr, ni, acc_r + nr * pr + ni * pi, acc_i + ni * pr - nr * pi

        def scan(k, carry):
            i = steps - 1 - k
            gr, gi, acc_r, acc_i = carry
            return step(i, gr, gi, sr_ref[_rows(i - 1), :], si_ref[_rows(i - 1), :], acc_r, acc_i)

        gr, gi, acc_r, acc_i = _scan_steps(steps - 1, scan, (e0r, e0i, zero, zero))
        row = lax.broadcasted_iota(jnp.int32, (N_SEG, cs), 0)
        last = _rows(steps - 1)
        pr = jnp.where(row == 0, 0.0, pltpu.roll(sr_ref[last, :], 1, 0))
        pi = jnp.where(row == 0, 0.0, pltpu.roll(si_ref[last, :], 1, 0))
        _, _, acc_r, acc_i = step(0, gr, gi, pr, pi, acc_r, acc_i)
        dar_ref[...] = jnp.sum(acc_r, axis=0, keepdims=True)
        dai_ref[...] = jnp.sum(acc_i, axis=0, keepdims=True)

        uv = u_ref[...]
        ub = uv.astype(BF16)
        grb = gr_ref[...].astype(BF16)
        gib = gi_ref[...].astype(BF16)
        du = d_ref[...] * dyv + _dot(grb, bbr_ref[...], NT) + _dot(gib, bbi_ref[...], NT)
        du_ref[...] = du.astype(BF16)
        def put_groups(ref, full):
            for gl in range(gpc):
                ref[gl] = full[gl * hb:(gl + 1) * hb, gl * pb:(gl + 1) * pb]

        put_groups(dbbr_ref, _dot(ub, grb, TN))
        put_groups(dbbi_ref, _dot(ub, gib, TN))
        put_groups(dccr_ref, _dot(dyb, sr_ref[...].astype(BF16), TN))
        put_groups(dcci_ref, -_dot(dyb, si_ref[...].astype(BF16), TN))
        dd_ref[...] = jnp.sum(dyv * uv, axis=0, keepdims=True)

    hb, pb = SSM_CHUNK // gpc, cs // gpc
    groups = pl.BlockSpec((None, gpc, hb, pb), lambda j: (j, 0, 0, 0))
    groups_shape = jax.ShapeDtypeStruct((nch, gpc, hb, pb), F32)
    chunk3 = lambda r, c: pl.BlockSpec((None, r, c), lambda j: (j, 0, 0))
    cols = lambda w: pl.BlockSpec((t, w), lambda j: (0, j))
    return pl.pallas_call(
        body, name="s5_bwd", grid=(nch,),
        in_specs=[cols(SSM_CHUNK), pl.BlockSpec((t, SSM_CHUNK), lambda j: (0, j + u_blk)), cols(cs), cols(cs),
                  chunk3(SSM_CHUNK, cs), chunk3(SSM_CHUNK, cs), chunk3(1, cs), chunk3(1, cs),
                  chunk3(SSM_CHUNK, cs), chunk3(SSM_CHUNK, cs), chunk3(1, SSM_CHUNK)],
        out_specs=[cols(SSM_CHUNK), groups, groups, chunk3(1, cs), chunk3(1, cs), groups, groups,
                   chunk3(1, SSM_CHUNK)],
        out_shape=[jax.ShapeDtypeStruct((t, ds), BF16), groups_shape, groups_shape,
                   jax.ShapeDtypeStruct((nch, 1, cs), F32), jax.ShapeDtypeStruct((nch, 1, cs), F32),
                   groups_shape, groups_shape, jax.ShapeDtypeStruct((nch, 1, SSM_CHUNK), F32)],
        scratch_shapes=[pltpu.VMEM((t, cs), F32), pltpu.VMEM((t, cs), F32)],
        compiler_params=_params(),
    )(dy, proj, s_re, s_im, bb_re, bb_im, a_re, a_im, cc_re, cc_im, d_skip)


def _discretize(lr, li, log_dt, br, bi):
    dt = jnp.exp(log_dt)
    mag = jnp.exp(lr * dt)
    ang = li * dt
    a_re = mag * jnp.cos(ang)
    a_im = mag * jnp.sin(ang)
    den = lr * lr + li * li
    nr = a_re - 1.0
    f_re = (nr * lr + a_im * li) / den
    f_im = (a_im * lr - nr * li) / den
    return a_re, a_im, f_re * br - f_im * bi, f_re * bi + f_im * br


def _whole(shape):
    return pl.BlockSpec(shape, lambda: (0,) * len(shape))


def _disc_fwd(lr, li, log_dt, br, bi, cr, ci, nch, gpc, deps=()):
    gh, n_state = br.shape
    grp = gh // (nch * gpc)
    rows, cs = gpc * grp, gpc * n_state

    def body(lr_ref, li_ref, dt_ref, br_ref, bi_ref, cr_ref, ci_ref,
             ar_ref, ai_ref, bbr_ref, bbi_ref, ccr_ref, cci_ref, wide_ref):
        a_re, a_im, bb_re, bb_im = _discretize(lr_ref[...], li_ref[...], dt_ref[...], br_ref[...], bi_ref[...])
        for ref, val in ((ar_ref, a_re), (ai_ref, a_im)):
            for j in range(nch):
                for gl in range(gpc):
                    g0 = (j * gpc + gl) * grp
                    ref[j, :, gl * n_state:(gl + 1) * n_state] = val[g0:g0 + 1]
        for ref, val in ((bbr_ref, bb_re), (bbi_ref, bb_im), (ccr_ref, cr_ref[...]), (cci_ref, ci_ref[...])):
            wide_ref[...] = jnp.zeros_like(wide_ref)
            for j in range(nch):
                for gl in range(gpc):
                    g0 = (j * gpc + gl) * grp
                    wide_ref[j, gl * grp:(gl + 1) * grp, gl * n_state:(gl + 1) * n_state] = val[g0:g0 + grp]
            ref[...] = wide_ref[...].astype(BF16)

    args = (lr, li, log_dt, br, bi, cr, ci)
    row_shape, wide_shape = (nch, 1, cs), (nch, rows, cs)
    outs = [jax.ShapeDtypeStruct(row_shape, F32)] * 2 + [jax.ShapeDtypeStruct(wide_shape, BF16)] * 4
    return pl.pallas_call(
        _ignoring_deps(body, 7, deps), name="disc_fwd",
        in_specs=[_whole(a.shape) for a in args] + [ANY_SPEC] * len(deps),
        out_specs=[_whole(o.shape) for o in outs], out_shape=outs,
        scratch_shapes=[pltpu.VMEM(wide_shape, F32)],
    )(*args, *deps)


def _disc_bwd(lr, li, log_dt, br, bi, dar, dai, dbbr, dbbi):
    def body(lr_ref, li_ref, dt_ref, br_ref, bi_ref, dar_ref, dai_ref, dbbr_ref, dbbi_ref,
             dlr_ref, dli_ref, ddt_ref, dbr_ref, dbi_ref):
        _, vjp = jax.vjp(_discretize, lr_ref[...], li_ref[...], dt_ref[...], br_ref[...], bi_ref[...])
        grads = vjp((dar_ref[...], dai_ref[...], dbbr_ref[...], dbbi_ref[...]))
        for ref, val in zip((dlr_ref, dli_ref, ddt_ref, dbr_ref, dbi_ref), grads):
            ref[...] = val

    args = (lr, li, log_dt, br, bi, dar, dai, dbbr, dbbi)
    outs = (lr, li, log_dt, br, bi)
    return pl.pallas_call(
        body, name="disc_bwd", in_specs=[_whole(a.shape) for a in args],
        out_specs=[_whole(a.shape) for a in outs],
        out_shape=[jax.ShapeDtypeStruct(a.shape, F32) for a in outs],
    )(*args)


def _adamw(w, g, m, v):
    m = ADAM_B1 * m + (1.0 - ADAM_B1) * g
    v = ADAM_B2 * v + (1.0 - ADAM_B2) * (g * g)
    m_hat = m / ADAM_BC1
    v_hat = v / ADAM_BC2
    delta = -ADAM_LR * (m_hat / (jnp.sqrt(v_hat) + ADAM_EPS) + ADAM_WD * w)
    return delta, m, v


def _adamw_reduce(name, parts, w, m, v):
    _, r, c = parts.shape
    tr = r
    for cand in (256, 176, 128):
        if r % cand == 0:
            tr = cand
            break

    def body(p_ref, w_ref, m_ref, v_ref, g_ref, d_ref, nm_ref, nv_ref):
        g = p_ref[0].astype(F32)
        for k in range(1, N_DEV):
            g = g + p_ref[k].astype(F32)
        delta, nm, nv = _adamw(w_ref[...], g, m_ref[...], v_ref[...])
        g_ref[...] = g
        d_ref[...] = delta
        nm_ref[...] = nm
        nv_ref[...] = nv

    blk = pl.BlockSpec((tr, c), lambda i: (i, 0))
    return pl.pallas_call(
        body, name=name, grid=(r // tr,),
        in_specs=[pl.BlockSpec((N_DEV, tr, c), lambda i: (0, i, 0)), blk, blk, blk],
        out_specs=[blk] * 4, out_shape=[jax.ShapeDtypeStruct((r, c), F32)] * 4,
        compiler_params=_params(),
    )(parts, w, m, v)


def _sum_parts(name, parts):
    _, r, c = parts.shape

    def body(p_ref, o_ref):
        g = p_ref[0]
        for k in range(1, N_DEV):
            g = g + p_ref[k]
        o_ref[...] = g

    return pl.pallas_call(
        body, name=name, in_specs=[_whole(parts.shape)], out_specs=_whole((r, c)),
        out_shape=jax.ShapeDtypeStruct((r, c), F32), compiler_params=_params(),
    )(parts)


def _adamw_many(name, grads, ws, ms, vs):
    n = len(grads)

    def body(*refs):
        ins, outs = refs[:4 * n], refs[4 * n:]
        for i in range(n):
            g, w, m, v = (ins[j * n + i][...] for j in range(4))
            for ref, val in zip((outs[i], outs[n + i], outs[2 * n + i]), _adamw(w, g, m, v)):
                ref[...] = val

    args = (*grads, *ws, *ms, *vs)
    outs = pl.pallas_call(
        body, name=name, in_specs=[_whole(a.shape) for a in args],
        out_specs=[_whole(a.shape) for a in ws] * 3,
        out_shape=[jax.ShapeDtypeStruct(a.shape, F32) for a in ws] * 3, compiler_params=_params(),
    )(*args)
    return outs[:n], outs[n:2 * n], outs[2 * n:]


def _pack(arrays, rows):
    flat = jnp.concatenate([a.reshape(-1) for a in arrays])
    return jnp.pad(flat, (0, rows * LANES - flat.shape[0])).reshape(rows, LANES)


def _unpack(packed, shapes):
    flat = packed.reshape(-1)
    out, off = [], 0
    for s in shapes:
        n = math.prod(s)
        out.append(flat[off:off + n].reshape(s))
        off += n
    return out


def _packed_rows(shapes):
    n = sum(math.prod(s) for s in shapes)
    return -(-n // (SUBLANES * LANES)) * SUBLANES


def kernel(x, meta_tokens, norm_mix_g, w_in, conv_w, ssm_lam_re, ssm_lam_im, ssm_log_dt, ssm_b_re, ssm_b_im, ssm_c_re, ssm_c_im, ssm_d, ssm_w_glu, gain_conv_out, gain_ssm_out, w_out, norm_ffn_g, w_up, ffn_conv_w, ffn_conv_b, w_down, norm_final_g, loss_target, m_meta_tokens, m_norm_mix_g, m_w_in, m_conv_w, m_ssm_lam_re, m_ssm_lam_im, m_ssm_log_dt, m_ssm_b_re, m_ssm_b_im, m_ssm_c_re, m_ssm_c_im, m_ssm_d, m_ssm_w_glu, m_gain_conv_out, m_gain_ssm_out, m_w_out, m_norm_ffn_g, m_w_up, m_ffn_conv_w, m_ffn_conv_b, m_w_down, m_norm_final_g, v_meta_tokens, v_norm_mix_g, v_w_in, v_conv_w, v_ssm_lam_re, v_ssm_lam_im, v_ssm_log_dt, v_ssm_b_re, v_ssm_b_im, v_ssm_c_re, v_ssm_c_im, v_ssm_d, v_ssm_w_glu, v_gain_conv_out, v_gain_ssm_out, v_w_out, v_norm_ffn_g, v_w_up, v_ffn_conv_w, v_ffn_conv_b, v_w_down, v_norm_final_g):
    weights = dict(meta_tokens=meta_tokens, norm_mix_g=norm_mix_g, w_in=w_in, conv_w=conv_w, ssm_lam_re=ssm_lam_re, ssm_lam_im=ssm_lam_im, ssm_log_dt=ssm_log_dt, ssm_b_re=ssm_b_re, ssm_b_im=ssm_b_im, ssm_c_re=ssm_c_re, ssm_c_im=ssm_c_im, ssm_d=ssm_d, ssm_w_glu=ssm_w_glu, gain_conv_out=gain_conv_out, gain_ssm_out=gain_ssm_out, w_out=w_out, norm_ffn_g=norm_ffn_g, w_up=w_up, ffn_conv_w=ffn_conv_w, ffn_conv_b=ffn_conv_b, w_down=w_down, norm_final_g=norm_final_g)
    mom_m = dict(meta_tokens=m_meta_tokens, norm_mix_g=m_norm_mix_g, w_in=m_w_in, conv_w=m_conv_w, ssm_lam_re=m_ssm_lam_re, ssm_lam_im=m_ssm_lam_im, ssm_log_dt=m_ssm_log_dt, ssm_b_re=m_ssm_b_re, ssm_b_im=m_ssm_b_im, ssm_c_re=m_ssm_c_re, ssm_c_im=m_ssm_c_im, ssm_d=m_ssm_d, ssm_w_glu=m_ssm_w_glu, gain_conv_out=m_gain_conv_out, gain_ssm_out=m_gain_ssm_out, w_out=m_w_out, norm_ffn_g=m_norm_ffn_g, w_up=m_w_up, ffn_conv_w=m_ffn_conv_w, ffn_conv_b=m_ffn_conv_b, w_down=m_w_down, norm_final_g=m_norm_final_g)
    mom_v = dict(meta_tokens=v_meta_tokens, norm_mix_g=v_norm_mix_g, w_in=v_w_in, conv_w=v_conv_w, ssm_lam_re=v_ssm_lam_re, ssm_lam_im=v_ssm_lam_im, ssm_log_dt=v_ssm_log_dt, ssm_b_re=v_ssm_b_re, ssm_b_im=v_ssm_b_im, ssm_c_re=v_ssm_c_re, ssm_c_im=v_ssm_c_im, ssm_d=v_ssm_d, ssm_w_glu=v_ssm_w_glu, gain_conv_out=v_gain_conv_out, gain_ssm_out=v_gain_ssm_out, w_out=v_w_out, norm_ffn_g=v_norm_ffn_g, w_up=v_w_up, ffn_conv_w=v_ffn_conv_w, ffn_conv_b=v_ffn_conv_b, w_down=v_w_down, norm_final_g=v_norm_final_g)
    names = list(weights)

    n_meta, d_meta = meta_tokens.shape
    seq, d = x.shape[1], x.shape[2]
    rows_used = n_meta + seq
    t = -(-rows_used // ROW_TILE) * ROW_TILE
    d_in_s = w_in.shape[2]
    dc_s = conv_w.shape[2]
    dc = dc_s * N_DEV
    ds = ssm_w_glu.shape[2]
    n_groups, n_state, grp = ssm_b_re.shape[1:]
    ns = n_groups * n_state
    nch = ds // SSM_CHUNK
    gpc = n_groups // nch
    ff_s = w_up.shape[2]
    dn_s = w_down.shape[1]
    assert 3 * dc + ds == d_in_s * N_DEV and 2 * dn_s == ff_s and t % (N_SEG * SUBLANES) == 0

    small_shard = jnp.concatenate([meta_tokens.reshape(-1), conv_w.reshape(-1), ffn_conv_w.reshape(-1)])
    n_small = small_shard.shape[0]
    small_rows = -(-n_small // LANES)
    small_shard = jnp.pad(small_shard, (0, small_rows * LANES - n_small)).reshape(small_rows, LANES)
    ag, ag_token = _gather2_start("gather_weights_start", [
        small_shard, w_in[0].astype(BF16), ssm_w_glu[0].astype(BF16), w_out[0].astype(BF16),
        jnp.swapaxes(w_up[0], 0, 1).astype(BF16), w_down[0].astype(BF16)])
    fb = ffn_conv_b

    gh = n_groups * grp
    per_h = lambda a: jnp.broadcast_to(a.reshape(n_groups, 1, -1), (n_groups, grp, n_state)).reshape(gh, n_state)
    ghp = lambda a: a.transpose(0, 1, 3, 2).reshape(gh, n_state)
    lr, li, log_dt_e = per_h(ssm_lam_re), per_h(ssm_lam_im), per_h(ssm_log_dt)
    br, bi = ghp(ssm_b_re), ghp(ssm_b_im)
    a_re_c, a_im_c, bbm_re, bbm_im, ccm_re, ccm_im = _disc_fwd(
        lr, li, log_dt_e, br, bi, ssm_c_re.reshape(gh, n_state), ssm_c_im.reshape(gh, n_state), nch, gpc,
        deps=(ag_token,))
    d_skip = ssm_d.reshape(nch, 1, SSM_CHUNK)

    target = _to_segments(jnp.pad(loss_target[0] + ag_token[0, 0], ((n_meta, t - rows_used), (0, 0))))
    token = _gather2_forward("gather_weights_forward_first", ag, (0, 1), (bbm_im, ccm_im, target))
    (g_small,) = _gather2_wait("gather_weights_wait_small", ag, (0,), token)
    g_small = g_small.reshape(N_DEV, -1)
    o1 = n_meta * d_meta
    o2 = o1 + 3 * dc_s
    meta_full = g_small[:, :o1].reshape(N_DEV, n_meta, d_meta).transpose(1, 0, 2).reshape(n_meta, d)
    conv_w_f = g_small[:, o1:o2].reshape(N_DEV, 3, dc_s).transpose(1, 0, 2).reshape(3, dc)
    fw = g_small[:, o2:o2 + 3 * ff_s].reshape(N_DEV, 3, ff_s).transpose(1, 0, 2).reshape(3, N_DEV * ff_s)
    h0 = _to_segments(jnp.concatenate([meta_full, x[0], jnp.zeros((t - rows_used, d), F32)], axis=0))
    full_t = lambda w: pl.BlockSpec((t, w), lambda *_: (0, 0))

    hn1 = _rms_fwd("norm_mix", h0, norm_mix_g)
    (g_in,) = _gather2_wait("gather_weights_wait_in", ag, (1,), hn1)
    proj = _mm("proj", hn1, g_in, dims=NN, grid=(N_DEV,), a_spec=full_t(d),
               b_spec=pl.BlockSpec((None, d, d_in_s), lambda j: (j, 0, 0)),
               o_spec=pl.BlockSpec((t, d_in_s), lambda j: (0, j)),
               out_shape=jax.ShapeDtypeStruct((t, N_DEV * d_in_s), F32))
    s_re, s_im, y_ssm = _s5_fwd(proj, 3 * dc, bbm_re, bbm_im, a_re_c, a_im_c, ccm_re, ccm_im, d_skip)
    token = _gather2_forward("gather_weights_forward_up", ag, (2, 3, 4), (y_ssm,))
    g_glu, g_out = _gather2_wait("gather_weights_wait_mix", ag, (2, 3), token)
    w_out_f = g_out.reshape(-1, d)
    w_glu_f = g_glu.reshape(ds, ds)
    mixed, z_glu, h1, hn2 = _mix_fwd(proj, y_ssm, w_glu_f, conv_w_f, gain_conv_out, gain_ssm_out,
                                     h0, w_out_f, norm_ffn_g)
    tn_out = OUT_TILE
    token = _gather2_forward("gather_weights_forward_down", ag, (5,), (hn2,))
    (g_up,) = _gather2_wait("gather_weights_wait_up", ag, (4,), token)
    f2 = N_DEV * ff_s
    pair = 2 * ff_s
    w_up_t = g_up.reshape(f2, d)
    up_pre = _mm("up_proj", hn2, w_up_t, dims=NT, grid=(f2 // pair,), a_spec=full_t(d),
                 b_spec=pl.BlockSpec((pair, d), lambda j: (j, 0)),
                 o_spec=pl.BlockSpec((t, pair), lambda j: (0, j)),
                 out_shape=jax.ShapeDtypeStruct((t, f2), BF16))
    act = _gated_fwd(up_pre, fw, fb)
    (g_down,) = _gather2_wait("gather_weights_wait_down", ag, (5,), act)
    w_down_f = g_down.reshape(f2 // 2, d)

    dh2, dh2_b, loss_part, d_norm_final = _down_loss_bwd(
        act, w_down_f, h1, target, norm_final_g.reshape(1, d), n_meta, rows_used)
    dw_down = _mm("down_wgrad", act, dh2_b, dims=TN, grid=(f2 // 2 // pair,),
                  a_spec=pl.BlockSpec((t, pair), lambda j: (0, j)), b_spec=full_t(d),
                  o_spec=pl.BlockSpec((pair, d), lambda j: (j, 0)),
                  out_shape=jax.ShapeDtypeStruct((f2 // 2, d), BF16))
    ex_down, token = _send_start("exchange_down_start", [dw_down.reshape(N_DEV, dn_s, d)], gather=False)
    dact = _mm("down_dgrad", dh2_b, w_down_f, dims=NT, grid=(f2 // 2 // pair,), a_spec=full_t(d),
               b_spec=pl.BlockSpec((pair, d), lambda j: (j, 0)),
               o_spec=pl.BlockSpec((t, pair), lambda j: (0, j)),
               out_shape=jax.ShapeDtypeStruct((t, f2 // 2), BF16))
    dup_pre, d_fw, d_fb = _gated_bwd(up_pre, dact, fw, fb, deps=(token,))
    dw_up = _mm("up_wgrad", dup_pre, hn2, dims=TN, grid=(f2 // pair,),
                a_spec=pl.BlockSpec((t, pair), lambda j: (0, j)), b_spec=full_t(d),
                o_spec=pl.BlockSpec((pair, d), lambda j: (j, 0)),
                out_shape=jax.ShapeDtypeStruct((f2, d), BF16))
    ex_up, token = _send_start("exchange_up_start", [dw_up.reshape(N_DEV, ff_s, d)], gather=False)
    dh1, dh1_b, d_norm_ffn = _dgrad_rms_bwd("up_dgrad_norm_bwd", dh2, dup_pre, w_up_t, NN, h1, norm_ffn_g,
                                            deps=(token,))
    dmixed = _mm("out_dgrad", dh1_b, w_out_f, dims=NT, grid=((dc + ds) // tn_out,), a_spec=full_t(d),
                 b_spec=pl.BlockSpec((tn_out, d), lambda i: (i, 0)),
                 o_spec=pl.BlockSpec((t, tn_out), lambda i: (0, i)),
                 out_shape=jax.ShapeDtypeStruct((t, dc + ds), F32))
    dw_out = _mm("out_wgrad", mixed, dh1_b, dims=TN, grid=((dc + ds) // tn_out,),
                 a_spec=pl.BlockSpec((t, tn_out), lambda i: (0, i)), b_spec=full_t(d),
                 o_spec=pl.BlockSpec((tn_out, d), lambda i: (i, 0)),
                 out_shape=jax.ShapeDtypeStruct((dc + ds, d), BF16))
    db_gate, dconv, dy_ssm, d_wglu, d_conv_w, d_gain_c, d_gain_s = _mix_bwd1(
        proj, y_ssm, z_glu, dmixed, w_glu_f, conv_w_f, gain_conv_out, gain_ssm_out)
    (du, d_bbm_re, d_bbm_im, d_a_re, d_a_im, d_ccm_re, d_ccm_im, d_dskip) = _s5_bwd(
        dy_ssm, proj, 3 * dc, s_re, s_im, bbm_re, bbm_im, a_re_c, a_im_c, ccm_re, ccm_im, d_skip, gpc)

    from_bb = from_cc = lambda a: a.reshape(gh, n_state)
    first_h = lambda a: jnp.pad(a.reshape(n_groups, 1, n_state), ((0, 0), (0, grp - 1), (0, 0))).reshape(gh, n_state)
    over_h = lambda a: a.reshape(n_groups, grp, n_state).sum(axis=1)
    d_lr, d_li, d_dt_e, d_br, d_bi = _disc_bwd(
        lr, li, log_dt_e, br, bi, first_h(d_a_re), first_h(d_a_im), from_bb(d_bbm_re), from_bb(d_bbm_im))

    rep2d = dict(
        ssm_lam_re=(n_groups, n_state), ssm_lam_im=(n_groups, n_state), ssm_log_dt=(1, n_groups),
        ssm_b_re=(gh, n_state), ssm_b_im=(gh, n_state), ssm_c_re=(gh, n_state),
        ssm_c_im=(gh, n_state), ssm_d=(n_groups, grp), gain_conv_out=(1, dc),
        gain_ssm_out=(1, ds), norm_ffn_g=(1, d), ffn_conv_b=(1, N_DEV * ff_s), norm_final_g=(1, d))
    rep_names = list(rep2d)
    rep_grads = dict(
        ssm_lam_re=over_h(d_lr), ssm_lam_im=over_h(d_li), ssm_log_dt=over_h(d_dt_e).sum(axis=1),
        ssm_b_re=d_br, ssm_b_im=d_bi, ssm_c_re=from_cc(d_ccm_re), ssm_c_im=from_cc(d_ccm_im),
        ssm_d=d_dskip, gain_conv_out=d_gain_c, gain_ssm_out=d_gain_s, norm_ffn_g=d_norm_ffn,
        ffn_conv_b=d_fb, norm_final_g=d_norm_final)
    rep_shapes = [rep2d[n] for n in rep_names] + [(1, 1)]
    rep_rows = _packed_rows(rep_shapes)
    rep_pack = _pack([rep_grads[n] for n in rep_names] + [loss_part], rep_rows)
    ex_mix, token = _send_start("exchange_mix_start", [
        rep_pack, dw_out.reshape(N_DEV, -1, d), d_wglu.astype(BF16).reshape(N_DEV, -1, ds),
        d_conv_w.reshape(3, N_DEV, dc_s).transpose(1, 0, 2),
        d_fw.reshape(3, N_DEV, ff_s).transpose(1, 0, 2)],
        gather=[True, False, False, False, False])
    dcdv = _mix_bwd2(proj, dconv, conv_w_f, deps=(token,))
    dproj = jnp.concatenate([db_gate, dcdv, du], axis=1)
    d_in = N_DEV * d_in_s
    w_in_rows = g_in.transpose(1, 0, 2).reshape(d, d_in)
    dw_in = _mm("proj_wgrad", hn1, dproj, dims=TN, grid=(N_DEV,), a_spec=full_t(d),
                b_spec=pl.BlockSpec((t, d_in_s), lambda j: (0, j)),
                o_spec=pl.BlockSpec((None, d, d_in_s), lambda j: (j, 0, 0)),
                out_shape=jax.ShapeDtypeStruct((N_DEV, d, d_in_s), BF16))
    dh0, _, d_norm_mix = _dgrad_rms_bwd("proj_dgrad_norm_bwd", dh1, dproj, w_in_rows, NT, h0, norm_mix_g)
    dh0 = _from_segments(dh0)
    grad_x = dh0[n_meta:rows_used][None]
    d_meta_b = dh0[:n_meta].reshape(n_meta, N_DEV, d_meta).transpose(1, 0, 2)
    ex_in, ex_in_token = _send_start("exchange_in_start", [dw_in, d_meta_b, d_norm_mix],
                                     gather=[False, False, True])

    shard_out = {}

    def update(n, parts, transposed=False):
        sh = weights[n].shape
        two_d = lambda a: (jnp.swapaxes(a[0], 0, 1) if transposed else a.reshape(parts.shape[1:]))
        res = _adamw_reduce("adamw_" + n, parts, two_d(weights[n]), two_d(mom_m[n]), two_d(mom_v[n]))
        shard_out[n] = [(jnp.swapaxes(r, 0, 1) if transposed else r).reshape(sh) for r in res]
        return res[0]

    (p_down,) = _send_wait("exchange_down_wait", ex_down, (0,), ex_in_token)
    done = update("w_down", p_down)
    (p_up,) = _send_wait("exchange_up_wait", ex_up, (0,), done)
    done = update("w_up", p_up, transposed=True)
    (rep_parts,) = _send_wait("gather_small_grads_wait", ex_mix, (0,), done)
    rep_sum = _sum_parts("sum_small_grads", rep_parts)
    *rep_g, loss = _unpack(rep_sum, rep_shapes)
    loss = loss.reshape(())
    swapped = ("ssm_b_re", "ssm_b_im")
    to2d = lambda n, a: ghp(a) if n in swapped else a.reshape(rep2d[n])
    from2d = lambda n, a: (a.reshape(1, n_groups, grp, n_state).transpose(0, 1, 3, 2) if n in swapped
                           else a.reshape(weights[n].shape))
    as2d = lambda tree: [to2d(n, tree[n]) for n in rep_names]
    rep_res = _adamw_many("adamw_replicated", rep_g, as2d(weights), as2d(mom_m), as2d(mom_v))
    for i, n in enumerate(rep_names):
        shard_out[n] = [from2d(n, r) for r in (rep_g[i], *(res[i] for res in rep_res))]

    p_out, p_glu, p_cw, p_fw = _send_wait("exchange_mix_wait", ex_mix, (1, 2, 3, 4), rep_sum)
    update("w_out", p_out)
    update("ssm_w_glu", p_glu)
    update("conv_w", p_cw)
    done = update("ffn_conv_w", p_fw)
    p_in, p_meta, p_nm = _send_wait("exchange_in_wait", ex_in, (0, 1, 2), done)
    update("w_in", p_in)
    update("meta_tokens", p_meta)
    update("norm_mix_g", p_nm)

    grads = [shard_out[n][0] for n in names]
    deltas = [shard_out[n][1] for n in names]
    new_m = [shard_out[n][2] for n in names]
    new_v = [shard_out[n][3] for n in names]
    return (loss, grad_x, *grads, *deltas, *new_m, *new_v)
```

```python
import math

import jax
import jax.numpy as jnp
from jax import lax
from jax.experimental import pallas as pl
from jax.experimental.pallas import tpu as pltpu

F32 = jnp.float32
BF16 = jnp.bfloat16
MESH = pl.DeviceIdType.MESH

N_DEV = 8
RMS_EPS = 1e-6
ADAM_LR = 0.001
ADAM_B1 = 0.9
ADAM_B2 = 0.999
ADAM_EPS = 1e-08
ADAM_WD = 0.01
ADAM_STEP = 10
ADAM_BC1 = 1.0 - ADAM_B1 ** ADAM_STEP
ADAM_BC2 = 1.0 - ADAM_B2 ** ADAM_STEP

SUBLANES = 8
LANES = 128
ROW_TILE = 128
ROW_CHUNK = 32
WIDE_TILES = (544, 256, 128)
FFN_COLS = 256
STAGE_COLS = 256
OUT_TILE = 512
N_SEG = 8
HALO_ROWS = 16
SSM_CHUNK = 128
SCAN_UNROLL = 8
VMEM_LIMIT = 48 * 1024 * 1024

NN = ((1,), (0,))
NT = ((1,), (1,))
TN = ((0,), (0,))


def _params(**kw):
    return pltpu.CompilerParams(vmem_limit_bytes=VMEM_LIMIT, **kw)


def _dot(a, b, dims):
    return lax.dot_general(a, b, (dims, ((), ())), preferred_element_type=F32)


def _mean_sq_rsqrt(x):
    return lax.rsqrt(jnp.mean(x * x, axis=-1, keepdims=True) + RMS_EPS)


def _rms_bwd(x, r, g, dy):
    xhat = x * r
    dxh = dy * g
    dx = r * (dxh - xhat * jnp.mean(dxh * xhat, axis=-1, keepdims=True))
    return dx, dy * xhat


def _gelu(y):
    c = math.sqrt(2.0 / math.pi)
    t = jnp.tanh(c * (y + 0.044715 * y * y * y))
    return 0.5 * y * (1.0 + t), t


def _gelu_grad(y, t):
    c = math.sqrt(2.0 / math.pi)
    return 0.5 * (1.0 + t) + 0.5 * y * (1.0 - t * t) * c * (1.0 + 3.0 * 0.044715 * y * y)


def _wrap_prev_halo(halo, first):
    seg = lax.broadcasted_iota(jnp.int32, halo.shape, 0) % N_SEG
    wrapped = jnp.where(seg == 0, 0.0, pltpu.roll(halo, 1, 0))
    return jnp.where(first, wrapped, halo)


def _wrap_next_halo(halo, last):
    seg = lax.broadcasted_iota(jnp.int32, halo.shape, 0) % N_SEG
    wrapped = jnp.where(seg == N_SEG - 1, 0.0, pltpu.roll(halo, halo.shape[0] - 1, 0))
    return jnp.where(last, wrapped, halo)


def _dev_index(p):
    return 4 * p[0] + 2 * p[1] + p[2]


def _allgather(name, shards, deps=()):
    n = len(shards)

    def body(*refs):
        ins, outs = refs[:n], refs[n:2 * n]
        send_sems, recv_sems, local_sems = refs[2 * n:]
        x, y, c = lax.axis_index("x"), lax.axis_index("y"), lax.axis_index("c")
        me, sibling = (x, y, c), (x, y, 1 - c)
        chips = [(1 - x, y), (x, 1 - y), (1 - x, 1 - y)]

        def copy(a, k, block, to, src=None):
            dst = outs[a].at[_dev_index(block)]
            return pltpu.make_async_remote_copy(
                src_ref=dst if src is None else src, dst_ref=dst,
                send_sem=send_sems.at[a, k], recv_sem=recv_sems.at[a, k],
                device_id=to, device_id_type=MESH)

        mine = [pltpu.make_async_copy(ins[a], outs[a].at[_dev_index(me)], local_sems.at[a])
                for a in range(n)]
        for cp in mine:
            cp.start()
        first = []
        for a in range(n):
            first.append(copy(a, 0, me, sibling, src=ins[a]))
            for j, chip in enumerate(chips):
                first.append(copy(a, 1 + j, me, (*chip, c), src=ins[a]))
        for cp in first:
            cp.start()
        passed = []
        for j, chip in enumerate(chips):
            for a in range(n):
                copy(a, 1 + j, (*chip, c), me).wait_recv()
                fwd = copy(a, 4 + j, (*chip, c), sibling)
                fwd.start()
                passed.append(fwd)
        for a in range(n):
            copy(a, 0, sibling, me).wait_recv()
            for j, chip in enumerate(chips):
                copy(a, 4 + j, (*chip, 1 - c), me).wait_recv()
        for cp in first + passed:
            cp.wait_send()
        for cp in mine:
            cp.wait()

    any_spec = pl.BlockSpec(memory_space=pl.ANY)
    return pl.pallas_call(
        _ignoring_deps(body, n, deps), name=name,
        out_shape=[jax.ShapeDtypeStruct((N_DEV,) + s.shape, s.dtype) for s in shards],
        in_specs=[any_spec] * (n + len(deps)), out_specs=[any_spec] * n,
        scratch_shapes=[pltpu.SemaphoreType.DMA((n, 7)), pltpu.SemaphoreType.DMA((n, 7)),
                        pltpu.SemaphoreType.DMA((n,))],
    )(*shards, *deps)


HBM_SPEC = pl.BlockSpec(memory_space=pltpu.HBM)
SEM_SPEC = pl.BlockSpec(memory_space=pltpu.SEMAPHORE)
ANY_SPEC = pl.BlockSpec(memory_space=pl.ANY)
DATAFLOW = pltpu.SideEffectType.DATAFLOW_SIDE_EFFECTING


def _ignoring_deps(body, n_in, deps):
    n_dep = len(deps)

    def wrapped(*refs):
        return body(*refs[:n_in], *refs[n_in + n_dep:])

    return wrapped


def _my_position():
    x, y, c = lax.axis_index("x"), lax.axis_index("y"), lax.axis_index("c")
    return (x, y, c)


def _peer(me, k):
    return tuple((1 - v) if (k >> s) & 1 else v for v, s in zip(me, (2, 1, 0)))


def _split_copies(src_refs, land_refs, send_sems, recv_sems, gather):
    me = _my_position()
    copies = []
    for a, (src, land) in enumerate(zip(src_refs, land_refs)):
        for k in range(1, N_DEV):
            peer = _peer(me, k)
            copies.append(pltpu.make_async_remote_copy(
                src_ref=src if gather[a] else src.at[_dev_index(peer)], dst_ref=land.at[_dev_index(me)],
                send_sem=send_sems[a].at[k - 1], recv_sem=recv_sems[a].at[k - 1],
                device_id=peer, device_id_type=MESH))
    return copies


def _own_slot(block, like_shape):
    me = _dev_index(_my_position())
    return lax.dynamic_update_index_in_dim(lax.empty(like_shape, block.dtype), block, me, 0)


def _send_start(name, srcs, gather):
    n = len(srcs)
    me = _dev_index(_my_position())
    gather = [gather] * n if isinstance(gather, bool) else list(gather)
    lands = [_own_slot(s, (N_DEV,) + s.shape) if g else
             _own_slot(lax.dynamic_index_in_dim(s, me, 0, keepdims=False), s.shape)
             for s, g in zip(srcs, gather)]

    def body(*refs):
        src_refs, land_refs = refs[:n], refs[n:2 * n]
        send_sems, recv_sems = refs[2 * n:3 * n], refs[3 * n:4 * n]
        token = refs[-1]
        for cp in _split_copies(src_refs, land_refs, send_sems, recv_sems, gather):
            cp.start()
        token[...] = jnp.zeros_like(token)

    hbm = lambda a: pltpu.HBM(a.shape, a.dtype)
    sems = [pltpu.SemaphoreType.DMA((N_DEV - 1,))] * n
    outs = pl.pallas_call(
        body, name=name,
        out_shape=(*sems, *sems, *[hbm(s) for s in srcs], *[hbm(l) for l in lands],
                   jax.ShapeDtypeStruct((SUBLANES, LANES), F32)),
        in_specs=[HBM_SPEC] * (2 * n),
        out_specs=(*[SEM_SPEC] * (2 * n), *[HBM_SPEC] * (2 * n), pl.BlockSpec(memory_space=pltpu.VMEM)),
        input_output_aliases={i: 2 * n + i for i in range(2 * n)},
        compiler_params=pltpu.CompilerParams(has_side_effects=DATAFLOW),
    )(*[pltpu.with_memory_space_constraint(a, pltpu.HBM) for a in (*srcs, *lands)])
    state = dict(send=outs[:n], recv=outs[n:2 * n], srcs=outs[2 * n:3 * n], lands=outs[3 * n:4 * n],
                 gather=gather)
    return state, outs[-1]


def _send_wait(name, state, which, after):
    n = len(which)
    pick = lambda key: [state[key][i] for i in which]
    gather = pick("gather")

    def body(*refs):
        src_refs, land_refs = refs[:n], refs[n:2 * n]
        send_sems, recv_sems = refs[2 * n:3 * n], refs[3 * n:4 * n]
        for cp in _split_copies(src_refs, land_refs, send_sems, recv_sems, gather):
            cp.wait_send()
            cp.wait_recv()

    srcs, lands = pick("srcs"), pick("lands")
    hbm = lambda a: pltpu.HBM(a.shape, a.dtype)
    outs = pl.pallas_call(
        body, name=name,
        out_shape=(*[hbm(s) for s in srcs], *[hbm(l) for l in lands]),
        in_specs=[*[HBM_SPEC] * (2 * n), *[SEM_SPEC] * (2 * n), ANY_SPEC],
        out_specs=tuple([HBM_SPEC] * (2 * n)),
        input_output_aliases={i: i for i in range(2 * n)},
        compiler_params=pltpu.CompilerParams(has_side_effects=DATAFLOW),
    )(*srcs, *lands, *pick("send"), *pick("recv"), after)
    return outs[n:]


def _two_level_copies(src_refs, land_refs, sems1, sems2):
    x, y, c = _my_position()
    me, sibling = (x, y, c), (x, y, 1 - c)
    chips = [(1 - x, y), (x, 1 - y), (1 - x, 1 - y)]
    stage1, stage2 = [], []
    for a, land in enumerate(land_refs):
        def copy(src, block, to, sems, k):
            return pltpu.make_async_remote_copy(
                src_ref=src, dst_ref=land.at[_dev_index(block)], send_sem=sems[0][a].at[k],
                recv_sem=sems[1][a].at[k], device_id=to, device_id_type=MESH)
        src = src_refs[a] if src_refs is not None else land.at[_dev_index(me)]
        stage1.append([copy(src, me, sibling, sems1, 0)] +
                      [copy(src, me, (*chip, c), sems1, 1 + j) for j, chip in enumerate(chips)])
        if sems2 is not None:
            stage2.append([copy(land.at[_dev_index((*chip, c))], (*chip, c), sibling, sems2, j)
                           for j, chip in enumerate(chips)])
    return stage1, stage2


def _gather2_start(name, shards):
    n = len(shards)
    lands = [_own_slot(s, (N_DEV,) + s.shape) for s in shards]

    def body(*refs):
        src_refs, land_refs = refs[:n], refs[n:2 * n]
        sems1 = (refs[2 * n:3 * n], refs[3 * n:4 * n])
        stage1, _ = _two_level_copies(src_refs, land_refs, sems1, None)
        for copies in stage1:
            for cp in copies:
                cp.start()
        refs[-1][...] = jnp.zeros_like(refs[-1])

    hbm = lambda a: pltpu.HBM(a.shape, a.dtype)
    sems = [pltpu.SemaphoreType.DMA((4,))] * n
    outs = pl.pallas_call(
        body, name=name,
        out_shape=(*sems, *sems, *[hbm(s) for s in shards], *[hbm(l) for l in lands],
                   jax.ShapeDtypeStruct((SUBLANES, LANES), F32)),
        in_specs=[HBM_SPEC] * (2 * n),
        out_specs=(*[SEM_SPEC] * (2 * n), *[HBM_SPEC] * (2 * n), pl.BlockSpec(memory_space=pltpu.VMEM)),
        input_output_aliases={i: 2 * n + i for i in range(2 * n)},
        compiler_params=pltpu.CompilerParams(has_side_effects=DATAFLOW),
    )(*[pltpu.with_memory_space_constraint(a, pltpu.HBM) for a in (*shards, *lands)])
    state = dict(send1=list(outs[:n]), recv1=list(outs[n:2 * n]), srcs=list(outs[2 * n:3 * n]),
                 lands=list(outs[3 * n:4 * n]), send2={}, recv2={})
    return state, outs[-1]


def _gather2_forward(name, state, which, after):
    n = len(which)
    pick = lambda key: [state[key][i] for i in which]

    def body(*refs):
        land_refs, recv1 = refs[:n], refs[n:2 * n]
        outs = refs[2 * n + len(after):]
        sems2 = (outs[:n], outs[n:2 * n])
        stage1, stage2 = _two_level_copies(None, land_refs, (recv1, recv1), sems2)
        for a in range(n):
            for j in range(3):
                stage1[a][1 + j].wait_recv()
                stage2[a][j].start()
        outs[-1][...] = jnp.zeros_like(outs[-1])

    lands = pick("lands")
    sems = [pltpu.SemaphoreType.DMA((3,))] * n
    outs = pl.pallas_call(
        body, name=name,
        out_shape=(*sems, *sems, *[pltpu.HBM(l.shape, l.dtype) for l in lands],
                   jax.ShapeDtypeStruct((SUBLANES, LANES), F32)),
        in_specs=[*[HBM_SPEC] * n, *[SEM_SPEC] * n, *[ANY_SPEC] * len(after)],
        out_specs=(*[SEM_SPEC] * (2 * n), *[HBM_SPEC] * n, pl.BlockSpec(memory_space=pltpu.VMEM)),
        input_output_aliases={i: 2 * n + i for i in range(n)},
        compiler_params=pltpu.CompilerParams(has_side_effects=DATAFLOW),
    )(*lands, *pick("recv1"), *after)
    for idx, i in enumerate(which):
        state["send2"][i], state["recv2"][i] = outs[idx], outs[n + idx]
        state["lands"][i] = outs[2 * n + idx]
    return outs[-1]


def _gather2_wait(name, state, which, after):
    n = len(which)
    pick = lambda key: [state[key][i] for i in which]

    def body(*refs):
        src_refs, land_refs = refs[:n], refs[n:2 * n]
        sems1 = (refs[2 * n:3 * n], refs[3 * n:4 * n])
        sems2 = (refs[4 * n:5 * n], refs[5 * n:6 * n])
        stage1, stage2 = _two_level_copies(src_refs, land_refs, sems1, sems2)
        for a in range(n):
            for cp in stage1[a]:
                cp.wait_send()
            stage1[a][0].wait_recv()
            for cp in stage2[a]:
                cp.wait_send()
                cp.wait_recv()

    srcs, lands = pick("srcs"), pick("lands")
    hbm = lambda a: pltpu.HBM(a.shape, a.dtype)
    outs = pl.pallas_call(
        body, name=name,
        out_shape=(*[hbm(s) for s in srcs], *[hbm(l) for l in lands]),
        in_specs=[*[HBM_SPEC] * (2 * n), *[SEM_SPEC] * (4 * n), ANY_SPEC],
        out_specs=tuple([HBM_SPEC] * (2 * n)),
        input_output_aliases={i: i for i in range(2 * n)},
        compiler_params=pltpu.CompilerParams(has_side_effects=DATAFLOW),
    )(*srcs, *lands, *pick("send1"), *pick("recv1"), *pick("send2"), *pick("recv2"), after)
    return outs[n:]


def _mm(name, a, b, *, dims, grid, a_spec, b_spec, o_spec, out_shape, acc_shape=None,
        res=None, res_spec=None):
    n_red = grid[-1] if acc_shape is not None else 1
    red_axis = len(grid) - 1

    def body(*refs):
        a_ref, b_ref = refs[0], refs[1]
        r_ref = refs[2] if res is not None else None
        o_ref = refs[3] if res is not None else refs[2]
        part = _dot(a_ref[...], b_ref[...], dims)
        if acc_shape is None:
            if r_ref is not None:
                part = part + r_ref[...]
            o_ref[...] = part.astype(o_ref.dtype)
            return
        acc_ref = refs[-1]
        k = pl.program_id(red_axis)

        @pl.when(k == 0)
        def _():
            acc_ref[...] = part

        @pl.when(k > 0)
        def _():
            acc_ref[...] += part

        @pl.when(k == n_red - 1)
        def _():
            total = acc_ref[...]
            if r_ref is not None:
                total = total + r_ref[...]
            o_ref[...] = total.astype(o_ref.dtype)

    ins, in_specs = [a, b], [a_spec, b_spec]
    if res is not None:
        ins.append(res)
        in_specs.append(res_spec)
    return pl.pallas_call(
        body, name=name, grid=grid, in_specs=in_specs, out_specs=o_spec, out_shape=out_shape,
        scratch_shapes=[pltpu.VMEM(acc_shape, F32)] if acc_shape is not None else [],
        compiler_params=_params(),
    )(*ins)


def _wide_tile(t):
    return next(c for c in WIDE_TILES if t % c == 0)


def _rms_fwd(name, h, g):
    t, d = h.shape
    tile = _wide_tile(t)

    def body(h_ref, g_ref, o_ref):
        x = h_ref[...]
        o_ref[...] = (x * _mean_sq_rsqrt(x) * g_ref[...]).astype(BF16)

    return pl.pallas_call(
        body, name=name, grid=(t // tile,),
        in_specs=[pl.BlockSpec((tile, d), lambda i: (i, 0)), pl.BlockSpec((1, d), lambda i: (0, 0))],
        out_specs=pl.BlockSpec((tile, d), lambda i: (i, 0)),
        out_shape=jax.ShapeDtypeStruct((t, d), BF16), compiler_params=_params(),
    )(h, g)


def _fused_tile(t):
    half = _wide_tile(t) // 2
    return half if half % 16 == 0 and t % half == 0 else _wide_tile(t)


def _dgrad_rms_bwd(name, dres, dy, w, dims, h, g, deps=()):
    t, d = h.shape
    k = dy.shape[1]

    def body(dres_ref, dy_ref, w_ref, h_ref, g_ref, dh_ref, dhb_ref, dg_ref):
        x = h_ref[...]
        dhn = _dot(dy_ref[...], w_ref[...], dims)
        dx, dgt = _rms_bwd(x, _mean_sq_rsqrt(x), g_ref[...], dhn)
        dh = dres_ref[...] + dx
        dh_ref[...] = dh
        dhb_ref[...] = dh.astype(BF16)

        @pl.when(pl.program_id(0) == 0)
        def _():
            dg_ref[...] = jnp.zeros_like(dg_ref)

        dg_ref[...] += jnp.sum(dgt, axis=0, keepdims=True)

    tile = _wide_tile(t) if 2 * w.size * w.dtype.itemsize <= VMEM_LIMIT // 4 else _fused_tile(t)
    row = pl.BlockSpec((tile, d), lambda i: (i, 0))
    vec = pl.BlockSpec((1, d), lambda i: (0, 0))
    return pl.pallas_call(
        _ignoring_deps(body, 5, deps), name=name, grid=(t // tile,),
        in_specs=[row, pl.BlockSpec((tile, k), lambda i: (i, 0)), pl.BlockSpec(w.shape, lambda i: (0, 0)), row,
                  vec] + [ANY_SPEC] * len(deps),
        out_specs=[row, row, vec],
        out_shape=[jax.ShapeDtypeStruct((t, d), F32), jax.ShapeDtypeStruct((t, d), BF16),
                   jax.ShapeDtypeStruct((1, d), F32)],
        compiler_params=_params(),
    )(dres, dy, w, h, g, *deps)


def _down_loss_bwd(act, w_down, h1, target, g, row_lo, row_hi):
    t, d = h1.shape
    f = act.shape[1]
    tile = _wide_tile(t)

    def body(act_ref, w_ref, h_ref, tg_ref, g_ref, dh_ref, dhb_ref, loss_ref, dg_ref):
        i = pl.program_id(0)
        x = h_ref[...] + _dot(act_ref[...], w_ref[...], NN)
        r = _mean_sq_rsqrt(x)
        gv = g_ref[...]
        y = x * r * gv
        rows = i * tile + lax.broadcasted_iota(jnp.int32, (tile, 1), 0)
        time = (rows % N_SEG) * (t // N_SEG) + rows // N_SEG
        valid = jnp.logical_and(time >= row_lo, time < row_hi)
        err = jnp.where(valid, y - tg_ref[...], 0.0)
        dy = err * (1.0 / d)
        dx, dgt = _rms_bwd(x, r, gv, dy)
        dh_ref[...] = dx
        dhb_ref[...] = dx.astype(BF16)

        @pl.when(i == 0)
        def _():
            loss_ref[...] = jnp.zeros_like(loss_ref)
            dg_ref[...] = jnp.zeros_like(dg_ref)

        row_loss = jnp.mean(err * err, axis=-1, keepdims=True)
        loss_ref[...] += 0.5 * jnp.sum(row_loss, axis=0, keepdims=True)
        dg_ref[...] += jnp.sum(dgt, axis=0, keepdims=True)

    row = pl.BlockSpec((tile, d), lambda i: (i, 0))
    vec = pl.BlockSpec((1, d), lambda i: (0, 0))
    return pl.pallas_call(
        body, name="down_proj_loss_bwd", grid=(t // tile,),
        in_specs=[pl.BlockSpec((tile, f), lambda i: (i, 0)), pl.BlockSpec(w_down.shape, lambda i: (0, 0)),
                  row, row, vec],
        out_specs=[row, row, pl.BlockSpec((1, 1), lambda i: (0, 0)), vec],
        out_shape=[jax.ShapeDtypeStruct((t, d), F32), jax.ShapeDtypeStruct((t, d), BF16),
                   jax.ShapeDtypeStruct((1, 1), F32), jax.ShapeDtypeStruct((1, d), F32)],
        compiler_params=_params(),
    )(act, w_down, h1, target, g)


def _prev_halo(i, t, tile=ROW_TILE):
    return jnp.where(i == 0, t // HALO_ROWS - 1, i * (tile // HALO_ROWS) - 1)


def _next_halo(i, t, tile=ROW_TILE):
    return jnp.where(i == t // tile - 1, 0, (i + 1) * (tile // HALO_ROWS))


def _causal_taps(cur, halo, first):
    rows = cur.shape[0]
    ext = jnp.concatenate([_wrap_prev_halo(halo, first), cur], axis=0)
    return ext[:rows], ext[N_SEG:N_SEG + rows]


def _anticausal_taps(cur, halo, last):
    rows = cur.shape[0]
    ext = jnp.concatenate([cur, _wrap_next_halo(halo, last)], axis=0)
    return ext[N_SEG:N_SEG + rows], ext[2 * N_SEG:2 * N_SEG + rows]


def _mix_fwd(proj, y, w_glu, conv_w, gain_c, gain_s, h0, w_out, gain_ffn):
    t, d = h0.shape
    dc = conv_w.shape[1]
    ds = y.shape[1]

    def body(p_ref, halo_ref, y_ref, wg_ref, cw_ref, gc_ref, gs_ref, h0_ref, wo_ref, gf_ref,
             mixed_ref, z_ref, h1_ref, hn_ref):
        i = pl.program_id(0)
        p = p_ref[...]
        b, c, v = p[:, :dc], p[:, dc:2 * dc], p[:, 2 * dc:3 * dc]
        cv = c * v
        hp = halo_ref[...]
        x2, x1 = _causal_taps(cv, hp[:, dc:2 * dc] * hp[:, 2 * dc:3 * dc], i == 0)
        cw = cw_ref[...]
        conv = cw[0:1] * x2 + cw[1:2] * x1 + cw[2:3] * cv
        co = b * conv
        mixed_ref[:, :dc] = (co * _mean_sq_rsqrt(co) * gc_ref[...]).astype(BF16)
        g, _ = _gelu(y_ref[...])
        z = _dot(g.astype(BF16), wg_ref[...], NN)
        z_ref[...] = z
        so = g * jax.nn.sigmoid(z)
        mixed_ref[:, dc:] = (so * _mean_sq_rsqrt(so) * gs_ref[...]).astype(BF16)
        h1 = h0_ref[...] + _dot(mixed_ref[...], wo_ref[...], NN)
        h1_ref[...] = h1
        hn_ref[...] = (h1 * _mean_sq_rsqrt(h1) * gf_ref[...]).astype(BF16)

    const = lambda i: (0, 0)
    tile = _fused_tile(t)
    row = lambda w: pl.BlockSpec((tile, w), lambda i: (i, 0))
    return pl.pallas_call(
        body, name="mix_fwd_out_proj", grid=(t // tile,),
        in_specs=[row(3 * dc), pl.BlockSpec((HALO_ROWS, 3 * dc), lambda i: (_prev_halo(i, t, tile), 0)), row(ds),
                  pl.BlockSpec((ds, ds), const), pl.BlockSpec(conv_w.shape, const),
                  pl.BlockSpec((1, dc), const), pl.BlockSpec((1, ds), const),
                  row(d), pl.BlockSpec(w_out.shape, const), pl.BlockSpec((1, d), const)],
        out_specs=[row(dc + ds), row(ds), row(d), row(d)],
        out_shape=[jax.ShapeDtypeStruct((t, dc + ds), BF16), jax.ShapeDtypeStruct((t, ds), F32),
                   jax.ShapeDtypeStruct((t, d), F32), jax.ShapeDtypeStruct((t, d), BF16)],
        compiler_params=_params(),
    )(proj, proj, y, w_glu, conv_w, gain_c, gain_s, h0, w_out, gain_ffn)


def _mix_bwd1(proj, y, z, dmixed, w_glu, conv_w, gain_c, gain_s):
    t = proj.shape[0]
    dc = conv_w.shape[1]
    ds = y.shape[1]

    def body(p_ref, halo_ref, y_ref, z_ref, dm_ref, wg_ref, cw_ref, gc_ref, gs_ref,
             db_ref, dconv_ref, dy_ref, dwg_ref, dcw_ref, dgc_ref, dgs_ref):
        i = pl.program_id(0)

        @pl.when(i == 0)
        def _():
            dwg_ref[...] = jnp.zeros_like(dwg_ref)
            dcw_ref[...] = jnp.zeros_like(dcw_ref)
            dgc_ref[...] = jnp.zeros_like(dgc_ref)
            dgs_ref[...] = jnp.zeros_like(dgs_ref)

        p = p_ref[...]
        b, c, v = p[:, :dc], p[:, dc:2 * dc], p[:, 2 * dc:3 * dc]
        cv = c * v
        hp = halo_ref[...]
        x2, x1 = _causal_taps(cv, hp[:, dc:2 * dc] * hp[:, 2 * dc:3 * dc], i == 0)
        cw = cw_ref[...]
        conv = cw[0:1] * x2 + cw[1:2] * x1 + cw[2:3] * cv
        co = b * conv
        dm = dm_ref[...]
        dco, dgt = _rms_bwd(co, _mean_sq_rsqrt(co), gc_ref[...], dm[:, :dc])
        dgc_ref[...] += jnp.sum(dgt, axis=0, keepdims=True)
        db_ref[...] = (dco * conv).astype(BF16)
        dconv = dco * b
        dconv_ref[...] = dconv
        dcw_ref[0:1, :] += jnp.sum(dconv * x2, axis=0, keepdims=True)
        dcw_ref[1:2, :] += jnp.sum(dconv * x1, axis=0, keepdims=True)
        dcw_ref[2:3, :] += jnp.sum(dconv * cv, axis=0, keepdims=True)

        yv = y_ref[...]
        g, th = _gelu(yv)
        sg = jax.nn.sigmoid(z_ref[...])
        so = g * sg
        dso, dgt = _rms_bwd(so, _mean_sq_rsqrt(so), gs_ref[...], dm[:, dc:])
        dgs_ref[...] += jnp.sum(dgt, axis=0, keepdims=True)
        dz = (dso * g * sg * (1.0 - sg)).astype(BF16)
        dg = dso * sg + _dot(dz, wg_ref[...], NT)
        dwg_ref[...] += _dot(g.astype(BF16), dz, TN)
        dy_ref[...] = dg * _gelu_grad(yv, th)

    const = lambda i: (0, 0)
    tile = _fused_tile(t)
    row = lambda w: pl.BlockSpec((tile, w), lambda i: (i, 0))
    return pl.pallas_call(
        body, name="mix_bwd1", grid=(t // tile,),
        in_specs=[row(3 * dc), pl.BlockSpec((HALO_ROWS, 3 * dc), lambda i: (_prev_halo(i, t, tile), 0)),
                  row(ds), row(ds), row(dc + ds), pl.BlockSpec((ds, ds), const),
                  pl.BlockSpec(conv_w.shape, const), pl.BlockSpec((1, dc), const),
                  pl.BlockSpec((1, ds), const)],
        out_specs=[row(dc), row(dc), row(ds), pl.BlockSpec((ds, ds), const),
                   pl.BlockSpec(conv_w.shape, const), pl.BlockSpec((1, dc), const),
                   pl.BlockSpec((1, ds), const)],
        out_shape=[jax.ShapeDtypeStruct((t, dc), BF16), jax.ShapeDtypeStruct((t, dc), F32),
                   jax.ShapeDtypeStruct((t, ds), F32), jax.ShapeDtypeStruct((ds, ds), F32),
                   jax.ShapeDtypeStruct(conv_w.shape, F32), jax.ShapeDtypeStruct((1, dc), F32),
                   jax.ShapeDtypeStruct((1, ds), F32)],
        compiler_params=_params(),
    )(proj, proj, y, z, dmixed, w_glu, conv_w, gain_c, gain_s)


def _mix_bwd2(proj, dconv, conv_w, deps=()):
    t = proj.shape[0]
    dc = conv_w.shape[1]
    tile = _wide_tile(t)
    n_tiles = t // tile

    def body(c_ref, v_ref, d_ref, halo_ref, cw_ref, o_ref):
        i = pl.program_id(0)
        d = d_ref[...]
        u1, u2 = _anticausal_taps(d, halo_ref[...], i == n_tiles - 1)
        cw = cw_ref[...]
        dcv = cw[2:3] * d + cw[1:2] * u1 + cw[0:1] * u2
        o_ref[:, :dc] = (dcv * v_ref[...]).astype(BF16)
        o_ref[:, dc:] = (dcv * c_ref[...]).astype(BF16)

    return pl.pallas_call(
        _ignoring_deps(body, 5, deps), name="mix_bwd2", grid=(n_tiles,),
        in_specs=[pl.BlockSpec((tile, dc), lambda i: (i, 1)),
                  pl.BlockSpec((tile, dc), lambda i: (i, 2)),
                  pl.BlockSpec((tile, dc), lambda i: (i, 0)),
                  pl.BlockSpec((HALO_ROWS, dc), lambda i: (_next_halo(i, t, tile), 0)),
                  pl.BlockSpec(conv_w.shape, lambda i: (0, 0))] + [ANY_SPEC] * len(deps),
        out_specs=pl.BlockSpec((tile, 2 * dc), lambda i: (i, 0)),
        out_shape=jax.ShapeDtypeStruct((t, 2 * dc), BF16), compiler_params=_params(),
    )(proj, proj, dconv, dconv, conv_w, *deps)


def _stage_rows(stage_ref, row0, src, wrap=None):
    n, width = src.shape
    for c in range(0, width, STAGE_COLS):
        v = src[:, pl.ds(c, STAGE_COLS)].astype(F32)
        stage_ref[pl.ds(row0, n), pl.ds(c, STAGE_COLS)] = v if wrap is None else wrap(v)


def _conv_taps(stage_ref, fw_ref, fb_ref, c0, r, rows=ROW_CHUNK):
    cols = pl.ds(c0, FFN_COLS)
    x0, x1, x2 = (stage_ref[pl.ds(HALO_ROWS + r - k * N_SEG, rows), cols] for k in range(3))
    w = fw_ref[:, cols]
    return w[0:1] * x2 + w[1:2] * x1 + w[2:3] * x0 + fb_ref[:, cols], x2, x1, x0


def _gated_fwd(up_pre, fw, fb):
    t, f2 = up_pre.shape
    f = f2 // 2

    def body(up_ref, halo_ref, fw_ref, fb_ref, act_ref, stage_ref):
        first = pl.program_id(0) == 0
        _stage_rows(stage_ref, 0, halo_ref, lambda v: _wrap_prev_halo(v, first))
        _stage_rows(stage_ref, HALO_ROWS, up_ref)
        for c0 in range(0, f, FFN_COLS):
            for r in range(0, ROW_TILE, ROW_CHUNK):
                a = _conv_taps(stage_ref, fw_ref, fb_ref, c0, r)[0]
                val = _conv_taps(stage_ref, fw_ref, fb_ref, f + c0, r)[0]
                act_ref[pl.ds(r, ROW_CHUNK), pl.ds(c0, FFN_COLS)] = (a * jax.nn.sigmoid(a) * val).astype(BF16)

    const = lambda a: pl.BlockSpec(a.shape, lambda i: (0, 0))
    return pl.pallas_call(
        body, name="ffn_fwd", grid=(t // ROW_TILE,),
        in_specs=[pl.BlockSpec((ROW_TILE, f2), lambda i: (i, 0)),
                  pl.BlockSpec((HALO_ROWS, f2), lambda i: (_prev_halo(i, t), 0)), const(fw), const(fb)],
        out_specs=pl.BlockSpec((ROW_TILE, f), lambda i: (i, 0)),
        out_shape=jax.ShapeDtypeStruct((t, f), BF16),
        scratch_shapes=[pltpu.VMEM((HALO_ROWS + ROW_TILE, f2), F32)], compiler_params=_params(),
    )(up_pre, up_pre, fw, fb)


def _gated_bwd(up_pre, dact, fw, fb, deps=()):
    t, f2 = up_pre.shape
    f = f2 // 2
    n_tiles = t // ROW_TILE
    chunks = [(r, ROW_CHUNK) for r in range(0, ROW_TILE, ROW_CHUNK)] + [(ROW_TILE, HALO_ROWS)]

    def body(up_ref, prev_ref, next_ref, dact_in_ref, dact_next_ref, fw_ref, fb_ref,
             out_ref, dfw_ref, dfb_ref, dup_ref, stage_ref, dact_ref):
        first = pl.program_id(0) == 0
        last = pl.program_id(0) == n_tiles - 1

        @pl.when(first)
        def _():
            dfw_ref[...] = jnp.zeros_like(dfw_ref)
            dfb_ref[...] = jnp.zeros_like(dfb_ref)

        _stage_rows(dact_ref, 0, dact_in_ref)
        _stage_rows(dact_ref, ROW_TILE, dact_next_ref, lambda v: _wrap_next_halo(v, last))
        _stage_rows(stage_ref, 0, prev_ref, lambda v: _wrap_prev_halo(v, first))
        _stage_rows(stage_ref, HALO_ROWS, up_ref)
        _stage_rows(stage_ref, HALO_ROWS + ROW_TILE, next_ref, lambda v: _wrap_next_halo(v, last))
        fold = lambda v: sum(v[s:s + SUBLANES] for s in range(0, ROW_CHUNK, SUBLANES))
        for c0 in range(0, f, FFN_COLS):
            starts = (c0, f + c0)
            sums = [[jnp.zeros((SUBLANES, FFN_COLS), F32)] * 4 for _ in starts]
            for r, rows in chunks:
                taps = [_conv_taps(stage_ref, fw_ref, fb_ref, c, r, rows) for c in starts]
                a, val = taps[0][0], taps[1][0]
                da_ct = dact_ref[pl.ds(r, rows), pl.ds(c0, FFN_COLS)]
                sg = jax.nn.sigmoid(a)
                dup = (da_ct * val * sg * (1.0 + a * (1.0 - sg)), da_ct * a * sg)
                for k in range(2):
                    dup_ref[k, pl.ds(r, rows), :] = dup[k]
                    if r < ROW_TILE:
                        terms = (dup[k], dup[k] * taps[k][1], dup[k] * taps[k][2], dup[k] * taps[k][3])
                        sums[k] = [s + fold(v) for s, v in zip(sums[k], terms)]
            for k, c in enumerate(starts):
                cols = pl.ds(c, FFN_COLS)
                s_b, s_w0, s_w1, s_w2 = (jnp.sum(p, axis=0, keepdims=True) for p in sums[k])
                dfb_ref[:, cols] += s_b
                for tap, s_w in enumerate((s_w0, s_w1, s_w2)):
                    dfw_ref[tap:tap + 1, cols] += s_w
                cw = fw_ref[:, cols]
                for r in range(0, ROW_TILE, ROW_CHUNK):
                    d, u1, u2 = (dup_ref[k, pl.ds(r + s * N_SEG, ROW_CHUNK), :] for s in range(3))
                    out_ref[pl.ds(r, ROW_CHUNK), cols] = (cw[2:3] * d + cw[1:2] * u1 + cw[0:1] * u2).astype(BF16)

    tile = lambda w: pl.BlockSpec((ROW_TILE, w), lambda i: (i, 0))
    halo = lambda w, index: pl.BlockSpec((HALO_ROWS, w), lambda i: (index(i, t), 0))
    const = lambda a: pl.BlockSpec(a.shape, lambda i: (0, 0))
    return pl.pallas_call(
        _ignoring_deps(body, 7, deps), name="ffn_bwd", grid=(n_tiles,),
        in_specs=[tile(f2), halo(f2, _prev_halo), halo(f2, _next_halo), tile(f), halo(f, _next_halo),
                  const(fw), const(fb)] + [ANY_SPEC] * len(deps),
        out_specs=[tile(f2), const(fw), const(fb)],
        out_shape=[jax.ShapeDtypeStruct((t, f2), BF16), jax.ShapeDtypeStruct(fw.shape, F32),
                   jax.ShapeDtypeStruct(fb.shape, F32)],
        scratch_shapes=[pltpu.VMEM((2, ROW_TILE + HALO_ROWS, FFN_COLS), F32),
                        pltpu.VMEM((HALO_ROWS + ROW_TILE + HALO_ROWS, f2), F32),
                        pltpu.VMEM((ROW_TILE + HALO_ROWS, f), F32)],
        compiler_params=_params(),
    )(up_pre, up_pre, up_pre, dact, dact, fw, fb, *deps)


def _to_segments(a):
    t, c = a.shape
    return a.reshape(N_SEG, t // N_SEG, c).transpose(1, 0, 2).reshape(t, c)


def _from_segments(a):
    t, c = a.shape
    return a.reshape(t // N_SEG, N_SEG, c).transpose(1, 0, 2).reshape(t, c)


def _cmul(ar, ai, br, bi):
    return ar * br - ai * bi, ar * bi + ai * br


def _cpow(ar, ai, n):
    out = None
    while n:
        if n & 1:
            out = (ar, ai) if out is None else _cmul(out[0], out[1], ar, ai)
        ar, ai = _cmul(ar, ai, ar, ai)
        n >>= 1
    return out


def _segment_carries(pr, pi, fr, fi, forward):
    row = lax.broadcasted_iota(jnp.int32, fr.shape, 0)
    edge = row == (0 if forward else N_SEG - 1)
    shift = 1 if forward else N_SEG - 1
    sr, si = jnp.zeros_like(fr), jnp.zeros_like(fi)
    for _ in range(N_SEG - 1):
        tr, ti = _cmul(pr, pi, sr, si)
        sr = jnp.where(edge, 0.0, pltpu.roll(tr + fr, shift, 0))
        si = jnp.where(edge, 0.0, pltpu.roll(ti + fi, shift, 0))
    return sr, si


def _rows(i):
    if isinstance(i, int):
        return pl.ds(i * SUBLANES, SUBLANES)
    return pl.ds(pl.multiple_of(i * SUBLANES, SUBLANES), SUBLANES)


def _scan_steps(n, body, init):
    def trip(k, carry):
        for s in range(SCAN_UNROLL):
            carry = body(k * SCAN_UNROLL + s, carry)
        return carry

    carry = lax.fori_loop(0, n // SCAN_UNROLL, trip, init)
    for i in range(n - n % SCAN_UNROLL, n):
        carry = body(i, carry)
    return carry


def _s5_fwd(proj, u_col, bb_re, bb_im, a_re, a_im, cc_re, cc_im, d_skip):
    t = proj.shape[0]
    nch, _, cs = bb_re.shape
    ds = nch * SSM_CHUNK
    u_blk = u_col // SSM_CHUNK
    steps = t // N_SEG

    def body(u_ref, bbr_ref, bbi_ref, ar_ref, ai_ref, ccr_ref, cci_ref, d_ref, sr_ref, si_ref, y_ref):
        ub = u_ref[...].astype(BF16)
        sr_ref[...] = _dot(ub, bbr_ref[...], NN)
        si_ref[...] = _dot(ub, bbi_ref[...], NN)
        ar = jnp.broadcast_to(ar_ref[...], (N_SEG, cs))
        ai = jnp.broadcast_to(ai_ref[...], (N_SEG, cs))
        zero = jnp.zeros((N_SEG, cs), F32)

        def totals(i, carry):
            tr, ti = _cmul(ar, ai, *carry)
            return tr + sr_ref[_rows(i), :], ti + si_ref[_rows(i), :]

        fr, fi = _scan_steps(steps, totals, (zero, zero))
        s0r, s0i = _segment_carries(*_cpow(ar, ai, steps), fr, fi, True)

        def scan(i, carry):
            tr, ti = _cmul(ar, ai, *carry)
            nr, ni = tr + sr_ref[_rows(i), :], ti + si_ref[_rows(i), :]
            sr_ref[_rows(i), :] = nr
            si_ref[_rows(i), :] = ni
            return nr, ni

        _scan_steps(steps, scan, (s0r, s0i))
        y_ref[...] = (_dot(sr_ref[...].astype(BF16), ccr_ref[...], NT)
                      - _dot(si_ref[...].astype(BF16), cci_ref[...], NT)
                      + d_ref[...] * u_ref[...])

    chunk3 = lambda r, c: pl.BlockSpec((None, r, c), lambda j: (j, 0, 0))
    return pl.pallas_call(
        body, name="s5_fwd", grid=(nch,),
        in_specs=[pl.BlockSpec((t, SSM_CHUNK), lambda j: (0, j + u_blk)),
                  chunk3(SSM_CHUNK, cs), chunk3(SSM_CHUNK, cs), chunk3(1, cs), chunk3(1, cs),
                  chunk3(SSM_CHUNK, cs), chunk3(SSM_CHUNK, cs), chunk3(1, SSM_CHUNK)],
        out_specs=[pl.BlockSpec((t, cs), lambda j: (0, j)), pl.BlockSpec((t, cs), lambda j: (0, j)),
                   pl.BlockSpec((t, SSM_CHUNK), lambda j: (0, j))],
        out_shape=[jax.ShapeDtypeStruct((t, nch * cs), F32), jax.ShapeDtypeStruct((t, nch * cs), F32),
                   jax.ShapeDtypeStruct((t, ds), F32)],
        compiler_params=_params(),
    )(proj, bb_re, bb_im, a_re, a_im, cc_re, cc_im, d_skip)


def _s5_bwd(dy, proj, u_col, s_re, s_im, bb_re, bb_im, a_re, a_im, cc_re, cc_im, d_skip, gpc):
    t, ds = dy.shape
    nch, _, cs = bb_re.shape
    u_blk = u_col // SSM_CHUNK
    steps = t // N_SEG

    def body(dy_ref, u_ref, sr_ref, si_ref, bbr_ref, bbi_ref, ar_ref, ai_ref, ccr_ref, cci_ref, d_ref,
             du_ref, dbbr_ref, dbbi_ref, dar_ref, dai_ref, dccr_ref, dcci_ref, dd_ref, gr_ref, gi_ref):
        dyv = dy_ref[...]
        dyb = dyv.astype(BF16)
        gr_ref[...] = _dot(dyb, ccr_ref[...], NN)
        gi_ref[...] = -_dot(dyb, cci_ref[...], NN)
        ar = jnp.broadcast_to(ar_ref[...], (N_SEG, cs))
        ai = -jnp.broadcast_to(ai_ref[...], (N_SEG, cs))
        zero = jnp.zeros((N_SEG, cs), F32)

        def totals(k, carry):
            i = steps - 1 - k
            tr, ti = _cmul(ar, ai, *carry)
            return tr + gr_ref[_rows(i), :], ti + gi_ref[_rows(i), :]

        fr, fi = _scan_steps(steps, totals, (zero, zero))
        e0r, e0i = _segment_carries(*_cpow(ar, ai, steps), fr, fi, False)

        def step(i, gr, gi, pr, pi, acc_r, acc_i):
            tr, ti = _cmul(ar, ai, gr, gi)
            nr, ni = tr + gr_ref[_rows(i), :], ti + gi_ref[_rows(i), :]
            gr_ref[_rows(i), :] = nr
            gi_ref[_rows(i), :] = ni
            return nr, ni, acc_r + nr * pr + ni * pi, acc_i + ni * pr - nr * pi

        def scan(k, carry):
            i = steps - 1 - k
            gr, gi, acc_r, acc_i = carry
            return step(i, gr, gi, sr_ref[_rows(i - 1), :], si_ref[_rows(i - 1), :], acc_r, acc_i)

        gr, gi, acc_r, acc_i = _scan_steps(steps - 1, scan, (e0r, e0i, zero, zero))
        row = lax.broadcasted_iota(jnp.int32, (N_SEG, cs), 0)
        last = _rows(steps - 1)
        pr = jnp.where(row == 0, 0.0, pltpu.roll(sr_ref[last, :], 1, 0))
        pi = jnp.where(row == 0, 0.0, pltpu.roll(si_ref[last, :], 1, 0))
        _, _, acc_r, acc_i = step(0, gr, gi, pr, pi, acc_r, acc_i)
        dar_ref[...] = jnp.sum(acc_r, axis=0, keepdims=True)
        dai_ref[...] = jnp.sum(acc_i, axis=0, keepdims=True)

        uv = u_ref[...]
        ub = uv.astype(BF16)
        grb = gr_ref[...].astype(BF16)
        gib = gi_ref[...].astype(BF16)
        du = d_ref[...] * dyv + _dot(grb, bbr_ref[...], NT) + _dot(gib, bbi_ref[...], NT)
        du_ref[...] = du.astype(BF16)
        def put_groups(ref, full):
            for gl in range(gpc):
                ref[gl] = full[gl * hb:(gl + 1) * hb, gl * pb:(gl + 1) * pb]

        put_groups(dbbr_ref, _dot(ub, grb, TN))
        put_groups(dbbi_ref, _dot(ub, gib, TN))
        put_groups(dccr_ref, _dot(dyb, sr_ref[...].astype(BF16), TN))
        put_groups(dcci_ref, -_dot(dyb, si_ref[...].astype(BF16), TN))
        dd_ref[...] = jnp.sum(dyv * uv, axis=0, keepdims=True)

    hb, pb = SSM_CHUNK // gpc, cs // gpc
    groups = pl.BlockSpec((None, gpc, hb, pb), lambda j: (j, 0, 0, 0))
    groups_shape = jax.ShapeDtypeStruct((nch, gpc, hb, pb), F32)
    chunk3 = lambda r, c: pl.BlockSpec((None, r, c), lambda j: (j, 0, 0))
    cols = lambda w: pl.BlockSpec((t, w), lambda j: (0, j))
    return pl.pallas_call(
        body, name="s5_bwd", grid=(nch,),
        in_specs=[cols(SSM_CHUNK), pl.BlockSpec((t, SSM_CHUNK), lambda j: (0, j + u_blk)), cols(cs), cols(cs),
                  chunk3(SSM_CHUNK, cs), chunk3(SSM_CHUNK, cs), chunk3(1, cs), chunk3(1, cs),
                  chunk3(SSM_CHUNK, cs), chunk3(SSM_CHUNK, cs), chunk3(1, SSM_CHUNK)],
        out_specs=[cols(SSM_CHUNK), groups, groups, chunk3(1, cs), chunk3(1, cs), groups, groups,
                   chunk3(1, SSM_CHUNK)],
        out_shape=[jax.ShapeDtypeStruct((t, ds), BF16), groups_shape, groups_shape,
                   jax.ShapeDtypeStruct((nch, 1, cs), F32), jax.ShapeDtypeStruct((nch, 1, cs), F32),
                   groups_shape, groups_shape, jax.ShapeDtypeStruct((nch, 1, SSM_CHUNK), F32)],
        scratch_shapes=[pltpu.VMEM((t, cs), F32), pltpu.VMEM((t, cs), F32)],
        compiler_params=_params(),
    )(dy, proj, s_re, s_im, bb_re, bb_im, a_re, a_im, cc_re, cc_im, d_skip)


def _discretize(lr, li, log_dt, br, bi):
    dt = jnp.exp(log_dt)
    mag = jnp.exp(lr * dt)
    ang = li * dt
    a_re = mag * jnp.cos(ang)
    a_im = mag * jnp.sin(ang)
    den = lr * lr + li * li
    nr = a_re - 1.0
    f_re = (nr * lr + a_im * li) / den
    f_im = (a_im * lr - nr * li) / den
    return a_re, a_im, f_re * br - f_im * bi, f_re * bi + f_im * br


def _whole(shape):
    return pl.BlockSpec(shape, lambda: (0,) * len(shape))


def _disc_fwd(lr, li, log_dt, br, bi, cr, ci, nch, gpc, deps=()):
    gh, n_state = br.shape
    grp = gh // (nch * gpc)
    rows, cs = gpc * grp, gpc * n_state

    def body(lr_ref, li_ref, dt_ref, br_ref, bi_ref, cr_ref, ci_ref,
             ar_ref, ai_ref, bbr_ref, bbi_ref, ccr_ref, cci_ref, wide_ref):
        a_re, a_im, bb_re, bb_im = _discretize(lr_ref[...], li_ref[...], dt_ref[...], br_ref[...], bi_ref[...])
        for ref, val in ((ar_ref, a_re), (ai_ref, a_im)):
            for j in range(nch):
                for gl in range(gpc):
                    g0 = (j * gpc + gl) * grp
                    ref[j, :, gl * n_state:(gl + 1) * n_state] = val[g0:g0 + 1]
        for ref, val in ((bbr_ref, bb_re), (bbi_ref, bb_im), (ccr_ref, cr_ref[...]), (cci_ref, ci_ref[...])):
            wide_ref[...] = jnp.zeros_like(wide_ref)
            for j in range(nch):
                for gl in range(gpc):
                    g0 = (j * gpc + gl) * grp
                    wide_ref[j, gl * grp:(gl + 1) * grp, gl * n_state:(gl + 1) * n_state] = val[g0:g0 + grp]
            ref[...] = wide_ref[...].astype(BF16)

    args = (lr, li, log_dt, br, bi, cr, ci)
    row_shape, wide_shape = (nch, 1, cs), (nch, rows, cs)
    outs = [jax.ShapeDtypeStruct(row_shape, F32)] * 2 + [jax.ShapeDtypeStruct(wide_shape, BF16)] * 4
    return pl.pallas_call(
        _ignoring_deps(body, 7, deps), name="disc_fwd",
        in_specs=[_whole(a.shape) for a in args] + [ANY_SPEC] * len(deps),
        out_specs=[_whole(o.shape) for o in outs], out_shape=outs,
        scratch_shapes=[pltpu.VMEM(wide_shape, F32)],
    )(*args, *deps)


def _disc_bwd(lr, li, log_dt, br, bi, dar, dai, dbbr, dbbi):
    def body(lr_ref, li_ref, dt_ref, br_ref, bi_ref, dar_ref, dai_ref, dbbr_ref, dbbi_ref,
             dlr_ref, dli_ref, ddt_ref, dbr_ref, dbi_ref):
        _, vjp = jax.vjp(_discretize, lr_ref[...], li_ref[...], dt_ref[...], br_ref[...], bi_ref[...])
        grads = vjp((dar_ref[...], dai_ref[...], dbbr_ref[...], dbbi_ref[...]))
        for ref, val in zip((dlr_ref, dli_ref, ddt_ref, dbr_ref, dbi_ref), grads):
            ref[...] = val

    args = (lr, li, log_dt, br, bi, dar, dai, dbbr, dbbi)
    outs = (lr, li, log_dt, br, bi)
    return pl.pallas_call(
        body, name="disc_bwd", in_specs=[_whole(a.shape) for a in args],
        out_specs=[_whole(a.shape) for a in outs],
        out_shape=[jax.ShapeDtypeStruct(a.shape, F32) for a in outs],
    )(*args)


def _adamw(w, g, m, v):
    m = ADAM_B1 * m + (1.0 - ADAM_B1) * g
    v = ADAM_B2 * v + (1.0 - ADAM_B2) * (g * g)
    m_hat = m / ADAM_BC1
    v_hat = v / ADAM_BC2
    delta = -ADAM_LR * (m_hat / (jnp.sqrt(v_hat) + ADAM_EPS) + ADAM_WD * w)
    return delta, m, v


def _adamw_reduce(name, parts, w, m, v):
    _, r, c = parts.shape
    tr = r
    for cand in (256, 176, 128):
        if r % cand == 0:
            tr = cand
            break

    def body(p_ref, w_ref, m_ref, v_ref, g_ref, d_ref, nm_ref, nv_ref):
        g = p_ref[0].astype(F32)
        for k in range(1, N_DEV):
            g = g + p_ref[k].astype(F32)
        delta, nm, nv = _adamw(w_ref[...], g, m_ref[...], v_ref[...])
        g_ref[...] = g
        d_ref[...] = delta
        nm_ref[...] = nm
        nv_ref[...] = nv

    blk = pl.BlockSpec((tr, c), lambda i: (i, 0))
    return pl.pallas_call(
        body, name=name, grid=(r // tr,),
        in_specs=[pl.BlockSpec((N_DEV, tr, c), lambda i: (0, i, 0)), blk, blk, blk],
        out_specs=[blk] * 4, out_shape=[jax.ShapeDtypeStruct((r, c), F32)] * 4,
        compiler_params=_params(),
    )(parts, w, m, v)


def _sum_parts(name, parts):
    _, r, c = parts.shape

    def body(p_ref, o_ref):
        g = p_ref[0]
        for k in range(1, N_DEV):
            g = g + p_ref[k]
        o_ref[...] = g

    return pl.pallas_call(
        body, name=name, in_specs=[_whole(parts.shape)], out_specs=_whole((r, c)),
        out_shape=jax.ShapeDtypeStruct((r, c), F32), compiler_params=_params(),
    )(parts)


def _adamw_many(name, grads, ws, ms, vs):
    n = len(grads)

    def body(*refs):
        ins, outs = refs[:4 * n], refs[4 * n:]
        for i in range(n):
            g, w, m, v = (ins[j * n + i][...] for j in range(4))
            for ref, val in zip((outs[i], outs[n + i], outs[2 * n + i]), _adamw(w, g, m, v)):
                ref[...] = val

    args = (*grads, *ws, *ms, *vs)
    outs = pl.pallas_call(
        body, name=name, in_specs=[_whole(a.shape) for a in args],
        out_specs=[_whole(a.shape) for a in ws] * 3,
        out_shape=[jax.ShapeDtypeStruct(a.shape, F32) for a in ws] * 3, compiler_params=_params(),
    )(*args)
    return outs[:n], outs[n:2 * n], outs[2 * n:]


def _pack(arrays, rows):
    flat = jnp.concatenate([a.reshape(-1) for a in arrays])
    return jnp.pad(flat, (0, rows * LANES - flat.shape[0])).reshape(rows, LANES)


def _unpack(packed, shapes):
    flat = packed.reshape(-1)
    out, off = [], 0
    for s in shapes:
        n = math.prod(s)
        out.append(flat[off:off + n].reshape(s))
        off += n
    return out


def _packed_rows(shapes):
    n = sum(math.prod(s) for s in shapes)
    return -(-n // (SUBLANES * LANES)) * SUBLANES


def kernel(x, meta_tokens, norm_mix_g, w_in, conv_w, ssm_lam_re, ssm_lam_im, ssm_log_dt, ssm_b_re, ssm_b_im, ssm_c_re, ssm_c_im, ssm_d, ssm_w_glu, gain_conv_out, gain_ssm_out, w_out, norm_ffn_g, w_up, ffn_conv_w, ffn_conv_b, w_down, norm_final_g, loss_target, m_meta_tokens, m_norm_mix_g, m_w_in, m_conv_w, m_ssm_lam_re, m_ssm_lam_im, m_ssm_log_dt, m_ssm_b_re, m_ssm_b_im, m_ssm_c_re, m_ssm_c_im, m_ssm_d, m_ssm_w_glu, m_gain_conv_out, m_gain_ssm_out, m_w_out, m_norm_ffn_g, m_w_up, m_ffn_conv_w, m_ffn_conv_b, m_w_down, m_norm_final_g, v_meta_tokens, v_norm_mix_g, v_w_in, v_conv_w, v_ssm_lam_re, v_ssm_lam_im, v_ssm_log_dt, v_ssm_b_re, v_ssm_b_im, v_ssm_c_re, v_ssm_c_im, v_ssm_d, v_ssm_w_glu, v_gain_conv_out, v_gain_ssm_out, v_w_out, v_norm_ffn_g, v_w_up, v_ffn_conv_w, v_ffn_conv_b, v_w_down, v_norm_final_g):
    weights = dict(meta_tokens=meta_tokens, norm_mix_g=norm_mix_g, w_in=w_in, conv_w=conv_w, ssm_lam_re=ssm_lam_re, ssm_lam_im=ssm_lam_im, ssm_log_dt=ssm_log_dt, ssm_b_re=ssm_b_re, ssm_b_im=ssm_b_im, ssm_c_re=ssm_c_re, ssm_c_im=ssm_c_im, ssm_d=ssm_d, ssm_w_glu=ssm_w_glu, gain_conv_out=gain_conv_out, gain_ssm_out=gain_ssm_out, w_out=w_out, norm_ffn_g=norm_ffn_g, w_up=w_up, ffn_conv_w=ffn_conv_w, ffn_conv_b=ffn_conv_b, w_down=w_down, norm_final_g=norm_final_g)
    mom_m = dict(meta_tokens=m_meta_tokens, norm_mix_g=m_norm_mix_g, w_in=m_w_in, conv_w=m_conv_w, ssm_lam_re=m_ssm_lam_re, ssm_lam_im=m_ssm_lam_im, ssm_log_dt=m_ssm_log_dt, ssm_b_re=m_ssm_b_re, ssm_b_im=m_ssm_b_im, ssm_c_re=m_ssm_c_re, ssm_c_im=m_ssm_c_im, ssm_d=m_ssm_d, ssm_w_glu=m_ssm_w_glu, gain_conv_out=m_gain_conv_out, gain_ssm_out=m_gain_ssm_out, w_out=m_w_out, norm_ffn_g=m_norm_ffn_g, w_up=m_w_up, ffn_conv_w=m_ffn_conv_w, ffn_conv_b=m_ffn_conv_b, w_down=m_w_down, norm_final_g=m_norm_final_g)
    mom_v = dict(meta_tokens=v_meta_tokens, norm_mix_g=v_norm_mix_g, w_in=v_w_in, conv_w=v_conv_w, ssm_lam_re=v_ssm_lam_re, ssm_lam_im=v_ssm_lam_im, ssm_log_dt=v_ssm_log_dt, ssm_b_re=v_ssm_b_re, ssm_b_im=v_ssm_b_im, ssm_c_re=v_ssm_c_re, ssm_c_im=v_ssm_c_im, ssm_d=v_ssm_d, ssm_w_glu=v_ssm_w_glu, gain_conv_out=v_gain_conv_out, gain_ssm_out=v_gain_ssm_out, w_out=v_w_out, norm_ffn_g=v_norm_ffn_g, w_up=v_w_up, ffn_conv_w=v_ffn_conv_w, ffn_conv_b=v_ffn_conv_b, w_down=v_w_down, norm_final_g=v_norm_final_g)
    names = list(weights)

    n_meta, d_meta = meta_tokens.shape
    seq, d = x.shape[1], x.shape[2]
    rows_used = n_meta + seq
    t = -(-rows_used // ROW_TILE) * ROW_TILE
    d_in_s = w_in.shape[2]
    dc_s = conv_w.shape[2]
    dc = dc_s * N_DEV
    ds = ssm_w_glu.shape[2]
    n_groups, n_state, grp = ssm_b_re.shape[1:]
    ns = n_groups * n_state
    nch = ds // SSM_CHUNK
    gpc = n_groups // nch
    ff_s = w_up.shape[2]
    dn_s = w_down.shape[1]
    assert 3 * dc + ds == d_in_s * N_DEV and 2 * dn_s == ff_s and t % (N_SEG * SUBLANES) == 0

    small_shard = jnp.concatenate([meta_tokens.reshape(-1), conv_w.reshape(-1), ffn_conv_w.reshape(-1)])
    n_small = small_shard.shape[0]
    small_rows = -(-n_small // LANES)
    small_shard = jnp.pad(small_shard, (0, small_rows * LANES - n_small)).reshape(small_rows, LANES)
    ag, ag_token = _gather2_start("gather_weights_start", [
        small_shard, w_in[0].astype(BF16), ssm_w_glu[0].astype(BF16), w_out[0].astype(BF16),
        jnp.swapaxes(w_up[0], 0, 1).astype(BF16), w_down[0].astype(BF16)])
    fb = ffn_conv_b

    gh = n_groups * grp
    per_h = lambda a: jnp.broadcast_to(a.reshape(n_groups, 1, -1), (n_groups, grp, n_state)).reshape(gh, n_state)
    ghp = lambda a: a.transpose(0, 1, 3, 2).reshape(gh, n_state)
    lr, li, log_dt_e = per_h(ssm_lam_re), per_h(ssm_lam_im), per_h(ssm_log_dt)
    br, bi = ghp(ssm_b_re), ghp(ssm_b_im)
    a_re_c, a_im_c, bbm_re, bbm_im, ccm_re, ccm_im = _disc_fwd(
        lr, li, log_dt_e, br, bi, ssm_c_re.reshape(gh, n_state), ssm_c_im.reshape(gh, n_state), nch, gpc,
        deps=(ag_token,))
    d_skip = ssm_d.reshape(nch, 1, SSM_CHUNK)

    target = _to_segments(jnp.pad(loss_target[0] + ag_token[0, 0], ((n_meta, t - rows_used), (0, 0))))
    token = _gather2_forward("gather_weights_forward_first", ag, (0, 1), (bbm_im, ccm_im, target))
    (g_small,) = _gather2_wait("gather_weights_wait_small", ag, (0,), token)
    g_small = g_small.reshape(N_DEV, -1)
    o1 = n_meta * d_meta
    o2 = o1 + 3 * dc_s
    meta_full = g_small[:, :o1].reshape(N_DEV, n_meta, d_meta).transpose(1, 0, 2).reshape(n_meta, d)
    conv_w_f = g_small[:, o1:o2].reshape(N_DEV, 3, dc_s).transpose(1, 0, 2).reshape(3, dc)
    fw = g_small[:, o2:o2 + 3 * ff_s].reshape(N_DEV, 3, ff_s).transpose(1, 0, 2).reshape(3, N_DEV * ff_s)
    h0 = _to_segments(jnp.concatenate([meta_full, x[0], jnp.zeros((t - rows_used, d), F32)], axis=0))
    full_t = lambda w: pl.BlockSpec((t, w), lambda *_: (0, 0))

    hn1 = _rms_fwd("norm_mix", h0, norm_mix_g)
    (g_in,) = _gather2_wait("gather_weights_wait_in", ag, (1,), hn1)
    proj = _mm("proj", hn1, g_in, dims=NN, grid=(N_DEV,), a_spec=full_t(d),
               b_spec=pl.BlockSpec((None, d, d_in_s), lambda j: (j, 0, 0)),
               o_spec=pl.BlockSpec((t, d_in_s), lambda j: (0, j)),
               out_shape=jax.ShapeDtypeStruct((t, N_DEV * d_in_s), F32))
    s_re, s_im, y_ssm = _s5_fwd(proj, 3 * dc, bbm_re, bbm_im, a_re_c, a_im_c, ccm_re, ccm_im, d_skip)
    token = _gather2_forward("gather_weights_forward_up", ag, (2, 3, 4), (y_ssm,))
    g_glu, g_out = _gather2_wait("gather_weights_wait_mix", ag, (2, 3), token)
    w_out_f = g_out.reshape(-1, d)
    w_glu_f = g_glu.reshape(ds, ds)
    mixed, z_glu, h1, hn2 = _mix_fwd(proj, y_ssm, w_glu_f, conv_w_f, gain_conv_out, gain_ssm_out,
                                     h0, w_out_f, norm_ffn_g)
    tn_out = OUT_TILE
    token = _gather2_forward("gather_weights_forward_down", ag, (5,), (hn2,))
    (g_up,) = _gather2_wait("gather_weights_wait_up", ag, (4,), token)
    f2 = N_DEV * ff_s
    pair = 2 * ff_s
    w_up_t = g_up.reshape(f2, d)
    up_pre = _mm("up_proj", hn2, w_up_t, dims=NT, grid=(f2 // pair,), a_spec=full_t(d),
                 b_spec=pl.BlockSpec((pair, d), lambda j: (j, 0)),
                 o_spec=pl.BlockSpec((t, pair), lambda j: (0, j)),
                 out_shape=jax.ShapeDtypeStruct((t, f2), BF16))
    act = _gated_fwd(up_pre, fw, fb)
    (g_down,) = _gather2_wait("gather_weights_wait_down", ag, (5,), act)
    w_down_f = g_down.reshape(f2 // 2, d)

    dh2, dh2_b, loss_part, d_norm_final = _down_loss_bwd(
        act, w_down_f, h1, target, norm_final_g.reshape(1, d), n_meta, rows_used)
    dw_down = _mm("down_wgrad", act, dh2_b, dims=TN, grid=(f2 // 2 // pair,),
                  a_spec=pl.BlockSpec((t, pair), lambda j: (0, j)), b_spec=full_t(d),
                  o_spec=pl.BlockSpec((pair, d), lambda j: (j, 0)),
                  out_shape=jax.ShapeDtypeStruct((f2 // 2, d), BF16))
    ex_down, token = _send_start("exchange_down_start", [dw_down.reshape(N_DEV, dn_s, d)], gather=False)
    dact = _mm("down_dgrad", dh2_b, w_down_f, dims=NT, grid=(f2 // 2 // pair,), a_spec=full_t(d),
               b_spec=pl.BlockSpec((pair, d), lambda j: (j, 0)),
               o_spec=pl.BlockSpec((t, pair), lambda j: (0, j)),
               out_shape=jax.ShapeDtypeStruct((t, f2 // 2), BF16))
    dup_pre, d_fw, d_fb = _gated_bwd(up_pre, dact, fw, fb, deps=(token,))
    dw_up = _mm("up_wgrad", dup_pre, hn2, dims=TN, grid=(f2 // pair,),
                a_spec=pl.BlockSpec((t, pair), lambda j: (0, j)), b_spec=full_t(d),
                o_spec=pl.BlockSpec((pair, d), lambda j: (j, 0)),
                out_shape=jax.ShapeDtypeStruct((f2, d), BF16))
    ex_up, token = _send_start("exchange_up_start", [dw_up.reshape(N_DEV, ff_s, d)], gather=False)
    dh1, dh1_b, d_norm_ffn = _dgrad_rms_bwd("up_dgrad_norm_bwd", dh2, dup_pre, w_up_t, NN, h1, norm_ffn_g,
                                            deps=(token,))
    dmixed = _mm("out_dgrad", dh1_b, w_out_f, dims=NT, grid=((dc + ds) // tn_out,), a_spec=full_t(d),
                 b_spec=pl.BlockSpec((tn_out, d), lambda i: (i, 0)),
                 o_spec=pl.BlockSpec((t, tn_out), lambda i: (0, i)),
                 out_shape=jax.ShapeDtypeStruct((t, dc + ds), F32))
    dw_out = _mm("out_wgrad", mixed, dh1_b, dims=TN, grid=((dc + ds) // tn_out,),
                 a_spec=pl.BlockSpec((t, tn_out), lambda i: (0, i)), b_spec=full_t(d),
                 o_spec=pl.BlockSpec((tn_out, d), lambda i: (i, 0)),
                 out_shape=jax.ShapeDtypeStruct((dc + ds, d), BF16))
    db_gate, dconv, dy_ssm, d_wglu, d_conv_w, d_gain_c, d_gain_s = _mix_bwd1(
        proj, y_ssm, z_glu, dmixed, w_glu_f, conv_w_f, gain_conv_out, gain_ssm_out)
    (du, d_bbm_re, d_bbm_im, d_a_re, d_a_im, d_ccm_re, d_ccm_im, d_dskip) = _s5_bwd(
        dy_ssm, proj, 3 * dc, s_re, s_im, bbm_re, bbm_im, a_re_c, a_im_c, ccm_re, ccm_im, d_skip, gpc)

    from_bb = from_cc = lambda a: a.reshape(gh, n_state)
    first_h = lambda a: jnp.pad(a.reshape(n_groups, 1, n_state), ((0, 0), (0, grp - 1), (0, 0))).reshape(gh, n_state)
    over_h = lambda a: a.reshape(n_groups, grp, n_state).sum(axis=1)
    d_lr, d_li, d_dt_e, d_br, d_bi = _disc_bwd(
        lr, li, log_dt_e, br, bi, first_h(d_a_re), first_h(d_a_im), from_bb(d_bbm_re), from_bb(d_bbm_im))

    rep2d = dict(
        ssm_lam_re=(n_groups, n_state), ssm_lam_im=(n_groups, n_state), ssm_log_dt=(1, n_groups),
        ssm_b_re=(gh, n_state), ssm_b_im=(gh, n_state), ssm_c_re=(gh, n_state),
        ssm_c_im=(gh, n_state), ssm_d=(n_groups, grp), gain_conv_out=(1, dc),
        gain_ssm_out=(1, ds), norm_ffn_g=(1, d), ffn_conv_b=(1, N_DEV * ff_s), norm_final_g=(1, d))
    rep_names = list(rep2d)
    rep_grads = dict(
        ssm_lam_re=over_h(d_lr), ssm_lam_im=over_h(d_li), ssm_log_dt=over_h(d_dt_e).sum(axis=1),
        ssm_b_re=d_br, ssm_b_im=d_bi, ssm_c_re=from_cc(d_ccm_re), ssm_c_im=from_cc(d_ccm_im),
        ssm_d=d_dskip, gain_conv_out=d_gain_c, gain_ssm_out=d_gain_s, norm_ffn_g=d_norm_ffn,
        ffn_conv_b=d_fb, norm_final_g=d_norm_final)
    rep_shapes = [rep2d[n] for n in rep_names] + [(1, 1)]
    rep_rows = _packed_rows(rep_shapes)
    rep_pack = _pack([rep_grads[n] for n in rep_names] + [loss_part], rep_rows)
    ex_mix, token = _send_start("exchange_mix_start", [
        rep_pack, dw_out.reshape(N_DEV, -1, d), d_wglu.astype(BF16).reshape(N_DEV, -1, ds),
        d_conv_w.reshape(3, N_DEV, dc_s).transpose(1, 0, 2),
        d_fw.reshape(3, N_DEV, ff_s).transpose(1, 0, 2)],
        gather=[True, False, False, False, False])
    dcdv = _mix_bwd2(proj, dconv, conv_w_f, deps=(token,))
    dproj = jnp.concatenate([db_gate, dcdv, du], axis=1)
    d_in = N_DEV * d_in_s
    w_in_rows = g_in.transpose(1, 0, 2).reshape(d, d_in)
    dw_in = _mm("proj_wgrad", hn1, dproj, dims=TN, grid=(N_DEV,), a_spec=full_t(d),
                b_spec=pl.BlockSpec((t, d_in_s), lambda j: (0, j)),
                o_spec=pl.BlockSpec((None, d, d_in_s), lambda j: (j, 0, 0)),
                out_shape=jax.ShapeDtypeStruct((N_DEV, d, d_in_s), BF16))
    dh0, _, d_norm_mix = _dgrad_rms_bwd("proj_dgrad_norm_bwd", dh1, dproj, w_in_rows, NT, h0, norm_mix_g)
    dh0 = _from_segments(dh0)
    grad_x = dh0[n_meta:rows_used][None]
    d_meta_b = dh0[:n_meta].reshape(n_meta, N_DEV, d_meta).transpose(1, 0, 2)
    ex_in, ex_in_token = _send_start("exchange_in_start", [dw_in, d_meta_b, d_norm_mix],
                                     gather=[False, False, True])

    shard_out = {}

    def update(n, parts, transposed=False):
        sh = weights[n].shape
        two_d = lambda a: (jnp.swapaxes(a[0], 0, 1) if transposed else a.reshape(parts.shape[1:]))
        res = _adamw_reduce("adamw_" + n, parts, two_d(weights[n]), two_d(mom_m[n]), two_d(mom_v[n]))
        shard_out[n] = [(jnp.swapaxes(r, 0, 1) if transposed else r).reshape(sh) for r in res]
        return res[0]

    (p_down,) = _send_wait("exchange_down_wait", ex_down, (0,), ex_in_token)
    done = update("w_down", p_down)
    (p_up,) = _send_wait("exchange_up_wait", ex_up, (0,), done)
    done = update("w_up", p_up, transposed=True)
    (rep_parts,) = _send_wait("gather_small_grads_wait", ex_mix, (0,), done)
    rep_sum = _sum_parts("sum_small_grads", rep_parts)
    *rep_g, loss = _unpack(rep_sum, rep_shapes)
    loss = loss.reshape(())
    swapped = ("ssm_b_re", "ssm_b_im")
    to2d = lambda n, a: ghp(a) if n in swapped else a.reshape(rep2d[n])
    from2d = lambda n, a: (a.reshape(1, n_groups, grp, n_state).transpose(0, 1, 3, 2) if n in swapped
                           else a.reshape(weights[n].shape))
    as2d = lambda tree: [to2d(n, tree[n]) for n in rep_names]
    rep_res = _adamw_many("adamw_replicated", rep_g, as2d(weights), as2d(mom_m), as2d(mom_v))
    for i, n in enumerate(rep_names):
        shard_out[n] = [from2d(n, r) for r in (rep_g[i], *(res[i] for res in rep_res))]

    p_out, p_glu, p_cw, p_fw = _send_wait("exchange_mix_wait", ex_mix, (1, 2, 3, 4), rep_sum)
    update("w_out", p_out)
    update("ssm_w_glu", p_glu)
    update("conv_w", p_cw)
    done = update("ffn_conv_w", p_fw)
    p_in, p_meta, p_nm = _send_wait("exchange_in_wait", ex_in, (0, 1, 2), done)
    update("w_in", p_in)
    update("meta_tokens", p_meta)
    update("norm_mix_g", p_nm)

    grads = [shard_out[n][0] for n in names]
    deltas = [shard_out[n][1] for n in names]
    new_m = [shard_out[n][2] for n in names]
    new_v = [shard_out[n][3] for n in names]
    return (loss, grad_x, *grads, *deltas, *new_m, *new_v)
```

```python
import math

import jax
import jax.numpy as jnp
from jax import lax
from jax.experimental import pallas as pl
from jax.experimental.pallas import tpu as pltpu

F32 = jnp.float32
BF16 = jnp.bfloat16
MESH = pl.DeviceIdType.MESH

N_DEV = 8
RMS_EPS = 1e-6
ADAM_LR = 0.001
ADAM_B1 = 0.9
ADAM_B2 = 0.999
ADAM_EPS = 1e-08
ADAM_WD = 0.01
ADAM_STEP = 10
ADAM_BC1 = 1.0 - ADAM_B1 ** ADAM_STEP
ADAM_BC2 = 1.0 - ADAM_B2 ** ADAM_STEP

SUBLANES = 8
LANES = 128
ROW_TILE = 128
ROW_CHUNK = 32
WIDE_TILES = (544, 256, 128)
FFN_COLS = 256
STAGE_COLS = 256
OUT_TILE = 512
N_SEG = 8
HALO_ROWS = 16
SSM_CHUNK = 128
SCAN_UNROLL = 8
VMEM_LIMIT = 48 * 1024 * 1024

NN = ((1,), (0,))
NT = ((1,), (1,))
TN = ((0,), (0,))


def _params(**kw):
    return pltpu.CompilerParams(vmem_limit_bytes=VMEM_LIMIT, **kw)


def _dot(a, b, dims):
    return lax.dot_general(a, b, (dims, ((), ())), preferred_element_type=F32)


def _mean_sq_rsqrt(x):
    return lax.rsqrt(jnp.mean(x * x, axis=-1, keepdims=True) + RMS_EPS)


def _rms_bwd(x, r, g, dy):
    xhat = x * r
    dxh = dy * g
    dx = r * (dxh - xhat * jnp.mean(dxh * xhat, axis=-1, keepdims=True))
    return dx, dy * xhat


def _sigmoid(x):
    return 0.5 * jnp.tanh(0.5 * x) + 0.5


def _gelu(y):
    c = math.sqrt(2.0 / math.pi)
    t = jnp.tanh(c * (y + 0.044715 * y * y * y))
    return 0.5 * y * (1.0 + t), t


def _gelu_grad(y, t):
    c = math.sqrt(2.0 / math.pi)
    return 0.5 * (1.0 + t) + 0.5 * y * (1.0 - t * t) * c * (1.0 + 3.0 * 0.044715 * y * y)


def _wrap_prev_halo(halo, first):
    seg = lax.broadcasted_iota(jnp.int32, halo.shape, 0) % N_SEG
    wrapped = jnp.where(seg == 0, 0.0, pltpu.roll(halo, 1, 0))
    return jnp.where(first, wrapped, halo)


def _wrap_next_halo(halo, last):
    seg = lax.broadcasted_iota(jnp.int32, halo.shape, 0) % N_SEG
    wrapped = jnp.where(seg == N_SEG - 1, 0.0, pltpu.roll(halo, halo.shape[0] - 1, 0))
    return jnp.where(last, wrapped, halo)


def _dev_index(p):
    return 4 * p[0] + 2 * p[1] + p[2]


def _allgather(name, shards, deps=()):
    n = len(shards)

    def body(*refs):
        ins, outs = refs[:n], refs[n:2 * n]
        send_sems, recv_sems, local_sems = refs[2 * n:]
        x, y, c = lax.axis_index("x"), lax.axis_index("y"), lax.axis_index("c")
        me, sibling = (x, y, c), (x, y, 1 - c)
        chips = [(1 - x, y), (x, 1 - y), (1 - x, 1 - y)]

        def copy(a, k, block, to, src=None):
            dst = outs[a].at[_dev_index(block)]
            return pltpu.make_async_remote_copy(
                src_ref=dst if src is None else src, dst_ref=dst,
                send_sem=send_sems.at[a, k], recv_sem=recv_sems.at[a, k],
                device_id=to, device_id_type=MESH)

        mine = [pltpu.make_async_copy(ins[a], outs[a].at[_dev_index(me)], local_sems.at[a])
                for a in range(n)]
        for cp in mine:
            cp.start()
        first = []
        for a in range(n):
            first.append(copy(a, 0, me, sibling, src=ins[a]))
            for j, chip in enumerate(chips):
                first.append(copy(a, 1 + j, me, (*chip, c), src=ins[a]))
        for cp in first:
            cp.start()
        passed = []
        for j, chip in enumerate(chips):
            for a in range(n):
                copy(a, 1 + j, (*chip, c), me).wait_recv()
                fwd = copy(a, 4 + j, (*chip, c), sibling)
                fwd.start()
                passed.append(fwd)
        for a in range(n):
            copy(a, 0, sibling, me).wait_recv()
            for j, chip in enumerate(chips):
                copy(a, 4 + j, (*chip, 1 - c), me).wait_recv()
        for cp in first + passed:
            cp.wait_send()
        for cp in mine:
            cp.wait()

    any_spec = pl.BlockSpec(memory_space=pl.ANY)
    return pl.pallas_call(
        _ignoring_deps(body, n, deps), name=name,
        out_shape=[jax.ShapeDtypeStruct((N_DEV,) + s.shape, s.dtype) for s in shards],
        in_specs=[any_spec] * (n + len(deps)), out_specs=[any_spec] * n,
        scratch_shapes=[pltpu.SemaphoreType.DMA((n, 7)), pltpu.SemaphoreType.DMA((n, 7)),
                        pltpu.SemaphoreType.DMA((n,))],
    )(*shards, *deps)


HBM_SPEC = pl.BlockSpec(memory_space=pltpu.HBM)
SEM_SPEC = pl.BlockSpec(memory_space=pltpu.SEMAPHORE)
ANY_SPEC = pl.BlockSpec(memory_space=pl.ANY)
DATAFLOW = pltpu.SideEffectType.DATAFLOW_SIDE_EFFECTING


def _ignoring_deps(body, n_in, deps):
    n_dep = len(deps)

    def wrapped(*refs):
        return body(*refs[:n_in], *refs[n_in + n_dep:])

    return wrapped


def _my_position():
    x, y, c = lax.axis_index("x"), lax.axis_index("y"), lax.axis_index("c")
    return (x, y, c)


def _peer(me, k):
    return tuple((1 - v) if (k >> s) & 1 else v for v, s in zip(me, (2, 1, 0)))


def _split_copies(src_refs, land_refs, send_sems, recv_sems, gather):
    me = _my_position()
    copies = []
    for a, (src, land) in enumerate(zip(src_refs, land_refs)):
        for k in range(1, N_DEV):
            peer = _peer(me, k)
            copies.append(pltpu.make_async_remote_copy(
                src_ref=src if gather[a] else src.at[_dev_index(peer)], dst_ref=land.at[_dev_index(me)],
                send_sem=send_sems[a].at[k - 1], recv_sem=recv_sems[a].at[k - 1],
                device_id=peer, device_id_type=MESH))
    return copies


def _own_slot(block, like_shape):
    me = _dev_index(_my_position())
    return lax.dynamic_update_index_in_dim(lax.empty(like_shape, block.dtype), block, me, 0)


def _send_start(name, srcs, gather):
    n = len(srcs)
    me = _dev_index(_my_position())
    gather = [gather] * n if isinstance(gather, bool) else list(gather)
    lands = [_own_slot(s, (N_DEV,) + s.shape) if g else
             _own_slot(lax.dynamic_index_in_dim(s, me, 0, keepdims=False), s.shape)
             for s, g in zip(srcs, gather)]

    def body(*refs):
        src_refs, land_refs = refs[:n], refs[n:2 * n]
        send_sems, recv_sems = refs[2 * n:3 * n], refs[3 * n:4 * n]
        token = refs[-1]
        for cp in _split_copies(src_refs, land_refs, send_sems, recv_sems, gather):
            cp.start()
        token[...] = jnp.zeros_like(token)

    hbm = lambda a: pltpu.HBM(a.shape, a.dtype)
    sems = [pltpu.SemaphoreType.DMA((N_DEV - 1,))] * n
    outs = pl.pallas_call(
        body, name=name,
        out_shape=(*sems, *sems, *[hbm(s) for s in srcs], *[hbm(l) for l in lands],
                   jax.ShapeDtypeStruct((SUBLANES, LANES), F32)),
        in_specs=[HBM_SPEC] * (2 * n),
        out_specs=(*[SEM_SPEC] * (2 * n), *[HBM_SPEC] * (2 * n), pl.BlockSpec(memory_space=pltpu.VMEM)),
        input_output_aliases={i: 2 * n + i for i in range(2 * n)},
        compiler_params=pltpu.CompilerParams(has_side_effects=DATAFLOW),
    )(*[pltpu.with_memory_space_constraint(a, pltpu.HBM) for a in (*srcs, *lands)])
    state = dict(send=outs[:n], recv=outs[n:2 * n], srcs=outs[2 * n:3 * n], lands=outs[3 * n:4 * n],
                 gather=gather)
    return state, outs[-1]


def _send_wait(name, state, which, after):
    n = len(which)
    pick = lambda key: [state[key][i] for i in which]
    gather = pick("gather")

    def body(*refs):
        src_refs, land_refs = refs[:n], refs[n:2 * n]
        send_sems, recv_sems = refs[2 * n:3 * n], refs[3 * n:4 * n]
        for cp in _split_copies(src_refs, land_refs, send_sems, recv_sems, gather):
            cp.wait_send()
            cp.wait_recv()

    srcs, lands = pick("srcs"), pick("lands")
    hbm = lambda a: pltpu.HBM(a.shape, a.dtype)
    outs = pl.pallas_call(
        body, name=name,
        out_shape=(*[hbm(s) for s in srcs], *[hbm(l) for l in lands]),
        in_specs=[*[HBM_SPEC] * (2 * n), *[SEM_SPEC] * (2 * n), ANY_SPEC],
        out_specs=tuple([HBM_SPEC] * (2 * n)),
        input_output_aliases={i: i for i in range(2 * n)},
        compiler_params=pltpu.CompilerParams(has_side_effects=DATAFLOW),
    )(*srcs, *lands, *pick("send"), *pick("recv"), after)
    return outs[n:]


def _two_level_copies(src_refs, land_refs, sems1, sems2):
    x, y, c = _my_position()
    me, sibling = (x, y, c), (x, y, 1 - c)
    chips = [(1 - x, y), (x, 1 - y), (1 - x, 1 - y)]
    stage1, stage2 = [], []
    for a, land in enumerate(land_refs):
        def copy(src, block, to, sems, k):
            return pltpu.make_async_remote_copy(
                src_ref=src, dst_ref=land.at[_dev_index(block)], send_sem=sems[0][a].at[k],
                recv_sem=sems[1][a].at[k], device_id=to, device_id_type=MESH)
        src = src_refs[a] if src_refs is not None else land.at[_dev_index(me)]
        stage1.append([copy(src, me, sibling, sems1, 0)] +
                      [copy(src, me, (*chip, c), sems1, 1 + j) for j, chip in enumerate(chips)])
        if sems2 is not None:
            stage2.append([copy(land.at[_dev_index((*chip, c))], (*chip, c), sibling, sems2, j)
                           for j, chip in enumerate(chips)])
    return stage1, stage2


def _gather2_start(name, shards):
    n = len(shards)
    lands = [_own_slot(s, (N_DEV,) + s.shape) for s in shards]

    def body(*refs):
        src_refs, land_refs = refs[:n], refs[n:2 * n]
        sems1 = (refs[2 * n:3 * n], refs[3 * n:4 * n])
        stage1, _ = _two_level_copies(src_refs, land_refs, sems1, None)
        for copies in stage1:
            for cp in copies:
                cp.start()
        refs[-1][...] = jnp.zeros_like(refs[-1])

    hbm = lambda a: pltpu.HBM(a.shape, a.dtype)
    sems = [pltpu.SemaphoreType.DMA((4,))] * n
    outs = pl.pallas_call(
        body, name=name,
        out_shape=(*sems, *sems, *[hbm(s) for s in shards], *[hbm(l) for l in lands],
                   jax.ShapeDtypeStruct((SUBLANES, LANES), F32)),
        in_specs=[HBM_SPEC] * (2 * n),
        out_specs=(*[SEM_SPEC] * (2 * n), *[HBM_SPEC] * (2 * n), pl.BlockSpec(memory_space=pltpu.VMEM)),
        input_output_aliases={i: 2 * n + i for i in range(2 * n)},
        compiler_params=pltpu.CompilerParams(has_side_effects=DATAFLOW),
    )(*[pltpu.with_memory_space_constraint(a, pltpu.HBM) for a in (*shards, *lands)])
    state = dict(send1=list(outs[:n]), recv1=list(outs[n:2 * n]), srcs=list(outs[2 * n:3 * n]),
                 lands=list(outs[3 * n:4 * n]), send2={}, recv2={})
    return state, outs[-1]


def _gather2_forward(name, state, which, after):
    n = len(which)
    pick = lambda key: [state[key][i] for i in which]

    def body(*refs):
        land_refs, recv1 = refs[:n], refs[n:2 * n]
        outs = refs[2 * n + len(after):]
        sems2 = (outs[:n], outs[n:2 * n])
        stage1, stage2 = _two_level_copies(None, land_refs, (recv1, recv1), sems2)
        for a in range(n):
            for j in range(3):
                stage1[a][1 + j].wait_recv()
                stage2[a][j].start()
        outs[-1][...] = jnp.zeros_like(outs[-1])

    lands = pick("lands")
    sems = [pltpu.SemaphoreType.DMA((3,))] * n
    outs = pl.pallas_call(
        body, name=name,
        out_shape=(*sems, *sems, *[pltpu.HBM(l.shape, l.dtype) for l in lands],
                   jax.ShapeDtypeStruct((SUBLANES, LANES), F32)),
        in_specs=[*[HBM_SPEC] * n, *[SEM_SPEC] * n, *[ANY_SPEC] * len(after)],
        out_specs=(*[SEM_SPEC] * (2 * n), *[HBM_SPEC] * n, pl.BlockSpec(memory_space=pltpu.VMEM)),
        input_output_aliases={i: 2 * n + i for i in range(n)},
        compiler_params=pltpu.CompilerParams(has_side_effects=DATAFLOW),
    )(*lands, *pick("recv1"), *after)
    for idx, i in enumerate(which):
        state["send2"][i], state["recv2"][i] = outs[idx], outs[n + idx]
        state["lands"][i] = outs[2 * n + idx]
    return outs[-1]


def _gather2_wait(name, state, which, after):
    n = len(which)
    pick = lambda key: [state[key][i] for i in which]

    def body(*refs):
        src_refs, land_refs = refs[:n], refs[n:2 * n]
        sems1 = (refs[2 * n:3 * n], refs[3 * n:4 * n])
        sems2 = (refs[4 * n:5 * n], refs[5 * n:6 * n])
        stage1, stage2 = _two_level_copies(src_refs, land_refs, sems1, sems2)
        for a in range(n):
            for cp in stage1[a]:
                cp.wait_send()
            stage1[a][0].wait_recv()
            for cp in stage2[a]:
                cp.wait_send()
                cp.wait_recv()

    srcs, lands = pick("srcs"), pick("lands")
    hbm = lambda a: pltpu.HBM(a.shape, a.dtype)
    outs = pl.pallas_call(
        body, name=name,
        out_shape=(*[hbm(s) for s in srcs], *[hbm(l) for l in lands]),
        in_specs=[*[HBM_SPEC] * (2 * n), *[SEM_SPEC] * (4 * n), ANY_SPEC],
        out_specs=tuple([HBM_SPEC] * (2 * n)),
        input_output_aliases={i: i for i in range(2 * n)},
        compiler_params=pltpu.CompilerParams(has_side_effects=DATAFLOW),
    )(*srcs, *lands, *pick("send1"), *pick("recv1"), *pick("send2"), *pick("recv2"), after)
    return outs[n:]


def _mm(name, a, b, *, dims, grid, a_spec, b_spec, o_spec, out_shape, acc_shape=None,
        res=None, res_spec=None):
    n_red = grid[-1] if acc_shape is not None else 1
    red_axis = len(grid) - 1

    def body(*refs):
        a_ref, b_ref = refs[0], refs[1]
        r_ref = refs[2] if res is not None else None
        o_ref = refs[3] if res is not None else refs[2]
        part = _dot(a_ref[...], b_ref[...], dims)
        if acc_shape is None:
            if r_ref is not None:
                part = part + r_ref[...]
            o_ref[...] = part.astype(o_ref.dtype)
            return
        acc_ref = refs[-1]
        k = pl.program_id(red_axis)

        @pl.when(k == 0)
        def _():
            acc_ref[...] = part

        @pl.when(k > 0)
        def _():
            acc_ref[...] += part

        @pl.when(k == n_red - 1)
        def _():
            total = acc_ref[...]
            if r_ref is not None:
                total = total + r_ref[...]
            o_ref[...] = total.astype(o_ref.dtype)

    ins, in_specs = [a, b], [a_spec, b_spec]
    if res is not None:
        ins.append(res)
        in_specs.append(res_spec)
    return pl.pallas_call(
        body, name=name, grid=grid, in_specs=in_specs, out_specs=o_spec, out_shape=out_shape,
        scratch_shapes=[pltpu.VMEM(acc_shape, F32)] if acc_shape is not None else [],
        compiler_params=_params(),
    )(*ins)


def _wide_tile(t):
    return next(c for c in WIDE_TILES if t % c == 0)


def _rms_fwd(name, h, g):
    t, d = h.shape
    tile = _wide_tile(t)

    def body(h_ref, g_ref, o_ref):
        x = h_ref[...]
        o_ref[...] = (x * _mean_sq_rsqrt(x) * g_ref[...]).astype(BF16)

    return pl.pallas_call(
        body, name=name, grid=(t // tile,),
        in_specs=[pl.BlockSpec((tile, d), lambda i: (i, 0)), pl.BlockSpec((1, d), lambda i: (0, 0))],
        out_specs=pl.BlockSpec((tile, d), lambda i: (i, 0)),
        out_shape=jax.ShapeDtypeStruct((t, d), BF16), compiler_params=_params(),
    )(h, g)


def _fused_tile(t):
    half = _wide_tile(t) // 2
    return half if half % 16 == 0 and t % half == 0 else _wide_tile(t)


def _dgrad_rms_bwd(name, dres, dy, w, dims, h, g, deps=()):
    t, d = h.shape
    k = dy.shape[1]

    def body(dres_ref, dy_ref, w_ref, h_ref, g_ref, dh_ref, dhb_ref, dg_ref):
        x = h_ref[...]
        dhn = _dot(dy_ref[...], w_ref[...], dims)
        dx, dgt = _rms_bwd(x, _mean_sq_rsqrt(x), g_ref[...], dhn)
        dh = dres_ref[...] + dx
        dh_ref[...] = dh
        dhb_ref[...] = dh.astype(BF16)

        @pl.when(pl.program_id(0) == 0)
        def _():
            dg_ref[...] = jnp.zeros_like(dg_ref)

        dg_ref[...] += jnp.sum(dgt, axis=0, keepdims=True)

    tile = _fused_tile(t)
    row = pl.BlockSpec((tile, d), lambda i: (i, 0))
    vec = pl.BlockSpec((1, d), lambda i: (0, 0))
    return pl.pallas_call(
        _ignoring_deps(body, 5, deps), name=name, grid=(t // tile,),
        in_specs=[row, pl.BlockSpec((tile, k), lambda i: (i, 0)), pl.BlockSpec(w.shape, lambda i: (0, 0)), row,
                  vec] + [ANY_SPEC] * len(deps),
        out_specs=[row, row, vec],
        out_shape=[jax.ShapeDtypeStruct((t, d), F32), jax.ShapeDtypeStruct((t, d), BF16),
                   jax.ShapeDtypeStruct((1, d), F32)],
        compiler_params=_params(),
    )(dres, dy, w, h, g, *deps)


def _down_loss_bwd(act, w_down, h1, target, g, row_lo, row_hi):
    t, d = h1.shape
    f = act.shape[1]
    tile = _fused_tile(t)

    def body(act_ref, w_ref, h_ref, tg_ref, g_ref, dh_ref, dhb_ref, loss_ref, dg_ref):
        i = pl.program_id(0)
        x = h_ref[...] + _dot(act_ref[...], w_ref[...], NN)
        r = _mean_sq_rsqrt(x)
        gv = g_ref[...]
        y = x * r * gv
        rows = i * tile + lax.broadcasted_iota(jnp.int32, (tile, 1), 0)
        time = (rows % N_SEG) * (t // N_SEG) + rows // N_SEG
        valid = jnp.logical_and(time >= row_lo, time < row_hi)
        err = jnp.where(valid, y - tg_ref[...], 0.0)
        dy = err * (1.0 / d)
        dx, dgt = _rms_bwd(x, r, gv, dy)
        dh_ref[...] = dx
        dhb_ref[...] = dx.astype(BF16)

        @pl.when(i == 0)
        def _():
            loss_ref[...] = jnp.zeros_like(loss_ref)
            dg_ref[...] = jnp.zeros_like(dg_ref)

        row_loss = jnp.mean(err * err, axis=-1, keepdims=True)
        loss_ref[...] += 0.5 * jnp.sum(row_loss, axis=0, keepdims=True)
        dg_ref[...] += jnp.sum(dgt, axis=0, keepdims=True)

    row = pl.BlockSpec((tile, d), lambda i: (i, 0))
    vec = pl.BlockSpec((1, d), lambda i: (0, 0))
    return pl.pallas_call(
        body, name="down_proj_loss_bwd", grid=(t // tile,),
        in_specs=[pl.BlockSpec((tile, f), lambda i: (i, 0)), pl.BlockSpec(w_down.shape, lambda i: (0, 0)),
                  row, row, vec],
        out_specs=[row, row, pl.BlockSpec((1, 1), lambda i: (0, 0)), vec],
        out_shape=[jax.ShapeDtypeStruct((t, d), F32), jax.ShapeDtypeStruct((t, d), BF16),
                   jax.ShapeDtypeStruct((1, 1), F32), jax.ShapeDtypeStruct((1, d), F32)],
        compiler_params=_params(),
    )(act, w_down, h1, target, g)


def _prev_halo(i, t, tile=ROW_TILE):
    return jnp.where(i == 0, t // HALO_ROWS - 1, i * (tile // HALO_ROWS) - 1)


def _next_halo(i, t, tile=ROW_TILE):
    return jnp.where(i == t // tile - 1, 0, (i + 1) * (tile // HALO_ROWS))


def _causal_taps(cur, halo, first):
    rows = cur.shape[0]
    ext = jnp.concatenate([_wrap_prev_halo(halo, first), cur], axis=0)
    return ext[:rows], ext[N_SEG:N_SEG + rows]


def _anticausal_taps(cur, halo, last):
    rows = cur.shape[0]
    ext = jnp.concatenate([cur, _wrap_next_halo(halo, last)], axis=0)
    return ext[N_SEG:N_SEG + rows], ext[2 * N_SEG:2 * N_SEG + rows]


def _mix_fwd(proj, y, w_glu, conv_w, gain_c, gain_s, h0, w_out, gain_ffn):
    t, d = h0.shape
    dc = conv_w.shape[1]
    ds = y.shape[1]

    def body(p_ref, halo_ref, y_ref, wg_ref, cw_ref, gc_ref, gs_ref, h0_ref, wo_ref, gf_ref,
             mixed_ref, z_ref, h1_ref, hn_ref):
        i = pl.program_id(0)
        p = p_ref[...]
        b, c, v = p[:, :dc], p[:, dc:2 * dc], p[:, 2 * dc:3 * dc]
        cv = c * v
        hp = halo_ref[...]
        x2, x1 = _causal_taps(cv, hp[:, dc:2 * dc] * hp[:, 2 * dc:3 * dc], i == 0)
        cw = cw_ref[...]
        conv = cw[0:1] * x2 + cw[1:2] * x1 + cw[2:3] * cv
        co = b * conv
        mixed_ref[:, :dc] = (co * _mean_sq_rsqrt(co) * gc_ref[...]).astype(BF16)
        g, _ = _gelu(y_ref[...])
        z = _dot(g.astype(BF16), wg_ref[...], NN)
        z_ref[...] = z
        so = g * _sigmoid(z)
        mixed_ref[:, dc:] = (so * _mean_sq_rsqrt(so) * gs_ref[...]).astype(BF16)
        h1 = h0_ref[...] + _dot(mixed_ref[...], wo_ref[...], NN)
        h1_ref[...] = h1
        hn_ref[...] = (h1 * _mean_sq_rsqrt(h1) * gf_ref[...]).astype(BF16)

    const = lambda i: (0, 0)
    tile = _fused_tile(t)
    row = lambda w: pl.BlockSpec((tile, w), lambda i: (i, 0))
    return pl.pallas_call(
        body, name="mix_fwd_out_proj", grid=(t // tile,),
        in_specs=[row(3 * dc), pl.BlockSpec((HALO_ROWS, 3 * dc), lambda i: (_prev_halo(i, t, tile), 0)), row(ds),
                  pl.BlockSpec((ds, ds), const), pl.BlockSpec(conv_w.shape, const),
                  pl.BlockSpec((1, dc), const), pl.BlockSpec((1, ds), const),
                  row(d), pl.BlockSpec(w_out.shape, const), pl.BlockSpec((1, d), const)],
        out_specs=[row(dc + ds), row(ds), row(d), row(d)],
        out_shape=[jax.ShapeDtypeStruct((t, dc + ds), BF16), jax.ShapeDtypeStruct((t, ds), F32),
                   jax.ShapeDtypeStruct((t, d), F32), jax.ShapeDtypeStruct((t, d), BF16)],
        compiler_params=_params(),
    )(proj, proj, y, w_glu, conv_w, gain_c, gain_s, h0, w_out, gain_ffn)


def _mix_bwd1(proj, y, z, dmixed, w_glu, conv_w, gain_c, gain_s):
    t = proj.shape[0]
    dc = conv_w.shape[1]
    ds = y.shape[1]

    def body(p_ref, halo_ref, y_ref, z_ref, dm_ref, wg_ref, cw_ref, gc_ref, gs_ref,
             db_ref, dconv_ref, dy_ref, dwg_ref, dcw_ref, dgc_ref, dgs_ref):
        i = pl.program_id(0)

        @pl.when(i == 0)
        def _():
            dwg_ref[...] = jnp.zeros_like(dwg_ref)
            dcw_ref[...] = jnp.zeros_like(dcw_ref)
            dgc_ref[...] = jnp.zeros_like(dgc_ref)
            dgs_ref[...] = jnp.zeros_like(dgs_ref)

        p = p_ref[...]
        b, c, v = p[:, :dc], p[:, dc:2 * dc], p[:, 2 * dc:3 * dc]
        cv = c * v
        hp = halo_ref[...]
        x2, x1 = _causal_taps(cv, hp[:, dc:2 * dc] * hp[:, 2 * dc:3 * dc], i == 0)
        cw = cw_ref[...]
        conv = cw[0:1] * x2 + cw[1:2] * x1 + cw[2:3] * cv
        co = b * conv
        dm = dm_ref[...]
        dco, dgt = _rms_bwd(co, _mean_sq_rsqrt(co), gc_ref[...], dm[:, :dc])
        dgc_ref[...] += jnp.sum(dgt, axis=0, keepdims=True)
        db_ref[...] = (dco * conv).astype(BF16)
        dconv = dco * b
        dconv_ref[...] = dconv
        dcw_ref[0:1, :] += jnp.sum(dconv * x2, axis=0, keepdims=True)
        dcw_ref[1:2, :] += jnp.sum(dconv * x1, axis=0, keepdims=True)
        dcw_ref[2:3, :] += jnp.sum(dconv * cv, axis=0, keepdims=True)

        yv = y_ref[...]
        g, th = _gelu(yv)
        sg = _sigmoid(z_ref[...])
        so = g * sg
        dso, dgt = _rms_bwd(so, _mean_sq_rsqrt(so), gs_ref[...], dm[:, dc:])
        dgs_ref[...] += jnp.sum(dgt, axis=0, keepdims=True)
        dz = (dso * g * sg * (1.0 - sg)).astype(BF16)
        dg = dso * sg + _dot(dz, wg_ref[...], NT)
        dwg_ref[...] += _dot(g.astype(BF16), dz, TN)
        dy_ref[...] = dg * _gelu_grad(yv, th)

    const = lambda i: (0, 0)
    tile = _fused_tile(t)
    row = lambda w: pl.BlockSpec((tile, w), lambda i: (i, 0))
    return pl.pallas_call(
        body, name="mix_bwd1", grid=(t // tile,),
        in_specs=[row(3 * dc), pl.BlockSpec((HALO_ROWS, 3 * dc), lambda i: (_prev_halo(i, t, tile), 0)),
                  row(ds), row(ds), row(dc + ds), pl.BlockSpec((ds, ds), const),
                  pl.BlockSpec(conv_w.shape, const), pl.BlockSpec((1, dc), const),
                  pl.BlockSpec((1, ds), const)],
        out_specs=[row(dc), row(dc), row(ds), pl.BlockSpec((ds, ds), const),
                   pl.BlockSpec(conv_w.shape, const), pl.BlockSpec((1, dc), const),
                   pl.BlockSpec((1, ds), const)],
        out_shape=[jax.ShapeDtypeStruct((t, dc), BF16), jax.ShapeDtypeStruct((t, dc), F32),
                   jax.ShapeDtypeStruct((t, ds), F32), jax.ShapeDtypeStruct((ds, ds), F32),
                   jax.ShapeDtypeStruct(conv_w.shape, F32), jax.ShapeDtypeStruct((1, dc), F32),
                   jax.ShapeDtypeStruct((1, ds), F32)],
        compiler_params=_params(),
    )(proj, proj, y, z, dmixed, w_glu, conv_w, gain_c, gain_s)


def _mix_bwd2(proj, dconv, conv_w, deps=()):
    t = proj.shape[0]
    dc = conv_w.shape[1]
    tile = _wide_tile(t)
    n_tiles = t // tile

    def body(c_ref, v_ref, d_ref, halo_ref, cw_ref, o_ref):
        i = pl.program_id(0)
        d = d_ref[...]
        u1, u2 = _anticausal_taps(d, halo_ref[...], i == n_tiles - 1)
        cw = cw_ref[...]
        dcv = cw[2:3] * d + cw[1:2] * u1 + cw[0:1] * u2
        o_ref[:, :dc] = (dcv * v_ref[...]).astype(BF16)
        o_ref[:, dc:] = (dcv * c_ref[...]).astype(BF16)

    return pl.pallas_call(
        _ignoring_deps(body, 5, deps), name="mix_bwd2", grid=(n_tiles,),
        in_specs=[pl.BlockSpec((tile, dc), lambda i: (i, 1)),
                  pl.BlockSpec((tile, dc), lambda i: (i, 2)),
                  pl.BlockSpec((tile, dc), lambda i: (i, 0)),
                  pl.BlockSpec((HALO_ROWS, dc), lambda i: (_next_halo(i, t, tile), 0)),
                  pl.BlockSpec(conv_w.shape, lambda i: (0, 0))] + [ANY_SPEC] * len(deps),
        out_specs=pl.BlockSpec((tile, 2 * dc), lambda i: (i, 0)),
        out_shape=jax.ShapeDtypeStruct((t, 2 * dc), BF16), compiler_params=_params(),
    )(proj, proj, dconv, dconv, conv_w, *deps)


def _stage_rows(stage_ref, row0, src, wrap=None):
    n, width = src.shape
    for c in range(0, width, STAGE_COLS):
        v = src[:, pl.ds(c, STAGE_COLS)].astype(F32)
        stage_ref[pl.ds(row0, n), pl.ds(c, STAGE_COLS)] = v if wrap is None else wrap(v)


def _conv_taps(stage_ref, fw_ref, fb_ref, c0, r, rows=ROW_CHUNK):
    cols = pl.ds(c0, FFN_COLS)
    x0, x1, x2 = (stage_ref[pl.ds(HALO_ROWS + r - k * N_SEG, rows), cols] for k in range(3))
    w = fw_ref[:, cols]
    return w[0:1] * x2 + w[1:2] * x1 + w[2:3] * x0 + fb_ref[:, cols], x2, x1, x0


def _gated_fwd(up_pre, fw, fb):
    t, f2 = up_pre.shape
    f = f2 // 2

    def body(up_ref, halo_ref, fw_ref, fb_ref, act_ref, stage_ref):
        first = pl.program_id(0) == 0
        _stage_rows(stage_ref, 0, halo_ref, lambda v: _wrap_prev_halo(v, first))
        _stage_rows(stage_ref, HALO_ROWS, up_ref)
        for c0 in range(0, f, FFN_COLS):
            for r in range(0, ROW_TILE, ROW_CHUNK):
                a = _conv_taps(stage_ref, fw_ref, fb_ref, c0, r)[0]
                val = _conv_taps(stage_ref, fw_ref, fb_ref, f + c0, r)[0]
                act_ref[pl.ds(r, ROW_CHUNK), pl.ds(c0, FFN_COLS)] = (a * _sigmoid(a) * val).astype(BF16)

    const = lambda a: pl.BlockSpec(a.shape, lambda i: (0, 0))
    return pl.pallas_call(
        body, name="ffn_fwd", grid=(t // ROW_TILE,),
        in_specs=[pl.BlockSpec((ROW_TILE, f2), lambda i: (i, 0)),
                  pl.BlockSpec((HALO_ROWS, f2), lambda i: (_prev_halo(i, t), 0)), const(fw), const(fb)],
        out_specs=pl.BlockSpec((ROW_TILE, f), lambda i: (i, 0)),
        out_shape=jax.ShapeDtypeStruct((t, f), BF16),
        scratch_shapes=[pltpu.VMEM((HALO_ROWS + ROW_TILE, f2), F32)], compiler_params=_params(),
    )(up_pre, up_pre, fw, fb)


def _gated_bwd(up_pre, dact, fw, fb, deps=()):
    t, f2 = up_pre.shape
    f = f2 // 2
    n_tiles = t // ROW_TILE
    chunks = [(r, ROW_CHUNK) for r in range(0, ROW_TILE, ROW_CHUNK)] + [(ROW_TILE, HALO_ROWS)]

    def body(up_ref, prev_ref, next_ref, dact_in_ref, dact_next_ref, fw_ref, fb_ref,
             out_ref, dfw_ref, dfb_ref, dup_ref, stage_ref, dact_ref):
        first = pl.program_id(0) == 0
        last = pl.program_id(0) == n_tiles - 1

        @pl.when(first)
        def _():
            dfw_ref[...] = jnp.zeros_like(dfw_ref)
            dfb_ref[...] = jnp.zeros_like(dfb_ref)

        _stage_rows(dact_ref, 0, dact_in_ref)
        _stage_rows(dact_ref, ROW_TILE, dact_next_ref, lambda v: _wrap_next_halo(v, last))
        _stage_rows(stage_ref, 0, prev_ref, lambda v: _wrap_prev_halo(v, first))
        _stage_rows(stage_ref, HALO_ROWS, up_ref)
        _stage_rows(stage_ref, HALO_ROWS + ROW_TILE, next_ref, lambda v: _wrap_next_halo(v, last))
        fold = lambda v: sum(v[s:s + SUBLANES] for s in range(0, ROW_CHUNK, SUBLANES))
        for c0 in range(0, f, FFN_COLS):
            starts = (c0, f + c0)
            sums = [[jnp.zeros((SUBLANES, FFN_COLS), F32)] * 4 for _ in starts]
            for r, rows in chunks:
                taps = [_conv_taps(stage_ref, fw_ref, fb_ref, c, r, rows) for c in starts]
                a, val = taps[0][0], taps[1][0]
                da_ct = dact_ref[pl.ds(r, rows), pl.ds(c0, FFN_COLS)]
                sg = _sigmoid(a)
                gated = da_ct * sg
                a_sg = a * sg
                dup = (gated * val * (1.0 + a - a_sg), gated * a)
                for k in range(2):
                    dup_ref[k, pl.ds(r, rows), :] = dup[k]
                    if r < ROW_TILE:
                        terms = (dup[k], dup[k] * taps[k][1], dup[k] * taps[k][2], dup[k] * taps[k][3])
                        sums[k] = [s + fold(v) for s, v in zip(sums[k], terms)]
            for k, c in enumerate(starts):
                cols = pl.ds(c, FFN_COLS)
                s_b, s_w0, s_w1, s_w2 = (jnp.sum(p, axis=0, keepdims=True) for p in sums[k])
                dfb_ref[:, cols] += s_b
                for tap, s_w in enumerate((s_w0, s_w1, s_w2)):
                    dfw_ref[tap:tap + 1, cols] += s_w
                cw = fw_ref[:, cols]
                for r in range(0, ROW_TILE, ROW_CHUNK):
                    d, u1, u2 = (dup_ref[k, pl.ds(r + s * N_SEG, ROW_CHUNK), :] for s in range(3))
                    out_ref[pl.ds(r, ROW_CHUNK), cols] = (cw[2:3] * d + cw[1:2] * u1 + cw[0:1] * u2).astype(BF16)

    tile = lambda w: pl.BlockSpec((ROW_TILE, w), lambda i: (i, 0))
    halo = lambda w, index: pl.BlockSpec((HALO_ROWS, w), lambda i: (index(i, t), 0))
    const = lambda a: pl.BlockSpec(a.shape, lambda i: (0, 0))
    return pl.pallas_call(
        _ignoring_deps(body, 7, deps), name="ffn_bwd", grid=(n_tiles,),
        in_specs=[tile(f2), halo(f2, _prev_halo), halo(f2, _next_halo), tile(f), halo(f, _next_halo),
                  const(fw), const(fb)] + [ANY_SPEC] * len(deps),
        out_specs=[tile(f2), const(fw), const(fb)],
        out_shape=[jax.ShapeDtypeStruct((t, f2), BF16), jax.ShapeDtypeStruct(fw.shape, F32),
                   jax.ShapeDtypeStruct(fb.shape, F32)],
        scratch_shapes=[pltpu.VMEM((2, ROW_TILE + HALO_ROWS, FFN_COLS), F32),
                        pltpu.VMEM((HALO_ROWS + ROW_TILE + HALO_ROWS, f2), F32),
                        pltpu.VMEM((ROW_TILE + HALO_ROWS, f), F32)],
        compiler_params=_params(),
    )(up_pre, up_pre, up_pre, dact, dact, fw, fb, *deps)


def _to_segments(a):
    t, c = a.shape
    return a.reshape(N_SEG, t // N_SEG, c).transpose(1, 0, 2).reshape(t, c)


def _from_segments(a):
    t, c = a.shape
    return a.reshape(t // N_SEG, N_SEG, c).transpose(1, 0, 2).reshape(t, c)


def _cmul(ar, ai, br, bi):
    return ar * br - ai * bi, ar * bi + ai * br


def _cpow(ar, ai, n):
    out = None
    while n:
        if n & 1:
            out = (ar, ai) if out is None else _cmul(out[0], out[1], ar, ai)
        ar, ai = _cmul(ar, ai, ar, ai)
        n >>= 1
    return out


def _segment_carries(pr, pi, fr, fi, forward):
    row = lax.broadcasted_iota(jnp.int32, fr.shape, 0)
    edge = row == (0 if forward else N_SEG - 1)
    shift = 1 if forward else N_SEG - 1
    sr, si = jnp.zeros_like(fr), jnp.zeros_like(fi)
    for _ in range(N_SEG - 1):
        tr, ti = _cmul(pr, pi, sr, si)
        sr = jnp.where(edge, 0.0, pltpu.roll(tr + fr, shift, 0))
        si = jnp.where(edge, 0.0, pltpu.roll(ti + fi, shift, 0))
    return sr, si


def _rows(i):
    if isinstance(i, int):
        return pl.ds(i * SUBLANES, SUBLANES)
    return pl.ds(pl.multiple_of(i * SUBLANES, SUBLANES), SUBLANES)


def _scan_steps(n, body, init):
    def trip(k, carry):
        for s in range(SCAN_UNROLL):
            carry = body(k * SCAN_UNROLL + s, carry)
        return carry

    carry = lax.fori_loop(0, n // SCAN_UNROLL, trip, init)
    for i in range(n - n % SCAN_UNROLL, n):
        carry = body(i, carry)
    return carry


def _s5_fwd(proj, u_col, bb_re, bb_im, a_re, a_im, cc_re, cc_im, d_skip):
    t = proj.shape[0]
    nch, _, cs = bb_re.shape
    ds = nch * SSM_CHUNK
    u_blk = u_col // SSM_CHUNK
    steps = t // N_SEG

    def body(u_ref, bbr_ref, bbi_ref, ar_ref, ai_ref, ccr_ref, cci_ref, d_ref, sr_ref, si_ref, y_ref):
        ub = u_ref[...].astype(BF16)
        sr_ref[...] = _dot(ub, bbr_ref[...], NN)
        si_ref[...] = _dot(ub, bbi_ref[...], NN)
        ar = jnp.broadcast_to(ar_ref[...], (N_SEG, cs))
        ai = jnp.broadcast_to(ai_ref[...], (N_SEG, cs))
        zero = jnp.zeros((N_SEG, cs), F32)

        def totals(i, carry):
            tr, ti = _cmul(ar, ai, *carry)
            return tr + sr_ref[_rows(i), :], ti + si_ref[_rows(i), :]

        fr, fi = _scan_steps(steps, totals, (zero, zero))
        s0r, s0i = _segment_carries(*_cpow(ar, ai, steps), fr, fi, True)

        def scan(i, carry):
            tr, ti = _cmul(ar, ai, *carry)
            nr, ni = tr + sr_ref[_rows(i), :], ti + si_ref[_rows(i), :]
            sr_ref[_rows(i), :] = nr
            si_ref[_rows(i), :] = ni
            return nr, ni

        _scan_steps(steps, scan, (s0r, s0i))
        y_ref[...] = (_dot(sr_ref[...].astype(BF16), ccr_ref[...], NT)
                      - _dot(si_ref[...].astype(BF16), cci_ref[...], NT)
                      + d_ref[...] * u_ref[...])

    chunk3 = lambda r, c: pl.BlockSpec((None, r, c), lambda j: (j, 0, 0))
    return pl.pallas_call(
        body, name="s5_fwd", grid=(nch,),
        in_specs=[pl.BlockSpec((t, SSM_CHUNK), lambda j: (0, j + u_blk)),
                  chunk3(SSM_CHUNK, cs), chunk3(SSM_CHUNK, cs), chunk3(1, cs), chunk3(1, cs),
                  chunk3(SSM_CHUNK, cs), chunk3(SSM_CHUNK, cs), chunk3(1, SSM_CHUNK)],
        out_specs=[pl.BlockSpec((t, cs), lambda j: (0, j)), pl.BlockSpec((t, cs), lambda j: (0, j)),
                   pl.BlockSpec((t, SSM_CHUNK), lambda j: (0, j))],
        out_shape=[jax.ShapeDtypeStruct((t, nch * cs), F32), jax.ShapeDtypeStruct((t, nch * cs), F32),
                   jax.ShapeDtypeStruct((t, ds), F32)],
        compiler_params=_params(),
    )(proj, bb_re, bb_im, a_re, a_im, cc_re, cc_im, d_skip)


def _s5_bwd(dy, proj, u_col, s_re, s_im, bb_re, bb_im, a_re, a_im, cc_re, cc_im, d_skip, gpc):
    t, ds = dy.shape
    nch, _, cs = bb_re.shape
    u_blk = u_col // SSM_CHUNK
    steps = t // N_SEG

    def body(dy_ref, u_ref, sr_ref, si_ref, bbr_ref, bbi_ref, ar_ref, ai_ref, ccr_ref, cci_ref, d_ref,
             du_ref, dbbr_ref, dbbi_ref, dar_ref, dai_ref, dccr_ref, dcci_ref, dd_ref, gr_ref, gi_ref):
        dyv = dy_ref[...]
        dyb = dyv.astype(BF16)
        gr_ref[...] = _dot(dyb, ccr_ref[...], NN)
        gi_ref[...] = -_dot(dyb, cci_ref[...], NN)
        ar = jnp.broadcast_to(ar_ref[...], (N_SEG, cs))
        ai = -jnp.broadcast_to(ai_ref[...], (N_SEG, cs))
        zero = jnp.zeros((N_SEG, cs), F32)

        def totals(k, carry):
            i = steps - 1 - k
            tr, ti = _cmul(ar, ai, *carry)
            return tr + gr_ref[_rows(i), :], ti + gi_ref[_rows(i), :]

        fr, fi = _scan_steps(steps, totals, (zero, zero))
        e0r, e0i = _segment_carries(*_cpow(ar, ai, steps), fr, fi, False)

        def step(i, gr, gi, pr, pi, acc_r, acc_i):
            tr, ti = _cmul(ar, ai, gr, gi)
            nr, ni = tr + gr_ref[_rows(i), :], ti + gi_ref[_rows(i), :]
            gr_ref[_rows(i), :] = nr
            gi_ref[_rows(i), :] = ni
            return nr, ni, acc_r + nr * pr + ni * pi, acc_i + ni * pr - nr * pi

        def scan(k, carry):
            i = steps - 1 - k
            gr, gi, acc_r, acc_i = carry
            return step(i, gr, gi, sr_ref[_rows(i - 1), :], si_ref[_rows(i - 1), :], acc_r, acc_i)

        gr, gi, acc_r, acc_i = _scan_steps(steps - 1, scan, (e0r, e0i, zero, zero))
        row = lax.broadcasted_iota(jnp.int32, (N_SEG, cs), 0)
        last = _rows(steps - 1)
        pr = jnp.where(row == 0, 0.0, pltpu.roll(sr_ref[last, :], 1, 0))
        pi = jnp.where(row == 0, 0.0, pltpu.roll(si_ref[last, :], 1, 0))
        _, _, acc_r, acc_i = step(0, gr, gi, pr, pi, acc_r, acc_i)
        dar_ref[...] = jnp.sum(acc_r, axis=0, keepdims=True)
        dai_ref[...] = jnp.sum(acc_i, axis=0, keepdims=True)

        uv = u_ref[...]
        ub = uv.astype(BF16)
        grb = gr_ref[...].astype(BF16)
        gib = gi_ref[...].astype(BF16)
        du = d_ref[...] * dyv + _dot(grb, bbr_ref[...], NT) + _dot(gib, bbi_ref[...], NT)
        du_ref[...] = du.astype(BF16)
        def put_groups(ref, full):
            for gl in range(gpc):
                ref[gl] = full[gl * hb:(gl + 1) * hb, gl * pb:(gl + 1) * pb]

        put_groups(dbbr_ref, _dot(ub, grb, TN))
        put_groups(dbbi_ref, _dot(ub, gib, TN))
        put_groups(dccr_ref, _dot(dyb, sr_ref[...].astype(BF16), TN))
        put_groups(dcci_ref, -_dot(dyb, si_ref[...].astype(BF16), TN))
        dd_ref[...] = jnp.sum(dyv * uv, axis=0, keepdims=True)

    hb, pb = SSM_CHUNK // gpc, cs // gpc
    groups = pl.BlockSpec((None, gpc, hb, pb), lambda j: (j, 0, 0, 0))
    groups_shape = jax.ShapeDtypeStruct((nch, gpc, hb, pb), F32)
    chunk3 = lambda r, c: pl.BlockSpec((None, r, c), lambda j: (j, 0, 0))
    cols = lambda w: pl.BlockSpec((t, w), lambda j: (0, j))
    return pl.pallas_call(
        body, name="s5_bwd", grid=(nch,),
        in_specs=[cols(SSM_CHUNK), pl.BlockSpec((t, SSM_CHUNK), lambda j: (0, j + u_blk)), cols(cs), cols(cs),
                  chunk3(SSM_CHUNK, cs), chunk3(SSM_CHUNK, cs), chunk3(1, cs), chunk3(1, cs),
                  chunk3(SSM_CHUNK, cs), chunk3(SSM_CHUNK, cs), chunk3(1, SSM_CHUNK)],
        out_specs=[cols(SSM_CHUNK), groups, groups, chunk3(1, cs), chunk3(1, cs), groups, groups,
                   chunk3(1, SSM_CHUNK)],
        out_shape=[jax.ShapeDtypeStruct((t, ds), BF16), groups_shape, groups_shape,
                   jax.ShapeDtypeStruct((nch, 1, cs), F32), jax.ShapeDtypeStruct((nch, 1, cs), F32),
                   groups_shape, groups_shape, jax.ShapeDtypeStruct((nch, 1, SSM_CHUNK), F32)],
        scratch_shapes=[pltpu.VMEM((t, cs), F32), pltpu.VMEM((t, cs), F32)],
        compiler_params=_params(),
    )(dy, proj, s_re, s_im, bb_re, bb_im, a_re, a_im, cc_re, cc_im, d_skip)


def _discretize(lr, li, log_dt, br, bi):
    dt = jnp.exp(log_dt)
    mag = jnp.exp(lr * dt)
    ang = li * dt
    a_re = mag * jnp.cos(ang)
    a_im = mag * jnp.sin(ang)
    den = lr * lr + li * li
    nr = a_re - 1.0
    f_re = (nr * lr + a_im * li) / den
    f_im = (a_im * lr - nr * li) / den
    return a_re, a_im, f_re * br - f_im * bi, f_re * bi + f_im * br


def _whole(shape):
    return pl.BlockSpec(shape, lambda: (0,) * len(shape))


def _disc_fwd(lr, li, log_dt, br, bi, cr, ci, nch, gpc, deps=()):
    gh, n_state = br.shape
    grp = gh // (nch * gpc)
    rows, cs = gpc * grp, gpc * n_state

    def body(lr_ref, li_ref, dt_ref, br_ref, bi_ref, cr_ref, ci_ref,
             ar_ref, ai_ref, bbr_ref, bbi_ref, ccr_ref, cci_ref, wide_ref):
        a_re, a_im, bb_re, bb_im = _discretize(lr_ref[...], li_ref[...], dt_ref[...], br_ref[...], bi_ref[...])
        for ref, val in ((ar_ref, a_re), (ai_ref, a_im)):
            for j in range(nch):
                for gl in range(gpc):
                    g0 = (j * gpc + gl) * grp
                    ref[j, :, gl * n_state:(gl + 1) * n_state] = val[g0:g0 + 1]
        for ref, val in ((bbr_ref, bb_re), (bbi_ref, bb_im), (ccr_ref, cr_ref[...]), (cci_ref, ci_ref[...])):
            wide_ref[...] = jnp.zeros_like(wide_ref)
            for j in range(nch):
                for gl in range(gpc):
                    g0 = (j * gpc + gl) * grp
                    wide_ref[j, gl * grp:(gl + 1) * grp, gl * n_state:(gl + 1) * n_state] = val[g0:g0 + grp]
            ref[...] = wide_ref[...].astype(BF16)

    args = (lr, li, log_dt, br, bi, cr, ci)
    row_shape, wide_shape = (nch, 1, cs), (nch, rows, cs)
    outs = [jax.ShapeDtypeStruct(row_shape, F32)] * 2 + [jax.ShapeDtypeStruct(wide_shape, BF16)] * 4
    return pl.pallas_call(
        _ignoring_deps(body, 7, deps), name="disc_fwd",
        in_specs=[_whole(a.shape) for a in args] + [ANY_SPEC] * len(deps),
        out_specs=[_whole(o.shape) for o in outs], out_shape=outs,
        scratch_shapes=[pltpu.VMEM(wide_shape, F32)],
    )(*args, *deps)


def _disc_bwd(lr, li, log_dt, br, bi, dar, dai, dbbr, dbbi):
    def body(lr_ref, li_ref, dt_ref, br_ref, bi_ref, dar_ref, dai_ref, dbbr_ref, dbbi_ref,
             dlr_ref, dli_ref, ddt_ref, dbr_ref, dbi_ref):
        _, vjp = jax.vjp(_discretize, lr_ref[...], li_ref[...], dt_ref[...], br_ref[...], bi_ref[...])
        grads = vjp((dar_ref[...], dai_ref[...], dbbr_ref[...], dbbi_ref[...]))
        for ref, val in zip((dlr_ref, dli_ref, ddt_ref, dbr_ref, dbi_ref), grads):
            ref[...] = val

    args = (lr, li, log_dt, br, bi, dar, dai, dbbr, dbbi)
    outs = (lr, li, log_dt, br, bi)
    return pl.pallas_call(
        body, name="disc_bwd", in_specs=[_whole(a.shape) for a in args],
        out_specs=[_whole(a.shape) for a in outs],
        out_shape=[jax.ShapeDtypeStruct(a.shape, F32) for a in outs],
    )(*args)


def _adamw(w, g, m, v):
    m = ADAM_B1 * m + (1.0 - ADAM_B1) * g
    v = ADAM_B2 * v + (1.0 - ADAM_B2) * (g * g)
    m_hat = m / ADAM_BC1
    v_hat = v / ADAM_BC2
    delta = -ADAM_LR * (m_hat / (jnp.sqrt(v_hat) + ADAM_EPS) + ADAM_WD * w)
    return delta, m, v


def _adamw_reduce(name, parts, w, m, v):
    _, r, c = parts.shape
    tr = r
    for cand in (256, 176, 128):
        if r % cand == 0:
            tr = cand
            break

    def body(p_ref, w_ref, m_ref, v_ref, g_ref, d_ref, nm_ref, nv_ref):
        g = p_ref[0].astype(F32)
        for k in range(1, N_DEV):
            g = g + p_ref[k].astype(F32)
        delta, nm, nv = _adamw(w_ref[...], g, m_ref[...], v_ref[...])
        g_ref[...] = g
        d_ref[...] = delta
        nm_ref[...] = nm
        nv_ref[...] = nv

    blk = pl.BlockSpec((tr, c), lambda i: (i, 0))
    return pl.pallas_call(
        body, name=name, grid=(r // tr,),
        in_specs=[pl.BlockSpec((N_DEV, tr, c), lambda i: (0, i, 0)), blk, blk, blk],
        out_specs=[blk] * 4, out_shape=[jax.ShapeDtypeStruct((r, c), F32)] * 4,
        compiler_params=_params(),
    )(parts, w, m, v)


def _sum_parts(name, parts):
    _, r, c = parts.shape

    def body(p_ref, o_ref):
        g = p_ref[0]
        for k in range(1, N_DEV):
            g = g + p_ref[k]
        o_ref[...] = g

    return pl.pallas_call(
        body, name=name, in_specs=[_whole(parts.shape)], out_specs=_whole((r, c)),
        out_shape=jax.ShapeDtypeStruct((r, c), F32), compiler_params=_params(),
    )(parts)


def _adamw_many(name, grads, ws, ms, vs):
    n = len(grads)

    def body(*refs):
        ins, outs = refs[:4 * n], refs[4 * n:]
        for i in range(n):
            g, w, m, v = (ins[j * n + i][...] for j in range(4))
            for ref, val in zip((outs[i], outs[n + i], outs[2 * n + i]), _adamw(w, g, m, v)):
                ref[...] = val

    args = (*grads, *ws, *ms, *vs)
    outs = pl.pallas_call(
        body, name=name, in_specs=[_whole(a.shape) for a in args],
        out_specs=[_whole(a.shape) for a in ws] * 3,
        out_shape=[jax.ShapeDtypeStruct(a.shape, F32) for a in ws] * 3, compiler_params=_params(),
    )(*args)
    return outs[:n], outs[n:2 * n], outs[2 * n:]


def _pack(arrays, rows):
    flat = jnp.concatenate([a.reshape(-1) for a in arrays])
    return jnp.pad(flat, (0, rows * LANES - flat.shape[0])).reshape(rows, LANES)


def _unpack(packed, shapes):
    flat = packed.reshape(-1)
    out, off = [], 0
    for s in shapes:
        n = math.prod(s)
        out.append(flat[off:off + n].reshape(s))
        off += n
    return out


def _packed_rows(shapes):
    n = sum(math.prod(s) for s in shapes)
    return -(-n // (SUBLANES * LANES)) * SUBLANES


def kernel(x, meta_tokens, norm_mix_g, w_in, conv_w, ssm_lam_re, ssm_lam_im, ssm_log_dt, ssm_b_re, ssm_b_im, ssm_c_re, ssm_c_im, ssm_d, ssm_w_glu, gain_conv_out, gain_ssm_out, w_out, norm_ffn_g, w_up, ffn_conv_w, ffn_conv_b, w_down, norm_final_g, loss_target, m_meta_tokens, m_norm_mix_g, m_w_in, m_conv_w, m_ssm_lam_re, m_ssm_lam_im, m_ssm_log_dt, m_ssm_b_re, m_ssm_b_im, m_ssm_c_re, m_ssm_c_im, m_ssm_d, m_ssm_w_glu, m_gain_conv_out, m_gain_ssm_out, m_w_out, m_norm_ffn_g, m_w_up, m_ffn_conv_w, m_ffn_conv_b, m_w_down, m_norm_final_g, v_meta_tokens, v_norm_mix_g, v_w_in, v_conv_w, v_ssm_lam_re, v_ssm_lam_im, v_ssm_log_dt, v_ssm_b_re, v_ssm_b_im, v_ssm_c_re, v_ssm_c_im, v_ssm_d, v_ssm_w_glu, v_gain_conv_out, v_gain_ssm_out, v_w_out, v_norm_ffn_g, v_w_up, v_ffn_conv_w, v_ffn_conv_b, v_w_down, v_norm_final_g):
    weights = dict(meta_tokens=meta_tokens, norm_mix_g=norm_mix_g, w_in=w_in, conv_w=conv_w, ssm_lam_re=ssm_lam_re, ssm_lam_im=ssm_lam_im, ssm_log_dt=ssm_log_dt, ssm_b_re=ssm_b_re, ssm_b_im=ssm_b_im, ssm_c_re=ssm_c_re, ssm_c_im=ssm_c_im, ssm_d=ssm_d, ssm_w_glu=ssm_w_glu, gain_conv_out=gain_conv_out, gain_ssm_out=gain_ssm_out, w_out=w_out, norm_ffn_g=norm_ffn_g, w_up=w_up, ffn_conv_w=ffn_conv_w, ffn_conv_b=ffn_conv_b, w_down=w_down, norm_final_g=norm_final_g)
    mom_m = dict(meta_tokens=m_meta_tokens, norm_mix_g=m_norm_mix_g, w_in=m_w_in, conv_w=m_conv_w, ssm_lam_re=m_ssm_lam_re, ssm_lam_im=m_ssm_lam_im, ssm_log_dt=m_ssm_log_dt, ssm_b_re=m_ssm_b_re, ssm_b_im=m_ssm_b_im, ssm_c_re=m_ssm_c_re, ssm_c_im=m_ssm_c_im, ssm_d=m_ssm_d, ssm_w_glu=m_ssm_w_glu, gain_conv_out=m_gain_conv_out, gain_ssm_out=m_gain_ssm_out, w_out=m_w_out, norm_ffn_g=m_norm_ffn_g, w_up=m_w_up, ffn_conv_w=m_ffn_conv_w, ffn_conv_b=m_ffn_conv_b, w_down=m_w_down, norm_final_g=m_norm_final_g)
    mom_v = dict(meta_tokens=v_meta_tokens, norm_mix_g=v_norm_mix_g, w_in=v_w_in, conv_w=v_conv_w, ssm_lam_re=v_ssm_lam_re, ssm_lam_im=v_ssm_lam_im, ssm_log_dt=v_ssm_log_dt, ssm_b_re=v_ssm_b_re, ssm_b_im=v_ssm_b_im, ssm_c_re=v_ssm_c_re, ssm_c_im=v_ssm_c_im, ssm_d=v_ssm_d, ssm_w_glu=v_ssm_w_glu, gain_conv_out=v_gain_conv_out, gain_ssm_out=v_gain_ssm_out, w_out=v_w_out, norm_ffn_g=v_norm_ffn_g, w_up=v_w_up, ffn_conv_w=v_ffn_conv_w, ffn_conv_b=v_ffn_conv_b, w_down=v_w_down, norm_final_g=v_norm_final_g)
    names = list(weights)

    n_meta, d_meta = meta_tokens.shape
    seq, d = x.shape[1], x.shape[2]
    rows_used = n_meta + seq
    t = -(-rows_used // ROW_TILE) * ROW_TILE
    d_in_s = w_in.shape[2]
    dc_s = conv_w.shape[2]
    dc = dc_s * N_DEV
    ds = ssm_w_glu.shape[2]
    n_groups, n_state, grp = ssm_b_re.shape[1:]
    ns = n_groups * n_state
    nch = ds // SSM_CHUNK
    gpc = n_groups // nch
    ff_s = w_up.shape[2]
    dn_s = w_down.shape[1]
    assert 3 * dc + ds == d_in_s * N_DEV and 2 * dn_s == ff_s and t % (N_SEG * SUBLANES) == 0

    small_shard = jnp.concatenate([meta_tokens.reshape(-1), conv_w.reshape(-1), ffn_conv_w.reshape(-1)])
    n_small = small_shard.shape[0]
    small_rows = -(-n_small // LANES)
    small_shard = jnp.pad(small_shard, (0, small_rows * LANES - n_small)).reshape(small_rows, LANES)
    ag, ag_token = _gather2_start("gather_weights_start", [
        small_shard, w_in[0].astype(BF16), ssm_w_glu[0].astype(BF16), w_out[0].astype(BF16),
        jnp.swapaxes(w_up[0], 0, 1).astype(BF16), w_down[0].astype(BF16)])
    fb = ffn_conv_b

    gh = n_groups * grp
    per_h = lambda a: jnp.broadcast_to(a.reshape(n_groups, 1, -1), (n_groups, grp, n_state)).reshape(gh, n_state)
    ghp = lambda a: a.transpose(0, 1, 3, 2).reshape(gh, n_state)
    lr, li, log_dt_e = per_h(ssm_lam_re), per_h(ssm_lam_im), per_h(ssm_log_dt)
    br, bi = ghp(ssm_b_re), ghp(ssm_b_im)
    a_re_c, a_im_c, bbm_re, bbm_im, ccm_re, ccm_im = _disc_fwd(
        lr, li, log_dt_e, br, bi, ssm_c_re.reshape(gh, n_state), ssm_c_im.reshape(gh, n_state), nch, gpc,
        deps=(ag_token,))
    d_skip = ssm_d.reshape(nch, 1, SSM_CHUNK)

    target = _to_segments(jnp.pad(loss_target[0] + ag_token[0, 0], ((n_meta, t - rows_used), (0, 0))))
    token = _gather2_forward("gather_weights_forward_first", ag, (0, 1), (bbm_im, ccm_im, target))
    (g_small,) = _gather2_wait("gather_weights_wait_small", ag, (0,), token)
    g_small = g_small.reshape(N_DEV, -1)
    o1 = n_meta * d_meta
    o2 = o1 + 3 * dc_s
    meta_full = g_small[:, :o1].reshape(N_DEV, n_meta, d_meta).transpose(1, 0, 2).reshape(n_meta, d)
    conv_w_f = g_small[:, o1:o2].reshape(N_DEV, 3, dc_s).transpose(1, 0, 2).reshape(3, dc)
    fw = g_small[:, o2:o2 + 3 * ff_s].reshape(N_DEV, 3, ff_s).transpose(1, 0, 2).reshape(3, N_DEV * ff_s)
    h0 = _to_segments(jnp.concatenate([meta_full, x[0], jnp.zeros((t - rows_used, d), F32)], axis=0))
    full_t = lambda w: pl.BlockSpec((t, w), lambda *_: (0, 0))

    hn1 = _rms_fwd("norm_mix", h0, norm_mix_g)
    (g_in,) = _gather2_wait("gather_weights_wait_in", ag, (1,), hn1)
    proj = _mm("proj", hn1, g_in, dims=NN, grid=(N_DEV,), a_spec=full_t(d),
               b_spec=pl.BlockSpec((None, d, d_in_s), lambda j: (j, 0, 0)),
               o_spec=pl.BlockSpec((t, d_in_s), lambda j: (0, j)),
               out_shape=jax.ShapeDtypeStruct((t, N_DEV * d_in_s), F32))
    s_re, s_im, y_ssm = _s5_fwd(proj, 3 * dc, bbm_re, bbm_im, a_re_c, a_im_c, ccm_re, ccm_im, d_skip)
    token = _gather2_forward("gather_weights_forward_up", ag, (2, 3, 4), (y_ssm,))
    g_glu, g_out = _gather2_wait("gather_weights_wait_mix", ag, (2, 3), token)
    w_out_f = g_out.reshape(-1, d)
    w_glu_f = g_glu.reshape(ds, ds)
    mixed, z_glu, h1, hn2 = _mix_fwd(proj, y_ssm, w_glu_f, conv_w_f, gain_conv_out, gain_ssm_out,
                                     h0, w_out_f, norm_ffn_g)
    tn_out = OUT_TILE
    token = _gather2_forward("gather_weights_forward_down", ag, (5,), (hn2,))
    (g_up,) = _gather2_wait("gather_weights_wait_up", ag, (4,), token)
    f2 = N_DEV * ff_s
    pair = 2 * ff_s
    w_up_t = g_up.reshape(f2, d)
    up_pre = _mm("up_proj", hn2, w_up_t, dims=NT, grid=(f2 // pair,), a_spec=full_t(d),
                 b_spec=pl.BlockSpec((pair, d), lambda j: (j, 0)),
                 o_spec=pl.BlockSpec((t, pair), lambda j: (0, j)),
                 out_shape=jax.ShapeDtypeStruct((t, f2), BF16))
    act = _gated_fwd(up_pre, fw, fb)
    (g_down,) = _gather2_wait("gather_weights_wait_down", ag, (5,), act)
    w_down_f = g_down.reshape(f2 // 2, d)

    dh2, dh2_b, loss_part, d_norm_final = _down_loss_bwd(
        act, w_down_f, h1, target, norm_final_g.reshape(1, d), n_meta, rows_used)
    dw_down = _mm("down_wgrad", act, dh2_b, dims=TN, grid=(f2 // 2 // pair,),
                  a_spec=pl.BlockSpec((t, pair), lambda j: (0, j)), b_spec=full_t(d),
                  o_spec=pl.BlockSpec((pair, d), lambda j: (j, 0)),
                  out_shape=jax.ShapeDtypeStruct((f2 // 2, d), BF16))
    ex_down, token = _send_start("exchange_down_start", [dw_down.reshape(N_DEV, dn_s, d)], gather=False)
    dact = _mm("down_dgrad", dh2_b, w_down_f, dims=NT, grid=(f2 // 2 // pair,), a_spec=full_t(d),
               b_spec=pl.BlockSpec((pair, d), lambda j: (j, 0)),
               o_spec=pl.BlockSpec((t, pair), lambda j: (0, j)),
               out_shape=jax.ShapeDtypeStruct((t, f2 // 2), BF16))
    dup_pre, d_fw, d_fb = _gated_bwd(up_pre, dact, fw, fb, deps=(token,))
    dw_up = _mm("up_wgrad", dup_pre, hn2, dims=TN, grid=(f2 // pair,),
                a_spec=pl.BlockSpec((t, pair), lambda j: (0, j)), b_spec=full_t(d),
                o_spec=pl.BlockSpec((pair, d), lambda j: (j, 0)),
                out_shape=jax.ShapeDtypeStruct((f2, d), BF16))
    ex_up, token = _send_start("exchange_up_start", [dw_up.reshape(N_DEV, ff_s, d)], gather=False)
    dh1, dh1_b, d_norm_ffn = _dgrad_rms_bwd("up_dgrad_norm_bwd", dh2, dup_pre, w_up_t, NN, h1, norm_ffn_g,
                                            deps=(token,))
    dmixed = _mm("out_dgrad", dh1_b, w_out_f, dims=NT, grid=((dc + ds) // tn_out,), a_spec=full_t(d),
                 b_spec=pl.BlockSpec((tn_out, d), lambda i: (i, 0)),
                 o_spec=pl.BlockSpec((t, tn_out), lambda i: (0, i)),
                 out_shape=jax.ShapeDtypeStruct((t, dc + ds), F32))
    dw_out = _mm("out_wgrad", mixed, dh1_b, dims=TN, grid=((dc + ds) // tn_out,),
                 a_spec=pl.BlockSpec((t, tn_out), lambda i: (0, i)), b_spec=full_t(d),
                 o_spec=pl.BlockSpec((tn_out, d), lambda i: (i, 0)),
                 out_shape=jax.ShapeDtypeStruct((dc + ds, d), BF16))
    db_gate, dconv, dy_ssm, d_wglu, d_conv_w, d_gain_c, d_gain_s = _mix_bwd1(
        proj, y_ssm, z_glu, dmixed, w_glu_f, conv_w_f, gain_conv_out, gain_ssm_out)
    (du, d_bbm_re, d_bbm_im, d_a_re, d_a_im, d_ccm_re, d_ccm_im, d_dskip) = _s5_bwd(
        dy_ssm, proj, 3 * dc, s_re, s_im, bbm_re, bbm_im, a_re_c, a_im_c, ccm_re, ccm_im, d_skip, gpc)

    from_bb = from_cc = lambda a: a.reshape(gh, n_state)
    first_h = lambda a: jnp.pad(a.reshape(n_groups, 1, n_state), ((0, 0), (0, grp - 1), (0, 0))).reshape(gh, n_state)
    over_h = lambda a: a.reshape(n_groups, grp, n_state).sum(axis=1)
    d_lr, d_li, d_dt_e, d_br, d_bi = _disc_bwd(
        lr, li, log_dt_e, br, bi, first_h(d_a_re), first_h(d_a_im), from_bb(d_bbm_re), from_bb(d_bbm_im))

    rep2d = dict(
        ssm_lam_re=(n_groups, n_state), ssm_lam_im=(n_groups, n_state), ssm_log_dt=(1, n_groups),
        ssm_b_re=(gh, n_state), ssm_b_im=(gh, n_state), ssm_c_re=(gh, n_state),
        ssm_c_im=(gh, n_state), ssm_d=(n_groups, grp), gain_conv_out=(1, dc),
        gain_ssm_out=(1, ds), norm_ffn_g=(1, d), ffn_conv_b=(1, N_DEV * ff_s), norm_final_g=(1, d))
    rep_names = list(rep2d)
    rep_grads = dict(
        ssm_lam_re=over_h(d_lr), ssm_lam_im=over_h(d_li), ssm_log_dt=over_h(d_dt_e).sum(axis=1),
        ssm_b_re=d_br, ssm_b_im=d_bi, ssm_c_re=from_cc(d_ccm_re), ssm_c_im=from_cc(d_ccm_im),
        ssm_d=d_dskip, gain_conv_out=d_gain_c, gain_ssm_out=d_gain_s, norm_ffn_g=d_norm_ffn,
        ffn_conv_b=d_fb, norm_final_g=d_norm_final)
    rep_shapes = [rep2d[n] for n in rep_names] + [(1, 1)]
    rep_rows = _packed_rows(rep_shapes)
    rep_pack = _pack([rep_grads[n] for n in rep_names] + [loss_part], rep_rows)
    ex_mix, token = _send_start("exchange_mix_start", [
        rep_pack, dw_out.reshape(N_DEV, -1, d), d_wglu.astype(BF16).reshape(N_DEV, -1, ds),
        d_conv_w.reshape(3, N_DEV, dc_s).transpose(1, 0, 2),
        d_fw.reshape(3, N_DEV, ff_s).transpose(1, 0, 2)],
        gather=[True, False, False, False, False])
    dcdv = _mix_bwd2(proj, dconv, conv_w_f, deps=(token,))
    dproj = jnp.concatenate([db_gate, dcdv, du], axis=1)
    d_in = N_DEV * d_in_s
    w_in_rows = g_in.transpose(1, 0, 2).reshape(d, d_in)
    dw_in = _mm("proj_wgrad", hn1, dproj, dims=TN, grid=(N_DEV,), a_spec=full_t(d),
                b_spec=pl.BlockSpec((t, d_in_s), lambda j: (0, j)),
                o_spec=pl.BlockSpec((None, d, d_in_s), lambda j: (j, 0, 0)),
                out_shape=jax.ShapeDtypeStruct((N_DEV, d, d_in_s), BF16))
    dh0, _, d_norm_mix = _dgrad_rms_bwd("proj_dgrad_norm_bwd", dh1, dproj, w_in_rows, NT, h0, norm_mix_g)
    dh0 = _from_segments(dh0)
    grad_x = dh0[n_meta:rows_used][None]
    d_meta_b = dh0[:n_meta].reshape(n_meta, N_DEV, d_meta).transpose(1, 0, 2)
    ex_in, ex_in_token = _send_start("exchange_in_start", [dw_in, d_meta_b, d_norm_mix],
                                     gather=[False, False, True])

    shard_out = {}

    def update(n, parts, transposed=False):
        sh = weights[n].shape
        two_d = lambda a: (jnp.swapaxes(a[0], 0, 1) if transposed else a.reshape(parts.shape[1:]))
        res = _adamw_reduce("adamw_" + n, parts, two_d(weights[n]), two_d(mom_m[n]), two_d(mom_v[n]))
        shard_out[n] = [(jnp.swapaxes(r, 0, 1) if transposed else r).reshape(sh) for r in res]
        return res[0]

    (p_down,) = _send_wait("exchange_down_wait", ex_down, (0,), ex_in_token)
    done = update("w_down", p_down)
    (p_up,) = _send_wait("exchange_up_wait", ex_up, (0,), done)
    done = update("w_up", p_up, transposed=True)
    (rep_parts,) = _send_wait("gather_small_grads_wait", ex_mix, (0,), done)
    rep_sum = _sum_parts("sum_small_grads", rep_parts)
    *rep_g, loss = _unpack(rep_sum, rep_shapes)
    loss = loss.reshape(())
    swapped = ("ssm_b_re", "ssm_b_im")
    to2d = lambda n, a: ghp(a) if n in swapped else a.reshape(rep2d[n])
    from2d = lambda n, a: (a.reshape(1, n_groups, grp, n_state).transpose(0, 1, 3, 2) if n in swapped
                           else a.reshape(weights[n].shape))
    as2d = lambda tree: [to2d(n, tree[n]) for n in rep_names]
    rep_res = _adamw_many("adamw_replicated", rep_g, as2d(weights), as2d(mom_m), as2d(mom_v))
    for i, n in enumerate(rep_names):
        shard_out[n] = [from2d(n, r) for r in (rep_g[i], *(res[i] for res in rep_res))]

    p_out, p_glu, p_cw, p_fw = _send_wait("exchange_mix_wait", ex_mix, (1, 2, 3, 4), rep_sum)
    update("w_out", p_out)
    update("ssm_w_glu", p_glu)
    update("conv_w", p_cw)
    done = update("ffn_conv_w", p_fw)
    p_in, p_meta, p_nm = _send_wait("exchange_in_wait", ex_in, (0, 1, 2), done)
    update("w_in", p_in)
    update("meta_tokens", p_meta)
    update("norm_mix_g", p_nm)

    grads = [shard_out[n][0] for n in names]
    deltas = [shard_out[n][1] for n in names]
    new_m = [shard_out[n][2] for n in names]
    new_v = [shard_out[n][3] for n in names]
    return (loss, grad_x, *grads, *deltas, *new_m, *new_v)
```

```python
import math

import jax
import jax.numpy as jnp
from jax import lax
from jax.experimental import pallas as pl
from jax.experimental.pallas import tpu as pltpu

F32 = jnp.float32
BF16 = jnp.bfloat16
MESH = pl.DeviceIdType.MESH

N_DEV = 8
RMS_EPS = 1e-6
ADAM_LR = 0.001
ADAM_B1 = 0.9
ADAM_B2 = 0.999
ADAM_EPS = 1e-08
ADAM_WD = 0.01
ADAM_STEP = 10
ADAM_BC1 = 1.0 - ADAM_B1 ** ADAM_STEP
ADAM_BC2 = 1.0 - ADAM_B2 ** ADAM_STEP

SUBLANES = 8
LANES = 128
ROW_TILE = 128
ROW_CHUNK = 32
WIDE_TILES = (544, 256, 128)
FFN_COLS = 256
STAGE_COLS = 256
OUT_TILE = 512
N_SEG = 8
HALO_ROWS = 16
SSM_CHUNK = 128
SCAN_UNROLL = 8
VMEM_LIMIT = 48 * 1024 * 1024

NN = ((1,), (0,))
NT = ((1,), (1,))
TN = ((0,), (0,))


def _params(**kw):
    return pltpu.CompilerParams(vmem_limit_bytes=VMEM_LIMIT, **kw)


def _dot(a, b, dims):
    return lax.dot_general(a, b, (dims, ((), ())), preferred_element_type=F32)


def _mean_sq_rsqrt(x):
    return lax.rsqrt(jnp.mean(x * x, axis=-1, keepdims=True) + RMS_EPS)


def _rms_bwd(x, r, g, dy):
    xhat = x * r
    dxh = dy * g
    dx = r * (dxh - xhat * jnp.mean(dxh * xhat, axis=-1, keepdims=True))
    return dx, dy * xhat


def _sigmoid(x):
    return 0.5 * jnp.tanh(0.5 * x) + 0.5


def _gelu(y):
    c = math.sqrt(2.0 / math.pi)
    t = jnp.tanh(c * (y + 0.044715 * y * y * y))
    return 0.5 * y * (1.0 + t), t


def _gelu_grad(y, t):
    c = math.sqrt(2.0 / math.pi)
    return 0.5 * (1.0 + t) + 0.5 * y * (1.0 - t * t) * c * (1.0 + 3.0 * 0.044715 * y * y)


def _wrap_prev_halo(halo, first):
    seg = lax.broadcasted_iota(jnp.int32, halo.shape, 0) % N_SEG
    wrapped = jnp.where(seg == 0, 0.0, pltpu.roll(halo, 1, 0))
    return jnp.where(first, wrapped, halo)


def _wrap_next_halo(halo, last):
    seg = lax.broadcasted_iota(jnp.int32, halo.shape, 0) % N_SEG
    wrapped = jnp.where(seg == N_SEG - 1, 0.0, pltpu.roll(halo, halo.shape[0] - 1, 0))
    return jnp.where(last, wrapped, halo)


def _dev_index(p):
    return 4 * p[0] + 2 * p[1] + p[2]


def _allgather(name, shards, deps=()):
    n = len(shards)

    def body(*refs):
        ins, outs = refs[:n], refs[n:2 * n]
        send_sems, recv_sems, local_sems = refs[2 * n:]
        x, y, c = lax.axis_index("x"), lax.axis_index("y"), lax.axis_index("c")
        me, sibling = (x, y, c), (x, y, 1 - c)
        chips = [(1 - x, y), (x, 1 - y), (1 - x, 1 - y)]

        def copy(a, k, block, to, src=None):
            dst = outs[a].at[_dev_index(block)]
            return pltpu.make_async_remote_copy(
                src_ref=dst if src is None else src, dst_ref=dst,
                send_sem=send_sems.at[a, k], recv_sem=recv_sems.at[a, k],
                device_id=to, device_id_type=MESH)

        mine = [pltpu.make_async_copy(ins[a], outs[a].at[_dev_index(me)], local_sems.at[a])
                for a in range(n)]
        for cp in mine:
            cp.start()
        first = []
        for a in range(n):
            first.append(copy(a, 0, me, sibling, src=ins[a]))
            for j, chip in enumerate(chips):
                first.append(copy(a, 1 + j, me, (*chip, c), src=ins[a]))
        for cp in first:
            cp.start()
        passed = []
        for j, chip in enumerate(chips):
            for a in range(n):
                copy(a, 1 + j, (*chip, c), me).wait_recv()
                fwd = copy(a, 4 + j, (*chip, c), sibling)
                fwd.start()
                passed.append(fwd)
        for a in range(n):
            copy(a, 0, sibling, me).wait_recv()
            for j, chip in enumerate(chips):
                copy(a, 4 + j, (*chip, 1 - c), me).wait_recv()
        for cp in first + passed:
            cp.wait_send()
        for cp in mine:
            cp.wait()

    any_spec = pl.BlockSpec(memory_space=pl.ANY)
    return pl.pallas_call(
        _ignoring_deps(body, n, deps), name=name,
        out_shape=[jax.ShapeDtypeStruct((N_DEV,) + s.shape, s.dtype) for s in shards],
        in_specs=[any_spec] * (n + len(deps)), out_specs=[any_spec] * n,
        scratch_shapes=[pltpu.SemaphoreType.DMA((n, 7)), pltpu.SemaphoreType.DMA((n, 7)),
                        pltpu.SemaphoreType.DMA((n,))],
    )(*shards, *deps)


HBM_SPEC = pl.BlockSpec(memory_space=pltpu.HBM)
SEM_SPEC = pl.BlockSpec(memory_space=pltpu.SEMAPHORE)
ANY_SPEC = pl.BlockSpec(memory_space=pl.ANY)
DATAFLOW = pltpu.SideEffectType.DATAFLOW_SIDE_EFFECTING


def _ignoring_deps(body, n_in, deps):
    n_dep = len(deps)

    def wrapped(*refs):
        return body(*refs[:n_in], *refs[n_in + n_dep:])

    return wrapped


def _my_position():
    x, y, c = lax.axis_index("x"), lax.axis_index("y"), lax.axis_index("c")
    return (x, y, c)


def _peer(me, k):
    return tuple((1 - v) if (k >> s) & 1 else v for v, s in zip(me, (2, 1, 0)))


def _split_copies(src_refs, land_refs, send_sems, recv_sems, gather):
    me = _my_position()
    copies = []
    for a, (src, land) in enumerate(zip(src_refs, land_refs)):
        for k in range(1, N_DEV):
            peer = _peer(me, k)
            copies.append(pltpu.make_async_remote_copy(
                src_ref=src if gather[a] else src.at[_dev_index(peer)], dst_ref=land.at[_dev_index(me)],
                send_sem=send_sems[a].at[k - 1], recv_sem=recv_sems[a].at[k - 1],
                device_id=peer, device_id_type=MESH))
    return copies


def _own_slot(block, like_shape):
    me = _dev_index(_my_position())
    return lax.dynamic_update_index_in_dim(lax.empty(like_shape, block.dtype), block, me, 0)


def _send_start(name, srcs, gather):
    n = len(srcs)
    me = _dev_index(_my_position())
    gather = [gather] * n if isinstance(gather, bool) else list(gather)
    lands = [_own_slot(s, (N_DEV,) + s.shape) if g else
             _own_slot(lax.dynamic_index_in_dim(s, me, 0, keepdims=False), s.shape)
             for s, g in zip(srcs, gather)]

    def body(*refs):
        src_refs, land_refs = refs[:n], refs[n:2 * n]
        send_sems, recv_sems = refs[2 * n:3 * n], refs[3 * n:4 * n]
        token = refs[-1]
        for cp in _split_copies(src_refs, land_refs, send_sems, recv_sems, gather):
            cp.start()
        token[...] = jnp.zeros_like(token)

    hbm = lambda a: pltpu.HBM(a.shape, a.dtype)
    sems = [pltpu.SemaphoreType.DMA((N_DEV - 1,))] * n
    outs = pl.pallas_call(
        body, name=name,
        out_shape=(*sems, *sems, *[hbm(s) for s in srcs], *[hbm(l) for l in lands],
                   jax.ShapeDtypeStruct((SUBLANES, LANES), F32)),
        in_specs=[HBM_SPEC] * (2 * n),
        out_specs=(*[SEM_SPEC] * (2 * n), *[HBM_SPEC] * (2 * n), pl.BlockSpec(memory_space=pltpu.VMEM)),
        input_output_aliases={i: 2 * n + i for i in range(2 * n)},
        compiler_params=pltpu.CompilerParams(has_side_effects=DATAFLOW),
    )(*[pltpu.with_memory_space_constraint(a, pltpu.HBM) for a in (*srcs, *lands)])
    state = dict(send=outs[:n], recv=outs[n:2 * n], srcs=outs[2 * n:3 * n], lands=outs[3 * n:4 * n],
                 gather=gather)
    return state, outs[-1]


def _send_wait(name, state, which, after):
    n = len(which)
    pick = lambda key: [state[key][i] for i in which]
    gather = pick("gather")

    def body(*refs):
        src_refs, land_refs = refs[:n], refs[n:2 * n]
        send_sems, recv_sems = refs[2 * n:3 * n], refs[3 * n:4 * n]
        for cp in _split_copies(src_refs, land_refs, send_sems, recv_sems, gather):
            cp.wait_send()
            cp.wait_recv()

    srcs, lands = pick("srcs"), pick("lands")
    hbm = lambda a: pltpu.HBM(a.shape, a.dtype)
    outs = pl.pallas_call(
        body, name=name,
        out_shape=(*[hbm(s) for s in srcs], *[hbm(l) for l in lands]),
        in_specs=[*[HBM_SPEC] * (2 * n), *[SEM_SPEC] * (2 * n), ANY_SPEC],
        out_specs=tuple([HBM_SPEC] * (2 * n)),
        input_output_aliases={i: i for i in range(2 * n)},
        compiler_params=pltpu.CompilerParams(has_side_effects=DATAFLOW),
    )(*srcs, *lands, *pick("send"), *pick("recv"), after)
    return outs[n:]


def _two_level_copies(src_refs, land_refs, sems1, sems2):
    x, y, c = _my_position()
    me, sibling = (x, y, c), (x, y, 1 - c)
    chips = [(1 - x, y), (x, 1 - y), (1 - x, 1 - y)]
    stage1, stage2 = [], []
    for a, land in enumerate(land_refs):
        def copy(src, block, to, sems, k):
            return pltpu.make_async_remote_copy(
                src_ref=src, dst_ref=land.at[_dev_index(block)], send_sem=sems[0][a].at[k],
                recv_sem=sems[1][a].at[k], device_id=to, device_id_type=MESH)
        src = src_refs[a] if src_refs is not None else land.at[_dev_index(me)]
        stage1.append([copy(src, me, sibling, sems1, 0)] +
                      [copy(src, me, (*chip, c), sems1, 1 + j) for j, chip in enumerate(chips)])
        if sems2 is not None:
            stage2.append([copy(land.at[_dev_index((*chip, c))], (*chip, c), sibling, sems2, j)
                           for j, chip in enumerate(chips)])
    return stage1, stage2


def _gather2_start(name, shards):
    n = len(shards)
    lands = [_own_slot(s, (N_DEV,) + s.shape) for s in shards]

    def body(*refs):
        src_refs, land_refs = refs[:n], refs[n:2 * n]
        sems1 = (refs[2 * n:3 * n], refs[3 * n:4 * n])
        stage1, _ = _two_level_copies(src_refs, land_refs, sems1, None)
        for copies in stage1:
            for cp in copies:
                cp.start()
        refs[-1][...] = jnp.zeros_like(refs[-1])

    hbm = lambda a: pltpu.HBM(a.shape, a.dtype)
    sems = [pltpu.SemaphoreType.DMA((4,))] * n
    outs = pl.pallas_call(
        body, name=name,
        out_shape=(*sems, *sems, *[hbm(s) for s in shards], *[hbm(l) for l in lands],
                   jax.ShapeDtypeStruct((SUBLANES, LANES), F32)),
        in_specs=[HBM_SPEC] * (2 * n),
        out_specs=(*[SEM_SPEC] * (2 * n), *[HBM_SPEC] * (2 * n), pl.BlockSpec(memory_space=pltpu.VMEM)),
        input_output_aliases={i: 2 * n + i for i in range(2 * n)},
        compiler_params=pltpu.CompilerParams(has_side_effects=DATAFLOW),
    )(*[pltpu.with_memory_space_constraint(a, pltpu.HBM) for a in (*shards, *lands)])
    state = dict(send1=list(outs[:n]), recv1=list(outs[n:2 * n]), srcs=list(outs[2 * n:3 * n]),
                 lands=list(outs[3 * n:4 * n]), send2={}, recv2={})
    return state, outs[-1]


def _gather2_forward(name, state, which, after):
    n = len(which)
    pick = lambda key: [state[key][i] for i in which]

    def body(*refs):
        land_refs, recv1 = refs[:n], refs[n:2 * n]
        outs = refs[2 * n + len(after):]
        sems2 = (outs[:n], outs[n:2 * n])
        stage1, stage2 = _two_level_copies(None, land_refs, (recv1, recv1), sems2)
        for a in range(n):
            for j in range(3):
                stage1[a][1 + j].wait_recv()
                stage2[a][j].start()
        outs[-1][...] = jnp.zeros_like(outs[-1])

    lands = pick("lands")
    sems = [pltpu.SemaphoreType.DMA((3,))] * n
    outs = pl.pallas_call(
        body, name=name,
        out_shape=(*sems, *sems, *[pltpu.HBM(l.shape, l.dtype) for l in lands],
                   jax.ShapeDtypeStruct((SUBLANES, LANES), F32)),
        in_specs=[*[HBM_SPEC] * n, *[SEM_SPEC] * n, *[ANY_SPEC] * len(after)],
        out_specs=(*[SEM_SPEC] * (2 * n), *[HBM_SPEC] * n, pl.BlockSpec(memory_space=pltpu.VMEM)),
        input_output_aliases={i: 2 * n + i for i in range(n)},
        compiler_params=pltpu.CompilerParams(has_side_effects=DATAFLOW),
    )(*lands, *pick("recv1"), *after)
    for idx, i in enumerate(which):
        state["send2"][i], state["recv2"][i] = outs[idx], outs[n + idx]
        state["lands"][i] = outs[2 * n + idx]
    return outs[-1]


def _gather2_wait(name, state, which, after):
    n = len(which)
    pick = lambda key: [state[key][i] for i in which]

    def body(*refs):
        src_refs, land_refs = refs[:n], refs[n:2 * n]
        sems1 = (refs[2 * n:3 * n], refs[3 * n:4 * n])
        sems2 = (refs[4 * n:5 * n], refs[5 * n:6 * n])
        stage1, stage2 = _two_level_copies(src_refs, land_refs, sems1, sems2)
        for a in range(n):
            for cp in stage1[a]:
                cp.wait_send()
            stage1[a][0].wait_recv()
            for cp in stage2[a]:
                cp.wait_send()
                cp.wait_recv()

    srcs, lands = pick("srcs"), pick("lands")
    hbm = lambda a: pltpu.HBM(a.shape, a.dtype)
    outs = pl.pallas_call(
        body, name=name,
        out_shape=(*[hbm(s) for s in srcs], *[hbm(l) for l in lands]),
        in_specs=[*[HBM_SPEC] * (2 * n), *[SEM_SPEC] * (4 * n), ANY_SPEC],
        out_specs=tuple([HBM_SPEC] * (2 * n)),
        input_output_aliases={i: i for i in range(2 * n)},
        compiler_params=pltpu.CompilerParams(has_side_effects=DATAFLOW),
    )(*srcs, *lands, *pick("send1"), *pick("recv1"), *pick("send2"), *pick("recv2"), after)
    return outs[n:]


def _mm(name, a, b, *, dims, grid, a_spec, b_spec, o_spec, out_shape, acc_shape=None,
        res=None, res_spec=None):
    n_red = grid[-1] if acc_shape is not None else 1
    red_axis = len(grid) - 1

    def body(*refs):
        a_ref, b_ref = refs[0], refs[1]
        r_ref = refs[2] if res is not None else None
        o_ref = refs[3] if res is not None else refs[2]
        part = _dot(a_ref[...], b_ref[...], dims)
        if acc_shape is None:
            if r_ref is not None:
                part = part + r_ref[...]
            o_ref[...] = part.astype(o_ref.dtype)
            return
        acc_ref = refs[-1]
        k = pl.program_id(red_axis)

        @pl.when(k == 0)
        def _():
            acc_ref[...] = part

        @pl.when(k > 0)
        def _():
            acc_ref[...] += part

        @pl.when(k == n_red - 1)
        def _():
            total = acc_ref[...]
            if r_ref is not None:
                total = total + r_ref[...]
            o_ref[...] = total.astype(o_ref.dtype)

    ins, in_specs = [a, b], [a_spec, b_spec]
    if res is not None:
        ins.append(res)
        in_specs.append(res_spec)
    return pl.pallas_call(
        body, name=name, grid=grid, in_specs=in_specs, out_specs=o_spec, out_shape=out_shape,
        scratch_shapes=[pltpu.VMEM(acc_shape, F32)] if acc_shape is not None else [],
        compiler_params=_params(),
    )(*ins)


def _wide_tile(t):
    return next(c for c in WIDE_TILES if t % c == 0)


def _rms_fwd(name, h, g):
    t, d = h.shape
    tile = _wide_tile(t)

    def body(h_ref, g_ref, o_ref):
        x = h_ref[...]
        o_ref[...] = (x * _mean_sq_rsqrt(x) * g_ref[...]).astype(BF16)

    return pl.pallas_call(
        body, name=name, grid=(t // tile,),
        in_specs=[pl.BlockSpec((tile, d), lambda i: (i, 0)), pl.BlockSpec((1, d), lambda i: (0, 0))],
        out_specs=pl.BlockSpec((tile, d), lambda i: (i, 0)),
        out_shape=jax.ShapeDtypeStruct((t, d), BF16), compiler_params=_params(),
    )(h, g)


def _fused_tile(t):
    half = _wide_tile(t) // 2
    return half if half % 16 == 0 and t % half == 0 else _wide_tile(t)


def _dgrad_rms_bwd(name, dres, dy, w, dims, h, g, deps=()):
    t, d = h.shape
    k = dy.shape[1]

    def body(dres_ref, dy_ref, w_ref, h_ref, g_ref, dh_ref, dhb_ref, dg_ref):
        x = h_ref[...]
        dhn = _dot(dy_ref[...], w_ref[...], dims)
        dx, dgt = _rms_bwd(x, _mean_sq_rsqrt(x), g_ref[...], dhn)
        dh = dres_ref[...] + dx
        dh_ref[...] = dh
        dhb_ref[...] = dh.astype(BF16)

        @pl.when(pl.program_id(0) == 0)
        def _():
            dg_ref[...] = jnp.zeros_like(dg_ref)

        dg_ref[...] += jnp.sum(dgt, axis=0, keepdims=True)

    tile = _fused_tile(t)
    row = pl.BlockSpec((tile, d), lambda i: (i, 0))
    vec = pl.BlockSpec((1, d), lambda i: (0, 0))
    return pl.pallas_call(
        _ignoring_deps(body, 5, deps), name=name, grid=(t // tile,),
        in_specs=[row, pl.BlockSpec((tile, k), lambda i: (i, 0)), pl.BlockSpec(w.shape, lambda i: (0, 0)), row,
                  vec] + [ANY_SPEC] * len(deps),
        out_specs=[row, row, vec],
        out_shape=[jax.ShapeDtypeStruct((t, d), F32), jax.ShapeDtypeStruct((t, d), BF16),
                   jax.ShapeDtypeStruct((1, d), F32)],
        compiler_params=_params(),
    )(dres, dy, w, h, g, *deps)


def _down_loss_bwd(act, w_down, h1, target, g, row_lo, row_hi):
    t, d = h1.shape
    f = act.shape[1]
    tile = _fused_tile(t)

    def body(act_ref, w_ref, h_ref, tg_ref, g_ref, dh_ref, dhb_ref, loss_ref, dg_ref):
        i = pl.program_id(0)
        x = h_ref[...] + _dot(act_ref[...], w_ref[...], NN)
        r = _mean_sq_rsqrt(x)
        gv = g_ref[...]
        y = x * r * gv
        rows = i * tile + lax.broadcasted_iota(jnp.int32, (tile, 1), 0)
        time = (rows % N_SEG) * (t // N_SEG) + rows // N_SEG
        valid = jnp.logical_and(time >= row_lo, time < row_hi)
        err = jnp.where(valid, y - tg_ref[...], 0.0)
        dy = err * (1.0 / d)
        dx, dgt = _rms_bwd(x, r, gv, dy)
        dh_ref[...] = dx
        dhb_ref[...] = dx.astype(BF16)

        @pl.when(i == 0)
        def _():
            loss_ref[...] = jnp.zeros_like(loss_ref)
            dg_ref[...] = jnp.zeros_like(dg_ref)

        row_loss = jnp.mean(err * err, axis=-1, keepdims=True)
        loss_ref[...] += 0.5 * jnp.sum(row_loss, axis=0, keepdims=True)
        dg_ref[...] += jnp.sum(dgt, axis=0, keepdims=True)

    row = pl.BlockSpec((tile, d), lambda i: (i, 0))
    vec = pl.BlockSpec((1, d), lambda i: (0, 0))
    return pl.pallas_call(
        body, name="down_proj_loss_bwd", grid=(t // tile,),
        in_specs=[pl.BlockSpec((tile, f), lambda i: (i, 0)), pl.BlockSpec(w_down.shape, lambda i: (0, 0)),
                  row, row, vec],
        out_specs=[row, row, pl.BlockSpec((1, 1), lambda i: (0, 0)), vec],
        out_shape=[jax.ShapeDtypeStruct((t, d), F32), jax.ShapeDtypeStruct((t, d), BF16),
                   jax.ShapeDtypeStruct((1, 1), F32), jax.ShapeDtypeStruct((1, d), F32)],
        compiler_params=_params(),
    )(act, w_down, h1, target, g)


def _prev_halo(i, t, tile=ROW_TILE):
    return jnp.where(i == 0, t // HALO_ROWS - 1, i * (tile // HALO_ROWS) - 1)


def _next_halo(i, t, tile=ROW_TILE):
    return jnp.where(i == t // tile - 1, 0, (i + 1) * (tile // HALO_ROWS))


def _causal_taps(cur, halo, first):
    rows = cur.shape[0]
    ext = jnp.concatenate([_wrap_prev_halo(halo, first), cur], axis=0)
    return ext[:rows], ext[N_SEG:N_SEG + rows]


def _anticausal_taps(cur, halo, last):
    rows = cur.shape[0]
    ext = jnp.concatenate([cur, _wrap_next_halo(halo, last)], axis=0)
    return ext[N_SEG:N_SEG + rows], ext[2 * N_SEG:2 * N_SEG + rows]


def _mix_fwd(proj, y, w_glu, conv_w, gain_c, gain_s, h0, w_out, gain_ffn):
    t, d = h0.shape
    dc = conv_w.shape[1]
    ds = y.shape[1]

    def body(p_ref, halo_ref, y_ref, wg_ref, cw_ref, gc_ref, gs_ref, h0_ref, wo_ref, gf_ref,
             mixed_ref, z_ref, h1_ref, hn_ref):
        i = pl.program_id(0)
        p = p_ref[...]
        b, c, v = p[:, :dc], p[:, dc:2 * dc], p[:, 2 * dc:3 * dc]
        cv = c * v
        hp = halo_ref[...]
        x2, x1 = _causal_taps(cv, hp[:, dc:2 * dc] * hp[:, 2 * dc:3 * dc], i == 0)
        cw = cw_ref[...]
        conv = cw[0:1] * x2 + cw[1:2] * x1 + cw[2:3] * cv
        co = b * conv
        mixed_ref[:, :dc] = (co * _mean_sq_rsqrt(co) * gc_ref[...]).astype(BF16)
        g, _ = _gelu(y_ref[...])
        z = _dot(g.astype(BF16), wg_ref[...], NN)
        z_ref[...] = z
        so = g * _sigmoid(z)
        mixed_ref[:, dc:] = (so * _mean_sq_rsqrt(so) * gs_ref[...]).astype(BF16)
        h1 = h0_ref[...] + _dot(mixed_ref[...], wo_ref[...], NN)
        h1_ref[...] = h1
        hn_ref[...] = (h1 * _mean_sq_rsqrt(h1) * gf_ref[...]).astype(BF16)

    const = lambda i: (0, 0)
    tile = _fused_tile(t)
    row = lambda w: pl.BlockSpec((tile, w), lambda i: (i, 0))
    return pl.pallas_call(
        body, name="mix_fwd_out_proj", grid=(t // tile,),
        in_specs=[row(3 * dc), pl.BlockSpec((HALO_ROWS, 3 * dc), lambda i: (_prev_halo(i, t, tile), 0)), row(ds),
                  pl.BlockSpec((ds, ds), const), pl.BlockSpec(conv_w.shape, const),
                  pl.BlockSpec((1, dc), const), pl.BlockSpec((1, ds), const),
                  row(d), pl.BlockSpec(w_out.shape, const), pl.BlockSpec((1, d), const)],
        out_specs=[row(dc + ds), row(ds), row(d), row(d)],
        out_shape=[jax.ShapeDtypeStruct((t, dc + ds), BF16), jax.ShapeDtypeStruct((t, ds), F32),
                   jax.ShapeDtypeStruct((t, d), F32), jax.ShapeDtypeStruct((t, d), BF16)],
        compiler_params=_params(),
    )(proj, proj, y, w_glu, conv_w, gain_c, gain_s, h0, w_out, gain_ffn)


def _mix_bwd1(proj, y, z, dmixed, w_glu, conv_w, gain_c, gain_s):
    t = proj.shape[0]
    dc = conv_w.shape[1]
    ds = y.shape[1]

    def body(p_ref, halo_ref, y_ref, z_ref, dm_ref, wg_ref, cw_ref, gc_ref, gs_ref,
             db_ref, dconv_ref, dy_ref, dwg_ref, dcw_ref, dgc_ref, dgs_ref):
        i = pl.program_id(0)

        @pl.when(i == 0)
        def _():
            dwg_ref[...] = jnp.zeros_like(dwg_ref)
            dcw_ref[...] = jnp.zeros_like(dcw_ref)
            dgc_ref[...] = jnp.zeros_like(dgc_ref)
            dgs_ref[...] = jnp.zeros_like(dgs_ref)

        p = p_ref[...]
        b, c, v = p[:, :dc], p[:, dc:2 * dc], p[:, 2 * dc:3 * dc]
        cv = c * v
        hp = halo_ref[...]
        x2, x1 = _causal_taps(cv, hp[:, dc:2 * dc] * hp[:, 2 * dc:3 * dc], i == 0)
        cw = cw_ref[...]
        conv = cw[0:1] * x2 + cw[1:2] * x1 + cw[2:3] * cv
        co = b * conv
        dm = dm_ref[...]
        dco, dgt = _rms_bwd(co, _mean_sq_rsqrt(co), gc_ref[...], dm[:, :dc])
        dgc_ref[...] += jnp.sum(dgt, axis=0, keepdims=True)
        db_ref[...] = (dco * conv).astype(BF16)
        dconv = dco * b
        dconv_ref[...] = dconv
        dcw_ref[0:1, :] += jnp.sum(dconv * x2, axis=0, keepdims=True)
        dcw_ref[1:2, :] += jnp.sum(dconv * x1, axis=0, keepdims=True)
        dcw_ref[2:3, :] += jnp.sum(dconv * cv, axis=0, keepdims=True)

        yv = y_ref[...]
        g, th = _gelu(yv)
        sg = _sigmoid(z_ref[...])
        so = g * sg
        dso, dgt = _rms_bwd(so, _mean_sq_rsqrt(so), gs_ref[...], dm[:, dc:])
        dgs_ref[...] += jnp.sum(dgt, axis=0, keepdims=True)
        dz = (dso * g * sg * (1.0 - sg)).astype(BF16)
        dg = dso * sg + _dot(dz, wg_ref[...], NT)
        dwg_ref[...] += _dot(g.astype(BF16), dz, TN)
        dy_ref[...] = dg * _gelu_grad(yv, th)

    const = lambda i: (0, 0)
    tile = _fused_tile(t)
    row = lambda w: pl.BlockSpec((tile, w), lambda i: (i, 0))
    return pl.pallas_call(
        body, name="mix_bwd1", grid=(t // tile,),
        in_specs=[row(3 * dc), pl.BlockSpec((HALO_ROWS, 3 * dc), lambda i: (_prev_halo(i, t, tile), 0)),
                  row(ds), row(ds), row(dc + ds), pl.BlockSpec((ds, ds), const),
                  pl.BlockSpec(conv_w.shape, const), pl.BlockSpec((1, dc), const),
                  pl.BlockSpec((1, ds), const)],
        out_specs=[row(dc), row(dc), row(ds), pl.BlockSpec((ds, ds), const),
                   pl.BlockSpec(conv_w.shape, const), pl.BlockSpec((1, dc), const),
                   pl.BlockSpec((1, ds), const)],
        out_shape=[jax.ShapeDtypeStruct((t, dc), BF16), jax.ShapeDtypeStruct((t, dc), F32),
                   jax.ShapeDtypeStruct((t, ds), F32), jax.ShapeDtypeStruct((ds, ds), F32),
                   jax.ShapeDtypeStruct(conv_w.shape, F32), jax.ShapeDtypeStruct((1, dc), F32),
                   jax.ShapeDtypeStruct((1, ds), F32)],
        compiler_params=_params(),
    )(proj, proj, y, z, dmixed, w_glu, conv_w, gain_c, gain_s)


def _mix_bwd2(proj, dconv, conv_w, deps=()):
    t = proj.shape[0]
    dc = conv_w.shape[1]
    tile = _wide_tile(t)
    n_tiles = t // tile

    def body(c_ref, v_ref, d_ref, halo_ref, cw_ref, o_ref):
        i = pl.program_id(0)
        d = d_ref[...]
        u1, u2 = _anticausal_taps(d, halo_ref[...], i == n_tiles - 1)
        cw = cw_ref[...]
        dcv = cw[2:3] * d + cw[1:2] * u1 + cw[0:1] * u2
        o_ref[:, :dc] = (dcv * v_ref[...]).astype(BF16)
        o_ref[:, dc:] = (dcv * c_ref[...]).astype(BF16)

    return pl.pallas_call(
        _ignoring_deps(body, 5, deps), name="mix_bwd2", grid=(n_tiles,),
        in_specs=[pl.BlockSpec((tile, dc), lambda i: (i, 1)),
                  pl.BlockSpec((tile, dc), lambda i: (i, 2)),
                  pl.BlockSpec((tile, dc), lambda i: (i, 0)),
                  pl.BlockSpec((HALO_ROWS, dc), lambda i: (_next_halo(i, t, tile), 0)),
                  pl.BlockSpec(conv_w.shape, lambda i: (0, 0))] + [ANY_SPEC] * len(deps),
        out_specs=pl.BlockSpec((tile, 2 * dc), lambda i: (i, 0)),
        out_shape=jax.ShapeDtypeStruct((t, 2 * dc), BF16), compiler_params=_params(),
    )(proj, proj, dconv, dconv, conv_w, *deps)


def _stage_rows(stage_ref, row0, src, wrap=None):
    n, width = src.shape
    for c in range(0, width, STAGE_COLS):
        v = src[:, pl.ds(c, STAGE_COLS)].astype(F32)
        stage_ref[pl.ds(row0, n), pl.ds(c, STAGE_COLS)] = v if wrap is None else wrap(v)


def _conv_taps(stage_ref, fw_ref, fb_ref, c0, r, rows=ROW_CHUNK):
    cols = pl.ds(c0, FFN_COLS)
    x0, x1, x2 = (stage_ref[pl.ds(HALO_ROWS + r - k * N_SEG, rows), cols] for k in range(3))
    w = fw_ref[:, cols]
    return w[0:1] * x2 + w[1:2] * x1 + w[2:3] * x0 + fb_ref[:, cols], x2, x1, x0


def _gated_fwd(up_pre, fw, fb):
    t, f2 = up_pre.shape
    f = f2 // 2

    def body(up_ref, halo_ref, fw_ref, fb_ref, act_ref, stage_ref):
        first = pl.program_id(0) == 0
        _stage_rows(stage_ref, 0, halo_ref, lambda v: _wrap_prev_halo(v, first))
        _stage_rows(stage_ref, HALO_ROWS, up_ref)
        for c0 in range(0, f, FFN_COLS):
            for r in range(0, ROW_TILE, ROW_CHUNK):
                a = _conv_taps(stage_ref, fw_ref, fb_ref, c0, r)[0]
                val = _conv_taps(stage_ref, fw_ref, fb_ref, f + c0, r)[0]
                half = 0.5 * a
                act_ref[pl.ds(r, ROW_CHUNK), pl.ds(c0, FFN_COLS)] = ((half * jnp.tanh(half) + half) * val).astype(BF16)

    const = lambda a: pl.BlockSpec(a.shape, lambda i: (0, 0))
    return pl.pallas_call(
        body, name="ffn_fwd", grid=(t // ROW_TILE,),
        in_specs=[pl.BlockSpec((ROW_TILE, f2), lambda i: (i, 0)),
                  pl.BlockSpec((HALO_ROWS, f2), lambda i: (_prev_halo(i, t), 0)), const(fw), const(fb)],
        out_specs=pl.BlockSpec((ROW_TILE, f), lambda i: (i, 0)),
        out_shape=jax.ShapeDtypeStruct((t, f), BF16),
        scratch_shapes=[pltpu.VMEM((HALO_ROWS + ROW_TILE, f2), F32)], compiler_params=_params(),
    )(up_pre, up_pre, fw, fb)


def _gated_bwd(up_pre, dact, fw, fb, deps=()):
    t, f2 = up_pre.shape
    f = f2 // 2
    n_tiles = t // ROW_TILE
    chunks = [(r, ROW_CHUNK) for r in range(0, ROW_TILE, ROW_CHUNK)] + [(ROW_TILE, HALO_ROWS)]

    def body(up_ref, prev_ref, next_ref, dact_in_ref, dact_next_ref, fw_ref, fb_ref,
             out_ref, dfw_ref, dfb_ref, dup_ref, stage_ref, dact_ref):
        first = pl.program_id(0) == 0
        last = pl.program_id(0) == n_tiles - 1

        @pl.when(first)
        def _():
            dfw_ref[...] = jnp.zeros_like(dfw_ref)
            dfb_ref[...] = jnp.zeros_like(dfb_ref)

        _stage_rows(dact_ref, 0, dact_in_ref)
        _stage_rows(dact_ref, ROW_TILE, dact_next_ref, lambda v: _wrap_next_halo(v, last))
        _stage_rows(stage_ref, 0, prev_ref, lambda v: _wrap_prev_halo(v, first))
        _stage_rows(stage_ref, HALO_ROWS, up_ref)
        _stage_rows(stage_ref, HALO_ROWS + ROW_TILE, next_ref, lambda v: _wrap_next_halo(v, last))
        fold = lambda v: sum(v[s:s + SUBLANES] for s in range(0, ROW_CHUNK, SUBLANES))
        for c0 in range(0, f, FFN_COLS):
            starts = (c0, f + c0)
            sums = [[jnp.zeros((SUBLANES, FFN_COLS), F32)] * 4 for _ in starts]
            for r, rows in chunks:
                taps = [_conv_taps(stage_ref, fw_ref, fb_ref, c, r, rows) for c in starts]
                a, val = taps[0][0], taps[1][0]
                da_ct = dact_ref[pl.ds(r, rows), pl.ds(c0, FFN_COLS)]
                sg = _sigmoid(a)
                gated = da_ct * sg
                a_sg = a * sg
                dup = (gated * val * (1.0 + a - a_sg), gated * a)
                for k in range(2):
                    dup_ref[k, pl.ds(r, rows), :] = dup[k]
                    if r < ROW_TILE:
                        terms = (dup[k], dup[k] * taps[k][1], dup[k] * taps[k][2], dup[k] * taps[k][3])
                        sums[k] = [s + fold(v) for s, v in zip(sums[k], terms)]
            for k, c in enumerate(starts):
                cols = pl.ds(c, FFN_COLS)
                s_b, s_w0, s_w1, s_w2 = (jnp.sum(p, axis=0, keepdims=True) for p in sums[k])
                dfb_ref[:, cols] += s_b
                for tap, s_w in enumerate((s_w0, s_w1, s_w2)):
                    dfw_ref[tap:tap + 1, cols] += s_w
                cw = fw_ref[:, cols]
                for r in range(0, ROW_TILE, ROW_CHUNK):
                    d, u1, u2 = (dup_ref[k, pl.ds(r + s * N_SEG, ROW_CHUNK), :] for s in range(3))
                    out_ref[pl.ds(r, ROW_CHUNK), cols] = (cw[2:3] * d + cw[1:2] * u1 + cw[0:1] * u2).astype(BF16)

    tile = lambda w: pl.BlockSpec((ROW_TILE, w), lambda i: (i, 0))
    halo = lambda w, index: pl.BlockSpec((HALO_ROWS, w), lambda i: (index(i, t), 0))
    const = lambda a: pl.BlockSpec(a.shape, lambda i: (0, 0))
    return pl.pallas_call(
        _ignoring_deps(body, 7, deps), name="ffn_bwd", grid=(n_tiles,),
        in_specs=[tile(f2), halo(f2, _prev_halo), halo(f2, _next_halo), tile(f), halo(f, _next_halo),
                  const(fw), const(fb)] + [ANY_SPEC] * len(deps),
        out_specs=[tile(f2), const(fw), const(fb)],
        out_shape=[jax.ShapeDtypeStruct((t, f2), BF16), jax.ShapeDtypeStruct(fw.shape, F32),
                   jax.ShapeDtypeStruct(fb.shape, F32)],
        scratch_shapes=[pltpu.VMEM((2, ROW_TILE + HALO_ROWS, FFN_COLS), F32),
                        pltpu.VMEM((HALO_ROWS + ROW_TILE + HALO_ROWS, f2), F32),
                        pltpu.VMEM((ROW_TILE + HALO_ROWS, f), F32)],
        compiler_params=_params(),
    )(up_pre, up_pre, up_pre, dact, dact, fw, fb, *deps)


def _to_segments(a):
    t, c = a.shape
    return a.reshape(N_SEG, t // N_SEG, c).transpose(1, 0, 2).reshape(t, c)


def _from_segments(a):
    t, c = a.shape
    return a.reshape(t // N_SEG, N_SEG, c).transpose(1, 0, 2).reshape(t, c)


def _cmul(ar, ai, br, bi):
    return ar * br - ai * bi, ar * bi + ai * br


def _cpow(ar, ai, n):
    out = None
    while n:
        if n & 1:
            out = (ar, ai) if out is None else _cmul(out[0], out[1], ar, ai)
        ar, ai = _cmul(ar, ai, ar, ai)
        n >>= 1
    return out


def _segment_carries(pr, pi, fr, fi, forward):
    row = lax.broadcasted_iota(jnp.int32, fr.shape, 0)
    edge = row == (0 if forward else N_SEG - 1)
    shift = 1 if forward else N_SEG - 1
    sr, si = jnp.zeros_like(fr), jnp.zeros_like(fi)
    for _ in range(N_SEG - 1):
        tr, ti = _cmul(pr, pi, sr, si)
        sr = jnp.where(edge, 0.0, pltpu.roll(tr + fr, shift, 0))
        si = jnp.where(edge, 0.0, pltpu.roll(ti + fi, shift, 0))
    return sr, si


def _rows(i):
    if isinstance(i, int):
        return pl.ds(i * SUBLANES, SUBLANES)
    return pl.ds(pl.multiple_of(i * SUBLANES, SUBLANES), SUBLANES)


def _scan_steps(n, body, init):
    def trip(k, carry):
        for s in range(SCAN_UNROLL):
            carry = body(k * SCAN_UNROLL + s, carry)
        return carry

    carry = lax.fori_loop(0, n // SCAN_UNROLL, trip, init)
    for i in range(n - n % SCAN_UNROLL, n):
        carry = body(i, carry)
    return carry


def _s5_fwd(proj, u_col, bb_re, bb_im, a_re, a_im, cc_re, cc_im, d_skip):
    t = proj.shape[0]
    nch, _, cs = bb_re.shape
    ds = nch * SSM_CHUNK
    u_blk = u_col // SSM_CHUNK
    steps = t // N_SEG

    def body(u_ref, bbr_ref, bbi_ref, ar_ref, ai_ref, ccr_ref, cci_ref, d_ref, sr_ref, si_ref, y_ref):
        ub = u_ref[...].astype(BF16)
        sr_ref[...] = _dot(ub, bbr_ref[...], NN)
        si_ref[...] = _dot(ub, bbi_ref[...], NN)
        ar = jnp.broadcast_to(ar_ref[...], (N_SEG, cs))
        ai = jnp.broadcast_to(ai_ref[...], (N_SEG, cs))
        zero = jnp.zeros((N_SEG, cs), F32)

        def totals(i, carry):
            tr, ti = _cmul(ar, ai, *carry)
            return tr + sr_ref[_rows(i), :], ti + si_ref[_rows(i), :]

        fr, fi = _scan_steps(steps, totals, (zero, zero))
        s0r, s0i = _segment_carries(*_cpow(ar, ai, steps), fr, fi, True)

        def scan(i, carry):
            tr, ti = _cmul(ar, ai, *carry)
            nr, ni = tr + sr_ref[_rows(i), :], ti + si_ref[_rows(i), :]
            sr_ref[_rows(i), :] = nr
            si_ref[_rows(i), :] = ni
            return nr, ni

        _scan_steps(steps, scan, (s0r, s0i))
        y_ref[...] = (_dot(sr_ref[...].astype(BF16), ccr_ref[...], NT)
                      - _dot(si_ref[...].astype(BF16), cci_ref[...], NT)
                      + d_ref[...] * u_ref[...])

    chunk3 = lambda r, c: pl.BlockSpec((None, r, c), lambda j: (j, 0, 0))
    return pl.pallas_call(
        body, name="s5_fwd", grid=(nch,),
        in_specs=[pl.BlockSpec((t, SSM_CHUNK), lambda j: (0, j + u_blk)),
                  chunk3(SSM_CHUNK, cs), chunk3(SSM_CHUNK, cs), chunk3(1, cs), chunk3(1, cs),
                  chunk3(SSM_CHUNK, cs), chunk3(SSM_CHUNK, cs), chunk3(1, SSM_CHUNK)],
        out_specs=[pl.BlockSpec((t, cs), lambda j: (0, j)), pl.BlockSpec((t, cs), lambda j: (0, j)),
                   pl.BlockSpec((t, SSM_CHUNK), lambda j: (0, j))],
        out_shape=[jax.ShapeDtypeStruct((t, nch * cs), F32), jax.ShapeDtypeStruct((t, nch * cs), F32),
                   jax.ShapeDtypeStruct((t, ds), F32)],
        compiler_params=_params(),
    )(proj, bb_re, bb_im, a_re, a_im, cc_re, cc_im, d_skip)


def _s5_bwd(dy, proj, u_col, s_re, s_im, bb_re, bb_im, a_re, a_im, cc_re, cc_im, d_skip, gpc):
    t, ds = dy.shape
    nch, _, cs = bb_re.shape
    u_blk = u_col // SSM_CHUNK
    steps = t // N_SEG

    def body(dy_ref, u_ref, sr_ref, si_ref, bbr_ref, bbi_ref, ar_ref, ai_ref, ccr_ref, cci_ref, d_ref,
             du_ref, dbbr_ref, dbbi_ref, dar_ref, dai_ref, dccr_ref, dcci_ref, dd_ref, gr_ref, gi_ref):
        dyv = dy_ref[...]
        dyb = dyv.astype(BF16)
        gr_ref[...] = _dot(dyb, ccr_ref[...], NN)
        gi_ref[...] = -_dot(dyb, cci_ref[...], NN)
        ar = jnp.broadcast_to(ar_ref[...], (N_SEG, cs))
        ai = -jnp.broadcast_to(ai_ref[...], (N_SEG, cs))
        zero = jnp.zeros((N_SEG, cs), F32)

        def totals(k, carry):
            i = steps - 1 - k
            tr, ti = _cmul(ar, ai, *carry)
            return tr + gr_ref[_rows(i), :], ti + gi_ref[_rows(i), :]

        fr, fi = _scan_steps(steps, totals, (zero, zero))
        e0r, e0i = _segment_carries(*_cpow(ar, ai, steps), fr, fi, False)

        def step(i, gr, gi, pr, pi, acc_r, acc_i):
            tr, ti = _cmul(ar, ai, gr, gi)
            nr, ni = tr + gr_ref[_rows(i), :], ti + gi_ref[_rows(i), :]
            gr_ref[_rows(i), :] = nr
            gi_ref[_rows(i), :] = ni
            return nr, ni, acc_r + nr * pr + ni * pi, acc_i + ni * pr - nr * pi

        def scan(k, carry):
            i = steps - 1 - k
            gr, gi, acc_r, acc_i = carry
            return step(i, gr, gi, sr_ref[_rows(i - 1), :], si_ref[_rows(i - 1), :], acc_r, acc_i)

        gr, gi, acc_r, acc_i = _scan_steps(steps - 1, scan, (e0r, e0i, zero, zero))
        row = lax.broadcasted_iota(jnp.int32, (N_SEG, cs), 0)
        last = _rows(steps - 1)
        pr = jnp.where(row == 0, 0.0, pltpu.roll(sr_ref[last, :], 1, 0))
        pi = jnp.where(row == 0, 0.0, pltpu.roll(si_ref[last, :], 1, 0))
        _, _, acc_r, acc_i = step(0, gr, gi, pr, pi, acc_r, acc_i)
        dar_ref[...] = jnp.sum(acc_r, axis=0, keepdims=True)
        dai_ref[...] = jnp.sum(acc_i, axis=0, keepdims=True)

        uv = u_ref[...]
        ub = uv.astype(BF16)
        grb = gr_ref[...].astype(BF16)
        gib = gi_ref[...].astype(BF16)
        du = d_ref[...] * dyv + _dot(grb, bbr_ref[...], NT) + _dot(gib, bbi_ref[...], NT)
        du_ref[...] = du.astype(BF16)
        def put_groups(ref, full):
            for gl in range(gpc):
                ref[gl] = full[gl * hb:(gl + 1) * hb, gl * pb:(gl + 1) * pb]

        put_groups(dbbr_ref, _dot(ub, grb, TN))
        put_groups(dbbi_ref, _dot(ub, gib, TN))
        put_groups(dccr_ref, _dot(dyb, sr_ref[...].astype(BF16), TN))
        put_groups(dcci_ref, -_dot(dyb, si_ref[...].astype(BF16), TN))
        dd_ref[...] = jnp.sum(dyv * uv, axis=0, keepdims=True)

    hb, pb = SSM_CHUNK // gpc, cs // gpc
    groups = pl.BlockSpec((None, gpc, hb, pb), lambda j: (j, 0, 0, 0))
    groups_shape = jax.ShapeDtypeStruct((nch, gpc, hb, pb), F32)
    chunk3 = lambda r, c: pl.BlockSpec((None, r, c), lambda j: (j, 0, 0))
    cols = lambda w: pl.BlockSpec((t, w), lambda j: (0, j))
    return pl.pallas_call(
        body, name="s5_bwd", grid=(nch,),
        in_specs=[cols(SSM_CHUNK), pl.BlockSpec((t, SSM_CHUNK), lambda j: (0, j + u_blk)), cols(cs), cols(cs),
                  chunk3(SSM_CHUNK, cs), chunk3(SSM_CHUNK, cs), chunk3(1, cs), chunk3(1, cs),
                  chunk3(SSM_CHUNK, cs), chunk3(SSM_CHUNK, cs), chunk3(1, SSM_CHUNK)],
        out_specs=[cols(SSM_CHUNK), groups, groups, chunk3(1, cs), chunk3(1, cs), groups, groups,
                   chunk3(1, SSM_CHUNK)],
        out_shape=[jax.ShapeDtypeStruct((t, ds), BF16), groups_shape, groups_shape,
                   jax.ShapeDtypeStruct((nch, 1, cs), F32), jax.ShapeDtypeStruct((nch, 1, cs), F32),
                   groups_shape, groups_shape, jax.ShapeDtypeStruct((nch, 1, SSM_CHUNK), F32)],
        scratch_shapes=[pltpu.VMEM((t, cs), F32), pltpu.VMEM((t, cs), F32)],
        compiler_params=_params(),
    )(dy, proj, s_re, s_im, bb_re, bb_im, a_re, a_im, cc_re, cc_im, d_skip)


def _discretize(lr, li, log_dt, br, bi):
    dt = jnp.exp(log_dt)
    mag = jnp.exp(lr * dt)
    ang = li * dt
    a_re = mag * jnp.cos(ang)
    a_im = mag * jnp.sin(ang)
    den = lr * lr + li * li
    nr = a_re - 1.0
    f_re = (nr * lr + a_im * li) / den
    f_im = (a_im * lr - nr * li) / den
    return a_re, a_im, f_re * br - f_im * bi, f_re * bi + f_im * br


def _whole(shape):
    return pl.BlockSpec(shape, lambda: (0,) * len(shape))


def _disc_fwd(lr, li, log_dt, br, bi, cr, ci, nch, gpc, deps=()):
    gh, n_state = br.shape
    grp = gh // (nch * gpc)
    rows, cs = gpc * grp, gpc * n_state

    def body(lr_ref, li_ref, dt_ref, br_ref, bi_ref, cr_ref, ci_ref,
             ar_ref, ai_ref, bbr_ref, bbi_ref, ccr_ref, cci_ref, wide_ref):
        a_re, a_im, bb_re, bb_im = _discretize(lr_ref[...], li_ref[...], dt_ref[...], br_ref[...], bi_ref[...])
        for ref, val in ((ar_ref, a_re), (ai_ref, a_im)):
            for j in range(nch):
                for gl in range(gpc):
                    g0 = (j * gpc + gl) * grp
                    ref[j, :, gl * n_state:(gl + 1) * n_state] = val[g0:g0 + 1]
        for ref, val in ((bbr_ref, bb_re), (bbi_ref, bb_im), (ccr_ref, cr_ref[...]), (cci_ref, ci_ref[...])):
            wide_ref[...] = jnp.zeros_like(wide_ref)
            for j in range(nch):
                for gl in range(gpc):
                    g0 = (j * gpc + gl) * grp
                    wide_ref[j, gl * grp:(gl + 1) * grp, gl * n_state:(gl + 1) * n_state] = val[g0:g0 + grp]
            ref[...] = wide_ref[...].astype(BF16)

    args = (lr, li, log_dt, br, bi, cr, ci)
    row_shape, wide_shape = (nch, 1, cs), (nch, rows, cs)
    outs = [jax.ShapeDtypeStruct(row_shape, F32)] * 2 + [jax.ShapeDtypeStruct(wide_shape, BF16)] * 4
    return pl.pallas_call(
        _ignoring_deps(body, 7, deps), name="disc_fwd",
        in_specs=[_whole(a.shape) for a in args] + [ANY_SPEC] * len(deps),
        out_specs=[_whole(o.shape) for o in outs], out_shape=outs,
        scratch_shapes=[pltpu.VMEM(wide_shape, F32)],
    )(*args, *deps)


def _disc_bwd(lr, li, log_dt, br, bi, dar, dai, dbbr, dbbi):
    def body(lr_ref, li_ref, dt_ref, br_ref, bi_ref, dar_ref, dai_ref, dbbr_ref, dbbi_ref,
             dlr_ref, dli_ref, ddt_ref, dbr_ref, dbi_ref):
        _, vjp = jax.vjp(_discretize, lr_ref[...], li_ref[...], dt_ref[...], br_ref[...], bi_ref[...])
        grads = vjp((dar_ref[...], dai_ref[...], dbbr_ref[...], dbbi_ref[...]))
        for ref, val in zip((dlr_ref, dli_ref, ddt_ref, dbr_ref, dbi_ref), grads):
            ref[...] = val

    args = (lr, li, log_dt, br, bi, dar, dai, dbbr, dbbi)
    outs = (lr, li, log_dt, br, bi)
    return pl.pallas_call(
        body, name="disc_bwd", in_specs=[_whole(a.shape) for a in args],
        out_specs=[_whole(a.shape) for a in outs],
        out_shape=[jax.ShapeDtypeStruct(a.shape, F32) for a in outs],
    )(*args)


def _adamw(w, g, m, v):
    m = ADAM_B1 * m + (1.0 - ADAM_B1) * g
    v = ADAM_B2 * v + (1.0 - ADAM_B2) * (g * g)
    m_hat = m / ADAM_BC1
    v_hat = v / ADAM_BC2
    delta = -ADAM_LR * (m_hat / (jnp.sqrt(v_hat) + ADAM_EPS) + ADAM_WD * w)
    return delta, m, v


def _adamw_reduce(name, parts, w, m, v):
    _, r, c = parts.shape
    tr = r
    for cand in (256, 176, 128):
        if r % cand == 0:
            tr = cand
            break

    def body(p_ref, w_ref, m_ref, v_ref, g_ref, d_ref, nm_ref, nv_ref):
        g = p_ref[0].astype(F32)
        for k in range(1, N_DEV):
            g = g + p_ref[k].astype(F32)
        delta, nm, nv = _adamw(w_ref[...], g, m_ref[...], v_ref[...])
        g_ref[...] = g
        d_ref[...] = delta
        nm_ref[...] = nm
        nv_ref[...] = nv

    blk = pl.BlockSpec((tr, c), lambda i: (i, 0))
    return pl.pallas_call(
        body, name=name, grid=(r // tr,),
        in_specs=[pl.BlockSpec((N_DEV, tr, c), lambda i: (0, i, 0)), blk, blk, blk],
        out_specs=[blk] * 4, out_shape=[jax.ShapeDtypeStruct((r, c), F32)] * 4,
        compiler_params=_params(),
    )(parts, w, m, v)


def _sum_parts(name, parts):
    _, r, c = parts.shape

    def body(p_ref, o_ref):
        g = p_ref[0]
        for k in range(1, N_DEV):
            g = g + p_ref[k]
        o_ref[...] = g

    return pl.pallas_call(
        body, name=name, in_specs=[_whole(parts.shape)], out_specs=_whole((r, c)),
        out_shape=jax.ShapeDtypeStruct((r, c), F32), compiler_params=_params(),
    )(parts)


def _adamw_many(name, grads, ws, ms, vs):
    n = len(grads)

    def body(*refs):
        ins, outs = refs[:4 * n], refs[4 * n:]
        for i in range(n):
            g, w, m, v = (ins[j * n + i][...] for j in range(4))
            for ref, val in zip((outs[i], outs[n + i], outs[2 * n + i]), _adamw(w, g, m, v)):
                ref[...] = val

    args = (*grads, *ws, *ms, *vs)
    outs = pl.pallas_call(
        body, name=name, in_specs=[_whole(a.shape) for a in args],
        out_specs=[_whole(a.shape) for a in ws] * 3,
        out_shape=[jax.ShapeDtypeStruct(a.shape, F32) for a in ws] * 3, compiler_params=_params(),
    )(*args)
    return outs[:n], outs[n:2 * n], outs[2 * n:]


def _pack(arrays, rows):
    flat = jnp.concatenate([a.reshape(-1) for a in arrays])
    return jnp.pad(flat, (0, rows * LANES - flat.shape[0])).reshape(rows, LANES)


def _unpack(packed, shapes):
    flat = packed.reshape(-1)
    out, off = [], 0
    for s in shapes:
        n = math.prod(s)
        out.append(flat[off:off + n].reshape(s))
        off += n
    return out


def _packed_rows(shapes):
    n = sum(math.prod(s) for s in shapes)
    return -(-n // (SUBLANES * LANES)) * SUBLANES


def kernel(x, meta_tokens, norm_mix_g, w_in, conv_w, ssm_lam_re, ssm_lam_im, ssm_log_dt, ssm_b_re, ssm_b_im, ssm_c_re, ssm_c_im, ssm_d, ssm_w_glu, gain_conv_out, gain_ssm_out, w_out, norm_ffn_g, w_up, ffn_conv_w, ffn_conv_b, w_down, norm_final_g, loss_target, m_meta_tokens, m_norm_mix_g, m_w_in, m_conv_w, m_ssm_lam_re, m_ssm_lam_im, m_ssm_log_dt, m_ssm_b_re, m_ssm_b_im, m_ssm_c_re, m_ssm_c_im, m_ssm_d, m_ssm_w_glu, m_gain_conv_out, m_gain_ssm_out, m_w_out, m_norm_ffn_g, m_w_up, m_ffn_conv_w, m_ffn_conv_b, m_w_down, m_norm_final_g, v_meta_tokens, v_norm_mix_g, v_w_in, v_conv_w, v_ssm_lam_re, v_ssm_lam_im, v_ssm_log_dt, v_ssm_b_re, v_ssm_b_im, v_ssm_c_re, v_ssm_c_im, v_ssm_d, v_ssm_w_glu, v_gain_conv_out, v_gain_ssm_out, v_w_out, v_norm_ffn_g, v_w_up, v_ffn_conv_w, v_ffn_conv_b, v_w_down, v_norm_final_g):
    weights = dict(meta_tokens=meta_tokens, norm_mix_g=norm_mix_g, w_in=w_in, conv_w=conv_w, ssm_lam_re=ssm_lam_re, ssm_lam_im=ssm_lam_im, ssm_log_dt=ssm_log_dt, ssm_b_re=ssm_b_re, ssm_b_im=ssm_b_im, ssm_c_re=ssm_c_re, ssm_c_im=ssm_c_im, ssm_d=ssm_d, ssm_w_glu=ssm_w_glu, gain_conv_out=gain_conv_out, gain_ssm_out=gain_ssm_out, w_out=w_out, norm_ffn_g=norm_ffn_g, w_up=w_up, ffn_conv_w=ffn_conv_w, ffn_conv_b=ffn_conv_b, w_down=w_down, norm_final_g=norm_final_g)
    mom_m = dict(meta_tokens=m_meta_tokens, norm_mix_g=m_norm_mix_g, w_in=m_w_in, conv_w=m_conv_w, ssm_lam_re=m_ssm_lam_re, ssm_lam_im=m_ssm_lam_im, ssm_log_dt=m_ssm_log_dt, ssm_b_re=m_ssm_b_re, ssm_b_im=m_ssm_b_im, ssm_c_re=m_ssm_c_re, ssm_c_im=m_ssm_c_im, ssm_d=m_ssm_d, ssm_w_glu=m_ssm_w_glu, gain_conv_out=m_gain_conv_out, gain_ssm_out=m_gain_ssm_out, w_out=m_w_out, norm_ffn_g=m_norm_ffn_g, w_up=m_w_up, ffn_conv_w=m_ffn_conv_w, ffn_conv_b=m_ffn_conv_b, w_down=m_w_down, norm_final_g=m_norm_final_g)
    mom_v = dict(meta_tokens=v_meta_tokens, norm_mix_g=v_norm_mix_g, w_in=v_w_in, conv_w=v_conv_w, ssm_lam_re=v_ssm_lam_re, ssm_lam_im=v_ssm_lam_im, ssm_log_dt=v_ssm_log_dt, ssm_b_re=v_ssm_b_re, ssm_b_im=v_ssm_b_im, ssm_c_re=v_ssm_c_re, ssm_c_im=v_ssm_c_im, ssm_d=v_ssm_d, ssm_w_glu=v_ssm_w_glu, gain_conv_out=v_gain_conv_out, gain_ssm_out=v_gain_ssm_out, w_out=v_w_out, norm_ffn_g=v_norm_ffn_g, w_up=v_w_up, ffn_conv_w=v_ffn_conv_w, ffn_conv_b=v_ffn_conv_b, w_down=v_w_down, norm_final_g=v_norm_final_g)
    names = list(weights)

    n_meta, d_meta = meta_tokens.shape
    seq, d = x.shape[1], x.shape[2]
    rows_used = n_meta + seq
    t = -(-rows_used // ROW_TILE) * ROW_TILE
    d_in_s = w_in.shape[2]
    dc_s = conv_w.shape[2]
    dc = dc_s * N_DEV
    ds = ssm_w_glu.shape[2]
    n_groups, n_state, grp = ssm_b_re.shape[1:]
    ns = n_groups * n_state
    nch = ds // SSM_CHUNK
    gpc = n_groups // nch
    ff_s = w_up.shape[2]
    dn_s = w_down.shape[1]
    assert 3 * dc + ds == d_in_s * N_DEV and 2 * dn_s == ff_s and t % (N_SEG * SUBLANES) == 0

    small_shard = jnp.concatenate([meta_tokens.reshape(-1), conv_w.reshape(-1), ffn_conv_w.reshape(-1)])
    n_small = small_shard.shape[0]
    small_rows = -(-n_small // LANES)
    small_shard = jnp.pad(small_shard, (0, small_rows * LANES - n_small)).reshape(small_rows, LANES)
    ag, ag_token = _gather2_start("gather_weights_start", [
        small_shard, w_in[0].astype(BF16), ssm_w_glu[0].astype(BF16), w_out[0].astype(BF16),
        jnp.swapaxes(w_up[0], 0, 1).astype(BF16), w_down[0].astype(BF16)])
    fb = ffn_conv_b

    gh = n_groups * grp
    per_h = lambda a: jnp.broadcast_to(a.reshape(n_groups, 1, -1), (n_groups, grp, n_state)).reshape(gh, n_state)
    ghp = lambda a: a.transpose(0, 1, 3, 2).reshape(gh, n_state)
    lr, li, log_dt_e = per_h(ssm_lam_re), per_h(ssm_lam_im), per_h(ssm_log_dt)
    br, bi = ghp(ssm_b_re), ghp(ssm_b_im)
    a_re_c, a_im_c, bbm_re, bbm_im, ccm_re, ccm_im = _disc_fwd(
        lr, li, log_dt_e, br, bi, ssm_c_re.reshape(gh, n_state), ssm_c_im.reshape(gh, n_state), nch, gpc,
        deps=(ag_token,))
    d_skip = ssm_d.reshape(nch, 1, SSM_CHUNK)

    target = _to_segments(jnp.pad(loss_target[0] + ag_token[0, 0], ((n_meta, t - rows_used), (0, 0))))
    token = _gather2_forward("gather_weights_forward_first", ag, (0, 1), (bbm_im, ccm_im, target))
    (g_small,) = _gather2_wait("gather_weights_wait_small", ag, (0,), token)
    g_small = g_small.reshape(N_DEV, -1)
    o1 = n_meta * d_meta
    o2 = o1 + 3 * dc_s
    meta_full = g_small[:, :o1].reshape(N_DEV, n_meta, d_meta).transpose(1, 0, 2).reshape(n_meta, d)
    conv_w_f = g_small[:, o1:o2].reshape(N_DEV, 3, dc_s).transpose(1, 0, 2).reshape(3, dc)
    fw = g_small[:, o2:o2 + 3 * ff_s].reshape(N_DEV, 3, ff_s).transpose(1, 0, 2).reshape(3, N_DEV * ff_s)
    h0 = _to_segments(jnp.concatenate([meta_full, x[0], jnp.zeros((t - rows_used, d), F32)], axis=0))
    full_t = lambda w: pl.BlockSpec((t, w), lambda *_: (0, 0))

    hn1 = _rms_fwd("norm_mix", h0, norm_mix_g)
    (g_in,) = _gather2_wait("gather_weights_wait_in", ag, (1,), hn1)
    proj = _mm("proj", hn1, g_in, dims=NN, grid=(N_DEV,), a_spec=full_t(d),
               b_spec=pl.BlockSpec((None, d, d_in_s), lambda j: (j, 0, 0)),
               o_spec=pl.BlockSpec((t, d_in_s), lambda j: (0, j)),
               out_shape=jax.ShapeDtypeStruct((t, N_DEV * d_in_s), F32))
    s_re, s_im, y_ssm = _s5_fwd(proj, 3 * dc, bbm_re, bbm_im, a_re_c, a_im_c, ccm_re, ccm_im, d_skip)
    token = _gather2_forward("gather_weights_forward_up", ag, (2, 3, 4), (y_ssm,))
    g_glu, g_out = _gather2_wait("gather_weights_wait_mix", ag, (2, 3), token)
    w_out_f = g_out.reshape(-1, d)
    w_glu_f = g_glu.reshape(ds, ds)
    mixed, z_glu, h1, hn2 = _mix_fwd(proj, y_ssm, w_glu_f, conv_w_f, gain_conv_out, gain_ssm_out,
                                     h0, w_out_f, norm_ffn_g)
    tn_out = OUT_TILE
    token = _gather2_forward("gather_weights_forward_down", ag, (5,), (hn2,))
    (g_up,) = _gather2_wait("gather_weights_wait_up", ag, (4,), token)
    f2 = N_DEV * ff_s
    pair = 2 * ff_s
    w_up_t = g_up.reshape(f2, d)
    up_pre = _mm("up_proj", hn2, w_up_t, dims=NT, grid=(f2 // pair,), a_spec=full_t(d),
                 b_spec=pl.BlockSpec((pair, d), lambda j: (j, 0)),
                 o_spec=pl.BlockSpec((t, pair), lambda j: (0, j)),
                 out_shape=jax.ShapeDtypeStruct((t, f2), BF16))
    act = _gated_fwd(up_pre, fw, fb)
    (g_down,) = _gather2_wait("gather_weights_wait_down", ag, (5,), act)
    w_down_f = g_down.reshape(f2 // 2, d)

    dh2, dh2_b, loss_part, d_norm_final = _down_loss_bwd(
        act, w_down_f, h1, target, norm_final_g.reshape(1, d), n_meta, rows_used)
    dw_down = _mm("down_wgrad", act, dh2_b, dims=TN, grid=(f2 // 2 // pair,),
                  a_spec=pl.BlockSpec((t, pair), lambda j: (0, j)), b_spec=full_t(d),
                  o_spec=pl.BlockSpec((pair, d), lambda j: (j, 0)),
                  out_shape=jax.ShapeDtypeStruct((f2 // 2, d), BF16))
    ex_down, token = _send_start("exchange_down_start", [dw_down.reshape(N_DEV, dn_s, d)], gather=False)
    dact = _mm("down_dgrad", dh2_b, w_down_f, dims=NT, grid=(f2 // 2 // pair,), a_spec=full_t(d),
               b_spec=pl.BlockSpec((pair, d), lambda j: (j, 0)),
               o_spec=pl.BlockSpec((t, pair), lambda j: (0, j)),
               out_shape=jax.ShapeDtypeStruct((t, f2 // 2), BF16))
    dup_pre, d_fw, d_fb = _gated_bwd(up_pre, dact, fw, fb, deps=(token,))
    dw_up = _mm("up_wgrad", dup_pre, hn2, dims=TN, grid=(f2 // pair,),
                a_spec=pl.BlockSpec((t, pair), lambda j: (0, j)), b_spec=full_t(d),
                o_spec=pl.BlockSpec((pair, d), lambda j: (j, 0)),
                out_shape=jax.ShapeDtypeStruct((f2, d), BF16))
    ex_up, token = _send_start("exchange_up_start", [dw_up.reshape(N_DEV, ff_s, d)], gather=False)
    dh1, dh1_b, d_norm_ffn = _dgrad_rms_bwd("up_dgrad_norm_bwd", dh2, dup_pre, w_up_t, NN, h1, norm_ffn_g,
                                            deps=(token,))
    dmixed = _mm("out_dgrad", dh1_b, w_out_f, dims=NT, grid=((dc + ds) // tn_out,), a_spec=full_t(d),
                 b_spec=pl.BlockSpec((tn_out, d), lambda i: (i, 0)),
                 o_spec=pl.BlockSpec((t, tn_out), lambda i: (0, i)),
                 out_shape=jax.ShapeDtypeStruct((t, dc + ds), F32))
    dw_out = _mm("out_wgrad", mixed, dh1_b, dims=TN, grid=((dc + ds) // tn_out,),
                 a_spec=pl.BlockSpec((t, tn_out), lambda i: (0, i)), b_spec=full_t(d),
                 o_spec=pl.BlockSpec((tn_out, d), lambda i: (i, 0)),
                 out_shape=jax.ShapeDtypeStruct((dc + ds, d), BF16))
    db_gate, dconv, dy_ssm, d_wglu, d_conv_w, d_gain_c, d_gain_s = _mix_bwd1(
        proj, y_ssm, z_glu, dmixed, w_glu_f, conv_w_f, gain_conv_out, gain_ssm_out)
    (du, d_bbm_re, d_bbm_im, d_a_re, d_a_im, d_ccm_re, d_ccm_im, d_dskip) = _s5_bwd(
        dy_ssm, proj, 3 * dc, s_re, s_im, bbm_re, bbm_im, a_re_c, a_im_c, ccm_re, ccm_im, d_skip, gpc)

    from_bb = from_cc = lambda a: a.reshape(gh, n_state)
    first_h = lambda a: jnp.pad(a.reshape(n_groups, 1, n_state), ((0, 0), (0, grp - 1), (0, 0))).reshape(gh, n_state)
    over_h = lambda a: a.reshape(n_groups, grp, n_state).sum(axis=1)
    d_lr, d_li, d_dt_e, d_br, d_bi = _disc_bwd(
        lr, li, log_dt_e, br, bi, first_h(d_a_re), first_h(d_a_im), from_bb(d_bbm_re), from_bb(d_bbm_im))

    rep2d = dict(
        ssm_lam_re=(n_groups, n_state), ssm_lam_im=(n_groups, n_state), ssm_log_dt=(1, n_groups),
        ssm_b_re=(gh, n_state), ssm_b_im=(gh, n_state), ssm_c_re=(gh, n_state),
        ssm_c_im=(gh, n_state), ssm_d=(n_groups, grp), gain_conv_out=(1, dc),
        gain_ssm_out=(1, ds), norm_ffn_g=(1, d), ffn_conv_b=(1, N_DEV * ff_s), norm_final_g=(1, d))
    rep_names = list(rep2d)
    rep_grads = dict(
        ssm_lam_re=over_h(d_lr), ssm_lam_im=over_h(d_li), ssm_log_dt=over_h(d_dt_e).sum(axis=1),
        ssm_b_re=d_br, ssm_b_im=d_bi, ssm_c_re=from_cc(d_ccm_re), ssm_c_im=from_cc(d_ccm_im),
        ssm_d=d_dskip, gain_conv_out=d_gain_c, gain_ssm_out=d_gain_s, norm_ffn_g=d_norm_ffn,
        ffn_conv_b=d_fb, norm_final_g=d_norm_final)
    rep_shapes = [rep2d[n] for n in rep_names] + [(1, 1)]
    rep_rows = _packed_rows(rep_shapes)
    rep_pack = _pack([rep_grads[n] for n in rep_names] + [loss_part], rep_rows)
    ex_mix, token = _send_start("exchange_mix_start", [
        rep_pack, dw_out.reshape(N_DEV, -1, d), d_wglu.astype(BF16).reshape(N_DEV, -1, ds),
        d_conv_w.reshape(3, N_DEV, dc_s).transpose(1, 0, 2),
        d_fw.reshape(3, N_DEV, ff_s).transpose(1, 0, 2)],
        gather=[True, False, False, False, False])
    dcdv = _mix_bwd2(proj, dconv, conv_w_f, deps=(token,))
    dproj = jnp.concatenate([db_gate, dcdv, du], axis=1)
    d_in = N_DEV * d_in_s
    w_in_rows = g_in.transpose(1, 0, 2).reshape(d, d_in)
    dw_in = _mm("proj_wgrad", hn1, dproj, dims=TN, grid=(N_DEV,), a_spec=full_t(d),
                b_spec=pl.BlockSpec((t, d_in_s), lambda j: (0, j)),
                o_spec=pl.BlockSpec((None, d, d_in_s), lambda j: (j, 0, 0)),
                out_shape=jax.ShapeDtypeStruct((N_DEV, d, d_in_s), BF16))
    dh0, _, d_norm_mix = _dgrad_rms_bwd("proj_dgrad_norm_bwd", dh1, dproj, w_in_rows, NT, h0, norm_mix_g)
    dh0 = _from_segments(dh0)
    grad_x = dh0[n_meta:rows_used][None]
    d_meta_b = dh0[:n_meta].reshape(n_meta, N_DEV, d_meta).transpose(1, 0, 2)
    ex_in, ex_in_token = _send_start("exchange_in_start", [dw_in, d_meta_b, d_norm_mix],
                                     gather=[False, False, True])

    shard_out = {}

    def update(n, parts, transposed=False):
        sh = weights[n].shape
        two_d = lambda a: (jnp.swapaxes(a[0], 0, 1) if transposed else a.reshape(parts.shape[1:]))
        res = _adamw_reduce("adamw_" + n, parts, two_d(weights[n]), two_d(mom_m[n]), two_d(mom_v[n]))
        shard_out[n] = [(jnp.swapaxes(r, 0, 1) if transposed else r).reshape(sh) for r in res]
        return res[0]

    (p_down,) = _send_wait("exchange_down_wait", ex_down, (0,), ex_in_token)
    done = update("w_down", p_down)
    (p_up,) = _send_wait("exchange_up_wait", ex_up, (0,), done)
    done = update("w_up", p_up, transposed=True)
    (rep_parts,) = _send_wait("gather_small_grads_wait", ex_mix, (0,), done)
    rep_sum = _sum_parts("sum_small_grads", rep_parts)
    *rep_g, loss = _unpack(rep_sum, rep_shapes)
    loss = loss.reshape(())
    swapped = ("ssm_b_re", "ssm_b_im")
    to2d = lambda n, a: ghp(a) if n in swapped else a.reshape(rep2d[n])
    from2d = lambda n, a: (a.reshape(1, n_groups, grp, n_state).transpose(0, 1, 3, 2) if n in swapped
                           else a.reshape(weights[n].shape))
    as2d = lambda tree: [to2d(n, tree[n]) for n in rep_names]
    rep_res = _adamw_many("adamw_replicated", rep_g, as2d(weights), as2d(mom_m), as2d(mom_v))
    for i, n in enumerate(rep_names):
        shard_out[n] = [from2d(n, r) for r in (rep_g[i], *(res[i] for res in rep_res))]

    p_out, p_glu, p_cw, p_fw = _send_wait("exchange_mix_wait", ex_mix, (1, 2, 3, 4), rep_sum)
    update("w_out", p_out)
    update("ssm_w_glu", p_glu)
    update("conv_w", p_cw)
    done = update("ffn_conv_w", p_fw)
    p_in, p_meta, p_nm = _send_wait("exchange_in_wait", ex_in, (0, 1, 2), done)
    update("w_in", p_in)
    update("meta_tokens", p_meta)
    update("norm_mix_g", p_nm)

    grads = [shard_out[n][0] for n in names]
    deltas = [shard_out[n][1] for n in names]
    new_m = [shard_out[n][2] for n in names]
    new_v = [shard_out[n][3] for n in names]
    return (loss, grad_x, *grads, *deltas, *new_m, *new_v)
```
